```python
import math
import jax, jax.numpy as jnp
from jax import lax
import numpy as np

D_MODEL = 1024
BATCH = 8
SEQ = 2048
DEPTH = 2

N_MIXERS = 2
N_A = (DEPTH + 1) // 2
N_B = DEPTH // 2
N_SUB = 3
D_FF = 2816
FFN_RES = 0.5
EPS = 1e-6

MLA_HEADS = 16
Q_LORA = 384
KV_LORA = 256
QK_NOPE = 64
QK_ROPE = 32
V_HEAD = 64
ROPE_THETA = 10000.0
Q_BLOCK = 128

DIL_GROUPS = ((128, 1), (512, 4), (2048, 16))
N_GROUPS = 3
DIL_HEADS = 16
DIL_HEAD_DIM = 64
DIL_BLOCK = 128
DIL_WIDTH = DIL_HEADS * DIL_HEAD_DIM

N_BUCKETS = 32
MAX_DISTANCE = 2048

kernel_name = "hybrid_mla_dilated_macaron"


def rmsnorm(x, g):
    xf = x.astype(jnp.float32)
    y = xf * lax.rsqrt(jnp.mean(xf * xf, axis=-1, keepdims=True) + EPS)
    return (y * g.astype(jnp.float32)).astype(x.dtype)


def swiglu(h, w_gate, w_up, w_down):
    return (jax.nn.silu(h @ w_gate) * (h @ w_up)) @ w_down


def rope(x, pos):
    half = x.shape[-1] // 2
    freqs = ROPE_THETA ** (-jnp.arange(half, dtype=jnp.float32) / half)
    ang = pos[:, None] * freqs[None, :]
    cos = jnp.cos(ang)[None, :, None, :]
    sin = jnp.sin(ang)[None, :, None, :]
    x1 = x[..., :half].astype(jnp.float32)
    x2 = x[..., half:].astype(jnp.float32)
    return jnp.concatenate([x1 * cos - x2 * sin, x1 * sin + x2 * cos], axis=-1).astype(x.dtype)


def causal_block_attention(q, k, v, scale):
    B, S, H, dq = q.shape
    nb = S // Q_BLOCK
    qb = q.reshape(B, nb, Q_BLOCK, H, dq).transpose(1, 0, 3, 2, 4)
    kt = k.transpose(0, 2, 1, 3)
    vt = v.transpose(0, 2, 1, 3)
    kpos = jnp.arange(S)

    def one_block(args):
        qi, n = args
        s = jnp.einsum('bhqd,bhkd->bhqk', qi, kt).astype(jnp.float32) * scale
        qpos = n * Q_BLOCK + jnp.arange(Q_BLOCK)
        s = jnp.where(kpos[None, :] <= qpos[:, None], s, -jnp.inf)
        p = jax.nn.softmax(s, axis=-1).astype(vt.dtype)
        return jnp.einsum('bhqk,bhkd->bhqd', p, vt)

    out = lax.map(one_block, (qb, jnp.arange(nb)))
    return out.transpose(1, 0, 3, 2, 4).reshape(B, S, H, v.shape[-1])


def mla(h, w_in, q_norm, w_q_up, kv_norm, w_kv_up, w_o):
    B, S, _ = h.shape
    lat = h @ w_in
    cq = lat[..., :Q_LORA]
    ckv = lat[..., Q_LORA:Q_LORA + KV_LORA]
    k_rope = lat[..., Q_LORA + KV_LORA:][:, :, None, :]
    q = (rmsnorm(cq, q_norm) @ w_q_up).reshape(B, S, MLA_HEADS, QK_NOPE + QK_ROPE)
    kv = (rmsnorm(ckv, kv_norm) @ w_kv_up).reshape(B, S, MLA_HEADS, QK_NOPE + V_HEAD)
    pos = jnp.arange(S, dtype=jnp.float32)
    q = jnp.concatenate([q[..., :QK_NOPE], rope(q[..., QK_NOPE:], pos)], axis=-1)
    k_rope = jnp.broadcast_to(rope(k_rope, pos), (B, S, MLA_HEADS, QK_ROPE))
    k = jnp.concatenate([kv[..., :QK_NOPE], k_rope.astype(kv.dtype)], axis=-1)
    v = kv[..., QK_NOPE:]
    o = causal_block_attention(q, k, v, (QK_NOPE + QK_ROPE) ** -0.5)
    return o.reshape(B, S, MLA_HEADS * V_HEAD) @ w_o


def t5_bucket(dist):
    max_exact = N_BUCKETS // 2
    d = jnp.maximum(dist, 1).astype(jnp.float32)
    large = max_exact + (jnp.log(d / max_exact) / math.log(MAX_DISTANCE / max_exact)
                         * (N_BUCKETS - max_exact)).astype(jnp.int32)
    large = jnp.minimum(large, N_BUCKETS - 1)
    return jnp.where(dist < max_exact, dist, large)


def strided_window_attention(q, k, v, dilation, span, bias_table):
    B, S, H, E = q.shape
    L = S // dilation
    nb = -(-L // DIL_BLOCK)
    Lp = nb * DIL_BLOCK
    qs = q.reshape(B, L, dilation, H, E)
    qb = jnp.pad(qs, ((0, 0), (0, Lp - L), (0, 0), (0, 0), (0, 0))).reshape(B, nb, DIL_BLOCK, dilation, H, E)

    def windows(t):
        tp = jnp.pad(t.reshape(B, L, dilation, H, E),
                     ((0, 0), (DIL_BLOCK, Lp - L), (0, 0), (0, 0), (0, 0)))
        tp = tp.reshape(B, nb + 1, DIL_BLOCK, dilation, H, E)
        return jnp.concatenate([tp[:, :-1], tp[:, 1:]], axis=2)

    kw, vw = windows(k), windows(v)
    s = jnp.einsum('bnqrhe,bnkrhe->bnrhqk', qb, kw).astype(jnp.float32) * (E ** -0.5)
    iq = jnp.arange(DIL_BLOCK)[:, None]
    ik = jnp.arange(2 * DIL_BLOCK)[None, :]
    rel = DIL_BLOCK + iq - ik
    in_window = (rel >= 0) & (rel <= span)
    bucket = t5_bucket(jnp.maximum(rel, 0) * dilation)
    bias = jnp.transpose(bias_table[bucket], (2, 0, 1)).astype(jnp.float32)
    key_m = (jnp.arange(nb)[:, None] - 1) * DIL_BLOCK + jnp.arange(2 * DIL_BLOCK)[None, :]
    valid = in_window[None] & (key_m >= 0)[:, None, :]
    logits = jnp.where(valid[None, :, None, None], s + bias, -jnp.inf)
    lse = jax.nn.logsumexp(logits, axis=-1)
    p = jnp.exp(logits - lse[..., None]).astype(v.dtype)
    o = jnp.einsum('bnrhqk,bnkrhe->bnqrhe', p, vw)
    o = o.reshape(B, Lp, dilation, H, E)[:, :L].reshape(B, S, H, E)
    lse = jnp.transpose(lse, (0, 1, 4, 2, 3)).reshape(B, Lp, dilation, H)[:, :L].reshape(B, S, H)
    return o, lse


def dilated_attention(h, w_in, w_o, rel_bias):
    B, S, _ = h.shape
    proj = (h @ w_in).reshape(B, S, N_GROUPS, 3, DIL_HEADS, DIL_HEAD_DIM)
    outs, lses = [], []
    for g, (window, dilation) in enumerate(DIL_GROUPS):
        o, lse = strided_window_attention(
            proj[:, :, g, 0], proj[:, :, g, 1], proj[:, :, g, 2],
            dilation, window // dilation, rel_bias[:, g * DIL_HEADS:(g + 1) * DIL_HEADS])
        outs.append(o)
        lses.append(lse)
    alpha = jax.nn.softmax(jnp.stack(lses, axis=0), axis=0)
    o = jnp.sum(alpha[..., None] * jnp.stack(outs, axis=0).astype(jnp.float32), axis=0)
    return o.astype(h.dtype).reshape(B, S, DIL_WIDTH) @ w_o


def sandwich(x, fn, pre_g, post_g, shift, scale, gate, res_w):
    hn = rmsnorm(x, pre_g) * (1 + scale[:, None, :]) + shift[:, None, :]
    y = rmsnorm(fn(hn), post_g)
    return x + res_w * gate[:, None, :] * y


def _fwd_setup_inputs(seed: int = 0) -> dict:
    key = jax.random.key(seed)
    ks = jax.random.split(key, 20)
    D = D_MODEL
    nrm = lambda k, shape, fan: jax.random.normal(k, shape, jnp.float32) * fan ** -0.5
    return {
        "x": jax.random.normal(ks[0], (BATCH, SEQ, D), jnp.float32),
        "c": jax.random.normal(ks[1], (BATCH, D), jnp.float32),
        "norm_pre": 1.0 + 0.05 * jax.random.normal(ks[2], (DEPTH, N_SUB, D), jnp.float32),
        "norm_post": 1.0 + 0.05 * jax.random.normal(ks[3], (DEPTH, N_SUB, D), jnp.float32),
        "w_mod": nrm(ks[4], (DEPTH, D, N_SUB * 3 * D), D) * 0.5,
        "b_mod": 0.02 * jax.random.normal(ks[5], (DEPTH, N_SUB * 3 * D), jnp.float32),
        "ffn_w_gate": nrm(ks[6], (DEPTH, 2, D, D_FF), D),
        "ffn_w_up": nrm(ks[7], (DEPTH, 2, D, D_FF), D),
        "ffn_w_down": nrm(ks[8], (DEPTH, 2, D_FF, D), D_FF),
        "mla_w_in": nrm(ks[9], (N_A, D, Q_LORA + KV_LORA + QK_ROPE), D),
        "mla_q_norm": 1.0 + 0.05 * jax.random.normal(ks[10], (N_A, Q_LORA), jnp.float32),
        "mla_w_q_up": nrm(ks[11], (N_A, Q_LORA, MLA_HEADS * (QK_NOPE + QK_ROPE)), Q_LORA),
        "mla_kv_norm": 1.0 + 0.05 * jax.random.normal(ks[12], (N_A, KV_LORA), jnp.float32),
        "mla_w_kv_up": nrm(ks[13], (N_A, KV_LORA, MLA_HEADS * (QK_NOPE + V_HEAD)), KV_LORA),
        "mla_w_o": nrm(ks[14], (N_A, MLA_HEADS * V_HEAD, D), MLA_HEADS * V_HEAD),
        "dil_w_in": nrm(ks[15], (N_B, D, N_GROUPS * 3 * DIL_WIDTH), D),
        "dil_w_o": nrm(ks[16], (N_B, DIL_WIDTH, D), DIL_WIDTH),
        "rel_bias": 0.5 * jax.random.normal(ks[17], (N_BUCKETS, N_GROUPS * DIL_HEADS), jnp.float32),
    }


def _fwd_reference(x, c, norm_pre, norm_post, w_mod, b_mod, ffn_w_gate, ffn_w_up, ffn_w_down,
              mla_w_in, mla_q_norm, mla_w_q_up, mla_kv_norm, mla_w_kv_up, mla_w_o,
              dil_w_in, dil_w_o, rel_bias):
    B = x.shape[0]
    for i in range(DEPTH):
        mod = (jax.nn.silu(c) @ w_mod[i] + b_mod[i]).reshape(B, N_SUB, 3, D_MODEL)

        def ffn_first(h, i=i):
            return swiglu(h, ffn_w_gate[i, 0], ffn_w_up[i, 0], ffn_w_down[i, 0])

        def ffn_second(h, i=i):
            return swiglu(h, ffn_w_gate[i, 1], ffn_w_up[i, 1], ffn_w_down[i, 1])

        if i % N_MIXERS == 0:
            a = i // N_MIXERS
            def mixer(h, a=a):
                return mla(h, mla_w_in[a], mla_q_norm[a], mla_w_q_up[a],
                           mla_kv_norm[a], mla_w_kv_up[a], mla_w_o[a])
        else:
            b = i // N_MIXERS
            def mixer(h, b=b):
                return dilated_attention(h, dil_w_in[b], dil_w_o[b], rel_bias)

        x = sandwich(x, ffn_first, norm_pre[i, 0], norm_post[i, 0],
                     mod[:, 0, 0], mod[:, 0, 1], mod[:, 0, 2], FFN_RES)
        x = sandwich(x, mixer, norm_pre[i, 1], norm_post[i, 1],
                     mod[:, 1, 0], mod[:, 1, 1], mod[:, 1, 2], 1.0)
        x = sandwich(x, ffn_second, norm_pre[i, 2], norm_post[i, 2],
                     mod[:, 2, 0], mod[:, 2, 1], mod[:, 2, 2], FFN_RES)
    return x


import jax as _jax
import jax.numpy as _jnp

TWIN_FORMAT = 'train_step'
FWD_PARAMS = ['x', 'c', 'norm_pre', 'norm_post', 'w_mod', 'b_mod', 'ffn_w_gate', 'ffn_w_up', 'ffn_w_down', 'mla_w_in', 'mla_q_norm', 'mla_w_q_up', 'mla_kv_norm', 'mla_w_kv_up', 'mla_w_o', 'dil_w_in', 'dil_w_o', 'rel_bias']
TWIN_WEIGHTS = ['norm_pre', 'norm_post', 'w_mod', 'b_mod', 'ffn_w_gate', 'ffn_w_up', 'ffn_w_down', 'mla_w_in', 'mla_q_norm', 'mla_w_q_up', 'mla_kv_norm', 'mla_w_kv_up', 'mla_w_o', 'dil_w_in', 'dil_w_o', 'rel_bias']
TWIN_DIFF_INPUT = 'x'
TWIN_INPUTS = ['x', 'c', 'norm_pre', 'norm_post', 'w_mod', 'b_mod', 'ffn_w_gate', 'ffn_w_up', 'ffn_w_down', 'mla_w_in', 'mla_q_norm', 'mla_w_q_up', 'mla_kv_norm', 'mla_w_kv_up', 'mla_w_o', 'dil_w_in', 'dil_w_o', 'rel_bias', 'loss_target', 'm_norm_pre', 'm_norm_post', 'm_w_mod', 'm_b_mod', 'm_ffn_w_gate', 'm_ffn_w_up', 'm_ffn_w_down', 'm_mla_w_in', 'm_mla_q_norm', 'm_mla_w_q_up', 'm_mla_kv_norm', 'm_mla_w_kv_up', 'm_mla_w_o', 'm_dil_w_in', 'm_dil_w_o', 'm_rel_bias', 'v_norm_pre', 'v_norm_post', 'v_w_mod', 'v_b_mod', 'v_ffn_w_gate', 'v_ffn_w_up', 'v_ffn_w_down', 'v_mla_w_in', 'v_mla_q_norm', 'v_mla_w_q_up', 'v_mla_kv_norm', 'v_mla_w_kv_up', 'v_mla_w_o', 'v_dil_w_in', 'v_dil_w_o', 'v_rel_bias']
TWIN_OUTPUTS = ['loss', 'grad_x', 'grad_norm_pre', 'grad_norm_post', 'grad_w_mod', 'grad_b_mod', 'grad_ffn_w_gate', 'grad_ffn_w_up', 'grad_ffn_w_down', 'grad_mla_w_in', 'grad_mla_q_norm', 'grad_mla_w_q_up', 'grad_mla_kv_norm', 'grad_mla_w_kv_up', 'grad_mla_w_o', 'grad_dil_w_in', 'grad_dil_w_o', 'grad_rel_bias', 'delta_norm_pre', 'delta_norm_post', 'delta_w_mod', 'delta_b_mod', 'delta_ffn_w_gate', 'delta_ffn_w_up', 'delta_ffn_w_down', 'delta_mla_w_in', 'delta_mla_q_norm', 'delta_mla_w_q_up', 'delta_mla_kv_norm', 'delta_mla_w_kv_up', 'delta_mla_w_o', 'delta_dil_w_in', 'delta_dil_w_o', 'delta_rel_bias', 'new_m_norm_pre', 'new_m_norm_post', 'new_m_w_mod', 'new_m_b_mod', 'new_m_ffn_w_gate', 'new_m_ffn_w_up', 'new_m_ffn_w_down', 'new_m_mla_w_in', 'new_m_mla_q_norm', 'new_m_mla_w_q_up', 'new_m_mla_kv_norm', 'new_m_mla_w_kv_up', 'new_m_mla_w_o', 'new_m_dil_w_in', 'new_m_dil_w_o', 'new_m_rel_bias', 'new_v_norm_pre', 'new_v_norm_post', 'new_v_w_mod', 'new_v_b_mod', 'new_v_ffn_w_gate', 'new_v_ffn_w_up', 'new_v_ffn_w_down', 'new_v_mla_w_in', 'new_v_mla_q_norm', 'new_v_mla_w_q_up', 'new_v_mla_kv_norm', 'new_v_mla_w_kv_up', 'new_v_mla_w_o', 'new_v_dil_w_in', 'new_v_dil_w_o', 'new_v_rel_bias']
TWIN_LEAF_KINDS = {'loss': 'loss', 'grad_x': 'grad_x', 'grad_norm_pre': 'grad_w', 'grad_norm_post': 'grad_w', 'grad_w_mod': 'grad_w', 'grad_b_mod': 'grad_w', 'grad_ffn_w_gate': 'grad_w', 'grad_ffn_w_up': 'grad_w', 'grad_ffn_w_down': 'grad_w', 'grad_mla_w_in': 'grad_w', 'grad_mla_q_norm': 'grad_w', 'grad_mla_w_q_up': 'grad_w', 'grad_mla_kv_norm': 'grad_w', 'grad_mla_w_kv_up': 'grad_w', 'grad_mla_w_o': 'grad_w', 'grad_dil_w_in': 'grad_w', 'grad_dil_w_o': 'grad_w', 'grad_rel_bias': 'grad_w', 'delta_norm_pre': 'delta_w', 'delta_norm_post': 'delta_w', 'delta_w_mod': 'delta_w', 'delta_b_mod': 'delta_w', 'delta_ffn_w_gate': 'delta_w', 'delta_ffn_w_up': 'delta_w', 'delta_ffn_w_down': 'delta_w', 'delta_mla_w_in': 'delta_w', 'delta_mla_q_norm': 'delta_w', 'delta_mla_w_q_up': 'delta_w', 'delta_mla_kv_norm': 'delta_w', 'delta_mla_w_kv_up': 'delta_w', 'delta_mla_w_o': 'delta_w', 'delta_dil_w_in': 'delta_w', 'delta_dil_w_o': 'delta_w', 'delta_rel_bias': 'delta_w', 'new_m_norm_pre': 'new_m', 'new_m_norm_post': 'new_m', 'new_m_w_mod': 'new_m', 'new_m_b_mod': 'new_m', 'new_m_ffn_w_gate': 'new_m', 'new_m_ffn_w_up': 'new_m', 'new_m_ffn_w_down': 'new_m', 'new_m_mla_w_in': 'new_m', 'new_m_mla_q_norm': 'new_m', 'new_m_mla_w_q_up': 'new_m', 'new_m_mla_kv_norm': 'new_m', 'new_m_mla_w_kv_up': 'new_m', 'new_m_mla_w_o': 'new_m', 'new_m_dil_w_in': 'new_m', 'new_m_dil_w_o': 'new_m', 'new_m_rel_bias': 'new_m', 'new_v_norm_pre': 'new_v', 'new_v_norm_post': 'new_v', 'new_v_w_mod': 'new_v', 'new_v_b_mod': 'new_v', 'new_v_ffn_w_gate': 'new_v', 'new_v_ffn_w_up': 'new_v', 'new_v_ffn_w_down': 'new_v', 'new_v_mla_w_in': 'new_v', 'new_v_mla_q_norm': 'new_v', 'new_v_mla_w_q_up': 'new_v', 'new_v_mla_kv_norm': 'new_v', 'new_v_mla_w_kv_up': 'new_v', 'new_v_mla_w_o': 'new_v', 'new_v_dil_w_in': 'new_v', 'new_v_dil_w_o': 'new_v', 'new_v_rel_bias': 'new_v'}


def _forward(args):
    return _fwd_reference(*[args[k] for k in FWD_PARAMS])


def _output_shape():
    out = _jax.eval_shape(lambda: _forward(_fwd_setup_inputs(0)))
    return out.shape, out.dtype

N_MICROBATCH = 1
ADAM_LR = 0.001
ADAM_B1 = 0.9
ADAM_B2 = 0.999
ADAM_EPS = 1e-08
ADAM_WD = 0.01
ADAM_STEP = 10
PER_EXAMPLE_BATCH_AXIS = {'x': 0, 'c': 0, 'loss_target': 0}
SHARED_INPUTS = []
_WEIGHT_DTYPES = {'norm_pre': _jnp.float32, 'norm_post': _jnp.float32, 'w_mod': _jnp.float32, 'b_mod': _jnp.float32, 'ffn_w_gate': _jnp.float32, 'ffn_w_up': _jnp.float32, 'ffn_w_down': _jnp.float32, 'mla_w_in': _jnp.float32, 'mla_q_norm': _jnp.float32, 'mla_w_q_up': _jnp.float32, 'mla_kv_norm': _jnp.float32, 'mla_w_kv_up': _jnp.float32, 'mla_w_o': _jnp.float32, 'dil_w_in': _jnp.float32, 'dil_w_o': _jnp.float32, 'rel_bias': _jnp.float32}
MOMENT_SCALE = {'norm_pre': 2.712261e-01, 'norm_post': 1.432422e+00, 'w_mod': 1.109271e+00, 'b_mod': 1.873025e+00, 'ffn_w_gate': 4.218479e-02, 'ffn_w_up': 6.169842e-02, 'ffn_w_down': 1.017293e-01, 'mla_w_in': 1.421850e+00, 'mla_q_norm': 8.261709e-02, 'mla_w_q_up': 4.280534e-02, 'mla_kv_norm': 2.258105e+00, 'mla_w_kv_up': 8.487106e-01, 'mla_w_o': 1.164483e+00, 'dil_w_in': 3.224964e-01, 'dil_w_o': 9.308959e-01, 'rel_bias': 1.535482e-01}


def _to_microbatches(a, axis):
    t = _jnp.moveaxis(a, axis, 0)
    t = t.reshape((N_MICROBATCH, t.shape[0] // N_MICROBATCH) + t.shape[1:])
    return _jnp.moveaxis(t, 1, axis + 1)


def setup_inputs(seed: int = 0) -> dict:
    inp = _fwd_setup_inputs(seed)
    key = _jax.random.fold_in(_jax.random.key(seed), 7919)
    shape, _ = _output_shape()
    out = dict(inp)
    out["loss_target"] = _jax.random.normal(_jax.random.fold_in(key, 0), shape, _jnp.float32)
    for i, name in enumerate(TWIN_WEIGHTS):
        w = inp[name].astype(_jnp.float32)
        if MOMENT_SCALE is None:
            s = _jnp.sqrt(_jnp.mean(_jnp.square(w)) + 1e-30)
        else:
            s = MOMENT_SCALE[name]
        km, kv = _jax.random.split(_jax.random.fold_in(key, i + 1))
        out[name] = w
        out["m_" + name] = s * _jax.random.normal(km, w.shape, _jnp.float32)
        out["v_" + name] = (s * s) * _jax.random.uniform(kv, w.shape, _jnp.float32, 0.5, 1.5)
    if N_MICROBATCH > 1:
        for name, axis in PER_EXAMPLE_BATCH_AXIS.items():
            out[name] = _to_microbatches(out[name], axis)
    return {'x': out['x'], 'c': out['c'], 'norm_pre': out['norm_pre'], 'norm_post': out['norm_post'], 'w_mod': out['w_mod'], 'b_mod': out['b_mod'], 'ffn_w_gate': out['ffn_w_gate'], 'ffn_w_up': out['ffn_w_up'], 'ffn_w_down': out['ffn_w_down'], 'mla_w_in': out['mla_w_in'], 'mla_q_norm': out['mla_q_norm'], 'mla_w_q_up': out['mla_w_q_up'], 'mla_kv_norm': out['mla_kv_norm'], 'mla_w_kv_up': out['mla_w_kv_up'], 'mla_w_o': out['mla_w_o'], 'dil_w_in': out['dil_w_in'], 'dil_w_o': out['dil_w_o'], 'rel_bias': out['rel_bias'], 'loss_target': out['loss_target'], 'm_norm_pre': out['m_norm_pre'], 'm_norm_post': out['m_norm_post'], 'm_w_mod': out['m_w_mod'], 'm_b_mod': out['m_b_mod'], 'm_ffn_w_gate': out['m_ffn_w_gate'], 'm_ffn_w_up': out['m_ffn_w_up'], 'm_ffn_w_down': out['m_ffn_w_down'], 'm_mla_w_in': out['m_mla_w_in'], 'm_mla_q_norm': out['m_mla_q_norm'], 'm_mla_w_q_up': out['m_mla_w_q_up'], 'm_mla_kv_norm': out['m_mla_kv_norm'], 'm_mla_w_kv_up': out['m_mla_w_kv_up'], 'm_mla_w_o': out['m_mla_w_o'], 'm_dil_w_in': out['m_dil_w_in'], 'm_dil_w_o': out['m_dil_w_o'], 'm_rel_bias': out['m_rel_bias'], 'v_norm_pre': out['v_norm_pre'], 'v_norm_post': out['v_norm_post'], 'v_w_mod': out['v_w_mod'], 'v_b_mod': out['v_b_mod'], 'v_ffn_w_gate': out['v_ffn_w_gate'], 'v_ffn_w_up': out['v_ffn_w_up'], 'v_ffn_w_down': out['v_ffn_w_down'], 'v_mla_w_in': out['v_mla_w_in'], 'v_mla_q_norm': out['v_mla_q_norm'], 'v_mla_w_q_up': out['v_mla_w_q_up'], 'v_mla_kv_norm': out['v_mla_kv_norm'], 'v_mla_w_kv_up': out['v_mla_w_kv_up'], 'v_mla_w_o': out['v_mla_w_o'], 'v_dil_w_in': out['v_dil_w_in'], 'v_dil_w_o': out['v_dil_w_o'], 'v_rel_bias': out['v_rel_bias']}


def _loss(weights, diff, rest, loss_target):
    with _jax.named_scope("forward"):
        args = {**rest, TWIN_DIFF_INPUT: diff, **{k: w.astype(_WEIGHT_DTYPES[k]) for k, w in weights.items()}}
        y = _forward(args)
    with _jax.named_scope("loss_head"):
        err = _jnp.square(y.astype(_jnp.float32) - loss_target)
        return 0.5 * _jnp.sum(_jnp.mean(err, axis=-1)) if err.ndim else 0.5 * err


def _adamw(w, g, m, v):
    m = ADAM_B1 * m + (1.0 - ADAM_B1) * g
    v = ADAM_B2 * v + (1.0 - ADAM_B2) * _jnp.square(g)
    m_hat = m / (1.0 - ADAM_B1 ** ADAM_STEP)
    v_hat = v / (1.0 - ADAM_B2 ** ADAM_STEP)
    delta = -ADAM_LR * (m_hat / (_jnp.sqrt(v_hat) + ADAM_EPS) + ADAM_WD * w)
    return delta, m, v


def reference(x, c, norm_pre, norm_post, w_mod, b_mod, ffn_w_gate, ffn_w_up, ffn_w_down, mla_w_in, mla_q_norm, mla_w_q_up, mla_kv_norm, mla_w_kv_up, mla_w_o, dil_w_in, dil_w_o, rel_bias, loss_target, m_norm_pre, m_norm_post, m_w_mod, m_b_mod, m_ffn_w_gate, m_ffn_w_up, m_ffn_w_down, m_mla_w_in, m_mla_q_norm, m_mla_w_q_up, m_mla_kv_norm, m_mla_w_kv_up, m_mla_w_o, m_dil_w_in, m_dil_w_o, m_rel_bias, v_norm_pre, v_norm_post, v_w_mod, v_b_mod, v_ffn_w_gate, v_ffn_w_up, v_ffn_w_down, v_mla_w_in, v_mla_q_norm, v_mla_w_q_up, v_mla_kv_norm, v_mla_w_kv_up, v_mla_w_o, v_dil_w_in, v_dil_w_o, v_rel_bias):
    given = dict(x=x, c=c, norm_pre=norm_pre, norm_post=norm_post, w_mod=w_mod, b_mod=b_mod, ffn_w_gate=ffn_w_gate, ffn_w_up=ffn_w_up, ffn_w_down=ffn_w_down, mla_w_in=mla_w_in, mla_q_norm=mla_q_norm, mla_w_q_up=mla_w_q_up, mla_kv_norm=mla_kv_norm, mla_w_kv_up=mla_w_kv_up, mla_w_o=mla_w_o, dil_w_in=dil_w_in, dil_w_o=dil_w_o, rel_bias=rel_bias, loss_target=loss_target, m_norm_pre=m_norm_pre, m_norm_post=m_norm_post, m_w_mod=m_w_mod, m_b_mod=m_b_mod, m_ffn_w_gate=m_ffn_w_gate, m_ffn_w_up=m_ffn_w_up, m_ffn_w_down=m_ffn_w_down, m_mla_w_in=m_mla_w_in, m_mla_q_norm=m_mla_q_norm, m_mla_w_q_up=m_mla_w_q_up, m_mla_kv_norm=m_mla_kv_norm, m_mla_w_kv_up=m_mla_w_kv_up, m_mla_w_o=m_mla_w_o, m_dil_w_in=m_dil_w_in, m_dil_w_o=m_dil_w_o, m_rel_bias=m_rel_bias, v_norm_pre=v_norm_pre, v_norm_post=v_norm_post, v_w_mod=v_w_mod, v_b_mod=v_b_mod, v_ffn_w_gate=v_ffn_w_gate, v_ffn_w_up=v_ffn_w_up, v_ffn_w_down=v_ffn_w_down, v_mla_w_in=v_mla_w_in, v_mla_q_norm=v_mla_q_norm, v_mla_w_q_up=v_mla_w_q_up, v_mla_kv_norm=v_mla_kv_norm, v_mla_w_kv_up=v_mla_w_kv_up, v_mla_w_o=v_mla_w_o, v_dil_w_in=v_dil_w_in, v_dil_w_o=v_dil_w_o, v_rel_bias=v_rel_bias)
    weights = {n: given[n] for n in TWIN_WEIGHTS}
    shared = {n: given[n] for n in SHARED_INPUTS}
    per_example = {n: given[n] for n in ['x', 'c']}
    grad_fn = _jax.value_and_grad(_loss, argnums=(0, 1))

    def one_microbatch(ex, loss_target):
        ex = dict(ex)
        diff = ex.pop(TWIN_DIFF_INPUT)
        return grad_fn(weights, diff, {**shared, **ex}, loss_target)

    if N_MICROBATCH == 1:
        loss, (grad_w, grad_x) = one_microbatch(per_example, given["loss_target"])
    else:
        def body(carry, xs):
            loss_sum, grad_sum = carry
            l_k, (gw_k, gx_k) = one_microbatch(xs[0], xs[1])
            with _jax.named_scope("update"):
                return (loss_sum + l_k, _jax.tree.map(_jnp.add, grad_sum, gw_k)), gx_k

        init = (_jnp.zeros((), _jnp.float32), _jax.tree.map(_jnp.zeros_like, weights))
        (loss, grad_w), grad_x = _jax.lax.scan(body, init, (per_example, given["loss_target"]))
    with _jax.named_scope("update"):
        delta_w, new_m, new_v = {}, {}, {}
        for n in TWIN_WEIGHTS:
            delta_w[n], new_m[n], new_v[n] = _adamw(weights[n], grad_w[n], given["m_" + n], given["v_" + n])
    return (loss, grad_x, *[grad_w[n] for n in TWIN_WEIGHTS], *[delta_w[n] for n in TWIN_WEIGHTS],
            *[new_m[n] for n in TWIN_WEIGHTS], *[new_v[n] for n in TWIN_WEIGHTS])
```

```python
import math

import jax
import jax.numpy as jnp
from jax import lax
from jax.experimental import pallas as pl
from jax.experimental.pallas import tpu as pltpu

F32 = jnp.float32
BF16 = jnp.bfloat16
MESH = pl.DeviceIdType.MESH

NDEV = 8
D_MODEL = 1024
SEQ = 2048
D_FF = 2816
EPS = 1e-6
FFN_RES = 0.5

MLA_HEADS = 16
Q_LORA = 384
KV_LORA = 256
QK_NOPE = 64
QK_ROPE = 32
V_HEAD = 64
ROPE_THETA = 10000.0
HEAD_PAD = 128
LAT_PAD = Q_LORA + KV_LORA + HEAD_PAD
MLA_SCALE = (QK_NOPE + QK_ROPE) ** -0.5

DIL_GROUPS = ((128, 1), (512, 4), (2048, 16))
DIL_HEADS = 16
DIL_HEAD_DIM = 64
DIL_BLOCK = 128
DIL_PAIRS = DIL_HEADS // 2
DIL_SCALE = DIL_HEAD_DIM ** -0.5
N_BUCKETS = 32
MAX_DISTANCE = 2048

ADAM_LR = 0.001
ADAM_B1 = 0.9
ADAM_B2 = 0.999
ADAM_EPS = 1e-08
ADAM_WD = 0.01
ADAM_STEP = 10

V7X_VMEM_BYTES = 64 * 2**20
VMEM_RESERVE = 10 * 2**20
TOKEN_TILE = 512


def _nbytes(shape, dtype):
    return math.prod(shape) * jnp.dtype(dtype).itemsize


def _params(semantics, blocks, extra=0):
    need = 2 * sum(_nbytes(s, d) for s, d in blocks) + extra + VMEM_RESERVE
    return pltpu.CompilerParams(dimension_semantics=semantics,
                                vmem_limit_bytes=int(min(need, V7X_VMEM_BYTES - VMEM_RESERVE)))


def _dot_nn(a, b):
    return lax.dot_general(a, b, (((1,), (0,)), ((), ())), preferred_element_type=F32)


def _dot_nt(a, b):
    return lax.dot_general(a, b, (((1,), (1,)), ((), ())), preferred_element_type=F32)


def _dot_tn(a, b):
    return lax.dot_general(a, b, (((0,), (0,)), ((), ())), preferred_element_type=F32)


_DOTS = {"nn": _dot_nn, "nt": _dot_nt, "tn": _dot_tn}


def _rstd(v):
    return lax.rsqrt(jnp.mean(v * v, axis=-1, keepdims=True) + EPS)


def _rms_bwd(v, r, t):
    return r * t - v * (r * r * r) * jnp.mean(t * v, axis=-1, keepdims=True)


def _mm(pairs, mode, out_dtype, tm, tn, name, out_perm=1):
    a0, b0 = pairs[0]
    m_dim = a0.shape[1] if mode == "tn" else a0.shape[0]
    n_dim = b0.shape[0] if mode == "nt" else b0.shape[1]
    tm, tn = min(tm, m_dim // out_perm), min(tn, n_dim)
    assert m_dim % tm == 0 and n_dim % tn == 0, (name, m_dim, n_dim, tm, tn)
    dot = _DOTS[mode]
    npairs = len(pairs)

    def body(*refs):
        acc = None
        for p in range(npairs):
            d = dot(refs[2 * p][...].astype(BF16), refs[2 * p + 1][...].astype(BF16))
            acc = d if acc is None else acc + d
        refs[-1][...] = acc.astype(out_dtype)

    in_specs, blocks, flat = [], [], []
    for a, b in pairs:
        if mode == "nn":
            k = a.shape[1]
            sa, sb = ((tm, k), lambda i, j: (i, 0)), ((k, tn), lambda i, j: (0, j))
        elif mode == "nt":
            k = a.shape[1]
            sa, sb = ((tm, k), lambda i, j: (i, 0)), ((tn, k), lambda i, j: (j, 0))
        else:
            k = a.shape[0]
            sa, sb = ((k, tm), lambda i, j: (0, i)), ((k, tn), lambda i, j: (0, j))
        in_specs += [pl.BlockSpec(*sa), pl.BlockSpec(*sb)]
        blocks += [(sa[0], a.dtype), (sb[0], b.dtype)]
        flat += [a, b]
    if out_perm == 1:
        out_shape = (m_dim, n_dim)
        out_spec = pl.BlockSpec((tm, tn), lambda i, j: (i, j))
    else:
        rows = m_dim // out_perm
        assert tn == n_dim and rows % tm == 0, (name, rows, tm)
        nb = rows // tm
        out_shape = (rows, out_perm * n_dim)
        out_spec = pl.BlockSpec((tm, n_dim), lambda i, j: (i % nb, i // nb))
    blocks.append(((tm, tn), out_dtype))
    res = pl.pallas_call(
        body, out_shape=jax.ShapeDtypeStruct(out_shape, out_dtype), grid=(m_dim // tm, n_dim // tn),
        in_specs=in_specs, out_specs=out_spec, name=name,
        compiler_params=_params(("parallel", "parallel"), blocks, extra=2 * tm * tn * 4),
    )(*flat)
    return res.reshape(m_dim, n_dim)


def _prenorm_mm(x, pre_g, scale, shift, w, w_mode, out_dtype, tn, name, perm=1):
    s_dim, d_dim = x.shape
    n_dim = w.shape[0] if w_mode == "nt" else w.shape[1]
    rows = s_dim // perm
    tm = min(TOKEN_TILE, rows)
    nb = rows // tm
    tn = min(tn, n_dim)
    assert n_dim % tn == 0
    dot = _DOTS[w_mode]

    def body(x_ref, g_ref, sc_ref, sh_ref, w_ref, hn_ref, o_ref):
        @pl.when(pl.program_id(1) == 0)
        def _():
            xf = x_ref[...]
            hn = (xf * _rstd(xf) * g_ref[...]) * (1.0 + sc_ref[...]) + sh_ref[...]
            hn_ref[...] = hn.astype(BF16)

        o_ref[...] = dot(hn_ref[...], w_ref[...]).astype(out_dtype)

    vec = pl.BlockSpec((1, d_dim), lambda i, j: (0, 0))
    w_block = (tn, d_dim) if w_mode == "nt" else (d_dim, tn)
    w_spec = pl.BlockSpec(w_block, (lambda i, j: (j, 0)) if w_mode == "nt" else (lambda i, j: (0, j)))
    hn, out = pl.pallas_call(
        body,
        out_shape=(jax.ShapeDtypeStruct((s_dim, d_dim), BF16), jax.ShapeDtypeStruct((s_dim, n_dim), out_dtype)),
        grid=(s_dim // tm, n_dim // tn),
        in_specs=[pl.BlockSpec((tm, d_dim), lambda i, j: (i % nb, i // nb)), vec, vec, vec, w_spec],
        out_specs=(pl.BlockSpec((tm, d_dim), lambda i, j: (i, 0)), pl.BlockSpec((tm, tn), lambda i, j: (i, j))),
        name=name,
        compiler_params=_params(("parallel", "arbitrary"),
                                [((tm, d_dim), F32), (w_block, BF16), ((tm, d_dim), BF16), ((tm, tn), out_dtype)],
                                extra=3 * tm * d_dim * 4 + tm * tn * 4),
    )(x.reshape(rows, perm * d_dim), pre_g, scale, shift, w)
    return hn, out


def _ffn_up(x, pre_g, scale, shift, wg_t, wu_t, name):
    s_dim, d_dim = x.shape
    f_dim = wg_t.shape[0]
    tm, tn = TOKEN_TILE, f_dim // 2

    def body(x_ref, g_ref, sc_ref, sh_ref, wg_ref, wu_ref, hn_ref, go_ref, uo_ref, a_ref):
        @pl.when(pl.program_id(1) == 0)
        def _():
            xf = x_ref[...]
            hn = (xf * _rstd(xf) * g_ref[...]) * (1.0 + sc_ref[...]) + sh_ref[...]
            hn_ref[...] = hn.astype(BF16)

        hn = hn_ref[...]
        g = _dot_nt(hn, wg_ref[...])
        u = _dot_nt(hn, wu_ref[...])
        go_ref[...] = g.astype(BF16)
        uo_ref[...] = u.astype(BF16)
        a_ref[...] = (g * jax.nn.sigmoid(g) * u).astype(BF16)

    vec = pl.BlockSpec((1, d_dim), lambda i, j: (0, 0))
    w_spec = pl.BlockSpec((tn, d_dim), lambda i, j: (j, 0))
    act = pl.BlockSpec((tm, tn), lambda i, j: (i, j))
    act_shape = jax.ShapeDtypeStruct((s_dim, f_dim), BF16)
    return pl.pallas_call(
        body,
        out_shape=(jax.ShapeDtypeStruct((s_dim, d_dim), BF16), act_shape, act_shape, act_shape),
        grid=(s_dim // tm, f_dim // tn),
        in_specs=[pl.BlockSpec((tm, d_dim), lambda i, j: (i, 0)), vec, vec, vec, w_spec, w_spec],
        out_specs=(pl.BlockSpec((tm, d_dim), lambda i, j: (i, 0)), act, act, act),
        name=name,
        compiler_params=_params(("parallel", "arbitrary"),
                                [((tm, d_dim), F32), ((tn, d_dim), BF16), ((tn, d_dim), BF16), ((tm, d_dim), BF16)]
                                + 3 * [((tm, tn), BF16)], extra=3 * tm * d_dim * 4 + 4 * tm * tn * 4),
    )(x, pre_g, scale, shift, wg_t, wu_t)


def _mm_post(a, w, x, post_g, gate, res_w, name):
    s_dim, k_dim = a.shape
    d_dim = w.shape[1]
    tm = TOKEN_TILE

    def body(a_ref, w_ref, x_ref, pg_ref, gt_ref, xo_ref, f_ref):
        f = _dot_nn(a_ref[...], w_ref[...])
        y = f * _rstd(f) * pg_ref[...]
        f_ref[...] = f
        xo_ref[...] = x_ref[...] + (res_w * gt_ref[...]) * y

    vec = pl.BlockSpec((1, d_dim), lambda i: (0, 0))
    row = pl.BlockSpec((tm, d_dim), lambda i: (i, 0))
    out = jax.ShapeDtypeStruct((s_dim, d_dim), F32)
    return pl.pallas_call(
        body, out_shape=(out, out), grid=(s_dim // tm,),
        in_specs=[pl.BlockSpec((tm, k_dim), lambda i: (i, 0)), pl.BlockSpec((k_dim, d_dim), lambda i: (0, 0)), row, vec, vec],
        out_specs=(row, row), name=name,
        compiler_params=_params(("parallel",), [((tm, k_dim), BF16), ((k_dim, d_dim), BF16)] + 3 * [((tm, d_dim), F32)],
                                extra=3 * tm * d_dim * 4),
    )(a, w, x, post_g, gate)


def _post_bwd(dx_out, f, post_g, gate, res_w, name):
    s_dim, d_dim = f.shape
    tm = TOKEN_TILE

    def body(dx_ref, f_ref, pg_ref, gt_ref, df_ref, dgate_ref, dpost_ref):
        @pl.when(pl.program_id(0) == 0)
        def _():
            dgate_ref[...] = jnp.zeros_like(dgate_ref)
            dpost_ref[...] = jnp.zeros_like(dpost_ref)

        dx, fv = dx_ref[...], f_ref[...]
        r = _rstd(fv)
        fr = fv * r
        dgate_ref[...] += res_w * jnp.sum(dx * (fr * pg_ref[...]), axis=0, keepdims=True)
        dy = (res_w * gt_ref[...]) * dx
        dpost_ref[...] += jnp.sum(dy * fr, axis=0, keepdims=True)
        df_ref[...] = _rms_bwd(fv, r, dy * pg_ref[...]).astype(BF16)

    vec = pl.BlockSpec((1, d_dim), lambda i: (0, 0))
    row = pl.BlockSpec((tm, d_dim), lambda i: (i, 0))
    vshape = jax.ShapeDtypeStruct((1, d_dim), F32)
    return pl.pallas_call(
        body, out_shape=(jax.ShapeDtypeStruct((s_dim, d_dim), BF16), vshape, vshape), grid=(s_dim // tm,),
        in_specs=[row, row, vec, vec], out_specs=(row, vec, vec), name=name,
        compiler_params=_params(("arbitrary",), 3 * [((tm, d_dim), F32)], extra=6 * tm * d_dim * 4),
    )(dx_out, f, post_g, gate)


def _prenorm_bwd(dx_out, dhns, x, pre_g, scale, name):
    s_dim, d_dim = x.shape
    tm = TOKEN_TILE
    n_in = len(dhns)

    def body(*refs):
        dx_ref, x_ref, pg_ref, sc_ref = refs[n_in + 0], refs[n_in + 1], refs[n_in + 2], refs[n_in + 3]
        dxo_ref, dsh_ref, dsc_ref, dpg_ref = refs[n_in + 4:]

        @pl.when(pl.program_id(0) == 0)
        def _():
            dsh_ref[...] = jnp.zeros_like(dsh_ref)
            dsc_ref[...] = jnp.zeros_like(dsc_ref)
            dpg_ref[...] = jnp.zeros_like(dpg_ref)

        dhn = refs[0][...]
        for k in range(1, n_in):
            dhn = dhn + refs[k][...]
        xv = x_ref[...]
        r = _rstd(xv)
        xr = xv * r
        dsh_ref[...] += jnp.sum(dhn, axis=0, keepdims=True)
        dsc_ref[...] += jnp.sum(dhn * (xr * pg_ref[...]), axis=0, keepdims=True)
        dn = dhn * (1.0 + sc_ref[...])
        dpg_ref[...] += jnp.sum(dn * xr, axis=0, keepdims=True)
        dxo_ref[...] = dx_ref[...] + _rms_bwd(xv, r, dn * pg_ref[...])

    vec = pl.BlockSpec((1, d_dim), lambda i: (0, 0))
    row = pl.BlockSpec((tm, d_dim), lambda i: (i, 0))
    vshape = jax.ShapeDtypeStruct((1, d_dim), F32)
    return pl.pallas_call(
        body, out_shape=(jax.ShapeDtypeStruct((s_dim, d_dim), F32), vshape, vshape, vshape), grid=(s_dim // tm,),
        in_specs=n_in * [row] + [row, row, vec, vec], out_specs=(row, vec, vec, vec), name=name,
        compiler_params=_params(("arbitrary",), (n_in + 3) * [((tm, d_dim), F32)], extra=6 * tm * d_dim * 4),
    )(*dhns, dx_out, x, pre_g, scale)


def _ffn_dgu(df, wd, g, u, name):
    s_dim, d_dim = df.shape
    f_dim = wd.shape[0]
    tm, tn = TOKEN_TILE, f_dim // 2

    def body(df_ref, wd_ref, g_ref, u_ref, dg_ref, du_ref):
        da = _dot_nt(df_ref[...], wd_ref[...])
        gv, uv = g_ref[...].astype(F32), u_ref[...].astype(F32)
        sg = jax.nn.sigmoid(gv)
        du_ref[...] = (da * (gv * sg)).astype(BF16)
        dg_ref[...] = (da * uv * (sg * (1.0 + gv * (1.0 - sg)))).astype(BF16)

    act = pl.BlockSpec((tm, tn), lambda i, j: (i, j))
    act_shape = jax.ShapeDtypeStruct((s_dim, f_dim), BF16)
    return pl.pallas_call(
        body, out_shape=(act_shape, act_shape), grid=(s_dim // tm, f_dim // tn),
        in_specs=[pl.BlockSpec((tm, d_dim), lambda i, j: (i, 0)), pl.BlockSpec((tn, d_dim), lambda i, j: (j, 0)), act, act],
        out_specs=(act, act), name=name,
        compiler_params=_params(("parallel", "parallel"), [((tm, d_dim), BF16), ((tn, d_dim), BF16)] + 4 * [((tm, tn), BF16)],
                                extra=6 * tm * tn * 4),
    )(df, wd, g, u)


def _rope_tables():
    half = QK_ROPE // 2
    freqs = ROPE_THETA ** (-jnp.arange(half, dtype=F32) / half)
    ang = jnp.arange(SEQ, dtype=F32)[:, None] * freqs[None, :]
    cos, sin = jnp.cos(ang), jnp.sin(ang)
    ones = jnp.ones((SEQ, QK_NOPE), F32)
    zeros = jnp.zeros((SEQ, QK_NOPE), F32)
    pad1 = jnp.ones((SEQ, HEAD_PAD - QK_NOPE - QK_ROPE), F32)
    pad0 = jnp.zeros((SEQ, HEAD_PAD - QK_NOPE - QK_ROPE), F32)
    zh = jnp.zeros((SEQ, half), F32)
    c = jnp.concatenate([ones, cos, cos, pad1], axis=1)
    s1 = jnp.concatenate([zeros, -sin, zh, pad0], axis=1)
    s2 = jnp.concatenate([zeros, zh, sin, pad0], axis=1)
    return c, s1, s2


def _rope(v, c, s1, s2):
    half = QK_ROPE // 2
    return v * c + pltpu.roll(v, HEAD_PAD - half, 1) * s1 + pltpu.roll(v, half, 1) * s2


def _rope_t(dv, c, s1, s2):
    half = QK_ROPE // 2
    return dv * c + pltpu.roll(dv * s1, half, 1) + pltpu.roll(dv * s2, HEAD_PAD - half, 1)


def _mla_qkv(lat, q_norm, kv_norm, wq_t, wkv_t, rope, name):
    s_dim = lat.shape[0]
    width = MLA_HEADS * HEAD_PAD
    tm = 256

    def body(lat_ref, qg_ref, kg_ref, wq_ref, wkv_ref, c_ref, s1_ref, s2_ref, q_ref, k_ref, v_ref, qn_ref, kvn_ref):
        cq = lat_ref[:, :Q_LORA]
        ckv = lat_ref[:, Q_LORA:Q_LORA + KV_LORA]
        kr = lat_ref[:, Q_LORA + KV_LORA:]
        c, s1, s2 = c_ref[...], s1_ref[...], s2_ref[...]
        qn = (cq * _rstd(cq) * qg_ref[...]).astype(BF16)
        kvn = (ckv * _rstd(ckv) * kg_ref[...]).astype(BF16)
        qn_ref[...] = qn
        kvn_ref[...] = kvn
        q = _dot_nt(qn, wq_ref[...])
        kv = _dot_nt(kvn, wkv_ref[...])
        krr = _rope(kr, c, s1, s2)
        low = lax.broadcasted_iota(jnp.int32, (tm, HEAD_PAD), 1) < QK_NOPE
        for h in range(MLA_HEADS):
            sl = slice(h * HEAD_PAD, (h + 1) * HEAD_PAD)
            q_ref[:, sl] = _rope(q[:, sl], c, s1, s2).astype(BF16)
            kvh = kv[:, sl]
            k_ref[:, sl] = (jnp.where(low, kvh, 0.0) + krr).astype(BF16)
            v_ref[:, sl] = jnp.where(low, 0.0, kvh).astype(BF16)

    row = lambda n: pl.BlockSpec((tm, n), lambda i: (i, 0))
    full = lambda a: pl.BlockSpec(a.shape, lambda i: (0, 0))
    wide = jax.ShapeDtypeStruct((s_dim, width), BF16)
    return pl.pallas_call(
        body,
        out_shape=(wide, wide, wide, jax.ShapeDtypeStruct((s_dim, Q_LORA), BF16), jax.ShapeDtypeStruct((s_dim, KV_LORA), BF16)),
        grid=(s_dim // tm,),
        in_specs=[row(LAT_PAD), full(q_norm), full(kv_norm), full(wq_t), full(wkv_t), row(HEAD_PAD), row(HEAD_PAD), row(HEAD_PAD)],
        out_specs=(row(width), row(width), row(width), row(Q_LORA), row(KV_LORA)), name=name,
        compiler_params=_params(("parallel",), [((tm, LAT_PAD), F32), (wq_t.shape, BF16), (wkv_t.shape, BF16)]
                                + 3 * [((tm, width), BF16)], extra=4 * tm * width * 4),
    )(lat, q_norm, kv_norm, wq_t, wkv_t, *rope)


def _mla_probs(q, k, t, tq):
    s = _dot_nt(q, k) * MLA_SCALE
    rows = lax.broadcasted_iota(jnp.int32, s.shape, 0) + t * tq
    cols = lax.broadcasted_iota(jnp.int32, s.shape, 1)
    s = jnp.where(cols <= rows, s, -jnp.inf)
    e = jnp.exp(s - jnp.max(s, axis=-1, keepdims=True))
    return e / jnp.sum(e, axis=-1, keepdims=True)


def _mla_attn_fwd(q, k, v, name):
    s_dim = q.shape[0]
    tq = 512

    def body(q_ref, k_ref, v_ref, o_ref):
        for t in range(s_dim // tq):
            kt = (t + 1) * tq
            p = _mla_probs(q_ref[t * tq:kt, :], k_ref[:kt, :], t, tq)
            o_ref[t * tq:kt, :] = _dot_nn(p.astype(BF16), v_ref[:kt, :]).astype(BF16)

    head = pl.BlockSpec((s_dim, HEAD_PAD), lambda h: (0, h))
    return pl.pallas_call(
        body, out_shape=jax.ShapeDtypeStruct(q.shape, BF16), grid=(MLA_HEADS,),
        in_specs=[head, head, head], out_specs=head, name=name,
        compiler_params=_params(("parallel",), 4 * [((s_dim, HEAD_PAD), BF16)], extra=4 * tq * s_dim * 4),
    )(q, k, v)


def _mla_attn_bwd(q, k, v, d_o, name):
    s_dim = q.shape[0]
    tq = 512

    def body(q_ref, k_ref, v_ref, do_ref, dq_ref, dk_ref, dv_ref):
        dk_ref[...] = jnp.zeros_like(dk_ref)
        dv_ref[...] = jnp.zeros_like(dv_ref)
        for t in range(s_dim // tq):
            kt = (t + 1) * tq
            qt = q_ref[t * tq:kt, :]
            dot = do_ref[t * tq:kt, :].astype(BF16)
            p = _mla_probs(qt, k_ref[:kt, :], t, tq)
            dp = _dot_nt(dot, v_ref[:kt, :])
            ds = p * (dp - jnp.sum(p * dp, axis=-1, keepdims=True))
            dsb = (ds * MLA_SCALE).astype(BF16)
            dq_ref[t * tq:kt, :] = _dot_nn(dsb, k_ref[:kt, :])
            dk_ref[:kt, :] += _dot_tn(dsb, qt)
            dv_ref[:kt, :] += _dot_tn(p.astype(BF16), dot)

    head = pl.BlockSpec((s_dim, HEAD_PAD), lambda h: (0, h))
    out = jax.ShapeDtypeStruct(q.shape, F32)
    return pl.pallas_call(
        body, out_shape=(out, out, out), grid=(MLA_HEADS,),
        in_specs=[head, head, head, head], out_specs=(head, head, head), name=name,
        compiler_params=_params(("parallel",), 3 * [((s_dim, HEAD_PAD), BF16)] + 4 * [((s_dim, HEAD_PAD), F32)],
                                extra=6 * tq * s_dim * 4),
    )(q, k, v, d_o)


def _mla_qkv_bwd(dq, dk, dv, lat, q_norm, kv_norm, wq_t, wkv_t, rope, name):
    s_dim = lat.shape[0]
    width = MLA_HEADS * HEAD_PAD
    tm = 256

    def body(dq_ref, dk_ref, dv_ref, lat_ref, qg_ref, kg_ref, wq_ref, wkv_ref, c_ref, s1_ref, s2_ref,
             dqp_ref, dkv_ref, dlat_ref, dqg_ref, dkg_ref):
        @pl.when(pl.program_id(0) == 0)
        def _():
            dqg_ref[...] = jnp.zeros_like(dqg_ref)
            dkg_ref[...] = jnp.zeros_like(dkg_ref)

        c, s1, s2 = c_ref[...], s1_ref[...], s2_ref[...]
        lane = lax.broadcasted_iota(jnp.int32, (tm, HEAD_PAD), 1)
        low = lane < QK_NOPE
        rot = (lane >= QK_NOPE) & (lane < QK_NOPE + QK_ROPE)
        dkrr = jnp.zeros((tm, HEAD_PAD), F32)
        for h in range(MLA_HEADS):
            sl = slice(h * HEAD_PAD, (h + 1) * HEAD_PAD)
            dqp_ref[:, sl] = _rope_t(dq_ref[:, sl], c, s1, s2).astype(BF16)
            dkh = dk_ref[:, sl]
            dkv_ref[:, sl] = jnp.where(low, dkh, dv_ref[:, sl]).astype(BF16)
            dkrr = dkrr + jnp.where(rot, dkh, 0.0)
        dqn = _dot_nn(dqp_ref[...], wq_ref[...])
        dkvn = _dot_nn(dkv_ref[...], wkv_ref[...])
        cq = lat_ref[:, :Q_LORA]
        ckv = lat_ref[:, Q_LORA:Q_LORA + KV_LORA]
        rq, rkv = _rstd(cq), _rstd(ckv)
        dqg_ref[...] += jnp.sum(dqn * cq * rq, axis=0, keepdims=True)
        dkg_ref[...] += jnp.sum(dkvn * ckv * rkv, axis=0, keepdims=True)
        dlat_ref[:, :Q_LORA] = _rms_bwd(cq, rq, dqn * qg_ref[...])
        dlat_ref[:, Q_LORA:Q_LORA + KV_LORA] = _rms_bwd(ckv, rkv, dkvn * kg_ref[...])
        dlat_ref[:, Q_LORA + KV_LORA:] = _rope_t(dkrr, c, s1, s2)

    row = lambda n: pl.BlockSpec((tm, n), lambda i: (i, 0))
    full = lambda a: pl.BlockSpec(a.shape, lambda i: (0, 0))
    wide = jax.ShapeDtypeStruct((s_dim, width), BF16)
    return pl.pallas_call(
        body,
        out_shape=(wide, wide, jax.ShapeDtypeStruct((s_dim, LAT_PAD), F32),
                   jax.ShapeDtypeStruct(q_norm.shape, F32), jax.ShapeDtypeStruct(kv_norm.shape, F32)),
        grid=(s_dim // tm,),
        in_specs=[row(width), row(width), row(width), row(LAT_PAD), full(q_norm), full(kv_norm), full(wq_t), full(wkv_t),
                  row(HEAD_PAD), row(HEAD_PAD), row(HEAD_PAD)],
        out_specs=(row(width), row(width), row(LAT_PAD), full(q_norm), full(kv_norm)), name=name,
        compiler_params=_params(("arbitrary",), 3 * [((tm, width), F32)] + [((tm, LAT_PAD), F32), (wq_t.shape, BF16),
                                                                           (wkv_t.shape, BF16)] + 2 * [((tm, width), BF16)],
                                extra=2 * tm * width * 4),
    )(dq, dk, dv, lat, q_norm, kv_norm, wq_t, wkv_t, *rope)


def _t5_bucket(dist):
    max_exact = N_BUCKETS // 2
    d = jnp.maximum(dist, 1).astype(F32)
    large = max_exact + (jnp.log(d / max_exact) / math.log(MAX_DISTANCE / max_exact)
                         * (N_BUCKETS - max_exact)).astype(jnp.int32)
    large = jnp.minimum(large, N_BUCKETS - 1)
    return jnp.where(dist < max_exact, dist, large)


def _dil_buckets(dilation):
    iq = jnp.arange(DIL_BLOCK)[:, None]
    ik = jnp.arange(2 * DIL_BLOCK)[None, :]
    return _t5_bucket(jnp.maximum(DIL_BLOCK + iq - ik, 0) * dilation)


def _dil_logits(qh, k_ref, bias_h, n, span):
    lo = n * DIL_BLOCK
    if n == 0:
        s = _dot_nt(qh, k_ref[lo:lo + DIL_BLOCK, :]) * DIL_SCALE + bias_h[:, DIL_BLOCK:]
        rel = lax.broadcasted_iota(jnp.int32, s.shape, 0) - lax.broadcasted_iota(jnp.int32, s.shape, 1)
    else:
        s = _dot_nt(qh, k_ref[lo - DIL_BLOCK:lo + DIL_BLOCK, :]) * DIL_SCALE + bias_h
        rel = DIL_BLOCK + lax.broadcasted_iota(jnp.int32, s.shape, 0) - lax.broadcasted_iota(jnp.int32, s.shape, 1)
    return jnp.where((rel >= 0) & (rel <= span), s, -jnp.inf)


def _dil_views(dilation, rows):
    col = lambda which: pl.BlockSpec((rows, HEAD_PAD), lambda p, r: (r, which * DIL_PAIRS + p))
    nat = pl.BlockSpec((rows, HEAD_PAD), lambda p, r: (0, r * DIL_PAIRS + p))
    bias = pl.BlockSpec((2, DIL_BLOCK, 2 * DIL_BLOCK), lambda p, r: (p, 0, 0))
    return col, nat, bias


def _dil_attn_fwd(qkv, bias, dilation, span, name):
    s_dim = qkv.shape[0]
    rows = s_dim // dilation
    d_dim = DIL_HEADS * DIL_HEAD_DIM
    col, nat, bias_spec = _dil_views(dilation, rows)

    def body(q_ref, k_ref, v_ref, b_ref, o_ref, l_ref):
        lane = lax.broadcasted_iota(jnp.int32, (DIL_BLOCK, HEAD_PAD), 1)
        klane = lax.broadcasted_iota(jnp.int32, (2 * DIL_BLOCK, HEAD_PAD), 1)
        for n in range(rows // DIL_BLOCK):
            lo = n * DIL_BLOCK
            kv_rows = slice(lo, lo + DIL_BLOCK) if n == 0 else slice(lo - DIL_BLOCK, lo + DIL_BLOCK)
            qb, vb = q_ref[lo:lo + DIL_BLOCK, :], v_ref[kv_rows, :]
            o_acc = jnp.zeros((DIL_BLOCK, HEAD_PAD), F32)
            lse_acc = jnp.zeros((DIL_BLOCK, HEAD_PAD), F32)
            for h in range(2):
                mine = (lane < DIL_HEAD_DIM) == (h == 0)
                kmine = (klane[:vb.shape[0]] < DIL_HEAD_DIM) == (h == 0)
                logits = _dil_logits(jnp.where(mine, qb, 0), k_ref, b_ref[h], n, span)
                mx = jnp.max(logits, axis=-1, keepdims=True)
                lse = mx + jnp.log(jnp.sum(jnp.exp(logits - mx), axis=-1, keepdims=True))
                p = jnp.exp(logits - lse)
                o_acc = o_acc + _dot_nn(p.astype(BF16), jnp.where(kmine, vb, 0))
                lse_acc = jnp.where(mine, lse, lse_acc)
            o_ref[lo:lo + DIL_BLOCK, :] = o_acc
            l_ref[lo:lo + DIL_BLOCK, :] = lse_acc

    out = jax.ShapeDtypeStruct((rows, dilation * d_dim), F32)
    o, lse = pl.pallas_call(
        body, out_shape=(out, out), grid=(DIL_PAIRS, dilation),
        in_specs=[col(0), col(1), col(2), bias_spec], out_specs=(nat, nat), name=name,
        compiler_params=_params(("parallel", "parallel"), 3 * [((rows, HEAD_PAD), BF16)] + 2 * [((rows, HEAD_PAD), F32)]
                                + [((2, DIL_BLOCK, 2 * DIL_BLOCK), F32)], extra=2**21),
    )(qkv, qkv, qkv, bias)
    return o.reshape(s_dim, d_dim), lse.reshape(s_dim, d_dim)


def _dil_mix(lses, outs, name):
    s_dim, d_dim = outs[0].shape
    tm = TOKEN_TILE
    ng = len(outs)

    def body(*refs):
        ls = [refs[g][...] for g in range(ng)]
        mx = ls[0]
        for g in range(1, ng):
            mx = jnp.maximum(mx, ls[g])
        es = [jnp.exp(l - mx) for l in ls]
        tot = es[0]
        for g in range(1, ng):
            tot = tot + es[g]
        o = None
        for g in range(ng):
            al = es[g] / tot
            refs[2 * ng + g][...] = al
            t = al * refs[ng + g][...]
            o = t if o is None else o + t
        refs[3 * ng][...] = o
        refs[3 * ng + 1][...] = o.astype(BF16)

    row = pl.BlockSpec((tm, d_dim), lambda i: (i, 0))
    f = jax.ShapeDtypeStruct((s_dim, d_dim), F32)
    res = pl.pallas_call(
        body, out_shape=tuple(ng * [f] + [f, jax.ShapeDtypeStruct((s_dim, d_dim), BF16)]), grid=(s_dim // tm,),
        in_specs=2 * ng * [row], out_specs=tuple((ng + 2) * [row]), name=name,
        compiler_params=_params(("parallel",), (3 * ng + 2) * [((tm, d_dim), F32)], extra=4 * tm * d_dim * 4),
    )(*lses, *outs)
    return res[:ng], res[ng], res[ng + 1]


def _dil_attn_bwd(qkv, bias, d_o, o_mix, alpha, lse, dilation, span, name):
    s_dim = qkv.shape[0]
    rows = s_dim // dilation
    d_dim = DIL_HEADS * DIL_HEAD_DIM
    col, nat, bias_spec = _dil_views(dilation, rows)
    nat_view = lambda a: a.reshape(rows, dilation * d_dim)

    def body(q_ref, k_ref, v_ref, b_ref, do_ref, om_ref, al_ref, l_ref, dq_ref, dk_ref, dv_ref, db_ref, dk_acc, dv_acc):
        @pl.when(pl.program_id(1) == 0)
        def _():
            db_ref[...] = jnp.zeros_like(db_ref)

        dk_acc[...] = jnp.zeros_like(dk_acc)
        dv_acc[...] = jnp.zeros_like(dv_acc)
        lane = lax.broadcasted_iota(jnp.int32, (DIL_BLOCK, HEAD_PAD), 1)
        klane = lax.broadcasted_iota(jnp.int32, (2 * DIL_BLOCK, HEAD_PAD), 1)
        for n in range(rows // DIL_BLOCK):
            lo = n * DIL_BLOCK
            blk = slice(lo, lo + DIL_BLOCK)
            kv_rows = blk if n == 0 else slice(lo - DIL_BLOCK, lo + DIL_BLOCK)
            qb, kb, vb = q_ref[blk, :], k_ref[kv_rows, :], v_ref[kv_rows, :]
            al = al_ref[blk, :]
            dog = al * do_ref[blk, :]
            row_term = dog * om_ref[blk, :]
            lse_b = l_ref[blk, :]
            dq_acc = jnp.zeros((DIL_BLOCK, HEAD_PAD), F32)
            dk_blk = jnp.zeros((kb.shape[0], HEAD_PAD), F32)
            dv_blk = jnp.zeros((kb.shape[0], HEAD_PAD), F32)
            for h in range(2):
                mine = (lane < DIL_HEAD_DIM) == (h == 0)
                kmine = (klane[:kb.shape[0]] < DIL_HEAD_DIM) == (h == 0)
                qh = jnp.where(mine, qb, 0)
                logits = _dil_logits(qh, k_ref, b_ref[h], n, span)
                lse_h = jnp.max(jnp.where(mine, lse_b, -jnp.inf), axis=-1, keepdims=True)
                p = jnp.exp(logits - lse_h)
                dogh = jnp.where(mine, dog, 0.0).astype(BF16)
                dp = _dot_nt(dogh, vb)
                ds = p * (dp - jnp.sum(jnp.where(mine, row_term, 0.0), axis=-1, keepdims=True))
                if n == 0:
                    db_ref[h, :, DIL_BLOCK:] += ds
                else:
                    db_ref[h] += ds
                dsb = (ds * DIL_SCALE).astype(BF16)
                dq_acc = dq_acc + _dot_nn(dsb, jnp.where(kmine, kb, 0))
                dk_blk = dk_blk + _dot_tn(dsb, qh)
                dv_blk = dv_blk + _dot_tn(p.astype(BF16), dogh)
            dq_ref[blk, :] = dq_acc.astype(BF16)
            dk_acc[kv_rows, :] += dk_blk
            dv_acc[kv_rows, :] += dv_blk
        dk_ref[...] = dk_acc[...].astype(BF16)
        dv_ref[...] = dv_acc[...].astype(BF16)

    out_col = pl.BlockSpec((rows, HEAD_PAD), lambda p, r: (r, p))
    grad = jax.ShapeDtypeStruct((s_dim, d_dim), BF16)
    return pl.pallas_call(
        body, out_shape=(grad, grad, grad, jax.ShapeDtypeStruct(bias.shape, F32)), grid=(DIL_PAIRS, dilation),
        in_specs=[col(0), col(1), col(2), bias_spec, nat, nat, nat, nat],
        out_specs=(out_col, out_col, out_col, bias_spec), name=name,
        scratch_shapes=[pltpu.VMEM((rows, HEAD_PAD), F32), pltpu.VMEM((rows, HEAD_PAD), F32)],
        compiler_params=_params(("parallel", "arbitrary"), 6 * [((rows, HEAD_PAD), BF16)] + 4 * [((rows, HEAD_PAD), F32)]
                                + 2 * [((2, DIL_BLOCK, 2 * DIL_BLOCK), F32)], extra=2 * rows * HEAD_PAD * 4 + 2**21),
    )(qkv, qkv, qkv, bias, nat_view(d_o), nat_view(o_mix), nat_view(alpha), nat_view(lse))


def _bias_reduce(dbias, buckets, name):
    n_heads = dbias.shape[0]

    def body(db_ref, bk_ref, o_ref):
        ds, bk = db_ref[0], bk_ref[0]
        lane = lax.broadcasted_iota(jnp.int32, (8, HEAD_PAD), 1)
        acc = jnp.zeros((8, HEAD_PAD), F32)
        for b in range(N_BUCKETS):
            acc = jnp.where(lane == b, jnp.sum(jnp.where(bk == b, ds, 0.0)), acc)
        o_ref[0] = acc

    blk = (1, DIL_BLOCK, 2 * DIL_BLOCK)
    return pl.pallas_call(
        body, out_shape=jax.ShapeDtypeStruct((n_heads, 8, HEAD_PAD), F32), grid=(n_heads,),
        in_specs=[pl.BlockSpec(blk, lambda h: (h, 0, 0)), pl.BlockSpec(blk, lambda h: (h // DIL_HEADS, 0, 0))],
        out_specs=pl.BlockSpec((1, 8, HEAD_PAD), lambda h: (h, 0, 0)), name=name,
        compiler_params=_params(("parallel",), [(blk, F32), (blk, jnp.int32)], extra=2**20),
    )(dbias, buckets)


def _loss_grad(y, target, name):
    s_dim, d_dim = y.shape
    tm = TOKEN_TILE

    def body(y_ref, t_ref, dy_ref, l_ref):
        @pl.when(pl.program_id(0) == 0)
        def _():
            l_ref[...] = jnp.zeros_like(l_ref)

        err = y_ref[...] - t_ref[...]
        dy_ref[...] = err / d_dim
        sq = (err * err).reshape(tm // 8, 8, d_dim)
        l_ref[...] += 0.5 * jnp.sum(sq, axis=0) / d_dim

    row = pl.BlockSpec((tm, d_dim), lambda i: (i, 0))
    acc = pl.BlockSpec((8, d_dim), lambda i: (0, 0))
    return pl.pallas_call(
        body, out_shape=(jax.ShapeDtypeStruct((s_dim, d_dim), F32), jax.ShapeDtypeStruct((8, d_dim), F32)),
        grid=(s_dim // tm,), in_specs=[row, row], out_specs=(row, acc), name=name,
        compiler_params=_params(("arbitrary",), 3 * [((tm, d_dim), F32)], extra=2 * tm * d_dim * 4),
    )(y, target)


def _mod_fwd(c_all, w_mod, b_loc, name):
    depth, d_dim, n = w_mod.shape
    nb = c_all.shape[0]

    def body(c_ref, w_ref, b_ref, o_ref, s_ref):
        cv = c_ref[...]
        sc = cv * jax.nn.sigmoid(cv)
        s_ref[...] = sc
        o_ref[0] = _dot_nn(sc.astype(BF16), w_ref[0].astype(BF16)) + b_ref[0]

    return pl.pallas_call(
        body, out_shape=(jax.ShapeDtypeStruct((depth, nb, n), F32), jax.ShapeDtypeStruct((nb, d_dim), F32)), grid=(depth,),
        in_specs=[pl.BlockSpec((nb, d_dim), lambda i: (0, 0)), pl.BlockSpec((1, d_dim, n), lambda i: (i, 0, 0)),
                  pl.BlockSpec((1, 1, n), lambda i: (i, 0, 0))],
        out_specs=(pl.BlockSpec((1, nb, n), lambda i: (i, 0, 0)), pl.BlockSpec((nb, d_dim), lambda i: (0, 0))), name=name,
        compiler_params=_params(("arbitrary",), [((1, d_dim, n), F32)], extra=d_dim * n * 2 + 2**20),
    )(c_all, w_mod, b_loc.reshape(depth, 1, n))


def _sum_parts(parts, name):
    _, rows, cols = parts.shape
    tr = rows
    for cand in (512, 384, 256, 128, 64, 32, 16):
        if rows % cand == 0 and rows > cand:
            tr = cand
            break

    def body(p_ref, o_ref):
        acc = p_ref[0].astype(F32)
        for k in range(1, NDEV):
            acc = acc + p_ref[k].astype(F32)
        o_ref[...] = acc

    return pl.pallas_call(
        body, out_shape=jax.ShapeDtypeStruct((rows, cols), F32), grid=(rows // tr,),
        in_specs=[pl.BlockSpec((NDEV, tr, cols), lambda i: (0, i, 0))], out_specs=pl.BlockSpec((tr, cols), lambda i: (i, 0)),
        name=name, compiler_params=_params(("parallel",), [((NDEV, tr, cols), parts.dtype), ((tr, cols), F32)], extra=2**20),
    )(parts)


def _adamw(w, g, m, v, name):
    shape = w.shape
    cols = shape[-1]
    rows = math.prod(shape[:-1])
    tr = rows
    for cand in (512, 256, 128, 64, 32, 16, 8):
        if rows % cand == 0 and rows > cand and cand * cols * 4 <= 2**21:
            tr = cand
            break

    def body(w_ref, g_ref, m_ref, v_ref, d_ref, mo_ref, vo_ref):
        gv = g_ref[...]
        mn = ADAM_B1 * m_ref[...] + (1.0 - ADAM_B1) * gv
        vn = ADAM_B2 * v_ref[...] + (1.0 - ADAM_B2) * (gv * gv)
        m_hat = mn / (1.0 - ADAM_B1 ** ADAM_STEP)
        v_hat = vn / (1.0 - ADAM_B2 ** ADAM_STEP)
        d_ref[...] = -ADAM_LR * (m_hat / (jnp.sqrt(v_hat) + ADAM_EPS) + ADAM_WD * w_ref[...])
        mo_ref[...] = mn
        vo_ref[...] = vn

    blk = pl.BlockSpec((tr, cols), lambda i: (i, 0))
    out = jax.ShapeDtypeStruct((rows, cols), F32)
    res = pl.pallas_call(
        body, out_shape=(out, out, out), grid=(rows // tr,), in_specs=4 * [blk], out_specs=(blk, blk, blk), name=name,
        compiler_params=_params(("parallel",), 7 * [((tr, cols), F32)], extra=4 * tr * cols * 4),
    )(*(a.reshape(rows, cols) for a in (w, g, m, v)))
    return tuple(r.reshape(shape) for r in res)


def _peers():
    x, y, c = lax.axis_index("x"), lax.axis_index("y"), lax.axis_index("c")
    flip = lambda v, f: 1 - v if f else v
    peers = []
    for f in range(1, NDEV):
        px, py, pc = flip(x, f & 4), flip(y, f & 2), flip(c, f & 1)
        peers.append(((px, py, pc), 4 * px + 2 * py + pc))
    return (x, y, c), 4 * x + 2 * y + c, peers


def _exchange(arrs, gather, name):
    n = len(arrs)
    hbm = pl.BlockSpec(memory_space=pltpu.HBM)
    if gather:
        out_shape = [jax.ShapeDtypeStruct((NDEV * a.shape[0], a.shape[1]), a.dtype) for a in arrs]
    else:
        out_shape = [jax.ShapeDtypeStruct((NDEV, a.shape[0] // NDEV, a.shape[1]), a.dtype) for a in arrs]

    def body(*refs):
        ins, outs = refs[:n], refs[n:2 * n]
        send_sems, recv_sems, local_sems = refs[2 * n:]
        me_pos, me, peers = _peers()
        local = []
        for k in range(n):
            rows = arrs[k].shape[0] if gather else arrs[k].shape[0] // NDEV
            if gather:
                src_of = lambda idx: ins[k]
                dst_of = lambda idx: outs[k].at[pl.ds(me * rows, rows)]
                mine = (ins[k], outs[k].at[pl.ds(me * rows, rows)])
            else:
                src_of = lambda idx: ins[k].at[pl.ds(idx * rows, rows)]
                dst_of = lambda idx: outs[k].at[me]
                mine = (ins[k].at[pl.ds(me * rows, rows)], outs[k].at[me])
            cp = pltpu.make_async_copy(mine[0], mine[1], local_sems.at[k])
            cp.start()
            local.append(cp)
            for pos, idx in peers:
                pltpu.make_async_remote_copy(src_ref=src_of(idx), dst_ref=dst_of(idx), send_sem=send_sems.at[k],
                                             recv_sem=recv_sems.at[k], device_id=pos, device_id_type=MESH).start()
        for k in range(n):
            rows = arrs[k].shape[0] if gather else arrs[k].shape[0] // NDEV
            sent = ins[k].at[pl.ds(0, (NDEV - 1) * rows)] if not gather else outs[k].at[pl.ds(0, (NDEV - 1) * rows)]
            got = outs[k].at[pl.ds(0, (NDEV - 1) * rows)] if gather else outs[k].at[pl.ds(0, NDEV - 1)]
            pltpu.make_async_remote_copy(src_ref=sent, dst_ref=sent, send_sem=send_sems.at[k], recv_sem=recv_sems.at[k],
                                         device_id=me_pos, device_id_type=MESH).wait_send()
            pltpu.make_async_remote_copy(src_ref=got, dst_ref=got, send_sem=send_sems.at[k], recv_sem=recv_sems.at[k],
                                         device_id=me_pos, device_id_type=MESH).wait_recv()
            local[k].wait()

    return pl.pallas_call(
        body, out_shape=out_shape, in_specs=n * [hbm], out_specs=n * [hbm], name=name,
        scratch_shapes=[pltpu.SemaphoreType.DMA((n,)), pltpu.SemaphoreType.DMA((n,)), pltpu.SemaphoreType.DMA((n,))],
        compiler_params=pltpu.CompilerParams(has_side_effects=True),
    )(*arrs)


def _ffn_fwd(x, norms, mod, w):
    (pre_g, post_g), (shift, scale, gate), (wg_t, wu_t, wd) = norms, mod, w
    hn, g, u, a = _ffn_up(x, pre_g, scale, shift, wg_t, wu_t, "ffn_up")
    x_out, f = _mm_post(a, wd, x, post_g, gate, FFN_RES, "ffn_down")
    return x_out, (x, hn, g, u, a, f)


def _ffn_bwd(dx_out, saved, norms, mod, w):
    (pre_g, post_g), (_, scale, gate), (wg_t, wu_t, wd) = norms, mod, w
    x, hn, g, u, a, f = saved
    d_model = x.shape[1]
    df, dgate, dpost = _post_bwd(dx_out, f, post_g, gate, FFN_RES, "ffn_post_bwd")
    dg, du = _ffn_dgu(df, wd, g, u, "ffn_dgu")
    dwd = _mm([(a, df)], "tn", BF16, 256, d_model, "ffn_dw")
    dwg_t = _mm([(dg, hn)], "tn", BF16, 256, d_model, "ffn_dw")
    dwu_t = _mm([(du, hn)], "tn", BF16, 256, d_model, "ffn_dw")
    dhn = _mm([(dg, wg_t), (du, wu_t)], "nn", F32, TOKEN_TILE, d_model, "ffn_dhn")
    dx, dshift, dscale, dpre = _prenorm_bwd(dx_out, [dhn], x, pre_g, scale, "prenorm_bwd")
    return dx, (dpre, dpost), (dshift, dscale, dgate), (dwg_t, dwu_t, dwd)


def _mla_fwd(x, norms, mod, w, rope):
    (pre_g, post_g), (shift, scale, gate) = norms, mod
    w_in, q_norm, wq_t, kv_norm, wkv_t, wo = w
    hn, lat = _prenorm_mm(x, pre_g, scale, shift, w_in, "nn", F32, LAT_PAD, "mla_in")
    q, k, v, qn, kvn = _mla_qkv(lat, q_norm, kv_norm, wq_t, wkv_t, rope, "mla_qkv")
    o = _mla_attn_fwd(q, k, v, "mla_attn_fwd")
    x_out, f = _mm_post(o, wo, x, post_g, gate, 1.0, "mla_out")
    return x_out, (x, hn, lat, q, k, v, qn, kvn, o, f)


def _mla_bwd(dx_out, saved, norms, mod, w, rope):
    (pre_g, post_g), (_, scale, gate) = norms, mod
    w_in, q_norm, wq_t, kv_norm, wkv_t, wo = w
    x, hn, lat, q, k, v, qn, kvn, o, f = saved
    d_model = x.shape[1]
    df, dgate, dpost = _post_bwd(dx_out, f, post_g, gate, 1.0, "mix_post_bwd")
    d_o = _mm([(df, wo)], "nt", F32, TOKEN_TILE, wo.shape[0], "mla_do")
    dwo = _mm([(o, df)], "tn", BF16, TOKEN_TILE, d_model, "mla_dwo")
    dq, dk, dv = _mla_attn_bwd(q, k, v, d_o, "mla_attn_bwd")
    dqp, dkv, dlat, dq_norm, dkv_norm = _mla_qkv_bwd(dq, dk, dv, lat, q_norm, kv_norm, wq_t, wkv_t, rope, "mla_qkv_bwd")
    dwq_t = _mm([(dqp, qn)], "tn", BF16, TOKEN_TILE, Q_LORA, "mla_dwq")
    dwkv_t = _mm([(dkv, kvn)], "tn", BF16, TOKEN_TILE, KV_LORA, "mla_dwkv")
    dw_in = _mm([(hn, dlat)], "tn", BF16, TOKEN_TILE, LAT_PAD, "mla_dwin")
    dhn = _mm([(dlat, w_in)], "nt", F32, TOKEN_TILE, d_model, "mla_dhn")
    dx, dshift, dscale, dpre = _prenorm_bwd(dx_out, [dhn], x, pre_g, scale, "prenorm_bwd")
    return dx, (dpre, dpost), (dshift, dscale, dgate), (dw_in, dq_norm, dwq_t, dkv_norm, dwkv_t, dwo)


def _dil_fwd(x, norms, mod, w, bias):
    (pre_g, post_g), (shift, scale, gate), (w_in_t, wo) = norms, mod, w
    width = 3 * DIL_HEADS * DIL_HEAD_DIM
    hns, qkvs, outs, lses = [], [], [], []
    for g, (window, dilation) in enumerate(DIL_GROUPS):
        hn, qkv = _prenorm_mm(x, pre_g, scale, shift, w_in_t[g * width:(g + 1) * width], "nt", BF16, width,
                              "dil_in", perm=dilation)
        o, lse = _dil_attn_fwd(qkv, bias[g], dilation, window // dilation, "dil_attn_fwd")
        hns.append(hn), qkvs.append(qkv), outs.append(o), lses.append(lse)
    alphas, o_mix, o_mix_b = _dil_mix(lses, outs, "dil_mix")
    x_out, f = _mm_post(o_mix_b, wo, x, post_g, gate, 1.0, "dil_out")
    return x_out, (x, hns, qkvs, lses, alphas, o_mix, o_mix_b, f)


def _dil_bwd(dx_out, saved, norms, mod, w, bias):
    (pre_g, post_g), (_, scale, gate), (w_in_t, wo) = norms, mod, w
    x, hns, qkvs, lses, alphas, o_mix, o_mix_b, f = saved
    d_model = x.shape[1]
    inner = DIL_HEADS * DIL_HEAD_DIM
    df, dgate, dpost = _post_bwd(dx_out, f, post_g, gate, 1.0, "mix_post_bwd")
    d_o = _mm([(df, wo)], "nt", F32, TOKEN_TILE, inner, "dil_do")
    dwo = _mm([(o_mix_b, df)], "tn", BF16, TOKEN_TILE, d_model, "dil_dwo")
    dhns, dws, dbs = [], [], []
    for g, (window, dilation) in enumerate(DIL_GROUPS):
        grads = _dil_attn_bwd(qkvs[g], bias[g], d_o, o_mix, alphas[g], lses[g], dilation, window // dilation, "dil_attn_bwd")
        dbs.append(grads[3])
        w_parts = [w_in_t[(3 * g + j) * inner:(3 * g + j + 1) * inner] for j in range(3)]
        dhns.append(_mm(list(zip(grads[:3], w_parts)), "nn", F32, TOKEN_TILE, d_model, "dil_dhn", out_perm=dilation))
        dws += [_mm([(grads[j], hns[g])], "tn", BF16, TOKEN_TILE, d_model, "dil_dwin") for j in range(3)]
    dx, dshift, dscale, dpre = _prenorm_bwd(dx_out, dhns, x, pre_g, scale, "prenorm_bwd3")
    return dx, (dpre, dpost), (dshift, dscale, dgate), (jnp.concatenate(dws, axis=0), dwo), jnp.concatenate(dbs, axis=0)


def _pad_rows(a, rows):
    return jnp.pad(a, ((0, rows - a.shape[0]), (0, 0)))


def _lanes(a):
    flat = a.reshape(-1).astype(F32)
    rows = -(-flat.shape[0] // 1024) * 8
    return jnp.pad(flat, (0, rows * 128 - flat.shape[0])).reshape(rows, 128)


def kernel(x, c, norm_pre, norm_post, w_mod, b_mod, ffn_w_gate, ffn_w_up, ffn_w_down, mla_w_in, mla_q_norm, mla_w_q_up, mla_kv_norm, mla_w_kv_up, mla_w_o, dil_w_in, dil_w_o, rel_bias, loss_target, m_norm_pre, m_norm_post, m_w_mod, m_b_mod, m_ffn_w_gate, m_ffn_w_up, m_ffn_w_down, m_mla_w_in, m_mla_q_norm, m_mla_w_q_up, m_mla_kv_norm, m_mla_w_kv_up, m_mla_w_o, m_dil_w_in, m_dil_w_o, m_rel_bias, v_norm_pre, v_norm_post, v_w_mod, v_b_mod, v_ffn_w_gate, v_ffn_w_up, v_ffn_w_down, v_mla_w_in, v_mla_q_norm, v_mla_w_q_up, v_mla_kv_norm, v_mla_w_kv_up, v_mla_w_o, v_dil_w_in, v_dil_w_o, v_rel_bias):
    me = 4 * lax.axis_index("x") + 2 * lax.axis_index("y") + lax.axis_index("c")
    depth, n_sub, d_loc = norm_pre.shape
    d_model = x.shape[2]
    mod_loc_cols = w_mod.shape[2]
    x0, target = x[0], loss_target[0]

    small = jnp.concatenate([c.reshape(8, 128), _pad_rows(norm_pre.reshape(depth * n_sub, d_loc), 8),
                             _pad_rows(norm_post.reshape(depth * n_sub, d_loc), 8)], axis=0)
    small_all = _exchange([small], True, "gather_small")[0].reshape(NDEV, 24, 128)
    c_all = small_all[:, 0:8].reshape(NDEV, d_model)
    gains = lambda lo: jnp.transpose(small_all[:, lo:lo + depth * n_sub], (1, 0, 2)).reshape(depth, n_sub, 1, d_model)
    pre_full, post_full = gains(8), gains(16)

    b_loc = lax.dynamic_slice(b_mod, (0, me * mod_loc_cols), (depth, mod_loc_cols))
    mod_cols, silu_c = _mod_fwd(c_all, w_mod, b_loc, "mod_fwd")
    mod_all = _exchange([mod_cols.reshape(depth * NDEV, mod_loc_cols)], True, "gather_mod")[0]
    mod_all = mod_all.reshape(NDEV, depth, NDEV, mod_loc_cols)
    mod_mine = lax.dynamic_index_in_dim(mod_all, me, axis=2, keepdims=False)
    mod = jnp.transpose(mod_mine, (1, 0, 2)).reshape(depth, n_sub, 3, 1, d_model)

    bf_t = lambda a: a.astype(BF16).T
    ffn_ids = [(i, h) for i in range(depth) for h in range(2)]
    shards = []
    for i, h in ffn_ids:
        shards += [bf_t(ffn_w_gate[i, h]), bf_t(ffn_w_up[i, h]), ffn_w_down[i, h].astype(BF16)]
    shards += [mla_w_in[0].astype(BF16), bf_t(mla_w_q_up[0]), bf_t(mla_w_kv_up[0]), mla_w_o[0].astype(BF16),
               bf_t(dil_w_in[0]), dil_w_o[0].astype(BF16)]
    full = _exchange(shards, True, "gather_weights")
    ffn_w = {ih: tuple(full[3 * n:3 * n + 3]) for n, ih in enumerate(ffn_ids)}
    w_in, wq_t, wkv_t, wo, dil_in_t, dil_wo = full[3 * len(ffn_ids):]
    lat_real = Q_LORA + KV_LORA
    w_in_pad = jnp.concatenate([w_in[:, :lat_real], jnp.zeros((d_model, QK_NOPE), BF16), w_in[:, lat_real:],
                                jnp.zeros((d_model, HEAD_PAD - QK_NOPE - QK_ROPE), BF16)], axis=1)
    qk = QK_NOPE + QK_ROPE
    wq_pad = jnp.pad(wq_t.reshape(MLA_HEADS, qk, Q_LORA), ((0, 0), (0, HEAD_PAD - qk), (0, 0))).reshape(MLA_HEADS * HEAD_PAD, Q_LORA)
    wo_pad = jnp.pad(wo.reshape(MLA_HEADS, V_HEAD, d_model), ((0, 0), (HEAD_PAD - V_HEAD, 0), (0, 0))).reshape(MLA_HEADS * HEAD_PAD, d_model)
    mla_w = (w_in_pad, mla_q_norm, wq_pad, mla_kv_norm, wkv_t, wo_pad)
    rope = _rope_tables()
    buckets = jnp.stack([_dil_buckets(dil) for _, dil in DIL_GROUPS])
    onehot = (buckets[..., None] == jnp.arange(N_BUCKETS)).astype(F32)
    bias = jnp.einsum("gqkb,bgh->ghqk", onehot, rel_bias.reshape(N_BUCKETS, len(DIL_GROUPS), DIL_HEADS),
                      precision=lax.Precision.HIGHEST)

    norms = lambda i, s: (pre_full[i, s], post_full[i, s])
    mods = lambda i, s: (mod[i, s, 0], mod[i, s, 1], mod[i, s, 2])
    saved = {}
    h = x0
    for i in range(depth):
        h, saved[i, 0] = _ffn_fwd(h, norms(i, 0), mods(i, 0), ffn_w[i, 0])
        if i % 2 == 0:
            h, saved[i, 1] = _mla_fwd(h, norms(i, 1), mods(i, 1), mla_w, rope)
        else:
            h, saved[i, 1] = _dil_fwd(h, norms(i, 1), mods(i, 1), (dil_in_t, dil_wo), bias)
        h, saved[i, 2] = _ffn_fwd(h, norms(i, 2), mods(i, 2), ffn_w[i, 1])
    dh, loss_parts = _loss_grad(h, target, "loss")

    dnorm, dmod, dffn = {}, {}, {}
    for i in reversed(range(depth)):
        dh, dnorm[i, 2], dmod[i, 2], dffn[i, 1] = _ffn_bwd(dh, saved[i, 2], norms(i, 2), mods(i, 2), ffn_w[i, 1])
        if i % 2 == 0:
            dh, dnorm[i, 1], dmod[i, 1], dmla = _mla_bwd(dh, saved[i, 1], norms(i, 1), mods(i, 1), mla_w, rope)
        else:
            dh, dnorm[i, 1], dmod[i, 1], ddil, dbias = _dil_bwd(dh, saved[i, 1], norms(i, 1), mods(i, 1), (dil_in_t, dil_wo), bias)
        dh, dnorm[i, 0], dmod[i, 0], dffn[i, 0] = _ffn_bwd(dh, saved[i, 0], norms(i, 0), mods(i, 0), ffn_w[i, 0])
    grad_x = dh[None]

    dw_in_pad, dq_norm, dwq_pad, dkv_norm, dwkv_t, dwo_pad = dmla
    dw_in = jnp.concatenate([dw_in_pad[:, :lat_real], dw_in_pad[:, lat_real + QK_NOPE:lat_real + QK_NOPE + QK_ROPE]], axis=1)
    dwq_t = dwq_pad.reshape(MLA_HEADS, HEAD_PAD, Q_LORA)[:, :qk].reshape(MLA_HEADS * qk, Q_LORA)
    dwo = dwo_pad.reshape(MLA_HEADS, HEAD_PAD, d_model)[:, HEAD_PAD - V_HEAD:].reshape(MLA_HEADS * V_HEAD, d_model)
    grads_full = []
    for ih in ffn_ids:
        grads_full += list(dffn[ih])
    grads_full += [dw_in, dwq_t, dwkv_t, dwo, ddil[0], ddil[1]]
    parts = _exchange(grads_full, False, "scatter_grads")
    mine = [_sum_parts(p, "sum_parts") for p in parts]
    g_gate = jnp.stack([mine[3 * n].T for n in range(len(ffn_ids))]).reshape(ffn_w_gate.shape)
    g_up = jnp.stack([mine[3 * n + 1].T for n in range(len(ffn_ids))]).reshape(ffn_w_up.shape)
    g_down = jnp.stack([mine[3 * n + 2] for n in range(len(ffn_ids))]).reshape(ffn_w_down.shape)
    g_mla_in, g_q_up, g_kv_up, g_mla_o, g_dil_in, g_dil_o = mine[3 * len(ffn_ids):]
    g_mla_in, g_q_up, g_kv_up, g_mla_o = g_mla_in[None], g_q_up.T[None], g_kv_up.T[None], g_mla_o[None]
    g_dil_in, g_dil_o = g_dil_in.T[None], g_dil_o[None]

    dmod_mine = jnp.concatenate([jnp.concatenate(dmod[i, s], axis=0) for i in range(depth) for s in range(n_sub)], axis=0)
    dpre_mine = jnp.concatenate([dnorm[i, s][0] for i in range(depth) for s in range(n_sub)], axis=0)
    dpost_mine = jnp.concatenate([dnorm[i, s][1] for i in range(depth) for s in range(n_sub)], axis=0)
    dbias_tab = _bias_reduce(dbias, buckets, "bias_reduce")[:, 0, :N_BUCKETS].T
    pieces = [dmod_mine, dpre_mine, dpost_mine, dq_norm, dkv_norm, dbias_tab, jnp.sum(loss_parts).reshape(1, 1)]
    packed = [_lanes(p) for p in pieces]
    offs = [0]
    for p in packed:
        offs.append(offs[-1] + p.shape[0])
    everyone = _exchange([jnp.concatenate(packed, axis=0)], True, "gather_small_grads")[0].reshape(NDEV, offs[-1], 128)
    total = _sum_parts(everyone, "sum_small")
    take = lambda n, shape: total[offs[n]:offs[n + 1]].reshape(-1)[:math.prod(shape)].reshape(shape)
    g_b_mod = take(0, b_mod.shape)
    col0 = me * d_loc
    g_norm_pre = lax.dynamic_slice(take(1, (depth, n_sub, d_model)), (0, 0, col0), norm_pre.shape)
    g_norm_post = lax.dynamic_slice(take(2, (depth, n_sub, d_model)), (0, 0, col0), norm_post.shape)
    g_q_norm, g_kv_norm = take(3, mla_q_norm.shape), take(4, mla_kv_norm.shape)
    g_rel_bias = take(5, rel_bias.shape)
    loss = take(6, ())

    dmod_all = everyone[:, offs[0]:offs[1]].reshape(NDEV, depth, NDEV * mod_loc_cols)
    dmod_cols = lax.dynamic_slice(dmod_all, (0, 0, me * mod_loc_cols), (NDEV, depth, mod_loc_cols))
    silu_t = jnp.pad(silu_c.T, ((0, 0), (0, HEAD_PAD - NDEV)))
    g_w_mod = jnp.stack([_mm([(silu_t, jnp.pad(dmod_cols[:, i], ((0, HEAD_PAD - NDEV), (0, 0))))], "nn", F32, TOKEN_TILE,
                             mod_loc_cols, "mod_bwd") for i in range(depth)])

    ws = (norm_pre, norm_post, w_mod, b_mod, ffn_w_gate, ffn_w_up, ffn_w_down, mla_w_in, mla_q_norm, mla_w_q_up, mla_kv_norm,
          mla_w_kv_up, mla_w_o, dil_w_in, dil_w_o, rel_bias)
    gs = (g_norm_pre, g_norm_post, g_w_mod, g_b_mod, g_gate, g_up, g_down, g_mla_in, g_q_norm, g_q_up, g_kv_norm, g_kv_up,
          g_mla_o, g_dil_in, g_dil_o, g_rel_bias)
    ms = (m_norm_pre, m_norm_post, m_w_mod, m_b_mod, m_ffn_w_gate, m_ffn_w_up, m_ffn_w_down, m_mla_w_in, m_mla_q_norm,
          m_mla_w_q_up, m_mla_kv_norm, m_mla_w_kv_up, m_mla_w_o, m_dil_w_in, m_dil_w_o, m_rel_bias)
    vs = (v_norm_pre, v_norm_post, v_w_mod, v_b_mod, v_ffn_w_gate, v_ffn_w_up, v_ffn_w_down, v_mla_w_in, v_mla_q_norm,
          v_mla_w_q_up, v_mla_kv_norm, v_mla_w_kv_up, v_mla_w_o, v_dil_w_in, v_dil_w_o, v_rel_bias)
    stepped = [_adamw(w, g, m, v, "adamw") for w, g, m, v in zip(ws, gs, ms, vs)]
    deltas, new_m, new_v = zip(*stepped)
    return (loss, grad_x, *gs, *deltas, *new_m, *new_v)
```

```python
import math

import jax
import jax.numpy as jnp
from jax import lax
from jax.experimental import pallas as pl
from jax.experimental.pallas import tpu as pltpu

F32 = jnp.float32
BF16 = jnp.bfloat16
MESH = pl.DeviceIdType.MESH

NDEV = 8
D_MODEL = 1024
SEQ = 2048
D_FF = 2816
EPS = 1e-6
FFN_RES = 0.5

MLA_HEADS = 16
Q_LORA = 384
KV_LORA = 256
QK_NOPE = 64
QK_ROPE = 32
V_HEAD = 64
ROPE_THETA = 10000.0
HEAD_PAD = 128
LAT_PAD = Q_LORA + KV_LORA + HEAD_PAD
MLA_SCALE = (QK_NOPE + QK_ROPE) ** -0.5

DIL_GROUPS = ((128, 1), (512, 4), (2048, 16))
DIL_HEADS = 16
DIL_HEAD_DIM = 64
DIL_BLOCK = 128
DIL_PAIRS = DIL_HEADS // 2
DIL_SCALE = DIL_HEAD_DIM ** -0.5
N_BUCKETS = 32
MAX_DISTANCE = 2048

ADAM_LR = 0.001
ADAM_B1 = 0.9
ADAM_B2 = 0.999
ADAM_EPS = 1e-08
ADAM_WD = 0.01
ADAM_STEP = 10

V7X_VMEM_BYTES = 64 * 2**20
VMEM_RESERVE = 10 * 2**20
TOKEN_TILE = 512


def _nbytes(shape, dtype):
    return math.prod(shape) * jnp.dtype(dtype).itemsize


def _params(semantics, blocks, extra=0):
    need = 2 * sum(_nbytes(s, d) for s, d in blocks) + extra + VMEM_RESERVE
    return pltpu.CompilerParams(dimension_semantics=semantics,
                                vmem_limit_bytes=int(min(need, V7X_VMEM_BYTES - VMEM_RESERVE)))


def _dot_nn(a, b):
    return lax.dot_general(a, b, (((1,), (0,)), ((), ())), preferred_element_type=F32)


def _dot_nt(a, b):
    return lax.dot_general(a, b, (((1,), (1,)), ((), ())), preferred_element_type=F32)


def _dot_tn(a, b):
    return lax.dot_general(a, b, (((0,), (0,)), ((), ())), preferred_element_type=F32)


_DOTS = {"nn": _dot_nn, "nt": _dot_nt, "tn": _dot_tn}


def _rstd(v):
    return lax.rsqrt(jnp.mean(v * v, axis=-1, keepdims=True) + EPS)


def _rms_bwd(v, r, t):
    return r * t - v * (r * r * r) * jnp.mean(t * v, axis=-1, keepdims=True)


def _mm(pairs, mode, out_dtype, tm, tn, name, out_perm=1):
    a0, b0 = pairs[0]
    m_dim = a0.shape[1] if mode == "tn" else a0.shape[0]
    n_dim = b0.shape[0] if mode == "nt" else b0.shape[1]
    tm, tn = min(tm, m_dim // out_perm), min(tn, n_dim)
    assert m_dim % tm == 0 and n_dim % tn == 0, (name, m_dim, n_dim, tm, tn)
    dot = _DOTS[mode]
    npairs = len(pairs)

    def body(*refs):
        acc = None
        for p in range(npairs):
            d = dot(refs[2 * p][...].astype(BF16), refs[2 * p + 1][...].astype(BF16))
            acc = d if acc is None else acc + d
        refs[-1][...] = acc.astype(out_dtype)

    in_specs, blocks, flat = [], [], []
    for a, b in pairs:
        if mode == "nn":
            k = a.shape[1]
            sa, sb = ((tm, k), lambda i, j: (i, 0)), ((k, tn), lambda i, j: (0, j))
        elif mode == "nt":
            k = a.shape[1]
            sa, sb = ((tm, k), lambda i, j: (i, 0)), ((tn, k), lambda i, j: (j, 0))
        else:
            k = a.shape[0]
            sa, sb = ((k, tm), lambda i, j: (0, i)), ((k, tn), lambda i, j: (0, j))
        in_specs += [pl.BlockSpec(*sa), pl.BlockSpec(*sb)]
        blocks += [(sa[0], a.dtype), (sb[0], b.dtype)]
        flat += [a, b]
    if out_perm == 1:
        out_shape = (m_dim, n_dim)
        out_spec = pl.BlockSpec((tm, tn), lambda i, j: (i, j))
    else:
        rows = m_dim // out_perm
        assert tn == n_dim and rows % tm == 0, (name, rows, tm)
        nb = rows // tm
        out_shape = (rows, out_perm * n_dim)
        out_spec = pl.BlockSpec((tm, n_dim), lambda i, j: (i % nb, i // nb))
    blocks.append(((tm, tn), out_dtype))
    res = pl.pallas_call(
        body, out_shape=jax.ShapeDtypeStruct(out_shape, out_dtype), grid=(m_dim // tm, n_dim // tn),
        in_specs=in_specs, out_specs=out_spec, name=name,
        compiler_params=_params(("parallel", "parallel"), blocks, extra=2 * tm * tn * 4),
    )(*flat)
    return res.reshape(m_dim, n_dim)


def _prenorm_mm(x, pre_g, scale, shift, w, w_mode, out_dtype, tn, name, perm=1):
    s_dim, d_dim = x.shape
    n_dim = w.shape[0] if w_mode == "nt" else w.shape[1]
    rows = s_dim // perm
    tm = min(TOKEN_TILE, rows)
    nb = rows // tm
    tn = min(tn, n_dim)
    assert n_dim % tn == 0
    dot = _DOTS[w_mode]

    def body(x_ref, g_ref, sc_ref, sh_ref, w_ref, hn_ref, o_ref):
        @pl.when(pl.program_id(1) == 0)
        def _():
            xf = x_ref[...]
            hn = (xf * _rstd(xf) * g_ref[...]) * (1.0 + sc_ref[...]) + sh_ref[...]
            hn_ref[...] = hn.astype(BF16)

        o_ref[...] = dot(hn_ref[...], w_ref[...]).astype(out_dtype)

    vec = pl.BlockSpec((1, d_dim), lambda i, j: (0, 0))
    w_block = (tn, d_dim) if w_mode == "nt" else (d_dim, tn)
    w_spec = pl.BlockSpec(w_block, (lambda i, j: (j, 0)) if w_mode == "nt" else (lambda i, j: (0, j)))
    hn, out = pl.pallas_call(
        body,
        out_shape=(jax.ShapeDtypeStruct((s_dim, d_dim), BF16), jax.ShapeDtypeStruct((s_dim, n_dim), out_dtype)),
        grid=(s_dim // tm, n_dim // tn),
        in_specs=[pl.BlockSpec((tm, d_dim), lambda i, j: (i % nb, i // nb)), vec, vec, vec, w_spec],
        out_specs=(pl.BlockSpec((tm, d_dim), lambda i, j: (i, 0)), pl.BlockSpec((tm, tn), lambda i, j: (i, j))),
        name=name,
        compiler_params=_params(("parallel", "arbitrary"),
                                [((tm, d_dim), F32), (w_block, BF16), ((tm, d_dim), BF16), ((tm, tn), out_dtype)],
                                extra=3 * tm * d_dim * 4 + tm * tn * 4),
    )(x.reshape(rows, perm * d_dim), pre_g, scale, shift, w)
    return hn, out


def _ffn_up(x, pre_g, scale, shift, wg_t, wu_t, name):
    s_dim, d_dim = x.shape
    f_dim = wg_t.shape[0]
    tm, tn = TOKEN_TILE, f_dim // 2

    def body(x_ref, g_ref, sc_ref, sh_ref, wg_ref, wu_ref, hn_ref, go_ref, uo_ref, a_ref):
        @pl.when(pl.program_id(1) == 0)
        def _():
            xf = x_ref[...]
            hn = (xf * _rstd(xf) * g_ref[...]) * (1.0 + sc_ref[...]) + sh_ref[...]
            hn_ref[...] = hn.astype(BF16)

        hn = hn_ref[...]
        g = _dot_nt(hn, wg_ref[...])
        u = _dot_nt(hn, wu_ref[...])
        go_ref[...] = g.astype(BF16)
        uo_ref[...] = u.astype(BF16)
        a_ref[...] = (g * jax.nn.sigmoid(g) * u).astype(BF16)

    vec = pl.BlockSpec((1, d_dim), lambda i, j: (0, 0))
    w_spec = pl.BlockSpec((tn, d_dim), lambda i, j: (j, 0))
    act = pl.BlockSpec((tm, tn), lambda i, j: (i, j))
    act_shape = jax.ShapeDtypeStruct((s_dim, f_dim), BF16)
    return pl.pallas_call(
        body,
        out_shape=(jax.ShapeDtypeStruct((s_dim, d_dim), BF16), act_shape, act_shape, act_shape),
        grid=(s_dim // tm, f_dim // tn),
        in_specs=[pl.BlockSpec((tm, d_dim), lambda i, j: (i, 0)), vec, vec, vec, w_spec, w_spec],
        out_specs=(pl.BlockSpec((tm, d_dim), lambda i, j: (i, 0)), act, act, act),
        name=name,
        compiler_params=_params(("parallel", "arbitrary"),
                                [((tm, d_dim), F32), ((tn, d_dim), BF16), ((tn, d_dim), BF16), ((tm, d_dim), BF16)]
                                + 3 * [((tm, tn), BF16)], extra=3 * tm * d_dim * 4 + 4 * tm * tn * 4),
    )(x, pre_g, scale, shift, wg_t, wu_t)


def _mm_post(a, w, x, post_g, gate, res_w, name):
    s_dim, k_dim = a.shape
    d_dim = w.shape[1]
    tm = TOKEN_TILE

    def body(a_ref, w_ref, x_ref, pg_ref, gt_ref, xo_ref, f_ref):
        f = _dot_nn(a_ref[...], w_ref[...])
        y = f * _rstd(f) * pg_ref[...]
        f_ref[...] = f
        xo_ref[...] = x_ref[...] + (res_w * gt_ref[...]) * y

    vec = pl.BlockSpec((1, d_dim), lambda i: (0, 0))
    row = pl.BlockSpec((tm, d_dim), lambda i: (i, 0))
    out = jax.ShapeDtypeStruct((s_dim, d_dim), F32)
    return pl.pallas_call(
        body, out_shape=(out, out), grid=(s_dim // tm,),
        in_specs=[pl.BlockSpec((tm, k_dim), lambda i: (i, 0)), pl.BlockSpec((k_dim, d_dim), lambda i: (0, 0)), row, vec, vec],
        out_specs=(row, row), name=name,
        compiler_params=_params(("parallel",), [((tm, k_dim), BF16), ((k_dim, d_dim), BF16)] + 3 * [((tm, d_dim), F32)],
                                extra=3 * tm * d_dim * 4),
    )(a, w, x, post_g, gate)


def _post_bwd(dx_out, f, post_g, gate, res_w, name):
    s_dim, d_dim = f.shape
    tm = TOKEN_TILE

    def body(dx_ref, f_ref, pg_ref, gt_ref, df_ref, dgate_ref, dpost_ref):
        @pl.when(pl.program_id(0) == 0)
        def _():
            dgate_ref[...] = jnp.zeros_like(dgate_ref)
            dpost_ref[...] = jnp.zeros_like(dpost_ref)

        dx, fv = dx_ref[...], f_ref[...]
        r = _rstd(fv)
        fr = fv * r
        dgate_ref[...] += res_w * jnp.sum(dx * (fr * pg_ref[...]), axis=0, keepdims=True)
        dy = (res_w * gt_ref[...]) * dx
        dpost_ref[...] += jnp.sum(dy * fr, axis=0, keepdims=True)
        df_ref[...] = _rms_bwd(fv, r, dy * pg_ref[...]).astype(BF16)

    vec = pl.BlockSpec((1, d_dim), lambda i: (0, 0))
    row = pl.BlockSpec((tm, d_dim), lambda i: (i, 0))
    vshape = jax.ShapeDtypeStruct((1, d_dim), F32)
    return pl.pallas_call(
        body, out_shape=(jax.ShapeDtypeStruct((s_dim, d_dim), BF16), vshape, vshape), grid=(s_dim // tm,),
        in_specs=[row, row, vec, vec], out_specs=(row, vec, vec), name=name,
        compiler_params=_params(("arbitrary",), 3 * [((tm, d_dim), F32)], extra=6 * tm * d_dim * 4),
    )(dx_out, f, post_g, gate)


def _prenorm_bwd(dx_out, dhns, x, pre_g, scale, name):
    s_dim, d_dim = x.shape
    tm = TOKEN_TILE
    n_in = len(dhns)

    def body(*refs):
        dx_ref, x_ref, pg_ref, sc_ref = refs[n_in + 0], refs[n_in + 1], refs[n_in + 2], refs[n_in + 3]
        dxo_ref, dsh_ref, dsc_ref, dpg_ref = refs[n_in + 4:]

        @pl.when(pl.program_id(0) == 0)
        def _():
            dsh_ref[...] = jnp.zeros_like(dsh_ref)
            dsc_ref[...] = jnp.zeros_like(dsc_ref)
            dpg_ref[...] = jnp.zeros_like(dpg_ref)

        dhn = refs[0][...]
        for k in range(1, n_in):
            dhn = dhn + refs[k][...]
        xv = x_ref[...]
        r = _rstd(xv)
        xr = xv * r
        dsh_ref[...] += jnp.sum(dhn, axis=0, keepdims=True)
        dsc_ref[...] += jnp.sum(dhn * (xr * pg_ref[...]), axis=0, keepdims=True)
        dn = dhn * (1.0 + sc_ref[...])
        dpg_ref[...] += jnp.sum(dn * xr, axis=0, keepdims=True)
        dxo_ref[...] = dx_ref[...] + _rms_bwd(xv, r, dn * pg_ref[...])

    vec = pl.BlockSpec((1, d_dim), lambda i: (0, 0))
    row = pl.BlockSpec((tm, d_dim), lambda i: (i, 0))
    vshape = jax.ShapeDtypeStruct((1, d_dim), F32)
    return pl.pallas_call(
        body, out_shape=(jax.ShapeDtypeStruct((s_dim, d_dim), F32), vshape, vshape, vshape), grid=(s_dim // tm,),
        in_specs=n_in * [row] + [row, row, vec, vec], out_specs=(row, vec, vec, vec), name=name,
        compiler_params=_params(("arbitrary",), (n_in + 3) * [((tm, d_dim), F32)], extra=6 * tm * d_dim * 4),
    )(*dhns, dx_out, x, pre_g, scale)


def _ffn_dgu(df, wd, g, u, name):
    s_dim, d_dim = df.shape
    f_dim = wd.shape[0]
    tm, tn = TOKEN_TILE, f_dim // 2

    def body(df_ref, wd_ref, g_ref, u_ref, dg_ref, du_ref):
        da = _dot_nt(df_ref[...], wd_ref[...])
        gv, uv = g_ref[...].astype(F32), u_ref[...].astype(F32)
        sg = jax.nn.sigmoid(gv)
        du_ref[...] = (da * (gv * sg)).astype(BF16)
        dg_ref[...] = (da * uv * (sg * (1.0 + gv * (1.0 - sg)))).astype(BF16)

    act = pl.BlockSpec((tm, tn), lambda i, j: (i, j))
    act_shape = jax.ShapeDtypeStruct((s_dim, f_dim), BF16)
    return pl.pallas_call(
        body, out_shape=(act_shape, act_shape), grid=(s_dim // tm, f_dim // tn),
        in_specs=[pl.BlockSpec((tm, d_dim), lambda i, j: (i, 0)), pl.BlockSpec((tn, d_dim), lambda i, j: (j, 0)), act, act],
        out_specs=(act, act), name=name,
        compiler_params=_params(("parallel", "parallel"), [((tm, d_dim), BF16), ((tn, d_dim), BF16)] + 4 * [((tm, tn), BF16)],
                                extra=6 * tm * tn * 4),
    )(df, wd, g, u)


def _rope_tables():
    half = QK_ROPE // 2
    freqs = ROPE_THETA ** (-jnp.arange(half, dtype=F32) / half)
    ang = jnp.arange(SEQ, dtype=F32)[:, None] * freqs[None, :]
    cos, sin = jnp.cos(ang), jnp.sin(ang)
    ones = jnp.ones((SEQ, QK_NOPE), F32)
    zeros = jnp.zeros((SEQ, QK_NOPE), F32)
    pad1 = jnp.ones((SEQ, HEAD_PAD - QK_NOPE - QK_ROPE), F32)
    pad0 = jnp.zeros((SEQ, HEAD_PAD - QK_NOPE - QK_ROPE), F32)
    zh = jnp.zeros((SEQ, half), F32)
    c = jnp.concatenate([ones, cos, cos, pad1], axis=1)
    s1 = jnp.concatenate([zeros, -sin, zh, pad0], axis=1)
    s2 = jnp.concatenate([zeros, zh, sin, pad0], axis=1)
    return c, s1, s2


def _rope(v, c, s1, s2):
    half = QK_ROPE // 2
    return v * c + pltpu.roll(v, HEAD_PAD - half, 1) * s1 + pltpu.roll(v, half, 1) * s2


def _rope_t(dv, c, s1, s2):
    half = QK_ROPE // 2
    return dv * c + pltpu.roll(dv * s1, half, 1) + pltpu.roll(dv * s2, HEAD_PAD - half, 1)


def _mla_qkv(lat, q_norm, kv_norm, wq_t, wkv_t, rope, name):
    s_dim = lat.shape[0]
    width = MLA_HEADS * HEAD_PAD
    tm = 256

    def body(lat_ref, qg_ref, kg_ref, wq_ref, wkv_ref, c_ref, s1_ref, s2_ref, q_ref, k_ref, v_ref, qn_ref, kvn_ref):
        cq = lat_ref[:, :Q_LORA]
        ckv = lat_ref[:, Q_LORA:Q_LORA + KV_LORA]
        kr = lat_ref[:, Q_LORA + KV_LORA:]
        c, s1, s2 = c_ref[...], s1_ref[...], s2_ref[...]
        qn = (cq * _rstd(cq) * qg_ref[...]).astype(BF16)
        kvn = (ckv * _rstd(ckv) * kg_ref[...]).astype(BF16)
        qn_ref[...] = qn
        kvn_ref[...] = kvn
        q = _dot_nt(qn, wq_ref[...])
        kv = _dot_nt(kvn, wkv_ref[...])
        krr = _rope(kr, c, s1, s2)
        low = lax.broadcasted_iota(jnp.int32, (tm, HEAD_PAD), 1) < QK_NOPE
        for h in range(MLA_HEADS):
            sl = slice(h * HEAD_PAD, (h + 1) * HEAD_PAD)
            q_ref[:, sl] = _rope(q[:, sl], c, s1, s2).astype(BF16)
            kvh = kv[:, sl]
            k_ref[:, sl] = (jnp.where(low, kvh, 0.0) + krr).astype(BF16)
            v_ref[:, sl] = jnp.where(low, 0.0, kvh).astype(BF16)

    row = lambda n: pl.BlockSpec((tm, n), lambda i: (i, 0))
    full = lambda a: pl.BlockSpec(a.shape, lambda i: (0, 0))
    wide = jax.ShapeDtypeStruct((s_dim, width), BF16)
    return pl.pallas_call(
        body,
        out_shape=(wide, wide, wide, jax.ShapeDtypeStruct((s_dim, Q_LORA), BF16), jax.ShapeDtypeStruct((s_dim, KV_LORA), BF16)),
        grid=(s_dim // tm,),
        in_specs=[row(LAT_PAD), full(q_norm), full(kv_norm), full(wq_t), full(wkv_t), row(HEAD_PAD), row(HEAD_PAD), row(HEAD_PAD)],
        out_specs=(row(width), row(width), row(width), row(Q_LORA), row(KV_LORA)), name=name,
        compiler_params=_params(("parallel",), [((tm, LAT_PAD), F32), (wq_t.shape, BF16), (wkv_t.shape, BF16)]
                                + 3 * [((tm, width), BF16)], extra=4 * tm * width * 4),
    )(lat, q_norm, kv_norm, wq_t, wkv_t, *rope)


def _mla_probs(q, k, t, tq):
    s = _dot_nt(q, k) * MLA_SCALE
    rows = lax.broadcasted_iota(jnp.int32, s.shape, 0) + t * tq
    cols = lax.broadcasted_iota(jnp.int32, s.shape, 1)
    s = jnp.where(cols <= rows, s, -jnp.inf)
    e = jnp.exp(s - jnp.max(s, axis=-1, keepdims=True))
    return e / jnp.sum(e, axis=-1, keepdims=True)


def _mla_attn_fwd(q, k, v, name):
    s_dim = q.shape[0]
    tq = 512

    def body(q_ref, k_ref, v_ref, o_ref):
        for t in range(s_dim // tq):
            kt = (t + 1) * tq
            p = _mla_probs(q_ref[t * tq:kt, :], k_ref[:kt, :], t, tq)
            o_ref[t * tq:kt, :] = _dot_nn(p.astype(BF16), v_ref[:kt, :]).astype(BF16)

    head = pl.BlockSpec((s_dim, HEAD_PAD), lambda h: (0, h))
    return pl.pallas_call(
        body, out_shape=jax.ShapeDtypeStruct(q.shape, BF16), grid=(MLA_HEADS,),
        in_specs=[head, head, head], out_specs=head, name=name,
        compiler_params=_params(("parallel",), 4 * [((s_dim, HEAD_PAD), BF16)], extra=4 * tq * s_dim * 4),
    )(q, k, v)


def _mla_attn_bwd(q, k, v, d_o, name):
    s_dim = q.shape[0]
    tq = 512

    def body(q_ref, k_ref, v_ref, do_ref, dq_ref, dk_ref, dv_ref):
        dk_ref[...] = jnp.zeros_like(dk_ref)
        dv_ref[...] = jnp.zeros_like(dv_ref)
        for t in range(s_dim // tq):
            kt = (t + 1) * tq
            qt = q_ref[t * tq:kt, :]
            dot = do_ref[t * tq:kt, :].astype(BF16)
            p = _mla_probs(qt, k_ref[:kt, :], t, tq)
            dp = _dot_nt(dot, v_ref[:kt, :])
            ds = p * (dp - jnp.sum(p * dp, axis=-1, keepdims=True))
            dsb = (ds * MLA_SCALE).astype(BF16)
            dq_ref[t * tq:kt, :] = _dot_nn(dsb, k_ref[:kt, :])
            dk_ref[:kt, :] += _dot_tn(dsb, qt)
            dv_ref[:kt, :] += _dot_tn(p.astype(BF16), dot)

    head = pl.BlockSpec((s_dim, HEAD_PAD), lambda h: (0, h))
    out = jax.ShapeDtypeStruct(q.shape, F32)
    return pl.pallas_call(
        body, out_shape=(out, out, out), grid=(MLA_HEADS,),
        in_specs=[head, head, head, head], out_specs=(head, head, head), name=name,
        compiler_params=_params(("parallel",), 3 * [((s_dim, HEAD_PAD), BF16)] + 4 * [((s_dim, HEAD_PAD), F32)],
                                extra=6 * tq * s_dim * 4),
    )(q, k, v, d_o)


def _mla_qkv_bwd(dq, dk, dv, lat, q_norm, kv_norm, wq_t, wkv_t, rope, name):
    s_dim = lat.shape[0]
    width = MLA_HEADS * HEAD_PAD
    tm = 256

    def body(dq_ref, dk_ref, dv_ref, lat_ref, qg_ref, kg_ref, wq_ref, wkv_ref, c_ref, s1_ref, s2_ref,
             dqp_ref, dkv_ref, dlat_ref, dqg_ref, dkg_ref):
        @pl.when(pl.program_id(0) == 0)
        def _():
            dqg_ref[...] = jnp.zeros_like(dqg_ref)
            dkg_ref[...] = jnp.zeros_like(dkg_ref)

        c, s1, s2 = c_ref[...], s1_ref[...], s2_ref[...]
        lane = lax.broadcasted_iota(jnp.int32, (tm, HEAD_PAD), 1)
        low = lane < QK_NOPE
        rot = (lane >= QK_NOPE) & (lane < QK_NOPE + QK_ROPE)
        dkrr = jnp.zeros((tm, HEAD_PAD), F32)
        for h in range(MLA_HEADS):
            sl = slice(h * HEAD_PAD, (h + 1) * HEAD_PAD)
            dqp_ref[:, sl] = _rope_t(dq_ref[:, sl], c, s1, s2).astype(BF16)
            dkh = dk_ref[:, sl]
            dkv_ref[:, sl] = jnp.where(low, dkh, dv_ref[:, sl]).astype(BF16)
            dkrr = dkrr + jnp.where(rot, dkh, 0.0)
        dqn = _dot_nn(dqp_ref[...], wq_ref[...])
        dkvn = _dot_nn(dkv_ref[...], wkv_ref[...])
        cq = lat_ref[:, :Q_LORA]
        ckv = lat_ref[:, Q_LORA:Q_LORA + KV_LORA]
        rq, rkv = _rstd(cq), _rstd(ckv)
        dqg_ref[...] += jnp.sum(dqn * cq * rq, axis=0, keepdims=True)
        dkg_ref[...] += jnp.sum(dkvn * ckv * rkv, axis=0, keepdims=True)
        dlat_ref[:, :Q_LORA] = _rms_bwd(cq, rq, dqn * qg_ref[...])
        dlat_ref[:, Q_LORA:Q_LORA + KV_LORA] = _rms_bwd(ckv, rkv, dkvn * kg_ref[...])
        dlat_ref[:, Q_LORA + KV_LORA:] = _rope_t(dkrr, c, s1, s2)

    row = lambda n: pl.BlockSpec((tm, n), lambda i: (i, 0))
    full = lambda a: pl.BlockSpec(a.shape, lambda i: (0, 0))
    wide = jax.ShapeDtypeStruct((s_dim, width), BF16)
    return pl.pallas_call(
        body,
        out_shape=(wide, wide, jax.ShapeDtypeStruct((s_dim, LAT_PAD), F32),
                   jax.ShapeDtypeStruct(q_norm.shape, F32), jax.ShapeDtypeStruct(kv_norm.shape, F32)),
        grid=(s_dim // tm,),
        in_specs=[row(width), row(width), row(width), row(LAT_PAD), full(q_norm), full(kv_norm), full(wq_t), full(wkv_t),
                  row(HEAD_PAD), row(HEAD_PAD), row(HEAD_PAD)],
        out_specs=(row(width), row(width), row(LAT_PAD), full(q_norm), full(kv_norm)), name=name,
        compiler_params=_params(("arbitrary",), 3 * [((tm, width), F32)] + [((tm, LAT_PAD), F32), (wq_t.shape, BF16),
                                                                           (wkv_t.shape, BF16)] + 2 * [((tm, width), BF16)],
                                extra=2 * tm * width * 4),
    )(dq, dk, dv, lat, q_norm, kv_norm, wq_t, wkv_t, *rope)


def _t5_bucket(dist):
    max_exact = N_BUCKETS // 2
    d = jnp.maximum(dist, 1).astype(F32)
    large = max_exact + (jnp.log(d / max_exact) / math.log(MAX_DISTANCE / max_exact)
                         * (N_BUCKETS - max_exact)).astype(jnp.int32)
    large = jnp.minimum(large, N_BUCKETS - 1)
    return jnp.where(dist < max_exact, dist, large)


def _dil_buckets(dilation):
    iq = jnp.arange(DIL_BLOCK)[:, None]
    ik = jnp.arange(2 * DIL_BLOCK)[None, :]
    return _t5_bucket(jnp.maximum(DIL_BLOCK + iq - ik, 0) * dilation)


def _dil_logits(qh, k_ref, bias_h, n, span):
    lo = n * DIL_BLOCK
    if n == 0:
        s = _dot_nt(qh, k_ref[lo:lo + DIL_BLOCK, :]) * DIL_SCALE + bias_h[:, DIL_BLOCK:]
        rel = lax.broadcasted_iota(jnp.int32, s.shape, 0) - lax.broadcasted_iota(jnp.int32, s.shape, 1)
    else:
        s = _dot_nt(qh, k_ref[lo - DIL_BLOCK:lo + DIL_BLOCK, :]) * DIL_SCALE + bias_h
        rel = DIL_BLOCK + lax.broadcasted_iota(jnp.int32, s.shape, 0) - lax.broadcasted_iota(jnp.int32, s.shape, 1)
    return jnp.where((rel >= 0) & (rel <= span), s, -jnp.inf)


def _dil_views(dilation, rows):
    col = lambda which: pl.BlockSpec((rows, HEAD_PAD), lambda p, r: (r, which * DIL_PAIRS + p))
    nat = pl.BlockSpec((rows, HEAD_PAD), lambda p, r: (0, r * DIL_PAIRS + p))
    bias = pl.BlockSpec((2, DIL_BLOCK, 2 * DIL_BLOCK), lambda p, r: (p, 0, 0))
    return col, nat, bias


def _dil_attn_fwd(qkv, bias, dilation, span, name):
    s_dim = qkv.shape[0]
    rows = s_dim // dilation
    d_dim = DIL_HEADS * DIL_HEAD_DIM
    col, nat, bias_spec = _dil_views(dilation, rows)

    def body(q_ref, k_ref, v_ref, b_ref, o_ref, l_ref):
        lane = lax.broadcasted_iota(jnp.int32, (DIL_BLOCK, HEAD_PAD), 1)
        klane = lax.broadcasted_iota(jnp.int32, (2 * DIL_BLOCK, HEAD_PAD), 1)
        for n in range(rows // DIL_BLOCK):
            lo = n * DIL_BLOCK
            kv_rows = slice(lo, lo + DIL_BLOCK) if n == 0 else slice(lo - DIL_BLOCK, lo + DIL_BLOCK)
            qb, vb = q_ref[lo:lo + DIL_BLOCK, :], v_ref[kv_rows, :]
            o_acc = jnp.zeros((DIL_BLOCK, HEAD_PAD), F32)
            lse_acc = jnp.zeros((DIL_BLOCK, HEAD_PAD), F32)
            for h in range(2):
                mine = (lane < DIL_HEAD_DIM) == (h == 0)
                kmine = (klane[:vb.shape[0]] < DIL_HEAD_DIM) == (h == 0)
                logits = _dil_logits(jnp.where(mine, qb, 0), k_ref, b_ref[h], n, span)
                mx = jnp.max(logits, axis=-1, keepdims=True)
                lse = mx + jnp.log(jnp.sum(jnp.exp(logits - mx), axis=-1, keepdims=True))
                p = jnp.exp(logits - lse)
                o_acc = o_acc + _dot_nn(p.astype(BF16), jnp.where(kmine, vb, 0))
                lse_acc = jnp.where(mine, lse, lse_acc)
            o_ref[lo:lo + DIL_BLOCK, :] = o_acc
            l_ref[lo:lo + DIL_BLOCK, :] = lse_acc

    out = jax.ShapeDtypeStruct((rows, dilation * d_dim), F32)
    o, lse = pl.pallas_call(
        body, out_shape=(out, out), grid=(DIL_PAIRS, dilation),
        in_specs=[col(0), col(1), col(2), bias_spec], out_specs=(nat, nat), name=name,
        compiler_params=_params(("parallel", "parallel"), 3 * [((rows, HEAD_PAD), BF16)] + 2 * [((rows, HEAD_PAD), F32)]
                                + [((2, DIL_BLOCK, 2 * DIL_BLOCK), F32)], extra=2**21),
    )(qkv, qkv, qkv, bias)
    return o.reshape(s_dim, d_dim), lse.reshape(s_dim, d_dim)


def _dil_mix(lses, outs, name):
    s_dim, d_dim = outs[0].shape
    tm = TOKEN_TILE
    ng = len(outs)

    def body(*refs):
        ls = [refs[g][...] for g in range(ng)]
        mx = ls[0]
        for g in range(1, ng):
            mx = jnp.maximum(mx, ls[g])
        es = [jnp.exp(l - mx) for l in ls]
        tot = es[0]
        for g in range(1, ng):
            tot = tot + es[g]
        o = None
        for g in range(ng):
            al = es[g] / tot
            refs[2 * ng + g][...] = al
            t = al * refs[ng + g][...]
            o = t if o is None else o + t
        refs[3 * ng][...] = o
        refs[3 * ng + 1][...] = o.astype(BF16)

    row = pl.BlockSpec((tm, d_dim), lambda i: (i, 0))
    f = jax.ShapeDtypeStruct((s_dim, d_dim), F32)
    res = pl.pallas_call(
        body, out_shape=tuple(ng * [f] + [f, jax.ShapeDtypeStruct((s_dim, d_dim), BF16)]), grid=(s_dim // tm,),
        in_specs=2 * ng * [row], out_specs=tuple((ng + 2) * [row]), name=name,
        compiler_params=_params(("parallel",), (3 * ng + 2) * [((tm, d_dim), F32)], extra=4 * tm * d_dim * 4),
    )(*lses, *outs)
    return res[:ng], res[ng], res[ng + 1]


def _dil_attn_bwd(qkv, bias, d_o, o_mix, alpha, lse, dilation, span, name):
    s_dim = qkv.shape[0]
    rows = s_dim // dilation
    d_dim = DIL_HEADS * DIL_HEAD_DIM
    col, nat, bias_spec = _dil_views(dilation, rows)
    nat_view = lambda a: a.reshape(rows, dilation * d_dim)

    def body(q_ref, k_ref, v_ref, b_ref, do_ref, om_ref, al_ref, l_ref, dq_ref, dk_ref, dv_ref, db_ref, dk_acc, dv_acc):
        @pl.when(pl.program_id(1) == 0)
        def _():
            db_ref[...] = jnp.zeros_like(db_ref)

        dk_acc[...] = jnp.zeros_like(dk_acc)
        dv_acc[...] = jnp.zeros_like(dv_acc)
        lane = lax.broadcasted_iota(jnp.int32, (DIL_BLOCK, HEAD_PAD), 1)
        klane = lax.broadcasted_iota(jnp.int32, (2 * DIL_BLOCK, HEAD_PAD), 1)
        for n in range(rows // DIL_BLOCK):
            lo = n * DIL_BLOCK
            blk = slice(lo, lo + DIL_BLOCK)
            kv_rows = blk if n == 0 else slice(lo - DIL_BLOCK, lo + DIL_BLOCK)
            qb, kb, vb = q_ref[blk, :], k_ref[kv_rows, :], v_ref[kv_rows, :]
            al = al_ref[blk, :]
            dog = al * do_ref[blk, :]
            row_term = dog * om_ref[blk, :]
            lse_b = l_ref[blk, :]
            dq_acc = jnp.zeros((DIL_BLOCK, HEAD_PAD), F32)
            dk_blk = jnp.zeros((kb.shape[0], HEAD_PAD), F32)
            dv_blk = jnp.zeros((kb.shape[0], HEAD_PAD), F32)
            for h in range(2):
                mine = (lane < DIL_HEAD_DIM) == (h == 0)
                kmine = (klane[:kb.shape[0]] < DIL_HEAD_DIM) == (h == 0)
                qh = jnp.where(mine, qb, 0)
                logits = _dil_logits(qh, k_ref, b_ref[h], n, span)
                lse_h = jnp.max(jnp.where(mine, lse_b, -jnp.inf), axis=-1, keepdims=True)
                p = jnp.exp(logits - lse_h)
                dogh = jnp.where(mine, dog, 0.0).astype(BF16)
                dp = _dot_nt(dogh, vb)
                ds = p * (dp - jnp.sum(jnp.where(mine, row_term, 0.0), axis=-1, keepdims=True))
                if n == 0:
                    db_ref[h, :, DIL_BLOCK:] += ds
                else:
                    db_ref[h] += ds
                dsb = (ds * DIL_SCALE).astype(BF16)
                dq_acc = dq_acc + _dot_nn(dsb, jnp.where(kmine, kb, 0))
                dk_blk = dk_blk + _dot_tn(dsb, qh)
                dv_blk = dv_blk + _dot_tn(p.astype(BF16), dogh)
            dq_ref[blk, :] = dq_acc.astype(BF16)
            dk_acc[kv_rows, :] += dk_blk
            dv_acc[kv_rows, :] += dv_blk
        dk_ref[...] = dk_acc[...].astype(BF16)
        dv_ref[...] = dv_acc[...].astype(BF16)

    out_col = pl.BlockSpec((rows, HEAD_PAD), lambda p, r: (r, p))
    grad = jax.ShapeDtypeStruct((s_dim, d_dim), BF16)
    return pl.pallas_call(
        body, out_shape=(grad, grad, grad, jax.ShapeDtypeStruct(bias.shape, F32)), grid=(DIL_PAIRS, dilation),
        in_specs=[col(0), col(1), col(2), bias_spec, nat, nat, nat, nat],
        out_specs=(out_col, out_col, out_col, bias_spec), name=name,
        scratch_shapes=[pltpu.VMEM((rows, HEAD_PAD), F32), pltpu.VMEM((rows, HEAD_PAD), F32)],
        compiler_params=_params(("parallel", "arbitrary"), 6 * [((rows, HEAD_PAD), BF16)] + 4 * [((rows, HEAD_PAD), F32)]
                                + 2 * [((2, DIL_BLOCK, 2 * DIL_BLOCK), F32)], extra=2 * rows * HEAD_PAD * 4 + 2**21),
    )(qkv, qkv, qkv, bias, nat_view(d_o), nat_view(o_mix), nat_view(alpha), nat_view(lse))


def _bias_reduce(dbias, buckets, name):
    n_heads = dbias.shape[0]

    def body(db_ref, bk_ref, o_ref):
        ds, bk = db_ref[0], bk_ref[0]
        lane = lax.broadcasted_iota(jnp.int32, (8, HEAD_PAD), 1)
        acc = jnp.zeros((8, HEAD_PAD), F32)
        for b in range(N_BUCKETS):
            acc = jnp.where(lane == b, jnp.sum(jnp.where(bk == b, ds, 0.0)), acc)
        o_ref[0] = acc

    blk = (1, DIL_BLOCK, 2 * DIL_BLOCK)
    return pl.pallas_call(
        body, out_shape=jax.ShapeDtypeStruct((n_heads, 8, HEAD_PAD), F32), grid=(n_heads,),
        in_specs=[pl.BlockSpec(blk, lambda h: (h, 0, 0)), pl.BlockSpec(blk, lambda h: (h // DIL_HEADS, 0, 0))],
        out_specs=pl.BlockSpec((1, 8, HEAD_PAD), lambda h: (h, 0, 0)), name=name,
        compiler_params=_params(("parallel",), [(blk, F32), (blk, jnp.int32)], extra=2**20),
    )(dbias, buckets)


def _loss_grad(y, target, name):
    s_dim, d_dim = y.shape
    tm = TOKEN_TILE

    def body(y_ref, t_ref, dy_ref, l_ref):
        @pl.when(pl.program_id(0) == 0)
        def _():
            l_ref[...] = jnp.zeros_like(l_ref)

        err = y_ref[...] - t_ref[...]
        dy_ref[...] = err / d_dim
        sq = (err * err).reshape(tm // 8, 8, d_dim)
        l_ref[...] += 0.5 * jnp.sum(sq, axis=0) / d_dim

    row = pl.BlockSpec((tm, d_dim), lambda i: (i, 0))
    acc = pl.BlockSpec((8, d_dim), lambda i: (0, 0))
    return pl.pallas_call(
        body, out_shape=(jax.ShapeDtypeStruct((s_dim, d_dim), F32), jax.ShapeDtypeStruct((8, d_dim), F32)),
        grid=(s_dim // tm,), in_specs=[row, row], out_specs=(row, acc), name=name,
        compiler_params=_params(("arbitrary",), 3 * [((tm, d_dim), F32)], extra=2 * tm * d_dim * 4),
    )(y, target)


def _mod_fwd(c_all, w_mod, b_loc, name):
    depth, d_dim, n = w_mod.shape
    nb = c_all.shape[0]

    def body(c_ref, w_ref, b_ref, o_ref, s_ref):
        cv = c_ref[...]
        sc = cv * jax.nn.sigmoid(cv)
        s_ref[...] = sc
        o_ref[0] = _dot_nn(sc.astype(BF16), w_ref[0].astype(BF16)) + b_ref[0]

    return pl.pallas_call(
        body, out_shape=(jax.ShapeDtypeStruct((depth, nb, n), F32), jax.ShapeDtypeStruct((nb, d_dim), F32)), grid=(depth,),
        in_specs=[pl.BlockSpec((nb, d_dim), lambda i: (0, 0)), pl.BlockSpec((1, d_dim, n), lambda i: (i, 0, 0)),
                  pl.BlockSpec((1, 1, n), lambda i: (i, 0, 0))],
        out_specs=(pl.BlockSpec((1, nb, n), lambda i: (i, 0, 0)), pl.BlockSpec((nb, d_dim), lambda i: (0, 0))), name=name,
        compiler_params=_params(("arbitrary",), [((1, d_dim, n), F32)], extra=d_dim * n * 2 + 2**20),
    )(c_all, w_mod, b_loc.reshape(depth, 1, n))


def _sum_parts(parts, name):
    _, rows, cols = parts.shape
    tr = rows
    for cand in (512, 384, 256, 128, 64, 32, 16):
        if rows % cand == 0 and rows > cand:
            tr = cand
            break

    def body(p_ref, o_ref):
        acc = p_ref[0].astype(F32)
        for k in range(1, NDEV):
            acc = acc + p_ref[k].astype(F32)
        o_ref[...] = acc

    return pl.pallas_call(
        body, out_shape=jax.ShapeDtypeStruct((rows, cols), F32), grid=(rows // tr,),
        in_specs=[pl.BlockSpec((NDEV, tr, cols), lambda i: (0, i, 0))], out_specs=pl.BlockSpec((tr, cols), lambda i: (i, 0)),
        name=name, compiler_params=_params(("parallel",), [((NDEV, tr, cols), parts.dtype), ((tr, cols), F32)], extra=2**20),
    )(parts)


def _adamw(w, g, m, v, name):
    shape = w.shape
    cols = shape[-1]
    rows = math.prod(shape[:-1])
    tr = rows
    for cand in (512, 256, 128, 64, 32, 16, 8):
        if rows % cand == 0 and rows > cand and cand * cols * 4 <= 2**21:
            tr = cand
            break

    def body(w_ref, g_ref, m_ref, v_ref, d_ref, mo_ref, vo_ref):
        gv = g_ref[...]
        mn = ADAM_B1 * m_ref[...] + (1.0 - ADAM_B1) * gv
        vn = ADAM_B2 * v_ref[...] + (1.0 - ADAM_B2) * (gv * gv)
        m_hat = mn / (1.0 - ADAM_B1 ** ADAM_STEP)
        v_hat = vn / (1.0 - ADAM_B2 ** ADAM_STEP)
        d_ref[...] = -ADAM_LR * (m_hat / (jnp.sqrt(v_hat) + ADAM_EPS) + ADAM_WD * w_ref[...])
        mo_ref[...] = mn
        vo_ref[...] = vn

    blk = pl.BlockSpec((tr, cols), lambda i: (i, 0))
    out = jax.ShapeDtypeStruct((rows, cols), F32)
    res = pl.pallas_call(
        body, out_shape=(out, out, out), grid=(rows // tr,), in_specs=4 * [blk], out_specs=(blk, blk, blk), name=name,
        compiler_params=_params(("parallel",), 7 * [((tr, cols), F32)], extra=4 * tr * cols * 4),
    )(*(a.reshape(rows, cols) for a in (w, g, m, v)))
    return tuple(r.reshape(shape) for r in res)


def _peers():
    x, y, c = lax.axis_index("x"), lax.axis_index("y"), lax.axis_index("c")
    flip = lambda v, f: 1 - v if f else v
    peers = []
    for f in range(1, NDEV):
        px, py, pc = flip(x, f & 4), flip(y, f & 2), flip(c, f & 1)
        peers.append(((px, py, pc), 4 * px + 2 * py + pc))
    return (x, y, c), 4 * x + 2 * y + c, peers


def _exchange(arrs, gather, name):
    n = len(arrs)
    hbm = pl.BlockSpec(memory_space=pltpu.HBM)
    if gather:
        out_shape = [jax.ShapeDtypeStruct((NDEV * a.shape[0], a.shape[1]), a.dtype) for a in arrs]
    else:
        out_shape = [jax.ShapeDtypeStruct((NDEV, a.shape[0] // NDEV, a.shape[1]), a.dtype) for a in arrs]

    def body(*refs):
        ins, outs = refs[:n], refs[n:2 * n]
        send_sems, recv_sems, local_sems = refs[2 * n:]
        me_pos, me, peers = _peers()
        local = []
        for k in range(n):
            rows = arrs[k].shape[0] if gather else arrs[k].shape[0] // NDEV
            if gather:
                src_of = lambda idx: ins[k]
                dst_of = lambda idx: outs[k].at[pl.ds(me * rows, rows)]
                mine = (ins[k], outs[k].at[pl.ds(me * rows, rows)])
            else:
                src_of = lambda idx: ins[k].at[pl.ds(idx * rows, rows)]
                dst_of = lambda idx: outs[k].at[me]
                mine = (ins[k].at[pl.ds(me * rows, rows)], outs[k].at[me])
            cp = pltpu.make_async_copy(mine[0], mine[1], local_sems.at[k])
            cp.start()
            local.append(cp)
            for pos, idx in peers:
                pltpu.make_async_remote_copy(src_ref=src_of(idx), dst_ref=dst_of(idx), send_sem=send_sems.at[k],
                                             recv_sem=recv_sems.at[k], device_id=pos, device_id_type=MESH).start()
        for k in range(n):
            rows = arrs[k].shape[0] if gather else arrs[k].shape[0] // NDEV
            sent = ins[k].at[pl.ds(0, (NDEV - 1) * rows)] if not gather else outs[k].at[pl.ds(0, (NDEV - 1) * rows)]
            got = outs[k].at[pl.ds(0, (NDEV - 1) * rows)] if gather else outs[k].at[pl.ds(0, NDEV - 1)]
            pltpu.make_async_remote_copy(src_ref=sent, dst_ref=sent, send_sem=send_sems.at[k], recv_sem=recv_sems.at[k],
                                         device_id=me_pos, device_id_type=MESH).wait_send()
            pltpu.make_async_remote_copy(src_ref=got, dst_ref=got, send_sem=send_sems.at[k], recv_sem=recv_sems.at[k],
                                         device_id=me_pos, device_id_type=MESH).wait_recv()
            local[k].wait()

    return pl.pallas_call(
        body, out_shape=out_shape, in_specs=n * [hbm], out_specs=n * [hbm], name=name,
        scratch_shapes=[pltpu.SemaphoreType.DMA((n,)), pltpu.SemaphoreType.DMA((n,)), pltpu.SemaphoreType.DMA((n,))],
        compiler_params=pltpu.CompilerParams(has_side_effects=True),
    )(*arrs)


_HBM = pl.BlockSpec(memory_space=pltpu.HBM)
_SEM = pl.BlockSpec(memory_space=pltpu.SEMAPHORE)
_DATAFLOW = pltpu.SideEffectType.DATAFLOW_SIDE_EFFECTING


def _landing(arr, gather, me):
    if gather:
        rows = arr.shape[0]
        return lax.dynamic_update_slice(lax.empty((NDEV * rows, arr.shape[1]), arr.dtype), arr, (me * rows, 0))
    rows = arr.shape[0] // NDEV
    own = lax.dynamic_slice(arr, (me * rows, 0), (rows, arr.shape[1]))
    return lax.dynamic_update_slice(lax.empty((NDEV, rows, arr.shape[1]), arr.dtype), own[None], (me, 0, 0))


def _split_start(srcs, groups, gather, me, name):
    n = len(srcs)
    lands = [_landing(a, gather, me) for a in srcs]
    n_sem = 2 * len(groups)

    def body(*refs):
        src_refs, land_refs = refs[:n], refs[n:2 * n]
        sems = refs[2 * n:2 * n + n_sem]
        token = refs[-1]
        _, my, peers = _peers()
        for g, members in enumerate(groups):
            for j, k in enumerate(members):
                rows = srcs[k].shape[0] if gather else srcs[k].shape[0] // NDEV
                for pos, idx in peers:
                    src = src_refs[k] if gather else src_refs[k].at[pl.ds(idx * rows, rows)]
                    dst = land_refs[k].at[pl.ds(my * rows, rows)] if gather else land_refs[k].at[my]
                    pltpu.make_async_remote_copy(src_ref=src, dst_ref=dst, send_sem=sems[2 * g].at[j],
                                                 recv_sem=sems[2 * g + 1].at[j], device_id=pos, device_id_type=MESH).start()
        token[...] = jnp.zeros_like(token)

    out_shape = []
    for members in groups:
        out_shape += [pltpu.SemaphoreType.DMA((len(members),)), pltpu.SemaphoreType.DMA((len(members),))]
    out_shape += [pltpu.HBM(a.shape, a.dtype) for a in srcs] + [pltpu.HBM(a.shape, a.dtype) for a in lands]
    out_shape.append(jax.ShapeDtypeStruct((8, 128), F32))
    res = pl.pallas_call(
        body, name=name, out_shape=tuple(out_shape), in_specs=2 * n * [_HBM],
        out_specs=tuple(n_sem * [_SEM] + 2 * n * [_HBM] + [pl.BlockSpec(memory_space=pltpu.VMEM)]),
        input_output_aliases={i: n_sem + i for i in range(2 * n)},
        compiler_params=pltpu.CompilerParams(has_side_effects=_DATAFLOW),
    )(*[pltpu.with_memory_space_constraint(a, pltpu.HBM) for a in list(srcs) + lands])
    sems = [(res[2 * g], res[2 * g + 1]) for g in range(len(groups))]
    return sems, list(res[n_sem:n_sem + n]), list(res[n_sem + n:n_sem + 2 * n]), res[-1]


def _split_wait(sems, srcs, lands, after, gather, name):
    n = len(srcs)

    def body(*refs):
        land_refs = refs[n:2 * n]
        send_sem, recv_sem = refs[2 * n], refs[2 * n + 1]
        me_pos, _, _ = _peers()
        for j in range(n):
            part = land_refs[j].at[pl.ds(0, (NDEV - 1) * (lands[j].shape[0] // NDEV))]
            pltpu.make_async_remote_copy(src_ref=part, dst_ref=part, send_sem=send_sem.at[j], recv_sem=recv_sem.at[j],
                                         device_id=me_pos, device_id_type=MESH).wait()

    res = pl.pallas_call(
        body, name=name, out_shape=tuple(pltpu.HBM(a.shape, a.dtype) for a in list(srcs) + list(lands)),
        in_specs=2 * n * [_HBM] + [_SEM, _SEM, pl.BlockSpec(memory_space=pl.ANY)], out_specs=tuple(2 * n * [_HBM]),
        input_output_aliases={i: i for i in range(2 * n)},
        compiler_params=pltpu.CompilerParams(has_side_effects=_DATAFLOW),
    )(*srcs, *lands, sems[0], sems[1], after)
    return list(res[n:])


def _ffn_fwd(x, norms, mod, w):
    (pre_g, post_g), (shift, scale, gate), (wg_t, wu_t, wd) = norms, mod, w
    hn, g, u, a = _ffn_up(x, pre_g, scale, shift, wg_t, wu_t, "ffn_up")
    x_out, f = _mm_post(a, wd, x, post_g, gate, FFN_RES, "ffn_down")
    return x_out, (x, hn, g, u, a, f)


def _ffn_bwd(dx_out, saved, norms, mod, w):
    (pre_g, post_g), (_, scale, gate), (wg_t, wu_t, wd) = norms, mod, w
    x, hn, g, u, a, f = saved
    d_model = x.shape[1]
    df, dgate, dpost = _post_bwd(dx_out, f, post_g, gate, FFN_RES, "ffn_post_bwd")
    dg, du = _ffn_dgu(df, wd, g, u, "ffn_dgu")
    dwd = _mm([(a, df)], "tn", BF16, 256, d_model, "ffn_dw")
    dwg_t = _mm([(dg, hn)], "tn", BF16, 256, d_model, "ffn_dw")
    dwu_t = _mm([(du, hn)], "tn", BF16, 256, d_model, "ffn_dw")
    dhn = _mm([(dg, wg_t), (du, wu_t)], "nn", F32, TOKEN_TILE, d_model, "ffn_dhn")
    dx, dshift, dscale, dpre = _prenorm_bwd(dx_out, [dhn], x, pre_g, scale, "prenorm_bwd")
    return dx, (dpre, dpost), (dshift, dscale, dgate), (dwg_t, dwu_t, dwd)


def _mla_fwd(x, norms, mod, w, rope):
    (pre_g, post_g), (shift, scale, gate) = norms, mod
    w_in, q_norm, wq_t, kv_norm, wkv_t, wo = w
    hn, lat = _prenorm_mm(x, pre_g, scale, shift, w_in, "nn", F32, LAT_PAD, "mla_in")
    q, k, v, qn, kvn = _mla_qkv(lat, q_norm, kv_norm, wq_t, wkv_t, rope, "mla_qkv")
    o = _mla_attn_fwd(q, k, v, "mla_attn_fwd")
    x_out, f = _mm_post(o, wo, x, post_g, gate, 1.0, "mla_out")
    return x_out, (x, hn, lat, q, k, v, qn, kvn, o, f)


def _mla_bwd(dx_out, saved, norms, mod, w, rope):
    (pre_g, post_g), (_, scale, gate) = norms, mod
    w_in, q_norm, wq_t, kv_norm, wkv_t, wo = w
    x, hn, lat, q, k, v, qn, kvn, o, f = saved
    d_model = x.shape[1]
    df, dgate, dpost = _post_bwd(dx_out, f, post_g, gate, 1.0, "mix_post_bwd")
    d_o = _mm([(df, wo)], "nt", F32, TOKEN_TILE, wo.shape[0], "mla_do")
    dwo = _mm([(o, df)], "tn", BF16, TOKEN_TILE, d_model, "mla_dwo")
    dq, dk, dv = _mla_attn_bwd(q, k, v, d_o, "mla_attn_bwd")
    dqp, dkv, dlat, dq_norm, dkv_norm = _mla_qkv_bwd(dq, dk, dv, lat, q_norm, kv_norm, wq_t, wkv_t, rope, "mla_qkv_bwd")
    dwq_t = _mm([(dqp, qn)], "tn", BF16, TOKEN_TILE, Q_LORA, "mla_dwq")
    dwkv_t = _mm([(dkv, kvn)], "tn", BF16, TOKEN_TILE, KV_LORA, "mla_dwkv")
    dw_in = _mm([(hn, dlat)], "tn", BF16, TOKEN_TILE, LAT_PAD, "mla_dwin")
    dhn = _mm([(dlat, w_in)], "nt", F32, TOKEN_TILE, d_model, "mla_dhn")
    dx, dshift, dscale, dpre = _prenorm_bwd(dx_out, [dhn], x, pre_g, scale, "prenorm_bwd")
    return dx, (dpre, dpost), (dshift, dscale, dgate), (dw_in, dq_norm, dwq_t, dkv_norm, dwkv_t, dwo)


def _dil_fwd(x, norms, mod, w, bias):
    (pre_g, post_g), (shift, scale, gate), (w_in_t, wo) = norms, mod, w
    width = 3 * DIL_HEADS * DIL_HEAD_DIM
    hns, qkvs, outs, lses = [], [], [], []
    for g, (window, dilation) in enumerate(DIL_GROUPS):
        hn, qkv = _prenorm_mm(x, pre_g, scale, shift, w_in_t[g * width:(g + 1) * width], "nt", BF16, width,
                              "dil_in", perm=dilation)
        o, lse = _dil_attn_fwd(qkv, bias[g], dilation, window // dilation, "dil_attn_fwd")
        hns.append(hn), qkvs.append(qkv), outs.append(o), lses.append(lse)
    alphas, o_mix, o_mix_b = _dil_mix(lses, outs, "dil_mix")
    x_out, f = _mm_post(o_mix_b, wo, x, post_g, gate, 1.0, "dil_out")
    return x_out, (x, hns, qkvs, lses, alphas, o_mix, o_mix_b, f)


def _dil_bwd(dx_out, saved, norms, mod, w, bias):
    (pre_g, post_g), (_, scale, gate), (w_in_t, wo) = norms, mod, w
    x, hns, qkvs, lses, alphas, o_mix, o_mix_b, f = saved
    d_model = x.shape[1]
    inner = DIL_HEADS * DIL_HEAD_DIM
    df, dgate, dpost = _post_bwd(dx_out, f, post_g, gate, 1.0, "mix_post_bwd")
    d_o = _mm([(df, wo)], "nt", F32, TOKEN_TILE, inner, "dil_do")
    dwo = _mm([(o_mix_b, df)], "tn", BF16, TOKEN_TILE, d_model, "dil_dwo")
    dhns, dws, dbs = [], [], []
    for g, (window, dilation) in enumerate(DIL_GROUPS):
        grads = _dil_attn_bwd(qkvs[g], bias[g], d_o, o_mix, alphas[g], lses[g], dilation, window // dilation, "dil_attn_bwd")
        dbs.append(grads[3])
        w_parts = [w_in_t[(3 * g + j) * inner:(3 * g + j + 1) * inner] for j in range(3)]
        dhns.append(_mm(list(zip(grads[:3], w_parts)), "nn", F32, TOKEN_TILE, d_model, "dil_dhn", out_perm=dilation))
        dws += [_mm([(grads[j], hns[g])], "tn", BF16, TOKEN_TILE, d_model, "dil_dwin") for j in range(3)]
    dx, dshift, dscale, dpre = _prenorm_bwd(dx_out, dhns, x, pre_g, scale, "prenorm_bwd3")
    return dx, (dpre, dpost), (dshift, dscale, dgate), (jnp.concatenate(dws, axis=0), dwo), jnp.concatenate(dbs, axis=0)


def _pad_rows(a, rows):
    return jnp.pad(a, ((0, rows - a.shape[0]), (0, 0)))


def _lanes(a):
    flat = a.reshape(-1).astype(F32)
    rows = -(-flat.shape[0] // 1024) * 8
    return jnp.pad(flat, (0, rows * 128 - flat.shape[0])).reshape(rows, 128)


def kernel(x, c, norm_pre, norm_post, w_mod, b_mod, ffn_w_gate, ffn_w_up, ffn_w_down, mla_w_in, mla_q_norm, mla_w_q_up, mla_kv_norm, mla_w_kv_up, mla_w_o, dil_w_in, dil_w_o, rel_bias, loss_target, m_norm_pre, m_norm_post, m_w_mod, m_b_mod, m_ffn_w_gate, m_ffn_w_up, m_ffn_w_down, m_mla_w_in, m_mla_q_norm, m_mla_w_q_up, m_mla_kv_norm, m_mla_w_kv_up, m_mla_w_o, m_dil_w_in, m_dil_w_o, m_rel_bias, v_norm_pre, v_norm_post, v_w_mod, v_b_mod, v_ffn_w_gate, v_ffn_w_up, v_ffn_w_down, v_mla_w_in, v_mla_q_norm, v_mla_w_q_up, v_mla_kv_norm, v_mla_w_kv_up, v_mla_w_o, v_dil_w_in, v_dil_w_o, v_rel_bias):
    me = 4 * lax.axis_index("x") + 2 * lax.axis_index("y") + lax.axis_index("c")
    depth, n_sub, d_loc = norm_pre.shape
    d_model = x.shape[2]
    mod_loc_cols = w_mod.shape[2]
    x0, target = x[0], loss_target[0]

    small = jnp.concatenate([c.reshape(8, 128), _pad_rows(norm_pre.reshape(depth * n_sub, d_loc), 8),
                             _pad_rows(norm_post.reshape(depth * n_sub, d_loc), 8)], axis=0)
    small_all = _exchange([small], True, "gather_small")[0].reshape(NDEV, 24, 128)
    c_all = small_all[:, 0:8].reshape(NDEV, d_model)
    gains = lambda lo: jnp.transpose(small_all[:, lo:lo + depth * n_sub], (1, 0, 2)).reshape(depth, n_sub, 1, d_model)
    pre_full, post_full = gains(8), gains(16)

    b_loc = lax.dynamic_slice(b_mod, (0, me * mod_loc_cols), (depth, mod_loc_cols))
    mod_cols, silu_c = _mod_fwd(c_all, w_mod, b_loc, "mod_fwd")
    mod_all = _exchange([mod_cols.reshape(depth * NDEV, mod_loc_cols)], True, "gather_mod")[0]
    mod_all = mod_all.reshape(NDEV, depth, NDEV, mod_loc_cols)
    mod_mine = lax.dynamic_index_in_dim(mod_all, me, axis=2, keepdims=False)
    mod = jnp.transpose(mod_mine, (1, 0, 2)).reshape(depth, n_sub, 3, 1, d_model)

    bf_t = lambda a: a.astype(BF16).T
    ffn_ids = [(i, h) for i in range(depth) for h in range(2)]
    shards = []
    for i, h in ffn_ids:
        shards += [bf_t(ffn_w_gate[i, h]), bf_t(ffn_w_up[i, h]), ffn_w_down[i, h].astype(BF16)]
    shards += [mla_w_in[0].astype(BF16), bf_t(mla_w_q_up[0]), bf_t(mla_w_kv_up[0]), mla_w_o[0].astype(BF16),
               bf_t(dil_w_in[0]), dil_w_o[0].astype(BF16)]
    n_ffn = 3 * len(ffn_ids)
    members = {(0, 0): [0, 1, 2], (0, 1): [n_ffn, n_ffn + 1, n_ffn + 2, n_ffn + 3], (0, 2): [3, 4, 5],
               (1, 0): [6, 7, 8], (1, 1): [n_ffn + 4, n_ffn + 5], (1, 2): [9, 10, 11]}
    order = [(i, s) for i in range(depth) for s in range(n_sub)]
    g_sems, g_srcs, g_lands, g_token = _split_start(shards, [members[k] for k in order], True, me, "gather_weights_start")

    def weights_of(key, after):
        idx = members[key]
        return _split_wait(g_sems[order.index(key)], [g_srcs[k] for k in idx], [g_lands[k] for k in idx], after, True,
                           "gather_wait_%d%d" % key)

    lat_real = Q_LORA + KV_LORA
    qk = QK_NOPE + QK_ROPE

    def mla_weights(after):
        w_in, wq_t, wkv_t, wo = weights_of((0, 1), after)
        w_in_pad = jnp.concatenate([w_in[:, :lat_real], jnp.zeros((d_model, QK_NOPE), BF16), w_in[:, lat_real:],
                                    jnp.zeros((d_model, HEAD_PAD - QK_NOPE - QK_ROPE), BF16)], axis=1)
        wq_pad = jnp.pad(wq_t.reshape(MLA_HEADS, qk, Q_LORA), ((0, 0), (0, HEAD_PAD - qk), (0, 0)))
        wo_pad = jnp.pad(wo.reshape(MLA_HEADS, V_HEAD, d_model), ((0, 0), (HEAD_PAD - V_HEAD, 0), (0, 0)))
        return (w_in_pad, mla_q_norm, wq_pad.reshape(MLA_HEADS * HEAD_PAD, Q_LORA), mla_kv_norm, wkv_t,
                wo_pad.reshape(MLA_HEADS * HEAD_PAD, d_model))

    rope = _rope_tables()
    buckets = jnp.stack([_dil_buckets(dil) for _, dil in DIL_GROUPS])
    onehot = (buckets[..., None] == jnp.arange(N_BUCKETS)).astype(F32)
    bias = jnp.einsum("gqkb,bgh->ghqk", onehot, rel_bias.reshape(N_BUCKETS, len(DIL_GROUPS), DIL_HEADS),
                      precision=lax.Precision.HIGHEST)

    norms = lambda i, s: (pre_full[i, s], post_full[i, s])
    mods = lambda i, s: (mod[i, s, 0], mod[i, s, 1], mod[i, s, 2])
    saved, weights = {}, {}
    h = x0
    for i, s in order:
        if s != 1:
            weights[i, s] = tuple(weights_of((i, s), h))
            h, saved[i, s] = _ffn_fwd(h, norms(i, s), mods(i, s), weights[i, s])
        elif i % 2 == 0:
            weights[i, s] = mla_weights(h)
            h, saved[i, s] = _mla_fwd(h, norms(i, s), mods(i, s), weights[i, s], rope)
        else:
            weights[i, s] = tuple(weights_of((i, s), h))
            h, saved[i, s] = _dil_fwd(h, norms(i, s), mods(i, s), weights[i, s], bias)
    dh, loss_parts = _loss_grad(h, target, "loss")

    dnorm, dmod, sent = {}, {}, {}
    for i, s in reversed(order):
        if s != 1:
            dh, dnorm[i, s], dmod[i, s], dws = _ffn_bwd(dh, saved[i, s], norms(i, s), mods(i, s), weights[i, s])
        elif i % 2 == 0:
            dh, dnorm[i, s], dmod[i, s], dmla = _mla_bwd(dh, saved[i, s], norms(i, s), mods(i, s), weights[i, s], rope)
            dw_in_pad, dq_norm, dwq_pad, dkv_norm, dwkv_t, dwo_pad = dmla
            dw_in = jnp.concatenate([dw_in_pad[:, :lat_real], dw_in_pad[:, lat_real + QK_NOPE:lat_real + qk]], axis=1)
            dwq_t = dwq_pad.reshape(MLA_HEADS, HEAD_PAD, Q_LORA)[:, :qk].reshape(MLA_HEADS * qk, Q_LORA)
            dwo = dwo_pad.reshape(MLA_HEADS, HEAD_PAD, d_model)[:, HEAD_PAD - V_HEAD:].reshape(MLA_HEADS * V_HEAD, d_model)
            dws = (dw_in, dwq_t, dwkv_t, dwo)
        else:
            dh, dnorm[i, s], dmod[i, s], dws, dbias = _dil_bwd(dh, saved[i, s], norms(i, s), mods(i, s), weights[i, s], bias)
        sent[i, s] = _split_start(list(dws), [list(range(len(dws)))], False, me, "scatter_start_%d%d" % (i, s))
    grad_x = dh[None]

    mine = {}
    for key in order:
        sems, srcs, lands, _ = sent[key]
        parts = _split_wait(sems[0], srcs, lands, dh, False, "scatter_wait_%d%d" % key)
        for k, p in zip(members[key], parts):
            mine[k] = _sum_parts(p, "sum_parts")
    g_gate = jnp.stack([mine[3 * n].T for n in range(len(ffn_ids))]).reshape(ffn_w_gate.shape)
    g_up = jnp.stack([mine[3 * n + 1].T for n in range(len(ffn_ids))]).reshape(ffn_w_up.shape)
    g_down = jnp.stack([mine[3 * n + 2] for n in range(len(ffn_ids))]).reshape(ffn_w_down.shape)
    g_mla_in, g_q_up, g_kv_up, g_mla_o, g_dil_in, g_dil_o = (mine[k] for k in range(n_ffn, n_ffn + 6))
    g_mla_in, g_q_up, g_kv_up, g_mla_o = g_mla_in[None], g_q_up.T[None], g_kv_up.T[None], g_mla_o[None]
    g_dil_in, g_dil_o = g_dil_in.T[None], g_dil_o[None]

    dmod_mine = jnp.concatenate([jnp.concatenate(dmod[i, s], axis=0) for i in range(depth) for s in range(n_sub)], axis=0)
    dpre_mine = jnp.concatenate([dnorm[i, s][0] for i in range(depth) for s in range(n_sub)], axis=0)
    dpost_mine = jnp.concatenate([dnorm[i, s][1] for i in range(depth) for s in range(n_sub)], axis=0)
    dbias_tab = _bias_reduce(dbias, buckets, "bias_reduce")[:, 0, :N_BUCKETS].T
    pieces = [dmod_mine, dpre_mine, dpost_mine, dq_norm, dkv_norm, dbias_tab, jnp.sum(loss_parts).reshape(1, 1)]
    packed = [_lanes(p) for p in pieces]
    offs = [0]
    for p in packed:
        offs.append(offs[-1] + p.shape[0])
    everyone = _exchange([jnp.concatenate(packed, axis=0)], True, "gather_small_grads")[0].reshape(NDEV, offs[-1], 128)
    total = _sum_parts(everyone, "sum_small")
    take = lambda n, shape: total[offs[n]:offs[n + 1]].reshape(-1)[:math.prod(shape)].reshape(shape)
    g_b_mod = take(0, b_mod.shape)
    col0 = me * d_loc
    g_norm_pre = lax.dynamic_slice(take(1, (depth, n_sub, d_model)), (0, 0, col0), norm_pre.shape)
    g_norm_post = lax.dynamic_slice(take(2, (depth, n_sub, d_model)), (0, 0, col0), norm_post.shape)
    g_q_norm, g_kv_norm = take(3, mla_q_norm.shape), take(4, mla_kv_norm.shape)
    g_rel_bias = take(5, rel_bias.shape)
    loss = take(6, ())

    dmod_all = everyone[:, offs[0]:offs[1]].reshape(NDEV, depth, NDEV * mod_loc_cols)
    dmod_cols = lax.dynamic_slice(dmod_all, (0, 0, me * mod_loc_cols), (NDEV, depth, mod_loc_cols))
    silu_t = jnp.pad(silu_c.T, ((0, 0), (0, HEAD_PAD - NDEV)))
    g_w_mod = jnp.stack([_mm([(silu_t, jnp.pad(dmod_cols[:, i], ((0, HEAD_PAD - NDEV), (0, 0))))], "nn", F32, TOKEN_TILE,
                             mod_loc_cols, "mod_bwd") for i in range(depth)])

    ws = (norm_pre, norm_post, w_mod, b_mod, ffn_w_gate, ffn_w_up, ffn_w_down, mla_w_in, mla_q_norm, mla_w_q_up, mla_kv_norm,
          mla_w_kv_up, mla_w_o, dil_w_in, dil_w_o, rel_bias)
    gs = (g_norm_pre, g_norm_post, g_w_mod, g_b_mod, g_gate, g_up, g_down, g_mla_in, g_q_norm, g_q_up, g_kv_norm, g_kv_up,
          g_mla_o, g_dil_in, g_dil_o, g_rel_bias)
    ms = (m_norm_pre, m_norm_post, m_w_mod, m_b_mod, m_ffn_w_gate, m_ffn_w_up, m_ffn_w_down, m_mla_w_in, m_mla_q_norm,
          m_mla_w_q_up, m_mla_kv_norm, m_mla_w_kv_up, m_mla_w_o, m_dil_w_in, m_dil_w_o, m_rel_bias)
    vs = (v_norm_pre, v_norm_post, v_w_mod, v_b_mod, v_ffn_w_gate, v_ffn_w_up, v_ffn_w_down, v_mla_w_in, v_mla_q_norm,
          v_mla_w_q_up, v_mla_kv_norm, v_mla_w_kv_up, v_mla_w_o, v_dil_w_in, v_dil_w_o, v_rel_bias)
    stepped = [_adamw(w, g, m, v, "adamw") for w, g, m, v in zip(ws, gs, ms, vs)]
    deltas, new_m, new_v = zip(*stepped)
    return (loss, grad_x, *gs, *deltas, *new_m, *new_v)
```

```python
import math

import jax
import jax.numpy as jnp
from jax import lax
from jax.experimental import pallas as pl
from jax.experimental.pallas import tpu as pltpu

F32 = jnp.float32
BF16 = jnp.bfloat16
MESH = pl.DeviceIdType.MESH

NDEV = 8
D_MODEL = 1024
SEQ = 2048
D_FF = 2816
EPS = 1e-6
FFN_RES = 0.5

MLA_HEADS = 16
Q_LORA = 384
KV_LORA = 256
QK_NOPE = 64
QK_ROPE = 32
V_HEAD = 64
ROPE_THETA = 10000.0
HEAD_PAD = 128
LAT_PAD = Q_LORA + KV_LORA + HEAD_PAD
MLA_SCALE = (QK_NOPE + QK_ROPE) ** -0.5

DIL_GROUPS = ((128, 1), (512, 4), (2048, 16))
DIL_HEADS = 16
DIL_HEAD_DIM = 64
DIL_BLOCK = 128
DIL_PAIRS = DIL_HEADS // 2
DIL_SCALE = DIL_HEAD_DIM ** -0.5
N_BUCKETS = 32
MAX_DISTANCE = 2048

ADAM_LR = 0.001
ADAM_B1 = 0.9
ADAM_B2 = 0.999
ADAM_EPS = 1e-08
ADAM_WD = 0.01
ADAM_STEP = 10

V7X_VMEM_BYTES = 64 * 2**20
VMEM_RESERVE = 10 * 2**20
TOKEN_TILE = 512


def _nbytes(shape, dtype):
    return math.prod(shape) * jnp.dtype(dtype).itemsize


def _params(semantics, blocks, extra=0):
    need = 2 * sum(_nbytes(s, d) for s, d in blocks) + extra + VMEM_RESERVE
    return pltpu.CompilerParams(dimension_semantics=semantics,
                                vmem_limit_bytes=int(min(need, V7X_VMEM_BYTES - VMEM_RESERVE)))


def _dot_nn(a, b):
    return lax.dot_general(a, b, (((1,), (0,)), ((), ())), preferred_element_type=F32)


def _dot_nt(a, b):
    return lax.dot_general(a, b, (((1,), (1,)), ((), ())), preferred_element_type=F32)


def _dot_tn(a, b):
    return lax.dot_general(a, b, (((0,), (0,)), ((), ())), preferred_element_type=F32)


_DOTS = {"nn": _dot_nn, "nt": _dot_nt, "tn": _dot_tn}


def _rstd(v):
    return lax.rsqrt(jnp.mean(v * v, axis=-1, keepdims=True) + EPS)


def _rms_bwd(v, r, t):
    return r * t - v * (r * r * r) * jnp.mean(t * v, axis=-1, keepdims=True)


def _mm(pairs, mode, out_dtype, tm, tn, name, out_perm=1):
    a0, b0 = pairs[0]
    m_dim = a0.shape[1] if mode == "tn" else a0.shape[0]
    n_dim = b0.shape[0] if mode == "nt" else b0.shape[1]
    tm, tn = min(tm, m_dim // out_perm), min(tn, n_dim)
    assert m_dim % tm == 0 and n_dim % tn == 0, (name, m_dim, n_dim, tm, tn)
    dot = _DOTS[mode]
    npairs = len(pairs)

    def body(*refs):
        acc = None
        for p in range(npairs):
            d = dot(refs[2 * p][...].astype(BF16), refs[2 * p + 1][...].astype(BF16))
            acc = d if acc is None else acc + d
        refs[-1][...] = acc.astype(out_dtype)

    in_specs, blocks, flat = [], [], []
    for a, b in pairs:
        if mode == "nn":
            k = a.shape[1]
            sa, sb = ((tm, k), lambda i, j: (i, 0)), ((k, tn), lambda i, j: (0, j))
        elif mode == "nt":
            k = a.shape[1]
            sa, sb = ((tm, k), lambda i, j: (i, 0)), ((tn, k), lambda i, j: (j, 0))
        else:
            k = a.shape[0]
            sa, sb = ((k, tm), lambda i, j: (0, i)), ((k, tn), lambda i, j: (0, j))
        in_specs += [pl.BlockSpec(*sa), pl.BlockSpec(*sb)]
        blocks += [(sa[0], a.dtype), (sb[0], b.dtype)]
        flat += [a, b]
    if out_perm == 1:
        out_shape = (m_dim, n_dim)
        out_spec = pl.BlockSpec((tm, tn), lambda i, j: (i, j))
    else:
        rows = m_dim // out_perm
        assert tn == n_dim and rows % tm == 0, (name, rows, tm)
        nb = rows // tm
        out_shape = (rows, out_perm * n_dim)
        out_spec = pl.BlockSpec((tm, n_dim), lambda i, j: (i % nb, i // nb))
    blocks.append(((tm, tn), out_dtype))
    res = pl.pallas_call(
        body, out_shape=jax.ShapeDtypeStruct(out_shape, out_dtype), grid=(m_dim // tm, n_dim // tn),
        in_specs=in_specs, out_specs=out_spec, name=name,
        compiler_params=_params(("parallel", "parallel"), blocks, extra=2 * tm * tn * 4),
    )(*flat)
    return res.reshape(m_dim, n_dim)


def _prenorm_mm(x, pre_g, scale, shift, w, w_mode, out_dtype, tn, name, perm=1):
    s_dim, d_dim = x.shape
    n_dim = w.shape[0] if w_mode == "nt" else w.shape[1]
    rows = s_dim // perm
    tm = min(TOKEN_TILE, rows)
    nb = rows // tm
    tn = min(tn, n_dim)
    assert n_dim % tn == 0
    dot = _DOTS[w_mode]

    def body(x_ref, g_ref, sc_ref, sh_ref, w_ref, hn_ref, o_ref):
        @pl.when(pl.program_id(1) == 0)
        def _():
            xf = x_ref[...]
            hn = (xf * _rstd(xf) * g_ref[...]) * (1.0 + sc_ref[...]) + sh_ref[...]
            hn_ref[...] = hn.astype(BF16)

        o_ref[...] = dot(hn_ref[...], w_ref[...]).astype(out_dtype)

    vec = pl.BlockSpec((1, d_dim), lambda i, j: (0, 0))
    w_block = (tn, d_dim) if w_mode == "nt" else (d_dim, tn)
    w_spec = pl.BlockSpec(w_block, (lambda i, j: (j, 0)) if w_mode == "nt" else (lambda i, j: (0, j)))
    hn, out = pl.pallas_call(
        body,
        out_shape=(jax.ShapeDtypeStruct((s_dim, d_dim), BF16), jax.ShapeDtypeStruct((s_dim, n_dim), out_dtype)),
        grid=(s_dim // tm, n_dim // tn),
        in_specs=[pl.BlockSpec((tm, d_dim), lambda i, j: (i % nb, i // nb)), vec, vec, vec, w_spec],
        out_specs=(pl.BlockSpec((tm, d_dim), lambda i, j: (i, 0)), pl.BlockSpec((tm, tn), lambda i, j: (i, j))),
        name=name,
        compiler_params=_params(("parallel", "arbitrary"),
                                [((tm, d_dim), F32), (w_block, BF16), ((tm, d_dim), BF16), ((tm, tn), out_dtype)],
                                extra=3 * tm * d_dim * 4 + tm * tn * 4),
    )(x.reshape(rows, perm * d_dim), pre_g, scale, shift, w)
    return hn, out


def _ffn_up(x, pre_g, scale, shift, wg_t, wu_t, name):
    s_dim, d_dim = x.shape
    f_dim = wg_t.shape[0]
    tm, tn = TOKEN_TILE, f_dim // 2

    def body(x_ref, g_ref, sc_ref, sh_ref, wg_ref, wu_ref, hn_ref, go_ref, uo_ref, a_ref):
        @pl.when(pl.program_id(1) == 0)
        def _():
            xf = x_ref[...]
            hn = (xf * _rstd(xf) * g_ref[...]) * (1.0 + sc_ref[...]) + sh_ref[...]
            hn_ref[...] = hn.astype(BF16)

        hn = hn_ref[...]
        g = _dot_nt(hn, wg_ref[...])
        u = _dot_nt(hn, wu_ref[...])
        go_ref[...] = g.astype(BF16)
        uo_ref[...] = u.astype(BF16)
        a_ref[...] = (g * jax.nn.sigmoid(g) * u).astype(BF16)

    vec = pl.BlockSpec((1, d_dim), lambda i, j: (0, 0))
    w_spec = pl.BlockSpec((tn, d_dim), lambda i, j: (j, 0))
    act = pl.BlockSpec((tm, tn), lambda i, j: (i, j))
    act_shape = jax.ShapeDtypeStruct((s_dim, f_dim), BF16)
    return pl.pallas_call(
        body,
        out_shape=(jax.ShapeDtypeStruct((s_dim, d_dim), BF16), act_shape, act_shape, act_shape),
        grid=(s_dim // tm, f_dim // tn),
        in_specs=[pl.BlockSpec((tm, d_dim), lambda i, j: (i, 0)), vec, vec, vec, w_spec, w_spec],
        out_specs=(pl.BlockSpec((tm, d_dim), lambda i, j: (i, 0)), act, act, act),
        name=name,
        compiler_params=_params(("parallel", "arbitrary"),
                                [((tm, d_dim), F32), ((tn, d_dim), BF16), ((tn, d_dim), BF16), ((tm, d_dim), BF16)]
                                + 3 * [((tm, tn), BF16)], extra=3 * tm * d_dim * 4 + 4 * tm * tn * 4),
    )(x, pre_g, scale, shift, wg_t, wu_t)


def _mm_post(a, w, x, post_g, gate, res_w, name):
    s_dim, k_dim = a.shape
    d_dim = w.shape[1]
    tm = TOKEN_TILE

    def body(a_ref, w_ref, x_ref, pg_ref, gt_ref, xo_ref, f_ref):
        f = _dot_nn(a_ref[...], w_ref[...])
        y = f * _rstd(f) * pg_ref[...]
        f_ref[...] = f
        xo_ref[...] = x_ref[...] + (res_w * gt_ref[...]) * y

    vec = pl.BlockSpec((1, d_dim), lambda i: (0, 0))
    row = pl.BlockSpec((tm, d_dim), lambda i: (i, 0))
    out = jax.ShapeDtypeStruct((s_dim, d_dim), F32)
    return pl.pallas_call(
        body, out_shape=(out, out), grid=(s_dim // tm,),
        in_specs=[pl.BlockSpec((tm, k_dim), lambda i: (i, 0)), pl.BlockSpec((k_dim, d_dim), lambda i: (0, 0)), row, vec, vec],
        out_specs=(row, row), name=name,
        compiler_params=_params(("parallel",), [((tm, k_dim), BF16), ((k_dim, d_dim), BF16)] + 3 * [((tm, d_dim), F32)],
                                extra=3 * tm * d_dim * 4),
    )(a, w, x, post_g, gate)


def _post_bwd(dx_out, f, post_g, gate, res_w, name):
    s_dim, d_dim = f.shape
    tm = TOKEN_TILE

    def body(dx_ref, f_ref, pg_ref, gt_ref, df_ref, dgate_ref, dpost_ref):
        @pl.when(pl.program_id(0) == 0)
        def _():
            dgate_ref[...] = jnp.zeros_like(dgate_ref)
            dpost_ref[...] = jnp.zeros_like(dpost_ref)

        dx, fv = dx_ref[...], f_ref[...]
        r = _rstd(fv)
        fr = fv * r
        dgate_ref[...] += res_w * jnp.sum(dx * (fr * pg_ref[...]), axis=0, keepdims=True)
        dy = (res_w * gt_ref[...]) * dx
        dpost_ref[...] += jnp.sum(dy * fr, axis=0, keepdims=True)
        df_ref[...] = _rms_bwd(fv, r, dy * pg_ref[...]).astype(BF16)

    vec = pl.BlockSpec((1, d_dim), lambda i: (0, 0))
    row = pl.BlockSpec((tm, d_dim), lambda i: (i, 0))
    vshape = jax.ShapeDtypeStruct((1, d_dim), F32)
    return pl.pallas_call(
        body, out_shape=(jax.ShapeDtypeStruct((s_dim, d_dim), BF16), vshape, vshape), grid=(s_dim // tm,),
        in_specs=[row, row, vec, vec], out_specs=(row, vec, vec), name=name,
        compiler_params=_params(("arbitrary",), 3 * [((tm, d_dim), F32)], extra=6 * tm * d_dim * 4),
    )(dx_out, f, post_g, gate)


def _prenorm_bwd(dx_out, dhns, x, pre_g, scale, name):
    s_dim, d_dim = x.shape
    tm = TOKEN_TILE
    n_in = len(dhns)

    def body(*refs):
        dx_ref, x_ref, pg_ref, sc_ref = refs[n_in + 0], refs[n_in + 1], refs[n_in + 2], refs[n_in + 3]
        dxo_ref, dsh_ref, dsc_ref, dpg_ref = refs[n_in + 4:]

        @pl.when(pl.program_id(0) == 0)
        def _():
            dsh_ref[...] = jnp.zeros_like(dsh_ref)
            dsc_ref[...] = jnp.zeros_like(dsc_ref)
            dpg_ref[...] = jnp.zeros_like(dpg_ref)

        dhn = refs[0][...]
        for k in range(1, n_in):
            dhn = dhn + refs[k][...]
        xv = x_ref[...]
        r = _rstd(xv)
        xr = xv * r
        dsh_ref[...] += jnp.sum(dhn, axis=0, keepdims=True)
        dsc_ref[...] += jnp.sum(dhn * (xr * pg_ref[...]), axis=0, keepdims=True)
        dn = dhn * (1.0 + sc_ref[...])
        dpg_ref[...] += jnp.sum(dn * xr, axis=0, keepdims=True)
        dxo_ref[...] = dx_ref[...] + _rms_bwd(xv, r, dn * pg_ref[...])

    vec = pl.BlockSpec((1, d_dim), lambda i: (0, 0))
    row = pl.BlockSpec((tm, d_dim), lambda i: (i, 0))
    vshape = jax.ShapeDtypeStruct((1, d_dim), F32)
    return pl.pallas_call(
        body, out_shape=(jax.ShapeDtypeStruct((s_dim, d_dim), F32), vshape, vshape, vshape), grid=(s_dim // tm,),
        in_specs=n_in * [row] + [row, row, vec, vec], out_specs=(row, vec, vec, vec), name=name,
        compiler_params=_params(("arbitrary",), (n_in + 3) * [((tm, d_dim), F32)], extra=6 * tm * d_dim * 4),
    )(*dhns, dx_out, x, pre_g, scale)


def _ffn_dgu(df, wd, g, u, name):
    s_dim, d_dim = df.shape
    f_dim = wd.shape[0]
    tm, tn = TOKEN_TILE, f_dim // 2

    def body(df_ref, wd_ref, g_ref, u_ref, dg_ref, du_ref):
        da = _dot_nt(df_ref[...], wd_ref[...])
        gv, uv = g_ref[...].astype(F32), u_ref[...].astype(F32)
        sg = jax.nn.sigmoid(gv)
        du_ref[...] = (da * (gv * sg)).astype(BF16)
        dg_ref[...] = (da * uv * (sg * (1.0 + gv * (1.0 - sg)))).astype(BF16)

    act = pl.BlockSpec((tm, tn), lambda i, j: (i, j))
    act_shape = jax.ShapeDtypeStruct((s_dim, f_dim), BF16)
    return pl.pallas_call(
        body, out_shape=(act_shape, act_shape), grid=(s_dim // tm, f_dim // tn),
        in_specs=[pl.BlockSpec((tm, d_dim), lambda i, j: (i, 0)), pl.BlockSpec((tn, d_dim), lambda i, j: (j, 0)), act, act],
        out_specs=(act, act), name=name,
        compiler_params=_params(("parallel", "parallel"), [((tm, d_dim), BF16), ((tn, d_dim), BF16)] + 4 * [((tm, tn), BF16)],
                                extra=6 * tm * tn * 4),
    )(df, wd, g, u)


def _rope_tables():
    half = QK_ROPE // 2
    freqs = ROPE_THETA ** (-jnp.arange(half, dtype=F32) / half)
    ang = jnp.arange(SEQ, dtype=F32)[:, None] * freqs[None, :]
    cos, sin = jnp.cos(ang), jnp.sin(ang)
    ones = jnp.ones((SEQ, QK_NOPE), F32)
    zeros = jnp.zeros((SEQ, QK_NOPE), F32)
    pad1 = jnp.ones((SEQ, HEAD_PAD - QK_NOPE - QK_ROPE), F32)
    pad0 = jnp.zeros((SEQ, HEAD_PAD - QK_NOPE - QK_ROPE), F32)
    zh = jnp.zeros((SEQ, half), F32)
    c = jnp.concatenate([ones, cos, cos, pad1], axis=1)
    s1 = jnp.concatenate([zeros, -sin, zh, pad0], axis=1)
    s2 = jnp.concatenate([zeros, zh, sin, pad0], axis=1)
    return c, s1, s2


def _rope(v, c, s1, s2):
    half = QK_ROPE // 2
    return v * c + pltpu.roll(v, HEAD_PAD - half, 1) * s1 + pltpu.roll(v, half, 1) * s2


def _rope_t(dv, c, s1, s2):
    half = QK_ROPE // 2
    return dv * c + pltpu.roll(dv * s1, half, 1) + pltpu.roll(dv * s2, HEAD_PAD - half, 1)


def _mla_qkv(lat, q_norm, kv_norm, wq_t, wkv_t, rope, name):
    s_dim = lat.shape[0]
    width = MLA_HEADS * HEAD_PAD
    tm = 256

    def body(lat_ref, qg_ref, kg_ref, wq_ref, wkv_ref, c_ref, s1_ref, s2_ref, q_ref, k_ref, v_ref, qn_ref, kvn_ref):
        cq = lat_ref[:, :Q_LORA]
        ckv = lat_ref[:, Q_LORA:Q_LORA + KV_LORA]
        kr = lat_ref[:, Q_LORA + KV_LORA:]
        c, s1, s2 = c_ref[...], s1_ref[...], s2_ref[...]
        qn = (cq * _rstd(cq) * qg_ref[...]).astype(BF16)
        kvn = (ckv * _rstd(ckv) * kg_ref[...]).astype(BF16)
        qn_ref[...] = qn
        kvn_ref[...] = kvn
        q = _dot_nt(qn, wq_ref[...])
        kv = _dot_nt(kvn, wkv_ref[...])
        krr = _rope(kr, c, s1, s2)
        low = lax.broadcasted_iota(jnp.int32, (tm, HEAD_PAD), 1) < QK_NOPE
        for h in range(MLA_HEADS):
            sl = slice(h * HEAD_PAD, (h + 1) * HEAD_PAD)
            q_ref[:, sl] = _rope(q[:, sl], c, s1, s2).astype(BF16)
            kvh = kv[:, sl]
            k_ref[:, sl] = (jnp.where(low, kvh, 0.0) + krr).astype(BF16)
            v_ref[:, sl] = jnp.where(low, 0.0, kvh).astype(BF16)

    row = lambda n: pl.BlockSpec((tm, n), lambda i: (i, 0))
    full = lambda a: pl.BlockSpec(a.shape, lambda i: (0, 0))
    wide = jax.ShapeDtypeStruct((s_dim, width), BF16)
    return pl.pallas_call(
        body,
        out_shape=(wide, wide, wide, jax.ShapeDtypeStruct((s_dim, Q_LORA), BF16), jax.ShapeDtypeStruct((s_dim, KV_LORA), BF16)),
        grid=(s_dim // tm,),
        in_specs=[row(LAT_PAD), full(q_norm), full(kv_norm), full(wq_t), full(wkv_t), row(HEAD_PAD), row(HEAD_PAD), row(HEAD_PAD)],
        out_specs=(row(width), row(width), row(width), row(Q_LORA), row(KV_LORA)), name=name,
        compiler_params=_params(("parallel",), [((tm, LAT_PAD), F32), (wq_t.shape, BF16), (wkv_t.shape, BF16)]
                                + 3 * [((tm, width), BF16)], extra=4 * tm * width * 4),
    )(lat, q_norm, kv_norm, wq_t, wkv_t, *rope)


def _mla_probs(q, k, t, tq):
    s = _dot_nt(q, k) * MLA_SCALE
    rows = lax.broadcasted_iota(jnp.int32, s.shape, 0) + t * tq
    cols = lax.broadcasted_iota(jnp.int32, s.shape, 1)
    s = jnp.where(cols <= rows, s, -jnp.inf)
    e = jnp.exp(s - jnp.max(s, axis=-1, keepdims=True))
    return e / jnp.sum(e, axis=-1, keepdims=True)


def _mla_attn_fwd(q, k, v, name):
    s_dim = q.shape[0]
    tq = 512

    def body(q_ref, k_ref, v_ref, o_ref):
        for t in range(s_dim // tq):
            kt = (t + 1) * tq
            p = _mla_probs(q_ref[t * tq:kt, :], k_ref[:kt, :], t, tq)
            o_ref[t * tq:kt, :] = _dot_nn(p.astype(BF16), v_ref[:kt, :]).astype(BF16)

    head = pl.BlockSpec((s_dim, HEAD_PAD), lambda h: (0, h))
    return pl.pallas_call(
        body, out_shape=jax.ShapeDtypeStruct(q.shape, BF16), grid=(MLA_HEADS,),
        in_specs=[head, head, head], out_specs=head, name=name,
        compiler_params=_params(("parallel",), 4 * [((s_dim, HEAD_PAD), BF16)], extra=4 * tq * s_dim * 4),
    )(q, k, v)


def _mla_attn_bwd(q, k, v, d_o, name):
    s_dim = q.shape[0]
    tq = 512

    def body(q_ref, k_ref, v_ref, do_ref, dq_ref, dk_ref, dv_ref):
        dk_ref[...] = jnp.zeros_like(dk_ref)
        dv_ref[...] = jnp.zeros_like(dv_ref)
        for t in range(s_dim // tq):
            kt = (t + 1) * tq
            qt = q_ref[t * tq:kt, :]
            dot = do_ref[t * tq:kt, :].astype(BF16)
            p = _mla_probs(qt, k_ref[:kt, :], t, tq)
            dp = _dot_nt(dot, v_ref[:kt, :])
            ds = p * (dp - jnp.sum(p * dp, axis=-1, keepdims=True))
            dsb = (ds * MLA_SCALE).astype(BF16)
            dq_ref[t * tq:kt, :] = _dot_nn(dsb, k_ref[:kt, :])
            dk_ref[:kt, :] += _dot_tn(dsb, qt)
            dv_ref[:kt, :] += _dot_tn(p.astype(BF16), dot)

    head = pl.BlockSpec((s_dim, HEAD_PAD), lambda h: (0, h))
    out = jax.ShapeDtypeStruct(q.shape, F32)
    return pl.pallas_call(
        body, out_shape=(out, out, out), grid=(MLA_HEADS,),
        in_specs=[head, head, head, head], out_specs=(head, head, head), name=name,
        compiler_params=_params(("parallel",), 3 * [((s_dim, HEAD_PAD), BF16)] + 4 * [((s_dim, HEAD_PAD), F32)],
                                extra=6 * tq * s_dim * 4),
    )(q, k, v, d_o)


def _mla_qkv_bwd(dq, dk, dv, lat, q_norm, kv_norm, wq_t, wkv_t, rope, name):
    s_dim = lat.shape[0]
    width = MLA_HEADS * HEAD_PAD
    tm = 256

    def body(dq_ref, dk_ref, dv_ref, lat_ref, qg_ref, kg_ref, wq_ref, wkv_ref, c_ref, s1_ref, s2_ref,
             dqp_ref, dkv_ref, dlat_ref, dqg_ref, dkg_ref):
        @pl.when(pl.program_id(0) == 0)
        def _():
            dqg_ref[...] = jnp.zeros_like(dqg_ref)
            dkg_ref[...] = jnp.zeros_like(dkg_ref)

        c, s1, s2 = c_ref[...], s1_ref[...], s2_ref[...]
        lane = lax.broadcasted_iota(jnp.int32, (tm, HEAD_PAD), 1)
        low = lane < QK_NOPE
        rot = (lane >= QK_NOPE) & (lane < QK_NOPE + QK_ROPE)
        dkrr = jnp.zeros((tm, HEAD_PAD), F32)
        for h in range(MLA_HEADS):
            sl = slice(h * HEAD_PAD, (h + 1) * HEAD_PAD)
            dqp_ref[:, sl] = _rope_t(dq_ref[:, sl], c, s1, s2).astype(BF16)
            dkh = dk_ref[:, sl]
            dkv_ref[:, sl] = jnp.where(low, dkh, dv_ref[:, sl]).astype(BF16)
            dkrr = dkrr + jnp.where(rot, dkh, 0.0)
        dqn = _dot_nn(dqp_ref[...], wq_ref[...])
        dkvn = _dot_nn(dkv_ref[...], wkv_ref[...])
        cq = lat_ref[:, :Q_LORA]
        ckv = lat_ref[:, Q_LORA:Q_LORA + KV_LORA]
        rq, rkv = _rstd(cq), _rstd(ckv)
        dqg_ref[...] += jnp.sum(dqn * cq * rq, axis=0, keepdims=True)
        dkg_ref[...] += jnp.sum(dkvn * ckv * rkv, axis=0, keepdims=True)
        dlat_ref[:, :Q_LORA] = _rms_bwd(cq, rq, dqn * qg_ref[...])
        dlat_ref[:, Q_LORA:Q_LORA + KV_LORA] = _rms_bwd(ckv, rkv, dkvn * kg_ref[...])
        dlat_ref[:, Q_LORA + KV_LORA:] = _rope_t(dkrr, c, s1, s2)

    row = lambda n: pl.BlockSpec((tm, n), lambda i: (i, 0))
    full = lambda a: pl.BlockSpec(a.shape, lambda i: (0, 0))
    wide = jax.ShapeDtypeStruct((s_dim, width), BF16)
    return pl.pallas_call(
        body,
        out_shape=(wide, wide, jax.ShapeDtypeStruct((s_dim, LAT_PAD), F32),
                   jax.ShapeDtypeStruct(q_norm.shape, F32), jax.ShapeDtypeStruct(kv_norm.shape, F32)),
        grid=(s_dim // tm,),
        in_specs=[row(width), row(width), row(width), row(LAT_PAD), full(q_norm), full(kv_norm), full(wq_t), full(wkv_t),
                  row(HEAD_PAD), row(HEAD_PAD), row(HEAD_PAD)],
        out_specs=(row(width), row(width), row(LAT_PAD), full(q_norm), full(kv_norm)), name=name,
        compiler_params=_params(("arbitrary",), 3 * [((tm, width), F32)] + [((tm, LAT_PAD), F32), (wq_t.shape, BF16),
                                                                           (wkv_t.shape, BF16)] + 2 * [((tm, width), BF16)],
                                extra=2 * tm * width * 4),
    )(dq, dk, dv, lat, q_norm, kv_norm, wq_t, wkv_t, *rope)


def _t5_bucket(dist):
    max_exact = N_BUCKETS // 2
    d = jnp.maximum(dist, 1).astype(F32)
    large = max_exact + (jnp.log(d / max_exact) / math.log(MAX_DISTANCE / max_exact)
                         * (N_BUCKETS - max_exact)).astype(jnp.int32)
    large = jnp.minimum(large, N_BUCKETS - 1)
    return jnp.where(dist < max_exact, dist, large)


def _dil_buckets(dilation):
    iq = jnp.arange(DIL_BLOCK)[:, None]
    ik = jnp.arange(2 * DIL_BLOCK)[None, :]
    return _t5_bucket(jnp.maximum(DIL_BLOCK + iq - ik, 0) * dilation)


def _dil_logits(qh, k_ref, bias_h, n, span):
    lo = n * DIL_BLOCK
    if n == 0:
        s = _dot_nt(qh, k_ref[lo:lo + DIL_BLOCK, :]) * DIL_SCALE + bias_h[:, DIL_BLOCK:]
        rel = lax.broadcasted_iota(jnp.int32, s.shape, 0) - lax.broadcasted_iota(jnp.int32, s.shape, 1)
    else:
        s = _dot_nt(qh, k_ref[lo - DIL_BLOCK:lo + DIL_BLOCK, :]) * DIL_SCALE + bias_h
        rel = DIL_BLOCK + lax.broadcasted_iota(jnp.int32, s.shape, 0) - lax.broadcasted_iota(jnp.int32, s.shape, 1)
    return jnp.where((rel >= 0) & (rel <= span), s, -jnp.inf)


def _dil_views(dilation, rows):
    col = lambda which: pl.BlockSpec((rows, HEAD_PAD), lambda p, r: (r, which * DIL_PAIRS + p))
    nat = pl.BlockSpec((rows, HEAD_PAD), lambda p, r: (0, r * DIL_PAIRS + p))
    bias = pl.BlockSpec((2, DIL_BLOCK, 2 * DIL_BLOCK), lambda p, r: (p, 0, 0))
    return col, nat, bias


def _dil_attn_fwd(qkv, bias, dilation, span, name):
    s_dim = qkv.shape[0]
    rows = s_dim // dilation
    d_dim = DIL_HEADS * DIL_HEAD_DIM
    col, nat, bias_spec = _dil_views(dilation, rows)

    def body(q_ref, k_ref, v_ref, b_ref, o_ref, l_ref):
        lane = lax.broadcasted_iota(jnp.int32, (DIL_BLOCK, HEAD_PAD), 1)
        klane = lax.broadcasted_iota(jnp.int32, (2 * DIL_BLOCK, HEAD_PAD), 1)
        for n in range(rows // DIL_BLOCK):
            lo = n * DIL_BLOCK
            kv_rows = slice(lo, lo + DIL_BLOCK) if n == 0 else slice(lo - DIL_BLOCK, lo + DIL_BLOCK)
            qb, vb = q_ref[lo:lo + DIL_BLOCK, :], v_ref[kv_rows, :]
            o_acc = jnp.zeros((DIL_BLOCK, HEAD_PAD), F32)
            lse_acc = jnp.zeros((DIL_BLOCK, HEAD_PAD), F32)
            for h in range(2):
                mine = (lane < DIL_HEAD_DIM) == (h == 0)
                kmine = (klane[:vb.shape[0]] < DIL_HEAD_DIM) == (h == 0)
                logits = _dil_logits(jnp.where(mine, qb, 0), k_ref, b_ref[h], n, span)
                mx = jnp.max(logits, axis=-1, keepdims=True)
                lse = mx + jnp.log(jnp.sum(jnp.exp(logits - mx), axis=-1, keepdims=True))
                p = jnp.exp(logits - lse)
                o_acc = o_acc + _dot_nn(p.astype(BF16), jnp.where(kmine, vb, 0))
                lse_acc = jnp.where(mine, lse, lse_acc)
            o_ref[lo:lo + DIL_BLOCK, :] = o_acc
            l_ref[lo:lo + DIL_BLOCK, :] = lse_acc

    out = jax.ShapeDtypeStruct((rows, dilation * d_dim), F32)
    o, lse = pl.pallas_call(
        body, out_shape=(out, out), grid=(DIL_PAIRS, dilation),
        in_specs=[col(0), col(1), col(2), bias_spec], out_specs=(nat, nat), name=name,
        compiler_params=_params(("parallel", "parallel"), 3 * [((rows, HEAD_PAD), BF16)] + 2 * [((rows, HEAD_PAD), F32)]
                                + [((2, DIL_BLOCK, 2 * DIL_BLOCK), F32)], extra=2**21),
    )(qkv, qkv, qkv, bias)
    return o.reshape(s_dim, d_dim), lse.reshape(s_dim, d_dim)


def _dil_mix(lses, outs, name):
    s_dim, d_dim = outs[0].shape
    tm = TOKEN_TILE
    ng = len(outs)

    def body(*refs):
        ls = [refs[g][...] for g in range(ng)]
        mx = ls[0]
        for g in range(1, ng):
            mx = jnp.maximum(mx, ls[g])
        es = [jnp.exp(l - mx) for l in ls]
        tot = es[0]
        for g in range(1, ng):
            tot = tot + es[g]
        o = None
        for g in range(ng):
            al = es[g] / tot
            refs[2 * ng + g][...] = al
            t = al * refs[ng + g][...]
            o = t if o is None else o + t
        refs[3 * ng][...] = o
        refs[3 * ng + 1][...] = o.astype(BF16)

    row = pl.BlockSpec((tm, d_dim), lambda i: (i, 0))
    f = jax.ShapeDtypeStruct((s_dim, d_dim), F32)
    res = pl.pallas_call(
        body, out_shape=tuple(ng * [f] + [f, jax.ShapeDtypeStruct((s_dim, d_dim), BF16)]), grid=(s_dim // tm,),
        in_specs=2 * ng * [row], out_specs=tuple((ng + 2) * [row]), name=name,
        compiler_params=_params(("parallel",), (3 * ng + 2) * [((tm, d_dim), F32)], extra=4 * tm * d_dim * 4),
    )(*lses, *outs)
    return res[:ng], res[ng], res[ng + 1]


def _dil_attn_bwd(qkv, bias, d_o, o_mix, alpha, lse, dilation, span, name):
    s_dim = qkv.shape[0]
    rows = s_dim // dilation
    d_dim = DIL_HEADS * DIL_HEAD_DIM
    col, nat, bias_spec = _dil_views(dilation, rows)
    nat_view = lambda a: a.reshape(rows, dilation * d_dim)

    def body(q_ref, k_ref, v_ref, b_ref, do_ref, om_ref, al_ref, l_ref, dq_ref, dk_ref, dv_ref, db_ref, dk_acc, dv_acc):
        @pl.when(pl.program_id(1) == 0)
        def _():
            db_ref[...] = jnp.zeros_like(db_ref)

        dk_acc[...] = jnp.zeros_like(dk_acc)
        dv_acc[...] = jnp.zeros_like(dv_acc)
        lane = lax.broadcasted_iota(jnp.int32, (DIL_BLOCK, HEAD_PAD), 1)
        klane = lax.broadcasted_iota(jnp.int32, (2 * DIL_BLOCK, HEAD_PAD), 1)
        for n in range(rows // DIL_BLOCK):
            lo = n * DIL_BLOCK
            blk = slice(lo, lo + DIL_BLOCK)
            kv_rows = blk if n == 0 else slice(lo - DIL_BLOCK, lo + DIL_BLOCK)
            qb, kb, vb = q_ref[blk, :], k_ref[kv_rows, :], v_ref[kv_rows, :]
            al = al_ref[blk, :]
            dog = al * do_ref[blk, :]
            row_term = dog * om_ref[blk, :]
            lse_b = l_ref[blk, :]
            dq_acc = jnp.zeros((DIL_BLOCK, HEAD_PAD), F32)
            dk_blk = jnp.zeros((kb.shape[0], HEAD_PAD), F32)
            dv_blk = jnp.zeros((kb.shape[0], HEAD_PAD), F32)
            for h in range(2):
                mine = (lane < DIL_HEAD_DIM) == (h == 0)
                kmine = (klane[:kb.shape[0]] < DIL_HEAD_DIM) == (h == 0)
                qh = jnp.where(mine, qb, 0)
                logits = _dil_logits(qh, k_ref, b_ref[h], n, span)
                lse_h = jnp.max(jnp.where(mine, lse_b, -jnp.inf), axis=-1, keepdims=True)
                p = jnp.exp(logits - lse_h)
                dogh = jnp.where(mine, dog, 0.0).astype(BF16)
                dp = _dot_nt(dogh, vb)
                ds = p * (dp - jnp.sum(jnp.where(mine, row_term, 0.0), axis=-1, keepdims=True))
                if n == 0:
                    db_ref[h, :, DIL_BLOCK:] += ds
                else:
                    db_ref[h] += ds
                dsb = (ds * DIL_SCALE).astype(BF16)
                dq_acc = dq_acc + _dot_nn(dsb, jnp.where(kmine, kb, 0))
                dk_blk = dk_blk + _dot_tn(dsb, qh)
                dv_blk = dv_blk + _dot_tn(p.astype(BF16), dogh)
            dq_ref[blk, :] = dq_acc.astype(BF16)
            dk_acc[kv_rows, :] += dk_blk
            dv_acc[kv_rows, :] += dv_blk
        dk_ref[...] = dk_acc[...].astype(BF16)
        dv_ref[...] = dv_acc[...].astype(BF16)

    out_col = pl.BlockSpec((rows, HEAD_PAD), lambda p, r: (r, p))
    grad = jax.ShapeDtypeStruct((s_dim, d_dim), BF16)
    return pl.pallas_call(
        body, out_shape=(grad, grad, grad, jax.ShapeDtypeStruct(bias.shape, F32)), grid=(DIL_PAIRS, dilation),
        in_specs=[col(0), col(1), col(2), bias_spec, nat, nat, nat, nat],
        out_specs=(out_col, out_col, out_col, bias_spec), name=name,
        scratch_shapes=[pltpu.VMEM((rows, HEAD_PAD), F32), pltpu.VMEM((rows, HEAD_PAD), F32)],
        compiler_params=_params(("parallel", "arbitrary"), 6 * [((rows, HEAD_PAD), BF16)] + 4 * [((rows, HEAD_PAD), F32)]
                                + 2 * [((2, DIL_BLOCK, 2 * DIL_BLOCK), F32)], extra=2 * rows * HEAD_PAD * 4 + 2**21),
    )(qkv, qkv, qkv, bias, nat_view(d_o), nat_view(o_mix), nat_view(alpha), nat_view(lse))


def _bias_reduce(dbias, buckets, name):
    n_heads = dbias.shape[0]

    def body(db_ref, bk_ref, o_ref):
        ds, bk = db_ref[0], bk_ref[0]
        lane = lax.broadcasted_iota(jnp.int32, (8, HEAD_PAD), 1)
        acc = jnp.zeros((8, HEAD_PAD), F32)
        for b in range(N_BUCKETS):
            acc = jnp.where(lane == b, jnp.sum(jnp.where(bk == b, ds, 0.0)), acc)
        o_ref[0] = acc

    blk = (1, DIL_BLOCK, 2 * DIL_BLOCK)
    return pl.pallas_call(
        body, out_shape=jax.ShapeDtypeStruct((n_heads, 8, HEAD_PAD), F32), grid=(n_heads,),
        in_specs=[pl.BlockSpec(blk, lambda h: (h, 0, 0)), pl.BlockSpec(blk, lambda h: (h // DIL_HEADS, 0, 0))],
        out_specs=pl.BlockSpec((1, 8, HEAD_PAD), lambda h: (h, 0, 0)), name=name,
        compiler_params=_params(("parallel",), [(blk, F32), (blk, jnp.int32)], extra=2**20),
    )(dbias, buckets)


def _loss_grad(y, target, name):
    s_dim, d_dim = y.shape
    tm = TOKEN_TILE

    def body(y_ref, t_ref, dy_ref, l_ref):
        @pl.when(pl.program_id(0) == 0)
        def _():
            l_ref[...] = jnp.zeros_like(l_ref)

        err = y_ref[...] - t_ref[...]
        dy_ref[...] = err / d_dim
        sq = (err * err).reshape(tm // 8, 8, d_dim)
        l_ref[...] += 0.5 * jnp.sum(sq, axis=0) / d_dim

    row = pl.BlockSpec((tm, d_dim), lambda i: (i, 0))
    acc = pl.BlockSpec((8, d_dim), lambda i: (0, 0))
    return pl.pallas_call(
        body, out_shape=(jax.ShapeDtypeStruct((s_dim, d_dim), F32), jax.ShapeDtypeStruct((8, d_dim), F32)),
        grid=(s_dim // tm,), in_specs=[row, row], out_specs=(row, acc), name=name,
        compiler_params=_params(("arbitrary",), 3 * [((tm, d_dim), F32)], extra=2 * tm * d_dim * 4),
    )(y, target)


def _mod_fwd(c_all, w_mod, b_loc, name):
    depth, d_dim, n = w_mod.shape
    nb = c_all.shape[0]

    def body(c_ref, w_ref, b_ref, o_ref, s_ref):
        cv = c_ref[...]
        sc = cv * jax.nn.sigmoid(cv)
        s_ref[...] = sc
        o_ref[0] = _dot_nn(sc.astype(BF16), w_ref[0].astype(BF16)) + b_ref[0]

    return pl.pallas_call(
        body, out_shape=(jax.ShapeDtypeStruct((depth, nb, n), F32), jax.ShapeDtypeStruct((nb, d_dim), F32)), grid=(depth,),
        in_specs=[pl.BlockSpec((nb, d_dim), lambda i: (0, 0)), pl.BlockSpec((1, d_dim, n), lambda i: (i, 0, 0)),
                  pl.BlockSpec((1, 1, n), lambda i: (i, 0, 0))],
        out_specs=(pl.BlockSpec((1, nb, n), lambda i: (i, 0, 0)), pl.BlockSpec((nb, d_dim), lambda i: (0, 0))), name=name,
        compiler_params=_params(("arbitrary",), [((1, d_dim, n), F32)], extra=d_dim * n * 2 + 2**20),
    )(c_all, w_mod, b_loc.reshape(depth, 1, n))


def _sum_parts(parts, name):
    _, rows, cols = parts.shape
    tr = rows
    for cand in (512, 384, 256, 128, 64, 32, 16):
        if rows % cand == 0 and rows > cand:
            tr = cand
            break

    def body(p_ref, o_ref):
        acc = p_ref[0].astype(F32)
        for k in range(1, NDEV):
            acc = acc + p_ref[k].astype(F32)
        o_ref[...] = acc

    return pl.pallas_call(
        body, out_shape=jax.ShapeDtypeStruct((rows, cols), F32), grid=(rows // tr,),
        in_specs=[pl.BlockSpec((NDEV, tr, cols), lambda i: (0, i, 0))], out_specs=pl.BlockSpec((tr, cols), lambda i: (i, 0)),
        name=name, compiler_params=_params(("parallel",), [((NDEV, tr, cols), parts.dtype), ((tr, cols), F32)], extra=2**20),
    )(parts)


def _adamw(w, g, m, v, name):
    shape = w.shape
    cols = shape[-1]
    rows = math.prod(shape[:-1])
    tr = rows
    for cand in (512, 256, 128, 64, 32, 16, 8):
        if rows % cand == 0 and rows > cand and cand * cols * 4 <= 2**21:
            tr = cand
            break

    def body(w_ref, g_ref, m_ref, v_ref, d_ref, mo_ref, vo_ref):
        gv = g_ref[...]
        mn = ADAM_B1 * m_ref[...] + (1.0 - ADAM_B1) * gv
        vn = ADAM_B2 * v_ref[...] + (1.0 - ADAM_B2) * (gv * gv)
        m_hat = mn / (1.0 - ADAM_B1 ** ADAM_STEP)
        v_hat = vn / (1.0 - ADAM_B2 ** ADAM_STEP)
        d_ref[...] = -ADAM_LR * (m_hat / (jnp.sqrt(v_hat) + ADAM_EPS) + ADAM_WD * w_ref[...])
        mo_ref[...] = mn
        vo_ref[...] = vn

    blk = pl.BlockSpec((tr, cols), lambda i: (i, 0))
    out = jax.ShapeDtypeStruct((rows, cols), F32)
    res = pl.pallas_call(
        body, out_shape=(out, out, out), grid=(rows // tr,), in_specs=4 * [blk], out_specs=(blk, blk, blk), name=name,
        compiler_params=_params(("parallel",), 7 * [((tr, cols), F32)], extra=4 * tr * cols * 4),
    )(*(a.reshape(rows, cols) for a in (w, g, m, v)))
    return tuple(r.reshape(shape) for r in res)


def _peers():
    x, y, c = lax.axis_index("x"), lax.axis_index("y"), lax.axis_index("c")
    flip = lambda v, f: 1 - v if f else v
    peers = []
    for f in range(1, NDEV):
        px, py, pc = flip(x, f & 4), flip(y, f & 2), flip(c, f & 1)
        peers.append(((px, py, pc), 4 * px + 2 * py + pc))
    return (x, y, c), 4 * x + 2 * y + c, peers


def _exchange(arrs, gather, name):
    n = len(arrs)
    hbm = pl.BlockSpec(memory_space=pltpu.HBM)
    if gather:
        out_shape = [jax.ShapeDtypeStruct((NDEV * a.shape[0], a.shape[1]), a.dtype) for a in arrs]
    else:
        out_shape = [jax.ShapeDtypeStruct((NDEV, a.shape[0] // NDEV, a.shape[1]), a.dtype) for a in arrs]

    def body(*refs):
        ins, outs = refs[:n], refs[n:2 * n]
        send_sems, recv_sems, local_sems = refs[2 * n:]
        me_pos, me, peers = _peers()
        local = []
        for k in range(n):
            rows = arrs[k].shape[0] if gather else arrs[k].shape[0] // NDEV
            if gather:
                src_of = lambda idx: ins[k]
                dst_of = lambda idx: outs[k].at[pl.ds(me * rows, rows)]
                mine = (ins[k], outs[k].at[pl.ds(me * rows, rows)])
            else:
                src_of = lambda idx: ins[k].at[pl.ds(idx * rows, rows)]
                dst_of = lambda idx: outs[k].at[me]
                mine = (ins[k].at[pl.ds(me * rows, rows)], outs[k].at[me])
            cp = pltpu.make_async_copy(mine[0], mine[1], local_sems.at[k])
            cp.start()
            local.append(cp)
            for pos, idx in peers:
                pltpu.make_async_remote_copy(src_ref=src_of(idx), dst_ref=dst_of(idx), send_sem=send_sems.at[k],
                                             recv_sem=recv_sems.at[k], device_id=pos, device_id_type=MESH).start()
        for k in range(n):
            rows = arrs[k].shape[0] if gather else arrs[k].shape[0] // NDEV
            sent = ins[k].at[pl.ds(0, (NDEV - 1) * rows)] if not gather else outs[k].at[pl.ds(0, (NDEV - 1) * rows)]
            got = outs[k].at[pl.ds(0, (NDEV - 1) * rows)] if gather else outs[k].at[pl.ds(0, NDEV - 1)]
            pltpu.make_async_remote_copy(src_ref=sent, dst_ref=sent, send_sem=send_sems.at[k], recv_sem=recv_sems.at[k],
                                         device_id=me_pos, device_id_type=MESH).wait_send()
            pltpu.make_async_remote_copy(src_ref=got, dst_ref=got, send_sem=send_sems.at[k], recv_sem=recv_sems.at[k],
                                         device_id=me_pos, device_id_type=MESH).wait_recv()
            local[k].wait()

    return pl.pallas_call(
        body, out_shape=out_shape, in_specs=n * [hbm], out_specs=n * [hbm], name=name,
        scratch_shapes=[pltpu.SemaphoreType.DMA((n,)), pltpu.SemaphoreType.DMA((n,)), pltpu.SemaphoreType.DMA((n,))],
        compiler_params=pltpu.CompilerParams(has_side_effects=True),
    )(*arrs)


_HBM = pl.BlockSpec(memory_space=pltpu.HBM)
_SEM = pl.BlockSpec(memory_space=pltpu.SEMAPHORE)
_DATAFLOW = pltpu.SideEffectType.DATAFLOW_SIDE_EFFECTING


def _landing(arr, gather, me):
    if gather:
        rows = arr.shape[0]
        return lax.dynamic_update_slice(lax.empty((NDEV * rows, arr.shape[1]), arr.dtype), arr, (me * rows, 0))
    rows = arr.shape[0] // NDEV
    own = lax.dynamic_slice(arr, (me * rows, 0), (rows, arr.shape[1]))
    return lax.dynamic_update_slice(lax.empty((NDEV, rows, arr.shape[1]), arr.dtype), own[None], (me, 0, 0))


def _split_start(srcs, groups, gather, me, name):
    n = len(srcs)
    lands = [_landing(a, gather, me) for a in srcs]
    n_sem = 2 * len(groups)

    def body(*refs):
        src_refs, land_refs = refs[:n], refs[n:2 * n]
        sems = refs[2 * n:2 * n + n_sem]
        token = refs[-1]
        _, my, peers = _peers()
        for g, members in enumerate(groups):
            for j, k in enumerate(members):
                rows = srcs[k].shape[0] if gather else srcs[k].shape[0] // NDEV
                for pos, idx in peers:
                    src = src_refs[k] if gather else src_refs[k].at[pl.ds(idx * rows, rows)]
                    dst = land_refs[k].at[pl.ds(my * rows, rows)] if gather else land_refs[k].at[my]
                    pltpu.make_async_remote_copy(src_ref=src, dst_ref=dst, send_sem=sems[2 * g].at[j],
                                                 recv_sem=sems[2 * g + 1].at[j], device_id=pos, device_id_type=MESH).start()
        token[...] = jnp.zeros_like(token)

    out_shape = []
    for members in groups:
        out_shape += [pltpu.SemaphoreType.DMA((len(members),)), pltpu.SemaphoreType.DMA((len(members),))]
    out_shape += [pltpu.HBM(a.shape, a.dtype) for a in srcs] + [pltpu.HBM(a.shape, a.dtype) for a in lands]
    out_shape.append(jax.ShapeDtypeStruct((8, 128), F32))
    res = pl.pallas_call(
        body, name=name, out_shape=tuple(out_shape), in_specs=2 * n * [_HBM],
        out_specs=tuple(n_sem * [_SEM] + 2 * n * [_HBM] + [pl.BlockSpec(memory_space=pltpu.VMEM)]),
        input_output_aliases={i: n_sem + i for i in range(2 * n)},
        compiler_params=pltpu.CompilerParams(has_side_effects=_DATAFLOW),
    )(*[pltpu.with_memory_space_constraint(a, pltpu.HBM) for a in list(srcs) + lands])
    sems = [(res[2 * g], res[2 * g + 1]) for g in range(len(groups))]
    return sems, list(res[n_sem:n_sem + n]), list(res[n_sem + n:n_sem + 2 * n]), res[-1]


def _split_wait(sems, srcs, lands, after, gather, name):
    n = len(srcs)

    def body(*refs):
        land_refs = refs[n:2 * n]
        send_sem, recv_sem = refs[2 * n], refs[2 * n + 1]
        me_pos, _, _ = _peers()
        for j in range(n):
            part = land_refs[j].at[pl.ds(0, (NDEV - 1) * (lands[j].shape[0] // NDEV))]
            pltpu.make_async_remote_copy(src_ref=part, dst_ref=part, send_sem=send_sem.at[j], recv_sem=recv_sem.at[j],
                                         device_id=me_pos, device_id_type=MESH).wait()

    res = pl.pallas_call(
        body, name=name, out_shape=tuple(pltpu.HBM(a.shape, a.dtype) for a in list(srcs) + list(lands)),
        in_specs=2 * n * [_HBM] + [_SEM, _SEM, pl.BlockSpec(memory_space=pl.ANY)], out_specs=tuple(2 * n * [_HBM]),
        input_output_aliases={i: i for i in range(2 * n)},
        compiler_params=pltpu.CompilerParams(has_side_effects=_DATAFLOW),
    )(*srcs, *lands, sems[0], sems[1], after)
    return list(res[n:])


def _ffn_fwd(x, norms, mod, w):
    (pre_g, post_g), (shift, scale, gate), (wg_t, wu_t, wd) = norms, mod, w
    hn, g, u, a = _ffn_up(x, pre_g, scale, shift, wg_t, wu_t, "ffn_up")
    x_out, f = _mm_post(a, wd, x, post_g, gate, FFN_RES, "ffn_down")
    return x_out, (x, hn, g, u, a, f)


def _ffn_bwd(dx_out, saved, norms, mod, w):
    (pre_g, post_g), (_, scale, gate), (wg_t, wu_t, wd) = norms, mod, w
    x, hn, g, u, a, f = saved
    d_model = x.shape[1]
    df, dgate, dpost = _post_bwd(dx_out, f, post_g, gate, FFN_RES, "ffn_post_bwd")
    dg, du = _ffn_dgu(df, wd, g, u, "ffn_dgu")
    dwd = _mm([(a, df)], "tn", BF16, 256, d_model, "ffn_dw")
    dwg_t = _mm([(dg, hn)], "tn", BF16, 256, d_model, "ffn_dw")
    dwu_t = _mm([(du, hn)], "tn", BF16, 256, d_model, "ffn_dw")
    dhn = _mm([(dg, wg_t), (du, wu_t)], "nn", F32, TOKEN_TILE, d_model, "ffn_dhn")
    dx, dshift, dscale, dpre = _prenorm_bwd(dx_out, [dhn], x, pre_g, scale, "prenorm_bwd")
    return dx, (dpre, dpost), (dshift, dscale, dgate), (dwg_t, dwu_t, dwd)


def _mla_fwd(x, norms, mod, w, rope):
    (pre_g, post_g), (shift, scale, gate) = norms, mod
    w_in, q_norm, wq_t, kv_norm, wkv_t, wo = w
    hn, lat = _prenorm_mm(x, pre_g, scale, shift, w_in, "nn", F32, LAT_PAD, "mla_in")
    q, k, v, qn, kvn = _mla_qkv(lat, q_norm, kv_norm, wq_t, wkv_t, rope, "mla_qkv")
    o = _mla_attn_fwd(q, k, v, "mla_attn_fwd")
    x_out, f = _mm_post(o, wo, x, post_g, gate, 1.0, "mla_out")
    return x_out, (x, hn, lat, q, k, v, qn, kvn, o, f)


def _mla_bwd(dx_out, saved, norms, mod, w, rope):
    (pre_g, post_g), (_, scale, gate) = norms, mod
    w_in, q_norm, wq_t, kv_norm, wkv_t, wo = w
    x, hn, lat, q, k, v, qn, kvn, o, f = saved
    d_model = x.shape[1]
    df, dgate, dpost = _post_bwd(dx_out, f, post_g, gate, 1.0, "mix_post_bwd")
    d_o = _mm([(df, wo)], "nt", F32, TOKEN_TILE, wo.shape[0], "mla_do")
    dwo = _mm([(o, df)], "tn", BF16, TOKEN_TILE, d_model, "mla_dwo")
    dq, dk, dv = _mla_attn_bwd(q, k, v, d_o, "mla_attn_bwd")
    dqp, dkv, dlat, dq_norm, dkv_norm = _mla_qkv_bwd(dq, dk, dv, lat, q_norm, kv_norm, wq_t, wkv_t, rope, "mla_qkv_bwd")
    dwq_t = _mm([(dqp, qn)], "tn", BF16, TOKEN_TILE, Q_LORA, "mla_dwq")
    dwkv_t = _mm([(dkv, kvn)], "tn", BF16, TOKEN_TILE, KV_LORA, "mla_dwkv")
    dw_in = _mm([(hn, dlat)], "tn", BF16, TOKEN_TILE, LAT_PAD, "mla_dwin")
    dhn = _mm([(dlat, w_in)], "nt", F32, TOKEN_TILE, d_model, "mla_dhn")
    dx, dshift, dscale, dpre = _prenorm_bwd(dx_out, [dhn], x, pre_g, scale, "prenorm_bwd")
    return dx, (dpre, dpost), (dshift, dscale, dgate), (dw_in, dq_norm, dwq_t, dkv_norm, dwkv_t, dwo)


def _dil_fwd(x, norms, mod, w, bias):
    (pre_g, post_g), (shift, scale, gate), (w_in_t, wo) = norms, mod, w
    width = 3 * DIL_HEADS * DIL_HEAD_DIM
    hns, qkvs, outs, lses = [], [], [], []
    for g, (window, dilation) in enumerate(DIL_GROUPS):
        hn, qkv = _prenorm_mm(x, pre_g, scale, shift, w_in_t[g * width:(g + 1) * width], "nt", BF16, width,
                              "dil_in", perm=dilation)
        o, lse = _dil_attn_fwd(qkv, bias[g], dilation, window // dilation, "dil_attn_fwd")
        hns.append(hn), qkvs.append(qkv), outs.append(o), lses.append(lse)
    alphas, o_mix, o_mix_b = _dil_mix(lses, outs, "dil_mix")
    x_out, f = _mm_post(o_mix_b, wo, x, post_g, gate, 1.0, "dil_out")
    return x_out, (x, hns, qkvs, lses, alphas, o_mix, o_mix_b, f)


def _dil_bwd(dx_out, saved, norms, mod, w, bias):
    (pre_g, post_g), (_, scale, gate), (w_in_t, wo) = norms, mod, w
    x, hns, qkvs, lses, alphas, o_mix, o_mix_b, f = saved
    d_model = x.shape[1]
    inner = DIL_HEADS * DIL_HEAD_DIM
    df, dgate, dpost = _post_bwd(dx_out, f, post_g, gate, 1.0, "mix_post_bwd")
    d_o = _mm([(df, wo)], "nt", F32, TOKEN_TILE, inner, "dil_do")
    dwo = _mm([(o_mix_b, df)], "tn", BF16, TOKEN_TILE, d_model, "dil_dwo")
    dhns, dws, dbs = [], [], []
    for g, (window, dilation) in enumerate(DIL_GROUPS):
        grads = _dil_attn_bwd(qkvs[g], bias[g], d_o, o_mix, alphas[g], lses[g], dilation, window // dilation, "dil_attn_bwd")
        dbs.append(grads[3])
        w_parts = [w_in_t[(3 * g + j) * inner:(3 * g + j + 1) * inner] for j in range(3)]
        dhns.append(_mm(list(zip(grads[:3], w_parts)), "nn", F32, TOKEN_TILE, d_model, "dil_dhn", out_perm=dilation))
        dws += [_mm([(grads[j], hns[g])], "tn", BF16, TOKEN_TILE, d_model, "dil_dwin") for j in range(3)]
    dx, dshift, dscale, dpre = _prenorm_bwd(dx_out, dhns, x, pre_g, scale, "prenorm_bwd3")
    return dx, (dpre, dpost), (dshift, dscale, dgate), (jnp.concatenate(dws, axis=0), dwo), jnp.concatenate(dbs, axis=0)


def _pad_rows(a, rows):
    return jnp.pad(a, ((0, rows - a.shape[0]), (0, 0)))


def _lanes(a):
    flat = a.reshape(-1).astype(F32)
    rows = -(-flat.shape[0] // 1024) * 8
    return jnp.pad(flat, (0, rows * 128 - flat.shape[0])).reshape(rows, 128)


def kernel(x, c, norm_pre, norm_post, w_mod, b_mod, ffn_w_gate, ffn_w_up, ffn_w_down, mla_w_in, mla_q_norm, mla_w_q_up, mla_kv_norm, mla_w_kv_up, mla_w_o, dil_w_in, dil_w_o, rel_bias, loss_target, m_norm_pre, m_norm_post, m_w_mod, m_b_mod, m_ffn_w_gate, m_ffn_w_up, m_ffn_w_down, m_mla_w_in, m_mla_q_norm, m_mla_w_q_up, m_mla_kv_norm, m_mla_w_kv_up, m_mla_w_o, m_dil_w_in, m_dil_w_o, m_rel_bias, v_norm_pre, v_norm_post, v_w_mod, v_b_mod, v_ffn_w_gate, v_ffn_w_up, v_ffn_w_down, v_mla_w_in, v_mla_q_norm, v_mla_w_q_up, v_mla_kv_norm, v_mla_w_kv_up, v_mla_w_o, v_dil_w_in, v_dil_w_o, v_rel_bias):
    me = 4 * lax.axis_index("x") + 2 * lax.axis_index("y") + lax.axis_index("c")
    depth, n_sub, d_loc = norm_pre.shape
    d_model = x.shape[2]
    mod_loc_cols = w_mod.shape[2]
    x0, target = x[0], loss_target[0]

    small = jnp.concatenate([c.reshape(8, 128), _pad_rows(norm_pre.reshape(depth * n_sub, d_loc), 8),
                             _pad_rows(norm_post.reshape(depth * n_sub, d_loc), 8)], axis=0)
    small_all = _exchange([small], True, "gather_small")[0].reshape(NDEV, 24, 128)
    c_all = small_all[:, 0:8].reshape(NDEV, d_model)
    gains = lambda lo: jnp.transpose(small_all[:, lo:lo + depth * n_sub], (1, 0, 2)).reshape(depth, n_sub, 1, d_model)
    pre_full, post_full = gains(8), gains(16)

    b_loc = lax.dynamic_slice(b_mod, (0, me * mod_loc_cols), (depth, mod_loc_cols))
    mod_cols, silu_c = _mod_fwd(c_all, w_mod, b_loc, "mod_fwd")
    mod_all = _exchange([mod_cols.reshape(depth * NDEV, mod_loc_cols)], True, "gather_mod")[0]
    mod_all = mod_all.reshape(NDEV, depth, NDEV, mod_loc_cols)
    mod_mine = lax.dynamic_index_in_dim(mod_all, me, axis=2, keepdims=False)
    mod = jnp.transpose(mod_mine, (1, 0, 2)).reshape(depth, n_sub, 3, 1, d_model)

    bf_t = lambda a: a.astype(BF16).T
    ffn_ids = [(i, h) for i in range(depth) for h in range(2)]
    shards = []
    for i, h in ffn_ids:
        shards += [bf_t(ffn_w_gate[i, h]), bf_t(ffn_w_up[i, h]), ffn_w_down[i, h].astype(BF16)]
    shards += [mla_w_in[0].astype(BF16), bf_t(mla_w_q_up[0]), bf_t(mla_w_kv_up[0]), mla_w_o[0].astype(BF16),
               bf_t(dil_w_in[0]), dil_w_o[0].astype(BF16)]
    n_ffn = 3 * len(ffn_ids)
    members = {(0, 0): [0, 1, 2], (0, 1): [n_ffn, n_ffn + 1, n_ffn + 2, n_ffn + 3], (0, 2): [3, 4, 5],
               (1, 0): [6, 7, 8], (1, 1): [n_ffn + 4, n_ffn + 5], (1, 2): [9, 10, 11]}
    order = [(i, s) for i in range(depth) for s in range(n_sub)]
    g_sems, g_srcs, g_lands, g_token = _split_start(shards, [members[k] for k in order], True, me, "gather_weights_start")

    def weights_of(key, after):
        idx = members[key]
        return _split_wait(g_sems[order.index(key)], [g_srcs[k] for k in idx], [g_lands[k] for k in idx], after, True,
                           "gather_wait_%d%d" % key)

    lat_real = Q_LORA + KV_LORA
    qk = QK_NOPE + QK_ROPE

    def mla_weights(after):
        w_in, wq_t, wkv_t, wo = weights_of((0, 1), after)
        w_in_pad = jnp.concatenate([w_in[:, :lat_real], jnp.zeros((d_model, QK_NOPE), BF16), w_in[:, lat_real:],
                                    jnp.zeros((d_model, HEAD_PAD - QK_NOPE - QK_ROPE), BF16)], axis=1)
        wq_pad = jnp.pad(wq_t.reshape(MLA_HEADS, qk, Q_LORA), ((0, 0), (0, HEAD_PAD - qk), (0, 0)))
        wo_pad = jnp.pad(wo.reshape(MLA_HEADS, V_HEAD, d_model), ((0, 0), (HEAD_PAD - V_HEAD, 0), (0, 0)))
        return (w_in_pad, mla_q_norm, wq_pad.reshape(MLA_HEADS * HEAD_PAD, Q_LORA), mla_kv_norm, wkv_t,
                wo_pad.reshape(MLA_HEADS * HEAD_PAD, d_model))

    rope = _rope_tables()
    buckets = jnp.stack([_dil_buckets(dil) for _, dil in DIL_GROUPS])
    onehot = (buckets[..., None] == jnp.arange(N_BUCKETS)).astype(F32)
    bias = jnp.einsum("gqkb,bgh->ghqk", onehot, rel_bias.reshape(N_BUCKETS, len(DIL_GROUPS), DIL_HEADS),
                      precision=lax.Precision.HIGHEST)

    norms = lambda i, s: (pre_full[i, s], post_full[i, s])
    mods = lambda i, s: (mod[i, s, 0], mod[i, s, 1], mod[i, s, 2])
    saved, weights = {}, {}
    h = x0
    for i, s in order:
        if s != 1:
            weights[i, s] = tuple(weights_of((i, s), h))
            h, saved[i, s] = _ffn_fwd(h, norms(i, s), mods(i, s), weights[i, s])
        elif i % 2 == 0:
            weights[i, s] = mla_weights(h)
            h, saved[i, s] = _mla_fwd(h, norms(i, s), mods(i, s), weights[i, s], rope)
        else:
            weights[i, s] = tuple(weights_of((i, s), h))
            h, saved[i, s] = _dil_fwd(h, norms(i, s), mods(i, s), weights[i, s], bias)
    dh, loss_parts = _loss_grad(h, target, "loss")

    dnorm, dmod, sent = {}, {}, {}
    token = jnp.zeros((8, 128), F32)
    for i, s in reversed(order):
        md = mods(i, s)
        md = (md[0], md[1], md[2] + token[:1, :1])
        if s != 1:
            dh, dnorm[i, s], dmod[i, s], dws = _ffn_bwd(dh, saved[i, s], norms(i, s), md, weights[i, s])
        elif i % 2 == 0:
            dh, dnorm[i, s], dmod[i, s], dmla = _mla_bwd(dh, saved[i, s], norms(i, s), md, weights[i, s], rope)
            dw_in_pad, dq_norm, dwq_pad, dkv_norm, dwkv_t, dwo_pad = dmla
            dw_in = jnp.concatenate([dw_in_pad[:, :lat_real], dw_in_pad[:, lat_real + QK_NOPE:lat_real + qk]], axis=1)
            dwq_t = dwq_pad.reshape(MLA_HEADS, HEAD_PAD, Q_LORA)[:, :qk].reshape(MLA_HEADS * qk, Q_LORA)
            dwo = dwo_pad.reshape(MLA_HEADS, HEAD_PAD, d_model)[:, HEAD_PAD - V_HEAD:].reshape(MLA_HEADS * V_HEAD, d_model)
            dws = (dw_in, dwq_t, dwkv_t, dwo)
        else:
            dh, dnorm[i, s], dmod[i, s], dws, dbias = _dil_bwd(dh, saved[i, s], norms(i, s), md, weights[i, s], bias)
        sent[i, s] = _split_start(list(dws), [list(range(len(dws)))], False, me, "scatter_start_%d%d" % (i, s))
        token = sent[i, s][3]
    grad_x = dh[None]

    mine = {}
    for key in order:
        sems, srcs, lands, _ = sent[key]
        parts = _split_wait(sems[0], srcs, lands, dh, False, "scatter_wait_%d%d" % key)
        for k, p in zip(members[key], parts):
            mine[k] = _sum_parts(p, "sum_parts")
    g_gate = jnp.stack([mine[3 * n].T for n in range(len(ffn_ids))]).reshape(ffn_w_gate.shape)
    g_up = jnp.stack([mine[3 * n + 1].T for n in range(len(ffn_ids))]).reshape(ffn_w_up.shape)
    g_down = jnp.stack([mine[3 * n + 2] for n in range(len(ffn_ids))]).reshape(ffn_w_down.shape)
    g_mla_in, g_q_up, g_kv_up, g_mla_o, g_dil_in, g_dil_o = (mine[k] for k in range(n_ffn, n_ffn + 6))
    g_mla_in, g_q_up, g_kv_up, g_mla_o = g_mla_in[None], g_q_up.T[None], g_kv_up.T[None], g_mla_o[None]
    g_dil_in, g_dil_o = g_dil_in.T[None], g_dil_o[None]

    dmod_mine = jnp.concatenate([jnp.concatenate(dmod[i, s], axis=0) for i in range(depth) for s in range(n_sub)], axis=0)
    dpre_mine = jnp.concatenate([dnorm[i, s][0] for i in range(depth) for s in range(n_sub)], axis=0)
    dpost_mine = jnp.concatenate([dnorm[i, s][1] for i in range(depth) for s in range(n_sub)], axis=0)
    dbias_tab = _bias_reduce(dbias, buckets, "bias_reduce")[:, 0, :N_BUCKETS].T
    pieces = [dmod_mine, dpre_mine, dpost_mine, dq_norm, dkv_norm, dbias_tab, jnp.sum(loss_parts).reshape(1, 1)]
    packed = [_lanes(p) for p in pieces]
    offs = [0]
    for p in packed:
        offs.append(offs[-1] + p.shape[0])
    everyone = _exchange([jnp.concatenate(packed, axis=0)], True, "gather_small_grads")[0].reshape(NDEV, offs[-1], 128)
    total = _sum_parts(everyone, "sum_small")
    take = lambda n, shape: total[offs[n]:offs[n + 1]].reshape(-1)[:math.prod(shape)].reshape(shape)
    g_b_mod = take(0, b_mod.shape)
    col0 = me * d_loc
    g_norm_pre = lax.dynamic_slice(take(1, (depth, n_sub, d_model)), (0, 0, col0), norm_pre.shape)
    g_norm_post = lax.dynamic_slice(take(2, (depth, n_sub, d_model)), (0, 0, col0), norm_post.shape)
    g_q_norm, g_kv_norm = take(3, mla_q_norm.shape), take(4, mla_kv_norm.shape)
    g_rel_bias = take(5, rel_bias.shape)
    loss = take(6, ())

    dmod_all = everyone[:, offs[0]:offs[1]].reshape(NDEV, depth, NDEV * mod_loc_cols)
    dmod_cols = lax.dynamic_slice(dmod_all, (0, 0, me * mod_loc_cols), (NDEV, depth, mod_loc_cols))
    silu_t = jnp.pad(silu_c.T, ((0, 0), (0, HEAD_PAD - NDEV)))
    g_w_mod = jnp.stack([_mm([(silu_t, jnp.pad(dmod_cols[:, i], ((0, HEAD_PAD - NDEV), (0, 0))))], "nn", F32, TOKEN_TILE,
                             mod_loc_cols, "mod_bwd") for i in range(depth)])

    ws = (norm_pre, norm_post, w_mod, b_mod, ffn_w_gate, ffn_w_up, ffn_w_down, mla_w_in, mla_q_norm, mla_w_q_up, mla_kv_norm,
          mla_w_kv_up, mla_w_o, dil_w_in, dil_w_o, rel_bias)
    gs = (g_norm_pre, g_norm_post, g_w_mod, g_b_mod, g_gate, g_up, g_down, g_mla_in, g_q_norm, g_q_up, g_kv_norm, g_kv_up,
          g_mla_o, g_dil_in, g_dil_o, g_rel_bias)
    ms = (m_norm_pre, m_norm_post, m_w_mod, m_b_mod, m_ffn_w_gate, m_ffn_w_up, m_ffn_w_down, m_mla_w_in, m_mla_q_norm,
          m_mla_w_q_up, m_mla_kv_norm, m_mla_w_kv_up, m_mla_w_o, m_dil_w_in, m_dil_w_o, m_rel_bias)
    vs = (v_norm_pre, v_norm_post, v_w_mod, v_b_mod, v_ffn_w_gate, v_ffn_w_up, v_ffn_w_down, v_mla_w_in, v_mla_q_norm,
          v_mla_w_q_up, v_mla_kv_norm, v_mla_w_kv_up, v_mla_w_o, v_dil_w_in, v_dil_w_o, v_rel_bias)
    stepped = [_adamw(w, g, m, v, "adamw") for w, g, m, v in zip(ws, gs, ms, vs)]
    deltas, new_m, new_v = zip(*stepped)
    return (loss, grad_x, *gs, *deltas, *new_m, *new_v)
```

```python
import math

import jax
import jax.numpy as jnp
from jax import lax
from jax.experimental import pallas as pl
from jax.experimental.pallas import tpu as pltpu

F32 = jnp.float32
BF16 = jnp.bfloat16
MESH = pl.DeviceIdType.MESH

NDEV = 8
D_MODEL = 1024
SEQ = 2048
D_FF = 2816
EPS = 1e-6
FFN_RES = 0.5

MLA_HEADS = 16
Q_LORA = 384
KV_LORA = 256
QK_NOPE = 64
QK_ROPE = 32
V_HEAD = 64
ROPE_THETA = 10000.0
HEAD_PAD = 128
LAT_PAD = Q_LORA + KV_LORA + HEAD_PAD
MLA_SCALE = (QK_NOPE + QK_ROPE) ** -0.5

DIL_GROUPS = ((128, 1), (512, 4), (2048, 16))
DIL_HEADS = 16
DIL_HEAD_DIM = 64
DIL_BLOCK = 128
DIL_PAIRS = DIL_HEADS // 2
DIL_SCALE = DIL_HEAD_DIM ** -0.5
N_BUCKETS = 32
MAX_DISTANCE = 2048

ADAM_LR = 0.001
ADAM_B1 = 0.9
ADAM_B2 = 0.999
ADAM_EPS = 1e-08
ADAM_WD = 0.01
ADAM_STEP = 10

V7X_VMEM_BYTES = 64 * 2**20
VMEM_RESERVE = 10 * 2**20
TOKEN_TILE = 512


def _nbytes(shape, dtype):
    return math.prod(shape) * jnp.dtype(dtype).itemsize


def _params(semantics, blocks, extra=0):
    need = 2 * sum(_nbytes(s, d) for s, d in blocks) + extra + VMEM_RESERVE
    return pltpu.CompilerParams(dimension_semantics=semantics,
                                vmem_limit_bytes=int(min(need, V7X_VMEM_BYTES - VMEM_RESERVE)))


def _pcall(body, out_shape, **kw):
    call = pl.pallas_call(body, out_shape=jax.tree.map(lambda s: pltpu.HBM(s.shape, s.dtype), out_shape), **kw)
    return lambda *args: call(*[pltpu.with_memory_space_constraint(a, pltpu.HBM) for a in args])


def _dot_nn(a, b):
    return lax.dot_general(a, b, (((1,), (0,)), ((), ())), preferred_element_type=F32)


def _dot_nt(a, b):
    return lax.dot_general(a, b, (((1,), (1,)), ((), ())), preferred_element_type=F32)


def _dot_tn(a, b):
    return lax.dot_general(a, b, (((0,), (0,)), ((), ())), preferred_element_type=F32)


_DOTS = {"nn": _dot_nn, "nt": _dot_nt, "tn": _dot_tn}


def _rstd(v):
    return lax.rsqrt(jnp.mean(v * v, axis=-1, keepdims=True) + EPS)


def _rms_bwd(v, r, t):
    return r * t - v * (r * r * r) * jnp.mean(t * v, axis=-1, keepdims=True)


def _mm(pairs, mode, out_dtype, tm, tn, name, out_perm=1):
    a0, b0 = pairs[0]
    m_dim = a0.shape[1] if mode == "tn" else a0.shape[0]
    n_dim = b0.shape[0] if mode == "nt" else b0.shape[1]
    tm, tn = min(tm, m_dim // out_perm), min(tn, n_dim)
    assert m_dim % tm == 0 and n_dim % tn == 0, (name, m_dim, n_dim, tm, tn)
    dot = _DOTS[mode]
    npairs = len(pairs)

    def body(*refs):
        acc = None
        for p in range(npairs):
            d = dot(refs[2 * p][...].astype(BF16), refs[2 * p + 1][...].astype(BF16))
            acc = d if acc is None else acc + d
        refs[-1][...] = acc.astype(out_dtype)

    in_specs, blocks, flat = [], [], []
    for a, b in pairs:
        if mode == "nn":
            k = a.shape[1]
            sa, sb = ((tm, k), lambda i, j: (i, 0)), ((k, tn), lambda i, j: (0, j))
        elif mode == "nt":
            k = a.shape[1]
            sa, sb = ((tm, k), lambda i, j: (i, 0)), ((tn, k), lambda i, j: (j, 0))
        else:
            k = a.shape[0]
            sa, sb = ((k, tm), lambda i, j: (0, i)), ((k, tn), lambda i, j: (0, j))
        in_specs += [pl.BlockSpec(*sa), pl.BlockSpec(*sb)]
        blocks += [(sa[0], a.dtype), (sb[0], b.dtype)]
        flat += [a, b]
    if out_perm == 1:
        out_shape = (m_dim, n_dim)
        out_spec = pl.BlockSpec((tm, tn), lambda i, j: (i, j))
    else:
        rows = m_dim // out_perm
        assert tn == n_dim and rows % tm == 0, (name, rows, tm)
        nb = rows // tm
        out_shape = (rows, out_perm * n_dim)
        out_spec = pl.BlockSpec((tm, n_dim), lambda i, j: (i % nb, i // nb))
    blocks.append(((tm, tn), out_dtype))
    res = _pcall(
        body, out_shape=jax.ShapeDtypeStruct(out_shape, out_dtype), grid=(m_dim // tm, n_dim // tn),
        in_specs=in_specs, out_specs=out_spec, name=name,
        compiler_params=_params(("parallel", "parallel"), blocks, extra=2 * tm * tn * 4),
    )(*flat)
    return res.reshape(m_dim, n_dim)


def _prenorm_mm(x, pre_g, scale, shift, w, w_mode, out_dtype, tn, name, perm=1):
    s_dim, d_dim = x.shape
    n_dim = w.shape[0] if w_mode == "nt" else w.shape[1]
    rows = s_dim // perm
    tm = min(TOKEN_TILE, rows)
    nb = rows // tm
    tn = min(tn, n_dim)
    assert n_dim % tn == 0
    dot = _DOTS[w_mode]

    def body(x_ref, g_ref, sc_ref, sh_ref, w_ref, hn_ref, o_ref):
        @pl.when(pl.program_id(1) == 0)
        def _():
            xf = x_ref[...]
            hn = (xf * _rstd(xf) * g_ref[...]) * (1.0 + sc_ref[...]) + sh_ref[...]
            hn_ref[...] = hn.astype(BF16)

        o_ref[...] = dot(hn_ref[...], w_ref[...]).astype(out_dtype)

    vec = pl.BlockSpec((1, d_dim), lambda i, j: (0, 0))
    w_block = (tn, d_dim) if w_mode == "nt" else (d_dim, tn)
    w_spec = pl.BlockSpec(w_block, (lambda i, j: (j, 0)) if w_mode == "nt" else (lambda i, j: (0, j)))
    hn, out = _pcall(
        body,
        out_shape=(jax.ShapeDtypeStruct((s_dim, d_dim), BF16), jax.ShapeDtypeStruct((s_dim, n_dim), out_dtype)),
        grid=(s_dim // tm, n_dim // tn),
        in_specs=[pl.BlockSpec((tm, d_dim), lambda i, j: (i % nb, i // nb)), vec, vec, vec, w_spec],
        out_specs=(pl.BlockSpec((tm, d_dim), lambda i, j: (i, 0)), pl.BlockSpec((tm, tn), lambda i, j: (i, j))),
        name=name,
        compiler_params=_params(("parallel", "arbitrary"),
                                [((tm, d_dim), F32), (w_block, BF16), ((tm, d_dim), BF16), ((tm, tn), out_dtype)],
                                extra=3 * tm * d_dim * 4 + tm * tn * 4),
    )(x.reshape(rows, perm * d_dim), pre_g, scale, shift, w)
    return hn, out


def _ffn_up(x, pre_g, scale, shift, wg_t, wu_t, name):
    s_dim, d_dim = x.shape
    f_dim = wg_t.shape[0]
    tm, tn = TOKEN_TILE, f_dim // 2

    def body(x_ref, g_ref, sc_ref, sh_ref, wg_ref, wu_ref, hn_ref, go_ref, uo_ref, a_ref):
        @pl.when(pl.program_id(1) == 0)
        def _():
            xf = x_ref[...]
            hn = (xf * _rstd(xf) * g_ref[...]) * (1.0 + sc_ref[...]) + sh_ref[...]
            hn_ref[...] = hn.astype(BF16)

        hn = hn_ref[...]
        g = _dot_nt(hn, wg_ref[...])
        u = _dot_nt(hn, wu_ref[...])
        go_ref[...] = g.astype(BF16)
        uo_ref[...] = u.astype(BF16)
        a_ref[...] = (g * jax.nn.sigmoid(g) * u).astype(BF16)

    vec = pl.BlockSpec((1, d_dim), lambda i, j: (0, 0))
    w_spec = pl.BlockSpec((tn, d_dim), lambda i, j: (j, 0))
    act = pl.BlockSpec((tm, tn), lambda i, j: (i, j))
    act_shape = jax.ShapeDtypeStruct((s_dim, f_dim), BF16)
    return _pcall(
        body,
        out_shape=(jax.ShapeDtypeStruct((s_dim, d_dim), BF16), act_shape, act_shape, act_shape),
        grid=(s_dim // tm, f_dim // tn),
        in_specs=[pl.BlockSpec((tm, d_dim), lambda i, j: (i, 0)), vec, vec, vec, w_spec, w_spec],
        out_specs=(pl.BlockSpec((tm, d_dim), lambda i, j: (i, 0)), act, act, act),
        name=name,
        compiler_params=_params(("parallel", "arbitrary"),
                                [((tm, d_dim), F32), ((tn, d_dim), BF16), ((tn, d_dim), BF16), ((tm, d_dim), BF16)]
                                + 3 * [((tm, tn), BF16)], extra=3 * tm * d_dim * 4 + 4 * tm * tn * 4),
    )(x, pre_g, scale, shift, wg_t, wu_t)


def _mm_post(a, w, x, post_g, gate, res_w, name):
    s_dim, k_dim = a.shape
    d_dim = w.shape[1]
    tm = TOKEN_TILE

    def body(a_ref, w_ref, x_ref, pg_ref, gt_ref, xo_ref, f_ref):
        f = _dot_nn(a_ref[...], w_ref[...])
        y = f * _rstd(f) * pg_ref[...]
        f_ref[...] = f
        xo_ref[...] = x_ref[...] + (res_w * gt_ref[...]) * y

    vec = pl.BlockSpec((1, d_dim), lambda i: (0, 0))
    row = pl.BlockSpec((tm, d_dim), lambda i: (i, 0))
    out = jax.ShapeDtypeStruct((s_dim, d_dim), F32)
    return _pcall(
        body, out_shape=(out, out), grid=(s_dim // tm,),
        in_specs=[pl.BlockSpec((tm, k_dim), lambda i: (i, 0)), pl.BlockSpec((k_dim, d_dim), lambda i: (0, 0)), row, vec, vec],
        out_specs=(row, row), name=name,
        compiler_params=_params(("parallel",), [((tm, k_dim), BF16), ((k_dim, d_dim), BF16)] + 3 * [((tm, d_dim), F32)],
                                extra=3 * tm * d_dim * 4),
    )(a, w, x, post_g, gate)


def _post_bwd(dx_out, f, post_g, gate, res_w, name):
    s_dim, d_dim = f.shape
    tm = TOKEN_TILE

    def body(dx_ref, f_ref, pg_ref, gt_ref, df_ref, dgate_ref, dpost_ref):
        @pl.when(pl.program_id(0) == 0)
        def _():
            dgate_ref[...] = jnp.zeros_like(dgate_ref)
            dpost_ref[...] = jnp.zeros_like(dpost_ref)

        dx, fv = dx_ref[...], f_ref[...]
        r = _rstd(fv)
        fr = fv * r
        dgate_ref[...] += res_w * jnp.sum(dx * (fr * pg_ref[...]), axis=0, keepdims=True)
        dy = (res_w * gt_ref[...]) * dx
        dpost_ref[...] += jnp.sum(dy * fr, axis=0, keepdims=True)
        df_ref[...] = _rms_bwd(fv, r, dy * pg_ref[...]).astype(BF16)

    vec = pl.BlockSpec((1, d_dim), lambda i: (0, 0))
    row = pl.BlockSpec((tm, d_dim), lambda i: (i, 0))
    vshape = jax.ShapeDtypeStruct((1, d_dim), F32)
    return _pcall(
        body, out_shape=(jax.ShapeDtypeStruct((s_dim, d_dim), BF16), vshape, vshape), grid=(s_dim // tm,),
        in_specs=[row, row, vec, vec], out_specs=(row, vec, vec), name=name,
        compiler_params=_params(("arbitrary",), 3 * [((tm, d_dim), F32)], extra=6 * tm * d_dim * 4),
    )(dx_out, f, post_g, gate)


def _prenorm_bwd(dx_out, dhns, x, pre_g, scale, name):
    s_dim, d_dim = x.shape
    tm = TOKEN_TILE
    n_in = len(dhns)

    def body(*refs):
        dx_ref, x_ref, pg_ref, sc_ref = refs[n_in + 0], refs[n_in + 1], refs[n_in + 2], refs[n_in + 3]
        dxo_ref, dsh_ref, dsc_ref, dpg_ref = refs[n_in + 4:]

        @pl.when(pl.program_id(0) == 0)
        def _():
            dsh_ref[...] = jnp.zeros_like(dsh_ref)
            dsc_ref[...] = jnp.zeros_like(dsc_ref)
            dpg_ref[...] = jnp.zeros_like(dpg_ref)

        dhn = refs[0][...]
        for k in range(1, n_in):
            dhn = dhn + refs[k][...]
        xv = x_ref[...]
        r = _rstd(xv)
        xr = xv * r
        dsh_ref[...] += jnp.sum(dhn, axis=0, keepdims=True)
        dsc_ref[...] += jnp.sum(dhn * (xr * pg_ref[...]), axis=0, keepdims=True)
        dn = dhn * (1.0 + sc_ref[...])
        dpg_ref[...] += jnp.sum(dn * xr, axis=0, keepdims=True)
        dxo_ref[...] = dx_ref[...] + _rms_bwd(xv, r, dn * pg_ref[...])

    vec = pl.BlockSpec((1, d_dim), lambda i: (0, 0))
    row = pl.BlockSpec((tm, d_dim), lambda i: (i, 0))
    vshape = jax.ShapeDtypeStruct((1, d_dim), F32)
    return _pcall(
        body, out_shape=(jax.ShapeDtypeStruct((s_dim, d_dim), F32), vshape, vshape, vshape), grid=(s_dim // tm,),
        in_specs=n_in * [row] + [row, row, vec, vec], out_specs=(row, vec, vec, vec), name=name,
        compiler_params=_params(("arbitrary",), (n_in + 3) * [((tm, d_dim), F32)], extra=6 * tm * d_dim * 4),
    )(*dhns, dx_out, x, pre_g, scale)


def _ffn_dgu(df, wd, g, u, name):
    s_dim, d_dim = df.shape
    f_dim = wd.shape[0]
    tm, tn = TOKEN_TILE, f_dim // 2

    def body(df_ref, wd_ref, g_ref, u_ref, dg_ref, du_ref):
        da = _dot_nt(df_ref[...], wd_ref[...])
        gv, uv = g_ref[...].astype(F32), u_ref[...].astype(F32)
        sg = jax.nn.sigmoid(gv)
        du_ref[...] = (da * (gv * sg)).astype(BF16)
        dg_ref[...] = (da * uv * (sg * (1.0 + gv * (1.0 - sg)))).astype(BF16)

    act = pl.BlockSpec((tm, tn), lambda i, j: (i, j))
    act_shape = jax.ShapeDtypeStruct((s_dim, f_dim), BF16)
    return _pcall(
        body, out_shape=(act_shape, act_shape), grid=(s_dim // tm, f_dim // tn),
        in_specs=[pl.BlockSpec((tm, d_dim), lambda i, j: (i, 0)), pl.BlockSpec((tn, d_dim), lambda i, j: (j, 0)), act, act],
        out_specs=(act, act), name=name,
        compiler_params=_params(("parallel", "parallel"), [((tm, d_dim), BF16), ((tn, d_dim), BF16)] + 4 * [((tm, tn), BF16)],
                                extra=6 * tm * tn * 4),
    )(df, wd, g, u)


def _rope_tables():
    half = QK_ROPE // 2
    freqs = ROPE_THETA ** (-jnp.arange(half, dtype=F32) / half)
    ang = jnp.arange(SEQ, dtype=F32)[:, None] * freqs[None, :]
    cos, sin = jnp.cos(ang), jnp.sin(ang)
    ones = jnp.ones((SEQ, QK_NOPE), F32)
    zeros = jnp.zeros((SEQ, QK_NOPE), F32)
    pad1 = jnp.ones((SEQ, HEAD_PAD - QK_NOPE - QK_ROPE), F32)
    pad0 = jnp.zeros((SEQ, HEAD_PAD - QK_NOPE - QK_ROPE), F32)
    zh = jnp.zeros((SEQ, half), F32)
    c = jnp.concatenate([ones, cos, cos, pad1], axis=1)
    s1 = jnp.concatenate([zeros, -sin, zh, pad0], axis=1)
    s2 = jnp.concatenate([zeros, zh, sin, pad0], axis=1)
    return c, s1, s2


def _rope(v, c, s1, s2):
    half = QK_ROPE // 2
    return v * c + pltpu.roll(v, HEAD_PAD - half, 1) * s1 + pltpu.roll(v, half, 1) * s2


def _rope_t(dv, c, s1, s2):
    half = QK_ROPE // 2
    return dv * c + pltpu.roll(dv * s1, half, 1) + pltpu.roll(dv * s2, HEAD_PAD - half, 1)


def _mla_qkv(lat, q_norm, kv_norm, wq_t, wkv_t, rope, name):
    s_dim = lat.shape[0]
    width = MLA_HEADS * HEAD_PAD
    tm = 256

    def body(lat_ref, qg_ref, kg_ref, wq_ref, wkv_ref, c_ref, s1_ref, s2_ref, q_ref, k_ref, v_ref, qn_ref, kvn_ref):
        cq = lat_ref[:, :Q_LORA]
        ckv = lat_ref[:, Q_LORA:Q_LORA + KV_LORA]
        kr = lat_ref[:, Q_LORA + KV_LORA:]
        c, s1, s2 = c_ref[...], s1_ref[...], s2_ref[...]
        qn = (cq * _rstd(cq) * qg_ref[...]).astype(BF16)
        kvn = (ckv * _rstd(ckv) * kg_ref[...]).astype(BF16)
        qn_ref[...] = qn
        kvn_ref[...] = kvn
        q = _dot_nt(qn, wq_ref[...])
        kv = _dot_nt(kvn, wkv_ref[...])
        krr = _rope(kr, c, s1, s2)
        low = lax.broadcasted_iota(jnp.int32, (tm, HEAD_PAD), 1) < QK_NOPE
        for h in range(MLA_HEADS):
            sl = slice(h * HEAD_PAD, (h + 1) * HEAD_PAD)
            q_ref[:, sl] = _rope(q[:, sl], c, s1, s2).astype(BF16)
            kvh = kv[:, sl]
            k_ref[:, sl] = (jnp.where(low, kvh, 0.0) + krr).astype(BF16)
            v_ref[:, sl] = jnp.where(low, 0.0, kvh).astype(BF16)

    row = lambda n: pl.BlockSpec((tm, n), lambda i: (i, 0))
    full = lambda a: pl.BlockSpec(a.shape, lambda i: (0, 0))
    wide = jax.ShapeDtypeStruct((s_dim, width), BF16)
    return _pcall(
        body,
        out_shape=(wide, wide, wide, jax.ShapeDtypeStruct((s_dim, Q_LORA), BF16), jax.ShapeDtypeStruct((s_dim, KV_LORA), BF16)),
        grid=(s_dim // tm,),
        in_specs=[row(LAT_PAD), full(q_norm), full(kv_norm), full(wq_t), full(wkv_t), row(HEAD_PAD), row(HEAD_PAD), row(HEAD_PAD)],
        out_specs=(row(width), row(width), row(width), row(Q_LORA), row(KV_LORA)), name=name,
        compiler_params=_params(("parallel",), [((tm, LAT_PAD), F32), (wq_t.shape, BF16), (wkv_t.shape, BF16)]
                                + 3 * [((tm, width), BF16)], extra=4 * tm * width * 4),
    )(lat, q_norm, kv_norm, wq_t, wkv_t, *rope)


def _mla_probs(q, k, t, tq):
    s = _dot_nt(q, k) * MLA_SCALE
    rows = lax.broadcasted_iota(jnp.int32, s.shape, 0) + t * tq
    cols = lax.broadcasted_iota(jnp.int32, s.shape, 1)
    s = jnp.where(cols <= rows, s, -jnp.inf)
    e = jnp.exp(s - jnp.max(s, axis=-1, keepdims=True))
    return e / jnp.sum(e, axis=-1, keepdims=True)


def _mla_attn_fwd(q, k, v, name):
    s_dim = q.shape[0]
    tq = 512

    def body(q_ref, k_ref, v_ref, o_ref):
        for t in range(s_dim // tq):
            kt = (t + 1) * tq
            p = _mla_probs(q_ref[t * tq:kt, :], k_ref[:kt, :], t, tq)
            o_ref[t * tq:kt, :] = _dot_nn(p.astype(BF16), v_ref[:kt, :]).astype(BF16)

    head = pl.BlockSpec((s_dim, HEAD_PAD), lambda h: (0, h))
    return _pcall(
        body, out_shape=jax.ShapeDtypeStruct(q.shape, BF16), grid=(MLA_HEADS,),
        in_specs=[head, head, head], out_specs=head, name=name,
        compiler_params=_params(("parallel",), 4 * [((s_dim, HEAD_PAD), BF16)], extra=4 * tq * s_dim * 4),
    )(q, k, v)


def _mla_attn_bwd(q, k, v, d_o, name):
    s_dim = q.shape[0]
    tq = 512

    def body(q_ref, k_ref, v_ref, do_ref, dq_ref, dk_ref, dv_ref):
        dk_ref[...] = jnp.zeros_like(dk_ref)
        dv_ref[...] = jnp.zeros_like(dv_ref)
        for t in range(s_dim // tq):
            kt = (t + 1) * tq
            qt = q_ref[t * tq:kt, :]
            dot = do_ref[t * tq:kt, :].astype(BF16)
            p = _mla_probs(qt, k_ref[:kt, :], t, tq)
            dp = _dot_nt(dot, v_ref[:kt, :])
            ds = p * (dp - jnp.sum(p * dp, axis=-1, keepdims=True))
            dsb = (ds * MLA_SCALE).astype(BF16)
            dq_ref[t * tq:kt, :] = _dot_nn(dsb, k_ref[:kt, :])
            dk_ref[:kt, :] += _dot_tn(dsb, qt)
            dv_ref[:kt, :] += _dot_tn(p.astype(BF16), dot)

    head = pl.BlockSpec((s_dim, HEAD_PAD), lambda h: (0, h))
    out = jax.ShapeDtypeStruct(q.shape, F32)
    return _pcall(
        body, out_shape=(out, out, out), grid=(MLA_HEADS,),
        in_specs=[head, head, head, head], out_specs=(head, head, head), name=name,
        compiler_params=_params(("parallel",), 3 * [((s_dim, HEAD_PAD), BF16)] + 4 * [((s_dim, HEAD_PAD), F32)],
                                extra=6 * tq * s_dim * 4),
    )(q, k, v, d_o)


def _mla_qkv_bwd(dq, dk, dv, lat, q_norm, kv_norm, wq_t, wkv_t, rope, name):
    s_dim = lat.shape[0]
    width = MLA_HEADS * HEAD_PAD
    tm = 256

    def body(dq_ref, dk_ref, dv_ref, lat_ref, qg_ref, kg_ref, wq_ref, wkv_ref, c_ref, s1_ref, s2_ref,
             dqp_ref, dkv_ref, dlat_ref, dqg_ref, dkg_ref):
        @pl.when(pl.program_id(0) == 0)
        def _():
            dqg_ref[...] = jnp.zeros_like(dqg_ref)
            dkg_ref[...] = jnp.zeros_like(dkg_ref)

        c, s1, s2 = c_ref[...], s1_ref[...], s2_ref[...]
        lane = lax.broadcasted_iota(jnp.int32, (tm, HEAD_PAD), 1)
        low = lane < QK_NOPE
        rot = (lane >= QK_NOPE) & (lane < QK_NOPE + QK_ROPE)
        dkrr = jnp.zeros((tm, HEAD_PAD), F32)
        for h in range(MLA_HEADS):
            sl = slice(h * HEAD_PAD, (h + 1) * HEAD_PAD)
            dqp_ref[:, sl] = _rope_t(dq_ref[:, sl], c, s1, s2).astype(BF16)
            dkh = dk_ref[:, sl]
            dkv_ref[:, sl] = jnp.where(low, dkh, dv_ref[:, sl]).astype(BF16)
            dkrr = dkrr + jnp.where(rot, dkh, 0.0)
        dqn = _dot_nn(dqp_ref[...], wq_ref[...])
        dkvn = _dot_nn(dkv_ref[...], wkv_ref[...])
        cq = lat_ref[:, :Q_LORA]
        ckv = lat_ref[:, Q_LORA:Q_LORA + KV_LORA]
        rq, rkv = _rstd(cq), _rstd(ckv)
        dqg_ref[...] += jnp.sum(dqn * cq * rq, axis=0, keepdims=True)
        dkg_ref[...] += jnp.sum(dkvn * ckv * rkv, axis=0, keepdims=True)
        dlat_ref[:, :Q_LORA] = _rms_bwd(cq, rq, dqn * qg_ref[...])
        dlat_ref[:, Q_LORA:Q_LORA + KV_LORA] = _rms_bwd(ckv, rkv, dkvn * kg_ref[...])
        dlat_ref[:, Q_LORA + KV_LORA:] = _rope_t(dkrr, c, s1, s2)

    row = lambda n: pl.BlockSpec((tm, n), lambda i: (i, 0))
    full = lambda a: pl.BlockSpec(a.shape, lambda i: (0, 0))
    wide = jax.ShapeDtypeStruct((s_dim, width), BF16)
    return _pcall(
        body,
        out_shape=(wide, wide, jax.ShapeDtypeStruct((s_dim, LAT_PAD), F32),
                   jax.ShapeDtypeStruct(q_norm.shape, F32), jax.ShapeDtypeStruct(kv_norm.shape, F32)),
        grid=(s_dim // tm,),
        in_specs=[row(width), row(width), row(width), row(LAT_PAD), full(q_norm), full(kv_norm), full(wq_t), full(wkv_t),
                  row(HEAD_PAD), row(HEAD_PAD), row(HEAD_PAD)],
        out_specs=(row(width), row(width), row(LAT_PAD), full(q_norm), full(kv_norm)), name=name,
        compiler_params=_params(("arbitrary",), 3 * [((tm, width), F32)] + [((tm, LAT_PAD), F32), (wq_t.shape, BF16),
                                                                           (wkv_t.shape, BF16)] + 2 * [((tm, width), BF16)],
                                extra=2 * tm * width * 4),
    )(dq, dk, dv, lat, q_norm, kv_norm, wq_t, wkv_t, *rope)


def _t5_bucket(dist):
    max_exact = N_BUCKETS // 2
    d = jnp.maximum(dist, 1).astype(F32)
    large = max_exact + (jnp.log(d / max_exact) / math.log(MAX_DISTANCE / max_exact)
                         * (N_BUCKETS - max_exact)).astype(jnp.int32)
    large = jnp.minimum(large, N_BUCKETS - 1)
    return jnp.where(dist < max_exact, dist, large)


def _dil_buckets(dilation):
    iq = jnp.arange(DIL_BLOCK)[:, None]
    ik = jnp.arange(2 * DIL_BLOCK)[None, :]
    return _t5_bucket(jnp.maximum(DIL_BLOCK + iq - ik, 0) * dilation)


def _dil_logits(qh, k_ref, bias_h, n, span):
    lo = n * DIL_BLOCK
    if n == 0:
        s = _dot_nt(qh, k_ref[lo:lo + DIL_BLOCK, :]) * DIL_SCALE + bias_h[:, DIL_BLOCK:]
        rel = lax.broadcasted_iota(jnp.int32, s.shape, 0) - lax.broadcasted_iota(jnp.int32, s.shape, 1)
    else:
        s = _dot_nt(qh, k_ref[lo - DIL_BLOCK:lo + DIL_BLOCK, :]) * DIL_SCALE + bias_h
        rel = DIL_BLOCK + lax.broadcasted_iota(jnp.int32, s.shape, 0) - lax.broadcasted_iota(jnp.int32, s.shape, 1)
    return jnp.where((rel >= 0) & (rel <= span), s, -jnp.inf)


def _dil_views(dilation, rows):
    col = lambda which: pl.BlockSpec((rows, HEAD_PAD), lambda p, r: (r, which * DIL_PAIRS + p))
    nat = pl.BlockSpec((rows, HEAD_PAD), lambda p, r: (0, r * DIL_PAIRS + p))
    bias = pl.BlockSpec((2, DIL_BLOCK, 2 * DIL_BLOCK), lambda p, r: (p, 0, 0))
    return col, nat, bias


def _dil_attn_fwd(qkv, bias, dilation, span, name):
    s_dim = qkv.shape[0]
    rows = s_dim // dilation
    d_dim = DIL_HEADS * DIL_HEAD_DIM
    col, nat, bias_spec = _dil_views(dilation, rows)

    def body(q_ref, k_ref, v_ref, b_ref, o_ref, l_ref):
        lane = lax.broadcasted_iota(jnp.int32, (DIL_BLOCK, HEAD_PAD), 1)
        klane = lax.broadcasted_iota(jnp.int32, (2 * DIL_BLOCK, HEAD_PAD), 1)
        for n in range(rows // DIL_BLOCK):
            lo = n * DIL_BLOCK
            kv_rows = slice(lo, lo + DIL_BLOCK) if n == 0 else slice(lo - DIL_BLOCK, lo + DIL_BLOCK)
            qb, vb = q_ref[lo:lo + DIL_BLOCK, :], v_ref[kv_rows, :]
            o_acc = jnp.zeros((DIL_BLOCK, HEAD_PAD), F32)
            lse_acc = jnp.zeros((DIL_BLOCK, HEAD_PAD), F32)
            for h in range(2):
                mine = (lane < DIL_HEAD_DIM) == (h == 0)
                kmine = (klane[:vb.shape[0]] < DIL_HEAD_DIM) == (h == 0)
                logits = _dil_logits(jnp.where(mine, qb, 0), k_ref, b_ref[h], n, span)
                mx = jnp.max(logits, axis=-1, keepdims=True)
                lse = mx + jnp.log(jnp.sum(jnp.exp(logits - mx), axis=-1, keepdims=True))
                p = jnp.exp(logits - lse)
                o_acc = o_acc + _dot_nn(p.astype(BF16), jnp.where(kmine, vb, 0))
                lse_acc = jnp.where(mine, lse, lse_acc)
            o_ref[lo:lo + DIL_BLOCK, :] = o_acc
            l_ref[lo:lo + DIL_BLOCK, :] = lse_acc

    out = jax.ShapeDtypeStruct((rows, dilation * d_dim), F32)
    o, lse = _pcall(
        body, out_shape=(out, out), grid=(DIL_PAIRS, dilation),
        in_specs=[col(0), col(1), col(2), bias_spec], out_specs=(nat, nat), name=name,
        compiler_params=_params(("parallel", "parallel"), 3 * [((rows, HEAD_PAD), BF16)] + 2 * [((rows, HEAD_PAD), F32)]
                                + [((2, DIL_BLOCK, 2 * DIL_BLOCK), F32)], extra=2**21),
    )(qkv, qkv, qkv, bias)
    return o.reshape(s_dim, d_dim), lse.reshape(s_dim, d_dim)


def _dil_mix(lses, outs, name):
    s_dim, d_dim = outs[0].shape
    tm = TOKEN_TILE
    ng = len(outs)

    def body(*refs):
        ls = [refs[g][...] for g in range(ng)]
        mx = ls[0]
        for g in range(1, ng):
            mx = jnp.maximum(mx, ls[g])
        es = [jnp.exp(l - mx) for l in ls]
        tot = es[0]
        for g in range(1, ng):
            tot = tot + es[g]
        o = None
        for g in range(ng):
            al = es[g] / tot
            refs[2 * ng + g][...] = al
            t = al * refs[ng + g][...]
            o = t if o is None else o + t
        refs[3 * ng][...] = o
        refs[3 * ng + 1][...] = o.astype(BF16)

    row = pl.BlockSpec((tm, d_dim), lambda i: (i, 0))
    f = jax.ShapeDtypeStruct((s_dim, d_dim), F32)
    res = _pcall(
        body, out_shape=tuple(ng * [f] + [f, jax.ShapeDtypeStruct((s_dim, d_dim), BF16)]), grid=(s_dim // tm,),
        in_specs=2 * ng * [row], out_specs=tuple((ng + 2) * [row]), name=name,
        compiler_params=_params(("parallel",), (3 * ng + 2) * [((tm, d_dim), F32)], extra=4 * tm * d_dim * 4),
    )(*lses, *outs)
    return res[:ng], res[ng], res[ng + 1]


def _dil_attn_bwd(qkv, bias, d_o, o_mix, alpha, lse, dilation, span, name):
    s_dim = qkv.shape[0]
    rows = s_dim // dilation
    d_dim = DIL_HEADS * DIL_HEAD_DIM
    col, nat, bias_spec = _dil_views(dilation, rows)
    nat_view = lambda a: a.reshape(rows, dilation * d_dim)

    def body(q_ref, k_ref, v_ref, b_ref, do_ref, om_ref, al_ref, l_ref, dq_ref, dk_ref, dv_ref, db_ref, dk_acc, dv_acc):
        @pl.when(pl.program_id(1) == 0)
        def _():
            db_ref[...] = jnp.zeros_like(db_ref)

        dk_acc[...] = jnp.zeros_like(dk_acc)
        dv_acc[...] = jnp.zeros_like(dv_acc)
        lane = lax.broadcasted_iota(jnp.int32, (DIL_BLOCK, HEAD_PAD), 1)
        klane = lax.broadcasted_iota(jnp.int32, (2 * DIL_BLOCK, HEAD_PAD), 1)
        for n in range(rows // DIL_BLOCK):
            lo = n * DIL_BLOCK
            blk = slice(lo, lo + DIL_BLOCK)
            kv_rows = blk if n == 0 else slice(lo - DIL_BLOCK, lo + DIL_BLOCK)
            qb, kb, vb = q_ref[blk, :], k_ref[kv_rows, :], v_ref[kv_rows, :]
            al = al_ref[blk, :]
            dog = al * do_ref[blk, :]
            row_term = dog * om_ref[blk, :]
            lse_b = l_ref[blk, :]
            dq_acc = jnp.zeros((DIL_BLOCK, HEAD_PAD), F32)
            dk_blk = jnp.zeros((kb.shape[0], HEAD_PAD), F32)
            dv_blk = jnp.zeros((kb.shape[0], HEAD_PAD), F32)
            for h in range(2):
                mine = (lane < DIL_HEAD_DIM) == (h == 0)
                kmine = (klane[:kb.shape[0]] < DIL_HEAD_DIM) == (h == 0)
                qh = jnp.where(mine, qb, 0)
                logits = _dil_logits(qh, k_ref, b_ref[h], n, span)
                lse_h = jnp.max(jnp.where(mine, lse_b, -jnp.inf), axis=-1, keepdims=True)
                p = jnp.exp(logits - lse_h)
                dogh = jnp.where(mine, dog, 0.0).astype(BF16)
                dp = _dot_nt(dogh, vb)
                ds = p * (dp - jnp.sum(jnp.where(mine, row_term, 0.0), axis=-1, keepdims=True))
                if n == 0:
                    db_ref[h, :, DIL_BLOCK:] += ds
                else:
                    db_ref[h] += ds
                dsb = (ds * DIL_SCALE).astype(BF16)
                dq_acc = dq_acc + _dot_nn(dsb, jnp.where(kmine, kb, 0))
                dk_blk = dk_blk + _dot_tn(dsb, qh)
                dv_blk = dv_blk + _dot_tn(p.astype(BF16), dogh)
            dq_ref[blk, :] = dq_acc.astype(BF16)
            dk_acc[kv_rows, :] += dk_blk
            dv_acc[kv_rows, :] += dv_blk
        dk_ref[...] = dk_acc[...].astype(BF16)
        dv_ref[...] = dv_acc[...].astype(BF16)

    out_col = pl.BlockSpec((rows, HEAD_PAD), lambda p, r: (r, p))
    grad = jax.ShapeDtypeStruct((s_dim, d_dim), BF16)
    return _pcall(
        body, out_shape=(grad, grad, grad, jax.ShapeDtypeStruct(bias.shape, F32)), grid=(DIL_PAIRS, dilation),
        in_specs=[col(0), col(1), col(2), bias_spec, nat, nat, nat, nat],
        out_specs=(out_col, out_col, out_col, bias_spec), name=name,
        scratch_shapes=[pltpu.VMEM((rows, HEAD_PAD), F32), pltpu.VMEM((rows, HEAD_PAD), F32)],
        compiler_params=_params(("parallel", "arbitrary"), 6 * [((rows, HEAD_PAD), BF16)] + 4 * [((rows, HEAD_PAD), F32)]
                                + 2 * [((2, DIL_BLOCK, 2 * DIL_BLOCK), F32)], extra=2 * rows * HEAD_PAD * 4 + 2**21),
    )(qkv, qkv, qkv, bias, nat_view(d_o), nat_view(o_mix), nat_view(alpha), nat_view(lse))


def _bias_reduce(dbias, buckets, name):
    n_heads = dbias.shape[0]

    def body(db_ref, bk_ref, o_ref):
        ds, bk = db_ref[0], bk_ref[0]
        lane = lax.broadcasted_iota(jnp.int32, (8, HEAD_PAD), 1)
        acc = jnp.zeros((8, HEAD_PAD), F32)
        for b in range(N_BUCKETS):
            acc = jnp.where(lane == b, jnp.sum(jnp.where(bk == b, ds, 0.0)), acc)
        o_ref[0] = acc

    blk = (1, DIL_BLOCK, 2 * DIL_BLOCK)
    return _pcall(
        body, out_shape=jax.ShapeDtypeStruct((n_heads, 8, HEAD_PAD), F32), grid=(n_heads,),
        in_specs=[pl.BlockSpec(blk, lambda h: (h, 0, 0)), pl.BlockSpec(blk, lambda h: (h // DIL_HEADS, 0, 0))],
        out_specs=pl.BlockSpec((1, 8, HEAD_PAD), lambda h: (h, 0, 0)), name=name,
        compiler_params=_params(("parallel",), [(blk, F32), (blk, jnp.int32)], extra=2**20),
    )(dbias, buckets)


def _loss_grad(y, target, name):
    s_dim, d_dim = y.shape
    tm = TOKEN_TILE

    def body(y_ref, t_ref, dy_ref, l_ref):
        @pl.when(pl.program_id(0) == 0)
        def _():
            l_ref[...] = jnp.zeros_like(l_ref)

        err = y_ref[...] - t_ref[...]
        dy_ref[...] = err / d_dim
        sq = (err * err).reshape(tm // 8, 8, d_dim)
        l_ref[...] += 0.5 * jnp.sum(sq, axis=0) / d_dim

    row = pl.BlockSpec((tm, d_dim), lambda i: (i, 0))
    acc = pl.BlockSpec((8, d_dim), lambda i: (0, 0))
    return _pcall(
        body, out_shape=(jax.ShapeDtypeStruct((s_dim, d_dim), F32), jax.ShapeDtypeStruct((8, d_dim), F32)),
        grid=(s_dim // tm,), in_specs=[row, row], out_specs=(row, acc), name=name,
        compiler_params=_params(("arbitrary",), 3 * [((tm, d_dim), F32)], extra=2 * tm * d_dim * 4),
    )(y, target)


def _mod_fwd(c_all, w_mod, b_loc, name):
    depth, d_dim, n = w_mod.shape
    nb = c_all.shape[0]

    def body(c_ref, w_ref, b_ref, o_ref, s_ref):
        cv = c_ref[...]
        sc = cv * jax.nn.sigmoid(cv)
        s_ref[...] = sc
        o_ref[0] = _dot_nn(sc.astype(BF16), w_ref[0].astype(BF16)) + b_ref[0]

    return _pcall(
        body, out_shape=(jax.ShapeDtypeStruct((depth, nb, n), F32), jax.ShapeDtypeStruct((nb, d_dim), F32)), grid=(depth,),
        in_specs=[pl.BlockSpec((nb, d_dim), lambda i: (0, 0)), pl.BlockSpec((1, d_dim, n), lambda i: (i, 0, 0)),
                  pl.BlockSpec((1, 1, n), lambda i: (i, 0, 0))],
        out_specs=(pl.BlockSpec((1, nb, n), lambda i: (i, 0, 0)), pl.BlockSpec((nb, d_dim), lambda i: (0, 0))), name=name,
        compiler_params=_params(("arbitrary",), [((1, d_dim, n), F32)], extra=d_dim * n * 2 + 2**20),
    )(c_all, w_mod, b_loc.reshape(depth, 1, n))


def _sum_parts(parts, name):
    _, rows, cols = parts.shape
    tr = rows
    for cand in (512, 384, 256, 128, 64, 32, 16):
        if rows % cand == 0 and rows > cand:
            tr = cand
            break

    def body(p_ref, o_ref):
        acc = p_ref[0].astype(F32)
        for k in range(1, NDEV):
            acc = acc + p_ref[k].astype(F32)
        o_ref[...] = acc

    return _pcall(
        body, out_shape=jax.ShapeDtypeStruct((rows, cols), F32), grid=(rows // tr,),
        in_specs=[pl.BlockSpec((NDEV, tr, cols), lambda i: (0, i, 0))], out_specs=pl.BlockSpec((tr, cols), lambda i: (i, 0)),
        name=name, compiler_params=_params(("parallel",), [((NDEV, tr, cols), parts.dtype), ((tr, cols), F32)], extra=2**20),
    )(parts)


def _adamw(w, g, m, v, name):
    shape = w.shape
    cols = shape[-1]
    rows = math.prod(shape[:-1])
    tr = rows
    for cand in (512, 256, 128, 64, 32, 16, 8):
        if rows % cand == 0 and rows > cand and cand * cols * 4 <= 2**21:
            tr = cand
            break

    def body(w_ref, g_ref, m_ref, v_ref, d_ref, mo_ref, vo_ref):
        gv = g_ref[...]
        mn = ADAM_B1 * m_ref[...] + (1.0 - ADAM_B1) * gv
        vn = ADAM_B2 * v_ref[...] + (1.0 - ADAM_B2) * (gv * gv)
        m_hat = mn / (1.0 - ADAM_B1 ** ADAM_STEP)
        v_hat = vn / (1.0 - ADAM_B2 ** ADAM_STEP)
        d_ref[...] = -ADAM_LR * (m_hat / (jnp.sqrt(v_hat) + ADAM_EPS) + ADAM_WD * w_ref[...])
        mo_ref[...] = mn
        vo_ref[...] = vn

    blk = pl.BlockSpec((tr, cols), lambda i: (i, 0))
    out = jax.ShapeDtypeStruct((rows, cols), F32)
    res = _pcall(
        body, out_shape=(out, out, out), grid=(rows // tr,), in_specs=4 * [blk], out_specs=(blk, blk, blk), name=name,
        compiler_params=_params(("parallel",), 7 * [((tr, cols), F32)], extra=4 * tr * cols * 4),
    )(*(a.reshape(rows, cols) for a in (w, g, m, v)))
    return tuple(r.reshape(shape) for r in res)


def _peers():
    x, y, c = lax.axis_index("x"), lax.axis_index("y"), lax.axis_index("c")
    flip = lambda v, f: 1 - v if f else v
    peers = []
    for f in range(1, NDEV):
        px, py, pc = flip(x, f & 4), flip(y, f & 2), flip(c, f & 1)
        peers.append(((px, py, pc), 4 * px + 2 * py + pc))
    return (x, y, c), 4 * x + 2 * y + c, peers


def _exchange(arrs, gather, name):
    n = len(arrs)
    hbm = pl.BlockSpec(memory_space=pltpu.HBM)
    if gather:
        out_shape = [jax.ShapeDtypeStruct((NDEV * a.shape[0], a.shape[1]), a.dtype) for a in arrs]
    else:
        out_shape = [jax.ShapeDtypeStruct((NDEV, a.shape[0] // NDEV, a.shape[1]), a.dtype) for a in arrs]

    def body(*refs):
        ins, outs = refs[:n], refs[n:2 * n]
        send_sems, recv_sems, local_sems = refs[2 * n:]
        me_pos, me, peers = _peers()
        local = []
        for k in range(n):
            rows = arrs[k].shape[0] if gather else arrs[k].shape[0] // NDEV
            if gather:
                src_of = lambda idx: ins[k]
                dst_of = lambda idx: outs[k].at[pl.ds(me * rows, rows)]
                mine = (ins[k], outs[k].at[pl.ds(me * rows, rows)])
            else:
                src_of = lambda idx: ins[k].at[pl.ds(idx * rows, rows)]
                dst_of = lambda idx: outs[k].at[me]
                mine = (ins[k].at[pl.ds(me * rows, rows)], outs[k].at[me])
            cp = pltpu.make_async_copy(mine[0], mine[1], local_sems.at[k])
            cp.start()
            local.append(cp)
            for pos, idx in peers:
                pltpu.make_async_remote_copy(src_ref=src_of(idx), dst_ref=dst_of(idx), send_sem=send_sems.at[k],
                                             recv_sem=recv_sems.at[k], device_id=pos, device_id_type=MESH).start()
        for k in range(n):
            rows = arrs[k].shape[0] if gather else arrs[k].shape[0] // NDEV
            sent = ins[k].at[pl.ds(0, (NDEV - 1) * rows)] if not gather else outs[k].at[pl.ds(0, (NDEV - 1) * rows)]
            got = outs[k].at[pl.ds(0, (NDEV - 1) * rows)] if gather else outs[k].at[pl.ds(0, NDEV - 1)]
            pltpu.make_async_remote_copy(src_ref=sent, dst_ref=sent, send_sem=send_sems.at[k], recv_sem=recv_sems.at[k],
                                         device_id=me_pos, device_id_type=MESH).wait_send()
            pltpu.make_async_remote_copy(src_ref=got, dst_ref=got, send_sem=send_sems.at[k], recv_sem=recv_sems.at[k],
                                         device_id=me_pos, device_id_type=MESH).wait_recv()
            local[k].wait()

    return pl.pallas_call(
        body, out_shape=out_shape, in_specs=n * [hbm], out_specs=n * [hbm], name=name,
        scratch_shapes=[pltpu.SemaphoreType.DMA((n,)), pltpu.SemaphoreType.DMA((n,)), pltpu.SemaphoreType.DMA((n,))],
        compiler_params=pltpu.CompilerParams(has_side_effects=True),
    )(*arrs)


_HBM = pl.BlockSpec(memory_space=pltpu.HBM)
_SEM = pl.BlockSpec(memory_space=pltpu.SEMAPHORE)
_DATAFLOW = pltpu.SideEffectType.DATAFLOW_SIDE_EFFECTING


def _split_start(srcs, groups, gather, name):
    n = len(srcs)
    if gather:
        lands = [lax.empty((NDEV * a.shape[0], a.shape[1]), a.dtype) for a in srcs]
    else:
        lands = [lax.empty((NDEV, a.shape[0] // NDEV, a.shape[1]), a.dtype) for a in srcs]
    n_sem = 2 * len(groups)

    def body(*refs):
        src_refs, land_refs = refs[:n], refs[n:2 * n]
        sems = refs[2 * n:2 * n + n_sem]
        token, local_sems = refs[-2], refs[-1]
        _, my, peers = _peers()
        local = []
        for g, members in enumerate(groups):
            for j, k in enumerate(members):
                rows = srcs[k].shape[0] if gather else srcs[k].shape[0] // NDEV
                for pos, idx in peers:
                    src = src_refs[k] if gather else src_refs[k].at[pl.ds(idx * rows, rows)]
                    dst = land_refs[k].at[pl.ds(my * rows, rows)] if gather else land_refs[k].at[my]
                    pltpu.make_async_remote_copy(src_ref=src, dst_ref=dst, send_sem=sems[2 * g].at[j],
                                                 recv_sem=sems[2 * g + 1].at[j], device_id=pos, device_id_type=MESH).start()
                own_src = src_refs[k] if gather else src_refs[k].at[pl.ds(my * rows, rows)]
                own_dst = land_refs[k].at[pl.ds(my * rows, rows)] if gather else land_refs[k].at[my]
                local.append(pltpu.make_async_copy(own_src, own_dst, local_sems.at[k]))
                local[-1].start()
        for cp in local:
            cp.wait()
        token[...] = jnp.zeros_like(token)

    out_shape = []
    for members in groups:
        out_shape += [pltpu.SemaphoreType.DMA((len(members),)), pltpu.SemaphoreType.DMA((len(members),))]
    out_shape += [pltpu.HBM(a.shape, a.dtype) for a in srcs] + [pltpu.HBM(a.shape, a.dtype) for a in lands]
    out_shape.append(jax.ShapeDtypeStruct((8, 128), F32))
    res = pl.pallas_call(
        body, name=name, out_shape=tuple(out_shape), in_specs=2 * n * [_HBM],
        out_specs=tuple(n_sem * [_SEM] + 2 * n * [_HBM] + [pl.BlockSpec(memory_space=pltpu.VMEM)]),
        input_output_aliases={i: n_sem + i for i in range(2 * n)},
        scratch_shapes=[pltpu.SemaphoreType.DMA((n,))],
        compiler_params=pltpu.CompilerParams(has_side_effects=_DATAFLOW),
    )(*[pltpu.with_memory_space_constraint(a, pltpu.HBM) for a in list(srcs) + lands])
    sems = [(res[2 * g], res[2 * g + 1]) for g in range(len(groups))]
    return sems, list(res[n_sem:n_sem + n]), list(res[n_sem + n:n_sem + 2 * n]), res[-1]


def _split_wait(sems, srcs, lands, after, gather, name):
    n = len(srcs)

    def body(*refs):
        land_refs = refs[n:2 * n]
        send_sem, recv_sem = refs[2 * n], refs[2 * n + 1]
        me_pos, _, _ = _peers()
        for j in range(n):
            part = land_refs[j].at[pl.ds(0, (NDEV - 1) * (lands[j].shape[0] // NDEV))]
            pltpu.make_async_remote_copy(src_ref=part, dst_ref=part, send_sem=send_sem.at[j], recv_sem=recv_sem.at[j],
                                         device_id=me_pos, device_id_type=MESH).wait()

    res = pl.pallas_call(
        body, name=name, out_shape=tuple(pltpu.HBM(a.shape, a.dtype) for a in list(srcs) + list(lands)),
        in_specs=2 * n * [_HBM] + [_SEM, _SEM, pl.BlockSpec(memory_space=pl.ANY)], out_specs=tuple(2 * n * [_HBM]),
        input_output_aliases={i: i for i in range(2 * n)},
        compiler_params=pltpu.CompilerParams(has_side_effects=_DATAFLOW),
    )(*srcs, *lands, sems[0], sems[1], after)
    return list(res[n:])


def _ffn_fwd(x, norms, mod, w):
    (pre_g, post_g), (shift, scale, gate), (wg_t, wu_t, wd) = norms, mod, w
    hn, g, u, a = _ffn_up(x, pre_g, scale, shift, wg_t, wu_t, "ffn_up")
    x_out, f = _mm_post(a, wd, x, post_g, gate, FFN_RES, "ffn_down")
    return x_out, (x, hn, g, u, a, f)


def _ffn_bwd(dx_out, saved, norms, mod, w):
    (pre_g, post_g), (_, scale, gate), (wg_t, wu_t, wd) = norms, mod, w
    x, hn, g, u, a, f = saved
    d_model = x.shape[1]
    df, dgate, dpost = _post_bwd(dx_out, f, post_g, gate, FFN_RES, "ffn_post_bwd")
    dg, du = _ffn_dgu(df, wd, g, u, "ffn_dgu")
    dwd = _mm([(a, df)], "tn", BF16, 256, d_model, "ffn_dw")
    dwg_t = _mm([(dg, hn)], "tn", BF16, 256, d_model, "ffn_dw")
    dwu_t = _mm([(du, hn)], "tn", BF16, 256, d_model, "ffn_dw")
    dhn = _mm([(dg, wg_t), (du, wu_t)], "nn", F32, TOKEN_TILE, d_model, "ffn_dhn")
    dx, dshift, dscale, dpre = _prenorm_bwd(dx_out, [dhn], x, pre_g, scale, "prenorm_bwd")
    return dx, (dpre, dpost), (dshift, dscale, dgate), (dwg_t, dwu_t, dwd)


def _mla_fwd(x, norms, mod, w, rope):
    (pre_g, post_g), (shift, scale, gate) = norms, mod
    w_in, q_norm, wq_t, kv_norm, wkv_t, wo = w
    hn, lat = _prenorm_mm(x, pre_g, scale, shift, w_in, "nn", F32, LAT_PAD, "mla_in")
    q, k, v, qn, kvn = _mla_qkv(lat, q_norm, kv_norm, wq_t, wkv_t, rope, "mla_qkv")
    o = _mla_attn_fwd(q, k, v, "mla_attn_fwd")
    x_out, f = _mm_post(o, wo, x, post_g, gate, 1.0, "mla_out")
    return x_out, (x, hn, lat, q, k, v, qn, kvn, o, f)


def _mla_bwd(dx_out, saved, norms, mod, w, rope):
    (pre_g, post_g), (_, scale, gate) = norms, mod
    w_in, q_norm, wq_t, kv_norm, wkv_t, wo = w
    x, hn, lat, q, k, v, qn, kvn, o, f = saved
    d_model = x.shape[1]
    df, dgate, dpost = _post_bwd(dx_out, f, post_g, gate, 1.0, "mix_post_bwd")
    d_o = _mm([(df, wo)], "nt", F32, TOKEN_TILE, wo.shape[0], "mla_do")
    dwo = _mm([(o, df)], "tn", BF16, TOKEN_TILE, d_model, "mla_dwo")
    dq, dk, dv = _mla_attn_bwd(q, k, v, d_o, "mla_attn_bwd")
    dqp, dkv, dlat, dq_norm, dkv_norm = _mla_qkv_bwd(dq, dk, dv, lat, q_norm, kv_norm, wq_t, wkv_t, rope, "mla_qkv_bwd")
    dwq_t = _mm([(dqp, qn)], "tn", BF16, TOKEN_TILE, Q_LORA, "mla_dwq")
    dwkv_t = _mm([(dkv, kvn)], "tn", BF16, TOKEN_TILE, KV_LORA, "mla_dwkv")
    dw_in = _mm([(hn, dlat)], "tn", BF16, TOKEN_TILE, LAT_PAD, "mla_dwin")
    dhn = _mm([(dlat, w_in)], "nt", F32, TOKEN_TILE, d_model, "mla_dhn")
    dx, dshift, dscale, dpre = _prenorm_bwd(dx_out, [dhn], x, pre_g, scale, "prenorm_bwd")
    return dx, (dpre, dpost), (dshift, dscale, dgate), (dw_in, dq_norm, dwq_t, dkv_norm, dwkv_t, dwo)


def _dil_fwd(x, norms, mod, w, bias):
    (pre_g, post_g), (shift, scale, gate), (w_in_t, wo) = norms, mod, w
    width = 3 * DIL_HEADS * DIL_HEAD_DIM
    hns, qkvs, outs, lses = [], [], [], []
    for g, (window, dilation) in enumerate(DIL_GROUPS):
        hn, qkv = _prenorm_mm(x, pre_g, scale, shift, w_in_t[g * width:(g + 1) * width], "nt", BF16, width,
                              "dil_in", perm=dilation)
        o, lse = _dil_attn_fwd(qkv, bias[g], dilation, window // dilation, "dil_attn_fwd")
        hns.append(hn), qkvs.append(qkv), outs.append(o), lses.append(lse)
    alphas, o_mix, o_mix_b = _dil_mix(lses, outs, "dil_mix")
    x_out, f = _mm_post(o_mix_b, wo, x, post_g, gate, 1.0, "dil_out")
    return x_out, (x, hns, qkvs, lses, alphas, o_mix, o_mix_b, f)


def _dil_bwd(dx_out, saved, norms, mod, w, bias):
    (pre_g, post_g), (_, scale, gate), (w_in_t, wo) = norms, mod, w
    x, hns, qkvs, lses, alphas, o_mix, o_mix_b, f = saved
    d_model = x.shape[1]
    inner = DIL_HEADS * DIL_HEAD_DIM
    df, dgate, dpost = _post_bwd(dx_out, f, post_g, gate, 1.0, "mix_post_bwd")
    d_o = _mm([(df, wo)], "nt", F32, TOKEN_TILE, inner, "dil_do")
    dwo = _mm([(o_mix_b, df)], "tn", BF16, TOKEN_TILE, d_model, "dil_dwo")
    dhns, dws, dbs = [], [], []
    for g, (window, dilation) in enumerate(DIL_GROUPS):
        grads = _dil_attn_bwd(qkvs[g], bias[g], d_o, o_mix, alphas[g], lses[g], dilation, window // dilation, "dil_attn_bwd")
        dbs.append(grads[3])
        w_parts = [w_in_t[(3 * g + j) * inner:(3 * g + j + 1) * inner] for j in range(3)]
        dhns.append(_mm(list(zip(grads[:3], w_parts)), "nn", F32, TOKEN_TILE, d_model, "dil_dhn", out_perm=dilation))
        dws += [_mm([(grads[j], hns[g])], "tn", BF16, TOKEN_TILE, d_model, "dil_dwin") for j in range(3)]
    dx, dshift, dscale, dpre = _prenorm_bwd(dx_out, dhns, x, pre_g, scale, "prenorm_bwd3")
    return dx, (dpre, dpost), (dshift, dscale, dgate), (jnp.concatenate(dws, axis=0), dwo), jnp.concatenate(dbs, axis=0)


def _pad_rows(a, rows):
    return jnp.pad(a, ((0, rows - a.shape[0]), (0, 0)))


def _lanes(a):
    flat = a.reshape(-1).astype(F32)
    rows = -(-flat.shape[0] // 1024) * 8
    return jnp.pad(flat, (0, rows * 128 - flat.shape[0])).reshape(rows, 128)


def kernel(x, c, norm_pre, norm_post, w_mod, b_mod, ffn_w_gate, ffn_w_up, ffn_w_down, mla_w_in, mla_q_norm, mla_w_q_up, mla_kv_norm, mla_w_kv_up, mla_w_o, dil_w_in, dil_w_o, rel_bias, loss_target, m_norm_pre, m_norm_post, m_w_mod, m_b_mod, m_ffn_w_gate, m_ffn_w_up, m_ffn_w_down, m_mla_w_in, m_mla_q_norm, m_mla_w_q_up, m_mla_kv_norm, m_mla_w_kv_up, m_mla_w_o, m_dil_w_in, m_dil_w_o, m_rel_bias, v_norm_pre, v_norm_post, v_w_mod, v_b_mod, v_ffn_w_gate, v_ffn_w_up, v_ffn_w_down, v_mla_w_in, v_mla_q_norm, v_mla_w_q_up, v_mla_kv_norm, v_mla_w_kv_up, v_mla_w_o, v_dil_w_in, v_dil_w_o, v_rel_bias):
    me = 4 * lax.axis_index("x") + 2 * lax.axis_index("y") + lax.axis_index("c")
    depth, n_sub, d_loc = norm_pre.shape
    d_model = x.shape[2]
    mod_loc_cols = w_mod.shape[2]
    x0, target = x[0], loss_target[0]

    small = jnp.concatenate([c.reshape(8, 128), _pad_rows(norm_pre.reshape(depth * n_sub, d_loc), 8),
                             _pad_rows(norm_post.reshape(depth * n_sub, d_loc), 8)], axis=0)
    small_all = _exchange([small], True, "gather_small")[0].reshape(NDEV, 24, 128)
    c_all = small_all[:, 0:8].reshape(NDEV, d_model)
    gains = lambda lo: jnp.transpose(small_all[:, lo:lo + depth * n_sub], (1, 0, 2)).reshape(depth, n_sub, 1, d_model)
    pre_full, post_full = gains(8), gains(16)

    b_loc = lax.dynamic_slice(b_mod, (0, me * mod_loc_cols), (depth, mod_loc_cols))
    mod_cols, silu_c = _mod_fwd(c_all, w_mod, b_loc, "mod_fwd")
    mod_all = _exchange([mod_cols.reshape(depth * NDEV, mod_loc_cols)], True, "gather_mod")[0]
    mod_all = mod_all.reshape(NDEV, depth, NDEV, mod_loc_cols)
    mod_mine = lax.dynamic_index_in_dim(mod_all, me, axis=2, keepdims=False)
    mod = jnp.transpose(mod_mine, (1, 0, 2)).reshape(depth, n_sub, 3, 1, d_model)

    bf_t = lambda a: a.astype(BF16).T
    ffn_ids = [(i, h) for i in range(depth) for h in range(2)]
    shards = []
    for i, h in ffn_ids:
        shards += [bf_t(ffn_w_gate[i, h]), bf_t(ffn_w_up[i, h]), ffn_w_down[i, h].astype(BF16)]
    shards += [mla_w_in[0].astype(BF16), bf_t(mla_w_q_up[0]), bf_t(mla_w_kv_up[0]), mla_w_o[0].astype(BF16),
               bf_t(dil_w_in[0]), dil_w_o[0].astype(BF16)]
    n_ffn = 3 * len(ffn_ids)
    members = {(0, 0): [0, 1, 2], (0, 1): [n_ffn, n_ffn + 1, n_ffn + 2, n_ffn + 3], (0, 2): [3, 4, 5],
               (1, 0): [6, 7, 8], (1, 1): [n_ffn + 4, n_ffn + 5], (1, 2): [9, 10, 11]}
    order = [(i, s) for i in range(depth) for s in range(n_sub)]
    g_sems, g_srcs, g_lands, _ = _split_start(shards, [members[k] for k in order], True, "gather_weights_start")

    def weights_of(key, after):
        idx = members[key]
        return _split_wait(g_sems[order.index(key)], [g_srcs[k] for k in idx], [g_lands[k] for k in idx], after, True,
                           "gather_wait_%d%d" % key)

    lat_real = Q_LORA + KV_LORA
    qk = QK_NOPE + QK_ROPE

    def mla_weights(after):
        w_in, wq_t, wkv_t, wo = weights_of((0, 1), after)
        w_in_pad = jnp.concatenate([w_in[:, :lat_real], jnp.zeros((d_model, QK_NOPE), BF16), w_in[:, lat_real:],
                                    jnp.zeros((d_model, HEAD_PAD - QK_NOPE - QK_ROPE), BF16)], axis=1)
        wq_pad = jnp.pad(wq_t.reshape(MLA_HEADS, qk, Q_LORA), ((0, 0), (0, HEAD_PAD - qk), (0, 0)))
        wo_pad = jnp.pad(wo.reshape(MLA_HEADS, V_HEAD, d_model), ((0, 0), (HEAD_PAD - V_HEAD, 0), (0, 0)))
        return (w_in_pad, mla_q_norm, wq_pad.reshape(MLA_HEADS * HEAD_PAD, Q_LORA), mla_kv_norm, wkv_t,
                wo_pad.reshape(MLA_HEADS * HEAD_PAD, d_model))

    rope = _rope_tables()
    buckets = jnp.stack([_dil_buckets(dil) for _, dil in DIL_GROUPS])
    onehot = (buckets[..., None] == jnp.arange(N_BUCKETS)).astype(F32)
    bias = jnp.einsum("gqkb,bgh->ghqk", onehot, rel_bias.reshape(N_BUCKETS, len(DIL_GROUPS), DIL_HEADS),
                      precision=lax.Precision.HIGHEST)

    norms = lambda i, s: (pre_full[i, s], post_full[i, s])
    mods = lambda i, s: (mod[i, s, 0], mod[i, s, 1], mod[i, s, 2])
    saved, weights = {}, {}
    h = x0
    for i, s in order:
        if s != 1:
            weights[i, s] = tuple(weights_of((i, s), h))
            h, saved[i, s] = _ffn_fwd(h, norms(i, s), mods(i, s), weights[i, s])
        elif i % 2 == 0:
            weights[i, s] = mla_weights(h)
            h, saved[i, s] = _mla_fwd(h, norms(i, s), mods(i, s), weights[i, s], rope)
        else:
            weights[i, s] = tuple(weights_of((i, s), h))
            h, saved[i, s] = _dil_fwd(h, norms(i, s), mods(i, s), weights[i, s], bias)
    dh, loss_parts = _loss_grad(h, target, "loss")

    dnorm, dmod, sent = {}, {}, {}
    token = jnp.zeros((8, 128), F32)
    for i, s in reversed(order):
        md = mods(i, s)
        md = (md[0], md[1], md[2] + token[:1, :1])
        if s != 1:
            dh, dnorm[i, s], dmod[i, s], dws = _ffn_bwd(dh, saved[i, s], norms(i, s), md, weights[i, s])
        elif i % 2 == 0:
            dh, dnorm[i, s], dmod[i, s], dmla = _mla_bwd(dh, saved[i, s], norms(i, s), md, weights[i, s], rope)
            dw_in_pad, dq_norm, dwq_pad, dkv_norm, dwkv_t, dwo_pad = dmla
            dw_in = jnp.concatenate([dw_in_pad[:, :lat_real], dw_in_pad[:, lat_real + QK_NOPE:lat_real + qk]], axis=1)
            dwq_t = dwq_pad.reshape(MLA_HEADS, HEAD_PAD, Q_LORA)[:, :qk].reshape(MLA_HEADS * qk, Q_LORA)
            dwo = dwo_pad.reshape(MLA_HEADS, HEAD_PAD, d_model)[:, HEAD_PAD - V_HEAD:].reshape(MLA_HEADS * V_HEAD, d_model)
            dws = (dw_in, dwq_t, dwkv_t, dwo)
        else:
            dh, dnorm[i, s], dmod[i, s], dws, dbias = _dil_bwd(dh, saved[i, s], norms(i, s), md, weights[i, s], bias)
        sent[i, s] = _split_start(list(dws), [list(range(len(dws)))], False, "scatter_start_%d%d" % (i, s))
        token = sent[i, s][3]
    grad_x = dh[None]

    mine = {}
    for key in order:
        sems, srcs, lands, _ = sent[key]
        parts = _split_wait(sems[0], srcs, lands, dh, False, "scatter_wait_%d%d" % key)
        for k, p in zip(members[key], parts):
            mine[k] = _sum_parts(p, "sum_parts")
    g_gate = jnp.stack([mine[3 * n].T for n in range(len(ffn_ids))]).reshape(ffn_w_gate.shape)
    g_up = jnp.stack([mine[3 * n + 1].T for n in range(len(ffn_ids))]).reshape(ffn_w_up.shape)
    g_down = jnp.stack([mine[3 * n + 2] for n in range(len(ffn_ids))]).reshape(ffn_w_down.shape)
    g_mla_in, g_q_up, g_kv_up, g_mla_o, g_dil_in, g_dil_o = (mine[k] for k in range(n_ffn, n_ffn + 6))
    g_mla_in, g_q_up, g_kv_up, g_mla_o = g_mla_in[None], g_q_up.T[None], g_kv_up.T[None], g_mla_o[None]
    g_dil_in, g_dil_o = g_dil_in.T[None], g_dil_o[None]

    dmod_mine = jnp.concatenate([jnp.concatenate(dmod[i, s], axis=0) for i in range(depth) for s in range(n_sub)], axis=0)
    dpre_mine = jnp.concatenate([dnorm[i, s][0] for i in range(depth) for s in range(n_sub)], axis=0)
    dpost_mine = jnp.concatenate([dnorm[i, s][1] for i in range(depth) for s in range(n_sub)], axis=0)
    dbias_tab = _bias_reduce(dbias, buckets, "bias_reduce")[:, 0, :N_BUCKETS].T
    pieces = [dmod_mine, dpre_mine, dpost_mine, dq_norm, dkv_norm, dbias_tab, jnp.sum(loss_parts).reshape(1, 1)]
    packed = [_lanes(p) for p in pieces]
    offs = [0]
    for p in packed:
        offs.append(offs[-1] + p.shape[0])
    everyone = _exchange([jnp.concatenate(packed, axis=0)], True, "gather_small_grads")[0].reshape(NDEV, offs[-1], 128)
    total = _sum_parts(everyone, "sum_small")
    take = lambda n, shape: total[offs[n]:offs[n + 1]].reshape(-1)[:math.prod(shape)].reshape(shape)
    g_b_mod = take(0, b_mod.shape)
    col0 = me * d_loc
    g_norm_pre = lax.dynamic_slice(take(1, (depth, n_sub, d_model)), (0, 0, col0), norm_pre.shape)
    g_norm_post = lax.dynamic_slice(take(2, (depth, n_sub, d_model)), (0, 0, col0), norm_post.shape)
    g_q_norm, g_kv_norm = take(3, mla_q_norm.shape), take(4, mla_kv_norm.shape)
    g_rel_bias = take(5, rel_bias.shape)
    loss = take(6, ())

    dmod_all = everyone[:, offs[0]:offs[1]].reshape(NDEV, depth, NDEV * mod_loc_cols)
    dmod_cols = lax.dynamic_slice(dmod_all, (0, 0, me * mod_loc_cols), (NDEV, depth, mod_loc_cols))
    silu_t = jnp.pad(silu_c.T, ((0, 0), (0, HEAD_PAD - NDEV)))
    g_w_mod = jnp.stack([_mm([(silu_t, jnp.pad(dmod_cols[:, i], ((0, HEAD_PAD - NDEV), (0, 0))))], "nn", F32, TOKEN_TILE,
                             mod_loc_cols, "mod_bwd") for i in range(depth)])

    ws = (norm_pre, norm_post, w_mod, b_mod, ffn_w_gate, ffn_w_up, ffn_w_down, mla_w_in, mla_q_norm, mla_w_q_up, mla_kv_norm,
          mla_w_kv_up, mla_w_o, dil_w_in, dil_w_o, rel_bias)
    gs = (g_norm_pre, g_norm_post, g_w_mod, g_b_mod, g_gate, g_up, g_down, g_mla_in, g_q_norm, g_q_up, g_kv_norm, g_kv_up,
          g_mla_o, g_dil_in, g_dil_o, g_rel_bias)
    ms = (m_norm_pre, m_norm_post, m_w_mod, m_b_mod, m_ffn_w_gate, m_ffn_w_up, m_ffn_w_down, m_mla_w_in, m_mla_q_norm,
          m_mla_w_q_up, m_mla_kv_norm, m_mla_w_kv_up, m_mla_w_o, m_dil_w_in, m_dil_w_o, m_rel_bias)
    vs = (v_norm_pre, v_norm_post, v_w_mod, v_b_mod, v_ffn_w_gate, v_ffn_w_up, v_ffn_w_down, v_mla_w_in, v_mla_q_norm,
          v_mla_w_q_up, v_mla_kv_norm, v_mla_w_kv_up, v_mla_w_o, v_dil_w_in, v_dil_w_o, v_rel_bias)
    stepped = [_adamw(w, g, m, v, "adamw") for w, g, m, v in zip(ws, gs, ms, vs)]
    deltas, new_m, new_v = zip(*stepped)
    return (loss, grad_x, *gs, *deltas, *new_m, *new_v)
```

```python
import math

import jax
import jax.numpy as jnp
from jax import lax
from jax.experimental import pallas as pl
from jax.experimental.pallas import tpu as pltpu

F32 = jnp.float32
BF16 = jnp.bfloat16
MESH = pl.DeviceIdType.MESH

NDEV = 8
OTHER_CHIPS = 3
D_MODEL = 1024
SEQ = 2048
D_FF = 2816
EPS = 1e-6
FFN_RES = 0.5

MLA_HEADS = 16
Q_LORA = 384
KV_LORA = 256
QK_NOPE = 64
QK_ROPE = 32
V_HEAD = 64
ROPE_THETA = 10000.0
HEAD_PAD = 128
LAT_PAD = Q_LORA + KV_LORA + HEAD_PAD
MLA_SCALE = (QK_NOPE + QK_ROPE) ** -0.5

DIL_GROUPS = ((128, 1), (512, 4), (2048, 16))
DIL_HEADS = 16
DIL_HEAD_DIM = 64
DIL_BLOCK = 128
DIL_PAIRS = DIL_HEADS // 2
DIL_SCALE = DIL_HEAD_DIM ** -0.5
N_BUCKETS = 32
MAX_DISTANCE = 2048

ADAM_LR = 0.001
ADAM_B1 = 0.9
ADAM_B2 = 0.999
ADAM_EPS = 1e-08
ADAM_WD = 0.01
ADAM_STEP = 10

V7X_VMEM_BYTES = 64 * 2**20
VMEM_RESERVE = 10 * 2**20
TOKEN_TILE = 512


def _nbytes(shape, dtype):
    return math.prod(shape) * jnp.dtype(dtype).itemsize


def _params(semantics, blocks, extra=0):
    need = 2 * sum(_nbytes(s, d) for s, d in blocks) + extra + VMEM_RESERVE
    return pltpu.CompilerParams(dimension_semantics=semantics,
                                vmem_limit_bytes=int(min(need, V7X_VMEM_BYTES - VMEM_RESERVE)))


def _pcall(body, out_shape, **kw):
    call = pl.pallas_call(body, out_shape=jax.tree.map(lambda s: pltpu.HBM(s.shape, s.dtype), out_shape), **kw)
    return lambda *args: call(*[pltpu.with_memory_space_constraint(a, pltpu.HBM) for a in args])


def _dot_nn(a, b):
    return lax.dot_general(a, b, (((1,), (0,)), ((), ())), preferred_element_type=F32)


def _dot_nt(a, b):
    return lax.dot_general(a, b, (((1,), (1,)), ((), ())), preferred_element_type=F32)


def _dot_tn(a, b):
    return lax.dot_general(a, b, (((0,), (0,)), ((), ())), preferred_element_type=F32)


_DOTS = {"nn": _dot_nn, "nt": _dot_nt, "tn": _dot_tn}


def _rstd(v):
    return lax.rsqrt(jnp.mean(v * v, axis=-1, keepdims=True) + EPS)


def _rms_bwd(v, r, t):
    return r * t - v * (r * r * r) * jnp.mean(t * v, axis=-1, keepdims=True)


def _mm(pairs, mode, out_dtype, tm, tn, name, out_perm=1):
    a0, b0 = pairs[0]
    m_dim = a0.shape[1] if mode == "tn" else a0.shape[0]
    n_dim = b0.shape[0] if mode == "nt" else b0.shape[1]
    tm, tn = min(tm, m_dim // out_perm), min(tn, n_dim)
    assert m_dim % tm == 0 and n_dim % tn == 0, (name, m_dim, n_dim, tm, tn)
    dot = _DOTS[mode]
    npairs = len(pairs)

    def body(*refs):
        acc = None
        for p in range(npairs):
            d = dot(refs[2 * p][...].astype(BF16), refs[2 * p + 1][...].astype(BF16))
            acc = d if acc is None else acc + d
        refs[-1][...] = acc.astype(out_dtype)

    in_specs, blocks, flat = [], [], []
    for a, b in pairs:
        if mode == "nn":
            k = a.shape[1]
            sa, sb = ((tm, k), lambda i, j: (i, 0)), ((k, tn), lambda i, j: (0, j))
        elif mode == "nt":
            k = a.shape[1]
            sa, sb = ((tm, k), lambda i, j: (i, 0)), ((tn, k), lambda i, j: (j, 0))
        else:
            k = a.shape[0]
            sa, sb = ((k, tm), lambda i, j: (0, i)), ((k, tn), lambda i, j: (0, j))
        in_specs += [pl.BlockSpec(*sa), pl.BlockSpec(*sb)]
        blocks += [(sa[0], a.dtype), (sb[0], b.dtype)]
        flat += [a, b]
    if out_perm == 1:
        out_shape = (m_dim, n_dim)
        out_spec = pl.BlockSpec((tm, tn), lambda i, j: (i, j))
    else:
        rows = m_dim // out_perm
        assert tn == n_dim and rows % tm == 0, (name, rows, tm)
        nb = rows // tm
        out_shape = (rows, out_perm * n_dim)
        out_spec = pl.BlockSpec((tm, n_dim), lambda i, j: (i % nb, i // nb))
    blocks.append(((tm, tn), out_dtype))
    res = _pcall(
        body, out_shape=jax.ShapeDtypeStruct(out_shape, out_dtype), grid=(m_dim // tm, n_dim // tn),
        in_specs=in_specs, out_specs=out_spec, name=name,
        compiler_params=_params(("parallel", "parallel"), blocks, extra=2 * tm * tn * 4),
    )(*flat)
    return res.reshape(m_dim, n_dim)


def _prenorm_mm(x, pre_g, scale, shift, w, w_mode, out_dtype, tn, name, perm=1):
    s_dim, d_dim = x.shape
    n_dim = w.shape[0] if w_mode == "nt" else w.shape[1]
    rows = s_dim // perm
    tm = min(TOKEN_TILE, rows)
    nb = rows // tm
    tn = min(tn, n_dim)
    assert n_dim % tn == 0
    dot = _DOTS[w_mode]

    def body(x_ref, g_ref, sc_ref, sh_ref, w_ref, hn_ref, o_ref):
        @pl.when(pl.program_id(1) == 0)
        def _():
            xf = x_ref[...]
            hn = (xf * _rstd(xf) * g_ref[...]) * (1.0 + sc_ref[...]) + sh_ref[...]
            hn_ref[...] = hn.astype(BF16)

        o_ref[...] = dot(hn_ref[...], w_ref[...]).astype(out_dtype)

    vec = pl.BlockSpec((1, d_dim), lambda i, j: (0, 0))
    w_block = (tn, d_dim) if w_mode == "nt" else (d_dim, tn)
    w_spec = pl.BlockSpec(w_block, (lambda i, j: (j, 0)) if w_mode == "nt" else (lambda i, j: (0, j)))
    hn, out = _pcall(
        body,
        out_shape=(jax.ShapeDtypeStruct((s_dim, d_dim), BF16), jax.ShapeDtypeStruct((s_dim, n_dim), out_dtype)),
        grid=(s_dim // tm, n_dim // tn),
        in_specs=[pl.BlockSpec((tm, d_dim), lambda i, j: (i % nb, i // nb)), vec, vec, vec, w_spec],
        out_specs=(pl.BlockSpec((tm, d_dim), lambda i, j: (i, 0)), pl.BlockSpec((tm, tn), lambda i, j: (i, j))),
        name=name,
        compiler_params=_params(("parallel", "arbitrary"),
                                [((tm, d_dim), F32), (w_block, BF16), ((tm, d_dim), BF16), ((tm, tn), out_dtype)],
                                extra=3 * tm * d_dim * 4 + tm * tn * 4),
    )(x.reshape(rows, perm * d_dim), pre_g, scale, shift, w)
    return hn, out


def _ffn_up(x, pre_g, scale, shift, wg_t, wu_t, name):
    s_dim, d_dim = x.shape
    f_dim = wg_t.shape[0]
    tm, tn = TOKEN_TILE, f_dim // 2

    def body(x_ref, g_ref, sc_ref, sh_ref, wg_ref, wu_ref, hn_ref, go_ref, uo_ref, a_ref):
        @pl.when(pl.program_id(1) == 0)
        def _():
            xf = x_ref[...]
            hn = (xf * _rstd(xf) * g_ref[...]) * (1.0 + sc_ref[...]) + sh_ref[...]
            hn_ref[...] = hn.astype(BF16)

        hn = hn_ref[...]
        g = _dot_nt(hn, wg_ref[...])
        u = _dot_nt(hn, wu_ref[...])
        go_ref[...] = g.astype(BF16)
        uo_ref[...] = u.astype(BF16)
        a_ref[...] = (g * jax.nn.sigmoid(g) * u).astype(BF16)

    vec = pl.BlockSpec((1, d_dim), lambda i, j: (0, 0))
    w_spec = pl.BlockSpec((tn, d_dim), lambda i, j: (j, 0))
    act = pl.BlockSpec((tm, tn), lambda i, j: (i, j))
    act_shape = jax.ShapeDtypeStruct((s_dim, f_dim), BF16)
    return _pcall(
        body,
        out_shape=(jax.ShapeDtypeStruct((s_dim, d_dim), BF16), act_shape, act_shape, act_shape),
        grid=(s_dim // tm, f_dim // tn),
        in_specs=[pl.BlockSpec((tm, d_dim), lambda i, j: (i, 0)), vec, vec, vec, w_spec, w_spec],
        out_specs=(pl.BlockSpec((tm, d_dim), lambda i, j: (i, 0)), act, act, act),
        name=name,
        compiler_params=_params(("parallel", "arbitrary"),
                                [((tm, d_dim), F32), ((tn, d_dim), BF16), ((tn, d_dim), BF16), ((tm, d_dim), BF16)]
                                + 3 * [((tm, tn), BF16)], extra=3 * tm * d_dim * 4 + 4 * tm * tn * 4),
    )(x, pre_g, scale, shift, wg_t, wu_t)


def _mm_post(a, w, x, post_g, gate, res_w, name):
    s_dim, k_dim = a.shape
    d_dim = w.shape[1]
    tm = TOKEN_TILE

    def body(a_ref, w_ref, x_ref, pg_ref, gt_ref, xo_ref, f_ref):
        f = _dot_nn(a_ref[...], w_ref[...])
        y = f * _rstd(f) * pg_ref[...]
        f_ref[...] = f
        xo_ref[...] = x_ref[...] + (res_w * gt_ref[...]) * y

    vec = pl.BlockSpec((1, d_dim), lambda i: (0, 0))
    row = pl.BlockSpec((tm, d_dim), lambda i: (i, 0))
    out = jax.ShapeDtypeStruct((s_dim, d_dim), F32)
    return _pcall(
        body, out_shape=(out, out), grid=(s_dim // tm,),
        in_specs=[pl.BlockSpec((tm, k_dim), lambda i: (i, 0)), pl.BlockSpec((k_dim, d_dim), lambda i: (0, 0)), row, vec, vec],
        out_specs=(row, row), name=name,
        compiler_params=_params(("parallel",), [((tm, k_dim), BF16), ((k_dim, d_dim), BF16)] + 3 * [((tm, d_dim), F32)],
                                extra=3 * tm * d_dim * 4),
    )(a, w, x, post_g, gate)


def _post_bwd(dx_out, f, post_g, gate, res_w, name):
    s_dim, d_dim = f.shape
    tm = TOKEN_TILE

    def body(dx_ref, f_ref, pg_ref, gt_ref, df_ref, dgate_ref, dpost_ref):
        @pl.when(pl.program_id(0) == 0)
        def _():
            dgate_ref[...] = jnp.zeros_like(dgate_ref)
            dpost_ref[...] = jnp.zeros_like(dpost_ref)

        dx, fv = dx_ref[...], f_ref[...]
        r = _rstd(fv)
        fr = fv * r
        dgate_ref[...] += res_w * jnp.sum(dx * (fr * pg_ref[...]), axis=0, keepdims=True)
        dy = (res_w * gt_ref[...]) * dx
        dpost_ref[...] += jnp.sum(dy * fr, axis=0, keepdims=True)
        df_ref[...] = _rms_bwd(fv, r, dy * pg_ref[...]).astype(BF16)

    vec = pl.BlockSpec((1, d_dim), lambda i: (0, 0))
    row = pl.BlockSpec((tm, d_dim), lambda i: (i, 0))
    vshape = jax.ShapeDtypeStruct((1, d_dim), F32)
    return _pcall(
        body, out_shape=(jax.ShapeDtypeStruct((s_dim, d_dim), BF16), vshape, vshape), grid=(s_dim // tm,),
        in_specs=[row, row, vec, vec], out_specs=(row, vec, vec), name=name,
        compiler_params=_params(("arbitrary",), 3 * [((tm, d_dim), F32)], extra=6 * tm * d_dim * 4),
    )(dx_out, f, post_g, gate)


def _prenorm_bwd(dx_out, dhns, x, pre_g, scale, name):
    s_dim, d_dim = x.shape
    tm = TOKEN_TILE
    n_in = len(dhns)

    def body(*refs):
        dx_ref, x_ref, pg_ref, sc_ref = refs[n_in + 0], refs[n_in + 1], refs[n_in + 2], refs[n_in + 3]
        dxo_ref, dsh_ref, dsc_ref, dpg_ref = refs[n_in + 4:]

        @pl.when(pl.program_id(0) == 0)
        def _():
            dsh_ref[...] = jnp.zeros_like(dsh_ref)
            dsc_ref[...] = jnp.zeros_like(dsc_ref)
            dpg_ref[...] = jnp.zeros_like(dpg_ref)

        dhn = refs[0][...]
        for k in range(1, n_in):
            dhn = dhn + refs[k][...]
        xv = x_ref[...]
        r = _rstd(xv)
        xr = xv * r
        dsh_ref[...] += jnp.sum(dhn, axis=0, keepdims=True)
        dsc_ref[...] += jnp.sum(dhn * (xr * pg_ref[...]), axis=0, keepdims=True)
        dn = dhn * (1.0 + sc_ref[...])
        dpg_ref[...] += jnp.sum(dn * xr, axis=0, keepdims=True)
        dxo_ref[...] = dx_ref[...] + _rms_bwd(xv, r, dn * pg_ref[...])

    vec = pl.BlockSpec((1, d_dim), lambda i: (0, 0))
    row = pl.BlockSpec((tm, d_dim), lambda i: (i, 0))
    vshape = jax.ShapeDtypeStruct((1, d_dim), F32)
    return _pcall(
        body, out_shape=(jax.ShapeDtypeStruct((s_dim, d_dim), F32), vshape, vshape, vshape), grid=(s_dim // tm,),
        in_specs=n_in * [row] + [row, row, vec, vec], out_specs=(row, vec, vec, vec), name=name,
        compiler_params=_params(("arbitrary",), (n_in + 3) * [((tm, d_dim), F32)], extra=6 * tm * d_dim * 4),
    )(*dhns, dx_out, x, pre_g, scale)


def _ffn_dgu(df, wd, g, u, name):
    s_dim, d_dim = df.shape
    f_dim = wd.shape[0]
    tm, tn = TOKEN_TILE, f_dim // 2

    def body(df_ref, wd_ref, g_ref, u_ref, dg_ref, du_ref):
        da = _dot_nt(df_ref[...], wd_ref[...])
        gv, uv = g_ref[...].astype(F32), u_ref[...].astype(F32)
        sg = jax.nn.sigmoid(gv)
        du_ref[...] = (da * (gv * sg)).astype(BF16)
        dg_ref[...] = (da * uv * (sg * (1.0 + gv * (1.0 - sg)))).astype(BF16)

    act = pl.BlockSpec((tm, tn), lambda i, j: (i, j))
    act_shape = jax.ShapeDtypeStruct((s_dim, f_dim), BF16)
    return _pcall(
        body, out_shape=(act_shape, act_shape), grid=(s_dim // tm, f_dim // tn),
        in_specs=[pl.BlockSpec((tm, d_dim), lambda i, j: (i, 0)), pl.BlockSpec((tn, d_dim), lambda i, j: (j, 0)), act, act],
        out_specs=(act, act), name=name,
        compiler_params=_params(("parallel", "parallel"), [((tm, d_dim), BF16), ((tn, d_dim), BF16)] + 4 * [((tm, tn), BF16)],
                                extra=6 * tm * tn * 4),
    )(df, wd, g, u)


def _rope_tables():
    half = QK_ROPE // 2
    freqs = ROPE_THETA ** (-jnp.arange(half, dtype=F32) / half)
    ang = jnp.arange(SEQ, dtype=F32)[:, None] * freqs[None, :]
    cos, sin = jnp.cos(ang), jnp.sin(ang)
    ones = jnp.ones((SEQ, QK_NOPE), F32)
    zeros = jnp.zeros((SEQ, QK_NOPE), F32)
    pad1 = jnp.ones((SEQ, HEAD_PAD - QK_NOPE - QK_ROPE), F32)
    pad0 = jnp.zeros((SEQ, HEAD_PAD - QK_NOPE - QK_ROPE), F32)
    zh = jnp.zeros((SEQ, half), F32)
    c = jnp.concatenate([ones, cos, cos, pad1], axis=1)
    s1 = jnp.concatenate([zeros, -sin, zh, pad0], axis=1)
    s2 = jnp.concatenate([zeros, zh, sin, pad0], axis=1)
    return c, s1, s2


def _rope(v, c, s1, s2):
    half = QK_ROPE // 2
    return v * c + pltpu.roll(v, HEAD_PAD - half, 1) * s1 + pltpu.roll(v, half, 1) * s2


def _rope_t(dv, c, s1, s2):
    half = QK_ROPE // 2
    return dv * c + pltpu.roll(dv * s1, half, 1) + pltpu.roll(dv * s2, HEAD_PAD - half, 1)


def _mla_qkv(lat, q_norm, kv_norm, wq_t, wkv_t, rope, name):
    s_dim = lat.shape[0]
    width = MLA_HEADS * HEAD_PAD
    tm = 256

    def body(lat_ref, qg_ref, kg_ref, wq_ref, wkv_ref, c_ref, s1_ref, s2_ref, q_ref, k_ref, v_ref, qn_ref, kvn_ref):
        cq = lat_ref[:, :Q_LORA]
        ckv = lat_ref[:, Q_LORA:Q_LORA + KV_LORA]
        kr = lat_ref[:, Q_LORA + KV_LORA:]
        c, s1, s2 = c_ref[...], s1_ref[...], s2_ref[...]
        qn = (cq * _rstd(cq) * qg_ref[...]).astype(BF16)
        kvn = (ckv * _rstd(ckv) * kg_ref[...]).astype(BF16)
        qn_ref[...] = qn
        kvn_ref[...] = kvn
        q = _dot_nt(qn, wq_ref[...])
        kv = _dot_nt(kvn, wkv_ref[...])
        krr = _rope(kr, c, s1, s2)
        low = lax.broadcasted_iota(jnp.int32, (tm, HEAD_PAD), 1) < QK_NOPE
        for h in range(MLA_HEADS):
            sl = slice(h * HEAD_PAD, (h + 1) * HEAD_PAD)
            q_ref[:, sl] = _rope(q[:, sl], c, s1, s2).astype(BF16)
            kvh = kv[:, sl]
            k_ref[:, sl] = (jnp.where(low, kvh, 0.0) + krr).astype(BF16)
            v_ref[:, sl] = jnp.where(low, 0.0, kvh).astype(BF16)

    row = lambda n: pl.BlockSpec((tm, n), lambda i: (i, 0))
    full = lambda a: pl.BlockSpec(a.shape, lambda i: (0, 0))
    wide = jax.ShapeDtypeStruct((s_dim, width), BF16)
    return _pcall(
        body,
        out_shape=(wide, wide, wide, jax.ShapeDtypeStruct((s_dim, Q_LORA), BF16), jax.ShapeDtypeStruct((s_dim, KV_LORA), BF16)),
        grid=(s_dim // tm,),
        in_specs=[row(LAT_PAD), full(q_norm), full(kv_norm), full(wq_t), full(wkv_t), row(HEAD_PAD), row(HEAD_PAD), row(HEAD_PAD)],
        out_specs=(row(width), row(width), row(width), row(Q_LORA), row(KV_LORA)), name=name,
        compiler_params=_params(("parallel",), [((tm, LAT_PAD), F32), (wq_t.shape, BF16), (wkv_t.shape, BF16)]
                                + 3 * [((tm, width), BF16)], extra=4 * tm * width * 4),
    )(lat, q_norm, kv_norm, wq_t, wkv_t, *rope)


def _mla_probs(q, k, t, tq):
    s = _dot_nt(q, k) * MLA_SCALE
    rows = lax.broadcasted_iota(jnp.int32, s.shape, 0) + t * tq
    cols = lax.broadcasted_iota(jnp.int32, s.shape, 1)
    s = jnp.where(cols <= rows, s, -jnp.inf)
    e = jnp.exp(s - jnp.max(s, axis=-1, keepdims=True))
    return e / jnp.sum(e, axis=-1, keepdims=True)


def _mla_attn_fwd(q, k, v, name):
    s_dim = q.shape[0]
    tq = 512

    def body(q_ref, k_ref, v_ref, o_ref):
        for t in range(s_dim // tq):
            kt = (t + 1) * tq
            p = _mla_probs(q_ref[t * tq:kt, :], k_ref[:kt, :], t, tq)
            o_ref[t * tq:kt, :] = _dot_nn(p.astype(BF16), v_ref[:kt, :]).astype(BF16)

    head = pl.BlockSpec((s_dim, HEAD_PAD), lambda h: (0, h))
    return _pcall(
        body, out_shape=jax.ShapeDtypeStruct(q.shape, BF16), grid=(MLA_HEADS,),
        in_specs=[head, head, head], out_specs=head, name=name,
        compiler_params=_params(("parallel",), 4 * [((s_dim, HEAD_PAD), BF16)], extra=4 * tq * s_dim * 4),
    )(q, k, v)


def _mla_attn_bwd(q, k, v, d_o, name):
    s_dim = q.shape[0]
    tq = 512

    def body(q_ref, k_ref, v_ref, do_ref, dq_ref, dk_ref, dv_ref):
        dk_ref[...] = jnp.zeros_like(dk_ref)
        dv_ref[...] = jnp.zeros_like(dv_ref)
        for t in range(s_dim // tq):
            kt = (t + 1) * tq
            qt = q_ref[t * tq:kt, :]
            dot = do_ref[t * tq:kt, :].astype(BF16)
            p = _mla_probs(qt, k_ref[:kt, :], t, tq)
            dp = _dot_nt(dot, v_ref[:kt, :])
            ds = p * (dp - jnp.sum(p * dp, axis=-1, keepdims=True))
            dsb = (ds * MLA_SCALE).astype(BF16)
            dq_ref[t * tq:kt, :] = _dot_nn(dsb, k_ref[:kt, :])
            dk_ref[:kt, :] += _dot_tn(dsb, qt)
            dv_ref[:kt, :] += _dot_tn(p.astype(BF16), dot)

    head = pl.BlockSpec((s_dim, HEAD_PAD), lambda h: (0, h))
    out = jax.ShapeDtypeStruct(q.shape, F32)
    return _pcall(
        body, out_shape=(out, out, out), grid=(MLA_HEADS,),
        in_specs=[head, head, head, head], out_specs=(head, head, head), name=name,
        compiler_params=_params(("parallel",), 3 * [((s_dim, HEAD_PAD), BF16)] + 4 * [((s_dim, HEAD_PAD), F32)],
                                extra=6 * tq * s_dim * 4),
    )(q, k, v, d_o)


def _mla_qkv_bwd(dq, dk, dv, lat, q_norm, kv_norm, wq_t, wkv_t, rope, name):
    s_dim = lat.shape[0]
    width = MLA_HEADS * HEAD_PAD
    tm = 256

    def body(dq_ref, dk_ref, dv_ref, lat_ref, qg_ref, kg_ref, wq_ref, wkv_ref, c_ref, s1_ref, s2_ref,
             dqp_ref, dkv_ref, dlat_ref, dqg_ref, dkg_ref):
        @pl.when(pl.program_id(0) == 0)
        def _():
            dqg_ref[...] = jnp.zeros_like(dqg_ref)
            dkg_ref[...] = jnp.zeros_like(dkg_ref)

        c, s1, s2 = c_ref[...], s1_ref[...], s2_ref[...]
        lane = lax.broadcasted_iota(jnp.int32, (tm, HEAD_PAD), 1)
        low = lane < QK_NOPE
        rot = (lane >= QK_NOPE) & (lane < QK_NOPE + QK_ROPE)
        dkrr = jnp.zeros((tm, HEAD_PAD), F32)
        for h in range(MLA_HEADS):
            sl = slice(h * HEAD_PAD, (h + 1) * HEAD_PAD)
            dqp_ref[:, sl] = _rope_t(dq_ref[:, sl], c, s1, s2).astype(BF16)
            dkh = dk_ref[:, sl]
            dkv_ref[:, sl] = jnp.where(low, dkh, dv_ref[:, sl]).astype(BF16)
            dkrr = dkrr + jnp.where(rot, dkh, 0.0)
        dqn = _dot_nn(dqp_ref[...], wq_ref[...])
        dkvn = _dot_nn(dkv_ref[...], wkv_ref[...])
        cq = lat_ref[:, :Q_LORA]
        ckv = lat_ref[:, Q_LORA:Q_LORA + KV_LORA]
        rq, rkv = _rstd(cq), _rstd(ckv)
        dqg_ref[...] += jnp.sum(dqn * cq * rq, axis=0, keepdims=True)
        dkg_ref[...] += jnp.sum(dkvn * ckv * rkv, axis=0, keepdims=True)
        dlat_ref[:, :Q_LORA] = _rms_bwd(cq, rq, dqn * qg_ref[...])
        dlat_ref[:, Q_LORA:Q_LORA + KV_LORA] = _rms_bwd(ckv, rkv, dkvn * kg_ref[...])
        dlat_ref[:, Q_LORA + KV_LORA:] = _rope_t(dkrr, c, s1, s2)

    row = lambda n: pl.BlockSpec((tm, n), lambda i: (i, 0))
    full = lambda a: pl.BlockSpec(a.shape, lambda i: (0, 0))
    wide = jax.ShapeDtypeStruct((s_dim, width), BF16)
    return _pcall(
        body,
        out_shape=(wide, wide, jax.ShapeDtypeStruct((s_dim, LAT_PAD), F32),
                   jax.ShapeDtypeStruct(q_norm.shape, F32), jax.ShapeDtypeStruct(kv_norm.shape, F32)),
        grid=(s_dim // tm,),
        in_specs=[row(width), row(width), row(width), row(LAT_PAD), full(q_norm), full(kv_norm), full(wq_t), full(wkv_t),
                  row(HEAD_PAD), row(HEAD_PAD), row(HEAD_PAD)],
        out_specs=(row(width), row(width), row(LAT_PAD), full(q_norm), full(kv_norm)), name=name,
        compiler_params=_params(("arbitrary",), 3 * [((tm, width), F32)] + [((tm, LAT_PAD), F32), (wq_t.shape, BF16),
                                                                           (wkv_t.shape, BF16)] + 2 * [((tm, width), BF16)],
                                extra=2 * tm * width * 4),
    )(dq, dk, dv, lat, q_norm, kv_norm, wq_t, wkv_t, *rope)


def _t5_bucket(dist):
    max_exact = N_BUCKETS // 2
    d = jnp.maximum(dist, 1).astype(F32)
    large = max_exact + (jnp.log(d / max_exact) / math.log(MAX_DISTANCE / max_exact)
                         * (N_BUCKETS - max_exact)).astype(jnp.int32)
    large = jnp.minimum(large, N_BUCKETS - 1)
    return jnp.where(dist < max_exact, dist, large)


def _dil_buckets(dilation):
    iq = jnp.arange(DIL_BLOCK)[:, None]
    ik = jnp.arange(2 * DIL_BLOCK)[None, :]
    return _t5_bucket(jnp.maximum(DIL_BLOCK + iq - ik, 0) * dilation)


def _dil_logits(qh, k_ref, bias_h, n, span):
    lo = n * DIL_BLOCK
    if n == 0:
        s = _dot_nt(qh, k_ref[lo:lo + DIL_BLOCK, :]) * DIL_SCALE + bias_h[:, DIL_BLOCK:]
        rel = lax.broadcasted_iota(jnp.int32, s.shape, 0) - lax.broadcasted_iota(jnp.int32, s.shape, 1)
    else:
        s = _dot_nt(qh, k_ref[lo - DIL_BLOCK:lo + DIL_BLOCK, :]) * DIL_SCALE + bias_h
        rel = DIL_BLOCK + lax.broadcasted_iota(jnp.int32, s.shape, 0) - lax.broadcasted_iota(jnp.int32, s.shape, 1)
    return jnp.where((rel >= 0) & (rel <= span), s, -jnp.inf)


def _dil_views(dilation, rows):
    col = lambda which: pl.BlockSpec((rows, HEAD_PAD), lambda p, r: (r, which * DIL_PAIRS + p))
    nat = pl.BlockSpec((rows, HEAD_PAD), lambda p, r: (0, r * DIL_PAIRS + p))
    bias = pl.BlockSpec((2, DIL_BLOCK, 2 * DIL_BLOCK), lambda p, r: (p, 0, 0))
    return col, nat, bias


def _dil_attn_fwd(qkv, bias, dilation, span, name):
    s_dim = qkv.shape[0]
    rows = s_dim // dilation
    d_dim = DIL_HEADS * DIL_HEAD_DIM
    col, nat, bias_spec = _dil_views(dilation, rows)

    def body(q_ref, k_ref, v_ref, b_ref, o_ref, l_ref):
        lane = lax.broadcasted_iota(jnp.int32, (DIL_BLOCK, HEAD_PAD), 1)
        klane = lax.broadcasted_iota(jnp.int32, (2 * DIL_BLOCK, HEAD_PAD), 1)
        for n in range(rows // DIL_BLOCK):
            lo = n * DIL_BLOCK
            kv_rows = slice(lo, lo + DIL_BLOCK) if n == 0 else slice(lo - DIL_BLOCK, lo + DIL_BLOCK)
            qb, vb = q_ref[lo:lo + DIL_BLOCK, :], v_ref[kv_rows, :]
            o_acc = jnp.zeros((DIL_BLOCK, HEAD_PAD), F32)
            lse_acc = jnp.zeros((DIL_BLOCK, HEAD_PAD), F32)
            for h in range(2):
                mine = (lane < DIL_HEAD_DIM) == (h == 0)
                kmine = (klane[:vb.shape[0]] < DIL_HEAD_DIM) == (h == 0)
                logits = _dil_logits(jnp.where(mine, qb, 0), k_ref, b_ref[h], n, span)
                mx = jnp.max(logits, axis=-1, keepdims=True)
                lse = mx + jnp.log(jnp.sum(jnp.exp(logits - mx), axis=-1, keepdims=True))
                p = jnp.exp(logits - lse)
                o_acc = o_acc + _dot_nn(p.astype(BF16), jnp.where(kmine, vb, 0))
                lse_acc = jnp.where(mine, lse, lse_acc)
            o_ref[lo:lo + DIL_BLOCK, :] = o_acc
            l_ref[lo:lo + DIL_BLOCK, :] = lse_acc

    out = jax.ShapeDtypeStruct((rows, dilation * d_dim), F32)
    o, lse = _pcall(
        body, out_shape=(out, out), grid=(DIL_PAIRS, dilation),
        in_specs=[col(0), col(1), col(2), bias_spec], out_specs=(nat, nat), name=name,
        compiler_params=_params(("parallel", "parallel"), 3 * [((rows, HEAD_PAD), BF16)] + 2 * [((rows, HEAD_PAD), F32)]
                                + [((2, DIL_BLOCK, 2 * DIL_BLOCK), F32)], extra=2**21),
    )(qkv, qkv, qkv, bias)
    return o.reshape(s_dim, d_dim), lse.reshape(s_dim, d_dim)


def _dil_mix(lses, outs, name):
    s_dim, d_dim = outs[0].shape
    tm = TOKEN_TILE
    ng = len(outs)

    def body(*refs):
        ls = [refs[g][...] for g in range(ng)]
        mx = ls[0]
        for g in range(1, ng):
            mx = jnp.maximum(mx, ls[g])
        es = [jnp.exp(l - mx) for l in ls]
        tot = es[0]
        for g in range(1, ng):
            tot = tot + es[g]
        o = None
        for g in range(ng):
            al = es[g] / tot
            refs[2 * ng + g][...] = al
            t = al * refs[ng + g][...]
            o = t if o is None else o + t
        refs[3 * ng][...] = o
        refs[3 * ng + 1][...] = o.astype(BF16)

    row = pl.BlockSpec((tm, d_dim), lambda i: (i, 0))
    f = jax.ShapeDtypeStruct((s_dim, d_dim), F32)
    res = _pcall(
        body, out_shape=tuple(ng * [f] + [f, jax.ShapeDtypeStruct((s_dim, d_dim), BF16)]), grid=(s_dim // tm,),
        in_specs=2 * ng * [row], out_specs=tuple((ng + 2) * [row]), name=name,
        compiler_params=_params(("parallel",), (3 * ng + 2) * [((tm, d_dim), F32)], extra=4 * tm * d_dim * 4),
    )(*lses, *outs)
    return res[:ng], res[ng], res[ng + 1]


def _dil_attn_bwd(qkv, bias, d_o, o_mix, alpha, lse, dilation, span, name):
    s_dim = qkv.shape[0]
    rows = s_dim // dilation
    d_dim = DIL_HEADS * DIL_HEAD_DIM
    col, nat, bias_spec = _dil_views(dilation, rows)
    nat_view = lambda a: a.reshape(rows, dilation * d_dim)

    def body(q_ref, k_ref, v_ref, b_ref, do_ref, om_ref, al_ref, l_ref, dq_ref, dk_ref, dv_ref, db_ref, dk_acc, dv_acc):
        @pl.when(pl.program_id(1) == 0)
        def _():
            db_ref[...] = jnp.zeros_like(db_ref)

        dk_acc[...] = jnp.zeros_like(dk_acc)
        dv_acc[...] = jnp.zeros_like(dv_acc)
        lane = lax.broadcasted_iota(jnp.int32, (DIL_BLOCK, HEAD_PAD), 1)
        klane = lax.broadcasted_iota(jnp.int32, (2 * DIL_BLOCK, HEAD_PAD), 1)
        for n in range(rows // DIL_BLOCK):
            lo = n * DIL_BLOCK
            blk = slice(lo, lo + DIL_BLOCK)
            kv_rows = blk if n == 0 else slice(lo - DIL_BLOCK, lo + DIL_BLOCK)
            qb, kb, vb = q_ref[blk, :], k_ref[kv_rows, :], v_ref[kv_rows, :]
            al = al_ref[blk, :]
            dog = al * do_ref[blk, :]
            row_term = dog * om_ref[blk, :]
            lse_b = l_ref[blk, :]
            dq_acc = jnp.zeros((DIL_BLOCK, HEAD_PAD), F32)
            dk_blk = jnp.zeros((kb.shape[0], HEAD_PAD), F32)
            dv_blk = jnp.zeros((kb.shape[0], HEAD_PAD), F32)
            for h in range(2):
                mine = (lane < DIL_HEAD_DIM) == (h == 0)
                kmine = (klane[:kb.shape[0]] < DIL_HEAD_DIM) == (h == 0)
                qh = jnp.where(mine, qb, 0)
                logits = _dil_logits(qh, k_ref, b_ref[h], n, span)
                lse_h = jnp.max(jnp.where(mine, lse_b, -jnp.inf), axis=-1, keepdims=True)
                p = jnp.exp(logits - lse_h)
                dogh = jnp.where(mine, dog, 0.0).astype(BF16)
                dp = _dot_nt(dogh, vb)
                ds = p * (dp - jnp.sum(jnp.where(mine, row_term, 0.0), axis=-1, keepdims=True))
                if n == 0:
                    db_ref[h, :, DIL_BLOCK:] += ds
                else:
                    db_ref[h] += ds
                dsb = (ds * DIL_SCALE).astype(BF16)
                dq_acc = dq_acc + _dot_nn(dsb, jnp.where(kmine, kb, 0))
                dk_blk = dk_blk + _dot_tn(dsb, qh)
                dv_blk = dv_blk + _dot_tn(p.astype(BF16), dogh)
            dq_ref[blk, :] = dq_acc.astype(BF16)
            dk_acc[kv_rows, :] += dk_blk
            dv_acc[kv_rows, :] += dv_blk
        dk_ref[...] = dk_acc[...].astype(BF16)
        dv_ref[...] = dv_acc[...].astype(BF16)

    out_col = pl.BlockSpec((rows, HEAD_PAD), lambda p, r: (r, p))
    grad = jax.ShapeDtypeStruct((s_dim, d_dim), BF16)
    return _pcall(
        body, out_shape=(grad, grad, grad, jax.ShapeDtypeStruct(bias.shape, F32)), grid=(DIL_PAIRS, dilation),
        in_specs=[col(0), col(1), col(2), bias_spec, nat, nat, nat, nat],
        out_specs=(out_col, out_col, out_col, bias_spec), name=name,
        scratch_shapes=[pltpu.VMEM((rows, HEAD_PAD), F32), pltpu.VMEM((rows, HEAD_PAD), F32)],
        compiler_params=_params(("parallel", "arbitrary"), 6 * [((rows, HEAD_PAD), BF16)] + 4 * [((rows, HEAD_PAD), F32)]
                                + 2 * [((2, DIL_BLOCK, 2 * DIL_BLOCK), F32)], extra=2 * rows * HEAD_PAD * 4 + 2**21),
    )(qkv, qkv, qkv, bias, nat_view(d_o), nat_view(o_mix), nat_view(alpha), nat_view(lse))


def _bias_reduce(dbias, buckets, name):
    n_heads = dbias.shape[0]

    def body(db_ref, bk_ref, o_ref):
        ds, bk = db_ref[0], bk_ref[0]
        lane = lax.broadcasted_iota(jnp.int32, (8, HEAD_PAD), 1)
        acc = jnp.zeros((8, HEAD_PAD), F32)
        for b in range(N_BUCKETS):
            acc = jnp.where(lane == b, jnp.sum(jnp.where(bk == b, ds, 0.0)), acc)
        o_ref[0] = acc

    blk = (1, DIL_BLOCK, 2 * DIL_BLOCK)
    return _pcall(
        body, out_shape=jax.ShapeDtypeStruct((n_heads, 8, HEAD_PAD), F32), grid=(n_heads,),
        in_specs=[pl.BlockSpec(blk, lambda h: (h, 0, 0)), pl.BlockSpec(blk, lambda h: (h // DIL_HEADS, 0, 0))],
        out_specs=pl.BlockSpec((1, 8, HEAD_PAD), lambda h: (h, 0, 0)), name=name,
        compiler_params=_params(("parallel",), [(blk, F32), (blk, jnp.int32)], extra=2**20),
    )(dbias, buckets)


def _loss_grad(y, target, name):
    s_dim, d_dim = y.shape
    tm = TOKEN_TILE

    def body(y_ref, t_ref, dy_ref, l_ref):
        @pl.when(pl.program_id(0) == 0)
        def _():
            l_ref[...] = jnp.zeros_like(l_ref)

        err = y_ref[...] - t_ref[...]
        dy_ref[...] = err / d_dim
        sq = (err * err).reshape(tm // 8, 8, d_dim)
        l_ref[...] += 0.5 * jnp.sum(sq, axis=0) / d_dim

    row = pl.BlockSpec((tm, d_dim), lambda i: (i, 0))
    acc = pl.BlockSpec((8, d_dim), lambda i: (0, 0))
    return _pcall(
        body, out_shape=(jax.ShapeDtypeStruct((s_dim, d_dim), F32), jax.ShapeDtypeStruct((8, d_dim), F32)),
        grid=(s_dim // tm,), in_specs=[row, row], out_specs=(row, acc), name=name,
        compiler_params=_params(("arbitrary",), 3 * [((tm, d_dim), F32)], extra=2 * tm * d_dim * 4),
    )(y, target)


def _mod_fwd(c_all, w_mod, b_loc, name):
    depth, d_dim, n = w_mod.shape
    nb = c_all.shape[0]

    def body(c_ref, w_ref, b_ref, o_ref, s_ref):
        cv = c_ref[...]
        sc = cv * jax.nn.sigmoid(cv)
        s_ref[...] = sc
        o_ref[0] = _dot_nn(sc.astype(BF16), w_ref[0].astype(BF16)) + b_ref[0]

    return _pcall(
        body, out_shape=(jax.ShapeDtypeStruct((depth, nb, n), F32), jax.ShapeDtypeStruct((nb, d_dim), F32)), grid=(depth,),
        in_specs=[pl.BlockSpec((nb, d_dim), lambda i: (0, 0)), pl.BlockSpec((1, d_dim, n), lambda i: (i, 0, 0)),
                  pl.BlockSpec((1, 1, n), lambda i: (i, 0, 0))],
        out_specs=(pl.BlockSpec((1, nb, n), lambda i: (i, 0, 0)), pl.BlockSpec((nb, d_dim), lambda i: (0, 0))), name=name,
        compiler_params=_params(("arbitrary",), [((1, d_dim, n), F32)], extra=d_dim * n * 2 + 2**20),
    )(c_all, w_mod, b_loc.reshape(depth, 1, n))


def _sum_parts(parts, name):
    _, rows, cols = parts.shape
    tr = rows
    for cand in (512, 384, 256, 128, 64, 32, 16):
        if rows % cand == 0 and rows > cand:
            tr = cand
            break

    def body(p_ref, o_ref):
        acc = p_ref[0].astype(F32)
        for k in range(1, NDEV):
            acc = acc + p_ref[k].astype(F32)
        o_ref[...] = acc

    return _pcall(
        body, out_shape=jax.ShapeDtypeStruct((rows, cols), F32), grid=(rows // tr,),
        in_specs=[pl.BlockSpec((NDEV, tr, cols), lambda i: (0, i, 0))], out_specs=pl.BlockSpec((tr, cols), lambda i: (i, 0)),
        name=name, compiler_params=_params(("parallel",), [((NDEV, tr, cols), parts.dtype), ((tr, cols), F32)], extra=2**20),
    )(parts)


def _adamw(w, g, m, v, name):
    shape = w.shape
    cols = shape[-1]
    rows = math.prod(shape[:-1])
    tr = rows
    for cand in (512, 256, 128, 64, 32, 16, 8):
        if rows % cand == 0 and rows > cand and cand * cols * 4 <= 2**21:
            tr = cand
            break

    def body(w_ref, g_ref, m_ref, v_ref, d_ref, mo_ref, vo_ref):
        gv = g_ref[...]
        mn = ADAM_B1 * m_ref[...] + (1.0 - ADAM_B1) * gv
        vn = ADAM_B2 * v_ref[...] + (1.0 - ADAM_B2) * (gv * gv)
        m_hat = mn / (1.0 - ADAM_B1 ** ADAM_STEP)
        v_hat = vn / (1.0 - ADAM_B2 ** ADAM_STEP)
        d_ref[...] = -ADAM_LR * (m_hat / (jnp.sqrt(v_hat) + ADAM_EPS) + ADAM_WD * w_ref[...])
        mo_ref[...] = mn
        vo_ref[...] = vn

    blk = pl.BlockSpec((tr, cols), lambda i: (i, 0))
    out = jax.ShapeDtypeStruct((rows, cols), F32)
    res = _pcall(
        body, out_shape=(out, out, out), grid=(rows // tr,), in_specs=4 * [blk], out_specs=(blk, blk, blk), name=name,
        compiler_params=_params(("parallel",), 7 * [((tr, cols), F32)], extra=4 * tr * cols * 4),
    )(*(a.reshape(rows, cols) for a in (w, g, m, v)))
    return tuple(r.reshape(shape) for r in res)


def _peers():
    x, y, c = lax.axis_index("x"), lax.axis_index("y"), lax.axis_index("c")
    flip = lambda v, f: 1 - v if f else v
    peers = []
    for f in range(1, NDEV):
        px, py, pc = flip(x, f & 4), flip(y, f & 2), flip(c, f & 1)
        peers.append(((px, py, pc), 4 * px + 2 * py + pc))
    return (x, y, c), 4 * x + 2 * y + c, peers


def _places():
    x, y, c = lax.axis_index("x"), lax.axis_index("y"), lax.axis_index("c")
    place = lambda px, py, pc: ((px, py, pc), 4 * px + 2 * py + pc)
    return place(x, y, c), place(x, y, 1 - c), [place(1 - x, y, c), place(x, 1 - y, c), place(1 - x, 1 - y, c)]


def _exchange(arrs, gather, name):
    n = len(arrs)
    hbm = pl.BlockSpec(memory_space=pltpu.HBM)
    if gather:
        out_shape = [jax.ShapeDtypeStruct((NDEV * a.shape[0], a.shape[1]), a.dtype) for a in arrs]
    else:
        out_shape = [jax.ShapeDtypeStruct((NDEV, a.shape[0] // NDEV, a.shape[1]), a.dtype) for a in arrs]

    def body(*refs):
        ins, outs = refs[:n], refs[n:2 * n]
        send_sems, recv_sems, local_sems = refs[2 * n:]
        me_pos, me, peers = _peers()
        local = []
        for k in range(n):
            rows = arrs[k].shape[0] if gather else arrs[k].shape[0] // NDEV
            if gather:
                src_of = lambda idx: ins[k]
                dst_of = lambda idx: outs[k].at[pl.ds(me * rows, rows)]
                mine = (ins[k], outs[k].at[pl.ds(me * rows, rows)])
            else:
                src_of = lambda idx: ins[k].at[pl.ds(idx * rows, rows)]
                dst_of = lambda idx: outs[k].at[me]
                mine = (ins[k].at[pl.ds(me * rows, rows)], outs[k].at[me])
            cp = pltpu.make_async_copy(mine[0], mine[1], local_sems.at[k])
            cp.start()
            local.append(cp)
            for pos, idx in peers:
                pltpu.make_async_remote_copy(src_ref=src_of(idx), dst_ref=dst_of(idx), send_sem=send_sems.at[k],
                                             recv_sem=recv_sems.at[k], device_id=pos, device_id_type=MESH).start()
        for k in range(n):
            rows = arrs[k].shape[0] if gather else arrs[k].shape[0] // NDEV
            sent = ins[k].at[pl.ds(0, (NDEV - 1) * rows)] if not gather else outs[k].at[pl.ds(0, (NDEV - 1) * rows)]
            got = outs[k].at[pl.ds(0, (NDEV - 1) * rows)] if gather else outs[k].at[pl.ds(0, NDEV - 1)]
            pltpu.make_async_remote_copy(src_ref=sent, dst_ref=sent, send_sem=send_sems.at[k], recv_sem=recv_sems.at[k],
                                         device_id=me_pos, device_id_type=MESH).wait_send()
            pltpu.make_async_remote_copy(src_ref=got, dst_ref=got, send_sem=send_sems.at[k], recv_sem=recv_sems.at[k],
                                         device_id=me_pos, device_id_type=MESH).wait_recv()
            local[k].wait()

    return pl.pallas_call(
        body, out_shape=out_shape, in_specs=n * [hbm], out_specs=n * [hbm], name=name,
        scratch_shapes=[pltpu.SemaphoreType.DMA((n,)), pltpu.SemaphoreType.DMA((n,)), pltpu.SemaphoreType.DMA((n,))],
        compiler_params=pltpu.CompilerParams(has_side_effects=True),
    )(*arrs)


_HBM = pl.BlockSpec(memory_space=pltpu.HBM)
_SEM = pl.BlockSpec(memory_space=pltpu.SEMAPHORE)
_DATAFLOW = pltpu.SideEffectType.DATAFLOW_SIDE_EFFECTING


def _split_start(srcs, groups, gather, name):
    n = len(srcs)
    if gather:
        lands = [lax.empty((NDEV * a.shape[0], a.shape[1]), a.dtype) for a in srcs]
    else:
        lands = [lax.empty((NDEV, a.shape[0] // NDEV, a.shape[1]), a.dtype) for a in srcs]
    n_sem = 2 * len(groups)

    def body(*refs):
        src_refs, land_refs = refs[:n], refs[n:2 * n]
        sems = refs[2 * n:2 * n + n_sem]
        token, local_sems = refs[-2], refs[-1]
        (_, my), sibling, chips = _places()
        _, _, peers = _peers()
        targets = [sibling] + chips if gather else peers
        rows_of = lambda k: srcs[k].shape[0] if gather else srcs[k].shape[0] // NDEV
        local = []
        for k in range(n):
            own_src = src_refs[k] if gather else src_refs[k].at[pl.ds(my * rows_of(k), rows_of(k))]
            own_dst = land_refs[k].at[pl.ds(my * rows_of(k), rows_of(k))] if gather else land_refs[k].at[my]
            local.append(pltpu.make_async_copy(own_src, own_dst, local_sems.at[k]))
            local[-1].start()
        for cp in local:
            cp.wait()
        for g, members in enumerate(groups):
            for j, k in enumerate(members):
                rows = rows_of(k)
                for pos, idx in targets:
                    src = src_refs[k] if gather else src_refs[k].at[pl.ds(idx * rows, rows)]
                    dst = land_refs[k].at[pl.ds(my * rows, rows)] if gather else land_refs[k].at[my]
                    pltpu.make_async_remote_copy(src_ref=src, dst_ref=dst, send_sem=sems[2 * g].at[j],
                                                 recv_sem=sems[2 * g + 1].at[j], device_id=pos, device_id_type=MESH).start()
        token[...] = jnp.zeros_like(token)

    out_shape = []
    for members in groups:
        out_shape += [pltpu.SemaphoreType.DMA((len(members),)), pltpu.SemaphoreType.DMA((len(members),))]
    out_shape += [pltpu.HBM(a.shape, a.dtype) for a in srcs] + [pltpu.HBM(a.shape, a.dtype) for a in lands]
    out_shape.append(jax.ShapeDtypeStruct((8, 128), F32))
    res = pl.pallas_call(
        body, name=name, out_shape=tuple(out_shape), in_specs=2 * n * [_HBM],
        out_specs=tuple(n_sem * [_SEM] + 2 * n * [_HBM] + [pl.BlockSpec(memory_space=pltpu.VMEM)]),
        input_output_aliases={i: n_sem + i for i in range(2 * n)},
        scratch_shapes=[pltpu.SemaphoreType.DMA((n,))],
        compiler_params=pltpu.CompilerParams(has_side_effects=_DATAFLOW),
    )(*[pltpu.with_memory_space_constraint(a, pltpu.HBM) for a in list(srcs) + lands])
    sems = [(res[2 * g], res[2 * g + 1]) for g in range(len(groups))]
    return sems, list(res[n_sem:n_sem + n]), list(res[n_sem + n:n_sem + 2 * n]), res[-1]


def _wait_all(land_ref, blocks_per_dev, copies, send_sem, recv_sem, me_pos):
    part = land_ref.at[pl.ds(0, copies * blocks_per_dev)]
    pltpu.make_async_remote_copy(src_ref=part, dst_ref=part, send_sem=send_sem, recv_sem=recv_sem,
                                 device_id=me_pos, device_id_type=MESH).wait()


def _gather_forward(sems, srcs, lands, after, name):
    n = len(srcs)

    def body(*refs):
        land_refs = refs[n:2 * n]
        send_a, recv_a = refs[2 * n], refs[2 * n + 1]
        send_b, recv_b = refs[2 * n + 3], refs[2 * n + 4]
        token = refs[-1]
        (me_pos, _), sibling, chips = _places()
        for j in range(n):
            _wait_all(land_refs[j], lands[j].shape[0] // NDEV, 1 + OTHER_CHIPS, send_a.at[j], recv_a.at[j], me_pos)
        for j in range(n):
            rows = lands[j].shape[0] // NDEV
            for _, idx in chips:
                block = land_refs[j].at[pl.ds(idx * rows, rows)]
                pltpu.make_async_remote_copy(src_ref=block, dst_ref=block, send_sem=send_b.at[j], recv_sem=recv_b.at[j],
                                             device_id=sibling[0], device_id_type=MESH).start()
        token[...] = jnp.zeros_like(token)

    res = pl.pallas_call(
        body, name=name,
        out_shape=(pltpu.SemaphoreType.DMA((n,)), pltpu.SemaphoreType.DMA((n,)))
        + tuple(pltpu.HBM(a.shape, a.dtype) for a in list(srcs) + list(lands)) + (jax.ShapeDtypeStruct((8, 128), F32),),
        in_specs=2 * n * [_HBM] + [_SEM, _SEM, pl.BlockSpec(memory_space=pl.ANY)],
        out_specs=tuple([_SEM, _SEM] + 2 * n * [_HBM] + [pl.BlockSpec(memory_space=pltpu.VMEM)]),
        input_output_aliases={i: 2 + i for i in range(2 * n)},
        compiler_params=pltpu.CompilerParams(has_side_effects=_DATAFLOW),
    )(*srcs, *lands, sems[0], sems[1], after)
    return (res[0], res[1]), list(res[2:2 + n]), list(res[2 + n:2 + 2 * n]), res[-1]


def _split_wait(sems, srcs, lands, after, copies, name):
    n = len(srcs)

    def body(*refs):
        land_refs = refs[n:2 * n]
        send_sem, recv_sem = refs[2 * n], refs[2 * n + 1]
        (me_pos, _), _, _ = _places()
        for j in range(n):
            _wait_all(land_refs[j], lands[j].shape[0] // NDEV, copies, send_sem.at[j], recv_sem.at[j], me_pos)

    res = pl.pallas_call(
        body, name=name, out_shape=tuple(pltpu.HBM(a.shape, a.dtype) for a in list(srcs) + list(lands)),
        in_specs=2 * n * [_HBM] + [_SEM, _SEM, pl.BlockSpec(memory_space=pl.ANY)], out_specs=tuple(2 * n * [_HBM]),
        input_output_aliases={i: i for i in range(2 * n)},
        compiler_params=pltpu.CompilerParams(has_side_effects=_DATAFLOW),
    )(*srcs, *lands, sems[0], sems[1], after)
    return list(res[n:])


def _ffn_fwd(x, norms, mod, w):
    (pre_g, post_g), (shift, scale, gate), (wg_t, wu_t, wd) = norms, mod, w
    hn, g, u, a = _ffn_up(x, pre_g, scale, shift, wg_t, wu_t, "ffn_up")
    x_out, f = _mm_post(a, wd, x, post_g, gate, FFN_RES, "ffn_down")
    return x_out, (x, hn, g, u, a, f)


def _ffn_bwd(dx_out, saved, norms, mod, w):
    (pre_g, post_g), (_, scale, gate), (wg_t, wu_t, wd) = norms, mod, w
    x, hn, g, u, a, f = saved
    d_model = x.shape[1]
    df, dgate, dpost = _post_bwd(dx_out, f, post_g, gate, FFN_RES, "ffn_post_bwd")
    dg, du = _ffn_dgu(df, wd, g, u, "ffn_dgu")
    dwd = _mm([(a, df)], "tn", BF16, 256, d_model, "ffn_dw")
    dwg_t = _mm([(dg, hn)], "tn", BF16, 256, d_model, "ffn_dw")
    dwu_t = _mm([(du, hn)], "tn", BF16, 256, d_model, "ffn_dw")
    dhn = _mm([(dg, wg_t), (du, wu_t)], "nn", F32, TOKEN_TILE, d_model, "ffn_dhn")
    dx, dshift, dscale, dpre = _prenorm_bwd(dx_out, [dhn], x, pre_g, scale, "prenorm_bwd")
    return dx, (dpre, dpost), (dshift, dscale, dgate), (dwg_t, dwu_t, dwd)


def _mla_fwd(x, norms, mod, w, rope):
    (pre_g, post_g), (shift, scale, gate) = norms, mod
    w_in, q_norm, wq_t, kv_norm, wkv_t, wo = w
    hn, lat = _prenorm_mm(x, pre_g, scale, shift, w_in, "nn", F32, LAT_PAD, "mla_in")
    q, k, v, qn, kvn = _mla_qkv(lat, q_norm, kv_norm, wq_t, wkv_t, rope, "mla_qkv")
    o = _mla_attn_fwd(q, k, v, "mla_attn_fwd")
    x_out, f = _mm_post(o, wo, x, post_g, gate, 1.0, "mla_out")
    return x_out, (x, hn, lat, q, k, v, qn, kvn, o, f)


def _mla_bwd(dx_out, saved, norms, mod, w, rope):
    (pre_g, post_g), (_, scale, gate) = norms, mod
    w_in, q_norm, wq_t, kv_norm, wkv_t, wo = w
    x, hn, lat, q, k, v, qn, kvn, o, f = saved
    d_model = x.shape[1]
    df, dgate, dpost = _post_bwd(dx_out, f, post_g, gate, 1.0, "mix_post_bwd")
    d_o = _mm([(df, wo)], "nt", F32, TOKEN_TILE, wo.shape[0], "mla_do")
    dwo = _mm([(o, df)], "tn", BF16, TOKEN_TILE, d_model, "mla_dwo")
    dq, dk, dv = _mla_attn_bwd(q, k, v, d_o, "mla_attn_bwd")
    dqp, dkv, dlat, dq_norm, dkv_norm = _mla_qkv_bwd(dq, dk, dv, lat, q_norm, kv_norm, wq_t, wkv_t, rope, "mla_qkv_bwd")
    dwq_t = _mm([(dqp, qn)], "tn", BF16, TOKEN_TILE, Q_LORA, "mla_dwq")
    dwkv_t = _mm([(dkv, kvn)], "tn", BF16, TOKEN_TILE, KV_LORA, "mla_dwkv")
    dw_in = _mm([(hn, dlat)], "tn", BF16, TOKEN_TILE, LAT_PAD, "mla_dwin")
    dhn = _mm([(dlat, w_in)], "nt", F32, TOKEN_TILE, d_model, "mla_dhn")
    dx, dshift, dscale, dpre = _prenorm_bwd(dx_out, [dhn], x, pre_g, scale, "prenorm_bwd")
    return dx, (dpre, dpost), (dshift, dscale, dgate), (dw_in, dq_norm, dwq_t, dkv_norm, dwkv_t, dwo)


def _dil_fwd(x, norms, mod, w, bias):
    (pre_g, post_g), (shift, scale, gate), (w_in_t, wo) = norms, mod, w
    width = 3 * DIL_HEADS * DIL_HEAD_DIM
    hns, qkvs, outs, lses = [], [], [], []
    for g, (window, dilation) in enumerate(DIL_GROUPS):
        hn, qkv = _prenorm_mm(x, pre_g, scale, shift, w_in_t[g * width:(g + 1) * width], "nt", BF16, width,
                              "dil_in", perm=dilation)
        o, lse = _dil_attn_fwd(qkv, bias[g], dilation, window // dilation, "dil_attn_fwd")
        hns.append(hn), qkvs.append(qkv), outs.append(o), lses.append(lse)
    alphas, o_mix, o_mix_b = _dil_mix(lses, outs, "dil_mix")
    x_out, f = _mm_post(o_mix_b, wo, x, post_g, gate, 1.0, "dil_out")
    return x_out, (x, hns, qkvs, lses, alphas, o_mix, o_mix_b, f)


def _dil_bwd(dx_out, saved, norms, mod, w, bias):
    (pre_g, post_g), (_, scale, gate), (w_in_t, wo) = norms, mod, w
    x, hns, qkvs, lses, alphas, o_mix, o_mix_b, f = saved
    d_model = x.shape[1]
    inner = DIL_HEADS * DIL_HEAD_DIM
    df, dgate, dpost = _post_bwd(dx_out, f, post_g, gate, 1.0, "mix_post_bwd")
    d_o = _mm([(df, wo)], "nt", F32, TOKEN_TILE, inner, "dil_do")
    dwo = _mm([(o_mix_b, df)], "tn", BF16, TOKEN_TILE, d_model, "dil_dwo")
    dhns, dws, dbs = [], [], []
    for g, (window, dilation) in enumerate(DIL_GROUPS):
        grads = _dil_attn_bwd(qkvs[g], bias[g], d_o, o_mix, alphas[g], lses[g], dilation, window // dilation, "dil_attn_bwd")
        dbs.append(grads[3])
        w_parts = [w_in_t[(3 * g + j) * inner:(3 * g + j + 1) * inner] for j in range(3)]
        dhns.append(_mm(list(zip(grads[:3], w_parts)), "nn", F32, TOKEN_TILE, d_model, "dil_dhn", out_perm=dilation))
        dws += [_mm([(grads[j], hns[g])], "tn", BF16, TOKEN_TILE, d_model, "dil_dwin") for j in range(3)]
    dx, dshift, dscale, dpre = _prenorm_bwd(dx_out, dhns, x, pre_g, scale, "prenorm_bwd3")
    return dx, (dpre, dpost), (dshift, dscale, dgate), (jnp.concatenate(dws, axis=0), dwo), jnp.concatenate(dbs, axis=0)


def _pad_rows(a, rows):
    return jnp.pad(a, ((0, rows - a.shape[0]), (0, 0)))


def _lanes(a):
    flat = a.reshape(-1).astype(F32)
    rows = -(-flat.shape[0] // 1024) * 8
    return jnp.pad(flat, (0, rows * 128 - flat.shape[0])).reshape(rows, 128)


def kernel(x, c, norm_pre, norm_post, w_mod, b_mod, ffn_w_gate, ffn_w_up, ffn_w_down, mla_w_in, mla_q_norm, mla_w_q_up, mla_kv_norm, mla_w_kv_up, mla_w_o, dil_w_in, dil_w_o, rel_bias, loss_target, m_norm_pre, m_norm_post, m_w_mod, m_b_mod, m_ffn_w_gate, m_ffn_w_up, m_ffn_w_down, m_mla_w_in, m_mla_q_norm, m_mla_w_q_up, m_mla_kv_norm, m_mla_w_kv_up, m_mla_w_o, m_dil_w_in, m_dil_w_o, m_rel_bias, v_norm_pre, v_norm_post, v_w_mod, v_b_mod, v_ffn_w_gate, v_ffn_w_up, v_ffn_w_down, v_mla_w_in, v_mla_q_norm, v_mla_w_q_up, v_mla_kv_norm, v_mla_w_kv_up, v_mla_w_o, v_dil_w_in, v_dil_w_o, v_rel_bias):
    me = 4 * lax.axis_index("x") + 2 * lax.axis_index("y") + lax.axis_index("c")
    depth, n_sub, d_loc = norm_pre.shape
    d_model = x.shape[2]
    mod_loc_cols = w_mod.shape[2]
    x0, target = x[0], loss_target[0]

    small = jnp.concatenate([c.reshape(8, 128), _pad_rows(norm_pre.reshape(depth * n_sub, d_loc), 8),
                             _pad_rows(norm_post.reshape(depth * n_sub, d_loc), 8)], axis=0)
    small_all = _exchange([small], True, "gather_small")[0].reshape(NDEV, 24, 128)
    c_all = small_all[:, 0:8].reshape(NDEV, d_model)
    gains = lambda lo: jnp.transpose(small_all[:, lo:lo + depth * n_sub], (1, 0, 2)).reshape(depth, n_sub, 1, d_model)
    pre_full, post_full = gains(8), gains(16)

    b_loc = lax.dynamic_slice(b_mod, (0, me * mod_loc_cols), (depth, mod_loc_cols))
    mod_cols, silu_c = _mod_fwd(c_all, w_mod, b_loc, "mod_fwd")
    mod_all = _exchange([mod_cols.reshape(depth * NDEV, mod_loc_cols)], True, "gather_mod")[0]
    mod_all = mod_all.reshape(NDEV, depth, NDEV, mod_loc_cols)
    mod_mine = lax.dynamic_index_in_dim(mod_all, me, axis=2, keepdims=False)
    mod = jnp.transpose(mod_mine, (1, 0, 2)).reshape(depth, n_sub, 3, 1, d_model)

    bf_t = lambda a: a.astype(BF16).T
    ffn_ids = [(i, h) for i in range(depth) for h in range(2)]
    shards = []
    for i, h in ffn_ids:
        shards += [bf_t(ffn_w_gate[i, h]), bf_t(ffn_w_up[i, h]), ffn_w_down[i, h].astype(BF16)]
    shards += [mla_w_in[0].astype(BF16), bf_t(mla_w_q_up[0]), bf_t(mla_w_kv_up[0]), mla_w_o[0].astype(BF16),
               bf_t(dil_w_in[0]), dil_w_o[0].astype(BF16)]
    n_ffn = 3 * len(ffn_ids)
    members = {(0, 0): [0, 1, 2], (0, 1): [n_ffn, n_ffn + 1, n_ffn + 2, n_ffn + 3], (0, 2): [3, 4, 5],
               (1, 0): [6, 7, 8], (1, 1): [n_ffn + 4, n_ffn + 5], (1, 2): [9, 10, 11]}
    order = [(i, s) for i in range(depth) for s in range(n_sub)]
    g_sems, g_srcs, g_lands, _ = _split_start(shards, [members[k] for k in order], True, "gather_weights_start")

    forwarded = {}

    def forward(key, after):
        idx = members[key]
        forwarded[key] = _gather_forward(g_sems[order.index(key)], [g_srcs[k] for k in idx], [g_lands[k] for k in idx], after,
                                         "gather_forward_%d%d" % key)
        return forwarded[key][3]

    def weights_of(key, after):
        sems, srcs, lands, _ = forwarded[key]
        return _split_wait(sems, srcs, lands, after, OTHER_CHIPS, "gather_wait_%d%d" % key)

    lat_real = Q_LORA + KV_LORA
    qk = QK_NOPE + QK_ROPE

    def mla_weights(after):
        w_in, wq_t, wkv_t, wo = weights_of((0, 1), after)
        w_in_pad = jnp.concatenate([w_in[:, :lat_real], jnp.zeros((d_model, QK_NOPE), BF16), w_in[:, lat_real:],
                                    jnp.zeros((d_model, HEAD_PAD - QK_NOPE - QK_ROPE), BF16)], axis=1)
        wq_pad = jnp.pad(wq_t.reshape(MLA_HEADS, qk, Q_LORA), ((0, 0), (0, HEAD_PAD - qk), (0, 0)))
        wo_pad = jnp.pad(wo.reshape(MLA_HEADS, V_HEAD, d_model), ((0, 0), (HEAD_PAD - V_HEAD, 0), (0, 0)))
        return (w_in_pad, mla_q_norm, wq_pad.reshape(MLA_HEADS * HEAD_PAD, Q_LORA), mla_kv_norm, wkv_t,
                wo_pad.reshape(MLA_HEADS * HEAD_PAD, d_model))

    rope = _rope_tables()
    buckets = jnp.stack([_dil_buckets(dil) for _, dil in DIL_GROUPS])
    onehot = (buckets[..., None] == jnp.arange(N_BUCKETS)).astype(F32)
    bias = jnp.einsum("gqkb,bgh->ghqk", onehot, rel_bias.reshape(N_BUCKETS, len(DIL_GROUPS), DIL_HEADS),
                      precision=lax.Precision.HIGHEST)

    norms = lambda i, s: (pre_full[i, s], post_full[i, s])
    mods = lambda i, s: (mod[i, s, 0], mod[i, s, 1], mod[i, s, 2])
    saved, weights = {}, {}
    h = x0
    forward(order[0], h)
    for n, (i, s) in enumerate(order):
        weights[i, s] = mla_weights(h) if (s == 1 and i % 2 == 0) else tuple(weights_of((i, s), h))
        md = mods(i, s)
        if n + 1 < len(order):
            md = (md[0] + forward(order[n + 1], weights[i, s][0])[:1, :1], md[1], md[2])
        if s != 1:
            h, saved[i, s] = _ffn_fwd(h, norms(i, s), md, weights[i, s])
        elif i % 2 == 0:
            h, saved[i, s] = _mla_fwd(h, norms(i, s), md, weights[i, s], rope)
        else:
            h, saved[i, s] = _dil_fwd(h, norms(i, s), md, weights[i, s], bias)
    dh, loss_parts = _loss_grad(h, target, "loss")

    dnorm, dmod, sent = {}, {}, {}
    token = jnp.zeros((8, 128), F32)
    for i, s in reversed(order):
        md = mods(i, s)
        md = (md[0], md[1], md[2] + token[:1, :1])
        if s != 1:
            dh, dnorm[i, s], dmod[i, s], dws = _ffn_bwd(dh, saved[i, s], norms(i, s), md, weights[i, s])
        elif i % 2 == 0:
            dh, dnorm[i, s], dmod[i, s], dmla = _mla_bwd(dh, saved[i, s], norms(i, s), md, weights[i, s], rope)
            dw_in_pad, dq_norm, dwq_pad, dkv_norm, dwkv_t, dwo_pad = dmla
            dw_in = jnp.concatenate([dw_in_pad[:, :lat_real], dw_in_pad[:, lat_real + QK_NOPE:lat_real + qk]], axis=1)
            dwq_t = dwq_pad.reshape(MLA_HEADS, HEAD_PAD, Q_LORA)[:, :qk].reshape(MLA_HEADS * qk, Q_LORA)
            dwo = dwo_pad.reshape(MLA_HEADS, HEAD_PAD, d_model)[:, HEAD_PAD - V_HEAD:].reshape(MLA_HEADS * V_HEAD, d_model)
            dws = (dw_in, dwq_t, dwkv_t, dwo)
        else:
            dh, dnorm[i, s], dmod[i, s], dws, dbias = _dil_bwd(dh, saved[i, s], norms(i, s), md, weights[i, s], bias)
        sent[i, s] = _split_start(list(dws), [list(range(len(dws)))], False, "scatter_start_%d%d" % (i, s))
        token = sent[i, s][3]
    grad_x = dh[None]

    mine = {}
    for key in order:
        sems, srcs, lands, _ = sent[key]
        parts = _split_wait(sems[0], srcs, lands, dh, NDEV - 1, "scatter_wait_%d%d" % key)
        for k, p in zip(members[key], parts):
            mine[k] = _sum_parts(p, "sum_parts")
    g_gate = jnp.stack([mine[3 * n].T for n in range(len(ffn_ids))]).reshape(ffn_w_gate.shape)
    g_up = jnp.stack([mine[3 * n + 1].T for n in range(len(ffn_ids))]).reshape(ffn_w_up.shape)
    g_down = jnp.stack([mine[3 * n + 2] for n in range(len(ffn_ids))]).reshape(ffn_w_down.shape)
    g_mla_in, g_q_up, g_kv_up, g_mla_o, g_dil_in, g_dil_o = (mine[k] for k in range(n_ffn, n_ffn + 6))
    g_mla_in, g_q_up, g_kv_up, g_mla_o = g_mla_in[None], g_q_up.T[None], g_kv_up.T[None], g_mla_o[None]
    g_dil_in, g_dil_o = g_dil_in.T[None], g_dil_o[None]

    dmod_mine = jnp.concatenate([jnp.concatenate(dmod[i, s], axis=0) for i in range(depth) for s in range(n_sub)], axis=0)
    dpre_mine = jnp.concatenate([dnorm[i, s][0] for i in range(depth) for s in range(n_sub)], axis=0)
    dpost_mine = jnp.concatenate([dnorm[i, s][1] for i in range(depth) for s in range(n_sub)], axis=0)
    dbias_tab = _bias_reduce(dbias, buckets, "bias_reduce")[:, 0, :N_BUCKETS].T
    pieces = [dmod_mine, dpre_mine, dpost_mine, dq_norm, dkv_norm, dbias_tab, jnp.sum(loss_parts).reshape(1, 1)]
    packed = [_lanes(p) for p in pieces]
    offs = [0]
    for p in packed:
        offs.append(offs[-1] + p.shape[0])
    everyone = _exchange([jnp.concatenate(packed, axis=0)], True, "gather_small_grads")[0].reshape(NDEV, offs[-1], 128)
    total = _sum_parts(everyone, "sum_small")
    take = lambda n, shape: total[offs[n]:offs[n + 1]].reshape(-1)[:math.prod(shape)].reshape(shape)
    g_b_mod = take(0, b_mod.shape)
    col0 = me * d_loc
    g_norm_pre = lax.dynamic_slice(take(1, (depth, n_sub, d_model)), (0, 0, col0), norm_pre.shape)
    g_norm_post = lax.dynamic_slice(take(2, (depth, n_sub, d_model)), (0, 0, col0), norm_post.shape)
    g_q_norm, g_kv_norm = take(3, mla_q_norm.shape), take(4, mla_kv_norm.shape)
    g_rel_bias = take(5, rel_bias.shape)
    loss = take(6, ())

    dmod_all = everyone[:, offs[0]:offs[1]].reshape(NDEV, depth, NDEV * mod_loc_cols)
    dmod_cols = lax.dynamic_slice(dmod_all, (0, 0, me * mod_loc_cols), (NDEV, depth, mod_loc_cols))
    silu_t = jnp.pad(silu_c.T, ((0, 0), (0, HEAD_PAD - NDEV)))
    g_w_mod = jnp.stack([_mm([(silu_t, jnp.pad(dmod_cols[:, i], ((0, HEAD_PAD - NDEV), (0, 0))))], "nn", F32, TOKEN_TILE,
                             mod_loc_cols, "mod_bwd") for i in range(depth)])

    ws = (norm_pre, norm_post, w_mod, b_mod, ffn_w_gate, ffn_w_up, ffn_w_down, mla_w_in, mla_q_norm, mla_w_q_up, mla_kv_norm,
          mla_w_kv_up, mla_w_o, dil_w_in, dil_w_o, rel_bias)
    gs = (g_norm_pre, g_norm_post, g_w_mod, g_b_mod, g_gate, g_up, g_down, g_mla_in, g_q_norm, g_q_up, g_kv_norm, g_kv_up,
          g_mla_o, g_dil_in, g_dil_o, g_rel_bias)
    ms = (m_norm_pre, m_norm_post, m_w_mod, m_b_mod, m_ffn_w_gate, m_ffn_w_up, m_ffn_w_down, m_mla_w_in, m_mla_q_norm,
          m_mla_w_q_up, m_mla_kv_norm, m_mla_w_kv_up, m_mla_w_o, m_dil_w_in, m_dil_w_o, m_rel_bias)
    vs = (v_norm_pre, v_norm_post, v_w_mod, v_b_mod, v_ffn_w_gate, v_ffn_w_up, v_ffn_w_down, v_mla_w_in, v_mla_q_norm,
          v_mla_w_q_up, v_mla_kv_norm, v_mla_w_kv_up, v_mla_w_o, v_dil_w_in, v_dil_w_o, v_rel_bias)
    stepped = [_adamw(w, g, m, v, "adamw") for w, g, m, v in zip(ws, gs, ms, vs)]
    deltas, new_m, new_v = zip(*stepped)
    return (loss, grad_x, *gs, *deltas, *new_m, *new_v)
```

```python
import math

import jax
import jax.numpy as jnp
from jax import lax
from jax.experimental import pallas as pl
from jax.experimental.pallas import tpu as pltpu

F32 = jnp.float32
BF16 = jnp.bfloat16
MESH = pl.DeviceIdType.MESH

NDEV = 8
OTHER_CHIPS = 3
D_MODEL = 1024
SEQ = 2048
D_FF = 2816
EPS = 1e-6
FFN_RES = 0.5

MLA_HEADS = 16
Q_LORA = 384
KV_LORA = 256
QK_NOPE = 64
QK_ROPE = 32
V_HEAD = 64
ROPE_THETA = 10000.0
HEAD_PAD = 128
LAT_PAD = Q_LORA + KV_LORA + HEAD_PAD
MLA_SCALE = (QK_NOPE + QK_ROPE) ** -0.5

DIL_GROUPS = ((128, 1), (512, 4), (2048, 16))
DIL_HEADS = 16
DIL_HEAD_DIM = 64
DIL_BLOCK = 128
DIL_PAIRS = DIL_HEADS // 2
DIL_SCALE = DIL_HEAD_DIM ** -0.5
N_BUCKETS = 32
MAX_DISTANCE = 2048

ADAM_LR = 0.001
ADAM_B1 = 0.9
ADAM_B2 = 0.999
ADAM_EPS = 1e-08
ADAM_WD = 0.01
ADAM_STEP = 10

V7X_VMEM_BYTES = 64 * 2**20
VMEM_RESERVE = 10 * 2**20
TOKEN_TILE = 512


def _nbytes(shape, dtype):
    return math.prod(shape) * jnp.dtype(dtype).itemsize


def _params(semantics, blocks, extra=0):
    need = 2 * sum(_nbytes(s, d) for s, d in blocks) + extra + VMEM_RESERVE
    return pltpu.CompilerParams(dimension_semantics=semantics,
                                vmem_limit_bytes=int(min(need, V7X_VMEM_BYTES - VMEM_RESERVE)))


def _pcall(body, out_shape, **kw):
    call = pl.pallas_call(body, out_shape=jax.tree.map(lambda s: pltpu.HBM(s.shape, s.dtype), out_shape), **kw)
    return lambda *args: call(*[pltpu.with_memory_space_constraint(a, pltpu.HBM) for a in args])


def _dot_nn(a, b):
    return lax.dot_general(a, b, (((1,), (0,)), ((), ())), preferred_element_type=F32)


def _dot_nt(a, b):
    return lax.dot_general(a, b, (((1,), (1,)), ((), ())), preferred_element_type=F32)


def _dot_tn(a, b):
    return lax.dot_general(a, b, (((0,), (0,)), ((), ())), preferred_element_type=F32)


_DOTS = {"nn": _dot_nn, "nt": _dot_nt, "tn": _dot_tn}


def _rstd(v):
    return lax.rsqrt(jnp.mean(v * v, axis=-1, keepdims=True) + EPS)


def _rms_bwd(v, r, t):
    return r * t - v * (r * r * r) * jnp.mean(t * v, axis=-1, keepdims=True)


def _mm(pairs, mode, out_dtype, tm, tn, name, out_perm=1):
    a0, b0 = pairs[0]
    m_dim = a0.shape[1] if mode == "tn" else a0.shape[0]
    n_dim = b0.shape[0] if mode == "nt" else b0.shape[1]
    tm, tn = min(tm, m_dim // out_perm), min(tn, n_dim)
    assert m_dim % tm == 0 and n_dim % tn == 0, (name, m_dim, n_dim, tm, tn)
    dot = _DOTS[mode]
    npairs = len(pairs)

    def body(*refs):
        acc = None
        for p in range(npairs):
            d = dot(refs[2 * p][...].astype(BF16), refs[2 * p + 1][...].astype(BF16))
            acc = d if acc is None else acc + d
        refs[-1][...] = acc.astype(out_dtype)

    in_specs, blocks, flat = [], [], []
    for a, b in pairs:
        if mode == "nn":
            k = a.shape[1]
            sa, sb = ((tm, k), lambda i, j: (i, 0)), ((k, tn), lambda i, j: (0, j))
        elif mode == "nt":
            k = a.shape[1]
            sa, sb = ((tm, k), lambda i, j: (i, 0)), ((tn, k), lambda i, j: (j, 0))
        else:
            k = a.shape[0]
            sa, sb = ((k, tm), lambda i, j: (0, i)), ((k, tn), lambda i, j: (0, j))
        in_specs += [pl.BlockSpec(*sa), pl.BlockSpec(*sb)]
        blocks += [(sa[0], a.dtype), (sb[0], b.dtype)]
        flat += [a, b]
    if out_perm == 1:
        out_shape = (m_dim, n_dim)
        out_spec = pl.BlockSpec((tm, tn), lambda i, j: (i, j))
    else:
        rows = m_dim // out_perm
        assert tn == n_dim and rows % tm == 0, (name, rows, tm)
        nb = rows // tm
        out_shape = (rows, out_perm * n_dim)
        out_spec = pl.BlockSpec((tm, n_dim), lambda i, j: (i % nb, i // nb))
    blocks.append(((tm, tn), out_dtype))
    res = _pcall(
        body, out_shape=jax.ShapeDtypeStruct(out_shape, out_dtype), grid=(m_dim // tm, n_dim // tn),
        in_specs=in_specs, out_specs=out_spec, name=name,
        compiler_params=_params(("parallel", "parallel"), blocks, extra=2 * tm * tn * 4),
    )(*flat)
    return res.reshape(m_dim, n_dim)


def _prenorm_mm(x, pre_g, scale, shift, w, w_mode, out_dtype, tn, name, perm=1):
    s_dim, d_dim = x.shape
    n_dim = w.shape[0] if w_mode == "nt" else w.shape[1]
    rows = s_dim // perm
    tm = min(TOKEN_TILE, rows)
    nb = rows // tm
    tn = min(tn, n_dim)
    assert n_dim % tn == 0
    dot = _DOTS[w_mode]

    def body(x_ref, g_ref, sc_ref, sh_ref, w_ref, hn_ref, o_ref):
        @pl.when(pl.program_id(1) == 0)
        def _():
            xf = x_ref[...]
            hn = (xf * _rstd(xf) * g_ref[...]) * (1.0 + sc_ref[...]) + sh_ref[...]
            hn_ref[...] = hn.astype(BF16)

        o_ref[...] = dot(hn_ref[...], w_ref[...]).astype(out_dtype)

    vec = pl.BlockSpec((1, d_dim), lambda i, j: (0, 0))
    w_block = (tn, d_dim) if w_mode == "nt" else (d_dim, tn)
    w_spec = pl.BlockSpec(w_block, (lambda i, j: (j, 0)) if w_mode == "nt" else (lambda i, j: (0, j)))
    hn, out = _pcall(
        body,
        out_shape=(jax.ShapeDtypeStruct((s_dim, d_dim), BF16), jax.ShapeDtypeStruct((s_dim, n_dim), out_dtype)),
        grid=(s_dim // tm, n_dim // tn),
        in_specs=[pl.BlockSpec((tm, d_dim), lambda i, j: (i % nb, i // nb)), vec, vec, vec, w_spec],
        out_specs=(pl.BlockSpec((tm, d_dim), lambda i, j: (i, 0)), pl.BlockSpec((tm, tn), lambda i, j: (i, j))),
        name=name,
        compiler_params=_params(("parallel", "arbitrary"),
                                [((tm, d_dim), F32), (w_block, BF16), ((tm, d_dim), BF16), ((tm, tn), out_dtype)],
                                extra=3 * tm * d_dim * 4 + tm * tn * 4),
    )(x.reshape(rows, perm * d_dim), pre_g, scale, shift, w)
    return hn, out


def _ffn_up(x, pre_g, scale, shift, wg_t, wu_t, name):
    s_dim, d_dim = x.shape
    f_dim = wg_t.shape[0]
    tm, tn = TOKEN_TILE, f_dim // 2

    def body(x_ref, g_ref, sc_ref, sh_ref, wg_ref, wu_ref, hn_ref, go_ref, uo_ref, a_ref):
        @pl.when(pl.program_id(1) == 0)
        def _():
            xf = x_ref[...]
            hn = (xf * _rstd(xf) * g_ref[...]) * (1.0 + sc_ref[...]) + sh_ref[...]
            hn_ref[...] = hn.astype(BF16)

        hn = hn_ref[...]
        g = _dot_nt(hn, wg_ref[...])
        u = _dot_nt(hn, wu_ref[...])
        go_ref[...] = g.astype(BF16)
        uo_ref[...] = u.astype(BF16)
        a_ref[...] = (g * jax.nn.sigmoid(g) * u).astype(BF16)

    vec = pl.BlockSpec((1, d_dim), lambda i, j: (0, 0))
    w_spec = pl.BlockSpec((tn, d_dim), lambda i, j: (j, 0))
    act = pl.BlockSpec((tm, tn), lambda i, j: (i, j))
    act_shape = jax.ShapeDtypeStruct((s_dim, f_dim), BF16)
    return _pcall(
        body,
        out_shape=(jax.ShapeDtypeStruct((s_dim, d_dim), BF16), act_shape, act_shape, act_shape),
        grid=(s_dim // tm, f_dim // tn),
        in_specs=[pl.BlockSpec((tm, d_dim), lambda i, j: (i, 0)), vec, vec, vec, w_spec, w_spec],
        out_specs=(pl.BlockSpec((tm, d_dim), lambda i, j: (i, 0)), act, act, act),
        name=name,
        compiler_params=_params(("parallel", "arbitrary"),
                                [((tm, d_dim), F32), ((tn, d_dim), BF16), ((tn, d_dim), BF16), ((tm, d_dim), BF16)]
                                + 3 * [((tm, tn), BF16)], extra=3 * tm * d_dim * 4 + 4 * tm * tn * 4),
    )(x, pre_g, scale, shift, wg_t, wu_t)


def _mm_post(a, w, x, post_g, gate, res_w, name):
    s_dim, k_dim = a.shape
    d_dim = w.shape[1]
    tm = TOKEN_TILE

    def body(a_ref, w_ref, x_ref, pg_ref, gt_ref, xo_ref, f_ref):
        f = _dot_nn(a_ref[...], w_ref[...])
        y = f * _rstd(f) * pg_ref[...]
        f_ref[...] = f
        xo_ref[...] = x_ref[...] + (res_w * gt_ref[...]) * y

    vec = pl.BlockSpec((1, d_dim), lambda i: (0, 0))
    row = pl.BlockSpec((tm, d_dim), lambda i: (i, 0))
    out = jax.ShapeDtypeStruct((s_dim, d_dim), F32)
    return _pcall(
        body, out_shape=(out, out), grid=(s_dim // tm,),
        in_specs=[pl.BlockSpec((tm, k_dim), lambda i: (i, 0)), pl.BlockSpec((k_dim, d_dim), lambda i: (0, 0)), row, vec, vec],
        out_specs=(row, row), name=name,
        compiler_params=_params(("parallel",), [((tm, k_dim), BF16), ((k_dim, d_dim), BF16)] + 3 * [((tm, d_dim), F32)],
                                extra=3 * tm * d_dim * 4),
    )(a, w, x, post_g, gate)


def _post_bwd(dx_out, f, post_g, gate, res_w, name):
    s_dim, d_dim = f.shape
    tm = TOKEN_TILE

    def body(dx_ref, f_ref, pg_ref, gt_ref, df_ref, dgate_ref, dpost_ref):
        @pl.when(pl.program_id(0) == 0)
        def _():
            dgate_ref[...] = jnp.zeros_like(dgate_ref)
            dpost_ref[...] = jnp.zeros_like(dpost_ref)

        dx, fv = dx_ref[...], f_ref[...]
        r = _rstd(fv)
        fr = fv * r
        dgate_ref[...] += res_w * jnp.sum(dx * (fr * pg_ref[...]), axis=0, keepdims=True)
        dy = (res_w * gt_ref[...]) * dx
        dpost_ref[...] += jnp.sum(dy * fr, axis=0, keepdims=True)
        df_ref[...] = _rms_bwd(fv, r, dy * pg_ref[...]).astype(BF16)

    vec = pl.BlockSpec((1, d_dim), lambda i: (0, 0))
    row = pl.BlockSpec((tm, d_dim), lambda i: (i, 0))
    vshape = jax.ShapeDtypeStruct((1, d_dim), F32)
    return _pcall(
        body, out_shape=(jax.ShapeDtypeStruct((s_dim, d_dim), BF16), vshape, vshape), grid=(s_dim // tm,),
        in_specs=[row, row, vec, vec], out_specs=(row, vec, vec), name=name,
        compiler_params=_params(("arbitrary",), 3 * [((tm, d_dim), F32)], extra=6 * tm * d_dim * 4),
    )(dx_out, f, post_g, gate)


def _prenorm_bwd(dx_out, dhns, x, pre_g, scale, name):
    s_dim, d_dim = x.shape
    tm = TOKEN_TILE
    n_in = len(dhns)

    def body(*refs):
        dx_ref, x_ref, pg_ref, sc_ref = refs[n_in + 0], refs[n_in + 1], refs[n_in + 2], refs[n_in + 3]
        dxo_ref, dsh_ref, dsc_ref, dpg_ref = refs[n_in + 4:]

        @pl.when(pl.program_id(0) == 0)
        def _():
            dsh_ref[...] = jnp.zeros_like(dsh_ref)
            dsc_ref[...] = jnp.zeros_like(dsc_ref)
            dpg_ref[...] = jnp.zeros_like(dpg_ref)

        dhn = refs[0][...]
        for k in range(1, n_in):
            dhn = dhn + refs[k][...]
        xv = x_ref[...]
        r = _rstd(xv)
        xr = xv * r
        dsh_ref[...] += jnp.sum(dhn, axis=0, keepdims=True)
        dsc_ref[...] += jnp.sum(dhn * (xr * pg_ref[...]), axis=0, keepdims=True)
        dn = dhn * (1.0 + sc_ref[...])
        dpg_ref[...] += jnp.sum(dn * xr, axis=0, keepdims=True)
        dxo_ref[...] = dx_ref[...] + _rms_bwd(xv, r, dn * pg_ref[...])

    vec = pl.BlockSpec((1, d_dim), lambda i: (0, 0))
    row = pl.BlockSpec((tm, d_dim), lambda i: (i, 0))
    vshape = jax.ShapeDtypeStruct((1, d_dim), F32)
    return _pcall(
        body, out_shape=(jax.ShapeDtypeStruct((s_dim, d_dim), F32), vshape, vshape, vshape), grid=(s_dim // tm,),
        in_specs=n_in * [row] + [row, row, vec, vec], out_specs=(row, vec, vec, vec), name=name,
        compiler_params=_params(("arbitrary",), (n_in + 3) * [((tm, d_dim), F32)], extra=6 * tm * d_dim * 4),
    )(*dhns, dx_out, x, pre_g, scale)


def _ffn_dgu(df, wd, g, u, name):
    s_dim, d_dim = df.shape
    f_dim = wd.shape[0]
    tm, tn = TOKEN_TILE, f_dim // 2

    def body(df_ref, wd_ref, g_ref, u_ref, dg_ref, du_ref):
        da = _dot_nt(df_ref[...], wd_ref[...])
        gv, uv = g_ref[...].astype(F32), u_ref[...].astype(F32)
        sg = jax.nn.sigmoid(gv)
        du_ref[...] = (da * (gv * sg)).astype(BF16)
        dg_ref[...] = (da * uv * (sg * (1.0 + gv * (1.0 - sg)))).astype(BF16)

    act = pl.BlockSpec((tm, tn), lambda i, j: (i, j))
    act_shape = jax.ShapeDtypeStruct((s_dim, f_dim), BF16)
    return _pcall(
        body, out_shape=(act_shape, act_shape), grid=(s_dim // tm, f_dim // tn),
        in_specs=[pl.BlockSpec((tm, d_dim), lambda i, j: (i, 0)), pl.BlockSpec((tn, d_dim), lambda i, j: (j, 0)), act, act],
        out_specs=(act, act), name=name,
        compiler_params=_params(("parallel", "parallel"), [((tm, d_dim), BF16), ((tn, d_dim), BF16)] + 4 * [((tm, tn), BF16)],
                                extra=6 * tm * tn * 4),
    )(df, wd, g, u)


def _rope_tables():
    half = QK_ROPE // 2
    freqs = ROPE_THETA ** (-jnp.arange(half, dtype=F32) / half)
    ang = jnp.arange(SEQ, dtype=F32)[:, None] * freqs[None, :]
    cos, sin = jnp.cos(ang), jnp.sin(ang)
    ones = jnp.ones((SEQ, QK_NOPE), F32)
    zeros = jnp.zeros((SEQ, QK_NOPE), F32)
    pad1 = jnp.ones((SEQ, HEAD_PAD - QK_NOPE - QK_ROPE), F32)
    pad0 = jnp.zeros((SEQ, HEAD_PAD - QK_NOPE - QK_ROPE), F32)
    zh = jnp.zeros((SEQ, half), F32)
    c = jnp.concatenate([ones, cos, cos, pad1], axis=1)
    s1 = jnp.concatenate([zeros, -sin, zh, pad0], axis=1)
    s2 = jnp.concatenate([zeros, zh, sin, pad0], axis=1)
    return c, s1, s2


def _rope(v, c, s1, s2):
    half = QK_ROPE // 2
    return v * c + pltpu.roll(v, HEAD_PAD - half, 1) * s1 + pltpu.roll(v, half, 1) * s2


def _rope_t(dv, c, s1, s2):
    half = QK_ROPE // 2
    return dv * c + pltpu.roll(dv * s1, half, 1) + pltpu.roll(dv * s2, HEAD_PAD - half, 1)


def _mla_qkv(lat, q_norm, kv_norm, wq_t, wkv_t, rope, name):
    s_dim = lat.shape[0]
    width = MLA_HEADS * HEAD_PAD
    tm = 256

    def body(lat_ref, qg_ref, kg_ref, wq_ref, wkv_ref, c_ref, s1_ref, s2_ref, q_ref, k_ref, v_ref, qn_ref, kvn_ref):
        cq = lat_ref[:, :Q_LORA]
        ckv = lat_ref[:, Q_LORA:Q_LORA + KV_LORA]
        kr = lat_ref[:, Q_LORA + KV_LORA:]
        c, s1, s2 = c_ref[...], s1_ref[...], s2_ref[...]
        qn = (cq * _rstd(cq) * qg_ref[...]).astype(BF16)
        kvn = (ckv * _rstd(ckv) * kg_ref[...]).astype(BF16)
        qn_ref[...] = qn
        kvn_ref[...] = kvn
        q = _dot_nt(qn, wq_ref[...])
        kv = _dot_nt(kvn, wkv_ref[...])
        krr = _rope(kr, c, s1, s2)
        low = lax.broadcasted_iota(jnp.int32, (tm, HEAD_PAD), 1) < QK_NOPE
        for h in range(MLA_HEADS):
            sl = slice(h * HEAD_PAD, (h + 1) * HEAD_PAD)
            q_ref[:, sl] = _rope(q[:, sl], c, s1, s2).astype(BF16)
            kvh = kv[:, sl]
            k_ref[:, sl] = (jnp.where(low, kvh, 0.0) + krr).astype(BF16)
            v_ref[:, sl] = jnp.where(low, 0.0, kvh).astype(BF16)

    row = lambda n: pl.BlockSpec((tm, n), lambda i: (i, 0))
    full = lambda a: pl.BlockSpec(a.shape, lambda i: (0, 0))
    wide = jax.ShapeDtypeStruct((s_dim, width), BF16)
    return _pcall(
        body,
        out_shape=(wide, wide, wide, jax.ShapeDtypeStruct((s_dim, Q_LORA), BF16), jax.ShapeDtypeStruct((s_dim, KV_LORA), BF16)),
        grid=(s_dim // tm,),
        in_specs=[row(LAT_PAD), full(q_norm), full(kv_norm), full(wq_t), full(wkv_t), row(HEAD_PAD), row(HEAD_PAD), row(HEAD_PAD)],
        out_specs=(row(width), row(width), row(width), row(Q_LORA), row(KV_LORA)), name=name,
        compiler_params=_params(("parallel",), [((tm, LAT_PAD), F32), (wq_t.shape, BF16), (wkv_t.shape, BF16)]
                                + 3 * [((tm, width), BF16)], extra=4 * tm * width * 4),
    )(lat, q_norm, kv_norm, wq_t, wkv_t, *rope)


def _mla_probs(q, k, t, tq):
    s = _dot_nt(q, k) * MLA_SCALE
    rows = lax.broadcasted_iota(jnp.int32, s.shape, 0) + t * tq
    cols = lax.broadcasted_iota(jnp.int32, s.shape, 1)
    s = jnp.where(cols <= rows, s, -jnp.inf)
    e = jnp.exp(s - jnp.max(s, axis=-1, keepdims=True))
    return e / jnp.sum(e, axis=-1, keepdims=True)


def _mla_attn_fwd(q, k, v, name):
    s_dim = q.shape[0]
    tq = 512

    def body(q_ref, k_ref, v_ref, o_ref):
        for t in range(s_dim // tq):
            kt = (t + 1) * tq
            p = _mla_probs(q_ref[t * tq:kt, :], k_ref[:kt, :], t, tq)
            o_ref[t * tq:kt, :] = _dot_nn(p.astype(BF16), v_ref[:kt, :]).astype(BF16)

    head = pl.BlockSpec((s_dim, HEAD_PAD), lambda h: (0, h))
    return _pcall(
        body, out_shape=jax.ShapeDtypeStruct(q.shape, BF16), grid=(MLA_HEADS,),
        in_specs=[head, head, head], out_specs=head, name=name,
        compiler_params=_params(("parallel",), 4 * [((s_dim, HEAD_PAD), BF16)], extra=4 * tq * s_dim * 4),
    )(q, k, v)


def _mla_attn_bwd(q, k, v, d_o, name):
    s_dim = q.shape[0]
    tq = 512

    def body(q_ref, k_ref, v_ref, do_ref, dq_ref, dk_ref, dv_ref):
        dk_ref[...] = jnp.zeros_like(dk_ref)
        dv_ref[...] = jnp.zeros_like(dv_ref)
        for t in range(s_dim // tq):
            kt = (t + 1) * tq
            qt = q_ref[t * tq:kt, :]
            dot = do_ref[t * tq:kt, :].astype(BF16)
            p = _mla_probs(qt, k_ref[:kt, :], t, tq)
            dp = _dot_nt(dot, v_ref[:kt, :])
            ds = p * (dp - jnp.sum(p * dp, axis=-1, keepdims=True))
            dsb = (ds * MLA_SCALE).astype(BF16)
            dq_ref[t * tq:kt, :] = _dot_nn(dsb, k_ref[:kt, :])
            dk_ref[:kt, :] += _dot_tn(dsb, qt)
            dv_ref[:kt, :] += _dot_tn(p.astype(BF16), dot)

    head = pl.BlockSpec((s_dim, HEAD_PAD), lambda h: (0, h))
    out = jax.ShapeDtypeStruct(q.shape, F32)
    return _pcall(
        body, out_shape=(out, out, out), grid=(MLA_HEADS,),
        in_specs=[head, head, head, head], out_specs=(head, head, head), name=name,
        compiler_params=_params(("parallel",), 3 * [((s_dim, HEAD_PAD), BF16)] + 4 * [((s_dim, HEAD_PAD), F32)],
                                extra=6 * tq * s_dim * 4),
    )(q, k, v, d_o)


def _mla_qkv_bwd(dq, dk, dv, lat, q_norm, kv_norm, wq_t, wkv_t, rope, name):
    s_dim = lat.shape[0]
    width = MLA_HEADS * HEAD_PAD
    tm = 256

    def body(dq_ref, dk_ref, dv_ref, lat_ref, qg_ref, kg_ref, wq_ref, wkv_ref, c_ref, s1_ref, s2_ref,
             dqp_ref, dkv_ref, dlat_ref, dqg_ref, dkg_ref):
        @pl.when(pl.program_id(0) == 0)
        def _():
            dqg_ref[...] = jnp.zeros_like(dqg_ref)
            dkg_ref[...] = jnp.zeros_like(dkg_ref)

        c, s1, s2 = c_ref[...], s1_ref[...], s2_ref[...]
        lane = lax.broadcasted_iota(jnp.int32, (tm, HEAD_PAD), 1)
        low = lane < QK_NOPE
        rot = (lane >= QK_NOPE) & (lane < QK_NOPE + QK_ROPE)
        dkrr = jnp.zeros((tm, HEAD_PAD), F32)
        for h in range(MLA_HEADS):
            sl = slice(h * HEAD_PAD, (h + 1) * HEAD_PAD)
            dqp_ref[:, sl] = _rope_t(dq_ref[:, sl], c, s1, s2).astype(BF16)
            dkh = dk_ref[:, sl]
            dkv_ref[:, sl] = jnp.where(low, dkh, dv_ref[:, sl]).astype(BF16)
            dkrr = dkrr + jnp.where(rot, dkh, 0.0)
        dqn = _dot_nn(dqp_ref[...], wq_ref[...])
        dkvn = _dot_nn(dkv_ref[...], wkv_ref[...])
        cq = lat_ref[:, :Q_LORA]
        ckv = lat_ref[:, Q_LORA:Q_LORA + KV_LORA]
        rq, rkv = _rstd(cq), _rstd(ckv)
        dqg_ref[...] += jnp.sum(dqn * cq * rq, axis=0, keepdims=True)
        dkg_ref[...] += jnp.sum(dkvn * ckv * rkv, axis=0, keepdims=True)
        dlat_ref[:, :Q_LORA] = _rms_bwd(cq, rq, dqn * qg_ref[...])
        dlat_ref[:, Q_LORA:Q_LORA + KV_LORA] = _rms_bwd(ckv, rkv, dkvn * kg_ref[...])
        dlat_ref[:, Q_LORA + KV_LORA:] = _rope_t(dkrr, c, s1, s2)

    row = lambda n: pl.BlockSpec((tm, n), lambda i: (i, 0))
    full = lambda a: pl.BlockSpec(a.shape, lambda i: (0, 0))
    wide = jax.ShapeDtypeStruct((s_dim, width), BF16)
    return _pcall(
        body,
        out_shape=(wide, wide, jax.ShapeDtypeStruct((s_dim, LAT_PAD), F32),
                   jax.ShapeDtypeStruct(q_norm.shape, F32), jax.ShapeDtypeStruct(kv_norm.shape, F32)),
        grid=(s_dim // tm,),
        in_specs=[row(width), row(width), row(width), row(LAT_PAD), full(q_norm), full(kv_norm), full(wq_t), full(wkv_t),
                  row(HEAD_PAD), row(HEAD_PAD), row(HEAD_PAD)],
        out_specs=(row(width), row(width), row(LAT_PAD), full(q_norm), full(kv_norm)), name=name,
        compiler_params=_params(("arbitrary",), 3 * [((tm, width), F32)] + [((tm, LAT_PAD), F32), (wq_t.shape, BF16),
                                                                           (wkv_t.shape, BF16)] + 2 * [((tm, width), BF16)],
                                extra=2 * tm * width * 4),
    )(dq, dk, dv, lat, q_norm, kv_norm, wq_t, wkv_t, *rope)


def _t5_bucket(dist):
    max_exact = N_BUCKETS // 2
    d = jnp.maximum(dist, 1).astype(F32)
    large = max_exact + (jnp.log(d / max_exact) / math.log(MAX_DISTANCE / max_exact)
                         * (N_BUCKETS - max_exact)).astype(jnp.int32)
    large = jnp.minimum(large, N_BUCKETS - 1)
    return jnp.where(dist < max_exact, dist, large)


def _dil_buckets(dilation):
    iq = jnp.arange(DIL_BLOCK)[:, None]
    ik = jnp.arange(2 * DIL_BLOCK)[None, :]
    return _t5_bucket(jnp.maximum(DIL_BLOCK + iq - ik, 0) * dilation)


def _dil_logits(qh, k_ref, bias_h, n, span):
    lo = n * DIL_BLOCK
    if n == 0:
        s = _dot_nt(qh, k_ref[lo:lo + DIL_BLOCK, :]) * DIL_SCALE + bias_h[:, DIL_BLOCK:]
        rel = lax.broadcasted_iota(jnp.int32, s.shape, 0) - lax.broadcasted_iota(jnp.int32, s.shape, 1)
    else:
        s = _dot_nt(qh, k_ref[lo - DIL_BLOCK:lo + DIL_BLOCK, :]) * DIL_SCALE + bias_h
        rel = DIL_BLOCK + lax.broadcasted_iota(jnp.int32, s.shape, 0) - lax.broadcasted_iota(jnp.int32, s.shape, 1)
    return jnp.where((rel >= 0) & (rel <= span), s, -jnp.inf)


def _dil_views(dilation, rows):
    col = lambda which: pl.BlockSpec((rows, HEAD_PAD), lambda p, r: (r, which * DIL_PAIRS + p))
    nat = pl.BlockSpec((rows, HEAD_PAD), lambda p, r: (0, r * DIL_PAIRS + p))
    bias = pl.BlockSpec((2, DIL_BLOCK, 2 * DIL_BLOCK), lambda p, r: (p, 0, 0))
    return col, nat, bias


def _dil_attn_fwd(qkv, bias, dilation, span, name):
    s_dim = qkv.shape[0]
    rows = s_dim // dilation
    d_dim = DIL_HEADS * DIL_HEAD_DIM
    col, nat, bias_spec = _dil_views(dilation, rows)

    def body(q_ref, k_ref, v_ref, b_ref, o_ref, l_ref):
        lane = lax.broadcasted_iota(jnp.int32, (DIL_BLOCK, HEAD_PAD), 1)
        klane = lax.broadcasted_iota(jnp.int32, (2 * DIL_BLOCK, HEAD_PAD), 1)
        for n in range(rows // DIL_BLOCK):
            lo = n * DIL_BLOCK
            kv_rows = slice(lo, lo + DIL_BLOCK) if n == 0 else slice(lo - DIL_BLOCK, lo + DIL_BLOCK)
            qb, vb = q_ref[lo:lo + DIL_BLOCK, :], v_ref[kv_rows, :]
            o_acc = jnp.zeros((DIL_BLOCK, HEAD_PAD), F32)
            lse_acc = jnp.zeros((DIL_BLOCK, HEAD_PAD), F32)
            for h in range(2):
                mine = (lane < DIL_HEAD_DIM) == (h == 0)
                kmine = (klane[:vb.shape[0]] < DIL_HEAD_DIM) == (h == 0)
                logits = _dil_logits(jnp.where(mine, qb, 0), k_ref, b_ref[h], n, span)
                mx = jnp.max(logits, axis=-1, keepdims=True)
                lse = mx + jnp.log(jnp.sum(jnp.exp(logits - mx), axis=-1, keepdims=True))
                p = jnp.exp(logits - lse)
                o_acc = o_acc + _dot_nn(p.astype(BF16), jnp.where(kmine, vb, 0))
                lse_acc = jnp.where(mine, lse, lse_acc)
            o_ref[lo:lo + DIL_BLOCK, :] = o_acc
            l_ref[lo:lo + DIL_BLOCK, :] = lse_acc

    out = jax.ShapeDtypeStruct((rows, dilation * d_dim), F32)
    o, lse = _pcall(
        body, out_shape=(out, out), grid=(DIL_PAIRS, dilation),
        in_specs=[col(0), col(1), col(2), bias_spec], out_specs=(nat, nat), name=name,
        compiler_params=_params(("parallel", "parallel"), 3 * [((rows, HEAD_PAD), BF16)] + 2 * [((rows, HEAD_PAD), F32)]
                                + [((2, DIL_BLOCK, 2 * DIL_BLOCK), F32)], extra=2**21),
    )(qkv, qkv, qkv, bias)
    return o.reshape(s_dim, d_dim), lse.reshape(s_dim, d_dim)


def _dil_mix(lses, outs, name):
    s_dim, d_dim = outs[0].shape
    tm = TOKEN_TILE
    ng = len(outs)

    def body(*refs):
        ls = [refs[g][...] for g in range(ng)]
        mx = ls[0]
        for g in range(1, ng):
            mx = jnp.maximum(mx, ls[g])
        es = [jnp.exp(l - mx) for l in ls]
        tot = es[0]
        for g in range(1, ng):
            tot = tot + es[g]
        o = None
        for g in range(ng):
            al = es[g] / tot
            refs[2 * ng + g][...] = al
            t = al * refs[ng + g][...]
            o = t if o is None else o + t
        refs[3 * ng][...] = o
        refs[3 * ng + 1][...] = o.astype(BF16)

    row = pl.BlockSpec((tm, d_dim), lambda i: (i, 0))
    f = jax.ShapeDtypeStruct((s_dim, d_dim), F32)
    res = _pcall(
        body, out_shape=tuple(ng * [f] + [f, jax.ShapeDtypeStruct((s_dim, d_dim), BF16)]), grid=(s_dim // tm,),
        in_specs=2 * ng * [row], out_specs=tuple((ng + 2) * [row]), name=name,
        compiler_params=_params(("parallel",), (3 * ng + 2) * [((tm, d_dim), F32)], extra=4 * tm * d_dim * 4),
    )(*lses, *outs)
    return res[:ng], res[ng], res[ng + 1]


def _dil_attn_bwd(qkv, bias, d_o, o_mix, alpha, lse, dilation, span, name):
    s_dim = qkv.shape[0]
    rows = s_dim // dilation
    d_dim = DIL_HEADS * DIL_HEAD_DIM
    col, nat, bias_spec = _dil_views(dilation, rows)
    nat_view = lambda a: a.reshape(rows, dilation * d_dim)

    def body(q_ref, k_ref, v_ref, b_ref, do_ref, om_ref, al_ref, l_ref, dq_ref, dk_ref, dv_ref, db_ref, dk_acc, dv_acc):
        @pl.when(pl.program_id(1) == 0)
        def _():
            db_ref[...] = jnp.zeros_like(db_ref)

        dk_acc[...] = jnp.zeros_like(dk_acc)
        dv_acc[...] = jnp.zeros_like(dv_acc)
        lane = lax.broadcasted_iota(jnp.int32, (DIL_BLOCK, HEAD_PAD), 1)
        klane = lax.broadcasted_iota(jnp.int32, (2 * DIL_BLOCK, HEAD_PAD), 1)
        for n in range(rows // DIL_BLOCK):
            lo = n * DIL_BLOCK
            blk = slice(lo, lo + DIL_BLOCK)
            kv_rows = blk if n == 0 else slice(lo - DIL_BLOCK, lo + DIL_BLOCK)
            qb, kb, vb = q_ref[blk, :], k_ref[kv_rows, :], v_ref[kv_rows, :]
            al = al_ref[blk, :]
            dog = al * do_ref[blk, :]
            row_term = dog * om_ref[blk, :]
            lse_b = l_ref[blk, :]
            dq_acc = jnp.zeros((DIL_BLOCK, HEAD_PAD), F32)
            dk_blk = jnp.zeros((kb.shape[0], HEAD_PAD), F32)
            dv_blk = jnp.zeros((kb.shape[0], HEAD_PAD), F32)
            for h in range(2):
                mine = (lane < DIL_HEAD_DIM) == (h == 0)
                kmine = (klane[:kb.shape[0]] < DIL_HEAD_DIM) == (h == 0)
                qh = jnp.where(mine, qb, 0)
                logits = _dil_logits(qh, k_ref, b_ref[h], n, span)
                lse_h = jnp.max(jnp.where(mine, lse_b, -jnp.inf), axis=-1, keepdims=True)
                p = jnp.exp(logits - lse_h)
                dogh = jnp.where(mine, dog, 0.0).astype(BF16)
                dp = _dot_nt(dogh, vb)
                ds = p * (dp - jnp.sum(jnp.where(mine, row_term, 0.0), axis=-1, keepdims=True))
                if n == 0:
                    db_ref[h, :, DIL_BLOCK:] += ds
                else:
                    db_ref[h] += ds
                dsb = (ds * DIL_SCALE).astype(BF16)
                dq_acc = dq_acc + _dot_nn(dsb, jnp.where(kmine, kb, 0))
                dk_blk = dk_blk + _dot_tn(dsb, qh)
                dv_blk = dv_blk + _dot_tn(p.astype(BF16), dogh)
            dq_ref[blk, :] = dq_acc.astype(BF16)
            dk_acc[kv_rows, :] += dk_blk
            dv_acc[kv_rows, :] += dv_blk
        dk_ref[...] = dk_acc[...].astype(BF16)
        dv_ref[...] = dv_acc[...].astype(BF16)

    out_col = pl.BlockSpec((rows, HEAD_PAD), lambda p, r: (r, p))
    grad = jax.ShapeDtypeStruct((s_dim, d_dim), BF16)
    return _pcall(
        body, out_shape=(grad, grad, grad, jax.ShapeDtypeStruct(bias.shape, F32)), grid=(DIL_PAIRS, dilation),
        in_specs=[col(0), col(1), col(2), bias_spec, nat, nat, nat, nat],
        out_specs=(out_col, out_col, out_col, bias_spec), name=name,
        scratch_shapes=[pltpu.VMEM((rows, HEAD_PAD), F32), pltpu.VMEM((rows, HEAD_PAD), F32)],
        compiler_params=_params(("parallel", "arbitrary"), 6 * [((rows, HEAD_PAD), BF16)] + 4 * [((rows, HEAD_PAD), F32)]
                                + 2 * [((2, DIL_BLOCK, 2 * DIL_BLOCK), F32)], extra=2 * rows * HEAD_PAD * 4 + 2**21),
    )(qkv, qkv, qkv, bias, nat_view(d_o), nat_view(o_mix), nat_view(alpha), nat_view(lse))


def _bias_reduce(dbias, buckets, name):
    n_heads = dbias.shape[0]

    def body(db_ref, bk_ref, o_ref):
        ds, bk = db_ref[0], bk_ref[0]
        lane = lax.broadcasted_iota(jnp.int32, (8, HEAD_PAD), 1)
        acc = jnp.zeros((8, HEAD_PAD), F32)
        for b in range(N_BUCKETS):
            acc = jnp.where(lane == b, jnp.sum(jnp.where(bk == b, ds, 0.0)), acc)
        o_ref[0] = acc

    blk = (1, DIL_BLOCK, 2 * DIL_BLOCK)
    return _pcall(
        body, out_shape=jax.ShapeDtypeStruct((n_heads, 8, HEAD_PAD), F32), grid=(n_heads,),
        in_specs=[pl.BlockSpec(blk, lambda h: (h, 0, 0)), pl.BlockSpec(blk, lambda h: (h // DIL_HEADS, 0, 0))],
        out_specs=pl.BlockSpec((1, 8, HEAD_PAD), lambda h: (h, 0, 0)), name=name,
        compiler_params=_params(("parallel",), [(blk, F32), (blk, jnp.int32)], extra=2**20),
    )(dbias, buckets)


def _loss_grad(y, target, name):
    s_dim, d_dim = y.shape
    tm = TOKEN_TILE

    def body(y_ref, t_ref, dy_ref, l_ref):
        @pl.when(pl.program_id(0) == 0)
        def _():
            l_ref[...] = jnp.zeros_like(l_ref)

        err = y_ref[...] - t_ref[...]
        dy_ref[...] = err / d_dim
        sq = (err * err).reshape(tm // 8, 8, d_dim)
        l_ref[...] += 0.5 * jnp.sum(sq, axis=0) / d_dim

    row = pl.BlockSpec((tm, d_dim), lambda i: (i, 0))
    acc = pl.BlockSpec((8, d_dim), lambda i: (0, 0))
    return _pcall(
        body, out_shape=(jax.ShapeDtypeStruct((s_dim, d_dim), F32), jax.ShapeDtypeStruct((8, d_dim), F32)),
        grid=(s_dim // tm,), in_specs=[row, row], out_specs=(row, acc), name=name,
        compiler_params=_params(("arbitrary",), 3 * [((tm, d_dim), F32)], extra=2 * tm * d_dim * 4),
    )(y, target)


def _mod_fwd(c_all, w_mod, b_loc, name):
    depth, d_dim, n = w_mod.shape
    nb = c_all.shape[0]

    def body(c_ref, w_ref, b_ref, o_ref, s_ref):
        cv = c_ref[...]
        sc = cv * jax.nn.sigmoid(cv)
        s_ref[...] = sc
        o_ref[0] = _dot_nn(sc.astype(BF16), w_ref[0].astype(BF16)) + b_ref[0]

    return _pcall(
        body, out_shape=(jax.ShapeDtypeStruct((depth, nb, n), F32), jax.ShapeDtypeStruct((nb, d_dim), F32)), grid=(depth,),
        in_specs=[pl.BlockSpec((nb, d_dim), lambda i: (0, 0)), pl.BlockSpec((1, d_dim, n), lambda i: (i, 0, 0)),
                  pl.BlockSpec((1, 1, n), lambda i: (i, 0, 0))],
        out_specs=(pl.BlockSpec((1, nb, n), lambda i: (i, 0, 0)), pl.BlockSpec((nb, d_dim), lambda i: (0, 0))), name=name,
        compiler_params=_params(("arbitrary",), [((1, d_dim, n), F32)], extra=d_dim * n * 2 + 2**20),
    )(c_all, w_mod, b_loc.reshape(depth, 1, n))


def _sum_parts(parts, name):
    _, rows, cols = parts.shape
    tr = rows
    for cand in (512, 384, 256, 128, 64, 32, 16):
        if rows % cand == 0 and rows > cand:
            tr = cand
            break

    def body(p_ref, o_ref):
        acc = p_ref[0].astype(F32)
        for k in range(1, NDEV):
            acc = acc + p_ref[k].astype(F32)
        o_ref[...] = acc

    return _pcall(
        body, out_shape=jax.ShapeDtypeStruct((rows, cols), F32), grid=(rows // tr,),
        in_specs=[pl.BlockSpec((NDEV, tr, cols), lambda i: (0, i, 0))], out_specs=pl.BlockSpec((tr, cols), lambda i: (i, 0)),
        name=name, compiler_params=_params(("parallel",), [((NDEV, tr, cols), parts.dtype), ((tr, cols), F32)], extra=2**20),
    )(parts)


def _adamw(w, g, m, v, name):
    shape = w.shape
    cols = shape[-1]
    rows = math.prod(shape[:-1])
    tr = rows
    for cand in (512, 256, 128, 64, 32, 16, 8):
        if rows % cand == 0 and rows > cand and cand * cols * 4 <= 2**21:
            tr = cand
            break

    def body(w_ref, g_ref, m_ref, v_ref, d_ref, mo_ref, vo_ref):
        gv = g_ref[...]
        mn = ADAM_B1 * m_ref[...] + (1.0 - ADAM_B1) * gv
        vn = ADAM_B2 * v_ref[...] + (1.0 - ADAM_B2) * (gv * gv)
        m_hat = mn / (1.0 - ADAM_B1 ** ADAM_STEP)
        v_hat = vn / (1.0 - ADAM_B2 ** ADAM_STEP)
        d_ref[...] = -ADAM_LR * (m_hat / (jnp.sqrt(v_hat) + ADAM_EPS) + ADAM_WD * w_ref[...])
        mo_ref[...] = mn
        vo_ref[...] = vn

    blk = pl.BlockSpec((tr, cols), lambda i: (i, 0))
    out = jax.ShapeDtypeStruct((rows, cols), F32)
    res = _pcall(
        body, out_shape=(out, out, out), grid=(rows // tr,), in_specs=4 * [blk], out_specs=(blk, blk, blk), name=name,
        compiler_params=_params(("parallel",), 7 * [((tr, cols), F32)], extra=4 * tr * cols * 4),
    )(*(a.reshape(rows, cols) for a in (w, g, m, v)))
    return tuple(r.reshape(shape) for r in res)


def _peers():
    x, y, c = lax.axis_index("x"), lax.axis_index("y"), lax.axis_index("c")
    flip = lambda v, f: 1 - v if f else v
    peers = []
    for f in range(1, NDEV):
        px, py, pc = flip(x, f & 4), flip(y, f & 2), flip(c, f & 1)
        peers.append(((px, py, pc), 4 * px + 2 * py + pc))
    return (x, y, c), 4 * x + 2 * y + c, peers


def _places():
    x, y, c = lax.axis_index("x"), lax.axis_index("y"), lax.axis_index("c")
    place = lambda px, py, pc: ((px, py, pc), 4 * px + 2 * py + pc)
    return place(x, y, c), place(x, y, 1 - c), [place(1 - x, y, c), place(x, 1 - y, c), place(1 - x, 1 - y, c)]


def _exchange(arrs, gather, name):
    n = len(arrs)
    hbm = pl.BlockSpec(memory_space=pltpu.HBM)
    if gather:
        out_shape = [jax.ShapeDtypeStruct((NDEV * a.shape[0], a.shape[1]), a.dtype) for a in arrs]
    else:
        out_shape = [jax.ShapeDtypeStruct((NDEV, a.shape[0] // NDEV, a.shape[1]), a.dtype) for a in arrs]

    def body(*refs):
        ins, outs = refs[:n], refs[n:2 * n]
        send_sems, recv_sems, local_sems = refs[2 * n:]
        me_pos, me, peers = _peers()
        local = []
        for k in range(n):
            rows = arrs[k].shape[0] if gather else arrs[k].shape[0] // NDEV
            if gather:
                src_of = lambda idx: ins[k]
                dst_of = lambda idx: outs[k].at[pl.ds(me * rows, rows)]
                mine = (ins[k], outs[k].at[pl.ds(me * rows, rows)])
            else:
                src_of = lambda idx: ins[k].at[pl.ds(idx * rows, rows)]
                dst_of = lambda idx: outs[k].at[me]
                mine = (ins[k].at[pl.ds(me * rows, rows)], outs[k].at[me])
            cp = pltpu.make_async_copy(mine[0], mine[1], local_sems.at[k])
            cp.start()
            local.append(cp)
            for pos, idx in peers:
                pltpu.make_async_remote_copy(src_ref=src_of(idx), dst_ref=dst_of(idx), send_sem=send_sems.at[k],
                                             recv_sem=recv_sems.at[k], device_id=pos, device_id_type=MESH).start()
        for k in range(n):
            rows = arrs[k].shape[0] if gather else arrs[k].shape[0] // NDEV
            sent = ins[k].at[pl.ds(0, (NDEV - 1) * rows)] if not gather else outs[k].at[pl.ds(0, (NDEV - 1) * rows)]
            got = outs[k].at[pl.ds(0, (NDEV - 1) * rows)] if gather else outs[k].at[pl.ds(0, NDEV - 1)]
            pltpu.make_async_remote_copy(src_ref=sent, dst_ref=sent, send_sem=send_sems.at[k], recv_sem=recv_sems.at[k],
                                         device_id=me_pos, device_id_type=MESH).wait_send()
            pltpu.make_async_remote_copy(src_ref=got, dst_ref=got, send_sem=send_sems.at[k], recv_sem=recv_sems.at[k],
                                         device_id=me_pos, device_id_type=MESH).wait_recv()
            local[k].wait()

    return pl.pallas_call(
        body, out_shape=out_shape, in_specs=n * [hbm], out_specs=n * [hbm], name=name,
        scratch_shapes=[pltpu.SemaphoreType.DMA((n,)), pltpu.SemaphoreType.DMA((n,)), pltpu.SemaphoreType.DMA((n,))],
        compiler_params=pltpu.CompilerParams(has_side_effects=True),
    )(*arrs)


_HBM = pl.BlockSpec(memory_space=pltpu.HBM)
_SEM = pl.BlockSpec(memory_space=pltpu.SEMAPHORE)
_DATAFLOW = pltpu.SideEffectType.DATAFLOW_SIDE_EFFECTING


def _split_start(srcs, groups, gather, name):
    n = len(srcs)
    if gather:
        lands = [lax.empty((NDEV * a.shape[0], a.shape[1]), a.dtype) for a in srcs]
    else:
        lands = [lax.empty((NDEV, a.shape[0] // NDEV, a.shape[1]), a.dtype) for a in srcs]
    n_sem = 3 * len(groups)

    def body(*refs):
        src_refs, land_refs = refs[:n], refs[n:2 * n]
        sems = refs[2 * n:2 * n + n_sem]
        token = refs[-1]
        (_, my), sibling, chips = _places()
        _, _, peers = _peers()
        targets = [sibling] + chips if gather else peers
        for g, members in enumerate(groups):
            for j, k in enumerate(members):
                _own_copy(src_refs[k], land_refs[k], sems[3 * g + 2].at[j], my, gather).start()
        for g, members in enumerate(groups):
            for j, k in enumerate(members):
                rows = srcs[k].shape[0] if gather else srcs[k].shape[0] // NDEV
                for pos, idx in targets:
                    src = src_refs[k] if gather else src_refs[k].at[pl.ds(idx * rows, rows)]
                    dst = land_refs[k].at[pl.ds(my * rows, rows)] if gather else land_refs[k].at[my]
                    pltpu.make_async_remote_copy(src_ref=src, dst_ref=dst, send_sem=sems[3 * g].at[j],
                                                 recv_sem=sems[3 * g + 1].at[j], device_id=pos, device_id_type=MESH).start()
        token[...] = jnp.zeros_like(token)

    out_shape = []
    for members in groups:
        out_shape += 3 * [pltpu.SemaphoreType.DMA((len(members),))]
    out_shape += [pltpu.HBM(a.shape, a.dtype) for a in srcs] + [pltpu.HBM(a.shape, a.dtype) for a in lands]
    out_shape.append(jax.ShapeDtypeStruct((8, 128), F32))
    res = pl.pallas_call(
        body, name=name, out_shape=tuple(out_shape), in_specs=2 * n * [_HBM],
        out_specs=tuple(n_sem * [_SEM] + 2 * n * [_HBM] + [pl.BlockSpec(memory_space=pltpu.VMEM)]),
        input_output_aliases={i: n_sem + i for i in range(2 * n)},
        compiler_params=pltpu.CompilerParams(has_side_effects=_DATAFLOW),
    )(*[pltpu.with_memory_space_constraint(a, pltpu.HBM) for a in list(srcs) + lands])
    sems = [tuple(res[3 * g:3 * g + 3]) for g in range(len(groups))]
    return sems, list(res[n_sem:n_sem + n]), list(res[n_sem + n:n_sem + 2 * n]), res[-1]


def _own_copy(src_ref, land_ref, sem, my, gather):
    if gather:
        rows = src_ref.shape[0]
        return pltpu.make_async_copy(src_ref, land_ref.at[pl.ds(my * rows, rows)], sem)
    rows = src_ref.shape[0] // NDEV
    return pltpu.make_async_copy(src_ref.at[pl.ds(my * rows, rows)], land_ref.at[my], sem)


def _wait_all(land_ref, blocks_per_dev, copies, send_sem, recv_sem, me_pos):
    part = land_ref.at[pl.ds(0, copies * blocks_per_dev)]
    pltpu.make_async_remote_copy(src_ref=part, dst_ref=part, send_sem=send_sem, recv_sem=recv_sem,
                                 device_id=me_pos, device_id_type=MESH).wait()


def _gather_forward(sems, srcs, lands, after, name):
    n = len(srcs)

    def body(*refs):
        land_refs = refs[n:2 * n]
        send_a, recv_a = refs[2 * n], refs[2 * n + 1]
        send_b, recv_b = refs[2 * n + 3], refs[2 * n + 4]
        token = refs[-1]
        (me_pos, _), sibling, chips = _places()
        for j in range(n):
            _wait_all(land_refs[j], lands[j].shape[0] // NDEV, 1 + OTHER_CHIPS, send_a.at[j], recv_a.at[j], me_pos)
        for j in range(n):
            rows = lands[j].shape[0] // NDEV
            for _, idx in chips:
                block = land_refs[j].at[pl.ds(idx * rows, rows)]
                pltpu.make_async_remote_copy(src_ref=block, dst_ref=block, send_sem=send_b.at[j], recv_sem=recv_b.at[j],
                                             device_id=sibling[0], device_id_type=MESH).start()
        token[...] = jnp.zeros_like(token)

    res = pl.pallas_call(
        body, name=name,
        out_shape=(pltpu.SemaphoreType.DMA((n,)), pltpu.SemaphoreType.DMA((n,)))
        + tuple(pltpu.HBM(a.shape, a.dtype) for a in list(srcs) + list(lands)) + (jax.ShapeDtypeStruct((8, 128), F32),),
        in_specs=2 * n * [_HBM] + [_SEM, _SEM, pl.BlockSpec(memory_space=pl.ANY)],
        out_specs=tuple([_SEM, _SEM] + 2 * n * [_HBM] + [pl.BlockSpec(memory_space=pltpu.VMEM)]),
        input_output_aliases={i: 2 + i for i in range(2 * n)},
        compiler_params=pltpu.CompilerParams(has_side_effects=_DATAFLOW),
    )(*srcs, *lands, sems[0], sems[1], after)
    return (res[0], res[1]), list(res[2:2 + n]), list(res[2 + n:2 + 2 * n]), res[-1]


def _split_wait(sems, srcs, lands, after, copies, gather, name):
    n = len(srcs)

    def body(*refs):
        src_refs, land_refs = refs[:n], refs[n:2 * n]
        send_sem, recv_sem, local_sem = refs[2 * n], refs[2 * n + 1], refs[2 * n + 2]
        (me_pos, my), _, _ = _places()
        for j in range(n):
            _wait_all(land_refs[j], lands[j].shape[0] // NDEV, copies, send_sem.at[j], recv_sem.at[j], me_pos)
            _own_copy(src_refs[j], land_refs[j], local_sem.at[j], my, gather).wait()

    res = pl.pallas_call(
        body, name=name, out_shape=tuple(pltpu.HBM(a.shape, a.dtype) for a in list(srcs) + list(lands)),
        in_specs=2 * n * [_HBM] + [_SEM, _SEM, _SEM, pl.BlockSpec(memory_space=pl.ANY)], out_specs=tuple(2 * n * [_HBM]),
        input_output_aliases={i: i for i in range(2 * n)},
        compiler_params=pltpu.CompilerParams(has_side_effects=_DATAFLOW),
    )(*srcs, *lands, sems[0], sems[1], sems[2], after)
    return list(res[n:])


def _ffn_fwd(x, norms, mod, w):
    (pre_g, post_g), (shift, scale, gate), (wg_t, wu_t, wd) = norms, mod, w
    hn, g, u, a = _ffn_up(x, pre_g, scale, shift, wg_t, wu_t, "ffn_up")
    x_out, f = _mm_post(a, wd, x, post_g, gate, FFN_RES, "ffn_down")
    return x_out, (x, hn, g, u, a, f)


def _ffn_bwd(dx_out, saved, norms, mod, w):
    (pre_g, post_g), (_, scale, gate), (wg_t, wu_t, wd) = norms, mod, w
    x, hn, g, u, a, f = saved
    d_model = x.shape[1]
    df, dgate, dpost = _post_bwd(dx_out, f, post_g, gate, FFN_RES, "ffn_post_bwd")
    dg, du = _ffn_dgu(df, wd, g, u, "ffn_dgu")
    dwd = _mm([(a, df)], "tn", BF16, 256, d_model, "ffn_dw")
    dwg_t = _mm([(dg, hn)], "tn", BF16, 256, d_model, "ffn_dw")
    dwu_t = _mm([(du, hn)], "tn", BF16, 256, d_model, "ffn_dw")
    dhn = _mm([(dg, wg_t), (du, wu_t)], "nn", F32, TOKEN_TILE, d_model, "ffn_dhn")
    dx, dshift, dscale, dpre = _prenorm_bwd(dx_out, [dhn], x, pre_g, scale, "prenorm_bwd")
    return dx, (dpre, dpost), (dshift, dscale, dgate), (dwg_t, dwu_t, dwd)


def _mla_fwd(x, norms, mod, w, rope):
    (pre_g, post_g), (shift, scale, gate) = norms, mod
    w_in, q_norm, wq_t, kv_norm, wkv_t, wo = w
    hn, lat = _prenorm_mm(x, pre_g, scale, shift, w_in, "nn", F32, LAT_PAD, "mla_in")
    q, k, v, qn, kvn = _mla_qkv(lat, q_norm, kv_norm, wq_t, wkv_t, rope, "mla_qkv")
    o = _mla_attn_fwd(q, k, v, "mla_attn_fwd")
    x_out, f = _mm_post(o, wo, x, post_g, gate, 1.0, "mla_out")
    return x_out, (x, hn, lat, q, k, v, qn, kvn, o, f)


def _mla_bwd(dx_out, saved, norms, mod, w, rope):
    (pre_g, post_g), (_, scale, gate) = norms, mod
    w_in, q_norm, wq_t, kv_norm, wkv_t, wo = w
    x, hn, lat, q, k, v, qn, kvn, o, f = saved
    d_model = x.shape[1]
    df, dgate, dpost = _post_bwd(dx_out, f, post_g, gate, 1.0, "mix_post_bwd")
    d_o = _mm([(df, wo)], "nt", F32, TOKEN_TILE, wo.shape[0], "mla_do")
    dwo = _mm([(o, df)], "tn", BF16, TOKEN_TILE, d_model, "mla_dwo")
    dq, dk, dv = _mla_attn_bwd(q, k, v, d_o, "mla_attn_bwd")
    dqp, dkv, dlat, dq_norm, dkv_norm = _mla_qkv_bwd(dq, dk, dv, lat, q_norm, kv_norm, wq_t, wkv_t, rope, "mla_qkv_bwd")
    dwq_t = _mm([(dqp, qn)], "tn", BF16, TOKEN_TILE, Q_LORA, "mla_dwq")
    dwkv_t = _mm([(dkv, kvn)], "tn", BF16, TOKEN_TILE, KV_LORA, "mla_dwkv")
    dw_in = _mm([(hn, dlat)], "tn", BF16, TOKEN_TILE, LAT_PAD, "mla_dwin")
    dhn = _mm([(dlat, w_in)], "nt", F32, TOKEN_TILE, d_model, "mla_dhn")
    dx, dshift, dscale, dpre = _prenorm_bwd(dx_out, [dhn], x, pre_g, scale, "prenorm_bwd")
    return dx, (dpre, dpost), (dshift, dscale, dgate), (dw_in, dq_norm, dwq_t, dkv_norm, dwkv_t, dwo)


def _dil_fwd(x, norms, mod, w, bias):
    (pre_g, post_g), (shift, scale, gate), (w_in_t, wo) = norms, mod, w
    width = 3 * DIL_HEADS * DIL_HEAD_DIM
    hns, qkvs, outs, lses = [], [], [], []
    for g, (window, dilation) in enumerate(DIL_GROUPS):
        hn, qkv = _prenorm_mm(x, pre_g, scale, shift, w_in_t[g * width:(g + 1) * width], "nt", BF16, width,
                              "dil_in", perm=dilation)
        o, lse = _dil_attn_fwd(qkv, bias[g], dilation, window // dilation, "dil_attn_fwd")
        hns.append(hn), qkvs.append(qkv), outs.append(o), lses.append(lse)
    alphas, o_mix, o_mix_b = _dil_mix(lses, outs, "dil_mix")
    x_out, f = _mm_post(o_mix_b, wo, x, post_g, gate, 1.0, "dil_out")
    return x_out, (x, hns, qkvs, lses, alphas, o_mix, o_mix_b, f)


def _dil_bwd(dx_out, saved, norms, mod, w, bias):
    (pre_g, post_g), (_, scale, gate), (w_in_t, wo) = norms, mod, w
    x, hns, qkvs, lses, alphas, o_mix, o_mix_b, f = saved
    d_model = x.shape[1]
    inner = DIL_HEADS * DIL_HEAD_DIM
    df, dgate, dpost = _post_bwd(dx_out, f, post_g, gate, 1.0, "mix_post_bwd")
    d_o = _mm([(df, wo)], "nt", F32, TOKEN_TILE, inner, "dil_do")
    dwo = _mm([(o_mix_b, df)], "tn", BF16, TOKEN_TILE, d_model, "dil_dwo")
    dhns, dws, dbs = [], [], []
    for g, (window, dilation) in enumerate(DIL_GROUPS):
        grads = _dil_attn_bwd(qkvs[g], bias[g], d_o, o_mix, alphas[g], lses[g], dilation, window // dilation, "dil_attn_bwd")
        dbs.append(grads[3])
        w_parts = [w_in_t[(3 * g + j) * inner:(3 * g + j + 1) * inner] for j in range(3)]
        dhns.append(_mm(list(zip(grads[:3], w_parts)), "nn", F32, TOKEN_TILE, d_model, "dil_dhn", out_perm=dilation))
        dws += [_mm([(grads[j], hns[g])], "tn", BF16, TOKEN_TILE, d_model, "dil_dwin") for j in range(3)]
    dx, dshift, dscale, dpre = _prenorm_bwd(dx_out, dhns, x, pre_g, scale, "prenorm_bwd3")
    return dx, (dpre, dpost), (dshift, dscale, dgate), (jnp.concatenate(dws, axis=0), dwo), jnp.concatenate(dbs, axis=0)


def _pad_rows(a, rows):
    return jnp.pad(a, ((0, rows - a.shape[0]), (0, 0)))


def _lanes(a):
    flat = a.reshape(-1).astype(F32)
    rows = -(-flat.shape[0] // 1024) * 8
    return jnp.pad(flat, (0, rows * 128 - flat.shape[0])).reshape(rows, 128)


def kernel(x, c, norm_pre, norm_post, w_mod, b_mod, ffn_w_gate, ffn_w_up, ffn_w_down, mla_w_in, mla_q_norm, mla_w_q_up, mla_kv_norm, mla_w_kv_up, mla_w_o, dil_w_in, dil_w_o, rel_bias, loss_target, m_norm_pre, m_norm_post, m_w_mod, m_b_mod, m_ffn_w_gate, m_ffn_w_up, m_ffn_w_down, m_mla_w_in, m_mla_q_norm, m_mla_w_q_up, m_mla_kv_norm, m_mla_w_kv_up, m_mla_w_o, m_dil_w_in, m_dil_w_o, m_rel_bias, v_norm_pre, v_norm_post, v_w_mod, v_b_mod, v_ffn_w_gate, v_ffn_w_up, v_ffn_w_down, v_mla_w_in, v_mla_q_norm, v_mla_w_q_up, v_mla_kv_norm, v_mla_w_kv_up, v_mla_w_o, v_dil_w_in, v_dil_w_o, v_rel_bias):
    me = 4 * lax.axis_index("x") + 2 * lax.axis_index("y") + lax.axis_index("c")
    depth, n_sub, d_loc = norm_pre.shape
    d_model = x.shape[2]
    mod_loc_cols = w_mod.shape[2]
    x0, target = x[0], loss_target[0]

    small = jnp.concatenate([c.reshape(8, 128), _pad_rows(norm_pre.reshape(depth * n_sub, d_loc), 8),
                             _pad_rows(norm_post.reshape(depth * n_sub, d_loc), 8)], axis=0)
    small_all = _exchange([small], True, "gather_small")[0].reshape(NDEV, 24, 128)
    c_all = small_all[:, 0:8].reshape(NDEV, d_model)
    gains = lambda lo: jnp.transpose(small_all[:, lo:lo + depth * n_sub], (1, 0, 2)).reshape(depth, n_sub, 1, d_model)
    pre_full, post_full = gains(8), gains(16)

    b_loc = lax.dynamic_slice(b_mod, (0, me * mod_loc_cols), (depth, mod_loc_cols))
    mod_cols, silu_c = _mod_fwd(c_all, w_mod, b_loc, "mod_fwd")
    mod_all = _exchange([mod_cols.reshape(depth * NDEV, mod_loc_cols)], True, "gather_mod")[0]
    mod_all = mod_all.reshape(NDEV, depth, NDEV, mod_loc_cols)
    mod_mine = lax.dynamic_index_in_dim(mod_all, me, axis=2, keepdims=False)
    mod = jnp.transpose(mod_mine, (1, 0, 2)).reshape(depth, n_sub, 3, 1, d_model)

    bf_t = lambda a: a.astype(BF16).T
    ffn_ids = [(i, h) for i in range(depth) for h in range(2)]
    shards = []
    for i, h in ffn_ids:
        shards += [bf_t(ffn_w_gate[i, h]), bf_t(ffn_w_up[i, h]), ffn_w_down[i, h].astype(BF16)]
    shards += [mla_w_in[0].astype(BF16), bf_t(mla_w_q_up[0]), bf_t(mla_w_kv_up[0]), mla_w_o[0].astype(BF16),
               bf_t(dil_w_in[0]), dil_w_o[0].astype(BF16)]
    n_ffn = 3 * len(ffn_ids)
    members = {(0, 0): [0, 1, 2], (0, 1): [n_ffn, n_ffn + 1, n_ffn + 2, n_ffn + 3], (0, 2): [3, 4, 5],
               (1, 0): [6, 7, 8], (1, 1): [n_ffn + 4, n_ffn + 5], (1, 2): [9, 10, 11]}
    order = [(i, s) for i in range(depth) for s in range(n_sub)]
    g_sems, g_srcs, g_lands, _ = _split_start(shards, [members[k] for k in order], True, "gather_weights_start")

    forwarded = {}

    def forward(key, after):
        idx = members[key]
        forwarded[key] = _gather_forward(g_sems[order.index(key)], [g_srcs[k] for k in idx], [g_lands[k] for k in idx], after,
                                         "gather_forward_%d%d" % key)
        return forwarded[key][3]

    def weights_of(key, after):
        (send_b, recv_b), srcs, lands, _ = forwarded[key]
        local = g_sems[order.index(key)][2]
        return _split_wait((send_b, recv_b, local), srcs, lands, after, OTHER_CHIPS, True, "gather_wait_%d%d" % key)

    lat_real = Q_LORA + KV_LORA
    qk = QK_NOPE + QK_ROPE

    def mla_weights(after):
        w_in, wq_t, wkv_t, wo = weights_of((0, 1), after)
        w_in_pad = jnp.concatenate([w_in[:, :lat_real], jnp.zeros((d_model, QK_NOPE), BF16), w_in[:, lat_real:],
                                    jnp.zeros((d_model, HEAD_PAD - QK_NOPE - QK_ROPE), BF16)], axis=1)
        wq_pad = jnp.pad(wq_t.reshape(MLA_HEADS, qk, Q_LORA), ((0, 0), (0, HEAD_PAD - qk), (0, 0)))
        wo_pad = jnp.pad(wo.reshape(MLA_HEADS, V_HEAD, d_model), ((0, 0), (HEAD_PAD - V_HEAD, 0), (0, 0)))
        return (w_in_pad, mla_q_norm, wq_pad.reshape(MLA_HEADS * HEAD_PAD, Q_LORA), mla_kv_norm, wkv_t,
                wo_pad.reshape(MLA_HEADS * HEAD_PAD, d_model))

    rope = _rope_tables()
    buckets = jnp.stack([_dil_buckets(dil) for _, dil in DIL_GROUPS])
    onehot = (buckets[..., None] == jnp.arange(N_BUCKETS)).astype(F32)
    bias = jnp.einsum("gqkb,bgh->ghqk", onehot, rel_bias.reshape(N_BUCKETS, len(DIL_GROUPS), DIL_HEADS),
                      precision=lax.Precision.HIGHEST)

    norms = lambda i, s: (pre_full[i, s], post_full[i, s])
    mods = lambda i, s: (mod[i, s, 0], mod[i, s, 1], mod[i, s, 2])
    saved, weights = {}, {}
    h = x0
    forward(order[0], h)
    for n, (i, s) in enumerate(order):
        weights[i, s] = mla_weights(h) if (s == 1 and i % 2 == 0) else tuple(weights_of((i, s), h))
        md = mods(i, s)
        if n + 1 < len(order):
            md = (md[0] + forward(order[n + 1], weights[i, s][0])[:1, :1], md[1], md[2])
        if s != 1:
            h, saved[i, s] = _ffn_fwd(h, norms(i, s), md, weights[i, s])
        elif i % 2 == 0:
            h, saved[i, s] = _mla_fwd(h, norms(i, s), md, weights[i, s], rope)
        else:
            h, saved[i, s] = _dil_fwd(h, norms(i, s), md, weights[i, s], bias)
    dh, loss_parts = _loss_grad(h, target, "loss")

    dnorm, dmod, sent = {}, {}, {}
    token = jnp.zeros((8, 128), F32)
    for i, s in reversed(order):
        md = mods(i, s)
        md = (md[0], md[1], md[2] + token[:1, :1])
        if s != 1:
            dh, dnorm[i, s], dmod[i, s], dws = _ffn_bwd(dh, saved[i, s], norms(i, s), md, weights[i, s])
        elif i % 2 == 0:
            dh, dnorm[i, s], dmod[i, s], dmla = _mla_bwd(dh, saved[i, s], norms(i, s), md, weights[i, s], rope)
            dw_in_pad, dq_norm, dwq_pad, dkv_norm, dwkv_t, dwo_pad = dmla
            dw_in = jnp.concatenate([dw_in_pad[:, :lat_real], dw_in_pad[:, lat_real + QK_NOPE:lat_real + qk]], axis=1)
            dwq_t = dwq_pad.reshape(MLA_HEADS, HEAD_PAD, Q_LORA)[:, :qk].reshape(MLA_HEADS * qk, Q_LORA)
            dwo = dwo_pad.reshape(MLA_HEADS, HEAD_PAD, d_model)[:, HEAD_PAD - V_HEAD:].reshape(MLA_HEADS * V_HEAD, d_model)
            dws = (dw_in, dwq_t, dwkv_t, dwo)
        else:
            dh, dnorm[i, s], dmod[i, s], dws, dbias = _dil_bwd(dh, saved[i, s], norms(i, s), md, weights[i, s], bias)
        sent[i, s] = _split_start(list(dws), [list(range(len(dws)))], False, "scatter_start_%d%d" % (i, s))
        token = sent[i, s][3]
    grad_x = dh[None]

    mine = {}
    for key in order:
        sems, srcs, lands, _ = sent[key]
        parts = _split_wait(sems[0], srcs, lands, dh, NDEV - 1, False, "scatter_wait_%d%d" % key)
        for k, p in zip(members[key], parts):
            mine[k] = _sum_parts(p, "sum_parts")
    g_gate = jnp.stack([mine[3 * n].T for n in range(len(ffn_ids))]).reshape(ffn_w_gate.shape)
    g_up = jnp.stack([mine[3 * n + 1].T for n in range(len(ffn_ids))]).reshape(ffn_w_up.shape)
    g_down = jnp.stack([mine[3 * n + 2] for n in range(len(ffn_ids))]).reshape(ffn_w_down.shape)
    g_mla_in, g_q_up, g_kv_up, g_mla_o, g_dil_in, g_dil_o = (mine[k] for k in range(n_ffn, n_ffn + 6))
    g_mla_in, g_q_up, g_kv_up, g_mla_o = g_mla_in[None], g_q_up.T[None], g_kv_up.T[None], g_mla_o[None]
    g_dil_in, g_dil_o = g_dil_in.T[None], g_dil_o[None]

    dmod_mine = jnp.concatenate([jnp.concatenate(dmod[i, s], axis=0) for i in range(depth) for s in range(n_sub)], axis=0)
    dpre_mine = jnp.concatenate([dnorm[i, s][0] for i in range(depth) for s in range(n_sub)], axis=0)
    dpost_mine = jnp.concatenate([dnorm[i, s][1] for i in range(depth) for s in range(n_sub)], axis=0)
    dbias_tab = _bias_reduce(dbias, buckets, "bias_reduce")[:, 0, :N_BUCKETS].T
    pieces = [dmod_mine, dpre_mine, dpost_mine, dq_norm, dkv_norm, dbias_tab, jnp.sum(loss_parts).reshape(1, 1)]
    packed = [_lanes(p) for p in pieces]
    offs = [0]
    for p in packed:
        offs.append(offs[-1] + p.shape[0])
    everyone = _exchange([jnp.concatenate(packed, axis=0)], True, "gather_small_grads")[0].reshape(NDEV, offs[-1], 128)
    total = _sum_parts(everyone, "sum_small")
    take = lambda n, shape: total[offs[n]:offs[n + 1]].reshape(-1)[:math.prod(shape)].reshape(shape)
    g_b_mod = take(0, b_mod.shape)
    col0 = me * d_loc
    g_norm_pre = lax.dynamic_slice(take(1, (depth, n_sub, d_model)), (0, 0, col0), norm_pre.shape)
    g_norm_post = lax.dynamic_slice(take(2, (depth, n_sub, d_model)), (0, 0, col0), norm_post.shape)
    g_q_norm, g_kv_norm = take(3, mla_q_norm.shape), take(4, mla_kv_norm.shape)
    g_rel_bias = take(5, rel_bias.shape)
    loss = take(6, ())

    dmod_all = everyone[:, offs[0]:offs[1]].reshape(NDEV, depth, NDEV * mod_loc_cols)
    dmod_cols = lax.dynamic_slice(dmod_all, (0, 0, me * mod_loc_cols), (NDEV, depth, mod_loc_cols))
    silu_t = jnp.pad(silu_c.T, ((0, 0), (0, HEAD_PAD - NDEV)))
    g_w_mod = jnp.stack([_mm([(silu_t, jnp.pad(dmod_cols[:, i], ((0, HEAD_PAD - NDEV), (0, 0))))], "nn", F32, TOKEN_TILE,
                             mod_loc_cols, "mod_bwd") for i in range(depth)])

    ws = (norm_pre, norm_post, w_mod, b_mod, ffn_w_gate, ffn_w_up, ffn_w_down, mla_w_in, mla_q_norm, mla_w_q_up, mla_kv_norm,
          mla_w_kv_up, mla_w_o, dil_w_in, dil_w_o, rel_bias)
    gs = (g_norm_pre, g_norm_post, g_w_mod, g_b_mod, g_gate, g_up, g_down, g_mla_in, g_q_norm, g_q_up, g_kv_norm, g_kv_up,
          g_mla_o, g_dil_in, g_dil_o, g_rel_bias)
    ms = (m_norm_pre, m_norm_post, m_w_mod, m_b_mod, m_ffn_w_gate, m_ffn_w_up, m_ffn_w_down, m_mla_w_in, m_mla_q_norm,
          m_mla_w_q_up, m_mla_kv_norm, m_mla_w_kv_up, m_mla_w_o, m_dil_w_in, m_dil_w_o, m_rel_bias)
    vs = (v_norm_pre, v_norm_post, v_w_mod, v_b_mod, v_ffn_w_gate, v_ffn_w_up, v_ffn_w_down, v_mla_w_in, v_mla_q_norm,
          v_mla_w_q_up, v_mla_kv_norm, v_mla_w_kv_up, v_mla_w_o, v_dil_w_in, v_dil_w_o, v_rel_bias)
    stepped = [_adamw(w, g, m, v, "adamw") for w, g, m, v in zip(ws, gs, ms, vs)]
    deltas, new_m, new_v = zip(*stepped)
    return (loss, grad_x, *gs, *deltas, *new_m, *new_v)
```

```python
import math

import jax
import jax.numpy as jnp
from jax import lax
from jax.experimental import pallas as pl
from jax.experimental.pallas import tpu as pltpu

F32 = jnp.float32
BF16 = jnp.bfloat16
MESH = pl.DeviceIdType.MESH

NDEV = 8
OTHER_CHIPS = 3
D_MODEL = 1024
SEQ = 2048
D_FF = 2816
EPS = 1e-6
FFN_RES = 0.5

MLA_HEADS = 16
Q_LORA = 384
KV_LORA = 256
QK_NOPE = 64
QK_ROPE = 32
V_HEAD = 64
ROPE_THETA = 10000.0
HEAD_PAD = 128
LAT_PAD = Q_LORA + KV_LORA + HEAD_PAD
MLA_SCALE = (QK_NOPE + QK_ROPE) ** -0.5

DIL_GROUPS = ((128, 1), (512, 4), (2048, 16))
DIL_HEADS = 16
DIL_HEAD_DIM = 64
DIL_BLOCK = 128
DIL_PAIRS = DIL_HEADS // 2
DIL_SCALE = DIL_HEAD_DIM ** -0.5
N_BUCKETS = 32
MAX_DISTANCE = 2048

ADAM_LR = 0.001
ADAM_B1 = 0.9
ADAM_B2 = 0.999
ADAM_EPS = 1e-08
ADAM_WD = 0.01
ADAM_STEP = 10

V7X_VMEM_BYTES = 64 * 2**20
VMEM_RESERVE = 10 * 2**20
TOKEN_TILE = 512


def _nbytes(shape, dtype):
    return math.prod(shape) * jnp.dtype(dtype).itemsize


def _params(semantics, blocks, extra=0):
    need = 2 * sum(_nbytes(s, d) for s, d in blocks) + extra + VMEM_RESERVE
    return pltpu.CompilerParams(dimension_semantics=semantics,
                                vmem_limit_bytes=int(min(need, V7X_VMEM_BYTES - VMEM_RESERVE)))


def _pcall(body, out_shape, **kw):
    call = pl.pallas_call(body, out_shape=jax.tree.map(lambda s: pltpu.HBM(s.shape, s.dtype), out_shape), **kw)
    return lambda *args: call(*[pltpu.with_memory_space_constraint(a, pltpu.HBM) for a in args])


def _dot_nn(a, b):
    return lax.dot_general(a, b, (((1,), (0,)), ((), ())), preferred_element_type=F32)


def _dot_nt(a, b):
    return lax.dot_general(a, b, (((1,), (1,)), ((), ())), preferred_element_type=F32)


def _dot_tn(a, b):
    return lax.dot_general(a, b, (((0,), (0,)), ((), ())), preferred_element_type=F32)


_DOTS = {"nn": _dot_nn, "nt": _dot_nt, "tn": _dot_tn}


def _rstd(v):
    return lax.rsqrt(jnp.mean(v * v, axis=-1, keepdims=True) + EPS)


def _rms_bwd(v, r, t):
    return r * t - v * (r * r * r) * jnp.mean(t * v, axis=-1, keepdims=True)


_TOKEN_SPEC = pl.BlockSpec((8, 128), lambda *_: (0, 0))


def _mm(pairs, mode, out_dtype, tm, tn, name, out_perm=1, after=None):
    a0, b0 = pairs[0]
    m_dim = a0.shape[1] if mode == "tn" else a0.shape[0]
    n_dim = b0.shape[0] if mode == "nt" else b0.shape[1]
    tm, tn = min(tm, m_dim // out_perm), min(tn, n_dim)
    assert m_dim % tm == 0 and n_dim % tn == 0, (name, m_dim, n_dim, tm, tn)
    dot = _DOTS[mode]
    npairs = len(pairs)

    def body(*refs):
        acc = None
        for p in range(npairs):
            d = dot(refs[2 * p][...].astype(BF16), refs[2 * p + 1][...].astype(BF16))
            acc = d if acc is None else acc + d
        refs[-1][...] = acc.astype(out_dtype)

    in_specs, blocks, flat = [], [], []
    for a, b in pairs:
        if mode == "nn":
            k = a.shape[1]
            sa, sb = ((tm, k), lambda i, j: (i, 0)), ((k, tn), lambda i, j: (0, j))
        elif mode == "nt":
            k = a.shape[1]
            sa, sb = ((tm, k), lambda i, j: (i, 0)), ((tn, k), lambda i, j: (j, 0))
        else:
            k = a.shape[0]
            sa, sb = ((k, tm), lambda i, j: (0, i)), ((k, tn), lambda i, j: (0, j))
        in_specs += [pl.BlockSpec(*sa), pl.BlockSpec(*sb)]
        blocks += [(sa[0], a.dtype), (sb[0], b.dtype)]
        flat += [a, b]
    if after is not None:
        in_specs.append(_TOKEN_SPEC)
        flat.append(after)
    if out_perm == 1:
        out_shape = (m_dim, n_dim)
        out_spec = pl.BlockSpec((tm, tn), lambda i, j: (i, j))
    else:
        rows = m_dim // out_perm
        assert tn == n_dim and rows % tm == 0, (name, rows, tm)
        nb = rows // tm
        out_shape = (rows, out_perm * n_dim)
        out_spec = pl.BlockSpec((tm, n_dim), lambda i, j: (i % nb, i // nb))
    blocks.append(((tm, tn), out_dtype))
    res = _pcall(
        body, out_shape=jax.ShapeDtypeStruct(out_shape, out_dtype), grid=(m_dim // tm, n_dim // tn),
        in_specs=in_specs, out_specs=out_spec, name=name,
        compiler_params=_params(("parallel", "parallel"), blocks, extra=2 * tm * tn * 4),
    )(*flat)
    return res.reshape(m_dim, n_dim)


def _prenorm_mm(x, pre_g, scale, shift, w, w_mode, out_dtype, tn, name, perm=1):
    s_dim, d_dim = x.shape
    n_dim = w.shape[0] if w_mode == "nt" else w.shape[1]
    rows = s_dim // perm
    tm = min(TOKEN_TILE, rows)
    nb = rows // tm
    tn = min(tn, n_dim)
    assert n_dim % tn == 0
    dot = _DOTS[w_mode]

    def body(x_ref, g_ref, sc_ref, sh_ref, w_ref, hn_ref, o_ref):
        @pl.when(pl.program_id(1) == 0)
        def _():
            xf = x_ref[...]
            hn = (xf * _rstd(xf) * g_ref[...]) * (1.0 + sc_ref[...]) + sh_ref[...]
            hn_ref[...] = hn.astype(BF16)

        o_ref[...] = dot(hn_ref[...], w_ref[...]).astype(out_dtype)

    vec = pl.BlockSpec((1, d_dim), lambda i, j: (0, 0))
    w_block = (tn, d_dim) if w_mode == "nt" else (d_dim, tn)
    w_spec = pl.BlockSpec(w_block, (lambda i, j: (j, 0)) if w_mode == "nt" else (lambda i, j: (0, j)))
    hn, out = _pcall(
        body,
        out_shape=(jax.ShapeDtypeStruct((s_dim, d_dim), BF16), jax.ShapeDtypeStruct((s_dim, n_dim), out_dtype)),
        grid=(s_dim // tm, n_dim // tn),
        in_specs=[pl.BlockSpec((tm, d_dim), lambda i, j: (i % nb, i // nb)), vec, vec, vec, w_spec],
        out_specs=(pl.BlockSpec((tm, d_dim), lambda i, j: (i, 0)), pl.BlockSpec((tm, tn), lambda i, j: (i, j))),
        name=name,
        compiler_params=_params(("parallel", "arbitrary"),
                                [((tm, d_dim), F32), (w_block, BF16), ((tm, d_dim), BF16), ((tm, tn), out_dtype)],
                                extra=3 * tm * d_dim * 4 + tm * tn * 4),
    )(x.reshape(rows, perm * d_dim), pre_g, scale, shift, w)
    return hn, out


def _ffn_up(x, pre_g, scale, shift, wg_t, wu_t, name):
    s_dim, d_dim = x.shape
    f_dim = wg_t.shape[0]
    tm, tn = TOKEN_TILE, f_dim // 2

    def body(x_ref, g_ref, sc_ref, sh_ref, wg_ref, wu_ref, hn_ref, go_ref, uo_ref, a_ref):
        @pl.when(pl.program_id(1) == 0)
        def _():
            xf = x_ref[...]
            hn = (xf * _rstd(xf) * g_ref[...]) * (1.0 + sc_ref[...]) + sh_ref[...]
            hn_ref[...] = hn.astype(BF16)

        hn = hn_ref[...]
        g = _dot_nt(hn, wg_ref[...])
        u = _dot_nt(hn, wu_ref[...])
        go_ref[...] = g.astype(BF16)
        uo_ref[...] = u.astype(BF16)
        a_ref[...] = (g * jax.nn.sigmoid(g) * u).astype(BF16)

    vec = pl.BlockSpec((1, d_dim), lambda i, j: (0, 0))
    w_spec = pl.BlockSpec((tn, d_dim), lambda i, j: (j, 0))
    act = pl.BlockSpec((tm, tn), lambda i, j: (i, j))
    act_shape = jax.ShapeDtypeStruct((s_dim, f_dim), BF16)
    return _pcall(
        body,
        out_shape=(jax.ShapeDtypeStruct((s_dim, d_dim), BF16), act_shape, act_shape, act_shape),
        grid=(s_dim // tm, f_dim // tn),
        in_specs=[pl.BlockSpec((tm, d_dim), lambda i, j: (i, 0)), vec, vec, vec, w_spec, w_spec],
        out_specs=(pl.BlockSpec((tm, d_dim), lambda i, j: (i, 0)), act, act, act),
        name=name,
        compiler_params=_params(("parallel", "arbitrary"),
                                [((tm, d_dim), F32), ((tn, d_dim), BF16), ((tn, d_dim), BF16), ((tm, d_dim), BF16)]
                                + 3 * [((tm, tn), BF16)], extra=3 * tm * d_dim * 4 + 4 * tm * tn * 4),
    )(x, pre_g, scale, shift, wg_t, wu_t)


def _mm_post(a, w, x, post_g, gate, res_w, name):
    s_dim, k_dim = a.shape
    d_dim = w.shape[1]
    tm = TOKEN_TILE

    def body(a_ref, w_ref, x_ref, pg_ref, gt_ref, xo_ref, f_ref):
        f = _dot_nn(a_ref[...], w_ref[...])
        y = f * _rstd(f) * pg_ref[...]
        f_ref[...] = f
        xo_ref[...] = x_ref[...] + (res_w * gt_ref[...]) * y

    vec = pl.BlockSpec((1, d_dim), lambda i: (0, 0))
    row = pl.BlockSpec((tm, d_dim), lambda i: (i, 0))
    out = jax.ShapeDtypeStruct((s_dim, d_dim), F32)
    return _pcall(
        body, out_shape=(out, out), grid=(s_dim // tm,),
        in_specs=[pl.BlockSpec((tm, k_dim), lambda i: (i, 0)), pl.BlockSpec((k_dim, d_dim), lambda i: (0, 0)), row, vec, vec],
        out_specs=(row, row), name=name,
        compiler_params=_params(("parallel",), [((tm, k_dim), BF16), ((k_dim, d_dim), BF16)] + 3 * [((tm, d_dim), F32)],
                                extra=3 * tm * d_dim * 4),
    )(a, w, x, post_g, gate)


def _post_bwd(dx_out, f, post_g, gate, res_w, name):
    s_dim, d_dim = f.shape
    tm = TOKEN_TILE

    def body(dx_ref, f_ref, pg_ref, gt_ref, df_ref, dgate_ref, dpost_ref):
        @pl.when(pl.program_id(0) == 0)
        def _():
            dgate_ref[...] = jnp.zeros_like(dgate_ref)
            dpost_ref[...] = jnp.zeros_like(dpost_ref)

        dx, fv = dx_ref[...], f_ref[...]
        r = _rstd(fv)
        fr = fv * r
        dgate_ref[...] += res_w * jnp.sum(dx * (fr * pg_ref[...]), axis=0, keepdims=True)
        dy = (res_w * gt_ref[...]) * dx
        dpost_ref[...] += jnp.sum(dy * fr, axis=0, keepdims=True)
        df_ref[...] = _rms_bwd(fv, r, dy * pg_ref[...]).astype(BF16)

    vec = pl.BlockSpec((1, d_dim), lambda i: (0, 0))
    row = pl.BlockSpec((tm, d_dim), lambda i: (i, 0))
    vshape = jax.ShapeDtypeStruct((1, d_dim), F32)
    return _pcall(
        body, out_shape=(jax.ShapeDtypeStruct((s_dim, d_dim), BF16), vshape, vshape), grid=(s_dim // tm,),
        in_specs=[row, row, vec, vec], out_specs=(row, vec, vec), name=name,
        compiler_params=_params(("arbitrary",), 3 * [((tm, d_dim), F32)], extra=6 * tm * d_dim * 4),
    )(dx_out, f, post_g, gate)


def _prenorm_bwd(dx_out, dhns, x, pre_g, scale, name):
    s_dim, d_dim = x.shape
    tm = TOKEN_TILE
    n_in = len(dhns)

    def body(*refs):
        dx_ref, x_ref, pg_ref, sc_ref = refs[n_in + 0], refs[n_in + 1], refs[n_in + 2], refs[n_in + 3]
        dxo_ref, dsh_ref, dsc_ref, dpg_ref = refs[n_in + 4:]

        @pl.when(pl.program_id(0) == 0)
        def _():
            dsh_ref[...] = jnp.zeros_like(dsh_ref)
            dsc_ref[...] = jnp.zeros_like(dsc_ref)
            dpg_ref[...] = jnp.zeros_like(dpg_ref)

        dhn = refs[0][...]
        for k in range(1, n_in):
            dhn = dhn + refs[k][...]
        xv = x_ref[...]
        r = _rstd(xv)
        xr = xv * r
        dsh_ref[...] += jnp.sum(dhn, axis=0, keepdims=True)
        dsc_ref[...] += jnp.sum(dhn * (xr * pg_ref[...]), axis=0, keepdims=True)
        dn = dhn * (1.0 + sc_ref[...])
        dpg_ref[...] += jnp.sum(dn * xr, axis=0, keepdims=True)
        dxo_ref[...] = dx_ref[...] + _rms_bwd(xv, r, dn * pg_ref[...])

    vec = pl.BlockSpec((1, d_dim), lambda i: (0, 0))
    row = pl.BlockSpec((tm, d_dim), lambda i: (i, 0))
    vshape = jax.ShapeDtypeStruct((1, d_dim), F32)
    return _pcall(
        body, out_shape=(jax.ShapeDtypeStruct((s_dim, d_dim), F32), vshape, vshape, vshape), grid=(s_dim // tm,),
        in_specs=n_in * [row] + [row, row, vec, vec], out_specs=(row, vec, vec, vec), name=name,
        compiler_params=_params(("arbitrary",), (n_in + 3) * [((tm, d_dim), F32)], extra=6 * tm * d_dim * 4),
    )(*dhns, dx_out, x, pre_g, scale)


def _ffn_dgu(df, wd, g, u, name, after=None):
    s_dim, d_dim = df.shape
    f_dim = wd.shape[0]
    tm, tn = TOKEN_TILE, f_dim // 2

    def body(df_ref, wd_ref, g_ref, u_ref, *rest):
        dg_ref, du_ref = rest[-2:]
        da = _dot_nt(df_ref[...], wd_ref[...])
        gv, uv = g_ref[...].astype(F32), u_ref[...].astype(F32)
        sg = jax.nn.sigmoid(gv)
        du_ref[...] = (da * (gv * sg)).astype(BF16)
        dg_ref[...] = (da * uv * (sg * (1.0 + gv * (1.0 - sg)))).astype(BF16)

    act = pl.BlockSpec((tm, tn), lambda i, j: (i, j))
    act_shape = jax.ShapeDtypeStruct((s_dim, f_dim), BF16)
    token = [] if after is None else [after]
    return _pcall(
        body, out_shape=(act_shape, act_shape), grid=(s_dim // tm, f_dim // tn),
        in_specs=[pl.BlockSpec((tm, d_dim), lambda i, j: (i, 0)), pl.BlockSpec((tn, d_dim), lambda i, j: (j, 0)), act, act]
        + len(token) * [_TOKEN_SPEC],
        out_specs=(act, act), name=name,
        compiler_params=_params(("parallel", "parallel"), [((tm, d_dim), BF16), ((tn, d_dim), BF16)] + 4 * [((tm, tn), BF16)],
                                extra=6 * tm * tn * 4),
    )(df, wd, g, u, *token)


def _rope_tables():
    half = QK_ROPE // 2
    freqs = ROPE_THETA ** (-jnp.arange(half, dtype=F32) / half)
    ang = jnp.arange(SEQ, dtype=F32)[:, None] * freqs[None, :]
    cos, sin = jnp.cos(ang), jnp.sin(ang)
    ones = jnp.ones((SEQ, QK_NOPE), F32)
    zeros = jnp.zeros((SEQ, QK_NOPE), F32)
    pad1 = jnp.ones((SEQ, HEAD_PAD - QK_NOPE - QK_ROPE), F32)
    pad0 = jnp.zeros((SEQ, HEAD_PAD - QK_NOPE - QK_ROPE), F32)
    zh = jnp.zeros((SEQ, half), F32)
    c = jnp.concatenate([ones, cos, cos, pad1], axis=1)
    s1 = jnp.concatenate([zeros, -sin, zh, pad0], axis=1)
    s2 = jnp.concatenate([zeros, zh, sin, pad0], axis=1)
    return c, s1, s2


def _rope(v, c, s1, s2):
    half = QK_ROPE // 2
    return v * c + pltpu.roll(v, HEAD_PAD - half, 1) * s1 + pltpu.roll(v, half, 1) * s2


def _rope_t(dv, c, s1, s2):
    half = QK_ROPE // 2
    return dv * c + pltpu.roll(dv * s1, half, 1) + pltpu.roll(dv * s2, HEAD_PAD - half, 1)


def _mla_qkv(lat, q_norm, kv_norm, wq_t, wkv_t, rope, name):
    s_dim = lat.shape[0]
    width = MLA_HEADS * HEAD_PAD
    tm = 256

    def body(lat_ref, qg_ref, kg_ref, wq_ref, wkv_ref, c_ref, s1_ref, s2_ref, q_ref, k_ref, v_ref, qn_ref, kvn_ref):
        cq = lat_ref[:, :Q_LORA]
        ckv = lat_ref[:, Q_LORA:Q_LORA + KV_LORA]
        kr = lat_ref[:, Q_LORA + KV_LORA:]
        c, s1, s2 = c_ref[...], s1_ref[...], s2_ref[...]
        qn = (cq * _rstd(cq) * qg_ref[...]).astype(BF16)
        kvn = (ckv * _rstd(ckv) * kg_ref[...]).astype(BF16)
        qn_ref[...] = qn
        kvn_ref[...] = kvn
        q = _dot_nt(qn, wq_ref[...])
        kv = _dot_nt(kvn, wkv_ref[...])
        krr = _rope(kr, c, s1, s2)
        low = lax.broadcasted_iota(jnp.int32, (tm, HEAD_PAD), 1) < QK_NOPE
        for h in range(MLA_HEADS):
            sl = slice(h * HEAD_PAD, (h + 1) * HEAD_PAD)
            q_ref[:, sl] = _rope(q[:, sl], c, s1, s2).astype(BF16)
            kvh = kv[:, sl]
            k_ref[:, sl] = (jnp.where(low, kvh, 0.0) + krr).astype(BF16)
            v_ref[:, sl] = jnp.where(low, 0.0, kvh).astype(BF16)

    row = lambda n: pl.BlockSpec((tm, n), lambda i: (i, 0))
    full = lambda a: pl.BlockSpec(a.shape, lambda i: (0, 0))
    wide = jax.ShapeDtypeStruct((s_dim, width), BF16)
    return _pcall(
        body,
        out_shape=(wide, wide, wide, jax.ShapeDtypeStruct((s_dim, Q_LORA), BF16), jax.ShapeDtypeStruct((s_dim, KV_LORA), BF16)),
        grid=(s_dim // tm,),
        in_specs=[row(LAT_PAD), full(q_norm), full(kv_norm), full(wq_t), full(wkv_t), row(HEAD_PAD), row(HEAD_PAD), row(HEAD_PAD)],
        out_specs=(row(width), row(width), row(width), row(Q_LORA), row(KV_LORA)), name=name,
        compiler_params=_params(("parallel",), [((tm, LAT_PAD), F32), (wq_t.shape, BF16), (wkv_t.shape, BF16)]
                                + 3 * [((tm, width), BF16)], extra=4 * tm * width * 4),
    )(lat, q_norm, kv_norm, wq_t, wkv_t, *rope)


def _mla_probs(q, k, t, tq):
    s = _dot_nt(q, k) * MLA_SCALE
    rows = lax.broadcasted_iota(jnp.int32, s.shape, 0) + t * tq
    cols = lax.broadcasted_iota(jnp.int32, s.shape, 1)
    s = jnp.where(cols <= rows, s, -jnp.inf)
    e = jnp.exp(s - jnp.max(s, axis=-1, keepdims=True))
    return e / jnp.sum(e, axis=-1, keepdims=True)


def _mla_attn_fwd(q, k, v, name):
    s_dim = q.shape[0]
    tq = 512

    def body(q_ref, k_ref, v_ref, o_ref):
        for t in range(s_dim // tq):
            kt = (t + 1) * tq
            p = _mla_probs(q_ref[t * tq:kt, :], k_ref[:kt, :], t, tq)
            o_ref[t * tq:kt, :] = _dot_nn(p.astype(BF16), v_ref[:kt, :]).astype(BF16)

    head = pl.BlockSpec((s_dim, HEAD_PAD), lambda h: (0, h))
    return _pcall(
        body, out_shape=jax.ShapeDtypeStruct(q.shape, BF16), grid=(MLA_HEADS,),
        in_specs=[head, head, head], out_specs=head, name=name,
        compiler_params=_params(("parallel",), 4 * [((s_dim, HEAD_PAD), BF16)], extra=4 * tq * s_dim * 4),
    )(q, k, v)


def _mla_attn_bwd(q, k, v, d_o, name):
    s_dim = q.shape[0]
    tq = 512

    def body(q_ref, k_ref, v_ref, do_ref, dq_ref, dk_ref, dv_ref):
        dk_ref[...] = jnp.zeros_like(dk_ref)
        dv_ref[...] = jnp.zeros_like(dv_ref)
        for t in range(s_dim // tq):
            kt = (t + 1) * tq
            qt = q_ref[t * tq:kt, :]
            dot = do_ref[t * tq:kt, :].astype(BF16)
            p = _mla_probs(qt, k_ref[:kt, :], t, tq)
            dp = _dot_nt(dot, v_ref[:kt, :])
            ds = p * (dp - jnp.sum(p * dp, axis=-1, keepdims=True))
            dsb = (ds * MLA_SCALE).astype(BF16)
            dq_ref[t * tq:kt, :] = _dot_nn(dsb, k_ref[:kt, :])
            dk_ref[:kt, :] += _dot_tn(dsb, qt)
            dv_ref[:kt, :] += _dot_tn(p.astype(BF16), dot)

    head = pl.BlockSpec((s_dim, HEAD_PAD), lambda h: (0, h))
    out = jax.ShapeDtypeStruct(q.shape, F32)
    return _pcall(
        body, out_shape=(out, out, out), grid=(MLA_HEADS,),
        in_specs=[head, head, head, head], out_specs=(head, head, head), name=name,
        compiler_params=_params(("parallel",), 3 * [((s_dim, HEAD_PAD), BF16)] + 4 * [((s_dim, HEAD_PAD), F32)],
                                extra=6 * tq * s_dim * 4),
    )(q, k, v, d_o)


def _mla_qkv_bwd(dq, dk, dv, lat, q_norm, kv_norm, wq_t, wkv_t, rope, name):
    s_dim = lat.shape[0]
    width = MLA_HEADS * HEAD_PAD
    tm = 256

    def body(dq_ref, dk_ref, dv_ref, lat_ref, qg_ref, kg_ref, wq_ref, wkv_ref, c_ref, s1_ref, s2_ref,
             dqp_ref, dkv_ref, dlat_ref, dqg_ref, dkg_ref):
        @pl.when(pl.program_id(0) == 0)
        def _():
            dqg_ref[...] = jnp.zeros_like(dqg_ref)
            dkg_ref[...] = jnp.zeros_like(dkg_ref)

        c, s1, s2 = c_ref[...], s1_ref[...], s2_ref[...]
        lane = lax.broadcasted_iota(jnp.int32, (tm, HEAD_PAD), 1)
        low = lane < QK_NOPE
        rot = (lane >= QK_NOPE) & (lane < QK_NOPE + QK_ROPE)
        dkrr = jnp.zeros((tm, HEAD_PAD), F32)
        for h in range(MLA_HEADS):
            sl = slice(h * HEAD_PAD, (h + 1) * HEAD_PAD)
            dqp_ref[:, sl] = _rope_t(dq_ref[:, sl], c, s1, s2).astype(BF16)
            dkh = dk_ref[:, sl]
            dkv_ref[:, sl] = jnp.where(low, dkh, dv_ref[:, sl]).astype(BF16)
            dkrr = dkrr + jnp.where(rot, dkh, 0.0)
        dqn = _dot_nn(dqp_ref[...], wq_ref[...])
        dkvn = _dot_nn(dkv_ref[...], wkv_ref[...])
        cq = lat_ref[:, :Q_LORA]
        ckv = lat_ref[:, Q_LORA:Q_LORA + KV_LORA]
        rq, rkv = _rstd(cq), _rstd(ckv)
        dqg_ref[...] += jnp.sum(dqn * cq * rq, axis=0, keepdims=True)
        dkg_ref[...] += jnp.sum(dkvn * ckv * rkv, axis=0, keepdims=True)
        dlat_ref[:, :Q_LORA] = _rms_bwd(cq, rq, dqn * qg_ref[...])
        dlat_ref[:, Q_LORA:Q_LORA + KV_LORA] = _rms_bwd(ckv, rkv, dkvn * kg_ref[...])
        dlat_ref[:, Q_LORA + KV_LORA:] = _rope_t(dkrr, c, s1, s2)

    row = lambda n: pl.BlockSpec((tm, n), lambda i: (i, 0))
    full = lambda a: pl.BlockSpec(a.shape, lambda i: (0, 0))
    wide = jax.ShapeDtypeStruct((s_dim, width), BF16)
    return _pcall(
        body,
        out_shape=(wide, wide, jax.ShapeDtypeStruct((s_dim, LAT_PAD), F32),
                   jax.ShapeDtypeStruct(q_norm.shape, F32), jax.ShapeDtypeStruct(kv_norm.shape, F32)),
        grid=(s_dim // tm,),
        in_specs=[row(width), row(width), row(width), row(LAT_PAD), full(q_norm), full(kv_norm), full(wq_t), full(wkv_t),
                  row(HEAD_PAD), row(HEAD_PAD), row(HEAD_PAD)],
        out_specs=(row(width), row(width), row(LAT_PAD), full(q_norm), full(kv_norm)), name=name,
        compiler_params=_params(("arbitrary",), 3 * [((tm, width), F32)] + [((tm, LAT_PAD), F32), (wq_t.shape, BF16),
                                                                           (wkv_t.shape, BF16)] + 2 * [((tm, width), BF16)],
                                extra=2 * tm * width * 4),
    )(dq, dk, dv, lat, q_norm, kv_norm, wq_t, wkv_t, *rope)


def _t5_bucket(dist):
    max_exact = N_BUCKETS // 2
    d = jnp.maximum(dist, 1).astype(F32)
    large = max_exact + (jnp.log(d / max_exact) / math.log(MAX_DISTANCE / max_exact)
                         * (N_BUCKETS - max_exact)).astype(jnp.int32)
    large = jnp.minimum(large, N_BUCKETS - 1)
    return jnp.where(dist < max_exact, dist, large)


def _dil_buckets(dilation):
    iq = jnp.arange(DIL_BLOCK)[:, None]
    ik = jnp.arange(2 * DIL_BLOCK)[None, :]
    return _t5_bucket(jnp.maximum(DIL_BLOCK + iq - ik, 0) * dilation)


def _dil_logits(qh, k_ref, bias_h, n, span):
    lo = n * DIL_BLOCK
    if n == 0:
        s = _dot_nt(qh, k_ref[lo:lo + DIL_BLOCK, :]) * DIL_SCALE + bias_h[:, DIL_BLOCK:]
        rel = lax.broadcasted_iota(jnp.int32, s.shape, 0) - lax.broadcasted_iota(jnp.int32, s.shape, 1)
    else:
        s = _dot_nt(qh, k_ref[lo - DIL_BLOCK:lo + DIL_BLOCK, :]) * DIL_SCALE + bias_h
        rel = DIL_BLOCK + lax.broadcasted_iota(jnp.int32, s.shape, 0) - lax.broadcasted_iota(jnp.int32, s.shape, 1)
    return jnp.where((rel >= 0) & (rel <= span), s, -jnp.inf)


def _dil_views(dilation, rows):
    col = lambda which: pl.BlockSpec((rows, HEAD_PAD), lambda p, r: (r, which * DIL_PAIRS + p))
    nat = pl.BlockSpec((rows, HEAD_PAD), lambda p, r: (0, r * DIL_PAIRS + p))
    bias = pl.BlockSpec((2, DIL_BLOCK, 2 * DIL_BLOCK), lambda p, r: (p, 0, 0))
    return col, nat, bias


def _dil_attn_fwd(qkv, bias, dilation, span, name):
    s_dim = qkv.shape[0]
    rows = s_dim // dilation
    d_dim = DIL_HEADS * DIL_HEAD_DIM
    col, nat, bias_spec = _dil_views(dilation, rows)

    def body(q_ref, k_ref, v_ref, b_ref, o_ref, l_ref):
        lane = lax.broadcasted_iota(jnp.int32, (DIL_BLOCK, HEAD_PAD), 1)
        klane = lax.broadcasted_iota(jnp.int32, (2 * DIL_BLOCK, HEAD_PAD), 1)
        for n in range(rows // DIL_BLOCK):
            lo = n * DIL_BLOCK
            kv_rows = slice(lo, lo + DIL_BLOCK) if n == 0 else slice(lo - DIL_BLOCK, lo + DIL_BLOCK)
            qb, vb = q_ref[lo:lo + DIL_BLOCK, :], v_ref[kv_rows, :]
            o_acc = jnp.zeros((DIL_BLOCK, HEAD_PAD), F32)
            lse_acc = jnp.zeros((DIL_BLOCK, HEAD_PAD), F32)
            for h in range(2):
                mine = (lane < DIL_HEAD_DIM) == (h == 0)
                kmine = (klane[:vb.shape[0]] < DIL_HEAD_DIM) == (h == 0)
                logits = _dil_logits(jnp.where(mine, qb, 0), k_ref, b_ref[h], n, span)
                mx = jnp.max(logits, axis=-1, keepdims=True)
                lse = mx + jnp.log(jnp.sum(jnp.exp(logits - mx), axis=-1, keepdims=True))
                p = jnp.exp(logits - lse)
                o_acc = o_acc + _dot_nn(p.astype(BF16), jnp.where(kmine, vb, 0))
                lse_acc = jnp.where(mine, lse, lse_acc)
            o_ref[lo:lo + DIL_BLOCK, :] = o_acc
            l_ref[lo:lo + DIL_BLOCK, :] = lse_acc

    out = jax.ShapeDtypeStruct((rows, dilation * d_dim), F32)
    o, lse = _pcall(
        body, out_shape=(out, out), grid=(DIL_PAIRS, dilation),
        in_specs=[col(0), col(1), col(2), bias_spec], out_specs=(nat, nat), name=name,
        compiler_params=_params(("parallel", "parallel"), 3 * [((rows, HEAD_PAD), BF16)] + 2 * [((rows, HEAD_PAD), F32)]
                                + [((2, DIL_BLOCK, 2 * DIL_BLOCK), F32)], extra=2**21),
    )(qkv, qkv, qkv, bias)
    return o.reshape(s_dim, d_dim), lse.reshape(s_dim, d_dim)


def _dil_mix(lses, outs, name):
    s_dim, d_dim = outs[0].shape
    tm = TOKEN_TILE
    ng = len(outs)

    def body(*refs):
        ls = [refs[g][...] for g in range(ng)]
        mx = ls[0]
        for g in range(1, ng):
            mx = jnp.maximum(mx, ls[g])
        es = [jnp.exp(l - mx) for l in ls]
        tot = es[0]
        for g in range(1, ng):
            tot = tot + es[g]
        o = None
        for g in range(ng):
            al = es[g] / tot
            refs[2 * ng + g][...] = al
            t = al * refs[ng + g][...]
            o = t if o is None else o + t
        refs[3 * ng][...] = o
        refs[3 * ng + 1][...] = o.astype(BF16)

    row = pl.BlockSpec((tm, d_dim), lambda i: (i, 0))
    f = jax.ShapeDtypeStruct((s_dim, d_dim), F32)
    res = _pcall(
        body, out_shape=tuple(ng * [f] + [f, jax.ShapeDtypeStruct((s_dim, d_dim), BF16)]), grid=(s_dim // tm,),
        in_specs=2 * ng * [row], out_specs=tuple((ng + 2) * [row]), name=name,
        compiler_params=_params(("parallel",), (3 * ng + 2) * [((tm, d_dim), F32)], extra=4 * tm * d_dim * 4),
    )(*lses, *outs)
    return res[:ng], res[ng], res[ng + 1]


def _dil_attn_bwd(qkv, bias, d_o, o_mix, alpha, lse, dilation, span, name):
    s_dim = qkv.shape[0]
    rows = s_dim // dilation
    d_dim = DIL_HEADS * DIL_HEAD_DIM
    col, nat, bias_spec = _dil_views(dilation, rows)
    nat_view = lambda a: a.reshape(rows, dilation * d_dim)

    def body(q_ref, k_ref, v_ref, b_ref, do_ref, om_ref, al_ref, l_ref, dq_ref, dk_ref, dv_ref, db_ref, dk_acc, dv_acc):
        @pl.when(pl.program_id(1) == 0)
        def _():
            db_ref[...] = jnp.zeros_like(db_ref)

        dk_acc[...] = jnp.zeros_like(dk_acc)
        dv_acc[...] = jnp.zeros_like(dv_acc)
        lane = lax.broadcasted_iota(jnp.int32, (DIL_BLOCK, HEAD_PAD), 1)
        klane = lax.broadcasted_iota(jnp.int32, (2 * DIL_BLOCK, HEAD_PAD), 1)
        for n in range(rows // DIL_BLOCK):
            lo = n * DIL_BLOCK
            blk = slice(lo, lo + DIL_BLOCK)
            kv_rows = blk if n == 0 else slice(lo - DIL_BLOCK, lo + DIL_BLOCK)
            qb, kb, vb = q_ref[blk, :], k_ref[kv_rows, :], v_ref[kv_rows, :]
            al = al_ref[blk, :]
            dog = al * do_ref[blk, :]
            row_term = dog * om_ref[blk, :]
            lse_b = l_ref[blk, :]
            dq_acc = jnp.zeros((DIL_BLOCK, HEAD_PAD), F32)
            dk_blk = jnp.zeros((kb.shape[0], HEAD_PAD), F32)
            dv_blk = jnp.zeros((kb.shape[0], HEAD_PAD), F32)
            for h in range(2):
                mine = (lane < DIL_HEAD_DIM) == (h == 0)
                kmine = (klane[:kb.shape[0]] < DIL_HEAD_DIM) == (h == 0)
                qh = jnp.where(mine, qb, 0)
                logits = _dil_logits(qh, k_ref, b_ref[h], n, span)
                lse_h = jnp.max(jnp.where(mine, lse_b, -jnp.inf), axis=-1, keepdims=True)
                p = jnp.exp(logits - lse_h)
                dogh = jnp.where(mine, dog, 0.0).astype(BF16)
                dp = _dot_nt(dogh, vb)
                ds = p * (dp - jnp.sum(jnp.where(mine, row_term, 0.0), axis=-1, keepdims=True))
                if n == 0:
                    db_ref[h, :, DIL_BLOCK:] += ds
                else:
                    db_ref[h] += ds
                dsb = (ds * DIL_SCALE).astype(BF16)
                dq_acc = dq_acc + _dot_nn(dsb, jnp.where(kmine, kb, 0))
                dk_blk = dk_blk + _dot_tn(dsb, qh)
                dv_blk = dv_blk + _dot_tn(p.astype(BF16), dogh)
            dq_ref[blk, :] = dq_acc.astype(BF16)
            dk_acc[kv_rows, :] += dk_blk
            dv_acc[kv_rows, :] += dv_blk
        dk_ref[...] = dk_acc[...].astype(BF16)
        dv_ref[...] = dv_acc[...].astype(BF16)

    out_col = pl.BlockSpec((rows, HEAD_PAD), lambda p, r: (r, p))
    grad = jax.ShapeDtypeStruct((s_dim, d_dim), BF16)
    return _pcall(
        body, out_shape=(grad, grad, grad, jax.ShapeDtypeStruct(bias.shape, F32)), grid=(DIL_PAIRS, dilation),
        in_specs=[col(0), col(1), col(2), bias_spec, nat, nat, nat, nat],
        out_specs=(out_col, out_col, out_col, bias_spec), name=name,
        scratch_shapes=[pltpu.VMEM((rows, HEAD_PAD), F32), pltpu.VMEM((rows, HEAD_PAD), F32)],
        compiler_params=_params(("parallel", "arbitrary"), 6 * [((rows, HEAD_PAD), BF16)] + 4 * [((rows, HEAD_PAD), F32)]
                                + 2 * [((2, DIL_BLOCK, 2 * DIL_BLOCK), F32)], extra=2 * rows * HEAD_PAD * 4 + 2**21),
    )(qkv, qkv, qkv, bias, nat_view(d_o), nat_view(o_mix), nat_view(alpha), nat_view(lse))


def _bias_reduce(dbias, buckets, name):
    n_heads = dbias.shape[0]

    def body(db_ref, bk_ref, o_ref):
        ds, bk = db_ref[0], bk_ref[0]
        lane = lax.broadcasted_iota(jnp.int32, (8, HEAD_PAD), 1)
        acc = jnp.zeros((8, HEAD_PAD), F32)
        for b in range(N_BUCKETS):
            acc = jnp.where(lane == b, jnp.sum(jnp.where(bk == b, ds, 0.0)), acc)
        o_ref[0] = acc

    blk = (1, DIL_BLOCK, 2 * DIL_BLOCK)
    return _pcall(
        body, out_shape=jax.ShapeDtypeStruct((n_heads, 8, HEAD_PAD), F32), grid=(n_heads,),
        in_specs=[pl.BlockSpec(blk, lambda h: (h, 0, 0)), pl.BlockSpec(blk, lambda h: (h // DIL_HEADS, 0, 0))],
        out_specs=pl.BlockSpec((1, 8, HEAD_PAD), lambda h: (h, 0, 0)), name=name,
        compiler_params=_params(("parallel",), [(blk, F32), (blk, jnp.int32)], extra=2**20),
    )(dbias, buckets)


def _loss_grad(y, target, name):
    s_dim, d_dim = y.shape
    tm = TOKEN_TILE

    def body(y_ref, t_ref, dy_ref, l_ref):
        @pl.when(pl.program_id(0) == 0)
        def _():
            l_ref[...] = jnp.zeros_like(l_ref)

        err = y_ref[...] - t_ref[...]
        dy_ref[...] = err / d_dim
        sq = (err * err).reshape(tm // 8, 8, d_dim)
        l_ref[...] += 0.5 * jnp.sum(sq, axis=0) / d_dim

    row = pl.BlockSpec((tm, d_dim), lambda i: (i, 0))
    acc = pl.BlockSpec((8, d_dim), lambda i: (0, 0))
    return _pcall(
        body, out_shape=(jax.ShapeDtypeStruct((s_dim, d_dim), F32), jax.ShapeDtypeStruct((8, d_dim), F32)),
        grid=(s_dim // tm,), in_specs=[row, row], out_specs=(row, acc), name=name,
        compiler_params=_params(("arbitrary",), 3 * [((tm, d_dim), F32)], extra=2 * tm * d_dim * 4),
    )(y, target)


def _mod_fwd(c_all, w_mod, b_loc, name):
    depth, d_dim, n = w_mod.shape
    nb = c_all.shape[0]

    def body(c_ref, w_ref, b_ref, o_ref, s_ref):
        cv = c_ref[...]
        sc = cv * jax.nn.sigmoid(cv)
        s_ref[...] = sc
        o_ref[0] = _dot_nn(sc.astype(BF16), w_ref[0].astype(BF16)) + b_ref[0]

    return _pcall(
        body, out_shape=(jax.ShapeDtypeStruct((depth, nb, n), F32), jax.ShapeDtypeStruct((nb, d_dim), F32)), grid=(depth,),
        in_specs=[pl.BlockSpec((nb, d_dim), lambda i: (0, 0)), pl.BlockSpec((1, d_dim, n), lambda i: (i, 0, 0)),
                  pl.BlockSpec((1, 1, n), lambda i: (i, 0, 0))],
        out_specs=(pl.BlockSpec((1, nb, n), lambda i: (i, 0, 0)), pl.BlockSpec((nb, d_dim), lambda i: (0, 0))), name=name,
        compiler_params=_params(("arbitrary",), [((1, d_dim, n), F32)], extra=d_dim * n * 2 + 2**20),
    )(c_all, w_mod, b_loc.reshape(depth, 1, n))


def _sum_parts(parts, name):
    _, rows, cols = parts.shape
    tr = rows
    for cand in (512, 384, 256, 128, 64, 32, 16):
        if rows % cand == 0 and rows > cand:
            tr = cand
            break

    def body(p_ref, o_ref):
        acc = p_ref[0].astype(F32)
        for k in range(1, NDEV):
            acc = acc + p_ref[k].astype(F32)
        o_ref[...] = acc

    return _pcall(
        body, out_shape=jax.ShapeDtypeStruct((rows, cols), F32), grid=(rows // tr,),
        in_specs=[pl.BlockSpec((NDEV, tr, cols), lambda i: (0, i, 0))], out_specs=pl.BlockSpec((tr, cols), lambda i: (i, 0)),
        name=name, compiler_params=_params(("parallel",), [((NDEV, tr, cols), parts.dtype), ((tr, cols), F32)], extra=2**20),
    )(parts)


def _adamw(w, g, m, v, name):
    shape = w.shape
    cols = shape[-1]
    rows = math.prod(shape[:-1])
    tr = rows
    for cand in (512, 256, 128, 64, 32, 16, 8):
        if rows % cand == 0 and rows > cand and cand * cols * 4 <= 2**21:
            tr = cand
            break

    def body(w_ref, g_ref, m_ref, v_ref, d_ref, mo_ref, vo_ref):
        gv = g_ref[...]
        mn = ADAM_B1 * m_ref[...] + (1.0 - ADAM_B1) * gv
        vn = ADAM_B2 * v_ref[...] + (1.0 - ADAM_B2) * (gv * gv)
        m_hat = mn / (1.0 - ADAM_B1 ** ADAM_STEP)
        v_hat = vn / (1.0 - ADAM_B2 ** ADAM_STEP)
        d_ref[...] = -ADAM_LR * (m_hat / (jnp.sqrt(v_hat) + ADAM_EPS) + ADAM_WD * w_ref[...])
        mo_ref[...] = mn
        vo_ref[...] = vn

    blk = pl.BlockSpec((tr, cols), lambda i: (i, 0))
    out = jax.ShapeDtypeStruct((rows, cols), F32)
    res = _pcall(
        body, out_shape=(out, out, out), grid=(rows // tr,), in_specs=4 * [blk], out_specs=(blk, blk, blk), name=name,
        compiler_params=_params(("parallel",), 7 * [((tr, cols), F32)], extra=4 * tr * cols * 4),
    )(*(a.reshape(rows, cols) for a in (w, g, m, v)))
    return tuple(r.reshape(shape) for r in res)


def _peers():
    x, y, c = lax.axis_index("x"), lax.axis_index("y"), lax.axis_index("c")
    flip = lambda v, f: 1 - v if f else v
    peers = []
    for f in range(1, NDEV):
        px, py, pc = flip(x, f & 4), flip(y, f & 2), flip(c, f & 1)
        peers.append(((px, py, pc), 4 * px + 2 * py + pc))
    return (x, y, c), 4 * x + 2 * y + c, peers


def _places():
    x, y, c = lax.axis_index("x"), lax.axis_index("y"), lax.axis_index("c")
    place = lambda px, py, pc: ((px, py, pc), 4 * px + 2 * py + pc)
    return place(x, y, c), place(x, y, 1 - c), [place(1 - x, y, c), place(x, 1 - y, c), place(1 - x, 1 - y, c)]


def _exchange(arrs, gather, name):
    n = len(arrs)
    hbm = pl.BlockSpec(memory_space=pltpu.HBM)
    if gather:
        out_shape = [jax.ShapeDtypeStruct((NDEV * a.shape[0], a.shape[1]), a.dtype) for a in arrs]
    else:
        out_shape = [jax.ShapeDtypeStruct((NDEV, a.shape[0] // NDEV, a.shape[1]), a.dtype) for a in arrs]

    def body(*refs):
        ins, outs = refs[:n], refs[n:2 * n]
        send_sems, recv_sems, local_sems = refs[2 * n:]
        me_pos, me, peers = _peers()
        local = []
        for k in range(n):
            rows = arrs[k].shape[0] if gather else arrs[k].shape[0] // NDEV
            if gather:
                src_of = lambda idx: ins[k]
                dst_of = lambda idx: outs[k].at[pl.ds(me * rows, rows)]
                mine = (ins[k], outs[k].at[pl.ds(me * rows, rows)])
            else:
                src_of = lambda idx: ins[k].at[pl.ds(idx * rows, rows)]
                dst_of = lambda idx: outs[k].at[me]
                mine = (ins[k].at[pl.ds(me * rows, rows)], outs[k].at[me])
            cp = pltpu.make_async_copy(mine[0], mine[1], local_sems.at[k])
            cp.start()
            local.append(cp)
            for pos, idx in peers:
                pltpu.make_async_remote_copy(src_ref=src_of(idx), dst_ref=dst_of(idx), send_sem=send_sems.at[k],
                                             recv_sem=recv_sems.at[k], device_id=pos, device_id_type=MESH).start()
        for k in range(n):
            rows = arrs[k].shape[0] if gather else arrs[k].shape[0] // NDEV
            sent = ins[k].at[pl.ds(0, (NDEV - 1) * rows)] if not gather else outs[k].at[pl.ds(0, (NDEV - 1) * rows)]
            got = outs[k].at[pl.ds(0, (NDEV - 1) * rows)] if gather else outs[k].at[pl.ds(0, NDEV - 1)]
            pltpu.make_async_remote_copy(src_ref=sent, dst_ref=sent, send_sem=send_sems.at[k], recv_sem=recv_sems.at[k],
                                         device_id=me_pos, device_id_type=MESH).wait_send()
            pltpu.make_async_remote_copy(src_ref=got, dst_ref=got, send_sem=send_sems.at[k], recv_sem=recv_sems.at[k],
                                         device_id=me_pos, device_id_type=MESH).wait_recv()
            local[k].wait()

    return pl.pallas_call(
        body, out_shape=out_shape, in_specs=n * [hbm], out_specs=n * [hbm], name=name,
        scratch_shapes=[pltpu.SemaphoreType.DMA((n,)), pltpu.SemaphoreType.DMA((n,)), pltpu.SemaphoreType.DMA((n,))],
        compiler_params=pltpu.CompilerParams(has_side_effects=True),
    )(*arrs)


_HBM = pl.BlockSpec(memory_space=pltpu.HBM)
_SEM = pl.BlockSpec(memory_space=pltpu.SEMAPHORE)
_DATAFLOW = pltpu.SideEffectType.DATAFLOW_SIDE_EFFECTING


def _split_start(srcs, groups, gather, name):
    n = len(srcs)
    if gather:
        lands = [lax.empty((NDEV * a.shape[0], a.shape[1]), a.dtype) for a in srcs]
    else:
        lands = [lax.empty((NDEV, a.shape[0] // NDEV, a.shape[1]), a.dtype) for a in srcs]
    n_sem = 3 * len(groups)

    def body(*refs):
        src_refs, land_refs = refs[:n], refs[n:2 * n]
        sems = refs[2 * n:2 * n + n_sem]
        token = refs[-1]
        (_, my), sibling, chips = _places()
        _, _, peers = _peers()
        targets = [sibling] + chips if gather else peers
        for g, members in enumerate(groups):
            for j, k in enumerate(members):
                _own_copy(src_refs[k], land_refs[k], sems[3 * g + 2].at[j], my, gather).start()
        for g, members in enumerate(groups):
            for j, k in enumerate(members):
                rows = srcs[k].shape[0] if gather else srcs[k].shape[0] // NDEV
                for pos, idx in targets:
                    src = src_refs[k] if gather else src_refs[k].at[pl.ds(idx * rows, rows)]
                    dst = land_refs[k].at[pl.ds(my * rows, rows)] if gather else land_refs[k].at[my]
                    pltpu.make_async_remote_copy(src_ref=src, dst_ref=dst, send_sem=sems[3 * g].at[j],
                                                 recv_sem=sems[3 * g + 1].at[j], device_id=pos, device_id_type=MESH).start()
        token[...] = jnp.zeros_like(token)

    out_shape = []
    for members in groups:
        out_shape += 3 * [pltpu.SemaphoreType.DMA((len(members),))]
    out_shape += [pltpu.HBM(a.shape, a.dtype) for a in srcs] + [pltpu.HBM(a.shape, a.dtype) for a in lands]
    out_shape.append(jax.ShapeDtypeStruct((8, 128), F32))
    res = pl.pallas_call(
        body, name=name, out_shape=tuple(out_shape), in_specs=2 * n * [_HBM],
        out_specs=tuple(n_sem * [_SEM] + 2 * n * [_HBM] + [pl.BlockSpec(memory_space=pltpu.VMEM)]),
        input_output_aliases={i: n_sem + i for i in range(2 * n)},
        compiler_params=pltpu.CompilerParams(has_side_effects=_DATAFLOW),
    )(*[pltpu.with_memory_space_constraint(a, pltpu.HBM) for a in list(srcs) + lands])
    sems = [tuple(res[3 * g:3 * g + 3]) for g in range(len(groups))]
    return sems, list(res[n_sem:n_sem + n]), list(res[n_sem + n:n_sem + 2 * n]), res[-1]


def _own_copy(src_ref, land_ref, sem, my, gather):
    if gather:
        rows = src_ref.shape[0]
        return pltpu.make_async_copy(src_ref, land_ref.at[pl.ds(my * rows, rows)], sem)
    rows = src_ref.shape[0] // NDEV
    return pltpu.make_async_copy(src_ref.at[pl.ds(my * rows, rows)], land_ref.at[my], sem)


def _wait_all(land_ref, blocks_per_dev, copies, send_sem, recv_sem, me_pos):
    part = land_ref.at[pl.ds(0, copies * blocks_per_dev)]
    pltpu.make_async_remote_copy(src_ref=part, dst_ref=part, send_sem=send_sem, recv_sem=recv_sem,
                                 device_id=me_pos, device_id_type=MESH).wait()


def _gather_forward(sems, srcs, lands, after, name):
    n = len(srcs)

    def body(*refs):
        land_refs = refs[n:2 * n]
        send_a, recv_a = refs[2 * n], refs[2 * n + 1]
        send_b, recv_b = refs[2 * n + 3], refs[2 * n + 4]
        token = refs[-1]
        (me_pos, _), sibling, chips = _places()
        for j in range(n):
            _wait_all(land_refs[j], lands[j].shape[0] // NDEV, 1 + OTHER_CHIPS, send_a.at[j], recv_a.at[j], me_pos)
        for j in range(n):
            rows = lands[j].shape[0] // NDEV
            for _, idx in chips:
                block = land_refs[j].at[pl.ds(idx * rows, rows)]
                pltpu.make_async_remote_copy(src_ref=block, dst_ref=block, send_sem=send_b.at[j], recv_sem=recv_b.at[j],
                                             device_id=sibling[0], device_id_type=MESH).start()
        token[...] = jnp.zeros_like(token)

    res = pl.pallas_call(
        body, name=name,
        out_shape=(pltpu.SemaphoreType.DMA((n,)), pltpu.SemaphoreType.DMA((n,)))
        + tuple(pltpu.HBM(a.shape, a.dtype) for a in list(srcs) + list(lands)) + (jax.ShapeDtypeStruct((8, 128), F32),),
        in_specs=2 * n * [_HBM] + [_SEM, _SEM, pl.BlockSpec(memory_space=pl.ANY)],
        out_specs=tuple([_SEM, _SEM] + 2 * n * [_HBM] + [pl.BlockSpec(memory_space=pltpu.VMEM)]),
        input_output_aliases={i: 2 + i for i in range(2 * n)},
        compiler_params=pltpu.CompilerParams(has_side_effects=_DATAFLOW),
    )(*srcs, *lands, sems[0], sems[1], after)
    return (res[0], res[1]), list(res[2:2 + n]), list(res[2 + n:2 + 2 * n]), res[-1]


def _split_wait(sems, srcs, lands, after, copies, gather, name):
    n = len(srcs)

    def body(*refs):
        src_refs, land_refs = refs[:n], refs[n:2 * n]
        send_sem, recv_sem, local_sem = refs[2 * n], refs[2 * n + 1], refs[2 * n + 2]
        (me_pos, my), _, _ = _places()
        for j in range(n):
            _wait_all(land_refs[j], lands[j].shape[0] // NDEV, copies, send_sem.at[j], recv_sem.at[j], me_pos)
            _own_copy(src_refs[j], land_refs[j], local_sem.at[j], my, gather).wait()

    res = pl.pallas_call(
        body, name=name, out_shape=tuple(pltpu.HBM(a.shape, a.dtype) for a in list(srcs) + list(lands)),
        in_specs=2 * n * [_HBM] + [_SEM, _SEM, _SEM, pl.BlockSpec(memory_space=pl.ANY)], out_specs=tuple(2 * n * [_HBM]),
        input_output_aliases={i: i for i in range(2 * n)},
        compiler_params=pltpu.CompilerParams(has_side_effects=_DATAFLOW),
    )(*srcs, *lands, sems[0], sems[1], sems[2], after)
    return list(res[n:])


def _chained(gate, mid, after):
    return gate if mid is None else gate + mid(after)[:1, :1]


def _ffn_fwd(x, norms, mod, w, mid=None):
    (pre_g, post_g), (shift, scale, gate), (wg_t, wu_t, wd) = norms, mod, w
    hn, g, u, a = _ffn_up(x, pre_g, scale, shift, wg_t, wu_t, "ffn_up")
    x_out, f = _mm_post(a, wd, x, post_g, _chained(gate, mid, a), FFN_RES, "ffn_down")
    return x_out, (x, hn, g, u, a, f)


def _ffn_bwd(dx_out, saved, norms, mod, w, send=None):
    (pre_g, post_g), (_, scale, gate), (wg_t, wu_t, wd) = norms, mod, w
    x, hn, g, u, a, f = saved
    d_model = x.shape[1]
    sent = (lambda j, dw: None) if send is None else send
    df, dgate, dpost = _post_bwd(dx_out, f, post_g, gate, FFN_RES, "ffn_post_bwd")
    dwd = _mm([(a, df)], "tn", BF16, 256, d_model, "ffn_dw")
    dg, du = _ffn_dgu(df, wd, g, u, "ffn_dgu", after=sent(2, dwd))
    dwg_t = _mm([(dg, hn)], "tn", BF16, 256, d_model, "ffn_dw")
    dwu_t = _mm([(du, hn)], "tn", BF16, 256, d_model, "ffn_dw", after=sent(0, dwg_t))
    dhn = _mm([(dg, wg_t), (du, wu_t)], "nn", F32, TOKEN_TILE, d_model, "ffn_dhn", after=sent(1, dwu_t))
    dx, dshift, dscale, dpre = _prenorm_bwd(dx_out, [dhn], x, pre_g, scale, "prenorm_bwd")
    return dx, (dpre, dpost), (dshift, dscale, dgate), (dwg_t, dwu_t, dwd)


def _mla_fwd(x, norms, mod, w, rope, mid=None):
    (pre_g, post_g), (shift, scale, gate) = norms, mod
    w_in, q_norm, wq_t, kv_norm, wkv_t, wo = w
    hn, lat = _prenorm_mm(x, pre_g, scale, shift, w_in, "nn", F32, LAT_PAD, "mla_in")
    gate = _chained(gate, mid, lat)
    q, k, v, qn, kvn = _mla_qkv(lat, q_norm, kv_norm, wq_t, wkv_t, rope, "mla_qkv")
    o = _mla_attn_fwd(q, k, v, "mla_attn_fwd")
    x_out, f = _mm_post(o, wo, x, post_g, gate, 1.0, "mla_out")
    return x_out, (x, hn, lat, q, k, v, qn, kvn, o, f)


def _mla_bwd(dx_out, saved, norms, mod, w, rope):
    (pre_g, post_g), (_, scale, gate) = norms, mod
    w_in, q_norm, wq_t, kv_norm, wkv_t, wo = w
    x, hn, lat, q, k, v, qn, kvn, o, f = saved
    d_model = x.shape[1]
    df, dgate, dpost = _post_bwd(dx_out, f, post_g, gate, 1.0, "mix_post_bwd")
    d_o = _mm([(df, wo)], "nt", F32, TOKEN_TILE, wo.shape[0], "mla_do")
    dwo = _mm([(o, df)], "tn", BF16, TOKEN_TILE, d_model, "mla_dwo")
    dq, dk, dv = _mla_attn_bwd(q, k, v, d_o, "mla_attn_bwd")
    dqp, dkv, dlat, dq_norm, dkv_norm = _mla_qkv_bwd(dq, dk, dv, lat, q_norm, kv_norm, wq_t, wkv_t, rope, "mla_qkv_bwd")
    dwq_t = _mm([(dqp, qn)], "tn", BF16, TOKEN_TILE, Q_LORA, "mla_dwq")
    dwkv_t = _mm([(dkv, kvn)], "tn", BF16, TOKEN_TILE, KV_LORA, "mla_dwkv")
    dw_in = _mm([(hn, dlat)], "tn", BF16, TOKEN_TILE, LAT_PAD, "mla_dwin")
    dhn = _mm([(dlat, w_in)], "nt", F32, TOKEN_TILE, d_model, "mla_dhn")
    dx, dshift, dscale, dpre = _prenorm_bwd(dx_out, [dhn], x, pre_g, scale, "prenorm_bwd")
    return dx, (dpre, dpost), (dshift, dscale, dgate), (dw_in, dq_norm, dwq_t, dkv_norm, dwkv_t, dwo)


def _dil_fwd(x, norms, mod, w, bias, mid=None):
    (pre_g, post_g), (shift, scale, gate), (w_in_t, wo) = norms, mod, w
    width = 3 * DIL_HEADS * DIL_HEAD_DIM
    hns, qkvs, outs, lses = [], [], [], []
    for g, (window, dilation) in enumerate(DIL_GROUPS):
        hn, qkv = _prenorm_mm(x, pre_g, scale, shift, w_in_t[g * width:(g + 1) * width], "nt", BF16, width,
                              "dil_in", perm=dilation)
        if g == 0:
            gate = _chained(gate, mid, qkv)
        o, lse = _dil_attn_fwd(qkv, bias[g], dilation, window // dilation, "dil_attn_fwd")
        hns.append(hn), qkvs.append(qkv), outs.append(o), lses.append(lse)
    alphas, o_mix, o_mix_b = _dil_mix(lses, outs, "dil_mix")
    x_out, f = _mm_post(o_mix_b, wo, x, post_g, gate, 1.0, "dil_out")
    return x_out, (x, hns, qkvs, lses, alphas, o_mix, o_mix_b, f)


def _dil_bwd(dx_out, saved, norms, mod, w, bias):
    (pre_g, post_g), (_, scale, gate), (w_in_t, wo) = norms, mod, w
    x, hns, qkvs, lses, alphas, o_mix, o_mix_b, f = saved
    d_model = x.shape[1]
    inner = DIL_HEADS * DIL_HEAD_DIM
    df, dgate, dpost = _post_bwd(dx_out, f, post_g, gate, 1.0, "mix_post_bwd")
    d_o = _mm([(df, wo)], "nt", F32, TOKEN_TILE, inner, "dil_do")
    dwo = _mm([(o_mix_b, df)], "tn", BF16, TOKEN_TILE, d_model, "dil_dwo")
    dhns, dws, dbs = [], [], []
    for g, (window, dilation) in enumerate(DIL_GROUPS):
        grads = _dil_attn_bwd(qkvs[g], bias[g], d_o, o_mix, alphas[g], lses[g], dilation, window // dilation, "dil_attn_bwd")
        dbs.append(grads[3])
        w_parts = [w_in_t[(3 * g + j) * inner:(3 * g + j + 1) * inner] for j in range(3)]
        dhns.append(_mm(list(zip(grads[:3], w_parts)), "nn", F32, TOKEN_TILE, d_model, "dil_dhn", out_perm=dilation))
        dws += [_mm([(grads[j], hns[g])], "tn", BF16, TOKEN_TILE, d_model, "dil_dwin") for j in range(3)]
    dx, dshift, dscale, dpre = _prenorm_bwd(dx_out, dhns, x, pre_g, scale, "prenorm_bwd3")
    return dx, (dpre, dpost), (dshift, dscale, dgate), (jnp.concatenate(dws, axis=0), dwo), jnp.concatenate(dbs, axis=0)


def _pad_rows(a, rows):
    return jnp.pad(a, ((0, rows - a.shape[0]), (0, 0)))


def _lanes(a):
    flat = a.reshape(-1).astype(F32)
    rows = -(-flat.shape[0] // 1024) * 8
    return jnp.pad(flat, (0, rows * 128 - flat.shape[0])).reshape(rows, 128)


def kernel(x, c, norm_pre, norm_post, w_mod, b_mod, ffn_w_gate, ffn_w_up, ffn_w_down, mla_w_in, mla_q_norm, mla_w_q_up, mla_kv_norm, mla_w_kv_up, mla_w_o, dil_w_in, dil_w_o, rel_bias, loss_target, m_norm_pre, m_norm_post, m_w_mod, m_b_mod, m_ffn_w_gate, m_ffn_w_up, m_ffn_w_down, m_mla_w_in, m_mla_q_norm, m_mla_w_q_up, m_mla_kv_norm, m_mla_w_kv_up, m_mla_w_o, m_dil_w_in, m_dil_w_o, m_rel_bias, v_norm_pre, v_norm_post, v_w_mod, v_b_mod, v_ffn_w_gate, v_ffn_w_up, v_ffn_w_down, v_mla_w_in, v_mla_q_norm, v_mla_w_q_up, v_mla_kv_norm, v_mla_w_kv_up, v_mla_w_o, v_dil_w_in, v_dil_w_o, v_rel_bias):
    me = 4 * lax.axis_index("x") + 2 * lax.axis_index("y") + lax.axis_index("c")
    depth, n_sub, d_loc = norm_pre.shape
    d_model = x.shape[2]
    mod_loc_cols = w_mod.shape[2]
    x0, target = x[0], loss_target[0]

    bf_t = lambda a: a.astype(BF16).T
    ffn_ids = [(i, h) for i in range(depth) for h in range(2)]
    shards = []
    for i, h in ffn_ids:
        shards += [bf_t(ffn_w_gate[i, h]), bf_t(ffn_w_up[i, h]), ffn_w_down[i, h].astype(BF16)]
    shards += [mla_w_in[0].astype(BF16), bf_t(mla_w_q_up[0]), bf_t(mla_w_kv_up[0]), mla_w_o[0].astype(BF16),
               bf_t(dil_w_in[0]), dil_w_o[0].astype(BF16)]
    n_ffn = 3 * len(ffn_ids)
    members = {(0, 0): [0, 1, 2], (0, 1): [n_ffn, n_ffn + 1, n_ffn + 2, n_ffn + 3], (0, 2): [3, 4, 5],
               (1, 0): [6, 7, 8], (1, 1): [n_ffn + 4, n_ffn + 5], (1, 2): [9, 10, 11]}
    order = [(i, s) for i in range(depth) for s in range(n_sub)]
    g_sems, g_srcs, g_lands, g_token = _split_start(shards, [members[k] for k in order], True, "gather_weights_start")

    small = jnp.concatenate([c.reshape(8, 128), _pad_rows(norm_pre.reshape(depth * n_sub, d_loc), 8),
                             _pad_rows(norm_post.reshape(depth * n_sub, d_loc), 8)], axis=0) + g_token[:1, :1]
    small_all = _exchange([small], True, "gather_small")[0].reshape(NDEV, 24, 128)
    c_all = small_all[:, 0:8].reshape(NDEV, d_model)
    gains = lambda lo: jnp.transpose(small_all[:, lo:lo + depth * n_sub], (1, 0, 2)).reshape(depth, n_sub, 1, d_model)
    pre_full, post_full = gains(8), gains(16)

    b_loc = lax.dynamic_slice(b_mod, (0, me * mod_loc_cols), (depth, mod_loc_cols))
    mod_cols, silu_c = _mod_fwd(c_all, w_mod, b_loc, "mod_fwd")
    mod_all = _exchange([mod_cols.reshape(depth * NDEV, mod_loc_cols)], True, "gather_mod")[0]
    mod_all = mod_all.reshape(NDEV, depth, NDEV, mod_loc_cols)
    mod_mine = lax.dynamic_index_in_dim(mod_all, me, axis=2, keepdims=False)
    mod = jnp.transpose(mod_mine, (1, 0, 2)).reshape(depth, n_sub, 3, 1, d_model)

    forwarded = {}

    def forward(key, after):
        idx = members[key]
        forwarded[key] = _gather_forward(g_sems[order.index(key)], [g_srcs[k] for k in idx], [g_lands[k] for k in idx], after,
                                         "gather_forward_%d%d" % key)
        return forwarded[key][3]

    def weights_of(key, after):
        (send_b, recv_b), srcs, lands, _ = forwarded[key]
        local = g_sems[order.index(key)][2]
        return _split_wait((send_b, recv_b, local), srcs, lands, after, OTHER_CHIPS, True, "gather_wait_%d%d" % key)

    lat_real = Q_LORA + KV_LORA
    qk = QK_NOPE + QK_ROPE

    def mla_weights(after):
        w_in, wq_t, wkv_t, wo = weights_of((0, 1), after)
        w_in_pad = jnp.concatenate([w_in[:, :lat_real], jnp.zeros((d_model, QK_NOPE), BF16), w_in[:, lat_real:],
                                    jnp.zeros((d_model, HEAD_PAD - QK_NOPE - QK_ROPE), BF16)], axis=1)
        wq_pad = jnp.pad(wq_t.reshape(MLA_HEADS, qk, Q_LORA), ((0, 0), (0, HEAD_PAD - qk), (0, 0)))
        wo_pad = jnp.pad(wo.reshape(MLA_HEADS, V_HEAD, d_model), ((0, 0), (HEAD_PAD - V_HEAD, 0), (0, 0)))
        return (w_in_pad, mla_q_norm, wq_pad.reshape(MLA_HEADS * HEAD_PAD, Q_LORA), mla_kv_norm, wkv_t,
                wo_pad.reshape(MLA_HEADS * HEAD_PAD, d_model))

    rope = _rope_tables()
    buckets = jnp.stack([_dil_buckets(dil) for _, dil in DIL_GROUPS])
    onehot = (buckets[..., None] == jnp.arange(N_BUCKETS)).astype(F32)
    bias = jnp.einsum("gqkb,bgh->ghqk", onehot, rel_bias.reshape(N_BUCKETS, len(DIL_GROUPS), DIL_HEADS),
                      precision=lax.Precision.HIGHEST)

    norms = lambda i, s: (pre_full[i, s], post_full[i, s])
    mods = lambda i, s: (mod[i, s, 0], mod[i, s, 1], mod[i, s, 2])
    saved, weights = {}, {}
    h = x0
    forward(order[0], h)
    for n, (i, s) in enumerate(order):
        weights[i, s] = mla_weights(h) if (s == 1 and i % 2 == 0) else tuple(weights_of((i, s), h))
        mid = None if n + 1 == len(order) else (lambda after, nxt=order[n + 1]: forward(nxt, after))
        if s != 1:
            h, saved[i, s] = _ffn_fwd(h, norms(i, s), mods(i, s), weights[i, s], mid)
        elif i % 2 == 0:
            h, saved[i, s] = _mla_fwd(h, norms(i, s), mods(i, s), weights[i, s], rope, mid)
        else:
            h, saved[i, s] = _dil_fwd(h, norms(i, s), mods(i, s), weights[i, s], bias, mid)
    dh, loss_parts = _loss_grad(h, target, "loss")

    dnorm, dmod, sent = {}, {}, {}
    token = jnp.zeros((8, 128), F32)
    last = order[0]

    def send_last(j, dw):
        sent[last, j] = _split_start([dw], [[0]], False, "scatter_start_%d%d_%d" % (*last, j))
        return sent[last, j][3]

    for i, s in reversed(order):
        md = mods(i, s)
        md = (md[0], md[1], md[2] + token[:1, :1])
        if (i, s) == last:
            dh, dnorm[i, s], dmod[i, s], _ = _ffn_bwd(dh, saved[i, s], norms(i, s), md, weights[i, s], send_last)
            continue
        if s != 1:
            dh, dnorm[i, s], dmod[i, s], dws = _ffn_bwd(dh, saved[i, s], norms(i, s), md, weights[i, s])
        elif i % 2 == 0:
            dh, dnorm[i, s], dmod[i, s], dmla = _mla_bwd(dh, saved[i, s], norms(i, s), md, weights[i, s], rope)
            dw_in_pad, dq_norm, dwq_pad, dkv_norm, dwkv_t, dwo_pad = dmla
            dw_in = jnp.concatenate([dw_in_pad[:, :lat_real], dw_in_pad[:, lat_real + QK_NOPE:lat_real + qk]], axis=1)
            dwq_t = dwq_pad.reshape(MLA_HEADS, HEAD_PAD, Q_LORA)[:, :qk].reshape(MLA_HEADS * qk, Q_LORA)
            dwo = dwo_pad.reshape(MLA_HEADS, HEAD_PAD, d_model)[:, HEAD_PAD - V_HEAD:].reshape(MLA_HEADS * V_HEAD, d_model)
            dws = (dw_in, dwq_t, dwkv_t, dwo)
        else:
            dh, dnorm[i, s], dmod[i, s], dws, dbias = _dil_bwd(dh, saved[i, s], norms(i, s), md, weights[i, s], bias)
        sent[i, s] = _split_start(list(dws), [list(range(len(dws)))], False, "scatter_start_%d%d" % (i, s))
        token = sent[i, s][3]
    grad_x = dh[None]

    mine = {}
    for key in order[1:]:
        sems, srcs, lands, _ = sent[key]
        parts = _split_wait(sems[0], srcs, lands, dh, NDEV - 1, False, "scatter_wait_%d%d" % key)
        for k, p in zip(members[key], parts):
            mine[k] = _sum_parts(p, "sum_parts")
    for j in (2, 0, 1):
        sems, srcs, lands, _ = sent[last, j]
        parts = _split_wait(sems[0], srcs, lands, dh, NDEV - 1, False, "scatter_wait_%d%d_%d" % (*last, j))
        mine[members[last][j]] = _sum_parts(parts[0], "sum_parts")
    g_gate = jnp.stack([mine[3 * n].T for n in range(len(ffn_ids))]).reshape(ffn_w_gate.shape)
    g_up = jnp.stack([mine[3 * n + 1].T for n in range(len(ffn_ids))]).reshape(ffn_w_up.shape)
    g_down = jnp.stack([mine[3 * n + 2] for n in range(len(ffn_ids))]).reshape(ffn_w_down.shape)
    g_mla_in, g_q_up, g_kv_up, g_mla_o, g_dil_in, g_dil_o = (mine[k] for k in range(n_ffn, n_ffn + 6))
    g_mla_in, g_q_up, g_kv_up, g_mla_o = g_mla_in[None], g_q_up.T[None], g_kv_up.T[None], g_mla_o[None]
    g_dil_in, g_dil_o = g_dil_in.T[None], g_dil_o[None]

    dmod_mine = jnp.concatenate([jnp.concatenate(dmod[i, s], axis=0) for i in range(depth) for s in range(n_sub)], axis=0)
    dpre_mine = jnp.concatenate([dnorm[i, s][0] for i in range(depth) for s in range(n_sub)], axis=0)
    dpost_mine = jnp.concatenate([dnorm[i, s][1] for i in range(depth) for s in range(n_sub)], axis=0)
    dbias_tab = _bias_reduce(dbias, buckets, "bias_reduce")[:, 0, :N_BUCKETS].T
    pieces = [dmod_mine, dpre_mine, dpost_mine, dq_norm, dkv_norm, dbias_tab, jnp.sum(loss_parts).reshape(1, 1)]
    packed = [_lanes(p) for p in pieces]
    offs = [0]
    for p in packed:
        offs.append(offs[-1] + p.shape[0])
    everyone = _exchange([jnp.concatenate(packed, axis=0)], True, "gather_small_grads")[0].reshape(NDEV, offs[-1], 128)
    total = _sum_parts(everyone, "sum_small")
    take = lambda n, shape: total[offs[n]:offs[n + 1]].reshape(-1)[:math.prod(shape)].reshape(shape)
    g_b_mod = take(0, b_mod.shape)
    col0 = me * d_loc
    g_norm_pre = lax.dynamic_slice(take(1, (depth, n_sub, d_model)), (0, 0, col0), norm_pre.shape)
    g_norm_post = lax.dynamic_slice(take(2, (depth, n_sub, d_model)), (0, 0, col0), norm_post.shape)
    g_q_norm, g_kv_norm = take(3, mla_q_norm.shape), take(4, mla_kv_norm.shape)
    g_rel_bias = take(5, rel_bias.shape)
    loss = take(6, ())

    dmod_all = everyone[:, offs[0]:offs[1]].reshape(NDEV, depth, NDEV * mod_loc_cols)
    dmod_cols = lax.dynamic_slice(dmod_all, (0, 0, me * mod_loc_cols), (NDEV, depth, mod_loc_cols))
    silu_t = jnp.pad(silu_c.T, ((0, 0), (0, HEAD_PAD - NDEV)))
    g_w_mod = jnp.stack([_mm([(silu_t, jnp.pad(dmod_cols[:, i], ((0, HEAD_PAD - NDEV), (0, 0))))], "nn", F32, TOKEN_TILE,
                             mod_loc_cols, "mod_bwd") for i in range(depth)])

    ws = (norm_pre, norm_post, w_mod, b_mod, ffn_w_gate, ffn_w_up, ffn_w_down, mla_w_in, mla_q_norm, mla_w_q_up, mla_kv_norm,
          mla_w_kv_up, mla_w_o, dil_w_in, dil_w_o, rel_bias)
    gs = (g_norm_pre, g_norm_post, g_w_mod, g_b_mod, g_gate, g_up, g_down, g_mla_in, g_q_norm, g_q_up, g_kv_norm, g_kv_up,
          g_mla_o, g_dil_in, g_dil_o, g_rel_bias)
    ms = (m_norm_pre, m_norm_post, m_w_mod, m_b_mod, m_ffn_w_gate, m_ffn_w_up, m_ffn_w_down, m_mla_w_in, m_mla_q_norm,
          m_mla_w_q_up, m_mla_kv_norm, m_mla_w_kv_up, m_mla_w_o, m_dil_w_in, m_dil_w_o, m_rel_bias)
    vs = (v_norm_pre, v_norm_post, v_w_mod, v_b_mod, v_ffn_w_gate, v_ffn_w_up, v_ffn_w_down, v_mla_w_in, v_mla_q_norm,
          v_mla_w_q_up, v_mla_kv_norm, v_mla_w_kv_up, v_mla_w_o, v_dil_w_in, v_dil_w_o, v_rel_bias)
    stepped = [_adamw(w, g, m, v, "adamw") for w, g, m, v in zip(ws, gs, ms, vs)]
    deltas, new_m, new_v = zip(*stepped)
    return (loss, grad_x, *gs, *deltas, *new_m, *new_v)
```

```python
import math

import jax
import jax.numpy as jnp
from jax import lax
from jax.experimental import pallas as pl
from jax.experimental.pallas import tpu as pltpu

F32 = jnp.float32
BF16 = jnp.bfloat16
MESH = pl.DeviceIdType.MESH

NDEV = 8
OTHER_CHIPS = 3
D_MODEL = 1024
SEQ = 2048
D_FF = 2816
EPS = 1e-6
FFN_RES = 0.5

MLA_HEADS = 16
Q_LORA = 384
KV_LORA = 256
QK_NOPE = 64
QK_ROPE = 32
V_HEAD = 64
ROPE_THETA = 10000.0
HEAD_PAD = 128
LAT_PAD = Q_LORA + KV_LORA + HEAD_PAD
MLA_SCALE = (QK_NOPE + QK_ROPE) ** -0.5

DIL_GROUPS = ((128, 1), (512, 4), (2048, 16))
DIL_HEADS = 16
DIL_HEAD_DIM = 64
DIL_BLOCK = 128
DIL_PAIRS = DIL_HEADS // 2
DIL_SCALE = DIL_HEAD_DIM ** -0.5
N_BUCKETS = 32
MAX_DISTANCE = 2048

ADAM_LR = 0.001
ADAM_B1 = 0.9
ADAM_B2 = 0.999
ADAM_EPS = 1e-08
ADAM_WD = 0.01
ADAM_STEP = 10

V7X_VMEM_BYTES = 64 * 2**20
VMEM_RESERVE = 10 * 2**20
TOKEN_TILE = 512


def _nbytes(shape, dtype):
    return math.prod(shape) * jnp.dtype(dtype).itemsize


def _params(semantics, blocks, extra=0):
    need = 2 * sum(_nbytes(s, d) for s, d in blocks) + extra + VMEM_RESERVE
    return pltpu.CompilerParams(dimension_semantics=semantics,
                                vmem_limit_bytes=int(min(need, V7X_VMEM_BYTES - VMEM_RESERVE)))


def _pcall(body, out_shape, **kw):
    call = pl.pallas_call(body, out_shape=jax.tree.map(lambda s: pltpu.HBM(s.shape, s.dtype), out_shape), **kw)
    return lambda *args: call(*[pltpu.with_memory_space_constraint(a, pltpu.HBM) for a in args])


def _dot_nn(a, b):
    return lax.dot_general(a, b, (((1,), (0,)), ((), ())), preferred_element_type=F32)


def _dot_nt(a, b):
    return lax.dot_general(a, b, (((1,), (1,)), ((), ())), preferred_element_type=F32)


def _dot_tn(a, b):
    return lax.dot_general(a, b, (((0,), (0,)), ((), ())), preferred_element_type=F32)


_DOTS = {"nn": _dot_nn, "nt": _dot_nt, "tn": _dot_tn}


def _rstd(v):
    return lax.rsqrt(jnp.mean(v * v, axis=-1, keepdims=True) + EPS)


def _rms_bwd(v, r, t):
    return r * t - v * (r * r * r) * jnp.mean(t * v, axis=-1, keepdims=True)


_TOKEN_SPEC = pl.BlockSpec((8, 128), lambda *_: (0, 0))


def _mm(pairs, mode, out_dtype, tm, tn, name, out_perm=1, after=None):
    a0, b0 = pairs[0]
    m_dim = a0.shape[1] if mode == "tn" else a0.shape[0]
    n_dim = b0.shape[0] if mode == "nt" else b0.shape[1]
    tm, tn = min(tm, m_dim // out_perm), min(tn, n_dim)
    assert m_dim % tm == 0 and n_dim % tn == 0, (name, m_dim, n_dim, tm, tn)
    dot = _DOTS[mode]
    npairs = len(pairs)

    def body(*refs):
        acc = None
        for p in range(npairs):
            d = dot(refs[2 * p][...].astype(BF16), refs[2 * p + 1][...].astype(BF16))
            acc = d if acc is None else acc + d
        refs[-1][...] = acc.astype(out_dtype)

    in_specs, blocks, flat = [], [], []
    for a, b in pairs:
        if mode == "nn":
            k = a.shape[1]
            sa, sb = ((tm, k), lambda i, j: (i, 0)), ((k, tn), lambda i, j: (0, j))
        elif mode == "nt":
            k = a.shape[1]
            sa, sb = ((tm, k), lambda i, j: (i, 0)), ((tn, k), lambda i, j: (j, 0))
        else:
            k = a.shape[0]
            sa, sb = ((k, tm), lambda i, j: (0, i)), ((k, tn), lambda i, j: (0, j))
        in_specs += [pl.BlockSpec(*sa), pl.BlockSpec(*sb)]
        blocks += [(sa[0], a.dtype), (sb[0], b.dtype)]
        flat += [a, b]
    if after is not None:
        in_specs.append(_TOKEN_SPEC)
        flat.append(after)
    if out_perm == 1:
        out_shape = (m_dim, n_dim)
        out_spec = pl.BlockSpec((tm, tn), lambda i, j: (i, j))
    else:
        rows = m_dim // out_perm
        assert tn == n_dim and rows % tm == 0, (name, rows, tm)
        nb = rows // tm
        out_shape = (rows, out_perm * n_dim)
        out_spec = pl.BlockSpec((tm, n_dim), lambda i, j: (i % nb, i // nb))
    blocks.append(((tm, tn), out_dtype))
    res = _pcall(
        body, out_shape=jax.ShapeDtypeStruct(out_shape, out_dtype), grid=(m_dim // tm, n_dim // tn),
        in_specs=in_specs, out_specs=out_spec, name=name,
        compiler_params=_params(("parallel", "parallel"), blocks, extra=2 * tm * tn * 4),
    )(*flat)
    return res.reshape(m_dim, n_dim)


def _prenorm_mm(x, pre_g, scale, shift, w, w_mode, out_dtype, tn, name, perm=1):
    s_dim, d_dim = x.shape
    n_dim = w.shape[0] if w_mode == "nt" else w.shape[1]
    rows = s_dim // perm
    tm = min(TOKEN_TILE, rows)
    nb = rows // tm
    tn = min(tn, n_dim)
    assert n_dim % tn == 0
    dot = _DOTS[w_mode]

    def body(x_ref, g_ref, sc_ref, sh_ref, w_ref, hn_ref, o_ref):
        @pl.when(pl.program_id(1) == 0)
        def _():
            xf = x_ref[...]
            hn = (xf * _rstd(xf) * g_ref[...]) * (1.0 + sc_ref[...]) + sh_ref[...]
            hn_ref[...] = hn.astype(BF16)

        o_ref[...] = dot(hn_ref[...], w_ref[...]).astype(out_dtype)

    vec = pl.BlockSpec((1, d_dim), lambda i, j: (0, 0))
    w_block = (tn, d_dim) if w_mode == "nt" else (d_dim, tn)
    w_spec = pl.BlockSpec(w_block, (lambda i, j: (j, 0)) if w_mode == "nt" else (lambda i, j: (0, j)))
    hn, out = _pcall(
        body,
        out_shape=(jax.ShapeDtypeStruct((s_dim, d_dim), BF16), jax.ShapeDtypeStruct((s_dim, n_dim), out_dtype)),
        grid=(s_dim // tm, n_dim // tn),
        in_specs=[pl.BlockSpec((tm, d_dim), lambda i, j: (i % nb, i // nb)), vec, vec, vec, w_spec],
        out_specs=(pl.BlockSpec((tm, d_dim), lambda i, j: (i, 0)), pl.BlockSpec((tm, tn), lambda i, j: (i, j))),
        name=name,
        compiler_params=_params(("parallel", "arbitrary"),
                                [((tm, d_dim), F32), (w_block, BF16), ((tm, d_dim), BF16), ((tm, tn), out_dtype)],
                                extra=3 * tm * d_dim * 4 + tm * tn * 4),
    )(x.reshape(rows, perm * d_dim), pre_g, scale, shift, w)
    return hn, out


def _ffn_up(x, pre_g, scale, shift, wg_t, wu_t, name):
    s_dim, d_dim = x.shape
    f_dim = wg_t.shape[0]
    tm, tn = TOKEN_TILE, f_dim // 2

    def body(x_ref, g_ref, sc_ref, sh_ref, wg_ref, wu_ref, hn_ref, go_ref, uo_ref, a_ref):
        @pl.when(pl.program_id(1) == 0)
        def _():
            xf = x_ref[...]
            hn = (xf * _rstd(xf) * g_ref[...]) * (1.0 + sc_ref[...]) + sh_ref[...]
            hn_ref[...] = hn.astype(BF16)

        hn = hn_ref[...]
        g = _dot_nt(hn, wg_ref[...])
        u = _dot_nt(hn, wu_ref[...])
        go_ref[...] = g.astype(BF16)
        uo_ref[...] = u.astype(BF16)
        a_ref[...] = (g * jax.nn.sigmoid(g) * u).astype(BF16)

    vec = pl.BlockSpec((1, d_dim), lambda i, j: (0, 0))
    w_spec = pl.BlockSpec((tn, d_dim), lambda i, j: (j, 0))
    act = pl.BlockSpec((tm, tn), lambda i, j: (i, j))
    act_shape = jax.ShapeDtypeStruct((s_dim, f_dim), BF16)
    return _pcall(
        body,
        out_shape=(jax.ShapeDtypeStruct((s_dim, d_dim), BF16), act_shape, act_shape, act_shape),
        grid=(s_dim // tm, f_dim // tn),
        in_specs=[pl.BlockSpec((tm, d_dim), lambda i, j: (i, 0)), vec, vec, vec, w_spec, w_spec],
        out_specs=(pl.BlockSpec((tm, d_dim), lambda i, j: (i, 0)), act, act, act),
        name=name,
        compiler_params=_params(("parallel", "arbitrary"),
                                [((tm, d_dim), F32), ((tn, d_dim), BF16), ((tn, d_dim), BF16), ((tm, d_dim), BF16)]
                                + 3 * [((tm, tn), BF16)], extra=3 * tm * d_dim * 4 + 4 * tm * tn * 4),
    )(x, pre_g, scale, shift, wg_t, wu_t)


def _mm_post(a, w, x, post_g, gate, res_w, name):
    s_dim, k_dim = a.shape
    d_dim = w.shape[1]
    tm = TOKEN_TILE

    def body(a_ref, w_ref, x_ref, pg_ref, gt_ref, xo_ref, f_ref):
        f = _dot_nn(a_ref[...], w_ref[...])
        y = f * _rstd(f) * pg_ref[...]
        f_ref[...] = f
        xo_ref[...] = x_ref[...] + (res_w * gt_ref[...]) * y

    vec = pl.BlockSpec((1, d_dim), lambda i: (0, 0))
    row = pl.BlockSpec((tm, d_dim), lambda i: (i, 0))
    out = jax.ShapeDtypeStruct((s_dim, d_dim), F32)
    return _pcall(
        body, out_shape=(out, out), grid=(s_dim // tm,),
        in_specs=[pl.BlockSpec((tm, k_dim), lambda i: (i, 0)), pl.BlockSpec((k_dim, d_dim), lambda i: (0, 0)), row, vec, vec],
        out_specs=(row, row), name=name,
        compiler_params=_params(("parallel",), [((tm, k_dim), BF16), ((k_dim, d_dim), BF16)] + 3 * [((tm, d_dim), F32)],
                                extra=3 * tm * d_dim * 4),
    )(a, w, x, post_g, gate)


def _post_bwd(dx_out, f, post_g, gate, res_w, name):
    s_dim, d_dim = f.shape
    tm = TOKEN_TILE

    def body(dx_ref, f_ref, pg_ref, gt_ref, df_ref, dgate_ref, dpost_ref):
        @pl.when(pl.program_id(0) == 0)
        def _():
            dgate_ref[...] = jnp.zeros_like(dgate_ref)
            dpost_ref[...] = jnp.zeros_like(dpost_ref)

        dx, fv = dx_ref[...], f_ref[...]
        r = _rstd(fv)
        fr = fv * r
        dgate_ref[...] += res_w * jnp.sum(dx * (fr * pg_ref[...]), axis=0, keepdims=True)
        dy = (res_w * gt_ref[...]) * dx
        dpost_ref[...] += jnp.sum(dy * fr, axis=0, keepdims=True)
        df_ref[...] = _rms_bwd(fv, r, dy * pg_ref[...]).astype(BF16)

    vec = pl.BlockSpec((1, d_dim), lambda i: (0, 0))
    row = pl.BlockSpec((tm, d_dim), lambda i: (i, 0))
    vshape = jax.ShapeDtypeStruct((1, d_dim), F32)
    return _pcall(
        body, out_shape=(jax.ShapeDtypeStruct((s_dim, d_dim), BF16), vshape, vshape), grid=(s_dim // tm,),
        in_specs=[row, row, vec, vec], out_specs=(row, vec, vec), name=name,
        compiler_params=_params(("arbitrary",), 3 * [((tm, d_dim), F32)], extra=6 * tm * d_dim * 4),
    )(dx_out, f, post_g, gate)


def _prenorm_bwd(dx_out, dhns, x, pre_g, scale, name):
    s_dim, d_dim = x.shape
    tm = TOKEN_TILE
    n_in = len(dhns)

    def body(*refs):
        dx_ref, x_ref, pg_ref, sc_ref = refs[n_in + 0], refs[n_in + 1], refs[n_in + 2], refs[n_in + 3]
        dxo_ref, dsh_ref, dsc_ref, dpg_ref = refs[n_in + 4:]

        @pl.when(pl.program_id(0) == 0)
        def _():
            dsh_ref[...] = jnp.zeros_like(dsh_ref)
            dsc_ref[...] = jnp.zeros_like(dsc_ref)
            dpg_ref[...] = jnp.zeros_like(dpg_ref)

        dhn = refs[0][...]
        for k in range(1, n_in):
            dhn = dhn + refs[k][...]
        xv = x_ref[...]
        r = _rstd(xv)
        xr = xv * r
        dsh_ref[...] += jnp.sum(dhn, axis=0, keepdims=True)
        dsc_ref[...] += jnp.sum(dhn * (xr * pg_ref[...]), axis=0, keepdims=True)
        dn = dhn * (1.0 + sc_ref[...])
        dpg_ref[...] += jnp.sum(dn * xr, axis=0, keepdims=True)
        dxo_ref[...] = dx_ref[...] + _rms_bwd(xv, r, dn * pg_ref[...])

    vec = pl.BlockSpec((1, d_dim), lambda i: (0, 0))
    row = pl.BlockSpec((tm, d_dim), lambda i: (i, 0))
    vshape = jax.ShapeDtypeStruct((1, d_dim), F32)
    return _pcall(
        body, out_shape=(jax.ShapeDtypeStruct((s_dim, d_dim), F32), vshape, vshape, vshape), grid=(s_dim // tm,),
        in_specs=n_in * [row] + [row, row, vec, vec], out_specs=(row, vec, vec, vec), name=name,
        compiler_params=_params(("arbitrary",), (n_in + 3) * [((tm, d_dim), F32)], extra=6 * tm * d_dim * 4),
    )(*dhns, dx_out, x, pre_g, scale)


def _ffn_dgu(df, wd, g, u, name, after=None):
    s_dim, d_dim = df.shape
    f_dim = wd.shape[0]
    tm, tn = TOKEN_TILE, f_dim // 2

    def body(df_ref, wd_ref, g_ref, u_ref, *rest):
        dg_ref, du_ref = rest[-2:]
        da = _dot_nt(df_ref[...], wd_ref[...])
        gv, uv = g_ref[...].astype(F32), u_ref[...].astype(F32)
        sg = jax.nn.sigmoid(gv)
        du_ref[...] = (da * (gv * sg)).astype(BF16)
        dg_ref[...] = (da * uv * (sg * (1.0 + gv * (1.0 - sg)))).astype(BF16)

    act = pl.BlockSpec((tm, tn), lambda i, j: (i, j))
    act_shape = jax.ShapeDtypeStruct((s_dim, f_dim), BF16)
    token = [] if after is None else [after]
    return _pcall(
        body, out_shape=(act_shape, act_shape), grid=(s_dim // tm, f_dim // tn),
        in_specs=[pl.BlockSpec((tm, d_dim), lambda i, j: (i, 0)), pl.BlockSpec((tn, d_dim), lambda i, j: (j, 0)), act, act]
        + len(token) * [_TOKEN_SPEC],
        out_specs=(act, act), name=name,
        compiler_params=_params(("parallel", "parallel"), [((tm, d_dim), BF16), ((tn, d_dim), BF16)] + 4 * [((tm, tn), BF16)],
                                extra=6 * tm * tn * 4),
    )(df, wd, g, u, *token)


def _rope_tables():
    half = QK_ROPE // 2
    freqs = ROPE_THETA ** (-jnp.arange(half, dtype=F32) / half)
    ang = jnp.arange(SEQ, dtype=F32)[:, None] * freqs[None, :]
    cos, sin = jnp.cos(ang), jnp.sin(ang)
    ones = jnp.ones((SEQ, QK_NOPE), F32)
    zeros = jnp.zeros((SEQ, QK_NOPE), F32)
    pad1 = jnp.ones((SEQ, HEAD_PAD - QK_NOPE - QK_ROPE), F32)
    pad0 = jnp.zeros((SEQ, HEAD_PAD - QK_NOPE - QK_ROPE), F32)
    zh = jnp.zeros((SEQ, half), F32)
    c = jnp.concatenate([ones, cos, cos, pad1], axis=1)
    s1 = jnp.concatenate([zeros, -sin, zh, pad0], axis=1)
    s2 = jnp.concatenate([zeros, zh, sin, pad0], axis=1)
    return c, s1, s2


def _rope(v, c, s1, s2):
    half = QK_ROPE // 2
    return v * c + pltpu.roll(v, HEAD_PAD - half, 1) * s1 + pltpu.roll(v, half, 1) * s2


def _rope_t(dv, c, s1, s2):
    half = QK_ROPE // 2
    return dv * c + pltpu.roll(dv * s1, half, 1) + pltpu.roll(dv * s2, HEAD_PAD - half, 1)


def _mla_qkv(lat, q_norm, kv_norm, wq_t, wkv_t, rope, name):
    s_dim = lat.shape[0]
    width = MLA_HEADS * HEAD_PAD
    tm = 256

    def body(lat_ref, qg_ref, kg_ref, wq_ref, wkv_ref, c_ref, s1_ref, s2_ref, q_ref, k_ref, v_ref, qn_ref, kvn_ref):
        cq = lat_ref[:, :Q_LORA]
        ckv = lat_ref[:, Q_LORA:Q_LORA + KV_LORA]
        kr = lat_ref[:, Q_LORA + KV_LORA:]
        c, s1, s2 = c_ref[...], s1_ref[...], s2_ref[...]
        qn = (cq * _rstd(cq) * qg_ref[...]).astype(BF16)
        kvn = (ckv * _rstd(ckv) * kg_ref[...]).astype(BF16)
        qn_ref[...] = qn
        kvn_ref[...] = kvn
        q = _dot_nt(qn, wq_ref[...])
        kv = _dot_nt(kvn, wkv_ref[...])
        krr = _rope(kr, c, s1, s2)
        low = lax.broadcasted_iota(jnp.int32, (tm, HEAD_PAD), 1) < QK_NOPE
        for h in range(MLA_HEADS):
            sl = slice(h * HEAD_PAD, (h + 1) * HEAD_PAD)
            q_ref[:, sl] = _rope(q[:, sl], c, s1, s2).astype(BF16)
            kvh = kv[:, sl]
            k_ref[:, sl] = (jnp.where(low, kvh, 0.0) + krr).astype(BF16)
            v_ref[:, sl] = jnp.where(low, 0.0, kvh).astype(BF16)

    row = lambda n: pl.BlockSpec((tm, n), lambda i: (i, 0))
    full = lambda a: pl.BlockSpec(a.shape, lambda i: (0, 0))
    wide = jax.ShapeDtypeStruct((s_dim, width), BF16)
    return _pcall(
        body,
        out_shape=(wide, wide, wide, jax.ShapeDtypeStruct((s_dim, Q_LORA), BF16), jax.ShapeDtypeStruct((s_dim, KV_LORA), BF16)),
        grid=(s_dim // tm,),
        in_specs=[row(LAT_PAD), full(q_norm), full(kv_norm), full(wq_t), full(wkv_t), row(HEAD_PAD), row(HEAD_PAD), row(HEAD_PAD)],
        out_specs=(row(width), row(width), row(width), row(Q_LORA), row(KV_LORA)), name=name,
        compiler_params=_params(("parallel",), [((tm, LAT_PAD), F32), (wq_t.shape, BF16), (wkv_t.shape, BF16)]
                                + 3 * [((tm, width), BF16)], extra=4 * tm * width * 4),
    )(lat, q_norm, kv_norm, wq_t, wkv_t, *rope)


def _mla_probs(q, k, t, tq):
    s = _dot_nt(q, k) * MLA_SCALE
    rows = lax.broadcasted_iota(jnp.int32, s.shape, 0) + t * tq
    cols = lax.broadcasted_iota(jnp.int32, s.shape, 1)
    s = jnp.where(cols <= rows, s, -jnp.inf)
    e = jnp.exp(s - jnp.max(s, axis=-1, keepdims=True))
    return e / jnp.sum(e, axis=-1, keepdims=True)


def _mla_attn_fwd(q, k, v, name):
    s_dim = q.shape[0]
    tq = 512

    def body(q_ref, k_ref, v_ref, o_ref):
        for t in range(s_dim // tq):
            kt = (t + 1) * tq
            p = _mla_probs(q_ref[t * tq:kt, :], k_ref[:kt, :], t, tq)
            o_ref[t * tq:kt, :] = _dot_nn(p.astype(BF16), v_ref[:kt, :]).astype(BF16)

    head = pl.BlockSpec((s_dim, HEAD_PAD), lambda h: (0, h))
    return _pcall(
        body, out_shape=jax.ShapeDtypeStruct(q.shape, BF16), grid=(MLA_HEADS,),
        in_specs=[head, head, head], out_specs=head, name=name,
        compiler_params=_params(("parallel",), 4 * [((s_dim, HEAD_PAD), BF16)], extra=4 * tq * s_dim * 4),
    )(q, k, v)


def _mla_attn_bwd(q, k, v, d_o, name):
    s_dim = q.shape[0]
    tq = 512

    def body(q_ref, k_ref, v_ref, do_ref, dq_ref, dk_ref, dv_ref):
        dk_ref[...] = jnp.zeros_like(dk_ref)
        dv_ref[...] = jnp.zeros_like(dv_ref)
        for t in range(s_dim // tq):
            kt = (t + 1) * tq
            qt = q_ref[t * tq:kt, :]
            dot = do_ref[t * tq:kt, :].astype(BF16)
            p = _mla_probs(qt, k_ref[:kt, :], t, tq)
            dp = _dot_nt(dot, v_ref[:kt, :])
            ds = p * (dp - jnp.sum(p * dp, axis=-1, keepdims=True))
            dsb = (ds * MLA_SCALE).astype(BF16)
            dq_ref[t * tq:kt, :] = _dot_nn(dsb, k_ref[:kt, :])
            dk_ref[:kt, :] += _dot_tn(dsb, qt)
            dv_ref[:kt, :] += _dot_tn(p.astype(BF16), dot)

    head = pl.BlockSpec((s_dim, HEAD_PAD), lambda h: (0, h))
    out = jax.ShapeDtypeStruct(q.shape, F32)
    return _pcall(
        body, out_shape=(out, out, out), grid=(MLA_HEADS,),
        in_specs=[head, head, head, head], out_specs=(head, head, head), name=name,
        compiler_params=_params(("parallel",), 3 * [((s_dim, HEAD_PAD), BF16)] + 4 * [((s_dim, HEAD_PAD), F32)],
                                extra=6 * tq * s_dim * 4),
    )(q, k, v, d_o)


def _mla_qkv_bwd(dq, dk, dv, lat, q_norm, kv_norm, wq_t, wkv_t, rope, name):
    s_dim = lat.shape[0]
    width = MLA_HEADS * HEAD_PAD
    tm = 256

    def body(dq_ref, dk_ref, dv_ref, lat_ref, qg_ref, kg_ref, wq_ref, wkv_ref, c_ref, s1_ref, s2_ref,
             dqp_ref, dkv_ref, dlat_ref, dqg_ref, dkg_ref):
        @pl.when(pl.program_id(0) == 0)
        def _():
            dqg_ref[...] = jnp.zeros_like(dqg_ref)
            dkg_ref[...] = jnp.zeros_like(dkg_ref)

        c, s1, s2 = c_ref[...], s1_ref[...], s2_ref[...]
        lane = lax.broadcasted_iota(jnp.int32, (tm, HEAD_PAD), 1)
        low = lane < QK_NOPE
        rot = (lane >= QK_NOPE) & (lane < QK_NOPE + QK_ROPE)
        dkrr = jnp.zeros((tm, HEAD_PAD), F32)
        for h in range(MLA_HEADS):
            sl = slice(h * HEAD_PAD, (h + 1) * HEAD_PAD)
            dqp_ref[:, sl] = _rope_t(dq_ref[:, sl], c, s1, s2).astype(BF16)
            dkh = dk_ref[:, sl]
            dkv_ref[:, sl] = jnp.where(low, dkh, dv_ref[:, sl]).astype(BF16)
            dkrr = dkrr + jnp.where(rot, dkh, 0.0)
        dqn = _dot_nn(dqp_ref[...], wq_ref[...])
        dkvn = _dot_nn(dkv_ref[...], wkv_ref[...])
        cq = lat_ref[:, :Q_LORA]
        ckv = lat_ref[:, Q_LORA:Q_LORA + KV_LORA]
        rq, rkv = _rstd(cq), _rstd(ckv)
        dqg_ref[...] += jnp.sum(dqn * cq * rq, axis=0, keepdims=True)
        dkg_ref[...] += jnp.sum(dkvn * ckv * rkv, axis=0, keepdims=True)
        dlat_ref[:, :Q_LORA] = _rms_bwd(cq, rq, dqn * qg_ref[...])
        dlat_ref[:, Q_LORA:Q_LORA + KV_LORA] = _rms_bwd(ckv, rkv, dkvn * kg_ref[...])
        dlat_ref[:, Q_LORA + KV_LORA:] = _rope_t(dkrr, c, s1, s2)

    row = lambda n: pl.BlockSpec((tm, n), lambda i: (i, 0))
    full = lambda a: pl.BlockSpec(a.shape, lambda i: (0, 0))
    wide = jax.ShapeDtypeStruct((s_dim, width), BF16)
    return _pcall(
        body,
        out_shape=(wide, wide, jax.ShapeDtypeStruct((s_dim, LAT_PAD), F32),
                   jax.ShapeDtypeStruct(q_norm.shape, F32), jax.ShapeDtypeStruct(kv_norm.shape, F32)),
        grid=(s_dim // tm,),
        in_specs=[row(width), row(width), row(width), row(LAT_PAD), full(q_norm), full(kv_norm), full(wq_t), full(wkv_t),
                  row(HEAD_PAD), row(HEAD_PAD), row(HEAD_PAD)],
        out_specs=(row(width), row(width), row(LAT_PAD), full(q_norm), full(kv_norm)), name=name,
        compiler_params=_params(("arbitrary",), 3 * [((tm, width), F32)] + [((tm, LAT_PAD), F32), (wq_t.shape, BF16),
                                                                           (wkv_t.shape, BF16)] + 2 * [((tm, width), BF16)],
                                extra=2 * tm * width * 4),
    )(dq, dk, dv, lat, q_norm, kv_norm, wq_t, wkv_t, *rope)


def _t5_bucket(dist):
    max_exact = N_BUCKETS // 2
    d = jnp.maximum(dist, 1).astype(F32)
    large = max_exact + (jnp.log(d / max_exact) / math.log(MAX_DISTANCE / max_exact)
                         * (N_BUCKETS - max_exact)).astype(jnp.int32)
    large = jnp.minimum(large, N_BUCKETS - 1)
    return jnp.where(dist < max_exact, dist, large)


def _dil_buckets(dilation):
    iq = jnp.arange(DIL_BLOCK)[:, None]
    ik = jnp.arange(2 * DIL_BLOCK)[None, :]
    return _t5_bucket(jnp.maximum(DIL_BLOCK + iq - ik, 0) * dilation)


def _dil_logits(qh, k_ref, bias_h, n, span):
    lo = n * DIL_BLOCK
    if n == 0:
        s = _dot_nt(qh, k_ref[lo:lo + DIL_BLOCK, :]) * DIL_SCALE + bias_h[:, DIL_BLOCK:]
        rel = lax.broadcasted_iota(jnp.int32, s.shape, 0) - lax.broadcasted_iota(jnp.int32, s.shape, 1)
    else:
        s = _dot_nt(qh, k_ref[lo - DIL_BLOCK:lo + DIL_BLOCK, :]) * DIL_SCALE + bias_h
        rel = DIL_BLOCK + lax.broadcasted_iota(jnp.int32, s.shape, 0) - lax.broadcasted_iota(jnp.int32, s.shape, 1)
    return jnp.where((rel >= 0) & (rel <= span), s, -jnp.inf)


def _dil_views(dilation, rows):
    col = lambda which: pl.BlockSpec((rows, HEAD_PAD), lambda p, r: (r, which * DIL_PAIRS + p))
    nat = pl.BlockSpec((rows, HEAD_PAD), lambda p, r: (0, r * DIL_PAIRS + p))
    bias = pl.BlockSpec((2, DIL_BLOCK, 2 * DIL_BLOCK), lambda p, r: (p, 0, 0))
    return col, nat, bias


def _dil_attn_fwd(qkv, bias, dilation, span, name):
    s_dim = qkv.shape[0]
    rows = s_dim // dilation
    d_dim = DIL_HEADS * DIL_HEAD_DIM
    col, nat, bias_spec = _dil_views(dilation, rows)

    def body(q_ref, k_ref, v_ref, b_ref, o_ref, l_ref):
        lane = lax.broadcasted_iota(jnp.int32, (DIL_BLOCK, HEAD_PAD), 1)
        klane = lax.broadcasted_iota(jnp.int32, (2 * DIL_BLOCK, HEAD_PAD), 1)
        for n in range(rows // DIL_BLOCK):
            lo = n * DIL_BLOCK
            kv_rows = slice(lo, lo + DIL_BLOCK) if n == 0 else slice(lo - DIL_BLOCK, lo + DIL_BLOCK)
            qb, vb = q_ref[lo:lo + DIL_BLOCK, :], v_ref[kv_rows, :]
            o_acc = jnp.zeros((DIL_BLOCK, HEAD_PAD), F32)
            lse_acc = jnp.zeros((DIL_BLOCK, HEAD_PAD), F32)
            for h in range(2):
                mine = (lane < DIL_HEAD_DIM) == (h == 0)
                kmine = (klane[:vb.shape[0]] < DIL_HEAD_DIM) == (h == 0)
                logits = _dil_logits(jnp.where(mine, qb, 0), k_ref, b_ref[h], n, span)
                mx = jnp.max(logits, axis=-1, keepdims=True)
                lse = mx + jnp.log(jnp.sum(jnp.exp(logits - mx), axis=-1, keepdims=True))
                p = jnp.exp(logits - lse)
                o_acc = o_acc + _dot_nn(p.astype(BF16), jnp.where(kmine, vb, 0))
                lse_acc = jnp.where(mine, lse, lse_acc)
            o_ref[lo:lo + DIL_BLOCK, :] = o_acc
            l_ref[lo:lo + DIL_BLOCK, :] = lse_acc

    out = jax.ShapeDtypeStruct((rows, dilation * d_dim), F32)
    o, lse = _pcall(
        body, out_shape=(out, out), grid=(DIL_PAIRS, dilation),
        in_specs=[col(0), col(1), col(2), bias_spec], out_specs=(nat, nat), name=name,
        compiler_params=_params(("parallel", "parallel"), 3 * [((rows, HEAD_PAD), BF16)] + 2 * [((rows, HEAD_PAD), F32)]
                                + [((2, DIL_BLOCK, 2 * DIL_BLOCK), F32)], extra=2**21),
    )(qkv, qkv, qkv, bias)
    return o.reshape(s_dim, d_dim), lse.reshape(s_dim, d_dim)


def _dil_mix(lses, outs, name):
    s_dim, d_dim = outs[0].shape
    tm = TOKEN_TILE
    ng = len(outs)

    def body(*refs):
        ls = [refs[g][...] for g in range(ng)]
        mx = ls[0]
        for g in range(1, ng):
            mx = jnp.maximum(mx, ls[g])
        es = [jnp.exp(l - mx) for l in ls]
        tot = es[0]
        for g in range(1, ng):
            tot = tot + es[g]
        o = None
        for g in range(ng):
            al = es[g] / tot
            refs[2 * ng + g][...] = al
            t = al * refs[ng + g][...]
            o = t if o is None else o + t
        refs[3 * ng][...] = o
        refs[3 * ng + 1][...] = o.astype(BF16)

    row = pl.BlockSpec((tm, d_dim), lambda i: (i, 0))
    f = jax.ShapeDtypeStruct((s_dim, d_dim), F32)
    res = _pcall(
        body, out_shape=tuple(ng * [f] + [f, jax.ShapeDtypeStruct((s_dim, d_dim), BF16)]), grid=(s_dim // tm,),
        in_specs=2 * ng * [row], out_specs=tuple((ng + 2) * [row]), name=name,
        compiler_params=_params(("parallel",), (3 * ng + 2) * [((tm, d_dim), F32)], extra=4 * tm * d_dim * 4),
    )(*lses, *outs)
    return res[:ng], res[ng], res[ng + 1]


def _dil_attn_bwd(qkv, bias, d_o, o_mix, alpha, lse, dilation, span, name):
    s_dim = qkv.shape[0]
    rows = s_dim // dilation
    d_dim = DIL_HEADS * DIL_HEAD_DIM
    col, nat, bias_spec = _dil_views(dilation, rows)
    nat_view = lambda a: a.reshape(rows, dilation * d_dim)

    def body(q_ref, k_ref, v_ref, b_ref, do_ref, om_ref, al_ref, l_ref, dq_ref, dk_ref, dv_ref, db_ref, dk_acc, dv_acc):
        @pl.when(pl.program_id(1) == 0)
        def _():
            db_ref[...] = jnp.zeros_like(db_ref)

        dk_acc[...] = jnp.zeros_like(dk_acc)
        dv_acc[...] = jnp.zeros_like(dv_acc)
        lane = lax.broadcasted_iota(jnp.int32, (DIL_BLOCK, HEAD_PAD), 1)
        klane = lax.broadcasted_iota(jnp.int32, (2 * DIL_BLOCK, HEAD_PAD), 1)
        for n in range(rows // DIL_BLOCK):
            lo = n * DIL_BLOCK
            blk = slice(lo, lo + DIL_BLOCK)
            kv_rows = blk if n == 0 else slice(lo - DIL_BLOCK, lo + DIL_BLOCK)
            qb, kb, vb = q_ref[blk, :], k_ref[kv_rows, :], v_ref[kv_rows, :]
            al = al_ref[blk, :]
            dog = al * do_ref[blk, :]
            row_term = dog * om_ref[blk, :]
            lse_b = l_ref[blk, :]
            dq_acc = jnp.zeros((DIL_BLOCK, HEAD_PAD), F32)
            dk_blk = jnp.zeros((kb.shape[0], HEAD_PAD), F32)
            dv_blk = jnp.zeros((kb.shape[0], HEAD_PAD), F32)
            for h in range(2):
                mine = (lane < DIL_HEAD_DIM) == (h == 0)
                kmine = (klane[:kb.shape[0]] < DIL_HEAD_DIM) == (h == 0)
                qh = jnp.where(mine, qb, 0)
                logits = _dil_logits(qh, k_ref, b_ref[h], n, span)
                lse_h = jnp.max(jnp.where(mine, lse_b, -jnp.inf), axis=-1, keepdims=True)
                p = jnp.exp(logits - lse_h)
                dogh = jnp.where(mine, dog, 0.0).astype(BF16)
                dp = _dot_nt(dogh, vb)
                ds = p * (dp - jnp.sum(jnp.where(mine, row_term, 0.0), axis=-1, keepdims=True))
                if n == 0:
                    db_ref[h, :, DIL_BLOCK:] += ds
                else:
                    db_ref[h] += ds
                dsb = (ds * DIL_SCALE).astype(BF16)
                dq_acc = dq_acc + _dot_nn(dsb, jnp.where(kmine, kb, 0))
                dk_blk = dk_blk + _dot_tn(dsb, qh)
                dv_blk = dv_blk + _dot_tn(p.astype(BF16), dogh)
            dq_ref[blk, :] = dq_acc.astype(BF16)
            dk_acc[kv_rows, :] += dk_blk
            dv_acc[kv_rows, :] += dv_blk
        dk_ref[...] = dk_acc[...].astype(BF16)
        dv_ref[...] = dv_acc[...].astype(BF16)

    out_col = pl.BlockSpec((rows, HEAD_PAD), lambda p, r: (r, p))
    grad = jax.ShapeDtypeStruct((s_dim, d_dim), BF16)
    return _pcall(
        body, out_shape=(grad, grad, grad, jax.ShapeDtypeStruct(bias.shape, F32)), grid=(DIL_PAIRS, dilation),
        in_specs=[col(0), col(1), col(2), bias_spec, nat, nat, nat, nat],
        out_specs=(out_col, out_col, out_col, bias_spec), name=name,
        scratch_shapes=[pltpu.VMEM((rows, HEAD_PAD), F32), pltpu.VMEM((rows, HEAD_PAD), F32)],
        compiler_params=_params(("parallel", "arbitrary"), 6 * [((rows, HEAD_PAD), BF16)] + 4 * [((rows, HEAD_PAD), F32)]
                                + 2 * [((2, DIL_BLOCK, 2 * DIL_BLOCK), F32)], extra=2 * rows * HEAD_PAD * 4 + 2**21),
    )(qkv, qkv, qkv, bias, nat_view(d_o), nat_view(o_mix), nat_view(alpha), nat_view(lse))


def _bias_reduce(dbias, buckets, name):
    n_heads = dbias.shape[0]

    def body(db_ref, bk_ref, o_ref):
        ds, bk = db_ref[0], bk_ref[0]
        lane = lax.broadcasted_iota(jnp.int32, (8, HEAD_PAD), 1)
        acc = jnp.zeros((8, HEAD_PAD), F32)
        for b in range(N_BUCKETS):
            acc = jnp.where(lane == b, jnp.sum(jnp.where(bk == b, ds, 0.0)), acc)
        o_ref[0] = acc

    blk = (1, DIL_BLOCK, 2 * DIL_BLOCK)
    return _pcall(
        body, out_shape=jax.ShapeDtypeStruct((n_heads, 8, HEAD_PAD), F32), grid=(n_heads,),
        in_specs=[pl.BlockSpec(blk, lambda h: (h, 0, 0)), pl.BlockSpec(blk, lambda h: (h // DIL_HEADS, 0, 0))],
        out_specs=pl.BlockSpec((1, 8, HEAD_PAD), lambda h: (h, 0, 0)), name=name,
        compiler_params=_params(("parallel",), [(blk, F32), (blk, jnp.int32)], extra=2**20),
    )(dbias, buckets)


def _loss_grad(y, target, name):
    s_dim, d_dim = y.shape
    tm = TOKEN_TILE

    def body(y_ref, t_ref, dy_ref, l_ref):
        @pl.when(pl.program_id(0) == 0)
        def _():
            l_ref[...] = jnp.zeros_like(l_ref)

        err = y_ref[...] - t_ref[...]
        dy_ref[...] = err / d_dim
        sq = (err * err).reshape(tm // 8, 8, d_dim)
        l_ref[...] += 0.5 * jnp.sum(sq, axis=0) / d_dim

    row = pl.BlockSpec((tm, d_dim), lambda i: (i, 0))
    acc = pl.BlockSpec((8, d_dim), lambda i: (0, 0))
    return _pcall(
        body, out_shape=(jax.ShapeDtypeStruct((s_dim, d_dim), F32), jax.ShapeDtypeStruct((8, d_dim), F32)),
        grid=(s_dim // tm,), in_specs=[row, row], out_specs=(row, acc), name=name,
        compiler_params=_params(("arbitrary",), 3 * [((tm, d_dim), F32)], extra=2 * tm * d_dim * 4),
    )(y, target)


def _mod_fwd(c_all, w_mod, b_loc, name):
    depth, d_dim, n = w_mod.shape
    nb = c_all.shape[0]

    def body(c_ref, w_ref, b_ref, o_ref, s_ref):
        cv = c_ref[...]
        sc = cv * jax.nn.sigmoid(cv)
        s_ref[...] = sc
        o_ref[0] = _dot_nn(sc.astype(BF16), w_ref[0].astype(BF16)) + b_ref[0]

    return _pcall(
        body, out_shape=(jax.ShapeDtypeStruct((depth, nb, n), F32), jax.ShapeDtypeStruct((nb, d_dim), F32)), grid=(depth,),
        in_specs=[pl.BlockSpec((nb, d_dim), lambda i: (0, 0)), pl.BlockSpec((1, d_dim, n), lambda i: (i, 0, 0)),
                  pl.BlockSpec((1, 1, n), lambda i: (i, 0, 0))],
        out_specs=(pl.BlockSpec((1, nb, n), lambda i: (i, 0, 0)), pl.BlockSpec((nb, d_dim), lambda i: (0, 0))), name=name,
        compiler_params=_params(("arbitrary",), [((1, d_dim, n), F32)], extra=d_dim * n * 2 + 2**20),
    )(c_all, w_mod, b_loc.reshape(depth, 1, n))


def _sum_parts(parts, name):
    _, rows, cols = parts.shape
    tr = rows
    for cand in (512, 384, 256, 128, 64, 32, 16):
        if rows % cand == 0 and rows > cand:
            tr = cand
            break

    def body(p_ref, o_ref):
        acc = p_ref[0].astype(F32)
        for k in range(1, NDEV):
            acc = acc + p_ref[k].astype(F32)
        o_ref[...] = acc

    return _pcall(
        body, out_shape=jax.ShapeDtypeStruct((rows, cols), F32), grid=(rows // tr,),
        in_specs=[pl.BlockSpec((NDEV, tr, cols), lambda i: (0, i, 0))], out_specs=pl.BlockSpec((tr, cols), lambda i: (i, 0)),
        name=name, compiler_params=_params(("parallel",), [((NDEV, tr, cols), parts.dtype), ((tr, cols), F32)], extra=2**20),
    )(parts)


def _adamw(w, g, m, v, name):
    shape = w.shape
    cols = shape[-1]
    rows = math.prod(shape[:-1])
    tr = rows
    for cand in (512, 256, 128, 64, 32, 16, 8):
        if rows % cand == 0 and rows > cand and cand * cols * 4 <= 2**21:
            tr = cand
            break

    def body(w_ref, g_ref, m_ref, v_ref, d_ref, mo_ref, vo_ref):
        gv = g_ref[...]
        mn = ADAM_B1 * m_ref[...] + (1.0 - ADAM_B1) * gv
        vn = ADAM_B2 * v_ref[...] + (1.0 - ADAM_B2) * (gv * gv)
        m_hat = mn / (1.0 - ADAM_B1 ** ADAM_STEP)
        v_hat = vn / (1.0 - ADAM_B2 ** ADAM_STEP)
        d_ref[...] = -ADAM_LR * (m_hat / (jnp.sqrt(v_hat) + ADAM_EPS) + ADAM_WD * w_ref[...])
        mo_ref[...] = mn
        vo_ref[...] = vn

    blk = pl.BlockSpec((tr, cols), lambda i: (i, 0))
    out = jax.ShapeDtypeStruct((rows, cols), F32)
    res = _pcall(
        body, out_shape=(out, out, out), grid=(rows // tr,), in_specs=4 * [blk], out_specs=(blk, blk, blk), name=name,
        compiler_params=_params(("parallel",), 7 * [((tr, cols), F32)], extra=4 * tr * cols * 4),
    )(*(a.reshape(rows, cols) for a in (w, g, m, v)))
    return tuple(r.reshape(shape) for r in res)


def _peers():
    x, y, c = lax.axis_index("x"), lax.axis_index("y"), lax.axis_index("c")
    flip = lambda v, f: 1 - v if f else v
    peers = []
    for f in range(1, NDEV):
        px, py, pc = flip(x, f & 4), flip(y, f & 2), flip(c, f & 1)
        peers.append(((px, py, pc), 4 * px + 2 * py + pc))
    return (x, y, c), 4 * x + 2 * y + c, peers


def _places():
    x, y, c = lax.axis_index("x"), lax.axis_index("y"), lax.axis_index("c")
    place = lambda px, py, pc: ((px, py, pc), 4 * px + 2 * py + pc)
    return place(x, y, c), place(x, y, 1 - c), [place(1 - x, y, c), place(x, 1 - y, c), place(1 - x, 1 - y, c)]


def _exchange(arrs, gather, name):
    n = len(arrs)
    hbm = pl.BlockSpec(memory_space=pltpu.HBM)
    if gather:
        out_shape = [jax.ShapeDtypeStruct((NDEV * a.shape[0], a.shape[1]), a.dtype) for a in arrs]
    else:
        out_shape = [jax.ShapeDtypeStruct((NDEV, a.shape[0] // NDEV, a.shape[1]), a.dtype) for a in arrs]

    def body(*refs):
        ins, outs = refs[:n], refs[n:2 * n]
        send_sems, recv_sems, local_sems = refs[2 * n:]
        me_pos, me, peers = _peers()
        local = []
        for k in range(n):
            rows = arrs[k].shape[0] if gather else arrs[k].shape[0] // NDEV
            if gather:
                src_of = lambda idx: ins[k]
                dst_of = lambda idx: outs[k].at[pl.ds(me * rows, rows)]
                mine = (ins[k], outs[k].at[pl.ds(me * rows, rows)])
            else:
                src_of = lambda idx: ins[k].at[pl.ds(idx * rows, rows)]
                dst_of = lambda idx: outs[k].at[me]
                mine = (ins[k].at[pl.ds(me * rows, rows)], outs[k].at[me])
            cp = pltpu.make_async_copy(mine[0], mine[1], local_sems.at[k])
            cp.start()
            local.append(cp)
            for pos, idx in peers:
                pltpu.make_async_remote_copy(src_ref=src_of(idx), dst_ref=dst_of(idx), send_sem=send_sems.at[k],
                                             recv_sem=recv_sems.at[k], device_id=pos, device_id_type=MESH).start()
        for k in range(n):
            rows = arrs[k].shape[0] if gather else arrs[k].shape[0] // NDEV
            sent = ins[k].at[pl.ds(0, (NDEV - 1) * rows)] if not gather else outs[k].at[pl.ds(0, (NDEV - 1) * rows)]
            got = outs[k].at[pl.ds(0, (NDEV - 1) * rows)] if gather else outs[k].at[pl.ds(0, NDEV - 1)]
            pltpu.make_async_remote_copy(src_ref=sent, dst_ref=sent, send_sem=send_sems.at[k], recv_sem=recv_sems.at[k],
                                         device_id=me_pos, device_id_type=MESH).wait_send()
            pltpu.make_async_remote_copy(src_ref=got, dst_ref=got, send_sem=send_sems.at[k], recv_sem=recv_sems.at[k],
                                         device_id=me_pos, device_id_type=MESH).wait_recv()
            local[k].wait()

    return pl.pallas_call(
        body, out_shape=out_shape, in_specs=n * [hbm], out_specs=n * [hbm], name=name,
        scratch_shapes=[pltpu.SemaphoreType.DMA((n,)), pltpu.SemaphoreType.DMA((n,)), pltpu.SemaphoreType.DMA((n,))],
        compiler_params=pltpu.CompilerParams(has_side_effects=True),
    )(*arrs)


_HBM = pl.BlockSpec(memory_space=pltpu.HBM)
_SEM = pl.BlockSpec(memory_space=pltpu.SEMAPHORE)
_DATAFLOW = pltpu.SideEffectType.DATAFLOW_SIDE_EFFECTING


def _split_start(srcs, groups, gather, name):
    n = len(srcs)
    if gather:
        lands = [lax.empty((NDEV * a.shape[0], a.shape[1]), a.dtype) for a in srcs]
    else:
        lands = [lax.empty((NDEV, a.shape[0] // NDEV, a.shape[1]), a.dtype) for a in srcs]
    n_sem = 3 * len(groups)

    def body(*refs):
        src_refs, land_refs = refs[:n], refs[n:2 * n]
        sems = refs[2 * n:2 * n + n_sem]
        token = refs[-1]
        (_, my), sibling, chips = _places()
        _, _, peers = _peers()
        targets = [sibling] + chips if gather else peers
        for g, members in enumerate(groups):
            for j, k in enumerate(members):
                _own_copy(src_refs[k], land_refs[k], sems[3 * g + 2].at[j], my, gather).start()
        for g, members in enumerate(groups):
            for j, k in enumerate(members):
                rows = srcs[k].shape[0] if gather else srcs[k].shape[0] // NDEV
                for pos, idx in targets:
                    src = src_refs[k] if gather else src_refs[k].at[pl.ds(idx * rows, rows)]
                    dst = land_refs[k].at[pl.ds(my * rows, rows)] if gather else land_refs[k].at[my]
                    pltpu.make_async_remote_copy(src_ref=src, dst_ref=dst, send_sem=sems[3 * g].at[j],
                                                 recv_sem=sems[3 * g + 1].at[j], device_id=pos, device_id_type=MESH).start()
        token[...] = jnp.zeros_like(token)

    out_shape = []
    for members in groups:
        out_shape += 3 * [pltpu.SemaphoreType.DMA((len(members),))]
    out_shape += [pltpu.HBM(a.shape, a.dtype) for a in srcs] + [pltpu.HBM(a.shape, a.dtype) for a in lands]
    out_shape.append(jax.ShapeDtypeStruct((8, 128), F32))
    res = pl.pallas_call(
        body, name=name, out_shape=tuple(out_shape), in_specs=2 * n * [_HBM],
        out_specs=tuple(n_sem * [_SEM] + 2 * n * [_HBM] + [pl.BlockSpec(memory_space=pltpu.VMEM)]),
        input_output_aliases={i: n_sem + i for i in range(2 * n)},
        compiler_params=pltpu.CompilerParams(has_side_effects=_DATAFLOW),
    )(*[pltpu.with_memory_space_constraint(a, pltpu.HBM) for a in list(srcs) + lands])
    sems = [tuple(res[3 * g:3 * g + 3]) for g in range(len(groups))]
    return sems, list(res[n_sem:n_sem + n]), list(res[n_sem + n:n_sem + 2 * n]), res[-1]


def _own_copy(src_ref, land_ref, sem, my, gather):
    if gather:
        rows = src_ref.shape[0]
        return pltpu.make_async_copy(src_ref, land_ref.at[pl.ds(my * rows, rows)], sem)
    rows = src_ref.shape[0] // NDEV
    return pltpu.make_async_copy(src_ref.at[pl.ds(my * rows, rows)], land_ref.at[my], sem)


def _wait_all(land_ref, blocks_per_dev, copies, send_sem, recv_sem, me_pos):
    part = land_ref.at[pl.ds(0, copies * blocks_per_dev)]
    pltpu.make_async_remote_copy(src_ref=part, dst_ref=part, send_sem=send_sem, recv_sem=recv_sem,
                                 device_id=me_pos, device_id_type=MESH).wait()


def _gather_forward(sems, srcs, lands, after, name):
    n = len(srcs)

    def body(*refs):
        land_refs = refs[n:2 * n]
        send_a, recv_a = refs[2 * n], refs[2 * n + 1]
        send_b, recv_b = refs[2 * n + 3], refs[2 * n + 4]
        token = refs[-1]
        (me_pos, _), sibling, chips = _places()
        for j in range(n):
            _wait_all(land_refs[j], lands[j].shape[0] // NDEV, 1 + OTHER_CHIPS, send_a.at[j], recv_a.at[j], me_pos)
        for j in range(n):
            rows = lands[j].shape[0] // NDEV
            for _, idx in chips:
                block = land_refs[j].at[pl.ds(idx * rows, rows)]
                pltpu.make_async_remote_copy(src_ref=block, dst_ref=block, send_sem=send_b.at[j], recv_sem=recv_b.at[j],
                                             device_id=sibling[0], device_id_type=MESH).start()
        token[...] = jnp.zeros_like(token)

    res = pl.pallas_call(
        body, name=name,
        out_shape=(pltpu.SemaphoreType.DMA((n,)), pltpu.SemaphoreType.DMA((n,)))
        + tuple(pltpu.HBM(a.shape, a.dtype) for a in list(srcs) + list(lands)) + (jax.ShapeDtypeStruct((8, 128), F32),),
        in_specs=2 * n * [_HBM] + [_SEM, _SEM, pl.BlockSpec(memory_space=pl.ANY)],
        out_specs=tuple([_SEM, _SEM] + 2 * n * [_HBM] + [pl.BlockSpec(memory_space=pltpu.VMEM)]),
        input_output_aliases={i: 2 + i for i in range(2 * n)},
        compiler_params=pltpu.CompilerParams(has_side_effects=_DATAFLOW),
    )(*srcs, *lands, sems[0], sems[1], after)
    return (res[0], res[1]), list(res[2:2 + n]), list(res[2 + n:2 + 2 * n]), res[-1]


def _split_wait(sems, srcs, lands, after, copies, gather, name):
    n = len(srcs)

    def body(*refs):
        src_refs, land_refs = refs[:n], refs[n:2 * n]
        send_sem, recv_sem, local_sem = refs[2 * n], refs[2 * n + 1], refs[2 * n + 2]
        (me_pos, my), _, _ = _places()
        for j in range(n):
            _wait_all(land_refs[j], lands[j].shape[0] // NDEV, copies, send_sem.at[j], recv_sem.at[j], me_pos)
            _own_copy(src_refs[j], land_refs[j], local_sem.at[j], my, gather).wait()

    res = pl.pallas_call(
        body, name=name, out_shape=tuple(pltpu.HBM(a.shape, a.dtype) for a in list(srcs) + list(lands)),
        in_specs=2 * n * [_HBM] + [_SEM, _SEM, _SEM, pl.BlockSpec(memory_space=pl.ANY)], out_specs=tuple(2 * n * [_HBM]),
        input_output_aliases={i: i for i in range(2 * n)},
        compiler_params=pltpu.CompilerParams(has_side_effects=_DATAFLOW),
    )(*srcs, *lands, sems[0], sems[1], sems[2], after)
    return list(res[n:])


def _chained(gate, mid, after):
    return gate if mid is None else gate + mid(after)[:1, :1]


def _ffn_fwd(x, norms, mod, w, mid=None):
    (pre_g, post_g), (shift, scale, gate), (wg_t, wu_t, wd) = norms, mod, w
    hn, g, u, a = _ffn_up(x, pre_g, scale, shift, wg_t, wu_t, "ffn_up")
    x_out, f = _mm_post(a, wd, x, post_g, _chained(gate, mid, a), FFN_RES, "ffn_down")
    return x_out, (x, hn, g, u, a, f)


def _ffn_bwd(dx_out, saved, norms, mod, w, send=None):
    (pre_g, post_g), (_, scale, gate), (wg_t, wu_t, wd) = norms, mod, w
    x, hn, g, u, a, f = saved
    d_model = x.shape[1]
    sent = (lambda j, dw: None) if send is None else send
    df, dgate, dpost = _post_bwd(dx_out, f, post_g, gate, FFN_RES, "ffn_post_bwd")
    dwd = _mm([(a, df)], "tn", BF16, 256, d_model, "ffn_dw")
    dg, du = _ffn_dgu(df, wd, g, u, "ffn_dgu", after=sent(2, dwd))
    dwg_t = _mm([(dg, hn)], "tn", BF16, 256, d_model, "ffn_dw")
    dwu_t = _mm([(du, hn)], "tn", BF16, 256, d_model, "ffn_dw", after=sent(0, dwg_t))
    dhn = _mm([(dg, wg_t), (du, wu_t)], "nn", F32, TOKEN_TILE, d_model, "ffn_dhn", after=sent(1, dwu_t))
    dx, dshift, dscale, dpre = _prenorm_bwd(dx_out, [dhn], x, pre_g, scale, "prenorm_bwd")
    return dx, (dpre, dpost), (dshift, dscale, dgate), (dwg_t, dwu_t, dwd)


def _mla_fwd(x, norms, mod, w, rope, mid=None):
    (pre_g, post_g), (shift, scale, gate) = norms, mod
    w_in, q_norm, wq_t, kv_norm, wkv_t, wo = w
    hn, lat = _prenorm_mm(x, pre_g, scale, shift, w_in, "nn", F32, LAT_PAD, "mla_in")
    gate = _chained(gate, mid, lat)
    q, k, v, qn, kvn = _mla_qkv(lat, q_norm, kv_norm, wq_t, wkv_t, rope, "mla_qkv")
    o = _mla_attn_fwd(q, k, v, "mla_attn_fwd")
    x_out, f = _mm_post(o, wo, x, post_g, gate, 1.0, "mla_out")
    return x_out, (x, hn, lat, q, k, v, qn, kvn, o, f)


def _mla_bwd(dx_out, saved, norms, mod, w, rope):
    (pre_g, post_g), (_, scale, gate) = norms, mod
    w_in, q_norm, wq_t, kv_norm, wkv_t, wo = w
    x, hn, lat, q, k, v, qn, kvn, o, f = saved
    d_model = x.shape[1]
    df, dgate, dpost = _post_bwd(dx_out, f, post_g, gate, 1.0, "mix_post_bwd")
    d_o = _mm([(df, wo)], "nt", F32, TOKEN_TILE, wo.shape[0], "mla_do")
    dwo = _mm([(o, df)], "tn", BF16, TOKEN_TILE, d_model, "mla_dwo")
    dq, dk, dv = _mla_attn_bwd(q, k, v, d_o, "mla_attn_bwd")
    dqp, dkv, dlat, dq_norm, dkv_norm = _mla_qkv_bwd(dq, dk, dv, lat, q_norm, kv_norm, wq_t, wkv_t, rope, "mla_qkv_bwd")
    dwq_t = _mm([(dqp, qn)], "tn", BF16, TOKEN_TILE, Q_LORA, "mla_dwq")
    dwkv_t = _mm([(dkv, kvn)], "tn", BF16, TOKEN_TILE, KV_LORA, "mla_dwkv")
    dw_in = _mm([(hn, dlat)], "tn", BF16, TOKEN_TILE, LAT_PAD, "mla_dwin")
    dhn = _mm([(dlat, w_in)], "nt", F32, TOKEN_TILE, d_model, "mla_dhn")
    dx, dshift, dscale, dpre = _prenorm_bwd(dx_out, [dhn], x, pre_g, scale, "prenorm_bwd")
    return dx, (dpre, dpost), (dshift, dscale, dgate), (dw_in, dq_norm, dwq_t, dkv_norm, dwkv_t, dwo)


def _dil_fwd(x, norms, mod, w, bias, mid=None):
    (pre_g, post_g), (shift, scale, gate), (w_in_t, wo) = norms, mod, w
    width = 3 * DIL_HEADS * DIL_HEAD_DIM
    hns, qkvs, outs, lses = [], [], [], []
    for g, (window, dilation) in enumerate(DIL_GROUPS):
        hn, qkv = _prenorm_mm(x, pre_g, scale, shift, w_in_t[g * width:(g + 1) * width], "nt", BF16, width,
                              "dil_in", perm=dilation)
        if g == 0:
            gate = _chained(gate, mid, qkv)
        o, lse = _dil_attn_fwd(qkv, bias[g], dilation, window // dilation, "dil_attn_fwd")
        hns.append(hn), qkvs.append(qkv), outs.append(o), lses.append(lse)
    alphas, o_mix, o_mix_b = _dil_mix(lses, outs, "dil_mix")
    x_out, f = _mm_post(o_mix_b, wo, x, post_g, gate, 1.0, "dil_out")
    return x_out, (x, hns, qkvs, lses, alphas, o_mix, o_mix_b, f)


def _dil_bwd(dx_out, saved, norms, mod, w, bias):
    (pre_g, post_g), (_, scale, gate), (w_in_t, wo) = norms, mod, w
    x, hns, qkvs, lses, alphas, o_mix, o_mix_b, f = saved
    d_model = x.shape[1]
    inner = DIL_HEADS * DIL_HEAD_DIM
    df, dgate, dpost = _post_bwd(dx_out, f, post_g, gate, 1.0, "mix_post_bwd")
    d_o = _mm([(df, wo)], "nt", F32, TOKEN_TILE, inner, "dil_do")
    dwo = _mm([(o_mix_b, df)], "tn", BF16, TOKEN_TILE, d_model, "dil_dwo")
    dhns, dws, dbs = [], [], []
    for g, (window, dilation) in enumerate(DIL_GROUPS):
        grads = _dil_attn_bwd(qkvs[g], bias[g], d_o, o_mix, alphas[g], lses[g], dilation, window // dilation, "dil_attn_bwd")
        dbs.append(grads[3])
        w_parts = [w_in_t[(3 * g + j) * inner:(3 * g + j + 1) * inner] for j in range(3)]
        dhns.append(_mm(list(zip(grads[:3], w_parts)), "nn", F32, TOKEN_TILE, d_model, "dil_dhn", out_perm=dilation))
        dws += [_mm([(grads[j], hns[g])], "tn", BF16, TOKEN_TILE, d_model, "dil_dwin") for j in range(3)]
    dx, dshift, dscale, dpre = _prenorm_bwd(dx_out, dhns, x, pre_g, scale, "prenorm_bwd3")
    return dx, (dpre, dpost), (dshift, dscale, dgate), (jnp.concatenate(dws, axis=0), dwo), jnp.concatenate(dbs, axis=0)


def _pad_rows(a, rows):
    return jnp.pad(a, ((0, rows - a.shape[0]), (0, 0)))


def _lanes(a):
    flat = a.reshape(-1).astype(F32)
    rows = -(-flat.shape[0] // 1024) * 8
    return jnp.pad(flat, (0, rows * 128 - flat.shape[0])).reshape(rows, 128)


def kernel(x, c, norm_pre, norm_post, w_mod, b_mod, ffn_w_gate, ffn_w_up, ffn_w_down, mla_w_in, mla_q_norm, mla_w_q_up, mla_kv_norm, mla_w_kv_up, mla_w_o, dil_w_in, dil_w_o, rel_bias, loss_target, m_norm_pre, m_norm_post, m_w_mod, m_b_mod, m_ffn_w_gate, m_ffn_w_up, m_ffn_w_down, m_mla_w_in, m_mla_q_norm, m_mla_w_q_up, m_mla_kv_norm, m_mla_w_kv_up, m_mla_w_o, m_dil_w_in, m_dil_w_o, m_rel_bias, v_norm_pre, v_norm_post, v_w_mod, v_b_mod, v_ffn_w_gate, v_ffn_w_up, v_ffn_w_down, v_mla_w_in, v_mla_q_norm, v_mla_w_q_up, v_mla_kv_norm, v_mla_w_kv_up, v_mla_w_o, v_dil_w_in, v_dil_w_o, v_rel_bias):
    me = 4 * lax.axis_index("x") + 2 * lax.axis_index("y") + lax.axis_index("c")
    depth, n_sub, d_loc = norm_pre.shape
    d_model = x.shape[2]
    mod_loc_cols = w_mod.shape[2]
    x0, target = x[0], loss_target[0]

    bf_t = lambda a: a.astype(BF16).T
    ffn_ids = [(i, h) for i in range(depth) for h in range(2)]
    shards = []
    for i, h in ffn_ids:
        shards += [bf_t(ffn_w_gate[i, h]), bf_t(ffn_w_up[i, h]), ffn_w_down[i, h].astype(BF16)]
    shards += [mla_w_in[0].astype(BF16), bf_t(mla_w_q_up[0]), bf_t(mla_w_kv_up[0]), mla_w_o[0].astype(BF16),
               bf_t(dil_w_in[0]), dil_w_o[0].astype(BF16)]
    n_ffn = 3 * len(ffn_ids)
    members = {(0, 0): [0, 1, 2], (0, 1): [n_ffn, n_ffn + 1, n_ffn + 2, n_ffn + 3], (0, 2): [3, 4, 5],
               (1, 0): [6, 7, 8], (1, 1): [n_ffn + 4, n_ffn + 5], (1, 2): [9, 10, 11]}
    order = [(i, s) for i in range(depth) for s in range(n_sub)]

    small = jnp.concatenate([c.reshape(8, 128), _pad_rows(norm_pre.reshape(depth * n_sub, d_loc), 8),
                             _pad_rows(norm_post.reshape(depth * n_sub, d_loc), 8)], axis=0)
    small_all = _exchange([small], True, "gather_small")[0].reshape(NDEV, 24, 128)
    c_all = small_all[:, 0:8].reshape(NDEV, d_model)
    gains = lambda lo: jnp.transpose(small_all[:, lo:lo + depth * n_sub], (1, 0, 2)).reshape(depth, n_sub, 1, d_model)
    pre_full, post_full = gains(8), gains(16)

    b_loc = lax.dynamic_slice(b_mod, (0, me * mod_loc_cols), (depth, mod_loc_cols))
    mod_cols, silu_c = _mod_fwd(c_all, w_mod, b_loc, "mod_fwd")
    mod_all = _exchange([mod_cols.reshape(depth * NDEV, mod_loc_cols)], True, "gather_mod")[0]
    mod_all = mod_all.reshape(NDEV, depth, NDEV, mod_loc_cols)
    mod_mine = lax.dynamic_index_in_dim(mod_all, me, axis=2, keepdims=False)
    mod = jnp.transpose(mod_mine, (1, 0, 2)).reshape(depth, n_sub, 3, 1, d_model)

    shards[0], _ = lax.optimization_barrier((shards[0], mod_all))
    g_sems, g_srcs, g_lands, _ = _split_start(shards, [members[k] for k in order], True, "gather_weights_start")

    forwarded = {}

    def forward(key, after):
        idx = members[key]
        forwarded[key] = _gather_forward(g_sems[order.index(key)], [g_srcs[k] for k in idx], [g_lands[k] for k in idx], after,
                                         "gather_forward_%d%d" % key)
        return forwarded[key][3]

    def weights_of(key, after):
        (send_b, recv_b), srcs, lands, _ = forwarded[key]
        local = g_sems[order.index(key)][2]
        return _split_wait((send_b, recv_b, local), srcs, lands, after, OTHER_CHIPS, True, "gather_wait_%d%d" % key)

    lat_real = Q_LORA + KV_LORA
    qk = QK_NOPE + QK_ROPE

    def mla_weights(after):
        w_in, wq_t, wkv_t, wo = weights_of((0, 1), after)
        w_in_pad = jnp.concatenate([w_in[:, :lat_real], jnp.zeros((d_model, QK_NOPE), BF16), w_in[:, lat_real:],
                                    jnp.zeros((d_model, HEAD_PAD - QK_NOPE - QK_ROPE), BF16)], axis=1)
        wq_pad = jnp.pad(wq_t.reshape(MLA_HEADS, qk, Q_LORA), ((0, 0), (0, HEAD_PAD - qk), (0, 0)))
        wo_pad = jnp.pad(wo.reshape(MLA_HEADS, V_HEAD, d_model), ((0, 0), (HEAD_PAD - V_HEAD, 0), (0, 0)))
        return (w_in_pad, mla_q_norm, wq_pad.reshape(MLA_HEADS * HEAD_PAD, Q_LORA), mla_kv_norm, wkv_t,
                wo_pad.reshape(MLA_HEADS * HEAD_PAD, d_model))

    rope = _rope_tables()
    buckets = jnp.stack([_dil_buckets(dil) for _, dil in DIL_GROUPS])
    onehot = (buckets[..., None] == jnp.arange(N_BUCKETS)).astype(F32)
    bias = jnp.einsum("gqkb,bgh->ghqk", onehot, rel_bias.reshape(N_BUCKETS, len(DIL_GROUPS), DIL_HEADS),
                      precision=lax.Precision.HIGHEST)

    norms = lambda i, s: (pre_full[i, s], post_full[i, s])
    mods = lambda i, s: (mod[i, s, 0], mod[i, s, 1], mod[i, s, 2])
    saved, weights = {}, {}
    h = x0
    forward(order[0], h)
    for n, (i, s) in enumerate(order):
        weights[i, s] = mla_weights(h) if (s == 1 and i % 2 == 0) else tuple(weights_of((i, s), h))
        mid = None if n + 1 == len(order) else (lambda after, nxt=order[n + 1]: forward(nxt, after))
        if s != 1:
            h, saved[i, s] = _ffn_fwd(h, norms(i, s), mods(i, s), weights[i, s], mid)
        elif i % 2 == 0:
            h, saved[i, s] = _mla_fwd(h, norms(i, s), mods(i, s), weights[i, s], rope, mid)
        else:
            h, saved[i, s] = _dil_fwd(h, norms(i, s), mods(i, s), weights[i, s], bias, mid)
    dh, loss_parts = _loss_grad(h, target, "loss")

    dnorm, dmod, sent = {}, {}, {}
    token = jnp.zeros((8, 128), F32)
    last = order[0]

    def send_last(j, dw):
        sent[last, j] = _split_start([dw], [[0]], False, "scatter_start_%d%d_%d" % (*last, j))
        return sent[last, j][3]

    for i, s in reversed(order):
        md = mods(i, s)
        md = (md[0], md[1], md[2] + token[:1, :1])
        if (i, s) == last:
            dh, dnorm[i, s], dmod[i, s], _ = _ffn_bwd(dh, saved[i, s], norms(i, s), md, weights[i, s], send_last)
            continue
        if s != 1:
            dh, dnorm[i, s], dmod[i, s], dws = _ffn_bwd(dh, saved[i, s], norms(i, s), md, weights[i, s])
        elif i % 2 == 0:
            dh, dnorm[i, s], dmod[i, s], dmla = _mla_bwd(dh, saved[i, s], norms(i, s), md, weights[i, s], rope)
            dw_in_pad, dq_norm, dwq_pad, dkv_norm, dwkv_t, dwo_pad = dmla
            dw_in = jnp.concatenate([dw_in_pad[:, :lat_real], dw_in_pad[:, lat_real + QK_NOPE:lat_real + qk]], axis=1)
            dwq_t = dwq_pad.reshape(MLA_HEADS, HEAD_PAD, Q_LORA)[:, :qk].reshape(MLA_HEADS * qk, Q_LORA)
            dwo = dwo_pad.reshape(MLA_HEADS, HEAD_PAD, d_model)[:, HEAD_PAD - V_HEAD:].reshape(MLA_HEADS * V_HEAD, d_model)
            dws = (dw_in, dwq_t, dwkv_t, dwo)
        else:
            dh, dnorm[i, s], dmod[i, s], dws, dbias = _dil_bwd(dh, saved[i, s], norms(i, s), md, weights[i, s], bias)
        sent[i, s] = _split_start(list(dws), [list(range(len(dws)))], False, "scatter_start_%d%d" % (i, s))
        token = sent[i, s][3]
    grad_x = dh[None]

    mine = {}
    for key in order[1:]:
        sems, srcs, lands, _ = sent[key]
        parts = _split_wait(sems[0], srcs, lands, dh, NDEV - 1, False, "scatter_wait_%d%d" % key)
        for k, p in zip(members[key], parts):
            mine[k] = _sum_parts(p, "sum_parts")
    for j in (2, 0, 1):
        sems, srcs, lands, _ = sent[last, j]
        parts = _split_wait(sems[0], srcs, lands, dh, NDEV - 1, False, "scatter_wait_%d%d_%d" % (*last, j))
        mine[members[last][j]] = _sum_parts(parts[0], "sum_parts")
    g_gate = jnp.stack([mine[3 * n].T for n in range(len(ffn_ids))]).reshape(ffn_w_gate.shape)
    g_up = jnp.stack([mine[3 * n + 1].T for n in range(len(ffn_ids))]).reshape(ffn_w_up.shape)
    g_down = jnp.stack([mine[3 * n + 2] for n in range(len(ffn_ids))]).reshape(ffn_w_down.shape)
    g_mla_in, g_q_up, g_kv_up, g_mla_o, g_dil_in, g_dil_o = (mine[k] for k in range(n_ffn, n_ffn + 6))
    g_mla_in, g_q_up, g_kv_up, g_mla_o = g_mla_in[None], g_q_up.T[None], g_kv_up.T[None], g_mla_o[None]
    g_dil_in, g_dil_o = g_dil_in.T[None], g_dil_o[None]

    dmod_mine = jnp.concatenate([jnp.concatenate(dmod[i, s], axis=0) for i in range(depth) for s in range(n_sub)], axis=0)
    dpre_mine = jnp.concatenate([dnorm[i, s][0] for i in range(depth) for s in range(n_sub)], axis=0)
    dpost_mine = jnp.concatenate([dnorm[i, s][1] for i in range(depth) for s in range(n_sub)], axis=0)
    dbias_tab = _bias_reduce(dbias, buckets, "bias_reduce")[:, 0, :N_BUCKETS].T
    pieces = [dmod_mine, dpre_mine, dpost_mine, dq_norm, dkv_norm, dbias_tab, jnp.sum(loss_parts).reshape(1, 1)]
    packed = [_lanes(p) for p in pieces]
    offs = [0]
    for p in packed:
        offs.append(offs[-1] + p.shape[0])
    everyone = _exchange([jnp.concatenate(packed, axis=0)], True, "gather_small_grads")[0].reshape(NDEV, offs[-1], 128)
    total = _sum_parts(everyone, "sum_small")
    take = lambda n, shape: total[offs[n]:offs[n + 1]].reshape(-1)[:math.prod(shape)].reshape(shape)
    g_b_mod = take(0, b_mod.shape)
    col0 = me * d_loc
    g_norm_pre = lax.dynamic_slice(take(1, (depth, n_sub, d_model)), (0, 0, col0), norm_pre.shape)
    g_norm_post = lax.dynamic_slice(take(2, (depth, n_sub, d_model)), (0, 0, col0), norm_post.shape)
    g_q_norm, g_kv_norm = take(3, mla_q_norm.shape), take(4, mla_kv_norm.shape)
    g_rel_bias = take(5, rel_bias.shape)
    loss = take(6, ())

    dmod_all = everyone[:, offs[0]:offs[1]].reshape(NDEV, depth, NDEV * mod_loc_cols)
    dmod_cols = lax.dynamic_slice(dmod_all, (0, 0, me * mod_loc_cols), (NDEV, depth, mod_loc_cols))
    silu_t = jnp.pad(silu_c.T, ((0, 0), (0, HEAD_PAD - NDEV)))
    g_w_mod = jnp.stack([_mm([(silu_t, jnp.pad(dmod_cols[:, i], ((0, HEAD_PAD - NDEV), (0, 0))))], "nn", F32, TOKEN_TILE,
                             mod_loc_cols, "mod_bwd") for i in range(depth)])

    ws = (norm_pre, norm_post, w_mod, b_mod, ffn_w_gate, ffn_w_up, ffn_w_down, mla_w_in, mla_q_norm, mla_w_q_up, mla_kv_norm,
          mla_w_kv_up, mla_w_o, dil_w_in, dil_w_o, rel_bias)
    gs = (g_norm_pre, g_norm_post, g_w_mod, g_b_mod, g_gate, g_up, g_down, g_mla_in, g_q_norm, g_q_up, g_kv_norm, g_kv_up,
          g_mla_o, g_dil_in, g_dil_o, g_rel_bias)
    ms = (m_norm_pre, m_norm_post, m_w_mod, m_b_mod, m_ffn_w_gate, m_ffn_w_up, m_ffn_w_down, m_mla_w_in, m_mla_q_norm,
          m_mla_w_q_up, m_mla_kv_norm, m_mla_w_kv_up, m_mla_w_o, m_dil_w_in, m_dil_w_o, m_rel_bias)
    vs = (v_norm_pre, v_norm_post, v_w_mod, v_b_mod, v_ffn_w_gate, v_ffn_w_up, v_ffn_w_down, v_mla_w_in, v_mla_q_norm,
          v_mla_w_q_up, v_mla_kv_norm, v_mla_w_kv_up, v_mla_w_o, v_dil_w_in, v_dil_w_o, v_rel_bias)
    stepped = [_adamw(w, g, m, v, "adamw") for w, g, m, v in zip(ws, gs, ms, vs)]
    deltas, new_m, new_v = zip(*stepped)
    return (loss, grad_x, *gs, *deltas, *new_m, *new_v)
```

```python
import math

import jax
import jax.numpy as jnp
from jax import lax
from jax.experimental import pallas as pl
from jax.experimental.pallas import tpu as pltpu

F32 = jnp.float32
BF16 = jnp.bfloat16
MESH = pl.DeviceIdType.MESH

NDEV = 8
OTHER_CHIPS = 3
D_MODEL = 1024
SEQ = 2048
D_FF = 2816
EPS = 1e-6
FFN_RES = 0.5

MLA_HEADS = 16
Q_LORA = 384
KV_LORA = 256
QK_NOPE = 64
QK_ROPE = 32
V_HEAD = 64
ROPE_THETA = 10000.0
HEAD_PAD = 128
LAT_PAD = Q_LORA + KV_LORA + HEAD_PAD
MLA_SCALE = (QK_NOPE + QK_ROPE) ** -0.5

DIL_GROUPS = ((128, 1), (512, 4), (2048, 16))
DIL_HEADS = 16
DIL_HEAD_DIM = 64
DIL_BLOCK = 128
DIL_PAIRS = DIL_HEADS // 2
DIL_SCALE = DIL_HEAD_DIM ** -0.5
N_BUCKETS = 32
MAX_DISTANCE = 2048

ADAM_LR = 0.001
ADAM_B1 = 0.9
ADAM_B2 = 0.999
ADAM_EPS = 1e-08
ADAM_WD = 0.01
ADAM_STEP = 10

V7X_VMEM_BYTES = 64 * 2**20
VMEM_RESERVE = 10 * 2**20
TOKEN_TILE = 512


def _nbytes(shape, dtype):
    return math.prod(shape) * jnp.dtype(dtype).itemsize


def _params(semantics, blocks, extra=0):
    need = 2 * sum(_nbytes(s, d) for s, d in blocks) + extra + VMEM_RESERVE
    return pltpu.CompilerParams(dimension_semantics=semantics,
                                vmem_limit_bytes=int(min(need, V7X_VMEM_BYTES - VMEM_RESERVE)))


def _pcall(body, out_shape, **kw):
    call = pl.pallas_call(body, out_shape=jax.tree.map(lambda s: pltpu.HBM(s.shape, s.dtype), out_shape), **kw)
    return lambda *args: call(*[pltpu.with_memory_space_constraint(a, pltpu.HBM) for a in args])


def _dot_nn(a, b):
    return lax.dot_general(a, b, (((1,), (0,)), ((), ())), preferred_element_type=F32)


def _dot_nt(a, b):
    return lax.dot_general(a, b, (((1,), (1,)), ((), ())), preferred_element_type=F32)


def _dot_tn(a, b):
    return lax.dot_general(a, b, (((0,), (0,)), ((), ())), preferred_element_type=F32)


_DOTS = {"nn": _dot_nn, "nt": _dot_nt, "tn": _dot_tn}


def _rstd(v):
    return lax.rsqrt(jnp.mean(v * v, axis=-1, keepdims=True) + EPS)


def _rms_bwd(v, r, t):
    return r * t - v * (r * r * r) * jnp.mean(t * v, axis=-1, keepdims=True)


_TOKEN_SPEC = pl.BlockSpec((8, 128), lambda *_: (0, 0))


def _mm(pairs, mode, out_dtype, tm, tn, name, out_perm=1, after=None):
    a0, b0 = pairs[0]
    m_dim = a0.shape[1] if mode == "tn" else a0.shape[0]
    n_dim = b0.shape[0] if mode == "nt" else b0.shape[1]
    tm, tn = min(tm, m_dim // out_perm), min(tn, n_dim)
    assert m_dim % tm == 0 and n_dim % tn == 0, (name, m_dim, n_dim, tm, tn)
    dot = _DOTS[mode]
    npairs = len(pairs)

    def body(*refs):
        acc = None
        for p in range(npairs):
            d = dot(refs[2 * p][...].astype(BF16), refs[2 * p + 1][...].astype(BF16))
            acc = d if acc is None else acc + d
        refs[-1][...] = acc.astype(out_dtype)

    in_specs, blocks, flat = [], [], []
    for a, b in pairs:
        if mode == "nn":
            k = a.shape[1]
            sa, sb = ((tm, k), lambda i, j: (i, 0)), ((k, tn), lambda i, j: (0, j))
        elif mode == "nt":
            k = a.shape[1]
            sa, sb = ((tm, k), lambda i, j: (i, 0)), ((tn, k), lambda i, j: (j, 0))
        else:
            k = a.shape[0]
            sa, sb = ((k, tm), lambda i, j: (0, i)), ((k, tn), lambda i, j: (0, j))
        in_specs += [pl.BlockSpec(*sa), pl.BlockSpec(*sb)]
        blocks += [(sa[0], a.dtype), (sb[0], b.dtype)]
        flat += [a, b]
    if after is not None:
        in_specs.append(_TOKEN_SPEC)
        flat.append(after)
    if out_perm == 1:
        out_shape = (m_dim, n_dim)
        out_spec = pl.BlockSpec((tm, tn), lambda i, j: (i, j))
    else:
        rows = m_dim // out_perm
        assert tn == n_dim and rows % tm == 0, (name, rows, tm)
        nb = rows // tm
        out_shape = (rows, out_perm * n_dim)
        out_spec = pl.BlockSpec((tm, n_dim), lambda i, j: (i % nb, i // nb))
    blocks.append(((tm, tn), out_dtype))
    res = _pcall(
        body, out_shape=jax.ShapeDtypeStruct(out_shape, out_dtype), grid=(m_dim // tm, n_dim // tn),
        in_specs=in_specs, out_specs=out_spec, name=name,
        compiler_params=_params(("parallel", "parallel"), blocks, extra=2 * tm * tn * 4),
    )(*flat)
    return res.reshape(m_dim, n_dim)


def _prenorm_mm(x, pre_g, scale, shift, w, w_mode, out_dtype, tn, name, perm=1):
    s_dim, d_dim = x.shape
    n_dim = w.shape[0] if w_mode == "nt" else w.shape[1]
    rows = s_dim // perm
    tm = min(TOKEN_TILE, rows)
    nb = rows // tm
    tn = min(tn, n_dim)
    assert n_dim % tn == 0
    dot = _DOTS[w_mode]

    def body(x_ref, g_ref, sc_ref, sh_ref, w_ref, hn_ref, o_ref):
        @pl.when(pl.program_id(1) == 0)
        def _():
            xf = x_ref[...]
            hn = (xf * _rstd(xf) * g_ref[...]) * (1.0 + sc_ref[...]) + sh_ref[...]
            hn_ref[...] = hn.astype(BF16)

        o_ref[...] = dot(hn_ref[...], w_ref[...]).astype(out_dtype)

    vec = pl.BlockSpec((1, d_dim), lambda i, j: (0, 0))
    w_block = (tn, d_dim) if w_mode == "nt" else (d_dim, tn)
    w_spec = pl.BlockSpec(w_block, (lambda i, j: (j, 0)) if w_mode == "nt" else (lambda i, j: (0, j)))
    hn, out = _pcall(
        body,
        out_shape=(jax.ShapeDtypeStruct((s_dim, d_dim), BF16), jax.ShapeDtypeStruct((s_dim, n_dim), out_dtype)),
        grid=(s_dim // tm, n_dim // tn),
        in_specs=[pl.BlockSpec((tm, d_dim), lambda i, j: (i % nb, i // nb)), vec, vec, vec, w_spec],
        out_specs=(pl.BlockSpec((tm, d_dim), lambda i, j: (i, 0)), pl.BlockSpec((tm, tn), lambda i, j: (i, j))),
        name=name,
        compiler_params=_params(("parallel", "arbitrary"),
                                [((tm, d_dim), F32), (w_block, BF16), ((tm, d_dim), BF16), ((tm, tn), out_dtype)],
                                extra=3 * tm * d_dim * 4 + tm * tn * 4),
    )(x.reshape(rows, perm * d_dim), pre_g, scale, shift, w)
    return hn, out


def _ffn_up(x, pre_g, scale, shift, wg_t, wu_t, name):
    s_dim, d_dim = x.shape
    f_dim = wg_t.shape[0]
    tm, tn = TOKEN_TILE, f_dim // 2

    def body(x_ref, g_ref, sc_ref, sh_ref, wg_ref, wu_ref, hn_ref, go_ref, uo_ref, a_ref):
        @pl.when(pl.program_id(1) == 0)
        def _():
            xf = x_ref[...]
            hn = (xf * _rstd(xf) * g_ref[...]) * (1.0 + sc_ref[...]) + sh_ref[...]
            hn_ref[...] = hn.astype(BF16)

        hn = hn_ref[...]
        g = _dot_nt(hn, wg_ref[...])
        u = _dot_nt(hn, wu_ref[...])
        go_ref[...] = g.astype(BF16)
        uo_ref[...] = u.astype(BF16)
        a_ref[...] = (g * jax.nn.sigmoid(g) * u).astype(BF16)

    vec = pl.BlockSpec((1, d_dim), lambda i, j: (0, 0))
    w_spec = pl.BlockSpec((tn, d_dim), lambda i, j: (j, 0))
    act = pl.BlockSpec((tm, tn), lambda i, j: (i, j))
    act_shape = jax.ShapeDtypeStruct((s_dim, f_dim), BF16)
    return _pcall(
        body,
        out_shape=(jax.ShapeDtypeStruct((s_dim, d_dim), BF16), act_shape, act_shape, act_shape),
        grid=(s_dim // tm, f_dim // tn),
        in_specs=[pl.BlockSpec((tm, d_dim), lambda i, j: (i, 0)), vec, vec, vec, w_spec, w_spec],
        out_specs=(pl.BlockSpec((tm, d_dim), lambda i, j: (i, 0)), act, act, act),
        name=name,
        compiler_params=_params(("parallel", "arbitrary"),
                                [((tm, d_dim), F32), ((tn, d_dim), BF16), ((tn, d_dim), BF16), ((tm, d_dim), BF16)]
                                + 3 * [((tm, tn), BF16)], extra=3 * tm * d_dim * 4 + 4 * tm * tn * 4),
    )(x, pre_g, scale, shift, wg_t, wu_t)


def _mm_post(a, w, x, post_g, gate, res_w, name):
    s_dim, k_dim = a.shape
    d_dim = w.shape[1]
    tm = TOKEN_TILE

    def body(a_ref, w_ref, x_ref, pg_ref, gt_ref, xo_ref, f_ref):
        f = _dot_nn(a_ref[...], w_ref[...])
        y = f * _rstd(f) * pg_ref[...]
        f_ref[...] = f
        xo_ref[...] = x_ref[...] + (res_w * gt_ref[...]) * y

    vec = pl.BlockSpec((1, d_dim), lambda i: (0, 0))
    row = pl.BlockSpec((tm, d_dim), lambda i: (i, 0))
    out = jax.ShapeDtypeStruct((s_dim, d_dim), F32)
    return _pcall(
        body, out_shape=(out, out), grid=(s_dim // tm,),
        in_specs=[pl.BlockSpec((tm, k_dim), lambda i: (i, 0)), pl.BlockSpec((k_dim, d_dim), lambda i: (0, 0)), row, vec, vec],
        out_specs=(row, row), name=name,
        compiler_params=_params(("parallel",), [((tm, k_dim), BF16), ((k_dim, d_dim), BF16)] + 3 * [((tm, d_dim), F32)],
                                extra=3 * tm * d_dim * 4),
    )(a, w, x, post_g, gate)


def _post_bwd(dx_out, f, post_g, gate, res_w, name):
    s_dim, d_dim = f.shape
    tm = TOKEN_TILE

    def body(dx_ref, f_ref, pg_ref, gt_ref, df_ref, dgate_ref, dpost_ref):
        @pl.when(pl.program_id(0) == 0)
        def _():
            dgate_ref[...] = jnp.zeros_like(dgate_ref)
            dpost_ref[...] = jnp.zeros_like(dpost_ref)

        dx, fv = dx_ref[...], f_ref[...]
        r = _rstd(fv)
        fr = fv * r
        dgate_ref[...] += res_w * jnp.sum(dx * (fr * pg_ref[...]), axis=0, keepdims=True)
        dy = (res_w * gt_ref[...]) * dx
        dpost_ref[...] += jnp.sum(dy * fr, axis=0, keepdims=True)
        df_ref[...] = _rms_bwd(fv, r, dy * pg_ref[...]).astype(BF16)

    vec = pl.BlockSpec((1, d_dim), lambda i: (0, 0))
    row = pl.BlockSpec((tm, d_dim), lambda i: (i, 0))
    vshape = jax.ShapeDtypeStruct((1, d_dim), F32)
    return _pcall(
        body, out_shape=(jax.ShapeDtypeStruct((s_dim, d_dim), BF16), vshape, vshape), grid=(s_dim // tm,),
        in_specs=[row, row, vec, vec], out_specs=(row, vec, vec), name=name,
        compiler_params=_params(("arbitrary",), 3 * [((tm, d_dim), F32)], extra=6 * tm * d_dim * 4),
    )(dx_out, f, post_g, gate)


def _prenorm_bwd(dx_out, dhns, x, pre_g, scale, name):
    s_dim, d_dim = x.shape
    tm = TOKEN_TILE
    n_in = len(dhns)

    def body(*refs):
        dx_ref, x_ref, pg_ref, sc_ref = refs[n_in + 0], refs[n_in + 1], refs[n_in + 2], refs[n_in + 3]
        dxo_ref, dsh_ref, dsc_ref, dpg_ref = refs[n_in + 4:]

        @pl.when(pl.program_id(0) == 0)
        def _():
            dsh_ref[...] = jnp.zeros_like(dsh_ref)
            dsc_ref[...] = jnp.zeros_like(dsc_ref)
            dpg_ref[...] = jnp.zeros_like(dpg_ref)

        dhn = refs[0][...]
        for k in range(1, n_in):
            dhn = dhn + refs[k][...]
        xv = x_ref[...]
        r = _rstd(xv)
        xr = xv * r
        dsh_ref[...] += jnp.sum(dhn, axis=0, keepdims=True)
        dsc_ref[...] += jnp.sum(dhn * (xr * pg_ref[...]), axis=0, keepdims=True)
        dn = dhn * (1.0 + sc_ref[...])
        dpg_ref[...] += jnp.sum(dn * xr, axis=0, keepdims=True)
        dxo_ref[...] = dx_ref[...] + _rms_bwd(xv, r, dn * pg_ref[...])

    vec = pl.BlockSpec((1, d_dim), lambda i: (0, 0))
    row = pl.BlockSpec((tm, d_dim), lambda i: (i, 0))
    vshape = jax.ShapeDtypeStruct((1, d_dim), F32)
    return _pcall(
        body, out_shape=(jax.ShapeDtypeStruct((s_dim, d_dim), F32), vshape, vshape, vshape), grid=(s_dim // tm,),
        in_specs=n_in * [row] + [row, row, vec, vec], out_specs=(row, vec, vec, vec), name=name,
        compiler_params=_params(("arbitrary",), (n_in + 3) * [((tm, d_dim), F32)], extra=6 * tm * d_dim * 4),
    )(*dhns, dx_out, x, pre_g, scale)


def _ffn_dgu(df, wd, g, u, name, after=None):
    s_dim, d_dim = df.shape
    f_dim = wd.shape[0]
    tm, tn = TOKEN_TILE, f_dim // 2

    def body(df_ref, wd_ref, g_ref, u_ref, *rest):
        dg_ref, du_ref = rest[-2:]
        da = _dot_nt(df_ref[...], wd_ref[...])
        gv, uv = g_ref[...].astype(F32), u_ref[...].astype(F32)
        sg = jax.nn.sigmoid(gv)
        du_ref[...] = (da * (gv * sg)).astype(BF16)
        dg_ref[...] = (da * uv * (sg * (1.0 + gv * (1.0 - sg)))).astype(BF16)

    act = pl.BlockSpec((tm, tn), lambda i, j: (i, j))
    act_shape = jax.ShapeDtypeStruct((s_dim, f_dim), BF16)
    token = [] if after is None else [after]
    return _pcall(
        body, out_shape=(act_shape, act_shape), grid=(s_dim // tm, f_dim // tn),
        in_specs=[pl.BlockSpec((tm, d_dim), lambda i, j: (i, 0)), pl.BlockSpec((tn, d_dim), lambda i, j: (j, 0)), act, act]
        + len(token) * [_TOKEN_SPEC],
        out_specs=(act, act), name=name,
        compiler_params=_params(("parallel", "parallel"), [((tm, d_dim), BF16), ((tn, d_dim), BF16)] + 4 * [((tm, tn), BF16)],
                                extra=6 * tm * tn * 4),
    )(df, wd, g, u, *token)


def _rope_tables():
    half = QK_ROPE // 2
    freqs = ROPE_THETA ** (-jnp.arange(half, dtype=F32) / half)
    ang = jnp.arange(SEQ, dtype=F32)[:, None] * freqs[None, :]
    cos, sin = jnp.cos(ang), jnp.sin(ang)
    ones = jnp.ones((SEQ, QK_NOPE), F32)
    zeros = jnp.zeros((SEQ, QK_NOPE), F32)
    pad1 = jnp.ones((SEQ, HEAD_PAD - QK_NOPE - QK_ROPE), F32)
    pad0 = jnp.zeros((SEQ, HEAD_PAD - QK_NOPE - QK_ROPE), F32)
    zh = jnp.zeros((SEQ, half), F32)
    c = jnp.concatenate([ones, cos, cos, pad1], axis=1)
    s1 = jnp.concatenate([zeros, -sin, zh, pad0], axis=1)
    s2 = jnp.concatenate([zeros, zh, sin, pad0], axis=1)
    return c, s1, s2


def _rope(v, c, s1, s2):
    half = QK_ROPE // 2
    return v * c + pltpu.roll(v, HEAD_PAD - half, 1) * s1 + pltpu.roll(v, half, 1) * s2


def _rope_t(dv, c, s1, s2):
    half = QK_ROPE // 2
    return dv * c + pltpu.roll(dv * s1, half, 1) + pltpu.roll(dv * s2, HEAD_PAD - half, 1)


def _mla_qkv(lat, q_norm, kv_norm, wq_t, wkv_t, rope, name):
    s_dim = lat.shape[0]
    width = MLA_HEADS * HEAD_PAD
    tm = 256

    def body(lat_ref, qg_ref, kg_ref, wq_ref, wkv_ref, c_ref, s1_ref, s2_ref, q_ref, k_ref, v_ref, qn_ref, kvn_ref):
        cq = lat_ref[:, :Q_LORA]
        ckv = lat_ref[:, Q_LORA:Q_LORA + KV_LORA]
        kr = lat_ref[:, Q_LORA + KV_LORA:]
        c, s1, s2 = c_ref[...], s1_ref[...], s2_ref[...]
        qn = (cq * _rstd(cq) * qg_ref[...]).astype(BF16)
        kvn = (ckv * _rstd(ckv) * kg_ref[...]).astype(BF16)
        qn_ref[...] = qn
        kvn_ref[...] = kvn
        q = _dot_nt(qn, wq_ref[...])
        kv = _dot_nt(kvn, wkv_ref[...])
        krr = _rope(kr, c, s1, s2)
        low = lax.broadcasted_iota(jnp.int32, (tm, HEAD_PAD), 1) < QK_NOPE
        for h in range(MLA_HEADS):
            sl = slice(h * HEAD_PAD, (h + 1) * HEAD_PAD)
            q_ref[:, sl] = _rope(q[:, sl], c, s1, s2).astype(BF16)
            kvh = kv[:, sl]
            k_ref[:, sl] = (jnp.where(low, kvh, 0.0) + krr).astype(BF16)
            v_ref[:, sl] = jnp.where(low, 0.0, kvh).astype(BF16)

    row = lambda n: pl.BlockSpec((tm, n), lambda i: (i, 0))
    full = lambda a: pl.BlockSpec(a.shape, lambda i: (0, 0))
    wide = jax.ShapeDtypeStruct((s_dim, width), BF16)
    return _pcall(
        body,
        out_shape=(wide, wide, wide, jax.ShapeDtypeStruct((s_dim, Q_LORA), BF16), jax.ShapeDtypeStruct((s_dim, KV_LORA), BF16)),
        grid=(s_dim // tm,),
        in_specs=[row(LAT_PAD), full(q_norm), full(kv_norm), full(wq_t), full(wkv_t), row(HEAD_PAD), row(HEAD_PAD), row(HEAD_PAD)],
        out_specs=(row(width), row(width), row(width), row(Q_LORA), row(KV_LORA)), name=name,
        compiler_params=_params(("parallel",), [((tm, LAT_PAD), F32), (wq_t.shape, BF16), (wkv_t.shape, BF16)]
                                + 3 * [((tm, width), BF16)], extra=4 * tm * width * 4),
    )(lat, q_norm, kv_norm, wq_t, wkv_t, *rope)


def _mla_probs(q, k_ref, t, tq):
    lo = t * tq
    own = slice(lo, lo + tq)
    s_own = _dot_nt(q, k_ref[own, :]) * MLA_SCALE
    rows = lax.broadcasted_iota(jnp.int32, s_own.shape, 0)
    cols = lax.broadcasted_iota(jnp.int32, s_own.shape, 1)
    s_own = jnp.where(cols <= rows, s_own, -jnp.inf)
    mx = jnp.max(s_own, axis=-1, keepdims=True)
    if t == 0:
        e_own = jnp.exp(s_own - mx)
        return [(e_own * (1.0 / jnp.sum(e_own, axis=-1, keepdims=True)), own)]
    before = slice(0, lo)
    s_pre = _dot_nt(q, k_ref[before, :]) * MLA_SCALE
    mx = jnp.maximum(mx, jnp.max(s_pre, axis=-1, keepdims=True))
    e_own, e_pre = jnp.exp(s_own - mx), jnp.exp(s_pre - mx)
    inv = 1.0 / (jnp.sum(e_own, axis=-1, keepdims=True) + jnp.sum(e_pre, axis=-1, keepdims=True))
    return [(e_pre * inv, before), (e_own * inv, own)]


def _mla_attn_fwd(q, k, v, name):
    s_dim = q.shape[0]
    tq = 512

    def body(q_ref, k_ref, v_ref, o_ref):
        for t in range(s_dim // tq):
            tile = slice(t * tq, (t + 1) * tq)
            o = None
            for p, keys in _mla_probs(q_ref[tile, :], k_ref, t, tq):
                part = _dot_nn(p.astype(BF16), v_ref[keys, :])
                o = part if o is None else o + part
            o_ref[tile, :] = o.astype(BF16)

    head = pl.BlockSpec((s_dim, HEAD_PAD), lambda h: (0, h))
    return _pcall(
        body, out_shape=jax.ShapeDtypeStruct(q.shape, BF16), grid=(MLA_HEADS,),
        in_specs=[head, head, head], out_specs=head, name=name,
        compiler_params=_params(("parallel",), 4 * [((s_dim, HEAD_PAD), BF16)], extra=4 * tq * s_dim * 4),
    )(q, k, v)


def _mla_attn_bwd(q, k, v, d_o, name):
    s_dim = q.shape[0]
    tq = 512

    def body(q_ref, k_ref, v_ref, do_ref, dq_ref, dk_ref, dv_ref):
        dk_ref[...] = jnp.zeros_like(dk_ref)
        dv_ref[...] = jnp.zeros_like(dv_ref)
        for t in range(s_dim // tq):
            tile = slice(t * tq, (t + 1) * tq)
            qt = q_ref[tile, :]
            dot = do_ref[tile, :].astype(BF16)
            pieces = [(p, keys, _dot_nt(dot, v_ref[keys, :])) for p, keys in _mla_probs(qt, k_ref, t, tq)]
            row = None
            for p, _, dp in pieces:
                part = jnp.sum(p * dp, axis=-1, keepdims=True)
                row = part if row is None else row + part
            dq = None
            for p, keys, dp in pieces:
                dsb = (p * (dp - row) * MLA_SCALE).astype(BF16)
                part = _dot_nn(dsb, k_ref[keys, :])
                dq = part if dq is None else dq + part
                dk_ref[keys, :] += _dot_tn(dsb, qt)
                dv_ref[keys, :] += _dot_tn(p.astype(BF16), dot)
            dq_ref[tile, :] = dq

    head = pl.BlockSpec((s_dim, HEAD_PAD), lambda h: (0, h))
    out = jax.ShapeDtypeStruct(q.shape, F32)
    return _pcall(
        body, out_shape=(out, out, out), grid=(MLA_HEADS,),
        in_specs=[head, head, head, head], out_specs=(head, head, head), name=name,
        compiler_params=_params(("parallel",), 3 * [((s_dim, HEAD_PAD), BF16)] + 4 * [((s_dim, HEAD_PAD), F32)],
                                extra=6 * tq * s_dim * 4),
    )(q, k, v, d_o)


def _mla_qkv_bwd(dq, dk, dv, lat, q_norm, kv_norm, wq_t, wkv_t, rope, name):
    s_dim = lat.shape[0]
    width = MLA_HEADS * HEAD_PAD
    tm = 256

    def body(dq_ref, dk_ref, dv_ref, lat_ref, qg_ref, kg_ref, wq_ref, wkv_ref, c_ref, s1_ref, s2_ref,
             dqp_ref, dkv_ref, dlat_ref, dqg_ref, dkg_ref):
        @pl.when(pl.program_id(0) == 0)
        def _():
            dqg_ref[...] = jnp.zeros_like(dqg_ref)
            dkg_ref[...] = jnp.zeros_like(dkg_ref)

        c, s1, s2 = c_ref[...], s1_ref[...], s2_ref[...]
        lane = lax.broadcasted_iota(jnp.int32, (tm, HEAD_PAD), 1)
        low = lane < QK_NOPE
        rot = (lane >= QK_NOPE) & (lane < QK_NOPE + QK_ROPE)
        dkrr = jnp.zeros((tm, HEAD_PAD), F32)
        for h in range(MLA_HEADS):
            sl = slice(h * HEAD_PAD, (h + 1) * HEAD_PAD)
            dqp_ref[:, sl] = _rope_t(dq_ref[:, sl], c, s1, s2).astype(BF16)
            dkh = dk_ref[:, sl]
            dkv_ref[:, sl] = jnp.where(low, dkh, dv_ref[:, sl]).astype(BF16)
            dkrr = dkrr + jnp.where(rot, dkh, 0.0)
        dqn = _dot_nn(dqp_ref[...], wq_ref[...])
        dkvn = _dot_nn(dkv_ref[...], wkv_ref[...])
        cq = lat_ref[:, :Q_LORA]
        ckv = lat_ref[:, Q_LORA:Q_LORA + KV_LORA]
        rq, rkv = _rstd(cq), _rstd(ckv)
        dqg_ref[...] += jnp.sum(dqn * cq * rq, axis=0, keepdims=True)
        dkg_ref[...] += jnp.sum(dkvn * ckv * rkv, axis=0, keepdims=True)
        dlat_ref[:, :Q_LORA] = _rms_bwd(cq, rq, dqn * qg_ref[...])
        dlat_ref[:, Q_LORA:Q_LORA + KV_LORA] = _rms_bwd(ckv, rkv, dkvn * kg_ref[...])
        dlat_ref[:, Q_LORA + KV_LORA:] = _rope_t(dkrr, c, s1, s2)

    row = lambda n: pl.BlockSpec((tm, n), lambda i: (i, 0))
    full = lambda a: pl.BlockSpec(a.shape, lambda i: (0, 0))
    wide = jax.ShapeDtypeStruct((s_dim, width), BF16)
    return _pcall(
        body,
        out_shape=(wide, wide, jax.ShapeDtypeStruct((s_dim, LAT_PAD), F32),
                   jax.ShapeDtypeStruct(q_norm.shape, F32), jax.ShapeDtypeStruct(kv_norm.shape, F32)),
        grid=(s_dim // tm,),
        in_specs=[row(width), row(width), row(width), row(LAT_PAD), full(q_norm), full(kv_norm), full(wq_t), full(wkv_t),
                  row(HEAD_PAD), row(HEAD_PAD), row(HEAD_PAD)],
        out_specs=(row(width), row(width), row(LAT_PAD), full(q_norm), full(kv_norm)), name=name,
        compiler_params=_params(("arbitrary",), 3 * [((tm, width), F32)] + [((tm, LAT_PAD), F32), (wq_t.shape, BF16),
                                                                           (wkv_t.shape, BF16)] + 2 * [((tm, width), BF16)],
                                extra=2 * tm * width * 4),
    )(dq, dk, dv, lat, q_norm, kv_norm, wq_t, wkv_t, *rope)


def _t5_bucket(dist):
    max_exact = N_BUCKETS // 2
    d = jnp.maximum(dist, 1).astype(F32)
    large = max_exact + (jnp.log(d / max_exact) / math.log(MAX_DISTANCE / max_exact)
                         * (N_BUCKETS - max_exact)).astype(jnp.int32)
    large = jnp.minimum(large, N_BUCKETS - 1)
    return jnp.where(dist < max_exact, dist, large)


def _dil_buckets(dilation):
    iq = jnp.arange(DIL_BLOCK)[:, None]
    ik = jnp.arange(2 * DIL_BLOCK)[None, :]
    return _t5_bucket(jnp.maximum(DIL_BLOCK + iq - ik, 0) * dilation)


def _dil_logits(qh, k_ref, bias_h, n, span):
    lo = n * DIL_BLOCK
    if n == 0:
        s = _dot_nt(qh, k_ref[lo:lo + DIL_BLOCK, :]) * DIL_SCALE + bias_h[:, DIL_BLOCK:]
        rel = lax.broadcasted_iota(jnp.int32, s.shape, 0) - lax.broadcasted_iota(jnp.int32, s.shape, 1)
    else:
        s = _dot_nt(qh, k_ref[lo - DIL_BLOCK:lo + DIL_BLOCK, :]) * DIL_SCALE + bias_h
        rel = DIL_BLOCK + lax.broadcasted_iota(jnp.int32, s.shape, 0) - lax.broadcasted_iota(jnp.int32, s.shape, 1)
    return jnp.where((rel >= 0) & (rel <= span), s, -jnp.inf)


def _dil_views(dilation, rows):
    col = lambda which: pl.BlockSpec((rows, HEAD_PAD), lambda p, r: (r, which * DIL_PAIRS + p))
    nat = pl.BlockSpec((rows, HEAD_PAD), lambda p, r: (0, r * DIL_PAIRS + p))
    bias = pl.BlockSpec((2, DIL_BLOCK, 2 * DIL_BLOCK), lambda p, r: (p, 0, 0))
    return col, nat, bias


def _dil_attn_fwd(qkv, bias, dilation, span, name):
    s_dim = qkv.shape[0]
    rows = s_dim // dilation
    d_dim = DIL_HEADS * DIL_HEAD_DIM
    col, nat, bias_spec = _dil_views(dilation, rows)

    def body(q_ref, k_ref, v_ref, b_ref, o_ref, l_ref):
        lane = lax.broadcasted_iota(jnp.int32, (DIL_BLOCK, HEAD_PAD), 1)
        klane = lax.broadcasted_iota(jnp.int32, (2 * DIL_BLOCK, HEAD_PAD), 1)
        for n in range(rows // DIL_BLOCK):
            lo = n * DIL_BLOCK
            kv_rows = slice(lo, lo + DIL_BLOCK) if n == 0 else slice(lo - DIL_BLOCK, lo + DIL_BLOCK)
            qb, vb = q_ref[lo:lo + DIL_BLOCK, :], v_ref[kv_rows, :]
            o_acc = jnp.zeros((DIL_BLOCK, HEAD_PAD), F32)
            lse_acc = jnp.zeros((DIL_BLOCK, HEAD_PAD), F32)
            for h in range(2):
                mine = (lane < DIL_HEAD_DIM) == (h == 0)
                kmine = (klane[:vb.shape[0]] < DIL_HEAD_DIM) == (h == 0)
                logits = _dil_logits(jnp.where(mine, qb, 0), k_ref, b_ref[h], n, span)
                mx = jnp.max(logits, axis=-1, keepdims=True)
                e = jnp.exp(logits - mx)
                tot = jnp.sum(e, axis=-1, keepdims=True)
                lse = mx + jnp.log(tot)
                p = e * (1.0 / tot)
                o_acc = o_acc + _dot_nn(p.astype(BF16), jnp.where(kmine, vb, 0))
                lse_acc = jnp.where(mine, lse, lse_acc)
            o_ref[lo:lo + DIL_BLOCK, :] = o_acc
            l_ref[lo:lo + DIL_BLOCK, :] = lse_acc

    out = jax.ShapeDtypeStruct((rows, dilation * d_dim), F32)
    o, lse = _pcall(
        body, out_shape=(out, out), grid=(DIL_PAIRS, dilation),
        in_specs=[col(0), col(1), col(2), bias_spec], out_specs=(nat, nat), name=name,
        compiler_params=_params(("parallel", "parallel"), 3 * [((rows, HEAD_PAD), BF16)] + 2 * [((rows, HEAD_PAD), F32)]
                                + [((2, DIL_BLOCK, 2 * DIL_BLOCK), F32)], extra=2**21),
    )(qkv, qkv, qkv, bias)
    return o.reshape(s_dim, d_dim), lse.reshape(s_dim, d_dim)


def _dil_mix(lses, outs, name):
    s_dim, d_dim = outs[0].shape
    tm = TOKEN_TILE
    ng = len(outs)

    def body(*refs):
        ls = [refs[g][...] for g in range(ng)]
        mx = ls[0]
        for g in range(1, ng):
            mx = jnp.maximum(mx, ls[g])
        es = [jnp.exp(l - mx) for l in ls]
        tot = es[0]
        for g in range(1, ng):
            tot = tot + es[g]
        o = None
        for g in range(ng):
            al = es[g] / tot
            refs[2 * ng + g][...] = al
            t = al * refs[ng + g][...]
            o = t if o is None else o + t
        refs[3 * ng][...] = o
        refs[3 * ng + 1][...] = o.astype(BF16)

    row = pl.BlockSpec((tm, d_dim), lambda i: (i, 0))
    f = jax.ShapeDtypeStruct((s_dim, d_dim), F32)
    res = _pcall(
        body, out_shape=tuple(ng * [f] + [f, jax.ShapeDtypeStruct((s_dim, d_dim), BF16)]), grid=(s_dim // tm,),
        in_specs=2 * ng * [row], out_specs=tuple((ng + 2) * [row]), name=name,
        compiler_params=_params(("parallel",), (3 * ng + 2) * [((tm, d_dim), F32)], extra=4 * tm * d_dim * 4),
    )(*lses, *outs)
    return res[:ng], res[ng], res[ng + 1]


def _dil_attn_bwd(qkv, bias, d_o, o_mix, alpha, lse, dilation, span, name):
    s_dim = qkv.shape[0]
    rows = s_dim // dilation
    d_dim = DIL_HEADS * DIL_HEAD_DIM
    col, nat, bias_spec = _dil_views(dilation, rows)
    nat_view = lambda a: a.reshape(rows, dilation * d_dim)

    def body(q_ref, k_ref, v_ref, b_ref, do_ref, om_ref, al_ref, l_ref, dq_ref, dk_ref, dv_ref, db_ref, dk_acc, dv_acc):
        @pl.when(pl.program_id(1) == 0)
        def _():
            db_ref[...] = jnp.zeros_like(db_ref)

        dk_acc[...] = jnp.zeros_like(dk_acc)
        dv_acc[...] = jnp.zeros_like(dv_acc)
        lane = lax.broadcasted_iota(jnp.int32, (DIL_BLOCK, HEAD_PAD), 1)
        klane = lax.broadcasted_iota(jnp.int32, (2 * DIL_BLOCK, HEAD_PAD), 1)
        for n in range(rows // DIL_BLOCK):
            lo = n * DIL_BLOCK
            blk = slice(lo, lo + DIL_BLOCK)
            kv_rows = blk if n == 0 else slice(lo - DIL_BLOCK, lo + DIL_BLOCK)
            qb, kb, vb = q_ref[blk, :], k_ref[kv_rows, :], v_ref[kv_rows, :]
            al = al_ref[blk, :]
            dog = al * do_ref[blk, :]
            row_term = dog * om_ref[blk, :]
            lse_b = l_ref[blk, :]
            dq_acc = jnp.zeros((DIL_BLOCK, HEAD_PAD), F32)
            dk_blk = jnp.zeros((kb.shape[0], HEAD_PAD), F32)
            dv_blk = jnp.zeros((kb.shape[0], HEAD_PAD), F32)
            for h in range(2):
                mine = (lane < DIL_HEAD_DIM) == (h == 0)
                kmine = (klane[:kb.shape[0]] < DIL_HEAD_DIM) == (h == 0)
                qh = jnp.where(mine, qb, 0)
                logits = _dil_logits(qh, k_ref, b_ref[h], n, span)
                lse_h = jnp.max(jnp.where(mine, lse_b, -jnp.inf), axis=-1, keepdims=True)
                p = jnp.exp(logits - lse_h)
                dogh = jnp.where(mine, dog, 0.0).astype(BF16)
                dp = _dot_nt(dogh, vb)
                ds = p * (dp - jnp.sum(jnp.where(mine, row_term, 0.0), axis=-1, keepdims=True))
                if n == 0:
                    db_ref[h, :, DIL_BLOCK:] += ds
                else:
                    db_ref[h] += ds
                dsb = (ds * DIL_SCALE).astype(BF16)
                dq_acc = dq_acc + _dot_nn(dsb, jnp.where(kmine, kb, 0))
                dk_blk = dk_blk + _dot_tn(dsb, qh)
                dv_blk = dv_blk + _dot_tn(p.astype(BF16), dogh)
            dq_ref[blk, :] = dq_acc.astype(BF16)
            dk_acc[kv_rows, :] += dk_blk
            dv_acc[kv_rows, :] += dv_blk
        dk_ref[...] = dk_acc[...].astype(BF16)
        dv_ref[...] = dv_acc[...].astype(BF16)

    out_col = pl.BlockSpec((rows, HEAD_PAD), lambda p, r: (r, p))
    grad = jax.ShapeDtypeStruct((s_dim, d_dim), BF16)
    return _pcall(
        body, out_shape=(grad, grad, grad, jax.ShapeDtypeStruct(bias.shape, F32)), grid=(DIL_PAIRS, dilation),
        in_specs=[col(0), col(1), col(2), bias_spec, nat, nat, nat, nat],
        out_specs=(out_col, out_col, out_col, bias_spec), name=name,
        scratch_shapes=[pltpu.VMEM((rows, HEAD_PAD), F32), pltpu.VMEM((rows, HEAD_PAD), F32)],
        compiler_params=_params(("parallel", "arbitrary"), 6 * [((rows, HEAD_PAD), BF16)] + 4 * [((rows, HEAD_PAD), F32)]
                                + 2 * [((2, DIL_BLOCK, 2 * DIL_BLOCK), F32)], extra=2 * rows * HEAD_PAD * 4 + 2**21),
    )(qkv, qkv, qkv, bias, nat_view(d_o), nat_view(o_mix), nat_view(alpha), nat_view(lse))


def _bias_reduce(dbias, buckets, name):
    n_heads = dbias.shape[0]

    def body(db_ref, bk_ref, o_ref):
        ds, bk = db_ref[0], bk_ref[0]
        lane = lax.broadcasted_iota(jnp.int32, (8, HEAD_PAD), 1)
        acc = jnp.zeros((8, HEAD_PAD), F32)
        for b in range(N_BUCKETS):
            acc = jnp.where(lane == b, jnp.sum(jnp.where(bk == b, ds, 0.0)), acc)
        o_ref[0] = acc

    blk = (1, DIL_BLOCK, 2 * DIL_BLOCK)
    return _pcall(
        body, out_shape=jax.ShapeDtypeStruct((n_heads, 8, HEAD_PAD), F32), grid=(n_heads,),
        in_specs=[pl.BlockSpec(blk, lambda h: (h, 0, 0)), pl.BlockSpec(blk, lambda h: (h // DIL_HEADS, 0, 0))],
        out_specs=pl.BlockSpec((1, 8, HEAD_PAD), lambda h: (h, 0, 0)), name=name,
        compiler_params=_params(("parallel",), [(blk, F32), (blk, jnp.int32)], extra=2**20),
    )(dbias, buckets)


def _loss_grad(y, target, name):
    s_dim, d_dim = y.shape
    tm = TOKEN_TILE

    def body(y_ref, t_ref, dy_ref, l_ref):
        @pl.when(pl.program_id(0) == 0)
        def _():
            l_ref[...] = jnp.zeros_like(l_ref)

        err = y_ref[...] - t_ref[...]
        dy_ref[...] = err / d_dim
        sq = (err * err).reshape(tm // 8, 8, d_dim)
        l_ref[...] += 0.5 * jnp.sum(sq, axis=0) / d_dim

    row = pl.BlockSpec((tm, d_dim), lambda i: (i, 0))
    acc = pl.BlockSpec((8, d_dim), lambda i: (0, 0))
    return _pcall(
        body, out_shape=(jax.ShapeDtypeStruct((s_dim, d_dim), F32), jax.ShapeDtypeStruct((8, d_dim), F32)),
        grid=(s_dim // tm,), in_specs=[row, row], out_specs=(row, acc), name=name,
        compiler_params=_params(("arbitrary",), 3 * [((tm, d_dim), F32)], extra=2 * tm * d_dim * 4),
    )(y, target)


def _mod_fwd(c_all, w_mod, b_loc, name):
    depth, d_dim, n = w_mod.shape
    nb = c_all.shape[0]

    def body(c_ref, w_ref, b_ref, o_ref, s_ref):
        cv = c_ref[...]
        sc = cv * jax.nn.sigmoid(cv)
        s_ref[...] = sc
        o_ref[0] = _dot_nn(sc.astype(BF16), w_ref[0].astype(BF16)) + b_ref[0]

    return _pcall(
        body, out_shape=(jax.ShapeDtypeStruct((depth, nb, n), F32), jax.ShapeDtypeStruct((nb, d_dim), F32)), grid=(depth,),
        in_specs=[pl.BlockSpec((nb, d_dim), lambda i: (0, 0)), pl.BlockSpec((1, d_dim, n), lambda i: (i, 0, 0)),
                  pl.BlockSpec((1, 1, n), lambda i: (i, 0, 0))],
        out_specs=(pl.BlockSpec((1, nb, n), lambda i: (i, 0, 0)), pl.BlockSpec((nb, d_dim), lambda i: (0, 0))), name=name,
        compiler_params=_params(("arbitrary",), [((1, d_dim, n), F32)], extra=d_dim * n * 2 + 2**20),
    )(c_all, w_mod, b_loc.reshape(depth, 1, n))


def _sum_parts(parts, name):
    _, rows, cols = parts.shape
    fits = [t for t in range(16, rows // 2 + 1, 16) if rows % t == 0 and NDEV * t * cols * parts.dtype.itemsize <= 3 * 2**20]
    tr = max(fits) if fits else rows

    def body(p_ref, o_ref):
        acc = p_ref[0].astype(F32)
        for k in range(1, NDEV):
            acc = acc + p_ref[k].astype(F32)
        o_ref[...] = acc

    return _pcall(
        body, out_shape=jax.ShapeDtypeStruct((rows, cols), F32), grid=(rows // tr,),
        in_specs=[pl.BlockSpec((NDEV, tr, cols), lambda i: (0, i, 0))], out_specs=pl.BlockSpec((tr, cols), lambda i: (i, 0)),
        name=name, compiler_params=_params(("parallel",), [((NDEV, tr, cols), parts.dtype), ((tr, cols), F32)], extra=2**20),
    )(parts)


def _adamw(w, g, m, v, name):
    shape = w.shape
    cols = shape[-1]
    rows = math.prod(shape[:-1])
    tr = rows
    for cand in (512, 256, 128, 64, 32, 16, 8):
        if rows % cand == 0 and rows > cand and cand * cols * 4 <= 2**21:
            tr = cand
            break

    def body(w_ref, g_ref, m_ref, v_ref, d_ref, mo_ref, vo_ref):
        gv = g_ref[...]
        mn = ADAM_B1 * m_ref[...] + (1.0 - ADAM_B1) * gv
        vn = ADAM_B2 * v_ref[...] + (1.0 - ADAM_B2) * (gv * gv)
        m_hat = mn / (1.0 - ADAM_B1 ** ADAM_STEP)
        v_hat = vn / (1.0 - ADAM_B2 ** ADAM_STEP)
        d_ref[...] = -ADAM_LR * (m_hat / (jnp.sqrt(v_hat) + ADAM_EPS) + ADAM_WD * w_ref[...])
        mo_ref[...] = mn
        vo_ref[...] = vn

    blk = pl.BlockSpec((tr, cols), lambda i: (i, 0))
    out = jax.ShapeDtypeStruct((rows, cols), F32)
    res = _pcall(
        body, out_shape=(out, out, out), grid=(rows // tr,), in_specs=4 * [blk], out_specs=(blk, blk, blk), name=name,
        compiler_params=_params(("parallel",), 7 * [((tr, cols), F32)], extra=4 * tr * cols * 4),
    )(*(a.reshape(rows, cols) for a in (w, g, m, v)))
    return tuple(r.reshape(shape) for r in res)


def _peers():
    x, y, c = lax.axis_index("x"), lax.axis_index("y"), lax.axis_index("c")
    flip = lambda v, f: 1 - v if f else v
    peers = []
    for f in range(1, NDEV):
        px, py, pc = flip(x, f & 4), flip(y, f & 2), flip(c, f & 1)
        peers.append(((px, py, pc), 4 * px + 2 * py + pc))
    return (x, y, c), 4 * x + 2 * y + c, peers


def _places():
    x, y, c = lax.axis_index("x"), lax.axis_index("y"), lax.axis_index("c")
    place = lambda px, py, pc: ((px, py, pc), 4 * px + 2 * py + pc)
    return place(x, y, c), place(x, y, 1 - c), [place(1 - x, y, c), place(x, 1 - y, c), place(1 - x, 1 - y, c)]


def _exchange(arrs, gather, name):
    n = len(arrs)
    hbm = pl.BlockSpec(memory_space=pltpu.HBM)
    if gather:
        out_shape = [jax.ShapeDtypeStruct((NDEV * a.shape[0], a.shape[1]), a.dtype) for a in arrs]
    else:
        out_shape = [jax.ShapeDtypeStruct((NDEV, a.shape[0] // NDEV, a.shape[1]), a.dtype) for a in arrs]

    def body(*refs):
        ins, outs = refs[:n], refs[n:2 * n]
        send_sems, recv_sems, local_sems = refs[2 * n:]
        me_pos, me, peers = _peers()
        local = []
        for k in range(n):
            rows = arrs[k].shape[0] if gather else arrs[k].shape[0] // NDEV
            if gather:
                src_of = lambda idx: ins[k]
                dst_of = lambda idx: outs[k].at[pl.ds(me * rows, rows)]
                mine = (ins[k], outs[k].at[pl.ds(me * rows, rows)])
            else:
                src_of = lambda idx: ins[k].at[pl.ds(idx * rows, rows)]
                dst_of = lambda idx: outs[k].at[me]
                mine = (ins[k].at[pl.ds(me * rows, rows)], outs[k].at[me])
            cp = pltpu.make_async_copy(mine[0], mine[1], local_sems.at[k])
            cp.start()
            local.append(cp)
            for pos, idx in peers:
                pltpu.make_async_remote_copy(src_ref=src_of(idx), dst_ref=dst_of(idx), send_sem=send_sems.at[k],
                                             recv_sem=recv_sems.at[k], device_id=pos, device_id_type=MESH).start()
        for k in range(n):
            rows = arrs[k].shape[0] if gather else arrs[k].shape[0] // NDEV
            sent = ins[k].at[pl.ds(0, (NDEV - 1) * rows)] if not gather else outs[k].at[pl.ds(0, (NDEV - 1) * rows)]
            got = outs[k].at[pl.ds(0, (NDEV - 1) * rows)] if gather else outs[k].at[pl.ds(0, NDEV - 1)]
            pltpu.make_async_remote_copy(src_ref=sent, dst_ref=sent, send_sem=send_sems.at[k], recv_sem=recv_sems.at[k],
                                         device_id=me_pos, device_id_type=MESH).wait_send()
            pltpu.make_async_remote_copy(src_ref=got, dst_ref=got, send_sem=send_sems.at[k], recv_sem=recv_sems.at[k],
                                         device_id=me_pos, device_id_type=MESH).wait_recv()
            local[k].wait()

    return pl.pallas_call(
        body, out_shape=out_shape, in_specs=n * [hbm], out_specs=n * [hbm], name=name,
        scratch_shapes=[pltpu.SemaphoreType.DMA((n,)), pltpu.SemaphoreType.DMA((n,)), pltpu.SemaphoreType.DMA((n,))],
        compiler_params=pltpu.CompilerParams(has_side_effects=True),
    )(*arrs)


_HBM = pl.BlockSpec(memory_space=pltpu.HBM)
_SEM = pl.BlockSpec(memory_space=pltpu.SEMAPHORE)
_DATAFLOW = pltpu.SideEffectType.DATAFLOW_SIDE_EFFECTING


def _split_start(srcs, groups, gather, name):
    n = len(srcs)
    if gather:
        lands = [lax.empty((NDEV * a.shape[0], a.shape[1]), a.dtype) for a in srcs]
    else:
        lands = [lax.empty((NDEV, a.shape[0] // NDEV, a.shape[1]), a.dtype) for a in srcs]
    n_sem = 3 * len(groups)

    def body(*refs):
        src_refs, land_refs = refs[:n], refs[n:2 * n]
        sems = refs[2 * n:2 * n + n_sem]
        token = refs[-1]
        (_, my), sibling, chips = _places()
        _, _, peers = _peers()
        targets = [sibling] + chips if gather else peers
        for g, members in enumerate(groups):
            for j, k in enumerate(members):
                _own_copy(src_refs[k], land_refs[k], sems[3 * g + 2].at[j], my, gather).start()
        for g, members in enumerate(groups):
            for j, k in enumerate(members):
                rows = srcs[k].shape[0] if gather else srcs[k].shape[0] // NDEV
                for pos, idx in targets:
                    src = src_refs[k] if gather else src_refs[k].at[pl.ds(idx * rows, rows)]
                    dst = land_refs[k].at[pl.ds(my * rows, rows)] if gather else land_refs[k].at[my]
                    pltpu.make_async_remote_copy(src_ref=src, dst_ref=dst, send_sem=sems[3 * g].at[j],
                                                 recv_sem=sems[3 * g + 1].at[j], device_id=pos, device_id_type=MESH).start()
        token[...] = jnp.zeros_like(token)

    out_shape = []
    for members in groups:
        out_shape += 3 * [pltpu.SemaphoreType.DMA((len(members),))]
    out_shape += [pltpu.HBM(a.shape, a.dtype) for a in srcs] + [pltpu.HBM(a.shape, a.dtype) for a in lands]
    out_shape.append(jax.ShapeDtypeStruct((8, 128), F32))
    res = pl.pallas_call(
        body, name=name, out_shape=tuple(out_shape), in_specs=2 * n * [_HBM],
        out_specs=tuple(n_sem * [_SEM] + 2 * n * [_HBM] + [pl.BlockSpec(memory_space=pltpu.VMEM)]),
        input_output_aliases={i: n_sem + i for i in range(2 * n)},
        compiler_params=pltpu.CompilerParams(has_side_effects=_DATAFLOW),
    )(*[pltpu.with_memory_space_constraint(a, pltpu.HBM) for a in list(srcs) + lands])
    sems = [tuple(res[3 * g:3 * g + 3]) for g in range(len(groups))]
    return sems, list(res[n_sem:n_sem + n]), list(res[n_sem + n:n_sem + 2 * n]), res[-1]


def _own_copy(src_ref, land_ref, sem, my, gather):
    if gather:
        rows = src_ref.shape[0]
        return pltpu.make_async_copy(src_ref, land_ref.at[pl.ds(my * rows, rows)], sem)
    rows = src_ref.shape[0] // NDEV
    return pltpu.make_async_copy(src_ref.at[pl.ds(my * rows, rows)], land_ref.at[my], sem)


def _wait_all(land_ref, blocks_per_dev, copies, send_sem, recv_sem, me_pos):
    part = land_ref.at[pl.ds(0, copies * blocks_per_dev)]
    pltpu.make_async_remote_copy(src_ref=part, dst_ref=part, send_sem=send_sem, recv_sem=recv_sem,
                                 device_id=me_pos, device_id_type=MESH).wait()


def _gather_forward(sems, srcs, lands, after, name):
    n = len(srcs)

    def body(*refs):
        land_refs = refs[n:2 * n]
        send_a, recv_a = refs[2 * n], refs[2 * n + 1]
        send_b, recv_b = refs[2 * n + 3], refs[2 * n + 4]
        token = refs[-1]
        (me_pos, _), sibling, chips = _places()
        for j in range(n):
            _wait_all(land_refs[j], lands[j].shape[0] // NDEV, 1 + OTHER_CHIPS, send_a.at[j], recv_a.at[j], me_pos)
        for j in range(n):
            rows = lands[j].shape[0] // NDEV
            for _, idx in chips:
                block = land_refs[j].at[pl.ds(idx * rows, rows)]
                pltpu.make_async_remote_copy(src_ref=block, dst_ref=block, send_sem=send_b.at[j], recv_sem=recv_b.at[j],
                                             device_id=sibling[0], device_id_type=MESH).start()
        token[...] = jnp.zeros_like(token)

    res = pl.pallas_call(
        body, name=name,
        out_shape=(pltpu.SemaphoreType.DMA((n,)), pltpu.SemaphoreType.DMA((n,)))
        + tuple(pltpu.HBM(a.shape, a.dtype) for a in list(srcs) + list(lands)) + (jax.ShapeDtypeStruct((8, 128), F32),),
        in_specs=2 * n * [_HBM] + [_SEM, _SEM, pl.BlockSpec(memory_space=pl.ANY)],
        out_specs=tuple([_SEM, _SEM] + 2 * n * [_HBM] + [pl.BlockSpec(memory_space=pltpu.VMEM)]),
        input_output_aliases={i: 2 + i for i in range(2 * n)},
        compiler_params=pltpu.CompilerParams(has_side_effects=_DATAFLOW),
    )(*srcs, *lands, sems[0], sems[1], after)
    return (res[0], res[1]), list(res[2:2 + n]), list(res[2 + n:2 + 2 * n]), res[-1]


def _split_wait(sems, srcs, lands, after, copies, gather, name):
    n = len(srcs)

    def body(*refs):
        src_refs, land_refs = refs[:n], refs[n:2 * n]
        send_sem, recv_sem, local_sem = refs[2 * n], refs[2 * n + 1], refs[2 * n + 2]
        (me_pos, my), _, _ = _places()
        for j in range(n):
            _wait_all(land_refs[j], lands[j].shape[0] // NDEV, copies, send_sem.at[j], recv_sem.at[j], me_pos)
            _own_copy(src_refs[j], land_refs[j], local_sem.at[j], my, gather).wait()

    res = pl.pallas_call(
        body, name=name, out_shape=tuple(pltpu.HBM(a.shape, a.dtype) for a in list(srcs) + list(lands)),
        in_specs=2 * n * [_HBM] + [_SEM, _SEM, _SEM, pl.BlockSpec(memory_space=pl.ANY)], out_specs=tuple(2 * n * [_HBM]),
        input_output_aliases={i: i for i in range(2 * n)},
        compiler_params=pltpu.CompilerParams(has_side_effects=_DATAFLOW),
    )(*srcs, *lands, sems[0], sems[1], sems[2], after)
    return list(res[n:])


def _chained(gate, mid, after):
    return gate if mid is None else gate + mid(after)[:1, :1]


def _ffn_fwd(x, norms, mod, w, mid=None):
    (pre_g, post_g), (shift, scale, gate), (wg_t, wu_t, wd) = norms, mod, w
    hn, g, u, a = _ffn_up(x, pre_g, scale, shift, wg_t, wu_t, "ffn_up")
    x_out, f = _mm_post(a, wd, x, post_g, _chained(gate, mid, a), FFN_RES, "ffn_down")
    return x_out, (x, hn, g, u, a, f)


def _ffn_bwd(dx_out, saved, norms, mod, w, send=None):
    (pre_g, post_g), (_, scale, gate), (wg_t, wu_t, wd) = norms, mod, w
    x, hn, g, u, a, f = saved
    d_model = x.shape[1]
    sent = (lambda j, dw: None) if send is None else send
    df, dgate, dpost = _post_bwd(dx_out, f, post_g, gate, FFN_RES, "ffn_post_bwd")
    dwd = _mm([(a, df)], "tn", BF16, 256, d_model, "ffn_dw")
    dg, du = _ffn_dgu(df, wd, g, u, "ffn_dgu", after=sent(2, dwd))
    dwg_t = _mm([(dg, hn)], "tn", BF16, 256, d_model, "ffn_dw")
    dwu_t = _mm([(du, hn)], "tn", BF16, 256, d_model, "ffn_dw", after=sent(0, dwg_t))
    dhn = _mm([(dg, wg_t), (du, wu_t)], "nn", F32, TOKEN_TILE, d_model, "ffn_dhn", after=sent(1, dwu_t))
    dx, dshift, dscale, dpre = _prenorm_bwd(dx_out, [dhn], x, pre_g, scale, "prenorm_bwd")
    return dx, (dpre, dpost), (dshift, dscale, dgate), (dwg_t, dwu_t, dwd)


def _mla_fwd(x, norms, mod, w, rope, mid=None):
    (pre_g, post_g), (shift, scale, gate) = norms, mod
    w_in, q_norm, wq_t, kv_norm, wkv_t, wo = w
    hn, lat = _prenorm_mm(x, pre_g, scale, shift, w_in, "nn", F32, LAT_PAD, "mla_in")
    gate = _chained(gate, mid, lat)
    q, k, v, qn, kvn = _mla_qkv(lat, q_norm, kv_norm, wq_t, wkv_t, rope, "mla_qkv")
    o = _mla_attn_fwd(q, k, v, "mla_attn_fwd")
    x_out, f = _mm_post(o, wo, x, post_g, gate, 1.0, "mla_out")
    return x_out, (x, hn, lat, q, k, v, qn, kvn, o, f)


def _mla_bwd(dx_out, saved, norms, mod, w, rope):
    (pre_g, post_g), (_, scale, gate) = norms, mod
    w_in, q_norm, wq_t, kv_norm, wkv_t, wo = w
    x, hn, lat, q, k, v, qn, kvn, o, f = saved
    d_model = x.shape[1]
    df, dgate, dpost = _post_bwd(dx_out, f, post_g, gate, 1.0, "mix_post_bwd")
    d_o = _mm([(df, wo)], "nt", F32, TOKEN_TILE, wo.shape[0], "mla_do")
    dwo = _mm([(o, df)], "tn", BF16, TOKEN_TILE, d_model, "mla_dwo")
    dq, dk, dv = _mla_attn_bwd(q, k, v, d_o, "mla_attn_bwd")
    dqp, dkv, dlat, dq_norm, dkv_norm = _mla_qkv_bwd(dq, dk, dv, lat, q_norm, kv_norm, wq_t, wkv_t, rope, "mla_qkv_bwd")
    dwq_t = _mm([(dqp, qn)], "tn", BF16, TOKEN_TILE, Q_LORA, "mla_dwq")
    dwkv_t = _mm([(dkv, kvn)], "tn", BF16, TOKEN_TILE, KV_LORA, "mla_dwkv")
    dw_in = _mm([(hn, dlat)], "tn", BF16, TOKEN_TILE, LAT_PAD, "mla_dwin")
    dhn = _mm([(dlat, w_in)], "nt", F32, TOKEN_TILE, d_model, "mla_dhn")
    dx, dshift, dscale, dpre = _prenorm_bwd(dx_out, [dhn], x, pre_g, scale, "prenorm_bwd")
    return dx, (dpre, dpost), (dshift, dscale, dgate), (dw_in, dq_norm, dwq_t, dkv_norm, dwkv_t, dwo)


def _dil_fwd(x, norms, mod, w, bias, mid=None):
    (pre_g, post_g), (shift, scale, gate), (w_in_t, wo) = norms, mod, w
    width = 3 * DIL_HEADS * DIL_HEAD_DIM
    hns, qkvs, outs, lses = [], [], [], []
    for g, (window, dilation) in enumerate(DIL_GROUPS):
        hn, qkv = _prenorm_mm(x, pre_g, scale, shift, w_in_t[g * width:(g + 1) * width], "nt", BF16, width,
                              "dil_in", perm=dilation)
        if g == 0:
            gate = _chained(gate, mid, qkv)
        o, lse = _dil_attn_fwd(qkv, bias[g], dilation, window // dilation, "dil_attn_fwd")
        hns.append(hn), qkvs.append(qkv), outs.append(o), lses.append(lse)
    alphas, o_mix, o_mix_b = _dil_mix(lses, outs, "dil_mix")
    x_out, f = _mm_post(o_mix_b, wo, x, post_g, gate, 1.0, "dil_out")
    return x_out, (x, hns, qkvs, lses, alphas, o_mix, o_mix_b, f)


def _dil_bwd(dx_out, saved, norms, mod, w, bias):
    (pre_g, post_g), (_, scale, gate), (w_in_t, wo) = norms, mod, w
    x, hns, qkvs, lses, alphas, o_mix, o_mix_b, f = saved
    d_model = x.shape[1]
    inner = DIL_HEADS * DIL_HEAD_DIM
    df, dgate, dpost = _post_bwd(dx_out, f, post_g, gate, 1.0, "mix_post_bwd")
    d_o = _mm([(df, wo)], "nt", F32, TOKEN_TILE, inner, "dil_do")
    dwo = _mm([(o_mix_b, df)], "tn", BF16, TOKEN_TILE, d_model, "dil_dwo")
    dhns, dws, dbs = [], [], []
    for g, (window, dilation) in enumerate(DIL_GROUPS):
        grads = _dil_attn_bwd(qkvs[g], bias[g], d_o, o_mix, alphas[g], lses[g], dilation, window // dilation, "dil_attn_bwd")
        dbs.append(grads[3])
        w_parts = [w_in_t[(3 * g + j) * inner:(3 * g + j + 1) * inner] for j in range(3)]
        dhns.append(_mm(list(zip(grads[:3], w_parts)), "nn", F32, TOKEN_TILE, d_model, "dil_dhn", out_perm=dilation))
        dws += [_mm([(grads[j], hns[g])], "tn", BF16, TOKEN_TILE, d_model, "dil_dwin") for j in range(3)]
    dx, dshift, dscale, dpre = _prenorm_bwd(dx_out, dhns, x, pre_g, scale, "prenorm_bwd3")
    return dx, (dpre, dpost), (dshift, dscale, dgate), (jnp.concatenate(dws, axis=0), dwo), jnp.concatenate(dbs, axis=0)


def _pad_rows(a, rows):
    return jnp.pad(a, ((0, rows - a.shape[0]), (0, 0)))


def _lanes(a):
    flat = a.reshape(-1).astype(F32)
    rows = -(-flat.shape[0] // 1024) * 8
    return jnp.pad(flat, (0, rows * 128 - flat.shape[0])).reshape(rows, 128)


def kernel(x, c, norm_pre, norm_post, w_mod, b_mod, ffn_w_gate, ffn_w_up, ffn_w_down, mla_w_in, mla_q_norm, mla_w_q_up, mla_kv_norm, mla_w_kv_up, mla_w_o, dil_w_in, dil_w_o, rel_bias, loss_target, m_norm_pre, m_norm_post, m_w_mod, m_b_mod, m_ffn_w_gate, m_ffn_w_up, m_ffn_w_down, m_mla_w_in, m_mla_q_norm, m_mla_w_q_up, m_mla_kv_norm, m_mla_w_kv_up, m_mla_w_o, m_dil_w_in, m_dil_w_o, m_rel_bias, v_norm_pre, v_norm_post, v_w_mod, v_b_mod, v_ffn_w_gate, v_ffn_w_up, v_ffn_w_down, v_mla_w_in, v_mla_q_norm, v_mla_w_q_up, v_mla_kv_norm, v_mla_w_kv_up, v_mla_w_o, v_dil_w_in, v_dil_w_o, v_rel_bias):
    me = 4 * lax.axis_index("x") + 2 * lax.axis_index("y") + lax.axis_index("c")
    depth, n_sub, d_loc = norm_pre.shape
    d_model = x.shape[2]
    mod_loc_cols = w_mod.shape[2]
    x0, target = x[0], loss_target[0]

    bf_t = lambda a: a.astype(BF16).T
    ffn_ids = [(i, h) for i in range(depth) for h in range(2)]
    shards = []
    for i, h in ffn_ids:
        shards += [bf_t(ffn_w_gate[i, h]), bf_t(ffn_w_up[i, h]), ffn_w_down[i, h].astype(BF16)]
    shards += [mla_w_in[0].astype(BF16), bf_t(mla_w_q_up[0]), bf_t(mla_w_kv_up[0]), mla_w_o[0].astype(BF16),
               bf_t(dil_w_in[0]), dil_w_o[0].astype(BF16)]
    n_ffn = 3 * len(ffn_ids)
    members = {(0, 0): [0, 1, 2], (0, 1): [n_ffn, n_ffn + 1, n_ffn + 2, n_ffn + 3], (0, 2): [3, 4, 5],
               (1, 0): [6, 7, 8], (1, 1): [n_ffn + 4, n_ffn + 5], (1, 2): [9, 10, 11]}
    order = [(i, s) for i in range(depth) for s in range(n_sub)]

    small = jnp.concatenate([c.reshape(8, 128), _pad_rows(norm_pre.reshape(depth * n_sub, d_loc), 8),
                             _pad_rows(norm_post.reshape(depth * n_sub, d_loc), 8)], axis=0)
    small_all = _exchange([small], True, "gather_small")[0].reshape(NDEV, 24, 128)
    c_all = small_all[:, 0:8].reshape(NDEV, d_model)
    gains = lambda lo: jnp.transpose(small_all[:, lo:lo + depth * n_sub], (1, 0, 2)).reshape(depth, n_sub, 1, d_model)
    pre_full, post_full = gains(8), gains(16)

    b_loc = lax.dynamic_slice(b_mod, (0, me * mod_loc_cols), (depth, mod_loc_cols))
    mod_cols, silu_c = _mod_fwd(c_all, w_mod, b_loc, "mod_fwd")
    mod_all = _exchange([mod_cols.reshape(depth * NDEV, mod_loc_cols)], True, "gather_mod")[0]
    mod_all = mod_all.reshape(NDEV, depth, NDEV, mod_loc_cols)
    mod_mine = lax.dynamic_index_in_dim(mod_all, me, axis=2, keepdims=False)
    mod = jnp.transpose(mod_mine, (1, 0, 2)).reshape(depth, n_sub, 3, 1, d_model)

    shards[0], _ = lax.optimization_barrier((shards[0], mod_all))
    g_sems, g_srcs, g_lands, _ = _split_start(shards, [members[k] for k in order], True, "gather_weights_start")

    forwarded = {}

    def forward(key, after):
        idx = members[key]
        forwarded[key] = _gather_forward(g_sems[order.index(key)], [g_srcs[k] for k in idx], [g_lands[k] for k in idx], after,
                                         "gather_forward_%d%d" % key)
        return forwarded[key][3]

    def weights_of(key, after):
        (send_b, recv_b), srcs, lands, _ = forwarded[key]
        local = g_sems[order.index(key)][2]
        return _split_wait((send_b, recv_b, local), srcs, lands, after, OTHER_CHIPS, True, "gather_wait_%d%d" % key)

    lat_real = Q_LORA + KV_LORA
    qk = QK_NOPE + QK_ROPE

    def mla_weights(after):
        w_in, wq_t, wkv_t, wo = weights_of((0, 1), after)
        w_in_pad = jnp.concatenate([w_in[:, :lat_real], jnp.zeros((d_model, QK_NOPE), BF16), w_in[:, lat_real:],
                                    jnp.zeros((d_model, HEAD_PAD - QK_NOPE - QK_ROPE), BF16)], axis=1)
        wq_pad = jnp.pad(wq_t.reshape(MLA_HEADS, qk, Q_LORA), ((0, 0), (0, HEAD_PAD - qk), (0, 0)))
        wo_pad = jnp.pad(wo.reshape(MLA_HEADS, V_HEAD, d_model), ((0, 0), (HEAD_PAD - V_HEAD, 0), (0, 0)))
        return (w_in_pad, mla_q_norm, wq_pad.reshape(MLA_HEADS * HEAD_PAD, Q_LORA), mla_kv_norm, wkv_t,
                wo_pad.reshape(MLA_HEADS * HEAD_PAD, d_model))

    rope = _rope_tables()
    buckets = jnp.stack([_dil_buckets(dil) for _, dil in DIL_GROUPS])
    onehot = (buckets[..., None] == jnp.arange(N_BUCKETS)).astype(F32)
    bias = jnp.einsum("gqkb,bgh->ghqk", onehot, rel_bias.reshape(N_BUCKETS, len(DIL_GROUPS), DIL_HEADS),
                      precision=lax.Precision.HIGHEST)

    norms = lambda i, s: (pre_full[i, s], post_full[i, s])
    mods = lambda i, s: (mod[i, s, 0], mod[i, s, 1], mod[i, s, 2])
    saved, weights = {}, {}
    h = x0
    forward(order[0], h)
    for n, (i, s) in enumerate(order):
        weights[i, s] = mla_weights(h) if (s == 1 and i % 2 == 0) else tuple(weights_of((i, s), h))
        mid = None if n + 1 == len(order) else (lambda after, nxt=order[n + 1]: forward(nxt, after))
        if s != 1:
            h, saved[i, s] = _ffn_fwd(h, norms(i, s), mods(i, s), weights[i, s], mid)
        elif i % 2 == 0:
            h, saved[i, s] = _mla_fwd(h, norms(i, s), mods(i, s), weights[i, s], rope, mid)
        else:
            h, saved[i, s] = _dil_fwd(h, norms(i, s), mods(i, s), weights[i, s], bias, mid)
    dh, loss_parts = _loss_grad(h, target, "loss")

    dnorm, dmod, sent = {}, {}, {}
    token = jnp.zeros((8, 128), F32)
    last = order[0]

    def send_last(j, dw):
        sent[last, j] = _split_start([dw], [[0]], False, "scatter_start_%d%d_%d" % (*last, j))
        return sent[last, j][3]

    for i, s in reversed(order):
        md = mods(i, s)
        md = (md[0], md[1], md[2] + token[:1, :1])
        if (i, s) == last:
            dh, dnorm[i, s], dmod[i, s], _ = _ffn_bwd(dh, saved[i, s], norms(i, s), md, weights[i, s], send_last)
            continue
        if s != 1:
            dh, dnorm[i, s], dmod[i, s], dws = _ffn_bwd(dh, saved[i, s], norms(i, s), md, weights[i, s])
        elif i % 2 == 0:
            dh, dnorm[i, s], dmod[i, s], dmla = _mla_bwd(dh, saved[i, s], norms(i, s), md, weights[i, s], rope)
            dw_in_pad, dq_norm, dwq_pad, dkv_norm, dwkv_t, dwo_pad = dmla
            dw_in = jnp.concatenate([dw_in_pad[:, :lat_real], dw_in_pad[:, lat_real + QK_NOPE:lat_real + qk]], axis=1)
            dwq_t = dwq_pad.reshape(MLA_HEADS, HEAD_PAD, Q_LORA)[:, :qk].reshape(MLA_HEADS * qk, Q_LORA)
            dwo = dwo_pad.reshape(MLA_HEADS, HEAD_PAD, d_model)[:, HEAD_PAD - V_HEAD:].reshape(MLA_HEADS * V_HEAD, d_model)
            dws = (dw_in, dwq_t, dwkv_t, dwo)
        else:
            dh, dnorm[i, s], dmod[i, s], dws, dbias = _dil_bwd(dh, saved[i, s], norms(i, s), md, weights[i, s], bias)
        sent[i, s] = _split_start(list(dws), [list(range(len(dws)))], False, "scatter_start_%d%d" % (i, s))
        token = sent[i, s][3]
    grad_x = dh[None]

    mine = {}
    for key in order[1:]:
        sems, srcs, lands, _ = sent[key]
        parts = _split_wait(sems[0], srcs, lands, dh, NDEV - 1, False, "scatter_wait_%d%d" % key)
        for k, p in zip(members[key], parts):
            mine[k] = _sum_parts(p, "sum_parts")
    for j in (2, 0, 1):
        sems, srcs, lands, _ = sent[last, j]
        parts = _split_wait(sems[0], srcs, lands, dh, NDEV - 1, False, "scatter_wait_%d%d_%d" % (*last, j))
        mine[members[last][j]] = _sum_parts(parts[0], "sum_parts")
    g_gate = jnp.stack([mine[3 * n].T for n in range(len(ffn_ids))]).reshape(ffn_w_gate.shape)
    g_up = jnp.stack([mine[3 * n + 1].T for n in range(len(ffn_ids))]).reshape(ffn_w_up.shape)
    g_down = jnp.stack([mine[3 * n + 2] for n in range(len(ffn_ids))]).reshape(ffn_w_down.shape)
    g_mla_in, g_q_up, g_kv_up, g_mla_o, g_dil_in, g_dil_o = (mine[k] for k in range(n_ffn, n_ffn + 6))
    g_mla_in, g_q_up, g_kv_up, g_mla_o = g_mla_in[None], g_q_up.T[None], g_kv_up.T[None], g_mla_o[None]
    g_dil_in, g_dil_o = g_dil_in.T[None], g_dil_o[None]

    dmod_mine = jnp.concatenate([jnp.concatenate(dmod[i, s], axis=0) for i in range(depth) for s in range(n_sub)], axis=0)
    dpre_mine = jnp.concatenate([dnorm[i, s][0] for i in range(depth) for s in range(n_sub)], axis=0)
    dpost_mine = jnp.concatenate([dnorm[i, s][1] for i in range(depth) for s in range(n_sub)], axis=0)
    dbias_tab = _bias_reduce(dbias, buckets, "bias_reduce")[:, 0, :N_BUCKETS].T
    pieces = [dmod_mine, dpre_mine, dpost_mine, dq_norm, dkv_norm, dbias_tab, jnp.sum(loss_parts).reshape(1, 1)]
    packed = [_lanes(p) for p in pieces]
    offs = [0]
    for p in packed:
        offs.append(offs[-1] + p.shape[0])
    everyone = _exchange([jnp.concatenate(packed, axis=0)], True, "gather_small_grads")[0].reshape(NDEV, offs[-1], 128)
    total = _sum_parts(everyone, "sum_small")
    take = lambda n, shape: total[offs[n]:offs[n + 1]].reshape(-1)[:math.prod(shape)].reshape(shape)
    g_b_mod = take(0, b_mod.shape)
    col0 = me * d_loc
    g_norm_pre = lax.dynamic_slice(take(1, (depth, n_sub, d_model)), (0, 0, col0), norm_pre.shape)
    g_norm_post = lax.dynamic_slice(take(2, (depth, n_sub, d_model)), (0, 0, col0), norm_post.shape)
    g_q_norm, g_kv_norm = take(3, mla_q_norm.shape), take(4, mla_kv_norm.shape)
    g_rel_bias = take(5, rel_bias.shape)
    loss = take(6, ())

    dmod_all = everyone[:, offs[0]:offs[1]].reshape(NDEV, depth, NDEV * mod_loc_cols)
    dmod_cols = lax.dynamic_slice(dmod_all, (0, 0, me * mod_loc_cols), (NDEV, depth, mod_loc_cols))
    silu_t = jnp.pad(silu_c.T, ((0, 0), (0, HEAD_PAD - NDEV)))
    g_w_mod = jnp.stack([_mm([(silu_t, jnp.pad(dmod_cols[:, i], ((0, HEAD_PAD - NDEV), (0, 0))))], "nn", F32, TOKEN_TILE,
                             mod_loc_cols, "mod_bwd") for i in range(depth)])

    ws = (norm_pre, norm_post, w_mod, b_mod, ffn_w_gate, ffn_w_up, ffn_w_down, mla_w_in, mla_q_norm, mla_w_q_up, mla_kv_norm,
          mla_w_kv_up, mla_w_o, dil_w_in, dil_w_o, rel_bias)
    gs = (g_norm_pre, g_norm_post, g_w_mod, g_b_mod, g_gate, g_up, g_down, g_mla_in, g_q_norm, g_q_up, g_kv_norm, g_kv_up,
          g_mla_o, g_dil_in, g_dil_o, g_rel_bias)
    ms = (m_norm_pre, m_norm_post, m_w_mod, m_b_mod, m_ffn_w_gate, m_ffn_w_up, m_ffn_w_down, m_mla_w_in, m_mla_q_norm,
          m_mla_w_q_up, m_mla_kv_norm, m_mla_w_kv_up, m_mla_w_o, m_dil_w_in, m_dil_w_o, m_rel_bias)
    vs = (v_norm_pre, v_norm_post, v_w_mod, v_b_mod, v_ffn_w_gate, v_ffn_w_up, v_ffn_w_down, v_mla_w_in, v_mla_q_norm,
          v_mla_w_q_up, v_mla_kv_norm, v_mla_w_kv_up, v_mla_w_o, v_dil_w_in, v_dil_w_o, v_rel_bias)
    stepped = [_adamw(w, g, m, v, "adamw") for w, g, m, v in zip(ws, gs, ms, vs)]
    deltas, new_m, new_v = zip(*stepped)
    return (loss, grad_x, *gs, *deltas, *new_m, *new_v)
```

```python
import math

import jax
import jax.numpy as jnp
from jax import lax
from jax.experimental import pallas as pl
from jax.experimental.pallas import tpu as pltpu

F32 = jnp.float32
BF16 = jnp.bfloat16
MESH = pl.DeviceIdType.MESH

NDEV = 8
OTHER_CHIPS = 3
D_MODEL = 1024
SEQ = 2048
D_FF = 2816
EPS = 1e-6
FFN_RES = 0.5

MLA_HEADS = 16
Q_LORA = 384
KV_LORA = 256
QK_NOPE = 64
QK_ROPE = 32
V_HEAD = 64
ROPE_THETA = 10000.0
HEAD_PAD = 128
LAT_PAD = Q_LORA + KV_LORA + HEAD_PAD
MLA_SCALE = (QK_NOPE + QK_ROPE) ** -0.5

DIL_GROUPS = ((128, 1), (512, 4), (2048, 16))
DIL_HEADS = 16
DIL_HEAD_DIM = 64
DIL_BLOCK = 128
DIL_PAIRS = DIL_HEADS // 2
DIL_SCALE = DIL_HEAD_DIM ** -0.5
N_BUCKETS = 32
MAX_DISTANCE = 2048

ADAM_LR = 0.001
ADAM_B1 = 0.9
ADAM_B2 = 0.999
ADAM_EPS = 1e-08
ADAM_WD = 0.01
ADAM_STEP = 10

V7X_VMEM_BYTES = 64 * 2**20
VMEM_RESERVE = 10 * 2**20
TOKEN_TILE = 512


def _nbytes(shape, dtype):
    return math.prod(shape) * jnp.dtype(dtype).itemsize


def _params(semantics, blocks, extra=0):
    need = 2 * sum(_nbytes(s, d) for s, d in blocks) + extra + VMEM_RESERVE
    return pltpu.CompilerParams(dimension_semantics=semantics,
                                vmem_limit_bytes=int(min(need, V7X_VMEM_BYTES - VMEM_RESERVE)))


def _pcall(body, out_shape, **kw):
    call = pl.pallas_call(body, out_shape=jax.tree.map(lambda s: pltpu.HBM(s.shape, s.dtype), out_shape), **kw)
    return lambda *args: call(*[pltpu.with_memory_space_constraint(a, pltpu.HBM) for a in args])


def _dot_nn(a, b):
    return lax.dot_general(a, b, (((1,), (0,)), ((), ())), preferred_element_type=F32)


def _dot_nt(a, b):
    return lax.dot_general(a, b, (((1,), (1,)), ((), ())), preferred_element_type=F32)


def _dot_tn(a, b):
    return lax.dot_general(a, b, (((0,), (0,)), ((), ())), preferred_element_type=F32)


_DOTS = {"nn": _dot_nn, "nt": _dot_nt, "tn": _dot_tn}


def _rstd(v):
    return lax.rsqrt(jnp.mean(v * v, axis=-1, keepdims=True) + EPS)


def _rms_bwd(v, r, t):
    return r * t - v * (r * r * r) * jnp.mean(t * v, axis=-1, keepdims=True)


_TOKEN_SPEC = pl.BlockSpec((8, 128), lambda *_: (0, 0))


def _mm(pairs, mode, out_dtype, tm, tn, name, out_perm=1, after=None):
    a0, b0 = pairs[0]
    m_dim = a0.shape[1] if mode == "tn" else a0.shape[0]
    n_dim = b0.shape[0] if mode == "nt" else b0.shape[1]
    tm, tn = min(tm, m_dim // out_perm), min(tn, n_dim)
    assert m_dim % tm == 0 and n_dim % tn == 0, (name, m_dim, n_dim, tm, tn)
    dot = _DOTS[mode]
    npairs = len(pairs)

    def body(*refs):
        acc = None
        for p in range(npairs):
            d = dot(refs[2 * p][...].astype(BF16), refs[2 * p + 1][...].astype(BF16))
            acc = d if acc is None else acc + d
        refs[-1][...] = acc.astype(out_dtype)

    in_specs, blocks, flat = [], [], []
    for a, b in pairs:
        if mode == "nn":
            k = a.shape[1]
            sa, sb = ((tm, k), lambda i, j: (i, 0)), ((k, tn), lambda i, j: (0, j))
        elif mode == "nt":
            k = a.shape[1]
            sa, sb = ((tm, k), lambda i, j: (i, 0)), ((tn, k), lambda i, j: (j, 0))
        else:
            k = a.shape[0]
            sa, sb = ((k, tm), lambda i, j: (0, i)), ((k, tn), lambda i, j: (0, j))
        in_specs += [pl.BlockSpec(*sa), pl.BlockSpec(*sb)]
        blocks += [(sa[0], a.dtype), (sb[0], b.dtype)]
        flat += [a, b]
    if after is not None:
        in_specs.append(_TOKEN_SPEC)
        flat.append(after)
    if out_perm == 1:
        out_shape = (m_dim, n_dim)
        out_spec = pl.BlockSpec((tm, tn), lambda i, j: (i, j))
    else:
        rows = m_dim // out_perm
        assert tn == n_dim and rows % tm == 0, (name, rows, tm)
        nb = rows // tm
        out_shape = (rows, out_perm * n_dim)
        out_spec = pl.BlockSpec((tm, n_dim), lambda i, j: (i % nb, i // nb))
    blocks.append(((tm, tn), out_dtype))
    res = _pcall(
        body, out_shape=jax.ShapeDtypeStruct(out_shape, out_dtype), grid=(m_dim // tm, n_dim // tn),
        in_specs=in_specs, out_specs=out_spec, name=name,
        compiler_params=_params(("parallel", "parallel"), blocks, extra=2 * tm * tn * 4),
    )(*flat)
    return res.reshape(m_dim, n_dim)


def _prenorm_mm(x, pre_g, scale, shift, w, w_mode, out_dtype, tn, name, perm=1):
    s_dim, d_dim = x.shape
    n_dim = w.shape[0] if w_mode == "nt" else w.shape[1]
    rows = s_dim // perm
    tm = min(TOKEN_TILE, rows)
    nb = rows // tm
    tn = min(tn, n_dim)
    assert n_dim % tn == 0
    dot = _DOTS[w_mode]

    def body(x_ref, g_ref, sc_ref, sh_ref, w_ref, hn_ref, o_ref):
        @pl.when(pl.program_id(1) == 0)
        def _():
            xf = x_ref[...]
            hn = (xf * _rstd(xf) * g_ref[...]) * (1.0 + sc_ref[...]) + sh_ref[...]
            hn_ref[...] = hn.astype(BF16)

        o_ref[...] = dot(hn_ref[...], w_ref[...]).astype(out_dtype)

    vec = pl.BlockSpec((1, d_dim), lambda i, j: (0, 0))
    w_block = (tn, d_dim) if w_mode == "nt" else (d_dim, tn)
    w_spec = pl.BlockSpec(w_block, (lambda i, j: (j, 0)) if w_mode == "nt" else (lambda i, j: (0, j)))
    hn, out = _pcall(
        body,
        out_shape=(jax.ShapeDtypeStruct((s_dim, d_dim), BF16), jax.ShapeDtypeStruct((s_dim, n_dim), out_dtype)),
        grid=(s_dim // tm, n_dim // tn),
        in_specs=[pl.BlockSpec((tm, d_dim), lambda i, j: (i % nb, i // nb)), vec, vec, vec, w_spec],
        out_specs=(pl.BlockSpec((tm, d_dim), lambda i, j: (i, 0)), pl.BlockSpec((tm, tn), lambda i, j: (i, j))),
        name=name,
        compiler_params=_params(("parallel", "arbitrary"),
                                [((tm, d_dim), F32), (w_block, BF16), ((tm, d_dim), BF16), ((tm, tn), out_dtype)],
                                extra=3 * tm * d_dim * 4 + tm * tn * 4),
    )(x.reshape(rows, perm * d_dim), pre_g, scale, shift, w)
    return hn, out


def _ffn_up(x, pre_g, scale, shift, wg_t, wu_t, name):
    s_dim, d_dim = x.shape
    f_dim = wg_t.shape[0]
    tm, tn = TOKEN_TILE, f_dim // 2

    def body(x_ref, g_ref, sc_ref, sh_ref, wg_ref, wu_ref, hn_ref, go_ref, uo_ref, a_ref):
        @pl.when(pl.program_id(1) == 0)
        def _():
            xf = x_ref[...]
            hn = (xf * _rstd(xf) * g_ref[...]) * (1.0 + sc_ref[...]) + sh_ref[...]
            hn_ref[...] = hn.astype(BF16)

        hn = hn_ref[...]
        g = _dot_nt(hn, wg_ref[...])
        u = _dot_nt(hn, wu_ref[...])
        go_ref[...] = g.astype(BF16)
        uo_ref[...] = u.astype(BF16)
        a_ref[...] = (g * jax.nn.sigmoid(g) * u).astype(BF16)

    vec = pl.BlockSpec((1, d_dim), lambda i, j: (0, 0))
    w_spec = pl.BlockSpec((tn, d_dim), lambda i, j: (j, 0))
    act = pl.BlockSpec((tm, tn), lambda i, j: (i, j))
    act_shape = jax.ShapeDtypeStruct((s_dim, f_dim), BF16)
    return _pcall(
        body,
        out_shape=(jax.ShapeDtypeStruct((s_dim, d_dim), BF16), act_shape, act_shape, act_shape),
        grid=(s_dim // tm, f_dim // tn),
        in_specs=[pl.BlockSpec((tm, d_dim), lambda i, j: (i, 0)), vec, vec, vec, w_spec, w_spec],
        out_specs=(pl.BlockSpec((tm, d_dim), lambda i, j: (i, 0)), act, act, act),
        name=name,
        compiler_params=_params(("parallel", "arbitrary"),
                                [((tm, d_dim), F32), ((tn, d_dim), BF16), ((tn, d_dim), BF16), ((tm, d_dim), BF16)]
                                + 3 * [((tm, tn), BF16)], extra=3 * tm * d_dim * 4 + 4 * tm * tn * 4),
    )(x, pre_g, scale, shift, wg_t, wu_t)


def _mm_post(a, w, x, post_g, gate, res_w, name):
    s_dim, k_dim = a.shape
    d_dim = w.shape[1]
    tm = TOKEN_TILE

    def body(a_ref, w_ref, x_ref, pg_ref, gt_ref, xo_ref, f_ref):
        f = _dot_nn(a_ref[...], w_ref[...])
        y = f * _rstd(f) * pg_ref[...]
        f_ref[...] = f
        xo_ref[...] = x_ref[...] + (res_w * gt_ref[...]) * y

    vec = pl.BlockSpec((1, d_dim), lambda i: (0, 0))
    row = pl.BlockSpec((tm, d_dim), lambda i: (i, 0))
    out = jax.ShapeDtypeStruct((s_dim, d_dim), F32)
    return _pcall(
        body, out_shape=(out, out), grid=(s_dim // tm,),
        in_specs=[pl.BlockSpec((tm, k_dim), lambda i: (i, 0)), pl.BlockSpec((k_dim, d_dim), lambda i: (0, 0)), row, vec, vec],
        out_specs=(row, row), name=name,
        compiler_params=_params(("parallel",), [((tm, k_dim), BF16), ((k_dim, d_dim), BF16)] + 3 * [((tm, d_dim), F32)],
                                extra=3 * tm * d_dim * 4),
    )(a, w, x, post_g, gate)


def _post_bwd(dx_out, f, post_g, gate, res_w, name):
    s_dim, d_dim = f.shape
    tm = TOKEN_TILE

    def body(dx_ref, f_ref, pg_ref, gt_ref, df_ref, dgate_ref, dpost_ref):
        @pl.when(pl.program_id(0) == 0)
        def _():
            dgate_ref[...] = jnp.zeros_like(dgate_ref)
            dpost_ref[...] = jnp.zeros_like(dpost_ref)

        dx, fv = dx_ref[...], f_ref[...]
        r = _rstd(fv)
        fr = fv * r
        dgate_ref[...] += res_w * jnp.sum(dx * (fr * pg_ref[...]), axis=0, keepdims=True)
        dy = (res_w * gt_ref[...]) * dx
        dpost_ref[...] += jnp.sum(dy * fr, axis=0, keepdims=True)
        df_ref[...] = _rms_bwd(fv, r, dy * pg_ref[...]).astype(BF16)

    vec = pl.BlockSpec((1, d_dim), lambda i: (0, 0))
    row = pl.BlockSpec((tm, d_dim), lambda i: (i, 0))
    vshape = jax.ShapeDtypeStruct((1, d_dim), F32)
    return _pcall(
        body, out_shape=(jax.ShapeDtypeStruct((s_dim, d_dim), BF16), vshape, vshape), grid=(s_dim // tm,),
        in_specs=[row, row, vec, vec], out_specs=(row, vec, vec), name=name,
        compiler_params=_params(("arbitrary",), 3 * [((tm, d_dim), F32)], extra=6 * tm * d_dim * 4),
    )(dx_out, f, post_g, gate)


def _prenorm_bwd(dx_out, dhns, x, pre_g, scale, name):
    s_dim, d_dim = x.shape
    tm = TOKEN_TILE
    n_in = len(dhns)

    def body(*refs):
        dx_ref, x_ref, pg_ref, sc_ref = refs[n_in + 0], refs[n_in + 1], refs[n_in + 2], refs[n_in + 3]
        dxo_ref, dsh_ref, dsc_ref, dpg_ref = refs[n_in + 4:]

        @pl.when(pl.program_id(0) == 0)
        def _():
            dsh_ref[...] = jnp.zeros_like(dsh_ref)
            dsc_ref[...] = jnp.zeros_like(dsc_ref)
            dpg_ref[...] = jnp.zeros_like(dpg_ref)

        dhn = refs[0][...]
        for k in range(1, n_in):
            dhn = dhn + refs[k][...]
        xv = x_ref[...]
        r = _rstd(xv)
        xr = xv * r
        dsh_ref[...] += jnp.sum(dhn, axis=0, keepdims=True)
        dsc_ref[...] += jnp.sum(dhn * (xr * pg_ref[...]), axis=0, keepdims=True)
        dn = dhn * (1.0 + sc_ref[...])
        dpg_ref[...] += jnp.sum(dn * xr, axis=0, keepdims=True)
        dxo_ref[...] = dx_ref[...] + _rms_bwd(xv, r, dn * pg_ref[...])

    vec = pl.BlockSpec((1, d_dim), lambda i: (0, 0))
    row = pl.BlockSpec((tm, d_dim), lambda i: (i, 0))
    vshape = jax.ShapeDtypeStruct((1, d_dim), F32)
    return _pcall(
        body, out_shape=(jax.ShapeDtypeStruct((s_dim, d_dim), F32), vshape, vshape, vshape), grid=(s_dim // tm,),
        in_specs=n_in * [row] + [row, row, vec, vec], out_specs=(row, vec, vec, vec), name=name,
        compiler_params=_params(("arbitrary",), (n_in + 3) * [((tm, d_dim), F32)], extra=6 * tm * d_dim * 4),
    )(*dhns, dx_out, x, pre_g, scale)


def _ffn_dgu(df, wd, g, u, name, after=None):
    s_dim, d_dim = df.shape
    f_dim = wd.shape[0]
    tm, tn = TOKEN_TILE, f_dim // 2

    def body(df_ref, wd_ref, g_ref, u_ref, *rest):
        dg_ref, du_ref = rest[-2:]
        da = _dot_nt(df_ref[...], wd_ref[...])
        gv, uv = g_ref[...].astype(F32), u_ref[...].astype(F32)
        sg = jax.nn.sigmoid(gv)
        du_ref[...] = (da * (gv * sg)).astype(BF16)
        dg_ref[...] = (da * uv * (sg * (1.0 + gv * (1.0 - sg)))).astype(BF16)

    act = pl.BlockSpec((tm, tn), lambda i, j: (i, j))
    act_shape = jax.ShapeDtypeStruct((s_dim, f_dim), BF16)
    token = [] if after is None else [after]
    return _pcall(
        body, out_shape=(act_shape, act_shape), grid=(s_dim // tm, f_dim // tn),
        in_specs=[pl.BlockSpec((tm, d_dim), lambda i, j: (i, 0)), pl.BlockSpec((tn, d_dim), lambda i, j: (j, 0)), act, act]
        + len(token) * [_TOKEN_SPEC],
        out_specs=(act, act), name=name,
        compiler_params=_params(("parallel", "parallel"), [((tm, d_dim), BF16), ((tn, d_dim), BF16)] + 4 * [((tm, tn), BF16)],
                                extra=6 * tm * tn * 4),
    )(df, wd, g, u, *token)


def _rope_tables():
    half = QK_ROPE // 2
    freqs = ROPE_THETA ** (-jnp.arange(half, dtype=F32) / half)
    ang = jnp.arange(SEQ, dtype=F32)[:, None] * freqs[None, :]
    cos, sin = jnp.cos(ang), jnp.sin(ang)
    ones = jnp.ones((SEQ, QK_NOPE), F32)
    zeros = jnp.zeros((SEQ, QK_NOPE), F32)
    pad1 = jnp.ones((SEQ, HEAD_PAD - QK_NOPE - QK_ROPE), F32)
    pad0 = jnp.zeros((SEQ, HEAD_PAD - QK_NOPE - QK_ROPE), F32)
    zh = jnp.zeros((SEQ, half), F32)
    c = jnp.concatenate([ones, cos, cos, pad1], axis=1)
    s1 = jnp.concatenate([zeros, -sin, zh, pad0], axis=1)
    s2 = jnp.concatenate([zeros, zh, sin, pad0], axis=1)
    return c, s1, s2


def _rope(v, c, s1, s2):
    half = QK_ROPE // 2
    return v * c + pltpu.roll(v, HEAD_PAD - half, 1) * s1 + pltpu.roll(v, half, 1) * s2


def _rope_t(dv, c, s1, s2):
    half = QK_ROPE // 2
    return dv * c + pltpu.roll(dv * s1, half, 1) + pltpu.roll(dv * s2, HEAD_PAD - half, 1)


def _mla_qkv(lat, q_norm, kv_norm, wq_t, wkv_t, rope, name):
    s_dim = lat.shape[0]
    width = MLA_HEADS * HEAD_PAD
    tm = 256

    def body(lat_ref, qg_ref, kg_ref, wq_ref, wkv_ref, c_ref, s1_ref, s2_ref, q_ref, k_ref, v_ref, qn_ref, kvn_ref):
        cq = lat_ref[:, :Q_LORA]
        ckv = lat_ref[:, Q_LORA:Q_LORA + KV_LORA]
        kr = lat_ref[:, Q_LORA + KV_LORA:]
        c, s1, s2 = c_ref[...], s1_ref[...], s2_ref[...]
        qn = (cq * _rstd(cq) * qg_ref[...]).astype(BF16)
        kvn = (ckv * _rstd(ckv) * kg_ref[...]).astype(BF16)
        qn_ref[...] = qn
        kvn_ref[...] = kvn
        q = _dot_nt(qn, wq_ref[...])
        kv = _dot_nt(kvn, wkv_ref[...])
        krr = _rope(kr, c, s1, s2)
        low = lax.broadcasted_iota(jnp.int32, (tm, HEAD_PAD), 1) < QK_NOPE
        for h in range(MLA_HEADS):
            sl = slice(h * HEAD_PAD, (h + 1) * HEAD_PAD)
            q_ref[:, sl] = _rope(q[:, sl], c, s1, s2).astype(BF16)
            kvh = kv[:, sl]
            k_ref[:, sl] = (jnp.where(low, kvh, 0.0) + krr).astype(BF16)
            v_ref[:, sl] = jnp.where(low, 0.0, kvh).astype(BF16)

    row = lambda n: pl.BlockSpec((tm, n), lambda i: (i, 0))
    full = lambda a: pl.BlockSpec(a.shape, lambda i: (0, 0))
    wide = jax.ShapeDtypeStruct((s_dim, width), BF16)
    return _pcall(
        body,
        out_shape=(wide, wide, wide, jax.ShapeDtypeStruct((s_dim, Q_LORA), BF16), jax.ShapeDtypeStruct((s_dim, KV_LORA), BF16)),
        grid=(s_dim // tm,),
        in_specs=[row(LAT_PAD), full(q_norm), full(kv_norm), full(wq_t), full(wkv_t), row(HEAD_PAD), row(HEAD_PAD), row(HEAD_PAD)],
        out_specs=(row(width), row(width), row(width), row(Q_LORA), row(KV_LORA)), name=name,
        compiler_params=_params(("parallel",), [((tm, LAT_PAD), F32), (wq_t.shape, BF16), (wkv_t.shape, BF16)]
                                + 3 * [((tm, width), BF16)], extra=4 * tm * width * 4),
    )(lat, q_norm, kv_norm, wq_t, wkv_t, *rope)


def _mla_probs(q, k_ref, t, tq):
    lo = t * tq
    own = slice(lo, lo + tq)
    s_own = _dot_nt(q, k_ref[own, :]) * MLA_SCALE
    rows = lax.broadcasted_iota(jnp.int32, s_own.shape, 0)
    cols = lax.broadcasted_iota(jnp.int32, s_own.shape, 1)
    s_own = jnp.where(cols <= rows, s_own, -jnp.inf)
    mx = jnp.max(s_own, axis=-1, keepdims=True)
    if t == 0:
        e_own = jnp.exp(s_own - mx)
        return [(e_own * (1.0 / jnp.sum(e_own, axis=-1, keepdims=True)), own)]
    before = slice(0, lo)
    s_pre = _dot_nt(q, k_ref[before, :]) * MLA_SCALE
    mx = jnp.maximum(mx, jnp.max(s_pre, axis=-1, keepdims=True))
    e_own, e_pre = jnp.exp(s_own - mx), jnp.exp(s_pre - mx)
    inv = 1.0 / (jnp.sum(e_own, axis=-1, keepdims=True) + jnp.sum(e_pre, axis=-1, keepdims=True))
    return [(e_pre * inv, before), (e_own * inv, own)]


def _mla_attn_fwd(q, k, v, name):
    s_dim = q.shape[0]
    tq = 512

    def body(q_ref, k_ref, v_ref, o_ref):
        for t in range(s_dim // tq):
            tile = slice(t * tq, (t + 1) * tq)
            o = None
            for p, keys in _mla_probs(q_ref[tile, :], k_ref, t, tq):
                part = _dot_nn(p.astype(BF16), v_ref[keys, :])
                o = part if o is None else o + part
            o_ref[tile, :] = o.astype(BF16)

    head = pl.BlockSpec((s_dim, HEAD_PAD), lambda h: (0, h))
    return _pcall(
        body, out_shape=jax.ShapeDtypeStruct(q.shape, BF16), grid=(MLA_HEADS,),
        in_specs=[head, head, head], out_specs=head, name=name,
        compiler_params=_params(("parallel",), 4 * [((s_dim, HEAD_PAD), BF16)], extra=4 * tq * s_dim * 4),
    )(q, k, v)


def _mla_attn_bwd(q, k, v, d_o, name):
    s_dim = q.shape[0]
    tq = 512

    def body(q_ref, k_ref, v_ref, do_ref, dq_ref, dk_ref, dv_ref):
        dk_ref[...] = jnp.zeros_like(dk_ref)
        dv_ref[...] = jnp.zeros_like(dv_ref)
        for t in range(s_dim // tq):
            tile = slice(t * tq, (t + 1) * tq)
            qt = q_ref[tile, :]
            dot = do_ref[tile, :].astype(BF16)
            pieces = [(p, keys, _dot_nt(dot, v_ref[keys, :])) for p, keys in _mla_probs(qt, k_ref, t, tq)]
            row = None
            for p, _, dp in pieces:
                part = jnp.sum(p * dp, axis=-1, keepdims=True)
                row = part if row is None else row + part
            dq = None
            for p, keys, dp in pieces:
                dsb = (p * (dp - row) * MLA_SCALE).astype(BF16)
                part = _dot_nn(dsb, k_ref[keys, :])
                dq = part if dq is None else dq + part
                dk_ref[keys, :] += _dot_tn(dsb, qt)
                dv_ref[keys, :] += _dot_tn(p.astype(BF16), dot)
            dq_ref[tile, :] = dq

    head = pl.BlockSpec((s_dim, HEAD_PAD), lambda h: (0, h))
    out = jax.ShapeDtypeStruct(q.shape, F32)
    return _pcall(
        body, out_shape=(out, out, out), grid=(MLA_HEADS,),
        in_specs=[head, head, head, head], out_specs=(head, head, head), name=name,
        compiler_params=_params(("parallel",), 3 * [((s_dim, HEAD_PAD), BF16)] + 4 * [((s_dim, HEAD_PAD), F32)],
                                extra=6 * tq * s_dim * 4),
    )(q, k, v, d_o)


def _mla_qkv_bwd(dq, dk, dv, lat, q_norm, kv_norm, wq_t, wkv_t, rope, name):
    s_dim = lat.shape[0]
    width = MLA_HEADS * HEAD_PAD
    tm = 256

    def body(dq_ref, dk_ref, dv_ref, lat_ref, qg_ref, kg_ref, wq_ref, wkv_ref, c_ref, s1_ref, s2_ref,
             dqp_ref, dkv_ref, dlat_ref, dqg_ref, dkg_ref):
        @pl.when(pl.program_id(0) == 0)
        def _():
            dqg_ref[...] = jnp.zeros_like(dqg_ref)
            dkg_ref[...] = jnp.zeros_like(dkg_ref)

        c, s1, s2 = c_ref[...], s1_ref[...], s2_ref[...]
        lane = lax.broadcasted_iota(jnp.int32, (tm, HEAD_PAD), 1)
        low = lane < QK_NOPE
        rot = (lane >= QK_NOPE) & (lane < QK_NOPE + QK_ROPE)
        dkrr = jnp.zeros((tm, HEAD_PAD), F32)
        for h in range(MLA_HEADS):
            sl = slice(h * HEAD_PAD, (h + 1) * HEAD_PAD)
            dqp_ref[:, sl] = _rope_t(dq_ref[:, sl], c, s1, s2).astype(BF16)
            dkh = dk_ref[:, sl]
            dkv_ref[:, sl] = jnp.where(low, dkh, dv_ref[:, sl]).astype(BF16)
            dkrr = dkrr + jnp.where(rot, dkh, 0.0)
        dqn = _dot_nn(dqp_ref[...], wq_ref[...])
        dkvn = _dot_nn(dkv_ref[...], wkv_ref[...])
        cq = lat_ref[:, :Q_LORA]
        ckv = lat_ref[:, Q_LORA:Q_LORA + KV_LORA]
        rq, rkv = _rstd(cq), _rstd(ckv)
        dqg_ref[...] += jnp.sum(dqn * cq * rq, axis=0, keepdims=True)
        dkg_ref[...] += jnp.sum(dkvn * ckv * rkv, axis=0, keepdims=True)
        dlat_ref[:, :Q_LORA] = _rms_bwd(cq, rq, dqn * qg_ref[...])
        dlat_ref[:, Q_LORA:Q_LORA + KV_LORA] = _rms_bwd(ckv, rkv, dkvn * kg_ref[...])
        dlat_ref[:, Q_LORA + KV_LORA:] = _rope_t(dkrr, c, s1, s2)

    row = lambda n: pl.BlockSpec((tm, n), lambda i: (i, 0))
    full = lambda a: pl.BlockSpec(a.shape, lambda i: (0, 0))
    wide = jax.ShapeDtypeStruct((s_dim, width), BF16)
    return _pcall(
        body,
        out_shape=(wide, wide, jax.ShapeDtypeStruct((s_dim, LAT_PAD), F32),
                   jax.ShapeDtypeStruct(q_norm.shape, F32), jax.ShapeDtypeStruct(kv_norm.shape, F32)),
        grid=(s_dim // tm,),
        in_specs=[row(width), row(width), row(width), row(LAT_PAD), full(q_norm), full(kv_norm), full(wq_t), full(wkv_t),
                  row(HEAD_PAD), row(HEAD_PAD), row(HEAD_PAD)],
        out_specs=(row(width), row(width), row(LAT_PAD), full(q_norm), full(kv_norm)), name=name,
        compiler_params=_params(("arbitrary",), 3 * [((tm, width), F32)] + [((tm, LAT_PAD), F32), (wq_t.shape, BF16),
                                                                           (wkv_t.shape, BF16)] + 2 * [((tm, width), BF16)],
                                extra=2 * tm * width * 4),
    )(dq, dk, dv, lat, q_norm, kv_norm, wq_t, wkv_t, *rope)


def _t5_bucket(dist):
    max_exact = N_BUCKETS // 2
    d = jnp.maximum(dist, 1).astype(F32)
    large = max_exact + (jnp.log(d / max_exact) / math.log(MAX_DISTANCE / max_exact)
                         * (N_BUCKETS - max_exact)).astype(jnp.int32)
    large = jnp.minimum(large, N_BUCKETS - 1)
    return jnp.where(dist < max_exact, dist, large)


def _dil_buckets(dilation):
    iq = jnp.arange(DIL_BLOCK)[:, None]
    ik = jnp.arange(2 * DIL_BLOCK)[None, :]
    return _t5_bucket(jnp.maximum(DIL_BLOCK + iq - ik, 0) * dilation)


def _dil_logits(qh, kb, bias_h, first, span):
    if first:
        s = _dot_nt(qh, kb) * DIL_SCALE + bias_h[:, DIL_BLOCK:]
        rel = lax.broadcasted_iota(jnp.int32, s.shape, 0) - lax.broadcasted_iota(jnp.int32, s.shape, 1)
    else:
        s = _dot_nt(qh, kb) * DIL_SCALE + bias_h
        rel = DIL_BLOCK + lax.broadcasted_iota(jnp.int32, s.shape, 0) - lax.broadcasted_iota(jnp.int32, s.shape, 1)
    return jnp.where((rel >= 0) & (rel <= span), s, -jnp.inf)


def _dil_blocks(s_dim, dilation):
    rows = s_dim // dilation
    for r in range(dilation):
        for n in range(rows // DIL_BLOCK):
            lo = r * rows + n * DIL_BLOCK
            keys = slice(lo, lo + DIL_BLOCK) if n == 0 else slice(lo - DIL_BLOCK, lo + DIL_BLOCK)
            start = r + n * DIL_BLOCK * dilation
            tokens = slice(start, start + DIL_BLOCK) if dilation == 1 else pl.ds(start, DIL_BLOCK, stride=dilation)
            yield n == 0, slice(lo, lo + DIL_BLOCK), keys, tokens


def _dil_views(s_dim):
    col = lambda which: pl.BlockSpec((s_dim, HEAD_PAD), lambda p: (0, which * DIL_PAIRS + p))
    nat = pl.BlockSpec((s_dim, HEAD_PAD), lambda p: (0, p))
    bias = pl.BlockSpec((2, DIL_BLOCK, 2 * DIL_BLOCK), lambda p: (p, 0, 0))
    return col, nat, bias


def _dil_attn_fwd(qkv, bias, dilation, span, name):
    s_dim = qkv.shape[0]
    d_dim = DIL_HEADS * DIL_HEAD_DIM
    col, nat, bias_spec = _dil_views(s_dim)

    def body(q_ref, k_ref, v_ref, b_ref, o_ref, l_ref):
        lane = lax.broadcasted_iota(jnp.int32, (DIL_BLOCK, HEAD_PAD), 1)
        klane = lax.broadcasted_iota(jnp.int32, (2 * DIL_BLOCK, HEAD_PAD), 1)
        for first, blk, keys, tokens in _dil_blocks(s_dim, dilation):
            qb, kb, vb = q_ref[blk, :], k_ref[keys, :], v_ref[keys, :]
            o_acc = jnp.zeros((DIL_BLOCK, HEAD_PAD), F32)
            lse_acc = jnp.zeros((DIL_BLOCK, HEAD_PAD), F32)
            for h in range(2):
                mine = (lane < DIL_HEAD_DIM) == (h == 0)
                kmine = (klane[:vb.shape[0]] < DIL_HEAD_DIM) == (h == 0)
                logits = _dil_logits(jnp.where(mine, qb, 0), kb, b_ref[h], first, span)
                mx = jnp.max(logits, axis=-1, keepdims=True)
                e = jnp.exp(logits - mx)
                tot = jnp.sum(e, axis=-1, keepdims=True)
                lse = mx + jnp.log(tot)
                p = e * (1.0 / tot)
                o_acc = o_acc + _dot_nn(p.astype(BF16), jnp.where(kmine, vb, 0))
                lse_acc = jnp.where(mine, lse, lse_acc)
            o_ref[tokens, :] = o_acc
            l_ref[tokens, :] = lse_acc

    out = jax.ShapeDtypeStruct((s_dim, d_dim), F32)
    return _pcall(
        body, out_shape=(out, out), grid=(DIL_PAIRS,),
        in_specs=[col(0), col(1), col(2), bias_spec], out_specs=(nat, nat), name=name,
        compiler_params=_params(("parallel",), 3 * [((s_dim, HEAD_PAD), BF16)] + 2 * [((s_dim, HEAD_PAD), F32)]
                                + [((2, DIL_BLOCK, 2 * DIL_BLOCK), F32)], extra=2**21),
    )(qkv, qkv, qkv, bias)


def _dil_mix(lses, outs, name):
    s_dim, d_dim = outs[0].shape
    tm = TOKEN_TILE
    ng = len(outs)

    def body(*refs):
        ls = [refs[g][...] for g in range(ng)]
        mx = ls[0]
        for g in range(1, ng):
            mx = jnp.maximum(mx, ls[g])
        es = [jnp.exp(l - mx) for l in ls]
        tot = es[0]
        for g in range(1, ng):
            tot = tot + es[g]
        o = None
        for g in range(ng):
            al = es[g] / tot
            refs[2 * ng + g][...] = al
            t = al * refs[ng + g][...]
            o = t if o is None else o + t
        refs[3 * ng][...] = o
        refs[3 * ng + 1][...] = o.astype(BF16)

    row = pl.BlockSpec((tm, d_dim), lambda i: (i, 0))
    f = jax.ShapeDtypeStruct((s_dim, d_dim), F32)
    res = _pcall(
        body, out_shape=tuple(ng * [f] + [f, jax.ShapeDtypeStruct((s_dim, d_dim), BF16)]), grid=(s_dim // tm,),
        in_specs=2 * ng * [row], out_specs=tuple((ng + 2) * [row]), name=name,
        compiler_params=_params(("parallel",), (3 * ng + 2) * [((tm, d_dim), F32)], extra=4 * tm * d_dim * 4),
    )(*lses, *outs)
    return res[:ng], res[ng], res[ng + 1]


def _dil_attn_bwd(qkv, bias, d_o, o_mix, alpha, lse, dilation, span, name):
    s_dim = qkv.shape[0]
    d_dim = DIL_HEADS * DIL_HEAD_DIM
    col, nat, bias_spec = _dil_views(s_dim)

    def body(q_ref, k_ref, v_ref, b_ref, do_ref, om_ref, al_ref, l_ref, dq_ref, dk_ref, dv_ref, db_ref, dk_acc, dv_acc):
        db_ref[...] = jnp.zeros_like(db_ref)
        dk_acc[...] = jnp.zeros_like(dk_acc)
        dv_acc[...] = jnp.zeros_like(dv_acc)
        lane = lax.broadcasted_iota(jnp.int32, (DIL_BLOCK, HEAD_PAD), 1)
        klane = lax.broadcasted_iota(jnp.int32, (2 * DIL_BLOCK, HEAD_PAD), 1)
        for first, blk, kv_rows, tokens in _dil_blocks(s_dim, dilation):
            qb, kb, vb = q_ref[blk, :], k_ref[kv_rows, :], v_ref[kv_rows, :]
            al = al_ref[tokens, :]
            dog = al * do_ref[tokens, :]
            row_term = dog * om_ref[tokens, :]
            lse_b = l_ref[tokens, :]
            dq_acc = jnp.zeros((DIL_BLOCK, HEAD_PAD), F32)
            dk_blk = jnp.zeros((kb.shape[0], HEAD_PAD), F32)
            dv_blk = jnp.zeros((kb.shape[0], HEAD_PAD), F32)
            for h in range(2):
                mine = (lane < DIL_HEAD_DIM) == (h == 0)
                kmine = (klane[:kb.shape[0]] < DIL_HEAD_DIM) == (h == 0)
                qh = jnp.where(mine, qb, 0)
                logits = _dil_logits(qh, kb, b_ref[h], first, span)
                lse_h = jnp.max(jnp.where(mine, lse_b, -jnp.inf), axis=-1, keepdims=True)
                p = jnp.exp(logits - lse_h)
                dogh = jnp.where(mine, dog, 0.0).astype(BF16)
                dp = _dot_nt(dogh, vb)
                ds = p * (dp - jnp.sum(jnp.where(mine, row_term, 0.0), axis=-1, keepdims=True))
                if first:
                    db_ref[h, :, DIL_BLOCK:] += ds
                else:
                    db_ref[h] += ds
                dsb = (ds * DIL_SCALE).astype(BF16)
                dq_acc = dq_acc + _dot_nn(dsb, jnp.where(kmine, kb, 0))
                dk_blk = dk_blk + _dot_tn(dsb, qh)
                dv_blk = dv_blk + _dot_tn(p.astype(BF16), dogh)
            dq_ref[blk, :] = dq_acc.astype(BF16)
            dk_acc[kv_rows, :] += dk_blk
            dv_acc[kv_rows, :] += dv_blk
        dk_ref[...] = dk_acc[...].astype(BF16)
        dv_ref[...] = dv_acc[...].astype(BF16)

    grad = jax.ShapeDtypeStruct((s_dim, d_dim), BF16)
    return _pcall(
        body, out_shape=(grad, grad, grad, jax.ShapeDtypeStruct(bias.shape, F32)), grid=(DIL_PAIRS,),
        in_specs=[col(0), col(1), col(2), bias_spec, nat, nat, nat, nat],
        out_specs=(nat, nat, nat, bias_spec), name=name,
        scratch_shapes=[pltpu.VMEM((s_dim, HEAD_PAD), F32), pltpu.VMEM((s_dim, HEAD_PAD), F32)],
        compiler_params=_params(("parallel",), 6 * [((s_dim, HEAD_PAD), BF16)] + 4 * [((s_dim, HEAD_PAD), F32)]
                                + 2 * [((2, DIL_BLOCK, 2 * DIL_BLOCK), F32)], extra=2 * s_dim * HEAD_PAD * 4 + 2**21),
    )(qkv, qkv, qkv, bias, d_o, o_mix, alpha, lse)


def _bias_reduce(dbias, buckets, name):
    n_heads = dbias.shape[0]

    def body(db_ref, bk_ref, o_ref):
        ds, bk = db_ref[0], bk_ref[0]
        lane = lax.broadcasted_iota(jnp.int32, (8, HEAD_PAD), 1)
        acc = jnp.zeros((8, HEAD_PAD), F32)
        for b in range(N_BUCKETS):
            acc = jnp.where(lane == b, jnp.sum(jnp.where(bk == b, ds, 0.0)), acc)
        o_ref[0] = acc

    blk = (1, DIL_BLOCK, 2 * DIL_BLOCK)
    return _pcall(
        body, out_shape=jax.ShapeDtypeStruct((n_heads, 8, HEAD_PAD), F32), grid=(n_heads,),
        in_specs=[pl.BlockSpec(blk, lambda h: (h, 0, 0)), pl.BlockSpec(blk, lambda h: (h // DIL_HEADS, 0, 0))],
        out_specs=pl.BlockSpec((1, 8, HEAD_PAD), lambda h: (h, 0, 0)), name=name,
        compiler_params=_params(("parallel",), [(blk, F32), (blk, jnp.int32)], extra=2**20),
    )(dbias, buckets)


def _loss_grad(y, target, name):
    s_dim, d_dim = y.shape
    tm = TOKEN_TILE

    def body(y_ref, t_ref, dy_ref, l_ref):
        @pl.when(pl.program_id(0) == 0)
        def _():
            l_ref[...] = jnp.zeros_like(l_ref)

        err = y_ref[...] - t_ref[...]
        dy_ref[...] = err / d_dim
        sq = (err * err).reshape(tm // 8, 8, d_dim)
        l_ref[...] += 0.5 * jnp.sum(sq, axis=0) / d_dim

    row = pl.BlockSpec((tm, d_dim), lambda i: (i, 0))
    acc = pl.BlockSpec((8, d_dim), lambda i: (0, 0))
    return _pcall(
        body, out_shape=(jax.ShapeDtypeStruct((s_dim, d_dim), F32), jax.ShapeDtypeStruct((8, d_dim), F32)),
        grid=(s_dim // tm,), in_specs=[row, row], out_specs=(row, acc), name=name,
        compiler_params=_params(("arbitrary",), 3 * [((tm, d_dim), F32)], extra=2 * tm * d_dim * 4),
    )(y, target)


def _mod_fwd(c_all, w_mod, b_loc, name):
    depth, d_dim, n = w_mod.shape
    nb = c_all.shape[0]

    def body(c_ref, w_ref, b_ref, o_ref, s_ref):
        cv = c_ref[...]
        sc = cv * jax.nn.sigmoid(cv)
        s_ref[...] = sc
        o_ref[0] = _dot_nn(sc.astype(BF16), w_ref[0].astype(BF16)) + b_ref[0]

    return _pcall(
        body, out_shape=(jax.ShapeDtypeStruct((depth, nb, n), F32), jax.ShapeDtypeStruct((nb, d_dim), F32)), grid=(depth,),
        in_specs=[pl.BlockSpec((nb, d_dim), lambda i: (0, 0)), pl.BlockSpec((1, d_dim, n), lambda i: (i, 0, 0)),
                  pl.BlockSpec((1, 1, n), lambda i: (i, 0, 0))],
        out_specs=(pl.BlockSpec((1, nb, n), lambda i: (i, 0, 0)), pl.BlockSpec((nb, d_dim), lambda i: (0, 0))), name=name,
        compiler_params=_params(("arbitrary",), [((1, d_dim, n), F32)], extra=d_dim * n * 2 + 2**20),
    )(c_all, w_mod, b_loc.reshape(depth, 1, n))


def _sum_parts(parts, name):
    _, rows, cols = parts.shape
    fits = [t for t in range(16, rows // 2 + 1, 16) if rows % t == 0 and NDEV * t * cols * parts.dtype.itemsize <= 3 * 2**20]
    tr = max(fits) if fits else rows

    def body(p_ref, o_ref):
        acc = p_ref[0].astype(F32)
        for k in range(1, NDEV):
            acc = acc + p_ref[k].astype(F32)
        o_ref[...] = acc

    return _pcall(
        body, out_shape=jax.ShapeDtypeStruct((rows, cols), F32), grid=(rows // tr,),
        in_specs=[pl.BlockSpec((NDEV, tr, cols), lambda i: (0, i, 0))], out_specs=pl.BlockSpec((tr, cols), lambda i: (i, 0)),
        name=name, compiler_params=_params(("parallel",), [((NDEV, tr, cols), parts.dtype), ((tr, cols), F32)], extra=2**20),
    )(parts)


def _adamw(w, g, m, v, name):
    shape = w.shape
    cols = shape[-1]
    rows = math.prod(shape[:-1])
    tr = rows
    for cand in (512, 256, 128, 64, 32, 16, 8):
        if rows % cand == 0 and rows > cand and cand * cols * 4 <= 2**21:
            tr = cand
            break

    def body(w_ref, g_ref, m_ref, v_ref, d_ref, mo_ref, vo_ref):
        gv = g_ref[...]
        mn = ADAM_B1 * m_ref[...] + (1.0 - ADAM_B1) * gv
        vn = ADAM_B2 * v_ref[...] + (1.0 - ADAM_B2) * (gv * gv)
        m_hat = mn / (1.0 - ADAM_B1 ** ADAM_STEP)
        v_hat = vn / (1.0 - ADAM_B2 ** ADAM_STEP)
        d_ref[...] = -ADAM_LR * (m_hat / (jnp.sqrt(v_hat) + ADAM_EPS) + ADAM_WD * w_ref[...])
        mo_ref[...] = mn
        vo_ref[...] = vn

    blk = pl.BlockSpec((tr, cols), lambda i: (i, 0))
    out = jax.ShapeDtypeStruct((rows, cols), F32)
    res = _pcall(
        body, out_shape=(out, out, out), grid=(rows // tr,), in_specs=4 * [blk], out_specs=(blk, blk, blk), name=name,
        compiler_params=_params(("parallel",), 7 * [((tr, cols), F32)], extra=4 * tr * cols * 4),
    )(*(a.reshape(rows, cols) for a in (w, g, m, v)))
    return tuple(r.reshape(shape) for r in res)


def _peers():
    x, y, c = lax.axis_index("x"), lax.axis_index("y"), lax.axis_index("c")
    flip = lambda v, f: 1 - v if f else v
    peers = []
    for f in range(1, NDEV):
        px, py, pc = flip(x, f & 4), flip(y, f & 2), flip(c, f & 1)
        peers.append(((px, py, pc), 4 * px + 2 * py + pc))
    return (x, y, c), 4 * x + 2 * y + c, peers


def _places():
    x, y, c = lax.axis_index("x"), lax.axis_index("y"), lax.axis_index("c")
    place = lambda px, py, pc: ((px, py, pc), 4 * px + 2 * py + pc)
    return place(x, y, c), place(x, y, 1 - c), [place(1 - x, y, c), place(x, 1 - y, c), place(1 - x, 1 - y, c)]


def _exchange(arrs, gather, name):
    n = len(arrs)
    hbm = pl.BlockSpec(memory_space=pltpu.HBM)
    if gather:
        out_shape = [jax.ShapeDtypeStruct((NDEV * a.shape[0], a.shape[1]), a.dtype) for a in arrs]
    else:
        out_shape = [jax.ShapeDtypeStruct((NDEV, a.shape[0] // NDEV, a.shape[1]), a.dtype) for a in arrs]

    def body(*refs):
        ins, outs = refs[:n], refs[n:2 * n]
        send_sems, recv_sems, local_sems = refs[2 * n:]
        me_pos, me, peers = _peers()
        local = []
        for k in range(n):
            rows = arrs[k].shape[0] if gather else arrs[k].shape[0] // NDEV
            if gather:
                src_of = lambda idx: ins[k]
                dst_of = lambda idx: outs[k].at[pl.ds(me * rows, rows)]
                mine = (ins[k], outs[k].at[pl.ds(me * rows, rows)])
            else:
                src_of = lambda idx: ins[k].at[pl.ds(idx * rows, rows)]
                dst_of = lambda idx: outs[k].at[me]
                mine = (ins[k].at[pl.ds(me * rows, rows)], outs[k].at[me])
            cp = pltpu.make_async_copy(mine[0], mine[1], local_sems.at[k])
            cp.start()
            local.append(cp)
            for pos, idx in peers:
                pltpu.make_async_remote_copy(src_ref=src_of(idx), dst_ref=dst_of(idx), send_sem=send_sems.at[k],
                                             recv_sem=recv_sems.at[k], device_id=pos, device_id_type=MESH).start()
        for k in range(n):
            rows = arrs[k].shape[0] if gather else arrs[k].shape[0] // NDEV
            sent = ins[k].at[pl.ds(0, (NDEV - 1) * rows)] if not gather else outs[k].at[pl.ds(0, (NDEV - 1) * rows)]
            got = outs[k].at[pl.ds(0, (NDEV - 1) * rows)] if gather else outs[k].at[pl.ds(0, NDEV - 1)]
            pltpu.make_async_remote_copy(src_ref=sent, dst_ref=sent, send_sem=send_sems.at[k], recv_sem=recv_sems.at[k],
                                         device_id=me_pos, device_id_type=MESH).wait_send()
            pltpu.make_async_remote_copy(src_ref=got, dst_ref=got, send_sem=send_sems.at[k], recv_sem=recv_sems.at[k],
                                         device_id=me_pos, device_id_type=MESH).wait_recv()
            local[k].wait()

    return pl.pallas_call(
        body, out_shape=out_shape, in_specs=n * [hbm], out_specs=n * [hbm], name=name,
        scratch_shapes=[pltpu.SemaphoreType.DMA((n,)), pltpu.SemaphoreType.DMA((n,)), pltpu.SemaphoreType.DMA((n,))],
        compiler_params=pltpu.CompilerParams(has_side_effects=True),
    )(*arrs)


_HBM = pl.BlockSpec(memory_space=pltpu.HBM)
_SEM = pl.BlockSpec(memory_space=pltpu.SEMAPHORE)
_DATAFLOW = pltpu.SideEffectType.DATAFLOW_SIDE_EFFECTING


def _split_start(srcs, groups, gather, name):
    n = len(srcs)
    if gather:
        lands = [lax.empty((NDEV * a.shape[0], a.shape[1]), a.dtype) for a in srcs]
    else:
        lands = [lax.empty((NDEV, a.shape[0] // NDEV, a.shape[1]), a.dtype) for a in srcs]
    n_sem = 3 * len(groups)

    def body(*refs):
        src_refs, land_refs = refs[:n], refs[n:2 * n]
        sems = refs[2 * n:2 * n + n_sem]
        token = refs[-1]
        (_, my), sibling, chips = _places()
        _, _, peers = _peers()
        targets = [sibling] + chips if gather else peers
        for g, members in enumerate(groups):
            for j, k in enumerate(members):
                _own_copy(src_refs[k], land_refs[k], sems[3 * g + 2].at[j], my, gather).start()
        for g, members in enumerate(groups):
            for j, k in enumerate(members):
                rows = srcs[k].shape[0] if gather else srcs[k].shape[0] // NDEV
                for pos, idx in targets:
                    src = src_refs[k] if gather else src_refs[k].at[pl.ds(idx * rows, rows)]
                    dst = land_refs[k].at[pl.ds(my * rows, rows)] if gather else land_refs[k].at[my]
                    pltpu.make_async_remote_copy(src_ref=src, dst_ref=dst, send_sem=sems[3 * g].at[j],
                                                 recv_sem=sems[3 * g + 1].at[j], device_id=pos, device_id_type=MESH).start()
        token[...] = jnp.zeros_like(token)

    out_shape = []
    for members in groups:
        out_shape += 3 * [pltpu.SemaphoreType.DMA((len(members),))]
    out_shape += [pltpu.HBM(a.shape, a.dtype) for a in srcs] + [pltpu.HBM(a.shape, a.dtype) for a in lands]
    out_shape.append(jax.ShapeDtypeStruct((8, 128), F32))
    res = pl.pallas_call(
        body, name=name, out_shape=tuple(out_shape), in_specs=2 * n * [_HBM],
        out_specs=tuple(n_sem * [_SEM] + 2 * n * [_HBM] + [pl.BlockSpec(memory_space=pltpu.VMEM)]),
        input_output_aliases={i: n_sem + i for i in range(2 * n)},
        compiler_params=pltpu.CompilerParams(has_side_effects=_DATAFLOW),
    )(*[pltpu.with_memory_space_constraint(a, pltpu.HBM) for a in list(srcs) + lands])
    sems = [tuple(res[3 * g:3 * g + 3]) for g in range(len(groups))]
    return sems, list(res[n_sem:n_sem + n]), list(res[n_sem + n:n_sem + 2 * n]), res[-1]


def _own_copy(src_ref, land_ref, sem, my, gather):
    if gather:
        rows = src_ref.shape[0]
        return pltpu.make_async_copy(src_ref, land_ref.at[pl.ds(my * rows, rows)], sem)
    rows = src_ref.shape[0] // NDEV
    return pltpu.make_async_copy(src_ref.at[pl.ds(my * rows, rows)], land_ref.at[my], sem)


def _wait_all(land_ref, blocks_per_dev, copies, send_sem, recv_sem, me_pos):
    part = land_ref.at[pl.ds(0, copies * blocks_per_dev)]
    pltpu.make_async_remote_copy(src_ref=part, dst_ref=part, send_sem=send_sem, recv_sem=recv_sem,
                                 device_id=me_pos, device_id_type=MESH).wait()


def _gather_forward(sems, srcs, lands, after, name):
    n = len(srcs)

    def body(*refs):
        land_refs = refs[n:2 * n]
        send_a, recv_a = refs[2 * n], refs[2 * n + 1]
        send_b, recv_b = refs[2 * n + 3], refs[2 * n + 4]
        token = refs[-1]
        (me_pos, _), sibling, chips = _places()
        for j in range(n):
            _wait_all(land_refs[j], lands[j].shape[0] // NDEV, 1 + OTHER_CHIPS, send_a.at[j], recv_a.at[j], me_pos)
        for j in range(n):
            rows = lands[j].shape[0] // NDEV
            for _, idx in chips:
                block = land_refs[j].at[pl.ds(idx * rows, rows)]
                pltpu.make_async_remote_copy(src_ref=block, dst_ref=block, send_sem=send_b.at[j], recv_sem=recv_b.at[j],
                                             device_id=sibling[0], device_id_type=MESH).start()
        token[...] = jnp.zeros_like(token)

    res = pl.pallas_call(
        body, name=name,
        out_shape=(pltpu.SemaphoreType.DMA((n,)), pltpu.SemaphoreType.DMA((n,)))
        + tuple(pltpu.HBM(a.shape, a.dtype) for a in list(srcs) + list(lands)) + (jax.ShapeDtypeStruct((8, 128), F32),),
        in_specs=2 * n * [_HBM] + [_SEM, _SEM, pl.BlockSpec(memory_space=pl.ANY)],
        out_specs=tuple([_SEM, _SEM] + 2 * n * [_HBM] + [pl.BlockSpec(memory_space=pltpu.VMEM)]),
        input_output_aliases={i: 2 + i for i in range(2 * n)},
        compiler_params=pltpu.CompilerParams(has_side_effects=_DATAFLOW),
    )(*srcs, *lands, sems[0], sems[1], after)
    return (res[0], res[1]), list(res[2:2 + n]), list(res[2 + n:2 + 2 * n]), res[-1]


def _split_wait(sems, srcs, lands, after, copies, gather, name):
    n = len(srcs)

    def body(*refs):
        src_refs, land_refs = refs[:n], refs[n:2 * n]
        send_sem, recv_sem, local_sem = refs[2 * n], refs[2 * n + 1], refs[2 * n + 2]
        (me_pos, my), _, _ = _places()
        for j in range(n):
            _wait_all(land_refs[j], lands[j].shape[0] // NDEV, copies, send_sem.at[j], recv_sem.at[j], me_pos)
            _own_copy(src_refs[j], land_refs[j], local_sem.at[j], my, gather).wait()

    res = pl.pallas_call(
        body, name=name, out_shape=tuple(pltpu.HBM(a.shape, a.dtype) for a in list(srcs) + list(lands)),
        in_specs=2 * n * [_HBM] + [_SEM, _SEM, _SEM, pl.BlockSpec(memory_space=pl.ANY)], out_specs=tuple(2 * n * [_HBM]),
        input_output_aliases={i: i for i in range(2 * n)},
        compiler_params=pltpu.CompilerParams(has_side_effects=_DATAFLOW),
    )(*srcs, *lands, sems[0], sems[1], sems[2], after)
    return list(res[n:])


def _chained(gate, mid, after):
    return gate if mid is None else gate + mid(after)[:1, :1]


def _ffn_fwd(x, norms, mod, w, mid=None):
    (pre_g, post_g), (shift, scale, gate), (wg_t, wu_t, wd) = norms, mod, w
    hn, g, u, a = _ffn_up(x, pre_g, scale, shift, wg_t, wu_t, "ffn_up")
    x_out, f = _mm_post(a, wd, x, post_g, _chained(gate, mid, a), FFN_RES, "ffn_down")
    return x_out, (x, hn, g, u, a, f)


def _ffn_bwd(dx_out, saved, norms, mod, w, send=None):
    (pre_g, post_g), (_, scale, gate), (wg_t, wu_t, wd) = norms, mod, w
    x, hn, g, u, a, f = saved
    d_model = x.shape[1]
    sent = (lambda j, dw: None) if send is None else send
    df, dgate, dpost = _post_bwd(dx_out, f, post_g, gate, FFN_RES, "ffn_post_bwd")
    dwd = _mm([(a, df)], "tn", BF16, 256, d_model, "ffn_dw")
    dg, du = _ffn_dgu(df, wd, g, u, "ffn_dgu", after=sent(2, dwd))
    dwg_t = _mm([(dg, hn)], "tn", BF16, 256, d_model, "ffn_dw")
    dwu_t = _mm([(du, hn)], "tn", BF16, 256, d_model, "ffn_dw", after=sent(0, dwg_t))
    dhn = _mm([(dg, wg_t), (du, wu_t)], "nn", F32, TOKEN_TILE, d_model, "ffn_dhn", after=sent(1, dwu_t))
    dx, dshift, dscale, dpre = _prenorm_bwd(dx_out, [dhn], x, pre_g, scale, "prenorm_bwd")
    return dx, (dpre, dpost), (dshift, dscale, dgate), (dwg_t, dwu_t, dwd)


def _mla_fwd(x, norms, mod, w, rope, mid=None):
    (pre_g, post_g), (shift, scale, gate) = norms, mod
    w_in, q_norm, wq_t, kv_norm, wkv_t, wo = w
    hn, lat = _prenorm_mm(x, pre_g, scale, shift, w_in, "nn", F32, LAT_PAD, "mla_in")
    gate = _chained(gate, mid, lat)
    q, k, v, qn, kvn = _mla_qkv(lat, q_norm, kv_norm, wq_t, wkv_t, rope, "mla_qkv")
    o = _mla_attn_fwd(q, k, v, "mla_attn_fwd")
    x_out, f = _mm_post(o, wo, x, post_g, gate, 1.0, "mla_out")
    return x_out, (x, hn, lat, q, k, v, qn, kvn, o, f)


def _mla_bwd(dx_out, saved, norms, mod, w, rope):
    (pre_g, post_g), (_, scale, gate) = norms, mod
    w_in, q_norm, wq_t, kv_norm, wkv_t, wo = w
    x, hn, lat, q, k, v, qn, kvn, o, f = saved
    d_model = x.shape[1]
    df, dgate, dpost = _post_bwd(dx_out, f, post_g, gate, 1.0, "mix_post_bwd")
    d_o = _mm([(df, wo)], "nt", F32, TOKEN_TILE, wo.shape[0], "mla_do")
    dwo = _mm([(o, df)], "tn", BF16, TOKEN_TILE, d_model, "mla_dwo")
    dq, dk, dv = _mla_attn_bwd(q, k, v, d_o, "mla_attn_bwd")
    dqp, dkv, dlat, dq_norm, dkv_norm = _mla_qkv_bwd(dq, dk, dv, lat, q_norm, kv_norm, wq_t, wkv_t, rope, "mla_qkv_bwd")
    dwq_t = _mm([(dqp, qn)], "tn", BF16, TOKEN_TILE, Q_LORA, "mla_dwq")
    dwkv_t = _mm([(dkv, kvn)], "tn", BF16, TOKEN_TILE, KV_LORA, "mla_dwkv")
    dw_in = _mm([(hn, dlat)], "tn", BF16, TOKEN_TILE, LAT_PAD, "mla_dwin")
    dhn = _mm([(dlat, w_in)], "nt", F32, TOKEN_TILE, d_model, "mla_dhn")
    dx, dshift, dscale, dpre = _prenorm_bwd(dx_out, [dhn], x, pre_g, scale, "prenorm_bwd")
    return dx, (dpre, dpost), (dshift, dscale, dgate), (dw_in, dq_norm, dwq_t, dkv_norm, dwkv_t, dwo)


def _dil_fwd(x, norms, mod, w, bias, mid=None):
    (pre_g, post_g), (shift, scale, gate), (w_in_t, wo) = norms, mod, w
    width = 3 * DIL_HEADS * DIL_HEAD_DIM
    hns, qkvs, outs, lses = [], [], [], []
    for g, (window, dilation) in enumerate(DIL_GROUPS):
        hn, qkv = _prenorm_mm(x, pre_g, scale, shift, w_in_t[g * width:(g + 1) * width], "nt", BF16, width,
                              "dil_in", perm=dilation)
        if g == 0:
            gate = _chained(gate, mid, qkv)
        o, lse = _dil_attn_fwd(qkv, bias[g], dilation, window // dilation, "dil_attn_fwd")
        hns.append(hn), qkvs.append(qkv), outs.append(o), lses.append(lse)
    alphas, o_mix, o_mix_b = _dil_mix(lses, outs, "dil_mix")
    x_out, f = _mm_post(o_mix_b, wo, x, post_g, gate, 1.0, "dil_out")
    return x_out, (x, hns, qkvs, lses, alphas, o_mix, o_mix_b, f)


def _dil_bwd(dx_out, saved, norms, mod, w, bias):
    (pre_g, post_g), (_, scale, gate), (w_in_t, wo) = norms, mod, w
    x, hns, qkvs, lses, alphas, o_mix, o_mix_b, f = saved
    d_model = x.shape[1]
    inner = DIL_HEADS * DIL_HEAD_DIM
    df, dgate, dpost = _post_bwd(dx_out, f, post_g, gate, 1.0, "mix_post_bwd")
    d_o = _mm([(df, wo)], "nt", F32, TOKEN_TILE, inner, "dil_do")
    dwo = _mm([(o_mix_b, df)], "tn", BF16, TOKEN_TILE, d_model, "dil_dwo")
    dhns, dws, dbs = [], [], []
    for g, (window, dilation) in enumerate(DIL_GROUPS):
        grads = _dil_attn_bwd(qkvs[g], bias[g], d_o, o_mix, alphas[g], lses[g], dilation, window // dilation, "dil_attn_bwd")
        dbs.append(grads[3])
        w_parts = [w_in_t[(3 * g + j) * inner:(3 * g + j + 1) * inner] for j in range(3)]
        dhns.append(_mm(list(zip(grads[:3], w_parts)), "nn", F32, TOKEN_TILE, d_model, "dil_dhn", out_perm=dilation))
        dws += [_mm([(grads[j], hns[g])], "tn", BF16, TOKEN_TILE, d_model, "dil_dwin") for j in range(3)]
    dx, dshift, dscale, dpre = _prenorm_bwd(dx_out, dhns, x, pre_g, scale, "prenorm_bwd3")
    return dx, (dpre, dpost), (dshift, dscale, dgate), (jnp.concatenate(dws, axis=0), dwo), jnp.concatenate(dbs, axis=0)


def _pad_rows(a, rows):
    return jnp.pad(a, ((0, rows - a.shape[0]), (0, 0)))


def _lanes(a):
    flat = a.reshape(-1).astype(F32)
    rows = -(-flat.shape[0] // 1024) * 8
    return jnp.pad(flat, (0, rows * 128 - flat.shape[0])).reshape(rows, 128)


def kernel(x, c, norm_pre, norm_post, w_mod, b_mod, ffn_w_gate, ffn_w_up, ffn_w_down, mla_w_in, mla_q_norm, mla_w_q_up, mla_kv_norm, mla_w_kv_up, mla_w_o, dil_w_in, dil_w_o, rel_bias, loss_target, m_norm_pre, m_norm_post, m_w_mod, m_b_mod, m_ffn_w_gate, m_ffn_w_up, m_ffn_w_down, m_mla_w_in, m_mla_q_norm, m_mla_w_q_up, m_mla_kv_norm, m_mla_w_kv_up, m_mla_w_o, m_dil_w_in, m_dil_w_o, m_rel_bias, v_norm_pre, v_norm_post, v_w_mod, v_b_mod, v_ffn_w_gate, v_ffn_w_up, v_ffn_w_down, v_mla_w_in, v_mla_q_norm, v_mla_w_q_up, v_mla_kv_norm, v_mla_w_kv_up, v_mla_w_o, v_dil_w_in, v_dil_w_o, v_rel_bias):
    me = 4 * lax.axis_index("x") + 2 * lax.axis_index("y") + lax.axis_index("c")
    depth, n_sub, d_loc = norm_pre.shape
    d_model = x.shape[2]
    mod_loc_cols = w_mod.shape[2]
    x0, target = x[0], loss_target[0]

    bf_t = lambda a: a.astype(BF16).T
    ffn_ids = [(i, h) for i in range(depth) for h in range(2)]
    shards = []
    for i, h in ffn_ids:
        shards += [bf_t(ffn_w_gate[i, h]), bf_t(ffn_w_up[i, h]), ffn_w_down[i, h].astype(BF16)]
    shards += [mla_w_in[0].astype(BF16), bf_t(mla_w_q_up[0]), bf_t(mla_w_kv_up[0]), mla_w_o[0].astype(BF16),
               bf_t(dil_w_in[0]), dil_w_o[0].astype(BF16)]
    n_ffn = 3 * len(ffn_ids)
    members = {(0, 0): [0, 1, 2], (0, 1): [n_ffn, n_ffn + 1, n_ffn + 2, n_ffn + 3], (0, 2): [3, 4, 5],
               (1, 0): [6, 7, 8], (1, 1): [n_ffn + 4, n_ffn + 5], (1, 2): [9, 10, 11]}
    order = [(i, s) for i in range(depth) for s in range(n_sub)]

    small = jnp.concatenate([c.reshape(8, 128), _pad_rows(norm_pre.reshape(depth * n_sub, d_loc), 8),
                             _pad_rows(norm_post.reshape(depth * n_sub, d_loc), 8)], axis=0)
    small_all = _exchange([small], True, "gather_small")[0].reshape(NDEV, 24, 128)
    c_all = small_all[:, 0:8].reshape(NDEV, d_model)
    gains = lambda lo: jnp.transpose(small_all[:, lo:lo + depth * n_sub], (1, 0, 2)).reshape(depth, n_sub, 1, d_model)
    pre_full, post_full = gains(8), gains(16)

    b_loc = lax.dynamic_slice(b_mod, (0, me * mod_loc_cols), (depth, mod_loc_cols))
    mod_cols, silu_c = _mod_fwd(c_all, w_mod, b_loc, "mod_fwd")
    mod_all = _exchange([mod_cols.reshape(depth * NDEV, mod_loc_cols)], True, "gather_mod")[0]
    mod_all = mod_all.reshape(NDEV, depth, NDEV, mod_loc_cols)
    mod_mine = lax.dynamic_index_in_dim(mod_all, me, axis=2, keepdims=False)
    mod = jnp.transpose(mod_mine, (1, 0, 2)).reshape(depth, n_sub, 3, 1, d_model)

    shards[0], _ = lax.optimization_barrier((shards[0], mod_all))
    g_sems, g_srcs, g_lands, _ = _split_start(shards, [members[k] for k in order], True, "gather_weights_start")

    forwarded = {}

    def forward(key, after):
        idx = members[key]
        forwarded[key] = _gather_forward(g_sems[order.index(key)], [g_srcs[k] for k in idx], [g_lands[k] for k in idx], after,
                                         "gather_forward_%d%d" % key)
        return forwarded[key][3]

    def weights_of(key, after):
        (send_b, recv_b), srcs, lands, _ = forwarded[key]
        local = g_sems[order.index(key)][2]
        return _split_wait((send_b, recv_b, local), srcs, lands, after, OTHER_CHIPS, True, "gather_wait_%d%d" % key)

    lat_real = Q_LORA + KV_LORA
    qk = QK_NOPE + QK_ROPE

    def mla_weights(after):
        w_in, wq_t, wkv_t, wo = weights_of((0, 1), after)
        w_in_pad = jnp.concatenate([w_in[:, :lat_real], jnp.zeros((d_model, QK_NOPE), BF16), w_in[:, lat_real:],
                                    jnp.zeros((d_model, HEAD_PAD - QK_NOPE - QK_ROPE), BF16)], axis=1)
        wq_pad = jnp.pad(wq_t.reshape(MLA_HEADS, qk, Q_LORA), ((0, 0), (0, HEAD_PAD - qk), (0, 0)))
        wo_pad = jnp.pad(wo.reshape(MLA_HEADS, V_HEAD, d_model), ((0, 0), (HEAD_PAD - V_HEAD, 0), (0, 0)))
        return (w_in_pad, mla_q_norm, wq_pad.reshape(MLA_HEADS * HEAD_PAD, Q_LORA), mla_kv_norm, wkv_t,
                wo_pad.reshape(MLA_HEADS * HEAD_PAD, d_model))

    rope = _rope_tables()
    buckets = jnp.stack([_dil_buckets(dil) for _, dil in DIL_GROUPS])
    onehot = (buckets[..., None] == jnp.arange(N_BUCKETS)).astype(F32)
    bias = jnp.einsum("gqkb,bgh->ghqk", onehot, rel_bias.reshape(N_BUCKETS, len(DIL_GROUPS), DIL_HEADS),
                      precision=lax.Precision.HIGHEST)

    norms = lambda i, s: (pre_full[i, s], post_full[i, s])
    mods = lambda i, s: (mod[i, s, 0], mod[i, s, 1], mod[i, s, 2])
    saved, weights = {}, {}
    h = x0
    forward(order[0], h)
    for n, (i, s) in enumerate(order):
        weights[i, s] = mla_weights(h) if (s == 1 and i % 2 == 0) else tuple(weights_of((i, s), h))
        mid = None if n + 1 == len(order) else (lambda after, nxt=order[n + 1]: forward(nxt, after))
        if s != 1:
            h, saved[i, s] = _ffn_fwd(h, norms(i, s), mods(i, s), weights[i, s], mid)
        elif i % 2 == 0:
            h, saved[i, s] = _mla_fwd(h, norms(i, s), mods(i, s), weights[i, s], rope, mid)
        else:
            h, saved[i, s] = _dil_fwd(h, norms(i, s), mods(i, s), weights[i, s], bias, mid)
    dh, loss_parts = _loss_grad(h, target, "loss")

    dnorm, dmod, sent = {}, {}, {}
    token = jnp.zeros((8, 128), F32)
    last = order[0]

    def send_last(j, dw):
        sent[last, j] = _split_start([dw], [[0]], False, "scatter_start_%d%d_%d" % (*last, j))
        return sent[last, j][3]

    for i, s in reversed(order):
        md = mods(i, s)
        md = (md[0], md[1], md[2] + token[:1, :1])
        if (i, s) == last:
            dh, dnorm[i, s], dmod[i, s], _ = _ffn_bwd(dh, saved[i, s], norms(i, s), md, weights[i, s], send_last)
            continue
        if s != 1:
            dh, dnorm[i, s], dmod[i, s], dws = _ffn_bwd(dh, saved[i, s], norms(i, s), md, weights[i, s])
        elif i % 2 == 0:
            dh, dnorm[i, s], dmod[i, s], dmla = _mla_bwd(dh, saved[i, s], norms(i, s), md, weights[i, s], rope)
            dw_in_pad, dq_norm, dwq_pad, dkv_norm, dwkv_t, dwo_pad = dmla
            dw_in = jnp.concatenate([dw_in_pad[:, :lat_real], dw_in_pad[:, lat_real + QK_NOPE:lat_real + qk]], axis=1)
            dwq_t = dwq_pad.reshape(MLA_HEADS, HEAD_PAD, Q_LORA)[:, :qk].reshape(MLA_HEADS * qk, Q_LORA)
            dwo = dwo_pad.reshape(MLA_HEADS, HEAD_PAD, d_model)[:, HEAD_PAD - V_HEAD:].reshape(MLA_HEADS * V_HEAD, d_model)
            dws = (dw_in, dwq_t, dwkv_t, dwo)
        else:
            dh, dnorm[i, s], dmod[i, s], dws, dbias = _dil_bwd(dh, saved[i, s], norms(i, s), md, weights[i, s], bias)
        sent[i, s] = _split_start(list(dws), [list(range(len(dws)))], False, "scatter_start_%d%d" % (i, s))
        token = sent[i, s][3]
    grad_x = dh[None]

    mine = {}
    for key in order[1:]:
        sems, srcs, lands, _ = sent[key]
        parts = _split_wait(sems[0], srcs, lands, dh, NDEV - 1, False, "scatter_wait_%d%d" % key)
        for k, p in zip(members[key], parts):
            mine[k] = _sum_parts(p, "sum_parts")
    for j in (2, 0, 1):
        sems, srcs, lands, _ = sent[last, j]
        parts = _split_wait(sems[0], srcs, lands, dh, NDEV - 1, False, "scatter_wait_%d%d_%d" % (*last, j))
        mine[members[last][j]] = _sum_parts(parts[0], "sum_parts")
    g_gate = jnp.stack([mine[3 * n].T for n in range(len(ffn_ids))]).reshape(ffn_w_gate.shape)
    g_up = jnp.stack([mine[3 * n + 1].T for n in range(len(ffn_ids))]).reshape(ffn_w_up.shape)
    g_down = jnp.stack([mine[3 * n + 2] for n in range(len(ffn_ids))]).reshape(ffn_w_down.shape)
    g_mla_in, g_q_up, g_kv_up, g_mla_o, g_dil_in, g_dil_o = (mine[k] for k in range(n_ffn, n_ffn + 6))
    g_mla_in, g_q_up, g_kv_up, g_mla_o = g_mla_in[None], g_q_up.T[None], g_kv_up.T[None], g_mla_o[None]
    g_dil_in, g_dil_o = g_dil_in.T[None], g_dil_o[None]

    dmod_mine = jnp.concatenate([jnp.concatenate(dmod[i, s], axis=0) for i in range(depth) for s in range(n_sub)], axis=0)
    dpre_mine = jnp.concatenate([dnorm[i, s][0] for i in range(depth) for s in range(n_sub)], axis=0)
    dpost_mine = jnp.concatenate([dnorm[i, s][1] for i in range(depth) for s in range(n_sub)], axis=0)
    dbias_tab = _bias_reduce(dbias, buckets, "bias_reduce")[:, 0, :N_BUCKETS].T
    pieces = [dmod_mine, dpre_mine, dpost_mine, dq_norm, dkv_norm, dbias_tab, jnp.sum(loss_parts).reshape(1, 1)]
    packed = [_lanes(p) for p in pieces]
    offs = [0]
    for p in packed:
        offs.append(offs[-1] + p.shape[0])
    everyone = _exchange([jnp.concatenate(packed, axis=0)], True, "gather_small_grads")[0].reshape(NDEV, offs[-1], 128)
    total = _sum_parts(everyone, "sum_small")
    take = lambda n, shape: total[offs[n]:offs[n + 1]].reshape(-1)[:math.prod(shape)].reshape(shape)
    g_b_mod = take(0, b_mod.shape)
    col0 = me * d_loc
    g_norm_pre = lax.dynamic_slice(take(1, (depth, n_sub, d_model)), (0, 0, col0), norm_pre.shape)
    g_norm_post = lax.dynamic_slice(take(2, (depth, n_sub, d_model)), (0, 0, col0), norm_post.shape)
    g_q_norm, g_kv_norm = take(3, mla_q_norm.shape), take(4, mla_kv_norm.shape)
    g_rel_bias = take(5, rel_bias.shape)
    loss = take(6, ())

    dmod_all = everyone[:, offs[0]:offs[1]].reshape(NDEV, depth, NDEV * mod_loc_cols)
    dmod_cols = lax.dynamic_slice(dmod_all, (0, 0, me * mod_loc_cols), (NDEV, depth, mod_loc_cols))
    silu_t = jnp.pad(silu_c.T, ((0, 0), (0, HEAD_PAD - NDEV)))
    g_w_mod = jnp.stack([_mm([(silu_t, jnp.pad(dmod_cols[:, i], ((0, HEAD_PAD - NDEV), (0, 0))))], "nn", F32, TOKEN_TILE,
                             mod_loc_cols, "mod_bwd") for i in range(depth)])

    ws = (norm_pre, norm_post, w_mod, b_mod, ffn_w_gate, ffn_w_up, ffn_w_down, mla_w_in, mla_q_norm, mla_w_q_up, mla_kv_norm,
          mla_w_kv_up, mla_w_o, dil_w_in, dil_w_o, rel_bias)
    gs = (g_norm_pre, g_norm_post, g_w_mod, g_b_mod, g_gate, g_up, g_down, g_mla_in, g_q_norm, g_q_up, g_kv_norm, g_kv_up,
          g_mla_o, g_dil_in, g_dil_o, g_rel_bias)
    ms = (m_norm_pre, m_norm_post, m_w_mod, m_b_mod, m_ffn_w_gate, m_ffn_w_up, m_ffn_w_down, m_mla_w_in, m_mla_q_norm,
          m_mla_w_q_up, m_mla_kv_norm, m_mla_w_kv_up, m_mla_w_o, m_dil_w_in, m_dil_w_o, m_rel_bias)
    vs = (v_norm_pre, v_norm_post, v_w_mod, v_b_mod, v_ffn_w_gate, v_ffn_w_up, v_ffn_w_down, v_mla_w_in, v_mla_q_norm,
          v_mla_w_q_up, v_mla_kv_norm, v_mla_w_kv_up, v_mla_w_o, v_dil_w_in, v_dil_w_o, v_rel_bias)
    stepped = [_adamw(w, g, m, v, "adamw") for w, g, m, v in zip(ws, gs, ms, vs)]
    deltas, new_m, new_v = zip(*stepped)
    return (loss, grad_x, *gs, *deltas, *new_m, *new_v)
```

```python
import math

import jax
import jax.numpy as jnp
from jax import lax
from jax.experimental import pallas as pl
from jax.experimental.pallas import tpu as pltpu

F32 = jnp.float32
BF16 = jnp.bfloat16
MESH = pl.DeviceIdType.MESH

NDEV = 8
OTHER_CHIPS = 3
D_MODEL = 1024
SEQ = 2048
D_FF = 2816
EPS = 1e-6
FFN_RES = 0.5

MLA_HEADS = 16
Q_LORA = 384
KV_LORA = 256
QK_NOPE = 64
QK_ROPE = 32
V_HEAD = 64
ROPE_THETA = 10000.0
HEAD_PAD = 128
LAT_PAD = Q_LORA + KV_LORA + HEAD_PAD
MLA_SCALE = (QK_NOPE + QK_ROPE) ** -0.5

DIL_GROUPS = ((128, 1), (512, 4), (2048, 16))
DIL_HEADS = 16
DIL_HEAD_DIM = 64
DIL_BLOCK = 128
DIL_PAIRS = DIL_HEADS // 2
DIL_SCALE = DIL_HEAD_DIM ** -0.5
N_BUCKETS = 32
MAX_DISTANCE = 2048

ADAM_LR = 0.001
ADAM_B1 = 0.9
ADAM_B2 = 0.999
ADAM_EPS = 1e-08
ADAM_WD = 0.01
ADAM_STEP = 10

V7X_VMEM_BYTES = 64 * 2**20
VMEM_RESERVE = 10 * 2**20
TOKEN_TILE = 512


def _nbytes(shape, dtype):
    return math.prod(shape) * jnp.dtype(dtype).itemsize


def _params(semantics, blocks, extra=0):
    need = 2 * sum(_nbytes(s, d) for s, d in blocks) + extra + VMEM_RESERVE
    return pltpu.CompilerParams(dimension_semantics=semantics,
                                vmem_limit_bytes=int(min(need, V7X_VMEM_BYTES - VMEM_RESERVE)))


def _pcall(body, out_shape, **kw):
    call = pl.pallas_call(body, out_shape=jax.tree.map(lambda s: pltpu.HBM(s.shape, s.dtype), out_shape), **kw)
    return lambda *args: call(*[pltpu.with_memory_space_constraint(a, pltpu.HBM) for a in args])


def _dot_nn(a, b):
    return lax.dot_general(a, b, (((1,), (0,)), ((), ())), preferred_element_type=F32)


def _dot_nt(a, b):
    return lax.dot_general(a, b, (((1,), (1,)), ((), ())), preferred_element_type=F32)


def _dot_tn(a, b):
    return lax.dot_general(a, b, (((0,), (0,)), ((), ())), preferred_element_type=F32)


_DOTS = {"nn": _dot_nn, "nt": _dot_nt, "tn": _dot_tn}


def _rstd(v):
    return lax.rsqrt(jnp.mean(v * v, axis=-1, keepdims=True) + EPS)


def _rms_bwd(v, r, t):
    return r * t - v * (r * r * r) * jnp.mean(t * v, axis=-1, keepdims=True)


_TOKEN_SPEC = pl.BlockSpec((8, 128), lambda *_: (0, 0))


def _mm(pairs, mode, out_dtype, tm, tn, name, out_perm=1, after=None):
    a0, b0 = pairs[0]
    m_dim = a0.shape[1] if mode == "tn" else a0.shape[0]
    n_dim = b0.shape[0] if mode == "nt" else b0.shape[1]
    tm, tn = min(tm, m_dim // out_perm), min(tn, n_dim)
    assert m_dim % tm == 0 and n_dim % tn == 0, (name, m_dim, n_dim, tm, tn)
    dot = _DOTS[mode]
    npairs = len(pairs)

    def body(*refs):
        acc = None
        for p in range(npairs):
            d = dot(refs[2 * p][...].astype(BF16), refs[2 * p + 1][...].astype(BF16))
            acc = d if acc is None else acc + d
        refs[-1][...] = acc.astype(out_dtype)

    in_specs, blocks, flat = [], [], []
    for a, b in pairs:
        if mode == "nn":
            k = a.shape[1]
            sa, sb = ((tm, k), lambda i, j: (i, 0)), ((k, tn), lambda i, j: (0, j))
        elif mode == "nt":
            k = a.shape[1]
            sa, sb = ((tm, k), lambda i, j: (i, 0)), ((tn, k), lambda i, j: (j, 0))
        else:
            k = a.shape[0]
            sa, sb = ((k, tm), lambda i, j: (0, i)), ((k, tn), lambda i, j: (0, j))
        in_specs += [pl.BlockSpec(*sa), pl.BlockSpec(*sb)]
        blocks += [(sa[0], a.dtype), (sb[0], b.dtype)]
        flat += [a, b]
    if after is not None:
        in_specs.append(_TOKEN_SPEC)
        flat.append(after)
    if out_perm == 1:
        out_shape = (m_dim, n_dim)
        out_spec = pl.BlockSpec((tm, tn), lambda i, j: (i, j))
    else:
        rows = m_dim // out_perm
        assert tn == n_dim and rows % tm == 0, (name, rows, tm)
        nb = rows // tm
        out_shape = (rows, out_perm * n_dim)
        out_spec = pl.BlockSpec((tm, n_dim), lambda i, j: (i % nb, i // nb))
    blocks.append(((tm, tn), out_dtype))
    res = _pcall(
        body, out_shape=jax.ShapeDtypeStruct(out_shape, out_dtype), grid=(m_dim // tm, n_dim // tn),
        in_specs=in_specs, out_specs=out_spec, name=name,
        compiler_params=_params(("parallel", "parallel"), blocks, extra=2 * tm * tn * 4),
    )(*flat)
    return res.reshape(m_dim, n_dim)


def _prenorm_mm(x, pre_g, scale, shift, w, w_mode, out_dtype, tn, name, perm=1):
    s_dim, d_dim = x.shape
    n_dim = w.shape[0] if w_mode == "nt" else w.shape[1]
    rows = s_dim // perm
    tm = min(TOKEN_TILE, rows)
    nb = rows // tm
    tn = min(tn, n_dim)
    assert n_dim % tn == 0
    dot = _DOTS[w_mode]

    def body(x_ref, g_ref, sc_ref, sh_ref, w_ref, hn_ref, o_ref):
        @pl.when(pl.program_id(1) == 0)
        def _():
            xf = x_ref[...]
            hn = (xf * _rstd(xf) * g_ref[...]) * (1.0 + sc_ref[...]) + sh_ref[...]
            hn_ref[...] = hn.astype(BF16)

        o_ref[...] = dot(hn_ref[...], w_ref[...]).astype(out_dtype)

    vec = pl.BlockSpec((1, d_dim), lambda i, j: (0, 0))
    w_block = (tn, d_dim) if w_mode == "nt" else (d_dim, tn)
    w_spec = pl.BlockSpec(w_block, (lambda i, j: (j, 0)) if w_mode == "nt" else (lambda i, j: (0, j)))
    hn, out = _pcall(
        body,
        out_shape=(jax.ShapeDtypeStruct((s_dim, d_dim), BF16), jax.ShapeDtypeStruct((s_dim, n_dim), out_dtype)),
        grid=(s_dim // tm, n_dim // tn),
        in_specs=[pl.BlockSpec((tm, d_dim), lambda i, j: (i % nb, i // nb)), vec, vec, vec, w_spec],
        out_specs=(pl.BlockSpec((tm, d_dim), lambda i, j: (i, 0)), pl.BlockSpec((tm, tn), lambda i, j: (i, j))),
        name=name,
        compiler_params=_params(("parallel", "arbitrary"),
                                [((tm, d_dim), F32), (w_block, BF16), ((tm, d_dim), BF16), ((tm, tn), out_dtype)],
                                extra=3 * tm * d_dim * 4 + tm * tn * 4),
    )(x.reshape(rows, perm * d_dim), pre_g, scale, shift, w)
    return hn, out


def _ffn_up(x, pre_g, scale, shift, wg_t, wu_t, name):
    s_dim, d_dim = x.shape
    f_dim = wg_t.shape[0]
    tm, tn = TOKEN_TILE, f_dim // 2

    def body(x_ref, g_ref, sc_ref, sh_ref, wg_ref, wu_ref, hn_ref, go_ref, uo_ref, a_ref):
        @pl.when(pl.program_id(1) == 0)
        def _():
            xf = x_ref[...]
            hn = (xf * _rstd(xf) * g_ref[...]) * (1.0 + sc_ref[...]) + sh_ref[...]
            hn_ref[...] = hn.astype(BF16)

        hn = hn_ref[...]
        g = _dot_nt(hn, wg_ref[...])
        u = _dot_nt(hn, wu_ref[...])
        go_ref[...] = g.astype(BF16)
        uo_ref[...] = u.astype(BF16)
        a_ref[...] = (g * jax.nn.sigmoid(g) * u).astype(BF16)

    vec = pl.BlockSpec((1, d_dim), lambda i, j: (0, 0))
    w_spec = pl.BlockSpec((tn, d_dim), lambda i, j: (j, 0))
    act = pl.BlockSpec((tm, tn), lambda i, j: (i, j))
    act_shape = jax.ShapeDtypeStruct((s_dim, f_dim), BF16)
    return _pcall(
        body,
        out_shape=(jax.ShapeDtypeStruct((s_dim, d_dim), BF16), act_shape, act_shape, act_shape),
        grid=(s_dim // tm, f_dim // tn),
        in_specs=[pl.BlockSpec((tm, d_dim), lambda i, j: (i, 0)), vec, vec, vec, w_spec, w_spec],
        out_specs=(pl.BlockSpec((tm, d_dim), lambda i, j: (i, 0)), act, act, act),
        name=name,
        compiler_params=_params(("parallel", "arbitrary"),
                                [((tm, d_dim), F32), ((tn, d_dim), BF16), ((tn, d_dim), BF16), ((tm, d_dim), BF16)]
                                + 3 * [((tm, tn), BF16)], extra=3 * tm * d_dim * 4 + 4 * tm * tn * 4),
    )(x, pre_g, scale, shift, wg_t, wu_t)


def _mm_post(a, w, x, post_g, gate, res_w, name):
    s_dim, k_dim = a.shape
    d_dim = w.shape[1]
    tm = TOKEN_TILE

    def body(a_ref, w_ref, x_ref, pg_ref, gt_ref, xo_ref, f_ref):
        f = _dot_nn(a_ref[...], w_ref[...])
        y = f * _rstd(f) * pg_ref[...]
        f_ref[...] = f
        xo_ref[...] = x_ref[...] + (res_w * gt_ref[...]) * y

    vec = pl.BlockSpec((1, d_dim), lambda i: (0, 0))
    row = pl.BlockSpec((tm, d_dim), lambda i: (i, 0))
    out = jax.ShapeDtypeStruct((s_dim, d_dim), F32)
    return _pcall(
        body, out_shape=(out, out), grid=(s_dim // tm,),
        in_specs=[pl.BlockSpec((tm, k_dim), lambda i: (i, 0)), pl.BlockSpec((k_dim, d_dim), lambda i: (0, 0)), row, vec, vec],
        out_specs=(row, row), name=name,
        compiler_params=_params(("parallel",), [((tm, k_dim), BF16), ((k_dim, d_dim), BF16)] + 3 * [((tm, d_dim), F32)],
                                extra=3 * tm * d_dim * 4),
    )(a, w, x, post_g, gate)


def _post_bwd(dx_out, f, post_g, gate, res_w, name):
    s_dim, d_dim = f.shape
    tm = TOKEN_TILE

    def body(dx_ref, f_ref, pg_ref, gt_ref, df_ref, dgate_ref, dpost_ref):
        @pl.when(pl.program_id(0) == 0)
        def _():
            dgate_ref[...] = jnp.zeros_like(dgate_ref)
            dpost_ref[...] = jnp.zeros_like(dpost_ref)

        dx, fv = dx_ref[...], f_ref[...]
        r = _rstd(fv)
        fr = fv * r
        dgate_ref[...] += res_w * jnp.sum(dx * (fr * pg_ref[...]), axis=0, keepdims=True)
        dy = (res_w * gt_ref[...]) * dx
        dpost_ref[...] += jnp.sum(dy * fr, axis=0, keepdims=True)
        df_ref[...] = _rms_bwd(fv, r, dy * pg_ref[...]).astype(BF16)

    vec = pl.BlockSpec((1, d_dim), lambda i: (0, 0))
    row = pl.BlockSpec((tm, d_dim), lambda i: (i, 0))
    vshape = jax.ShapeDtypeStruct((1, d_dim), F32)
    return _pcall(
        body, out_shape=(jax.ShapeDtypeStruct((s_dim, d_dim), BF16), vshape, vshape), grid=(s_dim // tm,),
        in_specs=[row, row, vec, vec], out_specs=(row, vec, vec), name=name,
        compiler_params=_params(("arbitrary",), 3 * [((tm, d_dim), F32)], extra=6 * tm * d_dim * 4),
    )(dx_out, f, post_g, gate)


def _prenorm_bwd(dx_out, dhns, x, pre_g, scale, name):
    s_dim, d_dim = x.shape
    tm = TOKEN_TILE
    n_in = len(dhns)

    def body(*refs):
        dx_ref, x_ref, pg_ref, sc_ref = refs[n_in + 0], refs[n_in + 1], refs[n_in + 2], refs[n_in + 3]
        dxo_ref, dsh_ref, dsc_ref, dpg_ref = refs[n_in + 4:]

        @pl.when(pl.program_id(0) == 0)
        def _():
            dsh_ref[...] = jnp.zeros_like(dsh_ref)
            dsc_ref[...] = jnp.zeros_like(dsc_ref)
            dpg_ref[...] = jnp.zeros_like(dpg_ref)

        dhn = refs[0][...]
        for k in range(1, n_in):
            dhn = dhn + refs[k][...]
        xv = x_ref[...]
        r = _rstd(xv)
        xr = xv * r
        dsh_ref[...] += jnp.sum(dhn, axis=0, keepdims=True)
        dsc_ref[...] += jnp.sum(dhn * (xr * pg_ref[...]), axis=0, keepdims=True)
        dn = dhn * (1.0 + sc_ref[...])
        dpg_ref[...] += jnp.sum(dn * xr, axis=0, keepdims=True)
        dxo_ref[...] = dx_ref[...] + _rms_bwd(xv, r, dn * pg_ref[...])

    vec = pl.BlockSpec((1, d_dim), lambda i: (0, 0))
    row = pl.BlockSpec((tm, d_dim), lambda i: (i, 0))
    vshape = jax.ShapeDtypeStruct((1, d_dim), F32)
    return _pcall(
        body, out_shape=(jax.ShapeDtypeStruct((s_dim, d_dim), F32), vshape, vshape, vshape), grid=(s_dim // tm,),
        in_specs=n_in * [row] + [row, row, vec, vec], out_specs=(row, vec, vec, vec), name=name,
        compiler_params=_params(("arbitrary",), (n_in + 3) * [((tm, d_dim), F32)], extra=6 * tm * d_dim * 4),
    )(*dhns, dx_out, x, pre_g, scale)


def _ffn_dgu(df, wd, g, u, name, after=None):
    s_dim, d_dim = df.shape
    f_dim = wd.shape[0]
    tm, tn = TOKEN_TILE, f_dim // 2

    def body(df_ref, wd_ref, g_ref, u_ref, *rest):
        dg_ref, du_ref = rest[-2:]
        da = _dot_nt(df_ref[...], wd_ref[...])
        gv, uv = g_ref[...].astype(F32), u_ref[...].astype(F32)
        sg = jax.nn.sigmoid(gv)
        du_ref[...] = (da * (gv * sg)).astype(BF16)
        dg_ref[...] = (da * uv * (sg * (1.0 + gv * (1.0 - sg)))).astype(BF16)

    act = pl.BlockSpec((tm, tn), lambda i, j: (i, j))
    act_shape = jax.ShapeDtypeStruct((s_dim, f_dim), BF16)
    token = [] if after is None else [after]
    return _pcall(
        body, out_shape=(act_shape, act_shape), grid=(s_dim // tm, f_dim // tn),
        in_specs=[pl.BlockSpec((tm, d_dim), lambda i, j: (i, 0)), pl.BlockSpec((tn, d_dim), lambda i, j: (j, 0)), act, act]
        + len(token) * [_TOKEN_SPEC],
        out_specs=(act, act), name=name,
        compiler_params=_params(("parallel", "parallel"), [((tm, d_dim), BF16), ((tn, d_dim), BF16)] + 4 * [((tm, tn), BF16)],
                                extra=6 * tm * tn * 4),
    )(df, wd, g, u, *token)


def _rope_tables():
    half = QK_ROPE // 2
    freqs = ROPE_THETA ** (-jnp.arange(half, dtype=F32) / half)
    ang = jnp.arange(SEQ, dtype=F32)[:, None] * freqs[None, :]
    cos, sin = jnp.cos(ang), jnp.sin(ang)
    ones = jnp.ones((SEQ, QK_NOPE), F32)
    zeros = jnp.zeros((SEQ, QK_NOPE), F32)
    pad1 = jnp.ones((SEQ, HEAD_PAD - QK_NOPE - QK_ROPE), F32)
    pad0 = jnp.zeros((SEQ, HEAD_PAD - QK_NOPE - QK_ROPE), F32)
    zh = jnp.zeros((SEQ, half), F32)
    c = jnp.concatenate([ones, cos, cos, pad1], axis=1)
    s1 = jnp.concatenate([zeros, -sin, zh, pad0], axis=1)
    s2 = jnp.concatenate([zeros, zh, sin, pad0], axis=1)
    return c, s1, s2


def _rope(v, c, s1, s2):
    half = QK_ROPE // 2
    return v * c + pltpu.roll(v, HEAD_PAD - half, 1) * s1 + pltpu.roll(v, half, 1) * s2


def _rope_t(dv, c, s1, s2):
    half = QK_ROPE // 2
    return dv * c + pltpu.roll(dv * s1, half, 1) + pltpu.roll(dv * s2, HEAD_PAD - half, 1)


def _mla_qkv(lat, q_norm, kv_norm, wq_t, wkv_t, rope, name):
    s_dim = lat.shape[0]
    width = MLA_HEADS * HEAD_PAD
    tm = 256

    def body(lat_ref, qg_ref, kg_ref, wq_ref, wkv_ref, c_ref, s1_ref, s2_ref, q_ref, k_ref, v_ref, qn_ref, kvn_ref):
        cq = lat_ref[:, :Q_LORA]
        ckv = lat_ref[:, Q_LORA:Q_LORA + KV_LORA]
        kr = lat_ref[:, Q_LORA + KV_LORA:]
        c, s1, s2 = c_ref[...], s1_ref[...], s2_ref[...]
        qn = (cq * _rstd(cq) * qg_ref[...]).astype(BF16)
        kvn = (ckv * _rstd(ckv) * kg_ref[...]).astype(BF16)
        qn_ref[...] = qn
        kvn_ref[...] = kvn
        q = _dot_nt(qn, wq_ref[...])
        kv = _dot_nt(kvn, wkv_ref[...])
        krr = _rope(kr, c, s1, s2)
        low = lax.broadcasted_iota(jnp.int32, (tm, HEAD_PAD), 1) < QK_NOPE
        for h in range(MLA_HEADS):
            sl = slice(h * HEAD_PAD, (h + 1) * HEAD_PAD)
            q_ref[:, sl] = _rope(q[:, sl], c, s1, s2).astype(BF16)
            kvh = kv[:, sl]
            k_ref[:, sl] = (jnp.where(low, kvh, 0.0) + krr).astype(BF16)
            v_ref[:, sl] = jnp.where(low, 0.0, kvh).astype(BF16)

    row = lambda n: pl.BlockSpec((tm, n), lambda i: (i, 0))
    full = lambda a: pl.BlockSpec(a.shape, lambda i: (0, 0))
    wide = jax.ShapeDtypeStruct((s_dim, width), BF16)
    return _pcall(
        body,
        out_shape=(wide, wide, wide, jax.ShapeDtypeStruct((s_dim, Q_LORA), BF16), jax.ShapeDtypeStruct((s_dim, KV_LORA), BF16)),
        grid=(s_dim // tm,),
        in_specs=[row(LAT_PAD), full(q_norm), full(kv_norm), full(wq_t), full(wkv_t), row(HEAD_PAD), row(HEAD_PAD), row(HEAD_PAD)],
        out_specs=(row(width), row(width), row(width), row(Q_LORA), row(KV_LORA)), name=name,
        compiler_params=_params(("parallel",), [((tm, LAT_PAD), F32), (wq_t.shape, BF16), (wkv_t.shape, BF16)]
                                + 3 * [((tm, width), BF16)], extra=4 * tm * width * 4),
    )(lat, q_norm, kv_norm, wq_t, wkv_t, *rope)


def _mla_probs(q, k_ref, t, tq):
    lo = t * tq
    own = slice(lo, lo + tq)
    s_own = _dot_nt(q, k_ref[own, :]) * MLA_SCALE
    rows = lax.broadcasted_iota(jnp.int32, s_own.shape, 0)
    cols = lax.broadcasted_iota(jnp.int32, s_own.shape, 1)
    s_own = jnp.where(cols <= rows, s_own, -jnp.inf)
    mx = jnp.max(s_own, axis=-1, keepdims=True)
    if t == 0:
        e_own = jnp.exp(s_own - mx)
        return [(e_own * (1.0 / jnp.sum(e_own, axis=-1, keepdims=True)), own)]
    before = slice(0, lo)
    s_pre = _dot_nt(q, k_ref[before, :]) * MLA_SCALE
    mx = jnp.maximum(mx, jnp.max(s_pre, axis=-1, keepdims=True))
    e_own, e_pre = jnp.exp(s_own - mx), jnp.exp(s_pre - mx)
    inv = 1.0 / (jnp.sum(e_own, axis=-1, keepdims=True) + jnp.sum(e_pre, axis=-1, keepdims=True))
    return [(e_pre * inv, before), (e_own * inv, own)]


def _mla_attn_fwd(q, k, v, name):
    s_dim = q.shape[0]
    tq = 512

    def body(q_ref, k_ref, v_ref, o_ref):
        for t in range(s_dim // tq):
            tile = slice(t * tq, (t + 1) * tq)
            o = None
            for p, keys in _mla_probs(q_ref[tile, :], k_ref, t, tq):
                part = _dot_nn(p.astype(BF16), v_ref[keys, :])
                o = part if o is None else o + part
            o_ref[tile, :] = o.astype(BF16)

    head = pl.BlockSpec((s_dim, HEAD_PAD), lambda h: (0, h))
    return _pcall(
        body, out_shape=jax.ShapeDtypeStruct(q.shape, BF16), grid=(MLA_HEADS,),
        in_specs=[head, head, head], out_specs=head, name=name,
        compiler_params=_params(("parallel",), 4 * [((s_dim, HEAD_PAD), BF16)], extra=4 * tq * s_dim * 4),
    )(q, k, v)


def _mla_attn_bwd(q, k, v, d_o, name):
    s_dim = q.shape[0]
    tq = 512

    def body(q_ref, k_ref, v_ref, do_ref, dq_ref, dk_ref, dv_ref):
        dk_ref[...] = jnp.zeros_like(dk_ref)
        dv_ref[...] = jnp.zeros_like(dv_ref)
        for t in range(s_dim // tq):
            tile = slice(t * tq, (t + 1) * tq)
            qt = q_ref[tile, :]
            dot = do_ref[tile, :].astype(BF16)
            pieces = [(p, keys, _dot_nt(dot, v_ref[keys, :])) for p, keys in _mla_probs(qt, k_ref, t, tq)]
            row = None
            for p, _, dp in pieces:
                part = jnp.sum(p * dp, axis=-1, keepdims=True)
                row = part if row is None else row + part
            dq = None
            for p, keys, dp in pieces:
                dsb = (p * (dp - row) * MLA_SCALE).astype(BF16)
                part = _dot_nn(dsb, k_ref[keys, :])
                dq = part if dq is None else dq + part
                dk_ref[keys, :] += _dot_tn(dsb, qt)
                dv_ref[keys, :] += _dot_tn(p.astype(BF16), dot)
            dq_ref[tile, :] = dq

    head = pl.BlockSpec((s_dim, HEAD_PAD), lambda h: (0, h))
    out = jax.ShapeDtypeStruct(q.shape, F32)
    return _pcall(
        body, out_shape=(out, out, out), grid=(MLA_HEADS,),
        in_specs=[head, head, head, head], out_specs=(head, head, head), name=name,
        compiler_params=_params(("parallel",), 3 * [((s_dim, HEAD_PAD), BF16)] + 4 * [((s_dim, HEAD_PAD), F32)],
                                extra=6 * tq * s_dim * 4),
    )(q, k, v, d_o)


def _mla_qkv_bwd(dq, dk, dv, lat, q_norm, kv_norm, wq_t, wkv_t, rope, name):
    s_dim = lat.shape[0]
    width = MLA_HEADS * HEAD_PAD
    tm = 256

    def body(dq_ref, dk_ref, dv_ref, lat_ref, qg_ref, kg_ref, wq_ref, wkv_ref, c_ref, s1_ref, s2_ref,
             dqp_ref, dkv_ref, dlat_ref, dqg_ref, dkg_ref):
        @pl.when(pl.program_id(0) == 0)
        def _():
            dqg_ref[...] = jnp.zeros_like(dqg_ref)
            dkg_ref[...] = jnp.zeros_like(dkg_ref)

        c, s1, s2 = c_ref[...], s1_ref[...], s2_ref[...]
        lane = lax.broadcasted_iota(jnp.int32, (tm, HEAD_PAD), 1)
        low = lane < QK_NOPE
        rot = (lane >= QK_NOPE) & (lane < QK_NOPE + QK_ROPE)
        dkrr = jnp.zeros((tm, HEAD_PAD), F32)
        for h in range(MLA_HEADS):
            sl = slice(h * HEAD_PAD, (h + 1) * HEAD_PAD)
            dqp_ref[:, sl] = _rope_t(dq_ref[:, sl], c, s1, s2).astype(BF16)
            dkh = dk_ref[:, sl]
            dkv_ref[:, sl] = jnp.where(low, dkh, dv_ref[:, sl]).astype(BF16)
            dkrr = dkrr + jnp.where(rot, dkh, 0.0)
        dqn = _dot_nn(dqp_ref[...], wq_ref[...])
        dkvn = _dot_nn(dkv_ref[...], wkv_ref[...])
        cq = lat_ref[:, :Q_LORA]
        ckv = lat_ref[:, Q_LORA:Q_LORA + KV_LORA]
        rq, rkv = _rstd(cq), _rstd(ckv)
        dqg_ref[...] += jnp.sum(dqn * cq * rq, axis=0, keepdims=True)
        dkg_ref[...] += jnp.sum(dkvn * ckv * rkv, axis=0, keepdims=True)
        dlat_ref[:, :Q_LORA] = _rms_bwd(cq, rq, dqn * qg_ref[...])
        dlat_ref[:, Q_LORA:Q_LORA + KV_LORA] = _rms_bwd(ckv, rkv, dkvn * kg_ref[...])
        dlat_ref[:, Q_LORA + KV_LORA:] = _rope_t(dkrr, c, s1, s2)

    row = lambda n: pl.BlockSpec((tm, n), lambda i: (i, 0))
    full = lambda a: pl.BlockSpec(a.shape, lambda i: (0, 0))
    wide = jax.ShapeDtypeStruct((s_dim, width), BF16)
    return _pcall(
        body,
        out_shape=(wide, wide, jax.ShapeDtypeStruct((s_dim, LAT_PAD), F32),
                   jax.ShapeDtypeStruct(q_norm.shape, F32), jax.ShapeDtypeStruct(kv_norm.shape, F32)),
        grid=(s_dim // tm,),
        in_specs=[row(width), row(width), row(width), row(LAT_PAD), full(q_norm), full(kv_norm), full(wq_t), full(wkv_t),
                  row(HEAD_PAD), row(HEAD_PAD), row(HEAD_PAD)],
        out_specs=(row(width), row(width), row(LAT_PAD), full(q_norm), full(kv_norm)), name=name,
        compiler_params=_params(("arbitrary",), 3 * [((tm, width), F32)] + [((tm, LAT_PAD), F32), (wq_t.shape, BF16),
                                                                           (wkv_t.shape, BF16)] + 2 * [((tm, width), BF16)],
                                extra=2 * tm * width * 4),
    )(dq, dk, dv, lat, q_norm, kv_norm, wq_t, wkv_t, *rope)


def _t5_bucket(dist):
    max_exact = N_BUCKETS // 2
    d = jnp.maximum(dist, 1).astype(F32)
    large = max_exact + (jnp.log(d / max_exact) / math.log(MAX_DISTANCE / max_exact)
                         * (N_BUCKETS - max_exact)).astype(jnp.int32)
    large = jnp.minimum(large, N_BUCKETS - 1)
    return jnp.where(dist < max_exact, dist, large)


def _dil_buckets(dilation):
    iq = jnp.arange(DIL_BLOCK)[:, None]
    ik = jnp.arange(2 * DIL_BLOCK)[None, :]
    return _t5_bucket(jnp.maximum(DIL_BLOCK + iq - ik, 0) * dilation)


def _dil_logits(qh, kb, bias_h, first, span):
    if first:
        s = _dot_nt(qh, kb) * DIL_SCALE + bias_h[:, DIL_BLOCK:]
        rel = lax.broadcasted_iota(jnp.int32, s.shape, 0) - lax.broadcasted_iota(jnp.int32, s.shape, 1)
    else:
        s = _dot_nt(qh, kb) * DIL_SCALE + bias_h
        rel = DIL_BLOCK + lax.broadcasted_iota(jnp.int32, s.shape, 0) - lax.broadcasted_iota(jnp.int32, s.shape, 1)
    return jnp.where((rel >= 0) & (rel <= span), s, -jnp.inf)


def _dil_blocks(s_dim, dilation):
    rows = s_dim // dilation
    for r in range(dilation):
        for n in range(rows // DIL_BLOCK):
            lo = r * rows + n * DIL_BLOCK
            keys = slice(lo, lo + DIL_BLOCK) if n == 0 else slice(lo - DIL_BLOCK, lo + DIL_BLOCK)
            start = r + n * DIL_BLOCK * dilation
            tokens = slice(start, start + DIL_BLOCK) if dilation == 1 else pl.ds(start, DIL_BLOCK, stride=dilation)
            yield n == 0, slice(lo, lo + DIL_BLOCK), keys, tokens


def _dil_views(s_dim):
    col = lambda which: pl.BlockSpec((s_dim, HEAD_PAD), lambda p: (0, which * DIL_PAIRS + p))
    nat = pl.BlockSpec((s_dim, HEAD_PAD), lambda p: (0, p))
    bias = pl.BlockSpec((2, DIL_BLOCK, 2 * DIL_BLOCK), lambda p: (p, 0, 0))
    return col, nat, bias


def _dil_attn_fwd(qkv, bias, dilation, span, name):
    s_dim = qkv.shape[0]
    d_dim = DIL_HEADS * DIL_HEAD_DIM
    col, nat, bias_spec = _dil_views(s_dim)

    def body(q_ref, k_ref, v_ref, b_ref, o_ref, l_ref):
        lane = lax.broadcasted_iota(jnp.int32, (DIL_BLOCK, HEAD_PAD), 1)
        klane = lax.broadcasted_iota(jnp.int32, (2 * DIL_BLOCK, HEAD_PAD), 1)
        for first, blk, keys, tokens in _dil_blocks(s_dim, dilation):
            qb, kb, vb = q_ref[blk, :], k_ref[keys, :], v_ref[keys, :]
            o_acc = jnp.zeros((DIL_BLOCK, HEAD_PAD), F32)
            lse_acc = jnp.zeros((DIL_BLOCK, HEAD_PAD), F32)
            for h in range(2):
                mine = (lane < DIL_HEAD_DIM) == (h == 0)
                kmine = (klane[:vb.shape[0]] < DIL_HEAD_DIM) == (h == 0)
                logits = _dil_logits(jnp.where(mine, qb, 0), kb, b_ref[h], first, span)
                mx = jnp.max(logits, axis=-1, keepdims=True)
                e = jnp.exp(logits - mx)
                tot = jnp.sum(e, axis=-1, keepdims=True)
                lse = mx + jnp.log(tot)
                p = e * (1.0 / tot)
                o_acc = o_acc + _dot_nn(p.astype(BF16), jnp.where(kmine, vb, 0))
                lse_acc = jnp.where(mine, lse, lse_acc)
            o_ref[tokens, :] = o_acc
            l_ref[tokens, :] = lse_acc

    out = jax.ShapeDtypeStruct((s_dim, d_dim), F32)
    return _pcall(
        body, out_shape=(out, out), grid=(DIL_PAIRS,),
        in_specs=[col(0), col(1), col(2), bias_spec], out_specs=(nat, nat), name=name,
        compiler_params=_params(("parallel",), 3 * [((s_dim, HEAD_PAD), BF16)] + 2 * [((s_dim, HEAD_PAD), F32)]
                                + [((2, DIL_BLOCK, 2 * DIL_BLOCK), F32)], extra=2**21),
    )(qkv, qkv, qkv, bias)


def _dil_mix(lses, outs, name):
    s_dim, d_dim = outs[0].shape
    tm = TOKEN_TILE
    ng = len(outs)

    def body(*refs):
        ls = [refs[g][...] for g in range(ng)]
        mx = ls[0]
        for g in range(1, ng):
            mx = jnp.maximum(mx, ls[g])
        es = [jnp.exp(l - mx) for l in ls]
        tot = es[0]
        for g in range(1, ng):
            tot = tot + es[g]
        o = None
        for g in range(ng):
            al = es[g] / tot
            refs[2 * ng + g][...] = al
            t = al * refs[ng + g][...]
            o = t if o is None else o + t
        refs[3 * ng][...] = o
        refs[3 * ng + 1][...] = o.astype(BF16)

    row = pl.BlockSpec((tm, d_dim), lambda i: (i, 0))
    f = jax.ShapeDtypeStruct((s_dim, d_dim), F32)
    res = _pcall(
        body, out_shape=tuple(ng * [f] + [f, jax.ShapeDtypeStruct((s_dim, d_dim), BF16)]), grid=(s_dim // tm,),
        in_specs=2 * ng * [row], out_specs=tuple((ng + 2) * [row]), name=name,
        compiler_params=_params(("parallel",), (3 * ng + 2) * [((tm, d_dim), F32)], extra=4 * tm * d_dim * 4),
    )(*lses, *outs)
    return res[:ng], res[ng], res[ng + 1]


def _dil_attn_bwd(qkv, bias, d_o, o_mix, alpha, lse, dilation, span, name):
    s_dim = qkv.shape[0]
    d_dim = DIL_HEADS * DIL_HEAD_DIM
    col, nat, bias_spec = _dil_views(s_dim)

    def body(q_ref, k_ref, v_ref, b_ref, do_ref, om_ref, al_ref, l_ref, dq_ref, dk_ref, dv_ref, db_ref, dk_acc, dv_acc):
        db_ref[...] = jnp.zeros_like(db_ref)
        dk_acc[...] = jnp.zeros_like(dk_acc)
        dv_acc[...] = jnp.zeros_like(dv_acc)
        lane = lax.broadcasted_iota(jnp.int32, (DIL_BLOCK, HEAD_PAD), 1)
        klane = lax.broadcasted_iota(jnp.int32, (2 * DIL_BLOCK, HEAD_PAD), 1)
        for first, blk, kv_rows, tokens in _dil_blocks(s_dim, dilation):
            qb, kb, vb = q_ref[blk, :], k_ref[kv_rows, :], v_ref[kv_rows, :]
            al = al_ref[tokens, :]
            dog = al * do_ref[tokens, :]
            row_term = dog * om_ref[tokens, :]
            lse_b = l_ref[tokens, :]
            dq_acc = jnp.zeros((DIL_BLOCK, HEAD_PAD), F32)
            dk_blk = jnp.zeros((kb.shape[0], HEAD_PAD), F32)
            dv_blk = jnp.zeros((kb.shape[0], HEAD_PAD), F32)
            for h in range(2):
                mine = (lane < DIL_HEAD_DIM) == (h == 0)
                kmine = (klane[:kb.shape[0]] < DIL_HEAD_DIM) == (h == 0)
                qh = jnp.where(mine, qb, 0)
                logits = _dil_logits(qh, kb, b_ref[h], first, span)
                lse_h = jnp.max(jnp.where(mine, lse_b, -jnp.inf), axis=-1, keepdims=True)
                p = jnp.exp(logits - lse_h)
                dogh = jnp.where(mine, dog, 0.0).astype(BF16)
                dp = _dot_nt(dogh, vb)
                ds = p * (dp - jnp.sum(jnp.where(mine, row_term, 0.0), axis=-1, keepdims=True))
                if first:
                    db_ref[h, :, DIL_BLOCK:] += ds
                else:
                    db_ref[h] += ds
                dsb = (ds * DIL_SCALE).astype(BF16)
                dq_acc = dq_acc + _dot_nn(dsb, jnp.where(kmine, kb, 0))
                dk_blk = dk_blk + _dot_tn(dsb, qh)
                dv_blk = dv_blk + _dot_tn(p.astype(BF16), dogh)
            dq_ref[blk, :] = dq_acc.astype(BF16)
            dk_acc[kv_rows, :] += dk_blk
            dv_acc[kv_rows, :] += dv_blk
        dk_ref[...] = dk_acc[...].astype(BF16)
        dv_ref[...] = dv_acc[...].astype(BF16)

    grad = jax.ShapeDtypeStruct((s_dim, d_dim), BF16)
    return _pcall(
        body, out_shape=(grad, grad, grad, jax.ShapeDtypeStruct(bias.shape, F32)), grid=(DIL_PAIRS,),
        in_specs=[col(0), col(1), col(2), bias_spec, nat, nat, nat, nat],
        out_specs=(nat, nat, nat, bias_spec), name=name,
        scratch_shapes=[pltpu.VMEM((s_dim, HEAD_PAD), F32), pltpu.VMEM((s_dim, HEAD_PAD), F32)],
        compiler_params=_params(("parallel",), 6 * [((s_dim, HEAD_PAD), BF16)] + 4 * [((s_dim, HEAD_PAD), F32)]
                                + 2 * [((2, DIL_BLOCK, 2 * DIL_BLOCK), F32)], extra=2 * s_dim * HEAD_PAD * 4 + 2**21),
    )(qkv, qkv, qkv, bias, d_o, o_mix, alpha, lse)


def _bias_reduce(dbias, buckets, name):
    n_heads = dbias.shape[0]

    def body(db_ref, bk_ref, o_ref):
        ds, bk = db_ref[0], bk_ref[0]
        lane = lax.broadcasted_iota(jnp.int32, (8, HEAD_PAD), 1)
        acc = jnp.zeros((8, HEAD_PAD), F32)
        for b in range(N_BUCKETS):
            acc = jnp.where(lane == b, jnp.sum(jnp.where(bk == b, ds, 0.0)), acc)
        o_ref[0] = acc

    blk = (1, DIL_BLOCK, 2 * DIL_BLOCK)
    return _pcall(
        body, out_shape=jax.ShapeDtypeStruct((n_heads, 8, HEAD_PAD), F32), grid=(n_heads,),
        in_specs=[pl.BlockSpec(blk, lambda h: (h, 0, 0)), pl.BlockSpec(blk, lambda h: (h // DIL_HEADS, 0, 0))],
        out_specs=pl.BlockSpec((1, 8, HEAD_PAD), lambda h: (h, 0, 0)), name=name,
        compiler_params=_params(("parallel",), [(blk, F32), (blk, jnp.int32)], extra=2**20),
    )(dbias, buckets)


def _loss_grad(y, target, name):
    s_dim, d_dim = y.shape
    tm = TOKEN_TILE

    def body(y_ref, t_ref, dy_ref, l_ref):
        @pl.when(pl.program_id(0) == 0)
        def _():
            l_ref[...] = jnp.zeros_like(l_ref)

        err = y_ref[...] - t_ref[...]
        dy_ref[...] = err / d_dim
        sq = (err * err).reshape(tm // 8, 8, d_dim)
        l_ref[...] += 0.5 * jnp.sum(sq, axis=0) / d_dim

    row = pl.BlockSpec((tm, d_dim), lambda i: (i, 0))
    acc = pl.BlockSpec((8, d_dim), lambda i: (0, 0))
    return _pcall(
        body, out_shape=(jax.ShapeDtypeStruct((s_dim, d_dim), F32), jax.ShapeDtypeStruct((8, d_dim), F32)),
        grid=(s_dim // tm,), in_specs=[row, row], out_specs=(row, acc), name=name,
        compiler_params=_params(("arbitrary",), 3 * [((tm, d_dim), F32)], extra=2 * tm * d_dim * 4),
    )(y, target)


def _mod_fwd(c_all, w_mod, b_loc, name):
    depth, d_dim, n = w_mod.shape
    nb = c_all.shape[0]

    def body(c_ref, w_ref, b_ref, o_ref, s_ref):
        cv = c_ref[...]
        sc = cv * jax.nn.sigmoid(cv)
        s_ref[...] = sc
        o_ref[0] = _dot_nn(sc.astype(BF16), w_ref[0].astype(BF16)) + b_ref[0]

    return _pcall(
        body, out_shape=(jax.ShapeDtypeStruct((depth, nb, n), F32), jax.ShapeDtypeStruct((nb, d_dim), F32)), grid=(depth,),
        in_specs=[pl.BlockSpec((nb, d_dim), lambda i: (0, 0)), pl.BlockSpec((1, d_dim, n), lambda i: (i, 0, 0)),
                  pl.BlockSpec((1, 1, n), lambda i: (i, 0, 0))],
        out_specs=(pl.BlockSpec((1, nb, n), lambda i: (i, 0, 0)), pl.BlockSpec((nb, d_dim), lambda i: (0, 0))), name=name,
        compiler_params=_params(("arbitrary",), [((1, d_dim, n), F32)], extra=d_dim * n * 2 + 2**20),
    )(c_all, w_mod, b_loc.reshape(depth, 1, n))


def _sum_parts(parts, name):
    _, rows, cols = parts.shape
    fits = [t for t in range(16, rows // 2 + 1, 16) if rows % t == 0 and NDEV * t * cols * parts.dtype.itemsize <= 3 * 2**20]
    tr = max(fits) if fits else rows

    def body(p_ref, o_ref):
        acc = p_ref[0].astype(F32)
        for k in range(1, NDEV):
            acc = acc + p_ref[k].astype(F32)
        o_ref[...] = acc

    return _pcall(
        body, out_shape=jax.ShapeDtypeStruct((rows, cols), F32), grid=(rows // tr,),
        in_specs=[pl.BlockSpec((NDEV, tr, cols), lambda i: (0, i, 0))], out_specs=pl.BlockSpec((tr, cols), lambda i: (i, 0)),
        name=name, compiler_params=_params(("parallel",), [((NDEV, tr, cols), parts.dtype), ((tr, cols), F32)], extra=2**20),
    )(parts)


def _adamw(w, g, m, v, name):
    shape = w.shape
    cols = shape[-1]
    rows = math.prod(shape[:-1])
    tr = rows
    for cand in (512, 256, 128, 64, 32, 16, 8):
        if rows % cand == 0 and rows > cand and cand * cols * 4 <= 2**21:
            tr = cand
            break

    def body(w_ref, g_ref, m_ref, v_ref, d_ref, mo_ref, vo_ref):
        gv = g_ref[...]
        mn = ADAM_B1 * m_ref[...] + (1.0 - ADAM_B1) * gv
        vn = ADAM_B2 * v_ref[...] + (1.0 - ADAM_B2) * (gv * gv)
        m_hat = mn / (1.0 - ADAM_B1 ** ADAM_STEP)
        v_hat = vn / (1.0 - ADAM_B2 ** ADAM_STEP)
        d_ref[...] = -ADAM_LR * (m_hat / (jnp.sqrt(v_hat) + ADAM_EPS) + ADAM_WD * w_ref[...])
        mo_ref[...] = mn
        vo_ref[...] = vn

    blk = pl.BlockSpec((tr, cols), lambda i: (i, 0))
    out = jax.ShapeDtypeStruct((rows, cols), F32)
    res = _pcall(
        body, out_shape=(out, out, out), grid=(rows // tr,), in_specs=4 * [blk], out_specs=(blk, blk, blk), name=name,
        compiler_params=_params(("parallel",), 7 * [((tr, cols), F32)], extra=4 * tr * cols * 4),
    )(*(a.reshape(rows, cols) for a in (w, g, m, v)))
    return tuple(r.reshape(shape) for r in res)


def _peers():
    x, y, c = lax.axis_index("x"), lax.axis_index("y"), lax.axis_index("c")
    flip = lambda v, f: 1 - v if f else v
    peers = []
    for f in range(1, NDEV):
        px, py, pc = flip(x, f & 4), flip(y, f & 2), flip(c, f & 1)
        peers.append(((px, py, pc), 4 * px + 2 * py + pc))
    return (x, y, c), 4 * x + 2 * y + c, peers


def _places():
    x, y, c = lax.axis_index("x"), lax.axis_index("y"), lax.axis_index("c")
    place = lambda px, py, pc: ((px, py, pc), 4 * px + 2 * py + pc)
    return place(x, y, c), place(x, y, 1 - c), [place(1 - x, y, c), place(x, 1 - y, c), place(1 - x, 1 - y, c)]


def _exchange(arrs, gather, name):
    n = len(arrs)
    hbm = pl.BlockSpec(memory_space=pltpu.HBM)
    if gather:
        out_shape = [jax.ShapeDtypeStruct((NDEV * a.shape[0], a.shape[1]), a.dtype) for a in arrs]
    else:
        out_shape = [jax.ShapeDtypeStruct((NDEV, a.shape[0] // NDEV, a.shape[1]), a.dtype) for a in arrs]

    def body(*refs):
        ins, outs = refs[:n], refs[n:2 * n]
        send_sems, recv_sems, local_sems = refs[2 * n:]
        me_pos, me, peers = _peers()
        local = []
        for k in range(n):
            rows = arrs[k].shape[0] if gather else arrs[k].shape[0] // NDEV
            if gather:
                src_of = lambda idx: ins[k]
                dst_of = lambda idx: outs[k].at[pl.ds(me * rows, rows)]
                mine = (ins[k], outs[k].at[pl.ds(me * rows, rows)])
            else:
                src_of = lambda idx: ins[k].at[pl.ds(idx * rows, rows)]
                dst_of = lambda idx: outs[k].at[me]
                mine = (ins[k].at[pl.ds(me * rows, rows)], outs[k].at[me])
            cp = pltpu.make_async_copy(mine[0], mine[1], local_sems.at[k])
            cp.start()
            local.append(cp)
            for pos, idx in peers:
                pltpu.make_async_remote_copy(src_ref=src_of(idx), dst_ref=dst_of(idx), send_sem=send_sems.at[k],
                                             recv_sem=recv_sems.at[k], device_id=pos, device_id_type=MESH).start()
        for k in range(n):
            rows = arrs[k].shape[0] if gather else arrs[k].shape[0] // NDEV
            sent = ins[k].at[pl.ds(0, (NDEV - 1) * rows)] if not gather else outs[k].at[pl.ds(0, (NDEV - 1) * rows)]
            got = outs[k].at[pl.ds(0, (NDEV - 1) * rows)] if gather else outs[k].at[pl.ds(0, NDEV - 1)]
            pltpu.make_async_remote_copy(src_ref=sent, dst_ref=sent, send_sem=send_sems.at[k], recv_sem=recv_sems.at[k],
                                         device_id=me_pos, device_id_type=MESH).wait_send()
            pltpu.make_async_remote_copy(src_ref=got, dst_ref=got, send_sem=send_sems.at[k], recv_sem=recv_sems.at[k],
                                         device_id=me_pos, device_id_type=MESH).wait_recv()
            local[k].wait()

    return pl.pallas_call(
        body, out_shape=out_shape, in_specs=n * [hbm], out_specs=n * [hbm], name=name,
        scratch_shapes=[pltpu.SemaphoreType.DMA((n,)), pltpu.SemaphoreType.DMA((n,)), pltpu.SemaphoreType.DMA((n,))],
        compiler_params=pltpu.CompilerParams(has_side_effects=True),
    )(*arrs)


_HBM = pl.BlockSpec(memory_space=pltpu.HBM)
_SEM = pl.BlockSpec(memory_space=pltpu.SEMAPHORE)
_DATAFLOW = pltpu.SideEffectType.DATAFLOW_SIDE_EFFECTING


def _split_start(srcs, groups, gather, name):
    n = len(srcs)
    if gather:
        lands = [lax.empty((NDEV * a.shape[0], a.shape[1]), a.dtype) for a in srcs]
    else:
        lands = [lax.empty((NDEV, a.shape[0] // NDEV, a.shape[1]), a.dtype) for a in srcs]
    n_sem = 3 * len(groups)

    def body(*refs):
        src_refs, land_refs = refs[:n], refs[n:2 * n]
        sems = refs[2 * n:2 * n + n_sem]
        token = refs[-1]
        (_, my), sibling, chips = _places()
        _, _, peers = _peers()
        targets = [sibling] + chips if gather else peers
        for g, members in enumerate(groups):
            for j, k in enumerate(members):
                _own_copy(src_refs[k], land_refs[k], sems[3 * g + 2].at[j], my, gather).start()
        for g, members in enumerate(groups):
            for j, k in enumerate(members):
                rows = srcs[k].shape[0] if gather else srcs[k].shape[0] // NDEV
                for pos, idx in targets:
                    src = src_refs[k] if gather else src_refs[k].at[pl.ds(idx * rows, rows)]
                    dst = land_refs[k].at[pl.ds(my * rows, rows)] if gather else land_refs[k].at[my]
                    pltpu.make_async_remote_copy(src_ref=src, dst_ref=dst, send_sem=sems[3 * g].at[j],
                                                 recv_sem=sems[3 * g + 1].at[j], device_id=pos, device_id_type=MESH).start()
        token[...] = jnp.zeros_like(token)

    out_shape = []
    for members in groups:
        out_shape += 3 * [pltpu.SemaphoreType.DMA((len(members),))]
    out_shape += [pltpu.HBM(a.shape, a.dtype) for a in srcs] + [pltpu.HBM(a.shape, a.dtype) for a in lands]
    out_shape.append(jax.ShapeDtypeStruct((8, 128), F32))
    res = pl.pallas_call(
        body, name=name, out_shape=tuple(out_shape), in_specs=2 * n * [_HBM],
        out_specs=tuple(n_sem * [_SEM] + 2 * n * [_HBM] + [pl.BlockSpec(memory_space=pltpu.VMEM)]),
        input_output_aliases={i: n_sem + i for i in range(2 * n)},
        compiler_params=pltpu.CompilerParams(has_side_effects=_DATAFLOW),
    )(*[pltpu.with_memory_space_constraint(a, pltpu.HBM) for a in list(srcs) + lands])
    sems = [tuple(res[3 * g:3 * g + 3]) for g in range(len(groups))]
    return sems, list(res[n_sem:n_sem + n]), list(res[n_sem + n:n_sem + 2 * n]), res[-1]


def _own_copy(src_ref, land_ref, sem, my, gather):
    if gather:
        rows = src_ref.shape[0]
        return pltpu.make_async_copy(src_ref, land_ref.at[pl.ds(my * rows, rows)], sem)
    rows = src_ref.shape[0] // NDEV
    return pltpu.make_async_copy(src_ref.at[pl.ds(my * rows, rows)], land_ref.at[my], sem)


def _wait_all(land_ref, blocks_per_dev, copies, send_sem, recv_sem, me_pos):
    part = land_ref.at[pl.ds(0, copies * blocks_per_dev)]
    pltpu.make_async_remote_copy(src_ref=part, dst_ref=part, send_sem=send_sem, recv_sem=recv_sem,
                                 device_id=me_pos, device_id_type=MESH).wait()


def _gather_forward(sems, srcs, lands, after, name):
    n = len(srcs)

    def body(*refs):
        land_refs = refs[n:2 * n]
        send_a, recv_a = refs[2 * n], refs[2 * n + 1]
        send_b, recv_b = refs[2 * n + 3], refs[2 * n + 4]
        token = refs[-1]
        (me_pos, _), sibling, chips = _places()
        for j in range(n):
            _wait_all(land_refs[j], lands[j].shape[0] // NDEV, 1 + OTHER_CHIPS, send_a.at[j], recv_a.at[j], me_pos)
        for j in range(n):
            rows = lands[j].shape[0] // NDEV
            for _, idx in chips:
                block = land_refs[j].at[pl.ds(idx * rows, rows)]
                pltpu.make_async_remote_copy(src_ref=block, dst_ref=block, send_sem=send_b.at[j], recv_sem=recv_b.at[j],
                                             device_id=sibling[0], device_id_type=MESH).start()
        token[...] = jnp.zeros_like(token)

    res = pl.pallas_call(
        body, name=name,
        out_shape=(pltpu.SemaphoreType.DMA((n,)), pltpu.SemaphoreType.DMA((n,)))
        + tuple(pltpu.HBM(a.shape, a.dtype) for a in list(srcs) + list(lands)) + (jax.ShapeDtypeStruct((8, 128), F32),),
        in_specs=2 * n * [_HBM] + [_SEM, _SEM, pl.BlockSpec(memory_space=pl.ANY)],
        out_specs=tuple([_SEM, _SEM] + 2 * n * [_HBM] + [pl.BlockSpec(memory_space=pltpu.VMEM)]),
        input_output_aliases={i: 2 + i for i in range(2 * n)},
        compiler_params=pltpu.CompilerParams(has_side_effects=_DATAFLOW),
    )(*srcs, *lands, sems[0], sems[1], after)
    return (res[0], res[1]), list(res[2:2 + n]), list(res[2 + n:2 + 2 * n]), res[-1]


def _split_wait(sems, srcs, lands, after, copies, gather, name):
    n = len(srcs)

    def body(*refs):
        src_refs, land_refs = refs[:n], refs[n:2 * n]
        send_sem, recv_sem, local_sem = refs[2 * n], refs[2 * n + 1], refs[2 * n + 2]
        (me_pos, my), _, _ = _places()
        for j in range(n):
            _wait_all(land_refs[j], lands[j].shape[0] // NDEV, copies, send_sem.at[j], recv_sem.at[j], me_pos)
            _own_copy(src_refs[j], land_refs[j], local_sem.at[j], my, gather).wait()

    res = pl.pallas_call(
        body, name=name, out_shape=tuple(pltpu.HBM(a.shape, a.dtype) for a in list(srcs) + list(lands)),
        in_specs=2 * n * [_HBM] + [_SEM, _SEM, _SEM, pl.BlockSpec(memory_space=pl.ANY)], out_specs=tuple(2 * n * [_HBM]),
        input_output_aliases={i: i for i in range(2 * n)},
        compiler_params=pltpu.CompilerParams(has_side_effects=_DATAFLOW),
    )(*srcs, *lands, sems[0], sems[1], sems[2], after)
    return list(res[n:])


def _chained(gate, mid, after):
    return gate if mid is None else gate + mid(after)[:1, :1]


def _ffn_fwd(x, norms, mod, w, mid=None):
    (pre_g, post_g), (shift, scale, gate), (wg_t, wu_t, wd) = norms, mod, w
    hn, g, u, a = _ffn_up(x, pre_g, scale, shift, wg_t, wu_t, "ffn_up")
    if callable(wd):
        wd = wd(a)
    x_out, f = _mm_post(a, wd, x, post_g, _chained(gate, mid, a), FFN_RES, "ffn_down")
    return x_out, (x, hn, g, u, a, f), (wg_t, wu_t, wd)


def _ffn_bwd(dx_out, saved, norms, mod, w, send=None):
    (pre_g, post_g), (_, scale, gate), (wg_t, wu_t, wd) = norms, mod, w
    x, hn, g, u, a, f = saved
    d_model = x.shape[1]
    sent = (lambda j, dw: None) if send is None else send
    df, dgate, dpost = _post_bwd(dx_out, f, post_g, gate, FFN_RES, "ffn_post_bwd")
    dwd = _mm([(a, df)], "tn", BF16, 256, d_model, "ffn_dw")
    dg, du = _ffn_dgu(df, wd, g, u, "ffn_dgu", after=sent(2, dwd))
    dwg_t = _mm([(dg, hn)], "tn", BF16, 256, d_model, "ffn_dw")
    dwu_t = _mm([(du, hn)], "tn", BF16, 256, d_model, "ffn_dw", after=sent(0, dwg_t))
    dhn = _mm([(dg, wg_t), (du, wu_t)], "nn", F32, TOKEN_TILE, d_model, "ffn_dhn", after=sent(1, dwu_t))
    dx, dshift, dscale, dpre = _prenorm_bwd(dx_out, [dhn], x, pre_g, scale, "prenorm_bwd")
    return dx, (dpre, dpost), (dshift, dscale, dgate), (dwg_t, dwu_t, dwd)


def _mla_fwd(x, norms, mod, w, rope, mid=None):
    (pre_g, post_g), (shift, scale, gate) = norms, mod
    w_in, q_norm, wq_t, kv_norm, wkv_t, wo = w
    hn, lat = _prenorm_mm(x, pre_g, scale, shift, w_in, "nn", F32, LAT_PAD, "mla_in")
    gate = _chained(gate, mid, lat)
    q, k, v, qn, kvn = _mla_qkv(lat, q_norm, kv_norm, wq_t, wkv_t, rope, "mla_qkv")
    o = _mla_attn_fwd(q, k, v, "mla_attn_fwd")
    x_out, f = _mm_post(o, wo, x, post_g, gate, 1.0, "mla_out")
    return x_out, (x, hn, lat, q, k, v, qn, kvn, o, f)


def _mla_bwd(dx_out, saved, norms, mod, w, rope):
    (pre_g, post_g), (_, scale, gate) = norms, mod
    w_in, q_norm, wq_t, kv_norm, wkv_t, wo = w
    x, hn, lat, q, k, v, qn, kvn, o, f = saved
    d_model = x.shape[1]
    df, dgate, dpost = _post_bwd(dx_out, f, post_g, gate, 1.0, "mix_post_bwd")
    d_o = _mm([(df, wo)], "nt", F32, TOKEN_TILE, wo.shape[0], "mla_do")
    dwo = _mm([(o, df)], "tn", BF16, TOKEN_TILE, d_model, "mla_dwo")
    dq, dk, dv = _mla_attn_bwd(q, k, v, d_o, "mla_attn_bwd")
    dqp, dkv, dlat, dq_norm, dkv_norm = _mla_qkv_bwd(dq, dk, dv, lat, q_norm, kv_norm, wq_t, wkv_t, rope, "mla_qkv_bwd")
    dwq_t = _mm([(dqp, qn)], "tn", BF16, TOKEN_TILE, Q_LORA, "mla_dwq")
    dwkv_t = _mm([(dkv, kvn)], "tn", BF16, TOKEN_TILE, KV_LORA, "mla_dwkv")
    dw_in = _mm([(hn, dlat)], "tn", BF16, TOKEN_TILE, LAT_PAD, "mla_dwin")
    dhn = _mm([(dlat, w_in)], "nt", F32, TOKEN_TILE, d_model, "mla_dhn")
    dx, dshift, dscale, dpre = _prenorm_bwd(dx_out, [dhn], x, pre_g, scale, "prenorm_bwd")
    return dx, (dpre, dpost), (dshift, dscale, dgate), (dw_in, dq_norm, dwq_t, dkv_norm, dwkv_t, dwo)


def _dil_fwd(x, norms, mod, w, bias, mid=None):
    (pre_g, post_g), (shift, scale, gate), (w_in_t, wo) = norms, mod, w
    width = 3 * DIL_HEADS * DIL_HEAD_DIM
    hns, qkvs, outs, lses = [], [], [], []
    for g, (window, dilation) in enumerate(DIL_GROUPS):
        hn, qkv = _prenorm_mm(x, pre_g, scale, shift, w_in_t[g * width:(g + 1) * width], "nt", BF16, width,
                              "dil_in", perm=dilation)
        if g == 0:
            gate = _chained(gate, mid, qkv)
        o, lse = _dil_attn_fwd(qkv, bias[g], dilation, window // dilation, "dil_attn_fwd")
        hns.append(hn), qkvs.append(qkv), outs.append(o), lses.append(lse)
    alphas, o_mix, o_mix_b = _dil_mix(lses, outs, "dil_mix")
    x_out, f = _mm_post(o_mix_b, wo, x, post_g, gate, 1.0, "dil_out")
    return x_out, (x, hns, qkvs, lses, alphas, o_mix, o_mix_b, f)


def _dil_bwd(dx_out, saved, norms, mod, w, bias):
    (pre_g, post_g), (_, scale, gate), (w_in_t, wo) = norms, mod, w
    x, hns, qkvs, lses, alphas, o_mix, o_mix_b, f = saved
    d_model = x.shape[1]
    inner = DIL_HEADS * DIL_HEAD_DIM
    df, dgate, dpost = _post_bwd(dx_out, f, post_g, gate, 1.0, "mix_post_bwd")
    d_o = _mm([(df, wo)], "nt", F32, TOKEN_TILE, inner, "dil_do")
    dwo = _mm([(o_mix_b, df)], "tn", BF16, TOKEN_TILE, d_model, "dil_dwo")
    dhns, dws, dbs = [], [], []
    for g, (window, dilation) in enumerate(DIL_GROUPS):
        grads = _dil_attn_bwd(qkvs[g], bias[g], d_o, o_mix, alphas[g], lses[g], dilation, window // dilation, "dil_attn_bwd")
        dbs.append(grads[3])
        w_parts = [w_in_t[(3 * g + j) * inner:(3 * g + j + 1) * inner] for j in range(3)]
        dhns.append(_mm(list(zip(grads[:3], w_parts)), "nn", F32, TOKEN_TILE, d_model, "dil_dhn", out_perm=dilation))
        dws += [_mm([(grads[j], hns[g])], "tn", BF16, TOKEN_TILE, d_model, "dil_dwin") for j in range(3)]
    dx, dshift, dscale, dpre = _prenorm_bwd(dx_out, dhns, x, pre_g, scale, "prenorm_bwd3")
    return dx, (dpre, dpost), (dshift, dscale, dgate), (jnp.concatenate(dws, axis=0), dwo), jnp.concatenate(dbs, axis=0)


def _pad_rows(a, rows):
    return jnp.pad(a, ((0, rows - a.shape[0]), (0, 0)))


def _lanes(a):
    flat = a.reshape(-1).astype(F32)
    rows = -(-flat.shape[0] // 1024) * 8
    return jnp.pad(flat, (0, rows * 128 - flat.shape[0])).reshape(rows, 128)


def kernel(x, c, norm_pre, norm_post, w_mod, b_mod, ffn_w_gate, ffn_w_up, ffn_w_down, mla_w_in, mla_q_norm, mla_w_q_up, mla_kv_norm, mla_w_kv_up, mla_w_o, dil_w_in, dil_w_o, rel_bias, loss_target, m_norm_pre, m_norm_post, m_w_mod, m_b_mod, m_ffn_w_gate, m_ffn_w_up, m_ffn_w_down, m_mla_w_in, m_mla_q_norm, m_mla_w_q_up, m_mla_kv_norm, m_mla_w_kv_up, m_mla_w_o, m_dil_w_in, m_dil_w_o, m_rel_bias, v_norm_pre, v_norm_post, v_w_mod, v_b_mod, v_ffn_w_gate, v_ffn_w_up, v_ffn_w_down, v_mla_w_in, v_mla_q_norm, v_mla_w_q_up, v_mla_kv_norm, v_mla_w_kv_up, v_mla_w_o, v_dil_w_in, v_dil_w_o, v_rel_bias):
    me = 4 * lax.axis_index("x") + 2 * lax.axis_index("y") + lax.axis_index("c")
    depth, n_sub, d_loc = norm_pre.shape
    d_model = x.shape[2]
    mod_loc_cols = w_mod.shape[2]
    x0, target = x[0], loss_target[0]

    bf_t = lambda a: a.astype(BF16).T
    ffn_ids = [(i, h) for i in range(depth) for h in range(2)]
    shards = []
    for i, h in ffn_ids:
        shards += [bf_t(ffn_w_gate[i, h]), bf_t(ffn_w_up[i, h]), ffn_w_down[i, h].astype(BF16)]
    shards += [mla_w_in[0].astype(BF16), bf_t(mla_w_q_up[0]), bf_t(mla_w_kv_up[0]), mla_w_o[0].astype(BF16),
               bf_t(dil_w_in[0]), dil_w_o[0].astype(BF16)]
    n_ffn = 3 * len(ffn_ids)
    members = {(0, 0): [0, 1, 2], (0, 1): [n_ffn, n_ffn + 1, n_ffn + 2, n_ffn + 3], (0, 2): [3, 4, 5],
               (1, 0): [6, 7, 8], (1, 1): [n_ffn + 4, n_ffn + 5], (1, 2): [9, 10, 11]}
    order = [(i, s) for i in range(depth) for s in range(n_sub)]

    small = jnp.concatenate([c.reshape(8, 128), _pad_rows(norm_pre.reshape(depth * n_sub, d_loc), 8),
                             _pad_rows(norm_post.reshape(depth * n_sub, d_loc), 8)], axis=0)
    small_all = _exchange([small], True, "gather_small")[0].reshape(NDEV, 24, 128)
    c_all = small_all[:, 0:8].reshape(NDEV, d_model)
    gains = lambda lo: jnp.transpose(small_all[:, lo:lo + depth * n_sub], (1, 0, 2)).reshape(depth, n_sub, 1, d_model)
    pre_full, post_full = gains(8), gains(16)

    b_loc = lax.dynamic_slice(b_mod, (0, me * mod_loc_cols), (depth, mod_loc_cols))
    mod_cols, silu_c = _mod_fwd(c_all, w_mod, b_loc, "mod_fwd")
    mod_all = _exchange([mod_cols.reshape(depth * NDEV, mod_loc_cols)], True, "gather_mod")[0]
    mod_all = mod_all.reshape(NDEV, depth, NDEV, mod_loc_cols)
    mod_mine = lax.dynamic_index_in_dim(mod_all, me, axis=2, keepdims=False)
    mod = jnp.transpose(mod_mine, (1, 0, 2)).reshape(depth, n_sub, 3, 1, d_model)

    shards[0], _ = lax.optimization_barrier((shards[0], mod_all))
    first = order[0]
    stages = [("%d%d" % first, members[first][:2]), ("%d%dd" % first, members[first][2:])]
    stages += [("%d%d" % key, members[key]) for key in order[1:]]
    stage_names = [name for name, _ in stages]
    g_sems, g_srcs, g_lands, _ = _split_start(shards, [idx for _, idx in stages], True, "gather_weights_start")

    forwarded = {}

    def forward(stage, after):
        idx = stages[stage_names.index(stage)][1]
        forwarded[stage] = _gather_forward(g_sems[stage_names.index(stage)], [g_srcs[k] for k in idx],
                                           [g_lands[k] for k in idx], after, "gather_forward_" + stage)
        return forwarded[stage][3]

    def weights_of(stage, after):
        (send_b, recv_b), srcs, lands, _ = forwarded[stage]
        local = g_sems[stage_names.index(stage)][2]
        return _split_wait((send_b, recv_b, local), srcs, lands, after, OTHER_CHIPS, True, "gather_wait_" + stage)

    def late_down(after):
        forward("%d%dd" % first, after)
        return weights_of("%d%dd" % first, after)[0]

    lat_real = Q_LORA + KV_LORA
    qk = QK_NOPE + QK_ROPE

    def mla_weights(after):
        w_in, wq_t, wkv_t, wo = weights_of("01", after)
        w_in_pad = jnp.concatenate([w_in[:, :lat_real], jnp.zeros((d_model, QK_NOPE), BF16), w_in[:, lat_real:],
                                    jnp.zeros((d_model, HEAD_PAD - QK_NOPE - QK_ROPE), BF16)], axis=1)
        wq_pad = jnp.pad(wq_t.reshape(MLA_HEADS, qk, Q_LORA), ((0, 0), (0, HEAD_PAD - qk), (0, 0)))
        wo_pad = jnp.pad(wo.reshape(MLA_HEADS, V_HEAD, d_model), ((0, 0), (HEAD_PAD - V_HEAD, 0), (0, 0)))
        return (w_in_pad, mla_q_norm, wq_pad.reshape(MLA_HEADS * HEAD_PAD, Q_LORA), mla_kv_norm, wkv_t,
                wo_pad.reshape(MLA_HEADS * HEAD_PAD, d_model))

    rope = _rope_tables()
    buckets = jnp.stack([_dil_buckets(dil) for _, dil in DIL_GROUPS])
    onehot = (buckets[..., None] == jnp.arange(N_BUCKETS)).astype(F32)
    bias = jnp.einsum("gqkb,bgh->ghqk", onehot, rel_bias.reshape(N_BUCKETS, len(DIL_GROUPS), DIL_HEADS),
                      precision=lax.Precision.HIGHEST)

    norms = lambda i, s: (pre_full[i, s], post_full[i, s])
    mods = lambda i, s: (mod[i, s, 0], mod[i, s, 1], mod[i, s, 2])
    saved, weights = {}, {}
    h = lax.optimization_barrier((x0, bias, buckets, *rope))[0]
    forward("%d%d" % first, h)
    for n, (i, s) in enumerate(order):
        got = mla_weights(h) if (s == 1 and i % 2 == 0) else tuple(weights_of("%d%d" % (i, s), h))
        mid = None if n + 1 == len(order) else (lambda after, nxt="%d%d" % order[n + 1]: forward(nxt, after))
        if s != 1:
            got = got if len(got) == 3 else (*got, late_down)
            h, saved[i, s], weights[i, s] = _ffn_fwd(h, norms(i, s), mods(i, s), got, mid)
            continue
        weights[i, s] = got
        if i % 2 == 0:
            h, saved[i, s] = _mla_fwd(h, norms(i, s), mods(i, s), weights[i, s], rope, mid)
        else:
            h, saved[i, s] = _dil_fwd(h, norms(i, s), mods(i, s), weights[i, s], bias, mid)
    dh, loss_parts = _loss_grad(h, target, "loss")

    dnorm, dmod, sent = {}, {}, {}
    token = jnp.zeros((8, 128), F32)
    last = order[0]

    def send_last(j, dw):
        sent[last, j] = _split_start([dw], [[0]], False, "scatter_start_%d%d_%d" % (*last, j))
        return sent[last, j][3]

    for i, s in reversed(order):
        md = mods(i, s)
        md = (md[0], md[1], md[2] + token[:1, :1])
        if (i, s) == last:
            dh, dnorm[i, s], dmod[i, s], _ = _ffn_bwd(dh, saved[i, s], norms(i, s), md, weights[i, s], send_last)
            continue
        if s != 1:
            dh, dnorm[i, s], dmod[i, s], dws = _ffn_bwd(dh, saved[i, s], norms(i, s), md, weights[i, s])
        elif i % 2 == 0:
            dh, dnorm[i, s], dmod[i, s], dmla = _mla_bwd(dh, saved[i, s], norms(i, s), md, weights[i, s], rope)
            dw_in_pad, dq_norm, dwq_pad, dkv_norm, dwkv_t, dwo_pad = dmla
            dw_in = jnp.concatenate([dw_in_pad[:, :lat_real], dw_in_pad[:, lat_real + QK_NOPE:lat_real + qk]], axis=1)
            dwq_t = dwq_pad.reshape(MLA_HEADS, HEAD_PAD, Q_LORA)[:, :qk].reshape(MLA_HEADS * qk, Q_LORA)
            dwo = dwo_pad.reshape(MLA_HEADS, HEAD_PAD, d_model)[:, HEAD_PAD - V_HEAD:].reshape(MLA_HEADS * V_HEAD, d_model)
            dws = (dw_in, dwq_t, dwkv_t, dwo)
        else:
            dh, dnorm[i, s], dmod[i, s], dws, dbias = _dil_bwd(dh, saved[i, s], norms(i, s), md, weights[i, s], bias)
        sent[i, s] = _split_start(list(dws), [list(range(len(dws)))], False, "scatter_start_%d%d" % (i, s))
        token = sent[i, s][3]
    grad_x = dh[None]

    mine = {}
    for key in order[1:]:
        sems, srcs, lands, _ = sent[key]
        parts = _split_wait(sems[0], srcs, lands, dh, NDEV - 1, False, "scatter_wait_%d%d" % key)
        for k, p in zip(members[key], parts):
            mine[k] = _sum_parts(p, "sum_parts")
    for j in (2, 0, 1):
        sems, srcs, lands, _ = sent[last, j]
        parts = _split_wait(sems[0], srcs, lands, dh, NDEV - 1, False, "scatter_wait_%d%d_%d" % (*last, j))
        mine[members[last][j]] = _sum_parts(parts[0], "sum_parts")
    g_gate = jnp.stack([mine[3 * n].T for n in range(len(ffn_ids))]).reshape(ffn_w_gate.shape)
    g_up = jnp.stack([mine[3 * n + 1].T for n in range(len(ffn_ids))]).reshape(ffn_w_up.shape)
    g_down = jnp.stack([mine[3 * n + 2] for n in range(len(ffn_ids))]).reshape(ffn_w_down.shape)
    g_mla_in, g_q_up, g_kv_up, g_mla_o, g_dil_in, g_dil_o = (mine[k] for k in range(n_ffn, n_ffn + 6))
    g_mla_in, g_q_up, g_kv_up, g_mla_o = g_mla_in[None], g_q_up.T[None], g_kv_up.T[None], g_mla_o[None]
    g_dil_in, g_dil_o = g_dil_in.T[None], g_dil_o[None]

    dmod_mine = jnp.concatenate([jnp.concatenate(dmod[i, s], axis=0) for i in range(depth) for s in range(n_sub)], axis=0)
    dpre_mine = jnp.concatenate([dnorm[i, s][0] for i in range(depth) for s in range(n_sub)], axis=0)
    dpost_mine = jnp.concatenate([dnorm[i, s][1] for i in range(depth) for s in range(n_sub)], axis=0)
    dbias_tab = _bias_reduce(dbias, buckets, "bias_reduce")[:, 0, :N_BUCKETS].T
    pieces = [dmod_mine, dpre_mine, dpost_mine, dq_norm, dkv_norm, dbias_tab, jnp.sum(loss_parts).reshape(1, 1)]
    packed = [_lanes(p) for p in pieces]
    offs = [0]
    for p in packed:
        offs.append(offs[-1] + p.shape[0])
    everyone = _exchange([jnp.concatenate(packed, axis=0)], True, "gather_small_grads")[0].reshape(NDEV, offs[-1], 128)
    total = _sum_parts(everyone, "sum_small")
    take = lambda n, shape: total[offs[n]:offs[n + 1]].reshape(-1)[:math.prod(shape)].reshape(shape)
    g_b_mod = take(0, b_mod.shape)
    col0 = me * d_loc
    g_norm_pre = lax.dynamic_slice(take(1, (depth, n_sub, d_model)), (0, 0, col0), norm_pre.shape)
    g_norm_post = lax.dynamic_slice(take(2, (depth, n_sub, d_model)), (0, 0, col0), norm_post.shape)
    g_q_norm, g_kv_norm = take(3, mla_q_norm.shape), take(4, mla_kv_norm.shape)
    g_rel_bias = take(5, rel_bias.shape)
    loss = take(6, ())

    dmod_all = everyone[:, offs[0]:offs[1]].reshape(NDEV, depth, NDEV * mod_loc_cols)
    dmod_cols = lax.dynamic_slice(dmod_all, (0, 0, me * mod_loc_cols), (NDEV, depth, mod_loc_cols))
    silu_t = jnp.pad(silu_c.T, ((0, 0), (0, HEAD_PAD - NDEV)))
    g_w_mod = jnp.stack([_mm([(silu_t, jnp.pad(dmod_cols[:, i], ((0, HEAD_PAD - NDEV), (0, 0))))], "nn", F32, TOKEN_TILE,
                             mod_loc_cols, "mod_bwd") for i in range(depth)])

    ws = (norm_pre, norm_post, w_mod, b_mod, ffn_w_gate, ffn_w_up, ffn_w_down, mla_w_in, mla_q_norm, mla_w_q_up, mla_kv_norm,
          mla_w_kv_up, mla_w_o, dil_w_in, dil_w_o, rel_bias)
    gs = (g_norm_pre, g_norm_post, g_w_mod, g_b_mod, g_gate, g_up, g_down, g_mla_in, g_q_norm, g_q_up, g_kv_norm, g_kv_up,
          g_mla_o, g_dil_in, g_dil_o, g_rel_bias)
    ms = (m_norm_pre, m_norm_post, m_w_mod, m_b_mod, m_ffn_w_gate, m_ffn_w_up, m_ffn_w_down, m_mla_w_in, m_mla_q_norm,
          m_mla_w_q_up, m_mla_kv_norm, m_mla_w_kv_up, m_mla_w_o, m_dil_w_in, m_dil_w_o, m_rel_bias)
    vs = (v_norm_pre, v_norm_post, v_w_mod, v_b_mod, v_ffn_w_gate, v_ffn_w_up, v_ffn_w_down, v_mla_w_in, v_mla_q_norm,
          v_mla_w_q_up, v_mla_kv_norm, v_mla_w_kv_up, v_mla_w_o, v_dil_w_in, v_dil_w_o, v_rel_bias)
    stepped = [_adamw(w, g, m, v, "adamw") for w, g, m, v in zip(ws, gs, ms, vs)]
    deltas, new_m, new_v = zip(*stepped)
    return (loss, grad_x, *gs, *deltas, *new_m, *new_v)
```

```python
import math

import jax
import jax.numpy as jnp
from jax import lax
from jax.experimental import pallas as pl
from jax.experimental.pallas import tpu as pltpu

F32 = jnp.float32
BF16 = jnp.bfloat16
MESH = pl.DeviceIdType.MESH

NDEV = 8
OTHER_CHIPS = 3
D_MODEL = 1024
SEQ = 2048
D_FF = 2816
EPS = 1e-6
FFN_RES = 0.5

MLA_HEADS = 16
Q_LORA = 384
KV_LORA = 256
QK_NOPE = 64
QK_ROPE = 32
V_HEAD = 64
ROPE_THETA = 10000.0
HEAD_PAD = 128
LAT_PAD = Q_LORA + KV_LORA + HEAD_PAD
MLA_SCALE = (QK_NOPE + QK_ROPE) ** -0.5

DIL_GROUPS = ((128, 1), (512, 4), (2048, 16))
DIL_HEADS = 16
DIL_HEAD_DIM = 64
DIL_BLOCK = 128
DIL_PAIRS = DIL_HEADS // 2
DIL_SCALE = DIL_HEAD_DIM ** -0.5
N_BUCKETS = 32
MAX_DISTANCE = 2048

ADAM_LR = 0.001
ADAM_B1 = 0.9
ADAM_B2 = 0.999
ADAM_EPS = 1e-08
ADAM_WD = 0.01
ADAM_STEP = 10

V7X_VMEM_BYTES = 64 * 2**20
VMEM_RESERVE = 10 * 2**20
TOKEN_TILE = 512


def _nbytes(shape, dtype):
    return math.prod(shape) * jnp.dtype(dtype).itemsize


def _params(semantics, blocks, extra=0):
    need = 2 * sum(_nbytes(s, d) for s, d in blocks) + extra + VMEM_RESERVE
    return pltpu.CompilerParams(dimension_semantics=semantics,
                                vmem_limit_bytes=int(min(need, V7X_VMEM_BYTES - VMEM_RESERVE)))


def _pcall(body, out_shape, **kw):
    call = pl.pallas_call(body, out_shape=jax.tree.map(lambda s: pltpu.HBM(s.shape, s.dtype), out_shape), **kw)
    return lambda *args: call(*[pltpu.with_memory_space_constraint(a, pltpu.HBM) for a in args])


def _dot_nn(a, b):
    return lax.dot_general(a, b, (((1,), (0,)), ((), ())), preferred_element_type=F32)


def _dot_nt(a, b):
    return lax.dot_general(a, b, (((1,), (1,)), ((), ())), preferred_element_type=F32)


def _dot_tn(a, b):
    return lax.dot_general(a, b, (((0,), (0,)), ((), ())), preferred_element_type=F32)


_DOTS = {"nn": _dot_nn, "nt": _dot_nt, "tn": _dot_tn}


def _rstd(v):
    return lax.rsqrt(jnp.mean(v * v, axis=-1, keepdims=True) + EPS)


def _rms_bwd(v, r, t):
    return r * t - v * (r * r * r) * jnp.mean(t * v, axis=-1, keepdims=True)


_TOKEN_SPEC = pl.BlockSpec((8, 128), lambda *_: (0, 0))


def _mm(pairs, mode, out_dtype, tm, tn, name, out_perm=1, after=None):
    a0, b0 = pairs[0]
    m_dim = a0.shape[1] if mode == "tn" else a0.shape[0]
    n_dim = b0.shape[0] if mode == "nt" else b0.shape[1]
    tm, tn = min(tm, m_dim // out_perm), min(tn, n_dim)
    assert m_dim % tm == 0 and n_dim % tn == 0, (name, m_dim, n_dim, tm, tn)
    dot = _DOTS[mode]
    npairs = len(pairs)

    def body(*refs):
        acc = None
        for p in range(npairs):
            d = dot(refs[2 * p][...].astype(BF16), refs[2 * p + 1][...].astype(BF16))
            acc = d if acc is None else acc + d
        refs[-1][...] = acc.astype(out_dtype)

    in_specs, blocks, flat = [], [], []
    for a, b in pairs:
        if mode == "nn":
            k = a.shape[1]
            sa, sb = ((tm, k), lambda i, j: (i, 0)), ((k, tn), lambda i, j: (0, j))
        elif mode == "nt":
            k = a.shape[1]
            sa, sb = ((tm, k), lambda i, j: (i, 0)), ((tn, k), lambda i, j: (j, 0))
        else:
            k = a.shape[0]
            sa, sb = ((k, tm), lambda i, j: (0, i)), ((k, tn), lambda i, j: (0, j))
        in_specs += [pl.BlockSpec(*sa), pl.BlockSpec(*sb)]
        blocks += [(sa[0], a.dtype), (sb[0], b.dtype)]
        flat += [a, b]
    if after is not None:
        in_specs.append(_TOKEN_SPEC)
        flat.append(after)
    if out_perm == 1:
        out_shape = (m_dim, n_dim)
        out_spec = pl.BlockSpec((tm, tn), lambda i, j: (i, j))
    else:
        rows = m_dim // out_perm
        assert tn == n_dim and rows % tm == 0, (name, rows, tm)
        nb = rows // tm
        out_shape = (rows, out_perm * n_dim)
        out_spec = pl.BlockSpec((tm, n_dim), lambda i, j: (i % nb, i // nb))
    blocks.append(((tm, tn), out_dtype))
    res = _pcall(
        body, out_shape=jax.ShapeDtypeStruct(out_shape, out_dtype), grid=(m_dim // tm, n_dim // tn),
        in_specs=in_specs, out_specs=out_spec, name=name,
        compiler_params=_params(("parallel", "parallel"), blocks, extra=2 * tm * tn * 4),
    )(*flat)
    return res.reshape(m_dim, n_dim)


def _prenorm_mm(x, pre_g, scale, shift, w, w_mode, out_dtype, tn, name, perm=1):
    s_dim, d_dim = x.shape
    n_dim = w.shape[0] if w_mode == "nt" else w.shape[1]
    rows = s_dim // perm
    tm = min(TOKEN_TILE, rows)
    nb = rows // tm
    tn = min(tn, n_dim)
    assert n_dim % tn == 0
    dot = _DOTS[w_mode]

    def body(x_ref, g_ref, sc_ref, sh_ref, w_ref, hn_ref, o_ref):
        @pl.when(pl.program_id(1) == 0)
        def _():
            xf = x_ref[...]
            hn = (xf * _rstd(xf) * g_ref[...]) * (1.0 + sc_ref[...]) + sh_ref[...]
            hn_ref[...] = hn.astype(BF16)

        o_ref[...] = dot(hn_ref[...], w_ref[...]).astype(out_dtype)

    vec = pl.BlockSpec((1, d_dim), lambda i, j: (0, 0))
    w_block = (tn, d_dim) if w_mode == "nt" else (d_dim, tn)
    w_spec = pl.BlockSpec(w_block, (lambda i, j: (j, 0)) if w_mode == "nt" else (lambda i, j: (0, j)))
    hn, out = _pcall(
        body,
        out_shape=(jax.ShapeDtypeStruct((s_dim, d_dim), BF16), jax.ShapeDtypeStruct((s_dim, n_dim), out_dtype)),
        grid=(s_dim // tm, n_dim // tn),
        in_specs=[pl.BlockSpec((tm, d_dim), lambda i, j: (i % nb, i // nb)), vec, vec, vec, w_spec],
        out_specs=(pl.BlockSpec((tm, d_dim), lambda i, j: (i, 0)), pl.BlockSpec((tm, tn), lambda i, j: (i, j))),
        name=name,
        compiler_params=_params(("parallel", "arbitrary"),
                                [((tm, d_dim), F32), (w_block, BF16), ((tm, d_dim), BF16), ((tm, tn), out_dtype)],
                                extra=3 * tm * d_dim * 4 + tm * tn * 4),
    )(x.reshape(rows, perm * d_dim), pre_g, scale, shift, w)
    return hn, out


def _ffn_up(x, pre_g, scale, shift, wg_t, wu_t, name):
    s_dim, d_dim = x.shape
    f_dim = wg_t.shape[0]
    tm, tn = TOKEN_TILE, f_dim // 2

    def body(x_ref, g_ref, sc_ref, sh_ref, wg_ref, wu_ref, hn_ref, go_ref, uo_ref, a_ref):
        @pl.when(pl.program_id(1) == 0)
        def _():
            xf = x_ref[...]
            hn = (xf * _rstd(xf) * g_ref[...]) * (1.0 + sc_ref[...]) + sh_ref[...]
            hn_ref[...] = hn.astype(BF16)

        hn = hn_ref[...]
        g = _dot_nt(hn, wg_ref[...])
        u = _dot_nt(hn, wu_ref[...])
        go_ref[...] = g.astype(BF16)
        uo_ref[...] = u.astype(BF16)
        a_ref[...] = (g * jax.nn.sigmoid(g) * u).astype(BF16)

    vec = pl.BlockSpec((1, d_dim), lambda i, j: (0, 0))
    w_spec = pl.BlockSpec((tn, d_dim), lambda i, j: (j, 0))
    act = pl.BlockSpec((tm, tn), lambda i, j: (i, j))
    act_shape = jax.ShapeDtypeStruct((s_dim, f_dim), BF16)
    return _pcall(
        body,
        out_shape=(jax.ShapeDtypeStruct((s_dim, d_dim), BF16), act_shape, act_shape, act_shape),
        grid=(s_dim // tm, f_dim // tn),
        in_specs=[pl.BlockSpec((tm, d_dim), lambda i, j: (i, 0)), vec, vec, vec, w_spec, w_spec],
        out_specs=(pl.BlockSpec((tm, d_dim), lambda i, j: (i, 0)), act, act, act),
        name=name,
        compiler_params=_params(("parallel", "arbitrary"),
                                [((tm, d_dim), F32), ((tn, d_dim), BF16), ((tn, d_dim), BF16), ((tm, d_dim), BF16)]
                                + 3 * [((tm, tn), BF16)], extra=3 * tm * d_dim * 4 + 4 * tm * tn * 4),
    )(x, pre_g, scale, shift, wg_t, wu_t)


def _mm_post(a, w, x, post_g, gate, res_w, name):
    s_dim, k_dim = a.shape
    d_dim = w.shape[1]
    tm = TOKEN_TILE

    def body(a_ref, w_ref, x_ref, pg_ref, gt_ref, xo_ref, f_ref):
        f = _dot_nn(a_ref[...], w_ref[...])
        y = f * _rstd(f) * pg_ref[...]
        f_ref[...] = f
        xo_ref[...] = x_ref[...] + (res_w * gt_ref[...]) * y

    vec = pl.BlockSpec((1, d_dim), lambda i: (0, 0))
    row = pl.BlockSpec((tm, d_dim), lambda i: (i, 0))
    out = jax.ShapeDtypeStruct((s_dim, d_dim), F32)
    return _pcall(
        body, out_shape=(out, out), grid=(s_dim // tm,),
        in_specs=[pl.BlockSpec((tm, k_dim), lambda i: (i, 0)), pl.BlockSpec((k_dim, d_dim), lambda i: (0, 0)), row, vec, vec],
        out_specs=(row, row), name=name,
        compiler_params=_params(("parallel",), [((tm, k_dim), BF16), ((k_dim, d_dim), BF16)] + 3 * [((tm, d_dim), F32)],
                                extra=3 * tm * d_dim * 4),
    )(a, w, x, post_g, gate)


def _post_bwd(dx_out, f, post_g, gate, res_w, name):
    s_dim, d_dim = f.shape
    tm = TOKEN_TILE

    def body(dx_ref, f_ref, pg_ref, gt_ref, df_ref, dgate_ref, dpost_ref):
        @pl.when(pl.program_id(0) == 0)
        def _():
            dgate_ref[...] = jnp.zeros_like(dgate_ref)
            dpost_ref[...] = jnp.zeros_like(dpost_ref)

        dx, fv = dx_ref[...], f_ref[...]
        r = _rstd(fv)
        fr = fv * r
        dgate_ref[...] += res_w * jnp.sum(dx * (fr * pg_ref[...]), axis=0, keepdims=True)
        dy = (res_w * gt_ref[...]) * dx
        dpost_ref[...] += jnp.sum(dy * fr, axis=0, keepdims=True)
        df_ref[...] = _rms_bwd(fv, r, dy * pg_ref[...]).astype(BF16)

    vec = pl.BlockSpec((1, d_dim), lambda i: (0, 0))
    row = pl.BlockSpec((tm, d_dim), lambda i: (i, 0))
    vshape = jax.ShapeDtypeStruct((1, d_dim), F32)
    return _pcall(
        body, out_shape=(jax.ShapeDtypeStruct((s_dim, d_dim), BF16), vshape, vshape), grid=(s_dim // tm,),
        in_specs=[row, row, vec, vec], out_specs=(row, vec, vec), name=name,
        compiler_params=_params(("arbitrary",), 3 * [((tm, d_dim), F32)], extra=6 * tm * d_dim * 4),
    )(dx_out, f, post_g, gate)


def _prenorm_bwd(dx_out, dhns, x, pre_g, scale, name):
    s_dim, d_dim = x.shape
    tm = TOKEN_TILE
    n_in = len(dhns)

    def body(*refs):
        dx_ref, x_ref, pg_ref, sc_ref = refs[n_in + 0], refs[n_in + 1], refs[n_in + 2], refs[n_in + 3]
        dxo_ref, dsh_ref, dsc_ref, dpg_ref = refs[n_in + 4:]

        @pl.when(pl.program_id(0) == 0)
        def _():
            dsh_ref[...] = jnp.zeros_like(dsh_ref)
            dsc_ref[...] = jnp.zeros_like(dsc_ref)
            dpg_ref[...] = jnp.zeros_like(dpg_ref)

        dhn = refs[0][...]
        for k in range(1, n_in):
            dhn = dhn + refs[k][...]
        xv = x_ref[...]
        r = _rstd(xv)
        xr = xv * r
        dsh_ref[...] += jnp.sum(dhn, axis=0, keepdims=True)
        dsc_ref[...] += jnp.sum(dhn * (xr * pg_ref[...]), axis=0, keepdims=True)
        dn = dhn * (1.0 + sc_ref[...])
        dpg_ref[...] += jnp.sum(dn * xr, axis=0, keepdims=True)
        dxo_ref[...] = dx_ref[...] + _rms_bwd(xv, r, dn * pg_ref[...])

    vec = pl.BlockSpec((1, d_dim), lambda i: (0, 0))
    row = pl.BlockSpec((tm, d_dim), lambda i: (i, 0))
    vshape = jax.ShapeDtypeStruct((1, d_dim), F32)
    return _pcall(
        body, out_shape=(jax.ShapeDtypeStruct((s_dim, d_dim), F32), vshape, vshape, vshape), grid=(s_dim // tm,),
        in_specs=n_in * [row] + [row, row, vec, vec], out_specs=(row, vec, vec, vec), name=name,
        compiler_params=_params(("arbitrary",), (n_in + 3) * [((tm, d_dim), F32)], extra=6 * tm * d_dim * 4),
    )(*dhns, dx_out, x, pre_g, scale)


def _ffn_dgu(df, wd, g, u, name, after=None):
    s_dim, d_dim = df.shape
    f_dim = wd.shape[0]
    tm, tn = TOKEN_TILE, f_dim // 2

    def body(df_ref, wd_ref, g_ref, u_ref, *rest):
        dg_ref, du_ref = rest[-2:]
        da = _dot_nt(df_ref[...], wd_ref[...])
        gv, uv = g_ref[...].astype(F32), u_ref[...].astype(F32)
        sg = jax.nn.sigmoid(gv)
        du_ref[...] = (da * (gv * sg)).astype(BF16)
        dg_ref[...] = (da * uv * (sg * (1.0 + gv * (1.0 - sg)))).astype(BF16)

    act = pl.BlockSpec((tm, tn), lambda i, j: (i, j))
    act_shape = jax.ShapeDtypeStruct((s_dim, f_dim), BF16)
    token = [] if after is None else [after]
    return _pcall(
        body, out_shape=(act_shape, act_shape), grid=(s_dim // tm, f_dim // tn),
        in_specs=[pl.BlockSpec((tm, d_dim), lambda i, j: (i, 0)), pl.BlockSpec((tn, d_dim), lambda i, j: (j, 0)), act, act]
        + len(token) * [_TOKEN_SPEC],
        out_specs=(act, act), name=name,
        compiler_params=_params(("parallel", "parallel"), [((tm, d_dim), BF16), ((tn, d_dim), BF16)] + 4 * [((tm, tn), BF16)],
                                extra=6 * tm * tn * 4),
    )(df, wd, g, u, *token)


def _rope_tables(zero=0.0):
    half = QK_ROPE // 2
    freqs = ROPE_THETA ** (-jnp.arange(half, dtype=F32) / half)
    ang = (jnp.arange(SEQ, dtype=F32)[:, None] + zero) * freqs[None, :]
    cos, sin = jnp.cos(ang), jnp.sin(ang)
    ones = jnp.ones((SEQ, QK_NOPE), F32)
    zeros = jnp.zeros((SEQ, QK_NOPE), F32)
    pad1 = jnp.ones((SEQ, HEAD_PAD - QK_NOPE - QK_ROPE), F32)
    pad0 = jnp.zeros((SEQ, HEAD_PAD - QK_NOPE - QK_ROPE), F32)
    zh = jnp.zeros((SEQ, half), F32)
    c = jnp.concatenate([ones, cos, cos, pad1], axis=1)
    s1 = jnp.concatenate([zeros, -sin, zh, pad0], axis=1)
    s2 = jnp.concatenate([zeros, zh, sin, pad0], axis=1)
    return c, s1, s2


def _rope(v, c, s1, s2):
    half = QK_ROPE // 2
    return v * c + pltpu.roll(v, HEAD_PAD - half, 1) * s1 + pltpu.roll(v, half, 1) * s2


def _rope_t(dv, c, s1, s2):
    half = QK_ROPE // 2
    return dv * c + pltpu.roll(dv * s1, half, 1) + pltpu.roll(dv * s2, HEAD_PAD - half, 1)


def _mla_qkv(lat, q_norm, kv_norm, wq_t, wkv_t, rope, name):
    s_dim = lat.shape[0]
    width = MLA_HEADS * HEAD_PAD
    tm = 256

    def body(lat_ref, qg_ref, kg_ref, wq_ref, wkv_ref, c_ref, s1_ref, s2_ref, q_ref, k_ref, v_ref, qn_ref, kvn_ref):
        cq = lat_ref[:, :Q_LORA]
        ckv = lat_ref[:, Q_LORA:Q_LORA + KV_LORA]
        kr = lat_ref[:, Q_LORA + KV_LORA:]
        c, s1, s2 = c_ref[...], s1_ref[...], s2_ref[...]
        qn = (cq * _rstd(cq) * qg_ref[...]).astype(BF16)
        kvn = (ckv * _rstd(ckv) * kg_ref[...]).astype(BF16)
        qn_ref[...] = qn
        kvn_ref[...] = kvn
        q = _dot_nt(qn, wq_ref[...])
        kv = _dot_nt(kvn, wkv_ref[...])
        krr = _rope(kr, c, s1, s2)
        low = lax.broadcasted_iota(jnp.int32, (tm, HEAD_PAD), 1) < QK_NOPE
        for h in range(MLA_HEADS):
            sl = slice(h * HEAD_PAD, (h + 1) * HEAD_PAD)
            q_ref[:, sl] = _rope(q[:, sl], c, s1, s2).astype(BF16)
            kvh = kv[:, sl]
            k_ref[:, sl] = (jnp.where(low, kvh, 0.0) + krr).astype(BF16)
            v_ref[:, sl] = jnp.where(low, 0.0, kvh).astype(BF16)

    row = lambda n: pl.BlockSpec((tm, n), lambda i: (i, 0))
    full = lambda a: pl.BlockSpec(a.shape, lambda i: (0, 0))
    wide = jax.ShapeDtypeStruct((s_dim, width), BF16)
    return _pcall(
        body,
        out_shape=(wide, wide, wide, jax.ShapeDtypeStruct((s_dim, Q_LORA), BF16), jax.ShapeDtypeStruct((s_dim, KV_LORA), BF16)),
        grid=(s_dim // tm,),
        in_specs=[row(LAT_PAD), full(q_norm), full(kv_norm), full(wq_t), full(wkv_t), row(HEAD_PAD), row(HEAD_PAD), row(HEAD_PAD)],
        out_specs=(row(width), row(width), row(width), row(Q_LORA), row(KV_LORA)), name=name,
        compiler_params=_params(("parallel",), [((tm, LAT_PAD), F32), (wq_t.shape, BF16), (wkv_t.shape, BF16)]
                                + 3 * [((tm, width), BF16)], extra=4 * tm * width * 4),
    )(lat, q_norm, kv_norm, wq_t, wkv_t, *rope)


def _mla_probs(q, k_ref, t, tq):
    lo = t * tq
    own = slice(lo, lo + tq)
    s_own = _dot_nt(q, k_ref[own, :]) * MLA_SCALE
    rows = lax.broadcasted_iota(jnp.int32, s_own.shape, 0)
    cols = lax.broadcasted_iota(jnp.int32, s_own.shape, 1)
    s_own = jnp.where(cols <= rows, s_own, -jnp.inf)
    mx = jnp.max(s_own, axis=-1, keepdims=True)
    if t == 0:
        e_own = jnp.exp(s_own - mx)
        return [(e_own * (1.0 / jnp.sum(e_own, axis=-1, keepdims=True)), own)]
    before = slice(0, lo)
    s_pre = _dot_nt(q, k_ref[before, :]) * MLA_SCALE
    mx = jnp.maximum(mx, jnp.max(s_pre, axis=-1, keepdims=True))
    e_own, e_pre = jnp.exp(s_own - mx), jnp.exp(s_pre - mx)
    inv = 1.0 / (jnp.sum(e_own, axis=-1, keepdims=True) + jnp.sum(e_pre, axis=-1, keepdims=True))
    return [(e_pre * inv, before), (e_own * inv, own)]


def _mla_attn_fwd(q, k, v, name):
    s_dim = q.shape[0]
    tq = 512

    def body(q_ref, k_ref, v_ref, o_ref):
        for t in range(s_dim // tq):
            tile = slice(t * tq, (t + 1) * tq)
            o = None
            for p, keys in _mla_probs(q_ref[tile, :], k_ref, t, tq):
                part = _dot_nn(p.astype(BF16), v_ref[keys, :])
                o = part if o is None else o + part
            o_ref[tile, :] = o.astype(BF16)

    head = pl.BlockSpec((s_dim, HEAD_PAD), lambda h: (0, h))
    return _pcall(
        body, out_shape=jax.ShapeDtypeStruct(q.shape, BF16), grid=(MLA_HEADS,),
        in_specs=[head, head, head], out_specs=head, name=name,
        compiler_params=_params(("parallel",), 4 * [((s_dim, HEAD_PAD), BF16)], extra=4 * tq * s_dim * 4),
    )(q, k, v)


def _mla_attn_bwd(q, k, v, d_o, name):
    s_dim = q.shape[0]
    tq = 512

    def body(q_ref, k_ref, v_ref, do_ref, dq_ref, dk_ref, dv_ref):
        dk_ref[...] = jnp.zeros_like(dk_ref)
        dv_ref[...] = jnp.zeros_like(dv_ref)
        for t in range(s_dim // tq):
            tile = slice(t * tq, (t + 1) * tq)
            qt = q_ref[tile, :]
            dot = do_ref[tile, :].astype(BF16)
            pieces = [(p, keys, _dot_nt(dot, v_ref[keys, :])) for p, keys in _mla_probs(qt, k_ref, t, tq)]
            row = None
            for p, _, dp in pieces:
                part = jnp.sum(p * dp, axis=-1, keepdims=True)
                row = part if row is None else row + part
            dq = None
            for p, keys, dp in pieces:
                dsb = (p * (dp - row) * MLA_SCALE).astype(BF16)
                part = _dot_nn(dsb, k_ref[keys, :])
                dq = part if dq is None else dq + part
                dk_ref[keys, :] += _dot_tn(dsb, qt)
                dv_ref[keys, :] += _dot_tn(p.astype(BF16), dot)
            dq_ref[tile, :] = dq

    head = pl.BlockSpec((s_dim, HEAD_PAD), lambda h: (0, h))
    out = jax.ShapeDtypeStruct(q.shape, F32)
    return _pcall(
        body, out_shape=(out, out, out), grid=(MLA_HEADS,),
        in_specs=[head, head, head, head], out_specs=(head, head, head), name=name,
        compiler_params=_params(("parallel",), 3 * [((s_dim, HEAD_PAD), BF16)] + 4 * [((s_dim, HEAD_PAD), F32)],
                                extra=6 * tq * s_dim * 4),
    )(q, k, v, d_o)


def _mla_qkv_bwd(dq, dk, dv, lat, q_norm, kv_norm, wq_t, wkv_t, rope, name):
    s_dim = lat.shape[0]
    width = MLA_HEADS * HEAD_PAD
    tm = 256

    def body(dq_ref, dk_ref, dv_ref, lat_ref, qg_ref, kg_ref, wq_ref, wkv_ref, c_ref, s1_ref, s2_ref,
             dqp_ref, dkv_ref, dlat_ref, dqg_ref, dkg_ref):
        @pl.when(pl.program_id(0) == 0)
        def _():
            dqg_ref[...] = jnp.zeros_like(dqg_ref)
            dkg_ref[...] = jnp.zeros_like(dkg_ref)

        c, s1, s2 = c_ref[...], s1_ref[...], s2_ref[...]
        lane = lax.broadcasted_iota(jnp.int32, (tm, HEAD_PAD), 1)
        low = lane < QK_NOPE
        rot = (lane >= QK_NOPE) & (lane < QK_NOPE + QK_ROPE)
        dkrr = jnp.zeros((tm, HEAD_PAD), F32)
        for h in range(MLA_HEADS):
            sl = slice(h * HEAD_PAD, (h + 1) * HEAD_PAD)
            dqp_ref[:, sl] = _rope_t(dq_ref[:, sl], c, s1, s2).astype(BF16)
            dkh = dk_ref[:, sl]
            dkv_ref[:, sl] = jnp.where(low, dkh, dv_ref[:, sl]).astype(BF16)
            dkrr = dkrr + jnp.where(rot, dkh, 0.0)
        dqn = _dot_nn(dqp_ref[...], wq_ref[...])
        dkvn = _dot_nn(dkv_ref[...], wkv_ref[...])
        cq = lat_ref[:, :Q_LORA]
        ckv = lat_ref[:, Q_LORA:Q_LORA + KV_LORA]
        rq, rkv = _rstd(cq), _rstd(ckv)
        dqg_ref[...] += jnp.sum(dqn * cq * rq, axis=0, keepdims=True)
        dkg_ref[...] += jnp.sum(dkvn * ckv * rkv, axis=0, keepdims=True)
        dlat_ref[:, :Q_LORA] = _rms_bwd(cq, rq, dqn * qg_ref[...])
        dlat_ref[:, Q_LORA:Q_LORA + KV_LORA] = _rms_bwd(ckv, rkv, dkvn * kg_ref[...])
        dlat_ref[:, Q_LORA + KV_LORA:] = _rope_t(dkrr, c, s1, s2)

    row = lambda n: pl.BlockSpec((tm, n), lambda i: (i, 0))
    full = lambda a: pl.BlockSpec(a.shape, lambda i: (0, 0))
    wide = jax.ShapeDtypeStruct((s_dim, width), BF16)
    return _pcall(
        body,
        out_shape=(wide, wide, jax.ShapeDtypeStruct((s_dim, LAT_PAD), F32),
                   jax.ShapeDtypeStruct(q_norm.shape, F32), jax.ShapeDtypeStruct(kv_norm.shape, F32)),
        grid=(s_dim // tm,),
        in_specs=[row(width), row(width), row(width), row(LAT_PAD), full(q_norm), full(kv_norm), full(wq_t), full(wkv_t),
                  row(HEAD_PAD), row(HEAD_PAD), row(HEAD_PAD)],
        out_specs=(row(width), row(width), row(LAT_PAD), full(q_norm), full(kv_norm)), name=name,
        compiler_params=_params(("arbitrary",), 3 * [((tm, width), F32)] + [((tm, LAT_PAD), F32), (wq_t.shape, BF16),
                                                                           (wkv_t.shape, BF16)] + 2 * [((tm, width), BF16)],
                                extra=2 * tm * width * 4),
    )(dq, dk, dv, lat, q_norm, kv_norm, wq_t, wkv_t, *rope)


def _t5_bucket(dist):
    max_exact = N_BUCKETS // 2
    d = jnp.maximum(dist, 1).astype(F32)
    large = max_exact + (jnp.log(d / max_exact) / math.log(MAX_DISTANCE / max_exact)
                         * (N_BUCKETS - max_exact)).astype(jnp.int32)
    large = jnp.minimum(large, N_BUCKETS - 1)
    return jnp.where(dist < max_exact, dist, large)


def _dil_buckets(dilation):
    iq = jnp.arange(DIL_BLOCK)[:, None]
    ik = jnp.arange(2 * DIL_BLOCK)[None, :]
    return _t5_bucket(jnp.maximum(DIL_BLOCK + iq - ik, 0) * dilation)


def _dil_logits(qh, kb, bias_h, first, span):
    if first:
        s = _dot_nt(qh, kb) * DIL_SCALE + bias_h[:, DIL_BLOCK:]
        rel = lax.broadcasted_iota(jnp.int32, s.shape, 0) - lax.broadcasted_iota(jnp.int32, s.shape, 1)
    else:
        s = _dot_nt(qh, kb) * DIL_SCALE + bias_h
        rel = DIL_BLOCK + lax.broadcasted_iota(jnp.int32, s.shape, 0) - lax.broadcasted_iota(jnp.int32, s.shape, 1)
    return jnp.where((rel >= 0) & (rel <= span), s, -jnp.inf)


def _dil_blocks(s_dim, dilation):
    rows = s_dim // dilation
    for r in range(dilation):
        for n in range(rows // DIL_BLOCK):
            lo = r * rows + n * DIL_BLOCK
            keys = slice(lo, lo + DIL_BLOCK) if n == 0 else slice(lo - DIL_BLOCK, lo + DIL_BLOCK)
            start = r + n * DIL_BLOCK * dilation
            tokens = slice(start, start + DIL_BLOCK) if dilation == 1 else pl.ds(start, DIL_BLOCK, stride=dilation)
            yield n == 0, slice(lo, lo + DIL_BLOCK), keys, tokens


def _dil_views(s_dim):
    col = lambda which: pl.BlockSpec((s_dim, HEAD_PAD), lambda p: (0, which * DIL_PAIRS + p))
    nat = pl.BlockSpec((s_dim, HEAD_PAD), lambda p: (0, p))
    bias = pl.BlockSpec((2, DIL_BLOCK, 2 * DIL_BLOCK), lambda p: (p, 0, 0))
    return col, nat, bias


def _dil_attn_fwd(qkv, bias, dilation, span, name):
    s_dim = qkv.shape[0]
    d_dim = DIL_HEADS * DIL_HEAD_DIM
    col, nat, bias_spec = _dil_views(s_dim)

    def body(q_ref, k_ref, v_ref, b_ref, o_ref, l_ref):
        lane = lax.broadcasted_iota(jnp.int32, (DIL_BLOCK, HEAD_PAD), 1)
        klane = lax.broadcasted_iota(jnp.int32, (2 * DIL_BLOCK, HEAD_PAD), 1)
        for first, blk, keys, tokens in _dil_blocks(s_dim, dilation):
            qb, kb, vb = q_ref[blk, :], k_ref[keys, :], v_ref[keys, :]
            o_acc = jnp.zeros((DIL_BLOCK, HEAD_PAD), F32)
            lse_acc = jnp.zeros((DIL_BLOCK, HEAD_PAD), F32)
            for h in range(2):
                mine = (lane < DIL_HEAD_DIM) == (h == 0)
                kmine = (klane[:vb.shape[0]] < DIL_HEAD_DIM) == (h == 0)
                logits = _dil_logits(jnp.where(mine, qb, 0), kb, b_ref[h], first, span)
                mx = jnp.max(logits, axis=-1, keepdims=True)
                e = jnp.exp(logits - mx)
                tot = jnp.sum(e, axis=-1, keepdims=True)
                lse = mx + jnp.log(tot)
                p = e * (1.0 / tot)
                o_acc = o_acc + _dot_nn(p.astype(BF16), jnp.where(kmine, vb, 0))
                lse_acc = jnp.where(mine, lse, lse_acc)
            o_ref[tokens, :] = o_acc
            l_ref[tokens, :] = lse_acc

    out = jax.ShapeDtypeStruct((s_dim, d_dim), F32)
    return _pcall(
        body, out_shape=(out, out), grid=(DIL_PAIRS,),
        in_specs=[col(0), col(1), col(2), bias_spec], out_specs=(nat, nat), name=name,
        compiler_params=_params(("parallel",), 3 * [((s_dim, HEAD_PAD), BF16)] + 2 * [((s_dim, HEAD_PAD), F32)]
                                + [((2, DIL_BLOCK, 2 * DIL_BLOCK), F32)], extra=2**21),
    )(qkv, qkv, qkv, bias)


def _dil_mix(lses, outs, name):
    s_dim, d_dim = outs[0].shape
    tm = TOKEN_TILE
    ng = len(outs)

    def body(*refs):
        ls = [refs[g][...] for g in range(ng)]
        mx = ls[0]
        for g in range(1, ng):
            mx = jnp.maximum(mx, ls[g])
        es = [jnp.exp(l - mx) for l in ls]
        tot = es[0]
        for g in range(1, ng):
            tot = tot + es[g]
        o = None
        for g in range(ng):
            al = es[g] / tot
            refs[2 * ng + g][...] = al
            t = al * refs[ng + g][...]
            o = t if o is None else o + t
        refs[3 * ng][...] = o
        refs[3 * ng + 1][...] = o.astype(BF16)

    row = pl.BlockSpec((tm, d_dim), lambda i: (i, 0))
    f = jax.ShapeDtypeStruct((s_dim, d_dim), F32)
    res = _pcall(
        body, out_shape=tuple(ng * [f] + [f, jax.ShapeDtypeStruct((s_dim, d_dim), BF16)]), grid=(s_dim // tm,),
        in_specs=2 * ng * [row], out_specs=tuple((ng + 2) * [row]), name=name,
        compiler_params=_params(("parallel",), (3 * ng + 2) * [((tm, d_dim), F32)], extra=4 * tm * d_dim * 4),
    )(*lses, *outs)
    return res[:ng], res[ng], res[ng + 1]


def _dil_attn_bwd(qkv, bias, d_o, o_mix, alpha, lse, dilation, span, name):
    s_dim = qkv.shape[0]
    d_dim = DIL_HEADS * DIL_HEAD_DIM
    col, nat, bias_spec = _dil_views(s_dim)

    def body(q_ref, k_ref, v_ref, b_ref, do_ref, om_ref, al_ref, l_ref, dq_ref, dk_ref, dv_ref, db_ref, dk_acc, dv_acc):
        db_ref[...] = jnp.zeros_like(db_ref)
        dk_acc[...] = jnp.zeros_like(dk_acc)
        dv_acc[...] = jnp.zeros_like(dv_acc)
        lane = lax.broadcasted_iota(jnp.int32, (DIL_BLOCK, HEAD_PAD), 1)
        klane = lax.broadcasted_iota(jnp.int32, (2 * DIL_BLOCK, HEAD_PAD), 1)
        for first, blk, kv_rows, tokens in _dil_blocks(s_dim, dilation):
            qb, kb, vb = q_ref[blk, :], k_ref[kv_rows, :], v_ref[kv_rows, :]
            al = al_ref[tokens, :]
            dog = al * do_ref[tokens, :]
            row_term = dog * om_ref[tokens, :]
            lse_b = l_ref[tokens, :]
            dq_acc = jnp.zeros((DIL_BLOCK, HEAD_PAD), F32)
            dk_blk = jnp.zeros((kb.shape[0], HEAD_PAD), F32)
            dv_blk = jnp.zeros((kb.shape[0], HEAD_PAD), F32)
            for h in range(2):
                mine = (lane < DIL_HEAD_DIM) == (h == 0)
                kmine = (klane[:kb.shape[0]] < DIL_HEAD_DIM) == (h == 0)
                qh = jnp.where(mine, qb, 0)
                logits = _dil_logits(qh, kb, b_ref[h], first, span)
                lse_h = jnp.max(jnp.where(mine, lse_b, -jnp.inf), axis=-1, keepdims=True)
                p = jnp.exp(logits - lse_h)
                dogh = jnp.where(mine, dog, 0.0).astype(BF16)
                dp = _dot_nt(dogh, vb)
                ds = p * (dp - jnp.sum(jnp.where(mine, row_term, 0.0), axis=-1, keepdims=True))
                if first:
                    db_ref[h, :, DIL_BLOCK:] += ds
                else:
                    db_ref[h] += ds
                dsb = (ds * DIL_SCALE).astype(BF16)
                dq_acc = dq_acc + _dot_nn(dsb, jnp.where(kmine, kb, 0))
                dk_blk = dk_blk + _dot_tn(dsb, qh)
                dv_blk = dv_blk + _dot_tn(p.astype(BF16), dogh)
            dq_ref[blk, :] = dq_acc.astype(BF16)
            dk_acc[kv_rows, :] += dk_blk
            dv_acc[kv_rows, :] += dv_blk
        dk_ref[...] = dk_acc[...].astype(BF16)
        dv_ref[...] = dv_acc[...].astype(BF16)

    grad = jax.ShapeDtypeStruct((s_dim, d_dim), BF16)
    return _pcall(
        body, out_shape=(grad, grad, grad, jax.ShapeDtypeStruct(bias.shape, F32)), grid=(DIL_PAIRS,),
        in_specs=[col(0), col(1), col(2), bias_spec, nat, nat, nat, nat],
        out_specs=(nat, nat, nat, bias_spec), name=name,
        scratch_shapes=[pltpu.VMEM((s_dim, HEAD_PAD), F32), pltpu.VMEM((s_dim, HEAD_PAD), F32)],
        compiler_params=_params(("parallel",), 6 * [((s_dim, HEAD_PAD), BF16)] + 4 * [((s_dim, HEAD_PAD), F32)]
                                + 2 * [((2, DIL_BLOCK, 2 * DIL_BLOCK), F32)], extra=2 * s_dim * HEAD_PAD * 4 + 2**21),
    )(qkv, qkv, qkv, bias, d_o, o_mix, alpha, lse)


def _bias_reduce(dbias, buckets, name):
    n_heads = dbias.shape[0]

    def body(db_ref, bk_ref, o_ref):
        ds, bk = db_ref[0], bk_ref[0]
        lane = lax.broadcasted_iota(jnp.int32, (8, HEAD_PAD), 1)
        acc = jnp.zeros((8, HEAD_PAD), F32)
        for b in range(N_BUCKETS):
            acc = jnp.where(lane == b, jnp.sum(jnp.where(bk == b, ds, 0.0)), acc)
        o_ref[0] = acc

    blk = (1, DIL_BLOCK, 2 * DIL_BLOCK)
    return _pcall(
        body, out_shape=jax.ShapeDtypeStruct((n_heads, 8, HEAD_PAD), F32), grid=(n_heads,),
        in_specs=[pl.BlockSpec(blk, lambda h: (h, 0, 0)), pl.BlockSpec(blk, lambda h: (h // DIL_HEADS, 0, 0))],
        out_specs=pl.BlockSpec((1, 8, HEAD_PAD), lambda h: (h, 0, 0)), name=name,
        compiler_params=_params(("parallel",), [(blk, F32), (blk, jnp.int32)], extra=2**20),
    )(dbias, buckets)


def _loss_grad(y, target, name):
    s_dim, d_dim = y.shape
    tm = TOKEN_TILE

    def body(y_ref, t_ref, dy_ref, l_ref):
        @pl.when(pl.program_id(0) == 0)
        def _():
            l_ref[...] = jnp.zeros_like(l_ref)

        err = y_ref[...] - t_ref[...]
        dy_ref[...] = err / d_dim
        sq = (err * err).reshape(tm // 8, 8, d_dim)
        l_ref[...] += 0.5 * jnp.sum(sq, axis=0) / d_dim

    row = pl.BlockSpec((tm, d_dim), lambda i: (i, 0))
    acc = pl.BlockSpec((8, d_dim), lambda i: (0, 0))
    return _pcall(
        body, out_shape=(jax.ShapeDtypeStruct((s_dim, d_dim), F32), jax.ShapeDtypeStruct((8, d_dim), F32)),
        grid=(s_dim // tm,), in_specs=[row, row], out_specs=(row, acc), name=name,
        compiler_params=_params(("arbitrary",), 3 * [((tm, d_dim), F32)], extra=2 * tm * d_dim * 4),
    )(y, target)


def _mod_fwd(c_all, w_mod, b_loc, name):
    depth, d_dim, n = w_mod.shape
    nb = c_all.shape[0]

    def body(c_ref, w_ref, b_ref, o_ref, s_ref):
        cv = c_ref[...]
        sc = cv * jax.nn.sigmoid(cv)
        s_ref[...] = sc
        o_ref[0] = _dot_nn(sc.astype(BF16), w_ref[0].astype(BF16)) + b_ref[0]

    return _pcall(
        body, out_shape=(jax.ShapeDtypeStruct((depth, nb, n), F32), jax.ShapeDtypeStruct((nb, d_dim), F32)), grid=(depth,),
        in_specs=[pl.BlockSpec((nb, d_dim), lambda i: (0, 0)), pl.BlockSpec((1, d_dim, n), lambda i: (i, 0, 0)),
                  pl.BlockSpec((1, 1, n), lambda i: (i, 0, 0))],
        out_specs=(pl.BlockSpec((1, nb, n), lambda i: (i, 0, 0)), pl.BlockSpec((nb, d_dim), lambda i: (0, 0))), name=name,
        compiler_params=_params(("arbitrary",), [((1, d_dim, n), F32)], extra=d_dim * n * 2 + 2**20),
    )(c_all, w_mod, b_loc.reshape(depth, 1, n))


def _sum_parts(parts, name):
    _, rows, cols = parts.shape
    fits = [t for t in range(16, rows // 2 + 1, 16) if rows % t == 0 and NDEV * t * cols * parts.dtype.itemsize <= 3 * 2**20]
    tr = max(fits) if fits else rows

    def body(p_ref, o_ref):
        acc = p_ref[0].astype(F32)
        for k in range(1, NDEV):
            acc = acc + p_ref[k].astype(F32)
        o_ref[...] = acc

    return _pcall(
        body, out_shape=jax.ShapeDtypeStruct((rows, cols), F32), grid=(rows // tr,),
        in_specs=[pl.BlockSpec((NDEV, tr, cols), lambda i: (0, i, 0))], out_specs=pl.BlockSpec((tr, cols), lambda i: (i, 0)),
        name=name, compiler_params=_params(("parallel",), [((NDEV, tr, cols), parts.dtype), ((tr, cols), F32)], extra=2**20),
    )(parts)


def _adamw(w, g, m, v, name):
    shape = w.shape
    cols = shape[-1]
    rows = math.prod(shape[:-1])
    tr = rows
    for cand in (512, 256, 128, 64, 32, 16, 8):
        if rows % cand == 0 and rows > cand and cand * cols * 4 <= 2**21:
            tr = cand
            break

    def body(w_ref, g_ref, m_ref, v_ref, d_ref, mo_ref, vo_ref):
        gv = g_ref[...]
        mn = ADAM_B1 * m_ref[...] + (1.0 - ADAM_B1) * gv
        vn = ADAM_B2 * v_ref[...] + (1.0 - ADAM_B2) * (gv * gv)
        m_hat = mn / (1.0 - ADAM_B1 ** ADAM_STEP)
        v_hat = vn / (1.0 - ADAM_B2 ** ADAM_STEP)
        d_ref[...] = -ADAM_LR * (m_hat / (jnp.sqrt(v_hat) + ADAM_EPS) + ADAM_WD * w_ref[...])
        mo_ref[...] = mn
        vo_ref[...] = vn

    blk = pl.BlockSpec((tr, cols), lambda i: (i, 0))
    out = jax.ShapeDtypeStruct((rows, cols), F32)
    res = _pcall(
        body, out_shape=(out, out, out), grid=(rows // tr,), in_specs=4 * [blk], out_specs=(blk, blk, blk), name=name,
        compiler_params=_params(("parallel",), 7 * [((tr, cols), F32)], extra=4 * tr * cols * 4),
    )(*(a.reshape(rows, cols) for a in (w, g, m, v)))
    return tuple(r.reshape(shape) for r in res)


def _peers():
    x, y, c = lax.axis_index("x"), lax.axis_index("y"), lax.axis_index("c")
    flip = lambda v, f: 1 - v if f else v
    peers = []
    for f in range(1, NDEV):
        px, py, pc = flip(x, f & 4), flip(y, f & 2), flip(c, f & 1)
        peers.append(((px, py, pc), 4 * px + 2 * py + pc))
    return (x, y, c), 4 * x + 2 * y + c, peers


def _places():
    x, y, c = lax.axis_index("x"), lax.axis_index("y"), lax.axis_index("c")
    place = lambda px, py, pc: ((px, py, pc), 4 * px + 2 * py + pc)
    return place(x, y, c), place(x, y, 1 - c), [place(1 - x, y, c), place(x, 1 - y, c), place(1 - x, 1 - y, c)]


def _exchange(arrs, gather, name):
    n = len(arrs)
    hbm = pl.BlockSpec(memory_space=pltpu.HBM)
    if gather:
        out_shape = [jax.ShapeDtypeStruct((NDEV * a.shape[0], a.shape[1]), a.dtype) for a in arrs]
    else:
        out_shape = [jax.ShapeDtypeStruct((NDEV, a.shape[0] // NDEV, a.shape[1]), a.dtype) for a in arrs]

    def body(*refs):
        ins, outs = refs[:n], refs[n:2 * n]
        send_sems, recv_sems, local_sems = refs[2 * n:]
        me_pos, me, peers = _peers()
        local = []
        for k in range(n):
            rows = arrs[k].shape[0] if gather else arrs[k].shape[0] // NDEV
            if gather:
                src_of = lambda idx: ins[k]
                dst_of = lambda idx: outs[k].at[pl.ds(me * rows, rows)]
                mine = (ins[k], outs[k].at[pl.ds(me * rows, rows)])
            else:
                src_of = lambda idx: ins[k].at[pl.ds(idx * rows, rows)]
                dst_of = lambda idx: outs[k].at[me]
                mine = (ins[k].at[pl.ds(me * rows, rows)], outs[k].at[me])
            cp = pltpu.make_async_copy(mine[0], mine[1], local_sems.at[k])
            cp.start()
            local.append(cp)
            for pos, idx in peers:
                pltpu.make_async_remote_copy(src_ref=src_of(idx), dst_ref=dst_of(idx), send_sem=send_sems.at[k],
                                             recv_sem=recv_sems.at[k], device_id=pos, device_id_type=MESH).start()
        for k in range(n):
            rows = arrs[k].shape[0] if gather else arrs[k].shape[0] // NDEV
            sent = ins[k].at[pl.ds(0, (NDEV - 1) * rows)] if not gather else outs[k].at[pl.ds(0, (NDEV - 1) * rows)]
            got = outs[k].at[pl.ds(0, (NDEV - 1) * rows)] if gather else outs[k].at[pl.ds(0, NDEV - 1)]
            pltpu.make_async_remote_copy(src_ref=sent, dst_ref=sent, send_sem=send_sems.at[k], recv_sem=recv_sems.at[k],
                                         device_id=me_pos, device_id_type=MESH).wait_send()
            pltpu.make_async_remote_copy(src_ref=got, dst_ref=got, send_sem=send_sems.at[k], recv_sem=recv_sems.at[k],
                                         device_id=me_pos, device_id_type=MESH).wait_recv()
            local[k].wait()

    return pl.pallas_call(
        body, out_shape=out_shape, in_specs=n * [hbm], out_specs=n * [hbm], name=name,
        scratch_shapes=[pltpu.SemaphoreType.DMA((n,)), pltpu.SemaphoreType.DMA((n,)), pltpu.SemaphoreType.DMA((n,))],
        compiler_params=pltpu.CompilerParams(has_side_effects=True),
    )(*arrs)


_HBM = pl.BlockSpec(memory_space=pltpu.HBM)
_SEM = pl.BlockSpec(memory_space=pltpu.SEMAPHORE)
_DATAFLOW = pltpu.SideEffectType.DATAFLOW_SIDE_EFFECTING


def _split_start(srcs, groups, gather, name):
    n = len(srcs)
    if gather:
        lands = [lax.empty((NDEV * a.shape[0], a.shape[1]), a.dtype) for a in srcs]
    else:
        lands = [lax.empty((NDEV, a.shape[0] // NDEV, a.shape[1]), a.dtype) for a in srcs]
    n_sem = 3 * len(groups)

    def body(*refs):
        src_refs, land_refs = refs[:n], refs[n:2 * n]
        sems = refs[2 * n:2 * n + n_sem]
        token = refs[-1]
        (_, my), sibling, chips = _places()
        _, _, peers = _peers()
        targets = [sibling] + chips if gather else peers
        for g, members in enumerate(groups):
            for j, k in enumerate(members):
                _own_copy(src_refs[k], land_refs[k], sems[3 * g + 2].at[j], my, gather).start()
        for g, members in enumerate(groups):
            for j, k in enumerate(members):
                rows = srcs[k].shape[0] if gather else srcs[k].shape[0] // NDEV
                for pos, idx in targets:
                    src = src_refs[k] if gather else src_refs[k].at[pl.ds(idx * rows, rows)]
                    dst = land_refs[k].at[pl.ds(my * rows, rows)] if gather else land_refs[k].at[my]
                    pltpu.make_async_remote_copy(src_ref=src, dst_ref=dst, send_sem=sems[3 * g].at[j],
                                                 recv_sem=sems[3 * g + 1].at[j], device_id=pos, device_id_type=MESH).start()
        token[...] = jnp.zeros_like(token)

    out_shape = []
    for members in groups:
        out_shape += 3 * [pltpu.SemaphoreType.DMA((len(members),))]
    out_shape += [pltpu.HBM(a.shape, a.dtype) for a in srcs] + [pltpu.HBM(a.shape, a.dtype) for a in lands]
    out_shape.append(jax.ShapeDtypeStruct((8, 128), F32))
    res = pl.pallas_call(
        body, name=name, out_shape=tuple(out_shape), in_specs=2 * n * [_HBM],
        out_specs=tuple(n_sem * [_SEM] + 2 * n * [_HBM] + [pl.BlockSpec(memory_space=pltpu.VMEM)]),
        input_output_aliases={i: n_sem + i for i in range(2 * n)},
        compiler_params=pltpu.CompilerParams(has_side_effects=_DATAFLOW),
    )(*[pltpu.with_memory_space_constraint(a, pltpu.HBM) for a in list(srcs) + lands])
    sems = [tuple(res[3 * g:3 * g + 3]) for g in range(len(groups))]
    return sems, list(res[n_sem:n_sem + n]), list(res[n_sem + n:n_sem + 2 * n]), res[-1]


def _own_copy(src_ref, land_ref, sem, my, gather):
    if gather:
        rows = src_ref.shape[0]
        return pltpu.make_async_copy(src_ref, land_ref.at[pl.ds(my * rows, rows)], sem)
    rows = src_ref.shape[0] // NDEV
    return pltpu.make_async_copy(src_ref.at[pl.ds(my * rows, rows)], land_ref.at[my], sem)


def _wait_all(land_ref, blocks_per_dev, copies, send_sem, recv_sem, me_pos):
    part = land_ref.at[pl.ds(0, copies * blocks_per_dev)]
    pltpu.make_async_remote_copy(src_ref=part, dst_ref=part, send_sem=send_sem, recv_sem=recv_sem,
                                 device_id=me_pos, device_id_type=MESH).wait()


def _gather_forward(sems, srcs, lands, after, name):
    n = len(srcs)

    def body(*refs):
        land_refs = refs[n:2 * n]
        send_a, recv_a = refs[2 * n], refs[2 * n + 1]
        send_b, recv_b = refs[2 * n + 3], refs[2 * n + 4]
        token = refs[-1]
        (me_pos, _), sibling, chips = _places()
        for j in range(n):
            _wait_all(land_refs[j], lands[j].shape[0] // NDEV, 1 + OTHER_CHIPS, send_a.at[j], recv_a.at[j], me_pos)
        for j in range(n):
            rows = lands[j].shape[0] // NDEV
            for _, idx in chips:
                block = land_refs[j].at[pl.ds(idx * rows, rows)]
                pltpu.make_async_remote_copy(src_ref=block, dst_ref=block, send_sem=send_b.at[j], recv_sem=recv_b.at[j],
                                             device_id=sibling[0], device_id_type=MESH).start()
        token[...] = jnp.zeros_like(token)

    res = pl.pallas_call(
        body, name=name,
        out_shape=(pltpu.SemaphoreType.DMA((n,)), pltpu.SemaphoreType.DMA((n,)))
        + tuple(pltpu.HBM(a.shape, a.dtype) for a in list(srcs) + list(lands)) + (jax.ShapeDtypeStruct((8, 128), F32),),
        in_specs=2 * n * [_HBM] + [_SEM, _SEM, pl.BlockSpec(memory_space=pl.ANY)],
        out_specs=tuple([_SEM, _SEM] + 2 * n * [_HBM] + [pl.BlockSpec(memory_space=pltpu.VMEM)]),
        input_output_aliases={i: 2 + i for i in range(2 * n)},
        compiler_params=pltpu.CompilerParams(has_side_effects=_DATAFLOW),
    )(*srcs, *lands, sems[0], sems[1], after)
    return (res[0], res[1]), list(res[2:2 + n]), list(res[2 + n:2 + 2 * n]), res[-1]


def _split_wait(sems, srcs, lands, after, copies, gather, name):
    n = len(srcs)

    def body(*refs):
        src_refs, land_refs = refs[:n], refs[n:2 * n]
        send_sem, recv_sem, local_sem = refs[2 * n], refs[2 * n + 1], refs[2 * n + 2]
        (me_pos, my), _, _ = _places()
        for j in range(n):
            _wait_all(land_refs[j], lands[j].shape[0] // NDEV, copies, send_sem.at[j], recv_sem.at[j], me_pos)
            _own_copy(src_refs[j], land_refs[j], local_sem.at[j], my, gather).wait()

    res = pl.pallas_call(
        body, name=name, out_shape=tuple(pltpu.HBM(a.shape, a.dtype) for a in list(srcs) + list(lands)),
        in_specs=2 * n * [_HBM] + [_SEM, _SEM, _SEM, pl.BlockSpec(memory_space=pl.ANY)], out_specs=tuple(2 * n * [_HBM]),
        input_output_aliases={i: i for i in range(2 * n)},
        compiler_params=pltpu.CompilerParams(has_side_effects=_DATAFLOW),
    )(*srcs, *lands, sems[0], sems[1], sems[2], after)
    return list(res[n:])


def _chained(gate, mid, after):
    return gate if mid is None else gate + mid(after)[:1, :1]


def _ffn_fwd(x, norms, mod, w, mid=None):
    (pre_g, post_g), (shift, scale, gate), (wg_t, wu_t, wd) = norms, mod, w
    hn, g, u, a = _ffn_up(x, pre_g, scale, shift, wg_t, wu_t, "ffn_up")
    if callable(wd):
        wd = wd(a)
    x_out, f = _mm_post(a, wd, x, post_g, _chained(gate, mid, a), FFN_RES, "ffn_down")
    return x_out, (x, hn, g, u, a, f), (wg_t, wu_t, wd)


def _ffn_bwd(dx_out, saved, norms, mod, w, send=None):
    (pre_g, post_g), (_, scale, gate), (wg_t, wu_t, wd) = norms, mod, w
    x, hn, g, u, a, f = saved
    d_model = x.shape[1]
    sent = (lambda j, dw: None) if send is None else send
    df, dgate, dpost = _post_bwd(dx_out, f, post_g, gate, FFN_RES, "ffn_post_bwd")
    dwd = _mm([(a, df)], "tn", BF16, 256, d_model, "ffn_dw")
    dg, du = _ffn_dgu(df, wd, g, u, "ffn_dgu", after=sent(2, dwd))
    dwg_t = _mm([(dg, hn)], "tn", BF16, 256, d_model, "ffn_dw")
    dwu_t = _mm([(du, hn)], "tn", BF16, 256, d_model, "ffn_dw", after=sent(0, dwg_t))
    dhn = _mm([(dg, wg_t), (du, wu_t)], "nn", F32, TOKEN_TILE, d_model, "ffn_dhn", after=sent(1, dwu_t))
    dx, dshift, dscale, dpre = _prenorm_bwd(dx_out, [dhn], x, pre_g, scale, "prenorm_bwd")
    return dx, (dpre, dpost), (dshift, dscale, dgate), (dwg_t, dwu_t, dwd)


def _mla_fwd(x, norms, mod, w, rope, mid=None):
    (pre_g, post_g), (shift, scale, gate) = norms, mod
    w_in, q_norm, wq_t, kv_norm, wkv_t, wo = w
    hn, lat = _prenorm_mm(x, pre_g, scale, shift, w_in, "nn", F32, LAT_PAD, "mla_in")
    gate = _chained(gate, mid, lat)
    q, k, v, qn, kvn = _mla_qkv(lat, q_norm, kv_norm, wq_t, wkv_t, rope, "mla_qkv")
    o = _mla_attn_fwd(q, k, v, "mla_attn_fwd")
    x_out, f = _mm_post(o, wo, x, post_g, gate, 1.0, "mla_out")
    return x_out, (x, hn, lat, q, k, v, qn, kvn, o, f)


def _mla_bwd(dx_out, saved, norms, mod, w, rope):
    (pre_g, post_g), (_, scale, gate) = norms, mod
    w_in, q_norm, wq_t, kv_norm, wkv_t, wo = w
    x, hn, lat, q, k, v, qn, kvn, o, f = saved
    d_model = x.shape[1]
    df, dgate, dpost = _post_bwd(dx_out, f, post_g, gate, 1.0, "mix_post_bwd")
    d_o = _mm([(df, wo)], "nt", F32, TOKEN_TILE, wo.shape[0], "mla_do")
    dwo = _mm([(o, df)], "tn", BF16, TOKEN_TILE, d_model, "mla_dwo")
    dq, dk, dv = _mla_attn_bwd(q, k, v, d_o, "mla_attn_bwd")
    dqp, dkv, dlat, dq_norm, dkv_norm = _mla_qkv_bwd(dq, dk, dv, lat, q_norm, kv_norm, wq_t, wkv_t, rope, "mla_qkv_bwd")
    dwq_t = _mm([(dqp, qn)], "tn", BF16, TOKEN_TILE, Q_LORA, "mla_dwq")
    dwkv_t = _mm([(dkv, kvn)], "tn", BF16, TOKEN_TILE, KV_LORA, "mla_dwkv")
    dw_in = _mm([(hn, dlat)], "tn", BF16, TOKEN_TILE, LAT_PAD, "mla_dwin")
    dhn = _mm([(dlat, w_in)], "nt", F32, TOKEN_TILE, d_model, "mla_dhn")
    dx, dshift, dscale, dpre = _prenorm_bwd(dx_out, [dhn], x, pre_g, scale, "prenorm_bwd")
    return dx, (dpre, dpost), (dshift, dscale, dgate), (dw_in, dq_norm, dwq_t, dkv_norm, dwkv_t, dwo)


def _dil_fwd(x, norms, mod, w, bias, mid=None):
    (pre_g, post_g), (shift, scale, gate), (w_in_t, wo) = norms, mod, w
    width = 3 * DIL_HEADS * DIL_HEAD_DIM
    hns, qkvs, outs, lses = [], [], [], []
    for g, (window, dilation) in enumerate(DIL_GROUPS):
        hn, qkv = _prenorm_mm(x, pre_g, scale, shift, w_in_t[g * width:(g + 1) * width], "nt", BF16, width,
                              "dil_in", perm=dilation)
        if g == 0:
            gate = _chained(gate, mid, qkv)
        o, lse = _dil_attn_fwd(qkv, bias[g], dilation, window // dilation, "dil_attn_fwd")
        hns.append(hn), qkvs.append(qkv), outs.append(o), lses.append(lse)
    alphas, o_mix, o_mix_b = _dil_mix(lses, outs, "dil_mix")
    x_out, f = _mm_post(o_mix_b, wo, x, post_g, gate, 1.0, "dil_out")
    return x_out, (x, hns, qkvs, lses, alphas, o_mix, o_mix_b, f)


def _dil_bwd(dx_out, saved, norms, mod, w, bias):
    (pre_g, post_g), (_, scale, gate), (w_in_t, wo) = norms, mod, w
    x, hns, qkvs, lses, alphas, o_mix, o_mix_b, f = saved
    d_model = x.shape[1]
    inner = DIL_HEADS * DIL_HEAD_DIM
    df, dgate, dpost = _post_bwd(dx_out, f, post_g, gate, 1.0, "mix_post_bwd")
    d_o = _mm([(df, wo)], "nt", F32, TOKEN_TILE, inner, "dil_do")
    dwo = _mm([(o_mix_b, df)], "tn", BF16, TOKEN_TILE, d_model, "dil_dwo")
    dhns, dws, dbs = [], [], []
    for g, (window, dilation) in enumerate(DIL_GROUPS):
        grads = _dil_attn_bwd(qkvs[g], bias[g], d_o, o_mix, alphas[g], lses[g], dilation, window // dilation, "dil_attn_bwd")
        dbs.append(grads[3])
        w_parts = [w_in_t[(3 * g + j) * inner:(3 * g + j + 1) * inner] for j in range(3)]
        dhns.append(_mm(list(zip(grads[:3], w_parts)), "nn", F32, TOKEN_TILE, d_model, "dil_dhn", out_perm=dilation))
        dws += [_mm([(grads[j], hns[g])], "tn", BF16, TOKEN_TILE, d_model, "dil_dwin") for j in range(3)]
    dx, dshift, dscale, dpre = _prenorm_bwd(dx_out, dhns, x, pre_g, scale, "prenorm_bwd3")
    return dx, (dpre, dpost), (dshift, dscale, dgate), (jnp.concatenate(dws, axis=0), dwo), jnp.concatenate(dbs, axis=0)


def _pad_rows(a, rows):
    return jnp.pad(a, ((0, rows - a.shape[0]), (0, 0)))


def _lanes(a):
    flat = a.reshape(-1).astype(F32)
    rows = -(-flat.shape[0] // 1024) * 8
    return jnp.pad(flat, (0, rows * 128 - flat.shape[0])).reshape(rows, 128)


def kernel(x, c, norm_pre, norm_post, w_mod, b_mod, ffn_w_gate, ffn_w_up, ffn_w_down, mla_w_in, mla_q_norm, mla_w_q_up, mla_kv_norm, mla_w_kv_up, mla_w_o, dil_w_in, dil_w_o, rel_bias, loss_target, m_norm_pre, m_norm_post, m_w_mod, m_b_mod, m_ffn_w_gate, m_ffn_w_up, m_ffn_w_down, m_mla_w_in, m_mla_q_norm, m_mla_w_q_up, m_mla_kv_norm, m_mla_w_kv_up, m_mla_w_o, m_dil_w_in, m_dil_w_o, m_rel_bias, v_norm_pre, v_norm_post, v_w_mod, v_b_mod, v_ffn_w_gate, v_ffn_w_up, v_ffn_w_down, v_mla_w_in, v_mla_q_norm, v_mla_w_q_up, v_mla_kv_norm, v_mla_w_kv_up, v_mla_w_o, v_dil_w_in, v_dil_w_o, v_rel_bias):
    me = 4 * lax.axis_index("x") + 2 * lax.axis_index("y") + lax.axis_index("c")
    depth, n_sub, d_loc = norm_pre.shape
    d_model = x.shape[2]
    mod_loc_cols = w_mod.shape[2]
    x0, target = x[0], loss_target[0]

    bf_t = lambda a: a.astype(BF16).T
    ffn_ids = [(i, h) for i in range(depth) for h in range(2)]
    shards = []
    for i, h in ffn_ids:
        shards += [bf_t(ffn_w_gate[i, h]), bf_t(ffn_w_up[i, h]), ffn_w_down[i, h].astype(BF16)]
    shards += [mla_w_in[0].astype(BF16), bf_t(mla_w_q_up[0]), bf_t(mla_w_kv_up[0]), mla_w_o[0].astype(BF16),
               bf_t(dil_w_in[0]), dil_w_o[0].astype(BF16)]
    n_ffn = 3 * len(ffn_ids)
    members = {(0, 0): [0, 1, 2], (0, 1): [n_ffn, n_ffn + 1, n_ffn + 2, n_ffn + 3], (0, 2): [3, 4, 5],
               (1, 0): [6, 7, 8], (1, 1): [n_ffn + 4, n_ffn + 5], (1, 2): [9, 10, 11]}
    order = [(i, s) for i in range(depth) for s in range(n_sub)]

    small = jnp.concatenate([c.reshape(8, 128), _pad_rows(norm_pre.reshape(depth * n_sub, d_loc), 8),
                             _pad_rows(norm_post.reshape(depth * n_sub, d_loc), 8)], axis=0)
    small_all = _exchange([small], True, "gather_small")[0].reshape(NDEV, 24, 128)
    c_all = small_all[:, 0:8].reshape(NDEV, d_model)
    gains = lambda lo: jnp.transpose(small_all[:, lo:lo + depth * n_sub], (1, 0, 2)).reshape(depth, n_sub, 1, d_model)
    pre_full, post_full = gains(8), gains(16)

    b_loc = lax.dynamic_slice(b_mod, (0, me * mod_loc_cols), (depth, mod_loc_cols))
    mod_cols, silu_c = _mod_fwd(c_all, w_mod, b_loc, "mod_fwd")
    mod_all = _exchange([mod_cols.reshape(depth * NDEV, mod_loc_cols)], True, "gather_mod")[0]
    mod_all = mod_all.reshape(NDEV, depth, NDEV, mod_loc_cols)
    mod_mine = lax.dynamic_index_in_dim(mod_all, me, axis=2, keepdims=False)
    mod = jnp.transpose(mod_mine, (1, 0, 2)).reshape(depth, n_sub, 3, 1, d_model)

    shards[0], _ = lax.optimization_barrier((shards[0], mod_all))
    first = order[0]
    stages = [("%d%d" % first, members[first][:2]), ("%d%dd" % first, members[first][2:])]
    stages += [("%d%d" % key, members[key]) for key in order[1:]]
    stage_names = [name for name, _ in stages]
    g_sems, g_srcs, g_lands, g_token = _split_start(shards, [idx for _, idx in stages], True, "gather_weights_start")

    forwarded = {}

    def forward(stage, after):
        idx = stages[stage_names.index(stage)][1]
        forwarded[stage] = _gather_forward(g_sems[stage_names.index(stage)], [g_srcs[k] for k in idx],
                                           [g_lands[k] for k in idx], after, "gather_forward_" + stage)
        return forwarded[stage][3]

    def weights_of(stage, after):
        (send_b, recv_b), srcs, lands, _ = forwarded[stage]
        local = g_sems[stage_names.index(stage)][2]
        return _split_wait((send_b, recv_b, local), srcs, lands, after, OTHER_CHIPS, True, "gather_wait_" + stage)

    def late_down(after):
        forward("%d%dd" % first, after)
        return weights_of("%d%dd" % first, after)[0]

    lat_real = Q_LORA + KV_LORA
    qk = QK_NOPE + QK_ROPE

    def mla_weights(after):
        w_in, wq_t, wkv_t, wo = weights_of("01", after)
        w_in_pad = jnp.concatenate([w_in[:, :lat_real], jnp.zeros((d_model, QK_NOPE), BF16), w_in[:, lat_real:],
                                    jnp.zeros((d_model, HEAD_PAD - QK_NOPE - QK_ROPE), BF16)], axis=1)
        wq_pad = jnp.pad(wq_t.reshape(MLA_HEADS, qk, Q_LORA), ((0, 0), (0, HEAD_PAD - qk), (0, 0)))
        wo_pad = jnp.pad(wo.reshape(MLA_HEADS, V_HEAD, d_model), ((0, 0), (HEAD_PAD - V_HEAD, 0), (0, 0)))
        return (w_in_pad, mla_q_norm, wq_pad.reshape(MLA_HEADS * HEAD_PAD, Q_LORA), mla_kv_norm, wkv_t,
                wo_pad.reshape(MLA_HEADS * HEAD_PAD, d_model))

    zero = g_token[0, 0]
    rope = _rope_tables(zero)
    buckets = jnp.stack([_dil_buckets(dil) for _, dil in DIL_GROUPS]) + zero.astype(jnp.int32)
    onehot = (buckets[..., None] == jnp.arange(N_BUCKETS)).astype(F32)
    bias = jnp.einsum("gqkb,bgh->ghqk", onehot, rel_bias.reshape(N_BUCKETS, len(DIL_GROUPS), DIL_HEADS),
                      precision=lax.Precision.HIGHEST)

    norms = lambda i, s: (pre_full[i, s], post_full[i, s])
    mods = lambda i, s: (mod[i, s, 0], mod[i, s, 1], mod[i, s, 2])
    saved, weights = {}, {}
    h = lax.optimization_barrier((x0, bias, buckets, *rope))[0]
    forward("%d%d" % first, h)
    for n, (i, s) in enumerate(order):
        got = mla_weights(h) if (s == 1 and i % 2 == 0) else tuple(weights_of("%d%d" % (i, s), h))
        mid = None if n + 1 == len(order) else (lambda after, nxt="%d%d" % order[n + 1]: forward(nxt, after))
        if s != 1:
            got = got if len(got) == 3 else (*got, late_down)
            h, saved[i, s], weights[i, s] = _ffn_fwd(h, norms(i, s), mods(i, s), got, mid)
            continue
        weights[i, s] = got
        if i % 2 == 0:
            h, saved[i, s] = _mla_fwd(h, norms(i, s), mods(i, s), weights[i, s], rope, mid)
        else:
            h, saved[i, s] = _dil_fwd(h, norms(i, s), mods(i, s), weights[i, s], bias, mid)
    dh, loss_parts = _loss_grad(h, target, "loss")

    dnorm, dmod, sent = {}, {}, {}
    token = jnp.zeros((8, 128), F32)
    last = order[0]

    def send_last(j, dw):
        sent[last, j] = _split_start([dw], [[0]], False, "scatter_start_%d%d_%d" % (*last, j))
        return sent[last, j][3]

    for i, s in reversed(order):
        md = mods(i, s)
        md = (md[0], md[1], md[2] + token[:1, :1])
        if (i, s) == last:
            dh, dnorm[i, s], dmod[i, s], _ = _ffn_bwd(dh, saved[i, s], norms(i, s), md, weights[i, s], send_last)
            continue
        if s != 1:
            dh, dnorm[i, s], dmod[i, s], dws = _ffn_bwd(dh, saved[i, s], norms(i, s), md, weights[i, s])
        elif i % 2 == 0:
            dh, dnorm[i, s], dmod[i, s], dmla = _mla_bwd(dh, saved[i, s], norms(i, s), md, weights[i, s], rope)
            dw_in_pad, dq_norm, dwq_pad, dkv_norm, dwkv_t, dwo_pad = dmla
            dw_in = jnp.concatenate([dw_in_pad[:, :lat_real], dw_in_pad[:, lat_real + QK_NOPE:lat_real + qk]], axis=1)
            dwq_t = dwq_pad.reshape(MLA_HEADS, HEAD_PAD, Q_LORA)[:, :qk].reshape(MLA_HEADS * qk, Q_LORA)
            dwo = dwo_pad.reshape(MLA_HEADS, HEAD_PAD, d_model)[:, HEAD_PAD - V_HEAD:].reshape(MLA_HEADS * V_HEAD, d_model)
            dws = (dw_in, dwq_t, dwkv_t, dwo)
        else:
            dh, dnorm[i, s], dmod[i, s], dws, dbias = _dil_bwd(dh, saved[i, s], norms(i, s), md, weights[i, s], bias)
        sent[i, s] = _split_start(list(dws), [list(range(len(dws)))], False, "scatter_start_%d%d" % (i, s))
        token = sent[i, s][3]
    grad_x = dh[None]

    mine = {}
    for key in order[1:]:
        sems, srcs, lands, _ = sent[key]
        parts = _split_wait(sems[0], srcs, lands, dh, NDEV - 1, False, "scatter_wait_%d%d" % key)
        for k, p in zip(members[key], parts):
            mine[k] = _sum_parts(p, "sum_parts")
    for j in (2, 0, 1):
        sems, srcs, lands, _ = sent[last, j]
        parts = _split_wait(sems[0], srcs, lands, dh, NDEV - 1, False, "scatter_wait_%d%d_%d" % (*last, j))
        mine[members[last][j]] = _sum_parts(parts[0], "sum_parts")
    g_gate = jnp.stack([mine[3 * n].T for n in range(len(ffn_ids))]).reshape(ffn_w_gate.shape)
    g_up = jnp.stack([mine[3 * n + 1].T for n in range(len(ffn_ids))]).reshape(ffn_w_up.shape)
    g_down = jnp.stack([mine[3 * n + 2] for n in range(len(ffn_ids))]).reshape(ffn_w_down.shape)
    g_mla_in, g_q_up, g_kv_up, g_mla_o, g_dil_in, g_dil_o = (mine[k] for k in range(n_ffn, n_ffn + 6))
    g_mla_in, g_q_up, g_kv_up, g_mla_o = g_mla_in[None], g_q_up.T[None], g_kv_up.T[None], g_mla_o[None]
    g_dil_in, g_dil_o = g_dil_in.T[None], g_dil_o[None]

    dmod_mine = jnp.concatenate([jnp.concatenate(dmod[i, s], axis=0) for i in range(depth) for s in range(n_sub)], axis=0)
    dpre_mine = jnp.concatenate([dnorm[i, s][0] for i in range(depth) for s in range(n_sub)], axis=0)
    dpost_mine = jnp.concatenate([dnorm[i, s][1] for i in range(depth) for s in range(n_sub)], axis=0)
    dbias_tab = _bias_reduce(dbias, buckets, "bias_reduce")[:, 0, :N_BUCKETS].T
    pieces = [dmod_mine, dpre_mine, dpost_mine, dq_norm, dkv_norm, dbias_tab, jnp.sum(loss_parts).reshape(1, 1)]
    packed = [_lanes(p) for p in pieces]
    offs = [0]
    for p in packed:
        offs.append(offs[-1] + p.shape[0])
    everyone = _exchange([jnp.concatenate(packed, axis=0)], True, "gather_small_grads")[0].reshape(NDEV, offs[-1], 128)
    total = _sum_parts(everyone, "sum_small")
    take = lambda n, shape: total[offs[n]:offs[n + 1]].reshape(-1)[:math.prod(shape)].reshape(shape)
    g_b_mod = take(0, b_mod.shape)
    col0 = me * d_loc
    g_norm_pre = lax.dynamic_slice(take(1, (depth, n_sub, d_model)), (0, 0, col0), norm_pre.shape)
    g_norm_post = lax.dynamic_slice(take(2, (depth, n_sub, d_model)), (0, 0, col0), norm_post.shape)
    g_q_norm, g_kv_norm = take(3, mla_q_norm.shape), take(4, mla_kv_norm.shape)
    g_rel_bias = take(5, rel_bias.shape)
    loss = take(6, ())

    dmod_all = everyone[:, offs[0]:offs[1]].reshape(NDEV, depth, NDEV * mod_loc_cols)
    dmod_cols = lax.dynamic_slice(dmod_all, (0, 0, me * mod_loc_cols), (NDEV, depth, mod_loc_cols))
    silu_t = jnp.pad(silu_c.T, ((0, 0), (0, HEAD_PAD - NDEV)))
    g_w_mod = jnp.stack([_mm([(silu_t, jnp.pad(dmod_cols[:, i], ((0, HEAD_PAD - NDEV), (0, 0))))], "nn", F32, TOKEN_TILE,
                             mod_loc_cols, "mod_bwd") for i in range(depth)])

    ws = (norm_pre, norm_post, w_mod, b_mod, ffn_w_gate, ffn_w_up, ffn_w_down, mla_w_in, mla_q_norm, mla_w_q_up, mla_kv_norm,
          mla_w_kv_up, mla_w_o, dil_w_in, dil_w_o, rel_bias)
    gs = (g_norm_pre, g_norm_post, g_w_mod, g_b_mod, g_gate, g_up, g_down, g_mla_in, g_q_norm, g_q_up, g_kv_norm, g_kv_up,
          g_mla_o, g_dil_in, g_dil_o, g_rel_bias)
    ms = (m_norm_pre, m_norm_post, m_w_mod, m_b_mod, m_ffn_w_gate, m_ffn_w_up, m_ffn_w_down, m_mla_w_in, m_mla_q_norm,
          m_mla_w_q_up, m_mla_kv_norm, m_mla_w_kv_up, m_mla_w_o, m_dil_w_in, m_dil_w_o, m_rel_bias)
    vs = (v_norm_pre, v_norm_post, v_w_mod, v_b_mod, v_ffn_w_gate, v_ffn_w_up, v_ffn_w_down, v_mla_w_in, v_mla_q_norm,
          v_mla_w_q_up, v_mla_kv_norm, v_mla_w_kv_up, v_mla_w_o, v_dil_w_in, v_dil_w_o, v_rel_bias)
    stepped = [_adamw(w, g, m, v, "adamw") for w, g, m, v in zip(ws, gs, ms, vs)]
    deltas, new_m, new_v = zip(*stepped)
    return (loss, grad_x, *gs, *deltas, *new_m, *new_v)
```

```python
import math

import jax
import jax.numpy as jnp
from jax import lax
from jax.experimental import pallas as pl
from jax.experimental.pallas import tpu as pltpu

F32 = jnp.float32
BF16 = jnp.bfloat16
MESH = pl.DeviceIdType.MESH

NDEV = 8
OTHER_CHIPS = 3
D_MODEL = 1024
SEQ = 2048
D_FF = 2816
EPS = 1e-6
FFN_RES = 0.5

MLA_HEADS = 16
Q_LORA = 384
KV_LORA = 256
QK_NOPE = 64
QK_ROPE = 32
V_HEAD = 64
ROPE_THETA = 10000.0
HEAD_PAD = 128
LAT_PAD = Q_LORA + KV_LORA + HEAD_PAD
MLA_SCALE = (QK_NOPE + QK_ROPE) ** -0.5

DIL_GROUPS = ((128, 1), (512, 4), (2048, 16))
DIL_HEADS = 16
DIL_HEAD_DIM = 64
DIL_BLOCK = 128
DIL_PAIRS = DIL_HEADS // 2
DIL_SCALE = DIL_HEAD_DIM ** -0.5
DIL_GROUPED = 4
N_BUCKETS = 32
MAX_DISTANCE = 2048

ADAM_LR = 0.001
ADAM_B1 = 0.9
ADAM_B2 = 0.999
ADAM_EPS = 1e-08
ADAM_WD = 0.01
ADAM_STEP = 10

V7X_VMEM_BYTES = 64 * 2**20
VMEM_RESERVE = 10 * 2**20
TOKEN_TILE = 512


def _nbytes(shape, dtype):
    return math.prod(shape) * jnp.dtype(dtype).itemsize


def _params(semantics, blocks, extra=0):
    need = 2 * sum(_nbytes(s, d) for s, d in blocks) + extra + VMEM_RESERVE
    return pltpu.CompilerParams(dimension_semantics=semantics,
                                vmem_limit_bytes=int(min(need, V7X_VMEM_BYTES - VMEM_RESERVE)))


def _pcall(body, out_shape, **kw):
    call = pl.pallas_call(body, out_shape=jax.tree.map(lambda s: pltpu.HBM(s.shape, s.dtype), out_shape), **kw)
    return lambda *args: call(*[pltpu.with_memory_space_constraint(a, pltpu.HBM) for a in args])


def _dot_nn(a, b):
    return lax.dot_general(a, b, (((1,), (0,)), ((), ())), preferred_element_type=F32)


def _dot_nt(a, b):
    return lax.dot_general(a, b, (((1,), (1,)), ((), ())), preferred_element_type=F32)


def _dot_tn(a, b):
    return lax.dot_general(a, b, (((0,), (0,)), ((), ())), preferred_element_type=F32)


_DOTS = {"nn": _dot_nn, "nt": _dot_nt, "tn": _dot_tn}


def _rstd(v):
    return lax.rsqrt(jnp.mean(v * v, axis=-1, keepdims=True) + EPS)


def _rms_bwd(v, r, t):
    return r * t - v * (r * r * r) * jnp.mean(t * v, axis=-1, keepdims=True)


_TOKEN_SPEC = pl.BlockSpec((8, 128), lambda *_: (0, 0))


def _mm(pairs, mode, out_dtype, tm, tn, name, out_perm=1, after=None):
    a0, b0 = pairs[0]
    m_dim = a0.shape[1] if mode == "tn" else a0.shape[0]
    n_dim = b0.shape[0] if mode == "nt" else b0.shape[1]
    tm, tn = min(tm, m_dim // out_perm), min(tn, n_dim)
    assert m_dim % tm == 0 and n_dim % tn == 0, (name, m_dim, n_dim, tm, tn)
    dot = _DOTS[mode]
    npairs = len(pairs)

    def body(*refs):
        acc = None
        for p in range(npairs):
            d = dot(refs[2 * p][...].astype(BF16), refs[2 * p + 1][...].astype(BF16))
            acc = d if acc is None else acc + d
        refs[-1][...] = acc.astype(out_dtype)

    in_specs, blocks, flat = [], [], []
    for a, b in pairs:
        if mode == "nn":
            k = a.shape[1]
            sa, sb = ((tm, k), lambda i, j: (i, 0)), ((k, tn), lambda i, j: (0, j))
        elif mode == "nt":
            k = a.shape[1]
            sa, sb = ((tm, k), lambda i, j: (i, 0)), ((tn, k), lambda i, j: (j, 0))
        else:
            k = a.shape[0]
            sa, sb = ((k, tm), lambda i, j: (0, i)), ((k, tn), lambda i, j: (0, j))
        in_specs += [pl.BlockSpec(*sa), pl.BlockSpec(*sb)]
        blocks += [(sa[0], a.dtype), (sb[0], b.dtype)]
        flat += [a, b]
    if after is not None:
        in_specs.append(_TOKEN_SPEC)
        flat.append(after)
    if out_perm == 1:
        out_shape = (m_dim, n_dim)
        out_spec = pl.BlockSpec((tm, tn), lambda i, j: (i, j))
    else:
        rows = m_dim // out_perm
        assert tn == n_dim and rows % tm == 0, (name, rows, tm)
        nb = rows // tm
        out_shape = (rows, out_perm * n_dim)
        out_spec = pl.BlockSpec((tm, n_dim), lambda i, j: (i % nb, i // nb))
    blocks.append(((tm, tn), out_dtype))
    res = _pcall(
        body, out_shape=jax.ShapeDtypeStruct(out_shape, out_dtype), grid=(m_dim // tm, n_dim // tn),
        in_specs=in_specs, out_specs=out_spec, name=name,
        compiler_params=_params(("parallel", "parallel"), blocks, extra=2 * tm * tn * 4),
    )(*flat)
    return res.reshape(m_dim, n_dim)


def _prenorm_mm(x, pre_g, scale, shift, w, w_mode, out_dtype, tn, name, perm=1):
    s_dim, d_dim = x.shape
    n_dim = w.shape[0] if w_mode == "nt" else w.shape[1]
    rows = s_dim // perm
    tm = min(TOKEN_TILE, rows)
    nb = rows // tm
    tn = min(tn, n_dim)
    assert n_dim % tn == 0
    dot = _DOTS[w_mode]

    def body(x_ref, g_ref, sc_ref, sh_ref, w_ref, hn_ref, o_ref):
        @pl.when(pl.program_id(1) == 0)
        def _():
            xf = x_ref[...]
            hn = (xf * _rstd(xf) * g_ref[...]) * (1.0 + sc_ref[...]) + sh_ref[...]
            hn_ref[...] = hn.astype(BF16)

        o_ref[...] = dot(hn_ref[...], w_ref[...]).astype(out_dtype)

    vec = pl.BlockSpec((1, d_dim), lambda i, j: (0, 0))
    w_block = (tn, d_dim) if w_mode == "nt" else (d_dim, tn)
    w_spec = pl.BlockSpec(w_block, (lambda i, j: (j, 0)) if w_mode == "nt" else (lambda i, j: (0, j)))
    hn, out = _pcall(
        body,
        out_shape=(jax.ShapeDtypeStruct((s_dim, d_dim), BF16), jax.ShapeDtypeStruct((s_dim, n_dim), out_dtype)),
        grid=(s_dim // tm, n_dim // tn),
        in_specs=[pl.BlockSpec((tm, d_dim), lambda i, j: (i % nb, i // nb)), vec, vec, vec, w_spec],
        out_specs=(pl.BlockSpec((tm, d_dim), lambda i, j: (i, 0)), pl.BlockSpec((tm, tn), lambda i, j: (i, j))),
        name=name,
        compiler_params=_params(("parallel", "arbitrary"),
                                [((tm, d_dim), F32), (w_block, BF16), ((tm, d_dim), BF16), ((tm, tn), out_dtype)],
                                extra=3 * tm * d_dim * 4 + tm * tn * 4),
    )(x.reshape(rows, perm * d_dim), pre_g, scale, shift, w)
    return hn, out


def _ffn_up(x, pre_g, scale, shift, wg_t, wu_t, name):
    s_dim, d_dim = x.shape
    f_dim = wg_t.shape[0]
    tm, tn = TOKEN_TILE, f_dim // 2

    def body(x_ref, g_ref, sc_ref, sh_ref, wg_ref, wu_ref, hn_ref, go_ref, uo_ref, a_ref):
        @pl.when(pl.program_id(1) == 0)
        def _():
            xf = x_ref[...]
            hn = (xf * _rstd(xf) * g_ref[...]) * (1.0 + sc_ref[...]) + sh_ref[...]
            hn_ref[...] = hn.astype(BF16)

        hn = hn_ref[...]
        g = _dot_nt(hn, wg_ref[...])
        u = _dot_nt(hn, wu_ref[...])
        go_ref[...] = g.astype(BF16)
        uo_ref[...] = u.astype(BF16)
        a_ref[...] = (g * jax.nn.sigmoid(g) * u).astype(BF16)

    vec = pl.BlockSpec((1, d_dim), lambda i, j: (0, 0))
    w_spec = pl.BlockSpec((tn, d_dim), lambda i, j: (j, 0))
    act = pl.BlockSpec((tm, tn), lambda i, j: (i, j))
    act_shape = jax.ShapeDtypeStruct((s_dim, f_dim), BF16)
    return _pcall(
        body,
        out_shape=(jax.ShapeDtypeStruct((s_dim, d_dim), BF16), act_shape, act_shape, act_shape),
        grid=(s_dim // tm, f_dim // tn),
        in_specs=[pl.BlockSpec((tm, d_dim), lambda i, j: (i, 0)), vec, vec, vec, w_spec, w_spec],
        out_specs=(pl.BlockSpec((tm, d_dim), lambda i, j: (i, 0)), act, act, act),
        name=name,
        compiler_params=_params(("parallel", "arbitrary"),
                                [((tm, d_dim), F32), ((tn, d_dim), BF16), ((tn, d_dim), BF16), ((tm, d_dim), BF16)]
                                + 3 * [((tm, tn), BF16)], extra=3 * tm * d_dim * 4 + 4 * tm * tn * 4),
    )(x, pre_g, scale, shift, wg_t, wu_t)


def _mm_post(a, w, x, post_g, gate, res_w, name):
    s_dim, k_dim = a.shape
    d_dim = w.shape[1]
    tm = TOKEN_TILE

    def body(a_ref, w_ref, x_ref, pg_ref, gt_ref, xo_ref, f_ref):
        f = _dot_nn(a_ref[...], w_ref[...])
        y = f * _rstd(f) * pg_ref[...]
        f_ref[...] = f
        xo_ref[...] = x_ref[...] + (res_w * gt_ref[...]) * y

    vec = pl.BlockSpec((1, d_dim), lambda i: (0, 0))
    row = pl.BlockSpec((tm, d_dim), lambda i: (i, 0))
    out = jax.ShapeDtypeStruct((s_dim, d_dim), F32)
    return _pcall(
        body, out_shape=(out, out), grid=(s_dim // tm,),
        in_specs=[pl.BlockSpec((tm, k_dim), lambda i: (i, 0)), pl.BlockSpec((k_dim, d_dim), lambda i: (0, 0)), row, vec, vec],
        out_specs=(row, row), name=name,
        compiler_params=_params(("parallel",), [((tm, k_dim), BF16), ((k_dim, d_dim), BF16)] + 3 * [((tm, d_dim), F32)],
                                extra=3 * tm * d_dim * 4),
    )(a, w, x, post_g, gate)


def _post_bwd(dx_out, f, post_g, gate, res_w, name):
    s_dim, d_dim = f.shape
    tm = TOKEN_TILE

    def body(dx_ref, f_ref, pg_ref, gt_ref, df_ref, dgate_ref, dpost_ref):
        @pl.when(pl.program_id(0) == 0)
        def _():
            dgate_ref[...] = jnp.zeros_like(dgate_ref)
            dpost_ref[...] = jnp.zeros_like(dpost_ref)

        dx, fv = dx_ref[...], f_ref[...]
        r = _rstd(fv)
        fr = fv * r
        dgate_ref[...] += res_w * jnp.sum(dx * (fr * pg_ref[...]), axis=0, keepdims=True)
        dy = (res_w * gt_ref[...]) * dx
        dpost_ref[...] += jnp.sum(dy * fr, axis=0, keepdims=True)
        df_ref[...] = _rms_bwd(fv, r, dy * pg_ref[...]).astype(BF16)

    vec = pl.BlockSpec((1, d_dim), lambda i: (0, 0))
    row = pl.BlockSpec((tm, d_dim), lambda i: (i, 0))
    vshape = jax.ShapeDtypeStruct((1, d_dim), F32)
    return _pcall(
        body, out_shape=(jax.ShapeDtypeStruct((s_dim, d_dim), BF16), vshape, vshape), grid=(s_dim // tm,),
        in_specs=[row, row, vec, vec], out_specs=(row, vec, vec), name=name,
        compiler_params=_params(("arbitrary",), 3 * [((tm, d_dim), F32)], extra=6 * tm * d_dim * 4),
    )(dx_out, f, post_g, gate)


def _prenorm_bwd(dx_out, dhns, x, pre_g, scale, name):
    s_dim, d_dim = x.shape
    tm = TOKEN_TILE
    n_in = len(dhns)

    def body(*refs):
        dx_ref, x_ref, pg_ref, sc_ref = refs[n_in + 0], refs[n_in + 1], refs[n_in + 2], refs[n_in + 3]
        dxo_ref, dsh_ref, dsc_ref, dpg_ref = refs[n_in + 4:]

        @pl.when(pl.program_id(0) == 0)
        def _():
            dsh_ref[...] = jnp.zeros_like(dsh_ref)
            dsc_ref[...] = jnp.zeros_like(dsc_ref)
            dpg_ref[...] = jnp.zeros_like(dpg_ref)

        dhn = refs[0][...]
        for k in range(1, n_in):
            dhn = dhn + refs[k][...]
        xv = x_ref[...]
        r = _rstd(xv)
        xr = xv * r
        dsh_ref[...] += jnp.sum(dhn, axis=0, keepdims=True)
        dsc_ref[...] += jnp.sum(dhn * (xr * pg_ref[...]), axis=0, keepdims=True)
        dn = dhn * (1.0 + sc_ref[...])
        dpg_ref[...] += jnp.sum(dn * xr, axis=0, keepdims=True)
        dxo_ref[...] = dx_ref[...] + _rms_bwd(xv, r, dn * pg_ref[...])

    vec = pl.BlockSpec((1, d_dim), lambda i: (0, 0))
    row = pl.BlockSpec((tm, d_dim), lambda i: (i, 0))
    vshape = jax.ShapeDtypeStruct((1, d_dim), F32)
    return _pcall(
        body, out_shape=(jax.ShapeDtypeStruct((s_dim, d_dim), F32), vshape, vshape, vshape), grid=(s_dim // tm,),
        in_specs=n_in * [row] + [row, row, vec, vec], out_specs=(row, vec, vec, vec), name=name,
        compiler_params=_params(("arbitrary",), (n_in + 3) * [((tm, d_dim), F32)], extra=6 * tm * d_dim * 4),
    )(*dhns, dx_out, x, pre_g, scale)


def _ffn_dgu(df, wd, g, u, name, after=None):
    s_dim, d_dim = df.shape
    f_dim = wd.shape[0]
    tm, tn = TOKEN_TILE, f_dim // 2

    def body(df_ref, wd_ref, g_ref, u_ref, *rest):
        dg_ref, du_ref = rest[-2:]
        da = _dot_nt(df_ref[...], wd_ref[...])
        gv, uv = g_ref[...].astype(F32), u_ref[...].astype(F32)
        sg = jax.nn.sigmoid(gv)
        du_ref[...] = (da * (gv * sg)).astype(BF16)
        dg_ref[...] = (da * uv * (sg * (1.0 + gv * (1.0 - sg)))).astype(BF16)

    act = pl.BlockSpec((tm, tn), lambda i, j: (i, j))
    act_shape = jax.ShapeDtypeStruct((s_dim, f_dim), BF16)
    token = [] if after is None else [after]
    return _pcall(
        body, out_shape=(act_shape, act_shape), grid=(s_dim // tm, f_dim // tn),
        in_specs=[pl.BlockSpec((tm, d_dim), lambda i, j: (i, 0)), pl.BlockSpec((tn, d_dim), lambda i, j: (j, 0)), act, act]
        + len(token) * [_TOKEN_SPEC],
        out_specs=(act, act), name=name,
        compiler_params=_params(("parallel", "parallel"), [((tm, d_dim), BF16), ((tn, d_dim), BF16)] + 4 * [((tm, tn), BF16)],
                                extra=6 * tm * tn * 4),
    )(df, wd, g, u, *token)


def _rope_tables(zero=0.0):
    half = QK_ROPE // 2
    freqs = ROPE_THETA ** (-jnp.arange(half, dtype=F32) / half)
    ang = (jnp.arange(SEQ, dtype=F32)[:, None] + zero) * freqs[None, :]
    cos, sin = jnp.cos(ang), jnp.sin(ang)
    ones = jnp.ones((SEQ, QK_NOPE), F32)
    zeros = jnp.zeros((SEQ, QK_NOPE), F32)
    pad1 = jnp.ones((SEQ, HEAD_PAD - QK_NOPE - QK_ROPE), F32)
    pad0 = jnp.zeros((SEQ, HEAD_PAD - QK_NOPE - QK_ROPE), F32)
    zh = jnp.zeros((SEQ, half), F32)
    c = jnp.concatenate([ones, cos, cos, pad1], axis=1)
    s1 = jnp.concatenate([zeros, -sin, zh, pad0], axis=1)
    s2 = jnp.concatenate([zeros, zh, sin, pad0], axis=1)
    return c, s1, s2


def _rope(v, c, s1, s2):
    half = QK_ROPE // 2
    return v * c + pltpu.roll(v, HEAD_PAD - half, 1) * s1 + pltpu.roll(v, half, 1) * s2


def _rope_t(dv, c, s1, s2):
    half = QK_ROPE // 2
    return dv * c + pltpu.roll(dv * s1, half, 1) + pltpu.roll(dv * s2, HEAD_PAD - half, 1)


def _mla_qkv(lat, q_norm, kv_norm, wq_t, wkv_t, rope, name):
    s_dim = lat.shape[0]
    width = MLA_HEADS * HEAD_PAD
    tm = 256

    def body(lat_ref, qg_ref, kg_ref, wq_ref, wkv_ref, c_ref, s1_ref, s2_ref, q_ref, k_ref, v_ref, qn_ref, kvn_ref):
        cq = lat_ref[:, :Q_LORA]
        ckv = lat_ref[:, Q_LORA:Q_LORA + KV_LORA]
        kr = lat_ref[:, Q_LORA + KV_LORA:]
        c, s1, s2 = c_ref[...], s1_ref[...], s2_ref[...]
        qn = (cq * _rstd(cq) * qg_ref[...]).astype(BF16)
        kvn = (ckv * _rstd(ckv) * kg_ref[...]).astype(BF16)
        qn_ref[...] = qn
        kvn_ref[...] = kvn
        q = _dot_nt(qn, wq_ref[...])
        kv = _dot_nt(kvn, wkv_ref[...])
        krr = _rope(kr, c, s1, s2)
        low = lax.broadcasted_iota(jnp.int32, (tm, HEAD_PAD), 1) < QK_NOPE
        for h in range(MLA_HEADS):
            sl = slice(h * HEAD_PAD, (h + 1) * HEAD_PAD)
            q_ref[:, sl] = _rope(q[:, sl], c, s1, s2).astype(BF16)
            kvh = kv[:, sl]
            k_ref[:, sl] = (jnp.where(low, kvh, 0.0) + krr).astype(BF16)
            v_ref[:, sl] = jnp.where(low, 0.0, kvh).astype(BF16)

    row = lambda n: pl.BlockSpec((tm, n), lambda i: (i, 0))
    full = lambda a: pl.BlockSpec(a.shape, lambda i: (0, 0))
    wide = jax.ShapeDtypeStruct((s_dim, width), BF16)
    return _pcall(
        body,
        out_shape=(wide, wide, wide, jax.ShapeDtypeStruct((s_dim, Q_LORA), BF16), jax.ShapeDtypeStruct((s_dim, KV_LORA), BF16)),
        grid=(s_dim // tm,),
        in_specs=[row(LAT_PAD), full(q_norm), full(kv_norm), full(wq_t), full(wkv_t), row(HEAD_PAD), row(HEAD_PAD), row(HEAD_PAD)],
        out_specs=(row(width), row(width), row(width), row(Q_LORA), row(KV_LORA)), name=name,
        compiler_params=_params(("parallel",), [((tm, LAT_PAD), F32), (wq_t.shape, BF16), (wkv_t.shape, BF16)]
                                + 3 * [((tm, width), BF16)], extra=4 * tm * width * 4),
    )(lat, q_norm, kv_norm, wq_t, wkv_t, *rope)


def _mla_scores(q, k_ref, t, tq):
    lo = t * tq
    own = slice(lo, lo + tq)
    scores = [(_dot_nt(q, k_ref[own, :]), own)]
    if t > 0:
        scores.append((_dot_nt(q, k_ref[0:lo, :]), slice(0, lo)))
    return scores


def _mla_softmax(scores):
    s_own = scores[0][0] * MLA_SCALE
    rows = lax.broadcasted_iota(jnp.int32, s_own.shape, 0)
    cols = lax.broadcasted_iota(jnp.int32, s_own.shape, 1)
    s_own = jnp.where(cols <= rows, s_own, -jnp.inf)
    mx = jnp.max(s_own, axis=-1, keepdims=True)
    if len(scores) == 1:
        e_own = jnp.exp(s_own - mx)
        return [(e_own * (1.0 / jnp.sum(e_own, axis=-1, keepdims=True)), scores[0][1])]
    s_pre = scores[1][0] * MLA_SCALE
    mx = jnp.maximum(mx, jnp.max(s_pre, axis=-1, keepdims=True))
    e_own, e_pre = jnp.exp(s_own - mx), jnp.exp(s_pre - mx)
    inv = 1.0 / (jnp.sum(e_own, axis=-1, keepdims=True) + jnp.sum(e_pre, axis=-1, keepdims=True))
    return [(e_pre * inv, scores[1][1]), (e_own * inv, scores[0][1])]


def _mla_attn_fwd(q, k, v, name):
    s_dim = q.shape[0]
    tq = 512

    def body(q_ref, k_ref, v_ref, o_ref):
        n_tiles = s_dim // tq
        tile_of = lambda t: slice(t * tq, (t + 1) * tq)
        def weighted_values(t, probs):
            o = None
            for p, keys in probs:
                part = _dot_nn(p, v_ref[keys, :])
                o = part if o is None else o + part
            o_ref[tile_of(t), :] = o.astype(BF16)

        scores = _mla_scores(q_ref[tile_of(0), :], k_ref, 0, tq)
        probs = None
        for t in range(n_tiles):
            ahead = _mla_scores(q_ref[tile_of(t + 1), :], k_ref, t + 1, tq) if t + 1 < n_tiles else None
            if probs is not None:
                weighted_values(t - 1, probs)
            probs = [(p.astype(BF16), keys) for p, keys in _mla_softmax(scores)]
            scores = ahead
        weighted_values(n_tiles - 1, probs)

    head = pl.BlockSpec((s_dim, HEAD_PAD), lambda h: (0, h))
    return _pcall(
        body, out_shape=jax.ShapeDtypeStruct(q.shape, BF16), grid=(MLA_HEADS,),
        in_specs=[head, head, head], out_specs=head, name=name,
        compiler_params=_params(("parallel",), 4 * [((s_dim, HEAD_PAD), BF16)], extra=4 * tq * s_dim * 4),
    )(q, k, v)


def _mla_attn_bwd(q, k, v, d_o, name):
    s_dim = q.shape[0]
    tq = 512

    def body(q_ref, k_ref, v_ref, do_ref, dq_ref, dk_ref, dv_ref):
        dk_ref[...] = jnp.zeros_like(dk_ref)
        dv_ref[...] = jnp.zeros_like(dv_ref)
        n_tiles = s_dim // tq
        tile_of = lambda t: slice(t * tq, (t + 1) * tq)

        def products(t):
            scores = _mla_scores(q_ref[tile_of(t), :], k_ref, t, tq)
            dot = do_ref[tile_of(t), :].astype(BF16)
            return scores, [_dot_nt(dot, v_ref[keys, :]) for _, keys in scores]

        def gradients_of_scores(scores, dps):
            probs = _mla_softmax(scores)
            dp_of = {(keys.start, keys.stop): dp for (_, keys), dp in zip(scores, dps)}
            terms = [(p, keys, dp_of[keys.start, keys.stop]) for p, keys in probs]
            row = None
            for p, _, dp in terms:
                part = jnp.sum(p * dp, axis=-1, keepdims=True)
                row = part if row is None else row + part
            return [((p * (dp - row) * MLA_SCALE).astype(BF16), p.astype(BF16), keys) for p, keys, dp in terms]

        def accumulate(t, terms):
            qt = q_ref[tile_of(t), :]
            dot = do_ref[tile_of(t), :].astype(BF16)
            dq = None
            for dsb, pb, keys in terms:
                part = _dot_nn(dsb, k_ref[keys, :])
                dq = part if dq is None else dq + part
                dk_ref[keys, :] += _dot_tn(dsb, qt)
                dv_ref[keys, :] += _dot_tn(pb, dot)
            dq_ref[tile_of(t), :] = dq

        ready = products(0)
        terms = None
        for t in range(n_tiles):
            ahead = products(t + 1) if t + 1 < n_tiles else None
            if terms is not None:
                accumulate(t - 1, terms)
            terms = gradients_of_scores(*ready)
            ready = ahead
        accumulate(n_tiles - 1, terms)

    head = pl.BlockSpec((s_dim, HEAD_PAD), lambda h: (0, h))
    out = jax.ShapeDtypeStruct(q.shape, F32)
    return _pcall(
        body, out_shape=(out, out, out), grid=(MLA_HEADS,),
        in_specs=[head, head, head, head], out_specs=(head, head, head), name=name,
        compiler_params=_params(("parallel",), 3 * [((s_dim, HEAD_PAD), BF16)] + 4 * [((s_dim, HEAD_PAD), F32)],
                                extra=6 * tq * s_dim * 4),
    )(q, k, v, d_o)


def _mla_qkv_bwd(dq, dk, dv, lat, q_norm, kv_norm, wq_t, wkv_t, rope, name):
    s_dim = lat.shape[0]
    width = MLA_HEADS * HEAD_PAD
    tm = 256

    def body(dq_ref, dk_ref, dv_ref, lat_ref, qg_ref, kg_ref, wq_ref, wkv_ref, c_ref, s1_ref, s2_ref,
             dqp_ref, dkv_ref, dlat_ref, dqg_ref, dkg_ref):
        @pl.when(pl.program_id(0) == 0)
        def _():
            dqg_ref[...] = jnp.zeros_like(dqg_ref)
            dkg_ref[...] = jnp.zeros_like(dkg_ref)

        c, s1, s2 = c_ref[...], s1_ref[...], s2_ref[...]
        lane = lax.broadcasted_iota(jnp.int32, (tm, HEAD_PAD), 1)
        low = lane < QK_NOPE
        rot = (lane >= QK_NOPE) & (lane < QK_NOPE + QK_ROPE)
        dkrr = jnp.zeros((tm, HEAD_PAD), F32)
        for h in range(MLA_HEADS):
            sl = slice(h * HEAD_PAD, (h + 1) * HEAD_PAD)
            dqp_ref[:, sl] = _rope_t(dq_ref[:, sl], c, s1, s2).astype(BF16)
            dkh = dk_ref[:, sl]
            dkv_ref[:, sl] = jnp.where(low, dkh, dv_ref[:, sl]).astype(BF16)
            dkrr = dkrr + jnp.where(rot, dkh, 0.0)
        dqn = _dot_nn(dqp_ref[...], wq_ref[...])
        dkvn = _dot_nn(dkv_ref[...], wkv_ref[...])
        cq = lat_ref[:, :Q_LORA]
        ckv = lat_ref[:, Q_LORA:Q_LORA + KV_LORA]
        rq, rkv = _rstd(cq), _rstd(ckv)
        dqg_ref[...] += jnp.sum(dqn * cq * rq, axis=0, keepdims=True)
        dkg_ref[...] += jnp.sum(dkvn * ckv * rkv, axis=0, keepdims=True)
        dlat_ref[:, :Q_LORA] = _rms_bwd(cq, rq, dqn * qg_ref[...])
        dlat_ref[:, Q_LORA:Q_LORA + KV_LORA] = _rms_bwd(ckv, rkv, dkvn * kg_ref[...])
        dlat_ref[:, Q_LORA + KV_LORA:] = _rope_t(dkrr, c, s1, s2)

    row = lambda n: pl.BlockSpec((tm, n), lambda i: (i, 0))
    full = lambda a: pl.BlockSpec(a.shape, lambda i: (0, 0))
    wide = jax.ShapeDtypeStruct((s_dim, width), BF16)
    return _pcall(
        body,
        out_shape=(wide, wide, jax.ShapeDtypeStruct((s_dim, LAT_PAD), F32),
                   jax.ShapeDtypeStruct(q_norm.shape, F32), jax.ShapeDtypeStruct(kv_norm.shape, F32)),
        grid=(s_dim // tm,),
        in_specs=[row(width), row(width), row(width), row(LAT_PAD), full(q_norm), full(kv_norm), full(wq_t), full(wkv_t),
                  row(HEAD_PAD), row(HEAD_PAD), row(HEAD_PAD)],
        out_specs=(row(width), row(width), row(LAT_PAD), full(q_norm), full(kv_norm)), name=name,
        compiler_params=_params(("arbitrary",), 3 * [((tm, width), F32)] + [((tm, LAT_PAD), F32), (wq_t.shape, BF16),
                                                                           (wkv_t.shape, BF16)] + 2 * [((tm, width), BF16)],
                                extra=2 * tm * width * 4),
    )(dq, dk, dv, lat, q_norm, kv_norm, wq_t, wkv_t, *rope)


def _t5_bucket(dist):
    max_exact = N_BUCKETS // 2
    d = jnp.maximum(dist, 1).astype(F32)
    large = max_exact + (jnp.log(d / max_exact) / math.log(MAX_DISTANCE / max_exact)
                         * (N_BUCKETS - max_exact)).astype(jnp.int32)
    large = jnp.minimum(large, N_BUCKETS - 1)
    return jnp.where(dist < max_exact, dist, large)


def _dil_buckets(dilation):
    iq = jnp.arange(DIL_BLOCK)[:, None]
    ik = jnp.arange(2 * DIL_BLOCK)[None, :]
    return _t5_bucket(jnp.maximum(DIL_BLOCK + iq - ik, 0) * dilation)


def _dil_logits(qh, kb, bias_h, first, span):
    if first:
        s = _dot_nt(qh, kb) * DIL_SCALE + bias_h[:, DIL_BLOCK:]
        rel = lax.broadcasted_iota(jnp.int32, s.shape, 0) - lax.broadcasted_iota(jnp.int32, s.shape, 1)
    else:
        s = _dot_nt(qh, kb) * DIL_SCALE + bias_h
        rel = DIL_BLOCK + lax.broadcasted_iota(jnp.int32, s.shape, 0) - lax.broadcasted_iota(jnp.int32, s.shape, 1)
    return jnp.where((rel >= 0) & (rel <= span), s, -jnp.inf)


def _dil_blocks(s_dim, dilation):
    rows = s_dim // dilation
    for r in range(dilation):
        for n in range(rows // DIL_BLOCK):
            lo = r * rows + n * DIL_BLOCK
            keys = slice(lo, lo + DIL_BLOCK) if n == 0 else slice(lo - DIL_BLOCK, lo + DIL_BLOCK)
            start = r + n * DIL_BLOCK * dilation
            tokens = slice(start, start + DIL_BLOCK) if dilation == 1 else pl.ds(start, DIL_BLOCK, stride=dilation)
            yield n == 0, slice(lo, lo + DIL_BLOCK), keys, tokens


def _dil_views(s_dim):
    col = lambda which: pl.BlockSpec((s_dim, HEAD_PAD), lambda p: (0, which * DIL_PAIRS + p))
    nat = pl.BlockSpec((s_dim, HEAD_PAD), lambda p: (0, p))
    bias = pl.BlockSpec((2, DIL_BLOCK, 2 * DIL_BLOCK), lambda p: (p, 0, 0))
    return col, nat, bias


def _dil_attn_fwd(qkv, bias, dilation, span, name):
    s_dim = qkv.shape[0]
    d_dim = DIL_HEADS * DIL_HEAD_DIM
    col, nat, bias_spec = _dil_views(s_dim)

    def body(q_ref, k_ref, v_ref, b_ref, o_ref, l_ref):
        lane = lax.broadcasted_iota(jnp.int32, (DIL_BLOCK, HEAD_PAD), 1)
        klane = lax.broadcasted_iota(jnp.int32, (2 * DIL_BLOCK, HEAD_PAD), 1)
        blocks = list(_dil_blocks(s_dim, dilation))
        for g0 in range(0, len(blocks), DIL_GROUPED):
            group = blocks[g0:g0 + DIL_GROUPED]
            logits = [_dil_logits(jnp.where((lane < DIL_HEAD_DIM) == (h == 0), q_ref[blk, :], 0), k_ref[keys, :], b_ref[h],
                                  first, span) for first, blk, keys, _ in group for h in range(2)]
            soft = []
            for lg in logits:
                mx = jnp.max(lg, axis=-1, keepdims=True)
                e = jnp.exp(lg - mx)
                tot = jnp.sum(e, axis=-1, keepdims=True)
                soft.append(((e * (1.0 / tot)).astype(BF16), mx + jnp.log(tot)))
            for i, (_, _, keys, tokens) in enumerate(group):
                vb = v_ref[keys, :]
                o_acc = jnp.zeros((DIL_BLOCK, HEAD_PAD), F32)
                lse_acc = jnp.zeros((DIL_BLOCK, HEAD_PAD), F32)
                for h in range(2):
                    p, lse = soft[2 * i + h]
                    kmine = (klane[:vb.shape[0]] < DIL_HEAD_DIM) == (h == 0)
                    o_acc = o_acc + _dot_nn(p, jnp.where(kmine, vb, 0))
                    lse_acc = jnp.where((lane < DIL_HEAD_DIM) == (h == 0), lse, lse_acc)
                o_ref[tokens, :] = o_acc
                l_ref[tokens, :] = lse_acc

    out = jax.ShapeDtypeStruct((s_dim, d_dim), F32)
    return _pcall(
        body, out_shape=(out, out), grid=(DIL_PAIRS,),
        in_specs=[col(0), col(1), col(2), bias_spec], out_specs=(nat, nat), name=name,
        compiler_params=_params(("parallel",), 3 * [((s_dim, HEAD_PAD), BF16)] + 2 * [((s_dim, HEAD_PAD), F32)]
                                + [((2, DIL_BLOCK, 2 * DIL_BLOCK), F32)], extra=2**21),
    )(qkv, qkv, qkv, bias)


def _dil_mix(lses, outs, name):
    s_dim, d_dim = outs[0].shape
    tm = TOKEN_TILE
    ng = len(outs)

    def body(*refs):
        ls = [refs[g][...] for g in range(ng)]
        mx = ls[0]
        for g in range(1, ng):
            mx = jnp.maximum(mx, ls[g])
        es = [jnp.exp(l - mx) for l in ls]
        tot = es[0]
        for g in range(1, ng):
            tot = tot + es[g]
        o = None
        for g in range(ng):
            al = es[g] / tot
            refs[2 * ng + g][...] = al
            t = al * refs[ng + g][...]
            o = t if o is None else o + t
        refs[3 * ng][...] = o
        refs[3 * ng + 1][...] = o.astype(BF16)

    row = pl.BlockSpec((tm, d_dim), lambda i: (i, 0))
    f = jax.ShapeDtypeStruct((s_dim, d_dim), F32)
    res = _pcall(
        body, out_shape=tuple(ng * [f] + [f, jax.ShapeDtypeStruct((s_dim, d_dim), BF16)]), grid=(s_dim // tm,),
        in_specs=2 * ng * [row], out_specs=tuple((ng + 2) * [row]), name=name,
        compiler_params=_params(("parallel",), (3 * ng + 2) * [((tm, d_dim), F32)], extra=4 * tm * d_dim * 4),
    )(*lses, *outs)
    return res[:ng], res[ng], res[ng + 1]


def _dil_attn_bwd(qkv, bias, d_o, o_mix, alpha, lse, dilation, span, name):
    s_dim = qkv.shape[0]
    d_dim = DIL_HEADS * DIL_HEAD_DIM
    col, nat, bias_spec = _dil_views(s_dim)

    def body(q_ref, k_ref, v_ref, b_ref, do_ref, om_ref, al_ref, l_ref, dq_ref, dk_ref, dv_ref, db_ref, dk_acc, dv_acc):
        db_ref[...] = jnp.zeros_like(db_ref)
        dk_acc[...] = jnp.zeros_like(dk_acc)
        dv_acc[...] = jnp.zeros_like(dv_acc)
        lane = lax.broadcasted_iota(jnp.int32, (DIL_BLOCK, HEAD_PAD), 1)
        klane = lax.broadcasted_iota(jnp.int32, (2 * DIL_BLOCK, HEAD_PAD), 1)
        blocks = list(_dil_blocks(s_dim, dilation))
        heads = [(lane < DIL_HEAD_DIM) == (h == 0) for h in range(2)]
        for g0 in range(0, len(blocks), DIL_GROUPED):
            group = blocks[g0:g0 + DIL_GROUPED]
            staged = []
            for first, blk, kv_rows, tokens in group:
                qb, kb, vb = q_ref[blk, :], k_ref[kv_rows, :], v_ref[kv_rows, :]
                dog = al_ref[tokens, :] * do_ref[tokens, :]
                row_term = dog * om_ref[tokens, :]
                lse_b = l_ref[tokens, :]
                for h in range(2):
                    qh = jnp.where(heads[h], qb, 0)
                    dogh = jnp.where(heads[h], dog, 0.0).astype(BF16)
                    staged.append((_dil_logits(qh, kb, b_ref[h], first, span), _dot_nt(dogh, vb), qh, dogh,
                                   jnp.max(jnp.where(heads[h], lse_b, -jnp.inf), axis=-1, keepdims=True),
                                   jnp.sum(jnp.where(heads[h], row_term, 0.0), axis=-1, keepdims=True)))
            grads = []
            for i, (logits, dp, qh, dogh, lse_h, row) in enumerate(staged):
                p = jnp.exp(logits - lse_h)
                ds = p * (dp - row)
                if group[i // 2][0]:
                    db_ref[i % 2, :, DIL_BLOCK:] += ds
                else:
                    db_ref[i % 2] += ds
                grads.append(((ds * DIL_SCALE).astype(BF16), p.astype(BF16), qh, dogh))
            for i, (_, blk, kv_rows, _) in enumerate(group):
                kb = k_ref[kv_rows, :]
                dq_acc = jnp.zeros((DIL_BLOCK, HEAD_PAD), F32)
                dk_blk = jnp.zeros((kb.shape[0], HEAD_PAD), F32)
                dv_blk = jnp.zeros((kb.shape[0], HEAD_PAD), F32)
                for h in range(2):
                    dsb, pb, qh, dogh = grads[2 * i + h]
                    kmine = (klane[:kb.shape[0]] < DIL_HEAD_DIM) == (h == 0)
                    dq_acc = dq_acc + _dot_nn(dsb, jnp.where(kmine, kb, 0))
                    dk_blk = dk_blk + _dot_tn(dsb, qh)
                    dv_blk = dv_blk + _dot_tn(pb, dogh)
                dq_ref[blk, :] = dq_acc.astype(BF16)
                dk_acc[kv_rows, :] += dk_blk
                dv_acc[kv_rows, :] += dv_blk
        dk_ref[...] = dk_acc[...].astype(BF16)
        dv_ref[...] = dv_acc[...].astype(BF16)

    grad = jax.ShapeDtypeStruct((s_dim, d_dim), BF16)
    return _pcall(
        body, out_shape=(grad, grad, grad, jax.ShapeDtypeStruct(bias.shape, F32)), grid=(DIL_PAIRS,),
        in_specs=[col(0), col(1), col(2), bias_spec, nat, nat, nat, nat],
        out_specs=(nat, nat, nat, bias_spec), name=name,
        scratch_shapes=[pltpu.VMEM((s_dim, HEAD_PAD), F32), pltpu.VMEM((s_dim, HEAD_PAD), F32)],
        compiler_params=_params(("parallel",), 6 * [((s_dim, HEAD_PAD), BF16)] + 4 * [((s_dim, HEAD_PAD), F32)]
                                + 2 * [((2, DIL_BLOCK, 2 * DIL_BLOCK), F32)], extra=2 * s_dim * HEAD_PAD * 4 + 2**21),
    )(qkv, qkv, qkv, bias, d_o, o_mix, alpha, lse)


def _bias_reduce(dbias, buckets, name):
    n_heads = dbias.shape[0]

    def body(db_ref, bk_ref, o_ref):
        ds, bk = db_ref[0], bk_ref[0]
        lane = lax.broadcasted_iota(jnp.int32, (8, HEAD_PAD), 1)
        acc = jnp.zeros((8, HEAD_PAD), F32)
        for b in range(N_BUCKETS):
            acc = jnp.where(lane == b, jnp.sum(jnp.where(bk == b, ds, 0.0)), acc)
        o_ref[0] = acc

    blk = (1, DIL_BLOCK, 2 * DIL_BLOCK)
    return _pcall(
        body, out_shape=jax.ShapeDtypeStruct((n_heads, 8, HEAD_PAD), F32), grid=(n_heads,),
        in_specs=[pl.BlockSpec(blk, lambda h: (h, 0, 0)), pl.BlockSpec(blk, lambda h: (h // DIL_HEADS, 0, 0))],
        out_specs=pl.BlockSpec((1, 8, HEAD_PAD), lambda h: (h, 0, 0)), name=name,
        compiler_params=_params(("parallel",), [(blk, F32), (blk, jnp.int32)], extra=2**20),
    )(dbias, buckets)


def _loss_grad(y, target, name):
    s_dim, d_dim = y.shape
    tm = TOKEN_TILE

    def body(y_ref, t_ref, dy_ref, l_ref):
        @pl.when(pl.program_id(0) == 0)
        def _():
            l_ref[...] = jnp.zeros_like(l_ref)

        err = y_ref[...] - t_ref[...]
        dy_ref[...] = err / d_dim
        sq = (err * err).reshape(tm // 8, 8, d_dim)
        l_ref[...] += 0.5 * jnp.sum(sq, axis=0) / d_dim

    row = pl.BlockSpec((tm, d_dim), lambda i: (i, 0))
    acc = pl.BlockSpec((8, d_dim), lambda i: (0, 0))
    return _pcall(
        body, out_shape=(jax.ShapeDtypeStruct((s_dim, d_dim), F32), jax.ShapeDtypeStruct((8, d_dim), F32)),
        grid=(s_dim // tm,), in_specs=[row, row], out_specs=(row, acc), name=name,
        compiler_params=_params(("arbitrary",), 3 * [((tm, d_dim), F32)], extra=2 * tm * d_dim * 4),
    )(y, target)


def _mod_fwd(c_all, w_mod, b_loc, name):
    depth, d_dim, n = w_mod.shape
    nb = c_all.shape[0]

    def body(c_ref, w_ref, b_ref, o_ref, s_ref):
        cv = c_ref[...]
        sc = cv * jax.nn.sigmoid(cv)
        s_ref[...] = sc
        o_ref[0] = _dot_nn(sc.astype(BF16), w_ref[0].astype(BF16)) + b_ref[0]

    return _pcall(
        body, out_shape=(jax.ShapeDtypeStruct((depth, nb, n), F32), jax.ShapeDtypeStruct((nb, d_dim), F32)), grid=(depth,),
        in_specs=[pl.BlockSpec((nb, d_dim), lambda i: (0, 0)), pl.BlockSpec((1, d_dim, n), lambda i: (i, 0, 0)),
                  pl.BlockSpec((1, 1, n), lambda i: (i, 0, 0))],
        out_specs=(pl.BlockSpec((1, nb, n), lambda i: (i, 0, 0)), pl.BlockSpec((nb, d_dim), lambda i: (0, 0))), name=name,
        compiler_params=_params(("arbitrary",), [((1, d_dim, n), F32)], extra=d_dim * n * 2 + 2**20),
    )(c_all, w_mod, b_loc.reshape(depth, 1, n))


def _sum_parts(parts, name):
    _, rows, cols = parts.shape
    fits = [t for t in range(16, rows // 2 + 1, 16) if rows % t == 0 and NDEV * t * cols * parts.dtype.itemsize <= 3 * 2**20]
    tr = max(fits) if fits else rows

    def body(p_ref, o_ref):
        acc = p_ref[0].astype(F32)
        for k in range(1, NDEV):
            acc = acc + p_ref[k].astype(F32)
        o_ref[...] = acc

    return _pcall(
        body, out_shape=jax.ShapeDtypeStruct((rows, cols), F32), grid=(rows // tr,),
        in_specs=[pl.BlockSpec((NDEV, tr, cols), lambda i: (0, i, 0))], out_specs=pl.BlockSpec((tr, cols), lambda i: (i, 0)),
        name=name, compiler_params=_params(("parallel",), [((NDEV, tr, cols), parts.dtype), ((tr, cols), F32)], extra=2**20),
    )(parts)


def _adamw(w, g, m, v, name):
    shape = w.shape
    cols = shape[-1]
    rows = math.prod(shape[:-1])
    tr = rows
    for cand in (512, 256, 128, 64, 32, 16, 8):
        if rows % cand == 0 and rows > cand and cand * cols * 4 <= 2**21:
            tr = cand
            break

    def body(w_ref, g_ref, m_ref, v_ref, d_ref, mo_ref, vo_ref):
        gv = g_ref[...]
        mn = ADAM_B1 * m_ref[...] + (1.0 - ADAM_B1) * gv
        vn = ADAM_B2 * v_ref[...] + (1.0 - ADAM_B2) * (gv * gv)
        m_hat = mn / (1.0 - ADAM_B1 ** ADAM_STEP)
        v_hat = vn / (1.0 - ADAM_B2 ** ADAM_STEP)
        d_ref[...] = -ADAM_LR * (m_hat / (jnp.sqrt(v_hat) + ADAM_EPS) + ADAM_WD * w_ref[...])
        mo_ref[...] = mn
        vo_ref[...] = vn

    blk = pl.BlockSpec((tr, cols), lambda i: (i, 0))
    out = jax.ShapeDtypeStruct((rows, cols), F32)
    res = _pcall(
        body, out_shape=(out, out, out), grid=(rows // tr,), in_specs=4 * [blk], out_specs=(blk, blk, blk), name=name,
        compiler_params=_params(("parallel",), 7 * [((tr, cols), F32)], extra=4 * tr * cols * 4),
    )(*(a.reshape(rows, cols) for a in (w, g, m, v)))
    return tuple(r.reshape(shape) for r in res)


def _peers():
    x, y, c = lax.axis_index("x"), lax.axis_index("y"), lax.axis_index("c")
    flip = lambda v, f: 1 - v if f else v
    peers = []
    for f in range(1, NDEV):
        px, py, pc = flip(x, f & 4), flip(y, f & 2), flip(c, f & 1)
        peers.append(((px, py, pc), 4 * px + 2 * py + pc))
    return (x, y, c), 4 * x + 2 * y + c, peers


def _places():
    x, y, c = lax.axis_index("x"), lax.axis_index("y"), lax.axis_index("c")
    place = lambda px, py, pc: ((px, py, pc), 4 * px + 2 * py + pc)
    return place(x, y, c), place(x, y, 1 - c), [place(1 - x, y, c), place(x, 1 - y, c), place(1 - x, 1 - y, c)]


def _exchange(arrs, gather, name):
    n = len(arrs)
    hbm = pl.BlockSpec(memory_space=pltpu.HBM)
    if gather:
        out_shape = [jax.ShapeDtypeStruct((NDEV * a.shape[0], a.shape[1]), a.dtype) for a in arrs]
    else:
        out_shape = [jax.ShapeDtypeStruct((NDEV, a.shape[0] // NDEV, a.shape[1]), a.dtype) for a in arrs]

    def body(*refs):
        ins, outs = refs[:n], refs[n:2 * n]
        send_sems, recv_sems, local_sems = refs[2 * n:]
        me_pos, me, peers = _peers()
        local = []
        for k in range(n):
            rows = arrs[k].shape[0] if gather else arrs[k].shape[0] // NDEV
            if gather:
                src_of = lambda idx: ins[k]
                dst_of = lambda idx: outs[k].at[pl.ds(me * rows, rows)]
                mine = (ins[k], outs[k].at[pl.ds(me * rows, rows)])
            else:
                src_of = lambda idx: ins[k].at[pl.ds(idx * rows, rows)]
                dst_of = lambda idx: outs[k].at[me]
                mine = (ins[k].at[pl.ds(me * rows, rows)], outs[k].at[me])
            cp = pltpu.make_async_copy(mine[0], mine[1], local_sems.at[k])
            cp.start()
            local.append(cp)
            for pos, idx in peers:
                pltpu.make_async_remote_copy(src_ref=src_of(idx), dst_ref=dst_of(idx), send_sem=send_sems.at[k],
                                             recv_sem=recv_sems.at[k], device_id=pos, device_id_type=MESH).start()
        for k in range(n):
            rows = arrs[k].shape[0] if gather else arrs[k].shape[0] // NDEV
            sent = ins[k].at[pl.ds(0, (NDEV - 1) * rows)] if not gather else outs[k].at[pl.ds(0, (NDEV - 1) * rows)]
            got = outs[k].at[pl.ds(0, (NDEV - 1) * rows)] if gather else outs[k].at[pl.ds(0, NDEV - 1)]
            pltpu.make_async_remote_copy(src_ref=sent, dst_ref=sent, send_sem=send_sems.at[k], recv_sem=recv_sems.at[k],
                                         device_id=me_pos, device_id_type=MESH).wait_send()
            pltpu.make_async_remote_copy(src_ref=got, dst_ref=got, send_sem=send_sems.at[k], recv_sem=recv_sems.at[k],
                                         device_id=me_pos, device_id_type=MESH).wait_recv()
            local[k].wait()

    return pl.pallas_call(
        body, out_shape=out_shape, in_specs=n * [hbm], out_specs=n * [hbm], name=name,
        scratch_shapes=[pltpu.SemaphoreType.DMA((n,)), pltpu.SemaphoreType.DMA((n,)), pltpu.SemaphoreType.DMA((n,))],
        compiler_params=pltpu.CompilerParams(has_side_effects=True),
    )(*arrs)


_HBM = pl.BlockSpec(memory_space=pltpu.HBM)
_SEM = pl.BlockSpec(memory_space=pltpu.SEMAPHORE)
_DATAFLOW = pltpu.SideEffectType.DATAFLOW_SIDE_EFFECTING


def _split_start(srcs, groups, gather, name):
    n = len(srcs)
    if gather:
        lands = [lax.empty((NDEV * a.shape[0], a.shape[1]), a.dtype) for a in srcs]
    else:
        lands = [lax.empty((NDEV, a.shape[0] // NDEV, a.shape[1]), a.dtype) for a in srcs]
    n_sem = 3 * len(groups)

    def body(*refs):
        src_refs, land_refs = refs[:n], refs[n:2 * n]
        sems = refs[2 * n:2 * n + n_sem]
        token = refs[-1]
        (_, my), sibling, chips = _places()
        _, _, peers = _peers()
        targets = [sibling] + chips if gather else peers
        for g, members in enumerate(groups):
            for j, k in enumerate(members):
                _own_copy(src_refs[k], land_refs[k], sems[3 * g + 2].at[j], my, gather).start()
        for g, members in enumerate(groups):
            for j, k in enumerate(members):
                rows = srcs[k].shape[0] if gather else srcs[k].shape[0] // NDEV
                for pos, idx in targets:
                    src = src_refs[k] if gather else src_refs[k].at[pl.ds(idx * rows, rows)]
                    dst = land_refs[k].at[pl.ds(my * rows, rows)] if gather else land_refs[k].at[my]
                    pltpu.make_async_remote_copy(src_ref=src, dst_ref=dst, send_sem=sems[3 * g].at[j],
                                                 recv_sem=sems[3 * g + 1].at[j], device_id=pos, device_id_type=MESH).start()
        token[...] = jnp.zeros_like(token)

    out_shape = []
    for members in groups:
        out_shape += 3 * [pltpu.SemaphoreType.DMA((len(members),))]
    out_shape += [pltpu.HBM(a.shape, a.dtype) for a in srcs] + [pltpu.HBM(a.shape, a.dtype) for a in lands]
    out_shape.append(jax.ShapeDtypeStruct((8, 128), F32))
    res = pl.pallas_call(
        body, name=name, out_shape=tuple(out_shape), in_specs=2 * n * [_HBM],
        out_specs=tuple(n_sem * [_SEM] + 2 * n * [_HBM] + [pl.BlockSpec(memory_space=pltpu.VMEM)]),
        input_output_aliases={i: n_sem + i for i in range(2 * n)},
        compiler_params=pltpu.CompilerParams(has_side_effects=_DATAFLOW),
    )(*[pltpu.with_memory_space_constraint(a, pltpu.HBM) for a in list(srcs) + lands])
    sems = [tuple(res[3 * g:3 * g + 3]) for g in range(len(groups))]
    return sems, list(res[n_sem:n_sem + n]), list(res[n_sem + n:n_sem + 2 * n]), res[-1]


def _own_copy(src_ref, land_ref, sem, my, gather):
    if gather:
        rows = src_ref.shape[0]
        return pltpu.make_async_copy(src_ref, land_ref.at[pl.ds(my * rows, rows)], sem)
    rows = src_ref.shape[0] // NDEV
    return pltpu.make_async_copy(src_ref.at[pl.ds(my * rows, rows)], land_ref.at[my], sem)


def _wait_all(land_ref, blocks_per_dev, copies, send_sem, recv_sem, me_pos):
    part = land_ref.at[pl.ds(0, copies * blocks_per_dev)]
    pltpu.make_async_remote_copy(src_ref=part, dst_ref=part, send_sem=send_sem, recv_sem=recv_sem,
                                 device_id=me_pos, device_id_type=MESH).wait()


def _gather_forward(sems, srcs, lands, after, name):
    n = len(srcs)

    def body(*refs):
        land_refs = refs[n:2 * n]
        send_a, recv_a = refs[2 * n], refs[2 * n + 1]
        send_b, recv_b = refs[2 * n + 3], refs[2 * n + 4]
        token = refs[-1]
        (me_pos, _), sibling, chips = _places()
        for j in range(n):
            _wait_all(land_refs[j], lands[j].shape[0] // NDEV, 1 + OTHER_CHIPS, send_a.at[j], recv_a.at[j], me_pos)
        for j in range(n):
            rows = lands[j].shape[0] // NDEV
            for _, idx in chips:
                block = land_refs[j].at[pl.ds(idx * rows, rows)]
                pltpu.make_async_remote_copy(src_ref=block, dst_ref=block, send_sem=send_b.at[j], recv_sem=recv_b.at[j],
                                             device_id=sibling[0], device_id_type=MESH).start()
        token[...] = jnp.zeros_like(token)

    res = pl.pallas_call(
        body, name=name,
        out_shape=(pltpu.SemaphoreType.DMA((n,)), pltpu.SemaphoreType.DMA((n,)))
        + tuple(pltpu.HBM(a.shape, a.dtype) for a in list(srcs) + list(lands)) + (jax.ShapeDtypeStruct((8, 128), F32),),
        in_specs=2 * n * [_HBM] + [_SEM, _SEM, pl.BlockSpec(memory_space=pl.ANY)],
        out_specs=tuple([_SEM, _SEM] + 2 * n * [_HBM] + [pl.BlockSpec(memory_space=pltpu.VMEM)]),
        input_output_aliases={i: 2 + i for i in range(2 * n)},
        compiler_params=pltpu.CompilerParams(has_side_effects=_DATAFLOW),
    )(*srcs, *lands, sems[0], sems[1], after)
    return (res[0], res[1]), list(res[2:2 + n]), list(res[2 + n:2 + 2 * n]), res[-1]


def _split_wait(sems, srcs, lands, after, copies, gather, name):
    n = len(srcs)

    def body(*refs):
        src_refs, land_refs = refs[:n], refs[n:2 * n]
        send_sem, recv_sem, local_sem = refs[2 * n], refs[2 * n + 1], refs[2 * n + 2]
        (me_pos, my), _, _ = _places()
        for j in range(n):
            _wait_all(land_refs[j], lands[j].shape[0] // NDEV, copies, send_sem.at[j], recv_sem.at[j], me_pos)
            _own_copy(src_refs[j], land_refs[j], local_sem.at[j], my, gather).wait()

    res = pl.pallas_call(
        body, name=name, out_shape=tuple(pltpu.HBM(a.shape, a.dtype) for a in list(srcs) + list(lands)),
        in_specs=2 * n * [_HBM] + [_SEM, _SEM, _SEM, pl.BlockSpec(memory_space=pl.ANY)], out_specs=tuple(2 * n * [_HBM]),
        input_output_aliases={i: i for i in range(2 * n)},
        compiler_params=pltpu.CompilerParams(has_side_effects=_DATAFLOW),
    )(*srcs, *lands, sems[0], sems[1], sems[2], after)
    return list(res[n:])


def _chained(gate, mid, after):
    return gate if mid is None else gate + mid(after)[:1, :1]


def _ffn_fwd(x, norms, mod, w, mid=None):
    (pre_g, post_g), (shift, scale, gate), (wg_t, wu_t, wd) = norms, mod, w
    hn, g, u, a = _ffn_up(x, pre_g, scale, shift, wg_t, wu_t, "ffn_up")
    if callable(wd):
        wd = wd(a)
    x_out, f = _mm_post(a, wd, x, post_g, _chained(gate, mid, a), FFN_RES, "ffn_down")
    return x_out, (x, hn, g, u, a, f), (wg_t, wu_t, wd)


def _ffn_bwd(dx_out, saved, norms, mod, w, send=None):
    (pre_g, post_g), (_, scale, gate), (wg_t, wu_t, wd) = norms, mod, w
    x, hn, g, u, a, f = saved
    d_model = x.shape[1]
    sent = (lambda j, dw: None) if send is None else send
    df, dgate, dpost = _post_bwd(dx_out, f, post_g, gate, FFN_RES, "ffn_post_bwd")
    dwd = _mm([(a, df)], "tn", BF16, 256, d_model, "ffn_dw")
    dg, du = _ffn_dgu(df, wd, g, u, "ffn_dgu", after=sent(2, dwd))
    dwg_t = _mm([(dg, hn)], "tn", BF16, 256, d_model, "ffn_dw")
    dwu_t = _mm([(du, hn)], "tn", BF16, 256, d_model, "ffn_dw", after=sent(0, dwg_t))
    dhn = _mm([(dg, wg_t), (du, wu_t)], "nn", F32, TOKEN_TILE, d_model, "ffn_dhn", after=sent(1, dwu_t))
    dx, dshift, dscale, dpre = _prenorm_bwd(dx_out, [dhn], x, pre_g, scale, "prenorm_bwd")
    return dx, (dpre, dpost), (dshift, dscale, dgate), (dwg_t, dwu_t, dwd)


def _mla_fwd(x, norms, mod, w, rope, mid=None):
    (pre_g, post_g), (shift, scale, gate) = norms, mod
    w_in, q_norm, wq_t, kv_norm, wkv_t, wo = w
    hn, lat = _prenorm_mm(x, pre_g, scale, shift, w_in, "nn", F32, LAT_PAD, "mla_in")
    gate = _chained(gate, mid, lat)
    q, k, v, qn, kvn = _mla_qkv(lat, q_norm, kv_norm, wq_t, wkv_t, rope, "mla_qkv")
    o = _mla_attn_fwd(q, k, v, "mla_attn_fwd")
    x_out, f = _mm_post(o, wo, x, post_g, gate, 1.0, "mla_out")
    return x_out, (x, hn, lat, q, k, v, qn, kvn, o, f)


def _mla_bwd(dx_out, saved, norms, mod, w, rope):
    (pre_g, post_g), (_, scale, gate) = norms, mod
    w_in, q_norm, wq_t, kv_norm, wkv_t, wo = w
    x, hn, lat, q, k, v, qn, kvn, o, f = saved
    d_model = x.shape[1]
    df, dgate, dpost = _post_bwd(dx_out, f, post_g, gate, 1.0, "mix_post_bwd")
    d_o = _mm([(df, wo)], "nt", F32, TOKEN_TILE, wo.shape[0], "mla_do")
    dwo = _mm([(o, df)], "tn", BF16, TOKEN_TILE, d_model, "mla_dwo")
    dq, dk, dv = _mla_attn_bwd(q, k, v, d_o, "mla_attn_bwd")
    dqp, dkv, dlat, dq_norm, dkv_norm = _mla_qkv_bwd(dq, dk, dv, lat, q_norm, kv_norm, wq_t, wkv_t, rope, "mla_qkv_bwd")
    dwq_t = _mm([(dqp, qn)], "tn", BF16, TOKEN_TILE, Q_LORA, "mla_dwq")
    dwkv_t = _mm([(dkv, kvn)], "tn", BF16, TOKEN_TILE, KV_LORA, "mla_dwkv")
    dw_in = _mm([(hn, dlat)], "tn", BF16, TOKEN_TILE, LAT_PAD, "mla_dwin")
    dhn = _mm([(dlat, w_in)], "nt", F32, TOKEN_TILE, d_model, "mla_dhn")
    dx, dshift, dscale, dpre = _prenorm_bwd(dx_out, [dhn], x, pre_g, scale, "prenorm_bwd")
    return dx, (dpre, dpost), (dshift, dscale, dgate), (dw_in, dq_norm, dwq_t, dkv_norm, dwkv_t, dwo)


def _dil_fwd(x, norms, mod, w, bias, mid=None):
    (pre_g, post_g), (shift, scale, gate), (w_in_t, wo) = norms, mod, w
    width = 3 * DIL_HEADS * DIL_HEAD_DIM
    hns, qkvs, outs, lses = [], [], [], []
    for g, (window, dilation) in enumerate(DIL_GROUPS):
        hn, qkv = _prenorm_mm(x, pre_g, scale, shift, w_in_t[g * width:(g + 1) * width], "nt", BF16, width,
                              "dil_in", perm=dilation)
        if g == 0:
            gate = _chained(gate, mid, qkv)
        o, lse = _dil_attn_fwd(qkv, bias[g], dilation, window // dilation, "dil_attn_fwd")
        hns.append(hn), qkvs.append(qkv), outs.append(o), lses.append(lse)
    alphas, o_mix, o_mix_b = _dil_mix(lses, outs, "dil_mix")
    x_out, f = _mm_post(o_mix_b, wo, x, post_g, gate, 1.0, "dil_out")
    return x_out, (x, hns, qkvs, lses, alphas, o_mix, o_mix_b, f)


def _dil_bwd(dx_out, saved, norms, mod, w, bias):
    (pre_g, post_g), (_, scale, gate), (w_in_t, wo) = norms, mod, w
    x, hns, qkvs, lses, alphas, o_mix, o_mix_b, f = saved
    d_model = x.shape[1]
    inner = DIL_HEADS * DIL_HEAD_DIM
    df, dgate, dpost = _post_bwd(dx_out, f, post_g, gate, 1.0, "mix_post_bwd")
    d_o = _mm([(df, wo)], "nt", F32, TOKEN_TILE, inner, "dil_do")
    dwo = _mm([(o_mix_b, df)], "tn", BF16, TOKEN_TILE, d_model, "dil_dwo")
    dhns, dws, dbs = [], [], []
    for g, (window, dilation) in enumerate(DIL_GROUPS):
        grads = _dil_attn_bwd(qkvs[g], bias[g], d_o, o_mix, alphas[g], lses[g], dilation, window // dilation, "dil_attn_bwd")
        dbs.append(grads[3])
        w_parts = [w_in_t[(3 * g + j) * inner:(3 * g + j + 1) * inner] for j in range(3)]
        dhns.append(_mm(list(zip(grads[:3], w_parts)), "nn", F32, TOKEN_TILE, d_model, "dil_dhn", out_perm=dilation))
        dws += [_mm([(grads[j], hns[g])], "tn", BF16, TOKEN_TILE, d_model, "dil_dwin") for j in range(3)]
    dx, dshift, dscale, dpre = _prenorm_bwd(dx_out, dhns, x, pre_g, scale, "prenorm_bwd3")
    return dx, (dpre, dpost), (dshift, dscale, dgate), (jnp.concatenate(dws, axis=0), dwo), jnp.concatenate(dbs, axis=0)


def _pad_rows(a, rows):
    return jnp.pad(a, ((0, rows - a.shape[0]), (0, 0)))


def _lanes(a):
    flat = a.reshape(-1).astype(F32)
    rows = -(-flat.shape[0] // 1024) * 8
    return jnp.pad(flat, (0, rows * 128 - flat.shape[0])).reshape(rows, 128)


def kernel(x, c, norm_pre, norm_post, w_mod, b_mod, ffn_w_gate, ffn_w_up, ffn_w_down, mla_w_in, mla_q_norm, mla_w_q_up, mla_kv_norm, mla_w_kv_up, mla_w_o, dil_w_in, dil_w_o, rel_bias, loss_target, m_norm_pre, m_norm_post, m_w_mod, m_b_mod, m_ffn_w_gate, m_ffn_w_up, m_ffn_w_down, m_mla_w_in, m_mla_q_norm, m_mla_w_q_up, m_mla_kv_norm, m_mla_w_kv_up, m_mla_w_o, m_dil_w_in, m_dil_w_o, m_rel_bias, v_norm_pre, v_norm_post, v_w_mod, v_b_mod, v_ffn_w_gate, v_ffn_w_up, v_ffn_w_down, v_mla_w_in, v_mla_q_norm, v_mla_w_q_up, v_mla_kv_norm, v_mla_w_kv_up, v_mla_w_o, v_dil_w_in, v_dil_w_o, v_rel_bias):
    me = 4 * lax.axis_index("x") + 2 * lax.axis_index("y") + lax.axis_index("c")
    depth, n_sub, d_loc = norm_pre.shape
    d_model = x.shape[2]
    mod_loc_cols = w_mod.shape[2]
    x0, target = x[0], loss_target[0]

    bf_t = lambda a: a.astype(BF16).T
    ffn_ids = [(i, h) for i in range(depth) for h in range(2)]
    shards = []
    for i, h in ffn_ids:
        shards += [bf_t(ffn_w_gate[i, h]), bf_t(ffn_w_up[i, h]), ffn_w_down[i, h].astype(BF16)]
    shards += [mla_w_in[0].astype(BF16), bf_t(mla_w_q_up[0]), bf_t(mla_w_kv_up[0]), mla_w_o[0].astype(BF16),
               bf_t(dil_w_in[0]), dil_w_o[0].astype(BF16)]
    n_ffn = 3 * len(ffn_ids)
    members = {(0, 0): [0, 1, 2], (0, 1): [n_ffn, n_ffn + 1, n_ffn + 2, n_ffn + 3], (0, 2): [3, 4, 5],
               (1, 0): [6, 7, 8], (1, 1): [n_ffn + 4, n_ffn + 5], (1, 2): [9, 10, 11]}
    order = [(i, s) for i in range(depth) for s in range(n_sub)]

    small = jnp.concatenate([c.reshape(8, 128), _pad_rows(norm_pre.reshape(depth * n_sub, d_loc), 8),
                             _pad_rows(norm_post.reshape(depth * n_sub, d_loc), 8)], axis=0)
    small_all = _exchange([small], True, "gather_small")[0].reshape(NDEV, 24, 128)
    c_all = small_all[:, 0:8].reshape(NDEV, d_model)
    gains = lambda lo: jnp.transpose(small_all[:, lo:lo + depth * n_sub], (1, 0, 2)).reshape(depth, n_sub, 1, d_model)
    pre_full, post_full = gains(8), gains(16)

    b_loc = lax.dynamic_slice(b_mod, (0, me * mod_loc_cols), (depth, mod_loc_cols))
    mod_cols, silu_c = _mod_fwd(c_all, w_mod, b_loc, "mod_fwd")
    mod_all = _exchange([mod_cols.reshape(depth * NDEV, mod_loc_cols)], True, "gather_mod")[0]
    mod_all = mod_all.reshape(NDEV, depth, NDEV, mod_loc_cols)
    mod_mine = lax.dynamic_index_in_dim(mod_all, me, axis=2, keepdims=False)
    mod = jnp.transpose(mod_mine, (1, 0, 2)).reshape(depth, n_sub, 3, 1, d_model)

    shards[0], _ = lax.optimization_barrier((shards[0], mod_all))
    first = order[0]
    stages = [("%d%d" % first, members[first][:2]), ("%d%dd" % first, members[first][2:])]
    stages += [("%d%d" % key, members[key]) for key in order[1:]]
    stage_names = [name for name, _ in stages]
    g_sems, g_srcs, g_lands, g_token = _split_start(shards, [idx for _, idx in stages], True, "gather_weights_start")

    forwarded = {}

    def forward(stage, after):
        idx = stages[stage_names.index(stage)][1]
        forwarded[stage] = _gather_forward(g_sems[stage_names.index(stage)], [g_srcs[k] for k in idx],
                                           [g_lands[k] for k in idx], after, "gather_forward_" + stage)
        return forwarded[stage][3]

    def weights_of(stage, after):
        (send_b, recv_b), srcs, lands, _ = forwarded[stage]
        local = g_sems[stage_names.index(stage)][2]
        return _split_wait((send_b, recv_b, local), srcs, lands, after, OTHER_CHIPS, True, "gather_wait_" + stage)

    def late_down(after):
        forward("%d%dd" % first, after)
        return weights_of("%d%dd" % first, after)[0]

    lat_real = Q_LORA + KV_LORA
    qk = QK_NOPE + QK_ROPE

    def mla_weights(after):
        w_in, wq_t, wkv_t, wo = weights_of("01", after)
        w_in_pad = jnp.concatenate([w_in[:, :lat_real], jnp.zeros((d_model, QK_NOPE), BF16), w_in[:, lat_real:],
                                    jnp.zeros((d_model, HEAD_PAD - QK_NOPE - QK_ROPE), BF16)], axis=1)
        wq_pad = jnp.pad(wq_t.reshape(MLA_HEADS, qk, Q_LORA), ((0, 0), (0, HEAD_PAD - qk), (0, 0)))
        wo_pad = jnp.pad(wo.reshape(MLA_HEADS, V_HEAD, d_model), ((0, 0), (HEAD_PAD - V_HEAD, 0), (0, 0)))
        return (w_in_pad, mla_q_norm, wq_pad.reshape(MLA_HEADS * HEAD_PAD, Q_LORA), mla_kv_norm, wkv_t,
                wo_pad.reshape(MLA_HEADS * HEAD_PAD, d_model))

    zero = g_token[0, 0]
    rope = _rope_tables(zero)
    buckets = jnp.stack([_dil_buckets(dil) for _, dil in DIL_GROUPS]) + zero.astype(jnp.int32)
    onehot = (buckets[..., None] == jnp.arange(N_BUCKETS)).astype(F32)
    bias = jnp.einsum("gqkb,bgh->ghqk", onehot, rel_bias.reshape(N_BUCKETS, len(DIL_GROUPS), DIL_HEADS),
                      precision=lax.Precision.HIGHEST)

    norms = lambda i, s: (pre_full[i, s], post_full[i, s])
    mods = lambda i, s: (mod[i, s, 0], mod[i, s, 1], mod[i, s, 2])
    saved, weights = {}, {}
    h = lax.optimization_barrier((x0, bias, buckets, *rope))[0]
    forward("%d%d" % first, h)
    for n, (i, s) in enumerate(order):
        got = mla_weights(h) if (s == 1 and i % 2 == 0) else tuple(weights_of("%d%d" % (i, s), h))
        mid = None if n + 1 == len(order) else (lambda after, nxt="%d%d" % order[n + 1]: forward(nxt, after))
        if s != 1:
            got = got if len(got) == 3 else (*got, late_down)
            h, saved[i, s], weights[i, s] = _ffn_fwd(h, norms(i, s), mods(i, s), got, mid)
            continue
        weights[i, s] = got
        if i % 2 == 0:
            h, saved[i, s] = _mla_fwd(h, norms(i, s), mods(i, s), weights[i, s], rope, mid)
        else:
            h, saved[i, s] = _dil_fwd(h, norms(i, s), mods(i, s), weights[i, s], bias, mid)
    dh, loss_parts = _loss_grad(h, target, "loss")

    dnorm, dmod, sent = {}, {}, {}
    token = jnp.zeros((8, 128), F32)
    last = order[0]

    def send_last(j, dw):
        sent[last, j] = _split_start([dw], [[0]], False, "scatter_start_%d%d_%d" % (*last, j))
        return sent[last, j][3]

    for i, s in reversed(order):
        md = mods(i, s)
        md = (md[0], md[1], md[2] + token[:1, :1])
        if (i, s) == last:
            dh, dnorm[i, s], dmod[i, s], _ = _ffn_bwd(dh, saved[i, s], norms(i, s), md, weights[i, s], send_last)
            continue
        if s != 1:
            dh, dnorm[i, s], dmod[i, s], dws = _ffn_bwd(dh, saved[i, s], norms(i, s), md, weights[i, s])
        elif i % 2 == 0:
            dh, dnorm[i, s], dmod[i, s], dmla = _mla_bwd(dh, saved[i, s], norms(i, s), md, weights[i, s], rope)
            dw_in_pad, dq_norm, dwq_pad, dkv_norm, dwkv_t, dwo_pad = dmla
            dw_in = jnp.concatenate([dw_in_pad[:, :lat_real], dw_in_pad[:, lat_real + QK_NOPE:lat_real + qk]], axis=1)
            dwq_t = dwq_pad.reshape(MLA_HEADS, HEAD_PAD, Q_LORA)[:, :qk].reshape(MLA_HEADS * qk, Q_LORA)
            dwo = dwo_pad.reshape(MLA_HEADS, HEAD_PAD, d_model)[:, HEAD_PAD - V_HEAD:].reshape(MLA_HEADS * V_HEAD, d_model)
            dws = (dw_in, dwq_t, dwkv_t, dwo)
        else:
            dh, dnorm[i, s], dmod[i, s], dws, dbias = _dil_bwd(dh, saved[i, s], norms(i, s), md, weights[i, s], bias)
        sent[i, s] = _split_start(list(dws), [list(range(len(dws)))], False, "scatter_start_%d%d" % (i, s))
        token = sent[i, s][3]
    grad_x = dh[None]

    mine = {}
    for key in order[1:]:
        sems, srcs, lands, _ = sent[key]
        parts = _split_wait(sems[0], srcs, lands, dh, NDEV - 1, False, "scatter_wait_%d%d" % key)
        for k, p in zip(members[key], parts):
            mine[k] = _sum_parts(p, "sum_parts")
    for j in (2, 0, 1):
        sems, srcs, lands, _ = sent[last, j]
        parts = _split_wait(sems[0], srcs, lands, dh, NDEV - 1, False, "scatter_wait_%d%d_%d" % (*last, j))
        mine[members[last][j]] = _sum_parts(parts[0], "sum_parts")
    g_gate = jnp.stack([mine[3 * n].T for n in range(len(ffn_ids))]).reshape(ffn_w_gate.shape)
    g_up = jnp.stack([mine[3 * n + 1].T for n in range(len(ffn_ids))]).reshape(ffn_w_up.shape)
    g_down = jnp.stack([mine[3 * n + 2] for n in range(len(ffn_ids))]).reshape(ffn_w_down.shape)
    g_mla_in, g_q_up, g_kv_up, g_mla_o, g_dil_in, g_dil_o = (mine[k] for k in range(n_ffn, n_ffn + 6))
    g_mla_in, g_q_up, g_kv_up, g_mla_o = g_mla_in[None], g_q_up.T[None], g_kv_up.T[None], g_mla_o[None]
    g_dil_in, g_dil_o = g_dil_in.T[None], g_dil_o[None]

    dmod_mine = jnp.concatenate([jnp.concatenate(dmod[i, s], axis=0) for i in range(depth) for s in range(n_sub)], axis=0)
    dpre_mine = jnp.concatenate([dnorm[i, s][0] for i in range(depth) for s in range(n_sub)], axis=0)
    dpost_mine = jnp.concatenate([dnorm[i, s][1] for i in range(depth) for s in range(n_sub)], axis=0)
    dbias_tab = _bias_reduce(dbias, buckets, "bias_reduce")[:, 0, :N_BUCKETS].T
    pieces = [dmod_mine, dpre_mine, dpost_mine, dq_norm, dkv_norm, dbias_tab, jnp.sum(loss_parts).reshape(1, 1)]
    packed = [_lanes(p) for p in pieces]
    offs = [0]
    for p in packed:
        offs.append(offs[-1] + p.shape[0])
    everyone = _exchange([jnp.concatenate(packed, axis=0)], True, "gather_small_grads")[0].reshape(NDEV, offs[-1], 128)
    total = _sum_parts(everyone, "sum_small")
    take = lambda n, shape: total[offs[n]:offs[n + 1]].reshape(-1)[:math.prod(shape)].reshape(shape)
    g_b_mod = take(0, b_mod.shape)
    col0 = me * d_loc
    g_norm_pre = lax.dynamic_slice(take(1, (depth, n_sub, d_model)), (0, 0, col0), norm_pre.shape)
    g_norm_post = lax.dynamic_slice(take(2, (depth, n_sub, d_model)), (0, 0, col0), norm_post.shape)
    g_q_norm, g_kv_norm = take(3, mla_q_norm.shape), take(4, mla_kv_norm.shape)
    g_rel_bias = take(5, rel_bias.shape)
    loss = take(6, ())

    dmod_all = everyone[:, offs[0]:offs[1]].reshape(NDEV, depth, NDEV * mod_loc_cols)
    dmod_cols = lax.dynamic_slice(dmod_all, (0, 0, me * mod_loc_cols), (NDEV, depth, mod_loc_cols))
    silu_t = jnp.pad(silu_c.T, ((0, 0), (0, HEAD_PAD - NDEV)))
    g_w_mod = jnp.stack([_mm([(silu_t, jnp.pad(dmod_cols[:, i], ((0, HEAD_PAD - NDEV), (0, 0))))], "nn", F32, TOKEN_TILE,
                             mod_loc_cols, "mod_bwd") for i in range(depth)])

    ws = (norm_pre, norm_post, w_mod, b_mod, ffn_w_gate, ffn_w_up, ffn_w_down, mla_w_in, mla_q_norm, mla_w_q_up, mla_kv_norm,
          mla_w_kv_up, mla_w_o, dil_w_in, dil_w_o, rel_bias)
    gs = (g_norm_pre, g_norm_post, g_w_mod, g_b_mod, g_gate, g_up, g_down, g_mla_in, g_q_norm, g_q_up, g_kv_norm, g_kv_up,
          g_mla_o, g_dil_in, g_dil_o, g_rel_bias)
    ms = (m_norm_pre, m_norm_post, m_w_mod, m_b_mod, m_ffn_w_gate, m_ffn_w_up, m_ffn_w_down, m_mla_w_in, m_mla_q_norm,
          m_mla_w_q_up, m_mla_kv_norm, m_mla_w_kv_up, m_mla_w_o, m_dil_w_in, m_dil_w_o, m_rel_bias)
    vs = (v_norm_pre, v_norm_post, v_w_mod, v_b_mod, v_ffn_w_gate, v_ffn_w_up, v_ffn_w_down, v_mla_w_in, v_mla_q_norm,
          v_mla_w_q_up, v_mla_kv_norm, v_mla_w_kv_up, v_mla_w_o, v_dil_w_in, v_dil_w_o, v_rel_bias)
    stepped = [_adamw(w, g, m, v, "adamw") for w, g, m, v in zip(ws, gs, ms, vs)]
    deltas, new_m, new_v = zip(*stepped)
    return (loss, grad_x, *gs, *deltas, *new_m, *new_v)
```

```python
import math

import jax
import jax.numpy as jnp
from jax import lax
from jax.experimental import pallas as pl
from jax.experimental.pallas import tpu as pltpu

F32 = jnp.float32
BF16 = jnp.bfloat16
MESH = pl.DeviceIdType.MESH

NDEV = 8
OTHER_CHIPS = 3
D_MODEL = 1024
SEQ = 2048
D_FF = 2816
EPS = 1e-6
FFN_RES = 0.5

MLA_HEADS = 16
Q_LORA = 384
KV_LORA = 256
QK_NOPE = 64
QK_ROPE = 32
V_HEAD = 64
ROPE_THETA = 10000.0
HEAD_PAD = 128
LAT_PAD = Q_LORA + KV_LORA + HEAD_PAD
MLA_SCALE = (QK_NOPE + QK_ROPE) ** -0.5

DIL_GROUPS = ((128, 1), (512, 4), (2048, 16))
DIL_HEADS = 16
DIL_HEAD_DIM = 64
DIL_BLOCK = 128
DIL_PAIRS = DIL_HEADS // 2
DIL_SCALE = DIL_HEAD_DIM ** -0.5
DIL_GROUPED = 4
N_BUCKETS = 32
MAX_DISTANCE = 2048

ADAM_LR = 0.001
ADAM_B1 = 0.9
ADAM_B2 = 0.999
ADAM_EPS = 1e-08
ADAM_WD = 0.01
ADAM_STEP = 10

V7X_VMEM_BYTES = 64 * 2**20
VMEM_RESERVE = 10 * 2**20
TOKEN_TILE = 512


def _nbytes(shape, dtype):
    return math.prod(shape) * jnp.dtype(dtype).itemsize


def _params(semantics, blocks, extra=0):
    need = 2 * sum(_nbytes(s, d) for s, d in blocks) + extra + VMEM_RESERVE
    return pltpu.CompilerParams(dimension_semantics=semantics,
                                vmem_limit_bytes=int(min(need, V7X_VMEM_BYTES - VMEM_RESERVE)))


def _pcall(body, out_shape, **kw):
    call = pl.pallas_call(body, out_shape=jax.tree.map(lambda s: pltpu.HBM(s.shape, s.dtype), out_shape), **kw)
    return lambda *args: call(*[pltpu.with_memory_space_constraint(a, pltpu.HBM) for a in args])


def _dot_nn(a, b):
    return lax.dot_general(a, b, (((1,), (0,)), ((), ())), preferred_element_type=F32)


def _dot_nt(a, b):
    return lax.dot_general(a, b, (((1,), (1,)), ((), ())), preferred_element_type=F32)


def _dot_tn(a, b):
    return lax.dot_general(a, b, (((0,), (0,)), ((), ())), preferred_element_type=F32)


_DOTS = {"nn": _dot_nn, "nt": _dot_nt, "tn": _dot_tn}


def _rstd(v):
    return lax.rsqrt(jnp.mean(v * v, axis=-1, keepdims=True) + EPS)


def _rms_bwd(v, r, t):
    return r * t - v * (r * r * r) * jnp.mean(t * v, axis=-1, keepdims=True)


_TOKEN_SPEC = pl.BlockSpec((8, 128), lambda *_: (0, 0))


def _mm(pairs, mode, out_dtype, tm, tn, name, out_perm=1, after=None):
    a0, b0 = pairs[0]
    m_dim = a0.shape[1] if mode == "tn" else a0.shape[0]
    n_dim = b0.shape[0] if mode == "nt" else b0.shape[1]
    tm, tn = min(tm, m_dim // out_perm), min(tn, n_dim)
    assert m_dim % tm == 0 and n_dim % tn == 0, (name, m_dim, n_dim, tm, tn)
    dot = _DOTS[mode]
    npairs = len(pairs)

    def body(*refs):
        acc = None
        for p in range(npairs):
            d = dot(refs[2 * p][...].astype(BF16), refs[2 * p + 1][...].astype(BF16))
            acc = d if acc is None else acc + d
        refs[-1][...] = acc.astype(out_dtype)

    in_specs, blocks, flat = [], [], []
    for a, b in pairs:
        if mode == "nn":
            k = a.shape[1]
            sa, sb = ((tm, k), lambda i, j: (i, 0)), ((k, tn), lambda i, j: (0, j))
        elif mode == "nt":
            k = a.shape[1]
            sa, sb = ((tm, k), lambda i, j: (i, 0)), ((tn, k), lambda i, j: (j, 0))
        else:
            k = a.shape[0]
            sa, sb = ((k, tm), lambda i, j: (0, i)), ((k, tn), lambda i, j: (0, j))
        in_specs += [pl.BlockSpec(*sa), pl.BlockSpec(*sb)]
        blocks += [(sa[0], a.dtype), (sb[0], b.dtype)]
        flat += [a, b]
    if after is not None:
        in_specs.append(_TOKEN_SPEC)
        flat.append(after)
    if out_perm == 1:
        out_shape = (m_dim, n_dim)
        out_spec = pl.BlockSpec((tm, tn), lambda i, j: (i, j))
    else:
        rows = m_dim // out_perm
        assert tn == n_dim and rows % tm == 0, (name, rows, tm)
        nb = rows // tm
        out_shape = (rows, out_perm * n_dim)
        out_spec = pl.BlockSpec((tm, n_dim), lambda i, j: (i % nb, i // nb))
    blocks.append(((tm, tn), out_dtype))
    res = _pcall(
        body, out_shape=jax.ShapeDtypeStruct(out_shape, out_dtype), grid=(m_dim // tm, n_dim // tn),
        in_specs=in_specs, out_specs=out_spec, name=name,
        compiler_params=_params(("parallel", "parallel"), blocks, extra=2 * tm * tn * 4),
    )(*flat)
    return res.reshape(m_dim, n_dim)


def _prenorm_mm(x, pre_g, scale, shift, w, w_mode, out_dtype, tn, name, perm=1):
    s_dim, d_dim = x.shape
    n_dim = w.shape[0] if w_mode == "nt" else w.shape[1]
    rows = s_dim // perm
    tm = min(TOKEN_TILE, rows)
    nb = rows // tm
    tn = min(tn, n_dim)
    assert n_dim % tn == 0
    dot = _DOTS[w_mode]

    def body(x_ref, g_ref, sc_ref, sh_ref, w_ref, hn_ref, o_ref):
        @pl.when(pl.program_id(1) == 0)
        def _():
            xf = x_ref[...]
            hn = (xf * _rstd(xf) * g_ref[...]) * (1.0 + sc_ref[...]) + sh_ref[...]
            hn_ref[...] = hn.astype(BF16)

        o_ref[...] = dot(hn_ref[...], w_ref[...]).astype(out_dtype)

    vec = pl.BlockSpec((1, d_dim), lambda i, j: (0, 0))
    w_block = (tn, d_dim) if w_mode == "nt" else (d_dim, tn)
    w_spec = pl.BlockSpec(w_block, (lambda i, j: (j, 0)) if w_mode == "nt" else (lambda i, j: (0, j)))
    hn, out = _pcall(
        body,
        out_shape=(jax.ShapeDtypeStruct((s_dim, d_dim), BF16), jax.ShapeDtypeStruct((s_dim, n_dim), out_dtype)),
        grid=(s_dim // tm, n_dim // tn),
        in_specs=[pl.BlockSpec((tm, d_dim), lambda i, j: (i % nb, i // nb)), vec, vec, vec, w_spec],
        out_specs=(pl.BlockSpec((tm, d_dim), lambda i, j: (i, 0)), pl.BlockSpec((tm, tn), lambda i, j: (i, j))),
        name=name,
        compiler_params=_params(("parallel", "arbitrary"),
                                [((tm, d_dim), F32), (w_block, BF16), ((tm, d_dim), BF16), ((tm, tn), out_dtype)],
                                extra=3 * tm * d_dim * 4 + tm * tn * 4),
    )(x.reshape(rows, perm * d_dim), pre_g, scale, shift, w)
    return hn, out


def _ffn_up(x, pre_g, scale, shift, wg_t, wu_t, name):
    s_dim, d_dim = x.shape
    f_dim = wg_t.shape[0]
    tm, tn = TOKEN_TILE, f_dim // 2

    def body(x_ref, g_ref, sc_ref, sh_ref, wg_ref, wu_ref, hn_ref, go_ref, uo_ref, a_ref):
        @pl.when(pl.program_id(1) == 0)
        def _():
            xf = x_ref[...]
            hn = (xf * _rstd(xf) * g_ref[...]) * (1.0 + sc_ref[...]) + sh_ref[...]
            hn_ref[...] = hn.astype(BF16)

        hn = hn_ref[...]
        g = _dot_nt(hn, wg_ref[...])
        u = _dot_nt(hn, wu_ref[...])
        go_ref[...] = g.astype(BF16)
        uo_ref[...] = u.astype(BF16)
        a_ref[...] = (g * jax.nn.sigmoid(g) * u).astype(BF16)

    vec = pl.BlockSpec((1, d_dim), lambda i, j: (0, 0))
    w_spec = pl.BlockSpec((tn, d_dim), lambda i, j: (j, 0))
    act = pl.BlockSpec((tm, tn), lambda i, j: (i, j))
    act_shape = jax.ShapeDtypeStruct((s_dim, f_dim), BF16)
    return _pcall(
        body,
        out_shape=(jax.ShapeDtypeStruct((s_dim, d_dim), BF16), act_shape, act_shape, act_shape),
        grid=(s_dim // tm, f_dim // tn),
        in_specs=[pl.BlockSpec((tm, d_dim), lambda i, j: (i, 0)), vec, vec, vec, w_spec, w_spec],
        out_specs=(pl.BlockSpec((tm, d_dim), lambda i, j: (i, 0)), act, act, act),
        name=name,
        compiler_params=_params(("parallel", "arbitrary"),
                                [((tm, d_dim), F32), ((tn, d_dim), BF16), ((tn, d_dim), BF16), ((tm, d_dim), BF16)]
                                + 3 * [((tm, tn), BF16)], extra=3 * tm * d_dim * 4 + 4 * tm * tn * 4),
    )(x, pre_g, scale, shift, wg_t, wu_t)


def _mm_post(a, w, x, post_g, gate, res_w, name):
    s_dim, k_dim = a.shape
    d_dim = w.shape[1]
    tm = TOKEN_TILE

    def body(a_ref, w_ref, x_ref, pg_ref, gt_ref, xo_ref, f_ref):
        f = _dot_nn(a_ref[...], w_ref[...])
        y = f * _rstd(f) * pg_ref[...]
        f_ref[...] = f
        xo_ref[...] = x_ref[...] + (res_w * gt_ref[...]) * y

    vec = pl.BlockSpec((1, d_dim), lambda i: (0, 0))
    row = pl.BlockSpec((tm, d_dim), lambda i: (i, 0))
    out = jax.ShapeDtypeStruct((s_dim, d_dim), F32)
    return _pcall(
        body, out_shape=(out, out), grid=(s_dim // tm,),
        in_specs=[pl.BlockSpec((tm, k_dim), lambda i: (i, 0)), pl.BlockSpec((k_dim, d_dim), lambda i: (0, 0)), row, vec, vec],
        out_specs=(row, row), name=name,
        compiler_params=_params(("parallel",), [((tm, k_dim), BF16), ((k_dim, d_dim), BF16)] + 3 * [((tm, d_dim), F32)],
                                extra=3 * tm * d_dim * 4),
    )(a, w, x, post_g, gate)


def _post_bwd(dx_out, f, post_g, gate, res_w, name):
    s_dim, d_dim = f.shape
    tm = TOKEN_TILE

    def body(dx_ref, f_ref, pg_ref, gt_ref, df_ref, dgate_ref, dpost_ref):
        @pl.when(pl.program_id(0) == 0)
        def _():
            dgate_ref[...] = jnp.zeros_like(dgate_ref)
            dpost_ref[...] = jnp.zeros_like(dpost_ref)

        dx, fv = dx_ref[...], f_ref[...]
        r = _rstd(fv)
        fr = fv * r
        dgate_ref[...] += res_w * jnp.sum(dx * (fr * pg_ref[...]), axis=0, keepdims=True)
        dy = (res_w * gt_ref[...]) * dx
        dpost_ref[...] += jnp.sum(dy * fr, axis=0, keepdims=True)
        df_ref[...] = _rms_bwd(fv, r, dy * pg_ref[...]).astype(BF16)

    vec = pl.BlockSpec((1, d_dim), lambda i: (0, 0))
    row = pl.BlockSpec((tm, d_dim), lambda i: (i, 0))
    vshape = jax.ShapeDtypeStruct((1, d_dim), F32)
    return _pcall(
        body, out_shape=(jax.ShapeDtypeStruct((s_dim, d_dim), BF16), vshape, vshape), grid=(s_dim // tm,),
        in_specs=[row, row, vec, vec], out_specs=(row, vec, vec), name=name,
        compiler_params=_params(("arbitrary",), 3 * [((tm, d_dim), F32)], extra=6 * tm * d_dim * 4),
    )(dx_out, f, post_g, gate)


def _prenorm_bwd(dx_out, dhns, x, pre_g, scale, name):
    s_dim, d_dim = x.shape
    tm = TOKEN_TILE
    n_in = len(dhns)

    def body(*refs):
        dx_ref, x_ref, pg_ref, sc_ref = refs[n_in + 0], refs[n_in + 1], refs[n_in + 2], refs[n_in + 3]
        dxo_ref, dsh_ref, dsc_ref, dpg_ref = refs[n_in + 4:]

        @pl.when(pl.program_id(0) == 0)
        def _():
            dsh_ref[...] = jnp.zeros_like(dsh_ref)
            dsc_ref[...] = jnp.zeros_like(dsc_ref)
            dpg_ref[...] = jnp.zeros_like(dpg_ref)

        dhn = refs[0][...]
        for k in range(1, n_in):
            dhn = dhn + refs[k][...]
        xv = x_ref[...]
        r = _rstd(xv)
        xr = xv * r
        dsh_ref[...] += jnp.sum(dhn, axis=0, keepdims=True)
        dsc_ref[...] += jnp.sum(dhn * (xr * pg_ref[...]), axis=0, keepdims=True)
        dn = dhn * (1.0 + sc_ref[...])
        dpg_ref[...] += jnp.sum(dn * xr, axis=0, keepdims=True)
        dxo_ref[...] = dx_ref[...] + _rms_bwd(xv, r, dn * pg_ref[...])

    vec = pl.BlockSpec((1, d_dim), lambda i: (0, 0))
    row = pl.BlockSpec((tm, d_dim), lambda i: (i, 0))
    vshape = jax.ShapeDtypeStruct((1, d_dim), F32)
    return _pcall(
        body, out_shape=(jax.ShapeDtypeStruct((s_dim, d_dim), F32), vshape, vshape, vshape), grid=(s_dim // tm,),
        in_specs=n_in * [row] + [row, row, vec, vec], out_specs=(row, vec, vec, vec), name=name,
        compiler_params=_params(("arbitrary",), (n_in + 3) * [((tm, d_dim), F32)], extra=6 * tm * d_dim * 4),
    )(*dhns, dx_out, x, pre_g, scale)


def _ffn_dgu(df, wd, g, u, name, after=None):
    s_dim, d_dim = df.shape
    f_dim = wd.shape[0]
    tm, tn = TOKEN_TILE, f_dim // 2

    def body(df_ref, wd_ref, g_ref, u_ref, *rest):
        dg_ref, du_ref = rest[-2:]
        da = _dot_nt(df_ref[...], wd_ref[...])
        gv, uv = g_ref[...].astype(F32), u_ref[...].astype(F32)
        sg = jax.nn.sigmoid(gv)
        du_ref[...] = (da * (gv * sg)).astype(BF16)
        dg_ref[...] = (da * uv * (sg * (1.0 + gv * (1.0 - sg)))).astype(BF16)

    act = pl.BlockSpec((tm, tn), lambda i, j: (i, j))
    act_shape = jax.ShapeDtypeStruct((s_dim, f_dim), BF16)
    token = [] if after is None else [after]
    return _pcall(
        body, out_shape=(act_shape, act_shape), grid=(s_dim // tm, f_dim // tn),
        in_specs=[pl.BlockSpec((tm, d_dim), lambda i, j: (i, 0)), pl.BlockSpec((tn, d_dim), lambda i, j: (j, 0)), act, act]
        + len(token) * [_TOKEN_SPEC],
        out_specs=(act, act), name=name,
        compiler_params=_params(("parallel", "parallel"), [((tm, d_dim), BF16), ((tn, d_dim), BF16)] + 4 * [((tm, tn), BF16)],
                                extra=6 * tm * tn * 4),
    )(df, wd, g, u, *token)


def _ffn_bwd_fused(dx_out, saved, pre_g, post_g, scale, gate, wg_t, wu_t, wd, name):
    x, _, g, u, _, f = saved
    s_dim, d_dim = x.shape
    f_dim = wd.shape[0]
    tm, chunks = 256, 2
    cw = f_dim // chunks

    def body(dx_ref, f_ref, g_ref, u_ref, x_ref, pg_ref, gt_ref, prg_ref, sc_ref, wd_ref, wg_ref, wu_ref,
             df_ref, dg_ref, du_ref, dxo_ref, dgate_ref, dpost_ref, dsh_ref, dsc_ref, dpg_ref):
        @pl.when(pl.program_id(0) == 0)
        def _():
            for acc in (dgate_ref, dpost_ref, dsh_ref, dsc_ref, dpg_ref):
                acc[...] = jnp.zeros_like(acc)

        dx, fv = dx_ref[...], f_ref[...]
        r = _rstd(fv)
        fr = fv * r
        dgate_ref[...] += FFN_RES * jnp.sum(dx * (fr * pg_ref[...]), axis=0, keepdims=True)
        dy = (FFN_RES * gt_ref[...]) * dx
        dpost_ref[...] += jnp.sum(dy * fr, axis=0, keepdims=True)
        df = _rms_bwd(fv, r, dy * pg_ref[...]).astype(BF16)
        df_ref[...] = df
        dhn = None
        ahead = _dot_nt(df, wd_ref[0:cw, :])
        for c in range(chunks):
            da = ahead
            if c + 1 < chunks:
                ahead = _dot_nt(df, wd_ref[(c + 1) * cw:(c + 2) * cw, :])
            cols = slice(c * cw, (c + 1) * cw)
            gv, uv = g_ref[:, cols].astype(F32), u_ref[:, cols].astype(F32)
            sg = jax.nn.sigmoid(gv)
            du = (da * (gv * sg)).astype(BF16)
            dg = (da * uv * (sg * (1.0 + gv * (1.0 - sg)))).astype(BF16)
            dg_ref[:, cols] = dg
            du_ref[:, cols] = du
            part = _dot_nn(dg, wg_ref[cols, :]) + _dot_nn(du, wu_ref[cols, :])
            dhn = part if dhn is None else dhn + part
        xv = x_ref[...]
        rx = _rstd(xv)
        xr = xv * rx
        dsh_ref[...] += jnp.sum(dhn, axis=0, keepdims=True)
        dsc_ref[...] += jnp.sum(dhn * (xr * prg_ref[...]), axis=0, keepdims=True)
        dn = dhn * (1.0 + sc_ref[...])
        dpg_ref[...] += jnp.sum(dn * xr, axis=0, keepdims=True)
        dxo_ref[...] = dx + _rms_bwd(xv, rx, dn * prg_ref[...])

    vec = pl.BlockSpec((1, d_dim), lambda i: (0, 0))
    row = pl.BlockSpec((tm, d_dim), lambda i: (i, 0))
    act = pl.BlockSpec((tm, f_dim), lambda i: (i, 0))
    weight = pl.BlockSpec((f_dim, d_dim), lambda i: (0, 0), pipeline_mode=pl.Buffered(1))
    vshape = jax.ShapeDtypeStruct((1, d_dim), F32)
    act_shape = jax.ShapeDtypeStruct((s_dim, f_dim), BF16)
    need = (3 * f_dim * d_dim * 2 + 2 * (3 * tm * d_dim * 4 + 2 * tm * f_dim * 2) + 2 * (tm * d_dim * 2 + 2 * tm * f_dim * 2 + tm * d_dim * 4)
            + 6 * tm * cw * 4 + 6 * tm * d_dim * 4)
    return _pcall(
        body, out_shape=(jax.ShapeDtypeStruct((s_dim, d_dim), BF16), act_shape, act_shape, jax.ShapeDtypeStruct((s_dim, d_dim), F32),
                         vshape, vshape, vshape, vshape, vshape),
        grid=(s_dim // tm,), in_specs=[row, row, act, act, row, vec, vec, vec, vec, weight, weight, weight],
        out_specs=(row, act, act, row, vec, vec, vec, vec, vec), name=name,
        compiler_params=pltpu.CompilerParams(dimension_semantics=("arbitrary",),
                                             vmem_limit_bytes=int(min(need + VMEM_RESERVE, V7X_VMEM_BYTES - VMEM_RESERVE))),
    )(dx_out, f, g, u, x, post_g, gate, pre_g, scale, wd, wg_t, wu_t)


def _rope_tables(zero=0.0):
    half = QK_ROPE // 2
    freqs = ROPE_THETA ** (-jnp.arange(half, dtype=F32) / half)
    ang = (jnp.arange(SEQ, dtype=F32)[:, None] + zero) * freqs[None, :]
    cos, sin = jnp.cos(ang), jnp.sin(ang)
    ones = jnp.ones((SEQ, QK_NOPE), F32)
    zeros = jnp.zeros((SEQ, QK_NOPE), F32)
    pad1 = jnp.ones((SEQ, HEAD_PAD - QK_NOPE - QK_ROPE), F32)
    pad0 = jnp.zeros((SEQ, HEAD_PAD - QK_NOPE - QK_ROPE), F32)
    zh = jnp.zeros((SEQ, half), F32)
    c = jnp.concatenate([ones, cos, cos, pad1], axis=1)
    s1 = jnp.concatenate([zeros, -sin, zh, pad0], axis=1)
    s2 = jnp.concatenate([zeros, zh, sin, pad0], axis=1)
    return c, s1, s2


def _rope(v, c, s1, s2):
    half = QK_ROPE // 2
    return v * c + pltpu.roll(v, HEAD_PAD - half, 1) * s1 + pltpu.roll(v, half, 1) * s2


def _rope_t(dv, c, s1, s2):
    half = QK_ROPE // 2
    return dv * c + pltpu.roll(dv * s1, half, 1) + pltpu.roll(dv * s2, HEAD_PAD - half, 1)


def _mla_qkv(lat, q_norm, kv_norm, wq_t, wkv_t, rope, name):
    s_dim = lat.shape[0]
    width = MLA_HEADS * HEAD_PAD
    tm = 256

    def body(lat_ref, qg_ref, kg_ref, wq_ref, wkv_ref, c_ref, s1_ref, s2_ref, q_ref, k_ref, v_ref, qn_ref, kvn_ref):
        cq = lat_ref[:, :Q_LORA]
        ckv = lat_ref[:, Q_LORA:Q_LORA + KV_LORA]
        kr = lat_ref[:, Q_LORA + KV_LORA:]
        c, s1, s2 = c_ref[...], s1_ref[...], s2_ref[...]
        qn = (cq * _rstd(cq) * qg_ref[...]).astype(BF16)
        kvn = (ckv * _rstd(ckv) * kg_ref[...]).astype(BF16)
        qn_ref[...] = qn
        kvn_ref[...] = kvn
        q = _dot_nt(qn, wq_ref[...])
        kv = _dot_nt(kvn, wkv_ref[...])
        krr = _rope(kr, c, s1, s2)
        low = lax.broadcasted_iota(jnp.int32, (tm, HEAD_PAD), 1) < QK_NOPE
        for h in range(MLA_HEADS):
            sl = slice(h * HEAD_PAD, (h + 1) * HEAD_PAD)
            q_ref[:, sl] = _rope(q[:, sl], c, s1, s2).astype(BF16)
            kvh = kv[:, sl]
            k_ref[:, sl] = (jnp.where(low, kvh, 0.0) + krr).astype(BF16)
            v_ref[:, sl] = jnp.where(low, 0.0, kvh).astype(BF16)

    row = lambda n: pl.BlockSpec((tm, n), lambda i: (i, 0))
    full = lambda a: pl.BlockSpec(a.shape, lambda i: (0, 0))
    wide = jax.ShapeDtypeStruct((s_dim, width), BF16)
    return _pcall(
        body,
        out_shape=(wide, wide, wide, jax.ShapeDtypeStruct((s_dim, Q_LORA), BF16), jax.ShapeDtypeStruct((s_dim, KV_LORA), BF16)),
        grid=(s_dim // tm,),
        in_specs=[row(LAT_PAD), full(q_norm), full(kv_norm), full(wq_t), full(wkv_t), row(HEAD_PAD), row(HEAD_PAD), row(HEAD_PAD)],
        out_specs=(row(width), row(width), row(width), row(Q_LORA), row(KV_LORA)), name=name,
        compiler_params=_params(("parallel",), [((tm, LAT_PAD), F32), (wq_t.shape, BF16), (wkv_t.shape, BF16)]
                                + 3 * [((tm, width), BF16)], extra=4 * tm * width * 4),
    )(lat, q_norm, kv_norm, wq_t, wkv_t, *rope)


def _mla_scores(q, k_ref, t, tq):
    lo = t * tq
    own = slice(lo, lo + tq)
    scores = [(_dot_nt(q, k_ref[own, :]), own)]
    if t > 0:
        scores.append((_dot_nt(q, k_ref[0:lo, :]), slice(0, lo)))
    return scores


def _mla_softmax(scores):
    s_own = scores[0][0] * MLA_SCALE
    rows = lax.broadcasted_iota(jnp.int32, s_own.shape, 0)
    cols = lax.broadcasted_iota(jnp.int32, s_own.shape, 1)
    s_own = jnp.where(cols <= rows, s_own, -jnp.inf)
    mx = jnp.max(s_own, axis=-1, keepdims=True)
    if len(scores) == 1:
        e_own = jnp.exp(s_own - mx)
        return [(e_own * (1.0 / jnp.sum(e_own, axis=-1, keepdims=True)), scores[0][1])]
    s_pre = scores[1][0] * MLA_SCALE
    mx = jnp.maximum(mx, jnp.max(s_pre, axis=-1, keepdims=True))
    e_own, e_pre = jnp.exp(s_own - mx), jnp.exp(s_pre - mx)
    inv = 1.0 / (jnp.sum(e_own, axis=-1, keepdims=True) + jnp.sum(e_pre, axis=-1, keepdims=True))
    return [(e_pre * inv, scores[1][1]), (e_own * inv, scores[0][1])]


def _mla_attn_fwd(q, k, v, name):
    s_dim = q.shape[0]
    tq = 512

    def body(q_ref, k_ref, v_ref, o_ref):
        n_tiles = s_dim // tq
        tile_of = lambda t: slice(t * tq, (t + 1) * tq)
        def weighted_values(t, probs):
            o = None
            for p, keys in probs:
                part = _dot_nn(p, v_ref[keys, :])
                o = part if o is None else o + part
            o_ref[tile_of(t), :] = o.astype(BF16)

        scores = _mla_scores(q_ref[tile_of(0), :], k_ref, 0, tq)
        probs = None
        for t in range(n_tiles):
            ahead = _mla_scores(q_ref[tile_of(t + 1), :], k_ref, t + 1, tq) if t + 1 < n_tiles else None
            if probs is not None:
                weighted_values(t - 1, probs)
            probs = [(p.astype(BF16), keys) for p, keys in _mla_softmax(scores)]
            scores = ahead
        weighted_values(n_tiles - 1, probs)

    head = pl.BlockSpec((s_dim, HEAD_PAD), lambda h: (0, h))
    return _pcall(
        body, out_shape=jax.ShapeDtypeStruct(q.shape, BF16), grid=(MLA_HEADS,),
        in_specs=[head, head, head], out_specs=head, name=name,
        compiler_params=_params(("parallel",), 4 * [((s_dim, HEAD_PAD), BF16)], extra=4 * tq * s_dim * 4),
    )(q, k, v)


def _mla_attn_bwd(q, k, v, d_o, name):
    s_dim = q.shape[0]
    tq = 512

    def body(q_ref, k_ref, v_ref, do_ref, dq_ref, dk_ref, dv_ref):
        dk_ref[...] = jnp.zeros_like(dk_ref)
        dv_ref[...] = jnp.zeros_like(dv_ref)
        n_tiles = s_dim // tq
        tile_of = lambda t: slice(t * tq, (t + 1) * tq)

        def products(t):
            scores = _mla_scores(q_ref[tile_of(t), :], k_ref, t, tq)
            dot = do_ref[tile_of(t), :].astype(BF16)
            return scores, [_dot_nt(dot, v_ref[keys, :]) for _, keys in scores]

        def gradients_of_scores(scores, dps):
            probs = _mla_softmax(scores)
            dp_of = {(keys.start, keys.stop): dp for (_, keys), dp in zip(scores, dps)}
            terms = [(p, keys, dp_of[keys.start, keys.stop]) for p, keys in probs]
            row = None
            for p, _, dp in terms:
                part = jnp.sum(p * dp, axis=-1, keepdims=True)
                row = part if row is None else row + part
            return [((p * (dp - row) * MLA_SCALE).astype(BF16), p.astype(BF16), keys) for p, keys, dp in terms]

        def accumulate(t, terms):
            qt = q_ref[tile_of(t), :]
            dot = do_ref[tile_of(t), :].astype(BF16)
            dq = None
            for dsb, pb, keys in terms:
                part = _dot_nn(dsb, k_ref[keys, :])
                dq = part if dq is None else dq + part
                dk_ref[keys, :] += _dot_tn(dsb, qt)
                dv_ref[keys, :] += _dot_tn(pb, dot)
            dq_ref[tile_of(t), :] = dq

        ready = products(0)
        terms = None
        for t in range(n_tiles):
            ahead = products(t + 1) if t + 1 < n_tiles else None
            if terms is not None:
                accumulate(t - 1, terms)
            terms = gradients_of_scores(*ready)
            ready = ahead
        accumulate(n_tiles - 1, terms)

    head = pl.BlockSpec((s_dim, HEAD_PAD), lambda h: (0, h))
    out = jax.ShapeDtypeStruct(q.shape, F32)
    return _pcall(
        body, out_shape=(out, out, out), grid=(MLA_HEADS,),
        in_specs=[head, head, head, head], out_specs=(head, head, head), name=name,
        compiler_params=_params(("parallel",), 3 * [((s_dim, HEAD_PAD), BF16)] + 4 * [((s_dim, HEAD_PAD), F32)],
                                extra=6 * tq * s_dim * 4),
    )(q, k, v, d_o)


def _mla_qkv_bwd(dq, dk, dv, lat, q_norm, kv_norm, wq_t, wkv_t, rope, name):
    s_dim = lat.shape[0]
    width = MLA_HEADS * HEAD_PAD
    tm = 256

    def body(dq_ref, dk_ref, dv_ref, lat_ref, qg_ref, kg_ref, wq_ref, wkv_ref, c_ref, s1_ref, s2_ref,
             dqp_ref, dkv_ref, dlat_ref, dqg_ref, dkg_ref):
        @pl.when(pl.program_id(0) == 0)
        def _():
            dqg_ref[...] = jnp.zeros_like(dqg_ref)
            dkg_ref[...] = jnp.zeros_like(dkg_ref)

        c, s1, s2 = c_ref[...], s1_ref[...], s2_ref[...]
        lane = lax.broadcasted_iota(jnp.int32, (tm, HEAD_PAD), 1)
        low = lane < QK_NOPE
        rot = (lane >= QK_NOPE) & (lane < QK_NOPE + QK_ROPE)
        dkrr = jnp.zeros((tm, HEAD_PAD), F32)
        for h in range(MLA_HEADS):
            sl = slice(h * HEAD_PAD, (h + 1) * HEAD_PAD)
            dqp_ref[:, sl] = _rope_t(dq_ref[:, sl], c, s1, s2).astype(BF16)
            dkh = dk_ref[:, sl]
            dkv_ref[:, sl] = jnp.where(low, dkh, dv_ref[:, sl]).astype(BF16)
            dkrr = dkrr + jnp.where(rot, dkh, 0.0)
        dqn = _dot_nn(dqp_ref[...], wq_ref[...])
        dkvn = _dot_nn(dkv_ref[...], wkv_ref[...])
        cq = lat_ref[:, :Q_LORA]
        ckv = lat_ref[:, Q_LORA:Q_LORA + KV_LORA]
        rq, rkv = _rstd(cq), _rstd(ckv)
        dqg_ref[...] += jnp.sum(dqn * cq * rq, axis=0, keepdims=True)
        dkg_ref[...] += jnp.sum(dkvn * ckv * rkv, axis=0, keepdims=True)
        dlat_ref[:, :Q_LORA] = _rms_bwd(cq, rq, dqn * qg_ref[...])
        dlat_ref[:, Q_LORA:Q_LORA + KV_LORA] = _rms_bwd(ckv, rkv, dkvn * kg_ref[...])
        dlat_ref[:, Q_LORA + KV_LORA:] = _rope_t(dkrr, c, s1, s2)

    row = lambda n: pl.BlockSpec((tm, n), lambda i: (i, 0))
    full = lambda a: pl.BlockSpec(a.shape, lambda i: (0, 0))
    wide = jax.ShapeDtypeStruct((s_dim, width), BF16)
    return _pcall(
        body,
        out_shape=(wide, wide, jax.ShapeDtypeStruct((s_dim, LAT_PAD), F32),
                   jax.ShapeDtypeStruct(q_norm.shape, F32), jax.ShapeDtypeStruct(kv_norm.shape, F32)),
        grid=(s_dim // tm,),
        in_specs=[row(width), row(width), row(width), row(LAT_PAD), full(q_norm), full(kv_norm), full(wq_t), full(wkv_t),
                  row(HEAD_PAD), row(HEAD_PAD), row(HEAD_PAD)],
        out_specs=(row(width), row(width), row(LAT_PAD), full(q_norm), full(kv_norm)), name=name,
        compiler_params=_params(("arbitrary",), 3 * [((tm, width), F32)] + [((tm, LAT_PAD), F32), (wq_t.shape, BF16),
                                                                           (wkv_t.shape, BF16)] + 2 * [((tm, width), BF16)],
                                extra=2 * tm * width * 4),
    )(dq, dk, dv, lat, q_norm, kv_norm, wq_t, wkv_t, *rope)


def _t5_bucket(dist):
    max_exact = N_BUCKETS // 2
    d = jnp.maximum(dist, 1).astype(F32)
    large = max_exact + (jnp.log(d / max_exact) / math.log(MAX_DISTANCE / max_exact)
                         * (N_BUCKETS - max_exact)).astype(jnp.int32)
    large = jnp.minimum(large, N_BUCKETS - 1)
    return jnp.where(dist < max_exact, dist, large)


def _dil_buckets(dilation):
    iq = jnp.arange(DIL_BLOCK)[:, None]
    ik = jnp.arange(2 * DIL_BLOCK)[None, :]
    return _t5_bucket(jnp.maximum(DIL_BLOCK + iq - ik, 0) * dilation)


def _dil_logits(qh, kb, bias_h, first, span):
    if first:
        s = _dot_nt(qh, kb) * DIL_SCALE + bias_h[:, DIL_BLOCK:]
        rel = lax.broadcasted_iota(jnp.int32, s.shape, 0) - lax.broadcasted_iota(jnp.int32, s.shape, 1)
    else:
        s = _dot_nt(qh, kb) * DIL_SCALE + bias_h
        rel = DIL_BLOCK + lax.broadcasted_iota(jnp.int32, s.shape, 0) - lax.broadcasted_iota(jnp.int32, s.shape, 1)
    return jnp.where((rel >= 0) & (rel <= span), s, -jnp.inf)


def _dil_blocks(s_dim, dilation):
    rows = s_dim // dilation
    for r in range(dilation):
        for n in range(rows // DIL_BLOCK):
            lo = r * rows + n * DIL_BLOCK
            keys = slice(lo, lo + DIL_BLOCK) if n == 0 else slice(lo - DIL_BLOCK, lo + DIL_BLOCK)
            start = r + n * DIL_BLOCK * dilation
            tokens = slice(start, start + DIL_BLOCK) if dilation == 1 else pl.ds(start, DIL_BLOCK, stride=dilation)
            yield n == 0, slice(lo, lo + DIL_BLOCK), keys, tokens


def _dil_views(s_dim):
    col = lambda which: pl.BlockSpec((s_dim, HEAD_PAD), lambda p: (0, which * DIL_PAIRS + p))
    nat = pl.BlockSpec((s_dim, HEAD_PAD), lambda p: (0, p))
    bias = pl.BlockSpec((2, DIL_BLOCK, 2 * DIL_BLOCK), lambda p: (p, 0, 0))
    return col, nat, bias


def _dil_attn_fwd(qkv, bias, dilation, span, name):
    s_dim = qkv.shape[0]
    d_dim = DIL_HEADS * DIL_HEAD_DIM
    col, nat, bias_spec = _dil_views(s_dim)

    def body(q_ref, k_ref, v_ref, b_ref, o_ref, l_ref):
        lane = lax.broadcasted_iota(jnp.int32, (DIL_BLOCK, HEAD_PAD), 1)
        klane = lax.broadcasted_iota(jnp.int32, (2 * DIL_BLOCK, HEAD_PAD), 1)
        blocks = list(_dil_blocks(s_dim, dilation))
        for g0 in range(0, len(blocks), DIL_GROUPED):
            group = blocks[g0:g0 + DIL_GROUPED]
            logits = [_dil_logits(jnp.where((lane < DIL_HEAD_DIM) == (h == 0), q_ref[blk, :], 0), k_ref[keys, :], b_ref[h],
                                  first, span) for first, blk, keys, _ in group for h in range(2)]
            soft = []
            for lg in logits:
                mx = jnp.max(lg, axis=-1, keepdims=True)
                e = jnp.exp(lg - mx)
                tot = jnp.sum(e, axis=-1, keepdims=True)
                soft.append(((e * (1.0 / tot)).astype(BF16), mx + jnp.log(tot)))
            for i, (_, _, keys, tokens) in enumerate(group):
                vb = v_ref[keys, :]
                o_acc = jnp.zeros((DIL_BLOCK, HEAD_PAD), F32)
                lse_acc = jnp.zeros((DIL_BLOCK, HEAD_PAD), F32)
                for h in range(2):
                    p, lse = soft[2 * i + h]
                    kmine = (klane[:vb.shape[0]] < DIL_HEAD_DIM) == (h == 0)
                    o_acc = o_acc + _dot_nn(p, jnp.where(kmine, vb, 0))
                    lse_acc = jnp.where((lane < DIL_HEAD_DIM) == (h == 0), lse, lse_acc)
                o_ref[tokens, :] = o_acc
                l_ref[tokens, :] = lse_acc

    out = jax.ShapeDtypeStruct((s_dim, d_dim), F32)
    return _pcall(
        body, out_shape=(out, out), grid=(DIL_PAIRS,),
        in_specs=[col(0), col(1), col(2), bias_spec], out_specs=(nat, nat), name=name,
        compiler_params=_params(("parallel",), 3 * [((s_dim, HEAD_PAD), BF16)] + 2 * [((s_dim, HEAD_PAD), F32)]
                                + [((2, DIL_BLOCK, 2 * DIL_BLOCK), F32)], extra=2**21),
    )(qkv, qkv, qkv, bias)


def _dil_mix(lses, outs, name):
    s_dim, d_dim = outs[0].shape
    tm = TOKEN_TILE
    ng = len(outs)

    def body(*refs):
        ls = [refs[g][...] for g in range(ng)]
        mx = ls[0]
        for g in range(1, ng):
            mx = jnp.maximum(mx, ls[g])
        es = [jnp.exp(l - mx) for l in ls]
        tot = es[0]
        for g in range(1, ng):
            tot = tot + es[g]
        o = None
        for g in range(ng):
            al = es[g] / tot
            refs[2 * ng + g][...] = al
            t = al * refs[ng + g][...]
            o = t if o is None else o + t
        refs[3 * ng][...] = o
        refs[3 * ng + 1][...] = o.astype(BF16)

    row = pl.BlockSpec((tm, d_dim), lambda i: (i, 0))
    f = jax.ShapeDtypeStruct((s_dim, d_dim), F32)
    res = _pcall(
        body, out_shape=tuple(ng * [f] + [f, jax.ShapeDtypeStruct((s_dim, d_dim), BF16)]), grid=(s_dim // tm,),
        in_specs=2 * ng * [row], out_specs=tuple((ng + 2) * [row]), name=name,
        compiler_params=_params(("parallel",), (3 * ng + 2) * [((tm, d_dim), F32)], extra=4 * tm * d_dim * 4),
    )(*lses, *outs)
    return res[:ng], res[ng], res[ng + 1]


def _dil_attn_bwd(qkv, bias, d_o, o_mix, alpha, lse, dilation, span, name):
    s_dim = qkv.shape[0]
    d_dim = DIL_HEADS * DIL_HEAD_DIM
    col, nat, bias_spec = _dil_views(s_dim)

    def body(q_ref, k_ref, v_ref, b_ref, do_ref, om_ref, al_ref, l_ref, dq_ref, dk_ref, dv_ref, db_ref, dk_acc, dv_acc):
        db_ref[...] = jnp.zeros_like(db_ref)
        dk_acc[...] = jnp.zeros_like(dk_acc)
        dv_acc[...] = jnp.zeros_like(dv_acc)
        lane = lax.broadcasted_iota(jnp.int32, (DIL_BLOCK, HEAD_PAD), 1)
        klane = lax.broadcasted_iota(jnp.int32, (2 * DIL_BLOCK, HEAD_PAD), 1)
        blocks = list(_dil_blocks(s_dim, dilation))
        heads = [(lane < DIL_HEAD_DIM) == (h == 0) for h in range(2)]
        for g0 in range(0, len(blocks), DIL_GROUPED):
            group = blocks[g0:g0 + DIL_GROUPED]
            staged = []
            for first, blk, kv_rows, tokens in group:
                qb, kb, vb = q_ref[blk, :], k_ref[kv_rows, :], v_ref[kv_rows, :]
                dog = al_ref[tokens, :] * do_ref[tokens, :]
                row_term = dog * om_ref[tokens, :]
                lse_b = l_ref[tokens, :]
                for h in range(2):
                    qh = jnp.where(heads[h], qb, 0)
                    dogh = jnp.where(heads[h], dog, 0.0).astype(BF16)
                    staged.append((_dil_logits(qh, kb, b_ref[h], first, span), _dot_nt(dogh, vb), qh, dogh,
                                   jnp.max(jnp.where(heads[h], lse_b, -jnp.inf), axis=-1, keepdims=True),
                                   jnp.sum(jnp.where(heads[h], row_term, 0.0), axis=-1, keepdims=True)))
            grads = []
            for i, (logits, dp, qh, dogh, lse_h, row) in enumerate(staged):
                p = jnp.exp(logits - lse_h)
                ds = p * (dp - row)
                if group[i // 2][0]:
                    db_ref[i % 2, :, DIL_BLOCK:] += ds
                else:
                    db_ref[i % 2] += ds
                grads.append(((ds * DIL_SCALE).astype(BF16), p.astype(BF16), qh, dogh))
            for i, (_, blk, kv_rows, _) in enumerate(group):
                kb = k_ref[kv_rows, :]
                dq_acc = jnp.zeros((DIL_BLOCK, HEAD_PAD), F32)
                dk_blk = jnp.zeros((kb.shape[0], HEAD_PAD), F32)
                dv_blk = jnp.zeros((kb.shape[0], HEAD_PAD), F32)
                for h in range(2):
                    dsb, pb, qh, dogh = grads[2 * i + h]
                    kmine = (klane[:kb.shape[0]] < DIL_HEAD_DIM) == (h == 0)
                    dq_acc = dq_acc + _dot_nn(dsb, jnp.where(kmine, kb, 0))
                    dk_blk = dk_blk + _dot_tn(dsb, qh)
                    dv_blk = dv_blk + _dot_tn(pb, dogh)
                dq_ref[blk, :] = dq_acc.astype(BF16)
                dk_acc[kv_rows, :] += dk_blk
                dv_acc[kv_rows, :] += dv_blk
        dk_ref[...] = dk_acc[...].astype(BF16)
        dv_ref[...] = dv_acc[...].astype(BF16)

    grad = jax.ShapeDtypeStruct((s_dim, d_dim), BF16)
    return _pcall(
        body, out_shape=(grad, grad, grad, jax.ShapeDtypeStruct(bias.shape, F32)), grid=(DIL_PAIRS,),
        in_specs=[col(0), col(1), col(2), bias_spec, nat, nat, nat, nat],
        out_specs=(nat, nat, nat, bias_spec), name=name,
        scratch_shapes=[pltpu.VMEM((s_dim, HEAD_PAD), F32), pltpu.VMEM((s_dim, HEAD_PAD), F32)],
        compiler_params=_params(("parallel",), 6 * [((s_dim, HEAD_PAD), BF16)] + 4 * [((s_dim, HEAD_PAD), F32)]
                                + 2 * [((2, DIL_BLOCK, 2 * DIL_BLOCK), F32)], extra=2 * s_dim * HEAD_PAD * 4 + 2**21),
    )(qkv, qkv, qkv, bias, d_o, o_mix, alpha, lse)


def _bias_reduce(dbias, buckets, name):
    n_heads = dbias.shape[0]

    def body(db_ref, bk_ref, o_ref):
        ds, bk = db_ref[0], bk_ref[0]
        lane = lax.broadcasted_iota(jnp.int32, (8, HEAD_PAD), 1)
        acc = jnp.zeros((8, HEAD_PAD), F32)
        for b in range(N_BUCKETS):
            acc = jnp.where(lane == b, jnp.sum(jnp.where(bk == b, ds, 0.0)), acc)
        o_ref[0] = acc

    blk = (1, DIL_BLOCK, 2 * DIL_BLOCK)
    return _pcall(
        body, out_shape=jax.ShapeDtypeStruct((n_heads, 8, HEAD_PAD), F32), grid=(n_heads,),
        in_specs=[pl.BlockSpec(blk, lambda h: (h, 0, 0)), pl.BlockSpec(blk, lambda h: (h // DIL_HEADS, 0, 0))],
        out_specs=pl.BlockSpec((1, 8, HEAD_PAD), lambda h: (h, 0, 0)), name=name,
        compiler_params=_params(("parallel",), [(blk, F32), (blk, jnp.int32)], extra=2**20),
    )(dbias, buckets)


def _loss_grad(y, target, name):
    s_dim, d_dim = y.shape
    tm = TOKEN_TILE

    def body(y_ref, t_ref, dy_ref, l_ref):
        @pl.when(pl.program_id(0) == 0)
        def _():
            l_ref[...] = jnp.zeros_like(l_ref)

        err = y_ref[...] - t_ref[...]
        dy_ref[...] = err / d_dim
        sq = (err * err).reshape(tm // 8, 8, d_dim)
        l_ref[...] += 0.5 * jnp.sum(sq, axis=0) / d_dim

    row = pl.BlockSpec((tm, d_dim), lambda i: (i, 0))
    acc = pl.BlockSpec((8, d_dim), lambda i: (0, 0))
    return _pcall(
        body, out_shape=(jax.ShapeDtypeStruct((s_dim, d_dim), F32), jax.ShapeDtypeStruct((8, d_dim), F32)),
        grid=(s_dim // tm,), in_specs=[row, row], out_specs=(row, acc), name=name,
        compiler_params=_params(("arbitrary",), 3 * [((tm, d_dim), F32)], extra=2 * tm * d_dim * 4),
    )(y, target)


def _mod_fwd(c_all, w_mod, b_loc, name):
    depth, d_dim, n = w_mod.shape
    nb = c_all.shape[0]

    def body(c_ref, w_ref, b_ref, o_ref, s_ref):
        cv = c_ref[...]
        sc = cv * jax.nn.sigmoid(cv)
        s_ref[...] = sc
        o_ref[0] = _dot_nn(sc.astype(BF16), w_ref[0].astype(BF16)) + b_ref[0]

    return _pcall(
        body, out_shape=(jax.ShapeDtypeStruct((depth, nb, n), F32), jax.ShapeDtypeStruct((nb, d_dim), F32)), grid=(depth,),
        in_specs=[pl.BlockSpec((nb, d_dim), lambda i: (0, 0)), pl.BlockSpec((1, d_dim, n), lambda i: (i, 0, 0)),
                  pl.BlockSpec((1, 1, n), lambda i: (i, 0, 0))],
        out_specs=(pl.BlockSpec((1, nb, n), lambda i: (i, 0, 0)), pl.BlockSpec((nb, d_dim), lambda i: (0, 0))), name=name,
        compiler_params=_params(("arbitrary",), [((1, d_dim, n), F32)], extra=d_dim * n * 2 + 2**20),
    )(c_all, w_mod, b_loc.reshape(depth, 1, n))


def _sum_parts(parts, name):
    _, rows, cols = parts.shape
    fits = [t for t in range(16, rows // 2 + 1, 16) if rows % t == 0 and NDEV * t * cols * parts.dtype.itemsize <= 3 * 2**20]
    tr = max(fits) if fits else rows

    def body(p_ref, o_ref):
        acc = p_ref[0].astype(F32)
        for k in range(1, NDEV):
            acc = acc + p_ref[k].astype(F32)
        o_ref[...] = acc

    return _pcall(
        body, out_shape=jax.ShapeDtypeStruct((rows, cols), F32), grid=(rows // tr,),
        in_specs=[pl.BlockSpec((NDEV, tr, cols), lambda i: (0, i, 0))], out_specs=pl.BlockSpec((tr, cols), lambda i: (i, 0)),
        name=name, compiler_params=_params(("parallel",), [((NDEV, tr, cols), parts.dtype), ((tr, cols), F32)], extra=2**20),
    )(parts)


def _adamw(w, g, m, v, name):
    shape = w.shape
    cols = shape[-1]
    rows = math.prod(shape[:-1])
    tr = rows
    for cand in (512, 256, 128, 64, 32, 16, 8):
        if rows % cand == 0 and rows > cand and cand * cols * 4 <= 2**21:
            tr = cand
            break

    def body(w_ref, g_ref, m_ref, v_ref, d_ref, mo_ref, vo_ref):
        gv = g_ref[...]
        mn = ADAM_B1 * m_ref[...] + (1.0 - ADAM_B1) * gv
        vn = ADAM_B2 * v_ref[...] + (1.0 - ADAM_B2) * (gv * gv)
        m_hat = mn / (1.0 - ADAM_B1 ** ADAM_STEP)
        v_hat = vn / (1.0 - ADAM_B2 ** ADAM_STEP)
        d_ref[...] = -ADAM_LR * (m_hat / (jnp.sqrt(v_hat) + ADAM_EPS) + ADAM_WD * w_ref[...])
        mo_ref[...] = mn
        vo_ref[...] = vn

    blk = pl.BlockSpec((tr, cols), lambda i: (i, 0))
    out = jax.ShapeDtypeStruct((rows, cols), F32)
    res = _pcall(
        body, out_shape=(out, out, out), grid=(rows // tr,), in_specs=4 * [blk], out_specs=(blk, blk, blk), name=name,
        compiler_params=_params(("parallel",), 7 * [((tr, cols), F32)], extra=4 * tr * cols * 4),
    )(*(a.reshape(rows, cols) for a in (w, g, m, v)))
    return tuple(r.reshape(shape) for r in res)


def _peers():
    x, y, c = lax.axis_index("x"), lax.axis_index("y"), lax.axis_index("c")
    flip = lambda v, f: 1 - v if f else v
    peers = []
    for f in range(1, NDEV):
        px, py, pc = flip(x, f & 4), flip(y, f & 2), flip(c, f & 1)
        peers.append(((px, py, pc), 4 * px + 2 * py + pc))
    return (x, y, c), 4 * x + 2 * y + c, peers


def _places():
    x, y, c = lax.axis_index("x"), lax.axis_index("y"), lax.axis_index("c")
    place = lambda px, py, pc: ((px, py, pc), 4 * px + 2 * py + pc)
    return place(x, y, c), place(x, y, 1 - c), [place(1 - x, y, c), place(x, 1 - y, c), place(1 - x, 1 - y, c)]


def _exchange(arrs, gather, name):
    n = len(arrs)
    hbm = pl.BlockSpec(memory_space=pltpu.HBM)
    if gather:
        out_shape = [jax.ShapeDtypeStruct((NDEV * a.shape[0], a.shape[1]), a.dtype) for a in arrs]
    else:
        out_shape = [jax.ShapeDtypeStruct((NDEV, a.shape[0] // NDEV, a.shape[1]), a.dtype) for a in arrs]

    def body(*refs):
        ins, outs = refs[:n], refs[n:2 * n]
        send_sems, recv_sems, local_sems = refs[2 * n:]
        me_pos, me, peers = _peers()
        local = []
        for k in range(n):
            rows = arrs[k].shape[0] if gather else arrs[k].shape[0] // NDEV
            if gather:
                src_of = lambda idx: ins[k]
                dst_of = lambda idx: outs[k].at[pl.ds(me * rows, rows)]
                mine = (ins[k], outs[k].at[pl.ds(me * rows, rows)])
            else:
                src_of = lambda idx: ins[k].at[pl.ds(idx * rows, rows)]
                dst_of = lambda idx: outs[k].at[me]
                mine = (ins[k].at[pl.ds(me * rows, rows)], outs[k].at[me])
            cp = pltpu.make_async_copy(mine[0], mine[1], local_sems.at[k])
            cp.start()
            local.append(cp)
            for pos, idx in peers:
                pltpu.make_async_remote_copy(src_ref=src_of(idx), dst_ref=dst_of(idx), send_sem=send_sems.at[k],
                                             recv_sem=recv_sems.at[k], device_id=pos, device_id_type=MESH).start()
        for k in range(n):
            rows = arrs[k].shape[0] if gather else arrs[k].shape[0] // NDEV
            sent = ins[k].at[pl.ds(0, (NDEV - 1) * rows)] if not gather else outs[k].at[pl.ds(0, (NDEV - 1) * rows)]
            got = outs[k].at[pl.ds(0, (NDEV - 1) * rows)] if gather else outs[k].at[pl.ds(0, NDEV - 1)]
            pltpu.make_async_remote_copy(src_ref=sent, dst_ref=sent, send_sem=send_sems.at[k], recv_sem=recv_sems.at[k],
                                         device_id=me_pos, device_id_type=MESH).wait_send()
            pltpu.make_async_remote_copy(src_ref=got, dst_ref=got, send_sem=send_sems.at[k], recv_sem=recv_sems.at[k],
                                         device_id=me_pos, device_id_type=MESH).wait_recv()
            local[k].wait()

    return pl.pallas_call(
        body, out_shape=out_shape, in_specs=n * [hbm], out_specs=n * [hbm], name=name,
        scratch_shapes=[pltpu.SemaphoreType.DMA((n,)), pltpu.SemaphoreType.DMA((n,)), pltpu.SemaphoreType.DMA((n,))],
        compiler_params=pltpu.CompilerParams(has_side_effects=True),
    )(*arrs)


_HBM = pl.BlockSpec(memory_space=pltpu.HBM)
_SEM = pl.BlockSpec(memory_space=pltpu.SEMAPHORE)
_DATAFLOW = pltpu.SideEffectType.DATAFLOW_SIDE_EFFECTING


def _split_start(srcs, groups, gather, name):
    n = len(srcs)
    if gather:
        lands = [lax.empty((NDEV * a.shape[0], a.shape[1]), a.dtype) for a in srcs]
    else:
        lands = [lax.empty((NDEV, a.shape[0] // NDEV, a.shape[1]), a.dtype) for a in srcs]
    n_sem = 3 * len(groups)

    def body(*refs):
        src_refs, land_refs = refs[:n], refs[n:2 * n]
        sems = refs[2 * n:2 * n + n_sem]
        token = refs[-1]
        (_, my), sibling, chips = _places()
        _, _, peers = _peers()
        targets = [sibling] + chips if gather else peers
        for g, members in enumerate(groups):
            for j, k in enumerate(members):
                _own_copy(src_refs[k], land_refs[k], sems[3 * g + 2].at[j], my, gather).start()
        for g, members in enumerate(groups):
            for j, k in enumerate(members):
                rows = srcs[k].shape[0] if gather else srcs[k].shape[0] // NDEV
                for pos, idx in targets:
                    src = src_refs[k] if gather else src_refs[k].at[pl.ds(idx * rows, rows)]
                    dst = land_refs[k].at[pl.ds(my * rows, rows)] if gather else land_refs[k].at[my]
                    pltpu.make_async_remote_copy(src_ref=src, dst_ref=dst, send_sem=sems[3 * g].at[j],
                                                 recv_sem=sems[3 * g + 1].at[j], device_id=pos, device_id_type=MESH).start()
        token[...] = jnp.zeros_like(token)

    out_shape = []
    for members in groups:
        out_shape += 3 * [pltpu.SemaphoreType.DMA((len(members),))]
    out_shape += [pltpu.HBM(a.shape, a.dtype) for a in srcs] + [pltpu.HBM(a.shape, a.dtype) for a in lands]
    out_shape.append(jax.ShapeDtypeStruct((8, 128), F32))
    res = pl.pallas_call(
        body, name=name, out_shape=tuple(out_shape), in_specs=2 * n * [_HBM],
        out_specs=tuple(n_sem * [_SEM] + 2 * n * [_HBM] + [pl.BlockSpec(memory_space=pltpu.VMEM)]),
        input_output_aliases={i: n_sem + i for i in range(2 * n)},
        compiler_params=pltpu.CompilerParams(has_side_effects=_DATAFLOW),
    )(*[pltpu.with_memory_space_constraint(a, pltpu.HBM) for a in list(srcs) + lands])
    sems = [tuple(res[3 * g:3 * g + 3]) for g in range(len(groups))]
    return sems, list(res[n_sem:n_sem + n]), list(res[n_sem + n:n_sem + 2 * n]), res[-1]


def _own_copy(src_ref, land_ref, sem, my, gather):
    if gather:
        rows = src_ref.shape[0]
        return pltpu.make_async_copy(src_ref, land_ref.at[pl.ds(my * rows, rows)], sem)
    rows = src_ref.shape[0] // NDEV
    return pltpu.make_async_copy(src_ref.at[pl.ds(my * rows, rows)], land_ref.at[my], sem)


def _wait_all(land_ref, blocks_per_dev, copies, send_sem, recv_sem, me_pos):
    part = land_ref.at[pl.ds(0, copies * blocks_per_dev)]
    pltpu.make_async_remote_copy(src_ref=part, dst_ref=part, send_sem=send_sem, recv_sem=recv_sem,
                                 device_id=me_pos, device_id_type=MESH).wait()


def _gather_forward(sems, srcs, lands, after, name):
    n = len(srcs)

    def body(*refs):
        land_refs = refs[n:2 * n]
        send_a, recv_a = refs[2 * n], refs[2 * n + 1]
        send_b, recv_b = refs[2 * n + 3], refs[2 * n + 4]
        token = refs[-1]
        (me_pos, _), sibling, chips = _places()
        for j in range(n):
            _wait_all(land_refs[j], lands[j].shape[0] // NDEV, 1 + OTHER_CHIPS, send_a.at[j], recv_a.at[j], me_pos)
        for j in range(n):
            rows = lands[j].shape[0] // NDEV
            for _, idx in chips:
                block = land_refs[j].at[pl.ds(idx * rows, rows)]
                pltpu.make_async_remote_copy(src_ref=block, dst_ref=block, send_sem=send_b.at[j], recv_sem=recv_b.at[j],
                                             device_id=sibling[0], device_id_type=MESH).start()
        token[...] = jnp.zeros_like(token)

    res = pl.pallas_call(
        body, name=name,
        out_shape=(pltpu.SemaphoreType.DMA((n,)), pltpu.SemaphoreType.DMA((n,)))
        + tuple(pltpu.HBM(a.shape, a.dtype) for a in list(srcs) + list(lands)) + (jax.ShapeDtypeStruct((8, 128), F32),),
        in_specs=2 * n * [_HBM] + [_SEM, _SEM, pl.BlockSpec(memory_space=pl.ANY)],
        out_specs=tuple([_SEM, _SEM] + 2 * n * [_HBM] + [pl.BlockSpec(memory_space=pltpu.VMEM)]),
        input_output_aliases={i: 2 + i for i in range(2 * n)},
        compiler_params=pltpu.CompilerParams(has_side_effects=_DATAFLOW),
    )(*srcs, *lands, sems[0], sems[1], after)
    return (res[0], res[1]), list(res[2:2 + n]), list(res[2 + n:2 + 2 * n]), res[-1]


def _split_wait(sems, srcs, lands, after, copies, gather, name):
    n = len(srcs)

    def body(*refs):
        src_refs, land_refs = refs[:n], refs[n:2 * n]
        send_sem, recv_sem, local_sem = refs[2 * n], refs[2 * n + 1], refs[2 * n + 2]
        (me_pos, my), _, _ = _places()
        for j in range(n):
            _wait_all(land_refs[j], lands[j].shape[0] // NDEV, copies, send_sem.at[j], recv_sem.at[j], me_pos)
            _own_copy(src_refs[j], land_refs[j], local_sem.at[j], my, gather).wait()

    res = pl.pallas_call(
        body, name=name, out_shape=tuple(pltpu.HBM(a.shape, a.dtype) for a in list(srcs) + list(lands)),
        in_specs=2 * n * [_HBM] + [_SEM, _SEM, _SEM, pl.BlockSpec(memory_space=pl.ANY)], out_specs=tuple(2 * n * [_HBM]),
        input_output_aliases={i: i for i in range(2 * n)},
        compiler_params=pltpu.CompilerParams(has_side_effects=_DATAFLOW),
    )(*srcs, *lands, sems[0], sems[1], sems[2], after)
    return list(res[n:])


def _chained(gate, mid, after):
    return gate if mid is None else gate + mid(after)[:1, :1]


def _ffn_fwd(x, norms, mod, w, mid=None):
    (pre_g, post_g), (shift, scale, gate), (wg_t, wu_t, wd) = norms, mod, w
    hn, g, u, a = _ffn_up(x, pre_g, scale, shift, wg_t, wu_t, "ffn_up")
    if callable(wd):
        wd = wd(a)
    x_out, f = _mm_post(a, wd, x, post_g, _chained(gate, mid, a), FFN_RES, "ffn_down")
    return x_out, (x, hn, g, u, a, f), (wg_t, wu_t, wd)


def _ffn_bwd(dx_out, saved, norms, mod, w, send=None):
    (pre_g, post_g), (_, scale, gate), (wg_t, wu_t, wd) = norms, mod, w
    x, hn, g, u, a, f = saved
    d_model = x.shape[1]
    if send is None:
        df, dg, du, dx, dgate, dpost, dshift, dscale, dpre = _ffn_bwd_fused(dx_out, saved, pre_g, post_g, scale, gate,
                                                                            wg_t, wu_t, wd, "ffn_bwd")
        dws = tuple(_mm([pair], "tn", BF16, 256, d_model, "ffn_dw") for pair in ((dg, hn), (du, hn), (a, df)))
        return dx, (dpre, dpost), (dshift, dscale, dgate), dws
    sent = send
    df, dgate, dpost = _post_bwd(dx_out, f, post_g, gate, FFN_RES, "ffn_post_bwd")
    dwd = _mm([(a, df)], "tn", BF16, 256, d_model, "ffn_dw")
    dg, du = _ffn_dgu(df, wd, g, u, "ffn_dgu", after=sent(2, dwd))
    dwg_t = _mm([(dg, hn)], "tn", BF16, 256, d_model, "ffn_dw")
    dwu_t = _mm([(du, hn)], "tn", BF16, 256, d_model, "ffn_dw", after=sent(0, dwg_t))
    dhn = _mm([(dg, wg_t), (du, wu_t)], "nn", F32, TOKEN_TILE, d_model, "ffn_dhn", after=sent(1, dwu_t))
    dx, dshift, dscale, dpre = _prenorm_bwd(dx_out, [dhn], x, pre_g, scale, "prenorm_bwd")
    return dx, (dpre, dpost), (dshift, dscale, dgate), (dwg_t, dwu_t, dwd)


def _mla_fwd(x, norms, mod, w, rope, mid=None):
    (pre_g, post_g), (shift, scale, gate) = norms, mod
    w_in, q_norm, wq_t, kv_norm, wkv_t, wo = w
    hn, lat = _prenorm_mm(x, pre_g, scale, shift, w_in, "nn", F32, LAT_PAD, "mla_in")
    gate = _chained(gate, mid, lat)
    q, k, v, qn, kvn = _mla_qkv(lat, q_norm, kv_norm, wq_t, wkv_t, rope, "mla_qkv")
    o = _mla_attn_fwd(q, k, v, "mla_attn_fwd")
    x_out, f = _mm_post(o, wo, x, post_g, gate, 1.0, "mla_out")
    return x_out, (x, hn, lat, q, k, v, qn, kvn, o, f)


def _mla_bwd(dx_out, saved, norms, mod, w, rope):
    (pre_g, post_g), (_, scale, gate) = norms, mod
    w_in, q_norm, wq_t, kv_norm, wkv_t, wo = w
    x, hn, lat, q, k, v, qn, kvn, o, f = saved
    d_model = x.shape[1]
    df, dgate, dpost = _post_bwd(dx_out, f, post_g, gate, 1.0, "mix_post_bwd")
    d_o = _mm([(df, wo)], "nt", F32, TOKEN_TILE, wo.shape[0], "mla_do")
    dwo = _mm([(o, df)], "tn", BF16, TOKEN_TILE, d_model, "mla_dwo")
    dq, dk, dv = _mla_attn_bwd(q, k, v, d_o, "mla_attn_bwd")
    dqp, dkv, dlat, dq_norm, dkv_norm = _mla_qkv_bwd(dq, dk, dv, lat, q_norm, kv_norm, wq_t, wkv_t, rope, "mla_qkv_bwd")
    dwq_t = _mm([(dqp, qn)], "tn", BF16, TOKEN_TILE, Q_LORA, "mla_dwq")
    dwkv_t = _mm([(dkv, kvn)], "tn", BF16, TOKEN_TILE, KV_LORA, "mla_dwkv")
    dw_in = _mm([(hn, dlat)], "tn", BF16, TOKEN_TILE, LAT_PAD, "mla_dwin")
    dhn = _mm([(dlat, w_in)], "nt", F32, TOKEN_TILE, d_model, "mla_dhn")
    dx, dshift, dscale, dpre = _prenorm_bwd(dx_out, [dhn], x, pre_g, scale, "prenorm_bwd")
    return dx, (dpre, dpost), (dshift, dscale, dgate), (dw_in, dq_norm, dwq_t, dkv_norm, dwkv_t, dwo)


def _dil_fwd(x, norms, mod, w, bias, mid=None):
    (pre_g, post_g), (shift, scale, gate), (w_in_t, wo) = norms, mod, w
    width = 3 * DIL_HEADS * DIL_HEAD_DIM
    hns, qkvs, outs, lses = [], [], [], []
    for g, (window, dilation) in enumerate(DIL_GROUPS):
        hn, qkv = _prenorm_mm(x, pre_g, scale, shift, w_in_t[g * width:(g + 1) * width], "nt", BF16, width,
                              "dil_in", perm=dilation)
        if g == 0:
            gate = _chained(gate, mid, qkv)
        o, lse = _dil_attn_fwd(qkv, bias[g], dilation, window // dilation, "dil_attn_fwd")
        hns.append(hn), qkvs.append(qkv), outs.append(o), lses.append(lse)
    alphas, o_mix, o_mix_b = _dil_mix(lses, outs, "dil_mix")
    x_out, f = _mm_post(o_mix_b, wo, x, post_g, gate, 1.0, "dil_out")
    return x_out, (x, hns, qkvs, lses, alphas, o_mix, o_mix_b, f)


def _dil_bwd(dx_out, saved, norms, mod, w, bias):
    (pre_g, post_g), (_, scale, gate), (w_in_t, wo) = norms, mod, w
    x, hns, qkvs, lses, alphas, o_mix, o_mix_b, f = saved
    d_model = x.shape[1]
    inner = DIL_HEADS * DIL_HEAD_DIM
    df, dgate, dpost = _post_bwd(dx_out, f, post_g, gate, 1.0, "mix_post_bwd")
    d_o = _mm([(df, wo)], "nt", F32, TOKEN_TILE, inner, "dil_do")
    dwo = _mm([(o_mix_b, df)], "tn", BF16, TOKEN_TILE, d_model, "dil_dwo")
    dhns, dws, dbs = [], [], []
    for g, (window, dilation) in enumerate(DIL_GROUPS):
        grads = _dil_attn_bwd(qkvs[g], bias[g], d_o, o_mix, alphas[g], lses[g], dilation, window // dilation, "dil_attn_bwd")
        dbs.append(grads[3])
        w_parts = [w_in_t[(3 * g + j) * inner:(3 * g + j + 1) * inner] for j in range(3)]
        dhns.append(_mm(list(zip(grads[:3], w_parts)), "nn", F32, TOKEN_TILE, d_model, "dil_dhn", out_perm=dilation))
        dws += [_mm([(grads[j], hns[g])], "tn", BF16, TOKEN_TILE, d_model, "dil_dwin") for j in range(3)]
    dx, dshift, dscale, dpre = _prenorm_bwd(dx_out, dhns, x, pre_g, scale, "prenorm_bwd3")
    return dx, (dpre, dpost), (dshift, dscale, dgate), (jnp.concatenate(dws, axis=0), dwo), jnp.concatenate(dbs, axis=0)


def _pad_rows(a, rows):
    return jnp.pad(a, ((0, rows - a.shape[0]), (0, 0)))


def _lanes(a):
    flat = a.reshape(-1).astype(F32)
    rows = -(-flat.shape[0] // 1024) * 8
    return jnp.pad(flat, (0, rows * 128 - flat.shape[0])).reshape(rows, 128)


def kernel(x, c, norm_pre, norm_post, w_mod, b_mod, ffn_w_gate, ffn_w_up, ffn_w_down, mla_w_in, mla_q_norm, mla_w_q_up, mla_kv_norm, mla_w_kv_up, mla_w_o, dil_w_in, dil_w_o, rel_bias, loss_target, m_norm_pre, m_norm_post, m_w_mod, m_b_mod, m_ffn_w_gate, m_ffn_w_up, m_ffn_w_down, m_mla_w_in, m_mla_q_norm, m_mla_w_q_up, m_mla_kv_norm, m_mla_w_kv_up, m_mla_w_o, m_dil_w_in, m_dil_w_o, m_rel_bias, v_norm_pre, v_norm_post, v_w_mod, v_b_mod, v_ffn_w_gate, v_ffn_w_up, v_ffn_w_down, v_mla_w_in, v_mla_q_norm, v_mla_w_q_up, v_mla_kv_norm, v_mla_w_kv_up, v_mla_w_o, v_dil_w_in, v_dil_w_o, v_rel_bias):
    me = 4 * lax.axis_index("x") + 2 * lax.axis_index("y") + lax.axis_index("c")
    depth, n_sub, d_loc = norm_pre.shape
    d_model = x.shape[2]
    mod_loc_cols = w_mod.shape[2]
    x0, target = x[0], loss_target[0]

    bf_t = lambda a: a.astype(BF16).T
    ffn_ids = [(i, h) for i in range(depth) for h in range(2)]
    shards = []
    for i, h in ffn_ids:
        shards += [bf_t(ffn_w_gate[i, h]), bf_t(ffn_w_up[i, h]), ffn_w_down[i, h].astype(BF16)]
    shards += [mla_w_in[0].astype(BF16), bf_t(mla_w_q_up[0]), bf_t(mla_w_kv_up[0]), mla_w_o[0].astype(BF16),
               bf_t(dil_w_in[0]), dil_w_o[0].astype(BF16)]
    n_ffn = 3 * len(ffn_ids)
    members = {(0, 0): [0, 1, 2], (0, 1): [n_ffn, n_ffn + 1, n_ffn + 2, n_ffn + 3], (0, 2): [3, 4, 5],
               (1, 0): [6, 7, 8], (1, 1): [n_ffn + 4, n_ffn + 5], (1, 2): [9, 10, 11]}
    order = [(i, s) for i in range(depth) for s in range(n_sub)]

    small = jnp.concatenate([c.reshape(8, 128), _pad_rows(norm_pre.reshape(depth * n_sub, d_loc), 8),
                             _pad_rows(norm_post.reshape(depth * n_sub, d_loc), 8)], axis=0)
    small_all = _exchange([small], True, "gather_small")[0].reshape(NDEV, 24, 128)
    c_all = small_all[:, 0:8].reshape(NDEV, d_model)
    gains = lambda lo: jnp.transpose(small_all[:, lo:lo + depth * n_sub], (1, 0, 2)).reshape(depth, n_sub, 1, d_model)
    pre_full, post_full = gains(8), gains(16)

    b_loc = lax.dynamic_slice(b_mod, (0, me * mod_loc_cols), (depth, mod_loc_cols))
    mod_cols, silu_c = _mod_fwd(c_all, w_mod, b_loc, "mod_fwd")
    mod_all = _exchange([mod_cols.reshape(depth * NDEV, mod_loc_cols)], True, "gather_mod")[0]
    mod_all = mod_all.reshape(NDEV, depth, NDEV, mod_loc_cols)
    mod_mine = lax.dynamic_index_in_dim(mod_all, me, axis=2, keepdims=False)
    mod = jnp.transpose(mod_mine, (1, 0, 2)).reshape(depth, n_sub, 3, 1, d_model)

    shards[0], _ = lax.optimization_barrier((shards[0], mod_all))
    first = order[0]
    stages = [("%d%d" % first, members[first][:2]), ("%d%dd" % first, members[first][2:])]
    stages += [("%d%d" % key, members[key]) for key in order[1:]]
    stage_names = [name for name, _ in stages]
    g_sems, g_srcs, g_lands, g_token = _split_start(shards, [idx for _, idx in stages], True, "gather_weights_start")

    forwarded = {}

    def forward(stage, after):
        idx = stages[stage_names.index(stage)][1]
        forwarded[stage] = _gather_forward(g_sems[stage_names.index(stage)], [g_srcs[k] for k in idx],
                                           [g_lands[k] for k in idx], after, "gather_forward_" + stage)
        return forwarded[stage][3]

    def weights_of(stage, after):
        (send_b, recv_b), srcs, lands, _ = forwarded[stage]
        local = g_sems[stage_names.index(stage)][2]
        return _split_wait((send_b, recv_b, local), srcs, lands, after, OTHER_CHIPS, True, "gather_wait_" + stage)

    def late_down(after):
        forward("%d%dd" % first, after)
        return weights_of("%d%dd" % first, after)[0]

    lat_real = Q_LORA + KV_LORA
    qk = QK_NOPE + QK_ROPE

    def mla_weights(after):
        w_in, wq_t, wkv_t, wo = weights_of("01", after)
        w_in_pad = jnp.concatenate([w_in[:, :lat_real], jnp.zeros((d_model, QK_NOPE), BF16), w_in[:, lat_real:],
                                    jnp.zeros((d_model, HEAD_PAD - QK_NOPE - QK_ROPE), BF16)], axis=1)
        wq_pad = jnp.pad(wq_t.reshape(MLA_HEADS, qk, Q_LORA), ((0, 0), (0, HEAD_PAD - qk), (0, 0)))
        wo_pad = jnp.pad(wo.reshape(MLA_HEADS, V_HEAD, d_model), ((0, 0), (HEAD_PAD - V_HEAD, 0), (0, 0)))
        return (w_in_pad, mla_q_norm, wq_pad.reshape(MLA_HEADS * HEAD_PAD, Q_LORA), mla_kv_norm, wkv_t,
                wo_pad.reshape(MLA_HEADS * HEAD_PAD, d_model))

    zero = g_token[0, 0]
    rope = _rope_tables(zero)
    buckets = jnp.stack([_dil_buckets(dil) for _, dil in DIL_GROUPS]) + zero.astype(jnp.int32)
    onehot = (buckets[..., None] == jnp.arange(N_BUCKETS)).astype(F32)
    bias = jnp.einsum("gqkb,bgh->ghqk", onehot, rel_bias.reshape(N_BUCKETS, len(DIL_GROUPS), DIL_HEADS),
                      precision=lax.Precision.HIGHEST)

    norms = lambda i, s: (pre_full[i, s], post_full[i, s])
    mods = lambda i, s: (mod[i, s, 0], mod[i, s, 1], mod[i, s, 2])
    saved, weights = {}, {}
    h = lax.optimization_barrier((x0, bias, buckets, *rope))[0]
    forward("%d%d" % first, h)
    for n, (i, s) in enumerate(order):
        got = mla_weights(h) if (s == 1 and i % 2 == 0) else tuple(weights_of("%d%d" % (i, s), h))
        mid = None if n + 1 == len(order) else (lambda after, nxt="%d%d" % order[n + 1]: forward(nxt, after))
        if s != 1:
            got = got if len(got) == 3 else (*got, late_down)
            h, saved[i, s], weights[i, s] = _ffn_fwd(h, norms(i, s), mods(i, s), got, mid)
            continue
        weights[i, s] = got
        if i % 2 == 0:
            h, saved[i, s] = _mla_fwd(h, norms(i, s), mods(i, s), weights[i, s], rope, mid)
        else:
            h, saved[i, s] = _dil_fwd(h, norms(i, s), mods(i, s), weights[i, s], bias, mid)
    dh, loss_parts = _loss_grad(h, target, "loss")

    dnorm, dmod, sent = {}, {}, {}
    token = jnp.zeros((8, 128), F32)
    last = order[0]

    def send_last(j, dw):
        sent[last, j] = _split_start([dw], [[0]], False, "scatter_start_%d%d_%d" % (*last, j))
        return sent[last, j][3]

    for i, s in reversed(order):
        md = mods(i, s)
        md = (md[0], md[1], md[2] + token[:1, :1])
        if (i, s) == last:
            dh, dnorm[i, s], dmod[i, s], _ = _ffn_bwd(dh, saved[i, s], norms(i, s), md, weights[i, s], send_last)
            continue
        if s != 1:
            dh, dnorm[i, s], dmod[i, s], dws = _ffn_bwd(dh, saved[i, s], norms(i, s), md, weights[i, s])
        elif i % 2 == 0:
            dh, dnorm[i, s], dmod[i, s], dmla = _mla_bwd(dh, saved[i, s], norms(i, s), md, weights[i, s], rope)
            dw_in_pad, dq_norm, dwq_pad, dkv_norm, dwkv_t, dwo_pad = dmla
            dw_in = jnp.concatenate([dw_in_pad[:, :lat_real], dw_in_pad[:, lat_real + QK_NOPE:lat_real + qk]], axis=1)
            dwq_t = dwq_pad.reshape(MLA_HEADS, HEAD_PAD, Q_LORA)[:, :qk].reshape(MLA_HEADS * qk, Q_LORA)
            dwo = dwo_pad.reshape(MLA_HEADS, HEAD_PAD, d_model)[:, HEAD_PAD - V_HEAD:].reshape(MLA_HEADS * V_HEAD, d_model)
            dws = (dw_in, dwq_t, dwkv_t, dwo)
        else:
            dh, dnorm[i, s], dmod[i, s], dws, dbias = _dil_bwd(dh, saved[i, s], norms(i, s), md, weights[i, s], bias)
        sent[i, s] = _split_start(list(dws), [list(range(len(dws)))], False, "scatter_start_%d%d" % (i, s))
        token = sent[i, s][3]
    grad_x = dh[None]

    mine = {}
    for key in order[1:]:
        sems, srcs, lands, _ = sent[key]
        parts = _split_wait(sems[0], srcs, lands, dh, NDEV - 1, False, "scatter_wait_%d%d" % key)
        for k, p in zip(members[key], parts):
            mine[k] = _sum_parts(p, "sum_parts")
    for j in (2, 0, 1):
        sems, srcs, lands, _ = sent[last, j]
        parts = _split_wait(sems[0], srcs, lands, dh, NDEV - 1, False, "scatter_wait_%d%d_%d" % (*last, j))
        mine[members[last][j]] = _sum_parts(parts[0], "sum_parts")
    g_gate = jnp.stack([mine[3 * n].T for n in range(len(ffn_ids))]).reshape(ffn_w_gate.shape)
    g_up = jnp.stack([mine[3 * n + 1].T for n in range(len(ffn_ids))]).reshape(ffn_w_up.shape)
    g_down = jnp.stack([mine[3 * n + 2] for n in range(len(ffn_ids))]).reshape(ffn_w_down.shape)
    g_mla_in, g_q_up, g_kv_up, g_mla_o, g_dil_in, g_dil_o = (mine[k] for k in range(n_ffn, n_ffn + 6))
    g_mla_in, g_q_up, g_kv_up, g_mla_o = g_mla_in[None], g_q_up.T[None], g_kv_up.T[None], g_mla_o[None]
    g_dil_in, g_dil_o = g_dil_in.T[None], g_dil_o[None]

    dmod_mine = jnp.concatenate([jnp.concatenate(dmod[i, s], axis=0) for i in range(depth) for s in range(n_sub)], axis=0)
    dpre_mine = jnp.concatenate([dnorm[i, s][0] for i in range(depth) for s in range(n_sub)], axis=0)
    dpost_mine = jnp.concatenate([dnorm[i, s][1] for i in range(depth) for s in range(n_sub)], axis=0)
    dbias_tab = _bias_reduce(dbias, buckets, "bias_reduce")[:, 0, :N_BUCKETS].T
    pieces = [dmod_mine, dpre_mine, dpost_mine, dq_norm, dkv_norm, dbias_tab, jnp.sum(loss_parts).reshape(1, 1)]
    packed = [_lanes(p) for p in pieces]
    offs = [0]
    for p in packed:
        offs.append(offs[-1] + p.shape[0])
    everyone = _exchange([jnp.concatenate(packed, axis=0)], True, "gather_small_grads")[0].reshape(NDEV, offs[-1], 128)
    total = _sum_parts(everyone, "sum_small")
    take = lambda n, shape: total[offs[n]:offs[n + 1]].reshape(-1)[:math.prod(shape)].reshape(shape)
    g_b_mod = take(0, b_mod.shape)
    col0 = me * d_loc
    g_norm_pre = lax.dynamic_slice(take(1, (depth, n_sub, d_model)), (0, 0, col0), norm_pre.shape)
    g_norm_post = lax.dynamic_slice(take(2, (depth, n_sub, d_model)), (0, 0, col0), norm_post.shape)
    g_q_norm, g_kv_norm = take(3, mla_q_norm.shape), take(4, mla_kv_norm.shape)
    g_rel_bias = take(5, rel_bias.shape)
    loss = take(6, ())

    dmod_all = everyone[:, offs[0]:offs[1]].reshape(NDEV, depth, NDEV * mod_loc_cols)
    dmod_cols = lax.dynamic_slice(dmod_all, (0, 0, me * mod_loc_cols), (NDEV, depth, mod_loc_cols))
    silu_t = jnp.pad(silu_c.T, ((0, 0), (0, HEAD_PAD - NDEV)))
    g_w_mod = jnp.stack([_mm([(silu_t, jnp.pad(dmod_cols[:, i], ((0, HEAD_PAD - NDEV), (0, 0))))], "nn", F32, TOKEN_TILE,
                             mod_loc_cols, "mod_bwd") for i in range(depth)])

    ws = (norm_pre, norm_post, w_mod, b_mod, ffn_w_gate, ffn_w_up, ffn_w_down, mla_w_in, mla_q_norm, mla_w_q_up, mla_kv_norm,
          mla_w_kv_up, mla_w_o, dil_w_in, dil_w_o, rel_bias)
    gs = (g_norm_pre, g_norm_post, g_w_mod, g_b_mod, g_gate, g_up, g_down, g_mla_in, g_q_norm, g_q_up, g_kv_norm, g_kv_up,
          g_mla_o, g_dil_in, g_dil_o, g_rel_bias)
    ms = (m_norm_pre, m_norm_post, m_w_mod, m_b_mod, m_ffn_w_gate, m_ffn_w_up, m_ffn_w_down, m_mla_w_in, m_mla_q_norm,
          m_mla_w_q_up, m_mla_kv_norm, m_mla_w_kv_up, m_mla_w_o, m_dil_w_in, m_dil_w_o, m_rel_bias)
    vs = (v_norm_pre, v_norm_post, v_w_mod, v_b_mod, v_ffn_w_gate, v_ffn_w_up, v_ffn_w_down, v_mla_w_in, v_mla_q_norm,
          v_mla_w_q_up, v_mla_kv_norm, v_mla_w_kv_up, v_mla_w_o, v_dil_w_in, v_dil_w_o, v_rel_bias)
    stepped = [_adamw(w, g, m, v, "adamw") for w, g, m, v in zip(ws, gs, ms, vs)]
    deltas, new_m, new_v = zip(*stepped)
    return (loss, grad_x, *gs, *deltas, *new_m, *new_v)
```

```python
import math

import jax
import jax.numpy as jnp
from jax import lax
from jax.experimental import pallas as pl
from jax.experimental.pallas import tpu as pltpu

F32 = jnp.float32
BF16 = jnp.bfloat16
MESH = pl.DeviceIdType.MESH

NDEV = 8
OTHER_CHIPS = 3
D_MODEL = 1024
SEQ = 2048
D_FF = 2816
EPS = 1e-6
FFN_RES = 0.5

MLA_HEADS = 16
Q_LORA = 384
KV_LORA = 256
QK_NOPE = 64
QK_ROPE = 32
V_HEAD = 64
ROPE_THETA = 10000.0
HEAD_PAD = 128
LAT_PAD = Q_LORA + KV_LORA + HEAD_PAD
MLA_SCALE = (QK_NOPE + QK_ROPE) ** -0.5

DIL_GROUPS = ((128, 1), (512, 4), (2048, 16))
DIL_HEADS = 16
DIL_HEAD_DIM = 64
DIL_BLOCK = 128
DIL_PAIRS = DIL_HEADS // 2
DIL_SCALE = DIL_HEAD_DIM ** -0.5
DIL_GROUPED = 4
N_BUCKETS = 32
MAX_DISTANCE = 2048

ADAM_LR = 0.001
ADAM_B1 = 0.9
ADAM_B2 = 0.999
ADAM_EPS = 1e-08
ADAM_WD = 0.01
ADAM_STEP = 10

V7X_VMEM_BYTES = 64 * 2**20
VMEM_RESERVE = 10 * 2**20
TOKEN_TILE = 512


def _nbytes(shape, dtype):
    return math.prod(shape) * jnp.dtype(dtype).itemsize


def _params(semantics, blocks, extra=0):
    need = 2 * sum(_nbytes(s, d) for s, d in blocks) + extra + VMEM_RESERVE
    return pltpu.CompilerParams(dimension_semantics=semantics,
                                vmem_limit_bytes=int(min(need, V7X_VMEM_BYTES - VMEM_RESERVE)))


def _pcall(body, out_shape, **kw):
    call = pl.pallas_call(body, out_shape=jax.tree.map(lambda s: pltpu.HBM(s.shape, s.dtype), out_shape), **kw)
    return lambda *args: call(*[pltpu.with_memory_space_constraint(a, pltpu.HBM) for a in args])


def _dot_nn(a, b):
    return lax.dot_general(a, b, (((1,), (0,)), ((), ())), preferred_element_type=F32)


def _dot_nt(a, b):
    return lax.dot_general(a, b, (((1,), (1,)), ((), ())), preferred_element_type=F32)


def _dot_tn(a, b):
    return lax.dot_general(a, b, (((0,), (0,)), ((), ())), preferred_element_type=F32)


_DOTS = {"nn": _dot_nn, "nt": _dot_nt, "tn": _dot_tn}


def _rstd(v):
    return lax.rsqrt(jnp.mean(v * v, axis=-1, keepdims=True) + EPS)


def _rms_bwd(v, r, t):
    return r * t - v * (r * r * r) * jnp.mean(t * v, axis=-1, keepdims=True)


_TOKEN_SPEC = pl.BlockSpec((8, 128), lambda *_: (0, 0))


def _mm(pairs, mode, out_dtype, tm, tn, name, out_perm=1, after=None):
    a0, b0 = pairs[0]
    m_dim = a0.shape[1] if mode == "tn" else a0.shape[0]
    n_dim = b0.shape[0] if mode == "nt" else b0.shape[1]
    tm, tn = min(tm, m_dim // out_perm), min(tn, n_dim)
    assert m_dim % tm == 0 and n_dim % tn == 0, (name, m_dim, n_dim, tm, tn)
    dot = _DOTS[mode]
    npairs = len(pairs)

    def body(*refs):
        acc = None
        for p in range(npairs):
            d = dot(refs[2 * p][...].astype(BF16), refs[2 * p + 1][...].astype(BF16))
            acc = d if acc is None else acc + d
        refs[-1][...] = acc.astype(out_dtype)

    in_specs, blocks, flat = [], [], []
    for a, b in pairs:
        if mode == "nn":
            k = a.shape[1]
            sa, sb = ((tm, k), lambda i, j: (i, 0)), ((k, tn), lambda i, j: (0, j))
        elif mode == "nt":
            k = a.shape[1]
            sa, sb = ((tm, k), lambda i, j: (i, 0)), ((tn, k), lambda i, j: (j, 0))
        else:
            k = a.shape[0]
            sa, sb = ((k, tm), lambda i, j: (0, i)), ((k, tn), lambda i, j: (0, j))
        in_specs += [pl.BlockSpec(*sa), pl.BlockSpec(*sb)]
        blocks += [(sa[0], a.dtype), (sb[0], b.dtype)]
        flat += [a, b]
    if after is not None:
        in_specs.append(_TOKEN_SPEC)
        flat.append(after)
    if out_perm == 1:
        out_shape = (m_dim, n_dim)
        out_spec = pl.BlockSpec((tm, tn), lambda i, j: (i, j))
    else:
        rows = m_dim // out_perm
        assert tn == n_dim and rows % tm == 0, (name, rows, tm)
        nb = rows // tm
        out_shape = (rows, out_perm * n_dim)
        out_spec = pl.BlockSpec((tm, n_dim), lambda i, j: (i % nb, i // nb))
    blocks.append(((tm, tn), out_dtype))
    res = _pcall(
        body, out_shape=jax.ShapeDtypeStruct(out_shape, out_dtype), grid=(m_dim // tm, n_dim // tn),
        in_specs=in_specs, out_specs=out_spec, name=name,
        compiler_params=_params(("parallel", "parallel"), blocks, extra=2 * tm * tn * 4),
    )(*flat)
    return res.reshape(m_dim, n_dim)


def _prenorm_mm(x, pre_g, scale, shift, w, w_mode, out_dtype, tn, name, perm=1):
    s_dim, d_dim = x.shape
    n_dim = w.shape[0] if w_mode == "nt" else w.shape[1]
    rows = s_dim // perm
    tm = min(TOKEN_TILE, rows)
    nb = rows // tm
    tn = min(tn, n_dim)
    assert n_dim % tn == 0
    dot = _DOTS[w_mode]

    def body(x_ref, g_ref, sc_ref, sh_ref, w_ref, hn_ref, o_ref):
        @pl.when(pl.program_id(1) == 0)
        def _():
            xf = x_ref[...]
            hn = (xf * _rstd(xf) * g_ref[...]) * (1.0 + sc_ref[...]) + sh_ref[...]
            hn_ref[...] = hn.astype(BF16)

        o_ref[...] = dot(hn_ref[...], w_ref[...]).astype(out_dtype)

    vec = pl.BlockSpec((1, d_dim), lambda i, j: (0, 0))
    w_block = (tn, d_dim) if w_mode == "nt" else (d_dim, tn)
    w_spec = pl.BlockSpec(w_block, (lambda i, j: (j, 0)) if w_mode == "nt" else (lambda i, j: (0, j)))
    hn, out = _pcall(
        body,
        out_shape=(jax.ShapeDtypeStruct((s_dim, d_dim), BF16), jax.ShapeDtypeStruct((s_dim, n_dim), out_dtype)),
        grid=(s_dim // tm, n_dim // tn),
        in_specs=[pl.BlockSpec((tm, d_dim), lambda i, j: (i % nb, i // nb)), vec, vec, vec, w_spec],
        out_specs=(pl.BlockSpec((tm, d_dim), lambda i, j: (i, 0)), pl.BlockSpec((tm, tn), lambda i, j: (i, j))),
        name=name,
        compiler_params=_params(("parallel", "arbitrary"),
                                [((tm, d_dim), F32), (w_block, BF16), ((tm, d_dim), BF16), ((tm, tn), out_dtype)],
                                extra=3 * tm * d_dim * 4 + tm * tn * 4),
    )(x.reshape(rows, perm * d_dim), pre_g, scale, shift, w)
    return hn, out


def _ffn_up(x, pre_g, scale, shift, wg_t, wu_t, name):
    s_dim, d_dim = x.shape
    f_dim = wg_t.shape[0]
    tm, tn = TOKEN_TILE, f_dim // 2

    def body(x_ref, g_ref, sc_ref, sh_ref, wg_ref, wu_ref, hn_ref, go_ref, uo_ref, a_ref):
        @pl.when(pl.program_id(1) == 0)
        def _():
            xf = x_ref[...]
            hn = (xf * _rstd(xf) * g_ref[...]) * (1.0 + sc_ref[...]) + sh_ref[...]
            hn_ref[...] = hn.astype(BF16)

        hn = hn_ref[...]
        g = _dot_nt(hn, wg_ref[...])
        u = _dot_nt(hn, wu_ref[...])
        go_ref[...] = g.astype(BF16)
        uo_ref[...] = u.astype(BF16)
        a_ref[...] = (g * jax.nn.sigmoid(g) * u).astype(BF16)

    vec = pl.BlockSpec((1, d_dim), lambda i, j: (0, 0))
    w_spec = pl.BlockSpec((tn, d_dim), lambda i, j: (j, 0))
    act = pl.BlockSpec((tm, tn), lambda i, j: (i, j))
    act_shape = jax.ShapeDtypeStruct((s_dim, f_dim), BF16)
    return _pcall(
        body,
        out_shape=(jax.ShapeDtypeStruct((s_dim, d_dim), BF16), act_shape, act_shape, act_shape),
        grid=(s_dim // tm, f_dim // tn),
        in_specs=[pl.BlockSpec((tm, d_dim), lambda i, j: (i, 0)), vec, vec, vec, w_spec, w_spec],
        out_specs=(pl.BlockSpec((tm, d_dim), lambda i, j: (i, 0)), act, act, act),
        name=name,
        compiler_params=_params(("parallel", "arbitrary"),
                                [((tm, d_dim), F32), ((tn, d_dim), BF16), ((tn, d_dim), BF16), ((tm, d_dim), BF16)]
                                + 3 * [((tm, tn), BF16)], extra=3 * tm * d_dim * 4 + 4 * tm * tn * 4),
    )(x, pre_g, scale, shift, wg_t, wu_t)


def _mm_post(a, w, x, post_g, gate, res_w, name):
    s_dim, k_dim = a.shape
    d_dim = w.shape[1]
    tm = TOKEN_TILE

    def body(a_ref, w_ref, x_ref, pg_ref, gt_ref, xo_ref, f_ref):
        f = _dot_nn(a_ref[...], w_ref[...])
        y = f * _rstd(f) * pg_ref[...]
        f_ref[...] = f
        xo_ref[...] = x_ref[...] + (res_w * gt_ref[...]) * y

    vec = pl.BlockSpec((1, d_dim), lambda i: (0, 0))
    row = pl.BlockSpec((tm, d_dim), lambda i: (i, 0))
    out = jax.ShapeDtypeStruct((s_dim, d_dim), F32)
    return _pcall(
        body, out_shape=(out, out), grid=(s_dim // tm,),
        in_specs=[pl.BlockSpec((tm, k_dim), lambda i: (i, 0)), pl.BlockSpec((k_dim, d_dim), lambda i: (0, 0)), row, vec, vec],
        out_specs=(row, row), name=name,
        compiler_params=_params(("parallel",), [((tm, k_dim), BF16), ((k_dim, d_dim), BF16)] + 3 * [((tm, d_dim), F32)],
                                extra=3 * tm * d_dim * 4),
    )(a, w, x, post_g, gate)


def _post_bwd(dx_out, f, post_g, gate, res_w, name):
    s_dim, d_dim = f.shape
    tm = TOKEN_TILE

    def body(dx_ref, f_ref, pg_ref, gt_ref, df_ref, dgate_ref, dpost_ref):
        @pl.when(pl.program_id(0) == 0)
        def _():
            dgate_ref[...] = jnp.zeros_like(dgate_ref)
            dpost_ref[...] = jnp.zeros_like(dpost_ref)

        dx, fv = dx_ref[...], f_ref[...]
        r = _rstd(fv)
        fr = fv * r
        dgate_ref[...] += res_w * jnp.sum(dx * (fr * pg_ref[...]), axis=0, keepdims=True)
        dy = (res_w * gt_ref[...]) * dx
        dpost_ref[...] += jnp.sum(dy * fr, axis=0, keepdims=True)
        df_ref[...] = _rms_bwd(fv, r, dy * pg_ref[...]).astype(BF16)

    vec = pl.BlockSpec((1, d_dim), lambda i: (0, 0))
    row = pl.BlockSpec((tm, d_dim), lambda i: (i, 0))
    vshape = jax.ShapeDtypeStruct((1, d_dim), F32)
    return _pcall(
        body, out_shape=(jax.ShapeDtypeStruct((s_dim, d_dim), BF16), vshape, vshape), grid=(s_dim // tm,),
        in_specs=[row, row, vec, vec], out_specs=(row, vec, vec), name=name,
        compiler_params=_params(("arbitrary",), 3 * [((tm, d_dim), F32)], extra=6 * tm * d_dim * 4),
    )(dx_out, f, post_g, gate)


def _prenorm_bwd(dx_out, dhns, x, pre_g, scale, name):
    s_dim, d_dim = x.shape
    tm = TOKEN_TILE
    n_in = len(dhns)

    def body(*refs):
        dx_ref, x_ref, pg_ref, sc_ref = refs[n_in + 0], refs[n_in + 1], refs[n_in + 2], refs[n_in + 3]
        dxo_ref, dsh_ref, dsc_ref, dpg_ref = refs[n_in + 4:]

        @pl.when(pl.program_id(0) == 0)
        def _():
            dsh_ref[...] = jnp.zeros_like(dsh_ref)
            dsc_ref[...] = jnp.zeros_like(dsc_ref)
            dpg_ref[...] = jnp.zeros_like(dpg_ref)

        dhn = refs[0][...]
        for k in range(1, n_in):
            dhn = dhn + refs[k][...]
        xv = x_ref[...]
        r = _rstd(xv)
        xr = xv * r
        dsh_ref[...] += jnp.sum(dhn, axis=0, keepdims=True)
        dsc_ref[...] += jnp.sum(dhn * (xr * pg_ref[...]), axis=0, keepdims=True)
        dn = dhn * (1.0 + sc_ref[...])
        dpg_ref[...] += jnp.sum(dn * xr, axis=0, keepdims=True)
        dxo_ref[...] = dx_ref[...] + _rms_bwd(xv, r, dn * pg_ref[...])

    vec = pl.BlockSpec((1, d_dim), lambda i: (0, 0))
    row = pl.BlockSpec((tm, d_dim), lambda i: (i, 0))
    vshape = jax.ShapeDtypeStruct((1, d_dim), F32)
    return _pcall(
        body, out_shape=(jax.ShapeDtypeStruct((s_dim, d_dim), F32), vshape, vshape, vshape), grid=(s_dim // tm,),
        in_specs=n_in * [row] + [row, row, vec, vec], out_specs=(row, vec, vec, vec), name=name,
        compiler_params=_params(("arbitrary",), (n_in + 3) * [((tm, d_dim), F32)], extra=6 * tm * d_dim * 4),
    )(*dhns, dx_out, x, pre_g, scale)


def _ffn_dgu(df, wd, g, u, name, after=None):
    s_dim, d_dim = df.shape
    f_dim = wd.shape[0]
    tm, tn = TOKEN_TILE, f_dim // 2

    def body(df_ref, wd_ref, g_ref, u_ref, *rest):
        dg_ref, du_ref = rest[-2:]
        da = _dot_nt(df_ref[...], wd_ref[...])
        gv, uv = g_ref[...].astype(F32), u_ref[...].astype(F32)
        sg = jax.nn.sigmoid(gv)
        du_ref[...] = (da * (gv * sg)).astype(BF16)
        dg_ref[...] = (da * uv * (sg * (1.0 + gv * (1.0 - sg)))).astype(BF16)

    act = pl.BlockSpec((tm, tn), lambda i, j: (i, j))
    act_shape = jax.ShapeDtypeStruct((s_dim, f_dim), BF16)
    token = [] if after is None else [after]
    return _pcall(
        body, out_shape=(act_shape, act_shape), grid=(s_dim // tm, f_dim // tn),
        in_specs=[pl.BlockSpec((tm, d_dim), lambda i, j: (i, 0)), pl.BlockSpec((tn, d_dim), lambda i, j: (j, 0)), act, act]
        + len(token) * [_TOKEN_SPEC],
        out_specs=(act, act), name=name,
        compiler_params=_params(("parallel", "parallel"), [((tm, d_dim), BF16), ((tn, d_dim), BF16)] + 4 * [((tm, tn), BF16)],
                                extra=6 * tm * tn * 4),
    )(df, wd, g, u, *token)


def _ffn_bwd_fused(dx_out, saved, pre_g, post_g, scale, gate, wg_t, wu_t, wd, name):
    x, _, g, u, _, f = saved
    s_dim, d_dim = x.shape
    f_dim = wd.shape[0]
    tm, chunks = 256, 2
    cw = f_dim // chunks

    def body(dx_ref, f_ref, g_ref, u_ref, x_ref, pg_ref, gt_ref, prg_ref, sc_ref, wd_ref, wg_ref, wu_ref,
             df_ref, dg_ref, du_ref, dxo_ref, dgate_ref, dpost_ref, dsh_ref, dsc_ref, dpg_ref):
        @pl.when(pl.program_id(0) == 0)
        def _():
            for acc in (dgate_ref, dpost_ref, dsh_ref, dsc_ref, dpg_ref):
                acc[...] = jnp.zeros_like(acc)

        dx, fv = dx_ref[...], f_ref[...]
        r = _rstd(fv)
        fr = fv * r
        dgate_ref[...] += FFN_RES * jnp.sum(dx * (fr * pg_ref[...]), axis=0, keepdims=True)
        dy = (FFN_RES * gt_ref[...]) * dx
        dpost_ref[...] += jnp.sum(dy * fr, axis=0, keepdims=True)
        df = _rms_bwd(fv, r, dy * pg_ref[...]).astype(BF16)
        df_ref[...] = df
        dhn = None
        ahead = _dot_nt(df, wd_ref[0:cw, :])
        for c in range(chunks):
            da = ahead
            if c + 1 < chunks:
                ahead = _dot_nt(df, wd_ref[(c + 1) * cw:(c + 2) * cw, :])
            cols = slice(c * cw, (c + 1) * cw)
            gv, uv = g_ref[:, cols].astype(F32), u_ref[:, cols].astype(F32)
            sg = jax.nn.sigmoid(gv)
            du = (da * (gv * sg)).astype(BF16)
            dg = (da * uv * (sg * (1.0 + gv * (1.0 - sg)))).astype(BF16)
            dg_ref[:, cols] = dg
            du_ref[:, cols] = du
            part = _dot_nn(dg, wg_ref[cols, :]) + _dot_nn(du, wu_ref[cols, :])
            dhn = part if dhn is None else dhn + part
        xv = x_ref[...]
        rx = _rstd(xv)
        xr = xv * rx
        dsh_ref[...] += jnp.sum(dhn, axis=0, keepdims=True)
        dsc_ref[...] += jnp.sum(dhn * (xr * prg_ref[...]), axis=0, keepdims=True)
        dn = dhn * (1.0 + sc_ref[...])
        dpg_ref[...] += jnp.sum(dn * xr, axis=0, keepdims=True)
        dxo_ref[...] = dx + _rms_bwd(xv, rx, dn * prg_ref[...])

    vec = pl.BlockSpec((1, d_dim), lambda i: (0, 0))
    row = pl.BlockSpec((tm, d_dim), lambda i: (i, 0))
    act = pl.BlockSpec((tm, f_dim), lambda i: (i, 0))
    weight = pl.BlockSpec((f_dim, d_dim), lambda i: (0, 0), pipeline_mode=pl.Buffered(1))
    vshape = jax.ShapeDtypeStruct((1, d_dim), F32)
    act_shape = jax.ShapeDtypeStruct((s_dim, f_dim), BF16)
    need = (3 * f_dim * d_dim * 2 + 2 * (3 * tm * d_dim * 4 + 2 * tm * f_dim * 2) + 2 * (tm * d_dim * 2 + 2 * tm * f_dim * 2 + tm * d_dim * 4)
            + 6 * tm * cw * 4 + 6 * tm * d_dim * 4)
    return _pcall(
        body, out_shape=(jax.ShapeDtypeStruct((s_dim, d_dim), BF16), act_shape, act_shape, jax.ShapeDtypeStruct((s_dim, d_dim), F32),
                         vshape, vshape, vshape, vshape, vshape),
        grid=(s_dim // tm,), in_specs=[row, row, act, act, row, vec, vec, vec, vec, weight, weight, weight],
        out_specs=(row, act, act, row, vec, vec, vec, vec, vec), name=name,
        compiler_params=pltpu.CompilerParams(dimension_semantics=("arbitrary",),
                                             vmem_limit_bytes=int(min(need + VMEM_RESERVE, V7X_VMEM_BYTES - VMEM_RESERVE))),
    )(dx_out, f, g, u, x, post_g, gate, pre_g, scale, wd, wg_t, wu_t)


def _rope_tables(zero=0.0):
    half = QK_ROPE // 2
    freqs = ROPE_THETA ** (-jnp.arange(half, dtype=F32) / half)
    ang = (jnp.arange(SEQ, dtype=F32)[:, None] + zero) * freqs[None, :]
    cos, sin = jnp.cos(ang), jnp.sin(ang)
    ones = jnp.ones((SEQ, QK_NOPE), F32)
    zeros = jnp.zeros((SEQ, QK_NOPE), F32)
    pad1 = jnp.ones((SEQ, HEAD_PAD - QK_NOPE - QK_ROPE), F32)
    pad0 = jnp.zeros((SEQ, HEAD_PAD - QK_NOPE - QK_ROPE), F32)
    zh = jnp.zeros((SEQ, half), F32)
    c = jnp.concatenate([ones, cos, cos, pad1], axis=1)
    s1 = jnp.concatenate([zeros, -sin, zh, pad0], axis=1)
    s2 = jnp.concatenate([zeros, zh, sin, pad0], axis=1)
    return c, s1, s2


def _rope(v, c, s1, s2):
    half = QK_ROPE // 2
    return v * c + pltpu.roll(v, HEAD_PAD - half, 1) * s1 + pltpu.roll(v, half, 1) * s2


def _rope_t(dv, c, s1, s2):
    half = QK_ROPE // 2
    return dv * c + pltpu.roll(dv * s1, half, 1) + pltpu.roll(dv * s2, HEAD_PAD - half, 1)


def _mla_qkv(lat, q_norm, kv_norm, wq_t, wkv_t, rope, name):
    s_dim = lat.shape[0]
    width = MLA_HEADS * HEAD_PAD
    tm = 256

    def body(lat_ref, qg_ref, kg_ref, wq_ref, wkv_ref, c_ref, s1_ref, s2_ref, q_ref, k_ref, v_ref, qn_ref, kvn_ref):
        cq = lat_ref[:, :Q_LORA]
        ckv = lat_ref[:, Q_LORA:Q_LORA + KV_LORA]
        kr = lat_ref[:, Q_LORA + KV_LORA:]
        c, s1, s2 = c_ref[...], s1_ref[...], s2_ref[...]
        qn = (cq * _rstd(cq) * qg_ref[...]).astype(BF16)
        kvn = (ckv * _rstd(ckv) * kg_ref[...]).astype(BF16)
        qn_ref[...] = qn
        kvn_ref[...] = kvn
        q = _dot_nt(qn, wq_ref[...])
        kv = _dot_nt(kvn, wkv_ref[...])
        krr = _rope(kr, c, s1, s2)
        low = lax.broadcasted_iota(jnp.int32, (tm, HEAD_PAD), 1) < QK_NOPE
        for h in range(MLA_HEADS):
            sl = slice(h * HEAD_PAD, (h + 1) * HEAD_PAD)
            q_ref[:, sl] = _rope(q[:, sl], c, s1, s2).astype(BF16)
            kvh = kv[:, sl]
            k_ref[:, sl] = (jnp.where(low, kvh, 0.0) + krr).astype(BF16)
            v_ref[:, sl] = jnp.where(low, 0.0, kvh).astype(BF16)

    row = lambda n: pl.BlockSpec((tm, n), lambda i: (i, 0))
    full = lambda a: pl.BlockSpec(a.shape, lambda i: (0, 0))
    wide = jax.ShapeDtypeStruct((s_dim, width), BF16)
    return _pcall(
        body,
        out_shape=(wide, wide, wide, jax.ShapeDtypeStruct((s_dim, Q_LORA), BF16), jax.ShapeDtypeStruct((s_dim, KV_LORA), BF16)),
        grid=(s_dim // tm,),
        in_specs=[row(LAT_PAD), full(q_norm), full(kv_norm), full(wq_t), full(wkv_t), row(HEAD_PAD), row(HEAD_PAD), row(HEAD_PAD)],
        out_specs=(row(width), row(width), row(width), row(Q_LORA), row(KV_LORA)), name=name,
        compiler_params=_params(("parallel",), [((tm, LAT_PAD), F32), (wq_t.shape, BF16), (wkv_t.shape, BF16)]
                                + 3 * [((tm, width), BF16)], extra=4 * tm * width * 4),
    )(lat, q_norm, kv_norm, wq_t, wkv_t, *rope)


def _mla_scores(q, k_ref, t, tq):
    lo = t * tq
    own = slice(lo, lo + tq)
    scores = [(_dot_nt(q, k_ref[own, :]), own)]
    if t > 0:
        scores.append((_dot_nt(q, k_ref[0:lo, :]), slice(0, lo)))
    return scores


def _mla_softmax(scores):
    s_own = scores[0][0] * MLA_SCALE
    rows = lax.broadcasted_iota(jnp.int32, s_own.shape, 0)
    cols = lax.broadcasted_iota(jnp.int32, s_own.shape, 1)
    s_own = jnp.where(cols <= rows, s_own, -jnp.inf)
    mx = jnp.max(s_own, axis=-1, keepdims=True)
    if len(scores) == 1:
        e_own = jnp.exp(s_own - mx)
        return [(e_own * (1.0 / jnp.sum(e_own, axis=-1, keepdims=True)), scores[0][1])]
    s_pre = scores[1][0] * MLA_SCALE
    mx = jnp.maximum(mx, jnp.max(s_pre, axis=-1, keepdims=True))
    e_own, e_pre = jnp.exp(s_own - mx), jnp.exp(s_pre - mx)
    inv = 1.0 / (jnp.sum(e_own, axis=-1, keepdims=True) + jnp.sum(e_pre, axis=-1, keepdims=True))
    return [(e_pre * inv, scores[1][1]), (e_own * inv, scores[0][1])]


def _mla_attn_fwd(q, k, v, name):
    s_dim = q.shape[0]
    tq = 512

    def body(q_ref, k_ref, v_ref, o_ref):
        n_tiles = s_dim // tq
        tile_of = lambda t: slice(t * tq, (t + 1) * tq)
        def weighted_values(t, probs):
            o = None
            for p, keys in probs:
                part = _dot_nn(p, v_ref[keys, :])
                o = part if o is None else o + part
            o_ref[tile_of(t), :] = o.astype(BF16)

        scores = _mla_scores(q_ref[tile_of(0), :], k_ref, 0, tq)
        probs = None
        for t in range(n_tiles):
            ahead = _mla_scores(q_ref[tile_of(t + 1), :], k_ref, t + 1, tq) if t + 1 < n_tiles else None
            if probs is not None:
                weighted_values(t - 1, probs)
            probs = [(p.astype(BF16), keys) for p, keys in _mla_softmax(scores)]
            scores = ahead
        weighted_values(n_tiles - 1, probs)

    head = pl.BlockSpec((s_dim, HEAD_PAD), lambda h: (0, h))
    return _pcall(
        body, out_shape=jax.ShapeDtypeStruct(q.shape, BF16), grid=(MLA_HEADS,),
        in_specs=[head, head, head], out_specs=head, name=name,
        compiler_params=_params(("parallel",), 4 * [((s_dim, HEAD_PAD), BF16)], extra=4 * tq * s_dim * 4),
    )(q, k, v)


def _mla_attn_bwd(q, k, v, d_o, name):
    s_dim = q.shape[0]
    tq = 512

    def body(q_ref, k_ref, v_ref, do_ref, dq_ref, dk_ref, dv_ref):
        dk_ref[...] = jnp.zeros_like(dk_ref)
        dv_ref[...] = jnp.zeros_like(dv_ref)
        n_tiles = s_dim // tq
        tile_of = lambda t: slice(t * tq, (t + 1) * tq)

        def products(t):
            scores = _mla_scores(q_ref[tile_of(t), :], k_ref, t, tq)
            dot = do_ref[tile_of(t), :].astype(BF16)
            return scores, [_dot_nt(dot, v_ref[keys, :]) for _, keys in scores]

        def gradients_of_scores(scores, dps):
            probs = _mla_softmax(scores)
            dp_of = {(keys.start, keys.stop): dp for (_, keys), dp in zip(scores, dps)}
            terms = [(p, keys, dp_of[keys.start, keys.stop]) for p, keys in probs]
            row = None
            for p, _, dp in terms:
                part = jnp.sum(p * dp, axis=-1, keepdims=True)
                row = part if row is None else row + part
            return [((p * (dp - row) * MLA_SCALE).astype(BF16), p.astype(BF16), keys) for p, keys, dp in terms]

        def accumulate(t, terms):
            qt = q_ref[tile_of(t), :]
            dot = do_ref[tile_of(t), :].astype(BF16)
            dq = None
            for dsb, pb, keys in terms:
                part = _dot_nn(dsb, k_ref[keys, :])
                dq = part if dq is None else dq + part
                dk_ref[keys, :] += _dot_tn(dsb, qt)
                dv_ref[keys, :] += _dot_tn(pb, dot)
            dq_ref[tile_of(t), :] = dq

        ready = products(0)
        terms = None
        for t in range(n_tiles):
            ahead = products(t + 1) if t + 1 < n_tiles else None
            if terms is not None:
                accumulate(t - 1, terms)
            terms = gradients_of_scores(*ready)
            ready = ahead
        accumulate(n_tiles - 1, terms)

    head = pl.BlockSpec((s_dim, HEAD_PAD), lambda h: (0, h))
    out = jax.ShapeDtypeStruct(q.shape, F32)
    return _pcall(
        body, out_shape=(out, out, out), grid=(MLA_HEADS,),
        in_specs=[head, head, head, head], out_specs=(head, head, head), name=name,
        compiler_params=_params(("parallel",), 3 * [((s_dim, HEAD_PAD), BF16)] + 4 * [((s_dim, HEAD_PAD), F32)],
                                extra=6 * tq * s_dim * 4),
    )(q, k, v, d_o)


def _mla_qkv_bwd(dq, dk, dv, lat, q_norm, kv_norm, wq_t, wkv_t, rope, name):
    s_dim = lat.shape[0]
    width = MLA_HEADS * HEAD_PAD
    tm = 256

    def body(dq_ref, dk_ref, dv_ref, lat_ref, qg_ref, kg_ref, wq_ref, wkv_ref, c_ref, s1_ref, s2_ref,
             dqp_ref, dkv_ref, dlat_ref, dqg_ref, dkg_ref):
        @pl.when(pl.program_id(0) == 0)
        def _():
            dqg_ref[...] = jnp.zeros_like(dqg_ref)
            dkg_ref[...] = jnp.zeros_like(dkg_ref)

        c, s1, s2 = c_ref[...], s1_ref[...], s2_ref[...]
        lane = lax.broadcasted_iota(jnp.int32, (tm, HEAD_PAD), 1)
        low = lane < QK_NOPE
        rot = (lane >= QK_NOPE) & (lane < QK_NOPE + QK_ROPE)
        dkrr = jnp.zeros((tm, HEAD_PAD), F32)
        for h in range(MLA_HEADS):
            sl = slice(h * HEAD_PAD, (h + 1) * HEAD_PAD)
            dqp_ref[:, sl] = _rope_t(dq_ref[:, sl], c, s1, s2).astype(BF16)
            dkh = dk_ref[:, sl]
            dkv_ref[:, sl] = jnp.where(low, dkh, dv_ref[:, sl]).astype(BF16)
            dkrr = dkrr + jnp.where(rot, dkh, 0.0)
        dqn = _dot_nn(dqp_ref[...], wq_ref[...])
        dkvn = _dot_nn(dkv_ref[...], wkv_ref[...])
        cq = lat_ref[:, :Q_LORA]
        ckv = lat_ref[:, Q_LORA:Q_LORA + KV_LORA]
        rq, rkv = _rstd(cq), _rstd(ckv)
        dqg_ref[...] += jnp.sum(dqn * cq * rq, axis=0, keepdims=True)
        dkg_ref[...] += jnp.sum(dkvn * ckv * rkv, axis=0, keepdims=True)
        dlat_ref[:, :Q_LORA] = _rms_bwd(cq, rq, dqn * qg_ref[...])
        dlat_ref[:, Q_LORA:Q_LORA + KV_LORA] = _rms_bwd(ckv, rkv, dkvn * kg_ref[...])
        dlat_ref[:, Q_LORA + KV_LORA:] = _rope_t(dkrr, c, s1, s2)

    row = lambda n: pl.BlockSpec((tm, n), lambda i: (i, 0))
    full = lambda a: pl.BlockSpec(a.shape, lambda i: (0, 0))
    wide = jax.ShapeDtypeStruct((s_dim, width), BF16)
    return _pcall(
        body,
        out_shape=(wide, wide, jax.ShapeDtypeStruct((s_dim, LAT_PAD), F32),
                   jax.ShapeDtypeStruct(q_norm.shape, F32), jax.ShapeDtypeStruct(kv_norm.shape, F32)),
        grid=(s_dim // tm,),
        in_specs=[row(width), row(width), row(width), row(LAT_PAD), full(q_norm), full(kv_norm), full(wq_t), full(wkv_t),
                  row(HEAD_PAD), row(HEAD_PAD), row(HEAD_PAD)],
        out_specs=(row(width), row(width), row(LAT_PAD), full(q_norm), full(kv_norm)), name=name,
        compiler_params=_params(("arbitrary",), 3 * [((tm, width), F32)] + [((tm, LAT_PAD), F32), (wq_t.shape, BF16),
                                                                           (wkv_t.shape, BF16)] + 2 * [((tm, width), BF16)],
                                extra=2 * tm * width * 4),
    )(dq, dk, dv, lat, q_norm, kv_norm, wq_t, wkv_t, *rope)


def _t5_bucket(dist):
    max_exact = N_BUCKETS // 2
    d = jnp.maximum(dist, 1).astype(F32)
    large = max_exact + (jnp.log(d / max_exact) / math.log(MAX_DISTANCE / max_exact)
                         * (N_BUCKETS - max_exact)).astype(jnp.int32)
    large = jnp.minimum(large, N_BUCKETS - 1)
    return jnp.where(dist < max_exact, dist, large)


def _dil_buckets(dilation):
    iq = jnp.arange(DIL_BLOCK)[:, None]
    ik = jnp.arange(2 * DIL_BLOCK)[None, :]
    return _t5_bucket(jnp.maximum(DIL_BLOCK + iq - ik, 0) * dilation)


def _dil_logits(qh, kb, bias_h, first, span):
    if first:
        s = _dot_nt(qh, kb) * DIL_SCALE + bias_h[:, DIL_BLOCK:]
        rel = lax.broadcasted_iota(jnp.int32, s.shape, 0) - lax.broadcasted_iota(jnp.int32, s.shape, 1)
    else:
        s = _dot_nt(qh, kb) * DIL_SCALE + bias_h
        rel = DIL_BLOCK + lax.broadcasted_iota(jnp.int32, s.shape, 0) - lax.broadcasted_iota(jnp.int32, s.shape, 1)
    return jnp.where((rel >= 0) & (rel <= span), s, -jnp.inf)


def _dil_blocks(s_dim, dilation):
    rows = s_dim // dilation
    for r in range(dilation):
        for n in range(rows // DIL_BLOCK):
            lo = r * rows + n * DIL_BLOCK
            keys = slice(lo, lo + DIL_BLOCK) if n == 0 else slice(lo - DIL_BLOCK, lo + DIL_BLOCK)
            start = r + n * DIL_BLOCK * dilation
            tokens = slice(start, start + DIL_BLOCK) if dilation == 1 else pl.ds(start, DIL_BLOCK, stride=dilation)
            yield n == 0, slice(lo, lo + DIL_BLOCK), keys, tokens


def _dil_views(s_dim):
    col = lambda which: pl.BlockSpec((s_dim, HEAD_PAD), lambda p: (0, which * DIL_PAIRS + p))
    nat = pl.BlockSpec((s_dim, HEAD_PAD), lambda p: (0, p))
    bias = pl.BlockSpec((2, DIL_BLOCK, 2 * DIL_BLOCK), lambda p: (p, 0, 0))
    return col, nat, bias


def _dil_attn_fwd(qkv, bias, dilation, span, name):
    s_dim = qkv.shape[0]
    d_dim = DIL_HEADS * DIL_HEAD_DIM
    col, nat, bias_spec = _dil_views(s_dim)

    def body(q_ref, k_ref, v_ref, b_ref, o_ref, l_ref):
        lane = lax.broadcasted_iota(jnp.int32, (DIL_BLOCK, HEAD_PAD), 1)
        klane = lax.broadcasted_iota(jnp.int32, (2 * DIL_BLOCK, HEAD_PAD), 1)
        blocks = list(_dil_blocks(s_dim, dilation))
        for g0 in range(0, len(blocks), DIL_GROUPED):
            group = blocks[g0:g0 + DIL_GROUPED]
            logits = [_dil_logits(jnp.where((lane < DIL_HEAD_DIM) == (h == 0), q_ref[blk, :], 0), k_ref[keys, :], b_ref[h],
                                  first, span) for first, blk, keys, _ in group for h in range(2)]
            soft = []
            for lg in logits:
                mx = jnp.max(lg, axis=-1, keepdims=True)
                e = jnp.exp(lg - mx)
                tot = jnp.sum(e, axis=-1, keepdims=True)
                soft.append(((e * (1.0 / tot)).astype(BF16), mx + jnp.log(tot)))
            for i, (_, _, keys, tokens) in enumerate(group):
                vb = v_ref[keys, :]
                o_acc = jnp.zeros((DIL_BLOCK, HEAD_PAD), F32)
                lse_acc = jnp.zeros((DIL_BLOCK, HEAD_PAD), F32)
                for h in range(2):
                    p, lse = soft[2 * i + h]
                    kmine = (klane[:vb.shape[0]] < DIL_HEAD_DIM) == (h == 0)
                    o_acc = o_acc + _dot_nn(p, jnp.where(kmine, vb, 0))
                    lse_acc = jnp.where((lane < DIL_HEAD_DIM) == (h == 0), lse, lse_acc)
                o_ref[tokens, :] = o_acc
                l_ref[tokens, :] = lse_acc

    out = jax.ShapeDtypeStruct((s_dim, d_dim), F32)
    return _pcall(
        body, out_shape=(out, out), grid=(DIL_PAIRS,),
        in_specs=[col(0), col(1), col(2), bias_spec], out_specs=(nat, nat), name=name,
        compiler_params=_params(("parallel",), 3 * [((s_dim, HEAD_PAD), BF16)] + 2 * [((s_dim, HEAD_PAD), F32)]
                                + [((2, DIL_BLOCK, 2 * DIL_BLOCK), F32)], extra=2**21),
    )(qkv, qkv, qkv, bias)


def _dil_mix(lses, outs, name):
    s_dim, d_dim = outs[0].shape
    tm = TOKEN_TILE
    ng = len(outs)

    def body(*refs):
        ls = [refs[g][...] for g in range(ng)]
        mx = ls[0]
        for g in range(1, ng):
            mx = jnp.maximum(mx, ls[g])
        es = [jnp.exp(l - mx) for l in ls]
        tot = es[0]
        for g in range(1, ng):
            tot = tot + es[g]
        o = None
        for g in range(ng):
            al = es[g] / tot
            refs[2 * ng + g][...] = al
            t = al * refs[ng + g][...]
            o = t if o is None else o + t
        refs[3 * ng][...] = o
        refs[3 * ng + 1][...] = o.astype(BF16)

    row = pl.BlockSpec((tm, d_dim), lambda i: (i, 0))
    f = jax.ShapeDtypeStruct((s_dim, d_dim), F32)
    res = _pcall(
        body, out_shape=tuple(ng * [f] + [f, jax.ShapeDtypeStruct((s_dim, d_dim), BF16)]), grid=(s_dim // tm,),
        in_specs=2 * ng * [row], out_specs=tuple((ng + 2) * [row]), name=name,
        compiler_params=_params(("parallel",), (3 * ng + 2) * [((tm, d_dim), F32)], extra=4 * tm * d_dim * 4),
    )(*lses, *outs)
    return res[:ng], res[ng], res[ng + 1]


def _dil_attn_bwd(qkv, bias, d_o, o_mix, alpha, lse, dilation, span, name):
    s_dim = qkv.shape[0]
    d_dim = DIL_HEADS * DIL_HEAD_DIM
    col, nat, bias_spec = _dil_views(s_dim)

    def body(q_ref, k_ref, v_ref, b_ref, do_ref, om_ref, al_ref, l_ref, dq_ref, dk_ref, dv_ref, db_ref, dk_acc, dv_acc):
        db_ref[...] = jnp.zeros_like(db_ref)
        dk_acc[...] = jnp.zeros_like(dk_acc)
        dv_acc[...] = jnp.zeros_like(dv_acc)
        lane = lax.broadcasted_iota(jnp.int32, (DIL_BLOCK, HEAD_PAD), 1)
        klane = lax.broadcasted_iota(jnp.int32, (2 * DIL_BLOCK, HEAD_PAD), 1)
        blocks = list(_dil_blocks(s_dim, dilation))
        heads = [(lane < DIL_HEAD_DIM) == (h == 0) for h in range(2)]
        for g0 in range(0, len(blocks), DIL_GROUPED):
            group = blocks[g0:g0 + DIL_GROUPED]
            staged = []
            for first, blk, kv_rows, tokens in group:
                qb, kb, vb = q_ref[blk, :], k_ref[kv_rows, :], v_ref[kv_rows, :]
                dog = al_ref[tokens, :] * do_ref[tokens, :]
                row_term = dog * om_ref[tokens, :]
                lse_b = l_ref[tokens, :]
                for h in range(2):
                    qh = jnp.where(heads[h], qb, 0)
                    dogh = jnp.where(heads[h], dog, 0.0).astype(BF16)
                    staged.append((_dil_logits(qh, kb, b_ref[h], first, span), _dot_nt(dogh, vb), qh, dogh,
                                   jnp.max(jnp.where(heads[h], lse_b, -jnp.inf), axis=-1, keepdims=True),
                                   jnp.sum(jnp.where(heads[h], row_term, 0.0), axis=-1, keepdims=True)))
            grads = []
            for i, (logits, dp, qh, dogh, lse_h, row) in enumerate(staged):
                p = jnp.exp(logits - lse_h)
                ds = p * (dp - row)
                if group[i // 2][0]:
                    db_ref[i % 2, :, DIL_BLOCK:] += ds
                else:
                    db_ref[i % 2] += ds
                grads.append(((ds * DIL_SCALE).astype(BF16), p.astype(BF16), qh, dogh))
            for i, (_, blk, kv_rows, _) in enumerate(group):
                kb = k_ref[kv_rows, :]
                dq_acc = jnp.zeros((DIL_BLOCK, HEAD_PAD), F32)
                dk_blk = jnp.zeros((kb.shape[0], HEAD_PAD), F32)
                dv_blk = jnp.zeros((kb.shape[0], HEAD_PAD), F32)
                for h in range(2):
                    dsb, pb, qh, dogh = grads[2 * i + h]
                    kmine = (klane[:kb.shape[0]] < DIL_HEAD_DIM) == (h == 0)
                    dq_acc = dq_acc + _dot_nn(dsb, jnp.where(kmine, kb, 0))
                    dk_blk = dk_blk + _dot_tn(dsb, qh)
                    dv_blk = dv_blk + _dot_tn(pb, dogh)
                dq_ref[blk, :] = dq_acc.astype(BF16)
                dk_acc[kv_rows, :] += dk_blk
                dv_acc[kv_rows, :] += dv_blk
        dk_ref[...] = dk_acc[...].astype(BF16)
        dv_ref[...] = dv_acc[...].astype(BF16)

    grad = jax.ShapeDtypeStruct((s_dim, d_dim), BF16)
    return _pcall(
        body, out_shape=(grad, grad, grad, jax.ShapeDtypeStruct(bias.shape, F32)), grid=(DIL_PAIRS,),
        in_specs=[col(0), col(1), col(2), bias_spec, nat, nat, nat, nat],
        out_specs=(nat, nat, nat, bias_spec), name=name,
        scratch_shapes=[pltpu.VMEM((s_dim, HEAD_PAD), F32), pltpu.VMEM((s_dim, HEAD_PAD), F32)],
        compiler_params=_params(("parallel",), 6 * [((s_dim, HEAD_PAD), BF16)] + 4 * [((s_dim, HEAD_PAD), F32)]
                                + 2 * [((2, DIL_BLOCK, 2 * DIL_BLOCK), F32)], extra=2 * s_dim * HEAD_PAD * 4 + 2**21),
    )(qkv, qkv, qkv, bias, d_o, o_mix, alpha, lse)


def _bias_reduce(dbias, buckets, name):
    n_heads = dbias.shape[0]

    def body(db_ref, bk_ref, o_ref):
        ds, bk = db_ref[0], bk_ref[0]
        lane = lax.broadcasted_iota(jnp.int32, (8, HEAD_PAD), 1)
        acc = jnp.zeros((8, HEAD_PAD), F32)
        for b in range(N_BUCKETS):
            acc = jnp.where(lane == b, jnp.sum(jnp.where(bk == b, ds, 0.0)), acc)
        o_ref[0] = acc

    blk = (1, DIL_BLOCK, 2 * DIL_BLOCK)
    return _pcall(
        body, out_shape=jax.ShapeDtypeStruct((n_heads, 8, HEAD_PAD), F32), grid=(n_heads,),
        in_specs=[pl.BlockSpec(blk, lambda h: (h, 0, 0)), pl.BlockSpec(blk, lambda h: (h // DIL_HEADS, 0, 0))],
        out_specs=pl.BlockSpec((1, 8, HEAD_PAD), lambda h: (h, 0, 0)), name=name,
        compiler_params=_params(("parallel",), [(blk, F32), (blk, jnp.int32)], extra=2**20),
    )(dbias, buckets)


def _loss_grad(y, target, name):
    s_dim, d_dim = y.shape
    tm = TOKEN_TILE

    def body(y_ref, t_ref, dy_ref, l_ref):
        @pl.when(pl.program_id(0) == 0)
        def _():
            l_ref[...] = jnp.zeros_like(l_ref)

        err = y_ref[...] - t_ref[...]
        dy_ref[...] = err / d_dim
        sq = (err * err).reshape(tm // 8, 8, d_dim)
        l_ref[...] += 0.5 * jnp.sum(sq, axis=0) / d_dim

    row = pl.BlockSpec((tm, d_dim), lambda i: (i, 0))
    acc = pl.BlockSpec((8, d_dim), lambda i: (0, 0))
    return _pcall(
        body, out_shape=(jax.ShapeDtypeStruct((s_dim, d_dim), F32), jax.ShapeDtypeStruct((8, d_dim), F32)),
        grid=(s_dim // tm,), in_specs=[row, row], out_specs=(row, acc), name=name,
        compiler_params=_params(("arbitrary",), 3 * [((tm, d_dim), F32)], extra=2 * tm * d_dim * 4),
    )(y, target)


def _mod_fwd(c_all, w_mod, b_loc, name):
    depth, d_dim, n = w_mod.shape
    nb = c_all.shape[0]

    def body(c_ref, w_ref, b_ref, o_ref, s_ref):
        cv = c_ref[...]
        sc = cv * jax.nn.sigmoid(cv)
        s_ref[...] = sc
        o_ref[0] = _dot_nn(sc.astype(BF16), w_ref[0].astype(BF16)) + b_ref[0]

    return _pcall(
        body, out_shape=(jax.ShapeDtypeStruct((depth, nb, n), F32), jax.ShapeDtypeStruct((nb, d_dim), F32)), grid=(depth,),
        in_specs=[pl.BlockSpec((nb, d_dim), lambda i: (0, 0)), pl.BlockSpec((1, d_dim, n), lambda i: (i, 0, 0)),
                  pl.BlockSpec((1, 1, n), lambda i: (i, 0, 0))],
        out_specs=(pl.BlockSpec((1, nb, n), lambda i: (i, 0, 0)), pl.BlockSpec((nb, d_dim), lambda i: (0, 0))), name=name,
        compiler_params=_params(("arbitrary",), [((1, d_dim, n), F32)], extra=d_dim * n * 2 + 2**20),
    )(c_all, w_mod, b_loc.reshape(depth, 1, n))


def _sum_parts(parts, name):
    _, rows, cols = parts.shape
    fits = [t for t in range(16, rows // 2 + 1, 16) if rows % t == 0 and NDEV * t * cols * parts.dtype.itemsize <= 3 * 2**20]
    tr = max(fits) if fits else rows

    def body(p_ref, o_ref):
        acc = p_ref[0].astype(F32)
        for k in range(1, NDEV):
            acc = acc + p_ref[k].astype(F32)
        o_ref[...] = acc

    return _pcall(
        body, out_shape=jax.ShapeDtypeStruct((rows, cols), F32), grid=(rows // tr,),
        in_specs=[pl.BlockSpec((NDEV, tr, cols), lambda i: (0, i, 0))], out_specs=pl.BlockSpec((tr, cols), lambda i: (i, 0)),
        name=name, compiler_params=_params(("parallel",), [((NDEV, tr, cols), parts.dtype), ((tr, cols), F32)], extra=2**20),
    )(parts)


def _adamw(w, g, m, v, name):
    shape = w.shape
    cols = shape[-1]
    rows = math.prod(shape[:-1])
    tr = rows
    for cand in (512, 256, 128, 64, 32, 16, 8):
        if rows % cand == 0 and rows > cand and cand * cols * 4 <= 2**21:
            tr = cand
            break

    def body(w_ref, g_ref, m_ref, v_ref, d_ref, mo_ref, vo_ref):
        gv = g_ref[...]
        mn = ADAM_B1 * m_ref[...] + (1.0 - ADAM_B1) * gv
        vn = ADAM_B2 * v_ref[...] + (1.0 - ADAM_B2) * (gv * gv)
        m_hat = mn / (1.0 - ADAM_B1 ** ADAM_STEP)
        v_hat = vn / (1.0 - ADAM_B2 ** ADAM_STEP)
        d_ref[...] = -ADAM_LR * (m_hat / (jnp.sqrt(v_hat) + ADAM_EPS) + ADAM_WD * w_ref[...])
        mo_ref[...] = mn
        vo_ref[...] = vn

    blk = pl.BlockSpec((tr, cols), lambda i: (i, 0))
    out = jax.ShapeDtypeStruct((rows, cols), F32)
    res = _pcall(
        body, out_shape=(out, out, out), grid=(rows // tr,), in_specs=4 * [blk], out_specs=(blk, blk, blk), name=name,
        compiler_params=_params(("parallel",), 7 * [((tr, cols), F32)], extra=4 * tr * cols * 4),
    )(*(a.reshape(rows, cols) for a in (w, g, m, v)))
    return tuple(r.reshape(shape) for r in res)


def _peers():
    x, y, c = lax.axis_index("x"), lax.axis_index("y"), lax.axis_index("c")
    flip = lambda v, f: 1 - v if f else v
    peers = []
    for f in range(1, NDEV):
        px, py, pc = flip(x, f & 4), flip(y, f & 2), flip(c, f & 1)
        peers.append(((px, py, pc), 4 * px + 2 * py + pc))
    return (x, y, c), 4 * x + 2 * y + c, peers


def _places():
    x, y, c = lax.axis_index("x"), lax.axis_index("y"), lax.axis_index("c")
    place = lambda px, py, pc: ((px, py, pc), 4 * px + 2 * py + pc)
    return place(x, y, c), place(x, y, 1 - c), [place(1 - x, y, c), place(x, 1 - y, c), place(1 - x, 1 - y, c)]


def _exchange(arrs, gather, name):
    n = len(arrs)
    hbm = pl.BlockSpec(memory_space=pltpu.HBM)
    if gather:
        out_shape = [jax.ShapeDtypeStruct((NDEV * a.shape[0], a.shape[1]), a.dtype) for a in arrs]
    else:
        out_shape = [jax.ShapeDtypeStruct((NDEV, a.shape[0] // NDEV, a.shape[1]), a.dtype) for a in arrs]

    def body(*refs):
        ins, outs = refs[:n], refs[n:2 * n]
        send_sems, recv_sems, local_sems = refs[2 * n:]
        me_pos, me, peers = _peers()
        local = []
        for k in range(n):
            rows = arrs[k].shape[0] if gather else arrs[k].shape[0] // NDEV
            if gather:
                src_of = lambda idx: ins[k]
                dst_of = lambda idx: outs[k].at[pl.ds(me * rows, rows)]
                mine = (ins[k], outs[k].at[pl.ds(me * rows, rows)])
            else:
                src_of = lambda idx: ins[k].at[pl.ds(idx * rows, rows)]
                dst_of = lambda idx: outs[k].at[me]
                mine = (ins[k].at[pl.ds(me * rows, rows)], outs[k].at[me])
            cp = pltpu.make_async_copy(mine[0], mine[1], local_sems.at[k])
            cp.start()
            local.append(cp)
            for pos, idx in peers:
                pltpu.make_async_remote_copy(src_ref=src_of(idx), dst_ref=dst_of(idx), send_sem=send_sems.at[k],
                                             recv_sem=recv_sems.at[k], device_id=pos, device_id_type=MESH).start()
        for k in range(n):
            rows = arrs[k].shape[0] if gather else arrs[k].shape[0] // NDEV
            sent = ins[k].at[pl.ds(0, (NDEV - 1) * rows)] if not gather else outs[k].at[pl.ds(0, (NDEV - 1) * rows)]
            got = outs[k].at[pl.ds(0, (NDEV - 1) * rows)] if gather else outs[k].at[pl.ds(0, NDEV - 1)]
            pltpu.make_async_remote_copy(src_ref=sent, dst_ref=sent, send_sem=send_sems.at[k], recv_sem=recv_sems.at[k],
                                         device_id=me_pos, device_id_type=MESH).wait_send()
            pltpu.make_async_remote_copy(src_ref=got, dst_ref=got, send_sem=send_sems.at[k], recv_sem=recv_sems.at[k],
                                         device_id=me_pos, device_id_type=MESH).wait_recv()
            local[k].wait()

    return pl.pallas_call(
        body, out_shape=out_shape, in_specs=n * [hbm], out_specs=n * [hbm], name=name,
        scratch_shapes=[pltpu.SemaphoreType.DMA((n,)), pltpu.SemaphoreType.DMA((n,)), pltpu.SemaphoreType.DMA((n,))],
        compiler_params=pltpu.CompilerParams(has_side_effects=True),
    )(*arrs)


_HBM = pl.BlockSpec(memory_space=pltpu.HBM)
_SEM = pl.BlockSpec(memory_space=pltpu.SEMAPHORE)
_DATAFLOW = pltpu.SideEffectType.DATAFLOW_SIDE_EFFECTING


def _split_start(srcs, groups, gather, name):
    n = len(srcs)
    if gather:
        lands = [lax.empty((NDEV * a.shape[0], a.shape[1]), a.dtype) for a in srcs]
    else:
        lands = [lax.empty((NDEV, a.shape[0] // NDEV, a.shape[1]), a.dtype) for a in srcs]
    n_sem = 3 * len(groups)

    def body(*refs):
        src_refs, land_refs = refs[:n], refs[n:2 * n]
        sems = refs[2 * n:2 * n + n_sem]
        token = refs[-1]
        (_, my), sibling, chips = _places()
        _, _, peers = _peers()
        targets = [sibling] + chips if gather else peers
        for g, members in enumerate(groups):
            for j, k in enumerate(members):
                _own_copy(src_refs[k], land_refs[k], sems[3 * g + 2].at[j], my, gather).start()
        for g, members in enumerate(groups):
            for j, k in enumerate(members):
                rows = srcs[k].shape[0] if gather else srcs[k].shape[0] // NDEV
                for pos, idx in targets:
                    src = src_refs[k] if gather else src_refs[k].at[pl.ds(idx * rows, rows)]
                    dst = land_refs[k].at[pl.ds(my * rows, rows)] if gather else land_refs[k].at[my]
                    pltpu.make_async_remote_copy(src_ref=src, dst_ref=dst, send_sem=sems[3 * g].at[j],
                                                 recv_sem=sems[3 * g + 1].at[j], device_id=pos, device_id_type=MESH).start()
        token[...] = jnp.zeros_like(token)

    out_shape = []
    for members in groups:
        out_shape += 3 * [pltpu.SemaphoreType.DMA((len(members),))]
    out_shape += [pltpu.HBM(a.shape, a.dtype) for a in srcs] + [pltpu.HBM(a.shape, a.dtype) for a in lands]
    out_shape.append(jax.ShapeDtypeStruct((8, 128), F32))
    res = pl.pallas_call(
        body, name=name, out_shape=tuple(out_shape), in_specs=2 * n * [_HBM],
        out_specs=tuple(n_sem * [_SEM] + 2 * n * [_HBM] + [pl.BlockSpec(memory_space=pltpu.VMEM)]),
        input_output_aliases={i: n_sem + i for i in range(2 * n)},
        compiler_params=pltpu.CompilerParams(has_side_effects=_DATAFLOW),
    )(*[pltpu.with_memory_space_constraint(a, pltpu.HBM) for a in list(srcs) + lands])
    sems = [tuple(res[3 * g:3 * g + 3]) for g in range(len(groups))]
    return sems, list(res[n_sem:n_sem + n]), list(res[n_sem + n:n_sem + 2 * n]), res[-1]


def _own_copy(src_ref, land_ref, sem, my, gather):
    if gather:
        rows = src_ref.shape[0]
        return pltpu.make_async_copy(src_ref, land_ref.at[pl.ds(my * rows, rows)], sem)
    rows = src_ref.shape[0] // NDEV
    return pltpu.make_async_copy(src_ref.at[pl.ds(my * rows, rows)], land_ref.at[my], sem)


def _wait_all(land_ref, blocks_per_dev, copies, send_sem, recv_sem, me_pos):
    part = land_ref.at[pl.ds(0, copies * blocks_per_dev)]
    pltpu.make_async_remote_copy(src_ref=part, dst_ref=part, send_sem=send_sem, recv_sem=recv_sem,
                                 device_id=me_pos, device_id_type=MESH).wait()


def _gather_forward(sems, srcs, lands, after, name):
    n = len(srcs)

    def body(*refs):
        land_refs = refs[n:2 * n]
        send_a, recv_a = refs[2 * n], refs[2 * n + 1]
        send_b, recv_b = refs[2 * n + 3], refs[2 * n + 4]
        token = refs[-1]
        (me_pos, _), sibling, chips = _places()
        for j in range(n):
            _wait_all(land_refs[j], lands[j].shape[0] // NDEV, 1 + OTHER_CHIPS, send_a.at[j], recv_a.at[j], me_pos)
        for j in range(n):
            rows = lands[j].shape[0] // NDEV
            for _, idx in chips:
                block = land_refs[j].at[pl.ds(idx * rows, rows)]
                pltpu.make_async_remote_copy(src_ref=block, dst_ref=block, send_sem=send_b.at[j], recv_sem=recv_b.at[j],
                                             device_id=sibling[0], device_id_type=MESH).start()
        token[...] = jnp.zeros_like(token)

    res = pl.pallas_call(
        body, name=name,
        out_shape=(pltpu.SemaphoreType.DMA((n,)), pltpu.SemaphoreType.DMA((n,)))
        + tuple(pltpu.HBM(a.shape, a.dtype) for a in list(srcs) + list(lands)) + (jax.ShapeDtypeStruct((8, 128), F32),),
        in_specs=2 * n * [_HBM] + [_SEM, _SEM, pl.BlockSpec(memory_space=pl.ANY)],
        out_specs=tuple([_SEM, _SEM] + 2 * n * [_HBM] + [pl.BlockSpec(memory_space=pltpu.VMEM)]),
        input_output_aliases={i: 2 + i for i in range(2 * n)},
        compiler_params=pltpu.CompilerParams(has_side_effects=_DATAFLOW),
    )(*srcs, *lands, sems[0], sems[1], after)
    return (res[0], res[1]), list(res[2:2 + n]), list(res[2 + n:2 + 2 * n]), res[-1]


def _split_wait(sems, srcs, lands, after, copies, gather, name):
    n = len(srcs)

    def body(*refs):
        src_refs, land_refs = refs[:n], refs[n:2 * n]
        send_sem, recv_sem, local_sem = refs[2 * n], refs[2 * n + 1], refs[2 * n + 2]
        (me_pos, my), _, _ = _places()
        for j in range(n):
            _wait_all(land_refs[j], lands[j].shape[0] // NDEV, copies, send_sem.at[j], recv_sem.at[j], me_pos)
            _own_copy(src_refs[j], land_refs[j], local_sem.at[j], my, gather).wait()

    res = pl.pallas_call(
        body, name=name, out_shape=tuple(pltpu.HBM(a.shape, a.dtype) for a in list(srcs) + list(lands)),
        in_specs=2 * n * [_HBM] + [_SEM, _SEM, _SEM, pl.BlockSpec(memory_space=pl.ANY)], out_specs=tuple(2 * n * [_HBM]),
        input_output_aliases={i: i for i in range(2 * n)},
        compiler_params=pltpu.CompilerParams(has_side_effects=_DATAFLOW),
    )(*srcs, *lands, sems[0], sems[1], sems[2], after)
    return list(res[n:])


def _chained(gate, mid, after):
    return gate if mid is None else gate + mid(after)[:1, :1]


def _ffn_fwd(x, norms, mod, w, mid=None):
    (pre_g, post_g), (shift, scale, gate), (wg_t, wu_t, wd) = norms, mod, w
    hn, g, u, a = _ffn_up(x, pre_g, scale, shift, wg_t, wu_t, "ffn_up")
    if callable(wd):
        wd = wd(a)
    x_out, f = _mm_post(a, wd, x, post_g, _chained(gate, mid, a), FFN_RES, "ffn_down")
    return x_out, (x, hn, g, u, a, f), (wg_t, wu_t, wd)


def _ffn_bwd(dx_out, saved, norms, mod, w, send=None):
    (pre_g, post_g), (_, scale, gate), (wg_t, wu_t, wd) = norms, mod, w
    x, hn, g, u, a, f = saved
    d_model = x.shape[1]
    if send is None:
        df, dg, du, dx, dgate, dpost, dshift, dscale, dpre = _ffn_bwd_fused(dx_out, saved, pre_g, post_g, scale, gate,
                                                                            wg_t, wu_t, wd, "ffn_bwd")
        dws = tuple(_mm([pair], "tn", BF16, 256, d_model, "ffn_dw") for pair in ((dg, hn), (du, hn), (a, df)))
        return dx, (dpre, dpost), (dshift, dscale, dgate), dws
    sent = send
    df, dgate, dpost = _post_bwd(dx_out, f, post_g, gate, FFN_RES, "ffn_post_bwd")
    dwd = _mm([(a, df)], "tn", BF16, 256, d_model, "ffn_dw")
    dg, du = _ffn_dgu(df, wd, g, u, "ffn_dgu", after=sent(2, dwd))
    dwg_t = _mm([(dg, hn)], "tn", BF16, 256, d_model, "ffn_dw")
    dwu_t = _mm([(du, hn)], "tn", BF16, 256, d_model, "ffn_dw", after=sent(0, dwg_t))
    dhn = _mm([(dg, wg_t), (du, wu_t)], "nn", F32, TOKEN_TILE, d_model, "ffn_dhn", after=sent(1, dwu_t))
    dx, dshift, dscale, dpre = _prenorm_bwd(dx_out, [dhn], x, pre_g, scale, "prenorm_bwd")
    return dx, (dpre, dpost), (dshift, dscale, dgate), (dwg_t, dwu_t, dwd)


def _mla_fwd(x, norms, mod, w, rope, mid=None):
    (pre_g, post_g), (shift, scale, gate) = norms, mod
    w_in, q_norm, wq_t, kv_norm, wkv_t, wo = w
    hn, lat = _prenorm_mm(x, pre_g, scale, shift, w_in, "nn", F32, LAT_PAD, "mla_in")
    gate = _chained(gate, mid, lat)
    q, k, v, qn, kvn = _mla_qkv(lat, q_norm, kv_norm, wq_t, wkv_t, rope, "mla_qkv")
    o = _mla_attn_fwd(q, k, v, "mla_attn_fwd")
    x_out, f = _mm_post(o, wo, x, post_g, gate, 1.0, "mla_out")
    return x_out, (x, hn, lat, q, k, v, qn, kvn, o, f)


def _mla_bwd(dx_out, saved, norms, mod, w, rope):
    (pre_g, post_g), (_, scale, gate) = norms, mod
    w_in, q_norm, wq_t, kv_norm, wkv_t, wo = w
    x, hn, lat, q, k, v, qn, kvn, o, f = saved
    d_model = x.shape[1]
    df, dgate, dpost = _post_bwd(dx_out, f, post_g, gate, 1.0, "mix_post_bwd")
    d_o = _mm([(df, wo)], "nt", F32, TOKEN_TILE, wo.shape[0], "mla_do")
    dwo = _mm([(o, df)], "tn", BF16, TOKEN_TILE, d_model, "mla_dwo")
    dq, dk, dv = _mla_attn_bwd(q, k, v, d_o, "mla_attn_bwd")
    dqp, dkv, dlat, dq_norm, dkv_norm = _mla_qkv_bwd(dq, dk, dv, lat, q_norm, kv_norm, wq_t, wkv_t, rope, "mla_qkv_bwd")
    dwq_t = _mm([(dqp, qn)], "tn", BF16, TOKEN_TILE, Q_LORA, "mla_dwq")
    dwkv_t = _mm([(dkv, kvn)], "tn", BF16, TOKEN_TILE, KV_LORA, "mla_dwkv")
    dw_in = _mm([(hn, dlat)], "tn", BF16, TOKEN_TILE, LAT_PAD, "mla_dwin")
    dhn = _mm([(dlat, w_in)], "nt", F32, TOKEN_TILE, d_model, "mla_dhn")
    dx, dshift, dscale, dpre = _prenorm_bwd(dx_out, [dhn], x, pre_g, scale, "prenorm_bwd")
    return dx, (dpre, dpost), (dshift, dscale, dgate), (dw_in, dq_norm, dwq_t, dkv_norm, dwkv_t, dwo)


def _dil_fwd(x, norms, mod, w, bias, mid=None):
    (pre_g, post_g), (shift, scale, gate), (w_in_t, wo) = norms, mod, w
    width = 3 * DIL_HEADS * DIL_HEAD_DIM
    hns, qkvs, outs, lses = [], [], [], []
    for g, (window, dilation) in enumerate(DIL_GROUPS):
        hn, qkv = _prenorm_mm(x, pre_g, scale, shift, w_in_t[g * width:(g + 1) * width], "nt", BF16, width,
                              "dil_in", perm=dilation)
        if g == 0:
            gate = _chained(gate, mid, qkv)
        o, lse = _dil_attn_fwd(qkv, bias[g], dilation, window // dilation, "dil_attn_fwd")
        hns.append(hn), qkvs.append(qkv), outs.append(o), lses.append(lse)
    alphas, o_mix, o_mix_b = _dil_mix(lses, outs, "dil_mix")
    x_out, f = _mm_post(o_mix_b, wo, x, post_g, gate, 1.0, "dil_out")
    return x_out, (x, hns, qkvs, lses, alphas, o_mix, o_mix_b, f)


def _dil_bwd(dx_out, saved, norms, mod, w, bias):
    (pre_g, post_g), (_, scale, gate), (w_in_t, wo) = norms, mod, w
    x, hns, qkvs, lses, alphas, o_mix, o_mix_b, f = saved
    d_model = x.shape[1]
    inner = DIL_HEADS * DIL_HEAD_DIM
    df, dgate, dpost = _post_bwd(dx_out, f, post_g, gate, 1.0, "mix_post_bwd")
    d_o = _mm([(df, wo)], "nt", F32, TOKEN_TILE, inner, "dil_do")
    dwo = _mm([(o_mix_b, df)], "tn", BF16, TOKEN_TILE, d_model, "dil_dwo")
    dhns, dws, dbs = [], [], []
    for g, (window, dilation) in enumerate(DIL_GROUPS):
        grads = _dil_attn_bwd(qkvs[g], bias[g], d_o, o_mix, alphas[g], lses[g], dilation, window // dilation, "dil_attn_bwd")
        dbs.append(grads[3])
        w_parts = [w_in_t[(3 * g + j) * inner:(3 * g + j + 1) * inner] for j in range(3)]
        dhns.append(_mm(list(zip(grads[:3], w_parts)), "nn", F32, TOKEN_TILE, d_model, "dil_dhn", out_perm=dilation))
        dws += [_mm([(grads[j], hns[g])], "tn", BF16, TOKEN_TILE, d_model, "dil_dwin") for j in range(3)]
    dx, dshift, dscale, dpre = _prenorm_bwd(dx_out, dhns, x, pre_g, scale, "prenorm_bwd3")
    return dx, (dpre, dpost), (dshift, dscale, dgate), (jnp.concatenate(dws, axis=0), dwo), jnp.concatenate(dbs, axis=0)


def _pad_rows(a, rows):
    return jnp.pad(a, ((0, rows - a.shape[0]), (0, 0)))


def _lanes(a):
    flat = a.reshape(-1).astype(F32)
    rows = -(-flat.shape[0] // 1024) * 8
    return jnp.pad(flat, (0, rows * 128 - flat.shape[0])).reshape(rows, 128)


def kernel(x, c, norm_pre, norm_post, w_mod, b_mod, ffn_w_gate, ffn_w_up, ffn_w_down, mla_w_in, mla_q_norm, mla_w_q_up, mla_kv_norm, mla_w_kv_up, mla_w_o, dil_w_in, dil_w_o, rel_bias, loss_target, m_norm_pre, m_norm_post, m_w_mod, m_b_mod, m_ffn_w_gate, m_ffn_w_up, m_ffn_w_down, m_mla_w_in, m_mla_q_norm, m_mla_w_q_up, m_mla_kv_norm, m_mla_w_kv_up, m_mla_w_o, m_dil_w_in, m_dil_w_o, m_rel_bias, v_norm_pre, v_norm_post, v_w_mod, v_b_mod, v_ffn_w_gate, v_ffn_w_up, v_ffn_w_down, v_mla_w_in, v_mla_q_norm, v_mla_w_q_up, v_mla_kv_norm, v_mla_w_kv_up, v_mla_w_o, v_dil_w_in, v_dil_w_o, v_rel_bias):
    me = 4 * lax.axis_index("x") + 2 * lax.axis_index("y") + lax.axis_index("c")
    depth, n_sub, d_loc = norm_pre.shape
    d_model = x.shape[2]
    mod_loc_cols = w_mod.shape[2]
    x0, target = x[0], loss_target[0]

    bf_t = lambda a: a.astype(BF16).T
    ffn_ids = [(i, h) for i in range(depth) for h in range(2)]
    shards = []
    for i, h in ffn_ids:
        shards += [bf_t(ffn_w_gate[i, h]), bf_t(ffn_w_up[i, h]), ffn_w_down[i, h].astype(BF16)]
    shards += [mla_w_in[0].astype(BF16), bf_t(mla_w_q_up[0]), bf_t(mla_w_kv_up[0]), mla_w_o[0].astype(BF16),
               bf_t(dil_w_in[0]), dil_w_o[0].astype(BF16)]
    n_ffn = 3 * len(ffn_ids)
    members = {(0, 0): [0, 1, 2], (0, 1): [n_ffn, n_ffn + 1, n_ffn + 2, n_ffn + 3], (0, 2): [3, 4, 5],
               (1, 0): [6, 7, 8], (1, 1): [n_ffn + 4, n_ffn + 5], (1, 2): [9, 10, 11]}
    order = [(i, s) for i in range(depth) for s in range(n_sub)]

    small = jnp.concatenate([c.reshape(8, 128), _pad_rows(norm_pre.reshape(depth * n_sub, d_loc), 8),
                             _pad_rows(norm_post.reshape(depth * n_sub, d_loc), 8)], axis=0)
    small_all = _exchange([small], True, "gather_small")[0].reshape(NDEV, 24, 128)
    c_all = small_all[:, 0:8].reshape(NDEV, d_model)
    gains = lambda lo: jnp.transpose(small_all[:, lo:lo + depth * n_sub], (1, 0, 2)).reshape(depth, n_sub, 1, d_model)
    pre_full, post_full = gains(8), gains(16)

    b_loc = lax.dynamic_slice(b_mod, (0, me * mod_loc_cols), (depth, mod_loc_cols))
    mod_cols, silu_c = _mod_fwd(c_all, w_mod, b_loc, "mod_fwd")
    mod_all = _exchange([mod_cols.reshape(depth * NDEV, mod_loc_cols)], True, "gather_mod")[0]
    mod_all = mod_all.reshape(NDEV, depth, NDEV, mod_loc_cols)
    mod_mine = lax.dynamic_index_in_dim(mod_all, me, axis=2, keepdims=False)
    mod = jnp.transpose(mod_mine, (1, 0, 2)).reshape(depth, n_sub, 3, 1, d_model)

    shards[0], _ = lax.optimization_barrier((shards[0], mod_all))
    first = order[0]
    stages = [("%d%d" % first, members[first][:2]), ("%d%dd" % first, members[first][2:])]
    stages += [("%d%d" % key, members[key]) for key in order[1:]]
    stage_names = [name for name, _ in stages]
    g_sems, g_srcs, g_lands, g_token = _split_start(shards, [idx for _, idx in stages], True, "gather_weights_start")

    forwarded = {}

    def forward(stage, after):
        idx = stages[stage_names.index(stage)][1]
        forwarded[stage] = _gather_forward(g_sems[stage_names.index(stage)], [g_srcs[k] for k in idx],
                                           [g_lands[k] for k in idx], after, "gather_forward_" + stage)
        return forwarded[stage][3]

    def weights_of(stage, after):
        (send_b, recv_b), srcs, lands, _ = forwarded[stage]
        local = g_sems[stage_names.index(stage)][2]
        return _split_wait((send_b, recv_b, local), srcs, lands, after, OTHER_CHIPS, True, "gather_wait_" + stage)

    def late_down(after):
        forward("%d%dd" % first, after)
        return weights_of("%d%dd" % first, after)[0]

    lat_real = Q_LORA + KV_LORA
    qk = QK_NOPE + QK_ROPE

    def mla_weights(after):
        w_in, wq_t, wkv_t, wo = weights_of("01", after)
        w_in_pad = jnp.concatenate([w_in[:, :lat_real], jnp.zeros((d_model, QK_NOPE), BF16), w_in[:, lat_real:],
                                    jnp.zeros((d_model, HEAD_PAD - QK_NOPE - QK_ROPE), BF16)], axis=1)
        wq_pad = jnp.pad(wq_t.reshape(MLA_HEADS, qk, Q_LORA), ((0, 0), (0, HEAD_PAD - qk), (0, 0)))
        wo_pad = jnp.pad(wo.reshape(MLA_HEADS, V_HEAD, d_model), ((0, 0), (HEAD_PAD - V_HEAD, 0), (0, 0)))
        return (w_in_pad, mla_q_norm, wq_pad.reshape(MLA_HEADS * HEAD_PAD, Q_LORA), mla_kv_norm, wkv_t,
                wo_pad.reshape(MLA_HEADS * HEAD_PAD, d_model))

    zero = g_token[0, 0]
    rope = _rope_tables(zero)
    buckets = jnp.stack([_dil_buckets(dil) for _, dil in DIL_GROUPS]) + zero.astype(jnp.int32)
    onehot = (buckets[..., None] == jnp.arange(N_BUCKETS)).astype(F32)
    bias = jnp.einsum("gqkb,bgh->ghqk", onehot, rel_bias.reshape(N_BUCKETS, len(DIL_GROUPS), DIL_HEADS),
                      precision=lax.Precision.HIGHEST)

    norms = lambda i, s: (pre_full[i, s], post_full[i, s])
    mods = lambda i, s: (mod[i, s, 0], mod[i, s, 1], mod[i, s, 2])
    saved, weights = {}, {}
    h = lax.optimization_barrier((x0, bias, buckets, *rope))[0]
    forward("%d%d" % first, h)
    for n, (i, s) in enumerate(order):
        got = mla_weights(h) if (s == 1 and i % 2 == 0) else tuple(weights_of("%d%d" % (i, s), h))
        mid = None if n + 1 == len(order) else (lambda after, nxt="%d%d" % order[n + 1]: forward(nxt, after))
        if s != 1:
            got = got if len(got) == 3 else (*got, late_down)
            h, saved[i, s], weights[i, s] = _ffn_fwd(h, norms(i, s), mods(i, s), got, mid)
            continue
        weights[i, s] = got
        if i % 2 == 0:
            h, saved[i, s] = _mla_fwd(h, norms(i, s), mods(i, s), weights[i, s], rope, mid)
        else:
            h, saved[i, s] = _dil_fwd(h, norms(i, s), mods(i, s), weights[i, s], bias, mid)
    dh, loss_parts = _loss_grad(h, target, "loss")

    dnorm, dmod, sent = {}, {}, {}
    token = jnp.zeros((8, 128), F32)
    last = order[0]

    def send_last(j, dw):
        sent[last, j] = _split_start([dw], [[0]], False, "scatter_start_%d%d_%d" % (*last, j))
        return sent[last, j][3]

    for i, s in reversed(order):
        md = mods(i, s)
        md = (md[0], md[1], md[2] + token[:1, :1])
        if (i, s) == last:
            dh, dnorm[i, s], dmod[i, s], _ = _ffn_bwd(dh, saved[i, s], norms(i, s), md, weights[i, s], send_last)
            continue
        if s != 1:
            dh, dnorm[i, s], dmod[i, s], dws = _ffn_bwd(dh, saved[i, s], norms(i, s), md, weights[i, s])
        elif i % 2 == 0:
            dh, dnorm[i, s], dmod[i, s], dmla = _mla_bwd(dh, saved[i, s], norms(i, s), md, weights[i, s], rope)
            dw_in_pad, dq_norm, dwq_pad, dkv_norm, dwkv_t, dwo_pad = dmla
            dw_in = jnp.concatenate([dw_in_pad[:, :lat_real], dw_in_pad[:, lat_real + QK_NOPE:lat_real + qk]], axis=1)
            dwq_t = dwq_pad.reshape(MLA_HEADS, HEAD_PAD, Q_LORA)[:, :qk].reshape(MLA_HEADS * qk, Q_LORA)
            dwo = dwo_pad.reshape(MLA_HEADS, HEAD_PAD, d_model)[:, HEAD_PAD - V_HEAD:].reshape(MLA_HEADS * V_HEAD, d_model)
            dws = (dw_in, dwq_t, dwkv_t, dwo)
        else:
            dh, dnorm[i, s], dmod[i, s], dws, dbias = _dil_bwd(dh, saved[i, s], norms(i, s), md, weights[i, s], bias)
        sent[i, s] = _split_start(list(dws), [list(range(len(dws)))], False, "scatter_start_%d%d" % (i, s))
        token = sent[i, s][3]
    grad_x = dh[None]

    mine = {}
    for key in reversed(order[1:]):
        sems, srcs, lands, _ = sent[key]
        parts = _split_wait(sems[0], srcs, lands, dh, NDEV - 1, False, "scatter_wait_%d%d" % key)
        for k, p in zip(members[key], parts):
            mine[k] = _sum_parts(p, "sum_parts")
    g_mla_in, g_q_up, g_kv_up, g_mla_o, g_dil_in, g_dil_o = (mine[k] for k in range(n_ffn, n_ffn + 6))
    g_mla_in, g_q_up, g_kv_up, g_mla_o = g_mla_in[None], g_q_up.T[None], g_kv_up.T[None], g_mla_o[None]
    g_dil_in, g_dil_o = g_dil_in.T[None], g_dil_o[None]
    early = {"mla_w_in": _adamw(mla_w_in, g_mla_in, m_mla_w_in, v_mla_w_in, "adamw"),
             "mla_w_q_up": _adamw(mla_w_q_up, g_q_up, m_mla_w_q_up, v_mla_w_q_up, "adamw"),
             "mla_w_kv_up": _adamw(mla_w_kv_up, g_kv_up, m_mla_w_kv_up, v_mla_w_kv_up, "adamw"),
             "mla_w_o": _adamw(mla_w_o, g_mla_o, m_mla_w_o, v_mla_w_o, "adamw"),
             "dil_w_in": _adamw(dil_w_in, g_dil_in, m_dil_w_in, v_dil_w_in, "adamw"),
             "dil_w_o": _adamw(dil_w_o, g_dil_o, m_dil_w_o, v_dil_w_o, "adamw")}
    dbias_sums = _bias_reduce(dbias, buckets, "bias_reduce")
    tied = lax.optimization_barrier((dbias_sums, *[a for step in early.values() for a in step]))
    dbias_sums, early = tied[0], {name: tuple(tied[1 + 3 * n:4 + 3 * n]) for n, name in enumerate(early)}
    for j in (2, 0, 1):
        sems, srcs, lands, _ = sent[last, j]
        parts = _split_wait(sems[0], srcs, lands, dbias_sums, NDEV - 1, False, "scatter_wait_%d%d_%d" % (*last, j))
        mine[members[last][j]] = _sum_parts(parts[0], "sum_parts")
    g_gate = jnp.stack([mine[3 * n].T for n in range(len(ffn_ids))]).reshape(ffn_w_gate.shape)
    g_up = jnp.stack([mine[3 * n + 1].T for n in range(len(ffn_ids))]).reshape(ffn_w_up.shape)
    g_down = jnp.stack([mine[3 * n + 2] for n in range(len(ffn_ids))]).reshape(ffn_w_down.shape)

    dmod_mine = jnp.concatenate([jnp.concatenate(dmod[i, s], axis=0) for i in range(depth) for s in range(n_sub)], axis=0)
    dpre_mine = jnp.concatenate([dnorm[i, s][0] for i in range(depth) for s in range(n_sub)], axis=0)
    dpost_mine = jnp.concatenate([dnorm[i, s][1] for i in range(depth) for s in range(n_sub)], axis=0)
    dbias_tab = dbias_sums[:, 0, :N_BUCKETS].T
    pieces = [dmod_mine, dpre_mine, dpost_mine, dq_norm, dkv_norm, dbias_tab, jnp.sum(loss_parts).reshape(1, 1)]
    packed = [_lanes(p) for p in pieces]
    offs = [0]
    for p in packed:
        offs.append(offs[-1] + p.shape[0])
    everyone = _exchange([jnp.concatenate(packed, axis=0)], True, "gather_small_grads")[0].reshape(NDEV, offs[-1], 128)
    total = _sum_parts(everyone, "sum_small")
    take = lambda n, shape: total[offs[n]:offs[n + 1]].reshape(-1)[:math.prod(shape)].reshape(shape)
    g_b_mod = take(0, b_mod.shape)
    col0 = me * d_loc
    g_norm_pre = lax.dynamic_slice(take(1, (depth, n_sub, d_model)), (0, 0, col0), norm_pre.shape)
    g_norm_post = lax.dynamic_slice(take(2, (depth, n_sub, d_model)), (0, 0, col0), norm_post.shape)
    g_q_norm, g_kv_norm = take(3, mla_q_norm.shape), take(4, mla_kv_norm.shape)
    g_rel_bias = take(5, rel_bias.shape)
    loss = take(6, ())

    dmod_all = everyone[:, offs[0]:offs[1]].reshape(NDEV, depth, NDEV * mod_loc_cols)
    dmod_cols = lax.dynamic_slice(dmod_all, (0, 0, me * mod_loc_cols), (NDEV, depth, mod_loc_cols))
    silu_t = jnp.pad(silu_c.T, ((0, 0), (0, HEAD_PAD - NDEV)))
    g_w_mod = jnp.stack([_mm([(silu_t, jnp.pad(dmod_cols[:, i], ((0, HEAD_PAD - NDEV), (0, 0))))], "nn", F32, TOKEN_TILE,
                             mod_loc_cols, "mod_bwd") for i in range(depth)])

    ws = (norm_pre, norm_post, w_mod, b_mod, ffn_w_gate, ffn_w_up, ffn_w_down, mla_w_in, mla_q_norm, mla_w_q_up, mla_kv_norm,
          mla_w_kv_up, mla_w_o, dil_w_in, dil_w_o, rel_bias)
    gs = (g_norm_pre, g_norm_post, g_w_mod, g_b_mod, g_gate, g_up, g_down, g_mla_in, g_q_norm, g_q_up, g_kv_norm, g_kv_up,
          g_mla_o, g_dil_in, g_dil_o, g_rel_bias)
    ms = (m_norm_pre, m_norm_post, m_w_mod, m_b_mod, m_ffn_w_gate, m_ffn_w_up, m_ffn_w_down, m_mla_w_in, m_mla_q_norm,
          m_mla_w_q_up, m_mla_kv_norm, m_mla_w_kv_up, m_mla_w_o, m_dil_w_in, m_dil_w_o, m_rel_bias)
    vs = (v_norm_pre, v_norm_post, v_w_mod, v_b_mod, v_ffn_w_gate, v_ffn_w_up, v_ffn_w_down, v_mla_w_in, v_mla_q_norm,
          v_mla_w_q_up, v_mla_kv_norm, v_mla_w_kv_up, v_mla_w_o, v_dil_w_in, v_dil_w_o, v_rel_bias)
    names = ("norm_pre", "norm_post", "w_mod", "b_mod", "ffn_w_gate", "ffn_w_up", "ffn_w_down", "mla_w_in", "mla_q_norm",
             "mla_w_q_up", "mla_kv_norm", "mla_w_kv_up", "mla_w_o", "dil_w_in", "dil_w_o", "rel_bias")
    stepped = [early[n] if n in early else _adamw(w, g, m, v, "adamw") for n, w, g, m, v in zip(names, ws, gs, ms, vs)]
    deltas, new_m, new_v = zip(*stepped)
    return (loss, grad_x, *gs, *deltas, *new_m, *new_v)
```

```python
import math

import jax
import jax.numpy as jnp
from jax import lax
from jax.experimental import pallas as pl
from jax.experimental.pallas import tpu as pltpu

F32 = jnp.float32
BF16 = jnp.bfloat16
MESH = pl.DeviceIdType.MESH

NDEV = 8
OTHER_CHIPS = 3
D_MODEL = 1024
SEQ = 2048
D_FF = 2816
EPS = 1e-6
FFN_RES = 0.5

MLA_HEADS = 16
Q_LORA = 384
KV_LORA = 256
QK_NOPE = 64
QK_ROPE = 32
V_HEAD = 64
ROPE_THETA = 10000.0
HEAD_PAD = 128
LAT_PAD = Q_LORA + KV_LORA + HEAD_PAD
MLA_SCALE = (QK_NOPE + QK_ROPE) ** -0.5

DIL_GROUPS = ((128, 1), (512, 4), (2048, 16))
DIL_HEADS = 16
DIL_HEAD_DIM = 64
DIL_BLOCK = 128
DIL_PAIRS = DIL_HEADS // 2
DIL_SCALE = DIL_HEAD_DIM ** -0.5
DIL_GROUPED = 4
N_BUCKETS = 32
MAX_DISTANCE = 2048

ADAM_LR = 0.001
ADAM_B1 = 0.9
ADAM_B2 = 0.999
ADAM_EPS = 1e-08
ADAM_WD = 0.01
ADAM_STEP = 10

V7X_VMEM_BYTES = 64 * 2**20
VMEM_RESERVE = 10 * 2**20
TOKEN_TILE = 512


def _nbytes(shape, dtype):
    return math.prod(shape) * jnp.dtype(dtype).itemsize


def _params(semantics, blocks, extra=0):
    need = 2 * sum(_nbytes(s, d) for s, d in blocks) + extra + VMEM_RESERVE
    return pltpu.CompilerParams(dimension_semantics=semantics,
                                vmem_limit_bytes=int(min(need, V7X_VMEM_BYTES - VMEM_RESERVE)))


def _pcall(body, out_shape, **kw):
    call = pl.pallas_call(body, out_shape=jax.tree.map(lambda s: pltpu.HBM(s.shape, s.dtype), out_shape), **kw)
    return lambda *args: call(*[pltpu.with_memory_space_constraint(a, pltpu.HBM) for a in args])


def _dot_nn(a, b):
    return lax.dot_general(a, b, (((1,), (0,)), ((), ())), preferred_element_type=F32)


def _dot_nt(a, b):
    return lax.dot_general(a, b, (((1,), (1,)), ((), ())), preferred_element_type=F32)


def _dot_tn(a, b):
    return lax.dot_general(a, b, (((0,), (0,)), ((), ())), preferred_element_type=F32)


_DOTS = {"nn": _dot_nn, "nt": _dot_nt, "tn": _dot_tn}


def _rstd(v):
    return lax.rsqrt(jnp.mean(v * v, axis=-1, keepdims=True) + EPS)


def _rms_bwd(v, r, t):
    return r * t - v * (r * r * r) * jnp.mean(t * v, axis=-1, keepdims=True)


_TOKEN_SPEC = pl.BlockSpec((8, 128), lambda *_: (0, 0))


def _mm(pairs, mode, out_dtype, tm, tn, name, out_perm=1, after=None, b_rows=None):
    a0, b0 = pairs[0]
    m_dim = a0.shape[1] if mode == "tn" else a0.shape[0]
    n_dim = b0.shape[0] if mode == "nt" else b0.shape[1]
    tm, tn = min(tm, m_dim // out_perm), min(tn, n_dim)
    assert m_dim % tm == 0 and n_dim % tn == 0, (name, m_dim, n_dim, tm, tn)
    dot = _DOTS[mode]
    npairs = len(pairs)

    def body(*refs):
        acc = None
        for p in range(npairs):
            d = dot(refs[2 * p][...].astype(BF16), refs[2 * p + 1][...].astype(BF16))
            acc = d if acc is None else acc + d
        refs[-1][...] = acc.astype(out_dtype)

    in_specs, blocks, flat = [], [], []
    for n_pair, (a, b) in enumerate(pairs):
        if mode == "nn":
            k = a.shape[1]
            first_block = 0 if b_rows is None else b_rows[n_pair] // k
            sa, sb = ((tm, k), lambda i, j: (i, 0)), ((k, tn), lambda i, j, o=first_block: (o, j))
        elif mode == "nt":
            k = a.shape[1]
            sa, sb = ((tm, k), lambda i, j: (i, 0)), ((tn, k), lambda i, j: (j, 0))
        else:
            k = a.shape[0]
            sa, sb = ((k, tm), lambda i, j: (0, i)), ((k, tn), lambda i, j: (0, j))
        in_specs += [pl.BlockSpec(*sa), pl.BlockSpec(*sb)]
        blocks += [(sa[0], a.dtype), (sb[0], b.dtype)]
        flat += [a, b]
    if after is not None:
        in_specs.append(_TOKEN_SPEC)
        flat.append(after)
    if out_perm == 1:
        out_shape = (m_dim, n_dim)
        out_spec = pl.BlockSpec((tm, tn), lambda i, j: (i, j))
    else:
        rows = m_dim // out_perm
        assert tn == n_dim and rows % tm == 0, (name, rows, tm)
        nb = rows // tm
        out_shape = (rows, out_perm * n_dim)
        out_spec = pl.BlockSpec((tm, n_dim), lambda i, j: (i % nb, i // nb))
    blocks.append(((tm, tn), out_dtype))
    res = _pcall(
        body, out_shape=jax.ShapeDtypeStruct(out_shape, out_dtype), grid=(m_dim // tm, n_dim // tn),
        in_specs=in_specs, out_specs=out_spec, name=name,
        compiler_params=_params(("parallel", "parallel"), blocks, extra=2 * tm * tn * 4),
    )(*flat)
    return res.reshape(m_dim, n_dim)


def _prenorm_mm(x, pre_g, scale, shift, w, w_mode, out_dtype, tn, name, perm=1, w_rows=None):
    s_dim, d_dim = x.shape
    n_dim = w.shape[0] if w_mode == "nt" else w.shape[1]
    w_first = 0
    if w_rows is not None:
        w_first, n_dim = w_rows
    rows = s_dim // perm
    tm = min(TOKEN_TILE, rows)
    nb = rows // tm
    tn = min(tn, n_dim)
    assert n_dim % tn == 0 and w_first % tn == 0
    w_block0 = w_first // tn
    dot = _DOTS[w_mode]

    def body(x_ref, g_ref, sc_ref, sh_ref, w_ref, hn_ref, o_ref):
        @pl.when(pl.program_id(1) == 0)
        def _():
            xf = x_ref[...]
            hn = (xf * _rstd(xf) * g_ref[...]) * (1.0 + sc_ref[...]) + sh_ref[...]
            hn_ref[...] = hn.astype(BF16)

        o_ref[...] = dot(hn_ref[...], w_ref[...]).astype(out_dtype)

    vec = pl.BlockSpec((1, d_dim), lambda i, j: (0, 0))
    w_block = (tn, d_dim) if w_mode == "nt" else (d_dim, tn)
    w_spec = pl.BlockSpec(w_block, (lambda i, j: (w_block0 + j, 0)) if w_mode == "nt" else (lambda i, j: (0, j)))
    hn, out = _pcall(
        body,
        out_shape=(jax.ShapeDtypeStruct((s_dim, d_dim), BF16), jax.ShapeDtypeStruct((s_dim, n_dim), out_dtype)),
        grid=(s_dim // tm, n_dim // tn),
        in_specs=[pl.BlockSpec((tm, d_dim), lambda i, j: (i % nb, i // nb)), vec, vec, vec, w_spec],
        out_specs=(pl.BlockSpec((tm, d_dim), lambda i, j: (i, 0)), pl.BlockSpec((tm, tn), lambda i, j: (i, j))),
        name=name,
        compiler_params=_params(("parallel", "arbitrary"),
                                [((tm, d_dim), F32), (w_block, BF16), ((tm, d_dim), BF16), ((tm, tn), out_dtype)],
                                extra=3 * tm * d_dim * 4 + tm * tn * 4),
    )(x.reshape(rows, perm * d_dim), pre_g, scale, shift, w)
    return hn, out


def _ffn_up(x, pre_g, scale, shift, wg_t, wu_t, name):
    s_dim, d_dim = x.shape
    f_dim = wg_t.shape[0]
    tm, tn = TOKEN_TILE, f_dim // 2

    def body(x_ref, g_ref, sc_ref, sh_ref, wg_ref, wu_ref, hn_ref, go_ref, uo_ref, a_ref):
        @pl.when(pl.program_id(1) == 0)
        def _():
            xf = x_ref[...]
            hn = (xf * _rstd(xf) * g_ref[...]) * (1.0 + sc_ref[...]) + sh_ref[...]
            hn_ref[...] = hn.astype(BF16)

        hn = hn_ref[...]
        g = _dot_nt(hn, wg_ref[...])
        u = _dot_nt(hn, wu_ref[...])
        go_ref[...] = g.astype(BF16)
        uo_ref[...] = u.astype(BF16)
        a_ref[...] = (g * jax.nn.sigmoid(g) * u).astype(BF16)

    vec = pl.BlockSpec((1, d_dim), lambda i, j: (0, 0))
    w_spec = pl.BlockSpec((tn, d_dim), lambda i, j: (j, 0))
    act = pl.BlockSpec((tm, tn), lambda i, j: (i, j))
    act_shape = jax.ShapeDtypeStruct((s_dim, f_dim), BF16)
    return _pcall(
        body,
        out_shape=(jax.ShapeDtypeStruct((s_dim, d_dim), BF16), act_shape, act_shape, act_shape),
        grid=(s_dim // tm, f_dim // tn),
        in_specs=[pl.BlockSpec((tm, d_dim), lambda i, j: (i, 0)), vec, vec, vec, w_spec, w_spec],
        out_specs=(pl.BlockSpec((tm, d_dim), lambda i, j: (i, 0)), act, act, act),
        name=name,
        compiler_params=_params(("parallel", "arbitrary"),
                                [((tm, d_dim), F32), ((tn, d_dim), BF16), ((tn, d_dim), BF16), ((tm, d_dim), BF16)]
                                + 3 * [((tm, tn), BF16)], extra=3 * tm * d_dim * 4 + 4 * tm * tn * 4),
    )(x, pre_g, scale, shift, wg_t, wu_t)


def _mm_post(a, w, x, post_g, gate, res_w, name):
    s_dim, k_dim = a.shape
    d_dim = w.shape[1]
    tm = TOKEN_TILE

    def body(a_ref, w_ref, x_ref, pg_ref, gt_ref, xo_ref, f_ref):
        f = _dot_nn(a_ref[...], w_ref[...])
        y = f * _rstd(f) * pg_ref[...]
        f_ref[...] = f
        xo_ref[...] = x_ref[...] + (res_w * gt_ref[...]) * y

    vec = pl.BlockSpec((1, d_dim), lambda i: (0, 0))
    row = pl.BlockSpec((tm, d_dim), lambda i: (i, 0))
    out = jax.ShapeDtypeStruct((s_dim, d_dim), F32)
    return _pcall(
        body, out_shape=(out, out), grid=(s_dim // tm,),
        in_specs=[pl.BlockSpec((tm, k_dim), lambda i: (i, 0)), pl.BlockSpec((k_dim, d_dim), lambda i: (0, 0)), row, vec, vec],
        out_specs=(row, row), name=name,
        compiler_params=_params(("parallel",), [((tm, k_dim), BF16), ((k_dim, d_dim), BF16)] + 3 * [((tm, d_dim), F32)],
                                extra=3 * tm * d_dim * 4),
    )(a, w, x, post_g, gate)


def _post_bwd(dx_out, f, post_g, gate, res_w, name):
    s_dim, d_dim = f.shape
    tm = TOKEN_TILE

    def body(dx_ref, f_ref, pg_ref, gt_ref, df_ref, dgate_ref, dpost_ref):
        @pl.when(pl.program_id(0) == 0)
        def _():
            dgate_ref[...] = jnp.zeros_like(dgate_ref)
            dpost_ref[...] = jnp.zeros_like(dpost_ref)

        dx, fv = dx_ref[...], f_ref[...]
        r = _rstd(fv)
        fr = fv * r
        dgate_ref[...] += res_w * jnp.sum(dx * (fr * pg_ref[...]), axis=0, keepdims=True)
        dy = (res_w * gt_ref[...]) * dx
        dpost_ref[...] += jnp.sum(dy * fr, axis=0, keepdims=True)
        df_ref[...] = _rms_bwd(fv, r, dy * pg_ref[...]).astype(BF16)

    vec = pl.BlockSpec((1, d_dim), lambda i: (0, 0))
    row = pl.BlockSpec((tm, d_dim), lambda i: (i, 0))
    vshape = jax.ShapeDtypeStruct((1, d_dim), F32)
    return _pcall(
        body, out_shape=(jax.ShapeDtypeStruct((s_dim, d_dim), BF16), vshape, vshape), grid=(s_dim // tm,),
        in_specs=[row, row, vec, vec], out_specs=(row, vec, vec), name=name,
        compiler_params=_params(("arbitrary",), 3 * [((tm, d_dim), F32)], extra=6 * tm * d_dim * 4),
    )(dx_out, f, post_g, gate)


def _prenorm_bwd(dx_out, dhns, x, pre_g, scale, name):
    s_dim, d_dim = x.shape
    tm = TOKEN_TILE
    n_in = len(dhns)

    def body(*refs):
        dx_ref, x_ref, pg_ref, sc_ref = refs[n_in + 0], refs[n_in + 1], refs[n_in + 2], refs[n_in + 3]
        dxo_ref, dsh_ref, dsc_ref, dpg_ref = refs[n_in + 4:]

        @pl.when(pl.program_id(0) == 0)
        def _():
            dsh_ref[...] = jnp.zeros_like(dsh_ref)
            dsc_ref[...] = jnp.zeros_like(dsc_ref)
            dpg_ref[...] = jnp.zeros_like(dpg_ref)

        dhn = refs[0][...]
        for k in range(1, n_in):
            dhn = dhn + refs[k][...]
        xv = x_ref[...]
        r = _rstd(xv)
        xr = xv * r
        dsh_ref[...] += jnp.sum(dhn, axis=0, keepdims=True)
        dsc_ref[...] += jnp.sum(dhn * (xr * pg_ref[...]), axis=0, keepdims=True)
        dn = dhn * (1.0 + sc_ref[...])
        dpg_ref[...] += jnp.sum(dn * xr, axis=0, keepdims=True)
        dxo_ref[...] = dx_ref[...] + _rms_bwd(xv, r, dn * pg_ref[...])

    vec = pl.BlockSpec((1, d_dim), lambda i: (0, 0))
    row = pl.BlockSpec((tm, d_dim), lambda i: (i, 0))
    vshape = jax.ShapeDtypeStruct((1, d_dim), F32)
    return _pcall(
        body, out_shape=(jax.ShapeDtypeStruct((s_dim, d_dim), F32), vshape, vshape, vshape), grid=(s_dim // tm,),
        in_specs=n_in * [row] + [row, row, vec, vec], out_specs=(row, vec, vec, vec), name=name,
        compiler_params=_params(("arbitrary",), (n_in + 3) * [((tm, d_dim), F32)], extra=6 * tm * d_dim * 4),
    )(*dhns, dx_out, x, pre_g, scale)


def _ffn_dgu(df, wd, g, u, name, after=None):
    s_dim, d_dim = df.shape
    f_dim = wd.shape[0]
    tm, tn = TOKEN_TILE, f_dim // 2

    def body(df_ref, wd_ref, g_ref, u_ref, *rest):
        dg_ref, du_ref = rest[-2:]
        da = _dot_nt(df_ref[...], wd_ref[...])
        gv, uv = g_ref[...].astype(F32), u_ref[...].astype(F32)
        sg = jax.nn.sigmoid(gv)
        du_ref[...] = (da * (gv * sg)).astype(BF16)
        dg_ref[...] = (da * uv * (sg * (1.0 + gv * (1.0 - sg)))).astype(BF16)

    act = pl.BlockSpec((tm, tn), lambda i, j: (i, j))
    act_shape = jax.ShapeDtypeStruct((s_dim, f_dim), BF16)
    token = [] if after is None else [after]
    return _pcall(
        body, out_shape=(act_shape, act_shape), grid=(s_dim // tm, f_dim // tn),
        in_specs=[pl.BlockSpec((tm, d_dim), lambda i, j: (i, 0)), pl.BlockSpec((tn, d_dim), lambda i, j: (j, 0)), act, act]
        + len(token) * [_TOKEN_SPEC],
        out_specs=(act, act), name=name,
        compiler_params=_params(("parallel", "parallel"), [((tm, d_dim), BF16), ((tn, d_dim), BF16)] + 4 * [((tm, tn), BF16)],
                                extra=6 * tm * tn * 4),
    )(df, wd, g, u, *token)


def _ffn_fwd_fused(x, pre_g, scale, shift, post_g, gate, wg_t, wu_t, wd, name):
    s_dim, d_dim = x.shape
    f_dim = wd.shape[0]
    tm, chunks = 256, 2
    cw = f_dim // chunks

    def body(x_ref, prg_ref, sc_ref, sh_ref, pg_ref, gt_ref, wg_ref, wu_ref, wd_ref, hn_ref, go_ref, uo_ref, a_ref, xo_ref, f_ref):
        xf = x_ref[...]
        hn = ((xf * _rstd(xf) * prg_ref[...]) * (1.0 + sc_ref[...]) + sh_ref[...]).astype(BF16)
        hn_ref[...] = hn
        f = None
        ahead = (_dot_nt(hn, wg_ref[0:cw, :]), _dot_nt(hn, wu_ref[0:cw, :]))
        for c in range(chunks):
            g, u = ahead
            if c + 1 < chunks:
                nxt = slice((c + 1) * cw, (c + 2) * cw)
                ahead = (_dot_nt(hn, wg_ref[nxt, :]), _dot_nt(hn, wu_ref[nxt, :]))
            cols = slice(c * cw, (c + 1) * cw)
            go_ref[:, cols] = g.astype(BF16)
            uo_ref[:, cols] = u.astype(BF16)
            a = (g * jax.nn.sigmoid(g) * u).astype(BF16)
            a_ref[:, cols] = a
            part = _dot_nn(a, wd_ref[cols, :])
            f = part if f is None else f + part
        f_ref[...] = f
        xo_ref[...] = xf + (FFN_RES * gt_ref[...]) * (f * _rstd(f) * pg_ref[...])

    vec = pl.BlockSpec((1, d_dim), lambda i: (0, 0))
    row = pl.BlockSpec((tm, d_dim), lambda i: (i, 0))
    act = pl.BlockSpec((tm, f_dim), lambda i: (i, 0))
    weight = pl.BlockSpec((f_dim, d_dim), lambda i: (0, 0), pipeline_mode=pl.Buffered(1))
    act_shape = jax.ShapeDtypeStruct((s_dim, f_dim), BF16)
    res_shape = jax.ShapeDtypeStruct((s_dim, d_dim), F32)
    need = (3 * f_dim * d_dim * 2 + 2 * tm * d_dim * 4 + 2 * (tm * d_dim * 2 + 3 * tm * f_dim * 2 + 2 * tm * d_dim * 4)
            + 8 * tm * cw * 4 + 4 * tm * d_dim * 4)
    return _pcall(
        body, out_shape=(jax.ShapeDtypeStruct((s_dim, d_dim), BF16), act_shape, act_shape, act_shape, res_shape, res_shape),
        grid=(s_dim // tm,), in_specs=[row, vec, vec, vec, vec, vec, weight, weight, weight],
        out_specs=(row, act, act, act, row, row), name=name,
        compiler_params=pltpu.CompilerParams(dimension_semantics=("parallel",),
                                             vmem_limit_bytes=int(min(need + VMEM_RESERVE, V7X_VMEM_BYTES - VMEM_RESERVE))),
    )(x, pre_g, scale, shift, post_g, gate, wg_t, wu_t, wd)


def _ffn_bwd_fused(dx_out, saved, pre_g, post_g, scale, gate, wg_t, wu_t, wd, name):
    x, _, g, u, _, f = saved
    s_dim, d_dim = x.shape
    f_dim = wd.shape[0]
    tm, chunks = 256, 2
    cw = f_dim // chunks

    def body(dx_ref, f_ref, g_ref, u_ref, x_ref, pg_ref, gt_ref, prg_ref, sc_ref, wd_ref, wg_ref, wu_ref,
             df_ref, dg_ref, du_ref, dxo_ref, dgate_ref, dpost_ref, dsh_ref, dsc_ref, dpg_ref):
        @pl.when(pl.program_id(0) == 0)
        def _():
            for acc in (dgate_ref, dpost_ref, dsh_ref, dsc_ref, dpg_ref):
                acc[...] = jnp.zeros_like(acc)

        dx, fv = dx_ref[...], f_ref[...]
        r = _rstd(fv)
        fr = fv * r
        dgate_ref[...] += FFN_RES * jnp.sum(dx * (fr * pg_ref[...]), axis=0, keepdims=True)
        dy = (FFN_RES * gt_ref[...]) * dx
        dpost_ref[...] += jnp.sum(dy * fr, axis=0, keepdims=True)
        df = _rms_bwd(fv, r, dy * pg_ref[...]).astype(BF16)
        df_ref[...] = df
        dhn = None
        ahead = _dot_nt(df, wd_ref[0:cw, :])
        for c in range(chunks):
            da = ahead
            if c + 1 < chunks:
                ahead = _dot_nt(df, wd_ref[(c + 1) * cw:(c + 2) * cw, :])
            cols = slice(c * cw, (c + 1) * cw)
            gv, uv = g_ref[:, cols].astype(F32), u_ref[:, cols].astype(F32)
            sg = jax.nn.sigmoid(gv)
            du = (da * (gv * sg)).astype(BF16)
            dg = (da * uv * (sg * (1.0 + gv * (1.0 - sg)))).astype(BF16)
            dg_ref[:, cols] = dg
            du_ref[:, cols] = du
            part = _dot_nn(dg, wg_ref[cols, :]) + _dot_nn(du, wu_ref[cols, :])
            dhn = part if dhn is None else dhn + part
        xv = x_ref[...]
        rx = _rstd(xv)
        xr = xv * rx
        dsh_ref[...] += jnp.sum(dhn, axis=0, keepdims=True)
        dsc_ref[...] += jnp.sum(dhn * (xr * prg_ref[...]), axis=0, keepdims=True)
        dn = dhn * (1.0 + sc_ref[...])
        dpg_ref[...] += jnp.sum(dn * xr, axis=0, keepdims=True)
        dxo_ref[...] = dx + _rms_bwd(xv, rx, dn * prg_ref[...])

    vec = pl.BlockSpec((1, d_dim), lambda i: (0, 0))
    row = pl.BlockSpec((tm, d_dim), lambda i: (i, 0))
    act = pl.BlockSpec((tm, f_dim), lambda i: (i, 0))
    weight = pl.BlockSpec((f_dim, d_dim), lambda i: (0, 0), pipeline_mode=pl.Buffered(1))
    vshape = jax.ShapeDtypeStruct((1, d_dim), F32)
    act_shape = jax.ShapeDtypeStruct((s_dim, f_dim), BF16)
    need = (3 * f_dim * d_dim * 2 + 2 * (3 * tm * d_dim * 4 + 2 * tm * f_dim * 2) + 2 * (tm * d_dim * 2 + 2 * tm * f_dim * 2 + tm * d_dim * 4)
            + 6 * tm * cw * 4 + 6 * tm * d_dim * 4)
    return _pcall(
        body, out_shape=(jax.ShapeDtypeStruct((s_dim, d_dim), BF16), act_shape, act_shape, jax.ShapeDtypeStruct((s_dim, d_dim), F32),
                         vshape, vshape, vshape, vshape, vshape),
        grid=(s_dim // tm,), in_specs=[row, row, act, act, row, vec, vec, vec, vec, weight, weight, weight],
        out_specs=(row, act, act, row, vec, vec, vec, vec, vec), name=name,
        compiler_params=pltpu.CompilerParams(dimension_semantics=("arbitrary",),
                                             vmem_limit_bytes=int(min(need + VMEM_RESERVE, V7X_VMEM_BYTES - VMEM_RESERVE))),
    )(dx_out, f, g, u, x, post_g, gate, pre_g, scale, wd, wg_t, wu_t)


def _rope_tables(zero=0.0):
    half = QK_ROPE // 2
    freqs = ROPE_THETA ** (-jnp.arange(half, dtype=F32) / half)
    ang = (jnp.arange(SEQ, dtype=F32)[:, None] + zero) * freqs[None, :]
    cos, sin = jnp.cos(ang), jnp.sin(ang)
    ones = jnp.ones((SEQ, QK_NOPE), F32)
    zeros = jnp.zeros((SEQ, QK_NOPE), F32)
    pad1 = jnp.ones((SEQ, HEAD_PAD - QK_NOPE - QK_ROPE), F32)
    pad0 = jnp.zeros((SEQ, HEAD_PAD - QK_NOPE - QK_ROPE), F32)
    zh = jnp.zeros((SEQ, half), F32)
    c = jnp.concatenate([ones, cos, cos, pad1], axis=1)
    s1 = jnp.concatenate([zeros, -sin, zh, pad0], axis=1)
    s2 = jnp.concatenate([zeros, zh, sin, pad0], axis=1)
    return c, s1, s2


def _rope(v, c, s1, s2):
    half = QK_ROPE // 2
    return v * c + pltpu.roll(v, HEAD_PAD - half, 1) * s1 + pltpu.roll(v, half, 1) * s2


def _rope_t(dv, c, s1, s2):
    half = QK_ROPE // 2
    return dv * c + pltpu.roll(dv * s1, half, 1) + pltpu.roll(dv * s2, HEAD_PAD - half, 1)


def _mla_qkv(lat, q_norm, kv_norm, wq_t, wkv_t, rope, name):
    s_dim = lat.shape[0]
    width = MLA_HEADS * HEAD_PAD
    tm = 256

    def body(lat_ref, qg_ref, kg_ref, wq_ref, wkv_ref, c_ref, s1_ref, s2_ref, q_ref, k_ref, v_ref, qn_ref, kvn_ref):
        cq = lat_ref[:, :Q_LORA]
        ckv = lat_ref[:, Q_LORA:Q_LORA + KV_LORA]
        kr = lat_ref[:, Q_LORA + KV_LORA:]
        c, s1, s2 = c_ref[...], s1_ref[...], s2_ref[...]
        qn = (cq * _rstd(cq) * qg_ref[...]).astype(BF16)
        kvn = (ckv * _rstd(ckv) * kg_ref[...]).astype(BF16)
        qn_ref[...] = qn
        kvn_ref[...] = kvn
        q = _dot_nt(qn, wq_ref[...])
        kv = _dot_nt(kvn, wkv_ref[...])
        krr = _rope(kr, c, s1, s2)
        low = lax.broadcasted_iota(jnp.int32, (tm, HEAD_PAD), 1) < QK_NOPE
        for h in range(MLA_HEADS):
            sl = slice(h * HEAD_PAD, (h + 1) * HEAD_PAD)
            q_ref[:, sl] = _rope(q[:, sl], c, s1, s2).astype(BF16)
            kvh = kv[:, sl]
            k_ref[:, sl] = (jnp.where(low, kvh, 0.0) + krr).astype(BF16)
            v_ref[:, sl] = jnp.where(low, 0.0, kvh).astype(BF16)

    row = lambda n: pl.BlockSpec((tm, n), lambda i: (i, 0))
    full = lambda a: pl.BlockSpec(a.shape, lambda i: (0, 0))
    wide = jax.ShapeDtypeStruct((s_dim, width), BF16)
    return _pcall(
        body,
        out_shape=(wide, wide, wide, jax.ShapeDtypeStruct((s_dim, Q_LORA), BF16), jax.ShapeDtypeStruct((s_dim, KV_LORA), BF16)),
        grid=(s_dim // tm,),
        in_specs=[row(LAT_PAD), full(q_norm), full(kv_norm), full(wq_t), full(wkv_t), row(HEAD_PAD), row(HEAD_PAD), row(HEAD_PAD)],
        out_specs=(row(width), row(width), row(width), row(Q_LORA), row(KV_LORA)), name=name,
        compiler_params=_params(("parallel",), [((tm, LAT_PAD), F32), (wq_t.shape, BF16), (wkv_t.shape, BF16)]
                                + 3 * [((tm, width), BF16)], extra=4 * tm * width * 4),
    )(lat, q_norm, kv_norm, wq_t, wkv_t, *rope)


def _mla_scores(q, k_ref, t, tq):
    lo = t * tq
    own = slice(lo, lo + tq)
    scores = [(_dot_nt(q, k_ref[own, :]), own)]
    if t > 0:
        scores.append((_dot_nt(q, k_ref[0:lo, :]), slice(0, lo)))
    return scores


def _mla_softmax(scores):
    s_own = scores[0][0] * MLA_SCALE
    rows = lax.broadcasted_iota(jnp.int32, s_own.shape, 0)
    cols = lax.broadcasted_iota(jnp.int32, s_own.shape, 1)
    s_own = jnp.where(cols <= rows, s_own, -jnp.inf)
    mx = jnp.max(s_own, axis=-1, keepdims=True)
    if len(scores) == 1:
        e_own = jnp.exp(s_own - mx)
        return [(e_own * (1.0 / jnp.sum(e_own, axis=-1, keepdims=True)), scores[0][1])]
    s_pre = scores[1][0] * MLA_SCALE
    mx = jnp.maximum(mx, jnp.max(s_pre, axis=-1, keepdims=True))
    e_own, e_pre = jnp.exp(s_own - mx), jnp.exp(s_pre - mx)
    inv = 1.0 / (jnp.sum(e_own, axis=-1, keepdims=True) + jnp.sum(e_pre, axis=-1, keepdims=True))
    return [(e_pre * inv, scores[1][1]), (e_own * inv, scores[0][1])]


def _mla_attn_fwd(q, k, v, name):
    s_dim = q.shape[0]
    tq = 512

    def body(q_ref, k_ref, v_ref, o_ref):
        n_tiles = s_dim // tq
        tile_of = lambda t: slice(t * tq, (t + 1) * tq)
        def weighted_values(t, probs):
            o = None
            for p, keys in probs:
                part = _dot_nn(p, v_ref[keys, :])
                o = part if o is None else o + part
            o_ref[tile_of(t), :] = o.astype(BF16)

        scores = _mla_scores(q_ref[tile_of(0), :], k_ref, 0, tq)
        probs = None
        for t in range(n_tiles):
            ahead = _mla_scores(q_ref[tile_of(t + 1), :], k_ref, t + 1, tq) if t + 1 < n_tiles else None
            if probs is not None:
                weighted_values(t - 1, probs)
            probs = [(p.astype(BF16), keys) for p, keys in _mla_softmax(scores)]
            scores = ahead
        weighted_values(n_tiles - 1, probs)

    head = pl.BlockSpec((s_dim, HEAD_PAD), lambda h: (0, h))
    return _pcall(
        body, out_shape=jax.ShapeDtypeStruct(q.shape, BF16), grid=(MLA_HEADS,),
        in_specs=[head, head, head], out_specs=head, name=name,
        compiler_params=_params(("parallel",), 4 * [((s_dim, HEAD_PAD), BF16)], extra=4 * tq * s_dim * 4),
    )(q, k, v)


def _mla_attn_bwd(q, k, v, d_o, name):
    s_dim = q.shape[0]
    tq = 512

    def body(q_ref, k_ref, v_ref, do_ref, dq_ref, dk_ref, dv_ref):
        dk_ref[...] = jnp.zeros_like(dk_ref)
        dv_ref[...] = jnp.zeros_like(dv_ref)
        n_tiles = s_dim // tq
        tile_of = lambda t: slice(t * tq, (t + 1) * tq)

        def products(t):
            scores = _mla_scores(q_ref[tile_of(t), :], k_ref, t, tq)
            dot = do_ref[tile_of(t), :].astype(BF16)
            return scores, [_dot_nt(dot, v_ref[keys, :]) for _, keys in scores]

        def gradients_of_scores(scores, dps):
            probs = _mla_softmax(scores)
            dp_of = {(keys.start, keys.stop): dp for (_, keys), dp in zip(scores, dps)}
            terms = [(p, keys, dp_of[keys.start, keys.stop]) for p, keys in probs]
            row = None
            for p, _, dp in terms:
                part = jnp.sum(p * dp, axis=-1, keepdims=True)
                row = part if row is None else row + part
            return [((p * (dp - row) * MLA_SCALE).astype(BF16), p.astype(BF16), keys) for p, keys, dp in terms]

        def accumulate(t, terms):
            qt = q_ref[tile_of(t), :]
            dot = do_ref[tile_of(t), :].astype(BF16)
            dq = None
            for dsb, pb, keys in terms:
                part = _dot_nn(dsb, k_ref[keys, :])
                dq = part if dq is None else dq + part
                dk_ref[keys, :] += _dot_tn(dsb, qt)
                dv_ref[keys, :] += _dot_tn(pb, dot)
            dq_ref[tile_of(t), :] = dq

        ready = products(0)
        terms = None
        for t in range(n_tiles):
            ahead = products(t + 1) if t + 1 < n_tiles else None
            if terms is not None:
                accumulate(t - 1, terms)
            terms = gradients_of_scores(*ready)
            ready = ahead
        accumulate(n_tiles - 1, terms)

    head = pl.BlockSpec((s_dim, HEAD_PAD), lambda h: (0, h))
    out = jax.ShapeDtypeStruct(q.shape, F32)
    return _pcall(
        body, out_shape=(out, out, out), grid=(MLA_HEADS,),
        in_specs=[head, head, head, head], out_specs=(head, head, head), name=name,
        compiler_params=_params(("parallel",), 3 * [((s_dim, HEAD_PAD), BF16)] + 4 * [((s_dim, HEAD_PAD), F32)],
                                extra=6 * tq * s_dim * 4),
    )(q, k, v, d_o)


def _mla_qkv_bwd(dq, dk, dv, lat, q_norm, kv_norm, wq_t, wkv_t, rope, name):
    s_dim = lat.shape[0]
    width = MLA_HEADS * HEAD_PAD
    tm = 256

    def body(dq_ref, dk_ref, dv_ref, lat_ref, qg_ref, kg_ref, wq_ref, wkv_ref, c_ref, s1_ref, s2_ref,
             dqp_ref, dkv_ref, dlat_ref, dqg_ref, dkg_ref):
        @pl.when(pl.program_id(0) == 0)
        def _():
            dqg_ref[...] = jnp.zeros_like(dqg_ref)
            dkg_ref[...] = jnp.zeros_like(dkg_ref)

        c, s1, s2 = c_ref[...], s1_ref[...], s2_ref[...]
        lane = lax.broadcasted_iota(jnp.int32, (tm, HEAD_PAD), 1)
        low = lane < QK_NOPE
        rot = (lane >= QK_NOPE) & (lane < QK_NOPE + QK_ROPE)
        dkrr = jnp.zeros((tm, HEAD_PAD), F32)
        for h in range(MLA_HEADS):
            sl = slice(h * HEAD_PAD, (h + 1) * HEAD_PAD)
            dqp_ref[:, sl] = _rope_t(dq_ref[:, sl], c, s1, s2).astype(BF16)
            dkh = dk_ref[:, sl]
            dkv_ref[:, sl] = jnp.where(low, dkh, dv_ref[:, sl]).astype(BF16)
            dkrr = dkrr + jnp.where(rot, dkh, 0.0)
        dqn = _dot_nn(dqp_ref[...], wq_ref[...])
        dkvn = _dot_nn(dkv_ref[...], wkv_ref[...])
        cq = lat_ref[:, :Q_LORA]
        ckv = lat_ref[:, Q_LORA:Q_LORA + KV_LORA]
        rq, rkv = _rstd(cq), _rstd(ckv)
        dqg_ref[...] += jnp.sum(dqn * cq * rq, axis=0, keepdims=True)
        dkg_ref[...] += jnp.sum(dkvn * ckv * rkv, axis=0, keepdims=True)
        dlat_ref[:, :Q_LORA] = _rms_bwd(cq, rq, dqn * qg_ref[...])
        dlat_ref[:, Q_LORA:Q_LORA + KV_LORA] = _rms_bwd(ckv, rkv, dkvn * kg_ref[...])
        dlat_ref[:, Q_LORA + KV_LORA:] = _rope_t(dkrr, c, s1, s2)

    row = lambda n: pl.BlockSpec((tm, n), lambda i: (i, 0))
    full = lambda a: pl.BlockSpec(a.shape, lambda i: (0, 0))
    wide = jax.ShapeDtypeStruct((s_dim, width), BF16)
    return _pcall(
        body,
        out_shape=(wide, wide, jax.ShapeDtypeStruct((s_dim, LAT_PAD), F32),
                   jax.ShapeDtypeStruct(q_norm.shape, F32), jax.ShapeDtypeStruct(kv_norm.shape, F32)),
        grid=(s_dim // tm,),
        in_specs=[row(width), row(width), row(width), row(LAT_PAD), full(q_norm), full(kv_norm), full(wq_t), full(wkv_t),
                  row(HEAD_PAD), row(HEAD_PAD), row(HEAD_PAD)],
        out_specs=(row(width), row(width), row(LAT_PAD), full(q_norm), full(kv_norm)), name=name,
        compiler_params=_params(("arbitrary",), 3 * [((tm, width), F32)] + [((tm, LAT_PAD), F32), (wq_t.shape, BF16),
                                                                           (wkv_t.shape, BF16)] + 2 * [((tm, width), BF16)],
                                extra=2 * tm * width * 4),
    )(dq, dk, dv, lat, q_norm, kv_norm, wq_t, wkv_t, *rope)


def _t5_bucket(dist):
    max_exact = N_BUCKETS // 2
    d = jnp.maximum(dist, 1).astype(F32)
    large = max_exact + (jnp.log(d / max_exact) / math.log(MAX_DISTANCE / max_exact)
                         * (N_BUCKETS - max_exact)).astype(jnp.int32)
    large = jnp.minimum(large, N_BUCKETS - 1)
    return jnp.where(dist < max_exact, dist, large)


def _dil_buckets(dilation):
    iq = jnp.arange(DIL_BLOCK)[:, None]
    ik = jnp.arange(2 * DIL_BLOCK)[None, :]
    return _t5_bucket(jnp.maximum(DIL_BLOCK + iq - ik, 0) * dilation)


def _dil_logits(qh, kb, bias_h, first, span):
    if first:
        s = _dot_nt(qh, kb) * DIL_SCALE + bias_h[:, DIL_BLOCK:]
        rel = lax.broadcasted_iota(jnp.int32, s.shape, 0) - lax.broadcasted_iota(jnp.int32, s.shape, 1)
    else:
        s = _dot_nt(qh, kb) * DIL_SCALE + bias_h
        rel = DIL_BLOCK + lax.broadcasted_iota(jnp.int32, s.shape, 0) - lax.broadcasted_iota(jnp.int32, s.shape, 1)
    return jnp.where((rel >= 0) & (rel <= span), s, -jnp.inf)


def _dil_blocks(s_dim, dilation):
    rows = s_dim // dilation
    for r in range(dilation):
        for n in range(rows // DIL_BLOCK):
            lo = r * rows + n * DIL_BLOCK
            keys = slice(lo, lo + DIL_BLOCK) if n == 0 else slice(lo - DIL_BLOCK, lo + DIL_BLOCK)
            start = r + n * DIL_BLOCK * dilation
            tokens = slice(start, start + DIL_BLOCK) if dilation == 1 else pl.ds(start, DIL_BLOCK, stride=dilation)
            yield n == 0, slice(lo, lo + DIL_BLOCK), keys, tokens


def _dil_views(s_dim):
    col = lambda which: pl.BlockSpec((s_dim, HEAD_PAD), lambda p: (0, which * DIL_PAIRS + p))
    nat = pl.BlockSpec((s_dim, HEAD_PAD), lambda p: (0, p))
    bias = pl.BlockSpec((2, DIL_BLOCK, 2 * DIL_BLOCK), lambda p: (p, 0, 0))
    return col, nat, bias


def _dil_attn_fwd(qkv, bias, dilation, span, name):
    s_dim = qkv.shape[0]
    d_dim = DIL_HEADS * DIL_HEAD_DIM
    col, nat, bias_spec = _dil_views(s_dim)

    def body(q_ref, k_ref, v_ref, b_ref, o_ref, l_ref):
        lane = lax.broadcasted_iota(jnp.int32, (DIL_BLOCK, HEAD_PAD), 1)
        klane = lax.broadcasted_iota(jnp.int32, (2 * DIL_BLOCK, HEAD_PAD), 1)
        blocks = list(_dil_blocks(s_dim, dilation))
        for g0 in range(0, len(blocks), DIL_GROUPED):
            group = blocks[g0:g0 + DIL_GROUPED]
            logits = [_dil_logits(jnp.where((lane < DIL_HEAD_DIM) == (h == 0), q_ref[blk, :], 0), k_ref[keys, :], b_ref[h],
                                  first, span) for first, blk, keys, _ in group for h in range(2)]
            soft = []
            for lg in logits:
                mx = jnp.max(lg, axis=-1, keepdims=True)
                e = jnp.exp(lg - mx)
                tot = jnp.sum(e, axis=-1, keepdims=True)
                soft.append(((e * (1.0 / tot)).astype(BF16), mx + jnp.log(tot)))
            for i, (_, _, keys, tokens) in enumerate(group):
                vb = v_ref[keys, :]
                o_acc = jnp.zeros((DIL_BLOCK, HEAD_PAD), F32)
                lse_acc = jnp.zeros((DIL_BLOCK, HEAD_PAD), F32)
                for h in range(2):
                    p, lse = soft[2 * i + h]
                    kmine = (klane[:vb.shape[0]] < DIL_HEAD_DIM) == (h == 0)
                    o_acc = o_acc + _dot_nn(p, jnp.where(kmine, vb, 0))
                    lse_acc = jnp.where((lane < DIL_HEAD_DIM) == (h == 0), lse, lse_acc)
                o_ref[tokens, :] = o_acc
                l_ref[tokens, :] = lse_acc

    out = jax.ShapeDtypeStruct((s_dim, d_dim), F32)
    return _pcall(
        body, out_shape=(out, out), grid=(DIL_PAIRS,),
        in_specs=[col(0), col(1), col(2), bias_spec], out_specs=(nat, nat), name=name,
        compiler_params=_params(("parallel",), 3 * [((s_dim, HEAD_PAD), BF16)] + 2 * [((s_dim, HEAD_PAD), F32)]
                                + [((2, DIL_BLOCK, 2 * DIL_BLOCK), F32)], extra=2**21),
    )(qkv, qkv, qkv, bias)


def _dil_mix(lses, outs, name):
    s_dim, d_dim = outs[0].shape
    tm = TOKEN_TILE
    ng = len(outs)

    def body(*refs):
        ls = [refs[g][...] for g in range(ng)]
        mx = ls[0]
        for g in range(1, ng):
            mx = jnp.maximum(mx, ls[g])
        es = [jnp.exp(l - mx) for l in ls]
        tot = es[0]
        for g in range(1, ng):
            tot = tot + es[g]
        o = None
        for g in range(ng):
            al = es[g] / tot
            refs[2 * ng + g][...] = al
            t = al * refs[ng + g][...]
            o = t if o is None else o + t
        refs[3 * ng][...] = o
        refs[3 * ng + 1][...] = o.astype(BF16)

    row = pl.BlockSpec((tm, d_dim), lambda i: (i, 0))
    f = jax.ShapeDtypeStruct((s_dim, d_dim), F32)
    res = _pcall(
        body, out_shape=tuple(ng * [f] + [f, jax.ShapeDtypeStruct((s_dim, d_dim), BF16)]), grid=(s_dim // tm,),
        in_specs=2 * ng * [row], out_specs=tuple((ng + 2) * [row]), name=name,
        compiler_params=_params(("parallel",), (3 * ng + 2) * [((tm, d_dim), F32)], extra=4 * tm * d_dim * 4),
    )(*lses, *outs)
    return res[:ng], res[ng], res[ng + 1]


def _dil_attn_bwd(qkv, bias, d_o, o_mix, alpha, lse, dilation, span, name):
    s_dim = qkv.shape[0]
    d_dim = DIL_HEADS * DIL_HEAD_DIM
    col, nat, bias_spec = _dil_views(s_dim)

    def body(q_ref, k_ref, v_ref, b_ref, do_ref, om_ref, al_ref, l_ref, dq_ref, dk_ref, dv_ref, db_ref, dk_acc, dv_acc):
        db_ref[...] = jnp.zeros_like(db_ref)
        dk_acc[...] = jnp.zeros_like(dk_acc)
        dv_acc[...] = jnp.zeros_like(dv_acc)
        lane = lax.broadcasted_iota(jnp.int32, (DIL_BLOCK, HEAD_PAD), 1)
        klane = lax.broadcasted_iota(jnp.int32, (2 * DIL_BLOCK, HEAD_PAD), 1)
        blocks = list(_dil_blocks(s_dim, dilation))
        heads = [(lane < DIL_HEAD_DIM) == (h == 0) for h in range(2)]
        for g0 in range(0, len(blocks), DIL_GROUPED):
            group = blocks[g0:g0 + DIL_GROUPED]
            staged = []
            for first, blk, kv_rows, tokens in group:
                qb, kb, vb = q_ref[blk, :], k_ref[kv_rows, :], v_ref[kv_rows, :]
                dog = al_ref[tokens, :] * do_ref[tokens, :]
                row_term = dog * om_ref[tokens, :]
                lse_b = l_ref[tokens, :]
                for h in range(2):
                    qh = jnp.where(heads[h], qb, 0)
                    dogh = jnp.where(heads[h], dog, 0.0).astype(BF16)
                    staged.append((_dil_logits(qh, kb, b_ref[h], first, span), _dot_nt(dogh, vb), qh, dogh,
                                   jnp.max(jnp.where(heads[h], lse_b, -jnp.inf), axis=-1, keepdims=True),
                                   jnp.sum(jnp.where(heads[h], row_term, 0.0), axis=-1, keepdims=True)))
            grads = []
            for i, (logits, dp, qh, dogh, lse_h, row) in enumerate(staged):
                p = jnp.exp(logits - lse_h)
                ds = p * (dp - row)
                if group[i // 2][0]:
                    db_ref[i % 2, :, DIL_BLOCK:] += ds
                else:
                    db_ref[i % 2] += ds
                grads.append(((ds * DIL_SCALE).astype(BF16), p.astype(BF16), qh, dogh))
            for i, (_, blk, kv_rows, _) in enumerate(group):
                kb = k_ref[kv_rows, :]
                dq_acc = jnp.zeros((DIL_BLOCK, HEAD_PAD), F32)
                dk_blk = jnp.zeros((kb.shape[0], HEAD_PAD), F32)
                dv_blk = jnp.zeros((kb.shape[0], HEAD_PAD), F32)
                for h in range(2):
                    dsb, pb, qh, dogh = grads[2 * i + h]
                    kmine = (klane[:kb.shape[0]] < DIL_HEAD_DIM) == (h == 0)
                    dq_acc = dq_acc + _dot_nn(dsb, jnp.where(kmine, kb, 0))
                    dk_blk = dk_blk + _dot_tn(dsb, qh)
                    dv_blk = dv_blk + _dot_tn(pb, dogh)
                dq_ref[blk, :] = dq_acc.astype(BF16)
                dk_acc[kv_rows, :] += dk_blk
                dv_acc[kv_rows, :] += dv_blk
        dk_ref[...] = dk_acc[...].astype(BF16)
        dv_ref[...] = dv_acc[...].astype(BF16)

    grad = jax.ShapeDtypeStruct((s_dim, d_dim), BF16)
    return _pcall(
        body, out_shape=(grad, grad, grad, jax.ShapeDtypeStruct(bias.shape, F32)), grid=(DIL_PAIRS,),
        in_specs=[col(0), col(1), col(2), bias_spec, nat, nat, nat, nat],
        out_specs=(nat, nat, nat, bias_spec), name=name,
        scratch_shapes=[pltpu.VMEM((s_dim, HEAD_PAD), F32), pltpu.VMEM((s_dim, HEAD_PAD), F32)],
        compiler_params=_params(("parallel",), 6 * [((s_dim, HEAD_PAD), BF16)] + 4 * [((s_dim, HEAD_PAD), F32)]
                                + 2 * [((2, DIL_BLOCK, 2 * DIL_BLOCK), F32)], extra=2 * s_dim * HEAD_PAD * 4 + 2**21),
    )(qkv, qkv, qkv, bias, d_o, o_mix, alpha, lse)


def _bias_reduce(dbias, buckets, name):
    n_heads = dbias.shape[0]

    def body(db_ref, bk_ref, o_ref):
        ds, bk = db_ref[0], bk_ref[0]
        lane = lax.broadcasted_iota(jnp.int32, (8, HEAD_PAD), 1)
        acc = jnp.zeros((8, HEAD_PAD), F32)
        for b in range(N_BUCKETS):
            acc = jnp.where(lane == b, jnp.sum(jnp.where(bk == b, ds, 0.0)), acc)
        o_ref[0] = acc

    blk = (1, DIL_BLOCK, 2 * DIL_BLOCK)
    return _pcall(
        body, out_shape=jax.ShapeDtypeStruct((n_heads, 8, HEAD_PAD), F32), grid=(n_heads,),
        in_specs=[pl.BlockSpec(blk, lambda h: (h, 0, 0)), pl.BlockSpec(blk, lambda h: (h // DIL_HEADS, 0, 0))],
        out_specs=pl.BlockSpec((1, 8, HEAD_PAD), lambda h: (h, 0, 0)), name=name,
        compiler_params=_params(("parallel",), [(blk, F32), (blk, jnp.int32)], extra=2**20),
    )(dbias, buckets)


def _loss_grad(y, target, name):
    s_dim, d_dim = y.shape
    tm = TOKEN_TILE

    def body(y_ref, t_ref, dy_ref, l_ref):
        @pl.when(pl.program_id(0) == 0)
        def _():
            l_ref[...] = jnp.zeros_like(l_ref)

        err = y_ref[...] - t_ref[...]
        dy_ref[...] = err / d_dim
        sq = (err * err).reshape(tm // 8, 8, d_dim)
        l_ref[...] += 0.5 * jnp.sum(sq, axis=0) / d_dim

    row = pl.BlockSpec((tm, d_dim), lambda i: (i, 0))
    acc = pl.BlockSpec((8, d_dim), lambda i: (0, 0))
    return _pcall(
        body, out_shape=(jax.ShapeDtypeStruct((s_dim, d_dim), F32), jax.ShapeDtypeStruct((8, d_dim), F32)),
        grid=(s_dim // tm,), in_specs=[row, row], out_specs=(row, acc), name=name,
        compiler_params=_params(("arbitrary",), 3 * [((tm, d_dim), F32)], extra=2 * tm * d_dim * 4),
    )(y, target)


def _mod_fwd(c_all, w_mod, b_loc, name):
    depth, d_dim, n = w_mod.shape
    nb = c_all.shape[0]

    def body(c_ref, w_ref, b_ref, o_ref, s_ref):
        cv = c_ref[...]
        sc = cv * jax.nn.sigmoid(cv)
        s_ref[...] = sc
        o_ref[0] = _dot_nn(sc.astype(BF16), w_ref[0].astype(BF16)) + b_ref[0]

    return _pcall(
        body, out_shape=(jax.ShapeDtypeStruct((depth, nb, n), F32), jax.ShapeDtypeStruct((nb, d_dim), F32)), grid=(depth,),
        in_specs=[pl.BlockSpec((nb, d_dim), lambda i: (0, 0)), pl.BlockSpec((1, d_dim, n), lambda i: (i, 0, 0)),
                  pl.BlockSpec((1, 1, n), lambda i: (i, 0, 0))],
        out_specs=(pl.BlockSpec((1, nb, n), lambda i: (i, 0, 0)), pl.BlockSpec((nb, d_dim), lambda i: (0, 0))), name=name,
        compiler_params=_params(("arbitrary",), [((1, d_dim, n), F32)], extra=d_dim * n * 2 + 2**20),
    )(c_all, w_mod, b_loc.reshape(depth, 1, n))


def _sum_parts(parts, name):
    _, rows, cols = parts.shape
    fits = [t for t in range(16, rows // 2 + 1, 16) if rows % t == 0 and NDEV * t * cols * parts.dtype.itemsize <= 3 * 2**20]
    tr = max(fits) if fits else rows

    def body(p_ref, o_ref):
        acc = p_ref[0].astype(F32)
        for k in range(1, NDEV):
            acc = acc + p_ref[k].astype(F32)
        o_ref[...] = acc

    return _pcall(
        body, out_shape=jax.ShapeDtypeStruct((rows, cols), F32), grid=(rows // tr,),
        in_specs=[pl.BlockSpec((NDEV, tr, cols), lambda i: (0, i, 0))], out_specs=pl.BlockSpec((tr, cols), lambda i: (i, 0)),
        name=name, compiler_params=_params(("parallel",), [((NDEV, tr, cols), parts.dtype), ((tr, cols), F32)], extra=2**20),
    )(parts)


def _adamw(w, g, m, v, name):
    shape = w.shape
    cols = shape[-1]
    rows = math.prod(shape[:-1])
    tr = rows
    for cand in (2048, 1024, 512, 256, 128, 64, 32, 16, 8):
        if rows % cand == 0 and rows > cand and cand * cols * 4 <= 2**21:
            tr = cand
            break

    def body(w_ref, g_ref, m_ref, v_ref, d_ref, mo_ref, vo_ref):
        gv = g_ref[...]
        mn = ADAM_B1 * m_ref[...] + (1.0 - ADAM_B1) * gv
        vn = ADAM_B2 * v_ref[...] + (1.0 - ADAM_B2) * (gv * gv)
        m_hat = mn / (1.0 - ADAM_B1 ** ADAM_STEP)
        v_hat = vn / (1.0 - ADAM_B2 ** ADAM_STEP)
        d_ref[...] = -ADAM_LR * (m_hat / (jnp.sqrt(v_hat) + ADAM_EPS) + ADAM_WD * w_ref[...])
        mo_ref[...] = mn
        vo_ref[...] = vn

    blk = pl.BlockSpec((tr, cols), lambda i: (i, 0))
    out = jax.ShapeDtypeStruct((rows, cols), F32)
    res = _pcall(
        body, out_shape=(out, out, out), grid=(rows // tr,), in_specs=4 * [blk], out_specs=(blk, blk, blk), name=name,
        compiler_params=_params(("parallel",), 7 * [((tr, cols), F32)], extra=4 * tr * cols * 4),
    )(*(a.reshape(rows, cols) for a in (w, g, m, v)))
    return tuple(r.reshape(shape) for r in res)


def _peers():
    x, y, c = lax.axis_index("x"), lax.axis_index("y"), lax.axis_index("c")
    flip = lambda v, f: 1 - v if f else v
    peers = []
    for f in range(1, NDEV):
        px, py, pc = flip(x, f & 4), flip(y, f & 2), flip(c, f & 1)
        peers.append(((px, py, pc), 4 * px + 2 * py + pc))
    return (x, y, c), 4 * x + 2 * y + c, peers


def _places():
    x, y, c = lax.axis_index("x"), lax.axis_index("y"), lax.axis_index("c")
    place = lambda px, py, pc: ((px, py, pc), 4 * px + 2 * py + pc)
    return place(x, y, c), place(x, y, 1 - c), [place(1 - x, y, c), place(x, 1 - y, c), place(1 - x, 1 - y, c)]


def _exchange(arrs, gather, name):
    n = len(arrs)
    hbm = pl.BlockSpec(memory_space=pltpu.HBM)
    if gather:
        out_shape = [jax.ShapeDtypeStruct((NDEV * a.shape[0], a.shape[1]), a.dtype) for a in arrs]
    else:
        out_shape = [jax.ShapeDtypeStruct((NDEV, a.shape[0] // NDEV, a.shape[1]), a.dtype) for a in arrs]

    def body(*refs):
        ins, outs = refs[:n], refs[n:2 * n]
        send_sems, recv_sems, local_sems = refs[2 * n:]
        me_pos, me, peers = _peers()
        local = []
        for k in range(n):
            rows = arrs[k].shape[0] if gather else arrs[k].shape[0] // NDEV
            if gather:
                src_of = lambda idx: ins[k]
                dst_of = lambda idx: outs[k].at[pl.ds(me * rows, rows)]
                mine = (ins[k], outs[k].at[pl.ds(me * rows, rows)])
            else:
                src_of = lambda idx: ins[k].at[pl.ds(idx * rows, rows)]
                dst_of = lambda idx: outs[k].at[me]
                mine = (ins[k].at[pl.ds(me * rows, rows)], outs[k].at[me])
            cp = pltpu.make_async_copy(mine[0], mine[1], local_sems.at[k])
            cp.start()
            local.append(cp)
            for pos, idx in peers:
                pltpu.make_async_remote_copy(src_ref=src_of(idx), dst_ref=dst_of(idx), send_sem=send_sems.at[k],
                                             recv_sem=recv_sems.at[k], device_id=pos, device_id_type=MESH).start()
        for k in range(n):
            rows = arrs[k].shape[0] if gather else arrs[k].shape[0] // NDEV
            sent = ins[k].at[pl.ds(0, (NDEV - 1) * rows)] if not gather else outs[k].at[pl.ds(0, (NDEV - 1) * rows)]
            got = outs[k].at[pl.ds(0, (NDEV - 1) * rows)] if gather else outs[k].at[pl.ds(0, NDEV - 1)]
            pltpu.make_async_remote_copy(src_ref=sent, dst_ref=sent, send_sem=send_sems.at[k], recv_sem=recv_sems.at[k],
                                         device_id=me_pos, device_id_type=MESH).wait_send()
            pltpu.make_async_remote_copy(src_ref=got, dst_ref=got, send_sem=send_sems.at[k], recv_sem=recv_sems.at[k],
                                         device_id=me_pos, device_id_type=MESH).wait_recv()
            local[k].wait()

    return pl.pallas_call(
        body, out_shape=out_shape, in_specs=n * [hbm], out_specs=n * [hbm], name=name,
        scratch_shapes=[pltpu.SemaphoreType.DMA((n,)), pltpu.SemaphoreType.DMA((n,)), pltpu.SemaphoreType.DMA((n,))],
        compiler_params=pltpu.CompilerParams(has_side_effects=True),
    )(*arrs)


_HBM = pl.BlockSpec(memory_space=pltpu.HBM)
_SEM = pl.BlockSpec(memory_space=pltpu.SEMAPHORE)
_DATAFLOW = pltpu.SideEffectType.DATAFLOW_SIDE_EFFECTING


def _split_start(srcs, groups, gather, name):
    n = len(srcs)
    if gather:
        lands = [lax.empty((NDEV * a.shape[0], a.shape[1]), a.dtype) for a in srcs]
    else:
        lands = [lax.empty((NDEV, a.shape[0] // NDEV, a.shape[1]), a.dtype) for a in srcs]
    n_sem = 3 * len(groups)

    def body(*refs):
        src_refs, land_refs = refs[:n], refs[n:2 * n]
        sems = refs[2 * n:2 * n + n_sem]
        token = refs[-1]
        (_, my), sibling, chips = _places()
        _, _, peers = _peers()
        targets = [sibling] + chips if gather else peers
        for g, members in enumerate(groups):
            for j, k in enumerate(members):
                _own_copy(src_refs[k], land_refs[k], sems[3 * g + 2].at[j], my, gather).start()
        for g, members in enumerate(groups):
            for j, k in enumerate(members):
                rows = srcs[k].shape[0] if gather else srcs[k].shape[0] // NDEV
                for pos, idx in targets:
                    src = src_refs[k] if gather else src_refs[k].at[pl.ds(idx * rows, rows)]
                    dst = land_refs[k].at[pl.ds(my * rows, rows)] if gather else land_refs[k].at[my]
                    pltpu.make_async_remote_copy(src_ref=src, dst_ref=dst, send_sem=sems[3 * g].at[j],
                                                 recv_sem=sems[3 * g + 1].at[j], device_id=pos, device_id_type=MESH).start()
        token[...] = jnp.zeros_like(token)

    out_shape = []
    for members in groups:
        out_shape += 3 * [pltpu.SemaphoreType.DMA((len(members),))]
    out_shape += [pltpu.HBM(a.shape, a.dtype) for a in srcs] + [pltpu.HBM(a.shape, a.dtype) for a in lands]
    out_shape.append(jax.ShapeDtypeStruct((8, 128), F32))
    res = pl.pallas_call(
        body, name=name, out_shape=tuple(out_shape), in_specs=2 * n * [_HBM],
        out_specs=tuple(n_sem * [_SEM] + 2 * n * [_HBM] + [pl.BlockSpec(memory_space=pltpu.VMEM)]),
        input_output_aliases={i: n_sem + i for i in range(2 * n)},
        compiler_params=pltpu.CompilerParams(has_side_effects=_DATAFLOW),
    )(*[pltpu.with_memory_space_constraint(a, pltpu.HBM) for a in list(srcs) + lands])
    sems = [tuple(res[3 * g:3 * g + 3]) for g in range(len(groups))]
    return sems, list(res[n_sem:n_sem + n]), list(res[n_sem + n:n_sem + 2 * n]), res[-1]


def _own_copy(src_ref, land_ref, sem, my, gather):
    if gather:
        rows = src_ref.shape[0]
        return pltpu.make_async_copy(src_ref, land_ref.at[pl.ds(my * rows, rows)], sem)
    rows = src_ref.shape[0] // NDEV
    return pltpu.make_async_copy(src_ref.at[pl.ds(my * rows, rows)], land_ref.at[my], sem)


def _wait_all(land_ref, blocks_per_dev, copies, send_sem, recv_sem, me_pos):
    part = land_ref.at[pl.ds(0, copies * blocks_per_dev)]
    pltpu.make_async_remote_copy(src_ref=part, dst_ref=part, send_sem=send_sem, recv_sem=recv_sem,
                                 device_id=me_pos, device_id_type=MESH).wait()


def _gather_forward(sems, srcs, lands, after, name):
    n = len(srcs)

    def body(*refs):
        land_refs = refs[n:2 * n]
        send_a, recv_a = refs[2 * n], refs[2 * n + 1]
        send_b, recv_b = refs[2 * n + 3], refs[2 * n + 4]
        token = refs[-1]
        (me_pos, _), sibling, chips = _places()
        for j in range(n):
            _wait_all(land_refs[j], lands[j].shape[0] // NDEV, 1 + OTHER_CHIPS, send_a.at[j], recv_a.at[j], me_pos)
        for j in range(n):
            rows = lands[j].shape[0] // NDEV
            for _, idx in chips:
                block = land_refs[j].at[pl.ds(idx * rows, rows)]
                pltpu.make_async_remote_copy(src_ref=block, dst_ref=block, send_sem=send_b.at[j], recv_sem=recv_b.at[j],
                                             device_id=sibling[0], device_id_type=MESH).start()
        token[...] = jnp.zeros_like(token)

    res = pl.pallas_call(
        body, name=name,
        out_shape=(pltpu.SemaphoreType.DMA((n,)), pltpu.SemaphoreType.DMA((n,)))
        + tuple(pltpu.HBM(a.shape, a.dtype) for a in list(srcs) + list(lands)) + (jax.ShapeDtypeStruct((8, 128), F32),),
        in_specs=2 * n * [_HBM] + [_SEM, _SEM, pl.BlockSpec(memory_space=pl.ANY)],
        out_specs=tuple([_SEM, _SEM] + 2 * n * [_HBM] + [pl.BlockSpec(memory_space=pltpu.VMEM)]),
        input_output_aliases={i: 2 + i for i in range(2 * n)},
        compiler_params=pltpu.CompilerParams(has_side_effects=_DATAFLOW),
    )(*srcs, *lands, sems[0], sems[1], after)
    return (res[0], res[1]), list(res[2:2 + n]), list(res[2 + n:2 + 2 * n]), res[-1]


def _split_wait(sems, srcs, lands, after, copies, gather, name):
    n = len(srcs)

    def body(*refs):
        src_refs, land_refs = refs[:n], refs[n:2 * n]
        send_sem, recv_sem, local_sem = refs[2 * n], refs[2 * n + 1], refs[2 * n + 2]
        (me_pos, my), _, _ = _places()
        for j in range(n):
            _wait_all(land_refs[j], lands[j].shape[0] // NDEV, copies, send_sem.at[j], recv_sem.at[j], me_pos)
            _own_copy(src_refs[j], land_refs[j], local_sem.at[j], my, gather).wait()

    res = pl.pallas_call(
        body, name=name, out_shape=tuple(pltpu.HBM(a.shape, a.dtype) for a in list(srcs) + list(lands)),
        in_specs=2 * n * [_HBM] + [_SEM, _SEM, _SEM, pl.BlockSpec(memory_space=pl.ANY)], out_specs=tuple(2 * n * [_HBM]),
        input_output_aliases={i: i for i in range(2 * n)},
        compiler_params=pltpu.CompilerParams(has_side_effects=_DATAFLOW),
    )(*srcs, *lands, sems[0], sems[1], sems[2], after)
    return list(res[n:])


def _chained(gate, mid, after):
    return gate if mid is None else gate + mid(after)[:1, :1]


def _ffn_fwd(x, norms, mod, w, mid=None):
    (pre_g, post_g), (shift, scale, gate), (wg_t, wu_t, wd) = norms, mod, w
    if not callable(wd):
        hn, g, u, a, x_out, f = _ffn_fwd_fused(x, pre_g, scale, shift, post_g, _chained(gate, mid, x), wg_t, wu_t, wd, "ffn_fwd")
        return x_out, (x, hn, g, u, a, f), (wg_t, wu_t, wd)
    hn, g, u, a = _ffn_up(x, pre_g, scale, shift, wg_t, wu_t, "ffn_up")
    wd = wd(a)
    x_out, f = _mm_post(a, wd, x, post_g, _chained(gate, mid, a), FFN_RES, "ffn_down")
    return x_out, (x, hn, g, u, a, f), (wg_t, wu_t, wd)


def _ffn_bwd(dx_out, saved, norms, mod, w, send=None):
    (pre_g, post_g), (_, scale, gate), (wg_t, wu_t, wd) = norms, mod, w
    x, hn, g, u, a, f = saved
    d_model = x.shape[1]
    if send is None:
        df, dg, du, dx, dgate, dpost, dshift, dscale, dpre = _ffn_bwd_fused(dx_out, saved, pre_g, post_g, scale, gate,
                                                                            wg_t, wu_t, wd, "ffn_bwd")
        dws = tuple(_mm([pair], "tn", BF16, 256, d_model, "ffn_dw") for pair in ((dg, hn), (du, hn), (a, df)))
        return dx, (dpre, dpost), (dshift, dscale, dgate), dws
    sent = send
    df, dgate, dpost = _post_bwd(dx_out, f, post_g, gate, FFN_RES, "ffn_post_bwd")
    dwd = _mm([(a, df)], "tn", BF16, 256, d_model, "ffn_dw")
    dg, du = _ffn_dgu(df, wd, g, u, "ffn_dgu", after=sent(2, dwd))
    dwg_t = _mm([(dg, hn)], "tn", BF16, 256, d_model, "ffn_dw")
    dwu_t = _mm([(du, hn)], "tn", BF16, 256, d_model, "ffn_dw", after=sent(0, dwg_t))
    dhn = _mm([(dg, wg_t), (du, wu_t)], "nn", F32, TOKEN_TILE, d_model, "ffn_dhn", after=sent(1, dwu_t))
    dx, dshift, dscale, dpre = _prenorm_bwd(dx_out, [dhn], x, pre_g, scale, "prenorm_bwd")
    return dx, (dpre, dpost), (dshift, dscale, dgate), (dwg_t, dwu_t, dwd)


def _mla_fwd(x, norms, mod, w, rope, mid=None):
    (pre_g, post_g), (shift, scale, gate) = norms, mod
    w_in, q_norm, wq_t, kv_norm, wkv_t, wo = w
    hn, lat = _prenorm_mm(x, pre_g, scale, shift, w_in, "nn", F32, LAT_PAD, "mla_in")
    gate = _chained(gate, mid, lat)
    q, k, v, qn, kvn = _mla_qkv(lat, q_norm, kv_norm, wq_t, wkv_t, rope, "mla_qkv")
    o = _mla_attn_fwd(q, k, v, "mla_attn_fwd")
    x_out, f = _mm_post(o, wo, x, post_g, gate, 1.0, "mla_out")
    return x_out, (x, hn, lat, q, k, v, qn, kvn, o, f)


def _mla_bwd(dx_out, saved, norms, mod, w, rope):
    (pre_g, post_g), (_, scale, gate) = norms, mod
    w_in, q_norm, wq_t, kv_norm, wkv_t, wo = w
    x, hn, lat, q, k, v, qn, kvn, o, f = saved
    d_model = x.shape[1]
    df, dgate, dpost = _post_bwd(dx_out, f, post_g, gate, 1.0, "mix_post_bwd")
    d_o = _mm([(df, wo)], "nt", F32, TOKEN_TILE, wo.shape[0], "mla_do")
    dwo = _mm([(o, df)], "tn", BF16, TOKEN_TILE, d_model, "mla_dwo")
    dq, dk, dv = _mla_attn_bwd(q, k, v, d_o, "mla_attn_bwd")
    dqp, dkv, dlat, dq_norm, dkv_norm = _mla_qkv_bwd(dq, dk, dv, lat, q_norm, kv_norm, wq_t, wkv_t, rope, "mla_qkv_bwd")
    dwq_t = _mm([(dqp, qn)], "tn", BF16, TOKEN_TILE, Q_LORA, "mla_dwq")
    dwkv_t = _mm([(dkv, kvn)], "tn", BF16, TOKEN_TILE, KV_LORA, "mla_dwkv")
    dw_in = _mm([(hn, dlat)], "tn", BF16, TOKEN_TILE, LAT_PAD, "mla_dwin")
    dhn = _mm([(dlat, w_in)], "nt", F32, TOKEN_TILE, d_model, "mla_dhn")
    dx, dshift, dscale, dpre = _prenorm_bwd(dx_out, [dhn], x, pre_g, scale, "prenorm_bwd")
    return dx, (dpre, dpost), (dshift, dscale, dgate), (dw_in, dq_norm, dwq_t, dkv_norm, dwkv_t, dwo)


def _dil_fwd(x, norms, mod, w, bias, mid=None):
    (pre_g, post_g), (shift, scale, gate), (w_in_t, wo) = norms, mod, w
    width = 3 * DIL_HEADS * DIL_HEAD_DIM
    hns, qkvs, outs, lses = [], [], [], []
    for g, (window, dilation) in enumerate(DIL_GROUPS):
        hn, qkv = _prenorm_mm(x, pre_g, scale, shift, w_in_t, "nt", BF16, width, "dil_in", perm=dilation,
                              w_rows=(g * width, width))
        if g == 0:
            gate = _chained(gate, mid, qkv)
        o, lse = _dil_attn_fwd(qkv, bias[g], dilation, window // dilation, "dil_attn_fwd")
        hns.append(hn), qkvs.append(qkv), outs.append(o), lses.append(lse)
    alphas, o_mix, o_mix_b = _dil_mix(lses, outs, "dil_mix")
    x_out, f = _mm_post(o_mix_b, wo, x, post_g, gate, 1.0, "dil_out")
    return x_out, (x, hns, qkvs, lses, alphas, o_mix, o_mix_b, f)


def _dil_bwd(dx_out, saved, norms, mod, w, bias):
    (pre_g, post_g), (_, scale, gate), (w_in_t, wo) = norms, mod, w
    x, hns, qkvs, lses, alphas, o_mix, o_mix_b, f = saved
    d_model = x.shape[1]
    inner = DIL_HEADS * DIL_HEAD_DIM
    df, dgate, dpost = _post_bwd(dx_out, f, post_g, gate, 1.0, "mix_post_bwd")
    d_o = _mm([(df, wo)], "nt", F32, TOKEN_TILE, inner, "dil_do")
    dwo = _mm([(o_mix_b, df)], "tn", BF16, TOKEN_TILE, d_model, "dil_dwo")
    dhns, dws, dbs = [], [], []
    for g, (window, dilation) in enumerate(DIL_GROUPS):
        grads = _dil_attn_bwd(qkvs[g], bias[g], d_o, o_mix, alphas[g], lses[g], dilation, window // dilation, "dil_attn_bwd")
        dbs.append(grads[3])
        dhns.append(_mm([(grads[j], w_in_t) for j in range(3)], "nn", F32, TOKEN_TILE, d_model, "dil_dhn", out_perm=dilation,
                        b_rows=[(3 * g + j) * inner for j in range(3)]))
        dws += [_mm([(grads[j], hns[g])], "tn", BF16, TOKEN_TILE, d_model, "dil_dwin") for j in range(3)]
    dx, dshift, dscale, dpre = _prenorm_bwd(dx_out, dhns, x, pre_g, scale, "prenorm_bwd3")
    return dx, (dpre, dpost), (dshift, dscale, dgate), (jnp.concatenate(dws, axis=0), dwo), jnp.concatenate(dbs, axis=0)


def _pad_rows(a, rows):
    return jnp.pad(a, ((0, rows - a.shape[0]), (0, 0)))


def _lanes(a):
    flat = a.reshape(-1).astype(F32)
    rows = -(-flat.shape[0] // 1024) * 8
    return jnp.pad(flat, (0, rows * 128 - flat.shape[0])).reshape(rows, 128)


def kernel(x, c, norm_pre, norm_post, w_mod, b_mod, ffn_w_gate, ffn_w_up, ffn_w_down, mla_w_in, mla_q_norm, mla_w_q_up, mla_kv_norm, mla_w_kv_up, mla_w_o, dil_w_in, dil_w_o, rel_bias, loss_target, m_norm_pre, m_norm_post, m_w_mod, m_b_mod, m_ffn_w_gate, m_ffn_w_up, m_ffn_w_down, m_mla_w_in, m_mla_q_norm, m_mla_w_q_up, m_mla_kv_norm, m_mla_w_kv_up, m_mla_w_o, m_dil_w_in, m_dil_w_o, m_rel_bias, v_norm_pre, v_norm_post, v_w_mod, v_b_mod, v_ffn_w_gate, v_ffn_w_up, v_ffn_w_down, v_mla_w_in, v_mla_q_norm, v_mla_w_q_up, v_mla_kv_norm, v_mla_w_kv_up, v_mla_w_o, v_dil_w_in, v_dil_w_o, v_rel_bias):
    me = 4 * lax.axis_index("x") + 2 * lax.axis_index("y") + lax.axis_index("c")
    depth, n_sub, d_loc = norm_pre.shape
    d_model = x.shape[2]
    mod_loc_cols = w_mod.shape[2]
    x0, target = x[0], loss_target[0]

    bf_t = lambda a: a.astype(BF16).T
    ffn_ids = [(i, h) for i in range(depth) for h in range(2)]
    shards = []
    for i, h in ffn_ids:
        shards += [bf_t(ffn_w_gate[i, h]), bf_t(ffn_w_up[i, h]), ffn_w_down[i, h].astype(BF16)]
    shards += [mla_w_in[0].astype(BF16), bf_t(mla_w_q_up[0]), bf_t(mla_w_kv_up[0]), mla_w_o[0].astype(BF16),
               bf_t(dil_w_in[0]), dil_w_o[0].astype(BF16)]
    n_ffn = 3 * len(ffn_ids)
    members = {(0, 0): [0, 1, 2], (0, 1): [n_ffn, n_ffn + 1, n_ffn + 2, n_ffn + 3], (0, 2): [3, 4, 5],
               (1, 0): [6, 7, 8], (1, 1): [n_ffn + 4, n_ffn + 5], (1, 2): [9, 10, 11]}
    order = [(i, s) for i in range(depth) for s in range(n_sub)]

    small = jnp.concatenate([c.reshape(8, 128), _pad_rows(norm_pre.reshape(depth * n_sub, d_loc), 8),
                             _pad_rows(norm_post.reshape(depth * n_sub, d_loc), 8)], axis=0)
    small_all = _exchange([small], True, "gather_small")[0].reshape(NDEV, 24, 128)
    c_all = small_all[:, 0:8].reshape(NDEV, d_model)
    gains = lambda lo: jnp.transpose(small_all[:, lo:lo + depth * n_sub], (1, 0, 2)).reshape(depth, n_sub, 1, d_model)
    pre_full, post_full = gains(8), gains(16)

    b_loc = lax.dynamic_slice(b_mod, (0, me * mod_loc_cols), (depth, mod_loc_cols))
    mod_cols, silu_c = _mod_fwd(c_all, w_mod, b_loc, "mod_fwd")
    mod_all = _exchange([mod_cols.reshape(depth * NDEV, mod_loc_cols)], True, "gather_mod")[0]
    mod_all = mod_all.reshape(NDEV, depth, NDEV, mod_loc_cols)
    mod_mine = lax.dynamic_index_in_dim(mod_all, me, axis=2, keepdims=False)
    mod = jnp.transpose(mod_mine, (1, 0, 2)).reshape(depth, n_sub, 3, 1, d_model)

    shards[0], _ = lax.optimization_barrier((shards[0], mod_all))
    first = order[0]
    stages = [("%d%d" % first, members[first][:2]), ("%d%dd" % first, members[first][2:])]
    stages += [("%d%d" % key, members[key]) for key in order[1:]]
    stage_names = [name for name, _ in stages]
    g_sems, g_srcs, g_lands, g_token = _split_start(shards, [idx for _, idx in stages], True, "gather_weights_start")

    forwarded = {}

    def forward(stage, after):
        idx = stages[stage_names.index(stage)][1]
        forwarded[stage] = _gather_forward(g_sems[stage_names.index(stage)], [g_srcs[k] for k in idx],
                                           [g_lands[k] for k in idx], after, "gather_forward_" + stage)
        return forwarded[stage][3]

    def weights_of(stage, after):
        (send_b, recv_b), srcs, lands, _ = forwarded[stage]
        local = g_sems[stage_names.index(stage)][2]
        return _split_wait((send_b, recv_b, local), srcs, lands, after, OTHER_CHIPS, True, "gather_wait_" + stage)

    def late_down(after):
        forward("%d%dd" % first, after)
        return weights_of("%d%dd" % first, after)[0]

    lat_real = Q_LORA + KV_LORA
    qk = QK_NOPE + QK_ROPE

    def mla_weights(after):
        w_in, wq_t, wkv_t, wo = weights_of("01", after)
        w_in_pad = jnp.concatenate([w_in[:, :lat_real], jnp.zeros((d_model, QK_NOPE), BF16), w_in[:, lat_real:],
                                    jnp.zeros((d_model, HEAD_PAD - QK_NOPE - QK_ROPE), BF16)], axis=1)
        wq_pad = jnp.pad(wq_t.reshape(MLA_HEADS, qk, Q_LORA), ((0, 0), (0, HEAD_PAD - qk), (0, 0)))
        wo_pad = jnp.pad(wo.reshape(MLA_HEADS, V_HEAD, d_model), ((0, 0), (HEAD_PAD - V_HEAD, 0), (0, 0)))
        return (w_in_pad, mla_q_norm, wq_pad.reshape(MLA_HEADS * HEAD_PAD, Q_LORA), mla_kv_norm, wkv_t,
                wo_pad.reshape(MLA_HEADS * HEAD_PAD, d_model))

    zero = g_token[0, 0]
    rope = _rope_tables(zero)
    buckets = jnp.stack([_dil_buckets(dil) for _, dil in DIL_GROUPS]) + zero.astype(jnp.int32)
    onehot = (buckets[..., None] == jnp.arange(N_BUCKETS)).astype(F32)
    bias = jnp.einsum("gqkb,bgh->ghqk", onehot, rel_bias.reshape(N_BUCKETS, len(DIL_GROUPS), DIL_HEADS),
                      precision=lax.Precision.HIGHEST)

    norms = lambda i, s: (pre_full[i, s], post_full[i, s])
    mods = lambda i, s: (mod[i, s, 0], mod[i, s, 1], mod[i, s, 2])
    saved, weights = {}, {}
    h = lax.optimization_barrier((x0, bias, buckets, *rope))[0]
    forward("%d%d" % first, h)
    for n, (i, s) in enumerate(order):
        got = mla_weights(h) if (s == 1 and i % 2 == 0) else tuple(weights_of("%d%d" % (i, s), h))
        mid = None if n + 1 == len(order) else (lambda after, nxt="%d%d" % order[n + 1]: forward(nxt, after))
        if s != 1:
            if len(got) == 3:
                h, saved[i, s], weights[i, s] = _ffn_fwd(h, norms(i, s), mods(i, s), got)
                if mid is not None:
                    mid(h)
            else:
                h, saved[i, s], weights[i, s] = _ffn_fwd(h, norms(i, s), mods(i, s), (*got, late_down), mid)
            continue
        weights[i, s] = got
        if i % 2 == 0:
            h, saved[i, s] = _mla_fwd(h, norms(i, s), mods(i, s), weights[i, s], rope, mid)
        else:
            h, saved[i, s] = _dil_fwd(h, norms(i, s), mods(i, s), weights[i, s], bias, mid)
    dh, loss_parts = _loss_grad(h, target, "loss")

    dnorm, dmod, sent = {}, {}, {}
    token = jnp.zeros((8, 128), F32)
    last = order[0]

    def send_last(j, dw):
        sent[last, j] = _split_start([dw], [[0]], False, "scatter_start_%d%d_%d" % (*last, j))
        return sent[last, j][3]

    for i, s in reversed(order):
        md = mods(i, s)
        md = (md[0], md[1], md[2] + token[:1, :1])
        if (i, s) == last:
            dh, dnorm[i, s], dmod[i, s], _ = _ffn_bwd(dh, saved[i, s], norms(i, s), md, weights[i, s], send_last)
            continue
        if s != 1:
            dh, dnorm[i, s], dmod[i, s], dws = _ffn_bwd(dh, saved[i, s], norms(i, s), md, weights[i, s])
        elif i % 2 == 0:
            dh, dnorm[i, s], dmod[i, s], dmla = _mla_bwd(dh, saved[i, s], norms(i, s), md, weights[i, s], rope)
            dw_in_pad, dq_norm, dwq_pad, dkv_norm, dwkv_t, dwo_pad = dmla
            dw_in = jnp.concatenate([dw_in_pad[:, :lat_real], dw_in_pad[:, lat_real + QK_NOPE:lat_real + qk]], axis=1)
            dwq_t = dwq_pad.reshape(MLA_HEADS, HEAD_PAD, Q_LORA)[:, :qk].reshape(MLA_HEADS * qk, Q_LORA)
            dwo = dwo_pad.reshape(MLA_HEADS, HEAD_PAD, d_model)[:, HEAD_PAD - V_HEAD:].reshape(MLA_HEADS * V_HEAD, d_model)
            dws = (dw_in, dwq_t, dwkv_t, dwo)
        else:
            dh, dnorm[i, s], dmod[i, s], dws, dbias = _dil_bwd(dh, saved[i, s], norms(i, s), md, weights[i, s], bias)
        sent[i, s] = _split_start(list(dws), [list(range(len(dws)))], False, "scatter_start_%d%d" % (i, s))
        token = sent[i, s][3]
    grad_x = dh[None]

    mine = {}
    for key in reversed(order[1:]):
        sems, srcs, lands, _ = sent[key]
        parts = _split_wait(sems[0], srcs, lands, dh, NDEV - 1, False, "scatter_wait_%d%d" % key)
        for k, p in zip(members[key], parts):
            mine[k] = _sum_parts(p, "sum_parts")
    g_mla_in, g_q_up, g_kv_up, g_mla_o, g_dil_in, g_dil_o = (mine[k] for k in range(n_ffn, n_ffn + 6))
    g_mla_in, g_q_up, g_kv_up, g_mla_o = g_mla_in[None], g_q_up.T[None], g_kv_up.T[None], g_mla_o[None]
    g_dil_in, g_dil_o = g_dil_in.T[None], g_dil_o[None]
    early = {"mla_w_in": _adamw(mla_w_in, g_mla_in, m_mla_w_in, v_mla_w_in, "adamw"),
             "mla_w_q_up": _adamw(mla_w_q_up, g_q_up, m_mla_w_q_up, v_mla_w_q_up, "adamw"),
             "mla_w_kv_up": _adamw(mla_w_kv_up, g_kv_up, m_mla_w_kv_up, v_mla_w_kv_up, "adamw"),
             "mla_w_o": _adamw(mla_w_o, g_mla_o, m_mla_w_o, v_mla_w_o, "adamw"),
             "dil_w_in": _adamw(dil_w_in, g_dil_in, m_dil_w_in, v_dil_w_in, "adamw"),
             "dil_w_o": _adamw(dil_w_o, g_dil_o, m_dil_w_o, v_dil_w_o, "adamw")}
    dbias_sums = _bias_reduce(dbias, buckets, "bias_reduce")
    tied = lax.optimization_barrier((dbias_sums, *[a for step in early.values() for a in step]))
    dbias_sums, early = tied[0], {name: tuple(tied[1 + 3 * n:4 + 3 * n]) for n, name in enumerate(early)}
    for j in (2, 0, 1):
        sems, srcs, lands, _ = sent[last, j]
        parts = _split_wait(sems[0], srcs, lands, dbias_sums, NDEV - 1, False, "scatter_wait_%d%d_%d" % (*last, j))
        mine[members[last][j]] = _sum_parts(parts[0], "sum_parts")
    g_gate = jnp.stack([mine[3 * n].T for n in range(len(ffn_ids))]).reshape(ffn_w_gate.shape)
    g_up = jnp.stack([mine[3 * n + 1].T for n in range(len(ffn_ids))]).reshape(ffn_w_up.shape)
    g_down = jnp.stack([mine[3 * n + 2] for n in range(len(ffn_ids))]).reshape(ffn_w_down.shape)

    dmod_mine = jnp.concatenate([jnp.concatenate(dmod[i, s], axis=0) for i in range(depth) for s in range(n_sub)], axis=0)
    dpre_mine = jnp.concatenate([dnorm[i, s][0] for i in range(depth) for s in range(n_sub)], axis=0)
    dpost_mine = jnp.concatenate([dnorm[i, s][1] for i in range(depth) for s in range(n_sub)], axis=0)
    dbias_tab = dbias_sums[:, 0, :N_BUCKETS].T
    pieces = [dmod_mine, dpre_mine, dpost_mine, dq_norm, dkv_norm, dbias_tab, jnp.sum(loss_parts).reshape(1, 1)]
    packed = [_lanes(p) for p in pieces]
    offs = [0]
    for p in packed:
        offs.append(offs[-1] + p.shape[0])
    everyone = _exchange([jnp.concatenate(packed, axis=0)], True, "gather_small_grads")[0].reshape(NDEV, offs[-1], 128)
    total = _sum_parts(everyone, "sum_small")
    take = lambda n, shape: total[offs[n]:offs[n + 1]].reshape(-1)[:math.prod(shape)].reshape(shape)
    g_b_mod = take(0, b_mod.shape)
    col0 = me * d_loc
    g_norm_pre = lax.dynamic_slice(take(1, (depth, n_sub, d_model)), (0, 0, col0), norm_pre.shape)
    g_norm_post = lax.dynamic_slice(take(2, (depth, n_sub, d_model)), (0, 0, col0), norm_post.shape)
    g_q_norm, g_kv_norm = take(3, mla_q_norm.shape), take(4, mla_kv_norm.shape)
    g_rel_bias = take(5, rel_bias.shape)
    loss = take(6, ())

    dmod_all = everyone[:, offs[0]:offs[1]].reshape(NDEV, depth, NDEV * mod_loc_cols)
    dmod_cols = lax.dynamic_slice(dmod_all, (0, 0, me * mod_loc_cols), (NDEV, depth, mod_loc_cols))
    silu_t = jnp.pad(silu_c.T, ((0, 0), (0, HEAD_PAD - NDEV)))
    g_w_mod = jnp.stack([_mm([(silu_t, jnp.pad(dmod_cols[:, i], ((0, HEAD_PAD - NDEV), (0, 0))))], "nn", F32, TOKEN_TILE,
                             mod_loc_cols, "mod_bwd") for i in range(depth)])

    ws = (norm_pre, norm_post, w_mod, b_mod, ffn_w_gate, ffn_w_up, ffn_w_down, mla_w_in, mla_q_norm, mla_w_q_up, mla_kv_norm,
          mla_w_kv_up, mla_w_o, dil_w_in, dil_w_o, rel_bias)
    gs = (g_norm_pre, g_norm_post, g_w_mod, g_b_mod, g_gate, g_up, g_down, g_mla_in, g_q_norm, g_q_up, g_kv_norm, g_kv_up,
          g_mla_o, g_dil_in, g_dil_o, g_rel_bias)
    ms = (m_norm_pre, m_norm_post, m_w_mod, m_b_mod, m_ffn_w_gate, m_ffn_w_up, m_ffn_w_down, m_mla_w_in, m_mla_q_norm,
          m_mla_w_q_up, m_mla_kv_norm, m_mla_w_kv_up, m_mla_w_o, m_dil_w_in, m_dil_w_o, m_rel_bias)
    vs = (v_norm_pre, v_norm_post, v_w_mod, v_b_mod, v_ffn_w_gate, v_ffn_w_up, v_ffn_w_down, v_mla_w_in, v_mla_q_norm,
          v_mla_w_q_up, v_mla_kv_norm, v_mla_w_kv_up, v_mla_w_o, v_dil_w_in, v_dil_w_o, v_rel_bias)
    names = ("norm_pre", "norm_post", "w_mod", "b_mod", "ffn_w_gate", "ffn_w_up", "ffn_w_down", "mla_w_in", "mla_q_norm",
             "mla_w_q_up", "mla_kv_norm", "mla_w_kv_up", "mla_w_o", "dil_w_in", "dil_w_o", "rel_bias")
    stepped = [early[n] if n in early else _adamw(w, g, m, v, "adamw") for n, w, g, m, v in zip(names, ws, gs, ms, vs)]
    deltas, new_m, new_v = zip(*stepped)
    return (loss, grad_x, *gs, *deltas, *new_m, *new_v)
```

```python
import math

import jax
import jax.numpy as jnp
from jax import lax
from jax.experimental import pallas as pl
from jax.experimental.pallas import tpu as pltpu

F32 = jnp.float32
BF16 = jnp.bfloat16
MESH = pl.DeviceIdType.MESH

NDEV = 8
OTHER_CHIPS = 3
D_MODEL = 1024
SEQ = 2048
D_FF = 2816
EPS = 1e-6
FFN_RES = 0.5

MLA_HEADS = 16
Q_LORA = 384
KV_LORA = 256
QK_NOPE = 64
QK_ROPE = 32
V_HEAD = 64
ROPE_THETA = 10000.0
HEAD_PAD = 128
LAT_PAD = Q_LORA + KV_LORA + HEAD_PAD
MLA_SCALE = (QK_NOPE + QK_ROPE) ** -0.5

DIL_GROUPS = ((128, 1), (512, 4), (2048, 16))
DIL_HEADS = 16
DIL_HEAD_DIM = 64
DIL_BLOCK = 128
DIL_PAIRS = DIL_HEADS // 2
DIL_SCALE = DIL_HEAD_DIM ** -0.5
DIL_GROUPED = 4
N_BUCKETS = 32
MAX_DISTANCE = 2048

ADAM_LR = 0.001
ADAM_B1 = 0.9
ADAM_B2 = 0.999
ADAM_EPS = 1e-08
ADAM_WD = 0.01
ADAM_STEP = 10

V7X_VMEM_BYTES = 64 * 2**20
VMEM_RESERVE = 10 * 2**20
TOKEN_TILE = 512


def _nbytes(shape, dtype):
    return math.prod(shape) * jnp.dtype(dtype).itemsize


def _params(semantics, blocks, extra=0):
    need = 2 * sum(_nbytes(s, d) for s, d in blocks) + extra + VMEM_RESERVE
    return pltpu.CompilerParams(dimension_semantics=semantics,
                                vmem_limit_bytes=int(min(need, V7X_VMEM_BYTES - VMEM_RESERVE)))


def _pcall(body, out_shape, **kw):
    call = pl.pallas_call(body, out_shape=jax.tree.map(lambda s: pltpu.HBM(s.shape, s.dtype), out_shape), **kw)
    return lambda *args: call(*[pltpu.with_memory_space_constraint(a, pltpu.HBM) for a in args])


def _dot_nn(a, b):
    return lax.dot_general(a, b, (((1,), (0,)), ((), ())), preferred_element_type=F32)


def _dot_nt(a, b):
    return lax.dot_general(a, b, (((1,), (1,)), ((), ())), preferred_element_type=F32)


def _dot_tn(a, b):
    return lax.dot_general(a, b, (((0,), (0,)), ((), ())), preferred_element_type=F32)


_DOTS = {"nn": _dot_nn, "nt": _dot_nt, "tn": _dot_tn}


def _rstd(v):
    return lax.rsqrt(jnp.mean(v * v, axis=-1, keepdims=True) + EPS)


def _rms_bwd(v, r, t):
    return r * t - v * (r * r * r) * jnp.mean(t * v, axis=-1, keepdims=True)


_TOKEN_SPEC = pl.BlockSpec((8, 128), lambda *_: (0, 0))


def _mm(pairs, mode, out_dtype, tm, tn, name, out_perm=1, after=None, b_rows=None):
    a0, b0 = pairs[0]
    m_dim = a0.shape[1] if mode == "tn" else a0.shape[0]
    n_dim = b0.shape[0] if mode == "nt" else b0.shape[1]
    tm, tn = min(tm, m_dim // out_perm), min(tn, n_dim)
    assert m_dim % tm == 0 and n_dim % tn == 0, (name, m_dim, n_dim, tm, tn)
    dot = _DOTS[mode]
    npairs = len(pairs)

    def body(*refs):
        acc = None
        for p in range(npairs):
            d = dot(refs[2 * p][...].astype(BF16), refs[2 * p + 1][...].astype(BF16))
            acc = d if acc is None else acc + d
        refs[-1][...] = acc.astype(out_dtype)

    in_specs, blocks, flat = [], [], []
    for n_pair, (a, b) in enumerate(pairs):
        if mode == "nn":
            k = a.shape[1]
            first_block = 0 if b_rows is None else b_rows[n_pair] // k
            sa, sb = ((tm, k), lambda i, j: (i, 0)), ((k, tn), lambda i, j, o=first_block: (o, j))
        elif mode == "nt":
            k = a.shape[1]
            sa, sb = ((tm, k), lambda i, j: (i, 0)), ((tn, k), lambda i, j: (j, 0))
        else:
            k = a.shape[0]
            sa, sb = ((k, tm), lambda i, j: (0, i)), ((k, tn), lambda i, j: (0, j))
        in_specs += [pl.BlockSpec(*sa), pl.BlockSpec(*sb)]
        blocks += [(sa[0], a.dtype), (sb[0], b.dtype)]
        flat += [a, b]
    if after is not None:
        in_specs.append(_TOKEN_SPEC)
        flat.append(after)
    if out_perm == 1:
        out_shape = (m_dim, n_dim)
        out_spec = pl.BlockSpec((tm, tn), lambda i, j: (i, j))
    else:
        rows = m_dim // out_perm
        assert tn == n_dim and rows % tm == 0, (name, rows, tm)
        nb = rows // tm
        out_shape = (rows, out_perm * n_dim)
        out_spec = pl.BlockSpec((tm, n_dim), lambda i, j: (i % nb, i // nb))
    blocks.append(((tm, tn), out_dtype))
    res = _pcall(
        body, out_shape=jax.ShapeDtypeStruct(out_shape, out_dtype), grid=(m_dim // tm, n_dim // tn),
        in_specs=in_specs, out_specs=out_spec, name=name,
        compiler_params=_params(("parallel", "parallel"), blocks, extra=2 * tm * tn * 4),
    )(*flat)
    return res.reshape(m_dim, n_dim)


def _prenorm_mm(x, pre_g, scale, shift, w, w_mode, out_dtype, tn, name, perm=1, w_rows=None):
    s_dim, d_dim = x.shape
    n_dim = w.shape[0] if w_mode == "nt" else w.shape[1]
    w_first = 0
    if w_rows is not None:
        w_first, n_dim = w_rows
    rows = s_dim // perm
    tm = min(TOKEN_TILE, rows)
    nb = rows // tm
    tn = min(tn, n_dim)
    assert n_dim % tn == 0 and w_first % tn == 0
    w_block0 = w_first // tn
    dot = _DOTS[w_mode]

    def body(x_ref, g_ref, sc_ref, sh_ref, w_ref, hn_ref, o_ref):
        @pl.when(pl.program_id(1) == 0)
        def _():
            xf = x_ref[...]
            hn = (xf * _rstd(xf) * g_ref[...]) * (1.0 + sc_ref[...]) + sh_ref[...]
            hn_ref[...] = hn.astype(BF16)

        o_ref[...] = dot(hn_ref[...], w_ref[...]).astype(out_dtype)

    vec = pl.BlockSpec((1, d_dim), lambda i, j: (0, 0))
    w_block = (tn, d_dim) if w_mode == "nt" else (d_dim, tn)
    w_spec = pl.BlockSpec(w_block, (lambda i, j: (w_block0 + j, 0)) if w_mode == "nt" else (lambda i, j: (0, j)))
    hn, out = _pcall(
        body,
        out_shape=(jax.ShapeDtypeStruct((s_dim, d_dim), BF16), jax.ShapeDtypeStruct((s_dim, n_dim), out_dtype)),
        grid=(s_dim // tm, n_dim // tn),
        in_specs=[pl.BlockSpec((tm, d_dim), lambda i, j: (i % nb, i // nb)), vec, vec, vec, w_spec],
        out_specs=(pl.BlockSpec((tm, d_dim), lambda i, j: (i, 0)), pl.BlockSpec((tm, tn), lambda i, j: (i, j))),
        name=name,
        compiler_params=_params(("parallel", "arbitrary"),
                                [((tm, d_dim), F32), (w_block, BF16), ((tm, d_dim), BF16), ((tm, tn), out_dtype)],
                                extra=3 * tm * d_dim * 4 + tm * tn * 4),
    )(x.reshape(rows, perm * d_dim), pre_g, scale, shift, w)
    return hn, out


def _ffn_up(x, pre_g, scale, shift, wg_t, wu_t, name):
    s_dim, d_dim = x.shape
    f_dim = wg_t.shape[0]
    tm, tn = TOKEN_TILE, f_dim // 2

    def body(x_ref, g_ref, sc_ref, sh_ref, wg_ref, wu_ref, hn_ref, go_ref, uo_ref, a_ref):
        @pl.when(pl.program_id(1) == 0)
        def _():
            xf = x_ref[...]
            hn = (xf * _rstd(xf) * g_ref[...]) * (1.0 + sc_ref[...]) + sh_ref[...]
            hn_ref[...] = hn.astype(BF16)

        hn = hn_ref[...]
        g = _dot_nt(hn, wg_ref[...])
        u = _dot_nt(hn, wu_ref[...])
        go_ref[...] = g.astype(BF16)
        uo_ref[...] = u.astype(BF16)
        a_ref[...] = (g * jax.nn.sigmoid(g) * u).astype(BF16)

    vec = pl.BlockSpec((1, d_dim), lambda i, j: (0, 0))
    w_spec = pl.BlockSpec((tn, d_dim), lambda i, j: (j, 0))
    act = pl.BlockSpec((tm, tn), lambda i, j: (i, j))
    act_shape = jax.ShapeDtypeStruct((s_dim, f_dim), BF16)
    return _pcall(
        body,
        out_shape=(jax.ShapeDtypeStruct((s_dim, d_dim), BF16), act_shape, act_shape, act_shape),
        grid=(s_dim // tm, f_dim // tn),
        in_specs=[pl.BlockSpec((tm, d_dim), lambda i, j: (i, 0)), vec, vec, vec, w_spec, w_spec],
        out_specs=(pl.BlockSpec((tm, d_dim), lambda i, j: (i, 0)), act, act, act),
        name=name,
        compiler_params=_params(("parallel", "arbitrary"),
                                [((tm, d_dim), F32), ((tn, d_dim), BF16), ((tn, d_dim), BF16), ((tm, d_dim), BF16)]
                                + 3 * [((tm, tn), BF16)], extra=3 * tm * d_dim * 4 + 4 * tm * tn * 4),
    )(x, pre_g, scale, shift, wg_t, wu_t)


def _mm_post(a, w, x, post_g, gate, res_w, name):
    s_dim, k_dim = a.shape
    d_dim = w.shape[1]
    tm = TOKEN_TILE

    def body(a_ref, w_ref, x_ref, pg_ref, gt_ref, xo_ref, f_ref):
        f = _dot_nn(a_ref[...], w_ref[...])
        y = f * _rstd(f) * pg_ref[...]
        f_ref[...] = f
        xo_ref[...] = x_ref[...] + (res_w * gt_ref[...]) * y

    vec = pl.BlockSpec((1, d_dim), lambda i: (0, 0))
    row = pl.BlockSpec((tm, d_dim), lambda i: (i, 0))
    out = jax.ShapeDtypeStruct((s_dim, d_dim), F32)
    return _pcall(
        body, out_shape=(out, out), grid=(s_dim // tm,),
        in_specs=[pl.BlockSpec((tm, k_dim), lambda i: (i, 0)), pl.BlockSpec((k_dim, d_dim), lambda i: (0, 0)), row, vec, vec],
        out_specs=(row, row), name=name,
        compiler_params=_params(("parallel",), [((tm, k_dim), BF16), ((k_dim, d_dim), BF16)] + 3 * [((tm, d_dim), F32)],
                                extra=3 * tm * d_dim * 4),
    )(a, w, x, post_g, gate)


def _post_bwd(dx_out, f, post_g, gate, res_w, name):
    s_dim, d_dim = f.shape
    tm = TOKEN_TILE

    def body(dx_ref, f_ref, pg_ref, gt_ref, df_ref, dgate_ref, dpost_ref):
        @pl.when(pl.program_id(0) == 0)
        def _():
            dgate_ref[...] = jnp.zeros_like(dgate_ref)
            dpost_ref[...] = jnp.zeros_like(dpost_ref)

        dx, fv = dx_ref[...], f_ref[...]
        r = _rstd(fv)
        fr = fv * r
        dgate_ref[...] += res_w * jnp.sum(dx * (fr * pg_ref[...]), axis=0, keepdims=True)
        dy = (res_w * gt_ref[...]) * dx
        dpost_ref[...] += jnp.sum(dy * fr, axis=0, keepdims=True)
        df_ref[...] = _rms_bwd(fv, r, dy * pg_ref[...]).astype(BF16)

    vec = pl.BlockSpec((1, d_dim), lambda i: (0, 0))
    row = pl.BlockSpec((tm, d_dim), lambda i: (i, 0))
    vshape = jax.ShapeDtypeStruct((1, d_dim), F32)
    return _pcall(
        body, out_shape=(jax.ShapeDtypeStruct((s_dim, d_dim), BF16), vshape, vshape), grid=(s_dim // tm,),
        in_specs=[row, row, vec, vec], out_specs=(row, vec, vec), name=name,
        compiler_params=_params(("arbitrary",), 3 * [((tm, d_dim), F32)], extra=6 * tm * d_dim * 4),
    )(dx_out, f, post_g, gate)


def _prenorm_bwd(dx_out, dhns, x, pre_g, scale, name):
    s_dim, d_dim = x.shape
    tm = TOKEN_TILE
    n_in = len(dhns)

    def body(*refs):
        dx_ref, x_ref, pg_ref, sc_ref = refs[n_in + 0], refs[n_in + 1], refs[n_in + 2], refs[n_in + 3]
        dxo_ref, dsh_ref, dsc_ref, dpg_ref = refs[n_in + 4:]

        @pl.when(pl.program_id(0) == 0)
        def _():
            dsh_ref[...] = jnp.zeros_like(dsh_ref)
            dsc_ref[...] = jnp.zeros_like(dsc_ref)
            dpg_ref[...] = jnp.zeros_like(dpg_ref)

        dhn = refs[0][...]
        for k in range(1, n_in):
            dhn = dhn + refs[k][...]
        xv = x_ref[...]
        r = _rstd(xv)
        xr = xv * r
        dsh_ref[...] += jnp.sum(dhn, axis=0, keepdims=True)
        dsc_ref[...] += jnp.sum(dhn * (xr * pg_ref[...]), axis=0, keepdims=True)
        dn = dhn * (1.0 + sc_ref[...])
        dpg_ref[...] += jnp.sum(dn * xr, axis=0, keepdims=True)
        dxo_ref[...] = dx_ref[...] + _rms_bwd(xv, r, dn * pg_ref[...])

    vec = pl.BlockSpec((1, d_dim), lambda i: (0, 0))
    row = pl.BlockSpec((tm, d_dim), lambda i: (i, 0))
    vshape = jax.ShapeDtypeStruct((1, d_dim), F32)
    return _pcall(
        body, out_shape=(jax.ShapeDtypeStruct((s_dim, d_dim), F32), vshape, vshape, vshape), grid=(s_dim // tm,),
        in_specs=n_in * [row] + [row, row, vec, vec], out_specs=(row, vec, vec, vec), name=name,
        compiler_params=_params(("arbitrary",), (n_in + 3) * [((tm, d_dim), F32)], extra=6 * tm * d_dim * 4),
    )(*dhns, dx_out, x, pre_g, scale)


def _ffn_dgu(df, wd, g, u, name, after=None):
    s_dim, d_dim = df.shape
    f_dim = wd.shape[0]
    tm, tn = TOKEN_TILE, f_dim // 2

    def body(df_ref, wd_ref, g_ref, u_ref, *rest):
        dg_ref, du_ref = rest[-2:]
        da = _dot_nt(df_ref[...], wd_ref[...])
        gv, uv = g_ref[...].astype(F32), u_ref[...].astype(F32)
        sg = jax.nn.sigmoid(gv)
        du_ref[...] = (da * (gv * sg)).astype(BF16)
        dg_ref[...] = (da * uv * (sg * (1.0 + gv * (1.0 - sg)))).astype(BF16)

    act = pl.BlockSpec((tm, tn), lambda i, j: (i, j))
    act_shape = jax.ShapeDtypeStruct((s_dim, f_dim), BF16)
    token = [] if after is None else [after]
    return _pcall(
        body, out_shape=(act_shape, act_shape), grid=(s_dim // tm, f_dim // tn),
        in_specs=[pl.BlockSpec((tm, d_dim), lambda i, j: (i, 0)), pl.BlockSpec((tn, d_dim), lambda i, j: (j, 0)), act, act]
        + len(token) * [_TOKEN_SPEC],
        out_specs=(act, act), name=name,
        compiler_params=_params(("parallel", "parallel"), [((tm, d_dim), BF16), ((tn, d_dim), BF16)] + 4 * [((tm, tn), BF16)],
                                extra=6 * tm * tn * 4),
    )(df, wd, g, u, *token)


def _ffn_dw(dg, du, a, hn, df, name):
    s_dim, f_dim = dg.shape
    d_dim = hn.shape[1]
    tm = 256

    def body(dg_ref, du_ref, a_ref, hn_ref, df_ref, dwg_ref, dwu_ref, dwd_ref):
        dwg_ref[...] = _dot_tn(dg_ref[...], hn_ref[...]).astype(BF16)
        dwu_ref[...] = _dot_tn(du_ref[...], hn_ref[...]).astype(BF16)
        dwd_ref[...] = _dot_tn(a_ref[...], df_ref[...]).astype(BF16)

    col = pl.BlockSpec((s_dim, tm), lambda i: (0, i))
    full = pl.BlockSpec((s_dim, d_dim), lambda i: (0, 0), pipeline_mode=pl.Buffered(1))
    out = pl.BlockSpec((tm, d_dim), lambda i: (i, 0))
    shape = jax.ShapeDtypeStruct((f_dim, d_dim), BF16)
    need = 2 * s_dim * d_dim * 2 + 2 * 3 * (s_dim * tm * 2 + tm * d_dim * 2) + 3 * tm * d_dim * 4 + 3 * s_dim * tm * 2
    return _pcall(
        body, out_shape=(shape, shape, shape), grid=(f_dim // tm,), in_specs=[col, col, col, full, full],
        out_specs=(out, out, out), name=name,
        compiler_params=pltpu.CompilerParams(dimension_semantics=("parallel",),
                                             vmem_limit_bytes=int(min(need + VMEM_RESERVE, V7X_VMEM_BYTES - VMEM_RESERVE))),
    )(dg, du, a, hn, df)


def _ffn_fwd_fused(x, pre_g, scale, shift, post_g, gate, wg_t, wu_t, wd, name):
    s_dim, d_dim = x.shape
    f_dim = wd.shape[0]
    tm, chunks = 256, 2
    cw = f_dim // chunks

    def body(x_ref, prg_ref, sc_ref, sh_ref, pg_ref, gt_ref, wg_ref, wu_ref, wd_ref, hn_ref, go_ref, uo_ref, a_ref, xo_ref, f_ref):
        xf = x_ref[...]
        hn = ((xf * _rstd(xf) * prg_ref[...]) * (1.0 + sc_ref[...]) + sh_ref[...]).astype(BF16)
        hn_ref[...] = hn
        f = None
        ahead = (_dot_nt(hn, wg_ref[0:cw, :]), _dot_nt(hn, wu_ref[0:cw, :]))
        for c in range(chunks):
            g, u = ahead
            if c + 1 < chunks:
                nxt = slice((c + 1) * cw, (c + 2) * cw)
                ahead = (_dot_nt(hn, wg_ref[nxt, :]), _dot_nt(hn, wu_ref[nxt, :]))
            cols = slice(c * cw, (c + 1) * cw)
            go_ref[:, cols] = g.astype(BF16)
            uo_ref[:, cols] = u.astype(BF16)
            a = (g * jax.nn.sigmoid(g) * u).astype(BF16)
            a_ref[:, cols] = a
            part = _dot_nn(a, wd_ref[cols, :])
            f = part if f is None else f + part
        f_ref[...] = f
        xo_ref[...] = xf + (FFN_RES * gt_ref[...]) * (f * _rstd(f) * pg_ref[...])

    vec = pl.BlockSpec((1, d_dim), lambda i: (0, 0))
    row = pl.BlockSpec((tm, d_dim), lambda i: (i, 0))
    act = pl.BlockSpec((tm, f_dim), lambda i: (i, 0))
    weight = pl.BlockSpec((f_dim, d_dim), lambda i: (0, 0), pipeline_mode=pl.Buffered(1))
    act_shape = jax.ShapeDtypeStruct((s_dim, f_dim), BF16)
    res_shape = jax.ShapeDtypeStruct((s_dim, d_dim), F32)
    need = (3 * f_dim * d_dim * 2 + 2 * tm * d_dim * 4 + 2 * (tm * d_dim * 2 + 3 * tm * f_dim * 2 + 2 * tm * d_dim * 4)
            + 8 * tm * cw * 4 + 4 * tm * d_dim * 4)
    return _pcall(
        body, out_shape=(jax.ShapeDtypeStruct((s_dim, d_dim), BF16), act_shape, act_shape, act_shape, res_shape, res_shape),
        grid=(s_dim // tm,), in_specs=[row, vec, vec, vec, vec, vec, weight, weight, weight],
        out_specs=(row, act, act, act, row, row), name=name,
        compiler_params=pltpu.CompilerParams(dimension_semantics=("parallel",),
                                             vmem_limit_bytes=int(min(need + VMEM_RESERVE, V7X_VMEM_BYTES - VMEM_RESERVE))),
    )(x, pre_g, scale, shift, post_g, gate, wg_t, wu_t, wd)


def _ffn_bwd_fused(dx_out, saved, pre_g, post_g, scale, gate, wg_t, wu_t, wd, name):
    x, _, g, u, _, f = saved
    s_dim, d_dim = x.shape
    f_dim = wd.shape[0]
    tm, chunks = 256, 2
    cw = f_dim // chunks

    def body(dx_ref, f_ref, g_ref, u_ref, x_ref, pg_ref, gt_ref, prg_ref, sc_ref, wd_ref, wg_ref, wu_ref,
             df_ref, dg_ref, du_ref, dxo_ref, dgate_ref, dpost_ref, dsh_ref, dsc_ref, dpg_ref):
        @pl.when(pl.program_id(0) == 0)
        def _():
            for acc in (dgate_ref, dpost_ref, dsh_ref, dsc_ref, dpg_ref):
                acc[...] = jnp.zeros_like(acc)

        dx, fv = dx_ref[...], f_ref[...]
        r = _rstd(fv)
        fr = fv * r
        dgate_ref[...] += FFN_RES * jnp.sum(dx * (fr * pg_ref[...]), axis=0, keepdims=True)
        dy = (FFN_RES * gt_ref[...]) * dx
        dpost_ref[...] += jnp.sum(dy * fr, axis=0, keepdims=True)
        df = _rms_bwd(fv, r, dy * pg_ref[...]).astype(BF16)
        df_ref[...] = df
        dhn = None
        ahead = _dot_nt(df, wd_ref[0:cw, :])
        for c in range(chunks):
            da = ahead
            if c + 1 < chunks:
                ahead = _dot_nt(df, wd_ref[(c + 1) * cw:(c + 2) * cw, :])
            cols = slice(c * cw, (c + 1) * cw)
            gv, uv = g_ref[:, cols].astype(F32), u_ref[:, cols].astype(F32)
            sg = jax.nn.sigmoid(gv)
            du = (da * (gv * sg)).astype(BF16)
            dg = (da * uv * (sg * (1.0 + gv * (1.0 - sg)))).astype(BF16)
            dg_ref[:, cols] = dg
            du_ref[:, cols] = du
            part = _dot_nn(dg, wg_ref[cols, :]) + _dot_nn(du, wu_ref[cols, :])
            dhn = part if dhn is None else dhn + part
        xv = x_ref[...]
        rx = _rstd(xv)
        xr = xv * rx
        dsh_ref[...] += jnp.sum(dhn, axis=0, keepdims=True)
        dsc_ref[...] += jnp.sum(dhn * (xr * prg_ref[...]), axis=0, keepdims=True)
        dn = dhn * (1.0 + sc_ref[...])
        dpg_ref[...] += jnp.sum(dn * xr, axis=0, keepdims=True)
        dxo_ref[...] = dx + _rms_bwd(xv, rx, dn * prg_ref[...])

    vec = pl.BlockSpec((1, d_dim), lambda i: (0, 0))
    row = pl.BlockSpec((tm, d_dim), lambda i: (i, 0))
    act = pl.BlockSpec((tm, f_dim), lambda i: (i, 0))
    weight = pl.BlockSpec((f_dim, d_dim), lambda i: (0, 0), pipeline_mode=pl.Buffered(1))
    vshape = jax.ShapeDtypeStruct((1, d_dim), F32)
    act_shape = jax.ShapeDtypeStruct((s_dim, f_dim), BF16)
    need = (3 * f_dim * d_dim * 2 + 2 * (3 * tm * d_dim * 4 + 2 * tm * f_dim * 2) + 2 * (tm * d_dim * 2 + 2 * tm * f_dim * 2 + tm * d_dim * 4)
            + 6 * tm * cw * 4 + 6 * tm * d_dim * 4)
    return _pcall(
        body, out_shape=(jax.ShapeDtypeStruct((s_dim, d_dim), BF16), act_shape, act_shape, jax.ShapeDtypeStruct((s_dim, d_dim), F32),
                         vshape, vshape, vshape, vshape, vshape),
        grid=(s_dim // tm,), in_specs=[row, row, act, act, row, vec, vec, vec, vec, weight, weight, weight],
        out_specs=(row, act, act, row, vec, vec, vec, vec, vec), name=name,
        compiler_params=pltpu.CompilerParams(dimension_semantics=("arbitrary",),
                                             vmem_limit_bytes=int(min(need + VMEM_RESERVE, V7X_VMEM_BYTES - VMEM_RESERVE))),
    )(dx_out, f, g, u, x, post_g, gate, pre_g, scale, wd, wg_t, wu_t)


def _rope_tables(zero=0.0):
    half = QK_ROPE // 2
    freqs = ROPE_THETA ** (-jnp.arange(half, dtype=F32) / half)
    ang = (jnp.arange(SEQ, dtype=F32)[:, None] + zero) * freqs[None, :]
    cos, sin = jnp.cos(ang), jnp.sin(ang)
    ones = jnp.ones((SEQ, QK_NOPE), F32)
    zeros = jnp.zeros((SEQ, QK_NOPE), F32)
    pad1 = jnp.ones((SEQ, HEAD_PAD - QK_NOPE - QK_ROPE), F32)
    pad0 = jnp.zeros((SEQ, HEAD_PAD - QK_NOPE - QK_ROPE), F32)
    zh = jnp.zeros((SEQ, half), F32)
    c = jnp.concatenate([ones, cos, cos, pad1], axis=1)
    s1 = jnp.concatenate([zeros, -sin, zh, pad0], axis=1)
    s2 = jnp.concatenate([zeros, zh, sin, pad0], axis=1)
    return c, s1, s2


def _rope(v, c, s1, s2):
    half = QK_ROPE // 2
    return v * c + pltpu.roll(v, HEAD_PAD - half, 1) * s1 + pltpu.roll(v, half, 1) * s2


def _rope_t(dv, c, s1, s2):
    half = QK_ROPE // 2
    return dv * c + pltpu.roll(dv * s1, half, 1) + pltpu.roll(dv * s2, HEAD_PAD - half, 1)


def _mla_qkv(lat, q_norm, kv_norm, wq_t, wkv_t, rope, name):
    s_dim = lat.shape[0]
    width = MLA_HEADS * HEAD_PAD
    tm = 256

    def body(lat_ref, qg_ref, kg_ref, wq_ref, wkv_ref, c_ref, s1_ref, s2_ref, q_ref, k_ref, v_ref, qn_ref, kvn_ref):
        cq = lat_ref[:, :Q_LORA]
        ckv = lat_ref[:, Q_LORA:Q_LORA + KV_LORA]
        kr = lat_ref[:, Q_LORA + KV_LORA:]
        c, s1, s2 = c_ref[...], s1_ref[...], s2_ref[...]
        qn = (cq * _rstd(cq) * qg_ref[...]).astype(BF16)
        kvn = (ckv * _rstd(ckv) * kg_ref[...]).astype(BF16)
        qn_ref[...] = qn
        kvn_ref[...] = kvn
        q = _dot_nt(qn, wq_ref[...])
        kv = _dot_nt(kvn, wkv_ref[...])
        krr = _rope(kr, c, s1, s2)
        low = lax.broadcasted_iota(jnp.int32, (tm, HEAD_PAD), 1) < QK_NOPE
        for h in range(MLA_HEADS):
            sl = slice(h * HEAD_PAD, (h + 1) * HEAD_PAD)
            q_ref[:, sl] = _rope(q[:, sl], c, s1, s2).astype(BF16)
            kvh = kv[:, sl]
            k_ref[:, sl] = (jnp.where(low, kvh, 0.0) + krr).astype(BF16)
            v_ref[:, sl] = jnp.where(low, 0.0, kvh).astype(BF16)

    row = lambda n: pl.BlockSpec((tm, n), lambda i: (i, 0))
    full = lambda a: pl.BlockSpec(a.shape, lambda i: (0, 0))
    wide = jax.ShapeDtypeStruct((s_dim, width), BF16)
    return _pcall(
        body,
        out_shape=(wide, wide, wide, jax.ShapeDtypeStruct((s_dim, Q_LORA), BF16), jax.ShapeDtypeStruct((s_dim, KV_LORA), BF16)),
        grid=(s_dim // tm,),
        in_specs=[row(LAT_PAD), full(q_norm), full(kv_norm), full(wq_t), full(wkv_t), row(HEAD_PAD), row(HEAD_PAD), row(HEAD_PAD)],
        out_specs=(row(width), row(width), row(width), row(Q_LORA), row(KV_LORA)), name=name,
        compiler_params=_params(("parallel",), [((tm, LAT_PAD), F32), (wq_t.shape, BF16), (wkv_t.shape, BF16)]
                                + 3 * [((tm, width), BF16)], extra=4 * tm * width * 4),
    )(lat, q_norm, kv_norm, wq_t, wkv_t, *rope)


def _mla_scores(q, k_ref, t, tq):
    lo = t * tq
    own = slice(lo, lo + tq)
    scores = [(_dot_nt(q, k_ref[own, :]), own)]
    if t > 0:
        scores.append((_dot_nt(q, k_ref[0:lo, :]), slice(0, lo)))
    return scores


def _mla_softmax(scores):
    s_own = scores[0][0] * MLA_SCALE
    rows = lax.broadcasted_iota(jnp.int32, s_own.shape, 0)
    cols = lax.broadcasted_iota(jnp.int32, s_own.shape, 1)
    s_own = jnp.where(cols <= rows, s_own, -jnp.inf)
    mx = jnp.max(s_own, axis=-1, keepdims=True)
    if len(scores) == 1:
        e_own = jnp.exp(s_own - mx)
        return [(e_own * (1.0 / jnp.sum(e_own, axis=-1, keepdims=True)), scores[0][1])]
    s_pre = scores[1][0] * MLA_SCALE
    mx = jnp.maximum(mx, jnp.max(s_pre, axis=-1, keepdims=True))
    e_own, e_pre = jnp.exp(s_own - mx), jnp.exp(s_pre - mx)
    inv = 1.0 / (jnp.sum(e_own, axis=-1, keepdims=True) + jnp.sum(e_pre, axis=-1, keepdims=True))
    return [(e_pre * inv, scores[1][1]), (e_own * inv, scores[0][1])]


def _mla_attn_fwd(q, k, v, name):
    s_dim = q.shape[0]
    tq = 512

    def body(q_ref, k_ref, v_ref, o_ref):
        n_tiles = s_dim // tq
        tile_of = lambda t: slice(t * tq, (t + 1) * tq)
        def weighted_values(t, probs):
            o = None
            for p, keys in probs:
                part = _dot_nn(p, v_ref[keys, :])
                o = part if o is None else o + part
            o_ref[tile_of(t), :] = o.astype(BF16)

        scores = _mla_scores(q_ref[tile_of(0), :], k_ref, 0, tq)
        probs = None
        for t in range(n_tiles):
            ahead = _mla_scores(q_ref[tile_of(t + 1), :], k_ref, t + 1, tq) if t + 1 < n_tiles else None
            if probs is not None:
                weighted_values(t - 1, probs)
            probs = [(p.astype(BF16), keys) for p, keys in _mla_softmax(scores)]
            scores = ahead
        weighted_values(n_tiles - 1, probs)

    head = pl.BlockSpec((s_dim, HEAD_PAD), lambda h: (0, h))
    return _pcall(
        body, out_shape=jax.ShapeDtypeStruct(q.shape, BF16), grid=(MLA_HEADS,),
        in_specs=[head, head, head], out_specs=head, name=name,
        compiler_params=_params(("parallel",), 4 * [((s_dim, HEAD_PAD), BF16)], extra=4 * tq * s_dim * 4),
    )(q, k, v)


def _mla_attn_bwd(q, k, v, d_o, name):
    s_dim = q.shape[0]
    tq = 512

    def body(q_ref, k_ref, v_ref, do_ref, dq_ref, dk_ref, dv_ref):
        dk_ref[...] = jnp.zeros_like(dk_ref)
        dv_ref[...] = jnp.zeros_like(dv_ref)
        n_tiles = s_dim // tq
        tile_of = lambda t: slice(t * tq, (t + 1) * tq)

        def products(t):
            scores = _mla_scores(q_ref[tile_of(t), :], k_ref, t, tq)
            dot = do_ref[tile_of(t), :].astype(BF16)
            return scores, [_dot_nt(dot, v_ref[keys, :]) for _, keys in scores]

        def gradients_of_scores(scores, dps):
            probs = _mla_softmax(scores)
            dp_of = {(keys.start, keys.stop): dp for (_, keys), dp in zip(scores, dps)}
            terms = [(p, keys, dp_of[keys.start, keys.stop]) for p, keys in probs]
            row = None
            for p, _, dp in terms:
                part = jnp.sum(p * dp, axis=-1, keepdims=True)
                row = part if row is None else row + part
            return [((p * (dp - row) * MLA_SCALE).astype(BF16), p.astype(BF16), keys) for p, keys, dp in terms]

        def accumulate(t, terms):
            qt = q_ref[tile_of(t), :]
            dot = do_ref[tile_of(t), :].astype(BF16)
            dq = None
            for dsb, pb, keys in terms:
                part = _dot_nn(dsb, k_ref[keys, :])
                dq = part if dq is None else dq + part
                dk_ref[keys, :] += _dot_tn(dsb, qt)
                dv_ref[keys, :] += _dot_tn(pb, dot)
            dq_ref[tile_of(t), :] = dq

        ready = products(0)
        terms = None
        for t in range(n_tiles):
            ahead = products(t + 1) if t + 1 < n_tiles else None
            if terms is not None:
                accumulate(t - 1, terms)
            terms = gradients_of_scores(*ready)
            ready = ahead
        accumulate(n_tiles - 1, terms)

    head = pl.BlockSpec((s_dim, HEAD_PAD), lambda h: (0, h))
    out = jax.ShapeDtypeStruct(q.shape, F32)
    return _pcall(
        body, out_shape=(out, out, out), grid=(MLA_HEADS,),
        in_specs=[head, head, head, head], out_specs=(head, head, head), name=name,
        compiler_params=_params(("parallel",), 3 * [((s_dim, HEAD_PAD), BF16)] + 4 * [((s_dim, HEAD_PAD), F32)],
                                extra=6 * tq * s_dim * 4),
    )(q, k, v, d_o)


def _mla_qkv_bwd(dq, dk, dv, lat, q_norm, kv_norm, wq_t, wkv_t, rope, name):
    s_dim = lat.shape[0]
    width = MLA_HEADS * HEAD_PAD
    tm = 256

    def body(dq_ref, dk_ref, dv_ref, lat_ref, qg_ref, kg_ref, wq_ref, wkv_ref, c_ref, s1_ref, s2_ref,
             dqp_ref, dkv_ref, dlat_ref, dqg_ref, dkg_ref):
        @pl.when(pl.program_id(0) == 0)
        def _():
            dqg_ref[...] = jnp.zeros_like(dqg_ref)
            dkg_ref[...] = jnp.zeros_like(dkg_ref)

        c, s1, s2 = c_ref[...], s1_ref[...], s2_ref[...]
        lane = lax.broadcasted_iota(jnp.int32, (tm, HEAD_PAD), 1)
        low = lane < QK_NOPE
        rot = (lane >= QK_NOPE) & (lane < QK_NOPE + QK_ROPE)
        dkrr = jnp.zeros((tm, HEAD_PAD), F32)
        for h in range(MLA_HEADS):
            sl = slice(h * HEAD_PAD, (h + 1) * HEAD_PAD)
            dqp_ref[:, sl] = _rope_t(dq_ref[:, sl], c, s1, s2).astype(BF16)
            dkh = dk_ref[:, sl]
            dkv_ref[:, sl] = jnp.where(low, dkh, dv_ref[:, sl]).astype(BF16)
            dkrr = dkrr + jnp.where(rot, dkh, 0.0)
        dqn = _dot_nn(dqp_ref[...], wq_ref[...])
        dkvn = _dot_nn(dkv_ref[...], wkv_ref[...])
        cq = lat_ref[:, :Q_LORA]
        ckv = lat_ref[:, Q_LORA:Q_LORA + KV_LORA]
        rq, rkv = _rstd(cq), _rstd(ckv)
        dqg_ref[...] += jnp.sum(dqn * cq * rq, axis=0, keepdims=True)
        dkg_ref[...] += jnp.sum(dkvn * ckv * rkv, axis=0, keepdims=True)
        dlat_ref[:, :Q_LORA] = _rms_bwd(cq, rq, dqn * qg_ref[...])
        dlat_ref[:, Q_LORA:Q_LORA + KV_LORA] = _rms_bwd(ckv, rkv, dkvn * kg_ref[...])
        dlat_ref[:, Q_LORA + KV_LORA:] = _rope_t(dkrr, c, s1, s2)

    row = lambda n: pl.BlockSpec((tm, n), lambda i: (i, 0))
    full = lambda a: pl.BlockSpec(a.shape, lambda i: (0, 0))
    wide = jax.ShapeDtypeStruct((s_dim, width), BF16)
    return _pcall(
        body,
        out_shape=(wide, wide, jax.ShapeDtypeStruct((s_dim, LAT_PAD), F32),
                   jax.ShapeDtypeStruct(q_norm.shape, F32), jax.ShapeDtypeStruct(kv_norm.shape, F32)),
        grid=(s_dim // tm,),
        in_specs=[row(width), row(width), row(width), row(LAT_PAD), full(q_norm), full(kv_norm), full(wq_t), full(wkv_t),
                  row(HEAD_PAD), row(HEAD_PAD), row(HEAD_PAD)],
        out_specs=(row(width), row(width), row(LAT_PAD), full(q_norm), full(kv_norm)), name=name,
        compiler_params=_params(("arbitrary",), 3 * [((tm, width), F32)] + [((tm, LAT_PAD), F32), (wq_t.shape, BF16),
                                                                           (wkv_t.shape, BF16)] + 2 * [((tm, width), BF16)],
                                extra=2 * tm * width * 4),
    )(dq, dk, dv, lat, q_norm, kv_norm, wq_t, wkv_t, *rope)


def _t5_bucket(dist):
    max_exact = N_BUCKETS // 2
    d = jnp.maximum(dist, 1).astype(F32)
    large = max_exact + (jnp.log(d / max_exact) / math.log(MAX_DISTANCE / max_exact)
                         * (N_BUCKETS - max_exact)).astype(jnp.int32)
    large = jnp.minimum(large, N_BUCKETS - 1)
    return jnp.where(dist < max_exact, dist, large)


def _dil_buckets(dilation):
    iq = jnp.arange(DIL_BLOCK)[:, None]
    ik = jnp.arange(2 * DIL_BLOCK)[None, :]
    return _t5_bucket(jnp.maximum(DIL_BLOCK + iq - ik, 0) * dilation)


def _dil_logits(qh, kb, bias_h, first, span):
    if first:
        s = _dot_nt(qh, kb) * DIL_SCALE + bias_h[:, DIL_BLOCK:]
        rel = lax.broadcasted_iota(jnp.int32, s.shape, 0) - lax.broadcasted_iota(jnp.int32, s.shape, 1)
    else:
        s = _dot_nt(qh, kb) * DIL_SCALE + bias_h
        rel = DIL_BLOCK + lax.broadcasted_iota(jnp.int32, s.shape, 0) - lax.broadcasted_iota(jnp.int32, s.shape, 1)
    return jnp.where((rel >= 0) & (rel <= span), s, -jnp.inf)


def _dil_blocks(s_dim, dilation):
    rows = s_dim // dilation
    for r in range(dilation):
        for n in range(rows // DIL_BLOCK):
            lo = r * rows + n * DIL_BLOCK
            keys = slice(lo, lo + DIL_BLOCK) if n == 0 else slice(lo - DIL_BLOCK, lo + DIL_BLOCK)
            start = r + n * DIL_BLOCK * dilation
            tokens = slice(start, start + DIL_BLOCK) if dilation == 1 else pl.ds(start, DIL_BLOCK, stride=dilation)
            yield n == 0, slice(lo, lo + DIL_BLOCK), keys, tokens


def _dil_views(s_dim):
    col = lambda which: pl.BlockSpec((s_dim, HEAD_PAD), lambda p: (0, which * DIL_PAIRS + p))
    nat = pl.BlockSpec((s_dim, HEAD_PAD), lambda p: (0, p))
    bias = pl.BlockSpec((2, DIL_BLOCK, 2 * DIL_BLOCK), lambda p: (p, 0, 0))
    return col, nat, bias


def _dil_attn_fwd(qkv, bias, dilation, span, name):
    s_dim = qkv.shape[0]
    d_dim = DIL_HEADS * DIL_HEAD_DIM
    col, nat, bias_spec = _dil_views(s_dim)

    def body(q_ref, k_ref, v_ref, b_ref, o_ref, l_ref):
        lane = lax.broadcasted_iota(jnp.int32, (DIL_BLOCK, HEAD_PAD), 1)
        klane = lax.broadcasted_iota(jnp.int32, (2 * DIL_BLOCK, HEAD_PAD), 1)
        blocks = list(_dil_blocks(s_dim, dilation))
        for g0 in range(0, len(blocks), DIL_GROUPED):
            group = blocks[g0:g0 + DIL_GROUPED]
            logits = [_dil_logits(jnp.where((lane < DIL_HEAD_DIM) == (h == 0), q_ref[blk, :], 0), k_ref[keys, :], b_ref[h],
                                  first, span) for first, blk, keys, _ in group for h in range(2)]
            soft = []
            for lg in logits:
                mx = jnp.max(lg, axis=-1, keepdims=True)
                e = jnp.exp(lg - mx)
                tot = jnp.sum(e, axis=-1, keepdims=True)
                soft.append(((e * (1.0 / tot)).astype(BF16), mx + jnp.log(tot)))
            for i, (_, _, keys, tokens) in enumerate(group):
                vb = v_ref[keys, :]
                o_acc = jnp.zeros((DIL_BLOCK, HEAD_PAD), F32)
                lse_acc = jnp.zeros((DIL_BLOCK, HEAD_PAD), F32)
                for h in range(2):
                    p, lse = soft[2 * i + h]
                    kmine = (klane[:vb.shape[0]] < DIL_HEAD_DIM) == (h == 0)
                    o_acc = o_acc + _dot_nn(p, jnp.where(kmine, vb, 0))
                    lse_acc = jnp.where((lane < DIL_HEAD_DIM) == (h == 0), lse, lse_acc)
                o_ref[tokens, :] = o_acc
                l_ref[tokens, :] = lse_acc

    out = jax.ShapeDtypeStruct((s_dim, d_dim), F32)
    return _pcall(
        body, out_shape=(out, out), grid=(DIL_PAIRS,),
        in_specs=[col(0), col(1), col(2), bias_spec], out_specs=(nat, nat), name=name,
        compiler_params=_params(("parallel",), 3 * [((s_dim, HEAD_PAD), BF16)] + 2 * [((s_dim, HEAD_PAD), F32)]
                                + [((2, DIL_BLOCK, 2 * DIL_BLOCK), F32)], extra=2**21),
    )(qkv, qkv, qkv, bias)


def _dil_mix(lses, outs, name):
    s_dim, d_dim = outs[0].shape
    tm = TOKEN_TILE
    ng = len(outs)

    def body(*refs):
        ls = [refs[g][...] for g in range(ng)]
        mx = ls[0]
        for g in range(1, ng):
            mx = jnp.maximum(mx, ls[g])
        es = [jnp.exp(l - mx) for l in ls]
        tot = es[0]
        for g in range(1, ng):
            tot = tot + es[g]
        o = None
        for g in range(ng):
            al = es[g] / tot
            refs[2 * ng + g][...] = al
            t = al * refs[ng + g][...]
            o = t if o is None else o + t
        refs[3 * ng][...] = o
        refs[3 * ng + 1][...] = o.astype(BF16)

    row = pl.BlockSpec((tm, d_dim), lambda i: (i, 0))
    f = jax.ShapeDtypeStruct((s_dim, d_dim), F32)
    res = _pcall(
        body, out_shape=tuple(ng * [f] + [f, jax.ShapeDtypeStruct((s_dim, d_dim), BF16)]), grid=(s_dim // tm,),
        in_specs=2 * ng * [row], out_specs=tuple((ng + 2) * [row]), name=name,
        compiler_params=_params(("parallel",), (3 * ng + 2) * [((tm, d_dim), F32)], extra=4 * tm * d_dim * 4),
    )(*lses, *outs)
    return res[:ng], res[ng], res[ng + 1]


def _dil_attn_bwd(qkv, bias, d_o, o_mix, alpha, lse, dilation, span, name):
    s_dim = qkv.shape[0]
    d_dim = DIL_HEADS * DIL_HEAD_DIM
    col, nat, bias_spec = _dil_views(s_dim)

    def body(q_ref, k_ref, v_ref, b_ref, do_ref, om_ref, al_ref, l_ref, dq_ref, dk_ref, dv_ref, db_ref, dk_acc, dv_acc):
        db_ref[...] = jnp.zeros_like(db_ref)
        dk_acc[...] = jnp.zeros_like(dk_acc)
        dv_acc[...] = jnp.zeros_like(dv_acc)
        lane = lax.broadcasted_iota(jnp.int32, (DIL_BLOCK, HEAD_PAD), 1)
        klane = lax.broadcasted_iota(jnp.int32, (2 * DIL_BLOCK, HEAD_PAD), 1)
        blocks = list(_dil_blocks(s_dim, dilation))
        heads = [(lane < DIL_HEAD_DIM) == (h == 0) for h in range(2)]
        for g0 in range(0, len(blocks), DIL_GROUPED):
            group = blocks[g0:g0 + DIL_GROUPED]
            staged = []
            for first, blk, kv_rows, tokens in group:
                qb, kb, vb = q_ref[blk, :], k_ref[kv_rows, :], v_ref[kv_rows, :]
                dog = al_ref[tokens, :] * do_ref[tokens, :]
                row_term = dog * om_ref[tokens, :]
                lse_b = l_ref[tokens, :]
                for h in range(2):
                    qh = jnp.where(heads[h], qb, 0)
                    dogh = jnp.where(heads[h], dog, 0.0).astype(BF16)
                    staged.append((_dil_logits(qh, kb, b_ref[h], first, span), _dot_nt(dogh, vb), qh, dogh,
                                   jnp.max(jnp.where(heads[h], lse_b, -jnp.inf), axis=-1, keepdims=True),
                                   jnp.sum(jnp.where(heads[h], row_term, 0.0), axis=-1, keepdims=True)))
            grads = []
            for i, (logits, dp, qh, dogh, lse_h, row) in enumerate(staged):
                p = jnp.exp(logits - lse_h)
                ds = p * (dp - row)
                if group[i // 2][0]:
                    db_ref[i % 2, :, DIL_BLOCK:] += ds
                else:
                    db_ref[i % 2] += ds
                grads.append(((ds * DIL_SCALE).astype(BF16), p.astype(BF16), qh, dogh))
            for i, (_, blk, kv_rows, _) in enumerate(group):
                kb = k_ref[kv_rows, :]
                dq_acc = jnp.zeros((DIL_BLOCK, HEAD_PAD), F32)
                dk_blk = jnp.zeros((kb.shape[0], HEAD_PAD), F32)
                dv_blk = jnp.zeros((kb.shape[0], HEAD_PAD), F32)
                for h in range(2):
                    dsb, pb, qh, dogh = grads[2 * i + h]
                    kmine = (klane[:kb.shape[0]] < DIL_HEAD_DIM) == (h == 0)
                    dq_acc = dq_acc + _dot_nn(dsb, jnp.where(kmine, kb, 0))
                    dk_blk = dk_blk + _dot_tn(dsb, qh)
                    dv_blk = dv_blk + _dot_tn(pb, dogh)
                dq_ref[blk, :] = dq_acc.astype(BF16)
                dk_acc[kv_rows, :] += dk_blk
                dv_acc[kv_rows, :] += dv_blk
        dk_ref[...] = dk_acc[...].astype(BF16)
        dv_ref[...] = dv_acc[...].astype(BF16)

    grad = jax.ShapeDtypeStruct((s_dim, d_dim), BF16)
    return _pcall(
        body, out_shape=(grad, grad, grad, jax.ShapeDtypeStruct(bias.shape, F32)), grid=(DIL_PAIRS,),
        in_specs=[col(0), col(1), col(2), bias_spec, nat, nat, nat, nat],
        out_specs=(nat, nat, nat, bias_spec), name=name,
        scratch_shapes=[pltpu.VMEM((s_dim, HEAD_PAD), F32), pltpu.VMEM((s_dim, HEAD_PAD), F32)],
        compiler_params=_params(("parallel",), 6 * [((s_dim, HEAD_PAD), BF16)] + 4 * [((s_dim, HEAD_PAD), F32)]
                                + 2 * [((2, DIL_BLOCK, 2 * DIL_BLOCK), F32)], extra=2 * s_dim * HEAD_PAD * 4 + 2**21),
    )(qkv, qkv, qkv, bias, d_o, o_mix, alpha, lse)


def _bias_reduce(dbias, buckets, name):
    n_heads = dbias.shape[0]

    def body(db_ref, bk_ref, o_ref):
        ds, bk = db_ref[0], bk_ref[0]
        lane = lax.broadcasted_iota(jnp.int32, (8, HEAD_PAD), 1)
        acc = jnp.zeros((8, HEAD_PAD), F32)
        for b in range(N_BUCKETS):
            acc = jnp.where(lane == b, jnp.sum(jnp.where(bk == b, ds, 0.0)), acc)
        o_ref[0] = acc

    blk = (1, DIL_BLOCK, 2 * DIL_BLOCK)
    return _pcall(
        body, out_shape=jax.ShapeDtypeStruct((n_heads, 8, HEAD_PAD), F32), grid=(n_heads,),
        in_specs=[pl.BlockSpec(blk, lambda h: (h, 0, 0)), pl.BlockSpec(blk, lambda h: (h // DIL_HEADS, 0, 0))],
        out_specs=pl.BlockSpec((1, 8, HEAD_PAD), lambda h: (h, 0, 0)), name=name,
        compiler_params=_params(("parallel",), [(blk, F32), (blk, jnp.int32)], extra=2**20),
    )(dbias, buckets)


def _loss_grad(y, target, name):
    s_dim, d_dim = y.shape
    tm = TOKEN_TILE

    def body(y_ref, t_ref, dy_ref, l_ref):
        @pl.when(pl.program_id(0) == 0)
        def _():
            l_ref[...] = jnp.zeros_like(l_ref)

        err = y_ref[...] - t_ref[...]
        dy_ref[...] = err / d_dim
        sq = (err * err).reshape(tm // 8, 8, d_dim)
        l_ref[...] += 0.5 * jnp.sum(sq, axis=0) / d_dim

    row = pl.BlockSpec((tm, d_dim), lambda i: (i, 0))
    acc = pl.BlockSpec((8, d_dim), lambda i: (0, 0))
    return _pcall(
        body, out_shape=(jax.ShapeDtypeStruct((s_dim, d_dim), F32), jax.ShapeDtypeStruct((8, d_dim), F32)),
        grid=(s_dim // tm,), in_specs=[row, row], out_specs=(row, acc), name=name,
        compiler_params=_params(("arbitrary",), 3 * [((tm, d_dim), F32)], extra=2 * tm * d_dim * 4),
    )(y, target)


def _mod_fwd(c_all, w_mod, b_loc, name):
    depth, d_dim, n = w_mod.shape
    nb = c_all.shape[0]

    def body(c_ref, w_ref, b_ref, o_ref, s_ref):
        cv = c_ref[...]
        sc = cv * jax.nn.sigmoid(cv)
        s_ref[...] = sc
        o_ref[0] = _dot_nn(sc.astype(BF16), w_ref[0].astype(BF16)) + b_ref[0]

    return _pcall(
        body, out_shape=(jax.ShapeDtypeStruct((depth, nb, n), F32), jax.ShapeDtypeStruct((nb, d_dim), F32)), grid=(depth,),
        in_specs=[pl.BlockSpec((nb, d_dim), lambda i: (0, 0)), pl.BlockSpec((1, d_dim, n), lambda i: (i, 0, 0)),
                  pl.BlockSpec((1, 1, n), lambda i: (i, 0, 0))],
        out_specs=(pl.BlockSpec((1, nb, n), lambda i: (i, 0, 0)), pl.BlockSpec((nb, d_dim), lambda i: (0, 0))), name=name,
        compiler_params=_params(("arbitrary",), [((1, d_dim, n), F32)], extra=d_dim * n * 2 + 2**20),
    )(c_all, w_mod, b_loc.reshape(depth, 1, n))


def _sum_parts(parts, name, transpose=False):
    _, rows, cols = parts.shape
    unit = 128 if transpose else 16
    budget = (7 if transpose else 3) * 2**20
    fits = [t for t in range(unit, rows // 2 + 1, unit) if rows % t == 0 and NDEV * t * cols * parts.dtype.itemsize <= budget]
    tr = max(fits) if fits else rows

    def body(p_ref, o_ref):
        acc = p_ref[0].astype(F32)
        for k in range(1, NDEV):
            acc = acc + p_ref[k].astype(F32)
        o_ref[...] = acc.T if transpose else acc

    out_shape, out_block = ((cols, rows), (cols, tr)) if transpose else ((rows, cols), (tr, cols))
    return _pcall(
        body, out_shape=jax.ShapeDtypeStruct(out_shape, F32), grid=(rows // tr,),
        in_specs=[pl.BlockSpec((NDEV, tr, cols), lambda i: (0, i, 0))],
        out_specs=pl.BlockSpec(out_block, (lambda i: (0, i)) if transpose else (lambda i: (i, 0))),
        name=name, compiler_params=_params(("parallel",), [((NDEV, tr, cols), parts.dtype), (out_block, F32)], extra=2**22),
    )(parts)


def _adamw(w, g, m, v, name):
    shape = w.shape
    cols = shape[-1]
    rows = math.prod(shape[:-1])
    tr = rows
    for cand in (2048, 1024, 512, 256, 128, 64, 32, 16, 8):
        if rows % cand == 0 and rows > cand and cand * cols * 4 <= 2**21:
            tr = cand
            break

    def body(w_ref, g_ref, m_ref, v_ref, d_ref, mo_ref, vo_ref):
        gv = g_ref[...]
        mn = ADAM_B1 * m_ref[...] + (1.0 - ADAM_B1) * gv
        vn = ADAM_B2 * v_ref[...] + (1.0 - ADAM_B2) * (gv * gv)
        m_hat = mn / (1.0 - ADAM_B1 ** ADAM_STEP)
        v_hat = vn / (1.0 - ADAM_B2 ** ADAM_STEP)
        d_ref[...] = -ADAM_LR * (m_hat / (jnp.sqrt(v_hat) + ADAM_EPS) + ADAM_WD * w_ref[...])
        mo_ref[...] = mn
        vo_ref[...] = vn

    blk = pl.BlockSpec((tr, cols), lambda i: (i, 0))
    out = jax.ShapeDtypeStruct((rows, cols), F32)
    res = _pcall(
        body, out_shape=(out, out, out), grid=(rows // tr,), in_specs=4 * [blk], out_specs=(blk, blk, blk), name=name,
        compiler_params=_params(("parallel",), 7 * [((tr, cols), F32)], extra=4 * tr * cols * 4),
    )(*(a.reshape(rows, cols) for a in (w, g, m, v)))
    return tuple(r.reshape(shape) for r in res)


def _peers():
    x, y, c = lax.axis_index("x"), lax.axis_index("y"), lax.axis_index("c")
    flip = lambda v, f: 1 - v if f else v
    peers = []
    for f in range(1, NDEV):
        px, py, pc = flip(x, f & 4), flip(y, f & 2), flip(c, f & 1)
        peers.append(((px, py, pc), 4 * px + 2 * py + pc))
    return (x, y, c), 4 * x + 2 * y + c, peers


def _places():
    x, y, c = lax.axis_index("x"), lax.axis_index("y"), lax.axis_index("c")
    place = lambda px, py, pc: ((px, py, pc), 4 * px + 2 * py + pc)
    return place(x, y, c), place(x, y, 1 - c), [place(1 - x, y, c), place(x, 1 - y, c), place(1 - x, 1 - y, c)]


def _exchange(arrs, gather, name):
    n = len(arrs)
    hbm = pl.BlockSpec(memory_space=pltpu.HBM)
    if gather:
        out_shape = [jax.ShapeDtypeStruct((NDEV * a.shape[0], a.shape[1]), a.dtype) for a in arrs]
    else:
        out_shape = [jax.ShapeDtypeStruct((NDEV, a.shape[0] // NDEV, a.shape[1]), a.dtype) for a in arrs]

    def body(*refs):
        ins, outs = refs[:n], refs[n:2 * n]
        send_sems, recv_sems, local_sems = refs[2 * n:]
        me_pos, me, peers = _peers()
        local = []
        for k in range(n):
            rows = arrs[k].shape[0] if gather else arrs[k].shape[0] // NDEV
            if gather:
                src_of = lambda idx: ins[k]
                dst_of = lambda idx: outs[k].at[pl.ds(me * rows, rows)]
                mine = (ins[k], outs[k].at[pl.ds(me * rows, rows)])
            else:
                src_of = lambda idx: ins[k].at[pl.ds(idx * rows, rows)]
                dst_of = lambda idx: outs[k].at[me]
                mine = (ins[k].at[pl.ds(me * rows, rows)], outs[k].at[me])
            cp = pltpu.make_async_copy(mine[0], mine[1], local_sems.at[k])
            cp.start()
            local.append(cp)
            for pos, idx in peers:
                pltpu.make_async_remote_copy(src_ref=src_of(idx), dst_ref=dst_of(idx), send_sem=send_sems.at[k],
                                             recv_sem=recv_sems.at[k], device_id=pos, device_id_type=MESH).start()
        for k in range(n):
            rows = arrs[k].shape[0] if gather else arrs[k].shape[0] // NDEV
            sent = ins[k].at[pl.ds(0, (NDEV - 1) * rows)] if not gather else outs[k].at[pl.ds(0, (NDEV - 1) * rows)]
            got = outs[k].at[pl.ds(0, (NDEV - 1) * rows)] if gather else outs[k].at[pl.ds(0, NDEV - 1)]
            pltpu.make_async_remote_copy(src_ref=sent, dst_ref=sent, send_sem=send_sems.at[k], recv_sem=recv_sems.at[k],
                                         device_id=me_pos, device_id_type=MESH).wait_send()
            pltpu.make_async_remote_copy(src_ref=got, dst_ref=got, send_sem=send_sems.at[k], recv_sem=recv_sems.at[k],
                                         device_id=me_pos, device_id_type=MESH).wait_recv()
            local[k].wait()

    return pl.pallas_call(
        body, out_shape=out_shape, in_specs=n * [hbm], out_specs=n * [hbm], name=name,
        scratch_shapes=[pltpu.SemaphoreType.DMA((n,)), pltpu.SemaphoreType.DMA((n,)), pltpu.SemaphoreType.DMA((n,))],
        compiler_params=pltpu.CompilerParams(has_side_effects=True),
    )(*arrs)


_HBM = pl.BlockSpec(memory_space=pltpu.HBM)
_SEM = pl.BlockSpec(memory_space=pltpu.SEMAPHORE)
_DATAFLOW = pltpu.SideEffectType.DATAFLOW_SIDE_EFFECTING


def _split_start(srcs, groups, gather, name):
    n = len(srcs)
    if gather:
        lands = [lax.empty((NDEV * a.shape[0], a.shape[1]), a.dtype) for a in srcs]
    else:
        lands = [lax.empty((NDEV, a.shape[0] // NDEV, a.shape[1]), a.dtype) for a in srcs]
    n_sem = 3 * len(groups)

    def body(*refs):
        src_refs, land_refs = refs[:n], refs[n:2 * n]
        sems = refs[2 * n:2 * n + n_sem]
        token = refs[-1]
        (_, my), sibling, chips = _places()
        _, _, peers = _peers()
        targets = [sibling] + chips if gather else peers
        for g, members in enumerate(groups):
            for j, k in enumerate(members):
                _own_copy(src_refs[k], land_refs[k], sems[3 * g + 2].at[j], my, gather).start()
        for g, members in enumerate(groups):
            for j, k in enumerate(members):
                rows = srcs[k].shape[0] if gather else srcs[k].shape[0] // NDEV
                for pos, idx in targets:
                    src = src_refs[k] if gather else src_refs[k].at[pl.ds(idx * rows, rows)]
                    dst = land_refs[k].at[pl.ds(my * rows, rows)] if gather else land_refs[k].at[my]
                    pltpu.make_async_remote_copy(src_ref=src, dst_ref=dst, send_sem=sems[3 * g].at[j],
                                                 recv_sem=sems[3 * g + 1].at[j], device_id=pos, device_id_type=MESH).start()
        token[...] = jnp.zeros_like(token)

    out_shape = []
    for members in groups:
        out_shape += 3 * [pltpu.SemaphoreType.DMA((len(members),))]
    out_shape += [pltpu.HBM(a.shape, a.dtype) for a in srcs] + [pltpu.HBM(a.shape, a.dtype) for a in lands]
    out_shape.append(jax.ShapeDtypeStruct((8, 128), F32))
    res = pl.pallas_call(
        body, name=name, out_shape=tuple(out_shape), in_specs=2 * n * [_HBM],
        out_specs=tuple(n_sem * [_SEM] + 2 * n * [_HBM] + [pl.BlockSpec(memory_space=pltpu.VMEM)]),
        input_output_aliases={i: n_sem + i for i in range(2 * n)},
        compiler_params=pltpu.CompilerParams(has_side_effects=_DATAFLOW),
    )(*[pltpu.with_memory_space_constraint(a, pltpu.HBM) for a in list(srcs) + lands])
    sems = [tuple(res[3 * g:3 * g + 3]) for g in range(len(groups))]
    return sems, list(res[n_sem:n_sem + n]), list(res[n_sem + n:n_sem + 2 * n]), res[-1]


def _own_copy(src_ref, land_ref, sem, my, gather):
    if gather:
        rows = src_ref.shape[0]
        return pltpu.make_async_copy(src_ref, land_ref.at[pl.ds(my * rows, rows)], sem)
    rows = src_ref.shape[0] // NDEV
    return pltpu.make_async_copy(src_ref.at[pl.ds(my * rows, rows)], land_ref.at[my], sem)


def _wait_all(land_ref, blocks_per_dev, copies, send_sem, recv_sem, me_pos):
    part = land_ref.at[pl.ds(0, copies * blocks_per_dev)]
    pltpu.make_async_remote_copy(src_ref=part, dst_ref=part, send_sem=send_sem, recv_sem=recv_sem,
                                 device_id=me_pos, device_id_type=MESH).wait()


def _gather_forward(sems, srcs, lands, after, name):
    n = len(srcs)

    def body(*refs):
        land_refs = refs[n:2 * n]
        send_a, recv_a = refs[2 * n], refs[2 * n + 1]
        send_b, recv_b = refs[2 * n + 3], refs[2 * n + 4]
        token = refs[-1]
        (me_pos, _), sibling, chips = _places()
        for j in range(n):
            _wait_all(land_refs[j], lands[j].shape[0] // NDEV, 1 + OTHER_CHIPS, send_a.at[j], recv_a.at[j], me_pos)
        for j in range(n):
            rows = lands[j].shape[0] // NDEV
            for _, idx in chips:
                block = land_refs[j].at[pl.ds(idx * rows, rows)]
                pltpu.make_async_remote_copy(src_ref=block, dst_ref=block, send_sem=send_b.at[j], recv_sem=recv_b.at[j],
                                             device_id=sibling[0], device_id_type=MESH).start()
        token[...] = jnp.zeros_like(token)

    res = pl.pallas_call(
        body, name=name,
        out_shape=(pltpu.SemaphoreType.DMA((n,)), pltpu.SemaphoreType.DMA((n,)))
        + tuple(pltpu.HBM(a.shape, a.dtype) for a in list(srcs) + list(lands)) + (jax.ShapeDtypeStruct((8, 128), F32),),
        in_specs=2 * n * [_HBM] + [_SEM, _SEM, pl.BlockSpec(memory_space=pl.ANY)],
        out_specs=tuple([_SEM, _SEM] + 2 * n * [_HBM] + [pl.BlockSpec(memory_space=pltpu.VMEM)]),
        input_output_aliases={i: 2 + i for i in range(2 * n)},
        compiler_params=pltpu.CompilerParams(has_side_effects=_DATAFLOW),
    )(*srcs, *lands, sems[0], sems[1], after)
    return (res[0], res[1]), list(res[2:2 + n]), list(res[2 + n:2 + 2 * n]), res[-1]


def _split_wait(sems, srcs, lands, after, copies, gather, name):
    n = len(srcs)

    def body(*refs):
        src_refs, land_refs = refs[:n], refs[n:2 * n]
        send_sem, recv_sem, local_sem = refs[2 * n], refs[2 * n + 1], refs[2 * n + 2]
        (me_pos, my), _, _ = _places()
        for j in range(n):
            _wait_all(land_refs[j], lands[j].shape[0] // NDEV, copies, send_sem.at[j], recv_sem.at[j], me_pos)
            _own_copy(src_refs[j], land_refs[j], local_sem.at[j], my, gather).wait()

    res = pl.pallas_call(
        body, name=name, out_shape=tuple(pltpu.HBM(a.shape, a.dtype) for a in list(srcs) + list(lands)),
        in_specs=2 * n * [_HBM] + [_SEM, _SEM, _SEM, pl.BlockSpec(memory_space=pl.ANY)], out_specs=tuple(2 * n * [_HBM]),
        input_output_aliases={i: i for i in range(2 * n)},
        compiler_params=pltpu.CompilerParams(has_side_effects=_DATAFLOW),
    )(*srcs, *lands, sems[0], sems[1], sems[2], after)
    return list(res[n:])


def _chained(gate, mid, after):
    return gate if mid is None else gate + mid(after)[:1, :1]


def _ffn_fwd(x, norms, mod, w, mid=None):
    (pre_g, post_g), (shift, scale, gate), (wg_t, wu_t, wd) = norms, mod, w
    if not callable(wd):
        hn, g, u, a, x_out, f = _ffn_fwd_fused(x, pre_g, scale, shift, post_g, _chained(gate, mid, x), wg_t, wu_t, wd, "ffn_fwd")
        return x_out, (x, hn, g, u, a, f), (wg_t, wu_t, wd)
    hn, g, u, a = _ffn_up(x, pre_g, scale, shift, wg_t, wu_t, "ffn_up")
    wd = wd(a)
    x_out, f = _mm_post(a, wd, x, post_g, _chained(gate, mid, a), FFN_RES, "ffn_down")
    return x_out, (x, hn, g, u, a, f), (wg_t, wu_t, wd)


def _ffn_bwd(dx_out, saved, norms, mod, w, send=None):
    (pre_g, post_g), (_, scale, gate), (wg_t, wu_t, wd) = norms, mod, w
    x, hn, g, u, a, f = saved
    d_model = x.shape[1]
    if send is None:
        df, dg, du, dx, dgate, dpost, dshift, dscale, dpre = _ffn_bwd_fused(dx_out, saved, pre_g, post_g, scale, gate,
                                                                            wg_t, wu_t, wd, "ffn_bwd")
        return dx, (dpre, dpost), (dshift, dscale, dgate), tuple(_ffn_dw(dg, du, a, hn, df, "ffn_dw3"))
    sent = send
    df, dgate, dpost = _post_bwd(dx_out, f, post_g, gate, FFN_RES, "ffn_post_bwd")
    dwd = _mm([(a, df)], "tn", BF16, 256, d_model, "ffn_dw")
    dg, du = _ffn_dgu(df, wd, g, u, "ffn_dgu", after=sent(2, dwd))
    dwg_t = _mm([(dg, hn)], "tn", BF16, 256, d_model, "ffn_dw")
    dwu_t = _mm([(du, hn)], "tn", BF16, 256, d_model, "ffn_dw", after=sent(0, dwg_t))
    dhn = _mm([(dg, wg_t), (du, wu_t)], "nn", F32, TOKEN_TILE, d_model, "ffn_dhn", after=sent(1, dwu_t))
    dx, dshift, dscale, dpre = _prenorm_bwd(dx_out, [dhn], x, pre_g, scale, "prenorm_bwd")
    return dx, (dpre, dpost), (dshift, dscale, dgate), (dwg_t, dwu_t, dwd)


def _mla_fwd(x, norms, mod, w, rope, mid=None):
    (pre_g, post_g), (shift, scale, gate) = norms, mod
    w_in, q_norm, wq_t, kv_norm, wkv_t, wo = w
    hn, lat = _prenorm_mm(x, pre_g, scale, shift, w_in, "nn", F32, LAT_PAD, "mla_in")
    gate = _chained(gate, mid, lat)
    q, k, v, qn, kvn = _mla_qkv(lat, q_norm, kv_norm, wq_t, wkv_t, rope, "mla_qkv")
    o = _mla_attn_fwd(q, k, v, "mla_attn_fwd")
    x_out, f = _mm_post(o, wo, x, post_g, gate, 1.0, "mla_out")
    return x_out, (x, hn, lat, q, k, v, qn, kvn, o, f)


def _mla_bwd(dx_out, saved, norms, mod, w, rope):
    (pre_g, post_g), (_, scale, gate) = norms, mod
    w_in, q_norm, wq_t, kv_norm, wkv_t, wo = w
    x, hn, lat, q, k, v, qn, kvn, o, f = saved
    d_model = x.shape[1]
    df, dgate, dpost = _post_bwd(dx_out, f, post_g, gate, 1.0, "mix_post_bwd")
    d_o = _mm([(df, wo)], "nt", F32, TOKEN_TILE, wo.shape[0], "mla_do")
    dwo = _mm([(o, df)], "tn", BF16, TOKEN_TILE, d_model, "mla_dwo")
    dq, dk, dv = _mla_attn_bwd(q, k, v, d_o, "mla_attn_bwd")
    dqp, dkv, dlat, dq_norm, dkv_norm = _mla_qkv_bwd(dq, dk, dv, lat, q_norm, kv_norm, wq_t, wkv_t, rope, "mla_qkv_bwd")
    dwq_t = _mm([(dqp, qn)], "tn", BF16, TOKEN_TILE, Q_LORA, "mla_dwq")
    dwkv_t = _mm([(dkv, kvn)], "tn", BF16, TOKEN_TILE, KV_LORA, "mla_dwkv")
    dw_in = _mm([(hn, dlat)], "tn", BF16, TOKEN_TILE, LAT_PAD, "mla_dwin")
    dhn = _mm([(dlat, w_in)], "nt", F32, TOKEN_TILE, d_model, "mla_dhn")
    dx, dshift, dscale, dpre = _prenorm_bwd(dx_out, [dhn], x, pre_g, scale, "prenorm_bwd")
    return dx, (dpre, dpost), (dshift, dscale, dgate), (dw_in, dq_norm, dwq_t, dkv_norm, dwkv_t, dwo)


def _dil_fwd(x, norms, mod, w, bias, mid=None):
    (pre_g, post_g), (shift, scale, gate), (w_in_t, wo) = norms, mod, w
    width = 3 * DIL_HEADS * DIL_HEAD_DIM
    hns, qkvs, outs, lses = [], [], [], []
    for g, (window, dilation) in enumerate(DIL_GROUPS):
        hn, qkv = _prenorm_mm(x, pre_g, scale, shift, w_in_t, "nt", BF16, width, "dil_in", perm=dilation,
                              w_rows=(g * width, width))
        if g == 0:
            gate = _chained(gate, mid, qkv)
        o, lse = _dil_attn_fwd(qkv, bias[g], dilation, window // dilation, "dil_attn_fwd")
        hns.append(hn), qkvs.append(qkv), outs.append(o), lses.append(lse)
    alphas, o_mix, o_mix_b = _dil_mix(lses, outs, "dil_mix")
    x_out, f = _mm_post(o_mix_b, wo, x, post_g, gate, 1.0, "dil_out")
    return x_out, (x, hns, qkvs, lses, alphas, o_mix, o_mix_b, f)


def _dil_bwd(dx_out, saved, norms, mod, w, bias):
    (pre_g, post_g), (_, scale, gate), (w_in_t, wo) = norms, mod, w
    x, hns, qkvs, lses, alphas, o_mix, o_mix_b, f = saved
    d_model = x.shape[1]
    inner = DIL_HEADS * DIL_HEAD_DIM
    df, dgate, dpost = _post_bwd(dx_out, f, post_g, gate, 1.0, "mix_post_bwd")
    d_o = _mm([(df, wo)], "nt", F32, TOKEN_TILE, inner, "dil_do")
    dwo = _mm([(o_mix_b, df)], "tn", BF16, TOKEN_TILE, d_model, "dil_dwo")
    dhns, dws, dbs = [], [], []
    for g, (window, dilation) in enumerate(DIL_GROUPS):
        grads = _dil_attn_bwd(qkvs[g], bias[g], d_o, o_mix, alphas[g], lses[g], dilation, window // dilation, "dil_attn_bwd")
        dbs.append(grads[3])
        dhns.append(_mm([(grads[j], w_in_t) for j in range(3)], "nn", F32, TOKEN_TILE, d_model, "dil_dhn", out_perm=dilation,
                        b_rows=[(3 * g + j) * inner for j in range(3)]))
        dws += [_mm([(grads[j], hns[g])], "tn", BF16, TOKEN_TILE, d_model, "dil_dwin") for j in range(3)]
    dx, dshift, dscale, dpre = _prenorm_bwd(dx_out, dhns, x, pre_g, scale, "prenorm_bwd3")
    return dx, (dpre, dpost), (dshift, dscale, dgate), (jnp.concatenate(dws, axis=0), dwo), jnp.concatenate(dbs, axis=0)


def _pad_rows(a, rows):
    return jnp.pad(a, ((0, rows - a.shape[0]), (0, 0)))


def _lanes(a):
    flat = a.reshape(-1).astype(F32)
    rows = -(-flat.shape[0] // 1024) * 8
    return jnp.pad(flat, (0, rows * 128 - flat.shape[0])).reshape(rows, 128)


def kernel(x, c, norm_pre, norm_post, w_mod, b_mod, ffn_w_gate, ffn_w_up, ffn_w_down, mla_w_in, mla_q_norm, mla_w_q_up, mla_kv_norm, mla_w_kv_up, mla_w_o, dil_w_in, dil_w_o, rel_bias, loss_target, m_norm_pre, m_norm_post, m_w_mod, m_b_mod, m_ffn_w_gate, m_ffn_w_up, m_ffn_w_down, m_mla_w_in, m_mla_q_norm, m_mla_w_q_up, m_mla_kv_norm, m_mla_w_kv_up, m_mla_w_o, m_dil_w_in, m_dil_w_o, m_rel_bias, v_norm_pre, v_norm_post, v_w_mod, v_b_mod, v_ffn_w_gate, v_ffn_w_up, v_ffn_w_down, v_mla_w_in, v_mla_q_norm, v_mla_w_q_up, v_mla_kv_norm, v_mla_w_kv_up, v_mla_w_o, v_dil_w_in, v_dil_w_o, v_rel_bias):
    me = 4 * lax.axis_index("x") + 2 * lax.axis_index("y") + lax.axis_index("c")
    depth, n_sub, d_loc = norm_pre.shape
    d_model = x.shape[2]
    mod_loc_cols = w_mod.shape[2]
    x0, target = x[0], loss_target[0]

    bf_t = lambda a: a.astype(BF16).T
    ffn_ids = [(i, h) for i in range(depth) for h in range(2)]
    shards = []
    for i, h in ffn_ids:
        shards += [bf_t(ffn_w_gate[i, h]), bf_t(ffn_w_up[i, h]), ffn_w_down[i, h].astype(BF16)]
    shards += [mla_w_in[0].astype(BF16), bf_t(mla_w_q_up[0]), bf_t(mla_w_kv_up[0]), mla_w_o[0].astype(BF16),
               bf_t(dil_w_in[0]), dil_w_o[0].astype(BF16)]
    n_ffn = 3 * len(ffn_ids)
    members = {(0, 0): [0, 1, 2], (0, 1): [n_ffn, n_ffn + 1, n_ffn + 2, n_ffn + 3], (0, 2): [3, 4, 5],
               (1, 0): [6, 7, 8], (1, 1): [n_ffn + 4, n_ffn + 5], (1, 2): [9, 10, 11]}
    order = [(i, s) for i in range(depth) for s in range(n_sub)]

    small = jnp.concatenate([c.reshape(8, 128), _pad_rows(norm_pre.reshape(depth * n_sub, d_loc), 8),
                             _pad_rows(norm_post.reshape(depth * n_sub, d_loc), 8)], axis=0)
    small_all = _exchange([small], True, "gather_small")[0].reshape(NDEV, 24, 128)
    c_all = small_all[:, 0:8].reshape(NDEV, d_model)
    gains = lambda lo: jnp.transpose(small_all[:, lo:lo + depth * n_sub], (1, 0, 2)).reshape(depth, n_sub, 1, d_model)
    pre_full, post_full = gains(8), gains(16)

    b_loc = lax.dynamic_slice(b_mod, (0, me * mod_loc_cols), (depth, mod_loc_cols))
    mod_cols, silu_c = _mod_fwd(c_all, w_mod, b_loc, "mod_fwd")
    mod_all = _exchange([mod_cols.reshape(depth * NDEV, mod_loc_cols)], True, "gather_mod")[0]
    mod_all = mod_all.reshape(NDEV, depth, NDEV, mod_loc_cols)
    mod_mine = lax.dynamic_index_in_dim(mod_all, me, axis=2, keepdims=False)
    mod = jnp.transpose(mod_mine, (1, 0, 2)).reshape(depth, n_sub, 3, 1, d_model)

    shards[0], _ = lax.optimization_barrier((shards[0], mod_all))
    first = order[0]
    stages = [("%d%d" % first, members[first][:2]), ("%d%dd" % first, members[first][2:])]
    stages += [("%d%d" % key, members[key]) for key in order[1:]]
    stage_names = [name for name, _ in stages]
    g_sems, g_srcs, g_lands, g_token = _split_start(shards, [idx for _, idx in stages], True, "gather_weights_start")

    forwarded = {}

    def forward(stage, after):
        idx = stages[stage_names.index(stage)][1]
        forwarded[stage] = _gather_forward(g_sems[stage_names.index(stage)], [g_srcs[k] for k in idx],
                                           [g_lands[k] for k in idx], after, "gather_forward_" + stage)
        return forwarded[stage][3]

    def weights_of(stage, after):
        (send_b, recv_b), srcs, lands, _ = forwarded[stage]
        local = g_sems[stage_names.index(stage)][2]
        return _split_wait((send_b, recv_b, local), srcs, lands, after, OTHER_CHIPS, True, "gather_wait_" + stage)

    def late_down(after):
        forward("%d%dd" % first, after)
        return weights_of("%d%dd" % first, after)[0]

    lat_real = Q_LORA + KV_LORA
    qk = QK_NOPE + QK_ROPE

    def mla_weights(after):
        w_in, wq_t, wkv_t, wo = weights_of("01", after)
        w_in_pad = jnp.concatenate([w_in[:, :lat_real], jnp.zeros((d_model, QK_NOPE), BF16), w_in[:, lat_real:],
                                    jnp.zeros((d_model, HEAD_PAD - QK_NOPE - QK_ROPE), BF16)], axis=1)
        wq_pad = jnp.pad(wq_t.reshape(MLA_HEADS, qk, Q_LORA), ((0, 0), (0, HEAD_PAD - qk), (0, 0)))
        wo_pad = jnp.pad(wo.reshape(MLA_HEADS, V_HEAD, d_model), ((0, 0), (HEAD_PAD - V_HEAD, 0), (0, 0)))
        return (w_in_pad, mla_q_norm, wq_pad.reshape(MLA_HEADS * HEAD_PAD, Q_LORA), mla_kv_norm, wkv_t,
                wo_pad.reshape(MLA_HEADS * HEAD_PAD, d_model))

    zero = g_token[0, 0]
    rope = _rope_tables(zero)
    buckets = jnp.stack([_dil_buckets(dil) for _, dil in DIL_GROUPS]) + zero.astype(jnp.int32)
    onehot = (buckets[..., None] == jnp.arange(N_BUCKETS)).astype(F32)
    bias = jnp.einsum("gqkb,bgh->ghqk", onehot, rel_bias.reshape(N_BUCKETS, len(DIL_GROUPS), DIL_HEADS),
                      precision=lax.Precision.HIGHEST)

    norms = lambda i, s: (pre_full[i, s], post_full[i, s])
    mods = lambda i, s: (mod[i, s, 0], mod[i, s, 1], mod[i, s, 2])
    saved, weights = {}, {}
    h = lax.optimization_barrier((x0, bias, buckets, *rope))[0]
    forward("%d%d" % first, h)
    for n, (i, s) in enumerate(order):
        got = mla_weights(h) if (s == 1 and i % 2 == 0) else tuple(weights_of("%d%d" % (i, s), h))
        mid = None if n + 1 == len(order) else (lambda after, nxt="%d%d" % order[n + 1]: forward(nxt, after))
        if s != 1:
            if len(got) == 3:
                h, saved[i, s], weights[i, s] = _ffn_fwd(h, norms(i, s), mods(i, s), got)
                if mid is not None:
                    mid(h)
            else:
                h, saved[i, s], weights[i, s] = _ffn_fwd(h, norms(i, s), mods(i, s), (*got, late_down), mid)
            continue
        weights[i, s] = got
        if i % 2 == 0:
            h, saved[i, s] = _mla_fwd(h, norms(i, s), mods(i, s), weights[i, s], rope, mid)
        else:
            h, saved[i, s] = _dil_fwd(h, norms(i, s), mods(i, s), weights[i, s], bias, mid)
    dh, loss_parts = _loss_grad(h, target, "loss")

    dnorm, dmod, sent = {}, {}, {}
    token = jnp.zeros((8, 128), F32)
    last = order[0]

    def send_last(j, dw):
        sent[last, j] = _split_start([dw], [[0]], False, "scatter_start_%d%d_%d" % (*last, j))
        return sent[last, j][3]

    for i, s in reversed(order):
        md = mods(i, s)
        md = (md[0], md[1], md[2] + token[:1, :1])
        if (i, s) == last:
            dh, dnorm[i, s], dmod[i, s], _ = _ffn_bwd(dh, saved[i, s], norms(i, s), md, weights[i, s], send_last)
            continue
        if s != 1:
            dh, dnorm[i, s], dmod[i, s], dws = _ffn_bwd(dh, saved[i, s], norms(i, s), md, weights[i, s])
        elif i % 2 == 0:
            dh, dnorm[i, s], dmod[i, s], dmla = _mla_bwd(dh, saved[i, s], norms(i, s), md, weights[i, s], rope)
            dw_in_pad, dq_norm, dwq_pad, dkv_norm, dwkv_t, dwo_pad = dmla
            dw_in = jnp.concatenate([dw_in_pad[:, :lat_real], dw_in_pad[:, lat_real + QK_NOPE:lat_real + qk]], axis=1)
            dwq_t = dwq_pad.reshape(MLA_HEADS, HEAD_PAD, Q_LORA)[:, :qk].reshape(MLA_HEADS * qk, Q_LORA)
            dwo = dwo_pad.reshape(MLA_HEADS, HEAD_PAD, d_model)[:, HEAD_PAD - V_HEAD:].reshape(MLA_HEADS * V_HEAD, d_model)
            dws = (dw_in, dwq_t, dwkv_t, dwo)
        else:
            dh, dnorm[i, s], dmod[i, s], dws, dbias = _dil_bwd(dh, saved[i, s], norms(i, s), md, weights[i, s], bias)
        sent[i, s] = _split_start(list(dws), [list(range(len(dws)))], False, "scatter_start_%d%d" % (i, s))
        token = sent[i, s][3]
    grad_x = dh[None]

    mine = {}
    transposed = {3 * n + j for n in range(len(ffn_ids)) for j in (0, 1)} | {n_ffn + 1, n_ffn + 2, n_ffn + 4}
    for key in reversed(order[1:]):
        sems, srcs, lands, _ = sent[key]
        parts = _split_wait(sems[0], srcs, lands, dh, NDEV - 1, False, "scatter_wait_%d%d" % key)
        for k, p in zip(members[key], parts):
            mine[k] = _sum_parts(p, "sum_parts", k in transposed)
    g_mla_in, g_q_up, g_kv_up, g_mla_o, g_dil_in, g_dil_o = (mine[k] for k in range(n_ffn, n_ffn + 6))
    g_mla_in, g_q_up, g_kv_up, g_mla_o = g_mla_in[None], g_q_up[None], g_kv_up[None], g_mla_o[None]
    g_dil_in, g_dil_o = g_dil_in[None], g_dil_o[None]
    early = {"mla_w_in": _adamw(mla_w_in, g_mla_in, m_mla_w_in, v_mla_w_in, "adamw"),
             "mla_w_q_up": _adamw(mla_w_q_up, g_q_up, m_mla_w_q_up, v_mla_w_q_up, "adamw"),
             "mla_w_kv_up": _adamw(mla_w_kv_up, g_kv_up, m_mla_w_kv_up, v_mla_w_kv_up, "adamw"),
             "mla_w_o": _adamw(mla_w_o, g_mla_o, m_mla_w_o, v_mla_w_o, "adamw"),
             "dil_w_in": _adamw(dil_w_in, g_dil_in, m_dil_w_in, v_dil_w_in, "adamw"),
             "dil_w_o": _adamw(dil_w_o, g_dil_o, m_dil_w_o, v_dil_w_o, "adamw")}
    dbias_sums = _bias_reduce(dbias, buckets, "bias_reduce")
    tied = lax.optimization_barrier((dbias_sums, *[a for step in early.values() for a in step]))
    dbias_sums, early = tied[0], {name: tuple(tied[1 + 3 * n:4 + 3 * n]) for n, name in enumerate(early)}
    for j in (2, 0, 1):
        sems, srcs, lands, _ = sent[last, j]
        parts = _split_wait(sems[0], srcs, lands, dbias_sums, NDEV - 1, False, "scatter_wait_%d%d_%d" % (*last, j))
        mine[members[last][j]] = _sum_parts(parts[0], "sum_parts", members[last][j] in transposed)
    g_gate = jnp.stack([mine[3 * n] for n in range(len(ffn_ids))]).reshape(ffn_w_gate.shape)
    g_up = jnp.stack([mine[3 * n + 1] for n in range(len(ffn_ids))]).reshape(ffn_w_up.shape)
    g_down = jnp.stack([mine[3 * n + 2] for n in range(len(ffn_ids))]).reshape(ffn_w_down.shape)

    dmod_mine = jnp.concatenate([jnp.concatenate(dmod[i, s], axis=0) for i in range(depth) for s in range(n_sub)], axis=0)
    dpre_mine = jnp.concatenate([dnorm[i, s][0] for i in range(depth) for s in range(n_sub)], axis=0)
    dpost_mine = jnp.concatenate([dnorm[i, s][1] for i in range(depth) for s in range(n_sub)], axis=0)
    dbias_tab = dbias_sums[:, 0, :N_BUCKETS].T
    pieces = [dmod_mine, dpre_mine, dpost_mine, dq_norm, dkv_norm, dbias_tab, jnp.sum(loss_parts).reshape(1, 1)]
    packed = [_lanes(p) for p in pieces]
    offs = [0]
    for p in packed:
        offs.append(offs[-1] + p.shape[0])
    everyone = _exchange([jnp.concatenate(packed, axis=0)], True, "gather_small_grads")[0].reshape(NDEV, offs[-1], 128)
    total = _sum_parts(everyone, "sum_small")
    take = lambda n, shape: total[offs[n]:offs[n + 1]].reshape(-1)[:math.prod(shape)].reshape(shape)
    g_b_mod = take(0, b_mod.shape)
    col0 = me * d_loc
    g_norm_pre = lax.dynamic_slice(take(1, (depth, n_sub, d_model)), (0, 0, col0), norm_pre.shape)
    g_norm_post = lax.dynamic_slice(take(2, (depth, n_sub, d_model)), (0, 0, col0), norm_post.shape)
    g_q_norm, g_kv_norm = take(3, mla_q_norm.shape), take(4, mla_kv_norm.shape)
    g_rel_bias = take(5, rel_bias.shape)
    loss = take(6, ())

    dmod_all = everyone[:, offs[0]:offs[1]].reshape(NDEV, depth, NDEV * mod_loc_cols)
    dmod_cols = lax.dynamic_slice(dmod_all, (0, 0, me * mod_loc_cols), (NDEV, depth, mod_loc_cols))
    silu_t = jnp.pad(silu_c.T, ((0, 0), (0, HEAD_PAD - NDEV)))
    g_w_mod = jnp.stack([_mm([(silu_t, jnp.pad(dmod_cols[:, i], ((0, HEAD_PAD - NDEV), (0, 0))))], "nn", F32, TOKEN_TILE,
                             mod_loc_cols, "mod_bwd") for i in range(depth)])

    ws = (norm_pre, norm_post, w_mod, b_mod, ffn_w_gate, ffn_w_up, ffn_w_down, mla_w_in, mla_q_norm, mla_w_q_up, mla_kv_norm,
          mla_w_kv_up, mla_w_o, dil_w_in, dil_w_o, rel_bias)
    gs = (g_norm_pre, g_norm_post, g_w_mod, g_b_mod, g_gate, g_up, g_down, g_mla_in, g_q_norm, g_q_up, g_kv_norm, g_kv_up,
          g_mla_o, g_dil_in, g_dil_o, g_rel_bias)
    ms = (m_norm_pre, m_norm_post, m_w_mod, m_b_mod, m_ffn_w_gate, m_ffn_w_up, m_ffn_w_down, m_mla_w_in, m_mla_q_norm,
          m_mla_w_q_up, m_mla_kv_norm, m_mla_w_kv_up, m_mla_w_o, m_dil_w_in, m_dil_w_o, m_rel_bias)
    vs = (v_norm_pre, v_norm_post, v_w_mod, v_b_mod, v_ffn_w_gate, v_ffn_w_up, v_ffn_w_down, v_mla_w_in, v_mla_q_norm,
          v_mla_w_q_up, v_mla_kv_norm, v_mla_w_kv_up, v_mla_w_o, v_dil_w_in, v_dil_w_o, v_rel_bias)
    names = ("norm_pre", "norm_post", "w_mod", "b_mod", "ffn_w_gate", "ffn_w_up", "ffn_w_down", "mla_w_in", "mla_q_norm",
             "mla_w_q_up", "mla_kv_norm", "mla_w_kv_up", "mla_w_o", "dil_w_in", "dil_w_o", "rel_bias")
    stepped = [early[n] if n in early else _adamw(w, g, m, v, "adamw") for n, w, g, m, v in zip(names, ws, gs, ms, vs)]
    deltas, new_m, new_v = zip(*stepped)
    return (loss, grad_x, *gs, *deltas, *new_m, *new_v)
```

```python
import math

import jax
import jax.numpy as jnp
from jax import lax
from jax.experimental import pallas as pl
from jax.experimental.pallas import tpu as pltpu

F32 = jnp.float32
BF16 = jnp.bfloat16
MESH = pl.DeviceIdType.MESH

NDEV = 8
OTHER_CHIPS = 3
D_MODEL = 1024
SEQ = 2048
D_FF = 2816
EPS = 1e-6
FFN_RES = 0.5

MLA_HEADS = 16
Q_LORA = 384
KV_LORA = 256
QK_NOPE = 64
QK_ROPE = 32
V_HEAD = 64
ROPE_THETA = 10000.0
HEAD_PAD = 128
LAT_PAD = Q_LORA + KV_LORA + HEAD_PAD
MLA_SCALE = (QK_NOPE + QK_ROPE) ** -0.5

DIL_GROUPS = ((128, 1), (512, 4), (2048, 16))
DIL_HEADS = 16
DIL_HEAD_DIM = 64
DIL_BLOCK = 128
DIL_PAIRS = DIL_HEADS // 2
DIL_SCALE = DIL_HEAD_DIM ** -0.5
DIL_GROUPED = 4
N_BUCKETS = 32
MAX_DISTANCE = 2048

ADAM_LR = 0.001
ADAM_B1 = 0.9
ADAM_B2 = 0.999
ADAM_EPS = 1e-08
ADAM_WD = 0.01
ADAM_STEP = 10

V7X_VMEM_BYTES = 64 * 2**20
VMEM_RESERVE = 10 * 2**20
TOKEN_TILE = 512


def _nbytes(shape, dtype):
    return math.prod(shape) * jnp.dtype(dtype).itemsize


def _params(semantics, blocks, extra=0):
    need = 2 * sum(_nbytes(s, d) for s, d in blocks) + extra + VMEM_RESERVE
    return pltpu.CompilerParams(dimension_semantics=semantics,
                                vmem_limit_bytes=int(min(need, V7X_VMEM_BYTES - VMEM_RESERVE)))


def _pcall(body, out_shape, **kw):
    call = pl.pallas_call(body, out_shape=jax.tree.map(lambda s: pltpu.HBM(s.shape, s.dtype), out_shape), **kw)
    return lambda *args: call(*[pltpu.with_memory_space_constraint(a, pltpu.HBM) for a in args])


def _dot_nn(a, b):
    return lax.dot_general(a, b, (((1,), (0,)), ((), ())), preferred_element_type=F32)


def _dot_nt(a, b):
    return lax.dot_general(a, b, (((1,), (1,)), ((), ())), preferred_element_type=F32)


def _dot_tn(a, b):
    return lax.dot_general(a, b, (((0,), (0,)), ((), ())), preferred_element_type=F32)


_DOTS = {"nn": _dot_nn, "nt": _dot_nt, "tn": _dot_tn}


def _rstd(v):
    return lax.rsqrt(jnp.mean(v * v, axis=-1, keepdims=True) + EPS)


def _rms_bwd(v, r, t):
    return r * t - v * (r * r * r) * jnp.mean(t * v, axis=-1, keepdims=True)


_TOKEN_SPEC = pl.BlockSpec((8, 128), lambda *_: (0, 0))


def _mm(pairs, mode, out_dtype, tm, tn, name, out_perm=1, after=None, b_rows=None):
    a0, b0 = pairs[0]
    m_dim = a0.shape[1] if mode == "tn" else a0.shape[0]
    n_dim = b0.shape[0] if mode == "nt" else b0.shape[1]
    tm, tn = min(tm, m_dim // out_perm), min(tn, n_dim)
    assert m_dim % tm == 0 and n_dim % tn == 0, (name, m_dim, n_dim, tm, tn)
    dot = _DOTS[mode]
    npairs = len(pairs)

    def body(*refs):
        acc = None
        for p in range(npairs):
            d = dot(refs[2 * p][...].astype(BF16), refs[2 * p + 1][...].astype(BF16))
            acc = d if acc is None else acc + d
        refs[-1][...] = acc.astype(out_dtype)

    in_specs, blocks, flat = [], [], []
    for n_pair, (a, b) in enumerate(pairs):
        if mode == "nn":
            k = a.shape[1]
            first_block = 0 if b_rows is None else b_rows[n_pair] // k
            sa, sb = ((tm, k), lambda i, j: (i, 0)), ((k, tn), lambda i, j, o=first_block: (o, j))
        elif mode == "nt":
            k = a.shape[1]
            sa, sb = ((tm, k), lambda i, j: (i, 0)), ((tn, k), lambda i, j: (j, 0))
        else:
            k = a.shape[0]
            sa, sb = ((k, tm), lambda i, j: (0, i)), ((k, tn), lambda i, j: (0, j))
        in_specs += [pl.BlockSpec(*sa), pl.BlockSpec(*sb)]
        blocks += [(sa[0], a.dtype), (sb[0], b.dtype)]
        flat += [a, b]
    if after is not None:
        in_specs.append(_TOKEN_SPEC)
        flat.append(after)
    if out_perm == 1:
        out_shape = (m_dim, n_dim)
        out_spec = pl.BlockSpec((tm, tn), lambda i, j: (i, j))
    else:
        rows = m_dim // out_perm
        assert tn == n_dim and rows % tm == 0, (name, rows, tm)
        nb = rows // tm
        out_shape = (rows, out_perm * n_dim)
        out_spec = pl.BlockSpec((tm, n_dim), lambda i, j: (i % nb, i // nb))
    blocks.append(((tm, tn), out_dtype))
    res = _pcall(
        body, out_shape=jax.ShapeDtypeStruct(out_shape, out_dtype), grid=(m_dim // tm, n_dim // tn),
        in_specs=in_specs, out_specs=out_spec, name=name,
        compiler_params=_params(("parallel", "parallel"), blocks, extra=2 * tm * tn * 4),
    )(*flat)
    return res.reshape(m_dim, n_dim)


def _prenorm_mm(x, pre_g, scale, shift, w, w_mode, out_dtype, tn, name, perm=1, w_rows=None):
    s_dim, d_dim = x.shape
    n_dim = w.shape[0] if w_mode == "nt" else w.shape[1]
    w_first = 0
    if w_rows is not None:
        w_first, n_dim = w_rows
    rows = s_dim // perm
    tm = min(TOKEN_TILE if perm == 1 else DIL_BLOCK, rows)
    nb = rows // tm
    tn = min(tn, n_dim)
    assert n_dim % tn == 0 and w_first % tn == 0
    w_block0 = w_first // tn
    dot = _DOTS[w_mode]

    def body(x_ref, g_ref, sc_ref, sh_ref, w_ref, hn_ref, o_ref):
        @pl.when(pl.program_id(1) == 0)
        def _():
            xf = x_ref[...]
            hn = (xf * _rstd(xf) * g_ref[...]) * (1.0 + sc_ref[...]) + sh_ref[...]
            hn_ref[...] = hn.astype(BF16)

        o_ref[...] = dot(hn_ref[...], w_ref[...]).astype(out_dtype)

    vec = pl.BlockSpec((1, d_dim), lambda i, j: (0, 0))
    w_block = (tn, d_dim) if w_mode == "nt" else (d_dim, tn)
    w_spec = pl.BlockSpec(w_block, (lambda i, j: (w_block0 + j, 0)) if w_mode == "nt" else (lambda i, j: (0, j)))
    hn, out = _pcall(
        body,
        out_shape=(jax.ShapeDtypeStruct((s_dim, d_dim), BF16), jax.ShapeDtypeStruct((s_dim, n_dim), out_dtype)),
        grid=(s_dim // tm, n_dim // tn),
        in_specs=[pl.BlockSpec((tm, d_dim), lambda i, j: (i % nb, i // nb)), vec, vec, vec, w_spec],
        out_specs=(pl.BlockSpec((tm, d_dim), lambda i, j: (i, 0)), pl.BlockSpec((tm, tn), lambda i, j: (i, j))),
        name=name,
        compiler_params=_params(("parallel", "arbitrary"),
                                [((tm, d_dim), F32), (w_block, BF16), ((tm, d_dim), BF16), ((tm, tn), out_dtype)],
                                extra=3 * tm * d_dim * 4 + tm * tn * 4),
    )(x.reshape(rows, perm * d_dim), pre_g, scale, shift, w)
    return hn, out


def _ffn_up(x, pre_g, scale, shift, wg_t, wu_t, name):
    s_dim, d_dim = x.shape
    f_dim = wg_t.shape[0]
    tm, tn = TOKEN_TILE, f_dim // 2

    def body(x_ref, g_ref, sc_ref, sh_ref, wg_ref, wu_ref, hn_ref, go_ref, uo_ref, a_ref):
        @pl.when(pl.program_id(1) == 0)
        def _():
            xf = x_ref[...]
            hn = (xf * _rstd(xf) * g_ref[...]) * (1.0 + sc_ref[...]) + sh_ref[...]
            hn_ref[...] = hn.astype(BF16)

        hn = hn_ref[...]
        g = _dot_nt(hn, wg_ref[...])
        u = _dot_nt(hn, wu_ref[...])
        go_ref[...] = g.astype(BF16)
        uo_ref[...] = u.astype(BF16)
        a_ref[...] = (g * jax.nn.sigmoid(g) * u).astype(BF16)

    vec = pl.BlockSpec((1, d_dim), lambda i, j: (0, 0))
    w_spec = pl.BlockSpec((tn, d_dim), lambda i, j: (j, 0))
    act = pl.BlockSpec((tm, tn), lambda i, j: (i, j))
    act_shape = jax.ShapeDtypeStruct((s_dim, f_dim), BF16)
    return _pcall(
        body,
        out_shape=(jax.ShapeDtypeStruct((s_dim, d_dim), BF16), act_shape, act_shape, act_shape),
        grid=(s_dim // tm, f_dim // tn),
        in_specs=[pl.BlockSpec((tm, d_dim), lambda i, j: (i, 0)), vec, vec, vec, w_spec, w_spec],
        out_specs=(pl.BlockSpec((tm, d_dim), lambda i, j: (i, 0)), act, act, act),
        name=name,
        compiler_params=_params(("parallel", "arbitrary"),
                                [((tm, d_dim), F32), ((tn, d_dim), BF16), ((tn, d_dim), BF16), ((tm, d_dim), BF16)]
                                + 3 * [((tm, tn), BF16)], extra=3 * tm * d_dim * 4 + 4 * tm * tn * 4),
    )(x, pre_g, scale, shift, wg_t, wu_t)


def _mm_post(a, w, x, post_g, gate, res_w, name):
    s_dim, k_dim = a.shape
    d_dim = w.shape[1]
    tm = TOKEN_TILE

    def body(a_ref, w_ref, x_ref, pg_ref, gt_ref, xo_ref, f_ref):
        f = _dot_nn(a_ref[...], w_ref[...])
        y = f * _rstd(f) * pg_ref[...]
        f_ref[...] = f
        xo_ref[...] = x_ref[...] + (res_w * gt_ref[...]) * y

    vec = pl.BlockSpec((1, d_dim), lambda i: (0, 0))
    row = pl.BlockSpec((tm, d_dim), lambda i: (i, 0))
    out = jax.ShapeDtypeStruct((s_dim, d_dim), F32)
    return _pcall(
        body, out_shape=(out, out), grid=(s_dim // tm,),
        in_specs=[pl.BlockSpec((tm, k_dim), lambda i: (i, 0)), pl.BlockSpec((k_dim, d_dim), lambda i: (0, 0)), row, vec, vec],
        out_specs=(row, row), name=name,
        compiler_params=_params(("parallel",), [((tm, k_dim), BF16), ((k_dim, d_dim), BF16)] + 3 * [((tm, d_dim), F32)],
                                extra=3 * tm * d_dim * 4),
    )(a, w, x, post_g, gate)


def _post_bwd(dx_out, f, post_g, gate, res_w, name):
    s_dim, d_dim = f.shape
    tm = TOKEN_TILE

    def body(dx_ref, f_ref, pg_ref, gt_ref, df_ref, dgate_ref, dpost_ref):
        @pl.when(pl.program_id(0) == 0)
        def _():
            dgate_ref[...] = jnp.zeros_like(dgate_ref)
            dpost_ref[...] = jnp.zeros_like(dpost_ref)

        dx, fv = dx_ref[...], f_ref[...]
        r = _rstd(fv)
        fr = fv * r
        dgate_ref[...] += res_w * jnp.sum(dx * (fr * pg_ref[...]), axis=0, keepdims=True)
        dy = (res_w * gt_ref[...]) * dx
        dpost_ref[...] += jnp.sum(dy * fr, axis=0, keepdims=True)
        df_ref[...] = _rms_bwd(fv, r, dy * pg_ref[...]).astype(BF16)

    vec = pl.BlockSpec((1, d_dim), lambda i: (0, 0))
    row = pl.BlockSpec((tm, d_dim), lambda i: (i, 0))
    vshape = jax.ShapeDtypeStruct((1, d_dim), F32)
    return _pcall(
        body, out_shape=(jax.ShapeDtypeStruct((s_dim, d_dim), BF16), vshape, vshape), grid=(s_dim // tm,),
        in_specs=[row, row, vec, vec], out_specs=(row, vec, vec), name=name,
        compiler_params=_params(("arbitrary",), 3 * [((tm, d_dim), F32)], extra=6 * tm * d_dim * 4),
    )(dx_out, f, post_g, gate)


def _prenorm_bwd(dx_out, dhns, x, pre_g, scale, name):
    s_dim, d_dim = x.shape
    tm = TOKEN_TILE
    n_in = len(dhns)

    def body(*refs):
        dx_ref, x_ref, pg_ref, sc_ref = refs[n_in + 0], refs[n_in + 1], refs[n_in + 2], refs[n_in + 3]
        dxo_ref, dsh_ref, dsc_ref, dpg_ref = refs[n_in + 4:]

        @pl.when(pl.program_id(0) == 0)
        def _():
            dsh_ref[...] = jnp.zeros_like(dsh_ref)
            dsc_ref[...] = jnp.zeros_like(dsc_ref)
            dpg_ref[...] = jnp.zeros_like(dpg_ref)

        dhn = refs[0][...]
        for k in range(1, n_in):
            dhn = dhn + refs[k][...]
        xv = x_ref[...]
        r = _rstd(xv)
        xr = xv * r
        dsh_ref[...] += jnp.sum(dhn, axis=0, keepdims=True)
        dsc_ref[...] += jnp.sum(dhn * (xr * pg_ref[...]), axis=0, keepdims=True)
        dn = dhn * (1.0 + sc_ref[...])
        dpg_ref[...] += jnp.sum(dn * xr, axis=0, keepdims=True)
        dxo_ref[...] = dx_ref[...] + _rms_bwd(xv, r, dn * pg_ref[...])

    vec = pl.BlockSpec((1, d_dim), lambda i: (0, 0))
    row = pl.BlockSpec((tm, d_dim), lambda i: (i, 0))
    vshape = jax.ShapeDtypeStruct((1, d_dim), F32)
    return _pcall(
        body, out_shape=(jax.ShapeDtypeStruct((s_dim, d_dim), F32), vshape, vshape, vshape), grid=(s_dim // tm,),
        in_specs=n_in * [row] + [row, row, vec, vec], out_specs=(row, vec, vec, vec), name=name,
        compiler_params=_params(("arbitrary",), (n_in + 3) * [((tm, d_dim), F32)], extra=6 * tm * d_dim * 4),
    )(*dhns, dx_out, x, pre_g, scale)


def _ffn_dgu(df, wd, g, u, name, after=None):
    s_dim, d_dim = df.shape
    f_dim = wd.shape[0]
    tm, tn = TOKEN_TILE, f_dim // 2

    def body(df_ref, wd_ref, g_ref, u_ref, *rest):
        dg_ref, du_ref = rest[-2:]
        da = _dot_nt(df_ref[...], wd_ref[...])
        gv, uv = g_ref[...].astype(F32), u_ref[...].astype(F32)
        sg = jax.nn.sigmoid(gv)
        du_ref[...] = (da * (gv * sg)).astype(BF16)
        dg_ref[...] = (da * uv * (sg * (1.0 + gv * (1.0 - sg)))).astype(BF16)

    act = pl.BlockSpec((tm, tn), lambda i, j: (i, j))
    act_shape = jax.ShapeDtypeStruct((s_dim, f_dim), BF16)
    token = [] if after is None else [after]
    return _pcall(
        body, out_shape=(act_shape, act_shape), grid=(s_dim // tm, f_dim // tn),
        in_specs=[pl.BlockSpec((tm, d_dim), lambda i, j: (i, 0)), pl.BlockSpec((tn, d_dim), lambda i, j: (j, 0)), act, act]
        + len(token) * [_TOKEN_SPEC],
        out_specs=(act, act), name=name,
        compiler_params=_params(("parallel", "parallel"), [((tm, d_dim), BF16), ((tn, d_dim), BF16)] + 4 * [((tm, tn), BF16)],
                                extra=6 * tm * tn * 4),
    )(df, wd, g, u, *token)


def _ffn_dw(dg, du, a, hn, df, name):
    s_dim, f_dim = dg.shape
    d_dim = hn.shape[1]
    tm = 256

    def body(dg_ref, du_ref, a_ref, hn_ref, df_ref, dwg_ref, dwu_ref, dwd_ref):
        dwg_ref[...] = _dot_tn(dg_ref[...], hn_ref[...]).astype(BF16)
        dwu_ref[...] = _dot_tn(du_ref[...], hn_ref[...]).astype(BF16)
        dwd_ref[...] = _dot_tn(a_ref[...], df_ref[...]).astype(BF16)

    col = pl.BlockSpec((s_dim, tm), lambda i: (0, i))
    full = pl.BlockSpec((s_dim, d_dim), lambda i: (0, 0), pipeline_mode=pl.Buffered(1))
    out = pl.BlockSpec((tm, d_dim), lambda i: (i, 0))
    shape = jax.ShapeDtypeStruct((f_dim, d_dim), BF16)
    need = 2 * s_dim * d_dim * 2 + 2 * 3 * (s_dim * tm * 2 + tm * d_dim * 2) + 3 * tm * d_dim * 4 + 3 * s_dim * tm * 2
    return _pcall(
        body, out_shape=(shape, shape, shape), grid=(f_dim // tm,), in_specs=[col, col, col, full, full],
        out_specs=(out, out, out), name=name,
        compiler_params=pltpu.CompilerParams(dimension_semantics=("parallel",),
                                             vmem_limit_bytes=int(min(need + VMEM_RESERVE, V7X_VMEM_BYTES - VMEM_RESERVE))),
    )(dg, du, a, hn, df)


def _mm_tn_shared(lhs, b, name):
    k_dim, m_dim = lhs[0].shape
    n_dim = b.shape[1]
    tm = 256
    n = len(lhs)

    def body(*refs):
        rhs = refs[n][...]
        for j in range(n):
            refs[n + 1 + j][...] = _dot_tn(refs[j][...], rhs).astype(BF16)

    col = pl.BlockSpec((k_dim, tm), lambda i: (0, i))
    out = pl.BlockSpec((tm, n_dim), lambda i: (i, 0))
    shape = jax.ShapeDtypeStruct((m_dim, n_dim), BF16)
    need = k_dim * n_dim * 2 + 2 * n * (k_dim * tm * 2 + tm * n_dim * 2) + n * tm * n_dim * 4 + n * k_dim * tm * 2
    return _pcall(
        body, out_shape=tuple(n * [shape]), grid=(m_dim // tm,),
        in_specs=n * [col] + [pl.BlockSpec((k_dim, n_dim), lambda i: (0, 0), pipeline_mode=pl.Buffered(1))],
        out_specs=tuple(n * [out]), name=name,
        compiler_params=pltpu.CompilerParams(dimension_semantics=("parallel",),
                                             vmem_limit_bytes=int(min(need + VMEM_RESERVE, V7X_VMEM_BYTES - VMEM_RESERVE))),
    )(*lhs, b)


def _ffn_fwd_fused(x, pre_g, scale, shift, post_g, gate, wg_t, wu_t, wd, name):
    s_dim, d_dim = x.shape
    f_dim = wd.shape[0]
    tm, chunks = 256, 2
    cw = f_dim // chunks

    def body(x_ref, prg_ref, sc_ref, sh_ref, pg_ref, gt_ref, wg_ref, wu_ref, wd_ref, hn_ref, go_ref, uo_ref, a_ref, xo_ref, f_ref):
        xf = x_ref[...]
        hn = ((xf * _rstd(xf) * prg_ref[...]) * (1.0 + sc_ref[...]) + sh_ref[...]).astype(BF16)
        hn_ref[...] = hn
        f = None
        ahead = (_dot_nt(hn, wg_ref[0:cw, :]), _dot_nt(hn, wu_ref[0:cw, :]))
        for c in range(chunks):
            g, u = ahead
            if c + 1 < chunks:
                nxt = slice((c + 1) * cw, (c + 2) * cw)
                ahead = (_dot_nt(hn, wg_ref[nxt, :]), _dot_nt(hn, wu_ref[nxt, :]))
            cols = slice(c * cw, (c + 1) * cw)
            go_ref[:, cols] = g.astype(BF16)
            uo_ref[:, cols] = u.astype(BF16)
            a = (g * jax.nn.sigmoid(g) * u).astype(BF16)
            a_ref[:, cols] = a
            part = _dot_nn(a, wd_ref[cols, :])
            f = part if f is None else f + part
        f_ref[...] = f
        xo_ref[...] = xf + (FFN_RES * gt_ref[...]) * (f * _rstd(f) * pg_ref[...])

    vec = pl.BlockSpec((1, d_dim), lambda i: (0, 0))
    row = pl.BlockSpec((tm, d_dim), lambda i: (i, 0))
    act = pl.BlockSpec((tm, f_dim), lambda i: (i, 0))
    weight = pl.BlockSpec((f_dim, d_dim), lambda i: (0, 0), pipeline_mode=pl.Buffered(1))
    act_shape = jax.ShapeDtypeStruct((s_dim, f_dim), BF16)
    res_shape = jax.ShapeDtypeStruct((s_dim, d_dim), F32)
    need = (3 * f_dim * d_dim * 2 + 2 * tm * d_dim * 4 + 2 * (tm * d_dim * 2 + 3 * tm * f_dim * 2 + 2 * tm * d_dim * 4)
            + 8 * tm * cw * 4 + 4 * tm * d_dim * 4)
    return _pcall(
        body, out_shape=(jax.ShapeDtypeStruct((s_dim, d_dim), BF16), act_shape, act_shape, act_shape, res_shape, res_shape),
        grid=(s_dim // tm,), in_specs=[row, vec, vec, vec, vec, vec, weight, weight, weight],
        out_specs=(row, act, act, act, row, row), name=name,
        compiler_params=pltpu.CompilerParams(dimension_semantics=("parallel",),
                                             vmem_limit_bytes=int(min(need + VMEM_RESERVE, V7X_VMEM_BYTES - VMEM_RESERVE))),
    )(x, pre_g, scale, shift, post_g, gate, wg_t, wu_t, wd)


def _ffn_bwd_fused(dx_out, saved, pre_g, post_g, scale, gate, wg_t, wu_t, wd, name):
    x, _, g, u, _, f = saved
    s_dim, d_dim = x.shape
    f_dim = wd.shape[0]
    tm, chunks = 256, 2
    cw = f_dim // chunks

    def body(dx_ref, f_ref, g_ref, u_ref, x_ref, pg_ref, gt_ref, prg_ref, sc_ref, wd_ref, wg_ref, wu_ref,
             df_ref, dg_ref, du_ref, dxo_ref, dgate_ref, dpost_ref, dsh_ref, dsc_ref, dpg_ref):
        @pl.when(pl.program_id(0) == 0)
        def _():
            for acc in (dgate_ref, dpost_ref, dsh_ref, dsc_ref, dpg_ref):
                acc[...] = jnp.zeros_like(acc)

        dx, fv = dx_ref[...], f_ref[...]
        r = _rstd(fv)
        fr = fv * r
        dgate_ref[...] += FFN_RES * jnp.sum(dx * (fr * pg_ref[...]), axis=0, keepdims=True)
        dy = (FFN_RES * gt_ref[...]) * dx
        dpost_ref[...] += jnp.sum(dy * fr, axis=0, keepdims=True)
        df = _rms_bwd(fv, r, dy * pg_ref[...]).astype(BF16)
        df_ref[...] = df
        dhn = None
        ahead = _dot_nt(df, wd_ref[0:cw, :])
        for c in range(chunks):
            da = ahead
            if c + 1 < chunks:
                ahead = _dot_nt(df, wd_ref[(c + 1) * cw:(c + 2) * cw, :])
            cols = slice(c * cw, (c + 1) * cw)
            gv, uv = g_ref[:, cols].astype(F32), u_ref[:, cols].astype(F32)
            sg = jax.nn.sigmoid(gv)
            du = (da * (gv * sg)).astype(BF16)
            dg = (da * uv * (sg * (1.0 + gv * (1.0 - sg)))).astype(BF16)
            dg_ref[:, cols] = dg
            du_ref[:, cols] = du
            part = _dot_nn(dg, wg_ref[cols, :]) + _dot_nn(du, wu_ref[cols, :])
            dhn = part if dhn is None else dhn + part
        xv = x_ref[...]
        rx = _rstd(xv)
        xr = xv * rx
        dsh_ref[...] += jnp.sum(dhn, axis=0, keepdims=True)
        dsc_ref[...] += jnp.sum(dhn * (xr * prg_ref[...]), axis=0, keepdims=True)
        dn = dhn * (1.0 + sc_ref[...])
        dpg_ref[...] += jnp.sum(dn * xr, axis=0, keepdims=True)
        dxo_ref[...] = dx + _rms_bwd(xv, rx, dn * prg_ref[...])

    vec = pl.BlockSpec((1, d_dim), lambda i: (0, 0))
    row = pl.BlockSpec((tm, d_dim), lambda i: (i, 0))
    act = pl.BlockSpec((tm, f_dim), lambda i: (i, 0))
    weight = pl.BlockSpec((f_dim, d_dim), lambda i: (0, 0), pipeline_mode=pl.Buffered(1))
    vshape = jax.ShapeDtypeStruct((1, d_dim), F32)
    act_shape = jax.ShapeDtypeStruct((s_dim, f_dim), BF16)
    need = (3 * f_dim * d_dim * 2 + 2 * (3 * tm * d_dim * 4 + 2 * tm * f_dim * 2) + 2 * (tm * d_dim * 2 + 2 * tm * f_dim * 2 + tm * d_dim * 4)
            + 6 * tm * cw * 4 + 6 * tm * d_dim * 4)
    return _pcall(
        body, out_shape=(jax.ShapeDtypeStruct((s_dim, d_dim), BF16), act_shape, act_shape, jax.ShapeDtypeStruct((s_dim, d_dim), F32),
                         vshape, vshape, vshape, vshape, vshape),
        grid=(s_dim // tm,), in_specs=[row, row, act, act, row, vec, vec, vec, vec, weight, weight, weight],
        out_specs=(row, act, act, row, vec, vec, vec, vec, vec), name=name,
        compiler_params=pltpu.CompilerParams(dimension_semantics=("arbitrary",),
                                             vmem_limit_bytes=int(min(need + VMEM_RESERVE, V7X_VMEM_BYTES - VMEM_RESERVE))),
    )(dx_out, f, g, u, x, post_g, gate, pre_g, scale, wd, wg_t, wu_t)


def _rope_tables(zero=0.0):
    half = QK_ROPE // 2
    freqs = ROPE_THETA ** (-jnp.arange(half, dtype=F32) / half)
    ang = (jnp.arange(SEQ, dtype=F32)[:, None] + zero) * freqs[None, :]
    cos, sin = jnp.cos(ang), jnp.sin(ang)
    ones = jnp.ones((SEQ, QK_NOPE), F32)
    zeros = jnp.zeros((SEQ, QK_NOPE), F32)
    pad1 = jnp.ones((SEQ, HEAD_PAD - QK_NOPE - QK_ROPE), F32)
    pad0 = jnp.zeros((SEQ, HEAD_PAD - QK_NOPE - QK_ROPE), F32)
    zh = jnp.zeros((SEQ, half), F32)
    c = jnp.concatenate([ones, cos, cos, pad1], axis=1)
    s1 = jnp.concatenate([zeros, -sin, zh, pad0], axis=1)
    s2 = jnp.concatenate([zeros, zh, sin, pad0], axis=1)
    return c, s1, s2


def _rope(v, c, s1, s2):
    half = QK_ROPE // 2
    return v * c + pltpu.roll(v, HEAD_PAD - half, 1) * s1 + pltpu.roll(v, half, 1) * s2


def _rope_t(dv, c, s1, s2):
    half = QK_ROPE // 2
    return dv * c + pltpu.roll(dv * s1, half, 1) + pltpu.roll(dv * s2, HEAD_PAD - half, 1)


def _mla_qkv(lat, q_norm, kv_norm, wq_t, wkv_t, rope, name):
    s_dim = lat.shape[0]
    width = MLA_HEADS * HEAD_PAD
    tm = 256

    def body(lat_ref, qg_ref, kg_ref, wq_ref, wkv_ref, c_ref, s1_ref, s2_ref, q_ref, k_ref, v_ref, qn_ref, kvn_ref):
        cq = lat_ref[:, :Q_LORA]
        ckv = lat_ref[:, Q_LORA:Q_LORA + KV_LORA]
        kr = lat_ref[:, Q_LORA + KV_LORA:]
        c, s1, s2 = c_ref[...], s1_ref[...], s2_ref[...]
        qn = (cq * _rstd(cq) * qg_ref[...]).astype(BF16)
        kvn = (ckv * _rstd(ckv) * kg_ref[...]).astype(BF16)
        qn_ref[...] = qn
        kvn_ref[...] = kvn
        q = _dot_nt(qn, wq_ref[...])
        kv = _dot_nt(kvn, wkv_ref[...])
        krr = _rope(kr, c, s1, s2)
        low = lax.broadcasted_iota(jnp.int32, (tm, HEAD_PAD), 1) < QK_NOPE
        for h in range(MLA_HEADS):
            sl = slice(h * HEAD_PAD, (h + 1) * HEAD_PAD)
            q_ref[:, sl] = _rope(q[:, sl], c, s1, s2).astype(BF16)
            kvh = kv[:, sl]
            k_ref[:, sl] = (jnp.where(low, kvh, 0.0) + krr).astype(BF16)
            v_ref[:, sl] = jnp.where(low, 0.0, kvh).astype(BF16)

    row = lambda n: pl.BlockSpec((tm, n), lambda i: (i, 0))
    full = lambda a: pl.BlockSpec(a.shape, lambda i: (0, 0))
    wide = jax.ShapeDtypeStruct((s_dim, width), BF16)
    return _pcall(
        body,
        out_shape=(wide, wide, wide, jax.ShapeDtypeStruct((s_dim, Q_LORA), BF16), jax.ShapeDtypeStruct((s_dim, KV_LORA), BF16)),
        grid=(s_dim // tm,),
        in_specs=[row(LAT_PAD), full(q_norm), full(kv_norm), full(wq_t), full(wkv_t), row(HEAD_PAD), row(HEAD_PAD), row(HEAD_PAD)],
        out_specs=(row(width), row(width), row(width), row(Q_LORA), row(KV_LORA)), name=name,
        compiler_params=_params(("parallel",), [((tm, LAT_PAD), F32), (wq_t.shape, BF16), (wkv_t.shape, BF16)]
                                + 3 * [((tm, width), BF16)], extra=4 * tm * width * 4),
    )(lat, q_norm, kv_norm, wq_t, wkv_t, *rope)


def _mla_scores(q, k_ref, t, tq):
    lo = t * tq
    own = slice(lo, lo + tq)
    scores = [(_dot_nt(q, k_ref[own, :]), own)]
    if t > 0:
        scores.append((_dot_nt(q, k_ref[0:lo, :]), slice(0, lo)))
    return scores


def _mla_softmax(scores):
    s_own = scores[0][0] * MLA_SCALE
    rows = lax.broadcasted_iota(jnp.int32, s_own.shape, 0)
    cols = lax.broadcasted_iota(jnp.int32, s_own.shape, 1)
    s_own = jnp.where(cols <= rows, s_own, -jnp.inf)
    mx = jnp.max(s_own, axis=-1, keepdims=True)
    if len(scores) == 1:
        e_own = jnp.exp(s_own - mx)
        return [(e_own * (1.0 / jnp.sum(e_own, axis=-1, keepdims=True)), scores[0][1])]
    s_pre = scores[1][0] * MLA_SCALE
    mx = jnp.maximum(mx, jnp.max(s_pre, axis=-1, keepdims=True))
    e_own, e_pre = jnp.exp(s_own - mx), jnp.exp(s_pre - mx)
    inv = 1.0 / (jnp.sum(e_own, axis=-1, keepdims=True) + jnp.sum(e_pre, axis=-1, keepdims=True))
    return [(e_pre * inv, scores[1][1]), (e_own * inv, scores[0][1])]


def _mla_attn_fwd(q, k, v, name):
    s_dim = q.shape[0]
    tq = 512

    def body(q_ref, k_ref, v_ref, o_ref):
        n_tiles = s_dim // tq
        tile_of = lambda t: slice(t * tq, (t + 1) * tq)
        def weighted_values(t, probs):
            o = None
            for p, keys in probs:
                part = _dot_nn(p, v_ref[keys, :])
                o = part if o is None else o + part
            o_ref[tile_of(t), :] = o.astype(BF16)

        scores = _mla_scores(q_ref[tile_of(0), :], k_ref, 0, tq)
        probs = None
        for t in range(n_tiles):
            ahead = _mla_scores(q_ref[tile_of(t + 1), :], k_ref, t + 1, tq) if t + 1 < n_tiles else None
            if probs is not None:
                weighted_values(t - 1, probs)
            probs = [(p.astype(BF16), keys) for p, keys in _mla_softmax(scores)]
            scores = ahead
        weighted_values(n_tiles - 1, probs)

    head = pl.BlockSpec((s_dim, HEAD_PAD), lambda h: (0, h))
    return _pcall(
        body, out_shape=jax.ShapeDtypeStruct(q.shape, BF16), grid=(MLA_HEADS,),
        in_specs=[head, head, head], out_specs=head, name=name,
        compiler_params=_params(("parallel",), 4 * [((s_dim, HEAD_PAD), BF16)], extra=4 * tq * s_dim * 4),
    )(q, k, v)


def _mla_attn_bwd(q, k, v, d_o, name):
    s_dim = q.shape[0]
    tq = 512

    def body(q_ref, k_ref, v_ref, do_ref, dq_ref, dk_ref, dv_ref):
        dk_ref[...] = jnp.zeros_like(dk_ref)
        dv_ref[...] = jnp.zeros_like(dv_ref)
        n_tiles = s_dim // tq
        tile_of = lambda t: slice(t * tq, (t + 1) * tq)

        def products(t):
            scores = _mla_scores(q_ref[tile_of(t), :], k_ref, t, tq)
            dot = do_ref[tile_of(t), :].astype(BF16)
            return scores, [_dot_nt(dot, v_ref[keys, :]) for _, keys in scores]

        def gradients_of_scores(scores, dps):
            probs = _mla_softmax(scores)
            dp_of = {(keys.start, keys.stop): dp for (_, keys), dp in zip(scores, dps)}
            terms = [(p, keys, dp_of[keys.start, keys.stop]) for p, keys in probs]
            row = None
            for p, _, dp in terms:
                part = jnp.sum(p * dp, axis=-1, keepdims=True)
                row = part if row is None else row + part
            return [((p * (dp - row) * MLA_SCALE).astype(BF16), p.astype(BF16), keys) for p, keys, dp in terms]

        def accumulate(t, terms):
            qt = q_ref[tile_of(t), :]
            dot = do_ref[tile_of(t), :].astype(BF16)
            dq = None
            for dsb, pb, keys in terms:
                part = _dot_nn(dsb, k_ref[keys, :])
                dq = part if dq is None else dq + part
                dk_ref[keys, :] += _dot_tn(dsb, qt)
                dv_ref[keys, :] += _dot_tn(pb, dot)
            dq_ref[tile_of(t), :] = dq

        ready = products(0)
        terms = None
        for t in range(n_tiles):
            ahead = products(t + 1) if t + 1 < n_tiles else None
            if terms is not None:
                accumulate(t - 1, terms)
            terms = gradients_of_scores(*ready)
            ready = ahead
        accumulate(n_tiles - 1, terms)

    head = pl.BlockSpec((s_dim, HEAD_PAD), lambda h: (0, h))
    out = jax.ShapeDtypeStruct(q.shape, F32)
    return _pcall(
        body, out_shape=(out, out, out), grid=(MLA_HEADS,),
        in_specs=[head, head, head, head], out_specs=(head, head, head), name=name,
        compiler_params=_params(("parallel",), 3 * [((s_dim, HEAD_PAD), BF16)] + 4 * [((s_dim, HEAD_PAD), F32)],
                                extra=6 * tq * s_dim * 4),
    )(q, k, v, d_o)


def _mla_qkv_bwd(dq, dk, dv, lat, q_norm, kv_norm, wq_t, wkv_t, rope, name):
    s_dim = lat.shape[0]
    width = MLA_HEADS * HEAD_PAD
    tm = 256

    def body(dq_ref, dk_ref, dv_ref, lat_ref, qg_ref, kg_ref, wq_ref, wkv_ref, c_ref, s1_ref, s2_ref,
             dqp_ref, dkv_ref, dlat_ref, dqg_ref, dkg_ref):
        @pl.when(pl.program_id(0) == 0)
        def _():
            dqg_ref[...] = jnp.zeros_like(dqg_ref)
            dkg_ref[...] = jnp.zeros_like(dkg_ref)

        c, s1, s2 = c_ref[...], s1_ref[...], s2_ref[...]
        lane = lax.broadcasted_iota(jnp.int32, (tm, HEAD_PAD), 1)
        low = lane < QK_NOPE
        rot = (lane >= QK_NOPE) & (lane < QK_NOPE + QK_ROPE)
        dkrr = jnp.zeros((tm, HEAD_PAD), F32)
        for h in range(MLA_HEADS):
            sl = slice(h * HEAD_PAD, (h + 1) * HEAD_PAD)
            dqp_ref[:, sl] = _rope_t(dq_ref[:, sl], c, s1, s2).astype(BF16)
            dkh = dk_ref[:, sl]
            dkv_ref[:, sl] = jnp.where(low, dkh, dv_ref[:, sl]).astype(BF16)
            dkrr = dkrr + jnp.where(rot, dkh, 0.0)
        dqn = _dot_nn(dqp_ref[...], wq_ref[...])
        dkvn = _dot_nn(dkv_ref[...], wkv_ref[...])
        cq = lat_ref[:, :Q_LORA]
        ckv = lat_ref[:, Q_LORA:Q_LORA + KV_LORA]
        rq, rkv = _rstd(cq), _rstd(ckv)
        dqg_ref[...] += jnp.sum(dqn * cq * rq, axis=0, keepdims=True)
        dkg_ref[...] += jnp.sum(dkvn * ckv * rkv, axis=0, keepdims=True)
        dlat_ref[:, :Q_LORA] = _rms_bwd(cq, rq, dqn * qg_ref[...])
        dlat_ref[:, Q_LORA:Q_LORA + KV_LORA] = _rms_bwd(ckv, rkv, dkvn * kg_ref[...])
        dlat_ref[:, Q_LORA + KV_LORA:] = _rope_t(dkrr, c, s1, s2)

    row = lambda n: pl.BlockSpec((tm, n), lambda i: (i, 0))
    full = lambda a: pl.BlockSpec(a.shape, lambda i: (0, 0))
    wide = jax.ShapeDtypeStruct((s_dim, width), BF16)
    return _pcall(
        body,
        out_shape=(wide, wide, jax.ShapeDtypeStruct((s_dim, LAT_PAD), F32),
                   jax.ShapeDtypeStruct(q_norm.shape, F32), jax.ShapeDtypeStruct(kv_norm.shape, F32)),
        grid=(s_dim // tm,),
        in_specs=[row(width), row(width), row(width), row(LAT_PAD), full(q_norm), full(kv_norm), full(wq_t), full(wkv_t),
                  row(HEAD_PAD), row(HEAD_PAD), row(HEAD_PAD)],
        out_specs=(row(width), row(width), row(LAT_PAD), full(q_norm), full(kv_norm)), name=name,
        compiler_params=_params(("arbitrary",), 3 * [((tm, width), F32)] + [((tm, LAT_PAD), F32), (wq_t.shape, BF16),
                                                                           (wkv_t.shape, BF16)] + 2 * [((tm, width), BF16)],
                                extra=2 * tm * width * 4),
    )(dq, dk, dv, lat, q_norm, kv_norm, wq_t, wkv_t, *rope)


def _t5_bucket(dist):
    max_exact = N_BUCKETS // 2
    d = jnp.maximum(dist, 1).astype(F32)
    large = max_exact + (jnp.log(d / max_exact) / math.log(MAX_DISTANCE / max_exact)
                         * (N_BUCKETS - max_exact)).astype(jnp.int32)
    large = jnp.minimum(large, N_BUCKETS - 1)
    return jnp.where(dist < max_exact, dist, large)


def _dil_buckets(dilation):
    iq = jnp.arange(DIL_BLOCK)[:, None]
    ik = jnp.arange(2 * DIL_BLOCK)[None, :]
    return _t5_bucket(jnp.maximum(DIL_BLOCK + iq - ik, 0) * dilation)


def _dil_logits(qh, kb, bias_h, first, span):
    if first:
        s = _dot_nt(qh, kb) * DIL_SCALE + bias_h[:, DIL_BLOCK:]
        rel = lax.broadcasted_iota(jnp.int32, s.shape, 0) - lax.broadcasted_iota(jnp.int32, s.shape, 1)
    else:
        s = _dot_nt(qh, kb) * DIL_SCALE + bias_h
        rel = DIL_BLOCK + lax.broadcasted_iota(jnp.int32, s.shape, 0) - lax.broadcasted_iota(jnp.int32, s.shape, 1)
    return jnp.where((rel >= 0) & (rel <= span), s, -jnp.inf)


def _dil_blocks(s_dim, dilation):
    rows = s_dim // dilation
    for r in range(dilation):
        for n in range(rows // DIL_BLOCK):
            lo = r * rows + n * DIL_BLOCK
            keys = slice(lo, lo + DIL_BLOCK) if n == 0 else slice(lo - DIL_BLOCK, lo + DIL_BLOCK)
            start = r + n * DIL_BLOCK * dilation
            tokens = slice(start, start + DIL_BLOCK) if dilation == 1 else pl.ds(start, DIL_BLOCK, stride=dilation)
            yield n == 0, slice(lo, lo + DIL_BLOCK), keys, tokens


def _dil_views(s_dim):
    col = lambda which: pl.BlockSpec((s_dim, HEAD_PAD), lambda p: (0, which * DIL_PAIRS + p))
    nat = pl.BlockSpec((s_dim, HEAD_PAD), lambda p: (0, p))
    bias = pl.BlockSpec((2, DIL_BLOCK, 2 * DIL_BLOCK), lambda p: (p, 0, 0))
    return col, nat, bias


def _dil_attn_fwd(qkv, bias, dilation, span, name):
    s_dim = qkv.shape[0]
    d_dim = DIL_HEADS * DIL_HEAD_DIM
    col, nat, bias_spec = _dil_views(s_dim)

    def body(q_ref, k_ref, v_ref, b_ref, o_ref, l_ref):
        lane = lax.broadcasted_iota(jnp.int32, (DIL_BLOCK, HEAD_PAD), 1)
        klane = lax.broadcasted_iota(jnp.int32, (2 * DIL_BLOCK, HEAD_PAD), 1)
        blocks = list(_dil_blocks(s_dim, dilation))
        for g0 in range(0, len(blocks), DIL_GROUPED):
            group = blocks[g0:g0 + DIL_GROUPED]
            logits = [_dil_logits(jnp.where((lane < DIL_HEAD_DIM) == (h == 0), q_ref[blk, :], 0), k_ref[keys, :], b_ref[h],
                                  first, span) for first, blk, keys, _ in group for h in range(2)]
            soft = []
            for lg in logits:
                mx = jnp.max(lg, axis=-1, keepdims=True)
                e = jnp.exp(lg - mx)
                tot = jnp.sum(e, axis=-1, keepdims=True)
                soft.append(((e * (1.0 / tot)).astype(BF16), mx + jnp.log(tot)))
            for i, (_, _, keys, tokens) in enumerate(group):
                vb = v_ref[keys, :]
                o_acc = jnp.zeros((DIL_BLOCK, HEAD_PAD), F32)
                lse_acc = jnp.zeros((DIL_BLOCK, HEAD_PAD), F32)
                for h in range(2):
                    p, lse = soft[2 * i + h]
                    kmine = (klane[:vb.shape[0]] < DIL_HEAD_DIM) == (h == 0)
                    o_acc = o_acc + _dot_nn(p, jnp.where(kmine, vb, 0))
                    lse_acc = jnp.where((lane < DIL_HEAD_DIM) == (h == 0), lse, lse_acc)
                o_ref[tokens, :] = o_acc
                l_ref[tokens, :] = lse_acc

    out = jax.ShapeDtypeStruct((s_dim, d_dim), F32)
    return _pcall(
        body, out_shape=(out, out), grid=(DIL_PAIRS,),
        in_specs=[col(0), col(1), col(2), bias_spec], out_specs=(nat, nat), name=name,
        compiler_params=_params(("parallel",), 3 * [((s_dim, HEAD_PAD), BF16)] + 2 * [((s_dim, HEAD_PAD), F32)]
                                + [((2, DIL_BLOCK, 2 * DIL_BLOCK), F32)], extra=2**21),
    )(qkv, qkv, qkv, bias)


def _dil_mix(lses, outs, name):
    s_dim, d_dim = outs[0].shape
    tm = TOKEN_TILE
    ng = len(outs)

    def body(*refs):
        ls = [refs[g][...] for g in range(ng)]
        mx = ls[0]
        for g in range(1, ng):
            mx = jnp.maximum(mx, ls[g])
        es = [jnp.exp(l - mx) for l in ls]
        tot = es[0]
        for g in range(1, ng):
            tot = tot + es[g]
        o = None
        for g in range(ng):
            al = es[g] / tot
            refs[2 * ng + g][...] = al
            t = al * refs[ng + g][...]
            o = t if o is None else o + t
        refs[3 * ng][...] = o
        refs[3 * ng + 1][...] = o.astype(BF16)

    row = pl.BlockSpec((tm, d_dim), lambda i: (i, 0))
    f = jax.ShapeDtypeStruct((s_dim, d_dim), F32)
    res = _pcall(
        body, out_shape=tuple(ng * [f] + [f, jax.ShapeDtypeStruct((s_dim, d_dim), BF16)]), grid=(s_dim // tm,),
        in_specs=2 * ng * [row], out_specs=tuple((ng + 2) * [row]), name=name,
        compiler_params=_params(("parallel",), (3 * ng + 2) * [((tm, d_dim), F32)], extra=4 * tm * d_dim * 4),
    )(*lses, *outs)
    return res[:ng], res[ng], res[ng + 1]


def _dil_attn_bwd(qkv, bias, d_o, o_mix, alpha, lse, dilation, span, name):
    s_dim = qkv.shape[0]
    d_dim = DIL_HEADS * DIL_HEAD_DIM
    col, nat, bias_spec = _dil_views(s_dim)

    def body(q_ref, k_ref, v_ref, b_ref, do_ref, om_ref, al_ref, l_ref, dq_ref, dk_ref, dv_ref, db_ref, dk_acc, dv_acc):
        db_ref[...] = jnp.zeros_like(db_ref)
        dk_acc[...] = jnp.zeros_like(dk_acc)
        dv_acc[...] = jnp.zeros_like(dv_acc)
        lane = lax.broadcasted_iota(jnp.int32, (DIL_BLOCK, HEAD_PAD), 1)
        klane = lax.broadcasted_iota(jnp.int32, (2 * DIL_BLOCK, HEAD_PAD), 1)
        blocks = list(_dil_blocks(s_dim, dilation))
        heads = [(lane < DIL_HEAD_DIM) == (h == 0) for h in range(2)]
        for g0 in range(0, len(blocks), DIL_GROUPED):
            group = blocks[g0:g0 + DIL_GROUPED]
            staged = []
            for first, blk, kv_rows, tokens in group:
                qb, kb, vb = q_ref[blk, :], k_ref[kv_rows, :], v_ref[kv_rows, :]
                dog = al_ref[tokens, :] * do_ref[tokens, :]
                row_term = dog * om_ref[tokens, :]
                lse_b = l_ref[tokens, :]
                for h in range(2):
                    qh = jnp.where(heads[h], qb, 0)
                    dogh = jnp.where(heads[h], dog, 0.0).astype(BF16)
                    staged.append((_dil_logits(qh, kb, b_ref[h], first, span), _dot_nt(dogh, vb), qh, dogh,
                                   jnp.max(jnp.where(heads[h], lse_b, -jnp.inf), axis=-1, keepdims=True),
                                   jnp.sum(jnp.where(heads[h], row_term, 0.0), axis=-1, keepdims=True)))
            grads = []
            for i, (logits, dp, qh, dogh, lse_h, row) in enumerate(staged):
                p = jnp.exp(logits - lse_h)
                ds = p * (dp - row)
                if group[i // 2][0]:
                    db_ref[i % 2, :, DIL_BLOCK:] += ds
                else:
                    db_ref[i % 2] += ds
                grads.append(((ds * DIL_SCALE).astype(BF16), p.astype(BF16), qh, dogh))
            for i, (_, blk, kv_rows, _) in enumerate(group):
                kb = k_ref[kv_rows, :]
                dq_acc = jnp.zeros((DIL_BLOCK, HEAD_PAD), F32)
                dk_blk = jnp.zeros((kb.shape[0], HEAD_PAD), F32)
                dv_blk = jnp.zeros((kb.shape[0], HEAD_PAD), F32)
                for h in range(2):
                    dsb, pb, qh, dogh = grads[2 * i + h]
                    kmine = (klane[:kb.shape[0]] < DIL_HEAD_DIM) == (h == 0)
                    dq_acc = dq_acc + _dot_nn(dsb, jnp.where(kmine, kb, 0))
                    dk_blk = dk_blk + _dot_tn(dsb, qh)
                    dv_blk = dv_blk + _dot_tn(pb, dogh)
                dq_ref[blk, :] = dq_acc.astype(BF16)
                dk_acc[kv_rows, :] += dk_blk
                dv_acc[kv_rows, :] += dv_blk
        dk_ref[...] = dk_acc[...].astype(BF16)
        dv_ref[...] = dv_acc[...].astype(BF16)

    grad = jax.ShapeDtypeStruct((s_dim, d_dim), BF16)
    return _pcall(
        body, out_shape=(grad, grad, grad, jax.ShapeDtypeStruct(bias.shape, F32)), grid=(DIL_PAIRS,),
        in_specs=[col(0), col(1), col(2), bias_spec, nat, nat, nat, nat],
        out_specs=(nat, nat, nat, bias_spec), name=name,
        scratch_shapes=[pltpu.VMEM((s_dim, HEAD_PAD), F32), pltpu.VMEM((s_dim, HEAD_PAD), F32)],
        compiler_params=_params(("parallel",), 6 * [((s_dim, HEAD_PAD), BF16)] + 4 * [((s_dim, HEAD_PAD), F32)]
                                + 2 * [((2, DIL_BLOCK, 2 * DIL_BLOCK), F32)], extra=2 * s_dim * HEAD_PAD * 4 + 2**21),
    )(qkv, qkv, qkv, bias, d_o, o_mix, alpha, lse)


def _bias_reduce(dbias, buckets, name):
    n_heads = dbias.shape[0]

    def body(db_ref, bk_ref, o_ref):
        ds, bk = db_ref[0], bk_ref[0]
        lane = lax.broadcasted_iota(jnp.int32, (8, HEAD_PAD), 1)
        acc = jnp.zeros((8, HEAD_PAD), F32)
        for b in range(N_BUCKETS):
            acc = jnp.where(lane == b, jnp.sum(jnp.where(bk == b, ds, 0.0)), acc)
        o_ref[0] = acc

    blk = (1, DIL_BLOCK, 2 * DIL_BLOCK)
    return _pcall(
        body, out_shape=jax.ShapeDtypeStruct((n_heads, 8, HEAD_PAD), F32), grid=(n_heads,),
        in_specs=[pl.BlockSpec(blk, lambda h: (h, 0, 0)), pl.BlockSpec(blk, lambda h: (h // DIL_HEADS, 0, 0))],
        out_specs=pl.BlockSpec((1, 8, HEAD_PAD), lambda h: (h, 0, 0)), name=name,
        compiler_params=_params(("parallel",), [(blk, F32), (blk, jnp.int32)], extra=2**20),
    )(dbias, buckets)


def _loss_grad(y, target, name):
    s_dim, d_dim = y.shape
    tm = TOKEN_TILE

    def body(y_ref, t_ref, dy_ref, l_ref):
        @pl.when(pl.program_id(0) == 0)
        def _():
            l_ref[...] = jnp.zeros_like(l_ref)

        err = y_ref[...] - t_ref[...]
        dy_ref[...] = err / d_dim
        sq = (err * err).reshape(tm // 8, 8, d_dim)
        l_ref[...] += 0.5 * jnp.sum(sq, axis=0) / d_dim

    row = pl.BlockSpec((tm, d_dim), lambda i: (i, 0))
    acc = pl.BlockSpec((8, d_dim), lambda i: (0, 0))
    return _pcall(
        body, out_shape=(jax.ShapeDtypeStruct((s_dim, d_dim), F32), jax.ShapeDtypeStruct((8, d_dim), F32)),
        grid=(s_dim // tm,), in_specs=[row, row], out_specs=(row, acc), name=name,
        compiler_params=_params(("arbitrary",), 3 * [((tm, d_dim), F32)], extra=2 * tm * d_dim * 4),
    )(y, target)


def _mod_fwd(c_all, w_mod, b_loc, name):
    depth, d_dim, n = w_mod.shape
    nb = c_all.shape[0]

    def body(c_ref, w_ref, b_ref, o_ref, s_ref):
        cv = c_ref[...]
        sc = cv * jax.nn.sigmoid(cv)
        s_ref[...] = sc
        o_ref[0] = _dot_nn(sc.astype(BF16), w_ref[0].astype(BF16)) + b_ref[0]

    return _pcall(
        body, out_shape=(jax.ShapeDtypeStruct((depth, nb, n), F32), jax.ShapeDtypeStruct((nb, d_dim), F32)), grid=(depth,),
        in_specs=[pl.BlockSpec((nb, d_dim), lambda i: (0, 0)), pl.BlockSpec((1, d_dim, n), lambda i: (i, 0, 0)),
                  pl.BlockSpec((1, 1, n), lambda i: (i, 0, 0))],
        out_specs=(pl.BlockSpec((1, nb, n), lambda i: (i, 0, 0)), pl.BlockSpec((nb, d_dim), lambda i: (0, 0))), name=name,
        compiler_params=_params(("arbitrary",), [((1, d_dim, n), F32)], extra=d_dim * n * 2 + 2**20),
    )(c_all, w_mod, b_loc.reshape(depth, 1, n))


def _sum_parts(parts, name, transpose=False):
    _, rows, cols = parts.shape
    unit = 128 if transpose else 16
    budget = (7 if transpose else 3) * 2**20
    fits = [t for t in range(unit, rows // 2 + 1, unit) if rows % t == 0 and NDEV * t * cols * parts.dtype.itemsize <= budget]
    tr = max(fits) if fits else rows

    def body(p_ref, o_ref):
        acc = p_ref[0].astype(F32)
        for k in range(1, NDEV):
            acc = acc + p_ref[k].astype(F32)
        o_ref[...] = acc.T if transpose else acc

    out_shape, out_block = ((cols, rows), (cols, tr)) if transpose else ((rows, cols), (tr, cols))
    return _pcall(
        body, out_shape=jax.ShapeDtypeStruct(out_shape, F32), grid=(rows // tr,),
        in_specs=[pl.BlockSpec((NDEV, tr, cols), lambda i: (0, i, 0))],
        out_specs=pl.BlockSpec(out_block, (lambda i: (0, i)) if transpose else (lambda i: (i, 0))),
        name=name, compiler_params=_params(("parallel",), [((NDEV, tr, cols), parts.dtype), (out_block, F32)], extra=2**22),
    )(parts)


def _adamw(w, g, m, v, name):
    shape = w.shape
    cols = shape[-1]
    rows = math.prod(shape[:-1])
    tr = rows
    for cand in (2048, 1024, 512, 256, 128, 64, 32, 16, 8):
        if rows % cand == 0 and rows > cand and cand * cols * 4 <= 2**21:
            tr = cand
            break

    def body(w_ref, g_ref, m_ref, v_ref, d_ref, mo_ref, vo_ref):
        gv = g_ref[...]
        mn = ADAM_B1 * m_ref[...] + (1.0 - ADAM_B1) * gv
        vn = ADAM_B2 * v_ref[...] + (1.0 - ADAM_B2) * (gv * gv)
        m_hat = mn / (1.0 - ADAM_B1 ** ADAM_STEP)
        v_hat = vn / (1.0 - ADAM_B2 ** ADAM_STEP)
        d_ref[...] = -ADAM_LR * (m_hat / (jnp.sqrt(v_hat) + ADAM_EPS) + ADAM_WD * w_ref[...])
        mo_ref[...] = mn
        vo_ref[...] = vn

    blk = pl.BlockSpec((tr, cols), lambda i: (i, 0))
    out = jax.ShapeDtypeStruct((rows, cols), F32)
    res = _pcall(
        body, out_shape=(out, out, out), grid=(rows // tr,), in_specs=4 * [blk], out_specs=(blk, blk, blk), name=name,
        compiler_params=_params(("parallel",), 7 * [((tr, cols), F32)], extra=4 * tr * cols * 4),
    )(*(a.reshape(rows, cols) for a in (w, g, m, v)))
    return tuple(r.reshape(shape) for r in res)


def _peers():
    x, y, c = lax.axis_index("x"), lax.axis_index("y"), lax.axis_index("c")
    flip = lambda v, f: 1 - v if f else v
    peers = []
    for f in range(1, NDEV):
        px, py, pc = flip(x, f & 4), flip(y, f & 2), flip(c, f & 1)
        peers.append(((px, py, pc), 4 * px + 2 * py + pc))
    return (x, y, c), 4 * x + 2 * y + c, peers


def _places():
    x, y, c = lax.axis_index("x"), lax.axis_index("y"), lax.axis_index("c")
    place = lambda px, py, pc: ((px, py, pc), 4 * px + 2 * py + pc)
    return place(x, y, c), place(x, y, 1 - c), [place(1 - x, y, c), place(x, 1 - y, c), place(1 - x, 1 - y, c)]


def _exchange(arrs, gather, name):
    n = len(arrs)
    hbm = pl.BlockSpec(memory_space=pltpu.HBM)
    if gather:
        out_shape = [jax.ShapeDtypeStruct((NDEV * a.shape[0], a.shape[1]), a.dtype) for a in arrs]
    else:
        out_shape = [jax.ShapeDtypeStruct((NDEV, a.shape[0] // NDEV, a.shape[1]), a.dtype) for a in arrs]

    def body(*refs):
        ins, outs = refs[:n], refs[n:2 * n]
        send_sems, recv_sems, local_sems = refs[2 * n:]
        me_pos, me, peers = _peers()
        local = []
        for k in range(n):
            rows = arrs[k].shape[0] if gather else arrs[k].shape[0] // NDEV
            if gather:
                src_of = lambda idx: ins[k]
                dst_of = lambda idx: outs[k].at[pl.ds(me * rows, rows)]
                mine = (ins[k], outs[k].at[pl.ds(me * rows, rows)])
            else:
                src_of = lambda idx: ins[k].at[pl.ds(idx * rows, rows)]
                dst_of = lambda idx: outs[k].at[me]
                mine = (ins[k].at[pl.ds(me * rows, rows)], outs[k].at[me])
            cp = pltpu.make_async_copy(mine[0], mine[1], local_sems.at[k])
            cp.start()
            local.append(cp)
            for pos, idx in peers:
                pltpu.make_async_remote_copy(src_ref=src_of(idx), dst_ref=dst_of(idx), send_sem=send_sems.at[k],
                                             recv_sem=recv_sems.at[k], device_id=pos, device_id_type=MESH).start()
        for k in range(n):
            rows = arrs[k].shape[0] if gather else arrs[k].shape[0] // NDEV
            sent = ins[k].at[pl.ds(0, (NDEV - 1) * rows)] if not gather else outs[k].at[pl.ds(0, (NDEV - 1) * rows)]
            got = outs[k].at[pl.ds(0, (NDEV - 1) * rows)] if gather else outs[k].at[pl.ds(0, NDEV - 1)]
            pltpu.make_async_remote_copy(src_ref=sent, dst_ref=sent, send_sem=send_sems.at[k], recv_sem=recv_sems.at[k],
                                         device_id=me_pos, device_id_type=MESH).wait_send()
            pltpu.make_async_remote_copy(src_ref=got, dst_ref=got, send_sem=send_sems.at[k], recv_sem=recv_sems.at[k],
                                         device_id=me_pos, device_id_type=MESH).wait_recv()
            local[k].wait()

    return pl.pallas_call(
        body, out_shape=out_shape, in_specs=n * [hbm], out_specs=n * [hbm], name=name,
        scratch_shapes=[pltpu.SemaphoreType.DMA((n,)), pltpu.SemaphoreType.DMA((n,)), pltpu.SemaphoreType.DMA((n,))],
        compiler_params=pltpu.CompilerParams(has_side_effects=True),
    )(*arrs)


_HBM = pl.BlockSpec(memory_space=pltpu.HBM)
_SEM = pl.BlockSpec(memory_space=pltpu.SEMAPHORE)
_DATAFLOW = pltpu.SideEffectType.DATAFLOW_SIDE_EFFECTING


def _split_start(srcs, groups, gather, name):
    n = len(srcs)
    if gather:
        lands = [lax.empty((NDEV * a.shape[0], a.shape[1]), a.dtype) for a in srcs]
    else:
        lands = [lax.empty((NDEV, a.shape[0] // NDEV, a.shape[1]), a.dtype) for a in srcs]
    n_sem = 3 * len(groups)

    def body(*refs):
        src_refs, land_refs = refs[:n], refs[n:2 * n]
        sems = refs[2 * n:2 * n + n_sem]
        token = refs[-1]
        (_, my), sibling, chips = _places()
        _, _, peers = _peers()
        targets = [sibling] + chips if gather else peers
        for g, members in enumerate(groups):
            for j, k in enumerate(members):
                _own_copy(src_refs[k], land_refs[k], sems[3 * g + 2].at[j], my, gather).start()
        for g, members in enumerate(groups):
            for j, k in enumerate(members):
                rows = srcs[k].shape[0] if gather else srcs[k].shape[0] // NDEV
                for pos, idx in targets:
                    src = src_refs[k] if gather else src_refs[k].at[pl.ds(idx * rows, rows)]
                    dst = land_refs[k].at[pl.ds(my * rows, rows)] if gather else land_refs[k].at[my]
                    pltpu.make_async_remote_copy(src_ref=src, dst_ref=dst, send_sem=sems[3 * g].at[j],
                                                 recv_sem=sems[3 * g + 1].at[j], device_id=pos, device_id_type=MESH).start()
        token[...] = jnp.zeros_like(token)

    out_shape = []
    for members in groups:
        out_shape += 3 * [pltpu.SemaphoreType.DMA((len(members),))]
    out_shape += [pltpu.HBM(a.shape, a.dtype) for a in srcs] + [pltpu.HBM(a.shape, a.dtype) for a in lands]
    out_shape.append(jax.ShapeDtypeStruct((8, 128), F32))
    res = pl.pallas_call(
        body, name=name, out_shape=tuple(out_shape), in_specs=2 * n * [_HBM],
        out_specs=tuple(n_sem * [_SEM] + 2 * n * [_HBM] + [pl.BlockSpec(memory_space=pltpu.VMEM)]),
        input_output_aliases={i: n_sem + i for i in range(2 * n)},
        compiler_params=pltpu.CompilerParams(has_side_effects=_DATAFLOW),
    )(*[pltpu.with_memory_space_constraint(a, pltpu.HBM) for a in list(srcs) + lands])
    sems = [tuple(res[3 * g:3 * g + 3]) for g in range(len(groups))]
    return sems, list(res[n_sem:n_sem + n]), list(res[n_sem + n:n_sem + 2 * n]), res[-1]


def _own_copy(src_ref, land_ref, sem, my, gather):
    if gather:
        rows = src_ref.shape[0]
        return pltpu.make_async_copy(src_ref, land_ref.at[pl.ds(my * rows, rows)], sem)
    rows = src_ref.shape[0] // NDEV
    return pltpu.make_async_copy(src_ref.at[pl.ds(my * rows, rows)], land_ref.at[my], sem)


def _wait_all(land_ref, blocks_per_dev, copies, send_sem, recv_sem, me_pos):
    part = land_ref.at[pl.ds(0, copies * blocks_per_dev)]
    pltpu.make_async_remote_copy(src_ref=part, dst_ref=part, send_sem=send_sem, recv_sem=recv_sem,
                                 device_id=me_pos, device_id_type=MESH).wait()


def _gather_forward(sems, srcs, lands, after, name):
    n = len(srcs)

    def body(*refs):
        land_refs = refs[n:2 * n]
        send_a, recv_a = refs[2 * n], refs[2 * n + 1]
        send_b, recv_b = refs[2 * n + 3], refs[2 * n + 4]
        token = refs[-1]
        (me_pos, _), sibling, chips = _places()
        for j in range(n):
            _wait_all(land_refs[j], lands[j].shape[0] // NDEV, 1 + OTHER_CHIPS, send_a.at[j], recv_a.at[j], me_pos)
        for j in range(n):
            rows = lands[j].shape[0] // NDEV
            for _, idx in chips:
                block = land_refs[j].at[pl.ds(idx * rows, rows)]
                pltpu.make_async_remote_copy(src_ref=block, dst_ref=block, send_sem=send_b.at[j], recv_sem=recv_b.at[j],
                                             device_id=sibling[0], device_id_type=MESH).start()
        token[...] = jnp.zeros_like(token)

    res = pl.pallas_call(
        body, name=name,
        out_shape=(pltpu.SemaphoreType.DMA((n,)), pltpu.SemaphoreType.DMA((n,)))
        + tuple(pltpu.HBM(a.shape, a.dtype) for a in list(srcs) + list(lands)) + (jax.ShapeDtypeStruct((8, 128), F32),),
        in_specs=2 * n * [_HBM] + [_SEM, _SEM, pl.BlockSpec(memory_space=pl.ANY)],
        out_specs=tuple([_SEM, _SEM] + 2 * n * [_HBM] + [pl.BlockSpec(memory_space=pltpu.VMEM)]),
        input_output_aliases={i: 2 + i for i in range(2 * n)},
        compiler_params=pltpu.CompilerParams(has_side_effects=_DATAFLOW),
    )(*srcs, *lands, sems[0], sems[1], after)
    return (res[0], res[1]), list(res[2:2 + n]), list(res[2 + n:2 + 2 * n]), res[-1]


def _split_wait(sems, srcs, lands, after, copies, gather, name):
    n = len(srcs)

    def body(*refs):
        src_refs, land_refs = refs[:n], refs[n:2 * n]
        send_sem, recv_sem, local_sem = refs[2 * n], refs[2 * n + 1], refs[2 * n + 2]
        (me_pos, my), _, _ = _places()
        for j in range(n):
            _wait_all(land_refs[j], lands[j].shape[0] // NDEV, copies, send_sem.at[j], recv_sem.at[j], me_pos)
            _own_copy(src_refs[j], land_refs[j], local_sem.at[j], my, gather).wait()

    res = pl.pallas_call(
        body, name=name, out_shape=tuple(pltpu.HBM(a.shape, a.dtype) for a in list(srcs) + list(lands)),
        in_specs=2 * n * [_HBM] + [_SEM, _SEM, _SEM, pl.BlockSpec(memory_space=pl.ANY)], out_specs=tuple(2 * n * [_HBM]),
        input_output_aliases={i: i for i in range(2 * n)},
        compiler_params=pltpu.CompilerParams(has_side_effects=_DATAFLOW),
    )(*srcs, *lands, sems[0], sems[1], sems[2], after)
    return list(res[n:])


def _chained(gate, mid, after):
    return gate if mid is None else gate + mid(after)[:1, :1]


def _ffn_fwd(x, norms, mod, w, mid=None):
    (pre_g, post_g), (shift, scale, gate), (wg_t, wu_t, wd) = norms, mod, w
    if not callable(wd):
        hn, g, u, a, x_out, f = _ffn_fwd_fused(x, pre_g, scale, shift, post_g, _chained(gate, mid, x), wg_t, wu_t, wd, "ffn_fwd")
        return x_out, (x, hn, g, u, a, f), (wg_t, wu_t, wd)
    hn, g, u, a = _ffn_up(x, pre_g, scale, shift, wg_t, wu_t, "ffn_up")
    wd = wd(a)
    x_out, f = _mm_post(a, wd, x, post_g, _chained(gate, mid, a), FFN_RES, "ffn_down")
    return x_out, (x, hn, g, u, a, f), (wg_t, wu_t, wd)


def _ffn_bwd(dx_out, saved, norms, mod, w, send=None):
    (pre_g, post_g), (_, scale, gate), (wg_t, wu_t, wd) = norms, mod, w
    x, hn, g, u, a, f = saved
    d_model = x.shape[1]
    if send is None:
        df, dg, du, dx, dgate, dpost, dshift, dscale, dpre = _ffn_bwd_fused(dx_out, saved, pre_g, post_g, scale, gate,
                                                                            wg_t, wu_t, wd, "ffn_bwd")
        return dx, (dpre, dpost), (dshift, dscale, dgate), tuple(_ffn_dw(dg, du, a, hn, df, "ffn_dw3"))
    sent = send
    df, dgate, dpost = _post_bwd(dx_out, f, post_g, gate, FFN_RES, "ffn_post_bwd")
    dwd = _mm([(a, df)], "tn", BF16, 256, d_model, "ffn_dw")
    dg, du = _ffn_dgu(df, wd, g, u, "ffn_dgu", after=sent(2, dwd))
    dwg_t = _mm([(dg, hn)], "tn", BF16, 256, d_model, "ffn_dw")
    dwu_t = _mm([(du, hn)], "tn", BF16, 256, d_model, "ffn_dw", after=sent(0, dwg_t))
    dhn = _mm([(dg, wg_t), (du, wu_t)], "nn", F32, TOKEN_TILE, d_model, "ffn_dhn", after=sent(1, dwu_t))
    dx, dshift, dscale, dpre = _prenorm_bwd(dx_out, [dhn], x, pre_g, scale, "prenorm_bwd")
    return dx, (dpre, dpost), (dshift, dscale, dgate), (dwg_t, dwu_t, dwd)


def _mla_fwd(x, norms, mod, w, rope, mid=None):
    (pre_g, post_g), (shift, scale, gate) = norms, mod
    w_in, q_norm, wq_t, kv_norm, wkv_t, wo = w
    hn, lat = _prenorm_mm(x, pre_g, scale, shift, w_in, "nn", F32, LAT_PAD, "mla_in")
    gate = _chained(gate, mid, lat)
    q, k, v, qn, kvn = _mla_qkv(lat, q_norm, kv_norm, wq_t, wkv_t, rope, "mla_qkv")
    o = _mla_attn_fwd(q, k, v, "mla_attn_fwd")
    x_out, f = _mm_post(o, wo, x, post_g, gate, 1.0, "mla_out")
    return x_out, (x, hn, lat, q, k, v, qn, kvn, o, f)


def _mla_bwd(dx_out, saved, norms, mod, w, rope):
    (pre_g, post_g), (_, scale, gate) = norms, mod
    w_in, q_norm, wq_t, kv_norm, wkv_t, wo = w
    x, hn, lat, q, k, v, qn, kvn, o, f = saved
    d_model = x.shape[1]
    df, dgate, dpost = _post_bwd(dx_out, f, post_g, gate, 1.0, "mix_post_bwd")
    d_o = _mm([(df, wo)], "nt", F32, TOKEN_TILE, wo.shape[0], "mla_do")
    dwo = _mm([(o, df)], "tn", BF16, TOKEN_TILE, d_model, "mla_dwo")
    dq, dk, dv = _mla_attn_bwd(q, k, v, d_o, "mla_attn_bwd")
    dqp, dkv, dlat, dq_norm, dkv_norm = _mla_qkv_bwd(dq, dk, dv, lat, q_norm, kv_norm, wq_t, wkv_t, rope, "mla_qkv_bwd")
    dwq_t = _mm([(dqp, qn)], "tn", BF16, TOKEN_TILE, Q_LORA, "mla_dwq")
    dwkv_t = _mm([(dkv, kvn)], "tn", BF16, TOKEN_TILE, KV_LORA, "mla_dwkv")
    dw_in = _mm([(hn, dlat)], "tn", BF16, TOKEN_TILE, LAT_PAD, "mla_dwin")
    dhn = _mm([(dlat, w_in)], "nt", F32, TOKEN_TILE, d_model, "mla_dhn")
    dx, dshift, dscale, dpre = _prenorm_bwd(dx_out, [dhn], x, pre_g, scale, "prenorm_bwd")
    return dx, (dpre, dpost), (dshift, dscale, dgate), (dw_in, dq_norm, dwq_t, dkv_norm, dwkv_t, dwo)


def _dil_fwd(x, norms, mod, w, bias, mid=None):
    (pre_g, post_g), (shift, scale, gate), (w_in_t, wo) = norms, mod, w
    width = 3 * DIL_HEADS * DIL_HEAD_DIM
    hns, qkvs, outs, lses = [], [], [], []
    for g, (window, dilation) in enumerate(DIL_GROUPS):
        hn, qkv = _prenorm_mm(x, pre_g, scale, shift, w_in_t, "nt", BF16, width, "dil_in", perm=dilation,
                              w_rows=(g * width, width))
        if g == 0:
            gate = _chained(gate, mid, qkv)
        o, lse = _dil_attn_fwd(qkv, bias[g], dilation, window // dilation, "dil_attn_fwd")
        hns.append(hn), qkvs.append(qkv), outs.append(o), lses.append(lse)
    alphas, o_mix, o_mix_b = _dil_mix(lses, outs, "dil_mix")
    x_out, f = _mm_post(o_mix_b, wo, x, post_g, gate, 1.0, "dil_out")
    return x_out, (x, hns, qkvs, lses, alphas, o_mix, o_mix_b, f)


def _dil_bwd(dx_out, saved, norms, mod, w, bias):
    (pre_g, post_g), (_, scale, gate), (w_in_t, wo) = norms, mod, w
    x, hns, qkvs, lses, alphas, o_mix, o_mix_b, f = saved
    d_model = x.shape[1]
    inner = DIL_HEADS * DIL_HEAD_DIM
    df, dgate, dpost = _post_bwd(dx_out, f, post_g, gate, 1.0, "mix_post_bwd")
    d_o = _mm([(df, wo)], "nt", F32, TOKEN_TILE, inner, "dil_do")
    dwo = _mm([(o_mix_b, df)], "tn", BF16, TOKEN_TILE, d_model, "dil_dwo")
    dhns, dws, dbs = [], [], []
    for g, (window, dilation) in enumerate(DIL_GROUPS):
        grads = _dil_attn_bwd(qkvs[g], bias[g], d_o, o_mix, alphas[g], lses[g], dilation, window // dilation, "dil_attn_bwd")
        dbs.append(grads[3])
        dhns.append(_mm([(grads[j], w_in_t) for j in range(3)], "nn", F32, TOKEN_TILE, d_model, "dil_dhn", out_perm=dilation,
                        b_rows=[(3 * g + j) * inner for j in range(3)]))
        dws += list(_mm_tn_shared(list(grads[:3]), hns[g], "dil_dwin"))
    dx, dshift, dscale, dpre = _prenorm_bwd(dx_out, dhns, x, pre_g, scale, "prenorm_bwd3")
    return dx, (dpre, dpost), (dshift, dscale, dgate), (jnp.concatenate(dws, axis=0), dwo), jnp.concatenate(dbs, axis=0)


def _pad_rows(a, rows):
    return jnp.pad(a, ((0, rows - a.shape[0]), (0, 0)))


def _lanes(a):
    flat = a.reshape(-1).astype(F32)
    rows = -(-flat.shape[0] // 1024) * 8
    return jnp.pad(flat, (0, rows * 128 - flat.shape[0])).reshape(rows, 128)


def kernel(x, c, norm_pre, norm_post, w_mod, b_mod, ffn_w_gate, ffn_w_up, ffn_w_down, mla_w_in, mla_q_norm, mla_w_q_up, mla_kv_norm, mla_w_kv_up, mla_w_o, dil_w_in, dil_w_o, rel_bias, loss_target, m_norm_pre, m_norm_post, m_w_mod, m_b_mod, m_ffn_w_gate, m_ffn_w_up, m_ffn_w_down, m_mla_w_in, m_mla_q_norm, m_mla_w_q_up, m_mla_kv_norm, m_mla_w_kv_up, m_mla_w_o, m_dil_w_in, m_dil_w_o, m_rel_bias, v_norm_pre, v_norm_post, v_w_mod, v_b_mod, v_ffn_w_gate, v_ffn_w_up, v_ffn_w_down, v_mla_w_in, v_mla_q_norm, v_mla_w_q_up, v_mla_kv_norm, v_mla_w_kv_up, v_mla_w_o, v_dil_w_in, v_dil_w_o, v_rel_bias):
    me = 4 * lax.axis_index("x") + 2 * lax.axis_index("y") + lax.axis_index("c")
    depth, n_sub, d_loc = norm_pre.shape
    d_model = x.shape[2]
    mod_loc_cols = w_mod.shape[2]
    x0, target = x[0], loss_target[0]

    bf_t = lambda a: a.astype(BF16).T
    ffn_ids = [(i, h) for i in range(depth) for h in range(2)]
    shards = []
    for i, h in ffn_ids:
        shards += [bf_t(ffn_w_gate[i, h]), bf_t(ffn_w_up[i, h]), ffn_w_down[i, h].astype(BF16)]
    shards += [mla_w_in[0].astype(BF16), bf_t(mla_w_q_up[0]), bf_t(mla_w_kv_up[0]), mla_w_o[0].astype(BF16),
               bf_t(dil_w_in[0]), dil_w_o[0].astype(BF16)]
    n_ffn = 3 * len(ffn_ids)
    members = {(0, 0): [0, 1, 2], (0, 1): [n_ffn, n_ffn + 1, n_ffn + 2, n_ffn + 3], (0, 2): [3, 4, 5],
               (1, 0): [6, 7, 8], (1, 1): [n_ffn + 4, n_ffn + 5], (1, 2): [9, 10, 11]}
    order = [(i, s) for i in range(depth) for s in range(n_sub)]

    small = jnp.concatenate([c.reshape(8, 128), _pad_rows(norm_pre.reshape(depth * n_sub, d_loc), 8),
                             _pad_rows(norm_post.reshape(depth * n_sub, d_loc), 8)], axis=0)
    small_all = _exchange([small], True, "gather_small")[0].reshape(NDEV, 24, 128)
    c_all = small_all[:, 0:8].reshape(NDEV, d_model)
    gains = lambda lo: jnp.transpose(small_all[:, lo:lo + depth * n_sub], (1, 0, 2)).reshape(depth, n_sub, 1, d_model)
    pre_full, post_full = gains(8), gains(16)

    b_loc = lax.dynamic_slice(b_mod, (0, me * mod_loc_cols), (depth, mod_loc_cols))
    mod_cols, silu_c = _mod_fwd(c_all, w_mod, b_loc, "mod_fwd")
    mod_all = _exchange([mod_cols.reshape(depth * NDEV, mod_loc_cols)], True, "gather_mod")[0]
    mod_all = mod_all.reshape(NDEV, depth, NDEV, mod_loc_cols)
    mod_mine = lax.dynamic_index_in_dim(mod_all, me, axis=2, keepdims=False)
    mod = jnp.transpose(mod_mine, (1, 0, 2)).reshape(depth, n_sub, 3, 1, d_model)

    shards[0], _ = lax.optimization_barrier((shards[0], mod_all))
    first = order[0]
    stages = [("%d%d" % first, members[first][:2]), ("%d%dd" % first, members[first][2:])]
    stages += [("%d%d" % key, members[key]) for key in order[1:]]
    stage_names = [name for name, _ in stages]
    g_sems, g_srcs, g_lands, g_token = _split_start(shards, [idx for _, idx in stages], True, "gather_weights_start")

    forwarded = {}

    def forward(stage, after):
        idx = stages[stage_names.index(stage)][1]
        forwarded[stage] = _gather_forward(g_sems[stage_names.index(stage)], [g_srcs[k] for k in idx],
                                           [g_lands[k] for k in idx], after, "gather_forward_" + stage)
        return forwarded[stage][3]

    def weights_of(stage, after):
        (send_b, recv_b), srcs, lands, _ = forwarded[stage]
        local = g_sems[stage_names.index(stage)][2]
        return _split_wait((send_b, recv_b, local), srcs, lands, after, OTHER_CHIPS, True, "gather_wait_" + stage)

    def late_down(after):
        forward("%d%dd" % first, after)
        return weights_of("%d%dd" % first, after)[0]

    lat_real = Q_LORA + KV_LORA
    qk = QK_NOPE + QK_ROPE

    def mla_weights(after):
        w_in, wq_t, wkv_t, wo = weights_of("01", after)
        w_in_pad = jnp.concatenate([w_in[:, :lat_real], jnp.zeros((d_model, QK_NOPE), BF16), w_in[:, lat_real:],
                                    jnp.zeros((d_model, HEAD_PAD - QK_NOPE - QK_ROPE), BF16)], axis=1)
        wq_pad = jnp.pad(wq_t.reshape(MLA_HEADS, qk, Q_LORA), ((0, 0), (0, HEAD_PAD - qk), (0, 0)))
        wo_pad = jnp.pad(wo.reshape(MLA_HEADS, V_HEAD, d_model), ((0, 0), (HEAD_PAD - V_HEAD, 0), (0, 0)))
        return (w_in_pad, mla_q_norm, wq_pad.reshape(MLA_HEADS * HEAD_PAD, Q_LORA), mla_kv_norm, wkv_t,
                wo_pad.reshape(MLA_HEADS * HEAD_PAD, d_model))

    zero = g_token[0, 0]
    rope = _rope_tables(zero)
    buckets = jnp.stack([_dil_buckets(dil) for _, dil in DIL_GROUPS]) + zero.astype(jnp.int32)
    onehot = (buckets[..., None] == jnp.arange(N_BUCKETS)).astype(F32)
    bias = jnp.einsum("gqkb,bgh->ghqk", onehot, rel_bias.reshape(N_BUCKETS, len(DIL_GROUPS), DIL_HEADS),
                      precision=lax.Precision.HIGHEST)

    norms = lambda i, s: (pre_full[i, s], post_full[i, s])
    mods = lambda i, s: (mod[i, s, 0], mod[i, s, 1], mod[i, s, 2])
    saved, weights = {}, {}
    h = lax.optimization_barrier((x0, bias, buckets, *rope))[0]
    forward("%d%d" % first, h)
    for n, (i, s) in enumerate(order):
        got = mla_weights(h) if (s == 1 and i % 2 == 0) else tuple(weights_of("%d%d" % (i, s), h))
        mid = None if n + 1 == len(order) else (lambda after, nxt="%d%d" % order[n + 1]: forward(nxt, after))
        if s != 1:
            if len(got) == 3:
                h, saved[i, s], weights[i, s] = _ffn_fwd(h, norms(i, s), mods(i, s), got)
                if mid is not None:
                    mid(h)
            else:
                h, saved[i, s], weights[i, s] = _ffn_fwd(h, norms(i, s), mods(i, s), (*got, late_down), mid)
            continue
        weights[i, s] = got
        if i % 2 == 0:
            h, saved[i, s] = _mla_fwd(h, norms(i, s), mods(i, s), weights[i, s], rope, mid)
        else:
            h, saved[i, s] = _dil_fwd(h, norms(i, s), mods(i, s), weights[i, s], bias, mid)
    dh, loss_parts = _loss_grad(h, target, "loss")

    dnorm, dmod, sent = {}, {}, {}
    token = jnp.zeros((8, 128), F32)
    last = order[0]

    def send_last(j, dw):
        sent[last, j] = _split_start([dw], [[0]], False, "scatter_start_%d%d_%d" % (*last, j))
        return sent[last, j][3]

    for i, s in reversed(order):
        md = mods(i, s)
        md = (md[0], md[1], md[2] + token[:1, :1])
        if (i, s) == last:
            dh, dnorm[i, s], dmod[i, s], _ = _ffn_bwd(dh, saved[i, s], norms(i, s), md, weights[i, s], send_last)
            continue
        if s != 1:
            dh, dnorm[i, s], dmod[i, s], dws = _ffn_bwd(dh, saved[i, s], norms(i, s), md, weights[i, s])
        elif i % 2 == 0:
            dh, dnorm[i, s], dmod[i, s], dmla = _mla_bwd(dh, saved[i, s], norms(i, s), md, weights[i, s], rope)
            dw_in_pad, dq_norm, dwq_pad, dkv_norm, dwkv_t, dwo_pad = dmla
            dw_in = jnp.concatenate([dw_in_pad[:, :lat_real], dw_in_pad[:, lat_real + QK_NOPE:lat_real + qk]], axis=1)
            dwq_t = dwq_pad.reshape(MLA_HEADS, HEAD_PAD, Q_LORA)[:, :qk].reshape(MLA_HEADS * qk, Q_LORA)
            dwo = dwo_pad.reshape(MLA_HEADS, HEAD_PAD, d_model)[:, HEAD_PAD - V_HEAD:].reshape(MLA_HEADS * V_HEAD, d_model)
            dws = (dw_in, dwq_t, dwkv_t, dwo)
        else:
            dh, dnorm[i, s], dmod[i, s], dws, dbias = _dil_bwd(dh, saved[i, s], norms(i, s), md, weights[i, s], bias)
        sent[i, s] = _split_start(list(dws), [list(range(len(dws)))], False, "scatter_start_%d%d" % (i, s))
        token = sent[i, s][3]
    grad_x = dh[None]

    mine = {}
    transposed = {3 * n + j for n in range(len(ffn_ids)) for j in (0, 1)} | {n_ffn + 1, n_ffn + 2, n_ffn + 4}
    for key in reversed(order[1:]):
        sems, srcs, lands, _ = sent[key]
        parts = _split_wait(sems[0], srcs, lands, dh, NDEV - 1, False, "scatter_wait_%d%d" % key)
        for k, p in zip(members[key], parts):
            mine[k] = _sum_parts(p, "sum_parts", k in transposed)
    g_mla_in, g_q_up, g_kv_up, g_mla_o, g_dil_in, g_dil_o = (mine[k] for k in range(n_ffn, n_ffn + 6))
    g_mla_in, g_q_up, g_kv_up, g_mla_o = g_mla_in[None], g_q_up[None], g_kv_up[None], g_mla_o[None]
    g_dil_in, g_dil_o = g_dil_in[None], g_dil_o[None]
    early = {"mla_w_in": _adamw(mla_w_in, g_mla_in, m_mla_w_in, v_mla_w_in, "adamw"),
             "mla_w_q_up": _adamw(mla_w_q_up, g_q_up, m_mla_w_q_up, v_mla_w_q_up, "adamw"),
             "mla_w_kv_up": _adamw(mla_w_kv_up, g_kv_up, m_mla_w_kv_up, v_mla_w_kv_up, "adamw"),
             "mla_w_o": _adamw(mla_w_o, g_mla_o, m_mla_w_o, v_mla_w_o, "adamw"),
             "dil_w_in": _adamw(dil_w_in, g_dil_in, m_dil_w_in, v_dil_w_in, "adamw"),
             "dil_w_o": _adamw(dil_w_o, g_dil_o, m_dil_w_o, v_dil_w_o, "adamw")}
    dbias_sums = _bias_reduce(dbias, buckets, "bias_reduce")
    tied = lax.optimization_barrier((dbias_sums, *[a for step in early.values() for a in step]))
    dbias_sums, early = tied[0], {name: tuple(tied[1 + 3 * n:4 + 3 * n]) for n, name in enumerate(early)}
    for j in (2, 0, 1):
        sems, srcs, lands, _ = sent[last, j]
        parts = _split_wait(sems[0], srcs, lands, dbias_sums, NDEV - 1, False, "scatter_wait_%d%d_%d" % (*last, j))
        mine[members[last][j]] = _sum_parts(parts[0], "sum_parts", members[last][j] in transposed)
    g_gate = jnp.stack([mine[3 * n] for n in range(len(ffn_ids))]).reshape(ffn_w_gate.shape)
    g_up = jnp.stack([mine[3 * n + 1] for n in range(len(ffn_ids))]).reshape(ffn_w_up.shape)
    g_down = jnp.stack([mine[3 * n + 2] for n in range(len(ffn_ids))]).reshape(ffn_w_down.shape)

    dmod_mine = jnp.concatenate([jnp.concatenate(dmod[i, s], axis=0) for i in range(depth) for s in range(n_sub)], axis=0)
    dpre_mine = jnp.concatenate([dnorm[i, s][0] for i in range(depth) for s in range(n_sub)], axis=0)
    dpost_mine = jnp.concatenate([dnorm[i, s][1] for i in range(depth) for s in range(n_sub)], axis=0)
    dbias_tab = dbias_sums[:, 0, :N_BUCKETS].T
    pieces = [dmod_mine, dpre_mine, dpost_mine, dq_norm, dkv_norm, dbias_tab, jnp.sum(loss_parts).reshape(1, 1)]
    packed = [_lanes(p) for p in pieces]
    offs = [0]
    for p in packed:
        offs.append(offs[-1] + p.shape[0])
    everyone = _exchange([jnp.concatenate(packed, axis=0)], True, "gather_small_grads")[0].reshape(NDEV, offs[-1], 128)
    total = _sum_parts(everyone, "sum_small")
    take = lambda n, shape: total[offs[n]:offs[n + 1]].reshape(-1)[:math.prod(shape)].reshape(shape)
    g_b_mod = take(0, b_mod.shape)
    col0 = me * d_loc
    g_norm_pre = lax.dynamic_slice(take(1, (depth, n_sub, d_model)), (0, 0, col0), norm_pre.shape)
    g_norm_post = lax.dynamic_slice(take(2, (depth, n_sub, d_model)), (0, 0, col0), norm_post.shape)
    g_q_norm, g_kv_norm = take(3, mla_q_norm.shape), take(4, mla_kv_norm.shape)
    g_rel_bias = take(5, rel_bias.shape)
    loss = take(6, ())

    dmod_all = everyone[:, offs[0]:offs[1]].reshape(NDEV, depth, NDEV * mod_loc_cols)
    dmod_cols = lax.dynamic_slice(dmod_all, (0, 0, me * mod_loc_cols), (NDEV, depth, mod_loc_cols))
    silu_t = jnp.pad(silu_c.T, ((0, 0), (0, HEAD_PAD - NDEV)))
    g_w_mod = jnp.stack([_mm([(silu_t, jnp.pad(dmod_cols[:, i], ((0, HEAD_PAD - NDEV), (0, 0))))], "nn", F32, TOKEN_TILE,
                             mod_loc_cols, "mod_bwd") for i in range(depth)])

    ws = (norm_pre, norm_post, w_mod, b_mod, ffn_w_gate, ffn_w_up, ffn_w_down, mla_w_in, mla_q_norm, mla_w_q_up, mla_kv_norm,
          mla_w_kv_up, mla_w_o, dil_w_in, dil_w_o, rel_bias)
    gs = (g_norm_pre, g_norm_post, g_w_mod, g_b_mod, g_gate, g_up, g_down, g_mla_in, g_q_norm, g_q_up, g_kv_norm, g_kv_up,
          g_mla_o, g_dil_in, g_dil_o, g_rel_bias)
    ms = (m_norm_pre, m_norm_post, m_w_mod, m_b_mod, m_ffn_w_gate, m_ffn_w_up, m_ffn_w_down, m_mla_w_in, m_mla_q_norm,
          m_mla_w_q_up, m_mla_kv_norm, m_mla_w_kv_up, m_mla_w_o, m_dil_w_in, m_dil_w_o, m_rel_bias)
    vs = (v_norm_pre, v_norm_post, v_w_mod, v_b_mod, v_ffn_w_gate, v_ffn_w_up, v_ffn_w_down, v_mla_w_in, v_mla_q_norm,
          v_mla_w_q_up, v_mla_kv_norm, v_mla_w_kv_up, v_mla_w_o, v_dil_w_in, v_dil_w_o, v_rel_bias)
    names = ("norm_pre", "norm_post", "w_mod", "b_mod", "ffn_w_gate", "ffn_w_up", "ffn_w_down", "mla_w_in", "mla_q_norm",
             "mla_w_q_up", "mla_kv_norm", "mla_w_kv_up", "mla_w_o", "dil_w_in", "dil_w_o", "rel_bias")
    stepped = [early[n] if n in early else _adamw(w, g, m, v, "adamw") for n, w, g, m, v in zip(names, ws, gs, ms, vs)]
    deltas, new_m, new_v = zip(*stepped)
    return (loss, grad_x, *gs, *deltas, *new_m, *new_v)
```

```python
import math

import jax
import jax.numpy as jnp
from jax import lax
from jax.experimental import pallas as pl
from jax.experimental.pallas import tpu as pltpu

F32 = jnp.float32
BF16 = jnp.bfloat16
MESH = pl.DeviceIdType.MESH

NDEV = 8
OTHER_CHIPS = 3
D_MODEL = 1024
SEQ = 2048
D_FF = 2816
EPS = 1e-6
FFN_RES = 0.5

MLA_HEADS = 16
Q_LORA = 384
KV_LORA = 256
QK_NOPE = 64
QK_ROPE = 32
V_HEAD = 64
ROPE_THETA = 10000.0
HEAD_PAD = 128
LAT_PAD = Q_LORA + KV_LORA + HEAD_PAD
MLA_SCALE = (QK_NOPE + QK_ROPE) ** -0.5

DIL_GROUPS = ((128, 1), (512, 4), (2048, 16))
DIL_HEADS = 16
DIL_HEAD_DIM = 64
DIL_BLOCK = 128
DIL_PAIRS = DIL_HEADS // 2
DIL_SCALE = DIL_HEAD_DIM ** -0.5
DIL_GROUPED = 4
N_BUCKETS = 32
MAX_DISTANCE = 2048

ADAM_LR = 0.001
ADAM_B1 = 0.9
ADAM_B2 = 0.999
ADAM_EPS = 1e-08
ADAM_WD = 0.01
ADAM_STEP = 10

V7X_VMEM_BYTES = 64 * 2**20
VMEM_RESERVE = 10 * 2**20
TOKEN_TILE = 512


def _nbytes(shape, dtype):
    return math.prod(shape) * jnp.dtype(dtype).itemsize


def _params(semantics, blocks, extra=0):
    need = 2 * sum(_nbytes(s, d) for s, d in blocks) + extra + VMEM_RESERVE
    return pltpu.CompilerParams(dimension_semantics=semantics,
                                vmem_limit_bytes=int(min(need, V7X_VMEM_BYTES - VMEM_RESERVE)))


def _pcall(body, out_shape, **kw):
    call = pl.pallas_call(body, out_shape=jax.tree.map(lambda s: pltpu.HBM(s.shape, s.dtype), out_shape), **kw)
    return lambda *args: call(*[pltpu.with_memory_space_constraint(a, pltpu.HBM) for a in args])


def _dot_nn(a, b):
    return lax.dot_general(a, b, (((1,), (0,)), ((), ())), preferred_element_type=F32)


def _dot_nt(a, b):
    return lax.dot_general(a, b, (((1,), (1,)), ((), ())), preferred_element_type=F32)


def _dot_tn(a, b):
    return lax.dot_general(a, b, (((0,), (0,)), ((), ())), preferred_element_type=F32)


_DOTS = {"nn": _dot_nn, "nt": _dot_nt, "tn": _dot_tn}


def _rstd(v):
    return lax.rsqrt(jnp.mean(v * v, axis=-1, keepdims=True) + EPS)


def _rms_bwd(v, r, t):
    return r * t - v * (r * r * r) * jnp.mean(t * v, axis=-1, keepdims=True)


_TOKEN_SPEC = pl.BlockSpec((8, 128), lambda *_: (0, 0))


def _mm(pairs, mode, out_dtype, tm, tn, name, out_perm=1, after=None, b_rows=None):
    a0, b0 = pairs[0]
    m_dim = a0.shape[1] if mode == "tn" else a0.shape[0]
    n_dim = b0.shape[0] if mode == "nt" else b0.shape[1]
    tm, tn = min(tm, m_dim // out_perm), min(tn, n_dim)
    assert m_dim % tm == 0 and n_dim % tn == 0, (name, m_dim, n_dim, tm, tn)
    dot = _DOTS[mode]
    npairs = len(pairs)

    def body(*refs):
        acc = None
        for p in range(npairs):
            d = dot(refs[2 * p][...].astype(BF16), refs[2 * p + 1][...].astype(BF16))
            acc = d if acc is None else acc + d
        refs[-1][...] = acc.astype(out_dtype)

    in_specs, blocks, flat = [], [], []
    for n_pair, (a, b) in enumerate(pairs):
        if mode == "nn":
            k = a.shape[1]
            first_block = 0 if b_rows is None else b_rows[n_pair] // k
            sa, sb = ((tm, k), lambda i, j: (i, 0)), ((k, tn), lambda i, j, o=first_block: (o, j))
        elif mode == "nt":
            k = a.shape[1]
            sa, sb = ((tm, k), lambda i, j: (i, 0)), ((tn, k), lambda i, j: (j, 0))
        else:
            k = a.shape[0]
            sa, sb = ((k, tm), lambda i, j: (0, i)), ((k, tn), lambda i, j: (0, j))
        in_specs += [pl.BlockSpec(*sa), pl.BlockSpec(*sb)]
        blocks += [(sa[0], a.dtype), (sb[0], b.dtype)]
        flat += [a, b]
    if after is not None:
        in_specs.append(_TOKEN_SPEC)
        flat.append(after)
    if out_perm == 1:
        out_shape = (m_dim, n_dim)
        out_spec = pl.BlockSpec((tm, tn), lambda i, j: (i, j))
    else:
        rows = m_dim // out_perm
        assert tn == n_dim and rows % tm == 0, (name, rows, tm)
        nb = rows // tm
        out_shape = (rows, out_perm * n_dim)
        out_spec = pl.BlockSpec((tm, n_dim), lambda i, j: (i % nb, i // nb))
    blocks.append(((tm, tn), out_dtype))
    res = _pcall(
        body, out_shape=jax.ShapeDtypeStruct(out_shape, out_dtype), grid=(m_dim // tm, n_dim // tn),
        in_specs=in_specs, out_specs=out_spec, name=name,
        compiler_params=_params(("parallel", "parallel"), blocks, extra=2 * tm * tn * 4),
    )(*flat)
    return res.reshape(m_dim, n_dim)


def _prenorm_mm(x, pre_g, scale, shift, w, w_mode, out_dtype, tn, name, perm=1, w_rows=None):
    s_dim, d_dim = x.shape
    n_dim = w.shape[0] if w_mode == "nt" else w.shape[1]
    w_first = 0
    if w_rows is not None:
        w_first, n_dim = w_rows
    rows = s_dim // perm
    side = max(1, TOKEN_TILE // rows)
    tm = side * min(TOKEN_TILE, rows)
    nb = max(1, rows // tm)
    tn = min(tn, n_dim)
    assert n_dim % tn == 0 and w_first % tn == 0
    w_block0 = w_first // tn
    dot = _DOTS[w_mode]

    def body(x_ref, g_ref, sc_ref, sh_ref, w_ref, hn_ref, o_ref):
        @pl.when(pl.program_id(1) == 0)
        def _():
            xf = x_ref[...]
            if side > 1:
                xf = jnp.concatenate([xf[:, c * d_dim:(c + 1) * d_dim] for c in range(side)], axis=0)
            hn = (xf * _rstd(xf) * g_ref[...]) * (1.0 + sc_ref[...]) + sh_ref[...]
            hn_ref[...] = hn.astype(BF16)

        o_ref[...] = dot(hn_ref[...], w_ref[...]).astype(out_dtype)

    vec = pl.BlockSpec((1, d_dim), lambda i, j: (0, 0))
    w_block = (tn, d_dim) if w_mode == "nt" else (d_dim, tn)
    w_spec = pl.BlockSpec(w_block, (lambda i, j: (w_block0 + j, 0)) if w_mode == "nt" else (lambda i, j: (0, j)))
    hn, out = _pcall(
        body,
        out_shape=(jax.ShapeDtypeStruct((s_dim, d_dim), BF16), jax.ShapeDtypeStruct((s_dim, n_dim), out_dtype)),
        grid=(s_dim // tm, n_dim // tn),
        in_specs=[pl.BlockSpec((tm // side, side * d_dim), lambda i, j: (i % nb, i // nb)), vec, vec, vec, w_spec],
        out_specs=(pl.BlockSpec((tm, d_dim), lambda i, j: (i, 0)), pl.BlockSpec((tm, tn), lambda i, j: (i, j))),
        name=name,
        compiler_params=_params(("parallel", "arbitrary"),
                                [((tm, d_dim), F32), (w_block, BF16), ((tm, d_dim), BF16), ((tm, tn), out_dtype)],
                                extra=3 * tm * d_dim * 4 + tm * tn * 4),
    )(x.reshape(rows, perm * d_dim), pre_g, scale, shift, w)
    return hn, out


def _ffn_up(x, pre_g, scale, shift, wg_t, wu_t, name):
    s_dim, d_dim = x.shape
    f_dim = wg_t.shape[0]
    tm, tn = TOKEN_TILE, f_dim // 2

    def body(x_ref, g_ref, sc_ref, sh_ref, wg_ref, wu_ref, hn_ref, go_ref, uo_ref, a_ref):
        @pl.when(pl.program_id(1) == 0)
        def _():
            xf = x_ref[...]
            hn = (xf * _rstd(xf) * g_ref[...]) * (1.0 + sc_ref[...]) + sh_ref[...]
            hn_ref[...] = hn.astype(BF16)

        hn = hn_ref[...]
        g = _dot_nt(hn, wg_ref[...])
        u = _dot_nt(hn, wu_ref[...])
        go_ref[...] = g.astype(BF16)
        uo_ref[...] = u.astype(BF16)
        a_ref[...] = (g * jax.nn.sigmoid(g) * u).astype(BF16)

    vec = pl.BlockSpec((1, d_dim), lambda i, j: (0, 0))
    w_spec = pl.BlockSpec((tn, d_dim), lambda i, j: (j, 0))
    act = pl.BlockSpec((tm, tn), lambda i, j: (i, j))
    act_shape = jax.ShapeDtypeStruct((s_dim, f_dim), BF16)
    return _pcall(
        body,
        out_shape=(jax.ShapeDtypeStruct((s_dim, d_dim), BF16), act_shape, act_shape, act_shape),
        grid=(s_dim // tm, f_dim // tn),
        in_specs=[pl.BlockSpec((tm, d_dim), lambda i, j: (i, 0)), vec, vec, vec, w_spec, w_spec],
        out_specs=(pl.BlockSpec((tm, d_dim), lambda i, j: (i, 0)), act, act, act),
        name=name,
        compiler_params=_params(("parallel", "arbitrary"),
                                [((tm, d_dim), F32), ((tn, d_dim), BF16), ((tn, d_dim), BF16), ((tm, d_dim), BF16)]
                                + 3 * [((tm, tn), BF16)], extra=3 * tm * d_dim * 4 + 4 * tm * tn * 4),
    )(x, pre_g, scale, shift, wg_t, wu_t)


def _mm_post(a, w, x, post_g, gate, res_w, name):
    s_dim, k_dim = a.shape
    d_dim = w.shape[1]
    tm = TOKEN_TILE

    def body(a_ref, w_ref, x_ref, pg_ref, gt_ref, xo_ref, f_ref):
        f = _dot_nn(a_ref[...], w_ref[...])
        y = f * _rstd(f) * pg_ref[...]
        f_ref[...] = f
        xo_ref[...] = x_ref[...] + (res_w * gt_ref[...]) * y

    vec = pl.BlockSpec((1, d_dim), lambda i: (0, 0))
    row = pl.BlockSpec((tm, d_dim), lambda i: (i, 0))
    out = jax.ShapeDtypeStruct((s_dim, d_dim), F32)
    return _pcall(
        body, out_shape=(out, out), grid=(s_dim // tm,),
        in_specs=[pl.BlockSpec((tm, k_dim), lambda i: (i, 0)), pl.BlockSpec((k_dim, d_dim), lambda i: (0, 0)), row, vec, vec],
        out_specs=(row, row), name=name,
        compiler_params=_params(("parallel",), [((tm, k_dim), BF16), ((k_dim, d_dim), BF16)] + 3 * [((tm, d_dim), F32)],
                                extra=3 * tm * d_dim * 4),
    )(a, w, x, post_g, gate)


def _post_bwd(dx_out, f, post_g, gate, res_w, name):
    s_dim, d_dim = f.shape
    tm = TOKEN_TILE

    def body(dx_ref, f_ref, pg_ref, gt_ref, df_ref, dgate_ref, dpost_ref):
        @pl.when(pl.program_id(0) == 0)
        def _():
            dgate_ref[...] = jnp.zeros_like(dgate_ref)
            dpost_ref[...] = jnp.zeros_like(dpost_ref)

        dx, fv = dx_ref[...], f_ref[...]
        r = _rstd(fv)
        fr = fv * r
        dgate_ref[...] += res_w * jnp.sum(dx * (fr * pg_ref[...]), axis=0, keepdims=True)
        dy = (res_w * gt_ref[...]) * dx
        dpost_ref[...] += jnp.sum(dy * fr, axis=0, keepdims=True)
        df_ref[...] = _rms_bwd(fv, r, dy * pg_ref[...]).astype(BF16)

    vec = pl.BlockSpec((1, d_dim), lambda i: (0, 0))
    row = pl.BlockSpec((tm, d_dim), lambda i: (i, 0))
    vshape = jax.ShapeDtypeStruct((1, d_dim), F32)
    return _pcall(
        body, out_shape=(jax.ShapeDtypeStruct((s_dim, d_dim), BF16), vshape, vshape), grid=(s_dim // tm,),
        in_specs=[row, row, vec, vec], out_specs=(row, vec, vec), name=name,
        compiler_params=_params(("arbitrary",), 3 * [((tm, d_dim), F32)], extra=6 * tm * d_dim * 4),
    )(dx_out, f, post_g, gate)


def _prenorm_bwd(dx_out, dhns, x, pre_g, scale, name):
    s_dim, d_dim = x.shape
    tm = TOKEN_TILE
    n_in = len(dhns)

    def body(*refs):
        dx_ref, x_ref, pg_ref, sc_ref = refs[n_in + 0], refs[n_in + 1], refs[n_in + 2], refs[n_in + 3]
        dxo_ref, dsh_ref, dsc_ref, dpg_ref = refs[n_in + 4:]

        @pl.when(pl.program_id(0) == 0)
        def _():
            dsh_ref[...] = jnp.zeros_like(dsh_ref)
            dsc_ref[...] = jnp.zeros_like(dsc_ref)
            dpg_ref[...] = jnp.zeros_like(dpg_ref)

        dhn = refs[0][...]
        for k in range(1, n_in):
            dhn = dhn + refs[k][...]
        xv = x_ref[...]
        r = _rstd(xv)
        xr = xv * r
        dsh_ref[...] += jnp.sum(dhn, axis=0, keepdims=True)
        dsc_ref[...] += jnp.sum(dhn * (xr * pg_ref[...]), axis=0, keepdims=True)
        dn = dhn * (1.0 + sc_ref[...])
        dpg_ref[...] += jnp.sum(dn * xr, axis=0, keepdims=True)
        dxo_ref[...] = dx_ref[...] + _rms_bwd(xv, r, dn * pg_ref[...])

    vec = pl.BlockSpec((1, d_dim), lambda i: (0, 0))
    row = pl.BlockSpec((tm, d_dim), lambda i: (i, 0))
    vshape = jax.ShapeDtypeStruct((1, d_dim), F32)
    return _pcall(
        body, out_shape=(jax.ShapeDtypeStruct((s_dim, d_dim), F32), vshape, vshape, vshape), grid=(s_dim // tm,),
        in_specs=n_in * [row] + [row, row, vec, vec], out_specs=(row, vec, vec, vec), name=name,
        compiler_params=_params(("arbitrary",), (n_in + 3) * [((tm, d_dim), F32)], extra=6 * tm * d_dim * 4),
    )(*dhns, dx_out, x, pre_g, scale)


def _ffn_dgu(df, wd, g, u, name, after=None):
    s_dim, d_dim = df.shape
    f_dim = wd.shape[0]
    tm, tn = TOKEN_TILE, f_dim // 2

    def body(df_ref, wd_ref, g_ref, u_ref, *rest):
        dg_ref, du_ref = rest[-2:]
        da = _dot_nt(df_ref[...], wd_ref[...])
        gv, uv = g_ref[...].astype(F32), u_ref[...].astype(F32)
        sg = jax.nn.sigmoid(gv)
        du_ref[...] = (da * (gv * sg)).astype(BF16)
        dg_ref[...] = (da * uv * (sg * (1.0 + gv * (1.0 - sg)))).astype(BF16)

    act = pl.BlockSpec((tm, tn), lambda i, j: (i, j))
    act_shape = jax.ShapeDtypeStruct((s_dim, f_dim), BF16)
    token = [] if after is None else [after]
    return _pcall(
        body, out_shape=(act_shape, act_shape), grid=(s_dim // tm, f_dim // tn),
        in_specs=[pl.BlockSpec((tm, d_dim), lambda i, j: (i, 0)), pl.BlockSpec((tn, d_dim), lambda i, j: (j, 0)), act, act]
        + len(token) * [_TOKEN_SPEC],
        out_specs=(act, act), name=name,
        compiler_params=_params(("parallel", "parallel"), [((tm, d_dim), BF16), ((tn, d_dim), BF16)] + 4 * [((tm, tn), BF16)],
                                extra=6 * tm * tn * 4),
    )(df, wd, g, u, *token)


def _ffn_dw(dg, du, a, hn, df, name):
    s_dim, f_dim = dg.shape
    d_dim = hn.shape[1]
    tm = 256

    def body(dg_ref, du_ref, a_ref, hn_ref, df_ref, dwg_ref, dwu_ref, dwd_ref):
        dwg_ref[...] = _dot_tn(dg_ref[...], hn_ref[...]).astype(BF16)
        dwu_ref[...] = _dot_tn(du_ref[...], hn_ref[...]).astype(BF16)
        dwd_ref[...] = _dot_tn(a_ref[...], df_ref[...]).astype(BF16)

    col = pl.BlockSpec((s_dim, tm), lambda i: (0, i))
    full = pl.BlockSpec((s_dim, d_dim), lambda i: (0, 0), pipeline_mode=pl.Buffered(1))
    out = pl.BlockSpec((tm, d_dim), lambda i: (i, 0))
    shape = jax.ShapeDtypeStruct((f_dim, d_dim), BF16)
    need = 2 * s_dim * d_dim * 2 + 2 * 3 * (s_dim * tm * 2 + tm * d_dim * 2) + 3 * tm * d_dim * 4 + 3 * s_dim * tm * 2
    return _pcall(
        body, out_shape=(shape, shape, shape), grid=(f_dim // tm,), in_specs=[col, col, col, full, full],
        out_specs=(out, out, out), name=name,
        compiler_params=pltpu.CompilerParams(dimension_semantics=("parallel",),
                                             vmem_limit_bytes=int(min(need + VMEM_RESERVE, V7X_VMEM_BYTES - VMEM_RESERVE))),
    )(dg, du, a, hn, df)


def _mm_tn_shared(lhs, b, name):
    k_dim, m_dim = lhs[0].shape
    n_dim = b.shape[1]
    tm = 256
    n = len(lhs)

    def body(*refs):
        rhs = refs[n][...]
        for j in range(n):
            refs[n + 1 + j][...] = _dot_tn(refs[j][...], rhs).astype(BF16)

    col = pl.BlockSpec((k_dim, tm), lambda i: (0, i))
    out = pl.BlockSpec((tm, n_dim), lambda i: (i, 0))
    shape = jax.ShapeDtypeStruct((m_dim, n_dim), BF16)
    need = k_dim * n_dim * 2 + 2 * n * (k_dim * tm * 2 + tm * n_dim * 2) + n * tm * n_dim * 4 + n * k_dim * tm * 2
    return _pcall(
        body, out_shape=tuple(n * [shape]), grid=(m_dim // tm,),
        in_specs=n * [col] + [pl.BlockSpec((k_dim, n_dim), lambda i: (0, 0), pipeline_mode=pl.Buffered(1))],
        out_specs=tuple(n * [out]), name=name,
        compiler_params=pltpu.CompilerParams(dimension_semantics=("parallel",),
                                             vmem_limit_bytes=int(min(need + VMEM_RESERVE, V7X_VMEM_BYTES - VMEM_RESERVE))),
    )(*lhs, b)


def _ffn_fwd_fused(x, pre_g, scale, shift, post_g, gate, wg_t, wu_t, wd, name):
    s_dim, d_dim = x.shape
    f_dim = wd.shape[0]
    tm, chunks = 256, 2
    cw = f_dim // chunks

    def body(x_ref, prg_ref, sc_ref, sh_ref, pg_ref, gt_ref, wg_ref, wu_ref, wd_ref, hn_ref, go_ref, uo_ref, a_ref, xo_ref, f_ref):
        xf = x_ref[...]
        hn = ((xf * _rstd(xf) * prg_ref[...]) * (1.0 + sc_ref[...]) + sh_ref[...]).astype(BF16)
        hn_ref[...] = hn
        f = None
        ahead = (_dot_nt(hn, wg_ref[0:cw, :]), _dot_nt(hn, wu_ref[0:cw, :]))
        for c in range(chunks):
            g, u = ahead
            if c + 1 < chunks:
                nxt = slice((c + 1) * cw, (c + 2) * cw)
                ahead = (_dot_nt(hn, wg_ref[nxt, :]), _dot_nt(hn, wu_ref[nxt, :]))
            cols = slice(c * cw, (c + 1) * cw)
            go_ref[:, cols] = g.astype(BF16)
            uo_ref[:, cols] = u.astype(BF16)
            a = (g * jax.nn.sigmoid(g) * u).astype(BF16)
            a_ref[:, cols] = a
            part = _dot_nn(a, wd_ref[cols, :])
            f = part if f is None else f + part
        f_ref[...] = f
        xo_ref[...] = xf + (FFN_RES * gt_ref[...]) * (f * _rstd(f) * pg_ref[...])

    vec = pl.BlockSpec((1, d_dim), lambda i: (0, 0))
    row = pl.BlockSpec((tm, d_dim), lambda i: (i, 0))
    act = pl.BlockSpec((tm, f_dim), lambda i: (i, 0))
    weight = pl.BlockSpec((f_dim, d_dim), lambda i: (0, 0), pipeline_mode=pl.Buffered(1))
    act_shape = jax.ShapeDtypeStruct((s_dim, f_dim), BF16)
    res_shape = jax.ShapeDtypeStruct((s_dim, d_dim), F32)
    need = (3 * f_dim * d_dim * 2 + 2 * tm * d_dim * 4 + 2 * (tm * d_dim * 2 + 3 * tm * f_dim * 2 + 2 * tm * d_dim * 4)
            + 8 * tm * cw * 4 + 4 * tm * d_dim * 4)
    return _pcall(
        body, out_shape=(jax.ShapeDtypeStruct((s_dim, d_dim), BF16), act_shape, act_shape, act_shape, res_shape, res_shape),
        grid=(s_dim // tm,), in_specs=[row, vec, vec, vec, vec, vec, weight, weight, weight],
        out_specs=(row, act, act, act, row, row), name=name,
        compiler_params=pltpu.CompilerParams(dimension_semantics=("parallel",),
                                             vmem_limit_bytes=int(min(need + VMEM_RESERVE, V7X_VMEM_BYTES - VMEM_RESERVE))),
    )(x, pre_g, scale, shift, post_g, gate, wg_t, wu_t, wd)


def _ffn_bwd_fused(dx_out, saved, pre_g, post_g, scale, gate, wg_t, wu_t, wd, name):
    x, _, g, u, _, f = saved
    s_dim, d_dim = x.shape
    f_dim = wd.shape[0]
    tm, chunks = 256, 2
    cw = f_dim // chunks

    def body(dx_ref, f_ref, g_ref, u_ref, x_ref, pg_ref, gt_ref, prg_ref, sc_ref, wd_ref, wg_ref, wu_ref,
             df_ref, dg_ref, du_ref, dxo_ref, dgate_ref, dpost_ref, dsh_ref, dsc_ref, dpg_ref):
        @pl.when(pl.program_id(0) == 0)
        def _():
            for acc in (dgate_ref, dpost_ref, dsh_ref, dsc_ref, dpg_ref):
                acc[...] = jnp.zeros_like(acc)

        dx, fv = dx_ref[...], f_ref[...]
        r = _rstd(fv)
        fr = fv * r
        dgate_ref[...] += FFN_RES * jnp.sum(dx * (fr * pg_ref[...]), axis=0, keepdims=True)
        dy = (FFN_RES * gt_ref[...]) * dx
        dpost_ref[...] += jnp.sum(dy * fr, axis=0, keepdims=True)
        df = _rms_bwd(fv, r, dy * pg_ref[...]).astype(BF16)
        df_ref[...] = df
        dhn = None
        ahead = _dot_nt(df, wd_ref[0:cw, :])
        for c in range(chunks):
            da = ahead
            if c + 1 < chunks:
                ahead = _dot_nt(df, wd_ref[(c + 1) * cw:(c + 2) * cw, :])
            cols = slice(c * cw, (c + 1) * cw)
            gv, uv = g_ref[:, cols].astype(F32), u_ref[:, cols].astype(F32)
            sg = jax.nn.sigmoid(gv)
            du = (da * (gv * sg)).astype(BF16)
            dg = (da * uv * (sg * (1.0 + gv * (1.0 - sg)))).astype(BF16)
            dg_ref[:, cols] = dg
            du_ref[:, cols] = du
            part = _dot_nn(dg, wg_ref[cols, :]) + _dot_nn(du, wu_ref[cols, :])
            dhn = part if dhn is None else dhn + part
        xv = x_ref[...]
        rx = _rstd(xv)
        xr = xv * rx
        dsh_ref[...] += jnp.sum(dhn, axis=0, keepdims=True)
        dsc_ref[...] += jnp.sum(dhn * (xr * prg_ref[...]), axis=0, keepdims=True)
        dn = dhn * (1.0 + sc_ref[...])
        dpg_ref[...] += jnp.sum(dn * xr, axis=0, keepdims=True)
        dxo_ref[...] = dx + _rms_bwd(xv, rx, dn * prg_ref[...])

    vec = pl.BlockSpec((1, d_dim), lambda i: (0, 0))
    row = pl.BlockSpec((tm, d_dim), lambda i: (i, 0))
    act = pl.BlockSpec((tm, f_dim), lambda i: (i, 0))
    weight = pl.BlockSpec((f_dim, d_dim), lambda i: (0, 0), pipeline_mode=pl.Buffered(1))
    vshape = jax.ShapeDtypeStruct((1, d_dim), F32)
    act_shape = jax.ShapeDtypeStruct((s_dim, f_dim), BF16)
    need = (3 * f_dim * d_dim * 2 + 2 * (3 * tm * d_dim * 4 + 2 * tm * f_dim * 2) + 2 * (tm * d_dim * 2 + 2 * tm * f_dim * 2 + tm * d_dim * 4)
            + 6 * tm * cw * 4 + 6 * tm * d_dim * 4)
    return _pcall(
        body, out_shape=(jax.ShapeDtypeStruct((s_dim, d_dim), BF16), act_shape, act_shape, jax.ShapeDtypeStruct((s_dim, d_dim), F32),
                         vshape, vshape, vshape, vshape, vshape),
        grid=(s_dim // tm,), in_specs=[row, row, act, act, row, vec, vec, vec, vec, weight, weight, weight],
        out_specs=(row, act, act, row, vec, vec, vec, vec, vec), name=name,
        compiler_params=pltpu.CompilerParams(dimension_semantics=("arbitrary",),
                                             vmem_limit_bytes=int(min(need + VMEM_RESERVE, V7X_VMEM_BYTES - VMEM_RESERVE))),
    )(dx_out, f, g, u, x, post_g, gate, pre_g, scale, wd, wg_t, wu_t)


def _rope_tables(zero=0.0):
    half = QK_ROPE // 2
    freqs = ROPE_THETA ** (-jnp.arange(half, dtype=F32) / half)
    ang = (jnp.arange(SEQ, dtype=F32)[:, None] + zero) * freqs[None, :]
    cos, sin = jnp.cos(ang), jnp.sin(ang)
    ones = jnp.ones((SEQ, QK_NOPE), F32)
    zeros = jnp.zeros((SEQ, QK_NOPE), F32)
    pad1 = jnp.ones((SEQ, HEAD_PAD - QK_NOPE - QK_ROPE), F32)
    pad0 = jnp.zeros((SEQ, HEAD_PAD - QK_NOPE - QK_ROPE), F32)
    zh = jnp.zeros((SEQ, half), F32)
    c = jnp.concatenate([ones, cos, cos, pad1], axis=1)
    s1 = jnp.concatenate([zeros, -sin, zh, pad0], axis=1)
    s2 = jnp.concatenate([zeros, zh, sin, pad0], axis=1)
    return c, s1, s2


def _rope(v, c, s1, s2):
    half = QK_ROPE // 2
    return v * c + pltpu.roll(v, HEAD_PAD - half, 1) * s1 + pltpu.roll(v, half, 1) * s2


def _rope_t(dv, c, s1, s2):
    half = QK_ROPE // 2
    return dv * c + pltpu.roll(dv * s1, half, 1) + pltpu.roll(dv * s2, HEAD_PAD - half, 1)


def _mla_qkv(lat, q_norm, kv_norm, wq_t, wkv_t, rope, name):
    s_dim = lat.shape[0]
    width = MLA_HEADS * HEAD_PAD
    tm = 256

    def body(lat_ref, qg_ref, kg_ref, wq_ref, wkv_ref, c_ref, s1_ref, s2_ref, q_ref, k_ref, v_ref, qn_ref, kvn_ref):
        cq = lat_ref[:, :Q_LORA]
        ckv = lat_ref[:, Q_LORA:Q_LORA + KV_LORA]
        kr = lat_ref[:, Q_LORA + KV_LORA:]
        c, s1, s2 = c_ref[...], s1_ref[...], s2_ref[...]
        qn = (cq * _rstd(cq) * qg_ref[...]).astype(BF16)
        kvn = (ckv * _rstd(ckv) * kg_ref[...]).astype(BF16)
        qn_ref[...] = qn
        kvn_ref[...] = kvn
        q = _dot_nt(qn, wq_ref[...])
        kv = _dot_nt(kvn, wkv_ref[...])
        krr = _rope(kr, c, s1, s2)
        low = lax.broadcasted_iota(jnp.int32, (tm, HEAD_PAD), 1) < QK_NOPE
        for h in range(MLA_HEADS):
            sl = slice(h * HEAD_PAD, (h + 1) * HEAD_PAD)
            q_ref[:, sl] = _rope(q[:, sl], c, s1, s2).astype(BF16)
            kvh = kv[:, sl]
            k_ref[:, sl] = (jnp.where(low, kvh, 0.0) + krr).astype(BF16)
            v_ref[:, sl] = jnp.where(low, 0.0, kvh).astype(BF16)

    row = lambda n: pl.BlockSpec((tm, n), lambda i: (i, 0))
    full = lambda a: pl.BlockSpec(a.shape, lambda i: (0, 0))
    wide = jax.ShapeDtypeStruct((s_dim, width), BF16)
    return _pcall(
        body,
        out_shape=(wide, wide, wide, jax.ShapeDtypeStruct((s_dim, Q_LORA), BF16), jax.ShapeDtypeStruct((s_dim, KV_LORA), BF16)),
        grid=(s_dim // tm,),
        in_specs=[row(LAT_PAD), full(q_norm), full(kv_norm), full(wq_t), full(wkv_t), row(HEAD_PAD), row(HEAD_PAD), row(HEAD_PAD)],
        out_specs=(row(width), row(width), row(width), row(Q_LORA), row(KV_LORA)), name=name,
        compiler_params=_params(("parallel",), [((tm, LAT_PAD), F32), (wq_t.shape, BF16), (wkv_t.shape, BF16)]
                                + 3 * [((tm, width), BF16)], extra=4 * tm * width * 4),
    )(lat, q_norm, kv_norm, wq_t, wkv_t, *rope)


def _mla_scores(q, k_ref, t, tq):
    lo = t * tq
    own = slice(lo, lo + tq)
    scores = [(_dot_nt(q, k_ref[own, :]), own)]
    if t > 0:
        scores.append((_dot_nt(q, k_ref[0:lo, :]), slice(0, lo)))
    return scores


def _mla_softmax(scores):
    s_own = scores[0][0] * MLA_SCALE
    rows = lax.broadcasted_iota(jnp.int32, s_own.shape, 0)
    cols = lax.broadcasted_iota(jnp.int32, s_own.shape, 1)
    s_own = jnp.where(cols <= rows, s_own, -jnp.inf)
    mx = jnp.max(s_own, axis=-1, keepdims=True)
    if len(scores) == 1:
        e_own = jnp.exp(s_own - mx)
        return [(e_own * (1.0 / jnp.sum(e_own, axis=-1, keepdims=True)), scores[0][1])]
    s_pre = scores[1][0] * MLA_SCALE
    mx = jnp.maximum(mx, jnp.max(s_pre, axis=-1, keepdims=True))
    e_own, e_pre = jnp.exp(s_own - mx), jnp.exp(s_pre - mx)
    inv = 1.0 / (jnp.sum(e_own, axis=-1, keepdims=True) + jnp.sum(e_pre, axis=-1, keepdims=True))
    return [(e_pre * inv, scores[1][1]), (e_own * inv, scores[0][1])]


def _mla_attn_fwd(q, k, v, name):
    s_dim = q.shape[0]
    tq = 512

    def body(q_ref, k_ref, v_ref, o_ref):
        n_tiles = s_dim // tq
        tile_of = lambda t: slice(t * tq, (t + 1) * tq)
        def weighted_values(t, probs):
            o = None
            for p, keys in probs:
                part = _dot_nn(p, v_ref[keys, :])
                o = part if o is None else o + part
            o_ref[tile_of(t), :] = o.astype(BF16)

        scores = _mla_scores(q_ref[tile_of(0), :], k_ref, 0, tq)
        probs = None
        for t in range(n_tiles):
            ahead = _mla_scores(q_ref[tile_of(t + 1), :], k_ref, t + 1, tq) if t + 1 < n_tiles else None
            if probs is not None:
                weighted_values(t - 1, probs)
            probs = [(p.astype(BF16), keys) for p, keys in _mla_softmax(scores)]
            scores = ahead
        weighted_values(n_tiles - 1, probs)

    head = pl.BlockSpec((s_dim, HEAD_PAD), lambda h: (0, h))
    return _pcall(
        body, out_shape=jax.ShapeDtypeStruct(q.shape, BF16), grid=(MLA_HEADS,),
        in_specs=[head, head, head], out_specs=head, name=name,
        compiler_params=_params(("parallel",), 4 * [((s_dim, HEAD_PAD), BF16)], extra=4 * tq * s_dim * 4),
    )(q, k, v)


def _mla_attn_bwd(q, k, v, d_o, name):
    s_dim = q.shape[0]
    tq = 512

    def body(q_ref, k_ref, v_ref, do_ref, dq_ref, dk_ref, dv_ref):
        dk_ref[...] = jnp.zeros_like(dk_ref)
        dv_ref[...] = jnp.zeros_like(dv_ref)
        n_tiles = s_dim // tq
        tile_of = lambda t: slice(t * tq, (t + 1) * tq)

        def products(t):
            scores = _mla_scores(q_ref[tile_of(t), :], k_ref, t, tq)
            dot = do_ref[tile_of(t), :].astype(BF16)
            return scores, [_dot_nt(dot, v_ref[keys, :]) for _, keys in scores]

        def gradients_of_scores(scores, dps):
            probs = _mla_softmax(scores)
            dp_of = {(keys.start, keys.stop): dp for (_, keys), dp in zip(scores, dps)}
            terms = [(p, keys, dp_of[keys.start, keys.stop]) for p, keys in probs]
            row = None
            for p, _, dp in terms:
                part = jnp.sum(p * dp, axis=-1, keepdims=True)
                row = part if row is None else row + part
            return [((p * (dp - row) * MLA_SCALE).astype(BF16), p.astype(BF16), keys) for p, keys, dp in terms]

        def accumulate(t, terms):
            qt = q_ref[tile_of(t), :]
            dot = do_ref[tile_of(t), :].astype(BF16)
            dq = None
            for dsb, pb, keys in terms:
                part = _dot_nn(dsb, k_ref[keys, :])
                dq = part if dq is None else dq + part
                dk_ref[keys, :] += _dot_tn(dsb, qt)
                dv_ref[keys, :] += _dot_tn(pb, dot)
            dq_ref[tile_of(t), :] = dq

        ready = products(0)
        terms = None
        for t in range(n_tiles):
            ahead = products(t + 1) if t + 1 < n_tiles else None
            if terms is not None:
                accumulate(t - 1, terms)
            terms = gradients_of_scores(*ready)
            ready = ahead
        accumulate(n_tiles - 1, terms)

    head = pl.BlockSpec((s_dim, HEAD_PAD), lambda h: (0, h))
    out = jax.ShapeDtypeStruct(q.shape, F32)
    return _pcall(
        body, out_shape=(out, out, out), grid=(MLA_HEADS,),
        in_specs=[head, head, head, head], out_specs=(head, head, head), name=name,
        compiler_params=_params(("parallel",), 3 * [((s_dim, HEAD_PAD), BF16)] + 4 * [((s_dim, HEAD_PAD), F32)],
                                extra=6 * tq * s_dim * 4),
    )(q, k, v, d_o)


def _mla_qkv_bwd(dq, dk, dv, lat, q_norm, kv_norm, wq_t, wkv_t, rope, name):
    s_dim = lat.shape[0]
    width = MLA_HEADS * HEAD_PAD
    tm = 256

    def body(dq_ref, dk_ref, dv_ref, lat_ref, qg_ref, kg_ref, wq_ref, wkv_ref, c_ref, s1_ref, s2_ref,
             dqp_ref, dkv_ref, dlat_ref, dqg_ref, dkg_ref):
        @pl.when(pl.program_id(0) == 0)
        def _():
            dqg_ref[...] = jnp.zeros_like(dqg_ref)
            dkg_ref[...] = jnp.zeros_like(dkg_ref)

        c, s1, s2 = c_ref[...], s1_ref[...], s2_ref[...]
        lane = lax.broadcasted_iota(jnp.int32, (tm, HEAD_PAD), 1)
        low = lane < QK_NOPE
        rot = (lane >= QK_NOPE) & (lane < QK_NOPE + QK_ROPE)
        dkrr = jnp.zeros((tm, HEAD_PAD), F32)
        for h in range(MLA_HEADS):
            sl = slice(h * HEAD_PAD, (h + 1) * HEAD_PAD)
            dqp_ref[:, sl] = _rope_t(dq_ref[:, sl], c, s1, s2).astype(BF16)
            dkh = dk_ref[:, sl]
            dkv_ref[:, sl] = jnp.where(low, dkh, dv_ref[:, sl]).astype(BF16)
            dkrr = dkrr + jnp.where(rot, dkh, 0.0)
        dqn = _dot_nn(dqp_ref[...], wq_ref[...])
        dkvn = _dot_nn(dkv_ref[...], wkv_ref[...])
        cq = lat_ref[:, :Q_LORA]
        ckv = lat_ref[:, Q_LORA:Q_LORA + KV_LORA]
        rq, rkv = _rstd(cq), _rstd(ckv)
        dqg_ref[...] += jnp.sum(dqn * cq * rq, axis=0, keepdims=True)
        dkg_ref[...] += jnp.sum(dkvn * ckv * rkv, axis=0, keepdims=True)
        dlat_ref[:, :Q_LORA] = _rms_bwd(cq, rq, dqn * qg_ref[...])
        dlat_ref[:, Q_LORA:Q_LORA + KV_LORA] = _rms_bwd(ckv, rkv, dkvn * kg_ref[...])
        dlat_ref[:, Q_LORA + KV_LORA:] = _rope_t(dkrr, c, s1, s2)

    row = lambda n: pl.BlockSpec((tm, n), lambda i: (i, 0))
    full = lambda a: pl.BlockSpec(a.shape, lambda i: (0, 0))
    wide = jax.ShapeDtypeStruct((s_dim, width), BF16)
    return _pcall(
        body,
        out_shape=(wide, wide, jax.ShapeDtypeStruct((s_dim, LAT_PAD), F32),
                   jax.ShapeDtypeStruct(q_norm.shape, F32), jax.ShapeDtypeStruct(kv_norm.shape, F32)),
        grid=(s_dim // tm,),
        in_specs=[row(width), row(width), row(width), row(LAT_PAD), full(q_norm), full(kv_norm), full(wq_t), full(wkv_t),
                  row(HEAD_PAD), row(HEAD_PAD), row(HEAD_PAD)],
        out_specs=(row(width), row(width), row(LAT_PAD), full(q_norm), full(kv_norm)), name=name,
        compiler_params=_params(("arbitrary",), 3 * [((tm, width), F32)] + [((tm, LAT_PAD), F32), (wq_t.shape, BF16),
                                                                           (wkv_t.shape, BF16)] + 2 * [((tm, width), BF16)],
                                extra=2 * tm * width * 4),
    )(dq, dk, dv, lat, q_norm, kv_norm, wq_t, wkv_t, *rope)


def _t5_bucket(dist):
    max_exact = N_BUCKETS // 2
    d = jnp.maximum(dist, 1).astype(F32)
    large = max_exact + (jnp.log(d / max_exact) / math.log(MAX_DISTANCE / max_exact)
                         * (N_BUCKETS - max_exact)).astype(jnp.int32)
    large = jnp.minimum(large, N_BUCKETS - 1)
    return jnp.where(dist < max_exact, dist, large)


def _dil_buckets(dilation):
    iq = jnp.arange(DIL_BLOCK)[:, None]
    ik = jnp.arange(2 * DIL_BLOCK)[None, :]
    return _t5_bucket(jnp.maximum(DIL_BLOCK + iq - ik, 0) * dilation)


def _dil_logits(qh, kb, bias_h, first, span):
    if first:
        s = _dot_nt(qh, kb) * DIL_SCALE + bias_h[:, DIL_BLOCK:]
        rel = lax.broadcasted_iota(jnp.int32, s.shape, 0) - lax.broadcasted_iota(jnp.int32, s.shape, 1)
    else:
        s = _dot_nt(qh, kb) * DIL_SCALE + bias_h
        rel = DIL_BLOCK + lax.broadcasted_iota(jnp.int32, s.shape, 0) - lax.broadcasted_iota(jnp.int32, s.shape, 1)
    return jnp.where((rel >= 0) & (rel <= span), s, -jnp.inf)


def _dil_blocks(s_dim, dilation):
    rows = s_dim // dilation
    for r in range(dilation):
        for n in range(rows // DIL_BLOCK):
            lo = r * rows + n * DIL_BLOCK
            keys = slice(lo, lo + DIL_BLOCK) if n == 0 else slice(lo - DIL_BLOCK, lo + DIL_BLOCK)
            start = r + n * DIL_BLOCK * dilation
            tokens = slice(start, start + DIL_BLOCK) if dilation == 1 else pl.ds(start, DIL_BLOCK, stride=dilation)
            yield n == 0, slice(lo, lo + DIL_BLOCK), keys, tokens


def _dil_views(s_dim):
    col = lambda which: pl.BlockSpec((s_dim, HEAD_PAD), lambda p: (0, which * DIL_PAIRS + p))
    nat = pl.BlockSpec((s_dim, HEAD_PAD), lambda p: (0, p))
    bias = pl.BlockSpec((2, DIL_BLOCK, 2 * DIL_BLOCK), lambda p: (p, 0, 0))
    return col, nat, bias


def _dil_attn_fwd(qkv, bias, dilation, span, name):
    s_dim = qkv.shape[0]
    d_dim = DIL_HEADS * DIL_HEAD_DIM
    col, nat, bias_spec = _dil_views(s_dim)

    def body(q_ref, k_ref, v_ref, b_ref, o_ref, l_ref):
        lane = lax.broadcasted_iota(jnp.int32, (DIL_BLOCK, HEAD_PAD), 1)
        klane = lax.broadcasted_iota(jnp.int32, (2 * DIL_BLOCK, HEAD_PAD), 1)
        blocks = list(_dil_blocks(s_dim, dilation))
        for g0 in range(0, len(blocks), DIL_GROUPED):
            group = blocks[g0:g0 + DIL_GROUPED]
            logits = [_dil_logits(jnp.where((lane < DIL_HEAD_DIM) == (h == 0), q_ref[blk, :], 0), k_ref[keys, :], b_ref[h],
                                  first, span) for first, blk, keys, _ in group for h in range(2)]
            soft = []
            for lg in logits:
                mx = jnp.max(lg, axis=-1, keepdims=True)
                e = jnp.exp(lg - mx)
                tot = jnp.sum(e, axis=-1, keepdims=True)
                soft.append(((e * (1.0 / tot)).astype(BF16), mx + jnp.log(tot)))
            for i, (_, _, keys, tokens) in enumerate(group):
                vb = v_ref[keys, :]
                o_acc = jnp.zeros((DIL_BLOCK, HEAD_PAD), F32)
                lse_acc = jnp.zeros((DIL_BLOCK, HEAD_PAD), F32)
                for h in range(2):
                    p, lse = soft[2 * i + h]
                    kmine = (klane[:vb.shape[0]] < DIL_HEAD_DIM) == (h == 0)
                    o_acc = o_acc + _dot_nn(p, jnp.where(kmine, vb, 0))
                    lse_acc = jnp.where((lane < DIL_HEAD_DIM) == (h == 0), lse, lse_acc)
                o_ref[tokens, :] = o_acc
                l_ref[tokens, :] = lse_acc

    out = jax.ShapeDtypeStruct((s_dim, d_dim), F32)
    return _pcall(
        body, out_shape=(out, out), grid=(DIL_PAIRS,),
        in_specs=[col(0), col(1), col(2), bias_spec], out_specs=(nat, nat), name=name,
        compiler_params=_params(("parallel",), 3 * [((s_dim, HEAD_PAD), BF16)] + 2 * [((s_dim, HEAD_PAD), F32)]
                                + [((2, DIL_BLOCK, 2 * DIL_BLOCK), F32)], extra=2**21),
    )(qkv, qkv, qkv, bias)


def _dil_mix(lses, outs, name):
    s_dim, d_dim = outs[0].shape
    tm = TOKEN_TILE
    ng = len(outs)

    def body(*refs):
        ls = [refs[g][...] for g in range(ng)]
        mx = ls[0]
        for g in range(1, ng):
            mx = jnp.maximum(mx, ls[g])
        es = [jnp.exp(l - mx) for l in ls]
        tot = es[0]
        for g in range(1, ng):
            tot = tot + es[g]
        o = None
        for g in range(ng):
            al = es[g] / tot
            refs[2 * ng + g][...] = al
            t = al * refs[ng + g][...]
            o = t if o is None else o + t
        refs[3 * ng][...] = o
        refs[3 * ng + 1][...] = o.astype(BF16)

    row = pl.BlockSpec((tm, d_dim), lambda i: (i, 0))
    f = jax.ShapeDtypeStruct((s_dim, d_dim), F32)
    res = _pcall(
        body, out_shape=tuple(ng * [f] + [f, jax.ShapeDtypeStruct((s_dim, d_dim), BF16)]), grid=(s_dim // tm,),
        in_specs=2 * ng * [row], out_specs=tuple((ng + 2) * [row]), name=name,
        compiler_params=_params(("parallel",), (3 * ng + 2) * [((tm, d_dim), F32)], extra=4 * tm * d_dim * 4),
    )(*lses, *outs)
    return res[:ng], res[ng], res[ng + 1]


def _dil_attn_bwd(qkv, bias, d_o, o_mix, alpha, lse, dilation, span, name):
    s_dim = qkv.shape[0]
    d_dim = DIL_HEADS * DIL_HEAD_DIM
    col, nat, bias_spec = _dil_views(s_dim)

    def body(q_ref, k_ref, v_ref, b_ref, do_ref, om_ref, al_ref, l_ref, dq_ref, dk_ref, dv_ref, db_ref, dk_acc, dv_acc):
        db_ref[...] = jnp.zeros_like(db_ref)
        dk_acc[...] = jnp.zeros_like(dk_acc)
        dv_acc[...] = jnp.zeros_like(dv_acc)
        lane = lax.broadcasted_iota(jnp.int32, (DIL_BLOCK, HEAD_PAD), 1)
        klane = lax.broadcasted_iota(jnp.int32, (2 * DIL_BLOCK, HEAD_PAD), 1)
        blocks = list(_dil_blocks(s_dim, dilation))
        heads = [(lane < DIL_HEAD_DIM) == (h == 0) for h in range(2)]
        for g0 in range(0, len(blocks), DIL_GROUPED):
            group = blocks[g0:g0 + DIL_GROUPED]
            staged = []
            for first, blk, kv_rows, tokens in group:
                qb, kb, vb = q_ref[blk, :], k_ref[kv_rows, :], v_ref[kv_rows, :]
                dog = al_ref[tokens, :] * do_ref[tokens, :]
                row_term = dog * om_ref[tokens, :]
                lse_b = l_ref[tokens, :]
                for h in range(2):
                    qh = jnp.where(heads[h], qb, 0)
                    dogh = jnp.where(heads[h], dog, 0.0).astype(BF16)
                    staged.append((_dil_logits(qh, kb, b_ref[h], first, span), _dot_nt(dogh, vb), qh, dogh,
                                   jnp.max(jnp.where(heads[h], lse_b, -jnp.inf), axis=-1, keepdims=True),
                                   jnp.sum(jnp.where(heads[h], row_term, 0.0), axis=-1, keepdims=True)))
            grads = []
            for i, (logits, dp, qh, dogh, lse_h, row) in enumerate(staged):
                p = jnp.exp(logits - lse_h)
                ds = p * (dp - row)
                if group[i // 2][0]:
                    db_ref[i % 2, :, DIL_BLOCK:] += ds
                else:
                    db_ref[i % 2] += ds
                grads.append(((ds * DIL_SCALE).astype(BF16), p.astype(BF16), qh, dogh))
            for i, (_, blk, kv_rows, _) in enumerate(group):
                kb = k_ref[kv_rows, :]
                dq_acc = jnp.zeros((DIL_BLOCK, HEAD_PAD), F32)
                dk_blk = jnp.zeros((kb.shape[0], HEAD_PAD), F32)
                dv_blk = jnp.zeros((kb.shape[0], HEAD_PAD), F32)
                for h in range(2):
                    dsb, pb, qh, dogh = grads[2 * i + h]
                    kmine = (klane[:kb.shape[0]] < DIL_HEAD_DIM) == (h == 0)
                    dq_acc = dq_acc + _dot_nn(dsb, jnp.where(kmine, kb, 0))
                    dk_blk = dk_blk + _dot_tn(dsb, qh)
                    dv_blk = dv_blk + _dot_tn(pb, dogh)
                dq_ref[blk, :] = dq_acc.astype(BF16)
                dk_acc[kv_rows, :] += dk_blk
                dv_acc[kv_rows, :] += dv_blk
        dk_ref[...] = dk_acc[...].astype(BF16)
        dv_ref[...] = dv_acc[...].astype(BF16)

    grad = jax.ShapeDtypeStruct((s_dim, d_dim), BF16)
    return _pcall(
        body, out_shape=(grad, grad, grad, jax.ShapeDtypeStruct(bias.shape, F32)), grid=(DIL_PAIRS,),
        in_specs=[col(0), col(1), col(2), bias_spec, nat, nat, nat, nat],
        out_specs=(nat, nat, nat, bias_spec), name=name,
        scratch_shapes=[pltpu.VMEM((s_dim, HEAD_PAD), F32), pltpu.VMEM((s_dim, HEAD_PAD), F32)],
        compiler_params=_params(("parallel",), 6 * [((s_dim, HEAD_PAD), BF16)] + 4 * [((s_dim, HEAD_PAD), F32)]
                                + 2 * [((2, DIL_BLOCK, 2 * DIL_BLOCK), F32)], extra=2 * s_dim * HEAD_PAD * 4 + 2**21),
    )(qkv, qkv, qkv, bias, d_o, o_mix, alpha, lse)


def _bias_reduce(dbias, buckets, name):
    n_heads = dbias.shape[0]

    def body(db_ref, bk_ref, o_ref):
        ds, bk = db_ref[0], bk_ref[0]
        lane = lax.broadcasted_iota(jnp.int32, (8, HEAD_PAD), 1)
        acc = jnp.zeros((8, HEAD_PAD), F32)
        for b in range(N_BUCKETS):
            acc = jnp.where(lane == b, jnp.sum(jnp.where(bk == b, ds, 0.0)), acc)
        o_ref[0] = acc

    blk = (1, DIL_BLOCK, 2 * DIL_BLOCK)
    return _pcall(
        body, out_shape=jax.ShapeDtypeStruct((n_heads, 8, HEAD_PAD), F32), grid=(n_heads,),
        in_specs=[pl.BlockSpec(blk, lambda h: (h, 0, 0)), pl.BlockSpec(blk, lambda h: (h // DIL_HEADS, 0, 0))],
        out_specs=pl.BlockSpec((1, 8, HEAD_PAD), lambda h: (h, 0, 0)), name=name,
        compiler_params=_params(("parallel",), [(blk, F32), (blk, jnp.int32)], extra=2**20),
    )(dbias, buckets)


def _loss_grad(y, target, name):
    s_dim, d_dim = y.shape
    tm = TOKEN_TILE

    def body(y_ref, t_ref, dy_ref, l_ref):
        @pl.when(pl.program_id(0) == 0)
        def _():
            l_ref[...] = jnp.zeros_like(l_ref)

        err = y_ref[...] - t_ref[...]
        dy_ref[...] = err / d_dim
        sq = (err * err).reshape(tm // 8, 8, d_dim)
        l_ref[...] += 0.5 * jnp.sum(sq, axis=0) / d_dim

    row = pl.BlockSpec((tm, d_dim), lambda i: (i, 0))
    acc = pl.BlockSpec((8, d_dim), lambda i: (0, 0))
    return _pcall(
        body, out_shape=(jax.ShapeDtypeStruct((s_dim, d_dim), F32), jax.ShapeDtypeStruct((8, d_dim), F32)),
        grid=(s_dim // tm,), in_specs=[row, row], out_specs=(row, acc), name=name,
        compiler_params=_params(("arbitrary",), 3 * [((tm, d_dim), F32)], extra=2 * tm * d_dim * 4),
    )(y, target)


def _mod_fwd(c_all, w_mod, b_loc, name):
    depth, d_dim, n = w_mod.shape
    nb = c_all.shape[0]

    def body(c_ref, w_ref, b_ref, o_ref, s_ref):
        cv = c_ref[...]
        sc = cv * jax.nn.sigmoid(cv)
        s_ref[...] = sc
        o_ref[0] = _dot_nn(sc.astype(BF16), w_ref[0].astype(BF16)) + b_ref[0]

    return _pcall(
        body, out_shape=(jax.ShapeDtypeStruct((depth, nb, n), F32), jax.ShapeDtypeStruct((nb, d_dim), F32)), grid=(depth,),
        in_specs=[pl.BlockSpec((nb, d_dim), lambda i: (0, 0)), pl.BlockSpec((1, d_dim, n), lambda i: (i, 0, 0)),
                  pl.BlockSpec((1, 1, n), lambda i: (i, 0, 0))],
        out_specs=(pl.BlockSpec((1, nb, n), lambda i: (i, 0, 0)), pl.BlockSpec((nb, d_dim), lambda i: (0, 0))), name=name,
        compiler_params=_params(("arbitrary",), [((1, d_dim, n), F32)], extra=d_dim * n * 2 + 2**20),
    )(c_all, w_mod, b_loc.reshape(depth, 1, n))


def _sum_parts(parts, name, transpose=False):
    _, rows, cols = parts.shape
    unit = 128 if transpose else 16
    budget = (7 if transpose else 3) * 2**20
    fits = [t for t in range(unit, rows // 2 + 1, unit) if rows % t == 0 and NDEV * t * cols * parts.dtype.itemsize <= budget]
    tr = max(fits) if fits else rows

    def body(p_ref, o_ref):
        acc = p_ref[0].astype(F32)
        for k in range(1, NDEV):
            acc = acc + p_ref[k].astype(F32)
        o_ref[...] = acc.T if transpose else acc

    out_shape, out_block = ((cols, rows), (cols, tr)) if transpose else ((rows, cols), (tr, cols))
    return _pcall(
        body, out_shape=jax.ShapeDtypeStruct(out_shape, F32), grid=(rows // tr,),
        in_specs=[pl.BlockSpec((NDEV, tr, cols), lambda i: (0, i, 0))],
        out_specs=pl.BlockSpec(out_block, (lambda i: (0, i)) if transpose else (lambda i: (i, 0))),
        name=name, compiler_params=_params(("parallel",), [((NDEV, tr, cols), parts.dtype), (out_block, F32)], extra=2**22),
    )(parts)


def _adamw(w, g, m, v, name):
    shape = w.shape
    cols = shape[-1]
    rows = math.prod(shape[:-1])
    tr = rows
    for cand in (2048, 1024, 512, 256, 128, 64, 32, 16, 8):
        if rows % cand == 0 and rows > cand and cand * cols * 4 <= 2**21:
            tr = cand
            break

    def body(w_ref, g_ref, m_ref, v_ref, d_ref, mo_ref, vo_ref):
        gv = g_ref[...]
        mn = ADAM_B1 * m_ref[...] + (1.0 - ADAM_B1) * gv
        vn = ADAM_B2 * v_ref[...] + (1.0 - ADAM_B2) * (gv * gv)
        m_hat = mn / (1.0 - ADAM_B1 ** ADAM_STEP)
        v_hat = vn / (1.0 - ADAM_B2 ** ADAM_STEP)
        d_ref[...] = -ADAM_LR * (m_hat / (jnp.sqrt(v_hat) + ADAM_EPS) + ADAM_WD * w_ref[...])
        mo_ref[...] = mn
        vo_ref[...] = vn

    blk = pl.BlockSpec((tr, cols), lambda i: (i, 0))
    out = jax.ShapeDtypeStruct((rows, cols), F32)
    res = _pcall(
        body, out_shape=(out, out, out), grid=(rows // tr,), in_specs=4 * [blk], out_specs=(blk, blk, blk), name=name,
        compiler_params=_params(("parallel",), 7 * [((tr, cols), F32)], extra=4 * tr * cols * 4),
    )(*(a.reshape(rows, cols) for a in (w, g, m, v)))
    return tuple(r.reshape(shape) for r in res)


def _peers():
    x, y, c = lax.axis_index("x"), lax.axis_index("y"), lax.axis_index("c")
    flip = lambda v, f: 1 - v if f else v
    peers = []
    for f in range(1, NDEV):
        px, py, pc = flip(x, f & 4), flip(y, f & 2), flip(c, f & 1)
        peers.append(((px, py, pc), 4 * px + 2 * py + pc))
    return (x, y, c), 4 * x + 2 * y + c, peers


def _places():
    x, y, c = lax.axis_index("x"), lax.axis_index("y"), lax.axis_index("c")
    place = lambda px, py, pc: ((px, py, pc), 4 * px + 2 * py + pc)
    return place(x, y, c), place(x, y, 1 - c), [place(1 - x, y, c), place(x, 1 - y, c), place(1 - x, 1 - y, c)]


def _exchange(arrs, gather, name):
    n = len(arrs)
    hbm = pl.BlockSpec(memory_space=pltpu.HBM)
    if gather:
        out_shape = [jax.ShapeDtypeStruct((NDEV * a.shape[0], a.shape[1]), a.dtype) for a in arrs]
    else:
        out_shape = [jax.ShapeDtypeStruct((NDEV, a.shape[0] // NDEV, a.shape[1]), a.dtype) for a in arrs]

    def body(*refs):
        ins, outs = refs[:n], refs[n:2 * n]
        send_sems, recv_sems, local_sems = refs[2 * n:]
        me_pos, me, peers = _peers()
        local = []
        for k in range(n):
            rows = arrs[k].shape[0] if gather else arrs[k].shape[0] // NDEV
            if gather:
                src_of = lambda idx: ins[k]
                dst_of = lambda idx: outs[k].at[pl.ds(me * rows, rows)]
                mine = (ins[k], outs[k].at[pl.ds(me * rows, rows)])
            else:
                src_of = lambda idx: ins[k].at[pl.ds(idx * rows, rows)]
                dst_of = lambda idx: outs[k].at[me]
                mine = (ins[k].at[pl.ds(me * rows, rows)], outs[k].at[me])
            cp = pltpu.make_async_copy(mine[0], mine[1], local_sems.at[k])
            cp.start()
            local.append(cp)
            for pos, idx in peers:
                pltpu.make_async_remote_copy(src_ref=src_of(idx), dst_ref=dst_of(idx), send_sem=send_sems.at[k],
                                             recv_sem=recv_sems.at[k], device_id=pos, device_id_type=MESH).start()
        for k in range(n):
            rows = arrs[k].shape[0] if gather else arrs[k].shape[0] // NDEV
            sent = ins[k].at[pl.ds(0, (NDEV - 1) * rows)] if not gather else outs[k].at[pl.ds(0, (NDEV - 1) * rows)]
            got = outs[k].at[pl.ds(0, (NDEV - 1) * rows)] if gather else outs[k].at[pl.ds(0, NDEV - 1)]
            pltpu.make_async_remote_copy(src_ref=sent, dst_ref=sent, send_sem=send_sems.at[k], recv_sem=recv_sems.at[k],
                                         device_id=me_pos, device_id_type=MESH).wait_send()
            pltpu.make_async_remote_copy(src_ref=got, dst_ref=got, send_sem=send_sems.at[k], recv_sem=recv_sems.at[k],
                                         device_id=me_pos, device_id_type=MESH).wait_recv()
            local[k].wait()

    return pl.pallas_call(
        body, out_shape=out_shape, in_specs=n * [hbm], out_specs=n * [hbm], name=name,
        scratch_shapes=[pltpu.SemaphoreType.DMA((n,)), pltpu.SemaphoreType.DMA((n,)), pltpu.SemaphoreType.DMA((n,))],
        compiler_params=pltpu.CompilerParams(has_side_effects=True),
    )(*arrs)


_HBM = pl.BlockSpec(memory_space=pltpu.HBM)
_SEM = pl.BlockSpec(memory_space=pltpu.SEMAPHORE)
_DATAFLOW = pltpu.SideEffectType.DATAFLOW_SIDE_EFFECTING


def _split_start(srcs, groups, gather, name):
    n = len(srcs)
    if gather:
        lands = [lax.empty((NDEV * a.shape[0], a.shape[1]), a.dtype) for a in srcs]
    else:
        lands = [lax.empty((NDEV, a.shape[0] // NDEV, a.shape[1]), a.dtype) for a in srcs]
    n_sem = 3 * len(groups)

    def body(*refs):
        src_refs, land_refs = refs[:n], refs[n:2 * n]
        sems = refs[2 * n:2 * n + n_sem]
        token = refs[-1]
        (_, my), sibling, chips = _places()
        _, _, peers = _peers()
        targets = [sibling] + chips if gather else peers
        for g, members in enumerate(groups):
            for j, k in enumerate(members):
                _own_copy(src_refs[k], land_refs[k], sems[3 * g + 2].at[j], my, gather).start()
        for g, members in enumerate(groups):
            for j, k in enumerate(members):
                rows = srcs[k].shape[0] if gather else srcs[k].shape[0] // NDEV
                for pos, idx in targets:
                    src = src_refs[k] if gather else src_refs[k].at[pl.ds(idx * rows, rows)]
                    dst = land_refs[k].at[pl.ds(my * rows, rows)] if gather else land_refs[k].at[my]
                    pltpu.make_async_remote_copy(src_ref=src, dst_ref=dst, send_sem=sems[3 * g].at[j],
                                                 recv_sem=sems[3 * g + 1].at[j], device_id=pos, device_id_type=MESH).start()
        token[...] = jnp.zeros_like(token)

    out_shape = []
    for members in groups:
        out_shape += 3 * [pltpu.SemaphoreType.DMA((len(members),))]
    out_shape += [pltpu.HBM(a.shape, a.dtype) for a in srcs] + [pltpu.HBM(a.shape, a.dtype) for a in lands]
    out_shape.append(jax.ShapeDtypeStruct((8, 128), F32))
    res = pl.pallas_call(
        body, name=name, out_shape=tuple(out_shape), in_specs=2 * n * [_HBM],
        out_specs=tuple(n_sem * [_SEM] + 2 * n * [_HBM] + [pl.BlockSpec(memory_space=pltpu.VMEM)]),
        input_output_aliases={i: n_sem + i for i in range(2 * n)},
        compiler_params=pltpu.CompilerParams(has_side_effects=_DATAFLOW),
    )(*[pltpu.with_memory_space_constraint(a, pltpu.HBM) for a in list(srcs) + lands])
    sems = [tuple(res[3 * g:3 * g + 3]) for g in range(len(groups))]
    return sems, list(res[n_sem:n_sem + n]), list(res[n_sem + n:n_sem + 2 * n]), res[-1]


def _own_copy(src_ref, land_ref, sem, my, gather):
    if gather:
        rows = src_ref.shape[0]
        return pltpu.make_async_copy(src_ref, land_ref.at[pl.ds(my * rows, rows)], sem)
    rows = src_ref.shape[0] // NDEV
    return pltpu.make_async_copy(src_ref.at[pl.ds(my * rows, rows)], land_ref.at[my], sem)


def _wait_all(land_ref, blocks_per_dev, copies, send_sem, recv_sem, me_pos):
    part = land_ref.at[pl.ds(0, copies * blocks_per_dev)]
    pltpu.make_async_remote_copy(src_ref=part, dst_ref=part, send_sem=send_sem, recv_sem=recv_sem,
                                 device_id=me_pos, device_id_type=MESH).wait()


def _gather_forward(sems, srcs, lands, after, name):
    n = len(srcs)

    def body(*refs):
        land_refs = refs[n:2 * n]
        send_a, recv_a = refs[2 * n], refs[2 * n + 1]
        send_b, recv_b = refs[2 * n + 3], refs[2 * n + 4]
        token = refs[-1]
        (me_pos, _), sibling, chips = _places()
        for j in range(n):
            _wait_all(land_refs[j], lands[j].shape[0] // NDEV, 1 + OTHER_CHIPS, send_a.at[j], recv_a.at[j], me_pos)
        for j in range(n):
            rows = lands[j].shape[0] // NDEV
            for _, idx in chips:
                block = land_refs[j].at[pl.ds(idx * rows, rows)]
                pltpu.make_async_remote_copy(src_ref=block, dst_ref=block, send_sem=send_b.at[j], recv_sem=recv_b.at[j],
                                             device_id=sibling[0], device_id_type=MESH).start()
        token[...] = jnp.zeros_like(token)

    res = pl.pallas_call(
        body, name=name,
        out_shape=(pltpu.SemaphoreType.DMA((n,)), pltpu.SemaphoreType.DMA((n,)))
        + tuple(pltpu.HBM(a.shape, a.dtype) for a in list(srcs) + list(lands)) + (jax.ShapeDtypeStruct((8, 128), F32),),
        in_specs=2 * n * [_HBM] + [_SEM, _SEM, pl.BlockSpec(memory_space=pl.ANY)],
        out_specs=tuple([_SEM, _SEM] + 2 * n * [_HBM] + [pl.BlockSpec(memory_space=pltpu.VMEM)]),
        input_output_aliases={i: 2 + i for i in range(2 * n)},
        compiler_params=pltpu.CompilerParams(has_side_effects=_DATAFLOW),
    )(*srcs, *lands, sems[0], sems[1], after)
    return (res[0], res[1]), list(res[2:2 + n]), list(res[2 + n:2 + 2 * n]), res[-1]


def _split_wait(sems, srcs, lands, after, copies, gather, name):
    n = len(srcs)

    def body(*refs):
        src_refs, land_refs = refs[:n], refs[n:2 * n]
        send_sem, recv_sem, local_sem = refs[2 * n], refs[2 * n + 1], refs[2 * n + 2]
        (me_pos, my), _, _ = _places()
        for j in range(n):
            _wait_all(land_refs[j], lands[j].shape[0] // NDEV, copies, send_sem.at[j], recv_sem.at[j], me_pos)
            _own_copy(src_refs[j], land_refs[j], local_sem.at[j], my, gather).wait()

    res = pl.pallas_call(
        body, name=name, out_shape=tuple(pltpu.HBM(a.shape, a.dtype) for a in list(srcs) + list(lands)),
        in_specs=2 * n * [_HBM] + [_SEM, _SEM, _SEM, pl.BlockSpec(memory_space=pl.ANY)], out_specs=tuple(2 * n * [_HBM]),
        input_output_aliases={i: i for i in range(2 * n)},
        compiler_params=pltpu.CompilerParams(has_side_effects=_DATAFLOW),
    )(*srcs, *lands, sems[0], sems[1], sems[2], after)
    return list(res[n:])


def _chained(gate, mid, after):
    return gate if mid is None else gate + mid(after)[:1, :1]


def _ffn_fwd(x, norms, mod, w, mid=None):
    (pre_g, post_g), (shift, scale, gate), (wg_t, wu_t, wd) = norms, mod, w
    if not callable(wd):
        hn, g, u, a, x_out, f = _ffn_fwd_fused(x, pre_g, scale, shift, post_g, _chained(gate, mid, x), wg_t, wu_t, wd, "ffn_fwd")
        return x_out, (x, hn, g, u, a, f), (wg_t, wu_t, wd)
    hn, g, u, a = _ffn_up(x, pre_g, scale, shift, wg_t, wu_t, "ffn_up")
    wd = wd(a)
    x_out, f = _mm_post(a, wd, x, post_g, _chained(gate, mid, a), FFN_RES, "ffn_down")
    return x_out, (x, hn, g, u, a, f), (wg_t, wu_t, wd)


def _ffn_bwd(dx_out, saved, norms, mod, w, send=None):
    (pre_g, post_g), (_, scale, gate), (wg_t, wu_t, wd) = norms, mod, w
    x, hn, g, u, a, f = saved
    d_model = x.shape[1]
    if send is None:
        df, dg, du, dx, dgate, dpost, dshift, dscale, dpre = _ffn_bwd_fused(dx_out, saved, pre_g, post_g, scale, gate,
                                                                            wg_t, wu_t, wd, "ffn_bwd")
        return dx, (dpre, dpost), (dshift, dscale, dgate), tuple(_ffn_dw(dg, du, a, hn, df, "ffn_dw3"))
    sent = send
    df, dgate, dpost = _post_bwd(dx_out, f, post_g, gate, FFN_RES, "ffn_post_bwd")
    dwd = _mm([(a, df)], "tn", BF16, 256, d_model, "ffn_dw")
    dg, du = _ffn_dgu(df, wd, g, u, "ffn_dgu", after=sent(2, dwd))
    dwg_t = _mm([(dg, hn)], "tn", BF16, 256, d_model, "ffn_dw")
    dwu_t = _mm([(du, hn)], "tn", BF16, 256, d_model, "ffn_dw", after=sent(0, dwg_t))
    dhn = _mm([(dg, wg_t), (du, wu_t)], "nn", F32, TOKEN_TILE, d_model, "ffn_dhn", after=sent(1, dwu_t))
    dx, dshift, dscale, dpre = _prenorm_bwd(dx_out, [dhn], x, pre_g, scale, "prenorm_bwd")
    return dx, (dpre, dpost), (dshift, dscale, dgate), (dwg_t, dwu_t, dwd)


def _mla_fwd(x, norms, mod, w, rope, mid=None):
    (pre_g, post_g), (shift, scale, gate) = norms, mod
    w_in, q_norm, wq_t, kv_norm, wkv_t, wo = w
    hn, lat = _prenorm_mm(x, pre_g, scale, shift, w_in, "nn", F32, LAT_PAD, "mla_in")
    gate = _chained(gate, mid, lat)
    q, k, v, qn, kvn = _mla_qkv(lat, q_norm, kv_norm, wq_t, wkv_t, rope, "mla_qkv")
    o = _mla_attn_fwd(q, k, v, "mla_attn_fwd")
    x_out, f = _mm_post(o, wo, x, post_g, gate, 1.0, "mla_out")
    return x_out, (x, hn, lat, q, k, v, qn, kvn, o, f)


def _mla_bwd(dx_out, saved, norms, mod, w, rope):
    (pre_g, post_g), (_, scale, gate) = norms, mod
    w_in, q_norm, wq_t, kv_norm, wkv_t, wo = w
    x, hn, lat, q, k, v, qn, kvn, o, f = saved
    d_model = x.shape[1]
    df, dgate, dpost = _post_bwd(dx_out, f, post_g, gate, 1.0, "mix_post_bwd")
    d_o = _mm([(df, wo)], "nt", F32, TOKEN_TILE, wo.shape[0], "mla_do")
    dwo = _mm([(o, df)], "tn", BF16, TOKEN_TILE, d_model, "mla_dwo")
    dq, dk, dv = _mla_attn_bwd(q, k, v, d_o, "mla_attn_bwd")
    dqp, dkv, dlat, dq_norm, dkv_norm = _mla_qkv_bwd(dq, dk, dv, lat, q_norm, kv_norm, wq_t, wkv_t, rope, "mla_qkv_bwd")
    dwq_t = _mm([(dqp, qn)], "tn", BF16, TOKEN_TILE, Q_LORA, "mla_dwq")
    dwkv_t = _mm([(dkv, kvn)], "tn", BF16, TOKEN_TILE, KV_LORA, "mla_dwkv")
    dw_in = _mm([(hn, dlat)], "tn", BF16, TOKEN_TILE, LAT_PAD, "mla_dwin")
    dhn = _mm([(dlat, w_in)], "nt", F32, TOKEN_TILE, d_model, "mla_dhn")
    dx, dshift, dscale, dpre = _prenorm_bwd(dx_out, [dhn], x, pre_g, scale, "prenorm_bwd")
    return dx, (dpre, dpost), (dshift, dscale, dgate), (dw_in, dq_norm, dwq_t, dkv_norm, dwkv_t, dwo)


def _dil_fwd(x, norms, mod, w, bias, mid=None):
    (pre_g, post_g), (shift, scale, gate), (w_in_t, wo) = norms, mod, w
    width = 3 * DIL_HEADS * DIL_HEAD_DIM
    hns, qkvs, outs, lses = [], [], [], []
    for g, (window, dilation) in enumerate(DIL_GROUPS):
        hn, qkv = _prenorm_mm(x, pre_g, scale, shift, w_in_t, "nt", BF16, width, "dil_in", perm=dilation,
                              w_rows=(g * width, width))
        if g == 0:
            gate = _chained(gate, mid, qkv)
        o, lse = _dil_attn_fwd(qkv, bias[g], dilation, window // dilation, "dil_attn_fwd")
        hns.append(hn), qkvs.append(qkv), outs.append(o), lses.append(lse)
    alphas, o_mix, o_mix_b = _dil_mix(lses, outs, "dil_mix")
    x_out, f = _mm_post(o_mix_b, wo, x, post_g, gate, 1.0, "dil_out")
    return x_out, (x, hns, qkvs, lses, alphas, o_mix, o_mix_b, f)


def _dil_bwd(dx_out, saved, norms, mod, w, bias):
    (pre_g, post_g), (_, scale, gate), (w_in_t, wo) = norms, mod, w
    x, hns, qkvs, lses, alphas, o_mix, o_mix_b, f = saved
    d_model = x.shape[1]
    inner = DIL_HEADS * DIL_HEAD_DIM
    df, dgate, dpost = _post_bwd(dx_out, f, post_g, gate, 1.0, "mix_post_bwd")
    d_o = _mm([(df, wo)], "nt", F32, TOKEN_TILE, inner, "dil_do")
    dwo = _mm([(o_mix_b, df)], "tn", BF16, TOKEN_TILE, d_model, "dil_dwo")
    dhns, dws, dbs = [], [], []
    for g, (window, dilation) in enumerate(DIL_GROUPS):
        grads = _dil_attn_bwd(qkvs[g], bias[g], d_o, o_mix, alphas[g], lses[g], dilation, window // dilation, "dil_attn_bwd")
        dbs.append(grads[3])
        dhns.append(_mm([(grads[j], w_in_t) for j in range(3)], "nn", F32, TOKEN_TILE, d_model, "dil_dhn", out_perm=dilation,
                        b_rows=[(3 * g + j) * inner for j in range(3)]))
        dws += list(_mm_tn_shared(list(grads[:3]), hns[g], "dil_dwin"))
    dx, dshift, dscale, dpre = _prenorm_bwd(dx_out, dhns, x, pre_g, scale, "prenorm_bwd3")
    return dx, (dpre, dpost), (dshift, dscale, dgate), (jnp.concatenate(dws, axis=0), dwo), jnp.concatenate(dbs, axis=0)


def _pad_rows(a, rows):
    return jnp.pad(a, ((0, rows - a.shape[0]), (0, 0)))


def _lanes(a):
    flat = a.reshape(-1).astype(F32)
    rows = -(-flat.shape[0] // 1024) * 8
    return jnp.pad(flat, (0, rows * 128 - flat.shape[0])).reshape(rows, 128)


def kernel(x, c, norm_pre, norm_post, w_mod, b_mod, ffn_w_gate, ffn_w_up, ffn_w_down, mla_w_in, mla_q_norm, mla_w_q_up, mla_kv_norm, mla_w_kv_up, mla_w_o, dil_w_in, dil_w_o, rel_bias, loss_target, m_norm_pre, m_norm_post, m_w_mod, m_b_mod, m_ffn_w_gate, m_ffn_w_up, m_ffn_w_down, m_mla_w_in, m_mla_q_norm, m_mla_w_q_up, m_mla_kv_norm, m_mla_w_kv_up, m_mla_w_o, m_dil_w_in, m_dil_w_o, m_rel_bias, v_norm_pre, v_norm_post, v_w_mod, v_b_mod, v_ffn_w_gate, v_ffn_w_up, v_ffn_w_down, v_mla_w_in, v_mla_q_norm, v_mla_w_q_up, v_mla_kv_norm, v_mla_w_kv_up, v_mla_w_o, v_dil_w_in, v_dil_w_o, v_rel_bias):
    me = 4 * lax.axis_index("x") + 2 * lax.axis_index("y") + lax.axis_index("c")
    depth, n_sub, d_loc = norm_pre.shape
    d_model = x.shape[2]
    mod_loc_cols = w_mod.shape[2]
    x0, target = x[0], loss_target[0]

    bf_t = lambda a: a.astype(BF16).T
    ffn_ids = [(i, h) for i in range(depth) for h in range(2)]
    shards = []
    for i, h in ffn_ids:
        shards += [bf_t(ffn_w_gate[i, h]), bf_t(ffn_w_up[i, h]), ffn_w_down[i, h].astype(BF16)]
    shards += [mla_w_in[0].astype(BF16), bf_t(mla_w_q_up[0]), bf_t(mla_w_kv_up[0]), mla_w_o[0].astype(BF16),
               bf_t(dil_w_in[0]), dil_w_o[0].astype(BF16)]
    n_ffn = 3 * len(ffn_ids)
    members = {(0, 0): [0, 1, 2], (0, 1): [n_ffn, n_ffn + 1, n_ffn + 2, n_ffn + 3], (0, 2): [3, 4, 5],
               (1, 0): [6, 7, 8], (1, 1): [n_ffn + 4, n_ffn + 5], (1, 2): [9, 10, 11]}
    order = [(i, s) for i in range(depth) for s in range(n_sub)]

    small = jnp.concatenate([c.reshape(8, 128), _pad_rows(norm_pre.reshape(depth * n_sub, d_loc), 8),
                             _pad_rows(norm_post.reshape(depth * n_sub, d_loc), 8)], axis=0)
    small_all = _exchange([small], True, "gather_small")[0].reshape(NDEV, 24, 128)
    c_all = small_all[:, 0:8].reshape(NDEV, d_model)
    gains = lambda lo: jnp.transpose(small_all[:, lo:lo + depth * n_sub], (1, 0, 2)).reshape(depth, n_sub, 1, d_model)
    pre_full, post_full = gains(8), gains(16)

    b_loc = lax.dynamic_slice(b_mod, (0, me * mod_loc_cols), (depth, mod_loc_cols))
    mod_cols, silu_c = _mod_fwd(c_all, w_mod, b_loc, "mod_fwd")
    mod_all = _exchange([mod_cols.reshape(depth * NDEV, mod_loc_cols)], True, "gather_mod")[0]
    mod_all = mod_all.reshape(NDEV, depth, NDEV, mod_loc_cols)
    mod_mine = lax.dynamic_index_in_dim(mod_all, me, axis=2, keepdims=False)
    mod = jnp.transpose(mod_mine, (1, 0, 2)).reshape(depth, n_sub, 3, 1, d_model)

    shards[0], _ = lax.optimization_barrier((shards[0], mod_all))
    first = order[0]
    stages = [("%d%d" % first, members[first][:2]), ("%d%dd" % first, members[first][2:])]
    stages += [("%d%d" % key, members[key]) for key in order[1:]]
    stage_names = [name for name, _ in stages]
    g_sems, g_srcs, g_lands, g_token = _split_start(shards, [idx for _, idx in stages], True, "gather_weights_start")

    forwarded = {}

    def forward(stage, after):
        idx = stages[stage_names.index(stage)][1]
        forwarded[stage] = _gather_forward(g_sems[stage_names.index(stage)], [g_srcs[k] for k in idx],
                                           [g_lands[k] for k in idx], after, "gather_forward_" + stage)
        return forwarded[stage][3]

    def weights_of(stage, after):
        (send_b, recv_b), srcs, lands, _ = forwarded[stage]
        local = g_sems[stage_names.index(stage)][2]
        return _split_wait((send_b, recv_b, local), srcs, lands, after, OTHER_CHIPS, True, "gather_wait_" + stage)

    def late_down(after):
        forward("%d%dd" % first, after)
        return weights_of("%d%dd" % first, after)[0]

    lat_real = Q_LORA + KV_LORA
    qk = QK_NOPE + QK_ROPE

    def mla_weights(after):
        w_in, wq_t, wkv_t, wo = weights_of("01", after)
        w_in_pad = jnp.concatenate([w_in[:, :lat_real], jnp.zeros((d_model, QK_NOPE), BF16), w_in[:, lat_real:],
                                    jnp.zeros((d_model, HEAD_PAD - QK_NOPE - QK_ROPE), BF16)], axis=1)
        wq_pad = jnp.pad(wq_t.reshape(MLA_HEADS, qk, Q_LORA), ((0, 0), (0, HEAD_PAD - qk), (0, 0)))
        wo_pad = jnp.pad(wo.reshape(MLA_HEADS, V_HEAD, d_model), ((0, 0), (HEAD_PAD - V_HEAD, 0), (0, 0)))
        return (w_in_pad, mla_q_norm, wq_pad.reshape(MLA_HEADS * HEAD_PAD, Q_LORA), mla_kv_norm, wkv_t,
                wo_pad.reshape(MLA_HEADS * HEAD_PAD, d_model))

    zero = g_token[0, 0]
    rope = _rope_tables(zero)
    buckets = jnp.stack([_dil_buckets(dil) for _, dil in DIL_GROUPS]) + zero.astype(jnp.int32)
    onehot = (buckets[..., None] == jnp.arange(N_BUCKETS)).astype(F32)
    bias = jnp.einsum("gqkb,bgh->ghqk", onehot, rel_bias.reshape(N_BUCKETS, len(DIL_GROUPS), DIL_HEADS),
                      precision=lax.Precision.HIGHEST)

    norms = lambda i, s: (pre_full[i, s], post_full[i, s])
    mods = lambda i, s: (mod[i, s, 0], mod[i, s, 1], mod[i, s, 2])
    saved, weights = {}, {}
    h = lax.optimization_barrier((x0, bias, buckets, *rope))[0]
    forward("%d%d" % first, h)
    for n, (i, s) in enumerate(order):
        got = mla_weights(h) if (s == 1 and i % 2 == 0) else tuple(weights_of("%d%d" % (i, s), h))
        mid = None if n + 1 == len(order) else (lambda after, nxt="%d%d" % order[n + 1]: forward(nxt, after))
        if s != 1:
            if len(got) == 3:
                h, saved[i, s], weights[i, s] = _ffn_fwd(h, norms(i, s), mods(i, s), got)
                if mid is not None:
                    mid(h)
            else:
                h, saved[i, s], weights[i, s] = _ffn_fwd(h, norms(i, s), mods(i, s), (*got, late_down), mid)
            continue
        weights[i, s] = got
        if i % 2 == 0:
            h, saved[i, s] = _mla_fwd(h, norms(i, s), mods(i, s), weights[i, s], rope, mid)
        else:
            h, saved[i, s] = _dil_fwd(h, norms(i, s), mods(i, s), weights[i, s], bias, mid)
    dh, loss_parts = _loss_grad(h, target, "loss")

    dnorm, dmod, sent = {}, {}, {}
    token = jnp.zeros((8, 128), F32)
    last = order[0]

    def send_last(j, dw):
        sent[last, j] = _split_start([dw], [[0]], False, "scatter_start_%d%d_%d" % (*last, j))
        return sent[last, j][3]

    for i, s in reversed(order):
        md = mods(i, s)
        md = (md[0], md[1], md[2] + token[:1, :1])
        if (i, s) == last:
            dh, dnorm[i, s], dmod[i, s], _ = _ffn_bwd(dh, saved[i, s], norms(i, s), md, weights[i, s], send_last)
            continue
        if s != 1:
            dh, dnorm[i, s], dmod[i, s], dws = _ffn_bwd(dh, saved[i, s], norms(i, s), md, weights[i, s])
        elif i % 2 == 0:
            dh, dnorm[i, s], dmod[i, s], dmla = _mla_bwd(dh, saved[i, s], norms(i, s), md, weights[i, s], rope)
            dw_in_pad, dq_norm, dwq_pad, dkv_norm, dwkv_t, dwo_pad = dmla
            dw_in = jnp.concatenate([dw_in_pad[:, :lat_real], dw_in_pad[:, lat_real + QK_NOPE:lat_real + qk]], axis=1)
            dwq_t = dwq_pad.reshape(MLA_HEADS, HEAD_PAD, Q_LORA)[:, :qk].reshape(MLA_HEADS * qk, Q_LORA)
            dwo = dwo_pad.reshape(MLA_HEADS, HEAD_PAD, d_model)[:, HEAD_PAD - V_HEAD:].reshape(MLA_HEADS * V_HEAD, d_model)
            dws = (dw_in, dwq_t, dwkv_t, dwo)
        else:
            dh, dnorm[i, s], dmod[i, s], dws, dbias = _dil_bwd(dh, saved[i, s], norms(i, s), md, weights[i, s], bias)
        sent[i, s] = _split_start(list(dws), [list(range(len(dws)))], False, "scatter_start_%d%d" % (i, s))
        token = sent[i, s][3]
    grad_x = dh[None]

    mine = {}
    transposed = {3 * n + j for n in range(len(ffn_ids)) for j in (0, 1)} | {n_ffn + 1, n_ffn + 2, n_ffn + 4}
    for key in reversed(order[1:]):
        sems, srcs, lands, _ = sent[key]
        parts = _split_wait(sems[0], srcs, lands, dh, NDEV - 1, False, "scatter_wait_%d%d" % key)
        for k, p in zip(members[key], parts):
            mine[k] = _sum_parts(p, "sum_parts", k in transposed)
    g_mla_in, g_q_up, g_kv_up, g_mla_o, g_dil_in, g_dil_o = (mine[k] for k in range(n_ffn, n_ffn + 6))
    g_mla_in, g_q_up, g_kv_up, g_mla_o = g_mla_in[None], g_q_up[None], g_kv_up[None], g_mla_o[None]
    g_dil_in, g_dil_o = g_dil_in[None], g_dil_o[None]
    early = {"mla_w_in": _adamw(mla_w_in, g_mla_in, m_mla_w_in, v_mla_w_in, "adamw"),
             "mla_w_q_up": _adamw(mla_w_q_up, g_q_up, m_mla_w_q_up, v_mla_w_q_up, "adamw"),
             "mla_w_kv_up": _adamw(mla_w_kv_up, g_kv_up, m_mla_w_kv_up, v_mla_w_kv_up, "adamw"),
             "mla_w_o": _adamw(mla_w_o, g_mla_o, m_mla_w_o, v_mla_w_o, "adamw"),
             "dil_w_in": _adamw(dil_w_in, g_dil_in, m_dil_w_in, v_dil_w_in, "adamw"),
             "dil_w_o": _adamw(dil_w_o, g_dil_o, m_dil_w_o, v_dil_w_o, "adamw")}
    dbias_sums = _bias_reduce(dbias, buckets, "bias_reduce")
    tied = lax.optimization_barrier((dbias_sums, *[a for step in early.values() for a in step]))
    dbias_sums, early = tied[0], {name: tuple(tied[1 + 3 * n:4 + 3 * n]) for n, name in enumerate(early)}
    for j in (2, 0, 1):
        sems, srcs, lands, _ = sent[last, j]
        parts = _split_wait(sems[0], srcs, lands, dbias_sums, NDEV - 1, False, "scatter_wait_%d%d_%d" % (*last, j))
        mine[members[last][j]] = _sum_parts(parts[0], "sum_parts", members[last][j] in transposed)
    g_gate = jnp.stack([mine[3 * n] for n in range(len(ffn_ids))]).reshape(ffn_w_gate.shape)
    g_up = jnp.stack([mine[3 * n + 1] for n in range(len(ffn_ids))]).reshape(ffn_w_up.shape)
    g_down = jnp.stack([mine[3 * n + 2] for n in range(len(ffn_ids))]).reshape(ffn_w_down.shape)

    dmod_mine = jnp.concatenate([jnp.concatenate(dmod[i, s], axis=0) for i in range(depth) for s in range(n_sub)], axis=0)
    dpre_mine = jnp.concatenate([dnorm[i, s][0] for i in range(depth) for s in range(n_sub)], axis=0)
    dpost_mine = jnp.concatenate([dnorm[i, s][1] for i in range(depth) for s in range(n_sub)], axis=0)
    dbias_tab = dbias_sums[:, 0, :N_BUCKETS].T
    pieces = [dmod_mine, dpre_mine, dpost_mine, dq_norm, dkv_norm, dbias_tab, jnp.sum(loss_parts).reshape(1, 1)]
    packed = [_lanes(p) for p in pieces]
    offs = [0]
    for p in packed:
        offs.append(offs[-1] + p.shape[0])
    everyone = _exchange([jnp.concatenate(packed, axis=0)], True, "gather_small_grads")[0].reshape(NDEV, offs[-1], 128)
    total = _sum_parts(everyone, "sum_small")
    take = lambda n, shape: total[offs[n]:offs[n + 1]].reshape(-1)[:math.prod(shape)].reshape(shape)
    g_b_mod = take(0, b_mod.shape)
    col0 = me * d_loc
    g_norm_pre = lax.dynamic_slice(take(1, (depth, n_sub, d_model)), (0, 0, col0), norm_pre.shape)
    g_norm_post = lax.dynamic_slice(take(2, (depth, n_sub, d_model)), (0, 0, col0), norm_post.shape)
    g_q_norm, g_kv_norm = take(3, mla_q_norm.shape), take(4, mla_kv_norm.shape)
    g_rel_bias = take(5, rel_bias.shape)
    loss = take(6, ())

    dmod_all = everyone[:, offs[0]:offs[1]].reshape(NDEV, depth, NDEV * mod_loc_cols)
    dmod_cols = lax.dynamic_slice(dmod_all, (0, 0, me * mod_loc_cols), (NDEV, depth, mod_loc_cols))
    silu_t = jnp.pad(silu_c.T, ((0, 0), (0, HEAD_PAD - NDEV)))
    g_w_mod = jnp.stack([_mm([(silu_t, jnp.pad(dmod_cols[:, i], ((0, HEAD_PAD - NDEV), (0, 0))))], "nn", F32, TOKEN_TILE,
                             mod_loc_cols, "mod_bwd") for i in range(depth)])

    ws = (norm_pre, norm_post, w_mod, b_mod, ffn_w_gate, ffn_w_up, ffn_w_down, mla_w_in, mla_q_norm, mla_w_q_up, mla_kv_norm,
          mla_w_kv_up, mla_w_o, dil_w_in, dil_w_o, rel_bias)
    gs = (g_norm_pre, g_norm_post, g_w_mod, g_b_mod, g_gate, g_up, g_down, g_mla_in, g_q_norm, g_q_up, g_kv_norm, g_kv_up,
          g_mla_o, g_dil_in, g_dil_o, g_rel_bias)
    ms = (m_norm_pre, m_norm_post, m_w_mod, m_b_mod, m_ffn_w_gate, m_ffn_w_up, m_ffn_w_down, m_mla_w_in, m_mla_q_norm,
          m_mla_w_q_up, m_mla_kv_norm, m_mla_w_kv_up, m_mla_w_o, m_dil_w_in, m_dil_w_o, m_rel_bias)
    vs = (v_norm_pre, v_norm_post, v_w_mod, v_b_mod, v_ffn_w_gate, v_ffn_w_up, v_ffn_w_down, v_mla_w_in, v_mla_q_norm,
          v_mla_w_q_up, v_mla_kv_norm, v_mla_w_kv_up, v_mla_w_o, v_dil_w_in, v_dil_w_o, v_rel_bias)
    names = ("norm_pre", "norm_post", "w_mod", "b_mod", "ffn_w_gate", "ffn_w_up", "ffn_w_down", "mla_w_in", "mla_q_norm",
             "mla_w_q_up", "mla_kv_norm", "mla_w_kv_up", "mla_w_o", "dil_w_in", "dil_w_o", "rel_bias")
    stepped = [early[n] if n in early else _adamw(w, g, m, v, "adamw") for n, w, g, m, v in zip(names, ws, gs, ms, vs)]
    deltas, new_m, new_v = zip(*stepped)
    return (loss, grad_x, *gs, *deltas, *new_m, *new_v)
```

```python
import math

import jax
import jax.numpy as jnp
from jax import lax
from jax.experimental import pallas as pl
from jax.experimental.pallas import tpu as pltpu

F32 = jnp.float32
BF16 = jnp.bfloat16
MESH = pl.DeviceIdType.MESH

NDEV = 8
OTHER_CHIPS = 3
D_MODEL = 1024
SEQ = 2048
D_FF = 2816
EPS = 1e-6
FFN_RES = 0.5

MLA_HEADS = 16
Q_LORA = 384
KV_LORA = 256
QK_NOPE = 64
QK_ROPE = 32
V_HEAD = 64
ROPE_THETA = 10000.0
HEAD_PAD = 128
LAT_PAD = Q_LORA + KV_LORA + HEAD_PAD
MLA_SCALE = (QK_NOPE + QK_ROPE) ** -0.5
MLA_QUERY_TILE = 256

DIL_GROUPS = ((128, 1), (512, 4), (2048, 16))
DIL_HEADS = 16
DIL_HEAD_DIM = 64
DIL_BLOCK = 128
DIL_PAIRS = DIL_HEADS // 2
DIL_SCALE = DIL_HEAD_DIM ** -0.5
DIL_GROUPED = 8
N_BUCKETS = 32
MAX_DISTANCE = 2048

ADAM_LR = 0.001
ADAM_B1 = 0.9
ADAM_B2 = 0.999
ADAM_EPS = 1e-08
ADAM_WD = 0.01
ADAM_STEP = 10

V7X_VMEM_BYTES = 64 * 2**20
VMEM_RESERVE = 10 * 2**20
TOKEN_TILE = 512


def _nbytes(shape, dtype):
    return math.prod(shape) * jnp.dtype(dtype).itemsize


def _params(semantics, blocks, extra=0):
    need = 2 * sum(_nbytes(s, d) for s, d in blocks) + extra + VMEM_RESERVE
    return pltpu.CompilerParams(dimension_semantics=semantics,
                                vmem_limit_bytes=int(min(need, V7X_VMEM_BYTES - VMEM_RESERVE)))


def _pcall(body, out_shape, **kw):
    call = pl.pallas_call(body, out_shape=jax.tree.map(lambda s: pltpu.HBM(s.shape, s.dtype), out_shape), **kw)
    return lambda *args: call(*[pltpu.with_memory_space_constraint(a, pltpu.HBM) for a in args])


def _dot_nn(a, b):
    return lax.dot_general(a, b, (((1,), (0,)), ((), ())), preferred_element_type=F32)


def _dot_nt(a, b):
    return lax.dot_general(a, b, (((1,), (1,)), ((), ())), preferred_element_type=F32)


def _dot_tn(a, b):
    return lax.dot_general(a, b, (((0,), (0,)), ((), ())), preferred_element_type=F32)


_DOTS = {"nn": _dot_nn, "nt": _dot_nt, "tn": _dot_tn}


def _rstd(v):
    return lax.rsqrt(jnp.mean(v * v, axis=-1, keepdims=True) + EPS)


def _rms_bwd(v, r, t):
    return r * t - v * (r * r * r) * jnp.mean(t * v, axis=-1, keepdims=True)


_TOKEN_SPEC = pl.BlockSpec((8, 128), lambda *_: (0, 0))


def _mm(pairs, mode, out_dtype, tm, tn, name, out_perm=1, after=None, b_rows=None):
    a0, b0 = pairs[0]
    m_dim = a0.shape[1] if mode == "tn" else a0.shape[0]
    n_dim = b0.shape[0] if mode == "nt" else b0.shape[1]
    tm, tn = min(tm, m_dim // out_perm), min(tn, n_dim)
    assert m_dim % tm == 0 and n_dim % tn == 0, (name, m_dim, n_dim, tm, tn)
    dot = _DOTS[mode]
    npairs = len(pairs)

    def body(*refs):
        acc = None
        for p in range(npairs):
            d = dot(refs[2 * p][...].astype(BF16), refs[2 * p + 1][...].astype(BF16))
            acc = d if acc is None else acc + d
        refs[-1][...] = acc.astype(out_dtype)

    in_specs, blocks, flat = [], [], []
    for n_pair, (a, b) in enumerate(pairs):
        if mode == "nn":
            k = a.shape[1]
            first_block = 0 if b_rows is None else b_rows[n_pair] // k
            sa, sb = ((tm, k), lambda i, j: (i, 0)), ((k, tn), lambda i, j, o=first_block: (o, j))
        elif mode == "nt":
            k = a.shape[1]
            sa, sb = ((tm, k), lambda i, j: (i, 0)), ((tn, k), lambda i, j: (j, 0))
        else:
            k = a.shape[0]
            sa, sb = ((k, tm), lambda i, j: (0, i)), ((k, tn), lambda i, j: (0, j))
        in_specs += [pl.BlockSpec(*sa), pl.BlockSpec(*sb)]
        blocks += [(sa[0], a.dtype), (sb[0], b.dtype)]
        flat += [a, b]
    if after is not None:
        in_specs.append(_TOKEN_SPEC)
        flat.append(after)
    if out_perm == 1:
        out_shape = (m_dim, n_dim)
        out_spec = pl.BlockSpec((tm, tn), lambda i, j: (i, j))
    else:
        rows = m_dim // out_perm
        assert tn == n_dim and rows % tm == 0, (name, rows, tm)
        nb = rows // tm
        out_shape = (rows, out_perm * n_dim)
        out_spec = pl.BlockSpec((tm, n_dim), lambda i, j: (i % nb, i // nb))
    blocks.append(((tm, tn), out_dtype))
    res = _pcall(
        body, out_shape=jax.ShapeDtypeStruct(out_shape, out_dtype), grid=(m_dim // tm, n_dim // tn),
        in_specs=in_specs, out_specs=out_spec, name=name,
        compiler_params=_params(("parallel", "parallel"), blocks, extra=2 * tm * tn * 4),
    )(*flat)
    return res.reshape(m_dim, n_dim)


def _prenorm_mm(x, pre_g, scale, shift, w, w_mode, out_dtype, tn, name, perm=1, w_rows=None):
    s_dim, d_dim = x.shape
    n_dim = w.shape[0] if w_mode == "nt" else w.shape[1]
    w_first = 0
    if w_rows is not None:
        w_first, n_dim = w_rows
    rows = s_dim // perm
    side = max(1, TOKEN_TILE // rows)
    tm = side * min(TOKEN_TILE, rows)
    nb = max(1, rows // tm)
    tn = min(tn, n_dim)
    assert n_dim % tn == 0 and w_first % tn == 0
    w_block0 = w_first // tn
    dot = _DOTS[w_mode]

    def body(x_ref, g_ref, sc_ref, sh_ref, w_ref, hn_ref, o_ref):
        @pl.when(pl.program_id(1) == 0)
        def _():
            xf = x_ref[...]
            if side > 1:
                xf = jnp.concatenate([xf[:, c * d_dim:(c + 1) * d_dim] for c in range(side)], axis=0)
            hn = (xf * _rstd(xf) * g_ref[...]) * (1.0 + sc_ref[...]) + sh_ref[...]
            hn_ref[...] = hn.astype(BF16)

        o_ref[...] = dot(hn_ref[...], w_ref[...]).astype(out_dtype)

    vec = pl.BlockSpec((1, d_dim), lambda i, j: (0, 0))
    w_block = (tn, d_dim) if w_mode == "nt" else (d_dim, tn)
    w_spec = pl.BlockSpec(w_block, (lambda i, j: (w_block0 + j, 0)) if w_mode == "nt" else (lambda i, j: (0, j)))
    hn, out = _pcall(
        body,
        out_shape=(jax.ShapeDtypeStruct((s_dim, d_dim), BF16), jax.ShapeDtypeStruct((s_dim, n_dim), out_dtype)),
        grid=(s_dim // tm, n_dim // tn),
        in_specs=[pl.BlockSpec((tm // side, side * d_dim), lambda i, j: (i % nb, i // nb)), vec, vec, vec, w_spec],
        out_specs=(pl.BlockSpec((tm, d_dim), lambda i, j: (i, 0)), pl.BlockSpec((tm, tn), lambda i, j: (i, j))),
        name=name,
        compiler_params=_params(("parallel", "arbitrary"),
                                [((tm, d_dim), F32), (w_block, BF16), ((tm, d_dim), BF16), ((tm, tn), out_dtype)],
                                extra=3 * tm * d_dim * 4 + tm * tn * 4),
    )(x.reshape(rows, perm * d_dim), pre_g, scale, shift, w)
    return hn, out


def _ffn_up(x, pre_g, scale, shift, wg_t, wu_t, name):
    s_dim, d_dim = x.shape
    f_dim = wg_t.shape[0]
    tm, tn = TOKEN_TILE, f_dim // 2

    def body(x_ref, g_ref, sc_ref, sh_ref, wg_ref, wu_ref, hn_ref, go_ref, uo_ref, a_ref):
        @pl.when(pl.program_id(1) == 0)
        def _():
            xf = x_ref[...]
            hn = (xf * _rstd(xf) * g_ref[...]) * (1.0 + sc_ref[...]) + sh_ref[...]
            hn_ref[...] = hn.astype(BF16)

        hn = hn_ref[...]
        g = _dot_nt(hn, wg_ref[...])
        u = _dot_nt(hn, wu_ref[...])
        go_ref[...] = g.astype(BF16)
        uo_ref[...] = u.astype(BF16)
        a_ref[...] = (g * jax.nn.sigmoid(g) * u).astype(BF16)

    vec = pl.BlockSpec((1, d_dim), lambda i, j: (0, 0))
    w_spec = pl.BlockSpec((tn, d_dim), lambda i, j: (j, 0))
    act = pl.BlockSpec((tm, tn), lambda i, j: (i, j))
    act_shape = jax.ShapeDtypeStruct((s_dim, f_dim), BF16)
    return _pcall(
        body,
        out_shape=(jax.ShapeDtypeStruct((s_dim, d_dim), BF16), act_shape, act_shape, act_shape),
        grid=(s_dim // tm, f_dim // tn),
        in_specs=[pl.BlockSpec((tm, d_dim), lambda i, j: (i, 0)), vec, vec, vec, w_spec, w_spec],
        out_specs=(pl.BlockSpec((tm, d_dim), lambda i, j: (i, 0)), act, act, act),
        name=name,
        compiler_params=_params(("parallel", "arbitrary"),
                                [((tm, d_dim), F32), ((tn, d_dim), BF16), ((tn, d_dim), BF16), ((tm, d_dim), BF16)]
                                + 3 * [((tm, tn), BF16)], extra=3 * tm * d_dim * 4 + 4 * tm * tn * 4),
    )(x, pre_g, scale, shift, wg_t, wu_t)


def _mm_post(a, w, x, post_g, gate, res_w, name):
    s_dim, k_dim = a.shape
    d_dim = w.shape[1]
    tm = TOKEN_TILE

    def body(a_ref, w_ref, x_ref, pg_ref, gt_ref, xo_ref, f_ref):
        f = _dot_nn(a_ref[...], w_ref[...])
        y = f * _rstd(f) * pg_ref[...]
        f_ref[...] = f
        xo_ref[...] = x_ref[...] + (res_w * gt_ref[...]) * y

    vec = pl.BlockSpec((1, d_dim), lambda i: (0, 0))
    row = pl.BlockSpec((tm, d_dim), lambda i: (i, 0))
    out = jax.ShapeDtypeStruct((s_dim, d_dim), F32)
    return _pcall(
        body, out_shape=(out, out), grid=(s_dim // tm,),
        in_specs=[pl.BlockSpec((tm, k_dim), lambda i: (i, 0)), pl.BlockSpec((k_dim, d_dim), lambda i: (0, 0)), row, vec, vec],
        out_specs=(row, row), name=name,
        compiler_params=_params(("parallel",), [((tm, k_dim), BF16), ((k_dim, d_dim), BF16)] + 3 * [((tm, d_dim), F32)],
                                extra=3 * tm * d_dim * 4),
    )(a, w, x, post_g, gate)


def _post_bwd(dx_out, f, post_g, gate, res_w, name):
    s_dim, d_dim = f.shape
    tm = TOKEN_TILE

    def body(dx_ref, f_ref, pg_ref, gt_ref, df_ref, dgate_ref, dpost_ref):
        @pl.when(pl.program_id(0) == 0)
        def _():
            dgate_ref[...] = jnp.zeros_like(dgate_ref)
            dpost_ref[...] = jnp.zeros_like(dpost_ref)

        dx, fv = dx_ref[...], f_ref[...]
        r = _rstd(fv)
        fr = fv * r
        dgate_ref[...] += res_w * jnp.sum(dx * (fr * pg_ref[...]), axis=0, keepdims=True)
        dy = (res_w * gt_ref[...]) * dx
        dpost_ref[...] += jnp.sum(dy * fr, axis=0, keepdims=True)
        df_ref[...] = _rms_bwd(fv, r, dy * pg_ref[...]).astype(BF16)

    vec = pl.BlockSpec((1, d_dim), lambda i: (0, 0))
    row = pl.BlockSpec((tm, d_dim), lambda i: (i, 0))
    vshape = jax.ShapeDtypeStruct((1, d_dim), F32)
    return _pcall(
        body, out_shape=(jax.ShapeDtypeStruct((s_dim, d_dim), BF16), vshape, vshape), grid=(s_dim // tm,),
        in_specs=[row, row, vec, vec], out_specs=(row, vec, vec), name=name,
        compiler_params=_params(("arbitrary",), 3 * [((tm, d_dim), F32)], extra=6 * tm * d_dim * 4),
    )(dx_out, f, post_g, gate)


def _prenorm_bwd(dx_out, dhns, x, pre_g, scale, name):
    s_dim, d_dim = x.shape
    tm = TOKEN_TILE
    n_in = len(dhns)

    def body(*refs):
        dx_ref, x_ref, pg_ref, sc_ref = refs[n_in + 0], refs[n_in + 1], refs[n_in + 2], refs[n_in + 3]
        dxo_ref, dsh_ref, dsc_ref, dpg_ref = refs[n_in + 4:]

        @pl.when(pl.program_id(0) == 0)
        def _():
            dsh_ref[...] = jnp.zeros_like(dsh_ref)
            dsc_ref[...] = jnp.zeros_like(dsc_ref)
            dpg_ref[...] = jnp.zeros_like(dpg_ref)

        dhn = refs[0][...]
        for k in range(1, n_in):
            dhn = dhn + refs[k][...]
        xv = x_ref[...]
        r = _rstd(xv)
        xr = xv * r
        dsh_ref[...] += jnp.sum(dhn, axis=0, keepdims=True)
        dsc_ref[...] += jnp.sum(dhn * (xr * pg_ref[...]), axis=0, keepdims=True)
        dn = dhn * (1.0 + sc_ref[...])
        dpg_ref[...] += jnp.sum(dn * xr, axis=0, keepdims=True)
        dxo_ref[...] = dx_ref[...] + _rms_bwd(xv, r, dn * pg_ref[...])

    vec = pl.BlockSpec((1, d_dim), lambda i: (0, 0))
    row = pl.BlockSpec((tm, d_dim), lambda i: (i, 0))
    vshape = jax.ShapeDtypeStruct((1, d_dim), F32)
    return _pcall(
        body, out_shape=(jax.ShapeDtypeStruct((s_dim, d_dim), F32), vshape, vshape, vshape), grid=(s_dim // tm,),
        in_specs=n_in * [row] + [row, row, vec, vec], out_specs=(row, vec, vec, vec), name=name,
        compiler_params=_params(("arbitrary",), (n_in + 3) * [((tm, d_dim), F32)], extra=6 * tm * d_dim * 4),
    )(*dhns, dx_out, x, pre_g, scale)


def _ffn_dgu(df, wd, g, u, name, after=None):
    s_dim, d_dim = df.shape
    f_dim = wd.shape[0]
    tm, tn = TOKEN_TILE, f_dim // 2

    def body(df_ref, wd_ref, g_ref, u_ref, *rest):
        dg_ref, du_ref = rest[-2:]
        da = _dot_nt(df_ref[...], wd_ref[...])
        gv, uv = g_ref[...].astype(F32), u_ref[...].astype(F32)
        sg = jax.nn.sigmoid(gv)
        du_ref[...] = (da * (gv * sg)).astype(BF16)
        dg_ref[...] = (da * uv * (sg * (1.0 + gv * (1.0 - sg)))).astype(BF16)

    act = pl.BlockSpec((tm, tn), lambda i, j: (i, j))
    act_shape = jax.ShapeDtypeStruct((s_dim, f_dim), BF16)
    token = [] if after is None else [after]
    return _pcall(
        body, out_shape=(act_shape, act_shape), grid=(s_dim // tm, f_dim // tn),
        in_specs=[pl.BlockSpec((tm, d_dim), lambda i, j: (i, 0)), pl.BlockSpec((tn, d_dim), lambda i, j: (j, 0)), act, act]
        + len(token) * [_TOKEN_SPEC],
        out_specs=(act, act), name=name,
        compiler_params=_params(("parallel", "parallel"), [((tm, d_dim), BF16), ((tn, d_dim), BF16)] + 4 * [((tm, tn), BF16)],
                                extra=6 * tm * tn * 4),
    )(df, wd, g, u, *token)


def _ffn_dw(dg, du, a, hn, df, name):
    s_dim, f_dim = dg.shape
    d_dim = hn.shape[1]
    tm = 256

    def body(dg_ref, du_ref, a_ref, hn_ref, df_ref, dwg_ref, dwu_ref, dwd_ref):
        dwg_ref[...] = _dot_tn(dg_ref[...], hn_ref[...]).astype(BF16)
        dwu_ref[...] = _dot_tn(du_ref[...], hn_ref[...]).astype(BF16)
        dwd_ref[...] = _dot_tn(a_ref[...], df_ref[...]).astype(BF16)

    col = pl.BlockSpec((s_dim, tm), lambda i: (0, i))
    full = pl.BlockSpec((s_dim, d_dim), lambda i: (0, 0), pipeline_mode=pl.Buffered(1))
    out = pl.BlockSpec((tm, d_dim), lambda i: (i, 0))
    shape = jax.ShapeDtypeStruct((f_dim, d_dim), BF16)
    need = 2 * s_dim * d_dim * 2 + 2 * 3 * (s_dim * tm * 2 + tm * d_dim * 2) + 3 * tm * d_dim * 4 + 3 * s_dim * tm * 2
    return _pcall(
        body, out_shape=(shape, shape, shape), grid=(f_dim // tm,), in_specs=[col, col, col, full, full],
        out_specs=(out, out, out), name=name,
        compiler_params=pltpu.CompilerParams(dimension_semantics=("parallel",),
                                             vmem_limit_bytes=int(min(need + VMEM_RESERVE, V7X_VMEM_BYTES - VMEM_RESERVE))),
    )(dg, du, a, hn, df)


def _mm_tn_shared(lhs, b, name):
    k_dim, m_dim = lhs[0].shape
    n_dim = b.shape[1]
    tm = 256
    n = len(lhs)

    def body(*refs):
        rhs = refs[n][...]
        for j in range(n):
            refs[n + 1 + j][...] = _dot_tn(refs[j][...], rhs).astype(BF16)

    col = pl.BlockSpec((k_dim, tm), lambda i: (0, i))
    out = pl.BlockSpec((tm, n_dim), lambda i: (i, 0))
    shape = jax.ShapeDtypeStruct((m_dim, n_dim), BF16)
    need = k_dim * n_dim * 2 + 2 * n * (k_dim * tm * 2 + tm * n_dim * 2) + n * tm * n_dim * 4 + n * k_dim * tm * 2
    return _pcall(
        body, out_shape=tuple(n * [shape]), grid=(m_dim // tm,),
        in_specs=n * [col] + [pl.BlockSpec((k_dim, n_dim), lambda i: (0, 0), pipeline_mode=pl.Buffered(1))],
        out_specs=tuple(n * [out]), name=name,
        compiler_params=pltpu.CompilerParams(dimension_semantics=("parallel",),
                                             vmem_limit_bytes=int(min(need + VMEM_RESERVE, V7X_VMEM_BYTES - VMEM_RESERVE))),
    )(*lhs, b)


def _ffn_fwd_fused(x, pre_g, scale, shift, post_g, gate, wg_t, wu_t, wd, name):
    s_dim, d_dim = x.shape
    f_dim = wd.shape[0]
    tm, chunks = 256, 2
    cw = f_dim // chunks

    def body(x_ref, prg_ref, sc_ref, sh_ref, pg_ref, gt_ref, wg_ref, wu_ref, wd_ref, hn_ref, go_ref, uo_ref, a_ref, xo_ref, f_ref):
        xf = x_ref[...]
        hn = ((xf * _rstd(xf) * prg_ref[...]) * (1.0 + sc_ref[...]) + sh_ref[...]).astype(BF16)
        hn_ref[...] = hn
        f = None
        ahead = (_dot_nt(hn, wg_ref[0:cw, :]), _dot_nt(hn, wu_ref[0:cw, :]))
        for c in range(chunks):
            g, u = ahead
            if c + 1 < chunks:
                nxt = slice((c + 1) * cw, (c + 2) * cw)
                ahead = (_dot_nt(hn, wg_ref[nxt, :]), _dot_nt(hn, wu_ref[nxt, :]))
            cols = slice(c * cw, (c + 1) * cw)
            go_ref[:, cols] = g.astype(BF16)
            uo_ref[:, cols] = u.astype(BF16)
            a = (g * jax.nn.sigmoid(g) * u).astype(BF16)
            a_ref[:, cols] = a
            part = _dot_nn(a, wd_ref[cols, :])
            f = part if f is None else f + part
        f_ref[...] = f
        xo_ref[...] = xf + (FFN_RES * gt_ref[...]) * (f * _rstd(f) * pg_ref[...])

    vec = pl.BlockSpec((1, d_dim), lambda i: (0, 0))
    row = pl.BlockSpec((tm, d_dim), lambda i: (i, 0))
    act = pl.BlockSpec((tm, f_dim), lambda i: (i, 0))
    weight = pl.BlockSpec((f_dim, d_dim), lambda i: (0, 0), pipeline_mode=pl.Buffered(1))
    act_shape = jax.ShapeDtypeStruct((s_dim, f_dim), BF16)
    res_shape = jax.ShapeDtypeStruct((s_dim, d_dim), F32)
    need = (3 * f_dim * d_dim * 2 + 2 * tm * d_dim * 4 + 2 * (tm * d_dim * 2 + 3 * tm * f_dim * 2 + 2 * tm * d_dim * 4)
            + 8 * tm * cw * 4 + 4 * tm * d_dim * 4)
    return _pcall(
        body, out_shape=(jax.ShapeDtypeStruct((s_dim, d_dim), BF16), act_shape, act_shape, act_shape, res_shape, res_shape),
        grid=(s_dim // tm,), in_specs=[row, vec, vec, vec, vec, vec, weight, weight, weight],
        out_specs=(row, act, act, act, row, row), name=name,
        compiler_params=pltpu.CompilerParams(dimension_semantics=("parallel",),
                                             vmem_limit_bytes=int(min(need + VMEM_RESERVE, V7X_VMEM_BYTES - VMEM_RESERVE))),
    )(x, pre_g, scale, shift, post_g, gate, wg_t, wu_t, wd)


def _ffn_bwd_fused(dx_out, saved, pre_g, post_g, scale, gate, wg_t, wu_t, wd, name):
    x, _, g, u, _, f = saved
    s_dim, d_dim = x.shape
    f_dim = wd.shape[0]
    tm, chunks = 256, 2
    cw = f_dim // chunks

    def body(dx_ref, f_ref, g_ref, u_ref, x_ref, pg_ref, gt_ref, prg_ref, sc_ref, wd_ref, wg_ref, wu_ref,
             df_ref, dg_ref, du_ref, dxo_ref, dgate_ref, dpost_ref, dsh_ref, dsc_ref, dpg_ref):
        @pl.when(pl.program_id(0) == 0)
        def _():
            for acc in (dgate_ref, dpost_ref, dsh_ref, dsc_ref, dpg_ref):
                acc[...] = jnp.zeros_like(acc)

        dx, fv = dx_ref[...], f_ref[...]
        r = _rstd(fv)
        fr = fv * r
        dgate_ref[...] += FFN_RES * jnp.sum(dx * (fr * pg_ref[...]), axis=0, keepdims=True)
        dy = (FFN_RES * gt_ref[...]) * dx
        dpost_ref[...] += jnp.sum(dy * fr, axis=0, keepdims=True)
        df = _rms_bwd(fv, r, dy * pg_ref[...]).astype(BF16)
        df_ref[...] = df
        dhn = None
        ahead = _dot_nt(df, wd_ref[0:cw, :])
        for c in range(chunks):
            da = ahead
            if c + 1 < chunks:
                ahead = _dot_nt(df, wd_ref[(c + 1) * cw:(c + 2) * cw, :])
            cols = slice(c * cw, (c + 1) * cw)
            gv, uv = g_ref[:, cols].astype(F32), u_ref[:, cols].astype(F32)
            sg = jax.nn.sigmoid(gv)
            du = (da * (gv * sg)).astype(BF16)
            dg = (da * uv * (sg * (1.0 + gv * (1.0 - sg)))).astype(BF16)
            dg_ref[:, cols] = dg
            du_ref[:, cols] = du
            part = _dot_nn(dg, wg_ref[cols, :]) + _dot_nn(du, wu_ref[cols, :])
            dhn = part if dhn is None else dhn + part
        xv = x_ref[...]
        rx = _rstd(xv)
        xr = xv * rx
        dsh_ref[...] += jnp.sum(dhn, axis=0, keepdims=True)
        dsc_ref[...] += jnp.sum(dhn * (xr * prg_ref[...]), axis=0, keepdims=True)
        dn = dhn * (1.0 + sc_ref[...])
        dpg_ref[...] += jnp.sum(dn * xr, axis=0, keepdims=True)
        dxo_ref[...] = dx + _rms_bwd(xv, rx, dn * prg_ref[...])

    vec = pl.BlockSpec((1, d_dim), lambda i: (0, 0))
    row = pl.BlockSpec((tm, d_dim), lambda i: (i, 0))
    act = pl.BlockSpec((tm, f_dim), lambda i: (i, 0))
    weight = pl.BlockSpec((f_dim, d_dim), lambda i: (0, 0), pipeline_mode=pl.Buffered(1))
    vshape = jax.ShapeDtypeStruct((1, d_dim), F32)
    act_shape = jax.ShapeDtypeStruct((s_dim, f_dim), BF16)
    need = (3 * f_dim * d_dim * 2 + 2 * (3 * tm * d_dim * 4 + 2 * tm * f_dim * 2) + 2 * (tm * d_dim * 2 + 2 * tm * f_dim * 2 + tm * d_dim * 4)
            + 6 * tm * cw * 4 + 6 * tm * d_dim * 4)
    return _pcall(
        body, out_shape=(jax.ShapeDtypeStruct((s_dim, d_dim), BF16), act_shape, act_shape, jax.ShapeDtypeStruct((s_dim, d_dim), F32),
                         vshape, vshape, vshape, vshape, vshape),
        grid=(s_dim // tm,), in_specs=[row, row, act, act, row, vec, vec, vec, vec, weight, weight, weight],
        out_specs=(row, act, act, row, vec, vec, vec, vec, vec), name=name,
        compiler_params=pltpu.CompilerParams(dimension_semantics=("arbitrary",),
                                             vmem_limit_bytes=int(min(need + VMEM_RESERVE, V7X_VMEM_BYTES - VMEM_RESERVE))),
    )(dx_out, f, g, u, x, post_g, gate, pre_g, scale, wd, wg_t, wu_t)


def _rope_tables(zero=0.0):
    half = QK_ROPE // 2
    freqs = ROPE_THETA ** (-jnp.arange(half, dtype=F32) / half)
    ang = (jnp.arange(SEQ, dtype=F32)[:, None] + zero) * freqs[None, :]
    cos, sin = jnp.cos(ang), jnp.sin(ang)
    ones = jnp.ones((SEQ, QK_NOPE), F32)
    zeros = jnp.zeros((SEQ, QK_NOPE), F32)
    pad1 = jnp.ones((SEQ, HEAD_PAD - QK_NOPE - QK_ROPE), F32)
    pad0 = jnp.zeros((SEQ, HEAD_PAD - QK_NOPE - QK_ROPE), F32)
    zh = jnp.zeros((SEQ, half), F32)
    c = jnp.concatenate([ones, cos, cos, pad1], axis=1)
    s1 = jnp.concatenate([zeros, -sin, zh, pad0], axis=1)
    s2 = jnp.concatenate([zeros, zh, sin, pad0], axis=1)
    return c, s1, s2


def _rope(v, c, s1, s2):
    half = QK_ROPE // 2
    return v * c + pltpu.roll(v, HEAD_PAD - half, 1) * s1 + pltpu.roll(v, half, 1) * s2


def _rope_t(dv, c, s1, s2):
    half = QK_ROPE // 2
    return dv * c + pltpu.roll(dv * s1, half, 1) + pltpu.roll(dv * s2, HEAD_PAD - half, 1)


def _mla_qkv(lat, q_norm, kv_norm, wq_t, wkv_t, rope, name):
    s_dim = lat.shape[0]
    width = MLA_HEADS * HEAD_PAD
    tm = 256

    def body(lat_ref, qg_ref, kg_ref, wq_ref, wkv_ref, c_ref, s1_ref, s2_ref, q_ref, k_ref, v_ref, qn_ref, kvn_ref):
        cq = lat_ref[:, :Q_LORA]
        ckv = lat_ref[:, Q_LORA:Q_LORA + KV_LORA]
        kr = lat_ref[:, Q_LORA + KV_LORA:]
        c, s1, s2 = c_ref[...], s1_ref[...], s2_ref[...]
        qn = (cq * _rstd(cq) * qg_ref[...]).astype(BF16)
        kvn = (ckv * _rstd(ckv) * kg_ref[...]).astype(BF16)
        qn_ref[...] = qn
        kvn_ref[...] = kvn
        q = _dot_nt(qn, wq_ref[...])
        kv = _dot_nt(kvn, wkv_ref[...])
        krr = _rope(kr, c, s1, s2)
        low = lax.broadcasted_iota(jnp.int32, (tm, HEAD_PAD), 1) < QK_NOPE
        for h in range(MLA_HEADS):
            sl = slice(h * HEAD_PAD, (h + 1) * HEAD_PAD)
            q_ref[:, sl] = _rope(q[:, sl], c, s1, s2).astype(BF16)
            kvh = kv[:, sl]
            k_ref[:, sl] = (jnp.where(low, kvh, 0.0) + krr).astype(BF16)
            v_ref[:, sl] = jnp.where(low, 0.0, kvh).astype(BF16)

    row = lambda n: pl.BlockSpec((tm, n), lambda i: (i, 0))
    full = lambda a: pl.BlockSpec(a.shape, lambda i: (0, 0))
    wide = jax.ShapeDtypeStruct((s_dim, width), BF16)
    return _pcall(
        body,
        out_shape=(wide, wide, wide, jax.ShapeDtypeStruct((s_dim, Q_LORA), BF16), jax.ShapeDtypeStruct((s_dim, KV_LORA), BF16)),
        grid=(s_dim // tm,),
        in_specs=[row(LAT_PAD), full(q_norm), full(kv_norm), full(wq_t), full(wkv_t), row(HEAD_PAD), row(HEAD_PAD), row(HEAD_PAD)],
        out_specs=(row(width), row(width), row(width), row(Q_LORA), row(KV_LORA)), name=name,
        compiler_params=_params(("parallel",), [((tm, LAT_PAD), F32), (wq_t.shape, BF16), (wkv_t.shape, BF16)]
                                + 3 * [((tm, width), BF16)], extra=4 * tm * width * 4),
    )(lat, q_norm, kv_norm, wq_t, wkv_t, *rope)


def _mla_scores(q, k_ref, t, tq):
    lo = t * tq
    own = slice(lo, lo + tq)
    scores = [(_dot_nt(q, k_ref[own, :]), own)]
    if t > 0:
        scores.append((_dot_nt(q, k_ref[0:lo, :]), slice(0, lo)))
    return scores


def _mla_softmax(scores):
    s_own = scores[0][0] * MLA_SCALE
    rows = lax.broadcasted_iota(jnp.int32, s_own.shape, 0)
    cols = lax.broadcasted_iota(jnp.int32, s_own.shape, 1)
    s_own = jnp.where(cols <= rows, s_own, -jnp.inf)
    mx = jnp.max(s_own, axis=-1, keepdims=True)
    if len(scores) == 1:
        e_own = jnp.exp(s_own - mx)
        return [(e_own * (1.0 / jnp.sum(e_own, axis=-1, keepdims=True)), scores[0][1])]
    s_pre = scores[1][0] * MLA_SCALE
    mx = jnp.maximum(mx, jnp.max(s_pre, axis=-1, keepdims=True))
    e_own, e_pre = jnp.exp(s_own - mx), jnp.exp(s_pre - mx)
    inv = 1.0 / (jnp.sum(e_own, axis=-1, keepdims=True) + jnp.sum(e_pre, axis=-1, keepdims=True))
    return [(e_pre * inv, scores[1][1]), (e_own * inv, scores[0][1])]


def _mla_attn_fwd(q, k, v, name):
    s_dim = q.shape[0]
    tq = MLA_QUERY_TILE

    def body(q_ref, k_ref, v_ref, o_ref):
        n_tiles = s_dim // tq
        tile_of = lambda t: slice(t * tq, (t + 1) * tq)
        def weighted_values(t, probs):
            o = None
            for p, keys in probs:
                part = _dot_nn(p, v_ref[keys, :])
                o = part if o is None else o + part
            o_ref[tile_of(t), :] = o.astype(BF16)

        scores = _mla_scores(q_ref[tile_of(0), :], k_ref, 0, tq)
        probs = None
        for t in range(n_tiles):
            ahead = _mla_scores(q_ref[tile_of(t + 1), :], k_ref, t + 1, tq) if t + 1 < n_tiles else None
            if probs is not None:
                weighted_values(t - 1, probs)
            probs = [(p.astype(BF16), keys) for p, keys in _mla_softmax(scores)]
            scores = ahead
        weighted_values(n_tiles - 1, probs)

    head = pl.BlockSpec((s_dim, HEAD_PAD), lambda h: (0, h))
    return _pcall(
        body, out_shape=jax.ShapeDtypeStruct(q.shape, BF16), grid=(MLA_HEADS,),
        in_specs=[head, head, head], out_specs=head, name=name,
        compiler_params=_params(("parallel",), 4 * [((s_dim, HEAD_PAD), BF16)], extra=4 * tq * s_dim * 4),
    )(q, k, v)


def _mla_attn_bwd(q, k, v, d_o, name):
    s_dim = q.shape[0]
    tq = MLA_QUERY_TILE

    def body(q_ref, k_ref, v_ref, do_ref, dq_ref, dk_ref, dv_ref):
        dk_ref[...] = jnp.zeros_like(dk_ref)
        dv_ref[...] = jnp.zeros_like(dv_ref)
        n_tiles = s_dim // tq
        tile_of = lambda t: slice(t * tq, (t + 1) * tq)

        def products(t):
            scores = _mla_scores(q_ref[tile_of(t), :], k_ref, t, tq)
            dot = do_ref[tile_of(t), :].astype(BF16)
            return scores, [_dot_nt(dot, v_ref[keys, :]) for _, keys in scores]

        def gradients_of_scores(scores, dps):
            probs = _mla_softmax(scores)
            dp_of = {(keys.start, keys.stop): dp for (_, keys), dp in zip(scores, dps)}
            terms = [(p, keys, dp_of[keys.start, keys.stop]) for p, keys in probs]
            row = None
            for p, _, dp in terms:
                part = jnp.sum(p * dp, axis=-1, keepdims=True)
                row = part if row is None else row + part
            return [((p * (dp - row) * MLA_SCALE).astype(BF16), p.astype(BF16), keys) for p, keys, dp in terms]

        def accumulate(t, terms):
            qt = q_ref[tile_of(t), :]
            dot = do_ref[tile_of(t), :].astype(BF16)
            dq = None
            for dsb, pb, keys in terms:
                part = _dot_nn(dsb, k_ref[keys, :])
                dq = part if dq is None else dq + part
                dk_ref[keys, :] += _dot_tn(dsb, qt)
                dv_ref[keys, :] += _dot_tn(pb, dot)
            dq_ref[tile_of(t), :] = dq

        ready = products(0)
        terms = None
        for t in range(n_tiles):
            ahead = products(t + 1) if t + 1 < n_tiles else None
            if terms is not None:
                accumulate(t - 1, terms)
            terms = gradients_of_scores(*ready)
            ready = ahead
        accumulate(n_tiles - 1, terms)

    head = pl.BlockSpec((s_dim, HEAD_PAD), lambda h: (0, h))
    out = jax.ShapeDtypeStruct(q.shape, F32)
    return _pcall(
        body, out_shape=(out, out, out), grid=(MLA_HEADS,),
        in_specs=[head, head, head, head], out_specs=(head, head, head), name=name,
        compiler_params=_params(("parallel",), 3 * [((s_dim, HEAD_PAD), BF16)] + 4 * [((s_dim, HEAD_PAD), F32)],
                                extra=6 * tq * s_dim * 4),
    )(q, k, v, d_o)


def _mla_qkv_bwd(dq, dk, dv, lat, q_norm, kv_norm, wq_t, wkv_t, rope, name):
    s_dim = lat.shape[0]
    width = MLA_HEADS * HEAD_PAD
    tm = 256

    def body(dq_ref, dk_ref, dv_ref, lat_ref, qg_ref, kg_ref, wq_ref, wkv_ref, c_ref, s1_ref, s2_ref,
             dqp_ref, dkv_ref, dlat_ref, dqg_ref, dkg_ref):
        @pl.when(pl.program_id(0) == 0)
        def _():
            dqg_ref[...] = jnp.zeros_like(dqg_ref)
            dkg_ref[...] = jnp.zeros_like(dkg_ref)

        c, s1, s2 = c_ref[...], s1_ref[...], s2_ref[...]
        lane = lax.broadcasted_iota(jnp.int32, (tm, HEAD_PAD), 1)
        low = lane < QK_NOPE
        rot = (lane >= QK_NOPE) & (lane < QK_NOPE + QK_ROPE)
        dkrr = jnp.zeros((tm, HEAD_PAD), F32)
        for h in range(MLA_HEADS):
            sl = slice(h * HEAD_PAD, (h + 1) * HEAD_PAD)
            dqp_ref[:, sl] = _rope_t(dq_ref[:, sl], c, s1, s2).astype(BF16)
            dkh = dk_ref[:, sl]
            dkv_ref[:, sl] = jnp.where(low, dkh, dv_ref[:, sl]).astype(BF16)
            dkrr = dkrr + jnp.where(rot, dkh, 0.0)
        dqn = _dot_nn(dqp_ref[...], wq_ref[...])
        dkvn = _dot_nn(dkv_ref[...], wkv_ref[...])
        cq = lat_ref[:, :Q_LORA]
        ckv = lat_ref[:, Q_LORA:Q_LORA + KV_LORA]
        rq, rkv = _rstd(cq), _rstd(ckv)
        dqg_ref[...] += jnp.sum(dqn * cq * rq, axis=0, keepdims=True)
        dkg_ref[...] += jnp.sum(dkvn * ckv * rkv, axis=0, keepdims=True)
        dlat_ref[:, :Q_LORA] = _rms_bwd(cq, rq, dqn * qg_ref[...])
        dlat_ref[:, Q_LORA:Q_LORA + KV_LORA] = _rms_bwd(ckv, rkv, dkvn * kg_ref[...])
        dlat_ref[:, Q_LORA + KV_LORA:] = _rope_t(dkrr, c, s1, s2)

    row = lambda n: pl.BlockSpec((tm, n), lambda i: (i, 0))
    full = lambda a: pl.BlockSpec(a.shape, lambda i: (0, 0))
    wide = jax.ShapeDtypeStruct((s_dim, width), BF16)
    return _pcall(
        body,
        out_shape=(wide, wide, jax.ShapeDtypeStruct((s_dim, LAT_PAD), F32),
                   jax.ShapeDtypeStruct(q_norm.shape, F32), jax.ShapeDtypeStruct(kv_norm.shape, F32)),
        grid=(s_dim // tm,),
        in_specs=[row(width), row(width), row(width), row(LAT_PAD), full(q_norm), full(kv_norm), full(wq_t), full(wkv_t),
                  row(HEAD_PAD), row(HEAD_PAD), row(HEAD_PAD)],
        out_specs=(row(width), row(width), row(LAT_PAD), full(q_norm), full(kv_norm)), name=name,
        compiler_params=_params(("arbitrary",), 3 * [((tm, width), F32)] + [((tm, LAT_PAD), F32), (wq_t.shape, BF16),
                                                                           (wkv_t.shape, BF16)] + 2 * [((tm, width), BF16)],
                                extra=2 * tm * width * 4),
    )(dq, dk, dv, lat, q_norm, kv_norm, wq_t, wkv_t, *rope)


def _t5_bucket(dist):
    max_exact = N_BUCKETS // 2
    d = jnp.maximum(dist, 1).astype(F32)
    large = max_exact + (jnp.log(d / max_exact) / math.log(MAX_DISTANCE / max_exact)
                         * (N_BUCKETS - max_exact)).astype(jnp.int32)
    large = jnp.minimum(large, N_BUCKETS - 1)
    return jnp.where(dist < max_exact, dist, large)


def _dil_buckets(dilation):
    iq = jnp.arange(DIL_BLOCK)[:, None]
    ik = jnp.arange(2 * DIL_BLOCK)[None, :]
    return _t5_bucket(jnp.maximum(DIL_BLOCK + iq - ik, 0) * dilation)


def _dil_logits(qh, kb, bias_h, first, span):
    if first:
        s = _dot_nt(qh, kb) * DIL_SCALE + bias_h[:, DIL_BLOCK:]
        rel = lax.broadcasted_iota(jnp.int32, s.shape, 0) - lax.broadcasted_iota(jnp.int32, s.shape, 1)
    else:
        s = _dot_nt(qh, kb) * DIL_SCALE + bias_h
        rel = DIL_BLOCK + lax.broadcasted_iota(jnp.int32, s.shape, 0) - lax.broadcasted_iota(jnp.int32, s.shape, 1)
    return jnp.where((rel >= 0) & (rel <= span), s, -jnp.inf)


def _dil_blocks(s_dim, dilation):
    rows = s_dim // dilation
    for r in range(dilation):
        for n in range(rows // DIL_BLOCK):
            lo = r * rows + n * DIL_BLOCK
            keys = slice(lo, lo + DIL_BLOCK) if n == 0 else slice(lo - DIL_BLOCK, lo + DIL_BLOCK)
            start = r + n * DIL_BLOCK * dilation
            tokens = slice(start, start + DIL_BLOCK) if dilation == 1 else pl.ds(start, DIL_BLOCK, stride=dilation)
            yield n == 0, slice(lo, lo + DIL_BLOCK), keys, tokens


def _dil_views(s_dim):
    col = lambda which: pl.BlockSpec((s_dim, HEAD_PAD), lambda p: (0, which * DIL_PAIRS + p))
    nat = pl.BlockSpec((s_dim, HEAD_PAD), lambda p: (0, p))
    bias = pl.BlockSpec((2, DIL_BLOCK, 2 * DIL_BLOCK), lambda p: (p, 0, 0))
    return col, nat, bias


def _dil_attn_fwd(qkv, bias, dilation, span, name):
    s_dim = qkv.shape[0]
    d_dim = DIL_HEADS * DIL_HEAD_DIM
    col, nat, bias_spec = _dil_views(s_dim)

    def body(q_ref, k_ref, v_ref, b_ref, o_ref, l_ref):
        lane = lax.broadcasted_iota(jnp.int32, (DIL_BLOCK, HEAD_PAD), 1)
        klane = lax.broadcasted_iota(jnp.int32, (2 * DIL_BLOCK, HEAD_PAD), 1)
        blocks = list(_dil_blocks(s_dim, dilation))
        for g0 in range(0, len(blocks), DIL_GROUPED):
            group = blocks[g0:g0 + DIL_GROUPED]
            logits = [_dil_logits(jnp.where((lane < DIL_HEAD_DIM) == (h == 0), q_ref[blk, :], 0), k_ref[keys, :], b_ref[h],
                                  first, span) for first, blk, keys, _ in group for h in range(2)]
            soft = []
            for lg in logits:
                mx = jnp.max(lg, axis=-1, keepdims=True)
                e = jnp.exp(lg - mx)
                tot = jnp.sum(e, axis=-1, keepdims=True)
                soft.append(((e * (1.0 / tot)).astype(BF16), mx + jnp.log(tot)))
            for i, (_, _, keys, tokens) in enumerate(group):
                vb = v_ref[keys, :]
                o_acc = jnp.zeros((DIL_BLOCK, HEAD_PAD), F32)
                lse_acc = jnp.zeros((DIL_BLOCK, HEAD_PAD), F32)
                for h in range(2):
                    p, lse = soft[2 * i + h]
                    kmine = (klane[:vb.shape[0]] < DIL_HEAD_DIM) == (h == 0)
                    o_acc = o_acc + _dot_nn(p, jnp.where(kmine, vb, 0))
                    lse_acc = jnp.where((lane < DIL_HEAD_DIM) == (h == 0), lse, lse_acc)
                o_ref[tokens, :] = o_acc
                l_ref[tokens, :] = lse_acc

    out = jax.ShapeDtypeStruct((s_dim, d_dim), F32)
    return _pcall(
        body, out_shape=(out, out), grid=(DIL_PAIRS,),
        in_specs=[col(0), col(1), col(2), bias_spec], out_specs=(nat, nat), name=name,
        compiler_params=_params(("parallel",), 3 * [((s_dim, HEAD_PAD), BF16)] + 2 * [((s_dim, HEAD_PAD), F32)]
                                + [((2, DIL_BLOCK, 2 * DIL_BLOCK), F32)], extra=2**21),
    )(qkv, qkv, qkv, bias)


def _dil_mix(lses, outs, name):
    s_dim, d_dim = outs[0].shape
    tm = TOKEN_TILE
    ng = len(outs)

    def body(*refs):
        ls = [refs[g][...] for g in range(ng)]
        mx = ls[0]
        for g in range(1, ng):
            mx = jnp.maximum(mx, ls[g])
        es = [jnp.exp(l - mx) for l in ls]
        tot = es[0]
        for g in range(1, ng):
            tot = tot + es[g]
        o = None
        for g in range(ng):
            al = es[g] / tot
            refs[2 * ng + g][...] = al
            t = al * refs[ng + g][...]
            o = t if o is None else o + t
        refs[3 * ng][...] = o
        refs[3 * ng + 1][...] = o.astype(BF16)

    row = pl.BlockSpec((tm, d_dim), lambda i: (i, 0))
    f = jax.ShapeDtypeStruct((s_dim, d_dim), F32)
    res = _pcall(
        body, out_shape=tuple(ng * [f] + [f, jax.ShapeDtypeStruct((s_dim, d_dim), BF16)]), grid=(s_dim // tm,),
        in_specs=2 * ng * [row], out_specs=tuple((ng + 2) * [row]), name=name,
        compiler_params=_params(("parallel",), (3 * ng + 2) * [((tm, d_dim), F32)], extra=4 * tm * d_dim * 4),
    )(*lses, *outs)
    return res[:ng], res[ng], res[ng + 1]


def _dil_attn_bwd(qkv, bias, d_o, o_mix, alpha, lse, dilation, span, name):
    s_dim = qkv.shape[0]
    d_dim = DIL_HEADS * DIL_HEAD_DIM
    col, nat, bias_spec = _dil_views(s_dim)

    def body(q_ref, k_ref, v_ref, b_ref, do_ref, om_ref, al_ref, l_ref, dq_ref, dk_ref, dv_ref, db_ref, dk_acc, dv_acc):
        db_ref[...] = jnp.zeros_like(db_ref)
        dk_acc[...] = jnp.zeros_like(dk_acc)
        dv_acc[...] = jnp.zeros_like(dv_acc)
        lane = lax.broadcasted_iota(jnp.int32, (DIL_BLOCK, HEAD_PAD), 1)
        klane = lax.broadcasted_iota(jnp.int32, (2 * DIL_BLOCK, HEAD_PAD), 1)
        blocks = list(_dil_blocks(s_dim, dilation))
        heads = [(lane < DIL_HEAD_DIM) == (h == 0) for h in range(2)]
        for g0 in range(0, len(blocks), DIL_GROUPED):
            group = blocks[g0:g0 + DIL_GROUPED]
            staged = []
            for first, blk, kv_rows, tokens in group:
                qb, kb, vb = q_ref[blk, :], k_ref[kv_rows, :], v_ref[kv_rows, :]
                dog = al_ref[tokens, :] * do_ref[tokens, :]
                row_term = dog * om_ref[tokens, :]
                lse_b = l_ref[tokens, :]
                for h in range(2):
                    qh = jnp.where(heads[h], qb, 0)
                    dogh = jnp.where(heads[h], dog, 0.0).astype(BF16)
                    staged.append((_dil_logits(qh, kb, b_ref[h], first, span), _dot_nt(dogh, vb), qh, dogh,
                                   jnp.max(jnp.where(heads[h], lse_b, -jnp.inf), axis=-1, keepdims=True),
                                   jnp.sum(jnp.where(heads[h], row_term, 0.0), axis=-1, keepdims=True)))
            grads = []
            for i, (logits, dp, qh, dogh, lse_h, row) in enumerate(staged):
                p = jnp.exp(logits - lse_h)
                ds = p * (dp - row)
                if group[i // 2][0]:
                    db_ref[i % 2, :, DIL_BLOCK:] += ds
                else:
                    db_ref[i % 2] += ds
                grads.append(((ds * DIL_SCALE).astype(BF16), p.astype(BF16), qh, dogh))
            for i, (_, blk, kv_rows, _) in enumerate(group):
                kb = k_ref[kv_rows, :]
                dq_acc = jnp.zeros((DIL_BLOCK, HEAD_PAD), F32)
                dk_blk = jnp.zeros((kb.shape[0], HEAD_PAD), F32)
                dv_blk = jnp.zeros((kb.shape[0], HEAD_PAD), F32)
                for h in range(2):
                    dsb, pb, qh, dogh = grads[2 * i + h]
                    kmine = (klane[:kb.shape[0]] < DIL_HEAD_DIM) == (h == 0)
                    dq_acc = dq_acc + _dot_nn(dsb, jnp.where(kmine, kb, 0))
                    dk_blk = dk_blk + _dot_tn(dsb, qh)
                    dv_blk = dv_blk + _dot_tn(pb, dogh)
                dq_ref[blk, :] = dq_acc.astype(BF16)
                dk_acc[kv_rows, :] += dk_blk
                dv_acc[kv_rows, :] += dv_blk
        dk_ref[...] = dk_acc[...].astype(BF16)
        dv_ref[...] = dv_acc[...].astype(BF16)

    grad = jax.ShapeDtypeStruct((s_dim, d_dim), BF16)
    return _pcall(
        body, out_shape=(grad, grad, grad, jax.ShapeDtypeStruct(bias.shape, F32)), grid=(DIL_PAIRS,),
        in_specs=[col(0), col(1), col(2), bias_spec, nat, nat, nat, nat],
        out_specs=(nat, nat, nat, bias_spec), name=name,
        scratch_shapes=[pltpu.VMEM((s_dim, HEAD_PAD), F32), pltpu.VMEM((s_dim, HEAD_PAD), F32)],
        compiler_params=_params(("parallel",), 6 * [((s_dim, HEAD_PAD), BF16)] + 4 * [((s_dim, HEAD_PAD), F32)]
                                + 2 * [((2, DIL_BLOCK, 2 * DIL_BLOCK), F32)], extra=2 * s_dim * HEAD_PAD * 4 + 2**21),
    )(qkv, qkv, qkv, bias, d_o, o_mix, alpha, lse)


def _bias_reduce(dbias, buckets, name):
    n_heads = dbias.shape[0]

    def body(db_ref, bk_ref, o_ref):
        ds, bk = db_ref[0], bk_ref[0]
        lane = lax.broadcasted_iota(jnp.int32, (8, HEAD_PAD), 1)
        acc = jnp.zeros((8, HEAD_PAD), F32)
        for b in range(N_BUCKETS):
            acc = jnp.where(lane == b, jnp.sum(jnp.where(bk == b, ds, 0.0)), acc)
        o_ref[0] = acc

    blk = (1, DIL_BLOCK, 2 * DIL_BLOCK)
    return _pcall(
        body, out_shape=jax.ShapeDtypeStruct((n_heads, 8, HEAD_PAD), F32), grid=(n_heads,),
        in_specs=[pl.BlockSpec(blk, lambda h: (h, 0, 0)), pl.BlockSpec(blk, lambda h: (h // DIL_HEADS, 0, 0))],
        out_specs=pl.BlockSpec((1, 8, HEAD_PAD), lambda h: (h, 0, 0)), name=name,
        compiler_params=_params(("parallel",), [(blk, F32), (blk, jnp.int32)], extra=2**20),
    )(dbias, buckets)


def _loss_grad(y, target, name):
    s_dim, d_dim = y.shape
    tm = TOKEN_TILE

    def body(y_ref, t_ref, dy_ref, l_ref):
        @pl.when(pl.program_id(0) == 0)
        def _():
            l_ref[...] = jnp.zeros_like(l_ref)

        err = y_ref[...] - t_ref[...]
        dy_ref[...] = err / d_dim
        sq = (err * err).reshape(tm // 8, 8, d_dim)
        l_ref[...] += 0.5 * jnp.sum(sq, axis=0) / d_dim

    row = pl.BlockSpec((tm, d_dim), lambda i: (i, 0))
    acc = pl.BlockSpec((8, d_dim), lambda i: (0, 0))
    return _pcall(
        body, out_shape=(jax.ShapeDtypeStruct((s_dim, d_dim), F32), jax.ShapeDtypeStruct((8, d_dim), F32)),
        grid=(s_dim // tm,), in_specs=[row, row], out_specs=(row, acc), name=name,
        compiler_params=_params(("arbitrary",), 3 * [((tm, d_dim), F32)], extra=2 * tm * d_dim * 4),
    )(y, target)


def _mod_fwd(c_all, w_mod, b_loc, name):
    depth, d_dim, n = w_mod.shape
    nb = c_all.shape[0]

    def body(c_ref, w_ref, b_ref, o_ref, s_ref):
        cv = c_ref[...]
        sc = cv * jax.nn.sigmoid(cv)
        s_ref[...] = sc
        o_ref[0] = _dot_nn(sc.astype(BF16), w_ref[0].astype(BF16)) + b_ref[0]

    return _pcall(
        body, out_shape=(jax.ShapeDtypeStruct((depth, nb, n), F32), jax.ShapeDtypeStruct((nb, d_dim), F32)), grid=(depth,),
        in_specs=[pl.BlockSpec((nb, d_dim), lambda i: (0, 0)), pl.BlockSpec((1, d_dim, n), lambda i: (i, 0, 0)),
                  pl.BlockSpec((1, 1, n), lambda i: (i, 0, 0))],
        out_specs=(pl.BlockSpec((1, nb, n), lambda i: (i, 0, 0)), pl.BlockSpec((nb, d_dim), lambda i: (0, 0))), name=name,
        compiler_params=_params(("arbitrary",), [((1, d_dim, n), F32)], extra=d_dim * n * 2 + 2**20),
    )(c_all, w_mod, b_loc.reshape(depth, 1, n))


def _sum_parts(parts, name, transpose=False):
    _, rows, cols = parts.shape
    unit = 128 if transpose else 16
    budget = (7 if transpose else 3) * 2**20
    fits = [t for t in range(unit, rows // 2 + 1, unit) if rows % t == 0 and NDEV * t * cols * parts.dtype.itemsize <= budget]
    tr = max(fits) if fits else rows

    def body(p_ref, o_ref):
        acc = p_ref[0].astype(F32)
        for k in range(1, NDEV):
            acc = acc + p_ref[k].astype(F32)
        o_ref[...] = acc.T if transpose else acc

    out_shape, out_block = ((cols, rows), (cols, tr)) if transpose else ((rows, cols), (tr, cols))
    return _pcall(
        body, out_shape=jax.ShapeDtypeStruct(out_shape, F32), grid=(rows // tr,),
        in_specs=[pl.BlockSpec((NDEV, tr, cols), lambda i: (0, i, 0))],
        out_specs=pl.BlockSpec(out_block, (lambda i: (0, i)) if transpose else (lambda i: (i, 0))),
        name=name, compiler_params=_params(("parallel",), [((NDEV, tr, cols), parts.dtype), (out_block, F32)], extra=2**22),
    )(parts)


def _adamw(w, g, m, v, name):
    shape = w.shape
    cols = shape[-1]
    rows = math.prod(shape[:-1])
    tr = rows
    for cand in (2048, 1024, 512, 256, 128, 64, 32, 16, 8):
        if rows % cand == 0 and rows > cand and cand * cols * 4 <= 2**21:
            tr = cand
            break

    def body(w_ref, g_ref, m_ref, v_ref, d_ref, mo_ref, vo_ref):
        gv = g_ref[...]
        mn = ADAM_B1 * m_ref[...] + (1.0 - ADAM_B1) * gv
        vn = ADAM_B2 * v_ref[...] + (1.0 - ADAM_B2) * (gv * gv)
        m_hat = mn / (1.0 - ADAM_B1 ** ADAM_STEP)
        v_hat = vn / (1.0 - ADAM_B2 ** ADAM_STEP)
        d_ref[...] = -ADAM_LR * (m_hat / (jnp.sqrt(v_hat) + ADAM_EPS) + ADAM_WD * w_ref[...])
        mo_ref[...] = mn
        vo_ref[...] = vn

    blk = pl.BlockSpec((tr, cols), lambda i: (i, 0))
    out = jax.ShapeDtypeStruct((rows, cols), F32)
    res = _pcall(
        body, out_shape=(out, out, out), grid=(rows // tr,), in_specs=4 * [blk], out_specs=(blk, blk, blk), name=name,
        compiler_params=_params(("parallel",), 7 * [((tr, cols), F32)], extra=4 * tr * cols * 4),
    )(*(a.reshape(rows, cols) for a in (w, g, m, v)))
    return tuple(r.reshape(shape) for r in res)


def _peers():
    x, y, c = lax.axis_index("x"), lax.axis_index("y"), lax.axis_index("c")
    flip = lambda v, f: 1 - v if f else v
    peers = []
    for f in range(1, NDEV):
        px, py, pc = flip(x, f & 4), flip(y, f & 2), flip(c, f & 1)
        peers.append(((px, py, pc), 4 * px + 2 * py + pc))
    return (x, y, c), 4 * x + 2 * y + c, peers


def _places():
    x, y, c = lax.axis_index("x"), lax.axis_index("y"), lax.axis_index("c")
    place = lambda px, py, pc: ((px, py, pc), 4 * px + 2 * py + pc)
    return place(x, y, c), place(x, y, 1 - c), [place(1 - x, y, c), place(x, 1 - y, c), place(1 - x, 1 - y, c)]


def _exchange(arrs, gather, name):
    n = len(arrs)
    hbm = pl.BlockSpec(memory_space=pltpu.HBM)
    if gather:
        out_shape = [jax.ShapeDtypeStruct((NDEV * a.shape[0], a.shape[1]), a.dtype) for a in arrs]
    else:
        out_shape = [jax.ShapeDtypeStruct((NDEV, a.shape[0] // NDEV, a.shape[1]), a.dtype) for a in arrs]

    def body(*refs):
        ins, outs = refs[:n], refs[n:2 * n]
        send_sems, recv_sems, local_sems = refs[2 * n:]
        me_pos, me, peers = _peers()
        local = []
        for k in range(n):
            rows = arrs[k].shape[0] if gather else arrs[k].shape[0] // NDEV
            if gather:
                src_of = lambda idx: ins[k]
                dst_of = lambda idx: outs[k].at[pl.ds(me * rows, rows)]
                mine = (ins[k], outs[k].at[pl.ds(me * rows, rows)])
            else:
                src_of = lambda idx: ins[k].at[pl.ds(idx * rows, rows)]
                dst_of = lambda idx: outs[k].at[me]
                mine = (ins[k].at[pl.ds(me * rows, rows)], outs[k].at[me])
            cp = pltpu.make_async_copy(mine[0], mine[1], local_sems.at[k])
            cp.start()
            local.append(cp)
            for pos, idx in peers:
                pltpu.make_async_remote_copy(src_ref=src_of(idx), dst_ref=dst_of(idx), send_sem=send_sems.at[k],
                                             recv_sem=recv_sems.at[k], device_id=pos, device_id_type=MESH).start()
        for k in range(n):
            rows = arrs[k].shape[0] if gather else arrs[k].shape[0] // NDEV
            sent = ins[k].at[pl.ds(0, (NDEV - 1) * rows)] if not gather else outs[k].at[pl.ds(0, (NDEV - 1) * rows)]
            got = outs[k].at[pl.ds(0, (NDEV - 1) * rows)] if gather else outs[k].at[pl.ds(0, NDEV - 1)]
            pltpu.make_async_remote_copy(src_ref=sent, dst_ref=sent, send_sem=send_sems.at[k], recv_sem=recv_sems.at[k],
                                         device_id=me_pos, device_id_type=MESH).wait_send()
            pltpu.make_async_remote_copy(src_ref=got, dst_ref=got, send_sem=send_sems.at[k], recv_sem=recv_sems.at[k],
                                         device_id=me_pos, device_id_type=MESH).wait_recv()
            local[k].wait()

    return pl.pallas_call(
        body, out_shape=out_shape, in_specs=n * [hbm], out_specs=n * [hbm], name=name,
        scratch_shapes=[pltpu.SemaphoreType.DMA((n,)), pltpu.SemaphoreType.DMA((n,)), pltpu.SemaphoreType.DMA((n,))],
        compiler_params=pltpu.CompilerParams(has_side_effects=True),
    )(*arrs)


_HBM = pl.BlockSpec(memory_space=pltpu.HBM)
_SEM = pl.BlockSpec(memory_space=pltpu.SEMAPHORE)
_DATAFLOW = pltpu.SideEffectType.DATAFLOW_SIDE_EFFECTING


def _split_start(srcs, groups, gather, name):
    n = len(srcs)
    if gather:
        lands = [lax.empty((NDEV * a.shape[0], a.shape[1]), a.dtype) for a in srcs]
    else:
        lands = [lax.empty((NDEV, a.shape[0] // NDEV, a.shape[1]), a.dtype) for a in srcs]
    n_sem = 3 * len(groups)

    def body(*refs):
        src_refs, land_refs = refs[:n], refs[n:2 * n]
        sems = refs[2 * n:2 * n + n_sem]
        token = refs[-1]
        (_, my), sibling, chips = _places()
        _, _, peers = _peers()
        targets = [sibling] + chips if gather else peers
        for g, members in enumerate(groups):
            for j, k in enumerate(members):
                _own_copy(src_refs[k], land_refs[k], sems[3 * g + 2].at[j], my, gather).start()
        for g, members in enumerate(groups):
            for j, k in enumerate(members):
                rows = srcs[k].shape[0] if gather else srcs[k].shape[0] // NDEV
                for pos, idx in targets:
                    src = src_refs[k] if gather else src_refs[k].at[pl.ds(idx * rows, rows)]
                    dst = land_refs[k].at[pl.ds(my * rows, rows)] if gather else land_refs[k].at[my]
                    pltpu.make_async_remote_copy(src_ref=src, dst_ref=dst, send_sem=sems[3 * g].at[j],
                                                 recv_sem=sems[3 * g + 1].at[j], device_id=pos, device_id_type=MESH).start()
        token[...] = jnp.zeros_like(token)

    out_shape = []
    for members in groups:
        out_shape += 3 * [pltpu.SemaphoreType.DMA((len(members),))]
    out_shape += [pltpu.HBM(a.shape, a.dtype) for a in srcs] + [pltpu.HBM(a.shape, a.dtype) for a in lands]
    out_shape.append(jax.ShapeDtypeStruct((8, 128), F32))
    res = pl.pallas_call(
        body, name=name, out_shape=tuple(out_shape), in_specs=2 * n * [_HBM],
        out_specs=tuple(n_sem * [_SEM] + 2 * n * [_HBM] + [pl.BlockSpec(memory_space=pltpu.VMEM)]),
        input_output_aliases={i: n_sem + i for i in range(2 * n)},
        compiler_params=pltpu.CompilerParams(has_side_effects=_DATAFLOW),
    )(*[pltpu.with_memory_space_constraint(a, pltpu.HBM) for a in list(srcs) + lands])
    sems = [tuple(res[3 * g:3 * g + 3]) for g in range(len(groups))]
    return sems, list(res[n_sem:n_sem + n]), list(res[n_sem + n:n_sem + 2 * n]), res[-1]


def _own_copy(src_ref, land_ref, sem, my, gather):
    if gather:
        rows = src_ref.shape[0]
        return pltpu.make_async_copy(src_ref, land_ref.at[pl.ds(my * rows, rows)], sem)
    rows = src_ref.shape[0] // NDEV
    return pltpu.make_async_copy(src_ref.at[pl.ds(my * rows, rows)], land_ref.at[my], sem)


def _wait_all(land_ref, blocks_per_dev, copies, send_sem, recv_sem, me_pos):
    part = land_ref.at[pl.ds(0, copies * blocks_per_dev)]
    pltpu.make_async_remote_copy(src_ref=part, dst_ref=part, send_sem=send_sem, recv_sem=recv_sem,
                                 device_id=me_pos, device_id_type=MESH).wait()


def _gather_forward(sems, srcs, lands, after, name):
    n = len(srcs)

    def body(*refs):
        land_refs = refs[n:2 * n]
        send_a, recv_a = refs[2 * n], refs[2 * n + 1]
        send_b, recv_b = refs[2 * n + 3], refs[2 * n + 4]
        token = refs[-1]
        (me_pos, _), sibling, chips = _places()
        for j in range(n):
            _wait_all(land_refs[j], lands[j].shape[0] // NDEV, 1 + OTHER_CHIPS, send_a.at[j], recv_a.at[j], me_pos)
        for j in range(n):
            rows = lands[j].shape[0] // NDEV
            for _, idx in chips:
                block = land_refs[j].at[pl.ds(idx * rows, rows)]
                pltpu.make_async_remote_copy(src_ref=block, dst_ref=block, send_sem=send_b.at[j], recv_sem=recv_b.at[j],
                                             device_id=sibling[0], device_id_type=MESH).start()
        token[...] = jnp.zeros_like(token)

    res = pl.pallas_call(
        body, name=name,
        out_shape=(pltpu.SemaphoreType.DMA((n,)), pltpu.SemaphoreType.DMA((n,)))
        + tuple(pltpu.HBM(a.shape, a.dtype) for a in list(srcs) + list(lands)) + (jax.ShapeDtypeStruct((8, 128), F32),),
        in_specs=2 * n * [_HBM] + [_SEM, _SEM, pl.BlockSpec(memory_space=pl.ANY)],
        out_specs=tuple([_SEM, _SEM] + 2 * n * [_HBM] + [pl.BlockSpec(memory_space=pltpu.VMEM)]),
        input_output_aliases={i: 2 + i for i in range(2 * n)},
        compiler_params=pltpu.CompilerParams(has_side_effects=_DATAFLOW),
    )(*srcs, *lands, sems[0], sems[1], after)
    return (res[0], res[1]), list(res[2:2 + n]), list(res[2 + n:2 + 2 * n]), res[-1]


def _split_wait(sems, srcs, lands, after, copies, gather, name):
    n = len(srcs)

    def body(*refs):
        src_refs, land_refs = refs[:n], refs[n:2 * n]
        send_sem, recv_sem, local_sem = refs[2 * n], refs[2 * n + 1], refs[2 * n + 2]
        (me_pos, my), _, _ = _places()
        for j in range(n):
            _wait_all(land_refs[j], lands[j].shape[0] // NDEV, copies, send_sem.at[j], recv_sem.at[j], me_pos)
            _own_copy(src_refs[j], land_refs[j], local_sem.at[j], my, gather).wait()

    res = pl.pallas_call(
        body, name=name, out_shape=tuple(pltpu.HBM(a.shape, a.dtype) for a in list(srcs) + list(lands)),
        in_specs=2 * n * [_HBM] + [_SEM, _SEM, _SEM, pl.BlockSpec(memory_space=pl.ANY)], out_specs=tuple(2 * n * [_HBM]),
        input_output_aliases={i: i for i in range(2 * n)},
        compiler_params=pltpu.CompilerParams(has_side_effects=_DATAFLOW),
    )(*srcs, *lands, sems[0], sems[1], sems[2], after)
    return list(res[n:])


def _chained(gate, mid, after):
    return gate if mid is None else gate + mid(after)[:1, :1]


def _ffn_fwd(x, norms, mod, w, mid=None):
    (pre_g, post_g), (shift, scale, gate), (wg_t, wu_t, wd) = norms, mod, w
    if not callable(wd):
        hn, g, u, a, x_out, f = _ffn_fwd_fused(x, pre_g, scale, shift, post_g, _chained(gate, mid, x), wg_t, wu_t, wd, "ffn_fwd")
        return x_out, (x, hn, g, u, a, f), (wg_t, wu_t, wd)
    hn, g, u, a = _ffn_up(x, pre_g, scale, shift, wg_t, wu_t, "ffn_up")
    wd = wd(a)
    x_out, f = _mm_post(a, wd, x, post_g, _chained(gate, mid, a), FFN_RES, "ffn_down")
    return x_out, (x, hn, g, u, a, f), (wg_t, wu_t, wd)


def _ffn_bwd(dx_out, saved, norms, mod, w, send=None):
    (pre_g, post_g), (_, scale, gate), (wg_t, wu_t, wd) = norms, mod, w
    x, hn, g, u, a, f = saved
    d_model = x.shape[1]
    if send is None:
        df, dg, du, dx, dgate, dpost, dshift, dscale, dpre = _ffn_bwd_fused(dx_out, saved, pre_g, post_g, scale, gate,
                                                                            wg_t, wu_t, wd, "ffn_bwd")
        return dx, (dpre, dpost), (dshift, dscale, dgate), tuple(_ffn_dw(dg, du, a, hn, df, "ffn_dw3"))
    sent = send
    df, dgate, dpost = _post_bwd(dx_out, f, post_g, gate, FFN_RES, "ffn_post_bwd")
    dwd = _mm([(a, df)], "tn", BF16, 256, d_model, "ffn_dw")
    dg, du = _ffn_dgu(df, wd, g, u, "ffn_dgu", after=sent(2, dwd))
    dwg_t = _mm([(dg, hn)], "tn", BF16, 256, d_model, "ffn_dw")
    dwu_t = _mm([(du, hn)], "tn", BF16, 256, d_model, "ffn_dw", after=sent(0, dwg_t))
    dhn = _mm([(dg, wg_t), (du, wu_t)], "nn", F32, TOKEN_TILE, d_model, "ffn_dhn", after=sent(1, dwu_t))
    dx, dshift, dscale, dpre = _prenorm_bwd(dx_out, [dhn], x, pre_g, scale, "prenorm_bwd")
    return dx, (dpre, dpost), (dshift, dscale, dgate), (dwg_t, dwu_t, dwd)


def _mla_fwd(x, norms, mod, w, rope, mid=None):
    (pre_g, post_g), (shift, scale, gate) = norms, mod
    w_in, q_norm, wq_t, kv_norm, wkv_t, wo = w
    hn, lat = _prenorm_mm(x, pre_g, scale, shift, w_in, "nn", F32, LAT_PAD, "mla_in")
    gate = _chained(gate, mid, lat)
    q, k, v, qn, kvn = _mla_qkv(lat, q_norm, kv_norm, wq_t, wkv_t, rope, "mla_qkv")
    o = _mla_attn_fwd(q, k, v, "mla_attn_fwd")
    x_out, f = _mm_post(o, wo, x, post_g, gate, 1.0, "mla_out")
    return x_out, (x, hn, lat, q, k, v, qn, kvn, o, f)


def _mla_bwd(dx_out, saved, norms, mod, w, rope):
    (pre_g, post_g), (_, scale, gate) = norms, mod
    w_in, q_norm, wq_t, kv_norm, wkv_t, wo = w
    x, hn, lat, q, k, v, qn, kvn, o, f = saved
    d_model = x.shape[1]
    df, dgate, dpost = _post_bwd(dx_out, f, post_g, gate, 1.0, "mix_post_bwd")
    d_o = _mm([(df, wo)], "nt", F32, TOKEN_TILE, wo.shape[0], "mla_do")
    dwo = _mm([(o, df)], "tn", BF16, TOKEN_TILE, d_model, "mla_dwo")
    dq, dk, dv = _mla_attn_bwd(q, k, v, d_o, "mla_attn_bwd")
    dqp, dkv, dlat, dq_norm, dkv_norm = _mla_qkv_bwd(dq, dk, dv, lat, q_norm, kv_norm, wq_t, wkv_t, rope, "mla_qkv_bwd")
    dwq_t = _mm([(dqp, qn)], "tn", BF16, TOKEN_TILE, Q_LORA, "mla_dwq")
    dwkv_t = _mm([(dkv, kvn)], "tn", BF16, TOKEN_TILE, KV_LORA, "mla_dwkv")
    dw_in = _mm([(hn, dlat)], "tn", BF16, TOKEN_TILE, LAT_PAD, "mla_dwin")
    dhn = _mm([(dlat, w_in)], "nt", F32, TOKEN_TILE, d_model, "mla_dhn")
    dx, dshift, dscale, dpre = _prenorm_bwd(dx_out, [dhn], x, pre_g, scale, "prenorm_bwd")
    return dx, (dpre, dpost), (dshift, dscale, dgate), (dw_in, dq_norm, dwq_t, dkv_norm, dwkv_t, dwo)


def _dil_fwd(x, norms, mod, w, bias, mid=None):
    (pre_g, post_g), (shift, scale, gate), (w_in_t, wo) = norms, mod, w
    width = 3 * DIL_HEADS * DIL_HEAD_DIM
    hns, qkvs, outs, lses = [], [], [], []
    for g, (window, dilation) in enumerate(DIL_GROUPS):
        hn, qkv = _prenorm_mm(x, pre_g, scale, shift, w_in_t, "nt", BF16, width, "dil_in", perm=dilation,
                              w_rows=(g * width, width))
        if g == 0:
            gate = _chained(gate, mid, qkv)
        o, lse = _dil_attn_fwd(qkv, bias[g], dilation, window // dilation, "dil_attn_fwd")
        hns.append(hn), qkvs.append(qkv), outs.append(o), lses.append(lse)
    alphas, o_mix, o_mix_b = _dil_mix(lses, outs, "dil_mix")
    x_out, f = _mm_post(o_mix_b, wo, x, post_g, gate, 1.0, "dil_out")
    return x_out, (x, hns, qkvs, lses, alphas, o_mix, o_mix_b, f)


def _dil_bwd(dx_out, saved, norms, mod, w, bias):
    (pre_g, post_g), (_, scale, gate), (w_in_t, wo) = norms, mod, w
    x, hns, qkvs, lses, alphas, o_mix, o_mix_b, f = saved
    d_model = x.shape[1]
    inner = DIL_HEADS * DIL_HEAD_DIM
    df, dgate, dpost = _post_bwd(dx_out, f, post_g, gate, 1.0, "mix_post_bwd")
    d_o = _mm([(df, wo)], "nt", F32, TOKEN_TILE, inner, "dil_do")
    dwo = _mm([(o_mix_b, df)], "tn", BF16, TOKEN_TILE, d_model, "dil_dwo")
    dhns, dws, dbs = [], [], []
    for g, (window, dilation) in enumerate(DIL_GROUPS):
        grads = _dil_attn_bwd(qkvs[g], bias[g], d_o, o_mix, alphas[g], lses[g], dilation, window // dilation, "dil_attn_bwd")
        dbs.append(grads[3])
        dhns.append(_mm([(grads[j], w_in_t) for j in range(3)], "nn", F32, TOKEN_TILE, d_model, "dil_dhn", out_perm=dilation,
                        b_rows=[(3 * g + j) * inner for j in range(3)]))
        dws += list(_mm_tn_shared(list(grads[:3]), hns[g], "dil_dwin"))
    dx, dshift, dscale, dpre = _prenorm_bwd(dx_out, dhns, x, pre_g, scale, "prenorm_bwd3")
    return dx, (dpre, dpost), (dshift, dscale, dgate), (jnp.concatenate(dws, axis=0), dwo), jnp.concatenate(dbs, axis=0)


def _pad_rows(a, rows):
    return jnp.pad(a, ((0, rows - a.shape[0]), (0, 0)))


def _lanes(a):
    flat = a.reshape(-1).astype(F32)
    rows = -(-flat.shape[0] // 1024) * 8
    return jnp.pad(flat, (0, rows * 128 - flat.shape[0])).reshape(rows, 128)


def kernel(x, c, norm_pre, norm_post, w_mod, b_mod, ffn_w_gate, ffn_w_up, ffn_w_down, mla_w_in, mla_q_norm, mla_w_q_up, mla_kv_norm, mla_w_kv_up, mla_w_o, dil_w_in, dil_w_o, rel_bias, loss_target, m_norm_pre, m_norm_post, m_w_mod, m_b_mod, m_ffn_w_gate, m_ffn_w_up, m_ffn_w_down, m_mla_w_in, m_mla_q_norm, m_mla_w_q_up, m_mla_kv_norm, m_mla_w_kv_up, m_mla_w_o, m_dil_w_in, m_dil_w_o, m_rel_bias, v_norm_pre, v_norm_post, v_w_mod, v_b_mod, v_ffn_w_gate, v_ffn_w_up, v_ffn_w_down, v_mla_w_in, v_mla_q_norm, v_mla_w_q_up, v_mla_kv_norm, v_mla_w_kv_up, v_mla_w_o, v_dil_w_in, v_dil_w_o, v_rel_bias):
    me = 4 * lax.axis_index("x") + 2 * lax.axis_index("y") + lax.axis_index("c")
    depth, n_sub, d_loc = norm_pre.shape
    d_model = x.shape[2]
    mod_loc_cols = w_mod.shape[2]
    x0, target = x[0], loss_target[0]

    bf_t = lambda a: a.astype(BF16).T
    ffn_ids = [(i, h) for i in range(depth) for h in range(2)]
    shards = []
    for i, h in ffn_ids:
        shards += [bf_t(ffn_w_gate[i, h]), bf_t(ffn_w_up[i, h]), ffn_w_down[i, h].astype(BF16)]
    shards += [mla_w_in[0].astype(BF16), bf_t(mla_w_q_up[0]), bf_t(mla_w_kv_up[0]), mla_w_o[0].astype(BF16),
               bf_t(dil_w_in[0]), dil_w_o[0].astype(BF16)]
    n_ffn = 3 * len(ffn_ids)
    members = {(0, 0): [0, 1, 2], (0, 1): [n_ffn, n_ffn + 1, n_ffn + 2, n_ffn + 3], (0, 2): [3, 4, 5],
               (1, 0): [6, 7, 8], (1, 1): [n_ffn + 4, n_ffn + 5], (1, 2): [9, 10, 11]}
    order = [(i, s) for i in range(depth) for s in range(n_sub)]

    small = jnp.concatenate([c.reshape(8, 128), _pad_rows(norm_pre.reshape(depth * n_sub, d_loc), 8),
                             _pad_rows(norm_post.reshape(depth * n_sub, d_loc), 8)], axis=0)
    small_all = _exchange([small], True, "gather_small")[0].reshape(NDEV, 24, 128)
    c_all = small_all[:, 0:8].reshape(NDEV, d_model)
    gains = lambda lo: jnp.transpose(small_all[:, lo:lo + depth * n_sub], (1, 0, 2)).reshape(depth, n_sub, 1, d_model)
    pre_full, post_full = gains(8), gains(16)

    b_loc = lax.dynamic_slice(b_mod, (0, me * mod_loc_cols), (depth, mod_loc_cols))
    mod_cols, silu_c = _mod_fwd(c_all, w_mod, b_loc, "mod_fwd")
    mod_all = _exchange([mod_cols.reshape(depth * NDEV, mod_loc_cols)], True, "gather_mod")[0]
    mod_all = mod_all.reshape(NDEV, depth, NDEV, mod_loc_cols)
    mod_mine = lax.dynamic_index_in_dim(mod_all, me, axis=2, keepdims=False)
    mod = jnp.transpose(mod_mine, (1, 0, 2)).reshape(depth, n_sub, 3, 1, d_model)

    shards[0], _ = lax.optimization_barrier((shards[0], mod_all))
    first = order[0]
    stages = [("%d%d" % first, members[first][:2]), ("%d%dd" % first, members[first][2:])]
    stages += [("%d%d" % key, members[key]) for key in order[1:]]
    stage_names = [name for name, _ in stages]
    g_sems, g_srcs, g_lands, g_token = _split_start(shards, [idx for _, idx in stages], True, "gather_weights_start")

    forwarded = {}

    def forward(stage, after):
        idx = stages[stage_names.index(stage)][1]
        forwarded[stage] = _gather_forward(g_sems[stage_names.index(stage)], [g_srcs[k] for k in idx],
                                           [g_lands[k] for k in idx], after, "gather_forward_" + stage)
        return forwarded[stage][3]

    def weights_of(stage, after):
        (send_b, recv_b), srcs, lands, _ = forwarded[stage]
        local = g_sems[stage_names.index(stage)][2]
        return _split_wait((send_b, recv_b, local), srcs, lands, after, OTHER_CHIPS, True, "gather_wait_" + stage)

    def late_down(after):
        forward("%d%dd" % first, after)
        return weights_of("%d%dd" % first, after)[0]

    lat_real = Q_LORA + KV_LORA
    qk = QK_NOPE + QK_ROPE

    def mla_weights(after):
        w_in, wq_t, wkv_t, wo = weights_of("01", after)
        w_in_pad = jnp.concatenate([w_in[:, :lat_real], jnp.zeros((d_model, QK_NOPE), BF16), w_in[:, lat_real:],
                                    jnp.zeros((d_model, HEAD_PAD - QK_NOPE - QK_ROPE), BF16)], axis=1)
        wq_pad = jnp.pad(wq_t.reshape(MLA_HEADS, qk, Q_LORA), ((0, 0), (0, HEAD_PAD - qk), (0, 0)))
        wo_pad = jnp.pad(wo.reshape(MLA_HEADS, V_HEAD, d_model), ((0, 0), (HEAD_PAD - V_HEAD, 0), (0, 0)))
        return (w_in_pad, mla_q_norm, wq_pad.reshape(MLA_HEADS * HEAD_PAD, Q_LORA), mla_kv_norm, wkv_t,
                wo_pad.reshape(MLA_HEADS * HEAD_PAD, d_model))

    zero = g_token[0, 0]
    rope = _rope_tables(zero)
    buckets = jnp.stack([_dil_buckets(dil) for _, dil in DIL_GROUPS]) + zero.astype(jnp.int32)
    onehot = (buckets[..., None] == jnp.arange(N_BUCKETS)).astype(F32)
    bias = jnp.einsum("gqkb,bgh->ghqk", onehot, rel_bias.reshape(N_BUCKETS, len(DIL_GROUPS), DIL_HEADS),
                      precision=lax.Precision.HIGHEST)

    norms = lambda i, s: (pre_full[i, s], post_full[i, s])
    mods = lambda i, s: (mod[i, s, 0], mod[i, s, 1], mod[i, s, 2])
    saved, weights = {}, {}
    h = lax.optimization_barrier((x0, bias, buckets, *rope))[0]
    forward("%d%d" % first, h)
    for n, (i, s) in enumerate(order):
        got = mla_weights(h) if (s == 1 and i % 2 == 0) else tuple(weights_of("%d%d" % (i, s), h))
        mid = None if n + 1 == len(order) else (lambda after, nxt="%d%d" % order[n + 1]: forward(nxt, after))
        if s != 1:
            if len(got) == 3:
                h, saved[i, s], weights[i, s] = _ffn_fwd(h, norms(i, s), mods(i, s), got)
                if mid is not None:
                    mid(h)
            else:
                h, saved[i, s], weights[i, s] = _ffn_fwd(h, norms(i, s), mods(i, s), (*got, late_down), mid)
            continue
        weights[i, s] = got
        if i % 2 == 0:
            h, saved[i, s] = _mla_fwd(h, norms(i, s), mods(i, s), weights[i, s], rope, mid)
        else:
            h, saved[i, s] = _dil_fwd(h, norms(i, s), mods(i, s), weights[i, s], bias, mid)
    dh, loss_parts = _loss_grad(h, target, "loss")

    dnorm, dmod, sent = {}, {}, {}
    token = jnp.zeros((8, 128), F32)
    last = order[0]

    def send_last(j, dw):
        sent[last, j] = _split_start([dw], [[0]], False, "scatter_start_%d%d_%d" % (*last, j))
        return sent[last, j][3]

    for i, s in reversed(order):
        md = mods(i, s)
        md = (md[0], md[1], md[2] + token[:1, :1])
        if (i, s) == last:
            dh, dnorm[i, s], dmod[i, s], _ = _ffn_bwd(dh, saved[i, s], norms(i, s), md, weights[i, s], send_last)
            continue
        if s != 1:
            dh, dnorm[i, s], dmod[i, s], dws = _ffn_bwd(dh, saved[i, s], norms(i, s), md, weights[i, s])
        elif i % 2 == 0:
            dh, dnorm[i, s], dmod[i, s], dmla = _mla_bwd(dh, saved[i, s], norms(i, s), md, weights[i, s], rope)
            dw_in_pad, dq_norm, dwq_pad, dkv_norm, dwkv_t, dwo_pad = dmla
            dw_in = jnp.concatenate([dw_in_pad[:, :lat_real], dw_in_pad[:, lat_real + QK_NOPE:lat_real + qk]], axis=1)
            dwq_t = dwq_pad.reshape(MLA_HEADS, HEAD_PAD, Q_LORA)[:, :qk].reshape(MLA_HEADS * qk, Q_LORA)
            dwo = dwo_pad.reshape(MLA_HEADS, HEAD_PAD, d_model)[:, HEAD_PAD - V_HEAD:].reshape(MLA_HEADS * V_HEAD, d_model)
            dws = (dw_in, dwq_t, dwkv_t, dwo)
        else:
            dh, dnorm[i, s], dmod[i, s], dws, dbias = _dil_bwd(dh, saved[i, s], norms(i, s), md, weights[i, s], bias)
        sent[i, s] = _split_start(list(dws), [list(range(len(dws)))], False, "scatter_start_%d%d" % (i, s))
        token = sent[i, s][3]
    grad_x = dh[None]

    mine = {}
    transposed = {3 * n + j for n in range(len(ffn_ids)) for j in (0, 1)} | {n_ffn + 1, n_ffn + 2, n_ffn + 4}
    for key in reversed(order[1:]):
        sems, srcs, lands, _ = sent[key]
        parts = _split_wait(sems[0], srcs, lands, dh, NDEV - 1, False, "scatter_wait_%d%d" % key)
        for k, p in zip(members[key], parts):
            mine[k] = _sum_parts(p, "sum_parts", k in transposed)
    g_mla_in, g_q_up, g_kv_up, g_mla_o, g_dil_in, g_dil_o = (mine[k] for k in range(n_ffn, n_ffn + 6))
    g_mla_in, g_q_up, g_kv_up, g_mla_o = g_mla_in[None], g_q_up[None], g_kv_up[None], g_mla_o[None]
    g_dil_in, g_dil_o = g_dil_in[None], g_dil_o[None]
    early = {"mla_w_in": _adamw(mla_w_in, g_mla_in, m_mla_w_in, v_mla_w_in, "adamw"),
             "mla_w_q_up": _adamw(mla_w_q_up, g_q_up, m_mla_w_q_up, v_mla_w_q_up, "adamw"),
             "mla_w_kv_up": _adamw(mla_w_kv_up, g_kv_up, m_mla_w_kv_up, v_mla_w_kv_up, "adamw"),
             "mla_w_o": _adamw(mla_w_o, g_mla_o, m_mla_w_o, v_mla_w_o, "adamw"),
             "dil_w_in": _adamw(dil_w_in, g_dil_in, m_dil_w_in, v_dil_w_in, "adamw"),
             "dil_w_o": _adamw(dil_w_o, g_dil_o, m_dil_w_o, v_dil_w_o, "adamw")}
    dbias_sums = _bias_reduce(dbias, buckets, "bias_reduce")
    tied = lax.optimization_barrier((dbias_sums, *[a for step in early.values() for a in step]))
    dbias_sums, early = tied[0], {name: tuple(tied[1 + 3 * n:4 + 3 * n]) for n, name in enumerate(early)}
    for j in (2, 0, 1):
        sems, srcs, lands, _ = sent[last, j]
        parts = _split_wait(sems[0], srcs, lands, dbias_sums, NDEV - 1, False, "scatter_wait_%d%d_%d" % (*last, j))
        mine[members[last][j]] = _sum_parts(parts[0], "sum_parts", members[last][j] in transposed)
    g_gate = jnp.stack([mine[3 * n] for n in range(len(ffn_ids))]).reshape(ffn_w_gate.shape)
    g_up = jnp.stack([mine[3 * n + 1] for n in range(len(ffn_ids))]).reshape(ffn_w_up.shape)
    g_down = jnp.stack([mine[3 * n + 2] for n in range(len(ffn_ids))]).reshape(ffn_w_down.shape)

    dmod_mine = jnp.concatenate([jnp.concatenate(dmod[i, s], axis=0) for i in range(depth) for s in range(n_sub)], axis=0)
    dpre_mine = jnp.concatenate([dnorm[i, s][0] for i in range(depth) for s in range(n_sub)], axis=0)
    dpost_mine = jnp.concatenate([dnorm[i, s][1] for i in range(depth) for s in range(n_sub)], axis=0)
    dbias_tab = dbias_sums[:, 0, :N_BUCKETS].T
    pieces = [dmod_mine, dpre_mine, dpost_mine, dq_norm, dkv_norm, dbias_tab, jnp.sum(loss_parts).reshape(1, 1)]
    packed = [_lanes(p) for p in pieces]
    offs = [0]
    for p in packed:
        offs.append(offs[-1] + p.shape[0])
    everyone = _exchange([jnp.concatenate(packed, axis=0)], True, "gather_small_grads")[0].reshape(NDEV, offs[-1], 128)
    total = _sum_parts(everyone, "sum_small")
    take = lambda n, shape: total[offs[n]:offs[n + 1]].reshape(-1)[:math.prod(shape)].reshape(shape)
    g_b_mod = take(0, b_mod.shape)
    col0 = me * d_loc
    g_norm_pre = lax.dynamic_slice(take(1, (depth, n_sub, d_model)), (0, 0, col0), norm_pre.shape)
    g_norm_post = lax.dynamic_slice(take(2, (depth, n_sub, d_model)), (0, 0, col0), norm_post.shape)
    g_q_norm, g_kv_norm = take(3, mla_q_norm.shape), take(4, mla_kv_norm.shape)
    g_rel_bias = take(5, rel_bias.shape)
    loss = take(6, ())

    dmod_all = everyone[:, offs[0]:offs[1]].reshape(NDEV, depth, NDEV * mod_loc_cols)
    dmod_cols = lax.dynamic_slice(dmod_all, (0, 0, me * mod_loc_cols), (NDEV, depth, mod_loc_cols))
    silu_t = jnp.pad(silu_c.T, ((0, 0), (0, HEAD_PAD - NDEV)))
    g_w_mod = jnp.stack([_mm([(silu_t, jnp.pad(dmod_cols[:, i], ((0, HEAD_PAD - NDEV), (0, 0))))], "nn", F32, TOKEN_TILE,
                             mod_loc_cols, "mod_bwd") for i in range(depth)])

    ws = (norm_pre, norm_post, w_mod, b_mod, ffn_w_gate, ffn_w_up, ffn_w_down, mla_w_in, mla_q_norm, mla_w_q_up, mla_kv_norm,
          mla_w_kv_up, mla_w_o, dil_w_in, dil_w_o, rel_bias)
    gs = (g_norm_pre, g_norm_post, g_w_mod, g_b_mod, g_gate, g_up, g_down, g_mla_in, g_q_norm, g_q_up, g_kv_norm, g_kv_up,
          g_mla_o, g_dil_in, g_dil_o, g_rel_bias)
    ms = (m_norm_pre, m_norm_post, m_w_mod, m_b_mod, m_ffn_w_gate, m_ffn_w_up, m_ffn_w_down, m_mla_w_in, m_mla_q_norm,
          m_mla_w_q_up, m_mla_kv_norm, m_mla_w_kv_up, m_mla_w_o, m_dil_w_in, m_dil_w_o, m_rel_bias)
    vs = (v_norm_pre, v_norm_post, v_w_mod, v_b_mod, v_ffn_w_gate, v_ffn_w_up, v_ffn_w_down, v_mla_w_in, v_mla_q_norm,
          v_mla_w_q_up, v_mla_kv_norm, v_mla_w_kv_up, v_mla_w_o, v_dil_w_in, v_dil_w_o, v_rel_bias)
    names = ("norm_pre", "norm_post", "w_mod", "b_mod", "ffn_w_gate", "ffn_w_up", "ffn_w_down", "mla_w_in", "mla_q_norm",
             "mla_w_q_up", "mla_kv_norm", "mla_w_kv_up", "mla_w_o", "dil_w_in", "dil_w_o", "rel_bias")
    stepped = [early[n] if n in early else _adamw(w, g, m, v, "adamw") for n, w, g, m, v in zip(names, ws, gs, ms, vs)]
    deltas, new_m, new_v = zip(*stepped)
    return (loss, grad_x, *gs, *deltas, *new_m, *new_v)
```

```python
import math

import jax
import jax.numpy as jnp
from jax import lax
from jax.experimental import pallas as pl
from jax.experimental.pallas import tpu as pltpu

F32 = jnp.float32
BF16 = jnp.bfloat16
MESH = pl.DeviceIdType.MESH

NDEV = 8
OTHER_CHIPS = 3
D_MODEL = 1024
SEQ = 2048
D_FF = 2816
EPS = 1e-6
FFN_RES = 0.5
FFN_CHUNKS = 11

MLA_HEADS = 16
Q_LORA = 384
KV_LORA = 256
QK_NOPE = 64
QK_ROPE = 32
V_HEAD = 64
ROPE_THETA = 10000.0
HEAD_PAD = 128
LAT_PAD = Q_LORA + KV_LORA + HEAD_PAD
MLA_SCALE = (QK_NOPE + QK_ROPE) ** -0.5
MLA_QUERY_TILE = 256

DIL_GROUPS = ((128, 1), (512, 4), (2048, 16))
DIL_HEADS = 16
DIL_HEAD_DIM = 64
DIL_BLOCK = 128
DIL_PAIRS = DIL_HEADS // 2
DIL_SCALE = DIL_HEAD_DIM ** -0.5
DIL_GROUPED = 8
N_BUCKETS = 32
MAX_DISTANCE = 2048

ADAM_LR = 0.001
ADAM_B1 = 0.9
ADAM_B2 = 0.999
ADAM_EPS = 1e-08
ADAM_WD = 0.01
ADAM_STEP = 10

V7X_VMEM_BYTES = 64 * 2**20
VMEM_RESERVE = 10 * 2**20
TOKEN_TILE = 512


def _nbytes(shape, dtype):
    return math.prod(shape) * jnp.dtype(dtype).itemsize


def _params(semantics, blocks, extra=0):
    need = 2 * sum(_nbytes(s, d) for s, d in blocks) + extra + VMEM_RESERVE
    return pltpu.CompilerParams(dimension_semantics=semantics,
                                vmem_limit_bytes=int(min(need, V7X_VMEM_BYTES - VMEM_RESERVE)))


def _pcall(body, out_shape, **kw):
    call = pl.pallas_call(body, out_shape=jax.tree.map(lambda s: pltpu.HBM(s.shape, s.dtype), out_shape), **kw)
    return lambda *args: call(*[pltpu.with_memory_space_constraint(a, pltpu.HBM) for a in args])


def _dot_nn(a, b):
    return lax.dot_general(a, b, (((1,), (0,)), ((), ())), preferred_element_type=F32)


def _dot_nt(a, b):
    return lax.dot_general(a, b, (((1,), (1,)), ((), ())), preferred_element_type=F32)


def _dot_tn(a, b):
    return lax.dot_general(a, b, (((0,), (0,)), ((), ())), preferred_element_type=F32)


_DOTS = {"nn": _dot_nn, "nt": _dot_nt, "tn": _dot_tn}


def _rstd(v):
    return lax.rsqrt(jnp.mean(v * v, axis=-1, keepdims=True) + EPS)


def _rms_bwd(v, r, t):
    return r * t - v * (r * r * r) * jnp.mean(t * v, axis=-1, keepdims=True)


_TOKEN_SPEC = pl.BlockSpec((8, 128), lambda *_: (0, 0))


def _mm(pairs, mode, out_dtype, tm, tn, name, out_perm=1, after=None, b_rows=None):
    a0, b0 = pairs[0]
    m_dim = a0.shape[1] if mode == "tn" else a0.shape[0]
    n_dim = b0.shape[0] if mode == "nt" else b0.shape[1]
    tm, tn = min(tm, m_dim // out_perm), min(tn, n_dim)
    assert m_dim % tm == 0 and n_dim % tn == 0, (name, m_dim, n_dim, tm, tn)
    dot = _DOTS[mode]
    npairs = len(pairs)

    def body(*refs):
        acc = None
        for p in range(npairs):
            d = dot(refs[2 * p][...].astype(BF16), refs[2 * p + 1][...].astype(BF16))
            acc = d if acc is None else acc + d
        refs[-1][...] = acc.astype(out_dtype)

    in_specs, blocks, flat = [], [], []
    for n_pair, (a, b) in enumerate(pairs):
        if mode == "nn":
            k = a.shape[1]
            first_block = 0 if b_rows is None else b_rows[n_pair] // k
            sa, sb = ((tm, k), lambda i, j: (i, 0)), ((k, tn), lambda i, j, o=first_block: (o, j))
        elif mode == "nt":
            k = a.shape[1]
            sa, sb = ((tm, k), lambda i, j: (i, 0)), ((tn, k), lambda i, j: (j, 0))
        else:
            k = a.shape[0]
            sa, sb = ((k, tm), lambda i, j: (0, i)), ((k, tn), lambda i, j: (0, j))
        in_specs += [pl.BlockSpec(*sa), pl.BlockSpec(*sb)]
        blocks += [(sa[0], a.dtype), (sb[0], b.dtype)]
        flat += [a, b]
    if after is not None:
        in_specs.append(_TOKEN_SPEC)
        flat.append(after)
    if out_perm == 1:
        out_shape = (m_dim, n_dim)
        out_spec = pl.BlockSpec((tm, tn), lambda i, j: (i, j))
    else:
        rows = m_dim // out_perm
        assert tn == n_dim and rows % tm == 0, (name, rows, tm)
        nb = rows // tm
        out_shape = (rows, out_perm * n_dim)
        out_spec = pl.BlockSpec((tm, n_dim), lambda i, j: (i % nb, i // nb))
    blocks.append(((tm, tn), out_dtype))
    res = _pcall(
        body, out_shape=jax.ShapeDtypeStruct(out_shape, out_dtype), grid=(m_dim // tm, n_dim // tn),
        in_specs=in_specs, out_specs=out_spec, name=name,
        compiler_params=_params(("parallel", "parallel"), blocks, extra=2 * tm * tn * 4),
    )(*flat)
    return res.reshape(m_dim, n_dim)


def _prenorm_mm(x, pre_g, scale, shift, w, w_mode, out_dtype, tn, name, perm=1, w_rows=None):
    s_dim, d_dim = x.shape
    n_dim = w.shape[0] if w_mode == "nt" else w.shape[1]
    w_first = 0
    if w_rows is not None:
        w_first, n_dim = w_rows
    rows = s_dim // perm
    side = max(1, TOKEN_TILE // rows)
    tm = side * min(TOKEN_TILE, rows)
    nb = max(1, rows // tm)
    tn = min(tn, n_dim)
    assert n_dim % tn == 0 and w_first % tn == 0
    w_block0 = w_first // tn
    dot = _DOTS[w_mode]

    def body(x_ref, g_ref, sc_ref, sh_ref, w_ref, hn_ref, o_ref):
        @pl.when(pl.program_id(1) == 0)
        def _():
            xf = x_ref[...]
            if side > 1:
                xf = jnp.concatenate([xf[:, c * d_dim:(c + 1) * d_dim] for c in range(side)], axis=0)
            hn = (xf * _rstd(xf) * g_ref[...]) * (1.0 + sc_ref[...]) + sh_ref[...]
            hn_ref[...] = hn.astype(BF16)

        o_ref[...] = dot(hn_ref[...], w_ref[...]).astype(out_dtype)

    vec = pl.BlockSpec((1, d_dim), lambda i, j: (0, 0))
    w_block = (tn, d_dim) if w_mode == "nt" else (d_dim, tn)
    w_spec = pl.BlockSpec(w_block, (lambda i, j: (w_block0 + j, 0)) if w_mode == "nt" else (lambda i, j: (0, j)))
    hn, out = _pcall(
        body,
        out_shape=(jax.ShapeDtypeStruct((s_dim, d_dim), BF16), jax.ShapeDtypeStruct((s_dim, n_dim), out_dtype)),
        grid=(s_dim // tm, n_dim // tn),
        in_specs=[pl.BlockSpec((tm // side, side * d_dim), lambda i, j: (i % nb, i // nb)), vec, vec, vec, w_spec],
        out_specs=(pl.BlockSpec((tm, d_dim), lambda i, j: (i, 0)), pl.BlockSpec((tm, tn), lambda i, j: (i, j))),
        name=name,
        compiler_params=_params(("parallel", "arbitrary"),
                                [((tm, d_dim), F32), (w_block, BF16), ((tm, d_dim), BF16), ((tm, tn), out_dtype)],
                                extra=3 * tm * d_dim * 4 + tm * tn * 4),
    )(x.reshape(rows, perm * d_dim), pre_g, scale, shift, w)
    return hn, out


def _ffn_up(x, pre_g, scale, shift, wg_t, wu_t, name):
    s_dim, d_dim = x.shape
    f_dim = wg_t.shape[0]
    tm, tn = TOKEN_TILE, f_dim // 2

    def body(x_ref, g_ref, sc_ref, sh_ref, wg_ref, wu_ref, hn_ref, go_ref, uo_ref, a_ref):
        @pl.when(pl.program_id(1) == 0)
        def _():
            xf = x_ref[...]
            hn = (xf * _rstd(xf) * g_ref[...]) * (1.0 + sc_ref[...]) + sh_ref[...]
            hn_ref[...] = hn.astype(BF16)

        hn = hn_ref[...]
        g = _dot_nt(hn, wg_ref[...])
        u = _dot_nt(hn, wu_ref[...])
        go_ref[...] = g.astype(BF16)
        uo_ref[...] = u.astype(BF16)
        a_ref[...] = (g * jax.nn.sigmoid(g) * u).astype(BF16)

    vec = pl.BlockSpec((1, d_dim), lambda i, j: (0, 0))
    w_spec = pl.BlockSpec((tn, d_dim), lambda i, j: (j, 0))
    act = pl.BlockSpec((tm, tn), lambda i, j: (i, j))
    act_shape = jax.ShapeDtypeStruct((s_dim, f_dim), BF16)
    return _pcall(
        body,
        out_shape=(jax.ShapeDtypeStruct((s_dim, d_dim), BF16), act_shape, act_shape, act_shape),
        grid=(s_dim // tm, f_dim // tn),
        in_specs=[pl.BlockSpec((tm, d_dim), lambda i, j: (i, 0)), vec, vec, vec, w_spec, w_spec],
        out_specs=(pl.BlockSpec((tm, d_dim), lambda i, j: (i, 0)), act, act, act),
        name=name,
        compiler_params=_params(("parallel", "arbitrary"),
                                [((tm, d_dim), F32), ((tn, d_dim), BF16), ((tn, d_dim), BF16), ((tm, d_dim), BF16)]
                                + 3 * [((tm, tn), BF16)], extra=3 * tm * d_dim * 4 + 4 * tm * tn * 4),
    )(x, pre_g, scale, shift, wg_t, wu_t)


def _mm_post(a, w, x, post_g, gate, res_w, name):
    s_dim, k_dim = a.shape
    d_dim = w.shape[1]
    tm = TOKEN_TILE

    def body(a_ref, w_ref, x_ref, pg_ref, gt_ref, xo_ref, f_ref):
        f = _dot_nn(a_ref[...], w_ref[...])
        y = f * _rstd(f) * pg_ref[...]
        f_ref[...] = f
        xo_ref[...] = x_ref[...] + (res_w * gt_ref[...]) * y

    vec = pl.BlockSpec((1, d_dim), lambda i: (0, 0))
    row = pl.BlockSpec((tm, d_dim), lambda i: (i, 0))
    out = jax.ShapeDtypeStruct((s_dim, d_dim), F32)
    return _pcall(
        body, out_shape=(out, out), grid=(s_dim // tm,),
        in_specs=[pl.BlockSpec((tm, k_dim), lambda i: (i, 0)), pl.BlockSpec((k_dim, d_dim), lambda i: (0, 0)), row, vec, vec],
        out_specs=(row, row), name=name,
        compiler_params=_params(("parallel",), [((tm, k_dim), BF16), ((k_dim, d_dim), BF16)] + 3 * [((tm, d_dim), F32)],
                                extra=3 * tm * d_dim * 4),
    )(a, w, x, post_g, gate)


def _post_bwd(dx_out, f, post_g, gate, res_w, name):
    s_dim, d_dim = f.shape
    tm = TOKEN_TILE

    def body(dx_ref, f_ref, pg_ref, gt_ref, df_ref, dgate_ref, dpost_ref):
        @pl.when(pl.program_id(0) == 0)
        def _():
            dgate_ref[...] = jnp.zeros_like(dgate_ref)
            dpost_ref[...] = jnp.zeros_like(dpost_ref)

        dx, fv = dx_ref[...], f_ref[...]
        r = _rstd(fv)
        fr = fv * r
        dgate_ref[...] += res_w * jnp.sum(dx * (fr * pg_ref[...]), axis=0, keepdims=True)
        dy = (res_w * gt_ref[...]) * dx
        dpost_ref[...] += jnp.sum(dy * fr, axis=0, keepdims=True)
        df_ref[...] = _rms_bwd(fv, r, dy * pg_ref[...]).astype(BF16)

    vec = pl.BlockSpec((1, d_dim), lambda i: (0, 0))
    row = pl.BlockSpec((tm, d_dim), lambda i: (i, 0))
    vshape = jax.ShapeDtypeStruct((1, d_dim), F32)
    return _pcall(
        body, out_shape=(jax.ShapeDtypeStruct((s_dim, d_dim), BF16), vshape, vshape), grid=(s_dim // tm,),
        in_specs=[row, row, vec, vec], out_specs=(row, vec, vec), name=name,
        compiler_params=_params(("arbitrary",), 3 * [((tm, d_dim), F32)], extra=6 * tm * d_dim * 4),
    )(dx_out, f, post_g, gate)


def _prenorm_bwd(dx_out, dhns, x, pre_g, scale, name):
    s_dim, d_dim = x.shape
    tm = TOKEN_TILE
    n_in = len(dhns)

    def body(*refs):
        dx_ref, x_ref, pg_ref, sc_ref = refs[n_in + 0], refs[n_in + 1], refs[n_in + 2], refs[n_in + 3]
        dxo_ref, dsh_ref, dsc_ref, dpg_ref = refs[n_in + 4:]

        @pl.when(pl.program_id(0) == 0)
        def _():
            dsh_ref[...] = jnp.zeros_like(dsh_ref)
            dsc_ref[...] = jnp.zeros_like(dsc_ref)
            dpg_ref[...] = jnp.zeros_like(dpg_ref)

        dhn = refs[0][...]
        for k in range(1, n_in):
            dhn = dhn + refs[k][...]
        xv = x_ref[...]
        r = _rstd(xv)
        xr = xv * r
        dsh_ref[...] += jnp.sum(dhn, axis=0, keepdims=True)
        dsc_ref[...] += jnp.sum(dhn * (xr * pg_ref[...]), axis=0, keepdims=True)
        dn = dhn * (1.0 + sc_ref[...])
        dpg_ref[...] += jnp.sum(dn * xr, axis=0, keepdims=True)
        dxo_ref[...] = dx_ref[...] + _rms_bwd(xv, r, dn * pg_ref[...])

    vec = pl.BlockSpec((1, d_dim), lambda i: (0, 0))
    row = pl.BlockSpec((tm, d_dim), lambda i: (i, 0))
    vshape = jax.ShapeDtypeStruct((1, d_dim), F32)
    return _pcall(
        body, out_shape=(jax.ShapeDtypeStruct((s_dim, d_dim), F32), vshape, vshape, vshape), grid=(s_dim // tm,),
        in_specs=n_in * [row] + [row, row, vec, vec], out_specs=(row, vec, vec, vec), name=name,
        compiler_params=_params(("arbitrary",), (n_in + 3) * [((tm, d_dim), F32)], extra=6 * tm * d_dim * 4),
    )(*dhns, dx_out, x, pre_g, scale)


def _ffn_dgu(df, wd, g, u, name, after=None):
    s_dim, d_dim = df.shape
    f_dim = wd.shape[0]
    tm, tn = TOKEN_TILE, f_dim // 2

    def body(df_ref, wd_ref, g_ref, u_ref, *rest):
        dg_ref, du_ref = rest[-2:]
        da = _dot_nt(df_ref[...], wd_ref[...])
        gv, uv = g_ref[...].astype(F32), u_ref[...].astype(F32)
        sg = jax.nn.sigmoid(gv)
        du_ref[...] = (da * (gv * sg)).astype(BF16)
        dg_ref[...] = (da * uv * (sg * (1.0 + gv * (1.0 - sg)))).astype(BF16)

    act = pl.BlockSpec((tm, tn), lambda i, j: (i, j))
    act_shape = jax.ShapeDtypeStruct((s_dim, f_dim), BF16)
    token = [] if after is None else [after]
    return _pcall(
        body, out_shape=(act_shape, act_shape), grid=(s_dim // tm, f_dim // tn),
        in_specs=[pl.BlockSpec((tm, d_dim), lambda i, j: (i, 0)), pl.BlockSpec((tn, d_dim), lambda i, j: (j, 0)), act, act]
        + len(token) * [_TOKEN_SPEC],
        out_specs=(act, act), name=name,
        compiler_params=_params(("parallel", "parallel"), [((tm, d_dim), BF16), ((tn, d_dim), BF16)] + 4 * [((tm, tn), BF16)],
                                extra=6 * tm * tn * 4),
    )(df, wd, g, u, *token)


def _ffn_dw(dg, du, a, hn, df, name):
    s_dim, f_dim = dg.shape
    d_dim = hn.shape[1]
    tm = 256

    def body(dg_ref, du_ref, a_ref, hn_ref, df_ref, dwg_ref, dwu_ref, dwd_ref):
        dwg_ref[...] = _dot_tn(dg_ref[...], hn_ref[...]).astype(BF16)
        dwu_ref[...] = _dot_tn(du_ref[...], hn_ref[...]).astype(BF16)
        dwd_ref[...] = _dot_tn(a_ref[...], df_ref[...]).astype(BF16)

    col = pl.BlockSpec((s_dim, tm), lambda i: (0, i))
    full = pl.BlockSpec((s_dim, d_dim), lambda i: (0, 0), pipeline_mode=pl.Buffered(1))
    out = pl.BlockSpec((tm, d_dim), lambda i: (i, 0))
    shape = jax.ShapeDtypeStruct((f_dim, d_dim), BF16)
    need = 2 * s_dim * d_dim * 2 + 2 * 3 * (s_dim * tm * 2 + tm * d_dim * 2) + 3 * tm * d_dim * 4 + 3 * s_dim * tm * 2
    return _pcall(
        body, out_shape=(shape, shape, shape), grid=(f_dim // tm,), in_specs=[col, col, col, full, full],
        out_specs=(out, out, out), name=name,
        compiler_params=pltpu.CompilerParams(dimension_semantics=("parallel",),
                                             vmem_limit_bytes=int(min(need + VMEM_RESERVE, V7X_VMEM_BYTES - VMEM_RESERVE))),
    )(dg, du, a, hn, df)


def _mm_tn_shared(lhs, b, name):
    k_dim, m_dim = lhs[0].shape
    n_dim = b.shape[1]
    tm = 256
    n = len(lhs)

    def body(*refs):
        rhs = refs[n][...]
        for j in range(n):
            refs[n + 1 + j][...] = _dot_tn(refs[j][...], rhs).astype(BF16)

    col = pl.BlockSpec((k_dim, tm), lambda i: (0, i))
    out = pl.BlockSpec((tm, n_dim), lambda i: (i, 0))
    shape = jax.ShapeDtypeStruct((m_dim, n_dim), BF16)
    need = k_dim * n_dim * 2 + 2 * n * (k_dim * tm * 2 + tm * n_dim * 2) + n * tm * n_dim * 4 + n * k_dim * tm * 2
    return _pcall(
        body, out_shape=tuple(n * [shape]), grid=(m_dim // tm,),
        in_specs=n * [col] + [pl.BlockSpec((k_dim, n_dim), lambda i: (0, 0), pipeline_mode=pl.Buffered(1))],
        out_specs=tuple(n * [out]), name=name,
        compiler_params=pltpu.CompilerParams(dimension_semantics=("parallel",),
                                             vmem_limit_bytes=int(min(need + VMEM_RESERVE, V7X_VMEM_BYTES - VMEM_RESERVE))),
    )(*lhs, b)


def _ffn_fwd_fused(x, pre_g, scale, shift, post_g, gate, wg_t, wu_t, wd, name):
    s_dim, d_dim = x.shape
    f_dim = wd.shape[0]
    tm, chunks = 256, FFN_CHUNKS
    cw = f_dim // chunks

    def body(x_ref, prg_ref, sc_ref, sh_ref, pg_ref, gt_ref, wg_ref, wu_ref, wd_ref, hn_ref, go_ref, uo_ref, a_ref, xo_ref, f_ref):
        xf = x_ref[...]
        hn = ((xf * _rstd(xf) * prg_ref[...]) * (1.0 + sc_ref[...]) + sh_ref[...]).astype(BF16)
        hn_ref[...] = hn
        f = None
        ahead = (_dot_nt(hn, wg_ref[0:cw, :]), _dot_nt(hn, wu_ref[0:cw, :]))
        for c in range(chunks):
            g, u = ahead
            if c + 1 < chunks:
                nxt = slice((c + 1) * cw, (c + 2) * cw)
                ahead = (_dot_nt(hn, wg_ref[nxt, :]), _dot_nt(hn, wu_ref[nxt, :]))
            cols = slice(c * cw, (c + 1) * cw)
            go_ref[:, cols] = g.astype(BF16)
            uo_ref[:, cols] = u.astype(BF16)
            a = (g * jax.nn.sigmoid(g) * u).astype(BF16)
            a_ref[:, cols] = a
            part = _dot_nn(a, wd_ref[cols, :])
            f = part if f is None else f + part
        f_ref[...] = f
        xo_ref[...] = xf + (FFN_RES * gt_ref[...]) * (f * _rstd(f) * pg_ref[...])

    vec = pl.BlockSpec((1, d_dim), lambda i: (0, 0))
    row = pl.BlockSpec((tm, d_dim), lambda i: (i, 0))
    act = pl.BlockSpec((tm, f_dim), lambda i: (i, 0))
    weight = pl.BlockSpec((f_dim, d_dim), lambda i: (0, 0), pipeline_mode=pl.Buffered(1))
    act_shape = jax.ShapeDtypeStruct((s_dim, f_dim), BF16)
    res_shape = jax.ShapeDtypeStruct((s_dim, d_dim), F32)
    need = (3 * f_dim * d_dim * 2 + 2 * tm * d_dim * 4 + 2 * (tm * d_dim * 2 + 3 * tm * f_dim * 2 + 2 * tm * d_dim * 4)
            + 8 * tm * cw * 4 + 4 * tm * d_dim * 4)
    return _pcall(
        body, out_shape=(jax.ShapeDtypeStruct((s_dim, d_dim), BF16), act_shape, act_shape, act_shape, res_shape, res_shape),
        grid=(s_dim // tm,), in_specs=[row, vec, vec, vec, vec, vec, weight, weight, weight],
        out_specs=(row, act, act, act, row, row), name=name,
        compiler_params=pltpu.CompilerParams(dimension_semantics=("parallel",),
                                             vmem_limit_bytes=int(min(need + VMEM_RESERVE, V7X_VMEM_BYTES - VMEM_RESERVE))),
    )(x, pre_g, scale, shift, post_g, gate, wg_t, wu_t, wd)


def _ffn_bwd_fused(dx_out, saved, pre_g, post_g, scale, gate, wg_t, wu_t, wd, name):
    x, _, g, u, _, f = saved
    s_dim, d_dim = x.shape
    f_dim = wd.shape[0]
    tm, chunks = 256, FFN_CHUNKS
    cw = f_dim // chunks

    def body(dx_ref, f_ref, g_ref, u_ref, x_ref, pg_ref, gt_ref, prg_ref, sc_ref, wd_ref, wg_ref, wu_ref,
             df_ref, dg_ref, du_ref, dxo_ref, dgate_ref, dpost_ref, dsh_ref, dsc_ref, dpg_ref):
        @pl.when(pl.program_id(0) == 0)
        def _():
            for acc in (dgate_ref, dpost_ref, dsh_ref, dsc_ref, dpg_ref):
                acc[...] = jnp.zeros_like(acc)

        dx, fv = dx_ref[...], f_ref[...]
        r = _rstd(fv)
        fr = fv * r
        dgate_ref[...] += FFN_RES * jnp.sum(dx * (fr * pg_ref[...]), axis=0, keepdims=True)
        dy = (FFN_RES * gt_ref[...]) * dx
        dpost_ref[...] += jnp.sum(dy * fr, axis=0, keepdims=True)
        df = _rms_bwd(fv, r, dy * pg_ref[...]).astype(BF16)
        df_ref[...] = df
        dhn = None
        ahead = _dot_nt(df, wd_ref[0:cw, :])
        for c in range(chunks):
            da = ahead
            if c + 1 < chunks:
                ahead = _dot_nt(df, wd_ref[(c + 1) * cw:(c + 2) * cw, :])
            cols = slice(c * cw, (c + 1) * cw)
            gv, uv = g_ref[:, cols].astype(F32), u_ref[:, cols].astype(F32)
            sg = jax.nn.sigmoid(gv)
            du = (da * (gv * sg)).astype(BF16)
            dg = (da * uv * (sg * (1.0 + gv * (1.0 - sg)))).astype(BF16)
            dg_ref[:, cols] = dg
            du_ref[:, cols] = du
            part = _dot_nn(dg, wg_ref[cols, :]) + _dot_nn(du, wu_ref[cols, :])
            dhn = part if dhn is None else dhn + part
        xv = x_ref[...]
        rx = _rstd(xv)
        xr = xv * rx
        dsh_ref[...] += jnp.sum(dhn, axis=0, keepdims=True)
        dsc_ref[...] += jnp.sum(dhn * (xr * prg_ref[...]), axis=0, keepdims=True)
        dn = dhn * (1.0 + sc_ref[...])
        dpg_ref[...] += jnp.sum(dn * xr, axis=0, keepdims=True)
        dxo_ref[...] = dx + _rms_bwd(xv, rx, dn * prg_ref[...])

    vec = pl.BlockSpec((1, d_dim), lambda i: (0, 0))
    row = pl.BlockSpec((tm, d_dim), lambda i: (i, 0))
    act = pl.BlockSpec((tm, f_dim), lambda i: (i, 0))
    weight = pl.BlockSpec((f_dim, d_dim), lambda i: (0, 0), pipeline_mode=pl.Buffered(1))
    vshape = jax.ShapeDtypeStruct((1, d_dim), F32)
    act_shape = jax.ShapeDtypeStruct((s_dim, f_dim), BF16)
    need = (3 * f_dim * d_dim * 2 + 2 * (3 * tm * d_dim * 4 + 2 * tm * f_dim * 2) + 2 * (tm * d_dim * 2 + 2 * tm * f_dim * 2 + tm * d_dim * 4)
            + 6 * tm * cw * 4 + 6 * tm * d_dim * 4)
    return _pcall(
        body, out_shape=(jax.ShapeDtypeStruct((s_dim, d_dim), BF16), act_shape, act_shape, jax.ShapeDtypeStruct((s_dim, d_dim), F32),
                         vshape, vshape, vshape, vshape, vshape),
        grid=(s_dim // tm,), in_specs=[row, row, act, act, row, vec, vec, vec, vec, weight, weight, weight],
        out_specs=(row, act, act, row, vec, vec, vec, vec, vec), name=name,
        compiler_params=pltpu.CompilerParams(dimension_semantics=("arbitrary",),
                                             vmem_limit_bytes=int(min(need + VMEM_RESERVE, V7X_VMEM_BYTES - VMEM_RESERVE))),
    )(dx_out, f, g, u, x, post_g, gate, pre_g, scale, wd, wg_t, wu_t)


def _rope_tables(zero=0.0):
    half = QK_ROPE // 2
    freqs = ROPE_THETA ** (-jnp.arange(half, dtype=F32) / half)
    ang = (jnp.arange(SEQ, dtype=F32)[:, None] + zero) * freqs[None, :]
    cos, sin = jnp.cos(ang), jnp.sin(ang)
    ones = jnp.ones((SEQ, QK_NOPE), F32)
    zeros = jnp.zeros((SEQ, QK_NOPE), F32)
    pad1 = jnp.ones((SEQ, HEAD_PAD - QK_NOPE - QK_ROPE), F32)
    pad0 = jnp.zeros((SEQ, HEAD_PAD - QK_NOPE - QK_ROPE), F32)
    zh = jnp.zeros((SEQ, half), F32)
    c = jnp.concatenate([ones, cos, cos, pad1], axis=1)
    s1 = jnp.concatenate([zeros, -sin, zh, pad0], axis=1)
    s2 = jnp.concatenate([zeros, zh, sin, pad0], axis=1)
    return c, s1, s2


def _rope(v, c, s1, s2):
    half = QK_ROPE // 2
    return v * c + pltpu.roll(v, HEAD_PAD - half, 1) * s1 + pltpu.roll(v, half, 1) * s2


def _rope_t(dv, c, s1, s2):
    half = QK_ROPE // 2
    return dv * c + pltpu.roll(dv * s1, half, 1) + pltpu.roll(dv * s2, HEAD_PAD - half, 1)


def _mla_qkv(lat, q_norm, kv_norm, wq_t, wkv_t, rope, name):
    s_dim = lat.shape[0]
    width = MLA_HEADS * HEAD_PAD
    tm = 256

    def body(lat_ref, qg_ref, kg_ref, wq_ref, wkv_ref, c_ref, s1_ref, s2_ref, q_ref, k_ref, v_ref, qn_ref, kvn_ref):
        cq = lat_ref[:, :Q_LORA]
        ckv = lat_ref[:, Q_LORA:Q_LORA + KV_LORA]
        kr = lat_ref[:, Q_LORA + KV_LORA:]
        c, s1, s2 = c_ref[...], s1_ref[...], s2_ref[...]
        qn = (cq * _rstd(cq) * qg_ref[...]).astype(BF16)
        kvn = (ckv * _rstd(ckv) * kg_ref[...]).astype(BF16)
        qn_ref[...] = qn
        kvn_ref[...] = kvn
        q = _dot_nt(qn, wq_ref[...])
        kv = _dot_nt(kvn, wkv_ref[...])
        krr = _rope(kr, c, s1, s2)
        low = lax.broadcasted_iota(jnp.int32, (tm, HEAD_PAD), 1) < QK_NOPE
        for h in range(MLA_HEADS):
            sl = slice(h * HEAD_PAD, (h + 1) * HEAD_PAD)
            q_ref[:, sl] = _rope(q[:, sl], c, s1, s2).astype(BF16)
            kvh = kv[:, sl]
            k_ref[:, sl] = (jnp.where(low, kvh, 0.0) + krr).astype(BF16)
            v_ref[:, sl] = jnp.where(low, 0.0, kvh).astype(BF16)

    row = lambda n: pl.BlockSpec((tm, n), lambda i: (i, 0))
    full = lambda a: pl.BlockSpec(a.shape, lambda i: (0, 0))
    wide = jax.ShapeDtypeStruct((s_dim, width), BF16)
    return _pcall(
        body,
        out_shape=(wide, wide, wide, jax.ShapeDtypeStruct((s_dim, Q_LORA), BF16), jax.ShapeDtypeStruct((s_dim, KV_LORA), BF16)),
        grid=(s_dim // tm,),
        in_specs=[row(LAT_PAD), full(q_norm), full(kv_norm), full(wq_t), full(wkv_t), row(HEAD_PAD), row(HEAD_PAD), row(HEAD_PAD)],
        out_specs=(row(width), row(width), row(width), row(Q_LORA), row(KV_LORA)), name=name,
        compiler_params=_params(("parallel",), [((tm, LAT_PAD), F32), (wq_t.shape, BF16), (wkv_t.shape, BF16)]
                                + 3 * [((tm, width), BF16)], extra=4 * tm * width * 4),
    )(lat, q_norm, kv_norm, wq_t, wkv_t, *rope)


def _mla_scores(q, k_ref, t, tq):
    lo = t * tq
    own = slice(lo, lo + tq)
    scores = [(_dot_nt(q, k_ref[own, :]), own)]
    if t > 0:
        scores.append((_dot_nt(q, k_ref[0:lo, :]), slice(0, lo)))
    return scores


def _mla_softmax(scores):
    s_own = scores[0][0] * MLA_SCALE
    rows = lax.broadcasted_iota(jnp.int32, s_own.shape, 0)
    cols = lax.broadcasted_iota(jnp.int32, s_own.shape, 1)
    s_own = jnp.where(cols <= rows, s_own, -jnp.inf)
    mx = jnp.max(s_own, axis=-1, keepdims=True)
    if len(scores) == 1:
        e_own = jnp.exp(s_own - mx)
        return [(e_own * (1.0 / jnp.sum(e_own, axis=-1, keepdims=True)), scores[0][1])]
    s_pre = scores[1][0] * MLA_SCALE
    mx = jnp.maximum(mx, jnp.max(s_pre, axis=-1, keepdims=True))
    e_own, e_pre = jnp.exp(s_own - mx), jnp.exp(s_pre - mx)
    inv = 1.0 / (jnp.sum(e_own, axis=-1, keepdims=True) + jnp.sum(e_pre, axis=-1, keepdims=True))
    return [(e_pre * inv, scores[1][1]), (e_own * inv, scores[0][1])]


def _mla_attn_fwd(q, k, v, name):
    s_dim = q.shape[0]
    tq = MLA_QUERY_TILE

    def body(q_ref, k_ref, v_ref, o_ref):
        n_tiles = s_dim // tq
        tile_of = lambda t: slice(t * tq, (t + 1) * tq)
        def weighted_values(t, probs):
            o = None
            for p, keys in probs:
                part = _dot_nn(p, v_ref[keys, :])
                o = part if o is None else o + part
            o_ref[tile_of(t), :] = o.astype(BF16)

        scores = _mla_scores(q_ref[tile_of(0), :], k_ref, 0, tq)
        probs = None
        for t in range(n_tiles):
            ahead = _mla_scores(q_ref[tile_of(t + 1), :], k_ref, t + 1, tq) if t + 1 < n_tiles else None
            if probs is not None:
                weighted_values(t - 1, probs)
            probs = [(p.astype(BF16), keys) for p, keys in _mla_softmax(scores)]
            scores = ahead
        weighted_values(n_tiles - 1, probs)

    head = pl.BlockSpec((s_dim, HEAD_PAD), lambda h: (0, h))
    return _pcall(
        body, out_shape=jax.ShapeDtypeStruct(q.shape, BF16), grid=(MLA_HEADS,),
        in_specs=[head, head, head], out_specs=head, name=name,
        compiler_params=_params(("parallel",), 4 * [((s_dim, HEAD_PAD), BF16)], extra=4 * tq * s_dim * 4),
    )(q, k, v)


def _mla_attn_bwd(q, k, v, d_o, name):
    s_dim = q.shape[0]
    tq = MLA_QUERY_TILE

    def body(q_ref, k_ref, v_ref, do_ref, dq_ref, dk_ref, dv_ref):
        dk_ref[...] = jnp.zeros_like(dk_ref)
        dv_ref[...] = jnp.zeros_like(dv_ref)
        n_tiles = s_dim // tq
        tile_of = lambda t: slice(t * tq, (t + 1) * tq)

        def products(t):
            scores = _mla_scores(q_ref[tile_of(t), :], k_ref, t, tq)
            dot = do_ref[tile_of(t), :].astype(BF16)
            return scores, [_dot_nt(dot, v_ref[keys, :]) for _, keys in scores]

        def gradients_of_scores(scores, dps):
            probs = _mla_softmax(scores)
            dp_of = {(keys.start, keys.stop): dp for (_, keys), dp in zip(scores, dps)}
            terms = [(p, keys, dp_of[keys.start, keys.stop]) for p, keys in probs]
            row = None
            for p, _, dp in terms:
                part = jnp.sum(p * dp, axis=-1, keepdims=True)
                row = part if row is None else row + part
            return [((p * (dp - row) * MLA_SCALE).astype(BF16), p.astype(BF16), keys) for p, keys, dp in terms]

        def accumulate(t, terms):
            qt = q_ref[tile_of(t), :]
            dot = do_ref[tile_of(t), :].astype(BF16)
            dq = None
            for dsb, pb, keys in terms:
                part = _dot_nn(dsb, k_ref[keys, :])
                dq = part if dq is None else dq + part
                dk_ref[keys, :] += _dot_tn(dsb, qt)
                dv_ref[keys, :] += _dot_tn(pb, dot)
            dq_ref[tile_of(t), :] = dq

        ready = products(0)
        terms = None
        for t in range(n_tiles):
            ahead = products(t + 1) if t + 1 < n_tiles else None
            if terms is not None:
                accumulate(t - 1, terms)
            terms = gradients_of_scores(*ready)
            ready = ahead
        accumulate(n_tiles - 1, terms)

    head = pl.BlockSpec((s_dim, HEAD_PAD), lambda h: (0, h))
    out = jax.ShapeDtypeStruct(q.shape, F32)
    return _pcall(
        body, out_shape=(out, out, out), grid=(MLA_HEADS,),
        in_specs=[head, head, head, head], out_specs=(head, head, head), name=name,
        compiler_params=_params(("parallel",), 3 * [((s_dim, HEAD_PAD), BF16)] + 4 * [((s_dim, HEAD_PAD), F32)],
                                extra=6 * tq * s_dim * 4),
    )(q, k, v, d_o)


def _mla_qkv_bwd(dq, dk, dv, lat, q_norm, kv_norm, wq_t, wkv_t, rope, name):
    s_dim = lat.shape[0]
    width = MLA_HEADS * HEAD_PAD
    tm = 256

    def body(dq_ref, dk_ref, dv_ref, lat_ref, qg_ref, kg_ref, wq_ref, wkv_ref, c_ref, s1_ref, s2_ref,
             dqp_ref, dkv_ref, dlat_ref, dqg_ref, dkg_ref):
        @pl.when(pl.program_id(0) == 0)
        def _():
            dqg_ref[...] = jnp.zeros_like(dqg_ref)
            dkg_ref[...] = jnp.zeros_like(dkg_ref)

        c, s1, s2 = c_ref[...], s1_ref[...], s2_ref[...]
        lane = lax.broadcasted_iota(jnp.int32, (tm, HEAD_PAD), 1)
        low = lane < QK_NOPE
        rot = (lane >= QK_NOPE) & (lane < QK_NOPE + QK_ROPE)
        dkrr = jnp.zeros((tm, HEAD_PAD), F32)
        for h in range(MLA_HEADS):
            sl = slice(h * HEAD_PAD, (h + 1) * HEAD_PAD)
            dqp_ref[:, sl] = _rope_t(dq_ref[:, sl], c, s1, s2).astype(BF16)
            dkh = dk_ref[:, sl]
            dkv_ref[:, sl] = jnp.where(low, dkh, dv_ref[:, sl]).astype(BF16)
            dkrr = dkrr + jnp.where(rot, dkh, 0.0)
        dqn = _dot_nn(dqp_ref[...], wq_ref[...])
        dkvn = _dot_nn(dkv_ref[...], wkv_ref[...])
        cq = lat_ref[:, :Q_LORA]
        ckv = lat_ref[:, Q_LORA:Q_LORA + KV_LORA]
        rq, rkv = _rstd(cq), _rstd(ckv)
        dqg_ref[...] += jnp.sum(dqn * cq * rq, axis=0, keepdims=True)
        dkg_ref[...] += jnp.sum(dkvn * ckv * rkv, axis=0, keepdims=True)
        dlat_ref[:, :Q_LORA] = _rms_bwd(cq, rq, dqn * qg_ref[...])
        dlat_ref[:, Q_LORA:Q_LORA + KV_LORA] = _rms_bwd(ckv, rkv, dkvn * kg_ref[...])
        dlat_ref[:, Q_LORA + KV_LORA:] = _rope_t(dkrr, c, s1, s2)

    row = lambda n: pl.BlockSpec((tm, n), lambda i: (i, 0))
    full = lambda a: pl.BlockSpec(a.shape, lambda i: (0, 0))
    wide = jax.ShapeDtypeStruct((s_dim, width), BF16)
    return _pcall(
        body,
        out_shape=(wide, wide, jax.ShapeDtypeStruct((s_dim, LAT_PAD), F32),
                   jax.ShapeDtypeStruct(q_norm.shape, F32), jax.ShapeDtypeStruct(kv_norm.shape, F32)),
        grid=(s_dim // tm,),
        in_specs=[row(width), row(width), row(width), row(LAT_PAD), full(q_norm), full(kv_norm), full(wq_t), full(wkv_t),
                  row(HEAD_PAD), row(HEAD_PAD), row(HEAD_PAD)],
        out_specs=(row(width), row(width), row(LAT_PAD), full(q_norm), full(kv_norm)), name=name,
        compiler_params=_params(("arbitrary",), 3 * [((tm, width), F32)] + [((tm, LAT_PAD), F32), (wq_t.shape, BF16),
                                                                           (wkv_t.shape, BF16)] + 2 * [((tm, width), BF16)],
                                extra=2 * tm * width * 4),
    )(dq, dk, dv, lat, q_norm, kv_norm, wq_t, wkv_t, *rope)


def _t5_bucket(dist):
    max_exact = N_BUCKETS // 2
    d = jnp.maximum(dist, 1).astype(F32)
    large = max_exact + (jnp.log(d / max_exact) / math.log(MAX_DISTANCE / max_exact)
                         * (N_BUCKETS - max_exact)).astype(jnp.int32)
    large = jnp.minimum(large, N_BUCKETS - 1)
    return jnp.where(dist < max_exact, dist, large)


def _dil_buckets(dilation):
    iq = jnp.arange(DIL_BLOCK)[:, None]
    ik = jnp.arange(2 * DIL_BLOCK)[None, :]
    return _t5_bucket(jnp.maximum(DIL_BLOCK + iq - ik, 0) * dilation)


def _dil_logits(qh, kb, bias_h, first, span):
    if first:
        s = _dot_nt(qh, kb) * DIL_SCALE + bias_h[:, DIL_BLOCK:]
        rel = lax.broadcasted_iota(jnp.int32, s.shape, 0) - lax.broadcasted_iota(jnp.int32, s.shape, 1)
    else:
        s = _dot_nt(qh, kb) * DIL_SCALE + bias_h
        rel = DIL_BLOCK + lax.broadcasted_iota(jnp.int32, s.shape, 0) - lax.broadcasted_iota(jnp.int32, s.shape, 1)
    return jnp.where((rel >= 0) & (rel <= span), s, -jnp.inf)


def _dil_blocks(s_dim, dilation):
    rows = s_dim // dilation
    for r in range(dilation):
        for n in range(rows // DIL_BLOCK):
            lo = r * rows + n * DIL_BLOCK
            keys = slice(lo, lo + DIL_BLOCK) if n == 0 else slice(lo - DIL_BLOCK, lo + DIL_BLOCK)
            start = r + n * DIL_BLOCK * dilation
            tokens = slice(start, start + DIL_BLOCK) if dilation == 1 else pl.ds(start, DIL_BLOCK, stride=dilation)
            yield n == 0, slice(lo, lo + DIL_BLOCK), keys, tokens


def _dil_views(s_dim):
    col = lambda which: pl.BlockSpec((s_dim, HEAD_PAD), lambda p: (0, which * DIL_PAIRS + p))
    nat = pl.BlockSpec((s_dim, HEAD_PAD), lambda p: (0, p))
    bias = pl.BlockSpec((2, DIL_BLOCK, 2 * DIL_BLOCK), lambda p: (p, 0, 0))
    return col, nat, bias


def _dil_attn_fwd(qkv, bias, dilation, span, name):
    s_dim = qkv.shape[0]
    d_dim = DIL_HEADS * DIL_HEAD_DIM
    col, nat, bias_spec = _dil_views(s_dim)

    def body(q_ref, k_ref, v_ref, b_ref, o_ref, l_ref):
        lane = lax.broadcasted_iota(jnp.int32, (DIL_BLOCK, HEAD_PAD), 1)
        klane = lax.broadcasted_iota(jnp.int32, (2 * DIL_BLOCK, HEAD_PAD), 1)
        blocks = list(_dil_blocks(s_dim, dilation))
        for g0 in range(0, len(blocks), DIL_GROUPED):
            group = blocks[g0:g0 + DIL_GROUPED]
            logits = [_dil_logits(jnp.where((lane < DIL_HEAD_DIM) == (h == 0), q_ref[blk, :], 0), k_ref[keys, :], b_ref[h],
                                  first, span) for first, blk, keys, _ in group for h in range(2)]
            soft = []
            for lg in logits:
                mx = jnp.max(lg, axis=-1, keepdims=True)
                e = jnp.exp(lg - mx)
                tot = jnp.sum(e, axis=-1, keepdims=True)
                soft.append(((e * (1.0 / tot)).astype(BF16), mx + jnp.log(tot)))
            for i, (_, _, keys, tokens) in enumerate(group):
                vb = v_ref[keys, :]
                o_acc = jnp.zeros((DIL_BLOCK, HEAD_PAD), F32)
                lse_acc = jnp.zeros((DIL_BLOCK, HEAD_PAD), F32)
                for h in range(2):
                    p, lse = soft[2 * i + h]
                    kmine = (klane[:vb.shape[0]] < DIL_HEAD_DIM) == (h == 0)
                    o_acc = o_acc + _dot_nn(p, jnp.where(kmine, vb, 0))
                    lse_acc = jnp.where((lane < DIL_HEAD_DIM) == (h == 0), lse, lse_acc)
                o_ref[tokens, :] = o_acc
                l_ref[tokens, :] = lse_acc

    out = jax.ShapeDtypeStruct((s_dim, d_dim), F32)
    return _pcall(
        body, out_shape=(out, out), grid=(DIL_PAIRS,),
        in_specs=[col(0), col(1), col(2), bias_spec], out_specs=(nat, nat), name=name,
        compiler_params=_params(("parallel",), 3 * [((s_dim, HEAD_PAD), BF16)] + 2 * [((s_dim, HEAD_PAD), F32)]
                                + [((2, DIL_BLOCK, 2 * DIL_BLOCK), F32)], extra=2**21),
    )(qkv, qkv, qkv, bias)


def _dil_mix(lses, outs, name):
    s_dim, d_dim = outs[0].shape
    tm = TOKEN_TILE
    ng = len(outs)

    def body(*refs):
        ls = [refs[g][...] for g in range(ng)]
        mx = ls[0]
        for g in range(1, ng):
            mx = jnp.maximum(mx, ls[g])
        es = [jnp.exp(l - mx) for l in ls]
        tot = es[0]
        for g in range(1, ng):
            tot = tot + es[g]
        o = None
        for g in range(ng):
            al = es[g] / tot
            refs[2 * ng + g][...] = al
            t = al * refs[ng + g][...]
            o = t if o is None else o + t
        refs[3 * ng][...] = o
        refs[3 * ng + 1][...] = o.astype(BF16)

    row = pl.BlockSpec((tm, d_dim), lambda i: (i, 0))
    f = jax.ShapeDtypeStruct((s_dim, d_dim), F32)
    res = _pcall(
        body, out_shape=tuple(ng * [f] + [f, jax.ShapeDtypeStruct((s_dim, d_dim), BF16)]), grid=(s_dim // tm,),
        in_specs=2 * ng * [row], out_specs=tuple((ng + 2) * [row]), name=name,
        compiler_params=_params(("parallel",), (3 * ng + 2) * [((tm, d_dim), F32)], extra=4 * tm * d_dim * 4),
    )(*lses, *outs)
    return res[:ng], res[ng], res[ng + 1]


def _dil_attn_bwd(qkv, bias, d_o, o_mix, alpha, lse, dilation, span, name):
    s_dim = qkv.shape[0]
    d_dim = DIL_HEADS * DIL_HEAD_DIM
    col, nat, bias_spec = _dil_views(s_dim)

    def body(q_ref, k_ref, v_ref, b_ref, do_ref, om_ref, al_ref, l_ref, dq_ref, dk_ref, dv_ref, db_ref, dk_acc, dv_acc):
        db_ref[...] = jnp.zeros_like(db_ref)
        dk_acc[...] = jnp.zeros_like(dk_acc)
        dv_acc[...] = jnp.zeros_like(dv_acc)
        lane = lax.broadcasted_iota(jnp.int32, (DIL_BLOCK, HEAD_PAD), 1)
        klane = lax.broadcasted_iota(jnp.int32, (2 * DIL_BLOCK, HEAD_PAD), 1)
        blocks = list(_dil_blocks(s_dim, dilation))
        heads = [(lane < DIL_HEAD_DIM) == (h == 0) for h in range(2)]
        for g0 in range(0, len(blocks), DIL_GROUPED):
            group = blocks[g0:g0 + DIL_GROUPED]
            staged = []
            for first, blk, kv_rows, tokens in group:
                qb, kb, vb = q_ref[blk, :], k_ref[kv_rows, :], v_ref[kv_rows, :]
                dog = al_ref[tokens, :] * do_ref[tokens, :]
                row_term = dog * om_ref[tokens, :]
                lse_b = l_ref[tokens, :]
                for h in range(2):
                    qh = jnp.where(heads[h], qb, 0)
                    dogh = jnp.where(heads[h], dog, 0.0).astype(BF16)
                    staged.append((_dil_logits(qh, kb, b_ref[h], first, span), _dot_nt(dogh, vb), qh, dogh,
                                   jnp.max(jnp.where(heads[h], lse_b, -jnp.inf), axis=-1, keepdims=True),
                                   jnp.sum(jnp.where(heads[h], row_term, 0.0), axis=-1, keepdims=True)))
            grads = []
            for i, (logits, dp, qh, dogh, lse_h, row) in enumerate(staged):
                p = jnp.exp(logits - lse_h)
                ds = p * (dp - row)
                if group[i // 2][0]:
                    db_ref[i % 2, :, DIL_BLOCK:] += ds
                else:
                    db_ref[i % 2] += ds
                grads.append(((ds * DIL_SCALE).astype(BF16), p.astype(BF16), qh, dogh))
            for i, (_, blk, kv_rows, _) in enumerate(group):
                kb = k_ref[kv_rows, :]
                dq_acc = jnp.zeros((DIL_BLOCK, HEAD_PAD), F32)
                dk_blk = jnp.zeros((kb.shape[0], HEAD_PAD), F32)
                dv_blk = jnp.zeros((kb.shape[0], HEAD_PAD), F32)
                for h in range(2):
                    dsb, pb, qh, dogh = grads[2 * i + h]
                    kmine = (klane[:kb.shape[0]] < DIL_HEAD_DIM) == (h == 0)
                    dq_acc = dq_acc + _dot_nn(dsb, jnp.where(kmine, kb, 0))
                    dk_blk = dk_blk + _dot_tn(dsb, qh)
                    dv_blk = dv_blk + _dot_tn(pb, dogh)
                dq_ref[blk, :] = dq_acc.astype(BF16)
                dk_acc[kv_rows, :] += dk_blk
                dv_acc[kv_rows, :] += dv_blk
        dk_ref[...] = dk_acc[...].astype(BF16)
        dv_ref[...] = dv_acc[...].astype(BF16)

    grad = jax.ShapeDtypeStruct((s_dim, d_dim), BF16)
    return _pcall(
        body, out_shape=(grad, grad, grad, jax.ShapeDtypeStruct(bias.shape, F32)), grid=(DIL_PAIRS,),
        in_specs=[col(0), col(1), col(2), bias_spec, nat, nat, nat, nat],
        out_specs=(nat, nat, nat, bias_spec), name=name,
        scratch_shapes=[pltpu.VMEM((s_dim, HEAD_PAD), F32), pltpu.VMEM((s_dim, HEAD_PAD), F32)],
        compiler_params=_params(("parallel",), 6 * [((s_dim, HEAD_PAD), BF16)] + 4 * [((s_dim, HEAD_PAD), F32)]
                                + 2 * [((2, DIL_BLOCK, 2 * DIL_BLOCK), F32)], extra=2 * s_dim * HEAD_PAD * 4 + 2**21),
    )(qkv, qkv, qkv, bias, d_o, o_mix, alpha, lse)


def _bias_reduce(dbias, buckets, name):
    n_heads = dbias.shape[0]

    def body(db_ref, bk_ref, o_ref):
        ds, bk = db_ref[0], bk_ref[0]
        lane = lax.broadcasted_iota(jnp.int32, (8, HEAD_PAD), 1)
        acc = jnp.zeros((8, HEAD_PAD), F32)
        for b in range(N_BUCKETS):
            acc = jnp.where(lane == b, jnp.sum(jnp.where(bk == b, ds, 0.0)), acc)
        o_ref[0] = acc

    blk = (1, DIL_BLOCK, 2 * DIL_BLOCK)
    return _pcall(
        body, out_shape=jax.ShapeDtypeStruct((n_heads, 8, HEAD_PAD), F32), grid=(n_heads,),
        in_specs=[pl.BlockSpec(blk, lambda h: (h, 0, 0)), pl.BlockSpec(blk, lambda h: (h // DIL_HEADS, 0, 0))],
        out_specs=pl.BlockSpec((1, 8, HEAD_PAD), lambda h: (h, 0, 0)), name=name,
        compiler_params=_params(("parallel",), [(blk, F32), (blk, jnp.int32)], extra=2**20),
    )(dbias, buckets)


def _loss_grad(y, target, name):
    s_dim, d_dim = y.shape
    tm = TOKEN_TILE

    def body(y_ref, t_ref, dy_ref, l_ref):
        @pl.when(pl.program_id(0) == 0)
        def _():
            l_ref[...] = jnp.zeros_like(l_ref)

        err = y_ref[...] - t_ref[...]
        dy_ref[...] = err / d_dim
        sq = (err * err).reshape(tm // 8, 8, d_dim)
        l_ref[...] += 0.5 * jnp.sum(sq, axis=0) / d_dim

    row = pl.BlockSpec((tm, d_dim), lambda i: (i, 0))
    acc = pl.BlockSpec((8, d_dim), lambda i: (0, 0))
    return _pcall(
        body, out_shape=(jax.ShapeDtypeStruct((s_dim, d_dim), F32), jax.ShapeDtypeStruct((8, d_dim), F32)),
        grid=(s_dim // tm,), in_specs=[row, row], out_specs=(row, acc), name=name,
        compiler_params=_params(("arbitrary",), 3 * [((tm, d_dim), F32)], extra=2 * tm * d_dim * 4),
    )(y, target)


def _mod_fwd(c_all, w_mod, b_loc, name):
    depth, d_dim, n = w_mod.shape
    nb = c_all.shape[0]

    def body(c_ref, w_ref, b_ref, o_ref, s_ref):
        cv = c_ref[...]
        sc = cv * jax.nn.sigmoid(cv)
        s_ref[...] = sc
        o_ref[0] = _dot_nn(sc.astype(BF16), w_ref[0].astype(BF16)) + b_ref[0]

    return _pcall(
        body, out_shape=(jax.ShapeDtypeStruct((depth, nb, n), F32), jax.ShapeDtypeStruct((nb, d_dim), F32)), grid=(depth,),
        in_specs=[pl.BlockSpec((nb, d_dim), lambda i: (0, 0)), pl.BlockSpec((1, d_dim, n), lambda i: (i, 0, 0)),
                  pl.BlockSpec((1, 1, n), lambda i: (i, 0, 0))],
        out_specs=(pl.BlockSpec((1, nb, n), lambda i: (i, 0, 0)), pl.BlockSpec((nb, d_dim), lambda i: (0, 0))), name=name,
        compiler_params=_params(("arbitrary",), [((1, d_dim, n), F32)], extra=d_dim * n * 2 + 2**20),
    )(c_all, w_mod, b_loc.reshape(depth, 1, n))


def _sum_parts(parts, name, transpose=False):
    _, rows, cols = parts.shape
    unit = 128 if transpose else 16
    budget = (7 if transpose else 3) * 2**20
    fits = [t for t in range(unit, rows // 2 + 1, unit) if rows % t == 0 and NDEV * t * cols * parts.dtype.itemsize <= budget]
    tr = max(fits) if fits else rows

    def body(p_ref, o_ref):
        acc = p_ref[0].astype(F32)
        for k in range(1, NDEV):
            acc = acc + p_ref[k].astype(F32)
        o_ref[...] = acc.T if transpose else acc

    out_shape, out_block = ((cols, rows), (cols, tr)) if transpose else ((rows, cols), (tr, cols))
    return _pcall(
        body, out_shape=jax.ShapeDtypeStruct(out_shape, F32), grid=(rows // tr,),
        in_specs=[pl.BlockSpec((NDEV, tr, cols), lambda i: (0, i, 0))],
        out_specs=pl.BlockSpec(out_block, (lambda i: (0, i)) if transpose else (lambda i: (i, 0))),
        name=name, compiler_params=_params(("parallel",), [((NDEV, tr, cols), parts.dtype), (out_block, F32)], extra=2**22),
    )(parts)


def _adamw(w, g, m, v, name):
    shape = w.shape
    cols = shape[-1]
    rows = math.prod(shape[:-1])
    tr = rows
    for cand in (2048, 1024, 512, 256, 128, 64, 32, 16, 8):
        if rows % cand == 0 and rows > cand and cand * cols * 4 <= 2**21:
            tr = cand
            break

    def body(w_ref, g_ref, m_ref, v_ref, d_ref, mo_ref, vo_ref):
        gv = g_ref[...]
        mn = ADAM_B1 * m_ref[...] + (1.0 - ADAM_B1) * gv
        vn = ADAM_B2 * v_ref[...] + (1.0 - ADAM_B2) * (gv * gv)
        m_hat = mn / (1.0 - ADAM_B1 ** ADAM_STEP)
        v_hat = vn / (1.0 - ADAM_B2 ** ADAM_STEP)
        d_ref[...] = -ADAM_LR * (m_hat / (jnp.sqrt(v_hat) + ADAM_EPS) + ADAM_WD * w_ref[...])
        mo_ref[...] = mn
        vo_ref[...] = vn

    blk = pl.BlockSpec((tr, cols), lambda i: (i, 0))
    out = jax.ShapeDtypeStruct((rows, cols), F32)
    res = _pcall(
        body, out_shape=(out, out, out), grid=(rows // tr,), in_specs=4 * [blk], out_specs=(blk, blk, blk), name=name,
        compiler_params=_params(("parallel",), 7 * [((tr, cols), F32)], extra=4 * tr * cols * 4),
    )(*(a.reshape(rows, cols) for a in (w, g, m, v)))
    return tuple(r.reshape(shape) for r in res)


def _peers():
    x, y, c = lax.axis_index("x"), lax.axis_index("y"), lax.axis_index("c")
    flip = lambda v, f: 1 - v if f else v
    peers = []
    for f in range(1, NDEV):
        px, py, pc = flip(x, f & 4), flip(y, f & 2), flip(c, f & 1)
        peers.append(((px, py, pc), 4 * px + 2 * py + pc))
    return (x, y, c), 4 * x + 2 * y + c, peers


def _places():
    x, y, c = lax.axis_index("x"), lax.axis_index("y"), lax.axis_index("c")
    place = lambda px, py, pc: ((px, py, pc), 4 * px + 2 * py + pc)
    return place(x, y, c), place(x, y, 1 - c), [place(1 - x, y, c), place(x, 1 - y, c), place(1 - x, 1 - y, c)]


def _exchange(arrs, gather, name):
    n = len(arrs)
    hbm = pl.BlockSpec(memory_space=pltpu.HBM)
    if gather:
        out_shape = [jax.ShapeDtypeStruct((NDEV * a.shape[0], a.shape[1]), a.dtype) for a in arrs]
    else:
        out_shape = [jax.ShapeDtypeStruct((NDEV, a.shape[0] // NDEV, a.shape[1]), a.dtype) for a in arrs]

    def body(*refs):
        ins, outs = refs[:n], refs[n:2 * n]
        send_sems, recv_sems, local_sems = refs[2 * n:]
        me_pos, me, peers = _peers()
        local = []
        for k in range(n):
            rows = arrs[k].shape[0] if gather else arrs[k].shape[0] // NDEV
            if gather:
                src_of = lambda idx: ins[k]
                dst_of = lambda idx: outs[k].at[pl.ds(me * rows, rows)]
                mine = (ins[k], outs[k].at[pl.ds(me * rows, rows)])
            else:
                src_of = lambda idx: ins[k].at[pl.ds(idx * rows, rows)]
                dst_of = lambda idx: outs[k].at[me]
                mine = (ins[k].at[pl.ds(me * rows, rows)], outs[k].at[me])
            cp = pltpu.make_async_copy(mine[0], mine[1], local_sems.at[k])
            cp.start()
            local.append(cp)
            for pos, idx in peers:
                pltpu.make_async_remote_copy(src_ref=src_of(idx), dst_ref=dst_of(idx), send_sem=send_sems.at[k],
                                             recv_sem=recv_sems.at[k], device_id=pos, device_id_type=MESH).start()
        for k in range(n):
            rows = arrs[k].shape[0] if gather else arrs[k].shape[0] // NDEV
            sent = ins[k].at[pl.ds(0, (NDEV - 1) * rows)] if not gather else outs[k].at[pl.ds(0, (NDEV - 1) * rows)]
            got = outs[k].at[pl.ds(0, (NDEV - 1) * rows)] if gather else outs[k].at[pl.ds(0, NDEV - 1)]
            pltpu.make_async_remote_copy(src_ref=sent, dst_ref=sent, send_sem=send_sems.at[k], recv_sem=recv_sems.at[k],
                                         device_id=me_pos, device_id_type=MESH).wait_send()
            pltpu.make_async_remote_copy(src_ref=got, dst_ref=got, send_sem=send_sems.at[k], recv_sem=recv_sems.at[k],
                                         device_id=me_pos, device_id_type=MESH).wait_recv()
            local[k].wait()

    return pl.pallas_call(
        body, out_shape=out_shape, in_specs=n * [hbm], out_specs=n * [hbm], name=name,
        scratch_shapes=[pltpu.SemaphoreType.DMA((n,)), pltpu.SemaphoreType.DMA((n,)), pltpu.SemaphoreType.DMA((n,))],
        compiler_params=pltpu.CompilerParams(has_side_effects=True),
    )(*arrs)


_HBM = pl.BlockSpec(memory_space=pltpu.HBM)
_SEM = pl.BlockSpec(memory_space=pltpu.SEMAPHORE)
_DATAFLOW = pltpu.SideEffectType.DATAFLOW_SIDE_EFFECTING


def _split_start(srcs, groups, gather, name):
    n = len(srcs)
    if gather:
        lands = [lax.empty((NDEV * a.shape[0], a.shape[1]), a.dtype) for a in srcs]
    else:
        lands = [lax.empty((NDEV, a.shape[0] // NDEV, a.shape[1]), a.dtype) for a in srcs]
    n_sem = 3 * len(groups)

    def body(*refs):
        src_refs, land_refs = refs[:n], refs[n:2 * n]
        sems = refs[2 * n:2 * n + n_sem]
        token = refs[-1]
        (_, my), sibling, chips = _places()
        _, _, peers = _peers()
        targets = [sibling] + chips if gather else peers
        for g, members in enumerate(groups):
            for j, k in enumerate(members):
                _own_copy(src_refs[k], land_refs[k], sems[3 * g + 2].at[j], my, gather).start()
        for g, members in enumerate(groups):
            for j, k in enumerate(members):
                rows = srcs[k].shape[0] if gather else srcs[k].shape[0] // NDEV
                for pos, idx in targets:
                    src = src_refs[k] if gather else src_refs[k].at[pl.ds(idx * rows, rows)]
                    dst = land_refs[k].at[pl.ds(my * rows, rows)] if gather else land_refs[k].at[my]
                    pltpu.make_async_remote_copy(src_ref=src, dst_ref=dst, send_sem=sems[3 * g].at[j],
                                                 recv_sem=sems[3 * g + 1].at[j], device_id=pos, device_id_type=MESH).start()
        token[...] = jnp.zeros_like(token)

    out_shape = []
    for members in groups:
        out_shape += 3 * [pltpu.SemaphoreType.DMA((len(members),))]
    out_shape += [pltpu.HBM(a.shape, a.dtype) for a in srcs] + [pltpu.HBM(a.shape, a.dtype) for a in lands]
    out_shape.append(jax.ShapeDtypeStruct((8, 128), F32))
    res = pl.pallas_call(
        body, name=name, out_shape=tuple(out_shape), in_specs=2 * n * [_HBM],
        out_specs=tuple(n_sem * [_SEM] + 2 * n * [_HBM] + [pl.BlockSpec(memory_space=pltpu.VMEM)]),
        input_output_aliases={i: n_sem + i for i in range(2 * n)},
        compiler_params=pltpu.CompilerParams(has_side_effects=_DATAFLOW),
    )(*[pltpu.with_memory_space_constraint(a, pltpu.HBM) for a in list(srcs) + lands])
    sems = [tuple(res[3 * g:3 * g + 3]) for g in range(len(groups))]
    return sems, list(res[n_sem:n_sem + n]), list(res[n_sem + n:n_sem + 2 * n]), res[-1]


def _own_copy(src_ref, land_ref, sem, my, gather):
    if gather:
        rows = src_ref.shape[0]
        return pltpu.make_async_copy(src_ref, land_ref.at[pl.ds(my * rows, rows)], sem)
    rows = src_ref.shape[0] // NDEV
    return pltpu.make_async_copy(src_ref.at[pl.ds(my * rows, rows)], land_ref.at[my], sem)


def _wait_all(land_ref, blocks_per_dev, copies, send_sem, recv_sem, me_pos):
    part = land_ref.at[pl.ds(0, copies * blocks_per_dev)]
    pltpu.make_async_remote_copy(src_ref=part, dst_ref=part, send_sem=send_sem, recv_sem=recv_sem,
                                 device_id=me_pos, device_id_type=MESH).wait()


def _gather_forward(sems, srcs, lands, after, name):
    n = len(srcs)

    def body(*refs):
        land_refs = refs[n:2 * n]
        send_a, recv_a = refs[2 * n], refs[2 * n + 1]
        send_b, recv_b = refs[2 * n + 3], refs[2 * n + 4]
        token = refs[-1]
        (me_pos, _), sibling, chips = _places()
        for j in range(n):
            _wait_all(land_refs[j], lands[j].shape[0] // NDEV, 1 + OTHER_CHIPS, send_a.at[j], recv_a.at[j], me_pos)
        for j in range(n):
            rows = lands[j].shape[0] // NDEV
            for _, idx in chips:
                block = land_refs[j].at[pl.ds(idx * rows, rows)]
                pltpu.make_async_remote_copy(src_ref=block, dst_ref=block, send_sem=send_b.at[j], recv_sem=recv_b.at[j],
                                             device_id=sibling[0], device_id_type=MESH).start()
        token[...] = jnp.zeros_like(token)

    res = pl.pallas_call(
        body, name=name,
        out_shape=(pltpu.SemaphoreType.DMA((n,)), pltpu.SemaphoreType.DMA((n,)))
        + tuple(pltpu.HBM(a.shape, a.dtype) for a in list(srcs) + list(lands)) + (jax.ShapeDtypeStruct((8, 128), F32),),
        in_specs=2 * n * [_HBM] + [_SEM, _SEM, pl.BlockSpec(memory_space=pl.ANY)],
        out_specs=tuple([_SEM, _SEM] + 2 * n * [_HBM] + [pl.BlockSpec(memory_space=pltpu.VMEM)]),
        input_output_aliases={i: 2 + i for i in range(2 * n)},
        compiler_params=pltpu.CompilerParams(has_side_effects=_DATAFLOW),
    )(*srcs, *lands, sems[0], sems[1], after)
    return (res[0], res[1]), list(res[2:2 + n]), list(res[2 + n:2 + 2 * n]), res[-1]


def _split_wait(sems, srcs, lands, after, copies, gather, name):
    n = len(srcs)

    def body(*refs):
        src_refs, land_refs = refs[:n], refs[n:2 * n]
        send_sem, recv_sem, local_sem = refs[2 * n], refs[2 * n + 1], refs[2 * n + 2]
        (me_pos, my), _, _ = _places()
        for j in range(n):
            _wait_all(land_refs[j], lands[j].shape[0] // NDEV, copies, send_sem.at[j], recv_sem.at[j], me_pos)
            _own_copy(src_refs[j], land_refs[j], local_sem.at[j], my, gather).wait()

    res = pl.pallas_call(
        body, name=name, out_shape=tuple(pltpu.HBM(a.shape, a.dtype) for a in list(srcs) + list(lands)),
        in_specs=2 * n * [_HBM] + [_SEM, _SEM, _SEM, pl.BlockSpec(memory_space=pl.ANY)], out_specs=tuple(2 * n * [_HBM]),
        input_output_aliases={i: i for i in range(2 * n)},
        compiler_params=pltpu.CompilerParams(has_side_effects=_DATAFLOW),
    )(*srcs, *lands, sems[0], sems[1], sems[2], after)
    return list(res[n:])


def _chained(gate, mid, after):
    return gate if mid is None else gate + mid(after)[:1, :1]


def _ffn_fwd(x, norms, mod, w, mid=None):
    (pre_g, post_g), (shift, scale, gate), (wg_t, wu_t, wd) = norms, mod, w
    if not callable(wd):
        hn, g, u, a, x_out, f = _ffn_fwd_fused(x, pre_g, scale, shift, post_g, _chained(gate, mid, x), wg_t, wu_t, wd, "ffn_fwd")
        return x_out, (x, hn, g, u, a, f), (wg_t, wu_t, wd)
    hn, g, u, a = _ffn_up(x, pre_g, scale, shift, wg_t, wu_t, "ffn_up")
    wd = wd(a)
    x_out, f = _mm_post(a, wd, x, post_g, _chained(gate, mid, a), FFN_RES, "ffn_down")
    return x_out, (x, hn, g, u, a, f), (wg_t, wu_t, wd)


def _ffn_bwd(dx_out, saved, norms, mod, w, send=None):
    (pre_g, post_g), (_, scale, gate), (wg_t, wu_t, wd) = norms, mod, w
    x, hn, g, u, a, f = saved
    d_model = x.shape[1]
    if send is None:
        df, dg, du, dx, dgate, dpost, dshift, dscale, dpre = _ffn_bwd_fused(dx_out, saved, pre_g, post_g, scale, gate,
                                                                            wg_t, wu_t, wd, "ffn_bwd")
        return dx, (dpre, dpost), (dshift, dscale, dgate), tuple(_ffn_dw(dg, du, a, hn, df, "ffn_dw3"))
    sent = send
    df, dgate, dpost = _post_bwd(dx_out, f, post_g, gate, FFN_RES, "ffn_post_bwd")
    dwd = _mm([(a, df)], "tn", BF16, 256, d_model, "ffn_dw")
    dg, du = _ffn_dgu(df, wd, g, u, "ffn_dgu", after=sent(2, dwd))
    dwg_t = _mm([(dg, hn)], "tn", BF16, 256, d_model, "ffn_dw")
    dwu_t = _mm([(du, hn)], "tn", BF16, 256, d_model, "ffn_dw", after=sent(0, dwg_t))
    dhn = _mm([(dg, wg_t), (du, wu_t)], "nn", F32, TOKEN_TILE, d_model, "ffn_dhn", after=sent(1, dwu_t))
    dx, dshift, dscale, dpre = _prenorm_bwd(dx_out, [dhn], x, pre_g, scale, "prenorm_bwd")
    return dx, (dpre, dpost), (dshift, dscale, dgate), (dwg_t, dwu_t, dwd)


def _mla_fwd(x, norms, mod, w, rope, mid=None):
    (pre_g, post_g), (shift, scale, gate) = norms, mod
    w_in, q_norm, wq_t, kv_norm, wkv_t, wo = w
    hn, lat = _prenorm_mm(x, pre_g, scale, shift, w_in, "nn", F32, LAT_PAD, "mla_in")
    gate = _chained(gate, mid, lat)
    q, k, v, qn, kvn = _mla_qkv(lat, q_norm, kv_norm, wq_t, wkv_t, rope, "mla_qkv")
    o = _mla_attn_fwd(q, k, v, "mla_attn_fwd")
    x_out, f = _mm_post(o, wo, x, post_g, gate, 1.0, "mla_out")
    return x_out, (x, hn, lat, q, k, v, qn, kvn, o, f)


def _mla_bwd(dx_out, saved, norms, mod, w, rope):
    (pre_g, post_g), (_, scale, gate) = norms, mod
    w_in, q_norm, wq_t, kv_norm, wkv_t, wo = w
    x, hn, lat, q, k, v, qn, kvn, o, f = saved
    d_model = x.shape[1]
    df, dgate, dpost = _post_bwd(dx_out, f, post_g, gate, 1.0, "mix_post_bwd")
    d_o = _mm([(df, wo)], "nt", F32, TOKEN_TILE, wo.shape[0], "mla_do")
    dwo = _mm([(o, df)], "tn", BF16, TOKEN_TILE, d_model, "mla_dwo")
    dq, dk, dv = _mla_attn_bwd(q, k, v, d_o, "mla_attn_bwd")
    dqp, dkv, dlat, dq_norm, dkv_norm = _mla_qkv_bwd(dq, dk, dv, lat, q_norm, kv_norm, wq_t, wkv_t, rope, "mla_qkv_bwd")
    dwq_t = _mm([(dqp, qn)], "tn", BF16, TOKEN_TILE, Q_LORA, "mla_dwq")
    dwkv_t = _mm([(dkv, kvn)], "tn", BF16, TOKEN_TILE, KV_LORA, "mla_dwkv")
    dw_in = _mm([(hn, dlat)], "tn", BF16, TOKEN_TILE, LAT_PAD, "mla_dwin")
    dhn = _mm([(dlat, w_in)], "nt", F32, TOKEN_TILE, d_model, "mla_dhn")
    dx, dshift, dscale, dpre = _prenorm_bwd(dx_out, [dhn], x, pre_g, scale, "prenorm_bwd")
    return dx, (dpre, dpost), (dshift, dscale, dgate), (dw_in, dq_norm, dwq_t, dkv_norm, dwkv_t, dwo)


def _dil_fwd(x, norms, mod, w, bias, mid=None):
    (pre_g, post_g), (shift, scale, gate), (w_in_t, wo) = norms, mod, w
    width = 3 * DIL_HEADS * DIL_HEAD_DIM
    hns, qkvs, outs, lses = [], [], [], []
    for g, (window, dilation) in enumerate(DIL_GROUPS):
        hn, qkv = _prenorm_mm(x, pre_g, scale, shift, w_in_t, "nt", BF16, width, "dil_in", perm=dilation,
                              w_rows=(g * width, width))
        if g == 0:
            gate = _chained(gate, mid, qkv)
        o, lse = _dil_attn_fwd(qkv, bias[g], dilation, window // dilation, "dil_attn_fwd")
        hns.append(hn), qkvs.append(qkv), outs.append(o), lses.append(lse)
    alphas, o_mix, o_mix_b = _dil_mix(lses, outs, "dil_mix")
    x_out, f = _mm_post(o_mix_b, wo, x, post_g, gate, 1.0, "dil_out")
    return x_out, (x, hns, qkvs, lses, alphas, o_mix, o_mix_b, f)


def _dil_bwd(dx_out, saved, norms, mod, w, bias):
    (pre_g, post_g), (_, scale, gate), (w_in_t, wo) = norms, mod, w
    x, hns, qkvs, lses, alphas, o_mix, o_mix_b, f = saved
    d_model = x.shape[1]
    inner = DIL_HEADS * DIL_HEAD_DIM
    df, dgate, dpost = _post_bwd(dx_out, f, post_g, gate, 1.0, "mix_post_bwd")
    d_o = _mm([(df, wo)], "nt", F32, TOKEN_TILE, inner, "dil_do")
    dwo = _mm([(o_mix_b, df)], "tn", BF16, TOKEN_TILE, d_model, "dil_dwo")
    dhns, dws, dbs = [], [], []
    for g, (window, dilation) in enumerate(DIL_GROUPS):
        grads = _dil_attn_bwd(qkvs[g], bias[g], d_o, o_mix, alphas[g], lses[g], dilation, window // dilation, "dil_attn_bwd")
        dbs.append(grads[3])
        dhns.append(_mm([(grads[j], w_in_t) for j in range(3)], "nn", F32, TOKEN_TILE, d_model, "dil_dhn", out_perm=dilation,
                        b_rows=[(3 * g + j) * inner for j in range(3)]))
        dws += list(_mm_tn_shared(list(grads[:3]), hns[g], "dil_dwin"))
    dx, dshift, dscale, dpre = _prenorm_bwd(dx_out, dhns, x, pre_g, scale, "prenorm_bwd3")
    return dx, (dpre, dpost), (dshift, dscale, dgate), (jnp.concatenate(dws, axis=0), dwo), jnp.concatenate(dbs, axis=0)


def _pad_rows(a, rows):
    return jnp.pad(a, ((0, rows - a.shape[0]), (0, 0)))


def _lanes(a):
    flat = a.reshape(-1).astype(F32)
    rows = -(-flat.shape[0] // 1024) * 8
    return jnp.pad(flat, (0, rows * 128 - flat.shape[0])).reshape(rows, 128)


def kernel(x, c, norm_pre, norm_post, w_mod, b_mod, ffn_w_gate, ffn_w_up, ffn_w_down, mla_w_in, mla_q_norm, mla_w_q_up, mla_kv_norm, mla_w_kv_up, mla_w_o, dil_w_in, dil_w_o, rel_bias, loss_target, m_norm_pre, m_norm_post, m_w_mod, m_b_mod, m_ffn_w_gate, m_ffn_w_up, m_ffn_w_down, m_mla_w_in, m_mla_q_norm, m_mla_w_q_up, m_mla_kv_norm, m_mla_w_kv_up, m_mla_w_o, m_dil_w_in, m_dil_w_o, m_rel_bias, v_norm_pre, v_norm_post, v_w_mod, v_b_mod, v_ffn_w_gate, v_ffn_w_up, v_ffn_w_down, v_mla_w_in, v_mla_q_norm, v_mla_w_q_up, v_mla_kv_norm, v_mla_w_kv_up, v_mla_w_o, v_dil_w_in, v_dil_w_o, v_rel_bias):
    me = 4 * lax.axis_index("x") + 2 * lax.axis_index("y") + lax.axis_index("c")
    depth, n_sub, d_loc = norm_pre.shape
    d_model = x.shape[2]
    mod_loc_cols = w_mod.shape[2]
    x0, target = x[0], loss_target[0]

    bf_t = lambda a: a.astype(BF16).T
    ffn_ids = [(i, h) for i in range(depth) for h in range(2)]
    shards = []
    for i, h in ffn_ids:
        shards += [bf_t(ffn_w_gate[i, h]), bf_t(ffn_w_up[i, h]), ffn_w_down[i, h].astype(BF16)]
    shards += [mla_w_in[0].astype(BF16), bf_t(mla_w_q_up[0]), bf_t(mla_w_kv_up[0]), mla_w_o[0].astype(BF16),
               bf_t(dil_w_in[0]), dil_w_o[0].astype(BF16)]
    n_ffn = 3 * len(ffn_ids)
    members = {(0, 0): [0, 1, 2], (0, 1): [n_ffn, n_ffn + 1, n_ffn + 2, n_ffn + 3], (0, 2): [3, 4, 5],
               (1, 0): [6, 7, 8], (1, 1): [n_ffn + 4, n_ffn + 5], (1, 2): [9, 10, 11]}
    order = [(i, s) for i in range(depth) for s in range(n_sub)]

    small = jnp.concatenate([c.reshape(8, 128), _pad_rows(norm_pre.reshape(depth * n_sub, d_loc), 8),
                             _pad_rows(norm_post.reshape(depth * n_sub, d_loc), 8)], axis=0)
    small_all = _exchange([small], True, "gather_small")[0].reshape(NDEV, 24, 128)
    c_all = small_all[:, 0:8].reshape(NDEV, d_model)
    gains = lambda lo: jnp.transpose(small_all[:, lo:lo + depth * n_sub], (1, 0, 2)).reshape(depth, n_sub, 1, d_model)
    pre_full, post_full = gains(8), gains(16)

    b_loc = lax.dynamic_slice(b_mod, (0, me * mod_loc_cols), (depth, mod_loc_cols))
    mod_cols, silu_c = _mod_fwd(c_all, w_mod, b_loc, "mod_fwd")
    mod_all = _exchange([mod_cols.reshape(depth * NDEV, mod_loc_cols)], True, "gather_mod")[0]
    mod_all = mod_all.reshape(NDEV, depth, NDEV, mod_loc_cols)
    mod_mine = lax.dynamic_index_in_dim(mod_all, me, axis=2, keepdims=False)
    mod = jnp.transpose(mod_mine, (1, 0, 2)).reshape(depth, n_sub, 3, 1, d_model)

    shards[0], _ = lax.optimization_barrier((shards[0], mod_all))
    first = order[0]
    stages = [("%d%d" % first, members[first][:2]), ("%d%dd" % first, members[first][2:])]
    stages += [("%d%d" % key, members[key]) for key in order[1:]]
    stage_names = [name for name, _ in stages]
    g_sems, g_srcs, g_lands, g_token = _split_start(shards, [idx for _, idx in stages], True, "gather_weights_start")

    forwarded = {}

    def forward(stage, after):
        idx = stages[stage_names.index(stage)][1]
        forwarded[stage] = _gather_forward(g_sems[stage_names.index(stage)], [g_srcs[k] for k in idx],
                                           [g_lands[k] for k in idx], after, "gather_forward_" + stage)
        return forwarded[stage][3]

    def weights_of(stage, after):
        (send_b, recv_b), srcs, lands, _ = forwarded[stage]
        local = g_sems[stage_names.index(stage)][2]
        return _split_wait((send_b, recv_b, local), srcs, lands, after, OTHER_CHIPS, True, "gather_wait_" + stage)

    def late_down(after):
        forward("%d%dd" % first, after)
        return weights_of("%d%dd" % first, after)[0]

    lat_real = Q_LORA + KV_LORA
    qk = QK_NOPE + QK_ROPE

    def mla_weights(after):
        w_in, wq_t, wkv_t, wo = weights_of("01", after)
        w_in_pad = jnp.concatenate([w_in[:, :lat_real], jnp.zeros((d_model, QK_NOPE), BF16), w_in[:, lat_real:],
                                    jnp.zeros((d_model, HEAD_PAD - QK_NOPE - QK_ROPE), BF16)], axis=1)
        wq_pad = jnp.pad(wq_t.reshape(MLA_HEADS, qk, Q_LORA), ((0, 0), (0, HEAD_PAD - qk), (0, 0)))
        wo_pad = jnp.pad(wo.reshape(MLA_HEADS, V_HEAD, d_model), ((0, 0), (HEAD_PAD - V_HEAD, 0), (0, 0)))
        return (w_in_pad, mla_q_norm, wq_pad.reshape(MLA_HEADS * HEAD_PAD, Q_LORA), mla_kv_norm, wkv_t,
                wo_pad.reshape(MLA_HEADS * HEAD_PAD, d_model))

    zero = g_token[0, 0]
    rope = _rope_tables(zero)
    buckets = jnp.stack([_dil_buckets(dil) for _, dil in DIL_GROUPS]) + zero.astype(jnp.int32)
    onehot = (buckets[..., None] == jnp.arange(N_BUCKETS)).astype(F32)
    bias = jnp.einsum("gqkb,bgh->ghqk", onehot, rel_bias.reshape(N_BUCKETS, len(DIL_GROUPS), DIL_HEADS),
                      precision=lax.Precision.HIGHEST)

    norms = lambda i, s: (pre_full[i, s], post_full[i, s])
    mods = lambda i, s: (mod[i, s, 0], mod[i, s, 1], mod[i, s, 2])
    saved, weights = {}, {}
    h = lax.optimization_barrier((x0, bias, buckets, *rope))[0]
    forward("%d%d" % first, h)
    for n, (i, s) in enumerate(order):
        got = mla_weights(h) if (s == 1 and i % 2 == 0) else tuple(weights_of("%d%d" % (i, s), h))
        mid = None if n + 1 == len(order) else (lambda after, nxt="%d%d" % order[n + 1]: forward(nxt, after))
        if s != 1:
            if len(got) == 3:
                h, saved[i, s], weights[i, s] = _ffn_fwd(h, norms(i, s), mods(i, s), got)
                if mid is not None:
                    mid(h)
            else:
                h, saved[i, s], weights[i, s] = _ffn_fwd(h, norms(i, s), mods(i, s), (*got, late_down), mid)
            continue
        weights[i, s] = got
        if i % 2 == 0:
            h, saved[i, s] = _mla_fwd(h, norms(i, s), mods(i, s), weights[i, s], rope, mid)
        else:
            h, saved[i, s] = _dil_fwd(h, norms(i, s), mods(i, s), weights[i, s], bias, mid)
    dh, loss_parts = _loss_grad(h, target, "loss")

    dnorm, dmod, sent = {}, {}, {}
    token = jnp.zeros((8, 128), F32)
    last = order[0]

    def send_last(j, dw):
        sent[last, j] = _split_start([dw], [[0]], False, "scatter_start_%d%d_%d" % (*last, j))
        return sent[last, j][3]

    for i, s in reversed(order):
        md = mods(i, s)
        md = (md[0], md[1], md[2] + token[:1, :1])
        if (i, s) == last:
            dh, dnorm[i, s], dmod[i, s], _ = _ffn_bwd(dh, saved[i, s], norms(i, s), md, weights[i, s], send_last)
            continue
        if s != 1:
            dh, dnorm[i, s], dmod[i, s], dws = _ffn_bwd(dh, saved[i, s], norms(i, s), md, weights[i, s])
        elif i % 2 == 0:
            dh, dnorm[i, s], dmod[i, s], dmla = _mla_bwd(dh, saved[i, s], norms(i, s), md, weights[i, s], rope)
            dw_in_pad, dq_norm, dwq_pad, dkv_norm, dwkv_t, dwo_pad = dmla
            dw_in = jnp.concatenate([dw_in_pad[:, :lat_real], dw_in_pad[:, lat_real + QK_NOPE:lat_real + qk]], axis=1)
            dwq_t = dwq_pad.reshape(MLA_HEADS, HEAD_PAD, Q_LORA)[:, :qk].reshape(MLA_HEADS * qk, Q_LORA)
            dwo = dwo_pad.reshape(MLA_HEADS, HEAD_PAD, d_model)[:, HEAD_PAD - V_HEAD:].reshape(MLA_HEADS * V_HEAD, d_model)
            dws = (dw_in, dwq_t, dwkv_t, dwo)
        else:
            dh, dnorm[i, s], dmod[i, s], dws, dbias = _dil_bwd(dh, saved[i, s], norms(i, s), md, weights[i, s], bias)
        sent[i, s] = _split_start(list(dws), [list(range(len(dws)))], False, "scatter_start_%d%d" % (i, s))
        token = sent[i, s][3]
    grad_x = dh[None]

    mine = {}
    transposed = {3 * n + j for n in range(len(ffn_ids)) for j in (0, 1)} | {n_ffn + 1, n_ffn + 2, n_ffn + 4}
    for key in reversed(order[1:]):
        sems, srcs, lands, _ = sent[key]
        parts = _split_wait(sems[0], srcs, lands, dh, NDEV - 1, False, "scatter_wait_%d%d" % key)
        for k, p in zip(members[key], parts):
            mine[k] = _sum_parts(p, "sum_parts", k in transposed)
    g_mla_in, g_q_up, g_kv_up, g_mla_o, g_dil_in, g_dil_o = (mine[k] for k in range(n_ffn, n_ffn + 6))
    g_mla_in, g_q_up, g_kv_up, g_mla_o = g_mla_in[None], g_q_up[None], g_kv_up[None], g_mla_o[None]
    g_dil_in, g_dil_o = g_dil_in[None], g_dil_o[None]
    early = {"mla_w_in": _adamw(mla_w_in, g_mla_in, m_mla_w_in, v_mla_w_in, "adamw"),
             "mla_w_q_up": _adamw(mla_w_q_up, g_q_up, m_mla_w_q_up, v_mla_w_q_up, "adamw"),
             "mla_w_kv_up": _adamw(mla_w_kv_up, g_kv_up, m_mla_w_kv_up, v_mla_w_kv_up, "adamw"),
             "mla_w_o": _adamw(mla_w_o, g_mla_o, m_mla_w_o, v_mla_w_o, "adamw"),
             "dil_w_in": _adamw(dil_w_in, g_dil_in, m_dil_w_in, v_dil_w_in, "adamw"),
             "dil_w_o": _adamw(dil_w_o, g_dil_o, m_dil_w_o, v_dil_w_o, "adamw")}
    dbias_sums = _bias_reduce(dbias, buckets, "bias_reduce")
    tied = lax.optimization_barrier((dbias_sums, *[a for step in early.values() for a in step]))
    dbias_sums, early = tied[0], {name: tuple(tied[1 + 3 * n:4 + 3 * n]) for n, name in enumerate(early)}
    for j in (2, 0, 1):
        sems, srcs, lands, _ = sent[last, j]
        parts = _split_wait(sems[0], srcs, lands, dbias_sums, NDEV - 1, False, "scatter_wait_%d%d_%d" % (*last, j))
        mine[members[last][j]] = _sum_parts(parts[0], "sum_parts", members[last][j] in transposed)
    g_gate = jnp.stack([mine[3 * n] for n in range(len(ffn_ids))]).reshape(ffn_w_gate.shape)
    g_up = jnp.stack([mine[3 * n + 1] for n in range(len(ffn_ids))]).reshape(ffn_w_up.shape)
    g_down = jnp.stack([mine[3 * n + 2] for n in range(len(ffn_ids))]).reshape(ffn_w_down.shape)

    dmod_mine = jnp.concatenate([jnp.concatenate(dmod[i, s], axis=0) for i in range(depth) for s in range(n_sub)], axis=0)
    dpre_mine = jnp.concatenate([dnorm[i, s][0] for i in range(depth) for s in range(n_sub)], axis=0)
    dpost_mine = jnp.concatenate([dnorm[i, s][1] for i in range(depth) for s in range(n_sub)], axis=0)
    dbias_tab = dbias_sums[:, 0, :N_BUCKETS].T
    pieces = [dmod_mine, dpre_mine, dpost_mine, dq_norm, dkv_norm, dbias_tab, jnp.sum(loss_parts).reshape(1, 1)]
    packed = [_lanes(p) for p in pieces]
    offs = [0]
    for p in packed:
        offs.append(offs[-1] + p.shape[0])
    everyone = _exchange([jnp.concatenate(packed, axis=0)], True, "gather_small_grads")[0].reshape(NDEV, offs[-1], 128)
    total = _sum_parts(everyone, "sum_small")
    take = lambda n, shape: total[offs[n]:offs[n + 1]].reshape(-1)[:math.prod(shape)].reshape(shape)
    g_b_mod = take(0, b_mod.shape)
    col0 = me * d_loc
    g_norm_pre = lax.dynamic_slice(take(1, (depth, n_sub, d_model)), (0, 0, col0), norm_pre.shape)
    g_norm_post = lax.dynamic_slice(take(2, (depth, n_sub, d_model)), (0, 0, col0), norm_post.shape)
    g_q_norm, g_kv_norm = take(3, mla_q_norm.shape), take(4, mla_kv_norm.shape)
    g_rel_bias = take(5, rel_bias.shape)
    loss = take(6, ())

    dmod_all = everyone[:, offs[0]:offs[1]].reshape(NDEV, depth, NDEV * mod_loc_cols)
    dmod_cols = lax.dynamic_slice(dmod_all, (0, 0, me * mod_loc_cols), (NDEV, depth, mod_loc_cols))
    silu_t = jnp.pad(silu_c.T, ((0, 0), (0, HEAD_PAD - NDEV)))
    g_w_mod = jnp.stack([_mm([(silu_t, jnp.pad(dmod_cols[:, i], ((0, HEAD_PAD - NDEV), (0, 0))))], "nn", F32, TOKEN_TILE,
                             mod_loc_cols, "mod_bwd") for i in range(depth)])

    ws = (norm_pre, norm_post, w_mod, b_mod, ffn_w_gate, ffn_w_up, ffn_w_down, mla_w_in, mla_q_norm, mla_w_q_up, mla_kv_norm,
          mla_w_kv_up, mla_w_o, dil_w_in, dil_w_o, rel_bias)
    gs = (g_norm_pre, g_norm_post, g_w_mod, g_b_mod, g_gate, g_up, g_down, g_mla_in, g_q_norm, g_q_up, g_kv_norm, g_kv_up,
          g_mla_o, g_dil_in, g_dil_o, g_rel_bias)
    ms = (m_norm_pre, m_norm_post, m_w_mod, m_b_mod, m_ffn_w_gate, m_ffn_w_up, m_ffn_w_down, m_mla_w_in, m_mla_q_norm,
          m_mla_w_q_up, m_mla_kv_norm, m_mla_w_kv_up, m_mla_w_o, m_dil_w_in, m_dil_w_o, m_rel_bias)
    vs = (v_norm_pre, v_norm_post, v_w_mod, v_b_mod, v_ffn_w_gate, v_ffn_w_up, v_ffn_w_down, v_mla_w_in, v_mla_q_norm,
          v_mla_w_q_up, v_mla_kv_norm, v_mla_w_kv_up, v_mla_w_o, v_dil_w_in, v_dil_w_o, v_rel_bias)
    names = ("norm_pre", "norm_post", "w_mod", "b_mod", "ffn_w_gate", "ffn_w_up", "ffn_w_down", "mla_w_in", "mla_q_norm",
             "mla_w_q_up", "mla_kv_norm", "mla_w_kv_up", "mla_w_o", "dil_w_in", "dil_w_o", "rel_bias")
    stepped = [early[n] if n in early else _adamw(w, g, m, v, "adamw") for n, w, g, m, v in zip(names, ws, gs, ms, vs)]
    deltas, new_m, new_v = zip(*stepped)
    return (loss, grad_x, *gs, *deltas, *new_m, *new_v)
```

```python
import math

import jax
import jax.numpy as jnp
from jax import lax
from jax.experimental import pallas as pl
from jax.experimental.pallas import tpu as pltpu

F32 = jnp.float32
BF16 = jnp.bfloat16
MESH = pl.DeviceIdType.MESH

NDEV = 8
OTHER_CHIPS = 3
D_MODEL = 1024
SEQ = 2048
D_FF = 2816
EPS = 1e-6
FFN_RES = 0.5
FFN_CHUNKS = 11

MLA_HEADS = 16
Q_LORA = 384
KV_LORA = 256
QK_NOPE = 64
QK_ROPE = 32
V_HEAD = 64
ROPE_THETA = 10000.0
HEAD_PAD = 128
LAT_PAD = Q_LORA + KV_LORA + HEAD_PAD
MLA_SCALE = (QK_NOPE + QK_ROPE) ** -0.5
MLA_QUERY_TILE = 256

DIL_GROUPS = ((128, 1), (512, 4), (2048, 16))
DIL_HEADS = 16
DIL_HEAD_DIM = 64
DIL_BLOCK = 128
DIL_PAIRS = DIL_HEADS // 2
DIL_SCALE = DIL_HEAD_DIM ** -0.5
DIL_GROUPED = 8
N_BUCKETS = 32
MAX_DISTANCE = 2048

ADAM_LR = 0.001
ADAM_B1 = 0.9
ADAM_B2 = 0.999
ADAM_EPS = 1e-08
ADAM_WD = 0.01
ADAM_STEP = 10

V7X_VMEM_BYTES = 64 * 2**20
VMEM_RESERVE = 10 * 2**20
TOKEN_TILE = 512


def _nbytes(shape, dtype):
    return math.prod(shape) * jnp.dtype(dtype).itemsize


def _params(semantics, blocks, extra=0):
    need = 2 * sum(_nbytes(s, d) for s, d in blocks) + extra + VMEM_RESERVE
    return pltpu.CompilerParams(dimension_semantics=semantics,
                                vmem_limit_bytes=int(min(need, V7X_VMEM_BYTES - VMEM_RESERVE)))


def _pcall(body, out_shape, **kw):
    call = pl.pallas_call(body, out_shape=jax.tree.map(lambda s: pltpu.HBM(s.shape, s.dtype), out_shape), **kw)
    return lambda *args: call(*[pltpu.with_memory_space_constraint(a, pltpu.HBM) for a in args])


def _dot_nn(a, b):
    return lax.dot_general(a, b, (((1,), (0,)), ((), ())), preferred_element_type=F32)


def _dot_nt(a, b):
    return lax.dot_general(a, b, (((1,), (1,)), ((), ())), preferred_element_type=F32)


def _dot_tn(a, b):
    return lax.dot_general(a, b, (((0,), (0,)), ((), ())), preferred_element_type=F32)


_DOTS = {"nn": _dot_nn, "nt": _dot_nt, "tn": _dot_tn}


def _rstd(v):
    return lax.rsqrt(jnp.mean(v * v, axis=-1, keepdims=True) + EPS)


def _rms_bwd(v, r, t):
    return r * t - v * (r * r * r) * jnp.mean(t * v, axis=-1, keepdims=True)


_TOKEN_SPEC = pl.BlockSpec((8, 128), lambda *_: (0, 0))


def _mm(pairs, mode, out_dtype, tm, tn, name, out_perm=1, after=None, b_rows=None):
    a0, b0 = pairs[0]
    m_dim = a0.shape[1] if mode == "tn" else a0.shape[0]
    n_dim = b0.shape[0] if mode == "nt" else b0.shape[1]
    tm, tn = min(tm, m_dim // out_perm), min(tn, n_dim)
    assert m_dim % tm == 0 and n_dim % tn == 0, (name, m_dim, n_dim, tm, tn)
    dot = _DOTS[mode]
    npairs = len(pairs)

    def body(*refs):
        acc = None
        for p in range(npairs):
            d = dot(refs[2 * p][...].astype(BF16), refs[2 * p + 1][...].astype(BF16))
            acc = d if acc is None else acc + d
        refs[-1][...] = acc.astype(out_dtype)

    in_specs, blocks, flat = [], [], []
    for n_pair, (a, b) in enumerate(pairs):
        if mode == "nn":
            k = a.shape[1]
            first_block = 0 if b_rows is None else b_rows[n_pair] // k
            sa, sb = ((tm, k), lambda i, j: (i, 0)), ((k, tn), lambda i, j, o=first_block: (o, j))
        elif mode == "nt":
            k = a.shape[1]
            sa, sb = ((tm, k), lambda i, j: (i, 0)), ((tn, k), lambda i, j: (j, 0))
        else:
            k = a.shape[0]
            sa, sb = ((k, tm), lambda i, j: (0, i)), ((k, tn), lambda i, j: (0, j))
        in_specs += [pl.BlockSpec(*sa), pl.BlockSpec(*sb)]
        blocks += [(sa[0], a.dtype), (sb[0], b.dtype)]
        flat += [a, b]
    if after is not None:
        in_specs.append(_TOKEN_SPEC)
        flat.append(after)
    if out_perm == 1:
        out_shape = (m_dim, n_dim)
        out_spec = pl.BlockSpec((tm, tn), lambda i, j: (i, j))
    else:
        rows = m_dim // out_perm
        assert tn == n_dim and rows % tm == 0, (name, rows, tm)
        nb = rows // tm
        out_shape = (rows, out_perm * n_dim)
        out_spec = pl.BlockSpec((tm, n_dim), lambda i, j: (i % nb, i // nb))
    blocks.append(((tm, tn), out_dtype))
    res = _pcall(
        body, out_shape=jax.ShapeDtypeStruct(out_shape, out_dtype), grid=(m_dim // tm, n_dim // tn),
        in_specs=in_specs, out_specs=out_spec, name=name,
        compiler_params=_params(("parallel", "parallel"), blocks, extra=2 * tm * tn * 4),
    )(*flat)
    return res.reshape(m_dim, n_dim)


def _prenorm_mm(x, pre_g, scale, shift, w, w_mode, out_dtype, tn, name, perm=1, w_rows=None):
    s_dim, d_dim = x.shape
    n_dim = w.shape[0] if w_mode == "nt" else w.shape[1]
    w_first = 0
    if w_rows is not None:
        w_first, n_dim = w_rows
    rows = s_dim // perm
    side = max(1, TOKEN_TILE // rows)
    tm = side * min(TOKEN_TILE, rows)
    nb = max(1, rows // tm)
    tn = min(tn, n_dim)
    assert n_dim % tn == 0 and w_first % tn == 0
    w_block0 = w_first // tn
    dot = _DOTS[w_mode]

    def body(x_ref, g_ref, sc_ref, sh_ref, w_ref, hn_ref, o_ref):
        @pl.when(pl.program_id(1) == 0)
        def _():
            xf = x_ref[...]
            if side > 1:
                xf = jnp.concatenate([xf[:, c * d_dim:(c + 1) * d_dim] for c in range(side)], axis=0)
            hn = (xf * _rstd(xf) * g_ref[...]) * (1.0 + sc_ref[...]) + sh_ref[...]
            hn_ref[...] = hn.astype(BF16)

        o_ref[...] = dot(hn_ref[...], w_ref[...]).astype(out_dtype)

    vec = pl.BlockSpec((1, d_dim), lambda i, j: (0, 0))
    w_block = (tn, d_dim) if w_mode == "nt" else (d_dim, tn)
    w_spec = pl.BlockSpec(w_block, (lambda i, j: (w_block0 + j, 0)) if w_mode == "nt" else (lambda i, j: (0, j)))
    hn, out = _pcall(
        body,
        out_shape=(jax.ShapeDtypeStruct((s_dim, d_dim), BF16), jax.ShapeDtypeStruct((s_dim, n_dim), out_dtype)),
        grid=(s_dim // tm, n_dim // tn),
        in_specs=[pl.BlockSpec((tm // side, side * d_dim), lambda i, j: (i % nb, i // nb)), vec, vec, vec, w_spec],
        out_specs=(pl.BlockSpec((tm, d_dim), lambda i, j: (i, 0)), pl.BlockSpec((tm, tn), lambda i, j: (i, j))),
        name=name,
        compiler_params=_params(("parallel", "arbitrary"),
                                [((tm, d_dim), F32), (w_block, BF16), ((tm, d_dim), BF16), ((tm, tn), out_dtype)],
                                extra=3 * tm * d_dim * 4 + tm * tn * 4),
    )(x.reshape(rows, perm * d_dim), pre_g, scale, shift, w)
    return hn, out


def _ffn_up(x, pre_g, scale, shift, wg_t, wu_t, name):
    s_dim, d_dim = x.shape
    f_dim = wg_t.shape[0]
    tm, tn = TOKEN_TILE, f_dim // 2

    def body(x_ref, g_ref, sc_ref, sh_ref, wg_ref, wu_ref, hn_ref, go_ref, uo_ref, a_ref):
        @pl.when(pl.program_id(1) == 0)
        def _():
            xf = x_ref[...]
            hn = (xf * _rstd(xf) * g_ref[...]) * (1.0 + sc_ref[...]) + sh_ref[...]
            hn_ref[...] = hn.astype(BF16)

        hn = hn_ref[...]
        g = _dot_nt(hn, wg_ref[...])
        u = _dot_nt(hn, wu_ref[...])
        go_ref[...] = g.astype(BF16)
        uo_ref[...] = u.astype(BF16)
        a_ref[...] = (g * jax.nn.sigmoid(g) * u).astype(BF16)

    vec = pl.BlockSpec((1, d_dim), lambda i, j: (0, 0))
    w_spec = pl.BlockSpec((tn, d_dim), lambda i, j: (j, 0))
    act = pl.BlockSpec((tm, tn), lambda i, j: (i, j))
    act_shape = jax.ShapeDtypeStruct((s_dim, f_dim), BF16)
    return _pcall(
        body,
        out_shape=(jax.ShapeDtypeStruct((s_dim, d_dim), BF16), act_shape, act_shape, act_shape),
        grid=(s_dim // tm, f_dim // tn),
        in_specs=[pl.BlockSpec((tm, d_dim), lambda i, j: (i, 0)), vec, vec, vec, w_spec, w_spec],
        out_specs=(pl.BlockSpec((tm, d_dim), lambda i, j: (i, 0)), act, act, act),
        name=name,
        compiler_params=_params(("parallel", "arbitrary"),
                                [((tm, d_dim), F32), ((tn, d_dim), BF16), ((tn, d_dim), BF16), ((tm, d_dim), BF16)]
                                + 3 * [((tm, tn), BF16)], extra=3 * tm * d_dim * 4 + 4 * tm * tn * 4),
    )(x, pre_g, scale, shift, wg_t, wu_t)


def _mm_post(a, w, x, post_g, gate, res_w, name):
    s_dim, k_dim = a.shape
    d_dim = w.shape[1]
    tm = TOKEN_TILE

    def body(a_ref, w_ref, x_ref, pg_ref, gt_ref, xo_ref, f_ref):
        f = _dot_nn(a_ref[...], w_ref[...])
        y = f * _rstd(f) * pg_ref[...]
        f_ref[...] = f
        xo_ref[...] = x_ref[...] + (res_w * gt_ref[...]) * y

    vec = pl.BlockSpec((1, d_dim), lambda i: (0, 0))
    row = pl.BlockSpec((tm, d_dim), lambda i: (i, 0))
    out = jax.ShapeDtypeStruct((s_dim, d_dim), F32)
    return _pcall(
        body, out_shape=(out, out), grid=(s_dim // tm,),
        in_specs=[pl.BlockSpec((tm, k_dim), lambda i: (i, 0)), pl.BlockSpec((k_dim, d_dim), lambda i: (0, 0)), row, vec, vec],
        out_specs=(row, row), name=name,
        compiler_params=_params(("parallel",), [((tm, k_dim), BF16), ((k_dim, d_dim), BF16)] + 3 * [((tm, d_dim), F32)],
                                extra=3 * tm * d_dim * 4),
    )(a, w, x, post_g, gate)


def _post_bwd(dx_out, f, post_g, gate, res_w, name):
    s_dim, d_dim = f.shape
    tm = TOKEN_TILE

    def body(dx_ref, f_ref, pg_ref, gt_ref, df_ref, dgate_ref, dpost_ref):
        @pl.when(pl.program_id(0) == 0)
        def _():
            dgate_ref[...] = jnp.zeros_like(dgate_ref)
            dpost_ref[...] = jnp.zeros_like(dpost_ref)

        dx, fv = dx_ref[...], f_ref[...]
        r = _rstd(fv)
        fr = fv * r
        dgate_ref[...] += res_w * jnp.sum(dx * (fr * pg_ref[...]), axis=0, keepdims=True)
        dy = (res_w * gt_ref[...]) * dx
        dpost_ref[...] += jnp.sum(dy * fr, axis=0, keepdims=True)
        df_ref[...] = _rms_bwd(fv, r, dy * pg_ref[...]).astype(BF16)

    vec = pl.BlockSpec((1, d_dim), lambda i: (0, 0))
    row = pl.BlockSpec((tm, d_dim), lambda i: (i, 0))
    vshape = jax.ShapeDtypeStruct((1, d_dim), F32)
    return _pcall(
        body, out_shape=(jax.ShapeDtypeStruct((s_dim, d_dim), BF16), vshape, vshape), grid=(s_dim // tm,),
        in_specs=[row, row, vec, vec], out_specs=(row, vec, vec), name=name,
        compiler_params=_params(("arbitrary",), 3 * [((tm, d_dim), F32)], extra=6 * tm * d_dim * 4),
    )(dx_out, f, post_g, gate)


def _prenorm_bwd(dx_out, dhns, x, pre_g, scale, name):
    s_dim, d_dim = x.shape
    tm = TOKEN_TILE
    n_in = len(dhns)

    def body(*refs):
        dx_ref, x_ref, pg_ref, sc_ref = refs[n_in + 0], refs[n_in + 1], refs[n_in + 2], refs[n_in + 3]
        dxo_ref, dsh_ref, dsc_ref, dpg_ref = refs[n_in + 4:]

        @pl.when(pl.program_id(0) == 0)
        def _():
            dsh_ref[...] = jnp.zeros_like(dsh_ref)
            dsc_ref[...] = jnp.zeros_like(dsc_ref)
            dpg_ref[...] = jnp.zeros_like(dpg_ref)

        dhn = refs[0][...]
        for k in range(1, n_in):
            dhn = dhn + refs[k][...]
        xv = x_ref[...]
        r = _rstd(xv)
        xr = xv * r
        dsh_ref[...] += jnp.sum(dhn, axis=0, keepdims=True)
        dsc_ref[...] += jnp.sum(dhn * (xr * pg_ref[...]), axis=0, keepdims=True)
        dn = dhn * (1.0 + sc_ref[...])
        dpg_ref[...] += jnp.sum(dn * xr, axis=0, keepdims=True)
        dxo_ref[...] = dx_ref[...] + _rms_bwd(xv, r, dn * pg_ref[...])

    vec = pl.BlockSpec((1, d_dim), lambda i: (0, 0))
    row = pl.BlockSpec((tm, d_dim), lambda i: (i, 0))
    vshape = jax.ShapeDtypeStruct((1, d_dim), F32)
    return _pcall(
        body, out_shape=(jax.ShapeDtypeStruct((s_dim, d_dim), F32), vshape, vshape, vshape), grid=(s_dim // tm,),
        in_specs=n_in * [row] + [row, row, vec, vec], out_specs=(row, vec, vec, vec), name=name,
        compiler_params=_params(("arbitrary",), (n_in + 3) * [((tm, d_dim), F32)], extra=6 * tm * d_dim * 4),
    )(*dhns, dx_out, x, pre_g, scale)


def _ffn_dgu(df, wd, g, u, name, after=None):
    s_dim, d_dim = df.shape
    f_dim = wd.shape[0]
    tm, tn = TOKEN_TILE, f_dim // 2

    def body(df_ref, wd_ref, g_ref, u_ref, *rest):
        dg_ref, du_ref = rest[-2:]
        da = _dot_nt(df_ref[...], wd_ref[...])
        gv, uv = g_ref[...].astype(F32), u_ref[...].astype(F32)
        sg = jax.nn.sigmoid(gv)
        du_ref[...] = (da * (gv * sg)).astype(BF16)
        dg_ref[...] = (da * uv * (sg * (1.0 + gv * (1.0 - sg)))).astype(BF16)

    act = pl.BlockSpec((tm, tn), lambda i, j: (i, j))
    act_shape = jax.ShapeDtypeStruct((s_dim, f_dim), BF16)
    token = [] if after is None else [after]
    return _pcall(
        body, out_shape=(act_shape, act_shape), grid=(s_dim // tm, f_dim // tn),
        in_specs=[pl.BlockSpec((tm, d_dim), lambda i, j: (i, 0)), pl.BlockSpec((tn, d_dim), lambda i, j: (j, 0)), act, act]
        + len(token) * [_TOKEN_SPEC],
        out_specs=(act, act), name=name,
        compiler_params=_params(("parallel", "parallel"), [((tm, d_dim), BF16), ((tn, d_dim), BF16)] + 4 * [((tm, tn), BF16)],
                                extra=6 * tm * tn * 4),
    )(df, wd, g, u, *token)


def _ffn_dw(dg, du, a, hn, df, name):
    s_dim, f_dim = dg.shape
    d_dim = hn.shape[1]
    tm = 256

    def body(dg_ref, du_ref, a_ref, hn_ref, df_ref, dwg_ref, dwu_ref, dwd_ref):
        dwg_ref[...] = _dot_tn(dg_ref[...], hn_ref[...]).astype(BF16)
        dwu_ref[...] = _dot_tn(du_ref[...], hn_ref[...]).astype(BF16)
        dwd_ref[...] = _dot_tn(a_ref[...], df_ref[...]).astype(BF16)

    col = pl.BlockSpec((s_dim, tm), lambda i: (0, i))
    full = pl.BlockSpec((s_dim, d_dim), lambda i: (0, 0), pipeline_mode=pl.Buffered(1))
    out = pl.BlockSpec((tm, d_dim), lambda i: (i, 0))
    shape = jax.ShapeDtypeStruct((f_dim, d_dim), BF16)
    need = 2 * s_dim * d_dim * 2 + 2 * 3 * (s_dim * tm * 2 + tm * d_dim * 2) + 3 * tm * d_dim * 4 + 3 * s_dim * tm * 2
    return _pcall(
        body, out_shape=(shape, shape, shape), grid=(f_dim // tm,), in_specs=[col, col, col, full, full],
        out_specs=(out, out, out), name=name,
        compiler_params=pltpu.CompilerParams(dimension_semantics=("parallel",),
                                             vmem_limit_bytes=int(min(need + VMEM_RESERVE, V7X_VMEM_BYTES - VMEM_RESERVE))),
    )(dg, du, a, hn, df)


def _mm_tn_shared(lhs, b, name):
    k_dim, m_dim = lhs[0].shape
    n_dim = b.shape[1]
    tm = 256
    n = len(lhs)

    def body(*refs):
        rhs = refs[n][...]
        for j in range(n):
            refs[n + 1 + j][...] = _dot_tn(refs[j][...], rhs).astype(BF16)

    col = pl.BlockSpec((k_dim, tm), lambda i: (0, i))
    out = pl.BlockSpec((tm, n_dim), lambda i: (i, 0))
    shape = jax.ShapeDtypeStruct((m_dim, n_dim), BF16)
    need = k_dim * n_dim * 2 + 2 * n * (k_dim * tm * 2 + tm * n_dim * 2) + n * tm * n_dim * 4 + n * k_dim * tm * 2
    return _pcall(
        body, out_shape=tuple(n * [shape]), grid=(m_dim // tm,),
        in_specs=n * [col] + [pl.BlockSpec((k_dim, n_dim), lambda i: (0, 0), pipeline_mode=pl.Buffered(1))],
        out_specs=tuple(n * [out]), name=name,
        compiler_params=pltpu.CompilerParams(dimension_semantics=("parallel",),
                                             vmem_limit_bytes=int(min(need + VMEM_RESERVE, V7X_VMEM_BYTES - VMEM_RESERVE))),
    )(*lhs, b)


def _ffn_fwd_fused(x, pre_g, scale, shift, post_g, gate, wg_t, wu_t, wd, name):
    s_dim, d_dim = x.shape
    f_dim = wd.shape[0]
    tm, chunks = 256, FFN_CHUNKS
    cw = f_dim // chunks

    def body(x_ref, prg_ref, sc_ref, sh_ref, pg_ref, gt_ref, wg_ref, wu_ref, wd_ref, hn_ref, go_ref, uo_ref, a_ref, xo_ref, f_ref):
        xf = x_ref[...]
        hn = ((xf * _rstd(xf) * prg_ref[...]) * (1.0 + sc_ref[...]) + sh_ref[...]).astype(BF16)
        hn_ref[...] = hn
        f = None
        ahead = (_dot_nt(hn, wg_ref[0:cw, :]), _dot_nt(hn, wu_ref[0:cw, :]))
        for c in range(chunks):
            g, u = ahead
            if c + 1 < chunks:
                nxt = slice((c + 1) * cw, (c + 2) * cw)
                ahead = (_dot_nt(hn, wg_ref[nxt, :]), _dot_nt(hn, wu_ref[nxt, :]))
            cols = slice(c * cw, (c + 1) * cw)
            go_ref[:, cols] = g.astype(BF16)
            uo_ref[:, cols] = u.astype(BF16)
            a = (g * jax.nn.sigmoid(g) * u).astype(BF16)
            a_ref[:, cols] = a
            part = _dot_nn(a, wd_ref[cols, :])
            f = part if f is None else f + part
        f_ref[...] = f
        xo_ref[...] = xf + (FFN_RES * gt_ref[...]) * (f * _rstd(f) * pg_ref[...])

    vec = pl.BlockSpec((1, d_dim), lambda i: (0, 0))
    row = pl.BlockSpec((tm, d_dim), lambda i: (i, 0))
    act = pl.BlockSpec((tm, f_dim), lambda i: (i, 0))
    weight = pl.BlockSpec((f_dim, d_dim), lambda i: (0, 0), pipeline_mode=pl.Buffered(1))
    act_shape = jax.ShapeDtypeStruct((s_dim, f_dim), BF16)
    res_shape = jax.ShapeDtypeStruct((s_dim, d_dim), F32)
    need = (3 * f_dim * d_dim * 2 + 2 * tm * d_dim * 4 + 2 * (tm * d_dim * 2 + 3 * tm * f_dim * 2 + 2 * tm * d_dim * 4)
            + 8 * tm * cw * 4 + 4 * tm * d_dim * 4)
    return _pcall(
        body, out_shape=(jax.ShapeDtypeStruct((s_dim, d_dim), BF16), act_shape, act_shape, act_shape, res_shape, res_shape),
        grid=(s_dim // tm,), in_specs=[row, vec, vec, vec, vec, vec, weight, weight, weight],
        out_specs=(row, act, act, act, row, row), name=name,
        compiler_params=pltpu.CompilerParams(dimension_semantics=("parallel",),
                                             vmem_limit_bytes=int(min(need + VMEM_RESERVE, V7X_VMEM_BYTES - VMEM_RESERVE))),
    )(x, pre_g, scale, shift, post_g, gate, wg_t, wu_t, wd)


def _ffn_bwd_fused(dx_out, saved, pre_g, post_g, scale, gate, wg_t, wu_t, wd, name):
    x, _, g, u, _, f = saved
    s_dim, d_dim = x.shape
    f_dim = wd.shape[0]
    tm, chunks = 256, FFN_CHUNKS
    cw = f_dim // chunks

    def body(dx_ref, f_ref, g_ref, u_ref, x_ref, pg_ref, gt_ref, prg_ref, sc_ref, wd_ref, wg_ref, wu_ref,
             df_ref, dg_ref, du_ref, dxo_ref, dgate_ref, dpost_ref, dsh_ref, dsc_ref, dpg_ref):
        @pl.when(pl.program_id(0) == 0)
        def _():
            for acc in (dgate_ref, dpost_ref, dsh_ref, dsc_ref, dpg_ref):
                acc[...] = jnp.zeros_like(acc)

        dx, fv = dx_ref[...], f_ref[...]
        r = _rstd(fv)
        fr = fv * r
        dgate_ref[...] += FFN_RES * jnp.sum(dx * (fr * pg_ref[...]), axis=0, keepdims=True)
        dy = (FFN_RES * gt_ref[...]) * dx
        dpost_ref[...] += jnp.sum(dy * fr, axis=0, keepdims=True)
        df = _rms_bwd(fv, r, dy * pg_ref[...]).astype(BF16)
        df_ref[...] = df
        dhn = None
        ahead = _dot_nt(df, wd_ref[0:cw, :])
        for c in range(chunks):
            da = ahead
            if c + 1 < chunks:
                ahead = _dot_nt(df, wd_ref[(c + 1) * cw:(c + 2) * cw, :])
            cols = slice(c * cw, (c + 1) * cw)
            gv, uv = g_ref[:, cols].astype(F32), u_ref[:, cols].astype(F32)
            sg = jax.nn.sigmoid(gv)
            du = (da * (gv * sg)).astype(BF16)
            dg = (da * uv * (sg * (1.0 + gv * (1.0 - sg)))).astype(BF16)
            dg_ref[:, cols] = dg
            du_ref[:, cols] = du
            part = _dot_nn(dg, wg_ref[cols, :]) + _dot_nn(du, wu_ref[cols, :])
            dhn = part if dhn is None else dhn + part
        xv = x_ref[...]
        rx = _rstd(xv)
        xr = xv * rx
        dsh_ref[...] += jnp.sum(dhn, axis=0, keepdims=True)
        dsc_ref[...] += jnp.sum(dhn * (xr * prg_ref[...]), axis=0, keepdims=True)
        dn = dhn * (1.0 + sc_ref[...])
        dpg_ref[...] += jnp.sum(dn * xr, axis=0, keepdims=True)
        dxo_ref[...] = dx + _rms_bwd(xv, rx, dn * prg_ref[...])

    vec = pl.BlockSpec((1, d_dim), lambda i: (0, 0))
    row = pl.BlockSpec((tm, d_dim), lambda i: (i, 0))
    act = pl.BlockSpec((tm, f_dim), lambda i: (i, 0))
    weight = pl.BlockSpec((f_dim, d_dim), lambda i: (0, 0), pipeline_mode=pl.Buffered(1))
    vshape = jax.ShapeDtypeStruct((1, d_dim), F32)
    act_shape = jax.ShapeDtypeStruct((s_dim, f_dim), BF16)
    need = (3 * f_dim * d_dim * 2 + 2 * (3 * tm * d_dim * 4 + 2 * tm * f_dim * 2) + 2 * (tm * d_dim * 2 + 2 * tm * f_dim * 2 + tm * d_dim * 4)
            + 6 * tm * cw * 4 + 6 * tm * d_dim * 4)
    return _pcall(
        body, out_shape=(jax.ShapeDtypeStruct((s_dim, d_dim), BF16), act_shape, act_shape, jax.ShapeDtypeStruct((s_dim, d_dim), F32),
                         vshape, vshape, vshape, vshape, vshape),
        grid=(s_dim // tm,), in_specs=[row, row, act, act, row, vec, vec, vec, vec, weight, weight, weight],
        out_specs=(row, act, act, row, vec, vec, vec, vec, vec), name=name,
        compiler_params=pltpu.CompilerParams(dimension_semantics=("arbitrary",),
                                             vmem_limit_bytes=int(min(need + VMEM_RESERVE, V7X_VMEM_BYTES - VMEM_RESERVE))),
    )(dx_out, f, g, u, x, post_g, gate, pre_g, scale, wd, wg_t, wu_t)


def _rope_tables(zero=0.0):
    half = QK_ROPE // 2
    freqs = ROPE_THETA ** (-jnp.arange(half, dtype=F32) / half)
    ang = (jnp.arange(SEQ, dtype=F32)[:, None] + zero) * freqs[None, :]
    cos, sin = jnp.cos(ang), jnp.sin(ang)
    ones = jnp.ones((SEQ, QK_NOPE), F32)
    zeros = jnp.zeros((SEQ, QK_NOPE), F32)
    pad1 = jnp.ones((SEQ, HEAD_PAD - QK_NOPE - QK_ROPE), F32)
    pad0 = jnp.zeros((SEQ, HEAD_PAD - QK_NOPE - QK_ROPE), F32)
    zh = jnp.zeros((SEQ, half), F32)
    c = jnp.concatenate([ones, cos, cos, pad1], axis=1)
    s1 = jnp.concatenate([zeros, -sin, zh, pad0], axis=1)
    s2 = jnp.concatenate([zeros, zh, sin, pad0], axis=1)
    return c, s1, s2


def _rope(v, c, s1, s2):
    half = QK_ROPE // 2
    return v * c + pltpu.roll(v, HEAD_PAD - half, 1) * s1 + pltpu.roll(v, half, 1) * s2


def _rope_t(dv, c, s1, s2):
    half = QK_ROPE // 2
    return dv * c + pltpu.roll(dv * s1, half, 1) + pltpu.roll(dv * s2, HEAD_PAD - half, 1)


def _mla_qkv(lat, q_norm, kv_norm, wq_t, wkv_t, rope, name):
    s_dim = lat.shape[0]
    width = MLA_HEADS * HEAD_PAD
    tm = 256

    def body(lat_ref, qg_ref, kg_ref, wq_ref, wkv_ref, c_ref, s1_ref, s2_ref, q_ref, k_ref, v_ref, qn_ref, kvn_ref):
        cq = lat_ref[:, :Q_LORA]
        ckv = lat_ref[:, Q_LORA:Q_LORA + KV_LORA]
        kr = lat_ref[:, Q_LORA + KV_LORA:]
        c, s1, s2 = c_ref[...], s1_ref[...], s2_ref[...]
        qn = (cq * _rstd(cq) * qg_ref[...]).astype(BF16)
        kvn = (ckv * _rstd(ckv) * kg_ref[...]).astype(BF16)
        qn_ref[...] = qn
        kvn_ref[...] = kvn
        q = _dot_nt(qn, wq_ref[...])
        kv = _dot_nt(kvn, wkv_ref[...])
        krr = _rope(kr, c, s1, s2)
        low = lax.broadcasted_iota(jnp.int32, (tm, HEAD_PAD), 1) < QK_NOPE
        for h in range(MLA_HEADS):
            sl = slice(h * HEAD_PAD, (h + 1) * HEAD_PAD)
            q_ref[:, sl] = _rope(q[:, sl], c, s1, s2).astype(BF16)
            kvh = kv[:, sl]
            k_ref[:, sl] = (jnp.where(low, kvh, 0.0) + krr).astype(BF16)
            v_ref[:, sl] = jnp.where(low, 0.0, kvh).astype(BF16)

    row = lambda n: pl.BlockSpec((tm, n), lambda i: (i, 0))
    full = lambda a: pl.BlockSpec(a.shape, lambda i: (0, 0))
    wide = jax.ShapeDtypeStruct((s_dim, width), BF16)
    return _pcall(
        body,
        out_shape=(wide, wide, wide, jax.ShapeDtypeStruct((s_dim, Q_LORA), BF16), jax.ShapeDtypeStruct((s_dim, KV_LORA), BF16)),
        grid=(s_dim // tm,),
        in_specs=[row(LAT_PAD), full(q_norm), full(kv_norm), full(wq_t), full(wkv_t), row(HEAD_PAD), row(HEAD_PAD), row(HEAD_PAD)],
        out_specs=(row(width), row(width), row(width), row(Q_LORA), row(KV_LORA)), name=name,
        compiler_params=_params(("parallel",), [((tm, LAT_PAD), F32), (wq_t.shape, BF16), (wkv_t.shape, BF16)]
                                + 3 * [((tm, width), BF16)], extra=4 * tm * width * 4),
    )(lat, q_norm, kv_norm, wq_t, wkv_t, *rope)


def _mla_scores(q, k_ref, t, tq):
    lo = t * tq
    own = slice(lo, lo + tq)
    scores = [(_dot_nt(q, k_ref[own, :]), own)]
    if t > 0:
        scores.append((_dot_nt(q, k_ref[0:lo, :]), slice(0, lo)))
    return scores


def _mla_softmax(scores):
    s_own = scores[0][0] * MLA_SCALE
    rows = lax.broadcasted_iota(jnp.int32, s_own.shape, 0)
    cols = lax.broadcasted_iota(jnp.int32, s_own.shape, 1)
    s_own = jnp.where(cols <= rows, s_own, -jnp.inf)
    mx = jnp.max(s_own, axis=-1, keepdims=True)
    if len(scores) == 1:
        e_own = jnp.exp(s_own - mx)
        return [(e_own * (1.0 / jnp.sum(e_own, axis=-1, keepdims=True)), scores[0][1])]
    s_pre = scores[1][0] * MLA_SCALE
    mx = jnp.maximum(mx, jnp.max(s_pre, axis=-1, keepdims=True))
    e_own, e_pre = jnp.exp(s_own - mx), jnp.exp(s_pre - mx)
    inv = 1.0 / (jnp.sum(e_own, axis=-1, keepdims=True) + jnp.sum(e_pre, axis=-1, keepdims=True))
    return [(e_pre * inv, scores[1][1]), (e_own * inv, scores[0][1])]


def _mla_attn_fwd(q, k, v, name):
    s_dim = q.shape[0]
    tq = MLA_QUERY_TILE

    def body(q_ref, k_ref, v_ref, o_ref):
        n_tiles = s_dim // tq
        tile_of = lambda t: slice(t * tq, (t + 1) * tq)
        def weighted_values(t, probs):
            o = None
            for p, keys in probs:
                part = _dot_nn(p, v_ref[keys, :])
                o = part if o is None else o + part
            o_ref[tile_of(t), :] = o.astype(BF16)

        scores = _mla_scores(q_ref[tile_of(0), :], k_ref, 0, tq)
        probs = None
        for t in range(n_tiles):
            ahead = _mla_scores(q_ref[tile_of(t + 1), :], k_ref, t + 1, tq) if t + 1 < n_tiles else None
            if probs is not None:
                weighted_values(t - 1, probs)
            probs = [(p.astype(BF16), keys) for p, keys in _mla_softmax(scores)]
            scores = ahead
        weighted_values(n_tiles - 1, probs)

    head = pl.BlockSpec((s_dim, HEAD_PAD), lambda h: (0, h))
    return _pcall(
        body, out_shape=jax.ShapeDtypeStruct(q.shape, BF16), grid=(MLA_HEADS,),
        in_specs=[head, head, head], out_specs=head, name=name,
        compiler_params=_params(("parallel",), 4 * [((s_dim, HEAD_PAD), BF16)], extra=4 * tq * s_dim * 4),
    )(q, k, v)


def _mla_attn_bwd(q, k, v, d_o, name):
    s_dim = q.shape[0]
    tq = MLA_QUERY_TILE

    def body(q_ref, k_ref, v_ref, do_ref, dq_ref, dk_ref, dv_ref):
        dk_ref[...] = jnp.zeros_like(dk_ref)
        dv_ref[...] = jnp.zeros_like(dv_ref)
        n_tiles = s_dim // tq
        tile_of = lambda t: slice(t * tq, (t + 1) * tq)

        def products(t):
            scores = _mla_scores(q_ref[tile_of(t), :], k_ref, t, tq)
            dot = do_ref[tile_of(t), :].astype(BF16)
            return scores, [_dot_nt(dot, v_ref[keys, :]) for _, keys in scores]

        def gradients_of_scores(scores, dps):
            probs = _mla_softmax(scores)
            dp_of = {(keys.start, keys.stop): dp for (_, keys), dp in zip(scores, dps)}
            terms = [(p, keys, dp_of[keys.start, keys.stop]) for p, keys in probs]
            row = None
            for p, _, dp in terms:
                part = jnp.sum(p * dp, axis=-1, keepdims=True)
                row = part if row is None else row + part
            return [((p * (dp - row) * MLA_SCALE).astype(BF16), p.astype(BF16), keys) for p, keys, dp in terms]

        def accumulate(t, terms):
            qt = q_ref[tile_of(t), :]
            dot = do_ref[tile_of(t), :].astype(BF16)
            dq = None
            for dsb, pb, keys in terms:
                part = _dot_nn(dsb, k_ref[keys, :])
                dq = part if dq is None else dq + part
                dk_ref[keys, :] += _dot_tn(dsb, qt)
                dv_ref[keys, :] += _dot_tn(pb, dot)
            dq_ref[tile_of(t), :] = dq

        ready = products(0)
        terms = None
        for t in range(n_tiles):
            ahead = products(t + 1) if t + 1 < n_tiles else None
            if terms is not None:
                accumulate(t - 1, terms)
            terms = gradients_of_scores(*ready)
            ready = ahead
        accumulate(n_tiles - 1, terms)

    head = pl.BlockSpec((s_dim, HEAD_PAD), lambda h: (0, h))
    out = jax.ShapeDtypeStruct(q.shape, F32)
    return _pcall(
        body, out_shape=(out, out, out), grid=(MLA_HEADS,),
        in_specs=[head, head, head, head], out_specs=(head, head, head), name=name,
        compiler_params=_params(("parallel",), 3 * [((s_dim, HEAD_PAD), BF16)] + 4 * [((s_dim, HEAD_PAD), F32)],
                                extra=6 * tq * s_dim * 4),
    )(q, k, v, d_o)


def _mla_qkv_bwd(dq, dk, dv, lat, q_norm, kv_norm, wq_t, wkv_t, rope, name):
    s_dim = lat.shape[0]
    width = MLA_HEADS * HEAD_PAD
    tm = 256

    def body(dq_ref, dk_ref, dv_ref, lat_ref, qg_ref, kg_ref, wq_ref, wkv_ref, c_ref, s1_ref, s2_ref,
             dqp_ref, dkv_ref, dlat_ref, dqg_ref, dkg_ref):
        @pl.when(pl.program_id(0) == 0)
        def _():
            dqg_ref[...] = jnp.zeros_like(dqg_ref)
            dkg_ref[...] = jnp.zeros_like(dkg_ref)

        c, s1, s2 = c_ref[...], s1_ref[...], s2_ref[...]
        lane = lax.broadcasted_iota(jnp.int32, (tm, HEAD_PAD), 1)
        low = lane < QK_NOPE
        rot = (lane >= QK_NOPE) & (lane < QK_NOPE + QK_ROPE)
        dkrr = jnp.zeros((tm, HEAD_PAD), F32)
        for h in range(MLA_HEADS):
            sl = slice(h * HEAD_PAD, (h + 1) * HEAD_PAD)
            dqp_ref[:, sl] = _rope_t(dq_ref[:, sl], c, s1, s2).astype(BF16)
            dkh = dk_ref[:, sl]
            dkv_ref[:, sl] = jnp.where(low, dkh, dv_ref[:, sl]).astype(BF16)
            dkrr = dkrr + jnp.where(rot, dkh, 0.0)
        dqn = _dot_nn(dqp_ref[...], wq_ref[...])
        dkvn = _dot_nn(dkv_ref[...], wkv_ref[...])
        cq = lat_ref[:, :Q_LORA]
        ckv = lat_ref[:, Q_LORA:Q_LORA + KV_LORA]
        rq, rkv = _rstd(cq), _rstd(ckv)
        dqg_ref[...] += jnp.sum(dqn * cq * rq, axis=0, keepdims=True)
        dkg_ref[...] += jnp.sum(dkvn * ckv * rkv, axis=0, keepdims=True)
        dlat_ref[:, :Q_LORA] = _rms_bwd(cq, rq, dqn * qg_ref[...])
        dlat_ref[:, Q_LORA:Q_LORA + KV_LORA] = _rms_bwd(ckv, rkv, dkvn * kg_ref[...])
        dlat_ref[:, Q_LORA + KV_LORA:] = _rope_t(dkrr, c, s1, s2)

    row = lambda n: pl.BlockSpec((tm, n), lambda i: (i, 0))
    full = lambda a: pl.BlockSpec(a.shape, lambda i: (0, 0))
    wide = jax.ShapeDtypeStruct((s_dim, width), BF16)
    return _pcall(
        body,
        out_shape=(wide, wide, jax.ShapeDtypeStruct((s_dim, LAT_PAD), F32),
                   jax.ShapeDtypeStruct(q_norm.shape, F32), jax.ShapeDtypeStruct(kv_norm.shape, F32)),
        grid=(s_dim // tm,),
        in_specs=[row(width), row(width), row(width), row(LAT_PAD), full(q_norm), full(kv_norm), full(wq_t), full(wkv_t),
                  row(HEAD_PAD), row(HEAD_PAD), row(HEAD_PAD)],
        out_specs=(row(width), row(width), row(LAT_PAD), full(q_norm), full(kv_norm)), name=name,
        compiler_params=_params(("arbitrary",), 3 * [((tm, width), F32)] + [((tm, LAT_PAD), F32), (wq_t.shape, BF16),
                                                                           (wkv_t.shape, BF16)] + 2 * [((tm, width), BF16)],
                                extra=2 * tm * width * 4),
    )(dq, dk, dv, lat, q_norm, kv_norm, wq_t, wkv_t, *rope)


def _t5_bucket(dist):
    max_exact = N_BUCKETS // 2
    d = jnp.maximum(dist, 1).astype(F32)
    large = max_exact + (jnp.log(d / max_exact) / math.log(MAX_DISTANCE / max_exact)
                         * (N_BUCKETS - max_exact)).astype(jnp.int32)
    large = jnp.minimum(large, N_BUCKETS - 1)
    return jnp.where(dist < max_exact, dist, large)


def _dil_buckets(dilation):
    iq = jnp.arange(DIL_BLOCK)[:, None]
    ik = jnp.arange(2 * DIL_BLOCK)[None, :]
    return _t5_bucket(jnp.maximum(DIL_BLOCK + iq - ik, 0) * dilation)


def _dil_logits(qh, kb, bias_h, first, span):
    if first:
        s = _dot_nt(qh, kb) * DIL_SCALE + bias_h[:, DIL_BLOCK:]
        rel = lax.broadcasted_iota(jnp.int32, s.shape, 0) - lax.broadcasted_iota(jnp.int32, s.shape, 1)
    else:
        s = _dot_nt(qh, kb) * DIL_SCALE + bias_h
        rel = DIL_BLOCK + lax.broadcasted_iota(jnp.int32, s.shape, 0) - lax.broadcasted_iota(jnp.int32, s.shape, 1)
    return jnp.where((rel >= 0) & (rel <= span), s, -jnp.inf)


def _dil_blocks(s_dim, dilation):
    rows = s_dim // dilation
    for r in range(dilation):
        for n in range(rows // DIL_BLOCK):
            lo = r * rows + n * DIL_BLOCK
            keys = slice(lo, lo + DIL_BLOCK) if n == 0 else slice(lo - DIL_BLOCK, lo + DIL_BLOCK)
            start = r + n * DIL_BLOCK * dilation
            tokens = slice(start, start + DIL_BLOCK) if dilation == 1 else pl.ds(start, DIL_BLOCK, stride=dilation)
            yield n == 0, slice(lo, lo + DIL_BLOCK), keys, tokens


def _dil_views(s_dim):
    col = lambda which: pl.BlockSpec((s_dim, HEAD_PAD), lambda p: (0, which * DIL_PAIRS + p))
    nat = pl.BlockSpec((s_dim, HEAD_PAD), lambda p: (0, p))
    bias = pl.BlockSpec((2, DIL_BLOCK, 2 * DIL_BLOCK), lambda p: (p, 0, 0))
    return col, nat, bias


def _dil_attn_fwd(qkv, bias, dilation, span, name):
    s_dim = qkv.shape[0]
    d_dim = DIL_HEADS * DIL_HEAD_DIM
    col, nat, bias_spec = _dil_views(s_dim)

    def body(q_ref, k_ref, v_ref, b_ref, o_ref, l_ref):
        lane = lax.broadcasted_iota(jnp.int32, (DIL_BLOCK, HEAD_PAD), 1)
        klane = lax.broadcasted_iota(jnp.int32, (2 * DIL_BLOCK, HEAD_PAD), 1)
        blocks = list(_dil_blocks(s_dim, dilation))
        for g0 in range(0, len(blocks), DIL_GROUPED):
            group = blocks[g0:g0 + DIL_GROUPED]
            logits = [_dil_logits(jnp.where((lane < DIL_HEAD_DIM) == (h == 0), q_ref[blk, :], 0), k_ref[keys, :], b_ref[h],
                                  first, span) for first, blk, keys, _ in group for h in range(2)]
            soft = []
            for lg in logits:
                mx = jnp.max(lg, axis=-1, keepdims=True)
                e = jnp.exp(lg - mx)
                tot = jnp.sum(e, axis=-1, keepdims=True)
                soft.append(((e * (1.0 / tot)).astype(BF16), mx + jnp.log(tot)))
            for i, (_, _, keys, tokens) in enumerate(group):
                vb = v_ref[keys, :]
                o_acc = jnp.zeros((DIL_BLOCK, HEAD_PAD), F32)
                lse_acc = jnp.zeros((DIL_BLOCK, HEAD_PAD), F32)
                for h in range(2):
                    p, lse = soft[2 * i + h]
                    kmine = (klane[:vb.shape[0]] < DIL_HEAD_DIM) == (h == 0)
                    o_acc = o_acc + _dot_nn(p, jnp.where(kmine, vb, 0))
                    lse_acc = jnp.where((lane < DIL_HEAD_DIM) == (h == 0), lse, lse_acc)
                o_ref[tokens, :] = o_acc
                l_ref[tokens, :] = lse_acc

    out = jax.ShapeDtypeStruct((s_dim, d_dim), F32)
    return _pcall(
        body, out_shape=(out, out), grid=(DIL_PAIRS,),
        in_specs=[col(0), col(1), col(2), bias_spec], out_specs=(nat, nat), name=name,
        compiler_params=_params(("parallel",), 3 * [((s_dim, HEAD_PAD), BF16)] + 2 * [((s_dim, HEAD_PAD), F32)]
                                + [((2, DIL_BLOCK, 2 * DIL_BLOCK), F32)], extra=2**21),
    )(qkv, qkv, qkv, bias)


def _dil_mix(lses, outs, name):
    s_dim, d_dim = outs[0].shape
    tm = TOKEN_TILE
    ng = len(outs)

    def body(*refs):
        ls = [refs[g][...] for g in range(ng)]
        mx = ls[0]
        for g in range(1, ng):
            mx = jnp.maximum(mx, ls[g])
        es = [jnp.exp(l - mx) for l in ls]
        tot = es[0]
        for g in range(1, ng):
            tot = tot + es[g]
        o = None
        for g in range(ng):
            al = es[g] / tot
            refs[2 * ng + g][...] = al
            t = al * refs[ng + g][...]
            o = t if o is None else o + t
        refs[3 * ng][...] = o
        refs[3 * ng + 1][...] = o.astype(BF16)

    row = pl.BlockSpec((tm, d_dim), lambda i: (i, 0))
    f = jax.ShapeDtypeStruct((s_dim, d_dim), F32)
    res = _pcall(
        body, out_shape=tuple(ng * [f] + [f, jax.ShapeDtypeStruct((s_dim, d_dim), BF16)]), grid=(s_dim // tm,),
        in_specs=2 * ng * [row], out_specs=tuple((ng + 2) * [row]), name=name,
        compiler_params=_params(("parallel",), (3 * ng + 2) * [((tm, d_dim), F32)], extra=4 * tm * d_dim * 4),
    )(*lses, *outs)
    return res[:ng], res[ng], res[ng + 1]


def _dil_attn_bwd(qkv, bias, d_o, o_mix, alpha, lse, dilation, span, name):
    s_dim = qkv.shape[0]
    d_dim = DIL_HEADS * DIL_HEAD_DIM
    col, nat, bias_spec = _dil_views(s_dim)

    def body(q_ref, k_ref, v_ref, b_ref, do_ref, om_ref, al_ref, l_ref, dq_ref, dk_ref, dv_ref, db_ref, dk_acc, dv_acc):
        db_ref[...] = jnp.zeros_like(db_ref)
        dk_acc[...] = jnp.zeros_like(dk_acc)
        dv_acc[...] = jnp.zeros_like(dv_acc)
        lane = lax.broadcasted_iota(jnp.int32, (DIL_BLOCK, HEAD_PAD), 1)
        klane = lax.broadcasted_iota(jnp.int32, (2 * DIL_BLOCK, HEAD_PAD), 1)
        blocks = list(_dil_blocks(s_dim, dilation))
        heads = [(lane < DIL_HEAD_DIM) == (h == 0) for h in range(2)]
        for g0 in range(0, len(blocks), DIL_GROUPED):
            group = blocks[g0:g0 + DIL_GROUPED]
            staged = []
            for first, blk, kv_rows, tokens in group:
                qb, kb, vb = q_ref[blk, :], k_ref[kv_rows, :], v_ref[kv_rows, :]
                dog = al_ref[tokens, :] * do_ref[tokens, :]
                row_term = dog * om_ref[tokens, :]
                lse_b = l_ref[tokens, :]
                for h in range(2):
                    qh = jnp.where(heads[h], qb, 0)
                    dogh = jnp.where(heads[h], dog, 0.0).astype(BF16)
                    staged.append((_dil_logits(qh, kb, b_ref[h], first, span), _dot_nt(dogh, vb), qh, dogh,
                                   jnp.max(jnp.where(heads[h], lse_b, -jnp.inf), axis=-1, keepdims=True),
                                   jnp.sum(jnp.where(heads[h], row_term, 0.0), axis=-1, keepdims=True)))
            grads = []
            for i, (logits, dp, qh, dogh, lse_h, row) in enumerate(staged):
                p = jnp.exp(logits - lse_h)
                ds = p * (dp - row)
                if group[i // 2][0]:
                    db_ref[i % 2, :, DIL_BLOCK:] += ds
                else:
                    db_ref[i % 2] += ds
                grads.append(((ds * DIL_SCALE).astype(BF16), p.astype(BF16), qh, dogh))
            for i, (_, blk, kv_rows, _) in enumerate(group):
                kb = k_ref[kv_rows, :]
                dq_acc = jnp.zeros((DIL_BLOCK, HEAD_PAD), F32)
                dk_blk = jnp.zeros((kb.shape[0], HEAD_PAD), F32)
                dv_blk = jnp.zeros((kb.shape[0], HEAD_PAD), F32)
                for h in range(2):
                    dsb, pb, qh, dogh = grads[2 * i + h]
                    kmine = (klane[:kb.shape[0]] < DIL_HEAD_DIM) == (h == 0)
                    dq_acc = dq_acc + _dot_nn(dsb, jnp.where(kmine, kb, 0))
                    dk_blk = dk_blk + _dot_tn(dsb, qh)
                    dv_blk = dv_blk + _dot_tn(pb, dogh)
                dq_ref[blk, :] = dq_acc.astype(BF16)
                dk_acc[kv_rows, :] += dk_blk
                dv_acc[kv_rows, :] += dv_blk
        dk_ref[...] = dk_acc[...].astype(BF16)
        dv_ref[...] = dv_acc[...].astype(BF16)

    grad = jax.ShapeDtypeStruct((s_dim, d_dim), BF16)
    return _pcall(
        body, out_shape=(grad, grad, grad, jax.ShapeDtypeStruct(bias.shape, F32)), grid=(DIL_PAIRS,),
        in_specs=[col(0), col(1), col(2), bias_spec, nat, nat, nat, nat],
        out_specs=(nat, nat, nat, bias_spec), name=name,
        scratch_shapes=[pltpu.VMEM((s_dim, HEAD_PAD), F32), pltpu.VMEM((s_dim, HEAD_PAD), F32)],
        compiler_params=_params(("parallel",), 6 * [((s_dim, HEAD_PAD), BF16)] + 4 * [((s_dim, HEAD_PAD), F32)]
                                + 2 * [((2, DIL_BLOCK, 2 * DIL_BLOCK), F32)], extra=2 * s_dim * HEAD_PAD * 4 + 2**21),
    )(qkv, qkv, qkv, bias, d_o, o_mix, alpha, lse)


def _bias_reduce(dbias, buckets, name):
    n_heads = dbias.shape[0]

    def body(db_ref, bk_ref, o_ref):
        ds, bk = db_ref[0], bk_ref[0]
        lane = lax.broadcasted_iota(jnp.int32, (8, HEAD_PAD), 1)
        acc = jnp.zeros((8, HEAD_PAD), F32)
        for b in range(N_BUCKETS):
            acc = jnp.where(lane == b, jnp.sum(jnp.where(bk == b, ds, 0.0)), acc)
        o_ref[0] = acc

    blk = (1, DIL_BLOCK, 2 * DIL_BLOCK)
    return _pcall(
        body, out_shape=jax.ShapeDtypeStruct((n_heads, 8, HEAD_PAD), F32), grid=(n_heads,),
        in_specs=[pl.BlockSpec(blk, lambda h: (h, 0, 0)), pl.BlockSpec(blk, lambda h: (h // DIL_HEADS, 0, 0))],
        out_specs=pl.BlockSpec((1, 8, HEAD_PAD), lambda h: (h, 0, 0)), name=name,
        compiler_params=_params(("parallel",), [(blk, F32), (blk, jnp.int32)], extra=2**20),
    )(dbias, buckets)


def _loss_grad(y, target, name):
    s_dim, d_dim = y.shape
    tm = TOKEN_TILE

    def body(y_ref, t_ref, dy_ref, l_ref):
        @pl.when(pl.program_id(0) == 0)
        def _():
            l_ref[...] = jnp.zeros_like(l_ref)

        err = y_ref[...] - t_ref[...]
        dy_ref[...] = err / d_dim
        sq = (err * err).reshape(tm // 8, 8, d_dim)
        l_ref[...] += 0.5 * jnp.sum(sq, axis=0) / d_dim

    row = pl.BlockSpec((tm, d_dim), lambda i: (i, 0))
    acc = pl.BlockSpec((8, d_dim), lambda i: (0, 0))
    return _pcall(
        body, out_shape=(jax.ShapeDtypeStruct((s_dim, d_dim), F32), jax.ShapeDtypeStruct((8, d_dim), F32)),
        grid=(s_dim // tm,), in_specs=[row, row], out_specs=(row, acc), name=name,
        compiler_params=_params(("arbitrary",), 3 * [((tm, d_dim), F32)], extra=2 * tm * d_dim * 4),
    )(y, target)


def _mod_fwd(c_all, w_mod, b_loc, name):
    depth, d_dim, n = w_mod.shape
    nb = c_all.shape[0]

    def body(c_ref, w_ref, b_ref, o_ref, s_ref):
        cv = c_ref[...]
        sc = cv * jax.nn.sigmoid(cv)
        s_ref[...] = sc
        o_ref[0] = _dot_nn(sc.astype(BF16), w_ref[0].astype(BF16)) + b_ref[0]

    return _pcall(
        body, out_shape=(jax.ShapeDtypeStruct((depth, nb, n), F32), jax.ShapeDtypeStruct((nb, d_dim), F32)), grid=(depth,),
        in_specs=[pl.BlockSpec((nb, d_dim), lambda i: (0, 0)), pl.BlockSpec((1, d_dim, n), lambda i: (i, 0, 0)),
                  pl.BlockSpec((1, 1, n), lambda i: (i, 0, 0))],
        out_specs=(pl.BlockSpec((1, nb, n), lambda i: (i, 0, 0)), pl.BlockSpec((nb, d_dim), lambda i: (0, 0))), name=name,
        compiler_params=_params(("arbitrary",), [((1, d_dim, n), F32)], extra=d_dim * n * 2 + 2**20),
    )(c_all, w_mod, b_loc.reshape(depth, 1, n))


def _sum_parts(parts, name, transpose=False):
    _, rows, cols = parts.shape
    unit = 128 if transpose else 16
    budget = (7 if transpose else 3) * 2**20
    fits = [t for t in range(unit, rows // 2 + 1, unit) if rows % t == 0 and NDEV * t * cols * parts.dtype.itemsize <= budget]
    tr = max(fits) if fits else rows

    def body(p_ref, o_ref):
        acc = p_ref[0].astype(F32)
        for k in range(1, NDEV):
            acc = acc + p_ref[k].astype(F32)
        o_ref[...] = acc.T if transpose else acc

    out_shape, out_block = ((cols, rows), (cols, tr)) if transpose else ((rows, cols), (tr, cols))
    return _pcall(
        body, out_shape=jax.ShapeDtypeStruct(out_shape, F32), grid=(rows // tr,),
        in_specs=[pl.BlockSpec((NDEV, tr, cols), lambda i: (0, i, 0))],
        out_specs=pl.BlockSpec(out_block, (lambda i: (0, i)) if transpose else (lambda i: (i, 0))),
        name=name, compiler_params=_params(("parallel",), [((NDEV, tr, cols), parts.dtype), (out_block, F32)], extra=2**22),
    )(parts)


def _adamw(w, g, m, v, name):
    shape = w.shape
    cols = shape[-1]
    rows = math.prod(shape[:-1])
    tr = rows
    for cand in (2048, 1024, 512, 256, 128, 64, 32, 16, 8):
        if rows % cand == 0 and rows > cand and cand * cols * 4 <= 2**21:
            tr = cand
            break

    def body(w_ref, g_ref, m_ref, v_ref, d_ref, mo_ref, vo_ref):
        gv = g_ref[...]
        mn = ADAM_B1 * m_ref[...] + (1.0 - ADAM_B1) * gv
        vn = ADAM_B2 * v_ref[...] + (1.0 - ADAM_B2) * (gv * gv)
        m_hat = mn / (1.0 - ADAM_B1 ** ADAM_STEP)
        v_hat = vn / (1.0 - ADAM_B2 ** ADAM_STEP)
        d_ref[...] = -ADAM_LR * (m_hat / (jnp.sqrt(v_hat) + ADAM_EPS) + ADAM_WD * w_ref[...])
        mo_ref[...] = mn
        vo_ref[...] = vn

    blk = pl.BlockSpec((tr, cols), lambda i: (i, 0))
    out = jax.ShapeDtypeStruct((rows, cols), F32)
    res = _pcall(
        body, out_shape=(out, out, out), grid=(rows // tr,), in_specs=4 * [blk], out_specs=(blk, blk, blk), name=name,
        compiler_params=_params(("parallel",), 7 * [((tr, cols), F32)], extra=4 * tr * cols * 4),
    )(*(a.reshape(rows, cols) for a in (w, g, m, v)))
    return tuple(r.reshape(shape) for r in res)


def _peers():
    x, y, c = lax.axis_index("x"), lax.axis_index("y"), lax.axis_index("c")
    flip = lambda v, f: 1 - v if f else v
    peers = []
    for f in range(1, NDEV):
        px, py, pc = flip(x, f & 4), flip(y, f & 2), flip(c, f & 1)
        peers.append(((px, py, pc), 4 * px + 2 * py + pc))
    return (x, y, c), 4 * x + 2 * y + c, peers


def _places():
    x, y, c = lax.axis_index("x"), lax.axis_index("y"), lax.axis_index("c")
    place = lambda px, py, pc: ((px, py, pc), 4 * px + 2 * py + pc)
    return place(x, y, c), place(x, y, 1 - c), [place(1 - x, y, c), place(x, 1 - y, c), place(1 - x, 1 - y, c)]


def _exchange(arrs, gather, name):
    n = len(arrs)
    hbm = pl.BlockSpec(memory_space=pltpu.HBM)
    if gather:
        out_shape = [jax.ShapeDtypeStruct((NDEV * a.shape[0], a.shape[1]), a.dtype) for a in arrs]
    else:
        out_shape = [jax.ShapeDtypeStruct((NDEV, a.shape[0] // NDEV, a.shape[1]), a.dtype) for a in arrs]

    def body(*refs):
        ins, outs = refs[:n], refs[n:2 * n]
        send_sems, recv_sems, local_sems = refs[2 * n:]
        me_pos, me, peers = _peers()
        local = []
        for k in range(n):
            rows = arrs[k].shape[0] if gather else arrs[k].shape[0] // NDEV
            if gather:
                src_of = lambda idx: ins[k]
                dst_of = lambda idx: outs[k].at[pl.ds(me * rows, rows)]
                mine = (ins[k], outs[k].at[pl.ds(me * rows, rows)])
            else:
                src_of = lambda idx: ins[k].at[pl.ds(idx * rows, rows)]
                dst_of = lambda idx: outs[k].at[me]
                mine = (ins[k].at[pl.ds(me * rows, rows)], outs[k].at[me])
            cp = pltpu.make_async_copy(mine[0], mine[1], local_sems.at[k])
            cp.start()
            local.append(cp)
            for pos, idx in peers:
                pltpu.make_async_remote_copy(src_ref=src_of(idx), dst_ref=dst_of(idx), send_sem=send_sems.at[k],
                                             recv_sem=recv_sems.at[k], device_id=pos, device_id_type=MESH).start()
        for k in range(n):
            rows = arrs[k].shape[0] if gather else arrs[k].shape[0] // NDEV
            sent = ins[k].at[pl.ds(0, (NDEV - 1) * rows)] if not gather else outs[k].at[pl.ds(0, (NDEV - 1) * rows)]
            got = outs[k].at[pl.ds(0, (NDEV - 1) * rows)] if gather else outs[k].at[pl.ds(0, NDEV - 1)]
            pltpu.make_async_remote_copy(src_ref=sent, dst_ref=sent, send_sem=send_sems.at[k], recv_sem=recv_sems.at[k],
                                         device_id=me_pos, device_id_type=MESH).wait_send()
            pltpu.make_async_remote_copy(src_ref=got, dst_ref=got, send_sem=send_sems.at[k], recv_sem=recv_sems.at[k],
                                         device_id=me_pos, device_id_type=MESH).wait_recv()
            local[k].wait()

    return pl.pallas_call(
        body, out_shape=out_shape, in_specs=n * [hbm], out_specs=n * [hbm], name=name,
        scratch_shapes=[pltpu.SemaphoreType.DMA((n,)), pltpu.SemaphoreType.DMA((n,)), pltpu.SemaphoreType.DMA((n,))],
        compiler_params=pltpu.CompilerParams(has_side_effects=True),
    )(*arrs)


_HBM = pl.BlockSpec(memory_space=pltpu.HBM)
_SEM = pl.BlockSpec(memory_space=pltpu.SEMAPHORE)
_DATAFLOW = pltpu.SideEffectType.DATAFLOW_SIDE_EFFECTING


def _split_start(srcs, groups, gather, name):
    n = len(srcs)
    if gather:
        lands = [lax.empty((NDEV * a.shape[0], a.shape[1]), a.dtype) for a in srcs]
    else:
        lands = [lax.empty((NDEV, a.shape[0] // NDEV, a.shape[1]), a.dtype) for a in srcs]
    n_sem = 3 * len(groups)

    def body(*refs):
        src_refs, land_refs = refs[:n], refs[n:2 * n]
        sems = refs[2 * n:2 * n + n_sem]
        token = refs[-1]
        (_, my), sibling, chips = _places()
        _, _, peers = _peers()
        targets = [sibling] + chips if gather else peers
        for g, members in enumerate(groups):
            for j, k in enumerate(members):
                _own_copy(src_refs[k], land_refs[k], sems[3 * g + 2].at[j], my, gather).start()
        for g, members in enumerate(groups):
            for j, k in enumerate(members):
                rows = srcs[k].shape[0] if gather else srcs[k].shape[0] // NDEV
                for pos, idx in targets:
                    src = src_refs[k] if gather else src_refs[k].at[pl.ds(idx * rows, rows)]
                    dst = land_refs[k].at[pl.ds(my * rows, rows)] if gather else land_refs[k].at[my]
                    pltpu.make_async_remote_copy(src_ref=src, dst_ref=dst, send_sem=sems[3 * g].at[j],
                                                 recv_sem=sems[3 * g + 1].at[j], device_id=pos, device_id_type=MESH).start()
        token[...] = jnp.zeros_like(token)

    out_shape = []
    for members in groups:
        out_shape += 3 * [pltpu.SemaphoreType.DMA((len(members),))]
    out_shape += [pltpu.HBM(a.shape, a.dtype) for a in srcs] + [pltpu.HBM(a.shape, a.dtype) for a in lands]
    out_shape.append(jax.ShapeDtypeStruct((8, 128), F32))
    res = pl.pallas_call(
        body, name=name, out_shape=tuple(out_shape), in_specs=2 * n * [_HBM],
        out_specs=tuple(n_sem * [_SEM] + 2 * n * [_HBM] + [pl.BlockSpec(memory_space=pltpu.VMEM)]),
        input_output_aliases={i: n_sem + i for i in range(2 * n)},
        compiler_params=pltpu.CompilerParams(has_side_effects=_DATAFLOW),
    )(*[pltpu.with_memory_space_constraint(a, pltpu.HBM) for a in list(srcs) + lands])
    sems = [tuple(res[3 * g:3 * g + 3]) for g in range(len(groups))]
    return sems, list(res[n_sem:n_sem + n]), list(res[n_sem + n:n_sem + 2 * n]), res[-1]


def _own_copy(src_ref, land_ref, sem, my, gather):
    if gather:
        rows = src_ref.shape[0]
        return pltpu.make_async_copy(src_ref, land_ref.at[pl.ds(my * rows, rows)], sem)
    rows = src_ref.shape[0] // NDEV
    return pltpu.make_async_copy(src_ref.at[pl.ds(my * rows, rows)], land_ref.at[my], sem)


def _wait_all(land_ref, blocks_per_dev, copies, send_sem, recv_sem, me_pos):
    part = land_ref.at[pl.ds(0, copies * blocks_per_dev)]
    pltpu.make_async_remote_copy(src_ref=part, dst_ref=part, send_sem=send_sem, recv_sem=recv_sem,
                                 device_id=me_pos, device_id_type=MESH).wait()


def _gather_forward(sems, srcs, lands, after, name):
    n = len(srcs)

    def body(*refs):
        land_refs = refs[n:2 * n]
        send_a, recv_a = refs[2 * n], refs[2 * n + 1]
        send_b, recv_b = refs[2 * n + 3], refs[2 * n + 4]
        token = refs[-1]
        (me_pos, _), sibling, chips = _places()
        for j in range(n):
            _wait_all(land_refs[j], lands[j].shape[0] // NDEV, 1 + OTHER_CHIPS, send_a.at[j], recv_a.at[j], me_pos)
        for j in range(n):
            rows = lands[j].shape[0] // NDEV
            for _, idx in chips:
                block = land_refs[j].at[pl.ds(idx * rows, rows)]
                pltpu.make_async_remote_copy(src_ref=block, dst_ref=block, send_sem=send_b.at[j], recv_sem=recv_b.at[j],
                                             device_id=sibling[0], device_id_type=MESH).start()
        token[...] = jnp.zeros_like(token)

    res = pl.pallas_call(
        body, name=name,
        out_shape=(pltpu.SemaphoreType.DMA((n,)), pltpu.SemaphoreType.DMA((n,)))
        + tuple(pltpu.HBM(a.shape, a.dtype) for a in list(srcs) + list(lands)) + (jax.ShapeDtypeStruct((8, 128), F32),),
        in_specs=2 * n * [_HBM] + [_SEM, _SEM, pl.BlockSpec(memory_space=pl.ANY)],
        out_specs=tuple([_SEM, _SEM] + 2 * n * [_HBM] + [pl.BlockSpec(memory_space=pltpu.VMEM)]),
        input_output_aliases={i: 2 + i for i in range(2 * n)},
        compiler_params=pltpu.CompilerParams(has_side_effects=_DATAFLOW),
    )(*srcs, *lands, sems[0], sems[1], after)
    return (res[0], res[1]), list(res[2:2 + n]), list(res[2 + n:2 + 2 * n]), res[-1]


def _split_wait(sems, srcs, lands, after, copies, gather, name):
    n = len(srcs)

    def body(*refs):
        src_refs, land_refs = refs[:n], refs[n:2 * n]
        send_sem, recv_sem, local_sem = refs[2 * n], refs[2 * n + 1], refs[2 * n + 2]
        (me_pos, my), _, _ = _places()
        for j in range(n):
            _wait_all(land_refs[j], lands[j].shape[0] // NDEV, copies, send_sem.at[j], recv_sem.at[j], me_pos)
            _own_copy(src_refs[j], land_refs[j], local_sem.at[j], my, gather).wait()

    res = pl.pallas_call(
        body, name=name, out_shape=tuple(pltpu.HBM(a.shape, a.dtype) for a in list(srcs) + list(lands)),
        in_specs=2 * n * [_HBM] + [_SEM, _SEM, _SEM, pl.BlockSpec(memory_space=pl.ANY)], out_specs=tuple(2 * n * [_HBM]),
        input_output_aliases={i: i for i in range(2 * n)},
        compiler_params=pltpu.CompilerParams(has_side_effects=_DATAFLOW),
    )(*srcs, *lands, sems[0], sems[1], sems[2], after)
    return list(res[n:])


def _chained(gate, mid, after):
    return gate if mid is None else gate + mid(after)[:1, :1]


def _ffn_fwd(x, norms, mod, w, mid=None):
    (pre_g, post_g), (shift, scale, gate), (wg_t, wu_t, wd) = norms, mod, w
    if not callable(wd):
        hn, g, u, a, x_out, f = _ffn_fwd_fused(x, pre_g, scale, shift, post_g, _chained(gate, mid, x), wg_t, wu_t, wd, "ffn_fwd")
        return x_out, (x, hn, g, u, a, f), (wg_t, wu_t, wd)
    hn, g, u, a = _ffn_up(x, pre_g, scale, shift, wg_t, wu_t, "ffn_up")
    wd = wd(a)
    x_out, f = _mm_post(a, wd, x, post_g, _chained(gate, mid, a), FFN_RES, "ffn_down")
    return x_out, (x, hn, g, u, a, f), (wg_t, wu_t, wd)


def _ffn_bwd(dx_out, saved, norms, mod, w, send=None):
    (pre_g, post_g), (_, scale, gate), (wg_t, wu_t, wd) = norms, mod, w
    x, hn, g, u, a, f = saved
    d_model = x.shape[1]
    if send is None:
        df, dg, du, dx, dgate, dpost, dshift, dscale, dpre = _ffn_bwd_fused(dx_out, saved, pre_g, post_g, scale, gate,
                                                                            wg_t, wu_t, wd, "ffn_bwd")
        return dx, (dpre, dpost), (dshift, dscale, dgate), tuple(_ffn_dw(dg, du, a, hn, df, "ffn_dw3"))
    sent = send
    df, dgate, dpost = _post_bwd(dx_out, f, post_g, gate, FFN_RES, "ffn_post_bwd")
    dwd = _mm([(a, df)], "tn", BF16, 256, d_model, "ffn_dw")
    dg, du = _ffn_dgu(df, wd, g, u, "ffn_dgu", after=sent(2, dwd))
    dwg_t = _mm([(dg, hn)], "tn", BF16, 256, d_model, "ffn_dw")
    dwu_t = _mm([(du, hn)], "tn", BF16, 256, d_model, "ffn_dw", after=sent(0, dwg_t))
    dhn = _mm([(dg, wg_t), (du, wu_t)], "nn", F32, TOKEN_TILE, d_model, "ffn_dhn", after=sent(1, dwu_t))
    dx, dshift, dscale, dpre = _prenorm_bwd(dx_out, [dhn], x, pre_g, scale, "prenorm_bwd")
    return dx, (dpre, dpost), (dshift, dscale, dgate), (dwg_t, dwu_t, dwd)


def _mla_fwd(x, norms, mod, w, rope, mid=None):
    (pre_g, post_g), (shift, scale, gate) = norms, mod
    w_in, q_norm, wq_t, kv_norm, wkv_t, wo = w
    hn, lat = _prenorm_mm(x, pre_g, scale, shift, w_in, "nn", F32, LAT_PAD, "mla_in")
    gate = _chained(gate, mid, lat)
    q, k, v, qn, kvn = _mla_qkv(lat, q_norm, kv_norm, wq_t, wkv_t, rope, "mla_qkv")
    o = _mla_attn_fwd(q, k, v, "mla_attn_fwd")
    x_out, f = _mm_post(o, wo, x, post_g, gate, 1.0, "mla_out")
    return x_out, (x, hn, lat, q, k, v, qn, kvn, o, f)


def _mla_bwd(dx_out, saved, norms, mod, w, rope):
    (pre_g, post_g), (_, scale, gate) = norms, mod
    w_in, q_norm, wq_t, kv_norm, wkv_t, wo = w
    x, hn, lat, q, k, v, qn, kvn, o, f = saved
    d_model = x.shape[1]
    df, dgate, dpost = _post_bwd(dx_out, f, post_g, gate, 1.0, "mix_post_bwd")
    d_o = _mm([(df, wo)], "nt", F32, TOKEN_TILE, wo.shape[0], "mla_do")
    dwo = _mm([(o, df)], "tn", BF16, TOKEN_TILE, d_model, "mla_dwo")
    dq, dk, dv = _mla_attn_bwd(q, k, v, d_o, "mla_attn_bwd")
    dqp, dkv, dlat, dq_norm, dkv_norm = _mla_qkv_bwd(dq, dk, dv, lat, q_norm, kv_norm, wq_t, wkv_t, rope, "mla_qkv_bwd")
    dwq_t = _mm([(dqp, qn)], "tn", BF16, TOKEN_TILE, Q_LORA, "mla_dwq")
    dwkv_t = _mm([(dkv, kvn)], "tn", BF16, TOKEN_TILE, KV_LORA, "mla_dwkv")
    dw_in = _mm([(hn, dlat)], "tn", BF16, TOKEN_TILE, LAT_PAD, "mla_dwin")
    dhn = _mm([(dlat, w_in)], "nt", F32, TOKEN_TILE, d_model, "mla_dhn")
    dx, dshift, dscale, dpre = _prenorm_bwd(dx_out, [dhn], x, pre_g, scale, "prenorm_bwd")
    return dx, (dpre, dpost), (dshift, dscale, dgate), (dw_in, dq_norm, dwq_t, dkv_norm, dwkv_t, dwo)


def _dil_fwd(x, norms, mod, w, bias, mid=None):
    (pre_g, post_g), (shift, scale, gate), (w_in_t, wo) = norms, mod, w
    width = 3 * DIL_HEADS * DIL_HEAD_DIM
    hns, qkvs, outs, lses = [], [], [], []
    for g, (window, dilation) in enumerate(DIL_GROUPS):
        hn, qkv = _prenorm_mm(x, pre_g, scale, shift, w_in_t, "nt", BF16, width, "dil_in", perm=dilation,
                              w_rows=(g * width, width))
        if g == 0:
            gate = _chained(gate, mid, qkv)
        o, lse = _dil_attn_fwd(qkv, bias[g], dilation, window // dilation, "dil_attn_fwd")
        hns.append(hn), qkvs.append(qkv), outs.append(o), lses.append(lse)
    alphas, o_mix, o_mix_b = _dil_mix(lses, outs, "dil_mix")
    x_out, f = _mm_post(o_mix_b, wo, x, post_g, gate, 1.0, "dil_out")
    return x_out, (x, hns, qkvs, lses, alphas, o_mix, o_mix_b, f)


def _dil_bwd(dx_out, saved, norms, mod, w, bias):
    (pre_g, post_g), (_, scale, gate), (w_in_t, wo) = norms, mod, w
    x, hns, qkvs, lses, alphas, o_mix, o_mix_b, f = saved
    d_model = x.shape[1]
    inner = DIL_HEADS * DIL_HEAD_DIM
    df, dgate, dpost = _post_bwd(dx_out, f, post_g, gate, 1.0, "mix_post_bwd")
    d_o = _mm([(df, wo)], "nt", F32, TOKEN_TILE, inner, "dil_do")
    dwo = _mm([(o_mix_b, df)], "tn", BF16, TOKEN_TILE, d_model, "dil_dwo")
    dhns, dws, dbs = [], [], []
    for g, (window, dilation) in enumerate(DIL_GROUPS):
        grads = _dil_attn_bwd(qkvs[g], bias[g], d_o, o_mix, alphas[g], lses[g], dilation, window // dilation, "dil_attn_bwd")
        dbs.append(grads[3])
        dhns.append(_mm([(grads[j], w_in_t) for j in range(3)], "nn", F32, TOKEN_TILE, d_model, "dil_dhn", out_perm=dilation,
                        b_rows=[(3 * g + j) * inner for j in range(3)]))
        dws += list(_mm_tn_shared(list(grads[:3]), hns[g], "dil_dwin"))
    dx, dshift, dscale, dpre = _prenorm_bwd(dx_out, dhns, x, pre_g, scale, "prenorm_bwd3")
    return dx, (dpre, dpost), (dshift, dscale, dgate), (jnp.concatenate(dws, axis=0), dwo), jnp.concatenate(dbs, axis=0)


def _pad_rows(a, rows):
    return jnp.pad(a, ((0, rows - a.shape[0]), (0, 0)))


def _lanes(a):
    flat = a.reshape(-1).astype(F32)
    rows = -(-flat.shape[0] // 1024) * 8
    return jnp.pad(flat, (0, rows * 128 - flat.shape[0])).reshape(rows, 128)


def kernel(x, c, norm_pre, norm_post, w_mod, b_mod, ffn_w_gate, ffn_w_up, ffn_w_down, mla_w_in, mla_q_norm, mla_w_q_up, mla_kv_norm, mla_w_kv_up, mla_w_o, dil_w_in, dil_w_o, rel_bias, loss_target, m_norm_pre, m_norm_post, m_w_mod, m_b_mod, m_ffn_w_gate, m_ffn_w_up, m_ffn_w_down, m_mla_w_in, m_mla_q_norm, m_mla_w_q_up, m_mla_kv_norm, m_mla_w_kv_up, m_mla_w_o, m_dil_w_in, m_dil_w_o, m_rel_bias, v_norm_pre, v_norm_post, v_w_mod, v_b_mod, v_ffn_w_gate, v_ffn_w_up, v_ffn_w_down, v_mla_w_in, v_mla_q_norm, v_mla_w_q_up, v_mla_kv_norm, v_mla_w_kv_up, v_mla_w_o, v_dil_w_in, v_dil_w_o, v_rel_bias):
    me = 4 * lax.axis_index("x") + 2 * lax.axis_index("y") + lax.axis_index("c")
    depth, n_sub, d_loc = norm_pre.shape
    d_model = x.shape[2]
    mod_loc_cols = w_mod.shape[2]
    x0, target = x[0], loss_target[0]

    bf_t = lambda a: a.astype(BF16).T
    ffn_ids = [(i, h) for i in range(depth) for h in range(2)]
    shards = []
    for i, h in ffn_ids:
        shards += [bf_t(ffn_w_gate[i, h]), bf_t(ffn_w_up[i, h]), ffn_w_down[i, h].astype(BF16)]
    shards += [mla_w_in[0].astype(BF16), bf_t(mla_w_q_up[0]), bf_t(mla_w_kv_up[0]), mla_w_o[0].astype(BF16),
               bf_t(dil_w_in[0]), dil_w_o[0].astype(BF16)]
    n_ffn = 3 * len(ffn_ids)
    members = {(0, 0): [0, 1, 2], (0, 1): [n_ffn, n_ffn + 1, n_ffn + 2, n_ffn + 3], (0, 2): [3, 4, 5],
               (1, 0): [6, 7, 8], (1, 1): [n_ffn + 4, n_ffn + 5], (1, 2): [9, 10, 11]}
    order = [(i, s) for i in range(depth) for s in range(n_sub)]

    small = jnp.concatenate([c.reshape(8, 128), _pad_rows(norm_pre.reshape(depth * n_sub, d_loc), 8),
                             _pad_rows(norm_post.reshape(depth * n_sub, d_loc), 8)], axis=0)
    small_all = _exchange([small], True, "gather_small")[0].reshape(NDEV, 24, 128)
    c_all = small_all[:, 0:8].reshape(NDEV, d_model)
    gains = lambda lo: jnp.transpose(small_all[:, lo:lo + depth * n_sub], (1, 0, 2)).reshape(depth, n_sub, 1, d_model)
    pre_full, post_full = gains(8), gains(16)

    b_loc = lax.dynamic_slice(b_mod, (0, me * mod_loc_cols), (depth, mod_loc_cols))
    mod_cols, silu_c = _mod_fwd(c_all, w_mod, b_loc, "mod_fwd")
    mod_all = _exchange([mod_cols.reshape(depth * NDEV, mod_loc_cols)], True, "gather_mod")[0]
    mod_all = mod_all.reshape(NDEV, depth, NDEV, mod_loc_cols)
    mod_mine = lax.dynamic_index_in_dim(mod_all, me, axis=2, keepdims=False)
    mod = jnp.transpose(mod_mine, (1, 0, 2)).reshape(depth, n_sub, 3, 1, d_model)

    shards[0], _ = lax.optimization_barrier((shards[0], mod_all))
    first = order[0]
    stages = [("%d%d" % first, members[first][:2]), ("%d%dd" % first, members[first][2:])]
    stages += [("%d%d" % key, members[key]) for key in order[1:]]
    started = {}

    def start(these, name):
        used = [k for _, idx in these for k in idx]
        sems, srcs, lands, token = _split_start([shards[k] for k in used], [[used.index(k) for k in idx] for _, idx in these],
                                                True, name)
        for n, (stage, idx) in enumerate(these):
            started[stage] = (sems[n], [srcs[used.index(k)] for k in idx], [lands[used.index(k)] for k in idx])
        return token

    g_token = start(stages[:2], "gather_weights_start_first")
    later = stages[2][1][0]
    shards[later], _ = lax.optimization_barrier((shards[later], g_token))
    start(stages[2:], "gather_weights_start_rest")

    forwarded = {}

    def forward(stage, after):
        sems, srcs, lands = started[stage]
        forwarded[stage] = _gather_forward(sems, srcs, lands, after, "gather_forward_" + stage)
        return forwarded[stage][3]

    def weights_of(stage, after):
        (send_b, recv_b), srcs, lands, _ = forwarded[stage]
        return _split_wait((send_b, recv_b, started[stage][0][2]), srcs, lands, after, OTHER_CHIPS, True, "gather_wait_" + stage)

    def late_down(after):
        forward("%d%dd" % first, after)
        return weights_of("%d%dd" % first, after)[0]

    lat_real = Q_LORA + KV_LORA
    qk = QK_NOPE + QK_ROPE

    def mla_weights(after):
        w_in, wq_t, wkv_t, wo = weights_of("01", after)
        w_in_pad = jnp.concatenate([w_in[:, :lat_real], jnp.zeros((d_model, QK_NOPE), BF16), w_in[:, lat_real:],
                                    jnp.zeros((d_model, HEAD_PAD - QK_NOPE - QK_ROPE), BF16)], axis=1)
        wq_pad = jnp.pad(wq_t.reshape(MLA_HEADS, qk, Q_LORA), ((0, 0), (0, HEAD_PAD - qk), (0, 0)))
        wo_pad = jnp.pad(wo.reshape(MLA_HEADS, V_HEAD, d_model), ((0, 0), (HEAD_PAD - V_HEAD, 0), (0, 0)))
        return (w_in_pad, mla_q_norm, wq_pad.reshape(MLA_HEADS * HEAD_PAD, Q_LORA), mla_kv_norm, wkv_t,
                wo_pad.reshape(MLA_HEADS * HEAD_PAD, d_model))

    zero = g_token[0, 0]
    rope = _rope_tables(zero)
    buckets = jnp.stack([_dil_buckets(dil) for _, dil in DIL_GROUPS]) + zero.astype(jnp.int32)
    onehot = (buckets[..., None] == jnp.arange(N_BUCKETS)).astype(F32)
    bias = jnp.einsum("gqkb,bgh->ghqk", onehot, rel_bias.reshape(N_BUCKETS, len(DIL_GROUPS), DIL_HEADS),
                      precision=lax.Precision.HIGHEST)

    norms = lambda i, s: (pre_full[i, s], post_full[i, s])
    mods = lambda i, s: (mod[i, s, 0], mod[i, s, 1], mod[i, s, 2])
    saved, weights = {}, {}
    h = lax.optimization_barrier((x0, bias, buckets, *rope))[0]
    forward("%d%d" % first, h)
    for n, (i, s) in enumerate(order):
        got = mla_weights(h) if (s == 1 and i % 2 == 0) else tuple(weights_of("%d%d" % (i, s), h))
        mid = None if n + 1 == len(order) else (lambda after, nxt="%d%d" % order[n + 1]: forward(nxt, after))
        if s != 1:
            if len(got) == 3:
                h, saved[i, s], weights[i, s] = _ffn_fwd(h, norms(i, s), mods(i, s), got)
                if mid is not None:
                    mid(h)
            else:
                h, saved[i, s], weights[i, s] = _ffn_fwd(h, norms(i, s), mods(i, s), (*got, late_down), mid)
            continue
        weights[i, s] = got
        if i % 2 == 0:
            h, saved[i, s] = _mla_fwd(h, norms(i, s), mods(i, s), weights[i, s], rope, mid)
        else:
            h, saved[i, s] = _dil_fwd(h, norms(i, s), mods(i, s), weights[i, s], bias, mid)
    dh, loss_parts = _loss_grad(h, target, "loss")

    dnorm, dmod, sent = {}, {}, {}
    token = jnp.zeros((8, 128), F32)
    last = order[0]

    def send_last(j, dw):
        sent[last, j] = _split_start([dw], [[0]], False, "scatter_start_%d%d_%d" % (*last, j))
        return sent[last, j][3]

    for i, s in reversed(order):
        md = mods(i, s)
        md = (md[0], md[1], md[2] + token[:1, :1])
        if (i, s) == last:
            dh, dnorm[i, s], dmod[i, s], _ = _ffn_bwd(dh, saved[i, s], norms(i, s), md, weights[i, s], send_last)
            continue
        if s != 1:
            dh, dnorm[i, s], dmod[i, s], dws = _ffn_bwd(dh, saved[i, s], norms(i, s), md, weights[i, s])
        elif i % 2 == 0:
            dh, dnorm[i, s], dmod[i, s], dmla = _mla_bwd(dh, saved[i, s], norms(i, s), md, weights[i, s], rope)
            dw_in_pad, dq_norm, dwq_pad, dkv_norm, dwkv_t, dwo_pad = dmla
            dw_in = jnp.concatenate([dw_in_pad[:, :lat_real], dw_in_pad[:, lat_real + QK_NOPE:lat_real + qk]], axis=1)
            dwq_t = dwq_pad.reshape(MLA_HEADS, HEAD_PAD, Q_LORA)[:, :qk].reshape(MLA_HEADS * qk, Q_LORA)
            dwo = dwo_pad.reshape(MLA_HEADS, HEAD_PAD, d_model)[:, HEAD_PAD - V_HEAD:].reshape(MLA_HEADS * V_HEAD, d_model)
            dws = (dw_in, dwq_t, dwkv_t, dwo)
        else:
            dh, dnorm[i, s], dmod[i, s], dws, dbias = _dil_bwd(dh, saved[i, s], norms(i, s), md, weights[i, s], bias)
        sent[i, s] = _split_start(list(dws), [list(range(len(dws)))], False, "scatter_start_%d%d" % (i, s))
        token = sent[i, s][3]
    grad_x = dh[None]

    mine = {}
    transposed = {3 * n + j for n in range(len(ffn_ids)) for j in (0, 1)} | {n_ffn + 1, n_ffn + 2, n_ffn + 4}
    for key in reversed(order[1:]):
        sems, srcs, lands, _ = sent[key]
        parts = _split_wait(sems[0], srcs, lands, dh, NDEV - 1, False, "scatter_wait_%d%d" % key)
        for k, p in zip(members[key], parts):
            mine[k] = _sum_parts(p, "sum_parts", k in transposed)
    g_mla_in, g_q_up, g_kv_up, g_mla_o, g_dil_in, g_dil_o = (mine[k] for k in range(n_ffn, n_ffn + 6))
    g_mla_in, g_q_up, g_kv_up, g_mla_o = g_mla_in[None], g_q_up[None], g_kv_up[None], g_mla_o[None]
    g_dil_in, g_dil_o = g_dil_in[None], g_dil_o[None]
    early = {"mla_w_in": _adamw(mla_w_in, g_mla_in, m_mla_w_in, v_mla_w_in, "adamw"),
             "mla_w_q_up": _adamw(mla_w_q_up, g_q_up, m_mla_w_q_up, v_mla_w_q_up, "adamw"),
             "mla_w_kv_up": _adamw(mla_w_kv_up, g_kv_up, m_mla_w_kv_up, v_mla_w_kv_up, "adamw"),
             "mla_w_o": _adamw(mla_w_o, g_mla_o, m_mla_w_o, v_mla_w_o, "adamw"),
             "dil_w_in": _adamw(dil_w_in, g_dil_in, m_dil_w_in, v_dil_w_in, "adamw"),
             "dil_w_o": _adamw(dil_w_o, g_dil_o, m_dil_w_o, v_dil_w_o, "adamw")}
    dbias_sums = _bias_reduce(dbias, buckets, "bias_reduce")
    tied = lax.optimization_barrier((dbias_sums, *[a for step in early.values() for a in step]))
    dbias_sums, early = tied[0], {name: tuple(tied[1 + 3 * n:4 + 3 * n]) for n, name in enumerate(early)}
    for j in (2, 0, 1):
        sems, srcs, lands, _ = sent[last, j]
        parts = _split_wait(sems[0], srcs, lands, dbias_sums, NDEV - 1, False, "scatter_wait_%d%d_%d" % (*last, j))
        mine[members[last][j]] = _sum_parts(parts[0], "sum_parts", members[last][j] in transposed)
    g_gate = jnp.stack([mine[3 * n] for n in range(len(ffn_ids))]).reshape(ffn_w_gate.shape)
    g_up = jnp.stack([mine[3 * n + 1] for n in range(len(ffn_ids))]).reshape(ffn_w_up.shape)
    g_down = jnp.stack([mine[3 * n + 2] for n in range(len(ffn_ids))]).reshape(ffn_w_down.shape)

    dmod_mine = jnp.concatenate([jnp.concatenate(dmod[i, s], axis=0) for i in range(depth) for s in range(n_sub)], axis=0)
    dpre_mine = jnp.concatenate([dnorm[i, s][0] for i in range(depth) for s in range(n_sub)], axis=0)
    dpost_mine = jnp.concatenate([dnorm[i, s][1] for i in range(depth) for s in range(n_sub)], axis=0)
    dbias_tab = dbias_sums[:, 0, :N_BUCKETS].T
    pieces = [dmod_mine, dpre_mine, dpost_mine, dq_norm, dkv_norm, dbias_tab, jnp.sum(loss_parts).reshape(1, 1)]
    packed = [_lanes(p) for p in pieces]
    offs = [0]
    for p in packed:
        offs.append(offs[-1] + p.shape[0])
    everyone = _exchange([jnp.concatenate(packed, axis=0)], True, "gather_small_grads")[0].reshape(NDEV, offs[-1], 128)
    total = _sum_parts(everyone, "sum_small")
    take = lambda n, shape: total[offs[n]:offs[n + 1]].reshape(-1)[:math.prod(shape)].reshape(shape)
    g_b_mod = take(0, b_mod.shape)
    col0 = me * d_loc
    g_norm_pre = lax.dynamic_slice(take(1, (depth, n_sub, d_model)), (0, 0, col0), norm_pre.shape)
    g_norm_post = lax.dynamic_slice(take(2, (depth, n_sub, d_model)), (0, 0, col0), norm_post.shape)
    g_q_norm, g_kv_norm = take(3, mla_q_norm.shape), take(4, mla_kv_norm.shape)
    g_rel_bias = take(5, rel_bias.shape)
    loss = take(6, ())

    dmod_all = everyone[:, offs[0]:offs[1]].reshape(NDEV, depth, NDEV * mod_loc_cols)
    dmod_cols = lax.dynamic_slice(dmod_all, (0, 0, me * mod_loc_cols), (NDEV, depth, mod_loc_cols))
    silu_t = jnp.pad(silu_c.T, ((0, 0), (0, HEAD_PAD - NDEV)))
    g_w_mod = jnp.stack([_mm([(silu_t, jnp.pad(dmod_cols[:, i], ((0, HEAD_PAD - NDEV), (0, 0))))], "nn", F32, TOKEN_TILE,
                             mod_loc_cols, "mod_bwd") for i in range(depth)])

    ws = (norm_pre, norm_post, w_mod, b_mod, ffn_w_gate, ffn_w_up, ffn_w_down, mla_w_in, mla_q_norm, mla_w_q_up, mla_kv_norm,
          mla_w_kv_up, mla_w_o, dil_w_in, dil_w_o, rel_bias)
    gs = (g_norm_pre, g_norm_post, g_w_mod, g_b_mod, g_gate, g_up, g_down, g_mla_in, g_q_norm, g_q_up, g_kv_norm, g_kv_up,
          g_mla_o, g_dil_in, g_dil_o, g_rel_bias)
    ms = (m_norm_pre, m_norm_post, m_w_mod, m_b_mod, m_ffn_w_gate, m_ffn_w_up, m_ffn_w_down, m_mla_w_in, m_mla_q_norm,
          m_mla_w_q_up, m_mla_kv_norm, m_mla_w_kv_up, m_mla_w_o, m_dil_w_in, m_dil_w_o, m_rel_bias)
    vs = (v_norm_pre, v_norm_post, v_w_mod, v_b_mod, v_ffn_w_gate, v_ffn_w_up, v_ffn_w_down, v_mla_w_in, v_mla_q_norm,
          v_mla_w_q_up, v_mla_kv_norm, v_mla_w_kv_up, v_mla_w_o, v_dil_w_in, v_dil_w_o, v_rel_bias)
    names = ("norm_pre", "norm_post", "w_mod", "b_mod", "ffn_w_gate", "ffn_w_up", "ffn_w_down", "mla_w_in", "mla_q_norm",
             "mla_w_q_up", "mla_kv_norm", "mla_w_kv_up", "mla_w_o", "dil_w_in", "dil_w_o", "rel_bias")
    stepped = [early[n] if n in early else _adamw(w, g, m, v, "adamw") for n, w, g, m, v in zip(names, ws, gs, ms, vs)]
    deltas, new_m, new_v = zip(*stepped)
    return (loss, grad_x, *gs, *deltas, *new_m, *new_v)
```

```python
import math

import jax
import jax.numpy as jnp
from jax import lax
from jax.experimental import pallas as pl
from jax.experimental.pallas import tpu as pltpu

F32 = jnp.float32
BF16 = jnp.bfloat16
MESH = pl.DeviceIdType.MESH

NDEV = 8
OTHER_CHIPS = 3
D_MODEL = 1024
SEQ = 2048
D_FF = 2816
EPS = 1e-6
FFN_RES = 0.5
FFN_CHUNKS = 11

MLA_HEADS = 16
Q_LORA = 384
KV_LORA = 256
QK_NOPE = 64
QK_ROPE = 32
V_HEAD = 64
ROPE_THETA = 10000.0
HEAD_PAD = 128
LAT_PAD = Q_LORA + KV_LORA + HEAD_PAD
MLA_SCALE = (QK_NOPE + QK_ROPE) ** -0.5
MLA_QUERY_TILE = 256

DIL_GROUPS = ((128, 1), (512, 4), (2048, 16))
DIL_HEADS = 16
DIL_HEAD_DIM = 64
DIL_BLOCK = 128
DIL_PAIRS = DIL_HEADS // 2
DIL_SCALE = DIL_HEAD_DIM ** -0.5
DIL_GROUPED = 8
N_BUCKETS = 32
MAX_DISTANCE = 2048

ADAM_LR = 0.001
ADAM_B1 = 0.9
ADAM_B2 = 0.999
ADAM_EPS = 1e-08
ADAM_WD = 0.01
ADAM_STEP = 10

V7X_VMEM_BYTES = 64 * 2**20
VMEM_RESERVE = 10 * 2**20
TOKEN_TILE = 512


def _nbytes(shape, dtype):
    return math.prod(shape) * jnp.dtype(dtype).itemsize


def _params(semantics, blocks, extra=0):
    need = 2 * sum(_nbytes(s, d) for s, d in blocks) + extra + VMEM_RESERVE
    return pltpu.CompilerParams(dimension_semantics=semantics,
                                vmem_limit_bytes=int(min(need, V7X_VMEM_BYTES - VMEM_RESERVE)))


def _pcall(body, out_shape, **kw):
    call = pl.pallas_call(body, out_shape=jax.tree.map(lambda s: pltpu.HBM(s.shape, s.dtype), out_shape), **kw)
    return lambda *args: call(*[pltpu.with_memory_space_constraint(a, pltpu.HBM) for a in args])


def _dot_nn(a, b):
    return lax.dot_general(a, b, (((1,), (0,)), ((), ())), preferred_element_type=F32)


def _dot_nt(a, b):
    return lax.dot_general(a, b, (((1,), (1,)), ((), ())), preferred_element_type=F32)


def _dot_tn(a, b):
    return lax.dot_general(a, b, (((0,), (0,)), ((), ())), preferred_element_type=F32)


_DOTS = {"nn": _dot_nn, "nt": _dot_nt, "tn": _dot_tn}


def _rstd(v):
    return lax.rsqrt(jnp.mean(v * v, axis=-1, keepdims=True) + EPS)


def _rms_bwd(v, r, t):
    return r * t - v * (r * r * r) * jnp.mean(t * v, axis=-1, keepdims=True)


_TOKEN_SPEC = pl.BlockSpec((8, 128), lambda *_: (0, 0))


def _mm(pairs, mode, out_dtype, tm, tn, name, out_perm=1, after=None, b_rows=None):
    a0, b0 = pairs[0]
    m_dim = a0.shape[1] if mode == "tn" else a0.shape[0]
    n_dim = b0.shape[0] if mode == "nt" else b0.shape[1]
    tm, tn = min(tm, m_dim // out_perm), min(tn, n_dim)
    assert m_dim % tm == 0 and n_dim % tn == 0, (name, m_dim, n_dim, tm, tn)
    dot = _DOTS[mode]
    npairs = len(pairs)

    def body(*refs):
        acc = None
        for p in range(npairs):
            d = dot(refs[2 * p][...].astype(BF16), refs[2 * p + 1][...].astype(BF16))
            acc = d if acc is None else acc + d
        refs[-1][...] = acc.astype(out_dtype)

    in_specs, blocks, flat = [], [], []
    for n_pair, (a, b) in enumerate(pairs):
        if mode == "nn":
            k = a.shape[1]
            first_block = 0 if b_rows is None else b_rows[n_pair] // k
            sa, sb = ((tm, k), lambda i, j: (i, 0)), ((k, tn), lambda i, j, o=first_block: (o, j))
        elif mode == "nt":
            k = a.shape[1]
            sa, sb = ((tm, k), lambda i, j: (i, 0)), ((tn, k), lambda i, j: (j, 0))
        else:
            k = a.shape[0]
            sa, sb = ((k, tm), lambda i, j: (0, i)), ((k, tn), lambda i, j: (0, j))
        in_specs += [pl.BlockSpec(*sa), pl.BlockSpec(*sb)]
        blocks += [(sa[0], a.dtype), (sb[0], b.dtype)]
        flat += [a, b]
    if after is not None:
        in_specs.append(_TOKEN_SPEC)
        flat.append(after)
    if out_perm == 1:
        out_shape = (m_dim, n_dim)
        out_spec = pl.BlockSpec((tm, tn), lambda i, j: (i, j))
    else:
        rows = m_dim // out_perm
        assert tn == n_dim and rows % tm == 0, (name, rows, tm)
        nb = rows // tm
        out_shape = (rows, out_perm * n_dim)
        out_spec = pl.BlockSpec((tm, n_dim), lambda i, j: (i % nb, i // nb))
    blocks.append(((tm, tn), out_dtype))
    res = _pcall(
        body, out_shape=jax.ShapeDtypeStruct(out_shape, out_dtype), grid=(m_dim // tm, n_dim // tn),
        in_specs=in_specs, out_specs=out_spec, name=name,
        compiler_params=_params(("parallel", "parallel"), blocks, extra=2 * tm * tn * 4),
    )(*flat)
    return res.reshape(m_dim, n_dim)


def _prenorm_mm(x, pre_g, scale, shift, w, w_mode, out_dtype, tn, name, perm=1, w_rows=None):
    s_dim, d_dim = x.shape
    n_dim = w.shape[0] if w_mode == "nt" else w.shape[1]
    w_first = 0
    if w_rows is not None:
        w_first, n_dim = w_rows
    rows = s_dim // perm
    side = max(1, TOKEN_TILE // rows)
    tm = side * min(TOKEN_TILE, rows)
    nb = max(1, rows // tm)
    tn = min(tn, n_dim)
    assert n_dim % tn == 0 and w_first % tn == 0
    w_block0 = w_first // tn
    dot = _DOTS[w_mode]

    def body(x_ref, g_ref, sc_ref, sh_ref, w_ref, hn_ref, o_ref):
        @pl.when(pl.program_id(1) == 0)
        def _():
            xf = x_ref[...]
            if side > 1:
                xf = jnp.concatenate([xf[:, c * d_dim:(c + 1) * d_dim] for c in range(side)], axis=0)
            hn = (xf * _rstd(xf) * g_ref[...]) * (1.0 + sc_ref[...]) + sh_ref[...]
            hn_ref[...] = hn.astype(BF16)

        o_ref[...] = dot(hn_ref[...], w_ref[...]).astype(out_dtype)

    vec = pl.BlockSpec((1, d_dim), lambda i, j: (0, 0))
    w_block = (tn, d_dim) if w_mode == "nt" else (d_dim, tn)
    w_spec = pl.BlockSpec(w_block, (lambda i, j: (w_block0 + j, 0)) if w_mode == "nt" else (lambda i, j: (0, j)))
    hn, out = _pcall(
        body,
        out_shape=(jax.ShapeDtypeStruct((s_dim, d_dim), BF16), jax.ShapeDtypeStruct((s_dim, n_dim), out_dtype)),
        grid=(s_dim // tm, n_dim // tn),
        in_specs=[pl.BlockSpec((tm // side, side * d_dim), lambda i, j: (i % nb, i // nb)), vec, vec, vec, w_spec],
        out_specs=(pl.BlockSpec((tm, d_dim), lambda i, j: (i, 0)), pl.BlockSpec((tm, tn), lambda i, j: (i, j))),
        name=name,
        compiler_params=_params(("parallel", "arbitrary"),
                                [((tm, d_dim), F32), (w_block, BF16), ((tm, d_dim), BF16), ((tm, tn), out_dtype)],
                                extra=3 * tm * d_dim * 4 + tm * tn * 4),
    )(x.reshape(rows, perm * d_dim), pre_g, scale, shift, w)
    return hn, out


def _ffn_up(x, pre_g, scale, shift, wg_t, wu_t, name):
    s_dim, d_dim = x.shape
    f_dim = wg_t.shape[0]
    tm, tn = TOKEN_TILE, f_dim // 2

    def body(x_ref, g_ref, sc_ref, sh_ref, wg_ref, wu_ref, hn_ref, go_ref, uo_ref, a_ref):
        @pl.when(pl.program_id(1) == 0)
        def _():
            xf = x_ref[...]
            hn = (xf * _rstd(xf) * g_ref[...]) * (1.0 + sc_ref[...]) + sh_ref[...]
            hn_ref[...] = hn.astype(BF16)

        hn = hn_ref[...]
        g = _dot_nt(hn, wg_ref[...])
        u = _dot_nt(hn, wu_ref[...])
        go_ref[...] = g.astype(BF16)
        uo_ref[...] = u.astype(BF16)
        a_ref[...] = (g * jax.nn.sigmoid(g) * u).astype(BF16)

    vec = pl.BlockSpec((1, d_dim), lambda i, j: (0, 0))
    w_spec = pl.BlockSpec((tn, d_dim), lambda i, j: (j, 0))
    act = pl.BlockSpec((tm, tn), lambda i, j: (i, j))
    act_shape = jax.ShapeDtypeStruct((s_dim, f_dim), BF16)
    return _pcall(
        body,
        out_shape=(jax.ShapeDtypeStruct((s_dim, d_dim), BF16), act_shape, act_shape, act_shape),
        grid=(s_dim // tm, f_dim // tn),
        in_specs=[pl.BlockSpec((tm, d_dim), lambda i, j: (i, 0)), vec, vec, vec, w_spec, w_spec],
        out_specs=(pl.BlockSpec((tm, d_dim), lambda i, j: (i, 0)), act, act, act),
        name=name,
        compiler_params=_params(("parallel", "arbitrary"),
                                [((tm, d_dim), F32), ((tn, d_dim), BF16), ((tn, d_dim), BF16), ((tm, d_dim), BF16)]
                                + 3 * [((tm, tn), BF16)], extra=3 * tm * d_dim * 4 + 4 * tm * tn * 4),
    )(x, pre_g, scale, shift, wg_t, wu_t)


def _mm_post(a, w, x, post_g, gate, res_w, name):
    s_dim, k_dim = a.shape
    d_dim = w.shape[1]
    tm = TOKEN_TILE

    def body(a_ref, w_ref, x_ref, pg_ref, gt_ref, xo_ref, f_ref):
        f = _dot_nn(a_ref[...], w_ref[...])
        y = f * _rstd(f) * pg_ref[...]
        f_ref[...] = f
        xo_ref[...] = x_ref[...] + (res_w * gt_ref[...]) * y

    vec = pl.BlockSpec((1, d_dim), lambda i: (0, 0))
    row = pl.BlockSpec((tm, d_dim), lambda i: (i, 0))
    out = jax.ShapeDtypeStruct((s_dim, d_dim), F32)
    return _pcall(
        body, out_shape=(out, out), grid=(s_dim // tm,),
        in_specs=[pl.BlockSpec((tm, k_dim), lambda i: (i, 0)), pl.BlockSpec((k_dim, d_dim), lambda i: (0, 0)), row, vec, vec],
        out_specs=(row, row), name=name,
        compiler_params=_params(("parallel",), [((tm, k_dim), BF16), ((k_dim, d_dim), BF16)] + 3 * [((tm, d_dim), F32)],
                                extra=3 * tm * d_dim * 4),
    )(a, w, x, post_g, gate)


def _post_bwd(dx_out, f, post_g, gate, res_w, name):
    s_dim, d_dim = f.shape
    tm = TOKEN_TILE

    def body(dx_ref, f_ref, pg_ref, gt_ref, df_ref, dgate_ref, dpost_ref):
        @pl.when(pl.program_id(0) == 0)
        def _():
            dgate_ref[...] = jnp.zeros_like(dgate_ref)
            dpost_ref[...] = jnp.zeros_like(dpost_ref)

        dx, fv = dx_ref[...], f_ref[...]
        r = _rstd(fv)
        fr = fv * r
        dgate_ref[...] += res_w * jnp.sum(dx * (fr * pg_ref[...]), axis=0, keepdims=True)
        dy = (res_w * gt_ref[...]) * dx
        dpost_ref[...] += jnp.sum(dy * fr, axis=0, keepdims=True)
        df_ref[...] = _rms_bwd(fv, r, dy * pg_ref[...]).astype(BF16)

    vec = pl.BlockSpec((1, d_dim), lambda i: (0, 0))
    row = pl.BlockSpec((tm, d_dim), lambda i: (i, 0))
    vshape = jax.ShapeDtypeStruct((1, d_dim), F32)
    return _pcall(
        body, out_shape=(jax.ShapeDtypeStruct((s_dim, d_dim), BF16), vshape, vshape), grid=(s_dim // tm,),
        in_specs=[row, row, vec, vec], out_specs=(row, vec, vec), name=name,
        compiler_params=_params(("arbitrary",), 3 * [((tm, d_dim), F32)], extra=6 * tm * d_dim * 4),
    )(dx_out, f, post_g, gate)


def _prenorm_bwd(dx_out, dhns, x, pre_g, scale, name):
    s_dim, d_dim = x.shape
    tm = TOKEN_TILE
    n_in = len(dhns)

    def body(*refs):
        dx_ref, x_ref, pg_ref, sc_ref = refs[n_in + 0], refs[n_in + 1], refs[n_in + 2], refs[n_in + 3]
        dxo_ref, dsh_ref, dsc_ref, dpg_ref = refs[n_in + 4:]

        @pl.when(pl.program_id(0) == 0)
        def _():
            dsh_ref[...] = jnp.zeros_like(dsh_ref)
            dsc_ref[...] = jnp.zeros_like(dsc_ref)
            dpg_ref[...] = jnp.zeros_like(dpg_ref)

        dhn = refs[0][...]
        for k in range(1, n_in):
            dhn = dhn + refs[k][...]
        xv = x_ref[...]
        r = _rstd(xv)
        xr = xv * r
        dsh_ref[...] += jnp.sum(dhn, axis=0, keepdims=True)
        dsc_ref[...] += jnp.sum(dhn * (xr * pg_ref[...]), axis=0, keepdims=True)
        dn = dhn * (1.0 + sc_ref[...])
        dpg_ref[...] += jnp.sum(dn * xr, axis=0, keepdims=True)
        dxo_ref[...] = dx_ref[...] + _rms_bwd(xv, r, dn * pg_ref[...])

    vec = pl.BlockSpec((1, d_dim), lambda i: (0, 0))
    row = pl.BlockSpec((tm, d_dim), lambda i: (i, 0))
    vshape = jax.ShapeDtypeStruct((1, d_dim), F32)
    return _pcall(
        body, out_shape=(jax.ShapeDtypeStruct((s_dim, d_dim), F32), vshape, vshape, vshape), grid=(s_dim // tm,),
        in_specs=n_in * [row] + [row, row, vec, vec], out_specs=(row, vec, vec, vec), name=name,
        compiler_params=_params(("arbitrary",), (n_in + 3) * [((tm, d_dim), F32)], extra=6 * tm * d_dim * 4),
    )(*dhns, dx_out, x, pre_g, scale)


def _ffn_dgu(df, wd, g, u, name, after=None):
    s_dim, d_dim = df.shape
    f_dim = wd.shape[0]
    tm, tn = TOKEN_TILE, f_dim // 2

    def body(df_ref, wd_ref, g_ref, u_ref, *rest):
        dg_ref, du_ref = rest[-2:]
        da = _dot_nt(df_ref[...], wd_ref[...])
        gv, uv = g_ref[...].astype(F32), u_ref[...].astype(F32)
        sg = jax.nn.sigmoid(gv)
        du_ref[...] = (da * (gv * sg)).astype(BF16)
        dg_ref[...] = (da * uv * (sg * (1.0 + gv * (1.0 - sg)))).astype(BF16)

    act = pl.BlockSpec((tm, tn), lambda i, j: (i, j))
    act_shape = jax.ShapeDtypeStruct((s_dim, f_dim), BF16)
    token = [] if after is None else [after]
    return _pcall(
        body, out_shape=(act_shape, act_shape), grid=(s_dim // tm, f_dim // tn),
        in_specs=[pl.BlockSpec((tm, d_dim), lambda i, j: (i, 0)), pl.BlockSpec((tn, d_dim), lambda i, j: (j, 0)), act, act]
        + len(token) * [_TOKEN_SPEC],
        out_specs=(act, act), name=name,
        compiler_params=_params(("parallel", "parallel"), [((tm, d_dim), BF16), ((tn, d_dim), BF16)] + 4 * [((tm, tn), BF16)],
                                extra=6 * tm * tn * 4),
    )(df, wd, g, u, *token)


def _ffn_dw(dg, du, a, hn, df, name):
    s_dim, f_dim = dg.shape
    d_dim = hn.shape[1]
    tm = 256

    def body(dg_ref, du_ref, a_ref, hn_ref, df_ref, dwg_ref, dwu_ref, dwd_ref):
        dwg_ref[...] = _dot_tn(dg_ref[...], hn_ref[...]).astype(BF16)
        dwu_ref[...] = _dot_tn(du_ref[...], hn_ref[...]).astype(BF16)
        dwd_ref[...] = _dot_tn(a_ref[...], df_ref[...]).astype(BF16)

    col = pl.BlockSpec((s_dim, tm), lambda i: (0, i))
    full = pl.BlockSpec((s_dim, d_dim), lambda i: (0, 0), pipeline_mode=pl.Buffered(1))
    out = pl.BlockSpec((tm, d_dim), lambda i: (i, 0))
    shape = jax.ShapeDtypeStruct((f_dim, d_dim), BF16)
    need = 2 * s_dim * d_dim * 2 + 2 * 3 * (s_dim * tm * 2 + tm * d_dim * 2) + 3 * tm * d_dim * 4 + 3 * s_dim * tm * 2
    return _pcall(
        body, out_shape=(shape, shape, shape), grid=(f_dim // tm,), in_specs=[col, col, col, full, full],
        out_specs=(out, out, out), name=name,
        compiler_params=pltpu.CompilerParams(dimension_semantics=("parallel",),
                                             vmem_limit_bytes=int(min(need + VMEM_RESERVE, V7X_VMEM_BYTES - VMEM_RESERVE))),
    )(dg, du, a, hn, df)


def _mm_tn_shared(lhs, b, name):
    k_dim, m_dim = lhs[0].shape
    n_dim = b.shape[1]
    tm = 256
    n = len(lhs)

    def body(*refs):
        rhs = refs[n][...]
        for j in range(n):
            refs[n + 1 + j][...] = _dot_tn(refs[j][...], rhs).astype(BF16)

    col = pl.BlockSpec((k_dim, tm), lambda i: (0, i))
    out = pl.BlockSpec((tm, n_dim), lambda i: (i, 0))
    shape = jax.ShapeDtypeStruct((m_dim, n_dim), BF16)
    need = k_dim * n_dim * 2 + 2 * n * (k_dim * tm * 2 + tm * n_dim * 2) + n * tm * n_dim * 4 + n * k_dim * tm * 2
    return _pcall(
        body, out_shape=tuple(n * [shape]), grid=(m_dim // tm,),
        in_specs=n * [col] + [pl.BlockSpec((k_dim, n_dim), lambda i: (0, 0), pipeline_mode=pl.Buffered(1))],
        out_specs=tuple(n * [out]), name=name,
        compiler_params=pltpu.CompilerParams(dimension_semantics=("parallel",),
                                             vmem_limit_bytes=int(min(need + VMEM_RESERVE, V7X_VMEM_BYTES - VMEM_RESERVE))),
    )(*lhs, b)


def _ffn_fwd_fused(x, pre_g, scale, shift, post_g, gate, wg_t, wu_t, wd, name):
    s_dim, d_dim = x.shape
    f_dim = wd.shape[0]
    tm, chunks = 256, FFN_CHUNKS
    cw = f_dim // chunks

    def body(x_ref, prg_ref, sc_ref, sh_ref, pg_ref, gt_ref, wg_ref, wu_ref, wd_ref, hn_ref, go_ref, uo_ref, a_ref, xo_ref, f_ref):
        xf = x_ref[...]
        hn = ((xf * _rstd(xf) * prg_ref[...]) * (1.0 + sc_ref[...]) + sh_ref[...]).astype(BF16)
        hn_ref[...] = hn
        f = None
        ahead = (_dot_nt(hn, wg_ref[0:cw, :]), _dot_nt(hn, wu_ref[0:cw, :]))
        for c in range(chunks):
            g, u = ahead
            if c + 1 < chunks:
                nxt = slice((c + 1) * cw, (c + 2) * cw)
                ahead = (_dot_nt(hn, wg_ref[nxt, :]), _dot_nt(hn, wu_ref[nxt, :]))
            cols = slice(c * cw, (c + 1) * cw)
            go_ref[:, cols] = g.astype(BF16)
            uo_ref[:, cols] = u.astype(BF16)
            a = (g * jax.nn.sigmoid(g) * u).astype(BF16)
            a_ref[:, cols] = a
            part = _dot_nn(a, wd_ref[cols, :])
            f = part if f is None else f + part
        f_ref[...] = f
        xo_ref[...] = xf + (FFN_RES * gt_ref[...]) * (f * _rstd(f) * pg_ref[...])

    vec = pl.BlockSpec((1, d_dim), lambda i: (0, 0))
    row = pl.BlockSpec((tm, d_dim), lambda i: (i, 0))
    act = pl.BlockSpec((tm, f_dim), lambda i: (i, 0))
    weight = pl.BlockSpec((f_dim, d_dim), lambda i: (0, 0), pipeline_mode=pl.Buffered(1))
    act_shape = jax.ShapeDtypeStruct((s_dim, f_dim), BF16)
    res_shape = jax.ShapeDtypeStruct((s_dim, d_dim), F32)
    need = (3 * f_dim * d_dim * 2 + 2 * tm * d_dim * 4 + 2 * (tm * d_dim * 2 + 3 * tm * f_dim * 2 + 2 * tm * d_dim * 4)
            + 8 * tm * cw * 4 + 4 * tm * d_dim * 4)
    return _pcall(
        body, out_shape=(jax.ShapeDtypeStruct((s_dim, d_dim), BF16), act_shape, act_shape, act_shape, res_shape, res_shape),
        grid=(s_dim // tm,), in_specs=[row, vec, vec, vec, vec, vec, weight, weight, weight],
        out_specs=(row, act, act, act, row, row), name=name,
        compiler_params=pltpu.CompilerParams(dimension_semantics=("parallel",),
                                             vmem_limit_bytes=int(min(need + VMEM_RESERVE, V7X_VMEM_BYTES - VMEM_RESERVE))),
    )(x, pre_g, scale, shift, post_g, gate, wg_t, wu_t, wd)


def _ffn_bwd_fused(dx_out, saved, pre_g, post_g, scale, gate, wg_t, wu_t, wd, name):
    x, _, g, u, _, f = saved
    s_dim, d_dim = x.shape
    f_dim = wd.shape[0]
    tm, chunks = 256, FFN_CHUNKS
    cw = f_dim // chunks

    def body(dx_ref, f_ref, g_ref, u_ref, x_ref, pg_ref, gt_ref, prg_ref, sc_ref, wd_ref, wg_ref, wu_ref,
             df_ref, dg_ref, du_ref, dxo_ref, dgate_ref, dpost_ref, dsh_ref, dsc_ref, dpg_ref):
        @pl.when(pl.program_id(0) == 0)
        def _():
            for acc in (dgate_ref, dpost_ref, dsh_ref, dsc_ref, dpg_ref):
                acc[...] = jnp.zeros_like(acc)

        dx, fv = dx_ref[...], f_ref[...]
        r = _rstd(fv)
        fr = fv * r
        dgate_ref[...] += FFN_RES * jnp.sum(dx * (fr * pg_ref[...]), axis=0, keepdims=True)
        dy = (FFN_RES * gt_ref[...]) * dx
        dpost_ref[...] += jnp.sum(dy * fr, axis=0, keepdims=True)
        df = _rms_bwd(fv, r, dy * pg_ref[...]).astype(BF16)
        df_ref[...] = df
        dhn = None
        ahead = _dot_nt(df, wd_ref[0:cw, :])
        for c in range(chunks):
            da = ahead
            if c + 1 < chunks:
                ahead = _dot_nt(df, wd_ref[(c + 1) * cw:(c + 2) * cw, :])
            cols = slice(c * cw, (c + 1) * cw)
            gv, uv = g_ref[:, cols].astype(F32), u_ref[:, cols].astype(F32)
            sg = jax.nn.sigmoid(gv)
            du = (da * (gv * sg)).astype(BF16)
            dg = (da * uv * (sg * (1.0 + gv * (1.0 - sg)))).astype(BF16)
            dg_ref[:, cols] = dg
            du_ref[:, cols] = du
            part = _dot_nn(dg, wg_ref[cols, :]) + _dot_nn(du, wu_ref[cols, :])
            dhn = part if dhn is None else dhn + part
        xv = x_ref[...]
        rx = _rstd(xv)
        xr = xv * rx
        dsh_ref[...] += jnp.sum(dhn, axis=0, keepdims=True)
        dsc_ref[...] += jnp.sum(dhn * (xr * prg_ref[...]), axis=0, keepdims=True)
        dn = dhn * (1.0 + sc_ref[...])
        dpg_ref[...] += jnp.sum(dn * xr, axis=0, keepdims=True)
        dxo_ref[...] = dx + _rms_bwd(xv, rx, dn * prg_ref[...])

    vec = pl.BlockSpec((1, d_dim), lambda i: (0, 0))
    row = pl.BlockSpec((tm, d_dim), lambda i: (i, 0))
    act = pl.BlockSpec((tm, f_dim), lambda i: (i, 0))
    weight = pl.BlockSpec((f_dim, d_dim), lambda i: (0, 0), pipeline_mode=pl.Buffered(1))
    vshape = jax.ShapeDtypeStruct((1, d_dim), F32)
    act_shape = jax.ShapeDtypeStruct((s_dim, f_dim), BF16)
    need = (3 * f_dim * d_dim * 2 + 2 * (3 * tm * d_dim * 4 + 2 * tm * f_dim * 2) + 2 * (tm * d_dim * 2 + 2 * tm * f_dim * 2 + tm * d_dim * 4)
            + 6 * tm * cw * 4 + 6 * tm * d_dim * 4)
    return _pcall(
        body, out_shape=(jax.ShapeDtypeStruct((s_dim, d_dim), BF16), act_shape, act_shape, jax.ShapeDtypeStruct((s_dim, d_dim), F32),
                         vshape, vshape, vshape, vshape, vshape),
        grid=(s_dim // tm,), in_specs=[row, row, act, act, row, vec, vec, vec, vec, weight, weight, weight],
        out_specs=(row, act, act, row, vec, vec, vec, vec, vec), name=name,
        compiler_params=pltpu.CompilerParams(dimension_semantics=("arbitrary",),
                                             vmem_limit_bytes=int(min(need + VMEM_RESERVE, V7X_VMEM_BYTES - VMEM_RESERVE))),
    )(dx_out, f, g, u, x, post_g, gate, pre_g, scale, wd, wg_t, wu_t)


def _rope_tables(zero=0.0):
    half = QK_ROPE // 2
    freqs = ROPE_THETA ** (-jnp.arange(half, dtype=F32) / half)
    ang = (jnp.arange(SEQ, dtype=F32)[:, None] + zero) * freqs[None, :]
    cos, sin = jnp.cos(ang), jnp.sin(ang)
    ones = jnp.ones((SEQ, QK_NOPE), F32)
    zeros = jnp.zeros((SEQ, QK_NOPE), F32)
    pad1 = jnp.ones((SEQ, HEAD_PAD - QK_NOPE - QK_ROPE), F32)
    pad0 = jnp.zeros((SEQ, HEAD_PAD - QK_NOPE - QK_ROPE), F32)
    zh = jnp.zeros((SEQ, half), F32)
    c = jnp.concatenate([ones, cos, cos, pad1], axis=1)
    s1 = jnp.concatenate([zeros, -sin, zh, pad0], axis=1)
    s2 = jnp.concatenate([zeros, zh, sin, pad0], axis=1)
    return c, s1, s2


def _rope(v, c, s1, s2):
    half = QK_ROPE // 2
    return v * c + pltpu.roll(v, HEAD_PAD - half, 1) * s1 + pltpu.roll(v, half, 1) * s2


def _rope_t(dv, c, s1, s2):
    half = QK_ROPE // 2
    return dv * c + pltpu.roll(dv * s1, half, 1) + pltpu.roll(dv * s2, HEAD_PAD - half, 1)


def _mla_qkv(lat, q_norm, kv_norm, wq_t, wkv_t, rope, name):
    s_dim = lat.shape[0]
    width = MLA_HEADS * HEAD_PAD
    tm = 256

    def body(lat_ref, qg_ref, kg_ref, wq_ref, wkv_ref, c_ref, s1_ref, s2_ref, q_ref, k_ref, v_ref, qn_ref, kvn_ref):
        cq = lat_ref[:, :Q_LORA]
        ckv = lat_ref[:, Q_LORA:Q_LORA + KV_LORA]
        kr = lat_ref[:, Q_LORA + KV_LORA:]
        c, s1, s2 = c_ref[...], s1_ref[...], s2_ref[...]
        qn = (cq * _rstd(cq) * qg_ref[...]).astype(BF16)
        kvn = (ckv * _rstd(ckv) * kg_ref[...]).astype(BF16)
        qn_ref[...] = qn
        kvn_ref[...] = kvn
        q = _dot_nt(qn, wq_ref[...])
        kv = _dot_nt(kvn, wkv_ref[...])
        krr = _rope(kr, c, s1, s2)
        low = lax.broadcasted_iota(jnp.int32, (tm, HEAD_PAD), 1) < QK_NOPE
        for h in range(MLA_HEADS):
            sl = slice(h * HEAD_PAD, (h + 1) * HEAD_PAD)
            q_ref[:, sl] = _rope(q[:, sl], c, s1, s2).astype(BF16)
            kvh = kv[:, sl]
            k_ref[:, sl] = (jnp.where(low, kvh, 0.0) + krr).astype(BF16)
            v_ref[:, sl] = jnp.where(low, 0.0, kvh).astype(BF16)

    row = lambda n: pl.BlockSpec((tm, n), lambda i: (i, 0))
    full = lambda a: pl.BlockSpec(a.shape, lambda i: (0, 0))
    wide = jax.ShapeDtypeStruct((s_dim, width), BF16)
    return _pcall(
        body,
        out_shape=(wide, wide, wide, jax.ShapeDtypeStruct((s_dim, Q_LORA), BF16), jax.ShapeDtypeStruct((s_dim, KV_LORA), BF16)),
        grid=(s_dim // tm,),
        in_specs=[row(LAT_PAD), full(q_norm), full(kv_norm), full(wq_t), full(wkv_t), row(HEAD_PAD), row(HEAD_PAD), row(HEAD_PAD)],
        out_specs=(row(width), row(width), row(width), row(Q_LORA), row(KV_LORA)), name=name,
        compiler_params=_params(("parallel",), [((tm, LAT_PAD), F32), (wq_t.shape, BF16), (wkv_t.shape, BF16)]
                                + 3 * [((tm, width), BF16)], extra=4 * tm * width * 4),
    )(lat, q_norm, kv_norm, wq_t, wkv_t, *rope)


def _mla_scores(q, k_ref, t, tq):
    lo = t * tq
    own = slice(lo, lo + tq)
    scores = [(_dot_nt(q, k_ref[own, :]), own)]
    if t > 0:
        scores.append((_dot_nt(q, k_ref[0:lo, :]), slice(0, lo)))
    return scores


def _mla_softmax(scores):
    s_own = scores[0][0] * MLA_SCALE
    rows = lax.broadcasted_iota(jnp.int32, s_own.shape, 0)
    cols = lax.broadcasted_iota(jnp.int32, s_own.shape, 1)
    s_own = jnp.where(cols <= rows, s_own, -jnp.inf)
    mx = jnp.max(s_own, axis=-1, keepdims=True)
    if len(scores) == 1:
        e_own = jnp.exp(s_own - mx)
        return [(e_own * (1.0 / jnp.sum(e_own, axis=-1, keepdims=True)), scores[0][1])]
    s_pre = scores[1][0] * MLA_SCALE
    mx = jnp.maximum(mx, jnp.max(s_pre, axis=-1, keepdims=True))
    e_own, e_pre = jnp.exp(s_own - mx), jnp.exp(s_pre - mx)
    inv = 1.0 / (jnp.sum(e_own, axis=-1, keepdims=True) + jnp.sum(e_pre, axis=-1, keepdims=True))
    return [(e_pre * inv, scores[1][1]), (e_own * inv, scores[0][1])]


def _mla_attn_fwd(q, k, v, name):
    s_dim = q.shape[0]
    tq = MLA_QUERY_TILE

    def body(q_ref, k_ref, v_ref, o_ref):
        n_tiles = s_dim // tq
        tile_of = lambda t: slice(t * tq, (t + 1) * tq)
        def weighted_values(t, probs):
            o = None
            for p, keys in probs:
                part = _dot_nn(p, v_ref[keys, :])
                o = part if o is None else o + part
            o_ref[tile_of(t), :] = o.astype(BF16)

        scores = _mla_scores(q_ref[tile_of(0), :], k_ref, 0, tq)
        probs = None
        for t in range(n_tiles):
            ahead = _mla_scores(q_ref[tile_of(t + 1), :], k_ref, t + 1, tq) if t + 1 < n_tiles else None
            if probs is not None:
                weighted_values(t - 1, probs)
            probs = [(p.astype(BF16), keys) for p, keys in _mla_softmax(scores)]
            scores = ahead
        weighted_values(n_tiles - 1, probs)

    head = pl.BlockSpec((s_dim, HEAD_PAD), lambda h: (0, h))
    return _pcall(
        body, out_shape=jax.ShapeDtypeStruct(q.shape, BF16), grid=(MLA_HEADS,),
        in_specs=[head, head, head], out_specs=head, name=name,
        compiler_params=_params(("parallel",), 4 * [((s_dim, HEAD_PAD), BF16)], extra=4 * tq * s_dim * 4),
    )(q, k, v)


def _mla_attn_bwd(q, k, v, d_o, name):
    s_dim = q.shape[0]
    tq = MLA_QUERY_TILE

    def body(q_ref, k_ref, v_ref, do_ref, dq_ref, dk_ref, dv_ref):
        dk_ref[...] = jnp.zeros_like(dk_ref)
        dv_ref[...] = jnp.zeros_like(dv_ref)
        n_tiles = s_dim // tq
        tile_of = lambda t: slice(t * tq, (t + 1) * tq)

        def products(t):
            scores = _mla_scores(q_ref[tile_of(t), :], k_ref, t, tq)
            dot = do_ref[tile_of(t), :].astype(BF16)
            return scores, [_dot_nt(dot, v_ref[keys, :]) for _, keys in scores]

        def gradients_of_scores(scores, dps):
            probs = _mla_softmax(scores)
            dp_of = {(keys.start, keys.stop): dp for (_, keys), dp in zip(scores, dps)}
            terms = [(p, keys, dp_of[keys.start, keys.stop]) for p, keys in probs]
            row = None
            for p, _, dp in terms:
                part = jnp.sum(p * dp, axis=-1, keepdims=True)
                row = part if row is None else row + part
            return [((p * (dp - row) * MLA_SCALE).astype(BF16), p.astype(BF16), keys) for p, keys, dp in terms]

        def accumulate(t, terms):
            qt = q_ref[tile_of(t), :]
            dot = do_ref[tile_of(t), :].astype(BF16)
            dq = None
            for dsb, pb, keys in terms:
                part = _dot_nn(dsb, k_ref[keys, :])
                dq = part if dq is None else dq + part
                dk_ref[keys, :] += _dot_tn(dsb, qt)
                dv_ref[keys, :] += _dot_tn(pb, dot)
            dq_ref[tile_of(t), :] = dq

        ready = products(0)
        terms = None
        for t in range(n_tiles):
            ahead = products(t + 1) if t + 1 < n_tiles else None
            if terms is not None:
                accumulate(t - 1, terms)
            terms = gradients_of_scores(*ready)
            ready = ahead
        accumulate(n_tiles - 1, terms)

    head = pl.BlockSpec((s_dim, HEAD_PAD), lambda h: (0, h))
    out = jax.ShapeDtypeStruct(q.shape, F32)
    return _pcall(
        body, out_shape=(out, out, out), grid=(MLA_HEADS,),
        in_specs=[head, head, head, head], out_specs=(head, head, head), name=name,
        compiler_params=_params(("parallel",), 3 * [((s_dim, HEAD_PAD), BF16)] + 4 * [((s_dim, HEAD_PAD), F32)],
                                extra=6 * tq * s_dim * 4),
    )(q, k, v, d_o)


def _mla_qkv_bwd(dq, dk, dv, lat, q_norm, kv_norm, wq_t, wkv_t, rope, name):
    s_dim = lat.shape[0]
    width = MLA_HEADS * HEAD_PAD
    tm = 256

    def body(dq_ref, dk_ref, dv_ref, lat_ref, qg_ref, kg_ref, wq_ref, wkv_ref, c_ref, s1_ref, s2_ref,
             dqp_ref, dkv_ref, dlat_ref, dqg_ref, dkg_ref):
        @pl.when(pl.program_id(0) == 0)
        def _():
            dqg_ref[...] = jnp.zeros_like(dqg_ref)
            dkg_ref[...] = jnp.zeros_like(dkg_ref)

        c, s1, s2 = c_ref[...], s1_ref[...], s2_ref[...]
        lane = lax.broadcasted_iota(jnp.int32, (tm, HEAD_PAD), 1)
        low = lane < QK_NOPE
        rot = (lane >= QK_NOPE) & (lane < QK_NOPE + QK_ROPE)
        dkrr = jnp.zeros((tm, HEAD_PAD), F32)
        for h in range(MLA_HEADS):
            sl = slice(h * HEAD_PAD, (h + 1) * HEAD_PAD)
            dqp_ref[:, sl] = _rope_t(dq_ref[:, sl], c, s1, s2).astype(BF16)
            dkh = dk_ref[:, sl]
            dkv_ref[:, sl] = jnp.where(low, dkh, dv_ref[:, sl]).astype(BF16)
            dkrr = dkrr + jnp.where(rot, dkh, 0.0)
        dqn = _dot_nn(dqp_ref[...], wq_ref[...])
        dkvn = _dot_nn(dkv_ref[...], wkv_ref[...])
        cq = lat_ref[:, :Q_LORA]
        ckv = lat_ref[:, Q_LORA:Q_LORA + KV_LORA]
        rq, rkv = _rstd(cq), _rstd(ckv)
        dqg_ref[...] += jnp.sum(dqn * cq * rq, axis=0, keepdims=True)
        dkg_ref[...] += jnp.sum(dkvn * ckv * rkv, axis=0, keepdims=True)
        dlat_ref[:, :Q_LORA] = _rms_bwd(cq, rq, dqn * qg_ref[...])
        dlat_ref[:, Q_LORA:Q_LORA + KV_LORA] = _rms_bwd(ckv, rkv, dkvn * kg_ref[...])
        dlat_ref[:, Q_LORA + KV_LORA:] = _rope_t(dkrr, c, s1, s2)

    row = lambda n: pl.BlockSpec((tm, n), lambda i: (i, 0))
    full = lambda a: pl.BlockSpec(a.shape, lambda i: (0, 0))
    wide = jax.ShapeDtypeStruct((s_dim, width), BF16)
    return _pcall(
        body,
        out_shape=(wide, wide, jax.ShapeDtypeStruct((s_dim, LAT_PAD), F32),
                   jax.ShapeDtypeStruct(q_norm.shape, F32), jax.ShapeDtypeStruct(kv_norm.shape, F32)),
        grid=(s_dim // tm,),
        in_specs=[row(width), row(width), row(width), row(LAT_PAD), full(q_norm), full(kv_norm), full(wq_t), full(wkv_t),
                  row(HEAD_PAD), row(HEAD_PAD), row(HEAD_PAD)],
        out_specs=(row(width), row(width), row(LAT_PAD), full(q_norm), full(kv_norm)), name=name,
        compiler_params=_params(("arbitrary",), 3 * [((tm, width), F32)] + [((tm, LAT_PAD), F32), (wq_t.shape, BF16),
                                                                           (wkv_t.shape, BF16)] + 2 * [((tm, width), BF16)],
                                extra=2 * tm * width * 4),
    )(dq, dk, dv, lat, q_norm, kv_norm, wq_t, wkv_t, *rope)


def _t5_bucket(dist):
    max_exact = N_BUCKETS // 2
    d = jnp.maximum(dist, 1).astype(F32)
    large = max_exact + (jnp.log(d / max_exact) / math.log(MAX_DISTANCE / max_exact)
                         * (N_BUCKETS - max_exact)).astype(jnp.int32)
    large = jnp.minimum(large, N_BUCKETS - 1)
    return jnp.where(dist < max_exact, dist, large)


def _dil_buckets(dilation):
    iq = jnp.arange(DIL_BLOCK)[:, None]
    ik = jnp.arange(2 * DIL_BLOCK)[None, :]
    return _t5_bucket(jnp.maximum(DIL_BLOCK + iq - ik, 0) * dilation)


def _dil_logits(qh, kb, bias_h, first, span):
    if first:
        s = _dot_nt(qh, kb) * DIL_SCALE + bias_h[:, DIL_BLOCK:]
        rel = lax.broadcasted_iota(jnp.int32, s.shape, 0) - lax.broadcasted_iota(jnp.int32, s.shape, 1)
    else:
        s = _dot_nt(qh, kb) * DIL_SCALE + bias_h
        rel = DIL_BLOCK + lax.broadcasted_iota(jnp.int32, s.shape, 0) - lax.broadcasted_iota(jnp.int32, s.shape, 1)
    return jnp.where((rel >= 0) & (rel <= span), s, -jnp.inf)


def _dil_blocks(s_dim, dilation):
    rows = s_dim // dilation
    for r in range(dilation):
        for n in range(rows // DIL_BLOCK):
            lo = r * rows + n * DIL_BLOCK
            keys = slice(lo, lo + DIL_BLOCK) if n == 0 else slice(lo - DIL_BLOCK, lo + DIL_BLOCK)
            start = r + n * DIL_BLOCK * dilation
            tokens = slice(start, start + DIL_BLOCK) if dilation == 1 else pl.ds(start, DIL_BLOCK, stride=dilation)
            yield n == 0, slice(lo, lo + DIL_BLOCK), keys, tokens


def _dil_views(s_dim):
    col = lambda which: pl.BlockSpec((s_dim, HEAD_PAD), lambda p: (0, which * DIL_PAIRS + p))
    nat = pl.BlockSpec((s_dim, HEAD_PAD), lambda p: (0, p))
    bias = pl.BlockSpec((2, DIL_BLOCK, 2 * DIL_BLOCK), lambda p: (p, 0, 0))
    return col, nat, bias


def _dil_attn_fwd(qkv, bias, dilation, span, name):
    s_dim = qkv.shape[0]
    d_dim = DIL_HEADS * DIL_HEAD_DIM
    col, nat, bias_spec = _dil_views(s_dim)

    def body(q_ref, k_ref, v_ref, b_ref, o_ref, l_ref):
        lane = lax.broadcasted_iota(jnp.int32, (DIL_BLOCK, HEAD_PAD), 1)
        klane = lax.broadcasted_iota(jnp.int32, (2 * DIL_BLOCK, HEAD_PAD), 1)
        blocks = list(_dil_blocks(s_dim, dilation))
        for g0 in range(0, len(blocks), DIL_GROUPED):
            group = blocks[g0:g0 + DIL_GROUPED]
            logits = [_dil_logits(jnp.where((lane < DIL_HEAD_DIM) == (h == 0), q_ref[blk, :], 0), k_ref[keys, :], b_ref[h],
                                  first, span) for first, blk, keys, _ in group for h in range(2)]
            soft = []
            for lg in logits:
                mx = jnp.max(lg, axis=-1, keepdims=True)
                e = jnp.exp(lg - mx)
                tot = jnp.sum(e, axis=-1, keepdims=True)
                soft.append(((e * (1.0 / tot)).astype(BF16), mx + jnp.log(tot)))
            for i, (_, _, keys, tokens) in enumerate(group):
                vb = v_ref[keys, :]
                o_acc = jnp.zeros((DIL_BLOCK, HEAD_PAD), F32)
                lse_acc = jnp.zeros((DIL_BLOCK, HEAD_PAD), F32)
                for h in range(2):
                    p, lse = soft[2 * i + h]
                    kmine = (klane[:vb.shape[0]] < DIL_HEAD_DIM) == (h == 0)
                    o_acc = o_acc + _dot_nn(p, jnp.where(kmine, vb, 0))
                    lse_acc = jnp.where((lane < DIL_HEAD_DIM) == (h == 0), lse, lse_acc)
                o_ref[tokens, :] = o_acc
                l_ref[tokens, :] = lse_acc

    out = jax.ShapeDtypeStruct((s_dim, d_dim), F32)
    return _pcall(
        body, out_shape=(out, out), grid=(DIL_PAIRS,),
        in_specs=[col(0), col(1), col(2), bias_spec], out_specs=(nat, nat), name=name,
        compiler_params=_params(("parallel",), 3 * [((s_dim, HEAD_PAD), BF16)] + 2 * [((s_dim, HEAD_PAD), F32)]
                                + [((2, DIL_BLOCK, 2 * DIL_BLOCK), F32)], extra=2**21),
    )(qkv, qkv, qkv, bias)


def _dil_mix(lses, outs, name):
    s_dim, d_dim = outs[0].shape
    tm = TOKEN_TILE
    ng = len(outs)

    def body(*refs):
        ls = [refs[g][...] for g in range(ng)]
        mx = ls[0]
        for g in range(1, ng):
            mx = jnp.maximum(mx, ls[g])
        es = [jnp.exp(l - mx) for l in ls]
        tot = es[0]
        for g in range(1, ng):
            tot = tot + es[g]
        o = None
        for g in range(ng):
            al = es[g] / tot
            refs[2 * ng + g][...] = al
            t = al * refs[ng + g][...]
            o = t if o is None else o + t
        refs[3 * ng][...] = o
        refs[3 * ng + 1][...] = o.astype(BF16)

    row = pl.BlockSpec((tm, d_dim), lambda i: (i, 0))
    f = jax.ShapeDtypeStruct((s_dim, d_dim), F32)
    res = _pcall(
        body, out_shape=tuple(ng * [f] + [f, jax.ShapeDtypeStruct((s_dim, d_dim), BF16)]), grid=(s_dim // tm,),
        in_specs=2 * ng * [row], out_specs=tuple((ng + 2) * [row]), name=name,
        compiler_params=_params(("parallel",), (3 * ng + 2) * [((tm, d_dim), F32)], extra=4 * tm * d_dim * 4),
    )(*lses, *outs)
    return res[:ng], res[ng], res[ng + 1]


def _dil_attn_bwd(qkv, bias, d_o, o_mix, alpha, lse, dilation, span, name):
    s_dim = qkv.shape[0]
    d_dim = DIL_HEADS * DIL_HEAD_DIM
    col, nat, bias_spec = _dil_views(s_dim)

    def body(q_ref, k_ref, v_ref, b_ref, do_ref, om_ref, al_ref, l_ref, dq_ref, dk_ref, dv_ref, db_ref, dk_acc, dv_acc):
        db_ref[...] = jnp.zeros_like(db_ref)
        dk_acc[...] = jnp.zeros_like(dk_acc)
        dv_acc[...] = jnp.zeros_like(dv_acc)
        lane = lax.broadcasted_iota(jnp.int32, (DIL_BLOCK, HEAD_PAD), 1)
        klane = lax.broadcasted_iota(jnp.int32, (2 * DIL_BLOCK, HEAD_PAD), 1)
        blocks = list(_dil_blocks(s_dim, dilation))
        heads = [(lane < DIL_HEAD_DIM) == (h == 0) for h in range(2)]
        for g0 in range(0, len(blocks), DIL_GROUPED):
            group = blocks[g0:g0 + DIL_GROUPED]
            staged = []
            for first, blk, kv_rows, tokens in group:
                qb, kb, vb = q_ref[blk, :], k_ref[kv_rows, :], v_ref[kv_rows, :]
                dog = al_ref[tokens, :] * do_ref[tokens, :]
                row_term = dog * om_ref[tokens, :]
                lse_b = l_ref[tokens, :]
                for h in range(2):
                    qh = jnp.where(heads[h], qb, 0)
                    dogh = jnp.where(heads[h], dog, 0.0).astype(BF16)
                    staged.append((_dil_logits(qh, kb, b_ref[h], first, span), _dot_nt(dogh, vb), qh, dogh,
                                   jnp.max(jnp.where(heads[h], lse_b, -jnp.inf), axis=-1, keepdims=True),
                                   jnp.sum(jnp.where(heads[h], row_term, 0.0), axis=-1, keepdims=True)))
            grads = []
            for i, (logits, dp, qh, dogh, lse_h, row) in enumerate(staged):
                p = jnp.exp(logits - lse_h)
                ds = p * (dp - row)
                if group[i // 2][0]:
                    db_ref[i % 2, :, DIL_BLOCK:] += ds
                else:
                    db_ref[i % 2] += ds
                grads.append(((ds * DIL_SCALE).astype(BF16), p.astype(BF16), qh, dogh))
            for i, (_, blk, kv_rows, _) in enumerate(group):
                kb = k_ref[kv_rows, :]
                dq_acc = jnp.zeros((DIL_BLOCK, HEAD_PAD), F32)
                dk_blk = jnp.zeros((kb.shape[0], HEAD_PAD), F32)
                dv_blk = jnp.zeros((kb.shape[0], HEAD_PAD), F32)
                for h in range(2):
                    dsb, pb, qh, dogh = grads[2 * i + h]
                    kmine = (klane[:kb.shape[0]] < DIL_HEAD_DIM) == (h == 0)
                    dq_acc = dq_acc + _dot_nn(dsb, jnp.where(kmine, kb, 0))
                    dk_blk = dk_blk + _dot_tn(dsb, qh)
                    dv_blk = dv_blk + _dot_tn(pb, dogh)
                dq_ref[blk, :] = dq_acc.astype(BF16)
                dk_acc[kv_rows, :] += dk_blk
                dv_acc[kv_rows, :] += dv_blk
        dk_ref[...] = dk_acc[...].astype(BF16)
        dv_ref[...] = dv_acc[...].astype(BF16)

    grad = jax.ShapeDtypeStruct((s_dim, d_dim), BF16)
    return _pcall(
        body, out_shape=(grad, grad, grad, jax.ShapeDtypeStruct(bias.shape, F32)), grid=(DIL_PAIRS,),
        in_specs=[col(0), col(1), col(2), bias_spec, nat, nat, nat, nat],
        out_specs=(nat, nat, nat, bias_spec), name=name,
        scratch_shapes=[pltpu.VMEM((s_dim, HEAD_PAD), F32), pltpu.VMEM((s_dim, HEAD_PAD), F32)],
        compiler_params=_params(("parallel",), 6 * [((s_dim, HEAD_PAD), BF16)] + 4 * [((s_dim, HEAD_PAD), F32)]
                                + 2 * [((2, DIL_BLOCK, 2 * DIL_BLOCK), F32)], extra=2 * s_dim * HEAD_PAD * 4 + 2**21),
    )(qkv, qkv, qkv, bias, d_o, o_mix, alpha, lse)


def _bias_reduce(dbias, buckets, name):
    n_heads = dbias.shape[0]

    def body(db_ref, bk_ref, o_ref):
        ds, bk = db_ref[0], bk_ref[0]
        lane = lax.broadcasted_iota(jnp.int32, (8, HEAD_PAD), 1)
        acc = jnp.zeros((8, HEAD_PAD), F32)
        for b in range(N_BUCKETS):
            acc = jnp.where(lane == b, jnp.sum(jnp.where(bk == b, ds, 0.0)), acc)
        o_ref[0] = acc

    blk = (1, DIL_BLOCK, 2 * DIL_BLOCK)
    return _pcall(
        body, out_shape=jax.ShapeDtypeStruct((n_heads, 8, HEAD_PAD), F32), grid=(n_heads,),
        in_specs=[pl.BlockSpec(blk, lambda h: (h, 0, 0)), pl.BlockSpec(blk, lambda h: (h // DIL_HEADS, 0, 0))],
        out_specs=pl.BlockSpec((1, 8, HEAD_PAD), lambda h: (h, 0, 0)), name=name,
        compiler_params=_params(("parallel",), [(blk, F32), (blk, jnp.int32)], extra=2**20),
    )(dbias, buckets)


def _loss_grad(y, target, name):
    s_dim, d_dim = y.shape
    tm = TOKEN_TILE

    def body(y_ref, t_ref, dy_ref, l_ref):
        @pl.when(pl.program_id(0) == 0)
        def _():
            l_ref[...] = jnp.zeros_like(l_ref)

        err = y_ref[...] - t_ref[...]
        dy_ref[...] = err / d_dim
        sq = (err * err).reshape(tm // 8, 8, d_dim)
        l_ref[...] += 0.5 * jnp.sum(sq, axis=0) / d_dim

    row = pl.BlockSpec((tm, d_dim), lambda i: (i, 0))
    acc = pl.BlockSpec((8, d_dim), lambda i: (0, 0))
    return _pcall(
        body, out_shape=(jax.ShapeDtypeStruct((s_dim, d_dim), F32), jax.ShapeDtypeStruct((8, d_dim), F32)),
        grid=(s_dim // tm,), in_specs=[row, row], out_specs=(row, acc), name=name,
        compiler_params=_params(("arbitrary",), 3 * [((tm, d_dim), F32)], extra=2 * tm * d_dim * 4),
    )(y, target)


def _mod_fwd(c_all, w_mod, b_loc, name):
    depth, d_dim, n = w_mod.shape
    nb = c_all.shape[0]

    def body(c_ref, w_ref, b_ref, o_ref, s_ref):
        cv = c_ref[...]
        sc = cv * jax.nn.sigmoid(cv)
        s_ref[...] = sc
        o_ref[0] = _dot_nn(sc.astype(BF16), w_ref[0].astype(BF16)) + b_ref[0]

    return _pcall(
        body, out_shape=(jax.ShapeDtypeStruct((depth, nb, n), F32), jax.ShapeDtypeStruct((nb, d_dim), F32)), grid=(depth,),
        in_specs=[pl.BlockSpec((nb, d_dim), lambda i: (0, 0)), pl.BlockSpec((1, d_dim, n), lambda i: (i, 0, 0)),
                  pl.BlockSpec((1, 1, n), lambda i: (i, 0, 0))],
        out_specs=(pl.BlockSpec((1, nb, n), lambda i: (i, 0, 0)), pl.BlockSpec((nb, d_dim), lambda i: (0, 0))), name=name,
        compiler_params=_params(("arbitrary",), [((1, d_dim, n), F32)], extra=d_dim * n * 2 + 2**20),
    )(c_all, w_mod, b_loc.reshape(depth, 1, n))


def _sum_parts(parts, name, transpose=False):
    _, rows, cols = parts.shape
    unit = 128 if transpose else 16
    budget = (7 if transpose else 3) * 2**20
    fits = [t for t in range(unit, rows // 2 + 1, unit) if rows % t == 0 and NDEV * t * cols * parts.dtype.itemsize <= budget]
    tr = max(fits) if fits else rows

    def body(p_ref, o_ref):
        acc = p_ref[0].astype(F32)
        for k in range(1, NDEV):
            acc = acc + p_ref[k].astype(F32)
        o_ref[...] = acc.T if transpose else acc

    out_shape, out_block = ((cols, rows), (cols, tr)) if transpose else ((rows, cols), (tr, cols))
    return _pcall(
        body, out_shape=jax.ShapeDtypeStruct(out_shape, F32), grid=(rows // tr,),
        in_specs=[pl.BlockSpec((NDEV, tr, cols), lambda i: (0, i, 0))],
        out_specs=pl.BlockSpec(out_block, (lambda i: (0, i)) if transpose else (lambda i: (i, 0))),
        name=name, compiler_params=_params(("parallel",), [((NDEV, tr, cols), parts.dtype), (out_block, F32)], extra=2**22),
    )(parts)


def _adamw(w, g, m, v, name):
    shape = w.shape
    cols = shape[-1]
    rows = math.prod(shape[:-1])
    tr = rows
    for cand in (2048, 1024, 512, 256, 128, 64, 32, 16, 8):
        if rows % cand == 0 and rows > cand and cand * cols * 4 <= 2**21:
            tr = cand
            break

    def body(w_ref, g_ref, m_ref, v_ref, d_ref, mo_ref, vo_ref):
        gv = g_ref[...]
        mn = ADAM_B1 * m_ref[...] + (1.0 - ADAM_B1) * gv
        vn = ADAM_B2 * v_ref[...] + (1.0 - ADAM_B2) * (gv * gv)
        m_hat = mn / (1.0 - ADAM_B1 ** ADAM_STEP)
        v_hat = vn / (1.0 - ADAM_B2 ** ADAM_STEP)
        d_ref[...] = -ADAM_LR * (m_hat / (jnp.sqrt(v_hat) + ADAM_EPS) + ADAM_WD * w_ref[...])
        mo_ref[...] = mn
        vo_ref[...] = vn

    blk = pl.BlockSpec((tr, cols), lambda i: (i, 0))
    out = jax.ShapeDtypeStruct((rows, cols), F32)
    res = _pcall(
        body, out_shape=(out, out, out), grid=(rows // tr,), in_specs=4 * [blk], out_specs=(blk, blk, blk), name=name,
        compiler_params=_params(("parallel",), 7 * [((tr, cols), F32)], extra=4 * tr * cols * 4),
    )(*(a.reshape(rows, cols) for a in (w, g, m, v)))
    return tuple(r.reshape(shape) for r in res)


def _peers():
    x, y, c = lax.axis_index("x"), lax.axis_index("y"), lax.axis_index("c")
    flip = lambda v, f: 1 - v if f else v
    peers = []
    for f in range(1, NDEV):
        px, py, pc = flip(x, f & 4), flip(y, f & 2), flip(c, f & 1)
        peers.append(((px, py, pc), 4 * px + 2 * py + pc))
    return (x, y, c), 4 * x + 2 * y + c, peers


def _places():
    x, y, c = lax.axis_index("x"), lax.axis_index("y"), lax.axis_index("c")
    place = lambda px, py, pc: ((px, py, pc), 4 * px + 2 * py + pc)
    return place(x, y, c), place(x, y, 1 - c), [place(1 - x, y, c), place(x, 1 - y, c), place(1 - x, 1 - y, c)]


def _exchange(arrs, gather, name):
    n = len(arrs)
    hbm = pl.BlockSpec(memory_space=pltpu.HBM)
    if gather:
        out_shape = [jax.ShapeDtypeStruct((NDEV * a.shape[0], a.shape[1]), a.dtype) for a in arrs]
    else:
        out_shape = [jax.ShapeDtypeStruct((NDEV, a.shape[0] // NDEV, a.shape[1]), a.dtype) for a in arrs]

    def body(*refs):
        ins, outs = refs[:n], refs[n:2 * n]
        send_sems, recv_sems, local_sems = refs[2 * n:]
        me_pos, me, peers = _peers()
        local = []
        for k in range(n):
            rows = arrs[k].shape[0] if gather else arrs[k].shape[0] // NDEV
            if gather:
                src_of = lambda idx: ins[k]
                dst_of = lambda idx: outs[k].at[pl.ds(me * rows, rows)]
                mine = (ins[k], outs[k].at[pl.ds(me * rows, rows)])
            else:
                src_of = lambda idx: ins[k].at[pl.ds(idx * rows, rows)]
                dst_of = lambda idx: outs[k].at[me]
                mine = (ins[k].at[pl.ds(me * rows, rows)], outs[k].at[me])
            cp = pltpu.make_async_copy(mine[0], mine[1], local_sems.at[k])
            cp.start()
            local.append(cp)
            for pos, idx in peers:
                pltpu.make_async_remote_copy(src_ref=src_of(idx), dst_ref=dst_of(idx), send_sem=send_sems.at[k],
                                             recv_sem=recv_sems.at[k], device_id=pos, device_id_type=MESH).start()
        for k in range(n):
            rows = arrs[k].shape[0] if gather else arrs[k].shape[0] // NDEV
            sent = ins[k].at[pl.ds(0, (NDEV - 1) * rows)] if not gather else outs[k].at[pl.ds(0, (NDEV - 1) * rows)]
            got = outs[k].at[pl.ds(0, (NDEV - 1) * rows)] if gather else outs[k].at[pl.ds(0, NDEV - 1)]
            pltpu.make_async_remote_copy(src_ref=sent, dst_ref=sent, send_sem=send_sems.at[k], recv_sem=recv_sems.at[k],
                                         device_id=me_pos, device_id_type=MESH).wait_send()
            pltpu.make_async_remote_copy(src_ref=got, dst_ref=got, send_sem=send_sems.at[k], recv_sem=recv_sems.at[k],
                                         device_id=me_pos, device_id_type=MESH).wait_recv()
            local[k].wait()

    return pl.pallas_call(
        body, out_shape=out_shape, in_specs=n * [hbm], out_specs=n * [hbm], name=name,
        scratch_shapes=[pltpu.SemaphoreType.DMA((n,)), pltpu.SemaphoreType.DMA((n,)), pltpu.SemaphoreType.DMA((n,))],
        compiler_params=pltpu.CompilerParams(has_side_effects=True),
    )(*arrs)


_HBM = pl.BlockSpec(memory_space=pltpu.HBM)
_SEM = pl.BlockSpec(memory_space=pltpu.SEMAPHORE)
_DATAFLOW = pltpu.SideEffectType.DATAFLOW_SIDE_EFFECTING


def _split_start(srcs, groups, gather, name):
    n = len(srcs)
    if gather:
        lands = [lax.empty((NDEV * a.shape[0], a.shape[1]), a.dtype) for a in srcs]
    else:
        lands = [lax.empty((NDEV, a.shape[0] // NDEV, a.shape[1]), a.dtype) for a in srcs]
    n_sem = 3 * len(groups)

    def body(*refs):
        src_refs, land_refs = refs[:n], refs[n:2 * n]
        sems = refs[2 * n:2 * n + n_sem]
        token = refs[-1]
        (_, my), sibling, chips = _places()
        _, _, peers = _peers()
        targets = [sibling] + chips if gather else peers
        for g, members in enumerate(groups):
            for j, k in enumerate(members):
                _own_copy(src_refs[k], land_refs[k], sems[3 * g + 2].at[j], my, gather).start()
        for g, members in enumerate(groups):
            for j, k in enumerate(members):
                rows = srcs[k].shape[0] if gather else srcs[k].shape[0] // NDEV
                for pos, idx in targets:
                    src = src_refs[k] if gather else src_refs[k].at[pl.ds(idx * rows, rows)]
                    dst = land_refs[k].at[pl.ds(my * rows, rows)] if gather else land_refs[k].at[my]
                    pltpu.make_async_remote_copy(src_ref=src, dst_ref=dst, send_sem=sems[3 * g].at[j],
                                                 recv_sem=sems[3 * g + 1].at[j], device_id=pos, device_id_type=MESH).start()
        token[...] = jnp.zeros_like(token)

    out_shape = []
    for members in groups:
        out_shape += 3 * [pltpu.SemaphoreType.DMA((len(members),))]
    out_shape += [pltpu.HBM(a.shape, a.dtype) for a in srcs] + [pltpu.HBM(a.shape, a.dtype) for a in lands]
    out_shape.append(jax.ShapeDtypeStruct((8, 128), F32))
    res = pl.pallas_call(
        body, name=name, out_shape=tuple(out_shape), in_specs=2 * n * [_HBM],
        out_specs=tuple(n_sem * [_SEM] + 2 * n * [_HBM] + [pl.BlockSpec(memory_space=pltpu.VMEM)]),
        input_output_aliases={i: n_sem + i for i in range(2 * n)},
        compiler_params=pltpu.CompilerParams(has_side_effects=_DATAFLOW),
    )(*[pltpu.with_memory_space_constraint(a, pltpu.HBM) for a in list(srcs) + lands])
    sems = [tuple(res[3 * g:3 * g + 3]) for g in range(len(groups))]
    return sems, list(res[n_sem:n_sem + n]), list(res[n_sem + n:n_sem + 2 * n]), res[-1]


def _own_copy(src_ref, land_ref, sem, my, gather):
    if gather:
        rows = src_ref.shape[0]
        return pltpu.make_async_copy(src_ref, land_ref.at[pl.ds(my * rows, rows)], sem)
    rows = src_ref.shape[0] // NDEV
    return pltpu.make_async_copy(src_ref.at[pl.ds(my * rows, rows)], land_ref.at[my], sem)


def _wait_all(land_ref, blocks_per_dev, copies, send_sem, recv_sem, me_pos):
    part = land_ref.at[pl.ds(0, copies * blocks_per_dev)]
    pltpu.make_async_remote_copy(src_ref=part, dst_ref=part, send_sem=send_sem, recv_sem=recv_sem,
                                 device_id=me_pos, device_id_type=MESH).wait()


def _gather_forward(sems, srcs, lands, after, name):
    n = len(srcs)

    def body(*refs):
        land_refs = refs[n:2 * n]
        send_a, recv_a = refs[2 * n], refs[2 * n + 1]
        send_b, recv_b = refs[2 * n + 3], refs[2 * n + 4]
        token = refs[-1]
        (me_pos, _), sibling, chips = _places()
        for j in range(n):
            _wait_all(land_refs[j], lands[j].shape[0] // NDEV, 1 + OTHER_CHIPS, send_a.at[j], recv_a.at[j], me_pos)
        for j in range(n):
            rows = lands[j].shape[0] // NDEV
            for _, idx in chips:
                block = land_refs[j].at[pl.ds(idx * rows, rows)]
                pltpu.make_async_remote_copy(src_ref=block, dst_ref=block, send_sem=send_b.at[j], recv_sem=recv_b.at[j],
                                             device_id=sibling[0], device_id_type=MESH).start()
        token[...] = jnp.zeros_like(token)

    res = pl.pallas_call(
        body, name=name,
        out_shape=(pltpu.SemaphoreType.DMA((n,)), pltpu.SemaphoreType.DMA((n,)))
        + tuple(pltpu.HBM(a.shape, a.dtype) for a in list(srcs) + list(lands)) + (jax.ShapeDtypeStruct((8, 128), F32),),
        in_specs=2 * n * [_HBM] + [_SEM, _SEM, pl.BlockSpec(memory_space=pl.ANY)],
        out_specs=tuple([_SEM, _SEM] + 2 * n * [_HBM] + [pl.BlockSpec(memory_space=pltpu.VMEM)]),
        input_output_aliases={i: 2 + i for i in range(2 * n)},
        compiler_params=pltpu.CompilerParams(has_side_effects=_DATAFLOW),
    )(*srcs, *lands, sems[0], sems[1], after)
    return (res[0], res[1]), list(res[2:2 + n]), list(res[2 + n:2 + 2 * n]), res[-1]


def _split_wait(sems, srcs, lands, after, copies, gather, name):
    n = len(srcs)

    def body(*refs):
        src_refs, land_refs = refs[:n], refs[n:2 * n]
        send_sem, recv_sem, local_sem = refs[2 * n], refs[2 * n + 1], refs[2 * n + 2]
        (me_pos, my), _, _ = _places()
        for j in range(n):
            _wait_all(land_refs[j], lands[j].shape[0] // NDEV, copies, send_sem.at[j], recv_sem.at[j], me_pos)
            _own_copy(src_refs[j], land_refs[j], local_sem.at[j], my, gather).wait()

    res = pl.pallas_call(
        body, name=name, out_shape=tuple(pltpu.HBM(a.shape, a.dtype) for a in list(srcs) + list(lands)),
        in_specs=2 * n * [_HBM] + [_SEM, _SEM, _SEM, pl.BlockSpec(memory_space=pl.ANY)], out_specs=tuple(2 * n * [_HBM]),
        input_output_aliases={i: i for i in range(2 * n)},
        compiler_params=pltpu.CompilerParams(has_side_effects=_DATAFLOW),
    )(*srcs, *lands, sems[0], sems[1], sems[2], after)
    return list(res[n:])


def _chained(gate, mid, after):
    return gate if mid is None else gate + mid(after)[:1, :1]


def _ffn_fwd(x, norms, mod, w, mid=None):
    (pre_g, post_g), (shift, scale, gate), (wg_t, wu_t, wd) = norms, mod, w
    if not callable(wd):
        hn, g, u, a, x_out, f = _ffn_fwd_fused(x, pre_g, scale, shift, post_g, _chained(gate, mid, x), wg_t, wu_t, wd, "ffn_fwd")
        return x_out, (x, hn, g, u, a, f), (wg_t, wu_t, wd)
    hn, g, u, a = _ffn_up(x, pre_g, scale, shift, wg_t, wu_t, "ffn_up")
    wd = wd(a)
    x_out, f = _mm_post(a, wd, x, post_g, _chained(gate, mid, a), FFN_RES, "ffn_down")
    return x_out, (x, hn, g, u, a, f), (wg_t, wu_t, wd)


def _ffn_bwd(dx_out, saved, norms, mod, w, send=None):
    (pre_g, post_g), (_, scale, gate), (wg_t, wu_t, wd) = norms, mod, w
    x, hn, g, u, a, f = saved
    d_model = x.shape[1]
    if send is None:
        df, dg, du, dx, dgate, dpost, dshift, dscale, dpre = _ffn_bwd_fused(dx_out, saved, pre_g, post_g, scale, gate,
                                                                            wg_t, wu_t, wd, "ffn_bwd")
        return dx, (dpre, dpost), (dshift, dscale, dgate), tuple(_ffn_dw(dg, du, a, hn, df, "ffn_dw3"))
    sent = send
    df, dgate, dpost = _post_bwd(dx_out, f, post_g, gate, FFN_RES, "ffn_post_bwd")
    dwd = _mm([(a, df)], "tn", BF16, 256, d_model, "ffn_dw")
    dg, du = _ffn_dgu(df, wd, g, u, "ffn_dgu", after=sent(2, dwd))
    dwg_t = _mm([(dg, hn)], "tn", BF16, 256, d_model, "ffn_dw")
    dwu_t = _mm([(du, hn)], "tn", BF16, 256, d_model, "ffn_dw", after=sent(0, dwg_t))
    dhn = _mm([(dg, wg_t), (du, wu_t)], "nn", F32, TOKEN_TILE, d_model, "ffn_dhn", after=sent(1, dwu_t))
    dx, dshift, dscale, dpre = _prenorm_bwd(dx_out, [dhn], x, pre_g, scale, "prenorm_bwd")
    return dx, (dpre, dpost), (dshift, dscale, dgate), (dwg_t, dwu_t, dwd)


def _mla_fwd(x, norms, mod, w, rope, mid=None):
    (pre_g, post_g), (shift, scale, gate) = norms, mod
    w_in, q_norm, wq_t, kv_norm, wkv_t, wo = w
    hn, lat = _prenorm_mm(x, pre_g, scale, shift, w_in, "nn", F32, LAT_PAD, "mla_in")
    gate = _chained(gate, mid, lat)
    q, k, v, qn, kvn = _mla_qkv(lat, q_norm, kv_norm, wq_t, wkv_t, rope, "mla_qkv")
    o = _mla_attn_fwd(q, k, v, "mla_attn_fwd")
    x_out, f = _mm_post(o, wo, x, post_g, gate, 1.0, "mla_out")
    return x_out, (x, hn, lat, q, k, v, qn, kvn, o, f)


def _mla_bwd(dx_out, saved, norms, mod, w, rope):
    (pre_g, post_g), (_, scale, gate) = norms, mod
    w_in, q_norm, wq_t, kv_norm, wkv_t, wo = w
    x, hn, lat, q, k, v, qn, kvn, o, f = saved
    d_model = x.shape[1]
    df, dgate, dpost = _post_bwd(dx_out, f, post_g, gate, 1.0, "mix_post_bwd")
    d_o = _mm([(df, wo)], "nt", F32, TOKEN_TILE, wo.shape[0], "mla_do")
    dwo = _mm([(o, df)], "tn", BF16, TOKEN_TILE, d_model, "mla_dwo")
    dq, dk, dv = _mla_attn_bwd(q, k, v, d_o, "mla_attn_bwd")
    dqp, dkv, dlat, dq_norm, dkv_norm = _mla_qkv_bwd(dq, dk, dv, lat, q_norm, kv_norm, wq_t, wkv_t, rope, "mla_qkv_bwd")
    dwq_t = _mm([(dqp, qn)], "tn", BF16, TOKEN_TILE, Q_LORA, "mla_dwq")
    dwkv_t = _mm([(dkv, kvn)], "tn", BF16, TOKEN_TILE, KV_LORA, "mla_dwkv")
    dw_in = _mm([(hn, dlat)], "tn", BF16, TOKEN_TILE, LAT_PAD, "mla_dwin")
    dhn = _mm([(dlat, w_in)], "nt", F32, TOKEN_TILE, d_model, "mla_dhn")
    dx, dshift, dscale, dpre = _prenorm_bwd(dx_out, [dhn], x, pre_g, scale, "prenorm_bwd")
    return dx, (dpre, dpost), (dshift, dscale, dgate), (dw_in, dq_norm, dwq_t, dkv_norm, dwkv_t, dwo)


def _dil_fwd(x, norms, mod, w, bias, mid=None):
    (pre_g, post_g), (shift, scale, gate), (w_in_t, wo) = norms, mod, w
    width = 3 * DIL_HEADS * DIL_HEAD_DIM
    hns, qkvs, outs, lses = [], [], [], []
    for g, (window, dilation) in enumerate(DIL_GROUPS):
        hn, qkv = _prenorm_mm(x, pre_g, scale, shift, w_in_t, "nt", BF16, width, "dil_in", perm=dilation,
                              w_rows=(g * width, width))
        if g == 0:
            gate = _chained(gate, mid, qkv)
        o, lse = _dil_attn_fwd(qkv, bias[g], dilation, window // dilation, "dil_attn_fwd")
        hns.append(hn), qkvs.append(qkv), outs.append(o), lses.append(lse)
    alphas, o_mix, o_mix_b = _dil_mix(lses, outs, "dil_mix")
    x_out, f = _mm_post(o_mix_b, wo, x, post_g, gate, 1.0, "dil_out")
    return x_out, (x, hns, qkvs, lses, alphas, o_mix, o_mix_b, f)


def _dil_bwd(dx_out, saved, norms, mod, w, bias):
    (pre_g, post_g), (_, scale, gate), (w_in_t, wo) = norms, mod, w
    x, hns, qkvs, lses, alphas, o_mix, o_mix_b, f = saved
    d_model = x.shape[1]
    inner = DIL_HEADS * DIL_HEAD_DIM
    df, dgate, dpost = _post_bwd(dx_out, f, post_g, gate, 1.0, "mix_post_bwd")
    d_o = _mm([(df, wo)], "nt", F32, TOKEN_TILE, inner, "dil_do")
    dwo = _mm([(o_mix_b, df)], "tn", BF16, TOKEN_TILE, d_model, "dil_dwo")
    dhns, dws, dbs = [], [], []
    for g, (window, dilation) in enumerate(DIL_GROUPS):
        grads = _dil_attn_bwd(qkvs[g], bias[g], d_o, o_mix, alphas[g], lses[g], dilation, window // dilation, "dil_attn_bwd")
        dbs.append(grads[3])
        dhns.append(_mm([(grads[j], w_in_t) for j in range(3)], "nn", F32, TOKEN_TILE, d_model, "dil_dhn", out_perm=dilation,
                        b_rows=[(3 * g + j) * inner for j in range(3)]))
        dws += list(_mm_tn_shared(list(grads[:3]), hns[g], "dil_dwin"))
    dx, dshift, dscale, dpre = _prenorm_bwd(dx_out, dhns, x, pre_g, scale, "prenorm_bwd3")
    return dx, (dpre, dpost), (dshift, dscale, dgate), (jnp.concatenate(dws, axis=0), dwo), jnp.concatenate(dbs, axis=0)


def _pad_rows(a, rows):
    return jnp.pad(a, ((0, rows - a.shape[0]), (0, 0)))


def _lanes(a):
    flat = a.reshape(-1).astype(F32)
    rows = -(-flat.shape[0] // 1024) * 8
    return jnp.pad(flat, (0, rows * 128 - flat.shape[0])).reshape(rows, 128)


def kernel(x, c, norm_pre, norm_post, w_mod, b_mod, ffn_w_gate, ffn_w_up, ffn_w_down, mla_w_in, mla_q_norm, mla_w_q_up, mla_kv_norm, mla_w_kv_up, mla_w_o, dil_w_in, dil_w_o, rel_bias, loss_target, m_norm_pre, m_norm_post, m_w_mod, m_b_mod, m_ffn_w_gate, m_ffn_w_up, m_ffn_w_down, m_mla_w_in, m_mla_q_norm, m_mla_w_q_up, m_mla_kv_norm, m_mla_w_kv_up, m_mla_w_o, m_dil_w_in, m_dil_w_o, m_rel_bias, v_norm_pre, v_norm_post, v_w_mod, v_b_mod, v_ffn_w_gate, v_ffn_w_up, v_ffn_w_down, v_mla_w_in, v_mla_q_norm, v_mla_w_q_up, v_mla_kv_norm, v_mla_w_kv_up, v_mla_w_o, v_dil_w_in, v_dil_w_o, v_rel_bias):
    me = 4 * lax.axis_index("x") + 2 * lax.axis_index("y") + lax.axis_index("c")
    depth, n_sub, d_loc = norm_pre.shape
    d_model = x.shape[2]
    mod_loc_cols = w_mod.shape[2]
    x0, target = x[0], loss_target[0]

    bf_t = lambda a: a.astype(BF16).T
    ffn_ids = [(i, h) for i in range(depth) for h in range(2)]
    shards = []
    for i, h in ffn_ids:
        shards += [bf_t(ffn_w_gate[i, h]), bf_t(ffn_w_up[i, h]), ffn_w_down[i, h].astype(BF16)]
    shards += [mla_w_in[0].astype(BF16), bf_t(mla_w_q_up[0]), bf_t(mla_w_kv_up[0]), mla_w_o[0].astype(BF16),
               bf_t(dil_w_in[0]), dil_w_o[0].astype(BF16)]
    n_ffn = 3 * len(ffn_ids)
    members = {(0, 0): [0, 1, 2], (0, 1): [n_ffn, n_ffn + 1, n_ffn + 2, n_ffn + 3], (0, 2): [3, 4, 5],
               (1, 0): [6, 7, 8], (1, 1): [n_ffn + 4, n_ffn + 5], (1, 2): [9, 10, 11]}
    order = [(i, s) for i in range(depth) for s in range(n_sub)]

    small = jnp.concatenate([c.reshape(8, 128), _pad_rows(norm_pre.reshape(depth * n_sub, d_loc), 8),
                             _pad_rows(norm_post.reshape(depth * n_sub, d_loc), 8)], axis=0)
    small_all = _exchange([small], True, "gather_small")[0].reshape(NDEV, 24, 128)
    c_all = small_all[:, 0:8].reshape(NDEV, d_model)
    gains = lambda lo: jnp.transpose(small_all[:, lo:lo + depth * n_sub], (1, 0, 2)).reshape(depth, n_sub, 1, d_model)
    pre_full, post_full = gains(8), gains(16)

    b_loc = lax.dynamic_slice(b_mod, (0, me * mod_loc_cols), (depth, mod_loc_cols))
    mod_cols, silu_c = _mod_fwd(c_all, w_mod, b_loc, "mod_fwd")
    mod_all = _exchange([mod_cols.reshape(depth * NDEV, mod_loc_cols)], True, "gather_mod")[0]
    mod_all = mod_all.reshape(NDEV, depth, NDEV, mod_loc_cols)
    mod_mine = lax.dynamic_index_in_dim(mod_all, me, axis=2, keepdims=False)
    mod = jnp.transpose(mod_mine, (1, 0, 2)).reshape(depth, n_sub, 3, 1, d_model)

    shards[0], _ = lax.optimization_barrier((shards[0], mod_all))
    first = order[0]
    stages = [("%d%d" % first, members[first][:2]), ("%d%dd" % first, members[first][2:])]
    stages += [("%d%d" % key, members[key]) for key in order[1:]]
    started = {}

    def start(these, name):
        used = [k for _, idx in these for k in idx]
        sems, srcs, lands, token = _split_start([shards[k] for k in used], [[used.index(k) for k in idx] for _, idx in these],
                                                True, name)
        for n, (stage, idx) in enumerate(these):
            started[stage] = (sems[n], [srcs[used.index(k)] for k in idx], [lands[used.index(k)] for k in idx])
        return token

    g_token = start(stages[:2], "gather_weights_start_first")
    later = stages[2][1][0]
    shards[later], _ = lax.optimization_barrier((shards[later], g_token))
    g_token = start(stages[2:], "gather_weights_start_rest")

    forwarded = {}

    def forward(stage, after):
        sems, srcs, lands = started[stage]
        forwarded[stage] = _gather_forward(sems, srcs, lands, after, "gather_forward_" + stage)
        return forwarded[stage][3]

    def weights_of(stage, after):
        (send_b, recv_b), srcs, lands, _ = forwarded[stage]
        return _split_wait((send_b, recv_b, started[stage][0][2]), srcs, lands, after, OTHER_CHIPS, True, "gather_wait_" + stage)

    def late_down(after):
        forward("%d%dd" % first, after)
        return weights_of("%d%dd" % first, after)[0]

    lat_real = Q_LORA + KV_LORA
    qk = QK_NOPE + QK_ROPE

    def mla_weights(after):
        w_in, wq_t, wkv_t, wo = weights_of("01", after)
        w_in_pad = jnp.concatenate([w_in[:, :lat_real], jnp.zeros((d_model, QK_NOPE), BF16), w_in[:, lat_real:],
                                    jnp.zeros((d_model, HEAD_PAD - QK_NOPE - QK_ROPE), BF16)], axis=1)
        wq_pad = jnp.pad(wq_t.reshape(MLA_HEADS, qk, Q_LORA), ((0, 0), (0, HEAD_PAD - qk), (0, 0)))
        wo_pad = jnp.pad(wo.reshape(MLA_HEADS, V_HEAD, d_model), ((0, 0), (HEAD_PAD - V_HEAD, 0), (0, 0)))
        return (w_in_pad, mla_q_norm, wq_pad.reshape(MLA_HEADS * HEAD_PAD, Q_LORA), mla_kv_norm, wkv_t,
                wo_pad.reshape(MLA_HEADS * HEAD_PAD, d_model))

    zero = g_token[0, 0]
    rope = _rope_tables(zero)
    buckets = jnp.stack([_dil_buckets(dil) for _, dil in DIL_GROUPS]) + zero.astype(jnp.int32)
    onehot = (buckets[..., None] == jnp.arange(N_BUCKETS)).astype(F32)
    bias = jnp.einsum("gqkb,bgh->ghqk", onehot, rel_bias.reshape(N_BUCKETS, len(DIL_GROUPS), DIL_HEADS),
                      precision=lax.Precision.HIGHEST)

    norms = lambda i, s: (pre_full[i, s], post_full[i, s])
    mods = lambda i, s: (mod[i, s, 0], mod[i, s, 1], mod[i, s, 2])
    saved, weights = {}, {}
    h = lax.optimization_barrier((x0, bias, buckets, *rope))[0]
    forward("%d%d" % first, h)
    for n, (i, s) in enumerate(order):
        got = mla_weights(h) if (s == 1 and i % 2 == 0) else tuple(weights_of("%d%d" % (i, s), h))
        mid = None if n + 1 == len(order) else (lambda after, nxt="%d%d" % order[n + 1]: forward(nxt, after))
        if s != 1:
            if len(got) == 3:
                h, saved[i, s], weights[i, s] = _ffn_fwd(h, norms(i, s), mods(i, s), got)
                if mid is not None:
                    mid(h)
            else:
                h, saved[i, s], weights[i, s] = _ffn_fwd(h, norms(i, s), mods(i, s), (*got, late_down), mid)
            continue
        weights[i, s] = got
        if i % 2 == 0:
            h, saved[i, s] = _mla_fwd(h, norms(i, s), mods(i, s), weights[i, s], rope, mid)
        else:
            h, saved[i, s] = _dil_fwd(h, norms(i, s), mods(i, s), weights[i, s], bias, mid)
    dh, loss_parts = _loss_grad(h, target, "loss")

    dnorm, dmod, sent = {}, {}, {}
    token = jnp.zeros((8, 128), F32)
    last = order[0]

    def send_last(j, dw):
        sent[last, j] = _split_start([dw], [[0]], False, "scatter_start_%d%d_%d" % (*last, j))
        return sent[last, j][3]

    for i, s in reversed(order):
        md = mods(i, s)
        md = (md[0], md[1], md[2] + token[:1, :1])
        if (i, s) == last:
            dh, dnorm[i, s], dmod[i, s], _ = _ffn_bwd(dh, saved[i, s], norms(i, s), md, weights[i, s], send_last)
            continue
        if s != 1:
            dh, dnorm[i, s], dmod[i, s], dws = _ffn_bwd(dh, saved[i, s], norms(i, s), md, weights[i, s])
        elif i % 2 == 0:
            dh, dnorm[i, s], dmod[i, s], dmla = _mla_bwd(dh, saved[i, s], norms(i, s), md, weights[i, s], rope)
            dw_in_pad, dq_norm, dwq_pad, dkv_norm, dwkv_t, dwo_pad = dmla
            dw_in = jnp.concatenate([dw_in_pad[:, :lat_real], dw_in_pad[:, lat_real + QK_NOPE:lat_real + qk]], axis=1)
            dwq_t = dwq_pad.reshape(MLA_HEADS, HEAD_PAD, Q_LORA)[:, :qk].reshape(MLA_HEADS * qk, Q_LORA)
            dwo = dwo_pad.reshape(MLA_HEADS, HEAD_PAD, d_model)[:, HEAD_PAD - V_HEAD:].reshape(MLA_HEADS * V_HEAD, d_model)
            dws = (dw_in, dwq_t, dwkv_t, dwo)
        else:
            dh, dnorm[i, s], dmod[i, s], dws, dbias = _dil_bwd(dh, saved[i, s], norms(i, s), md, weights[i, s], bias)
        sent[i, s] = _split_start(list(dws), [list(range(len(dws)))], False, "scatter_start_%d%d" % (i, s))
        token = sent[i, s][3]
    grad_x = dh[None]

    mine = {}
    transposed = {3 * n + j for n in range(len(ffn_ids)) for j in (0, 1)} | {n_ffn + 1, n_ffn + 2, n_ffn + 4}
    for key in reversed(order[1:]):
        sems, srcs, lands, _ = sent[key]
        parts = _split_wait(sems[0], srcs, lands, dh, NDEV - 1, False, "scatter_wait_%d%d" % key)
        for k, p in zip(members[key], parts):
            mine[k] = _sum_parts(p, "sum_parts", k in transposed)
    g_mla_in, g_q_up, g_kv_up, g_mla_o, g_dil_in, g_dil_o = (mine[k] for k in range(n_ffn, n_ffn + 6))
    g_mla_in, g_q_up, g_kv_up, g_mla_o = g_mla_in[None], g_q_up[None], g_kv_up[None], g_mla_o[None]
    g_dil_in, g_dil_o = g_dil_in[None], g_dil_o[None]
    early = {"mla_w_in": _adamw(mla_w_in, g_mla_in, m_mla_w_in, v_mla_w_in, "adamw"),
             "mla_w_q_up": _adamw(mla_w_q_up, g_q_up, m_mla_w_q_up, v_mla_w_q_up, "adamw"),
             "mla_w_kv_up": _adamw(mla_w_kv_up, g_kv_up, m_mla_w_kv_up, v_mla_w_kv_up, "adamw"),
             "mla_w_o": _adamw(mla_w_o, g_mla_o, m_mla_w_o, v_mla_w_o, "adamw"),
             "dil_w_in": _adamw(dil_w_in, g_dil_in, m_dil_w_in, v_dil_w_in, "adamw"),
             "dil_w_o": _adamw(dil_w_o, g_dil_o, m_dil_w_o, v_dil_w_o, "adamw")}
    dbias_sums = _bias_reduce(dbias, buckets, "bias_reduce")
    tied = lax.optimization_barrier((dbias_sums, *[a for step in early.values() for a in step]))
    dbias_sums, early = tied[0], {name: tuple(tied[1 + 3 * n:4 + 3 * n]) for n, name in enumerate(early)}
    for j in (2, 0, 1):
        sems, srcs, lands, _ = sent[last, j]
        parts = _split_wait(sems[0], srcs, lands, dbias_sums, NDEV - 1, False, "scatter_wait_%d%d_%d" % (*last, j))
        mine[members[last][j]] = _sum_parts(parts[0], "sum_parts", members[last][j] in transposed)
    g_gate = jnp.stack([mine[3 * n] for n in range(len(ffn_ids))]).reshape(ffn_w_gate.shape)
    g_up = jnp.stack([mine[3 * n + 1] for n in range(len(ffn_ids))]).reshape(ffn_w_up.shape)
    g_down = jnp.stack([mine[3 * n + 2] for n in range(len(ffn_ids))]).reshape(ffn_w_down.shape)

    dmod_mine = jnp.concatenate([jnp.concatenate(dmod[i, s], axis=0) for i in range(depth) for s in range(n_sub)], axis=0)
    dpre_mine = jnp.concatenate([dnorm[i, s][0] for i in range(depth) for s in range(n_sub)], axis=0)
    dpost_mine = jnp.concatenate([dnorm[i, s][1] for i in range(depth) for s in range(n_sub)], axis=0)
    dbias_tab = dbias_sums[:, 0, :N_BUCKETS].T
    pieces = [dmod_mine, dpre_mine, dpost_mine, dq_norm, dkv_norm, dbias_tab, jnp.sum(loss_parts).reshape(1, 1)]
    packed = [_lanes(p) for p in pieces]
    offs = [0]
    for p in packed:
        offs.append(offs[-1] + p.shape[0])
    everyone = _exchange([jnp.concatenate(packed, axis=0)], True, "gather_small_grads")[0].reshape(NDEV, offs[-1], 128)
    total = _sum_parts(everyone, "sum_small")
    take = lambda n, shape: total[offs[n]:offs[n + 1]].reshape(-1)[:math.prod(shape)].reshape(shape)
    g_b_mod = take(0, b_mod.shape)
    col0 = me * d_loc
    g_norm_pre = lax.dynamic_slice(take(1, (depth, n_sub, d_model)), (0, 0, col0), norm_pre.shape)
    g_norm_post = lax.dynamic_slice(take(2, (depth, n_sub, d_model)), (0, 0, col0), norm_post.shape)
    g_q_norm, g_kv_norm = take(3, mla_q_norm.shape), take(4, mla_kv_norm.shape)
    g_rel_bias = take(5, rel_bias.shape)
    loss = take(6, ())

    dmod_all = everyone[:, offs[0]:offs[1]].reshape(NDEV, depth, NDEV * mod_loc_cols)
    dmod_cols = lax.dynamic_slice(dmod_all, (0, 0, me * mod_loc_cols), (NDEV, depth, mod_loc_cols))
    silu_t = jnp.pad(silu_c.T, ((0, 0), (0, HEAD_PAD - NDEV)))
    g_w_mod = jnp.stack([_mm([(silu_t, jnp.pad(dmod_cols[:, i], ((0, HEAD_PAD - NDEV), (0, 0))))], "nn", F32, TOKEN_TILE,
                             mod_loc_cols, "mod_bwd") for i in range(depth)])

    ws = (norm_pre, norm_post, w_mod, b_mod, ffn_w_gate, ffn_w_up, ffn_w_down, mla_w_in, mla_q_norm, mla_w_q_up, mla_kv_norm,
          mla_w_kv_up, mla_w_o, dil_w_in, dil_w_o, rel_bias)
    gs = (g_norm_pre, g_norm_post, g_w_mod, g_b_mod, g_gate, g_up, g_down, g_mla_in, g_q_norm, g_q_up, g_kv_norm, g_kv_up,
          g_mla_o, g_dil_in, g_dil_o, g_rel_bias)
    ms = (m_norm_pre, m_norm_post, m_w_mod, m_b_mod, m_ffn_w_gate, m_ffn_w_up, m_ffn_w_down, m_mla_w_in, m_mla_q_norm,
          m_mla_w_q_up, m_mla_kv_norm, m_mla_w_kv_up, m_mla_w_o, m_dil_w_in, m_dil_w_o, m_rel_bias)
    vs = (v_norm_pre, v_norm_post, v_w_mod, v_b_mod, v_ffn_w_gate, v_ffn_w_up, v_ffn_w_down, v_mla_w_in, v_mla_q_norm,
          v_mla_w_q_up, v_mla_kv_norm, v_mla_w_kv_up, v_mla_w_o, v_dil_w_in, v_dil_w_o, v_rel_bias)
    names = ("norm_pre", "norm_post", "w_mod", "b_mod", "ffn_w_gate", "ffn_w_up", "ffn_w_down", "mla_w_in", "mla_q_norm",
             "mla_w_q_up", "mla_kv_norm", "mla_w_kv_up", "mla_w_o", "dil_w_in", "dil_w_o", "rel_bias")
    stepped = [early[n] if n in early else _adamw(w, g, m, v, "adamw") for n, w, g, m, v in zip(names, ws, gs, ms, vs)]
    deltas, new_m, new_v = zip(*stepped)
    return (loss, grad_x, *gs, *deltas, *new_m, *new_v)
```

```python
import math

import jax
import jax.numpy as jnp
from jax import lax
from jax.experimental import pallas as pl
from jax.experimental.pallas import tpu as pltpu

F32 = jnp.float32
BF16 = jnp.bfloat16
MESH = pl.DeviceIdType.MESH

NDEV = 8
OTHER_CHIPS = 3
D_MODEL = 1024
SEQ = 2048
D_FF = 2816
EPS = 1e-6
FFN_RES = 0.5
FFN_CHUNKS = 11

MLA_HEADS = 16
Q_LORA = 384
KV_LORA = 256
QK_NOPE = 64
QK_ROPE = 32
V_HEAD = 64
ROPE_THETA = 10000.0
HEAD_PAD = 128
LAT_PAD = Q_LORA + KV_LORA + HEAD_PAD
MLA_SCALE = (QK_NOPE + QK_ROPE) ** -0.5
MLA_QUERY_TILE = 256

DIL_GROUPS = ((128, 1), (512, 4), (2048, 16))
DIL_HEADS = 16
DIL_HEAD_DIM = 64
DIL_BLOCK = 128
DIL_PAIRS = DIL_HEADS // 2
DIL_SCALE = DIL_HEAD_DIM ** -0.5
DIL_GROUPED = 8
N_BUCKETS = 32
MAX_DISTANCE = 2048

ADAM_LR = 0.001
ADAM_B1 = 0.9
ADAM_B2 = 0.999
ADAM_EPS = 1e-08
ADAM_WD = 0.01
ADAM_STEP = 10

V7X_VMEM_BYTES = 64 * 2**20
VMEM_RESERVE = 10 * 2**20
TOKEN_TILE = 512


def _nbytes(shape, dtype):
    return math.prod(shape) * jnp.dtype(dtype).itemsize


def _params(semantics, blocks, extra=0):
    need = 2 * sum(_nbytes(s, d) for s, d in blocks) + extra + VMEM_RESERVE
    return pltpu.CompilerParams(dimension_semantics=semantics,
                                vmem_limit_bytes=int(min(need, V7X_VMEM_BYTES - VMEM_RESERVE)))


def _pcall(body, out_shape, **kw):
    call = pl.pallas_call(body, out_shape=jax.tree.map(lambda s: pltpu.HBM(s.shape, s.dtype), out_shape), **kw)
    return lambda *args: call(*[pltpu.with_memory_space_constraint(a, pltpu.HBM) for a in args])


def _dot_nn(a, b):
    return lax.dot_general(a, b, (((1,), (0,)), ((), ())), preferred_element_type=F32)


def _dot_nt(a, b):
    return lax.dot_general(a, b, (((1,), (1,)), ((), ())), preferred_element_type=F32)


def _dot_tn(a, b):
    return lax.dot_general(a, b, (((0,), (0,)), ((), ())), preferred_element_type=F32)


_DOTS = {"nn": _dot_nn, "nt": _dot_nt, "tn": _dot_tn}


def _rstd(v):
    return lax.rsqrt(jnp.mean(v * v, axis=-1, keepdims=True) + EPS)


def _rms_bwd(v, r, t):
    return r * t - v * (r * r * r) * jnp.mean(t * v, axis=-1, keepdims=True)


_TOKEN_SPEC = pl.BlockSpec((8, 128), lambda *_: (0, 0))


def _mm(pairs, mode, out_dtype, tm, tn, name, out_perm=1, after=None, b_rows=None):
    a0, b0 = pairs[0]
    m_dim = a0.shape[1] if mode == "tn" else a0.shape[0]
    n_dim = b0.shape[0] if mode == "nt" else b0.shape[1]
    tm, tn = min(tm, m_dim // out_perm), min(tn, n_dim)
    assert m_dim % tm == 0 and n_dim % tn == 0, (name, m_dim, n_dim, tm, tn)
    dot = _DOTS[mode]
    npairs = len(pairs)

    def body(*refs):
        acc = None
        for p in range(npairs):
            d = dot(refs[2 * p][...].astype(BF16), refs[2 * p + 1][...].astype(BF16))
            acc = d if acc is None else acc + d
        refs[-1][...] = acc.astype(out_dtype)

    in_specs, blocks, flat = [], [], []
    for n_pair, (a, b) in enumerate(pairs):
        if mode == "nn":
            k = a.shape[1]
            first_block = 0 if b_rows is None else b_rows[n_pair] // k
            sa, sb = ((tm, k), lambda i, j: (i, 0)), ((k, tn), lambda i, j, o=first_block: (o, j))
        elif mode == "nt":
            k = a.shape[1]
            sa, sb = ((tm, k), lambda i, j: (i, 0)), ((tn, k), lambda i, j: (j, 0))
        else:
            k = a.shape[0]
            sa, sb = ((k, tm), lambda i, j: (0, i)), ((k, tn), lambda i, j: (0, j))
        in_specs += [pl.BlockSpec(*sa), pl.BlockSpec(*sb)]
        blocks += [(sa[0], a.dtype), (sb[0], b.dtype)]
        flat += [a, b]
    if after is not None:
        in_specs.append(_TOKEN_SPEC)
        flat.append(after)
    if out_perm == 1:
        out_shape = (m_dim, n_dim)
        out_spec = pl.BlockSpec((tm, tn), lambda i, j: (i, j))
    else:
        rows = m_dim // out_perm
        assert tn == n_dim and rows % tm == 0, (name, rows, tm)
        nb = rows // tm
        out_shape = (rows, out_perm * n_dim)
        out_spec = pl.BlockSpec((tm, n_dim), lambda i, j: (i % nb, i // nb))
    blocks.append(((tm, tn), out_dtype))
    res = _pcall(
        body, out_shape=jax.ShapeDtypeStruct(out_shape, out_dtype), grid=(m_dim // tm, n_dim // tn),
        in_specs=in_specs, out_specs=out_spec, name=name,
        compiler_params=_params(("parallel", "parallel"), blocks, extra=2 * tm * tn * 4),
    )(*flat)
    return res.reshape(m_dim, n_dim)


def _prenorm_mm(x, pre_g, scale, shift, w, w_mode, out_dtype, tn, name, perm=1, w_rows=None):
    s_dim, d_dim = x.shape
    n_dim = w.shape[0] if w_mode == "nt" else w.shape[1]
    w_first = 0
    if w_rows is not None:
        w_first, n_dim = w_rows
    rows = s_dim // perm
    side = max(1, TOKEN_TILE // rows)
    tm = side * min(TOKEN_TILE, rows)
    nb = max(1, rows // tm)
    tn = min(tn, n_dim)
    assert n_dim % tn == 0 and w_first % tn == 0
    w_block0 = w_first // tn
    dot = _DOTS[w_mode]

    def body(x_ref, g_ref, sc_ref, sh_ref, w_ref, hn_ref, o_ref):
        @pl.when(pl.program_id(1) == 0)
        def _():
            xf = x_ref[...]
            if side > 1:
                xf = jnp.concatenate([xf[:, c * d_dim:(c + 1) * d_dim] for c in range(side)], axis=0)
            hn = (xf * _rstd(xf) * g_ref[...]) * (1.0 + sc_ref[...]) + sh_ref[...]
            hn_ref[...] = hn.astype(BF16)

        o_ref[...] = dot(hn_ref[...], w_ref[...]).astype(out_dtype)

    vec = pl.BlockSpec((1, d_dim), lambda i, j: (0, 0))
    w_block = (tn, d_dim) if w_mode == "nt" else (d_dim, tn)
    w_spec = pl.BlockSpec(w_block, (lambda i, j: (w_block0 + j, 0)) if w_mode == "nt" else (lambda i, j: (0, j)))
    hn, out = _pcall(
        body,
        out_shape=(jax.ShapeDtypeStruct((s_dim, d_dim), BF16), jax.ShapeDtypeStruct((s_dim, n_dim), out_dtype)),
        grid=(s_dim // tm, n_dim // tn),
        in_specs=[pl.BlockSpec((tm // side, side * d_dim), lambda i, j: (i % nb, i // nb)), vec, vec, vec, w_spec],
        out_specs=(pl.BlockSpec((tm, d_dim), lambda i, j: (i, 0)), pl.BlockSpec((tm, tn), lambda i, j: (i, j))),
        name=name,
        compiler_params=_params(("parallel", "arbitrary"),
                                [((tm, d_dim), F32), (w_block, BF16), ((tm, d_dim), BF16), ((tm, tn), out_dtype)],
                                extra=3 * tm * d_dim * 4 + tm * tn * 4),
    )(x.reshape(rows, perm * d_dim), pre_g, scale, shift, w)
    return hn, out


def _ffn_up(x, pre_g, scale, shift, wg_t, wu_t, name):
    s_dim, d_dim = x.shape
    f_dim = wg_t.shape[0]
    tm, tn = TOKEN_TILE, f_dim // 2

    def body(x_ref, g_ref, sc_ref, sh_ref, wg_ref, wu_ref, hn_ref, go_ref, uo_ref, a_ref):
        @pl.when(pl.program_id(1) == 0)
        def _():
            xf = x_ref[...]
            hn = (xf * _rstd(xf) * g_ref[...]) * (1.0 + sc_ref[...]) + sh_ref[...]
            hn_ref[...] = hn.astype(BF16)

        hn = hn_ref[...]
        g = _dot_nt(hn, wg_ref[...])
        u = _dot_nt(hn, wu_ref[...])
        go_ref[...] = g.astype(BF16)
        uo_ref[...] = u.astype(BF16)
        a_ref[...] = (g * jax.nn.sigmoid(g) * u).astype(BF16)

    vec = pl.BlockSpec((1, d_dim), lambda i, j: (0, 0))
    w_spec = pl.BlockSpec((tn, d_dim), lambda i, j: (j, 0))
    act = pl.BlockSpec((tm, tn), lambda i, j: (i, j))
    act_shape = jax.ShapeDtypeStruct((s_dim, f_dim), BF16)
    return _pcall(
        body,
        out_shape=(jax.ShapeDtypeStruct((s_dim, d_dim), BF16), act_shape, act_shape, act_shape),
        grid=(s_dim // tm, f_dim // tn),
        in_specs=[pl.BlockSpec((tm, d_dim), lambda i, j: (i, 0)), vec, vec, vec, w_spec, w_spec],
        out_specs=(pl.BlockSpec((tm, d_dim), lambda i, j: (i, 0)), act, act, act),
        name=name,
        compiler_params=_params(("parallel", "arbitrary"),
                                [((tm, d_dim), F32), ((tn, d_dim), BF16), ((tn, d_dim), BF16), ((tm, d_dim), BF16)]
                                + 3 * [((tm, tn), BF16)], extra=3 * tm * d_dim * 4 + 4 * tm * tn * 4),
    )(x, pre_g, scale, shift, wg_t, wu_t)


def _mm_post(a, w, x, post_g, gate, res_w, name):
    s_dim, k_dim = a.shape
    d_dim = w.shape[1]
    tm = TOKEN_TILE

    def body(a_ref, w_ref, x_ref, pg_ref, gt_ref, xo_ref, f_ref):
        f = _dot_nn(a_ref[...], w_ref[...])
        y = f * _rstd(f) * pg_ref[...]
        f_ref[...] = f
        xo_ref[...] = x_ref[...] + (res_w * gt_ref[...]) * y

    vec = pl.BlockSpec((1, d_dim), lambda i: (0, 0))
    row = pl.BlockSpec((tm, d_dim), lambda i: (i, 0))
    out = jax.ShapeDtypeStruct((s_dim, d_dim), F32)
    return _pcall(
        body, out_shape=(out, out), grid=(s_dim // tm,),
        in_specs=[pl.BlockSpec((tm, k_dim), lambda i: (i, 0)), pl.BlockSpec((k_dim, d_dim), lambda i: (0, 0)), row, vec, vec],
        out_specs=(row, row), name=name,
        compiler_params=_params(("parallel",), [((tm, k_dim), BF16), ((k_dim, d_dim), BF16)] + 3 * [((tm, d_dim), F32)],
                                extra=3 * tm * d_dim * 4),
    )(a, w, x, post_g, gate)


def _post_bwd(dx_out, f, post_g, gate, res_w, name):
    s_dim, d_dim = f.shape
    tm = TOKEN_TILE

    def body(dx_ref, f_ref, pg_ref, gt_ref, df_ref, dgate_ref, dpost_ref):
        @pl.when(pl.program_id(0) == 0)
        def _():
            dgate_ref[...] = jnp.zeros_like(dgate_ref)
            dpost_ref[...] = jnp.zeros_like(dpost_ref)

        dx, fv = dx_ref[...], f_ref[...]
        r = _rstd(fv)
        fr = fv * r
        dgate_ref[...] += res_w * jnp.sum(dx * (fr * pg_ref[...]), axis=0, keepdims=True)
        dy = (res_w * gt_ref[...]) * dx
        dpost_ref[...] += jnp.sum(dy * fr, axis=0, keepdims=True)
        df_ref[...] = _rms_bwd(fv, r, dy * pg_ref[...]).astype(BF16)

    vec = pl.BlockSpec((1, d_dim), lambda i: (0, 0))
    row = pl.BlockSpec((tm, d_dim), lambda i: (i, 0))
    vshape = jax.ShapeDtypeStruct((1, d_dim), F32)
    return _pcall(
        body, out_shape=(jax.ShapeDtypeStruct((s_dim, d_dim), BF16), vshape, vshape), grid=(s_dim // tm,),
        in_specs=[row, row, vec, vec], out_specs=(row, vec, vec), name=name,
        compiler_params=_params(("arbitrary",), 3 * [((tm, d_dim), F32)], extra=6 * tm * d_dim * 4),
    )(dx_out, f, post_g, gate)


def _prenorm_bwd(dx_out, dhns, x, pre_g, scale, name):
    s_dim, d_dim = x.shape
    tm = TOKEN_TILE
    n_in = len(dhns)

    def body(*refs):
        dx_ref, x_ref, pg_ref, sc_ref = refs[n_in + 0], refs[n_in + 1], refs[n_in + 2], refs[n_in + 3]
        dxo_ref, dsh_ref, dsc_ref, dpg_ref = refs[n_in + 4:]

        @pl.when(pl.program_id(0) == 0)
        def _():
            dsh_ref[...] = jnp.zeros_like(dsh_ref)
            dsc_ref[...] = jnp.zeros_like(dsc_ref)
            dpg_ref[...] = jnp.zeros_like(dpg_ref)

        dhn = refs[0][...]
        for k in range(1, n_in):
            dhn = dhn + refs[k][...]
        xv = x_ref[...]
        r = _rstd(xv)
        xr = xv * r
        dsh_ref[...] += jnp.sum(dhn, axis=0, keepdims=True)
        dsc_ref[...] += jnp.sum(dhn * (xr * pg_ref[...]), axis=0, keepdims=True)
        dn = dhn * (1.0 + sc_ref[...])
        dpg_ref[...] += jnp.sum(dn * xr, axis=0, keepdims=True)
        dxo_ref[...] = dx_ref[...] + _rms_bwd(xv, r, dn * pg_ref[...])

    vec = pl.BlockSpec((1, d_dim), lambda i: (0, 0))
    row = pl.BlockSpec((tm, d_dim), lambda i: (i, 0))
    vshape = jax.ShapeDtypeStruct((1, d_dim), F32)
    return _pcall(
        body, out_shape=(jax.ShapeDtypeStruct((s_dim, d_dim), F32), vshape, vshape, vshape), grid=(s_dim // tm,),
        in_specs=n_in * [row] + [row, row, vec, vec], out_specs=(row, vec, vec, vec), name=name,
        compiler_params=_params(("arbitrary",), (n_in + 3) * [((tm, d_dim), F32)], extra=6 * tm * d_dim * 4),
    )(*dhns, dx_out, x, pre_g, scale)


def _ffn_dgu(df, wd, g, u, name, after=None):
    s_dim, d_dim = df.shape
    f_dim = wd.shape[0]
    tm, tn = TOKEN_TILE, f_dim // 2

    def body(df_ref, wd_ref, g_ref, u_ref, *rest):
        dg_ref, du_ref = rest[-2:]
        da = _dot_nt(df_ref[...], wd_ref[...])
        gv, uv = g_ref[...].astype(F32), u_ref[...].astype(F32)
        sg = jax.nn.sigmoid(gv)
        du_ref[...] = (da * (gv * sg)).astype(BF16)
        dg_ref[...] = (da * uv * (sg * (1.0 + gv * (1.0 - sg)))).astype(BF16)

    act = pl.BlockSpec((tm, tn), lambda i, j: (i, j))
    act_shape = jax.ShapeDtypeStruct((s_dim, f_dim), BF16)
    token = [] if after is None else [after]
    return _pcall(
        body, out_shape=(act_shape, act_shape), grid=(s_dim // tm, f_dim // tn),
        in_specs=[pl.BlockSpec((tm, d_dim), lambda i, j: (i, 0)), pl.BlockSpec((tn, d_dim), lambda i, j: (j, 0)), act, act]
        + len(token) * [_TOKEN_SPEC],
        out_specs=(act, act), name=name,
        compiler_params=_params(("parallel", "parallel"), [((tm, d_dim), BF16), ((tn, d_dim), BF16)] + 4 * [((tm, tn), BF16)],
                                extra=6 * tm * tn * 4),
    )(df, wd, g, u, *token)


def _ffn_dw(dg, du, a, hn, df, name):
    s_dim, f_dim = dg.shape
    d_dim = hn.shape[1]
    tm = 256

    def body(dg_ref, du_ref, a_ref, hn_ref, df_ref, dwg_ref, dwu_ref, dwd_ref):
        dwg_ref[...] = _dot_tn(dg_ref[...], hn_ref[...]).astype(BF16)
        dwu_ref[...] = _dot_tn(du_ref[...], hn_ref[...]).astype(BF16)
        dwd_ref[...] = _dot_tn(a_ref[...], df_ref[...]).astype(BF16)

    col = pl.BlockSpec((s_dim, tm), lambda i: (0, i))
    full = pl.BlockSpec((s_dim, d_dim), lambda i: (0, 0), pipeline_mode=pl.Buffered(1))
    out = pl.BlockSpec((tm, d_dim), lambda i: (i, 0))
    shape = jax.ShapeDtypeStruct((f_dim, d_dim), BF16)
    need = 2 * s_dim * d_dim * 2 + 2 * 3 * (s_dim * tm * 2 + tm * d_dim * 2) + 3 * tm * d_dim * 4 + 3 * s_dim * tm * 2
    return _pcall(
        body, out_shape=(shape, shape, shape), grid=(f_dim // tm,), in_specs=[col, col, col, full, full],
        out_specs=(out, out, out), name=name,
        compiler_params=pltpu.CompilerParams(dimension_semantics=("parallel",),
                                             vmem_limit_bytes=int(min(need + VMEM_RESERVE, V7X_VMEM_BYTES - VMEM_RESERVE))),
    )(dg, du, a, hn, df)


def _mm_tn_shared(lhs, b, name):
    k_dim, m_dim = lhs[0].shape
    n_dim = b.shape[1]
    tm = 256
    n = len(lhs)

    def body(*refs):
        rhs = refs[n][...]
        for j in range(n):
            refs[n + 1 + j][...] = _dot_tn(refs[j][...], rhs).astype(BF16)

    col = pl.BlockSpec((k_dim, tm), lambda i: (0, i))
    out = pl.BlockSpec((tm, n_dim), lambda i: (i, 0))
    shape = jax.ShapeDtypeStruct((m_dim, n_dim), BF16)
    need = k_dim * n_dim * 2 + 2 * n * (k_dim * tm * 2 + tm * n_dim * 2) + n * tm * n_dim * 4 + n * k_dim * tm * 2
    return _pcall(
        body, out_shape=tuple(n * [shape]), grid=(m_dim // tm,),
        in_specs=n * [col] + [pl.BlockSpec((k_dim, n_dim), lambda i: (0, 0), pipeline_mode=pl.Buffered(1))],
        out_specs=tuple(n * [out]), name=name,
        compiler_params=pltpu.CompilerParams(dimension_semantics=("parallel",),
                                             vmem_limit_bytes=int(min(need + VMEM_RESERVE, V7X_VMEM_BYTES - VMEM_RESERVE))),
    )(*lhs, b)


def _ffn_fwd_fused(x, pre_g, scale, shift, post_g, gate, wg_t, wu_t, wd, name):
    s_dim, d_dim = x.shape
    f_dim = wd.shape[0]
    tm, chunks = 256, FFN_CHUNKS
    cw = f_dim // chunks

    def body(x_ref, prg_ref, sc_ref, sh_ref, pg_ref, gt_ref, wg_ref, wu_ref, wd_ref, hn_ref, go_ref, uo_ref, a_ref, xo_ref, f_ref):
        xf = x_ref[...]
        hn = ((xf * _rstd(xf) * prg_ref[...]) * (1.0 + sc_ref[...]) + sh_ref[...]).astype(BF16)
        hn_ref[...] = hn
        f = None
        ahead = (_dot_nt(hn, wg_ref[0:cw, :]), _dot_nt(hn, wu_ref[0:cw, :]))
        for c in range(chunks):
            g, u = ahead
            if c + 1 < chunks:
                nxt = slice((c + 1) * cw, (c + 2) * cw)
                ahead = (_dot_nt(hn, wg_ref[nxt, :]), _dot_nt(hn, wu_ref[nxt, :]))
            cols = slice(c * cw, (c + 1) * cw)
            go_ref[:, cols] = g.astype(BF16)
            uo_ref[:, cols] = u.astype(BF16)
            a = (g * jax.nn.sigmoid(g) * u).astype(BF16)
            a_ref[:, cols] = a
            part = _dot_nn(a, wd_ref[cols, :])
            f = part if f is None else f + part
        f_ref[...] = f
        xo_ref[...] = xf + (FFN_RES * gt_ref[...]) * (f * _rstd(f) * pg_ref[...])

    vec = pl.BlockSpec((1, d_dim), lambda i: (0, 0))
    row = pl.BlockSpec((tm, d_dim), lambda i: (i, 0))
    act = pl.BlockSpec((tm, f_dim), lambda i: (i, 0))
    weight = pl.BlockSpec((f_dim, d_dim), lambda i: (0, 0), pipeline_mode=pl.Buffered(1))
    act_shape = jax.ShapeDtypeStruct((s_dim, f_dim), BF16)
    res_shape = jax.ShapeDtypeStruct((s_dim, d_dim), F32)
    need = (3 * f_dim * d_dim * 2 + 2 * tm * d_dim * 4 + 2 * (tm * d_dim * 2 + 3 * tm * f_dim * 2 + 2 * tm * d_dim * 4)
            + 8 * tm * cw * 4 + 4 * tm * d_dim * 4)
    return _pcall(
        body, out_shape=(jax.ShapeDtypeStruct((s_dim, d_dim), BF16), act_shape, act_shape, act_shape, res_shape, res_shape),
        grid=(s_dim // tm,), in_specs=[row, vec, vec, vec, vec, vec, weight, weight, weight],
        out_specs=(row, act, act, act, row, row), name=name,
        compiler_params=pltpu.CompilerParams(dimension_semantics=("parallel",),
                                             vmem_limit_bytes=int(min(need + VMEM_RESERVE, V7X_VMEM_BYTES - VMEM_RESERVE))),
    )(x, pre_g, scale, shift, post_g, gate, wg_t, wu_t, wd)


def _ffn_bwd_fused(dx_out, saved, pre_g, post_g, scale, gate, wg_t, wu_t, wd, name):
    x, _, g, u, _, f = saved
    s_dim, d_dim = x.shape
    f_dim = wd.shape[0]
    tm, chunks = 256, FFN_CHUNKS
    cw = f_dim // chunks

    def body(dx_ref, f_ref, g_ref, u_ref, x_ref, pg_ref, gt_ref, prg_ref, sc_ref, wd_ref, wg_ref, wu_ref,
             df_ref, dg_ref, du_ref, dxo_ref, dgate_ref, dpost_ref, dsh_ref, dsc_ref, dpg_ref):
        @pl.when(pl.program_id(0) == 0)
        def _():
            for acc in (dgate_ref, dpost_ref, dsh_ref, dsc_ref, dpg_ref):
                acc[...] = jnp.zeros_like(acc)

        dx, fv = dx_ref[...], f_ref[...]
        r = _rstd(fv)
        fr = fv * r
        dgate_ref[...] += FFN_RES * jnp.sum(dx * (fr * pg_ref[...]), axis=0, keepdims=True)
        dy = (FFN_RES * gt_ref[...]) * dx
        dpost_ref[...] += jnp.sum(dy * fr, axis=0, keepdims=True)
        df = _rms_bwd(fv, r, dy * pg_ref[...]).astype(BF16)
        df_ref[...] = df
        dhn = None
        ahead = _dot_nt(df, wd_ref[0:cw, :])
        for c in range(chunks):
            da = ahead
            if c + 1 < chunks:
                ahead = _dot_nt(df, wd_ref[(c + 1) * cw:(c + 2) * cw, :])
            cols = slice(c * cw, (c + 1) * cw)
            gv, uv = g_ref[:, cols].astype(F32), u_ref[:, cols].astype(F32)
            sg = jax.nn.sigmoid(gv)
            du = (da * (gv * sg)).astype(BF16)
            dg = (da * uv * (sg * (1.0 + gv * (1.0 - sg)))).astype(BF16)
            dg_ref[:, cols] = dg
            du_ref[:, cols] = du
            part = _dot_nn(dg, wg_ref[cols, :]) + _dot_nn(du, wu_ref[cols, :])
            dhn = part if dhn is None else dhn + part
        xv = x_ref[...]
        rx = _rstd(xv)
        xr = xv * rx
        dsh_ref[...] += jnp.sum(dhn, axis=0, keepdims=True)
        dsc_ref[...] += jnp.sum(dhn * (xr * prg_ref[...]), axis=0, keepdims=True)
        dn = dhn * (1.0 + sc_ref[...])
        dpg_ref[...] += jnp.sum(dn * xr, axis=0, keepdims=True)
        dxo_ref[...] = dx + _rms_bwd(xv, rx, dn * prg_ref[...])

    vec = pl.BlockSpec((1, d_dim), lambda i: (0, 0))
    row = pl.BlockSpec((tm, d_dim), lambda i: (i, 0))
    act = pl.BlockSpec((tm, f_dim), lambda i: (i, 0))
    weight = pl.BlockSpec((f_dim, d_dim), lambda i: (0, 0), pipeline_mode=pl.Buffered(1))
    vshape = jax.ShapeDtypeStruct((1, d_dim), F32)
    act_shape = jax.ShapeDtypeStruct((s_dim, f_dim), BF16)
    need = (3 * f_dim * d_dim * 2 + 2 * (3 * tm * d_dim * 4 + 2 * tm * f_dim * 2) + 2 * (tm * d_dim * 2 + 2 * tm * f_dim * 2 + tm * d_dim * 4)
            + 6 * tm * cw * 4 + 6 * tm * d_dim * 4)
    return _pcall(
        body, out_shape=(jax.ShapeDtypeStruct((s_dim, d_dim), BF16), act_shape, act_shape, jax.ShapeDtypeStruct((s_dim, d_dim), F32),
                         vshape, vshape, vshape, vshape, vshape),
        grid=(s_dim // tm,), in_specs=[row, row, act, act, row, vec, vec, vec, vec, weight, weight, weight],
        out_specs=(row, act, act, row, vec, vec, vec, vec, vec), name=name,
        compiler_params=pltpu.CompilerParams(dimension_semantics=("arbitrary",),
                                             vmem_limit_bytes=int(min(need + VMEM_RESERVE, V7X_VMEM_BYTES - VMEM_RESERVE))),
    )(dx_out, f, g, u, x, post_g, gate, pre_g, scale, wd, wg_t, wu_t)


def _rope_tables(zero=0.0):
    half = QK_ROPE // 2
    freqs = ROPE_THETA ** (-jnp.arange(half, dtype=F32) / half)
    ang = (jnp.arange(SEQ, dtype=F32)[:, None] + zero) * freqs[None, :]
    cos, sin = jnp.cos(ang), jnp.sin(ang)
    ones = jnp.ones((SEQ, QK_NOPE), F32)
    zeros = jnp.zeros((SEQ, QK_NOPE), F32)
    pad1 = jnp.ones((SEQ, HEAD_PAD - QK_NOPE - QK_ROPE), F32)
    pad0 = jnp.zeros((SEQ, HEAD_PAD - QK_NOPE - QK_ROPE), F32)
    zh = jnp.zeros((SEQ, half), F32)
    c = jnp.concatenate([ones, cos, cos, pad1], axis=1)
    s1 = jnp.concatenate([zeros, -sin, zh, pad0], axis=1)
    s2 = jnp.concatenate([zeros, zh, sin, pad0], axis=1)
    return c, s1, s2


def _rope(v, c, s1, s2):
    half = QK_ROPE // 2
    return v * c + pltpu.roll(v, HEAD_PAD - half, 1) * s1 + pltpu.roll(v, half, 1) * s2


def _rope_t(dv, c, s1, s2):
    half = QK_ROPE // 2
    return dv * c + pltpu.roll(dv * s1, half, 1) + pltpu.roll(dv * s2, HEAD_PAD - half, 1)


def _mla_qkv(lat, q_norm, kv_norm, wq_t, wkv_t, rope, name):
    s_dim = lat.shape[0]
    width = MLA_HEADS * HEAD_PAD
    tm = 256

    def body(lat_ref, qg_ref, kg_ref, wq_ref, wkv_ref, c_ref, s1_ref, s2_ref, q_ref, k_ref, v_ref, qn_ref, kvn_ref):
        cq = lat_ref[:, :Q_LORA]
        ckv = lat_ref[:, Q_LORA:Q_LORA + KV_LORA]
        kr = lat_ref[:, Q_LORA + KV_LORA:]
        c, s1, s2 = c_ref[...], s1_ref[...], s2_ref[...]
        qn = (cq * _rstd(cq) * qg_ref[...]).astype(BF16)
        kvn = (ckv * _rstd(ckv) * kg_ref[...]).astype(BF16)
        qn_ref[...] = qn
        kvn_ref[...] = kvn
        q = _dot_nt(qn, wq_ref[...])
        kv = _dot_nt(kvn, wkv_ref[...])
        krr = _rope(kr, c, s1, s2)
        low = lax.broadcasted_iota(jnp.int32, (tm, HEAD_PAD), 1) < QK_NOPE
        for h in range(MLA_HEADS):
            sl = slice(h * HEAD_PAD, (h + 1) * HEAD_PAD)
            q_ref[:, sl] = _rope(q[:, sl], c, s1, s2).astype(BF16)
            kvh = kv[:, sl]
            k_ref[:, sl] = (jnp.where(low, kvh, 0.0) + krr).astype(BF16)
            v_ref[:, sl] = jnp.where(low, 0.0, kvh).astype(BF16)

    row = lambda n: pl.BlockSpec((tm, n), lambda i: (i, 0))
    full = lambda a: pl.BlockSpec(a.shape, lambda i: (0, 0))
    wide = jax.ShapeDtypeStruct((s_dim, width), BF16)
    return _pcall(
        body,
        out_shape=(wide, wide, wide, jax.ShapeDtypeStruct((s_dim, Q_LORA), BF16), jax.ShapeDtypeStruct((s_dim, KV_LORA), BF16)),
        grid=(s_dim // tm,),
        in_specs=[row(LAT_PAD), full(q_norm), full(kv_norm), full(wq_t), full(wkv_t), row(HEAD_PAD), row(HEAD_PAD), row(HEAD_PAD)],
        out_specs=(row(width), row(width), row(width), row(Q_LORA), row(KV_LORA)), name=name,
        compiler_params=_params(("parallel",), [((tm, LAT_PAD), F32), (wq_t.shape, BF16), (wkv_t.shape, BF16)]
                                + 3 * [((tm, width), BF16)], extra=4 * tm * width * 4),
    )(lat, q_norm, kv_norm, wq_t, wkv_t, *rope)


def _mla_scores(q, k_ref, t, tq):
    lo = t * tq
    own = slice(lo, lo + tq)
    scores = [(_dot_nt(q, k_ref[own, :]), own)]
    if t > 0:
        scores.append((_dot_nt(q, k_ref[0:lo, :]), slice(0, lo)))
    return scores


def _mla_softmax(scores):
    s_own = scores[0][0] * MLA_SCALE
    rows = lax.broadcasted_iota(jnp.int32, s_own.shape, 0)
    cols = lax.broadcasted_iota(jnp.int32, s_own.shape, 1)
    s_own = jnp.where(cols <= rows, s_own, -jnp.inf)
    mx = jnp.max(s_own, axis=-1, keepdims=True)
    if len(scores) == 1:
        e_own = jnp.exp(s_own - mx)
        return [(e_own * (1.0 / jnp.sum(e_own, axis=-1, keepdims=True)), scores[0][1])]
    s_pre = scores[1][0] * MLA_SCALE
    mx = jnp.maximum(mx, jnp.max(s_pre, axis=-1, keepdims=True))
    e_own, e_pre = jnp.exp(s_own - mx), jnp.exp(s_pre - mx)
    inv = 1.0 / (jnp.sum(e_own, axis=-1, keepdims=True) + jnp.sum(e_pre, axis=-1, keepdims=True))
    return [(e_pre * inv, scores[1][1]), (e_own * inv, scores[0][1])]


def _mla_attn_fwd(q, k, v, name):
    s_dim = q.shape[0]
    tq = MLA_QUERY_TILE

    def body(q_ref, k_ref, v_ref, o_ref):
        n_tiles = s_dim // tq
        tile_of = lambda t: slice(t * tq, (t + 1) * tq)
        def weighted_values(t, probs):
            o = None
            for p, keys in probs:
                part = _dot_nn(p, v_ref[keys, :])
                o = part if o is None else o + part
            o_ref[tile_of(t), :] = o.astype(BF16)

        scores = _mla_scores(q_ref[tile_of(0), :], k_ref, 0, tq)
        probs = None
        for t in range(n_tiles):
            ahead = _mla_scores(q_ref[tile_of(t + 1), :], k_ref, t + 1, tq) if t + 1 < n_tiles else None
            if probs is not None:
                weighted_values(t - 1, probs)
            probs = [(p.astype(BF16), keys) for p, keys in _mla_softmax(scores)]
            scores = ahead
        weighted_values(n_tiles - 1, probs)

    head = pl.BlockSpec((s_dim, HEAD_PAD), lambda h: (0, h))
    return _pcall(
        body, out_shape=jax.ShapeDtypeStruct(q.shape, BF16), grid=(MLA_HEADS,),
        in_specs=[head, head, head], out_specs=head, name=name,
        compiler_params=_params(("parallel",), 4 * [((s_dim, HEAD_PAD), BF16)], extra=4 * tq * s_dim * 4),
    )(q, k, v)


def _mla_attn_bwd(q, k, v, d_o, name):
    s_dim = q.shape[0]
    tq = MLA_QUERY_TILE

    def body(q_ref, k_ref, v_ref, do_ref, dq_ref, dk_ref, dv_ref):
        dk_ref[...] = jnp.zeros_like(dk_ref)
        dv_ref[...] = jnp.zeros_like(dv_ref)
        n_tiles = s_dim // tq
        tile_of = lambda t: slice(t * tq, (t + 1) * tq)

        def products(t):
            scores = _mla_scores(q_ref[tile_of(t), :], k_ref, t, tq)
            dot = do_ref[tile_of(t), :].astype(BF16)
            return scores, [_dot_nt(dot, v_ref[keys, :]) for _, keys in scores]

        def gradients_of_scores(scores, dps):
            probs = _mla_softmax(scores)
            dp_of = {(keys.start, keys.stop): dp for (_, keys), dp in zip(scores, dps)}
            terms = [(p, keys, dp_of[keys.start, keys.stop]) for p, keys in probs]
            row = None
            for p, _, dp in terms:
                part = jnp.sum(p * dp, axis=-1, keepdims=True)
                row = part if row is None else row + part
            return [((p * (dp - row) * MLA_SCALE).astype(BF16), p.astype(BF16), keys) for p, keys, dp in terms]

        def accumulate(t, terms):
            qt = q_ref[tile_of(t), :]
            dot = do_ref[tile_of(t), :].astype(BF16)
            dq = None
            for dsb, pb, keys in terms:
                part = _dot_nn(dsb, k_ref[keys, :])
                dq = part if dq is None else dq + part
                dk_ref[keys, :] += _dot_tn(dsb, qt)
                dv_ref[keys, :] += _dot_tn(pb, dot)
            dq_ref[tile_of(t), :] = dq

        ready = products(0)
        terms = None
        for t in range(n_tiles):
            ahead = products(t + 1) if t + 1 < n_tiles else None
            if terms is not None:
                accumulate(t - 1, terms)
            terms = gradients_of_scores(*ready)
            ready = ahead
        accumulate(n_tiles - 1, terms)

    head = pl.BlockSpec((s_dim, HEAD_PAD), lambda h: (0, h))
    out = jax.ShapeDtypeStruct(q.shape, F32)
    return _pcall(
        body, out_shape=(out, out, out), grid=(MLA_HEADS,),
        in_specs=[head, head, head, head], out_specs=(head, head, head), name=name,
        compiler_params=_params(("parallel",), 3 * [((s_dim, HEAD_PAD), BF16)] + 4 * [((s_dim, HEAD_PAD), F32)],
                                extra=6 * tq * s_dim * 4),
    )(q, k, v, d_o)


def _mla_qkv_bwd(dq, dk, dv, lat, q_norm, kv_norm, wq_t, wkv_t, rope, name):
    s_dim = lat.shape[0]
    width = MLA_HEADS * HEAD_PAD
    tm = 256

    def body(dq_ref, dk_ref, dv_ref, lat_ref, qg_ref, kg_ref, wq_ref, wkv_ref, c_ref, s1_ref, s2_ref,
             dqp_ref, dkv_ref, dlat_ref, dqg_ref, dkg_ref):
        @pl.when(pl.program_id(0) == 0)
        def _():
            dqg_ref[...] = jnp.zeros_like(dqg_ref)
            dkg_ref[...] = jnp.zeros_like(dkg_ref)

        c, s1, s2 = c_ref[...], s1_ref[...], s2_ref[...]
        lane = lax.broadcasted_iota(jnp.int32, (tm, HEAD_PAD), 1)
        low = lane < QK_NOPE
        rot = (lane >= QK_NOPE) & (lane < QK_NOPE + QK_ROPE)
        dkrr = jnp.zeros((tm, HEAD_PAD), F32)
        for h in range(MLA_HEADS):
            sl = slice(h * HEAD_PAD, (h + 1) * HEAD_PAD)
            dqp_ref[:, sl] = _rope_t(dq_ref[:, sl], c, s1, s2).astype(BF16)
            dkh = dk_ref[:, sl]
            dkv_ref[:, sl] = jnp.where(low, dkh, dv_ref[:, sl]).astype(BF16)
            dkrr = dkrr + jnp.where(rot, dkh, 0.0)
        dqn = _dot_nn(dqp_ref[...], wq_ref[...])
        dkvn = _dot_nn(dkv_ref[...], wkv_ref[...])
        cq = lat_ref[:, :Q_LORA]
        ckv = lat_ref[:, Q_LORA:Q_LORA + KV_LORA]
        rq, rkv = _rstd(cq), _rstd(ckv)
        dqg_ref[...] += jnp.sum(dqn * cq * rq, axis=0, keepdims=True)
        dkg_ref[...] += jnp.sum(dkvn * ckv * rkv, axis=0, keepdims=True)
        dlat_ref[:, :Q_LORA] = _rms_bwd(cq, rq, dqn * qg_ref[...])
        dlat_ref[:, Q_LORA:Q_LORA + KV_LORA] = _rms_bwd(ckv, rkv, dkvn * kg_ref[...])
        dlat_ref[:, Q_LORA + KV_LORA:] = _rope_t(dkrr, c, s1, s2)

    row = lambda n: pl.BlockSpec((tm, n), lambda i: (i, 0))
    full = lambda a: pl.BlockSpec(a.shape, lambda i: (0, 0))
    wide = jax.ShapeDtypeStruct((s_dim, width), BF16)
    return _pcall(
        body,
        out_shape=(wide, wide, jax.ShapeDtypeStruct((s_dim, LAT_PAD), F32),
                   jax.ShapeDtypeStruct(q_norm.shape, F32), jax.ShapeDtypeStruct(kv_norm.shape, F32)),
        grid=(s_dim // tm,),
        in_specs=[row(width), row(width), row(width), row(LAT_PAD), full(q_norm), full(kv_norm), full(wq_t), full(wkv_t),
                  row(HEAD_PAD), row(HEAD_PAD), row(HEAD_PAD)],
        out_specs=(row(width), row(width), row(LAT_PAD), full(q_norm), full(kv_norm)), name=name,
        compiler_params=_params(("arbitrary",), 3 * [((tm, width), F32)] + [((tm, LAT_PAD), F32), (wq_t.shape, BF16),
                                                                           (wkv_t.shape, BF16)] + 2 * [((tm, width), BF16)],
                                extra=2 * tm * width * 4),
    )(dq, dk, dv, lat, q_norm, kv_norm, wq_t, wkv_t, *rope)


def _t5_bucket(dist):
    max_exact = N_BUCKETS // 2
    d = jnp.maximum(dist, 1).astype(F32)
    large = max_exact + (jnp.log(d / max_exact) / math.log(MAX_DISTANCE / max_exact)
                         * (N_BUCKETS - max_exact)).astype(jnp.int32)
    large = jnp.minimum(large, N_BUCKETS - 1)
    return jnp.where(dist < max_exact, dist, large)


def _dil_buckets(dilation):
    iq = jnp.arange(DIL_BLOCK)[:, None]
    ik = jnp.arange(2 * DIL_BLOCK)[None, :]
    return _t5_bucket(jnp.maximum(DIL_BLOCK + iq - ik, 0) * dilation)


def _dil_logits(qh, kb, bias_h, first, span):
    if first:
        s = _dot_nt(qh, kb) * DIL_SCALE + bias_h[:, DIL_BLOCK:]
        rel = lax.broadcasted_iota(jnp.int32, s.shape, 0) - lax.broadcasted_iota(jnp.int32, s.shape, 1)
    else:
        s = _dot_nt(qh, kb) * DIL_SCALE + bias_h
        rel = DIL_BLOCK + lax.broadcasted_iota(jnp.int32, s.shape, 0) - lax.broadcasted_iota(jnp.int32, s.shape, 1)
    return jnp.where((rel >= 0) & (rel <= span), s, -jnp.inf)


def _dil_blocks(s_dim, dilation):
    rows = s_dim // dilation
    for r in range(dilation):
        for n in range(rows // DIL_BLOCK):
            lo = r * rows + n * DIL_BLOCK
            keys = slice(lo, lo + DIL_BLOCK) if n == 0 else slice(lo - DIL_BLOCK, lo + DIL_BLOCK)
            start = r + n * DIL_BLOCK * dilation
            tokens = slice(start, start + DIL_BLOCK) if dilation == 1 else pl.ds(start, DIL_BLOCK, stride=dilation)
            yield n == 0, slice(lo, lo + DIL_BLOCK), keys, tokens


def _dil_views(s_dim):
    col = lambda which: pl.BlockSpec((s_dim, HEAD_PAD), lambda p: (0, which * DIL_PAIRS + p))
    nat = pl.BlockSpec((s_dim, HEAD_PAD), lambda p: (0, p))
    bias = pl.BlockSpec((2, DIL_BLOCK, 2 * DIL_BLOCK), lambda p: (p, 0, 0))
    return col, nat, bias


def _dil_attn_fwd(qkv, bias, dilation, span, name):
    s_dim = qkv.shape[0]
    d_dim = DIL_HEADS * DIL_HEAD_DIM
    col, nat, bias_spec = _dil_views(s_dim)

    def body(q_ref, k_ref, v_ref, b_ref, o_ref, l_ref):
        lane = lax.broadcasted_iota(jnp.int32, (DIL_BLOCK, HEAD_PAD), 1)
        klane = lax.broadcasted_iota(jnp.int32, (2 * DIL_BLOCK, HEAD_PAD), 1)
        blocks = list(_dil_blocks(s_dim, dilation))
        for g0 in range(0, len(blocks), DIL_GROUPED):
            group = blocks[g0:g0 + DIL_GROUPED]
            logits = [_dil_logits(jnp.where((lane < DIL_HEAD_DIM) == (h == 0), q_ref[blk, :], 0), k_ref[keys, :], b_ref[h],
                                  first, span) for first, blk, keys, _ in group for h in range(2)]
            soft = []
            for lg in logits:
                mx = jnp.max(lg, axis=-1, keepdims=True)
                e = jnp.exp(lg - mx)
                tot = jnp.sum(e, axis=-1, keepdims=True)
                soft.append(((e * (1.0 / tot)).astype(BF16), mx + jnp.log(tot)))
            for i, (_, _, keys, tokens) in enumerate(group):
                vb = v_ref[keys, :]
                o_acc = jnp.zeros((DIL_BLOCK, HEAD_PAD), F32)
                lse_acc = jnp.zeros((DIL_BLOCK, HEAD_PAD), F32)
                for h in range(2):
                    p, lse = soft[2 * i + h]
                    kmine = (klane[:vb.shape[0]] < DIL_HEAD_DIM) == (h == 0)
                    o_acc = o_acc + _dot_nn(p, jnp.where(kmine, vb, 0))
                    lse_acc = jnp.where((lane < DIL_HEAD_DIM) == (h == 0), lse, lse_acc)
                o_ref[tokens, :] = o_acc
                l_ref[tokens, :] = lse_acc

    out = jax.ShapeDtypeStruct((s_dim, d_dim), F32)
    return _pcall(
        body, out_shape=(out, out), grid=(DIL_PAIRS,),
        in_specs=[col(0), col(1), col(2), bias_spec], out_specs=(nat, nat), name=name,
        compiler_params=_params(("parallel",), 3 * [((s_dim, HEAD_PAD), BF16)] + 2 * [((s_dim, HEAD_PAD), F32)]
                                + [((2, DIL_BLOCK, 2 * DIL_BLOCK), F32)], extra=2**21),
    )(qkv, qkv, qkv, bias)


def _dil_mix(lses, outs, name):
    s_dim, d_dim = outs[0].shape
    tm = TOKEN_TILE
    ng = len(outs)

    def body(*refs):
        ls = [refs[g][...] for g in range(ng)]
        mx = ls[0]
        for g in range(1, ng):
            mx = jnp.maximum(mx, ls[g])
        es = [jnp.exp(l - mx) for l in ls]
        tot = es[0]
        for g in range(1, ng):
            tot = tot + es[g]
        o = None
        for g in range(ng):
            al = es[g] / tot
            refs[2 * ng + g][...] = al
            t = al * refs[ng + g][...]
            o = t if o is None else o + t
        refs[3 * ng][...] = o
        refs[3 * ng + 1][...] = o.astype(BF16)

    row = pl.BlockSpec((tm, d_dim), lambda i: (i, 0))
    f = jax.ShapeDtypeStruct((s_dim, d_dim), F32)
    res = _pcall(
        body, out_shape=tuple(ng * [f] + [f, jax.ShapeDtypeStruct((s_dim, d_dim), BF16)]), grid=(s_dim // tm,),
        in_specs=2 * ng * [row], out_specs=tuple((ng + 2) * [row]), name=name,
        compiler_params=_params(("parallel",), (3 * ng + 2) * [((tm, d_dim), F32)], extra=4 * tm * d_dim * 4),
    )(*lses, *outs)
    return res[:ng], res[ng], res[ng + 1]


def _dil_attn_bwd(qkv, bias, d_o, o_mix, alpha, lse, dilation, span, name):
    s_dim = qkv.shape[0]
    d_dim = DIL_HEADS * DIL_HEAD_DIM
    col, nat, bias_spec = _dil_views(s_dim)

    def body(q_ref, k_ref, v_ref, b_ref, do_ref, om_ref, al_ref, l_ref, dq_ref, dk_ref, dv_ref, db_ref, dk_acc, dv_acc):
        db_ref[...] = jnp.zeros_like(db_ref)
        dk_acc[...] = jnp.zeros_like(dk_acc)
        dv_acc[...] = jnp.zeros_like(dv_acc)
        lane = lax.broadcasted_iota(jnp.int32, (DIL_BLOCK, HEAD_PAD), 1)
        klane = lax.broadcasted_iota(jnp.int32, (2 * DIL_BLOCK, HEAD_PAD), 1)
        blocks = list(_dil_blocks(s_dim, dilation))
        heads = [(lane < DIL_HEAD_DIM) == (h == 0) for h in range(2)]
        for g0 in range(0, len(blocks), DIL_GROUPED):
            group = blocks[g0:g0 + DIL_GROUPED]
            staged = []
            for first, blk, kv_rows, tokens in group:
                qb, kb, vb = q_ref[blk, :], k_ref[kv_rows, :], v_ref[kv_rows, :]
                dog = al_ref[tokens, :] * do_ref[tokens, :]
                row_term = dog * om_ref[tokens, :]
                lse_b = l_ref[tokens, :]
                for h in range(2):
                    qh = jnp.where(heads[h], qb, 0)
                    dogh = jnp.where(heads[h], dog, 0.0).astype(BF16)
                    staged.append((_dil_logits(qh, kb, b_ref[h], first, span), _dot_nt(dogh, vb), qh, dogh,
                                   jnp.max(jnp.where(heads[h], lse_b, -jnp.inf), axis=-1, keepdims=True),
                                   jnp.sum(jnp.where(heads[h], row_term, 0.0), axis=-1, keepdims=True)))
            grads = []
            for i, (logits, dp, qh, dogh, lse_h, row) in enumerate(staged):
                p = jnp.exp(logits - lse_h)
                ds = p * (dp - row)
                if group[i // 2][0]:
                    db_ref[i % 2, :, DIL_BLOCK:] += ds
                else:
                    db_ref[i % 2] += ds
                grads.append(((ds * DIL_SCALE).astype(BF16), p.astype(BF16), qh, dogh))
            for i, (_, blk, kv_rows, _) in enumerate(group):
                kb = k_ref[kv_rows, :]
                dq_acc = jnp.zeros((DIL_BLOCK, HEAD_PAD), F32)
                dk_blk = jnp.zeros((kb.shape[0], HEAD_PAD), F32)
                dv_blk = jnp.zeros((kb.shape[0], HEAD_PAD), F32)
                for h in range(2):
                    dsb, pb, qh, dogh = grads[2 * i + h]
                    kmine = (klane[:kb.shape[0]] < DIL_HEAD_DIM) == (h == 0)
                    dq_acc = dq_acc + _dot_nn(dsb, jnp.where(kmine, kb, 0))
                    dk_blk = dk_blk + _dot_tn(dsb, qh)
                    dv_blk = dv_blk + _dot_tn(pb, dogh)
                dq_ref[blk, :] = dq_acc.astype(BF16)
                dk_acc[kv_rows, :] += dk_blk
                dv_acc[kv_rows, :] += dv_blk
        dk_ref[...] = dk_acc[...].astype(BF16)
        dv_ref[...] = dv_acc[...].astype(BF16)

    grad = jax.ShapeDtypeStruct((s_dim, d_dim), BF16)
    return _pcall(
        body, out_shape=(grad, grad, grad, jax.ShapeDtypeStruct(bias.shape, F32)), grid=(DIL_PAIRS,),
        in_specs=[col(0), col(1), col(2), bias_spec, nat, nat, nat, nat],
        out_specs=(nat, nat, nat, bias_spec), name=name,
        scratch_shapes=[pltpu.VMEM((s_dim, HEAD_PAD), F32), pltpu.VMEM((s_dim, HEAD_PAD), F32)],
        compiler_params=_params(("parallel",), 6 * [((s_dim, HEAD_PAD), BF16)] + 4 * [((s_dim, HEAD_PAD), F32)]
                                + 2 * [((2, DIL_BLOCK, 2 * DIL_BLOCK), F32)], extra=2 * s_dim * HEAD_PAD * 4 + 2**21),
    )(qkv, qkv, qkv, bias, d_o, o_mix, alpha, lse)


def _bias_reduce(dbias, buckets, name):
    n_heads = dbias.shape[0]

    def body(db_ref, bk_ref, o_ref):
        ds, bk = db_ref[0], bk_ref[0]
        lane = lax.broadcasted_iota(jnp.int32, (8, HEAD_PAD), 1)
        acc = jnp.zeros((8, HEAD_PAD), F32)
        for b in range(N_BUCKETS):
            acc = jnp.where(lane == b, jnp.sum(jnp.where(bk == b, ds, 0.0)), acc)
        o_ref[0] = acc

    blk = (1, DIL_BLOCK, 2 * DIL_BLOCK)
    return _pcall(
        body, out_shape=jax.ShapeDtypeStruct((n_heads, 8, HEAD_PAD), F32), grid=(n_heads,),
        in_specs=[pl.BlockSpec(blk, lambda h: (h, 0, 0)), pl.BlockSpec(blk, lambda h: (h // DIL_HEADS, 0, 0))],
        out_specs=pl.BlockSpec((1, 8, HEAD_PAD), lambda h: (h, 0, 0)), name=name,
        compiler_params=_params(("parallel",), [(blk, F32), (blk, jnp.int32)], extra=2**20),
    )(dbias, buckets)


def _loss_grad(y, target, name):
    s_dim, d_dim = y.shape
    tm = TOKEN_TILE

    def body(y_ref, t_ref, dy_ref, l_ref):
        @pl.when(pl.program_id(0) == 0)
        def _():
            l_ref[...] = jnp.zeros_like(l_ref)

        err = y_ref[...] - t_ref[...]
        dy_ref[...] = err / d_dim
        sq = (err * err).reshape(tm // 8, 8, d_dim)
        l_ref[...] += 0.5 * jnp.sum(sq, axis=0) / d_dim

    row = pl.BlockSpec((tm, d_dim), lambda i: (i, 0))
    acc = pl.BlockSpec((8, d_dim), lambda i: (0, 0))
    return _pcall(
        body, out_shape=(jax.ShapeDtypeStruct((s_dim, d_dim), F32), jax.ShapeDtypeStruct((8, d_dim), F32)),
        grid=(s_dim // tm,), in_specs=[row, row], out_specs=(row, acc), name=name,
        compiler_params=_params(("arbitrary",), 3 * [((tm, d_dim), F32)], extra=2 * tm * d_dim * 4),
    )(y, target)


def _mod_fwd(c_all, w_mod, b_loc, name):
    depth, d_dim, n = w_mod.shape
    nb = c_all.shape[0]

    def body(c_ref, w_ref, b_ref, o_ref, s_ref):
        cv = c_ref[...]
        sc = cv * jax.nn.sigmoid(cv)
        s_ref[...] = sc
        o_ref[0] = _dot_nn(sc.astype(BF16), w_ref[0].astype(BF16)) + b_ref[0]

    return _pcall(
        body, out_shape=(jax.ShapeDtypeStruct((depth, nb, n), F32), jax.ShapeDtypeStruct((nb, d_dim), F32)), grid=(depth,),
        in_specs=[pl.BlockSpec((nb, d_dim), lambda i: (0, 0)), pl.BlockSpec((1, d_dim, n), lambda i: (i, 0, 0)),
                  pl.BlockSpec((1, 1, n), lambda i: (i, 0, 0))],
        out_specs=(pl.BlockSpec((1, nb, n), lambda i: (i, 0, 0)), pl.BlockSpec((nb, d_dim), lambda i: (0, 0))), name=name,
        compiler_params=_params(("arbitrary",), [((1, d_dim, n), F32)], extra=d_dim * n * 2 + 2**20),
    )(c_all, w_mod, b_loc.reshape(depth, 1, n))


def _sum_parts(parts, name, transpose=False):
    _, rows, cols = parts.shape
    unit = 128 if transpose else 16
    budget = (7 if transpose else 3) * 2**20
    fits = [t for t in range(unit, rows // 2 + 1, unit) if rows % t == 0 and NDEV * t * cols * parts.dtype.itemsize <= budget]
    tr = max(fits) if fits else rows

    def body(p_ref, o_ref):
        acc = p_ref[0].astype(F32)
        for k in range(1, NDEV):
            acc = acc + p_ref[k].astype(F32)
        o_ref[...] = acc.T if transpose else acc

    out_shape, out_block = ((cols, rows), (cols, tr)) if transpose else ((rows, cols), (tr, cols))
    return _pcall(
        body, out_shape=jax.ShapeDtypeStruct(out_shape, F32), grid=(rows // tr,),
        in_specs=[pl.BlockSpec((NDEV, tr, cols), lambda i: (0, i, 0))],
        out_specs=pl.BlockSpec(out_block, (lambda i: (0, i)) if transpose else (lambda i: (i, 0))),
        name=name, compiler_params=_params(("parallel",), [((NDEV, tr, cols), parts.dtype), (out_block, F32)], extra=2**22),
    )(parts)


def _adamw(w, g, m, v, name):
    shape = w.shape
    cols = shape[-1]
    rows = math.prod(shape[:-1])
    tr = rows
    for cand in (2048, 1024, 512, 256, 128, 64, 32, 16, 8):
        if rows % cand == 0 and rows > cand and cand * cols * 4 <= 2**21:
            tr = cand
            break

    def body(w_ref, g_ref, m_ref, v_ref, d_ref, mo_ref, vo_ref):
        gv = g_ref[...]
        mn = ADAM_B1 * m_ref[...] + (1.0 - ADAM_B1) * gv
        vn = ADAM_B2 * v_ref[...] + (1.0 - ADAM_B2) * (gv * gv)
        m_hat = mn / (1.0 - ADAM_B1 ** ADAM_STEP)
        v_hat = vn / (1.0 - ADAM_B2 ** ADAM_STEP)
        d_ref[...] = -ADAM_LR * (m_hat / (jnp.sqrt(v_hat) + ADAM_EPS) + ADAM_WD * w_ref[...])
        mo_ref[...] = mn
        vo_ref[...] = vn

    blk = pl.BlockSpec((tr, cols), lambda i: (i, 0))
    out = jax.ShapeDtypeStruct((rows, cols), F32)
    res = _pcall(
        body, out_shape=(out, out, out), grid=(rows // tr,), in_specs=4 * [blk], out_specs=(blk, blk, blk), name=name,
        compiler_params=_params(("parallel",), 7 * [((tr, cols), F32)], extra=4 * tr * cols * 4),
    )(*(a.reshape(rows, cols) for a in (w, g, m, v)))
    return tuple(r.reshape(shape) for r in res)


def _peers():
    x, y, c = lax.axis_index("x"), lax.axis_index("y"), lax.axis_index("c")
    flip = lambda v, f: 1 - v if f else v
    peers = []
    for f in range(1, NDEV):
        px, py, pc = flip(x, f & 4), flip(y, f & 2), flip(c, f & 1)
        peers.append(((px, py, pc), 4 * px + 2 * py + pc))
    return (x, y, c), 4 * x + 2 * y + c, peers


def _places():
    x, y, c = lax.axis_index("x"), lax.axis_index("y"), lax.axis_index("c")
    place = lambda px, py, pc: ((px, py, pc), 4 * px + 2 * py + pc)
    return place(x, y, c), place(x, y, 1 - c), [place(1 - x, y, c), place(x, 1 - y, c), place(1 - x, 1 - y, c)]


def _exchange(arrs, gather, name):
    n = len(arrs)
    hbm = pl.BlockSpec(memory_space=pltpu.HBM)
    if gather:
        out_shape = [jax.ShapeDtypeStruct((NDEV * a.shape[0], a.shape[1]), a.dtype) for a in arrs]
    else:
        out_shape = [jax.ShapeDtypeStruct((NDEV, a.shape[0] // NDEV, a.shape[1]), a.dtype) for a in arrs]

    def body(*refs):
        ins, outs = refs[:n], refs[n:2 * n]
        send_sems, recv_sems, local_sems = refs[2 * n:]
        me_pos, me, peers = _peers()
        local = []
        for k in range(n):
            rows = arrs[k].shape[0] if gather else arrs[k].shape[0] // NDEV
            if gather:
                src_of = lambda idx: ins[k]
                dst_of = lambda idx: outs[k].at[pl.ds(me * rows, rows)]
                mine = (ins[k], outs[k].at[pl.ds(me * rows, rows)])
            else:
                src_of = lambda idx: ins[k].at[pl.ds(idx * rows, rows)]
                dst_of = lambda idx: outs[k].at[me]
                mine = (ins[k].at[pl.ds(me * rows, rows)], outs[k].at[me])
            cp = pltpu.make_async_copy(mine[0], mine[1], local_sems.at[k])
            cp.start()
            local.append(cp)
            for pos, idx in peers:
                pltpu.make_async_remote_copy(src_ref=src_of(idx), dst_ref=dst_of(idx), send_sem=send_sems.at[k],
                                             recv_sem=recv_sems.at[k], device_id=pos, device_id_type=MESH).start()
        for k in range(n):
            rows = arrs[k].shape[0] if gather else arrs[k].shape[0] // NDEV
            sent = ins[k].at[pl.ds(0, (NDEV - 1) * rows)] if not gather else outs[k].at[pl.ds(0, (NDEV - 1) * rows)]
            got = outs[k].at[pl.ds(0, (NDEV - 1) * rows)] if gather else outs[k].at[pl.ds(0, NDEV - 1)]
            pltpu.make_async_remote_copy(src_ref=sent, dst_ref=sent, send_sem=send_sems.at[k], recv_sem=recv_sems.at[k],
                                         device_id=me_pos, device_id_type=MESH).wait_send()
            pltpu.make_async_remote_copy(src_ref=got, dst_ref=got, send_sem=send_sems.at[k], recv_sem=recv_sems.at[k],
                                         device_id=me_pos, device_id_type=MESH).wait_recv()
            local[k].wait()

    return pl.pallas_call(
        body, out_shape=out_shape, in_specs=n * [hbm], out_specs=n * [hbm], name=name,
        scratch_shapes=[pltpu.SemaphoreType.DMA((n,)), pltpu.SemaphoreType.DMA((n,)), pltpu.SemaphoreType.DMA((n,))],
        compiler_params=pltpu.CompilerParams(has_side_effects=True),
    )(*arrs)


_HBM = pl.BlockSpec(memory_space=pltpu.HBM)
_SEM = pl.BlockSpec(memory_space=pltpu.SEMAPHORE)
_DATAFLOW = pltpu.SideEffectType.DATAFLOW_SIDE_EFFECTING


def _split_start(srcs, groups, gather, name, after=None):
    n = len(srcs)
    if gather:
        lands = [lax.empty((NDEV * a.shape[0], a.shape[1]), a.dtype) for a in srcs]
    else:
        lands = [lax.empty((NDEV, a.shape[0] // NDEV, a.shape[1]), a.dtype) for a in srcs]
    n_sem = 3 * len(groups)
    extra = [] if after is None else [after]
    n_in = 2 * n + len(extra)

    def body(*refs):
        src_refs, land_refs = refs[:n], refs[n:2 * n]
        sems = refs[n_in:n_in + n_sem]
        token = refs[-1]
        (_, my), sibling, chips = _places()
        _, _, peers = _peers()
        targets = [sibling] + chips if gather else peers
        for g, members in enumerate(groups):
            for j, k in enumerate(members):
                _own_copy(src_refs[k], land_refs[k], sems[3 * g + 2].at[j], my, gather).start()
        for g, members in enumerate(groups):
            for j, k in enumerate(members):
                rows = srcs[k].shape[0] if gather else srcs[k].shape[0] // NDEV
                for pos, idx in targets:
                    src = src_refs[k] if gather else src_refs[k].at[pl.ds(idx * rows, rows)]
                    dst = land_refs[k].at[pl.ds(my * rows, rows)] if gather else land_refs[k].at[my]
                    pltpu.make_async_remote_copy(src_ref=src, dst_ref=dst, send_sem=sems[3 * g].at[j],
                                                 recv_sem=sems[3 * g + 1].at[j], device_id=pos, device_id_type=MESH).start()
        token[...] = jnp.zeros_like(token)

    out_shape = []
    for members in groups:
        out_shape += 3 * [pltpu.SemaphoreType.DMA((len(members),))]
    out_shape += [pltpu.HBM(a.shape, a.dtype) for a in srcs] + [pltpu.HBM(a.shape, a.dtype) for a in lands]
    out_shape.append(jax.ShapeDtypeStruct((8, 128), F32))
    res = pl.pallas_call(
        body, name=name, out_shape=tuple(out_shape), in_specs=2 * n * [_HBM] + len(extra) * [pl.BlockSpec(memory_space=pl.ANY)],
        out_specs=tuple(n_sem * [_SEM] + 2 * n * [_HBM] + [pl.BlockSpec(memory_space=pltpu.VMEM)]),
        input_output_aliases={i: n_sem + i for i in range(2 * n)},
        compiler_params=pltpu.CompilerParams(has_side_effects=_DATAFLOW),
    )(*[pltpu.with_memory_space_constraint(a, pltpu.HBM) for a in list(srcs) + lands], *extra)
    sems = [tuple(res[3 * g:3 * g + 3]) for g in range(len(groups))]
    return sems, list(res[n_sem:n_sem + n]), list(res[n_sem + n:n_sem + 2 * n]), res[-1]


def _own_copy(src_ref, land_ref, sem, my, gather):
    if gather:
        rows = src_ref.shape[0]
        return pltpu.make_async_copy(src_ref, land_ref.at[pl.ds(my * rows, rows)], sem)
    rows = src_ref.shape[0] // NDEV
    return pltpu.make_async_copy(src_ref.at[pl.ds(my * rows, rows)], land_ref.at[my], sem)


def _wait_all(land_ref, blocks_per_dev, copies, send_sem, recv_sem, me_pos):
    part = land_ref.at[pl.ds(0, copies * blocks_per_dev)]
    pltpu.make_async_remote_copy(src_ref=part, dst_ref=part, send_sem=send_sem, recv_sem=recv_sem,
                                 device_id=me_pos, device_id_type=MESH).wait()


def _gather_forward(sems, srcs, lands, after, name):
    n = len(srcs)

    def body(*refs):
        land_refs = refs[n:2 * n]
        send_a, recv_a = refs[2 * n], refs[2 * n + 1]
        send_b, recv_b = refs[2 * n + 3], refs[2 * n + 4]
        token = refs[-1]
        (me_pos, _), sibling, chips = _places()
        for j in range(n):
            _wait_all(land_refs[j], lands[j].shape[0] // NDEV, 1 + OTHER_CHIPS, send_a.at[j], recv_a.at[j], me_pos)
        for j in range(n):
            rows = lands[j].shape[0] // NDEV
            for _, idx in chips:
                block = land_refs[j].at[pl.ds(idx * rows, rows)]
                pltpu.make_async_remote_copy(src_ref=block, dst_ref=block, send_sem=send_b.at[j], recv_sem=recv_b.at[j],
                                             device_id=sibling[0], device_id_type=MESH).start()
        token[...] = jnp.zeros_like(token)

    res = pl.pallas_call(
        body, name=name,
        out_shape=(pltpu.SemaphoreType.DMA((n,)), pltpu.SemaphoreType.DMA((n,)))
        + tuple(pltpu.HBM(a.shape, a.dtype) for a in list(srcs) + list(lands)) + (jax.ShapeDtypeStruct((8, 128), F32),),
        in_specs=2 * n * [_HBM] + [_SEM, _SEM, pl.BlockSpec(memory_space=pl.ANY)],
        out_specs=tuple([_SEM, _SEM] + 2 * n * [_HBM] + [pl.BlockSpec(memory_space=pltpu.VMEM)]),
        input_output_aliases={i: 2 + i for i in range(2 * n)},
        compiler_params=pltpu.CompilerParams(has_side_effects=_DATAFLOW),
    )(*srcs, *lands, sems[0], sems[1], after)
    return (res[0], res[1]), list(res[2:2 + n]), list(res[2 + n:2 + 2 * n]), res[-1]


def _split_wait(sems, srcs, lands, after, copies, gather, name):
    n = len(srcs)

    def body(*refs):
        src_refs, land_refs = refs[:n], refs[n:2 * n]
        send_sem, recv_sem, local_sem = refs[2 * n], refs[2 * n + 1], refs[2 * n + 2]
        (me_pos, my), _, _ = _places()
        for j in range(n):
            _wait_all(land_refs[j], lands[j].shape[0] // NDEV, copies, send_sem.at[j], recv_sem.at[j], me_pos)
            _own_copy(src_refs[j], land_refs[j], local_sem.at[j], my, gather).wait()

    res = pl.pallas_call(
        body, name=name, out_shape=tuple(pltpu.HBM(a.shape, a.dtype) for a in list(srcs) + list(lands)),
        in_specs=2 * n * [_HBM] + [_SEM, _SEM, _SEM, pl.BlockSpec(memory_space=pl.ANY)], out_specs=tuple(2 * n * [_HBM]),
        input_output_aliases={i: i for i in range(2 * n)},
        compiler_params=pltpu.CompilerParams(has_side_effects=_DATAFLOW),
    )(*srcs, *lands, sems[0], sems[1], sems[2], after)
    return list(res[n:])


def _chained(gate, mid, after):
    return gate if mid is None else gate + mid(after)[:1, :1]


def _ffn_fwd(x, norms, mod, w, mid=None):
    (pre_g, post_g), (shift, scale, gate), (wg_t, wu_t, wd) = norms, mod, w
    if not callable(wd):
        hn, g, u, a, x_out, f = _ffn_fwd_fused(x, pre_g, scale, shift, post_g, _chained(gate, mid, x), wg_t, wu_t, wd, "ffn_fwd")
        return x_out, (x, hn, g, u, a, f), (wg_t, wu_t, wd)
    hn, g, u, a = _ffn_up(x, pre_g, scale, shift, wg_t, wu_t, "ffn_up")
    wd = wd(a)
    x_out, f = _mm_post(a, wd, x, post_g, _chained(gate, mid, a), FFN_RES, "ffn_down")
    return x_out, (x, hn, g, u, a, f), (wg_t, wu_t, wd)


def _ffn_bwd(dx_out, saved, norms, mod, w, send=None):
    (pre_g, post_g), (_, scale, gate), (wg_t, wu_t, wd) = norms, mod, w
    x, hn, g, u, a, f = saved
    d_model = x.shape[1]
    if send is None:
        df, dg, du, dx, dgate, dpost, dshift, dscale, dpre = _ffn_bwd_fused(dx_out, saved, pre_g, post_g, scale, gate,
                                                                            wg_t, wu_t, wd, "ffn_bwd")
        return dx, (dpre, dpost), (dshift, dscale, dgate), tuple(_ffn_dw(dg, du, a, hn, df, "ffn_dw3"))
    sent = send
    df, dgate, dpost = _post_bwd(dx_out, f, post_g, gate, FFN_RES, "ffn_post_bwd")
    dwd = _mm([(a, df)], "tn", BF16, 256, d_model, "ffn_dw")
    dg, du = _ffn_dgu(df, wd, g, u, "ffn_dgu", after=sent(2, dwd))
    dwg_t = _mm([(dg, hn)], "tn", BF16, 256, d_model, "ffn_dw")
    dwu_t = _mm([(du, hn)], "tn", BF16, 256, d_model, "ffn_dw", after=sent(0, dwg_t))
    dhn = _mm([(dg, wg_t), (du, wu_t)], "nn", F32, TOKEN_TILE, d_model, "ffn_dhn", after=sent(1, dwu_t))
    dx, dshift, dscale, dpre = _prenorm_bwd(dx_out, [dhn], x, pre_g, scale, "prenorm_bwd")
    return dx, (dpre, dpost), (dshift, dscale, dgate), (dwg_t, dwu_t, dwd)


def _mla_fwd(x, norms, mod, w, rope, mid=None):
    (pre_g, post_g), (shift, scale, gate) = norms, mod
    w_in, q_norm, wq_t, kv_norm, wkv_t, wo = w
    hn, lat = _prenorm_mm(x, pre_g, scale, shift, w_in, "nn", F32, LAT_PAD, "mla_in")
    gate = _chained(gate, mid, lat)
    q, k, v, qn, kvn = _mla_qkv(lat, q_norm, kv_norm, wq_t, wkv_t, rope, "mla_qkv")
    o = _mla_attn_fwd(q, k, v, "mla_attn_fwd")
    x_out, f = _mm_post(o, wo, x, post_g, gate, 1.0, "mla_out")
    return x_out, (x, hn, lat, q, k, v, qn, kvn, o, f)


def _mla_bwd(dx_out, saved, norms, mod, w, rope):
    (pre_g, post_g), (_, scale, gate) = norms, mod
    w_in, q_norm, wq_t, kv_norm, wkv_t, wo = w
    x, hn, lat, q, k, v, qn, kvn, o, f = saved
    d_model = x.shape[1]
    df, dgate, dpost = _post_bwd(dx_out, f, post_g, gate, 1.0, "mix_post_bwd")
    d_o = _mm([(df, wo)], "nt", F32, TOKEN_TILE, wo.shape[0], "mla_do")
    dwo = _mm([(o, df)], "tn", BF16, TOKEN_TILE, d_model, "mla_dwo")
    dq, dk, dv = _mla_attn_bwd(q, k, v, d_o, "mla_attn_bwd")
    dqp, dkv, dlat, dq_norm, dkv_norm = _mla_qkv_bwd(dq, dk, dv, lat, q_norm, kv_norm, wq_t, wkv_t, rope, "mla_qkv_bwd")
    dwq_t = _mm([(dqp, qn)], "tn", BF16, TOKEN_TILE, Q_LORA, "mla_dwq")
    dwkv_t = _mm([(dkv, kvn)], "tn", BF16, TOKEN_TILE, KV_LORA, "mla_dwkv")
    dw_in = _mm([(hn, dlat)], "tn", BF16, TOKEN_TILE, LAT_PAD, "mla_dwin")
    dhn = _mm([(dlat, w_in)], "nt", F32, TOKEN_TILE, d_model, "mla_dhn")
    dx, dshift, dscale, dpre = _prenorm_bwd(dx_out, [dhn], x, pre_g, scale, "prenorm_bwd")
    return dx, (dpre, dpost), (dshift, dscale, dgate), (dw_in, dq_norm, dwq_t, dkv_norm, dwkv_t, dwo)


def _dil_fwd(x, norms, mod, w, bias, mid=None):
    (pre_g, post_g), (shift, scale, gate), (w_in_t, wo) = norms, mod, w
    width = 3 * DIL_HEADS * DIL_HEAD_DIM
    hns, qkvs, outs, lses = [], [], [], []
    for g, (window, dilation) in enumerate(DIL_GROUPS):
        hn, qkv = _prenorm_mm(x, pre_g, scale, shift, w_in_t, "nt", BF16, width, "dil_in", perm=dilation,
                              w_rows=(g * width, width))
        if g == 0:
            gate = _chained(gate, mid, qkv)
        o, lse = _dil_attn_fwd(qkv, bias[g], dilation, window // dilation, "dil_attn_fwd")
        hns.append(hn), qkvs.append(qkv), outs.append(o), lses.append(lse)
    alphas, o_mix, o_mix_b = _dil_mix(lses, outs, "dil_mix")
    x_out, f = _mm_post(o_mix_b, wo, x, post_g, gate, 1.0, "dil_out")
    return x_out, (x, hns, qkvs, lses, alphas, o_mix, o_mix_b, f)


def _dil_bwd(dx_out, saved, norms, mod, w, bias):
    (pre_g, post_g), (_, scale, gate), (w_in_t, wo) = norms, mod, w
    x, hns, qkvs, lses, alphas, o_mix, o_mix_b, f = saved
    d_model = x.shape[1]
    inner = DIL_HEADS * DIL_HEAD_DIM
    df, dgate, dpost = _post_bwd(dx_out, f, post_g, gate, 1.0, "mix_post_bwd")
    d_o = _mm([(df, wo)], "nt", F32, TOKEN_TILE, inner, "dil_do")
    dwo = _mm([(o_mix_b, df)], "tn", BF16, TOKEN_TILE, d_model, "dil_dwo")
    dhns, dws, dbs = [], [], []
    for g, (window, dilation) in enumerate(DIL_GROUPS):
        grads = _dil_attn_bwd(qkvs[g], bias[g], d_o, o_mix, alphas[g], lses[g], dilation, window // dilation, "dil_attn_bwd")
        dbs.append(grads[3])
        dhns.append(_mm([(grads[j], w_in_t) for j in range(3)], "nn", F32, TOKEN_TILE, d_model, "dil_dhn", out_perm=dilation,
                        b_rows=[(3 * g + j) * inner for j in range(3)]))
        dws += list(_mm_tn_shared(list(grads[:3]), hns[g], "dil_dwin"))
    dx, dshift, dscale, dpre = _prenorm_bwd(dx_out, dhns, x, pre_g, scale, "prenorm_bwd3")
    return dx, (dpre, dpost), (dshift, dscale, dgate), (jnp.concatenate(dws, axis=0), dwo), jnp.concatenate(dbs, axis=0)


def _pad_rows(a, rows):
    return jnp.pad(a, ((0, rows - a.shape[0]), (0, 0)))


def _lanes(a):
    flat = a.reshape(-1).astype(F32)
    rows = -(-flat.shape[0] // 1024) * 8
    return jnp.pad(flat, (0, rows * 128 - flat.shape[0])).reshape(rows, 128)


def kernel(x, c, norm_pre, norm_post, w_mod, b_mod, ffn_w_gate, ffn_w_up, ffn_w_down, mla_w_in, mla_q_norm, mla_w_q_up, mla_kv_norm, mla_w_kv_up, mla_w_o, dil_w_in, dil_w_o, rel_bias, loss_target, m_norm_pre, m_norm_post, m_w_mod, m_b_mod, m_ffn_w_gate, m_ffn_w_up, m_ffn_w_down, m_mla_w_in, m_mla_q_norm, m_mla_w_q_up, m_mla_kv_norm, m_mla_w_kv_up, m_mla_w_o, m_dil_w_in, m_dil_w_o, m_rel_bias, v_norm_pre, v_norm_post, v_w_mod, v_b_mod, v_ffn_w_gate, v_ffn_w_up, v_ffn_w_down, v_mla_w_in, v_mla_q_norm, v_mla_w_q_up, v_mla_kv_norm, v_mla_w_kv_up, v_mla_w_o, v_dil_w_in, v_dil_w_o, v_rel_bias):
    me = 4 * lax.axis_index("x") + 2 * lax.axis_index("y") + lax.axis_index("c")
    depth, n_sub, d_loc = norm_pre.shape
    d_model = x.shape[2]
    mod_loc_cols = w_mod.shape[2]
    x0, target = x[0], loss_target[0]

    bf_t = lambda a: a.astype(BF16).T
    ffn_ids = [(i, h) for i in range(depth) for h in range(2)]
    shards = []
    for i, h in ffn_ids:
        shards += [bf_t(ffn_w_gate[i, h]), bf_t(ffn_w_up[i, h]), ffn_w_down[i, h].astype(BF16)]
    shards += [mla_w_in[0].astype(BF16), bf_t(mla_w_q_up[0]), bf_t(mla_w_kv_up[0]), mla_w_o[0].astype(BF16),
               bf_t(dil_w_in[0]), dil_w_o[0].astype(BF16)]
    n_ffn = 3 * len(ffn_ids)
    members = {(0, 0): [0, 1, 2], (0, 1): [n_ffn, n_ffn + 1, n_ffn + 2, n_ffn + 3], (0, 2): [3, 4, 5],
               (1, 0): [6, 7, 8], (1, 1): [n_ffn + 4, n_ffn + 5], (1, 2): [9, 10, 11]}
    order = [(i, s) for i in range(depth) for s in range(n_sub)]

    small = jnp.concatenate([c.reshape(8, 128), _pad_rows(norm_pre.reshape(depth * n_sub, d_loc), 8),
                             _pad_rows(norm_post.reshape(depth * n_sub, d_loc), 8)], axis=0)
    small_all = _exchange([small], True, "gather_small")[0].reshape(NDEV, 24, 128)
    c_all = small_all[:, 0:8].reshape(NDEV, d_model)
    gains = lambda lo: jnp.transpose(small_all[:, lo:lo + depth * n_sub], (1, 0, 2)).reshape(depth, n_sub, 1, d_model)
    pre_full, post_full = gains(8), gains(16)

    b_loc = lax.dynamic_slice(b_mod, (0, me * mod_loc_cols), (depth, mod_loc_cols))
    mod_cols, silu_c = _mod_fwd(c_all, w_mod, b_loc, "mod_fwd")
    mod_all = _exchange([mod_cols.reshape(depth * NDEV, mod_loc_cols)], True, "gather_mod")[0]
    mod_all = mod_all.reshape(NDEV, depth, NDEV, mod_loc_cols)
    mod_mine = lax.dynamic_index_in_dim(mod_all, me, axis=2, keepdims=False)
    mod = jnp.transpose(mod_mine, (1, 0, 2)).reshape(depth, n_sub, 3, 1, d_model)

    first = order[0]
    stages = [("%d%d" % first, members[first][:2]), ("%d%dd" % first, members[first][2:])]
    stages += [("%d%d" % key, members[key]) for key in order[1:]]
    started = {}

    def start(these, name, after):
        used = [k for _, idx in these for k in idx]
        sems, srcs, lands, token = _split_start([shards[k] for k in used], [[used.index(k) for k in idx] for _, idx in these],
                                                True, name, after)
        for n, (stage, idx) in enumerate(these):
            started[stage] = (sems[n], [srcs[used.index(k)] for k in idx], [lands[used.index(k)] for k in idx])
        return token

    g_token = start(stages[:2], "gather_weights_start_first", mod_all)
    g_token = start(stages[2:], "gather_weights_start_rest", g_token)

    forwarded = {}

    def forward(stage, after):
        sems, srcs, lands = started[stage]
        forwarded[stage] = _gather_forward(sems, srcs, lands, after, "gather_forward_" + stage)
        return forwarded[stage][3]

    def weights_of(stage, after):
        (send_b, recv_b), srcs, lands, _ = forwarded[stage]
        return _split_wait((send_b, recv_b, started[stage][0][2]), srcs, lands, after, OTHER_CHIPS, True, "gather_wait_" + stage)

    def late_down(after):
        forward("%d%dd" % first, after)
        return weights_of("%d%dd" % first, after)[0]

    lat_real = Q_LORA + KV_LORA
    qk = QK_NOPE + QK_ROPE

    def mla_weights(after):
        w_in, wq_t, wkv_t, wo = weights_of("01", after)
        w_in_pad = jnp.concatenate([w_in[:, :lat_real], jnp.zeros((d_model, QK_NOPE), BF16), w_in[:, lat_real:],
                                    jnp.zeros((d_model, HEAD_PAD - QK_NOPE - QK_ROPE), BF16)], axis=1)
        wq_pad = jnp.pad(wq_t.reshape(MLA_HEADS, qk, Q_LORA), ((0, 0), (0, HEAD_PAD - qk), (0, 0)))
        wo_pad = jnp.pad(wo.reshape(MLA_HEADS, V_HEAD, d_model), ((0, 0), (HEAD_PAD - V_HEAD, 0), (0, 0)))
        return (w_in_pad, mla_q_norm, wq_pad.reshape(MLA_HEADS * HEAD_PAD, Q_LORA), mla_kv_norm, wkv_t,
                wo_pad.reshape(MLA_HEADS * HEAD_PAD, d_model))

    zero = g_token[0, 0]
    rope = _rope_tables(zero)
    buckets = jnp.stack([_dil_buckets(dil) for _, dil in DIL_GROUPS]) + zero.astype(jnp.int32)
    onehot = (buckets[..., None] == jnp.arange(N_BUCKETS)).astype(F32)
    bias = jnp.einsum("gqkb,bgh->ghqk", onehot, rel_bias.reshape(N_BUCKETS, len(DIL_GROUPS), DIL_HEADS),
                      precision=lax.Precision.HIGHEST)

    norms = lambda i, s: (pre_full[i, s], post_full[i, s])
    mods = lambda i, s: (mod[i, s, 0], mod[i, s, 1], mod[i, s, 2])
    saved, weights = {}, {}
    h = x0
    forward("%d%d" % first, bias)
    for n, (i, s) in enumerate(order):
        got = mla_weights(h) if (s == 1 and i % 2 == 0) else tuple(weights_of("%d%d" % (i, s), h))
        mid = None if n + 1 == len(order) else (lambda after, nxt="%d%d" % order[n + 1]: forward(nxt, after))
        if s != 1:
            if len(got) == 3:
                h, saved[i, s], weights[i, s] = _ffn_fwd(h, norms(i, s), mods(i, s), got)
                if mid is not None:
                    mid(h)
            else:
                h, saved[i, s], weights[i, s] = _ffn_fwd(h, norms(i, s), mods(i, s), (*got, late_down), mid)
            continue
        weights[i, s] = got
        if i % 2 == 0:
            h, saved[i, s] = _mla_fwd(h, norms(i, s), mods(i, s), weights[i, s], rope, mid)
        else:
            h, saved[i, s] = _dil_fwd(h, norms(i, s), mods(i, s), weights[i, s], bias, mid)
    dh, loss_parts = _loss_grad(h, target, "loss")

    dnorm, dmod, sent = {}, {}, {}
    token = jnp.zeros((8, 128), F32)
    last = order[0]

    def send_last(j, dw):
        sent[last, j] = _split_start([dw], [[0]], False, "scatter_start_%d%d_%d" % (*last, j))
        return sent[last, j][3]

    for i, s in reversed(order):
        md = mods(i, s)
        md = (md[0], md[1], md[2] + token[:1, :1])
        if (i, s) == last:
            dh, dnorm[i, s], dmod[i, s], _ = _ffn_bwd(dh, saved[i, s], norms(i, s), md, weights[i, s], send_last)
            continue
        if s != 1:
            dh, dnorm[i, s], dmod[i, s], dws = _ffn_bwd(dh, saved[i, s], norms(i, s), md, weights[i, s])
        elif i % 2 == 0:
            dh, dnorm[i, s], dmod[i, s], dmla = _mla_bwd(dh, saved[i, s], norms(i, s), md, weights[i, s], rope)
            dw_in_pad, dq_norm, dwq_pad, dkv_norm, dwkv_t, dwo_pad = dmla
            dw_in = jnp.concatenate([dw_in_pad[:, :lat_real], dw_in_pad[:, lat_real + QK_NOPE:lat_real + qk]], axis=1)
            dwq_t = dwq_pad.reshape(MLA_HEADS, HEAD_PAD, Q_LORA)[:, :qk].reshape(MLA_HEADS * qk, Q_LORA)
            dwo = dwo_pad.reshape(MLA_HEADS, HEAD_PAD, d_model)[:, HEAD_PAD - V_HEAD:].reshape(MLA_HEADS * V_HEAD, d_model)
            dws = (dw_in, dwq_t, dwkv_t, dwo)
        else:
            dh, dnorm[i, s], dmod[i, s], dws, dbias = _dil_bwd(dh, saved[i, s], norms(i, s), md, weights[i, s], bias)
        sent[i, s] = _split_start(list(dws), [list(range(len(dws)))], False, "scatter_start_%d%d" % (i, s))
        token = sent[i, s][3]
    grad_x = dh[None]

    mine = {}
    transposed = {3 * n + j for n in range(len(ffn_ids)) for j in (0, 1)} | {n_ffn + 1, n_ffn + 2, n_ffn + 4}
    for key in reversed(order[1:]):
        sems, srcs, lands, _ = sent[key]
        parts = _split_wait(sems[0], srcs, lands, dh, NDEV - 1, False, "scatter_wait_%d%d" % key)
        for k, p in zip(members[key], parts):
            mine[k] = _sum_parts(p, "sum_parts", k in transposed)
    g_mla_in, g_q_up, g_kv_up, g_mla_o, g_dil_in, g_dil_o = (mine[k] for k in range(n_ffn, n_ffn + 6))
    g_mla_in, g_q_up, g_kv_up, g_mla_o = g_mla_in[None], g_q_up[None], g_kv_up[None], g_mla_o[None]
    g_dil_in, g_dil_o = g_dil_in[None], g_dil_o[None]
    early = {"mla_w_in": _adamw(mla_w_in, g_mla_in, m_mla_w_in, v_mla_w_in, "adamw"),
             "mla_w_q_up": _adamw(mla_w_q_up, g_q_up, m_mla_w_q_up, v_mla_w_q_up, "adamw"),
             "mla_w_kv_up": _adamw(mla_w_kv_up, g_kv_up, m_mla_w_kv_up, v_mla_w_kv_up, "adamw"),
             "mla_w_o": _adamw(mla_w_o, g_mla_o, m_mla_w_o, v_mla_w_o, "adamw"),
             "dil_w_in": _adamw(dil_w_in, g_dil_in, m_dil_w_in, v_dil_w_in, "adamw"),
             "dil_w_o": _adamw(dil_w_o, g_dil_o, m_dil_w_o, v_dil_w_o, "adamw")}
    dbias_sums = _bias_reduce(dbias, buckets, "bias_reduce")
    tied = lax.optimization_barrier((dbias_sums, *[a for step in early.values() for a in step]))
    dbias_sums, early = tied[0], {name: tuple(tied[1 + 3 * n:4 + 3 * n]) for n, name in enumerate(early)}
    for j in (2, 0, 1):
        sems, srcs, lands, _ = sent[last, j]
        parts = _split_wait(sems[0], srcs, lands, dbias_sums, NDEV - 1, False, "scatter_wait_%d%d_%d" % (*last, j))
        mine[members[last][j]] = _sum_parts(parts[0], "sum_parts", members[last][j] in transposed)
    g_gate = jnp.stack([mine[3 * n] for n in range(len(ffn_ids))]).reshape(ffn_w_gate.shape)
    g_up = jnp.stack([mine[3 * n + 1] for n in range(len(ffn_ids))]).reshape(ffn_w_up.shape)
    g_down = jnp.stack([mine[3 * n + 2] for n in range(len(ffn_ids))]).reshape(ffn_w_down.shape)

    dmod_mine = jnp.concatenate([jnp.concatenate(dmod[i, s], axis=0) for i in range(depth) for s in range(n_sub)], axis=0)
    dpre_mine = jnp.concatenate([dnorm[i, s][0] for i in range(depth) for s in range(n_sub)], axis=0)
    dpost_mine = jnp.concatenate([dnorm[i, s][1] for i in range(depth) for s in range(n_sub)], axis=0)
    dbias_tab = dbias_sums[:, 0, :N_BUCKETS].T
    pieces = [dmod_mine, dpre_mine, dpost_mine, dq_norm, dkv_norm, dbias_tab, jnp.sum(loss_parts).reshape(1, 1)]
    packed = [_lanes(p) for p in pieces]
    offs = [0]
    for p in packed:
        offs.append(offs[-1] + p.shape[0])
    everyone = _exchange([jnp.concatenate(packed, axis=0)], True, "gather_small_grads")[0].reshape(NDEV, offs[-1], 128)
    total = _sum_parts(everyone, "sum_small")
    take = lambda n, shape: total[offs[n]:offs[n + 1]].reshape(-1)[:math.prod(shape)].reshape(shape)
    g_b_mod = take(0, b_mod.shape)
    col0 = me * d_loc
    g_norm_pre = lax.dynamic_slice(take(1, (depth, n_sub, d_model)), (0, 0, col0), norm_pre.shape)
    g_norm_post = lax.dynamic_slice(take(2, (depth, n_sub, d_model)), (0, 0, col0), norm_post.shape)
    g_q_norm, g_kv_norm = take(3, mla_q_norm.shape), take(4, mla_kv_norm.shape)
    g_rel_bias = take(5, rel_bias.shape)
    loss = take(6, ())

    dmod_all = everyone[:, offs[0]:offs[1]].reshape(NDEV, depth, NDEV * mod_loc_cols)
    dmod_cols = lax.dynamic_slice(dmod_all, (0, 0, me * mod_loc_cols), (NDEV, depth, mod_loc_cols))
    silu_t = jnp.pad(silu_c.T, ((0, 0), (0, HEAD_PAD - NDEV)))
    g_w_mod = jnp.stack([_mm([(silu_t, jnp.pad(dmod_cols[:, i], ((0, HEAD_PAD - NDEV), (0, 0))))], "nn", F32, TOKEN_TILE,
                             mod_loc_cols, "mod_bwd") for i in range(depth)])

    ws = (norm_pre, norm_post, w_mod, b_mod, ffn_w_gate, ffn_w_up, ffn_w_down, mla_w_in, mla_q_norm, mla_w_q_up, mla_kv_norm,
          mla_w_kv_up, mla_w_o, dil_w_in, dil_w_o, rel_bias)
    gs = (g_norm_pre, g_norm_post, g_w_mod, g_b_mod, g_gate, g_up, g_down, g_mla_in, g_q_norm, g_q_up, g_kv_norm, g_kv_up,
          g_mla_o, g_dil_in, g_dil_o, g_rel_bias)
    ms = (m_norm_pre, m_norm_post, m_w_mod, m_b_mod, m_ffn_w_gate, m_ffn_w_up, m_ffn_w_down, m_mla_w_in, m_mla_q_norm,
          m_mla_w_q_up, m_mla_kv_norm, m_mla_w_kv_up, m_mla_w_o, m_dil_w_in, m_dil_w_o, m_rel_bias)
    vs = (v_norm_pre, v_norm_post, v_w_mod, v_b_mod, v_ffn_w_gate, v_ffn_w_up, v_ffn_w_down, v_mla_w_in, v_mla_q_norm,
          v_mla_w_q_up, v_mla_kv_norm, v_mla_w_kv_up, v_mla_w_o, v_dil_w_in, v_dil_w_o, v_rel_bias)
    names = ("norm_pre", "norm_post", "w_mod", "b_mod", "ffn_w_gate", "ffn_w_up", "ffn_w_down", "mla_w_in", "mla_q_norm",
             "mla_w_q_up", "mla_kv_norm", "mla_w_kv_up", "mla_w_o", "dil_w_in", "dil_w_o", "rel_bias")
    stepped = [early[n] if n in early else _adamw(w, g, m, v, "adamw") for n, w, g, m, v in zip(names, ws, gs, ms, vs)]
    deltas, new_m, new_v = zip(*stepped)
    return (loss, grad_x, *gs, *deltas, *new_m, *new_v)
```

```python
import math

import jax
import jax.numpy as jnp
from jax import lax
from jax.experimental import pallas as pl
from jax.experimental.pallas import tpu as pltpu

F32 = jnp.float32
BF16 = jnp.bfloat16
MESH = pl.DeviceIdType.MESH

NDEV = 8
OTHER_CHIPS = 3
D_MODEL = 1024
SEQ = 2048
D_FF = 2816
EPS = 1e-6
FFN_RES = 0.5
FFN_CHUNKS = 11

MLA_HEADS = 16
Q_LORA = 384
KV_LORA = 256
QK_NOPE = 64
QK_ROPE = 32
V_HEAD = 64
ROPE_THETA = 10000.0
HEAD_PAD = 128
LAT_PAD = Q_LORA + KV_LORA + HEAD_PAD
MLA_SCALE = (QK_NOPE + QK_ROPE) ** -0.5
MLA_QUERY_TILE = 256

DIL_GROUPS = ((128, 1), (512, 4), (2048, 16))
DIL_HEADS = 16
DIL_HEAD_DIM = 64
DIL_BLOCK = 128
DIL_PAIRS = DIL_HEADS // 2
DIL_SCALE = DIL_HEAD_DIM ** -0.5
DIL_GROUPED = 8
N_BUCKETS = 32
MAX_DISTANCE = 2048

ADAM_LR = 0.001
ADAM_B1 = 0.9
ADAM_B2 = 0.999
ADAM_EPS = 1e-08
ADAM_WD = 0.01
ADAM_STEP = 10

V7X_VMEM_BYTES = 64 * 2**20
VMEM_RESERVE = 10 * 2**20
TOKEN_TILE = 512


def _nbytes(shape, dtype):
    return math.prod(shape) * jnp.dtype(dtype).itemsize


def _params(semantics, blocks, extra=0):
    need = 2 * sum(_nbytes(s, d) for s, d in blocks) + extra + VMEM_RESERVE
    return pltpu.CompilerParams(dimension_semantics=semantics,
                                vmem_limit_bytes=int(min(need, V7X_VMEM_BYTES - VMEM_RESERVE)))


def _pcall(body, out_shape, **kw):
    call = pl.pallas_call(body, out_shape=jax.tree.map(lambda s: pltpu.HBM(s.shape, s.dtype), out_shape), **kw)
    return lambda *args: call(*[pltpu.with_memory_space_constraint(a, pltpu.HBM) for a in args])


def _dot_nn(a, b):
    return lax.dot_general(a, b, (((1,), (0,)), ((), ())), preferred_element_type=F32)


def _dot_nt(a, b):
    return lax.dot_general(a, b, (((1,), (1,)), ((), ())), preferred_element_type=F32)


def _dot_tn(a, b):
    return lax.dot_general(a, b, (((0,), (0,)), ((), ())), preferred_element_type=F32)


_DOTS = {"nn": _dot_nn, "nt": _dot_nt, "tn": _dot_tn}


def _rstd(v):
    return lax.rsqrt(jnp.mean(v * v, axis=-1, keepdims=True) + EPS)


def _rms_bwd(v, r, t):
    return r * t - v * (r * r * r) * jnp.mean(t * v, axis=-1, keepdims=True)


_TOKEN_SPEC = pl.BlockSpec((8, 128), lambda *_: (0, 0))


def _mm(pairs, mode, out_dtype, tm, tn, name, out_perm=1, after=None, b_rows=None):
    a0, b0 = pairs[0]
    m_dim = a0.shape[1] if mode == "tn" else a0.shape[0]
    n_dim = b0.shape[0] if mode == "nt" else b0.shape[1]
    tm, tn = min(tm, m_dim // out_perm), min(tn, n_dim)
    assert m_dim % tm == 0 and n_dim % tn == 0, (name, m_dim, n_dim, tm, tn)
    dot = _DOTS[mode]
    npairs = len(pairs)

    def body(*refs):
        acc = None
        for p in range(npairs):
            d = dot(refs[2 * p][...].astype(BF16), refs[2 * p + 1][...].astype(BF16))
            acc = d if acc is None else acc + d
        refs[-1][...] = acc.astype(out_dtype)

    in_specs, blocks, flat = [], [], []
    for n_pair, (a, b) in enumerate(pairs):
        if mode == "nn":
            k = a.shape[1]
            first_block = 0 if b_rows is None else b_rows[n_pair] // k
            sa, sb = ((tm, k), lambda i, j: (i, 0)), ((k, tn), lambda i, j, o=first_block: (o, j))
        elif mode == "nt":
            k = a.shape[1]
            sa, sb = ((tm, k), lambda i, j: (i, 0)), ((tn, k), lambda i, j: (j, 0))
        else:
            k = a.shape[0]
            sa, sb = ((k, tm), lambda i, j: (0, i)), ((k, tn), lambda i, j: (0, j))
        in_specs += [pl.BlockSpec(*sa), pl.BlockSpec(*sb)]
        blocks += [(sa[0], a.dtype), (sb[0], b.dtype)]
        flat += [a, b]
    if after is not None:
        in_specs.append(_TOKEN_SPEC)
        flat.append(after)
    if out_perm == 1:
        out_shape = (m_dim, n_dim)
        out_spec = pl.BlockSpec((tm, tn), lambda i, j: (i, j))
    else:
        rows = m_dim // out_perm
        assert tn == n_dim and rows % tm == 0, (name, rows, tm)
        nb = rows // tm
        out_shape = (rows, out_perm * n_dim)
        out_spec = pl.BlockSpec((tm, n_dim), lambda i, j: (i % nb, i // nb))
    blocks.append(((tm, tn), out_dtype))
    res = _pcall(
        body, out_shape=jax.ShapeDtypeStruct(out_shape, out_dtype), grid=(m_dim // tm, n_dim // tn),
        in_specs=in_specs, out_specs=out_spec, name=name,
        compiler_params=_params(("parallel", "parallel"), blocks, extra=2 * tm * tn * 4),
    )(*flat)
    return res.reshape(m_dim, n_dim)


def _prenorm_mm(x, pre_g, scale, shift, w, w_mode, out_dtype, tn, name, perm=1, w_rows=None):
    s_dim, d_dim = x.shape
    n_dim = w.shape[0] if w_mode == "nt" else w.shape[1]
    w_first = 0
    if w_rows is not None:
        w_first, n_dim = w_rows
    rows = s_dim // perm
    side = max(1, TOKEN_TILE // rows)
    tm = side * min(TOKEN_TILE, rows)
    nb = max(1, rows // tm)
    tn = min(tn, n_dim)
    assert n_dim % tn == 0 and w_first % tn == 0
    w_block0 = w_first // tn
    dot = _DOTS[w_mode]

    def body(x_ref, g_ref, sc_ref, sh_ref, w_ref, hn_ref, o_ref):
        @pl.when(pl.program_id(1) == 0)
        def _():
            xf = x_ref[...]
            if side > 1:
                xf = jnp.concatenate([xf[:, c * d_dim:(c + 1) * d_dim] for c in range(side)], axis=0)
            hn = (xf * _rstd(xf) * g_ref[...]) * (1.0 + sc_ref[...]) + sh_ref[...]
            hn_ref[...] = hn.astype(BF16)

        o_ref[...] = dot(hn_ref[...], w_ref[...]).astype(out_dtype)

    vec = pl.BlockSpec((1, d_dim), lambda i, j: (0, 0))
    w_block = (tn, d_dim) if w_mode == "nt" else (d_dim, tn)
    w_spec = pl.BlockSpec(w_block, (lambda i, j: (w_block0 + j, 0)) if w_mode == "nt" else (lambda i, j: (0, j)))
    hn, out = _pcall(
        body,
        out_shape=(jax.ShapeDtypeStruct((s_dim, d_dim), BF16), jax.ShapeDtypeStruct((s_dim, n_dim), out_dtype)),
        grid=(s_dim // tm, n_dim // tn),
        in_specs=[pl.BlockSpec((tm // side, side * d_dim), lambda i, j: (i % nb, i // nb)), vec, vec, vec, w_spec],
        out_specs=(pl.BlockSpec((tm, d_dim), lambda i, j: (i, 0)), pl.BlockSpec((tm, tn), lambda i, j: (i, j))),
        name=name,
        compiler_params=_params(("parallel", "arbitrary"),
                                [((tm, d_dim), F32), (w_block, BF16), ((tm, d_dim), BF16), ((tm, tn), out_dtype)],
                                extra=3 * tm * d_dim * 4 + tm * tn * 4),
    )(x.reshape(rows, perm * d_dim), pre_g, scale, shift, w)
    return hn, out


def _ffn_up(x, pre_g, scale, shift, wg_t, wu_t, name):
    s_dim, d_dim = x.shape
    f_dim = wg_t.shape[0]
    tm, tn = TOKEN_TILE, f_dim // 2

    def body(x_ref, g_ref, sc_ref, sh_ref, wg_ref, wu_ref, hn_ref, go_ref, uo_ref, a_ref):
        @pl.when(pl.program_id(1) == 0)
        def _():
            xf = x_ref[...]
            hn = (xf * _rstd(xf) * g_ref[...]) * (1.0 + sc_ref[...]) + sh_ref[...]
            hn_ref[...] = hn.astype(BF16)

        hn = hn_ref[...]
        g = _dot_nt(hn, wg_ref[...])
        u = _dot_nt(hn, wu_ref[...])
        go_ref[...] = g.astype(BF16)
        uo_ref[...] = u.astype(BF16)
        a_ref[...] = (g * jax.nn.sigmoid(g) * u).astype(BF16)

    vec = pl.BlockSpec((1, d_dim), lambda i, j: (0, 0))
    w_spec = pl.BlockSpec((tn, d_dim), lambda i, j: (j, 0))
    act = pl.BlockSpec((tm, tn), lambda i, j: (i, j))
    act_shape = jax.ShapeDtypeStruct((s_dim, f_dim), BF16)
    return _pcall(
        body,
        out_shape=(jax.ShapeDtypeStruct((s_dim, d_dim), BF16), act_shape, act_shape, act_shape),
        grid=(s_dim // tm, f_dim // tn),
        in_specs=[pl.BlockSpec((tm, d_dim), lambda i, j: (i, 0)), vec, vec, vec, w_spec, w_spec],
        out_specs=(pl.BlockSpec((tm, d_dim), lambda i, j: (i, 0)), act, act, act),
        name=name,
        compiler_params=_params(("parallel", "arbitrary"),
                                [((tm, d_dim), F32), ((tn, d_dim), BF16), ((tn, d_dim), BF16), ((tm, d_dim), BF16)]
                                + 3 * [((tm, tn), BF16)], extra=3 * tm * d_dim * 4 + 4 * tm * tn * 4),
    )(x, pre_g, scale, shift, wg_t, wu_t)


def _mm_post(a, w, x, post_g, gate, res_w, name):
    s_dim, k_dim = a.shape
    d_dim = w.shape[1]
    tm = TOKEN_TILE

    def body(a_ref, w_ref, x_ref, pg_ref, gt_ref, xo_ref, f_ref):
        f = _dot_nn(a_ref[...], w_ref[...])
        y = f * _rstd(f) * pg_ref[...]
        f_ref[...] = f
        xo_ref[...] = x_ref[...] + (res_w * gt_ref[...]) * y

    vec = pl.BlockSpec((1, d_dim), lambda i: (0, 0))
    row = pl.BlockSpec((tm, d_dim), lambda i: (i, 0))
    out = jax.ShapeDtypeStruct((s_dim, d_dim), F32)
    return _pcall(
        body, out_shape=(out, out), grid=(s_dim // tm,),
        in_specs=[pl.BlockSpec((tm, k_dim), lambda i: (i, 0)), pl.BlockSpec((k_dim, d_dim), lambda i: (0, 0)), row, vec, vec],
        out_specs=(row, row), name=name,
        compiler_params=_params(("parallel",), [((tm, k_dim), BF16), ((k_dim, d_dim), BF16)] + 3 * [((tm, d_dim), F32)],
                                extra=3 * tm * d_dim * 4),
    )(a, w, x, post_g, gate)


def _post_bwd(dx_out, f, post_g, gate, res_w, name):
    s_dim, d_dim = f.shape
    tm = TOKEN_TILE

    def body(dx_ref, f_ref, pg_ref, gt_ref, df_ref, dgate_ref, dpost_ref):
        @pl.when(pl.program_id(0) == 0)
        def _():
            dgate_ref[...] = jnp.zeros_like(dgate_ref)
            dpost_ref[...] = jnp.zeros_like(dpost_ref)

        dx, fv = dx_ref[...], f_ref[...]
        r = _rstd(fv)
        fr = fv * r
        dgate_ref[...] += res_w * jnp.sum(dx * (fr * pg_ref[...]), axis=0, keepdims=True)
        dy = (res_w * gt_ref[...]) * dx
        dpost_ref[...] += jnp.sum(dy * fr, axis=0, keepdims=True)
        df_ref[...] = _rms_bwd(fv, r, dy * pg_ref[...]).astype(BF16)

    vec = pl.BlockSpec((1, d_dim), lambda i: (0, 0))
    row = pl.BlockSpec((tm, d_dim), lambda i: (i, 0))
    vshape = jax.ShapeDtypeStruct((1, d_dim), F32)
    return _pcall(
        body, out_shape=(jax.ShapeDtypeStruct((s_dim, d_dim), BF16), vshape, vshape), grid=(s_dim // tm,),
        in_specs=[row, row, vec, vec], out_specs=(row, vec, vec), name=name,
        compiler_params=_params(("arbitrary",), 3 * [((tm, d_dim), F32)], extra=6 * tm * d_dim * 4),
    )(dx_out, f, post_g, gate)


def _prenorm_bwd(dx_out, dhns, x, pre_g, scale, name):
    s_dim, d_dim = x.shape
    tm = TOKEN_TILE
    n_in = len(dhns)

    def body(*refs):
        dx_ref, x_ref, pg_ref, sc_ref = refs[n_in + 0], refs[n_in + 1], refs[n_in + 2], refs[n_in + 3]
        dxo_ref, dsh_ref, dsc_ref, dpg_ref = refs[n_in + 4:]

        @pl.when(pl.program_id(0) == 0)
        def _():
            dsh_ref[...] = jnp.zeros_like(dsh_ref)
            dsc_ref[...] = jnp.zeros_like(dsc_ref)
            dpg_ref[...] = jnp.zeros_like(dpg_ref)

        dhn = refs[0][...]
        for k in range(1, n_in):
            dhn = dhn + refs[k][...]
        xv = x_ref[...]
        r = _rstd(xv)
        xr = xv * r
        dsh_ref[...] += jnp.sum(dhn, axis=0, keepdims=True)
        dsc_ref[...] += jnp.sum(dhn * (xr * pg_ref[...]), axis=0, keepdims=True)
        dn = dhn * (1.0 + sc_ref[...])
        dpg_ref[...] += jnp.sum(dn * xr, axis=0, keepdims=True)
        dxo_ref[...] = dx_ref[...] + _rms_bwd(xv, r, dn * pg_ref[...])

    vec = pl.BlockSpec((1, d_dim), lambda i: (0, 0))
    row = pl.BlockSpec((tm, d_dim), lambda i: (i, 0))
    vshape = jax.ShapeDtypeStruct((1, d_dim), F32)
    return _pcall(
        body, out_shape=(jax.ShapeDtypeStruct((s_dim, d_dim), F32), vshape, vshape, vshape), grid=(s_dim // tm,),
        in_specs=n_in * [row] + [row, row, vec, vec], out_specs=(row, vec, vec, vec), name=name,
        compiler_params=_params(("arbitrary",), (n_in + 3) * [((tm, d_dim), F32)], extra=6 * tm * d_dim * 4),
    )(*dhns, dx_out, x, pre_g, scale)


def _ffn_dgu(df, wd, g, u, name, after=None):
    s_dim, d_dim = df.shape
    f_dim = wd.shape[0]
    tm, tn = TOKEN_TILE, f_dim // 2

    def body(df_ref, wd_ref, g_ref, u_ref, *rest):
        dg_ref, du_ref = rest[-2:]
        da = _dot_nt(df_ref[...], wd_ref[...])
        gv, uv = g_ref[...].astype(F32), u_ref[...].astype(F32)
        sg = jax.nn.sigmoid(gv)
        du_ref[...] = (da * (gv * sg)).astype(BF16)
        dg_ref[...] = (da * uv * (sg * (1.0 + gv * (1.0 - sg)))).astype(BF16)

    act = pl.BlockSpec((tm, tn), lambda i, j: (i, j))
    act_shape = jax.ShapeDtypeStruct((s_dim, f_dim), BF16)
    token = [] if after is None else [after]
    return _pcall(
        body, out_shape=(act_shape, act_shape), grid=(s_dim // tm, f_dim // tn),
        in_specs=[pl.BlockSpec((tm, d_dim), lambda i, j: (i, 0)), pl.BlockSpec((tn, d_dim), lambda i, j: (j, 0)), act, act]
        + len(token) * [_TOKEN_SPEC],
        out_specs=(act, act), name=name,
        compiler_params=_params(("parallel", "parallel"), [((tm, d_dim), BF16), ((tn, d_dim), BF16)] + 4 * [((tm, tn), BF16)],
                                extra=6 * tm * tn * 4),
    )(df, wd, g, u, *token)


def _ffn_dw(dg, du, a, hn, df, name):
    s_dim, f_dim = dg.shape
    d_dim = hn.shape[1]
    tm = 256

    def body(dg_ref, du_ref, a_ref, hn_ref, df_ref, dwg_ref, dwu_ref, dwd_ref):
        dwg_ref[...] = _dot_tn(dg_ref[...], hn_ref[...]).astype(BF16)
        dwu_ref[...] = _dot_tn(du_ref[...], hn_ref[...]).astype(BF16)
        dwd_ref[...] = _dot_tn(a_ref[...], df_ref[...]).astype(BF16)

    col = pl.BlockSpec((s_dim, tm), lambda i: (0, i))
    full = pl.BlockSpec((s_dim, d_dim), lambda i: (0, 0), pipeline_mode=pl.Buffered(1))
    out = pl.BlockSpec((tm, d_dim), lambda i: (i, 0))
    shape = jax.ShapeDtypeStruct((f_dim, d_dim), BF16)
    need = 2 * s_dim * d_dim * 2 + 2 * 3 * (s_dim * tm * 2 + tm * d_dim * 2) + 3 * tm * d_dim * 4 + 3 * s_dim * tm * 2
    return _pcall(
        body, out_shape=(shape, shape, shape), grid=(f_dim // tm,), in_specs=[col, col, col, full, full],
        out_specs=(out, out, out), name=name,
        compiler_params=pltpu.CompilerParams(dimension_semantics=("parallel",),
                                             vmem_limit_bytes=int(min(need + VMEM_RESERVE, V7X_VMEM_BYTES - VMEM_RESERVE))),
    )(dg, du, a, hn, df)


def _mm_tn_shared(lhs, b, name):
    k_dim, m_dim = lhs[0].shape
    n_dim = b.shape[1]
    tm = 256
    n = len(lhs)

    def body(*refs):
        rhs = refs[n][...]
        for j in range(n):
            refs[n + 1 + j][...] = _dot_tn(refs[j][...], rhs).astype(BF16)

    col = pl.BlockSpec((k_dim, tm), lambda i: (0, i))
    out = pl.BlockSpec((tm, n_dim), lambda i: (i, 0))
    shape = jax.ShapeDtypeStruct((m_dim, n_dim), BF16)
    need = k_dim * n_dim * 2 + 2 * n * (k_dim * tm * 2 + tm * n_dim * 2) + n * tm * n_dim * 4 + n * k_dim * tm * 2
    return _pcall(
        body, out_shape=tuple(n * [shape]), grid=(m_dim // tm,),
        in_specs=n * [col] + [pl.BlockSpec((k_dim, n_dim), lambda i: (0, 0), pipeline_mode=pl.Buffered(1))],
        out_specs=tuple(n * [out]), name=name,
        compiler_params=pltpu.CompilerParams(dimension_semantics=("parallel",),
                                             vmem_limit_bytes=int(min(need + VMEM_RESERVE, V7X_VMEM_BYTES - VMEM_RESERVE))),
    )(*lhs, b)


def _ffn_fwd_fused(x, pre_g, scale, shift, post_g, gate, wg_t, wu_t, wd, name):
    s_dim, d_dim = x.shape
    f_dim = wd.shape[0]
    tm, chunks = 256, FFN_CHUNKS
    cw = f_dim // chunks

    def body(x_ref, prg_ref, sc_ref, sh_ref, pg_ref, gt_ref, wg_ref, wu_ref, wd_ref, hn_ref, go_ref, uo_ref, a_ref, xo_ref, f_ref):
        xf = x_ref[...]
        hn = ((xf * _rstd(xf) * prg_ref[...]) * (1.0 + sc_ref[...]) + sh_ref[...]).astype(BF16)
        hn_ref[...] = hn
        f = None
        ahead = (_dot_nt(hn, wg_ref[0:cw, :]), _dot_nt(hn, wu_ref[0:cw, :]))
        for c in range(chunks):
            g, u = ahead
            if c + 1 < chunks:
                nxt = slice((c + 1) * cw, (c + 2) * cw)
                ahead = (_dot_nt(hn, wg_ref[nxt, :]), _dot_nt(hn, wu_ref[nxt, :]))
            cols = slice(c * cw, (c + 1) * cw)
            go_ref[:, cols] = g.astype(BF16)
            uo_ref[:, cols] = u.astype(BF16)
            a = (g * jax.nn.sigmoid(g) * u).astype(BF16)
            a_ref[:, cols] = a
            part = _dot_nn(a, wd_ref[cols, :])
            f = part if f is None else f + part
        f_ref[...] = f
        xo_ref[...] = xf + (FFN_RES * gt_ref[...]) * (f * _rstd(f) * pg_ref[...])

    vec = pl.BlockSpec((1, d_dim), lambda i: (0, 0))
    row = pl.BlockSpec((tm, d_dim), lambda i: (i, 0))
    act = pl.BlockSpec((tm, f_dim), lambda i: (i, 0))
    weight = pl.BlockSpec((f_dim, d_dim), lambda i: (0, 0), pipeline_mode=pl.Buffered(1))
    act_shape = jax.ShapeDtypeStruct((s_dim, f_dim), BF16)
    res_shape = jax.ShapeDtypeStruct((s_dim, d_dim), F32)
    need = (3 * f_dim * d_dim * 2 + 2 * tm * d_dim * 4 + 2 * (tm * d_dim * 2 + 3 * tm * f_dim * 2 + 2 * tm * d_dim * 4)
            + 8 * tm * cw * 4 + 4 * tm * d_dim * 4)
    return _pcall(
        body, out_shape=(jax.ShapeDtypeStruct((s_dim, d_dim), BF16), act_shape, act_shape, act_shape, res_shape, res_shape),
        grid=(s_dim // tm,), in_specs=[row, vec, vec, vec, vec, vec, weight, weight, weight],
        out_specs=(row, act, act, act, row, row), name=name,
        compiler_params=pltpu.CompilerParams(dimension_semantics=("parallel",),
                                             vmem_limit_bytes=int(min(need + VMEM_RESERVE, V7X_VMEM_BYTES - VMEM_RESERVE))),
    )(x, pre_g, scale, shift, post_g, gate, wg_t, wu_t, wd)


def _ffn_bwd_fused(dx_out, saved, pre_g, post_g, scale, gate, wg_t, wu_t, wd, name):
    x, _, g, u, _, f = saved
    s_dim, d_dim = x.shape
    f_dim = wd.shape[0]
    tm, chunks = 256, FFN_CHUNKS
    cw = f_dim // chunks

    def body(dx_ref, f_ref, g_ref, u_ref, x_ref, pg_ref, gt_ref, prg_ref, sc_ref, wd_ref, wg_ref, wu_ref,
             df_ref, dg_ref, du_ref, dxo_ref, dgate_ref, dpost_ref, dsh_ref, dsc_ref, dpg_ref):
        @pl.when(pl.program_id(0) == 0)
        def _():
            for acc in (dgate_ref, dpost_ref, dsh_ref, dsc_ref, dpg_ref):
                acc[...] = jnp.zeros_like(acc)

        dx, fv = dx_ref[...], f_ref[...]
        r = _rstd(fv)
        fr = fv * r
        dgate_ref[...] += FFN_RES * jnp.sum(dx * (fr * pg_ref[...]), axis=0, keepdims=True)
        dy = (FFN_RES * gt_ref[...]) * dx
        dpost_ref[...] += jnp.sum(dy * fr, axis=0, keepdims=True)
        df = _rms_bwd(fv, r, dy * pg_ref[...]).astype(BF16)
        df_ref[...] = df
        dhn = None
        ahead = _dot_nt(df, wd_ref[0:cw, :])
        for c in range(chunks):
            da = ahead
            if c + 1 < chunks:
                ahead = _dot_nt(df, wd_ref[(c + 1) * cw:(c + 2) * cw, :])
            cols = slice(c * cw, (c + 1) * cw)
            gv, uv = g_ref[:, cols].astype(F32), u_ref[:, cols].astype(F32)
            sg = jax.nn.sigmoid(gv)
            du = (da * (gv * sg)).astype(BF16)
            dg = (da * uv * (sg * (1.0 + gv * (1.0 - sg)))).astype(BF16)
            dg_ref[:, cols] = dg
            du_ref[:, cols] = du
            part = _dot_nn(dg, wg_ref[cols, :]) + _dot_nn(du, wu_ref[cols, :])
            dhn = part if dhn is None else dhn + part
        xv = x_ref[...]
        rx = _rstd(xv)
        xr = xv * rx
        dsh_ref[...] += jnp.sum(dhn, axis=0, keepdims=True)
        dsc_ref[...] += jnp.sum(dhn * (xr * prg_ref[...]), axis=0, keepdims=True)
        dn = dhn * (1.0 + sc_ref[...])
        dpg_ref[...] += jnp.sum(dn * xr, axis=0, keepdims=True)
        dxo_ref[...] = dx + _rms_bwd(xv, rx, dn * prg_ref[...])

    vec = pl.BlockSpec((1, d_dim), lambda i: (0, 0))
    row = pl.BlockSpec((tm, d_dim), lambda i: (i, 0))
    act = pl.BlockSpec((tm, f_dim), lambda i: (i, 0))
    weight = pl.BlockSpec((f_dim, d_dim), lambda i: (0, 0), pipeline_mode=pl.Buffered(1))
    vshape = jax.ShapeDtypeStruct((1, d_dim), F32)
    act_shape = jax.ShapeDtypeStruct((s_dim, f_dim), BF16)
    need = (3 * f_dim * d_dim * 2 + 2 * (3 * tm * d_dim * 4 + 2 * tm * f_dim * 2) + 2 * (tm * d_dim * 2 + 2 * tm * f_dim * 2 + tm * d_dim * 4)
            + 6 * tm * cw * 4 + 6 * tm * d_dim * 4)
    return _pcall(
        body, out_shape=(jax.ShapeDtypeStruct((s_dim, d_dim), BF16), act_shape, act_shape, jax.ShapeDtypeStruct((s_dim, d_dim), F32),
                         vshape, vshape, vshape, vshape, vshape),
        grid=(s_dim // tm,), in_specs=[row, row, act, act, row, vec, vec, vec, vec, weight, weight, weight],
        out_specs=(row, act, act, row, vec, vec, vec, vec, vec), name=name,
        compiler_params=pltpu.CompilerParams(dimension_semantics=("arbitrary",),
                                             vmem_limit_bytes=int(min(need + VMEM_RESERVE, V7X_VMEM_BYTES - VMEM_RESERVE))),
    )(dx_out, f, g, u, x, post_g, gate, pre_g, scale, wd, wg_t, wu_t)


def _rope_tables(zero=0.0):
    half = QK_ROPE // 2
    freqs = ROPE_THETA ** (-jnp.arange(half, dtype=F32) / half)
    ang = (jnp.arange(SEQ, dtype=F32)[:, None] + zero) * freqs[None, :]
    cos, sin = jnp.cos(ang), jnp.sin(ang)
    ones = jnp.ones((SEQ, QK_NOPE), F32)
    zeros = jnp.zeros((SEQ, QK_NOPE), F32)
    pad1 = jnp.ones((SEQ, HEAD_PAD - QK_NOPE - QK_ROPE), F32)
    pad0 = jnp.zeros((SEQ, HEAD_PAD - QK_NOPE - QK_ROPE), F32)
    zh = jnp.zeros((SEQ, half), F32)
    c = jnp.concatenate([ones, cos, cos, pad1], axis=1)
    s1 = jnp.concatenate([zeros, -sin, zh, pad0], axis=1)
    s2 = jnp.concatenate([zeros, zh, sin, pad0], axis=1)
    return c, s1, s2


def _rope(v, c, s1, s2):
    half = QK_ROPE // 2
    return v * c + pltpu.roll(v, HEAD_PAD - half, 1) * s1 + pltpu.roll(v, half, 1) * s2


def _rope_t(dv, c, s1, s2):
    half = QK_ROPE // 2
    return dv * c + pltpu.roll(dv * s1, half, 1) + pltpu.roll(dv * s2, HEAD_PAD - half, 1)


def _mla_qkv(lat, q_norm, kv_norm, wq_t, wkv_t, rope, name):
    s_dim = lat.shape[0]
    width = MLA_HEADS * HEAD_PAD
    tm = 256

    def body(lat_ref, qg_ref, kg_ref, wq_ref, wkv_ref, c_ref, s1_ref, s2_ref, q_ref, k_ref, v_ref, qn_ref, kvn_ref):
        cq = lat_ref[:, :Q_LORA]
        ckv = lat_ref[:, Q_LORA:Q_LORA + KV_LORA]
        kr = lat_ref[:, Q_LORA + KV_LORA:]
        c, s1, s2 = c_ref[...], s1_ref[...], s2_ref[...]
        qn = (cq * _rstd(cq) * qg_ref[...]).astype(BF16)
        kvn = (ckv * _rstd(ckv) * kg_ref[...]).astype(BF16)
        qn_ref[...] = qn
        kvn_ref[...] = kvn
        q = _dot_nt(qn, wq_ref[...])
        kv = _dot_nt(kvn, wkv_ref[...])
        krr = _rope(kr, c, s1, s2)
        low = lax.broadcasted_iota(jnp.int32, (tm, HEAD_PAD), 1) < QK_NOPE
        for h in range(MLA_HEADS):
            sl = slice(h * HEAD_PAD, (h + 1) * HEAD_PAD)
            q_ref[:, sl] = _rope(q[:, sl], c, s1, s2).astype(BF16)
            kvh = kv[:, sl]
            k_ref[:, sl] = (jnp.where(low, kvh, 0.0) + krr).astype(BF16)
            v_ref[:, sl] = jnp.where(low, 0.0, kvh).astype(BF16)

    row = lambda n: pl.BlockSpec((tm, n), lambda i: (i, 0))
    full = lambda a: pl.BlockSpec(a.shape, lambda i: (0, 0))
    wide = jax.ShapeDtypeStruct((s_dim, width), BF16)
    return _pcall(
        body,
        out_shape=(wide, wide, wide, jax.ShapeDtypeStruct((s_dim, Q_LORA), BF16), jax.ShapeDtypeStruct((s_dim, KV_LORA), BF16)),
        grid=(s_dim // tm,),
        in_specs=[row(LAT_PAD), full(q_norm), full(kv_norm), full(wq_t), full(wkv_t), row(HEAD_PAD), row(HEAD_PAD), row(HEAD_PAD)],
        out_specs=(row(width), row(width), row(width), row(Q_LORA), row(KV_LORA)), name=name,
        compiler_params=_params(("parallel",), [((tm, LAT_PAD), F32), (wq_t.shape, BF16), (wkv_t.shape, BF16)]
                                + 3 * [((tm, width), BF16)], extra=4 * tm * width * 4),
    )(lat, q_norm, kv_norm, wq_t, wkv_t, *rope)


def _mla_scores(q, k_ref, t, tq):
    lo = t * tq
    own = slice(lo, lo + tq)
    scores = [(_dot_nt(q, k_ref[own, :]), own)]
    if t > 0:
        scores.append((_dot_nt(q, k_ref[0:lo, :]), slice(0, lo)))
    return scores


def _mla_softmax(scores):
    s_own = scores[0][0] * MLA_SCALE
    rows = lax.broadcasted_iota(jnp.int32, s_own.shape, 0)
    cols = lax.broadcasted_iota(jnp.int32, s_own.shape, 1)
    s_own = jnp.where(cols <= rows, s_own, -jnp.inf)
    mx = jnp.max(s_own, axis=-1, keepdims=True)
    if len(scores) == 1:
        e_own = jnp.exp(s_own - mx)
        return [(e_own * (1.0 / jnp.sum(e_own, axis=-1, keepdims=True)), scores[0][1])]
    s_pre = scores[1][0] * MLA_SCALE
    mx = jnp.maximum(mx, jnp.max(s_pre, axis=-1, keepdims=True))
    e_own, e_pre = jnp.exp(s_own - mx), jnp.exp(s_pre - mx)
    inv = 1.0 / (jnp.sum(e_own, axis=-1, keepdims=True) + jnp.sum(e_pre, axis=-1, keepdims=True))
    return [(e_pre * inv, scores[1][1]), (e_own * inv, scores[0][1])]


def _mla_attn_fwd(q, k, v, name):
    s_dim = q.shape[0]
    tq = MLA_QUERY_TILE

    def body(q_ref, k_ref, v_ref, o_ref):
        n_tiles = s_dim // tq
        tile_of = lambda t: slice(t * tq, (t + 1) * tq)
        def weighted_values(t, probs):
            o = None
            for p, keys in probs:
                part = _dot_nn(p, v_ref[keys, :])
                o = part if o is None else o + part
            o_ref[tile_of(t), :] = o.astype(BF16)

        scores = _mla_scores(q_ref[tile_of(0), :], k_ref, 0, tq)
        probs = None
        for t in range(n_tiles):
            ahead = _mla_scores(q_ref[tile_of(t + 1), :], k_ref, t + 1, tq) if t + 1 < n_tiles else None
            if probs is not None:
                weighted_values(t - 1, probs)
            probs = [(p.astype(BF16), keys) for p, keys in _mla_softmax(scores)]
            scores = ahead
        weighted_values(n_tiles - 1, probs)

    head = pl.BlockSpec((s_dim, HEAD_PAD), lambda h: (0, h))
    return _pcall(
        body, out_shape=jax.ShapeDtypeStruct(q.shape, BF16), grid=(MLA_HEADS,),
        in_specs=[head, head, head], out_specs=head, name=name,
        compiler_params=_params(("parallel",), 4 * [((s_dim, HEAD_PAD), BF16)], extra=4 * tq * s_dim * 4),
    )(q, k, v)


def _mla_attn_bwd(q, k, v, d_o, name):
    s_dim = q.shape[0]
    tq = MLA_QUERY_TILE

    def body(q_ref, k_ref, v_ref, do_ref, dq_ref, dk_ref, dv_ref):
        dk_ref[...] = jnp.zeros_like(dk_ref)
        dv_ref[...] = jnp.zeros_like(dv_ref)
        n_tiles = s_dim // tq
        tile_of = lambda t: slice(t * tq, (t + 1) * tq)

        def products(t):
            scores = _mla_scores(q_ref[tile_of(t), :], k_ref, t, tq)
            dot = do_ref[tile_of(t), :].astype(BF16)
            return scores, [_dot_nt(dot, v_ref[keys, :]) for _, keys in scores]

        def gradients_of_scores(scores, dps):
            probs = _mla_softmax(scores)
            dp_of = {(keys.start, keys.stop): dp for (_, keys), dp in zip(scores, dps)}
            terms = [(p, keys, dp_of[keys.start, keys.stop]) for p, keys in probs]
            row = None
            for p, _, dp in terms:
                part = jnp.sum(p * dp, axis=-1, keepdims=True)
                row = part if row is None else row + part
            return [((p * (dp - row) * MLA_SCALE).astype(BF16), p.astype(BF16), keys) for p, keys, dp in terms]

        def accumulate(t, terms):
            qt = q_ref[tile_of(t), :]
            dot = do_ref[tile_of(t), :].astype(BF16)
            dq = None
            for dsb, pb, keys in terms:
                part = _dot_nn(dsb, k_ref[keys, :])
                dq = part if dq is None else dq + part
                dk_ref[keys, :] += _dot_tn(dsb, qt)
                dv_ref[keys, :] += _dot_tn(pb, dot)
            dq_ref[tile_of(t), :] = dq

        ready = products(0)
        terms = None
        for t in range(n_tiles):
            ahead = products(t + 1) if t + 1 < n_tiles else None
            if terms is not None:
                accumulate(t - 1, terms)
            terms = gradients_of_scores(*ready)
            ready = ahead
        accumulate(n_tiles - 1, terms)

    head = pl.BlockSpec((s_dim, HEAD_PAD), lambda h: (0, h))
    out = jax.ShapeDtypeStruct(q.shape, F32)
    return _pcall(
        body, out_shape=(out, out, out), grid=(MLA_HEADS,),
        in_specs=[head, head, head, head], out_specs=(head, head, head), name=name,
        compiler_params=_params(("parallel",), 3 * [((s_dim, HEAD_PAD), BF16)] + 4 * [((s_dim, HEAD_PAD), F32)],
                                extra=6 * tq * s_dim * 4),
    )(q, k, v, d_o)


def _mla_qkv_bwd(dq, dk, dv, lat, q_norm, kv_norm, wq_t, wkv_t, rope, name):
    s_dim = lat.shape[0]
    width = MLA_HEADS * HEAD_PAD
    tm = 256

    def body(dq_ref, dk_ref, dv_ref, lat_ref, qg_ref, kg_ref, wq_ref, wkv_ref, c_ref, s1_ref, s2_ref,
             dqp_ref, dkv_ref, dlat_ref, dqg_ref, dkg_ref):
        @pl.when(pl.program_id(0) == 0)
        def _():
            dqg_ref[...] = jnp.zeros_like(dqg_ref)
            dkg_ref[...] = jnp.zeros_like(dkg_ref)

        c, s1, s2 = c_ref[...], s1_ref[...], s2_ref[...]
        lane = lax.broadcasted_iota(jnp.int32, (tm, HEAD_PAD), 1)
        low = lane < QK_NOPE
        rot = (lane >= QK_NOPE) & (lane < QK_NOPE + QK_ROPE)
        dkrr = jnp.zeros((tm, HEAD_PAD), F32)
        for h in range(MLA_HEADS):
            sl = slice(h * HEAD_PAD, (h + 1) * HEAD_PAD)
            dqp_ref[:, sl] = _rope_t(dq_ref[:, sl], c, s1, s2).astype(BF16)
            dkh = dk_ref[:, sl]
            dkv_ref[:, sl] = jnp.where(low, dkh, dv_ref[:, sl]).astype(BF16)
            dkrr = dkrr + jnp.where(rot, dkh, 0.0)
        dqn = _dot_nn(dqp_ref[...], wq_ref[...])
        dkvn = _dot_nn(dkv_ref[...], wkv_ref[...])
        cq = lat_ref[:, :Q_LORA]
        ckv = lat_ref[:, Q_LORA:Q_LORA + KV_LORA]
        rq, rkv = _rstd(cq), _rstd(ckv)
        dqg_ref[...] += jnp.sum(dqn * cq * rq, axis=0, keepdims=True)
        dkg_ref[...] += jnp.sum(dkvn * ckv * rkv, axis=0, keepdims=True)
        dlat_ref[:, :Q_LORA] = _rms_bwd(cq, rq, dqn * qg_ref[...])
        dlat_ref[:, Q_LORA:Q_LORA + KV_LORA] = _rms_bwd(ckv, rkv, dkvn * kg_ref[...])
        dlat_ref[:, Q_LORA + KV_LORA:] = _rope_t(dkrr, c, s1, s2)

    row = lambda n: pl.BlockSpec((tm, n), lambda i: (i, 0))
    full = lambda a: pl.BlockSpec(a.shape, lambda i: (0, 0))
    wide = jax.ShapeDtypeStruct((s_dim, width), BF16)
    return _pcall(
        body,
        out_shape=(wide, wide, jax.ShapeDtypeStruct((s_dim, LAT_PAD), F32),
                   jax.ShapeDtypeStruct(q_norm.shape, F32), jax.ShapeDtypeStruct(kv_norm.shape, F32)),
        grid=(s_dim // tm,),
        in_specs=[row(width), row(width), row(width), row(LAT_PAD), full(q_norm), full(kv_norm), full(wq_t), full(wkv_t),
                  row(HEAD_PAD), row(HEAD_PAD), row(HEAD_PAD)],
        out_specs=(row(width), row(width), row(LAT_PAD), full(q_norm), full(kv_norm)), name=name,
        compiler_params=_params(("arbitrary",), 3 * [((tm, width), F32)] + [((tm, LAT_PAD), F32), (wq_t.shape, BF16),
                                                                           (wkv_t.shape, BF16)] + 2 * [((tm, width), BF16)],
                                extra=2 * tm * width * 4),
    )(dq, dk, dv, lat, q_norm, kv_norm, wq_t, wkv_t, *rope)


def _t5_bucket(dist):
    max_exact = N_BUCKETS // 2
    d = jnp.maximum(dist, 1).astype(F32)
    large = max_exact + (jnp.log(d / max_exact) / math.log(MAX_DISTANCE / max_exact)
                         * (N_BUCKETS - max_exact)).astype(jnp.int32)
    large = jnp.minimum(large, N_BUCKETS - 1)
    return jnp.where(dist < max_exact, dist, large)


def _dil_buckets(dilation):
    iq = jnp.arange(DIL_BLOCK)[:, None]
    ik = jnp.arange(2 * DIL_BLOCK)[None, :]
    return _t5_bucket(jnp.maximum(DIL_BLOCK + iq - ik, 0) * dilation)


def _dil_logits(qh, kb, bias_h, first, span):
    if first:
        s = _dot_nt(qh, kb) * DIL_SCALE + bias_h[:, DIL_BLOCK:]
        rel = lax.broadcasted_iota(jnp.int32, s.shape, 0) - lax.broadcasted_iota(jnp.int32, s.shape, 1)
    else:
        s = _dot_nt(qh, kb) * DIL_SCALE + bias_h
        rel = DIL_BLOCK + lax.broadcasted_iota(jnp.int32, s.shape, 0) - lax.broadcasted_iota(jnp.int32, s.shape, 1)
    return jnp.where((rel >= 0) & (rel <= span), s, -jnp.inf)


def _dil_blocks(s_dim, dilation):
    rows = s_dim // dilation
    for r in range(dilation):
        for n in range(rows // DIL_BLOCK):
            lo = r * rows + n * DIL_BLOCK
            keys = slice(lo, lo + DIL_BLOCK) if n == 0 else slice(lo - DIL_BLOCK, lo + DIL_BLOCK)
            start = r + n * DIL_BLOCK * dilation
            tokens = slice(start, start + DIL_BLOCK) if dilation == 1 else pl.ds(start, DIL_BLOCK, stride=dilation)
            yield n == 0, slice(lo, lo + DIL_BLOCK), keys, tokens


def _dil_views(s_dim):
    col = lambda which: pl.BlockSpec((s_dim, HEAD_PAD), lambda p: (0, which * DIL_PAIRS + p))
    nat = pl.BlockSpec((s_dim, HEAD_PAD), lambda p: (0, p))
    bias = pl.BlockSpec((2, DIL_BLOCK, 2 * DIL_BLOCK), lambda p: (p, 0, 0))
    return col, nat, bias


def _dil_attn_fwd(qkv, bias, dilation, span, name):
    s_dim = qkv.shape[0]
    d_dim = DIL_HEADS * DIL_HEAD_DIM
    col, nat, bias_spec = _dil_views(s_dim)

    def body(q_ref, k_ref, v_ref, b_ref, o_ref, l_ref):
        lane = lax.broadcasted_iota(jnp.int32, (DIL_BLOCK, HEAD_PAD), 1)
        klane = lax.broadcasted_iota(jnp.int32, (2 * DIL_BLOCK, HEAD_PAD), 1)
        blocks = list(_dil_blocks(s_dim, dilation))
        for g0 in range(0, len(blocks), DIL_GROUPED):
            group = blocks[g0:g0 + DIL_GROUPED]
            logits = [_dil_logits(jnp.where((lane < DIL_HEAD_DIM) == (h == 0), q_ref[blk, :], 0), k_ref[keys, :], b_ref[h],
                                  first, span) for first, blk, keys, _ in group for h in range(2)]
            soft = []
            for lg in logits:
                mx = jnp.max(lg, axis=-1, keepdims=True)
                e = jnp.exp(lg - mx)
                tot = jnp.sum(e, axis=-1, keepdims=True)
                soft.append(((e * (1.0 / tot)).astype(BF16), mx + jnp.log(tot)))
            for i, (_, _, keys, tokens) in enumerate(group):
                vb = v_ref[keys, :]
                o_acc = jnp.zeros((DIL_BLOCK, HEAD_PAD), F32)
                lse_acc = jnp.zeros((DIL_BLOCK, HEAD_PAD), F32)
                for h in range(2):
                    p, lse = soft[2 * i + h]
                    kmine = (klane[:vb.shape[0]] < DIL_HEAD_DIM) == (h == 0)
                    o_acc = o_acc + _dot_nn(p, jnp.where(kmine, vb, 0))
                    lse_acc = jnp.where((lane < DIL_HEAD_DIM) == (h == 0), lse, lse_acc)
                o_ref[tokens, :] = o_acc
                l_ref[tokens, :] = lse_acc

    out = jax.ShapeDtypeStruct((s_dim, d_dim), F32)
    return _pcall(
        body, out_shape=(out, out), grid=(DIL_PAIRS,),
        in_specs=[col(0), col(1), col(2), bias_spec], out_specs=(nat, nat), name=name,
        compiler_params=_params(("parallel",), 3 * [((s_dim, HEAD_PAD), BF16)] + 2 * [((s_dim, HEAD_PAD), F32)]
                                + [((2, DIL_BLOCK, 2 * DIL_BLOCK), F32)], extra=2**21),
    )(qkv, qkv, qkv, bias)


def _dil_mix(lses, outs, name):
    s_dim, d_dim = outs[0].shape
    tm = TOKEN_TILE
    ng = len(outs)

    def body(*refs):
        ls = [refs[g][...] for g in range(ng)]
        mx = ls[0]
        for g in range(1, ng):
            mx = jnp.maximum(mx, ls[g])
        es = [jnp.exp(l - mx) for l in ls]
        tot = es[0]
        for g in range(1, ng):
            tot = tot + es[g]
        o = None
        for g in range(ng):
            al = es[g] / tot
            refs[2 * ng + g][...] = al
            t = al * refs[ng + g][...]
            o = t if o is None else o + t
        refs[3 * ng][...] = o
        refs[3 * ng + 1][...] = o.astype(BF16)

    row = pl.BlockSpec((tm, d_dim), lambda i: (i, 0))
    f = jax.ShapeDtypeStruct((s_dim, d_dim), F32)
    res = _pcall(
        body, out_shape=tuple(ng * [f] + [f, jax.ShapeDtypeStruct((s_dim, d_dim), BF16)]), grid=(s_dim // tm,),
        in_specs=2 * ng * [row], out_specs=tuple((ng + 2) * [row]), name=name,
        compiler_params=_params(("parallel",), (3 * ng + 2) * [((tm, d_dim), F32)], extra=4 * tm * d_dim * 4),
    )(*lses, *outs)
    return res[:ng], res[ng], res[ng + 1]


def _dil_attn_bwd(qkv, bias, d_o, o_mix, alpha, lse, dilation, span, name):
    s_dim = qkv.shape[0]
    d_dim = DIL_HEADS * DIL_HEAD_DIM
    col, nat, bias_spec = _dil_views(s_dim)

    def body(q_ref, k_ref, v_ref, b_ref, do_ref, om_ref, al_ref, l_ref, dq_ref, dk_ref, dv_ref, db_ref, dk_acc, dv_acc):
        db_ref[...] = jnp.zeros_like(db_ref)
        dk_acc[...] = jnp.zeros_like(dk_acc)
        dv_acc[...] = jnp.zeros_like(dv_acc)
        lane = lax.broadcasted_iota(jnp.int32, (DIL_BLOCK, HEAD_PAD), 1)
        klane = lax.broadcasted_iota(jnp.int32, (2 * DIL_BLOCK, HEAD_PAD), 1)
        blocks = list(_dil_blocks(s_dim, dilation))
        heads = [(lane < DIL_HEAD_DIM) == (h == 0) for h in range(2)]
        for g0 in range(0, len(blocks), DIL_GROUPED):
            group = blocks[g0:g0 + DIL_GROUPED]
            staged = []
            for first, blk, kv_rows, tokens in group:
                qb, kb, vb = q_ref[blk, :], k_ref[kv_rows, :], v_ref[kv_rows, :]
                dog = al_ref[tokens, :] * do_ref[tokens, :]
                row_term = dog * om_ref[tokens, :]
                lse_b = l_ref[tokens, :]
                for h in range(2):
                    qh = jnp.where(heads[h], qb, 0)
                    dogh = jnp.where(heads[h], dog, 0.0).astype(BF16)
                    staged.append((_dil_logits(qh, kb, b_ref[h], first, span), _dot_nt(dogh, vb), qh, dogh,
                                   jnp.max(jnp.where(heads[h], lse_b, -jnp.inf), axis=-1, keepdims=True),
                                   jnp.sum(jnp.where(heads[h], row_term, 0.0), axis=-1, keepdims=True)))
            grads = []
            for i, (logits, dp, qh, dogh, lse_h, row) in enumerate(staged):
                p = jnp.exp(logits - lse_h)
                ds = p * (dp - row)
                if group[i // 2][0]:
                    db_ref[i % 2, :, DIL_BLOCK:] += ds
                else:
                    db_ref[i % 2] += ds
                grads.append(((ds * DIL_SCALE).astype(BF16), p.astype(BF16), qh, dogh))
            for i, (_, blk, kv_rows, _) in enumerate(group):
                kb = k_ref[kv_rows, :]
                dq_acc = jnp.zeros((DIL_BLOCK, HEAD_PAD), F32)
                dk_blk = jnp.zeros((kb.shape[0], HEAD_PAD), F32)
                dv_blk = jnp.zeros((kb.shape[0], HEAD_PAD), F32)
                for h in range(2):
                    dsb, pb, qh, dogh = grads[2 * i + h]
                    kmine = (klane[:kb.shape[0]] < DIL_HEAD_DIM) == (h == 0)
                    dq_acc = dq_acc + _dot_nn(dsb, jnp.where(kmine, kb, 0))
                    dk_blk = dk_blk + _dot_tn(dsb, qh)
                    dv_blk = dv_blk + _dot_tn(pb, dogh)
                dq_ref[blk, :] = dq_acc.astype(BF16)
                dk_acc[kv_rows, :] += dk_blk
                dv_acc[kv_rows, :] += dv_blk
        dk_ref[...] = dk_acc[...].astype(BF16)
        dv_ref[...] = dv_acc[...].astype(BF16)

    grad = jax.ShapeDtypeStruct((s_dim, d_dim), BF16)
    return _pcall(
        body, out_shape=(grad, grad, grad, jax.ShapeDtypeStruct(bias.shape, F32)), grid=(DIL_PAIRS,),
        in_specs=[col(0), col(1), col(2), bias_spec, nat, nat, nat, nat],
        out_specs=(nat, nat, nat, bias_spec), name=name,
        scratch_shapes=[pltpu.VMEM((s_dim, HEAD_PAD), F32), pltpu.VMEM((s_dim, HEAD_PAD), F32)],
        compiler_params=_params(("parallel",), 6 * [((s_dim, HEAD_PAD), BF16)] + 4 * [((s_dim, HEAD_PAD), F32)]
                                + 2 * [((2, DIL_BLOCK, 2 * DIL_BLOCK), F32)], extra=2 * s_dim * HEAD_PAD * 4 + 2**21),
    )(qkv, qkv, qkv, bias, d_o, o_mix, alpha, lse)


def _bias_reduce(dbias, buckets, name):
    n_heads = dbias.shape[0]

    def body(db_ref, bk_ref, o_ref):
        ds, bk = db_ref[0], bk_ref[0]
        lane = lax.broadcasted_iota(jnp.int32, (8, HEAD_PAD), 1)
        acc = jnp.zeros((8, HEAD_PAD), F32)
        for b in range(N_BUCKETS):
            acc = jnp.where(lane == b, jnp.sum(jnp.where(bk == b, ds, 0.0)), acc)
        o_ref[0] = acc

    blk = (1, DIL_BLOCK, 2 * DIL_BLOCK)
    return _pcall(
        body, out_shape=jax.ShapeDtypeStruct((n_heads, 8, HEAD_PAD), F32), grid=(n_heads,),
        in_specs=[pl.BlockSpec(blk, lambda h: (h, 0, 0)), pl.BlockSpec(blk, lambda h: (h // DIL_HEADS, 0, 0))],
        out_specs=pl.BlockSpec((1, 8, HEAD_PAD), lambda h: (h, 0, 0)), name=name,
        compiler_params=_params(("parallel",), [(blk, F32), (blk, jnp.int32)], extra=2**20),
    )(dbias, buckets)


def _loss_grad(y, target, name):
    s_dim, d_dim = y.shape
    tm = TOKEN_TILE

    def body(y_ref, t_ref, dy_ref, l_ref):
        @pl.when(pl.program_id(0) == 0)
        def _():
            l_ref[...] = jnp.zeros_like(l_ref)

        err = y_ref[...] - t_ref[...]
        dy_ref[...] = err / d_dim
        sq = (err * err).reshape(tm // 8, 8, d_dim)
        l_ref[...] += 0.5 * jnp.sum(sq, axis=0) / d_dim

    row = pl.BlockSpec((tm, d_dim), lambda i: (i, 0))
    acc = pl.BlockSpec((8, d_dim), lambda i: (0, 0))
    return _pcall(
        body, out_shape=(jax.ShapeDtypeStruct((s_dim, d_dim), F32), jax.ShapeDtypeStruct((8, d_dim), F32)),
        grid=(s_dim // tm,), in_specs=[row, row], out_specs=(row, acc), name=name,
        compiler_params=_params(("arbitrary",), 3 * [((tm, d_dim), F32)], extra=2 * tm * d_dim * 4),
    )(y, target)


def _mod_fwd(c_all, w_mod, b_loc, name):
    depth, d_dim, n = w_mod.shape
    nb = c_all.shape[0]

    def body(c_ref, w_ref, b_ref, o_ref, s_ref):
        cv = c_ref[...]
        sc = cv * jax.nn.sigmoid(cv)
        s_ref[...] = sc
        o_ref[0] = _dot_nn(sc.astype(BF16), w_ref[0].astype(BF16)) + b_ref[0]

    return _pcall(
        body, out_shape=(jax.ShapeDtypeStruct((depth, nb, n), F32), jax.ShapeDtypeStruct((nb, d_dim), F32)), grid=(depth,),
        in_specs=[pl.BlockSpec((nb, d_dim), lambda i: (0, 0)), pl.BlockSpec((1, d_dim, n), lambda i: (i, 0, 0)),
                  pl.BlockSpec((1, 1, n), lambda i: (i, 0, 0))],
        out_specs=(pl.BlockSpec((1, nb, n), lambda i: (i, 0, 0)), pl.BlockSpec((nb, d_dim), lambda i: (0, 0))), name=name,
        compiler_params=_params(("arbitrary",), [((1, d_dim, n), F32)], extra=d_dim * n * 2 + 2**20),
    )(c_all, w_mod, b_loc.reshape(depth, 1, n))


def _sum_parts(parts, name, transpose=False):
    _, rows, cols = parts.shape
    unit = 128 if transpose else 16
    budget = (7 if transpose else 3) * 2**20
    fits = [t for t in range(unit, rows // 2 + 1, unit) if rows % t == 0 and NDEV * t * cols * parts.dtype.itemsize <= budget]
    tr = max(fits) if fits else rows

    def body(p_ref, o_ref):
        acc = p_ref[0].astype(F32)
        for k in range(1, NDEV):
            acc = acc + p_ref[k].astype(F32)
        o_ref[...] = acc.T if transpose else acc

    out_shape, out_block = ((cols, rows), (cols, tr)) if transpose else ((rows, cols), (tr, cols))
    return _pcall(
        body, out_shape=jax.ShapeDtypeStruct(out_shape, F32), grid=(rows // tr,),
        in_specs=[pl.BlockSpec((NDEV, tr, cols), lambda i: (0, i, 0))],
        out_specs=pl.BlockSpec(out_block, (lambda i: (0, i)) if transpose else (lambda i: (i, 0))),
        name=name, compiler_params=_params(("parallel",), [((NDEV, tr, cols), parts.dtype), (out_block, F32)], extra=2**22),
    )(parts)


def _adamw(w, g, m, v, name):
    shape = w.shape
    cols = shape[-1]
    rows = math.prod(shape[:-1])
    tr = rows
    for cand in (2048, 1024, 512, 256, 128, 64, 32, 16, 8):
        if rows % cand == 0 and rows > cand and cand * cols * 4 <= 2**21:
            tr = cand
            break

    def body(w_ref, g_ref, m_ref, v_ref, d_ref, mo_ref, vo_ref):
        gv = g_ref[...]
        mn = ADAM_B1 * m_ref[...] + (1.0 - ADAM_B1) * gv
        vn = ADAM_B2 * v_ref[...] + (1.0 - ADAM_B2) * (gv * gv)
        m_hat = mn / (1.0 - ADAM_B1 ** ADAM_STEP)
        v_hat = vn / (1.0 - ADAM_B2 ** ADAM_STEP)
        d_ref[...] = -ADAM_LR * (m_hat / (jnp.sqrt(v_hat) + ADAM_EPS) + ADAM_WD * w_ref[...])
        mo_ref[...] = mn
        vo_ref[...] = vn

    blk = pl.BlockSpec((tr, cols), lambda i: (i, 0))
    out = jax.ShapeDtypeStruct((rows, cols), F32)
    res = _pcall(
        body, out_shape=(out, out, out), grid=(rows // tr,), in_specs=4 * [blk], out_specs=(blk, blk, blk), name=name,
        compiler_params=_params(("parallel",), 7 * [((tr, cols), F32)], extra=4 * tr * cols * 4),
    )(*(a.reshape(rows, cols) for a in (w, g, m, v)))
    return tuple(r.reshape(shape) for r in res)


def _peers():
    x, y, c = lax.axis_index("x"), lax.axis_index("y"), lax.axis_index("c")
    flip = lambda v, f: 1 - v if f else v
    peers = []
    for f in range(1, NDEV):
        px, py, pc = flip(x, f & 4), flip(y, f & 2), flip(c, f & 1)
        peers.append(((px, py, pc), 4 * px + 2 * py + pc))
    return (x, y, c), 4 * x + 2 * y + c, peers


def _places():
    x, y, c = lax.axis_index("x"), lax.axis_index("y"), lax.axis_index("c")
    place = lambda px, py, pc: ((px, py, pc), 4 * px + 2 * py + pc)
    return place(x, y, c), place(x, y, 1 - c), [place(1 - x, y, c), place(x, 1 - y, c), place(1 - x, 1 - y, c)]


def _exchange(arrs, gather, name):
    n = len(arrs)
    hbm = pl.BlockSpec(memory_space=pltpu.HBM)
    if gather:
        out_shape = [jax.ShapeDtypeStruct((NDEV * a.shape[0], a.shape[1]), a.dtype) for a in arrs]
    else:
        out_shape = [jax.ShapeDtypeStruct((NDEV, a.shape[0] // NDEV, a.shape[1]), a.dtype) for a in arrs]

    def body(*refs):
        ins, outs = refs[:n], refs[n:2 * n]
        send_sems, recv_sems, local_sems = refs[2 * n:]
        me_pos, me, peers = _peers()
        local = []
        for k in range(n):
            rows = arrs[k].shape[0] if gather else arrs[k].shape[0] // NDEV
            if gather:
                src_of = lambda idx: ins[k]
                dst_of = lambda idx: outs[k].at[pl.ds(me * rows, rows)]
                mine = (ins[k], outs[k].at[pl.ds(me * rows, rows)])
            else:
                src_of = lambda idx: ins[k].at[pl.ds(idx * rows, rows)]
                dst_of = lambda idx: outs[k].at[me]
                mine = (ins[k].at[pl.ds(me * rows, rows)], outs[k].at[me])
            cp = pltpu.make_async_copy(mine[0], mine[1], local_sems.at[k])
            cp.start()
            local.append(cp)
            for pos, idx in peers:
                pltpu.make_async_remote_copy(src_ref=src_of(idx), dst_ref=dst_of(idx), send_sem=send_sems.at[k],
                                             recv_sem=recv_sems.at[k], device_id=pos, device_id_type=MESH).start()
        for k in range(n):
            rows = arrs[k].shape[0] if gather else arrs[k].shape[0] // NDEV
            sent = ins[k].at[pl.ds(0, (NDEV - 1) * rows)] if not gather else outs[k].at[pl.ds(0, (NDEV - 1) * rows)]
            got = outs[k].at[pl.ds(0, (NDEV - 1) * rows)] if gather else outs[k].at[pl.ds(0, NDEV - 1)]
            pltpu.make_async_remote_copy(src_ref=sent, dst_ref=sent, send_sem=send_sems.at[k], recv_sem=recv_sems.at[k],
                                         device_id=me_pos, device_id_type=MESH).wait_send()
            pltpu.make_async_remote_copy(src_ref=got, dst_ref=got, send_sem=send_sems.at[k], recv_sem=recv_sems.at[k],
                                         device_id=me_pos, device_id_type=MESH).wait_recv()
            local[k].wait()

    return pl.pallas_call(
        body, out_shape=out_shape, in_specs=n * [hbm], out_specs=n * [hbm], name=name,
        scratch_shapes=[pltpu.SemaphoreType.DMA((n,)), pltpu.SemaphoreType.DMA((n,)), pltpu.SemaphoreType.DMA((n,))],
        compiler_params=pltpu.CompilerParams(has_side_effects=True),
    )(*arrs)


_HBM = pl.BlockSpec(memory_space=pltpu.HBM)
_SEM = pl.BlockSpec(memory_space=pltpu.SEMAPHORE)
_DATAFLOW = pltpu.SideEffectType.DATAFLOW_SIDE_EFFECTING


def _split_start(srcs, groups, gather, name, after=None):
    n = len(srcs)
    if gather:
        lands = [lax.empty((NDEV * a.shape[0], a.shape[1]), a.dtype) for a in srcs]
    else:
        lands = [lax.empty((NDEV, a.shape[0] // NDEV, a.shape[1]), a.dtype) for a in srcs]
    n_sem = 3 * len(groups)
    extra = [] if after is None else [after]
    n_in = 2 * n + len(extra)

    def body(*refs):
        src_refs, land_refs = refs[:n], refs[n:2 * n]
        sems = refs[n_in:n_in + n_sem]
        token = refs[-1]
        (_, my), sibling, chips = _places()
        _, _, peers = _peers()
        targets = [sibling] + chips if gather else peers
        for g, members in enumerate(groups):
            for j, k in enumerate(members):
                _own_copy(src_refs[k], land_refs[k], sems[3 * g + 2].at[j], my, gather).start()
        for g, members in enumerate(groups):
            for j, k in enumerate(members):
                rows = srcs[k].shape[0] if gather else srcs[k].shape[0] // NDEV
                for pos, idx in targets:
                    src = src_refs[k] if gather else src_refs[k].at[pl.ds(idx * rows, rows)]
                    dst = land_refs[k].at[pl.ds(my * rows, rows)] if gather else land_refs[k].at[my]
                    pltpu.make_async_remote_copy(src_ref=src, dst_ref=dst, send_sem=sems[3 * g].at[j],
                                                 recv_sem=sems[3 * g + 1].at[j], device_id=pos, device_id_type=MESH).start()
        token[...] = jnp.zeros_like(token)

    out_shape = []
    for members in groups:
        out_shape += 3 * [pltpu.SemaphoreType.DMA((len(members),))]
    out_shape += [pltpu.HBM(a.shape, a.dtype) for a in srcs] + [pltpu.HBM(a.shape, a.dtype) for a in lands]
    out_shape.append(jax.ShapeDtypeStruct((8, 128), F32))
    res = pl.pallas_call(
        body, name=name, out_shape=tuple(out_shape), in_specs=2 * n * [_HBM] + len(extra) * [pl.BlockSpec(memory_space=pl.ANY)],
        out_specs=tuple(n_sem * [_SEM] + 2 * n * [_HBM] + [pl.BlockSpec(memory_space=pltpu.VMEM)]),
        input_output_aliases={i: n_sem + i for i in range(2 * n)},
        compiler_params=pltpu.CompilerParams(has_side_effects=_DATAFLOW),
    )(*[pltpu.with_memory_space_constraint(a, pltpu.HBM) for a in list(srcs) + lands], *extra)
    sems = [tuple(res[3 * g:3 * g + 3]) for g in range(len(groups))]
    return sems, list(res[n_sem:n_sem + n]), list(res[n_sem + n:n_sem + 2 * n]), res[-1]


def _own_copy(src_ref, land_ref, sem, my, gather):
    if gather:
        rows = src_ref.shape[0]
        return pltpu.make_async_copy(src_ref, land_ref.at[pl.ds(my * rows, rows)], sem)
    rows = src_ref.shape[0] // NDEV
    return pltpu.make_async_copy(src_ref.at[pl.ds(my * rows, rows)], land_ref.at[my], sem)


def _wait_all(land_ref, blocks_per_dev, copies, send_sem, recv_sem, me_pos):
    part = land_ref.at[pl.ds(0, copies * blocks_per_dev)]
    pltpu.make_async_remote_copy(src_ref=part, dst_ref=part, send_sem=send_sem, recv_sem=recv_sem,
                                 device_id=me_pos, device_id_type=MESH).wait()


def _gather_forward(sems, srcs, lands, after, name):
    n = len(srcs)

    def body(*refs):
        land_refs = refs[n:2 * n]
        send_a, recv_a = refs[2 * n], refs[2 * n + 1]
        send_b, recv_b = refs[2 * n + 3], refs[2 * n + 4]
        token = refs[-1]
        (me_pos, _), sibling, chips = _places()
        for j in range(n):
            _wait_all(land_refs[j], lands[j].shape[0] // NDEV, 1 + OTHER_CHIPS, send_a.at[j], recv_a.at[j], me_pos)
        for j in range(n):
            rows = lands[j].shape[0] // NDEV
            for _, idx in chips:
                block = land_refs[j].at[pl.ds(idx * rows, rows)]
                pltpu.make_async_remote_copy(src_ref=block, dst_ref=block, send_sem=send_b.at[j], recv_sem=recv_b.at[j],
                                             device_id=sibling[0], device_id_type=MESH).start()
        token[...] = jnp.zeros_like(token)

    res = pl.pallas_call(
        body, name=name,
        out_shape=(pltpu.SemaphoreType.DMA((n,)), pltpu.SemaphoreType.DMA((n,)))
        + tuple(pltpu.HBM(a.shape, a.dtype) for a in list(srcs) + list(lands)) + (jax.ShapeDtypeStruct((8, 128), F32),),
        in_specs=2 * n * [_HBM] + [_SEM, _SEM, pl.BlockSpec(memory_space=pl.ANY)],
        out_specs=tuple([_SEM, _SEM] + 2 * n * [_HBM] + [pl.BlockSpec(memory_space=pltpu.VMEM)]),
        input_output_aliases={i: 2 + i for i in range(2 * n)},
        compiler_params=pltpu.CompilerParams(has_side_effects=_DATAFLOW),
    )(*srcs, *lands, sems[0], sems[1], after)
    return (res[0], res[1]), list(res[2:2 + n]), list(res[2 + n:2 + 2 * n]), res[-1]


def _split_wait(sems, srcs, lands, after, copies, gather, name):
    n = len(srcs)

    def body(*refs):
        src_refs, land_refs = refs[:n], refs[n:2 * n]
        send_sem, recv_sem, local_sem = refs[2 * n], refs[2 * n + 1], refs[2 * n + 2]
        (me_pos, my), _, _ = _places()
        for j in range(n):
            _wait_all(land_refs[j], lands[j].shape[0] // NDEV, copies, send_sem.at[j], recv_sem.at[j], me_pos)
            _own_copy(src_refs[j], land_refs[j], local_sem.at[j], my, gather).wait()

    res = pl.pallas_call(
        body, name=name, out_shape=tuple(pltpu.HBM(a.shape, a.dtype) for a in list(srcs) + list(lands)),
        in_specs=2 * n * [_HBM] + [_SEM, _SEM, _SEM, pl.BlockSpec(memory_space=pl.ANY)], out_specs=tuple(2 * n * [_HBM]),
        input_output_aliases={i: i for i in range(2 * n)},
        compiler_params=pltpu.CompilerParams(has_side_effects=_DATAFLOW),
    )(*srcs, *lands, sems[0], sems[1], sems[2], after)
    return list(res[n:])


def _chained(gate, mid, after):
    return gate if mid is None else gate + mid(after)[:1, :1]


def _ffn_fwd(x, norms, mod, w, mid=None):
    (pre_g, post_g), (shift, scale, gate), (wg_t, wu_t, wd) = norms, mod, w
    if not callable(wd):
        hn, g, u, a, x_out, f = _ffn_fwd_fused(x, pre_g, scale, shift, post_g, _chained(gate, mid, x), wg_t, wu_t, wd, "ffn_fwd")
        return x_out, (x, hn, g, u, a, f), (wg_t, wu_t, wd)
    hn, g, u, a = _ffn_up(x, pre_g, scale, shift, wg_t, wu_t, "ffn_up")
    wd = wd(a)
    x_out, f = _mm_post(a, wd, x, post_g, _chained(gate, mid, a), FFN_RES, "ffn_down")
    return x_out, (x, hn, g, u, a, f), (wg_t, wu_t, wd)


def _ffn_bwd(dx_out, saved, norms, mod, w, send=None):
    (pre_g, post_g), (_, scale, gate), (wg_t, wu_t, wd) = norms, mod, w
    x, hn, g, u, a, f = saved
    d_model = x.shape[1]
    if send is None:
        df, dg, du, dx, dgate, dpost, dshift, dscale, dpre = _ffn_bwd_fused(dx_out, saved, pre_g, post_g, scale, gate,
                                                                            wg_t, wu_t, wd, "ffn_bwd")
        return dx, (dpre, dpost), (dshift, dscale, dgate), tuple(_ffn_dw(dg, du, a, hn, df, "ffn_dw3"))
    sent = send
    df, dgate, dpost = _post_bwd(dx_out, f, post_g, gate, FFN_RES, "ffn_post_bwd")
    dwd = _mm([(a, df)], "tn", BF16, 256, d_model, "ffn_dw")
    dg, du = _ffn_dgu(df, wd, g, u, "ffn_dgu", after=sent(2, dwd))
    dwg_t = _mm([(dg, hn)], "tn", BF16, 256, d_model, "ffn_dw")
    dwu_t = _mm([(du, hn)], "tn", BF16, 256, d_model, "ffn_dw", after=sent(0, dwg_t))
    dhn = _mm([(dg, wg_t), (du, wu_t)], "nn", F32, TOKEN_TILE, d_model, "ffn_dhn", after=sent(1, dwu_t))
    dx, dshift, dscale, dpre = _prenorm_bwd(dx_out, [dhn], x, pre_g, scale, "prenorm_bwd")
    return dx, (dpre, dpost), (dshift, dscale, dgate), (dwg_t, dwu_t, dwd)


def _mla_fwd(x, norms, mod, w, rope, mid=None):
    (pre_g, post_g), (shift, scale, gate) = norms, mod
    w_in, q_norm, wq_t, kv_norm, wkv_t, wo = w
    hn, lat = _prenorm_mm(x, pre_g, scale, shift, w_in, "nn", F32, LAT_PAD, "mla_in")
    gate = _chained(gate, mid, lat)
    q, k, v, qn, kvn = _mla_qkv(lat, q_norm, kv_norm, wq_t, wkv_t, rope, "mla_qkv")
    o = _mla_attn_fwd(q, k, v, "mla_attn_fwd")
    x_out, f = _mm_post(o, wo, x, post_g, gate, 1.0, "mla_out")
    return x_out, (x, hn, lat, q, k, v, qn, kvn, o, f)


def _mla_bwd(dx_out, saved, norms, mod, w, rope):
    (pre_g, post_g), (_, scale, gate) = norms, mod
    w_in, q_norm, wq_t, kv_norm, wkv_t, wo = w
    x, hn, lat, q, k, v, qn, kvn, o, f = saved
    d_model = x.shape[1]
    df, dgate, dpost = _post_bwd(dx_out, f, post_g, gate, 1.0, "mix_post_bwd")
    d_o = _mm([(df, wo)], "nt", F32, TOKEN_TILE, wo.shape[0], "mla_do")
    dwo = _mm([(o, df)], "tn", BF16, TOKEN_TILE, d_model, "mla_dwo")
    dq, dk, dv = _mla_attn_bwd(q, k, v, d_o, "mla_attn_bwd")
    dqp, dkv, dlat, dq_norm, dkv_norm = _mla_qkv_bwd(dq, dk, dv, lat, q_norm, kv_norm, wq_t, wkv_t, rope, "mla_qkv_bwd")
    dwq_t = _mm([(dqp, qn)], "tn", BF16, TOKEN_TILE, Q_LORA, "mla_dwq")
    dwkv_t = _mm([(dkv, kvn)], "tn", BF16, TOKEN_TILE, KV_LORA, "mla_dwkv")
    dw_in = _mm([(hn, dlat)], "tn", BF16, TOKEN_TILE, LAT_PAD, "mla_dwin")
    dhn = _mm([(dlat, w_in)], "nt", F32, TOKEN_TILE, d_model, "mla_dhn")
    dx, dshift, dscale, dpre = _prenorm_bwd(dx_out, [dhn], x, pre_g, scale, "prenorm_bwd")
    return dx, (dpre, dpost), (dshift, dscale, dgate), (dw_in, dq_norm, dwq_t, dkv_norm, dwkv_t, dwo)


def _dil_fwd(x, norms, mod, w, bias, mid=None):
    (pre_g, post_g), (shift, scale, gate), (w_in_t, wo) = norms, mod, w
    width = 3 * DIL_HEADS * DIL_HEAD_DIM
    hns, qkvs, outs, lses = [], [], [], []
    for g, (window, dilation) in enumerate(DIL_GROUPS):
        hn, qkv = _prenorm_mm(x, pre_g, scale, shift, w_in_t, "nt", BF16, width, "dil_in", perm=dilation,
                              w_rows=(g * width, width))
        if g == 0:
            gate = _chained(gate, mid, qkv)
        o, lse = _dil_attn_fwd(qkv, bias[g], dilation, window // dilation, "dil_attn_fwd")
        hns.append(hn), qkvs.append(qkv), outs.append(o), lses.append(lse)
    alphas, o_mix, o_mix_b = _dil_mix(lses, outs, "dil_mix")
    x_out, f = _mm_post(o_mix_b, wo, x, post_g, gate, 1.0, "dil_out")
    return x_out, (x, hns, qkvs, lses, alphas, o_mix, o_mix_b, f)


def _dil_bwd(dx_out, saved, norms, mod, w, bias):
    (pre_g, post_g), (_, scale, gate), (w_in_t, wo) = norms, mod, w
    x, hns, qkvs, lses, alphas, o_mix, o_mix_b, f = saved
    d_model = x.shape[1]
    inner = DIL_HEADS * DIL_HEAD_DIM
    df, dgate, dpost = _post_bwd(dx_out, f, post_g, gate, 1.0, "mix_post_bwd")
    d_o = _mm([(df, wo)], "nt", F32, TOKEN_TILE, inner, "dil_do")
    dwo = _mm([(o_mix_b, df)], "tn", BF16, TOKEN_TILE, d_model, "dil_dwo")
    dhns, dws, dbs = [], [], []
    for g, (window, dilation) in enumerate(DIL_GROUPS):
        grads = _dil_attn_bwd(qkvs[g], bias[g], d_o, o_mix, alphas[g], lses[g], dilation, window // dilation, "dil_attn_bwd")
        dbs.append(grads[3])
        dhns.append(_mm([(grads[j], w_in_t) for j in range(3)], "nn", F32, TOKEN_TILE, d_model, "dil_dhn", out_perm=dilation,
                        b_rows=[(3 * g + j) * inner for j in range(3)]))
        dws += list(_mm_tn_shared(list(grads[:3]), hns[g], "dil_dwin"))
    dx, dshift, dscale, dpre = _prenorm_bwd(dx_out, dhns, x, pre_g, scale, "prenorm_bwd3")
    return dx, (dpre, dpost), (dshift, dscale, dgate), (jnp.concatenate(dws, axis=0), dwo), jnp.concatenate(dbs, axis=0)


def _pad_rows(a, rows):
    return jnp.pad(a, ((0, rows - a.shape[0]), (0, 0)))


def _lanes(a):
    flat = a.reshape(-1).astype(F32)
    rows = -(-flat.shape[0] // 1024) * 8
    return jnp.pad(flat, (0, rows * 128 - flat.shape[0])).reshape(rows, 128)


def kernel(x, c, norm_pre, norm_post, w_mod, b_mod, ffn_w_gate, ffn_w_up, ffn_w_down, mla_w_in, mla_q_norm, mla_w_q_up, mla_kv_norm, mla_w_kv_up, mla_w_o, dil_w_in, dil_w_o, rel_bias, loss_target, m_norm_pre, m_norm_post, m_w_mod, m_b_mod, m_ffn_w_gate, m_ffn_w_up, m_ffn_w_down, m_mla_w_in, m_mla_q_norm, m_mla_w_q_up, m_mla_kv_norm, m_mla_w_kv_up, m_mla_w_o, m_dil_w_in, m_dil_w_o, m_rel_bias, v_norm_pre, v_norm_post, v_w_mod, v_b_mod, v_ffn_w_gate, v_ffn_w_up, v_ffn_w_down, v_mla_w_in, v_mla_q_norm, v_mla_w_q_up, v_mla_kv_norm, v_mla_w_kv_up, v_mla_w_o, v_dil_w_in, v_dil_w_o, v_rel_bias):
    me = 4 * lax.axis_index("x") + 2 * lax.axis_index("y") + lax.axis_index("c")
    depth, n_sub, d_loc = norm_pre.shape
    d_model = x.shape[2]
    mod_loc_cols = w_mod.shape[2]
    x0, target = x[0], loss_target[0]

    bf_t = lambda a: a.astype(BF16).T
    ffn_ids = [(i, h) for i in range(depth) for h in range(2)]
    shards = []
    for i, h in ffn_ids:
        shards += [bf_t(ffn_w_gate[i, h]), bf_t(ffn_w_up[i, h]), ffn_w_down[i, h].astype(BF16)]
    shards += [mla_w_in[0].astype(BF16), bf_t(mla_w_q_up[0]), bf_t(mla_w_kv_up[0]), mla_w_o[0].astype(BF16),
               bf_t(dil_w_in[0]), dil_w_o[0].astype(BF16)]
    n_ffn = 3 * len(ffn_ids)
    members = {(0, 0): [0, 1, 2], (0, 1): [n_ffn, n_ffn + 1, n_ffn + 2, n_ffn + 3], (0, 2): [3, 4, 5],
               (1, 0): [6, 7, 8], (1, 1): [n_ffn + 4, n_ffn + 5], (1, 2): [9, 10, 11]}
    order = [(i, s) for i in range(depth) for s in range(n_sub)]

    small = jnp.concatenate([c.reshape(8, 128), _pad_rows(norm_pre.reshape(depth * n_sub, d_loc), 8),
                             _pad_rows(norm_post.reshape(depth * n_sub, d_loc), 8)], axis=0)
    small_all = _exchange([small], True, "gather_small")[0].reshape(NDEV, 24, 128)
    c_all = small_all[:, 0:8].reshape(NDEV, d_model)
    gains = lambda lo: jnp.transpose(small_all[:, lo:lo + depth * n_sub], (1, 0, 2)).reshape(depth, n_sub, 1, d_model)
    pre_full, post_full = gains(8), gains(16)

    b_loc = lax.dynamic_slice(b_mod, (0, me * mod_loc_cols), (depth, mod_loc_cols))
    mod_cols, silu_c = _mod_fwd(c_all, w_mod, b_loc, "mod_fwd")
    mod_all = _exchange([mod_cols.reshape(depth * NDEV, mod_loc_cols)], True, "gather_mod")[0]
    mod_all = mod_all.reshape(NDEV, depth, NDEV, mod_loc_cols)
    mod_mine = lax.dynamic_index_in_dim(mod_all, me, axis=2, keepdims=False)
    mod = jnp.transpose(mod_mine, (1, 0, 2)).reshape(depth, n_sub, 3, 1, d_model)

    first = order[0]
    stages = [("%d%d" % first, members[first][:2]), ("%d%dd" % first, members[first][2:])]
    stages += [("%d%d" % key, members[key]) for key in order[1:]]
    started = {}

    def start(these, name, after):
        used = [k for _, idx in these for k in idx]
        sems, srcs, lands, token = _split_start([shards[k] for k in used], [[used.index(k) for k in idx] for _, idx in these],
                                                True, name, after)
        for n, (stage, idx) in enumerate(these):
            started[stage] = (sems[n], [srcs[used.index(k)] for k in idx], [lands[used.index(k)] for k in idx])
        return token

    g_token = start(stages[:2], "gather_weights_start_first", mod_all)
    start(stages[2:], "gather_weights_start_rest", g_token)

    forwarded = {}

    def forward(stage, after):
        sems, srcs, lands = started[stage]
        forwarded[stage] = _gather_forward(sems, srcs, lands, after, "gather_forward_" + stage)
        return forwarded[stage][3]

    def weights_of(stage, after):
        (send_b, recv_b), srcs, lands, _ = forwarded[stage]
        return _split_wait((send_b, recv_b, started[stage][0][2]), srcs, lands, after, OTHER_CHIPS, True, "gather_wait_" + stage)

    def late_down(after):
        forward("%d%dd" % first, after)
        return weights_of("%d%dd" % first, after)[0]

    lat_real = Q_LORA + KV_LORA
    qk = QK_NOPE + QK_ROPE

    def mla_weights(after):
        w_in, wq_t, wkv_t, wo = weights_of("01", after)
        w_in_pad = jnp.concatenate([w_in[:, :lat_real], jnp.zeros((d_model, QK_NOPE), BF16), w_in[:, lat_real:],
                                    jnp.zeros((d_model, HEAD_PAD - QK_NOPE - QK_ROPE), BF16)], axis=1)
        wq_pad = jnp.pad(wq_t.reshape(MLA_HEADS, qk, Q_LORA), ((0, 0), (0, HEAD_PAD - qk), (0, 0)))
        wo_pad = jnp.pad(wo.reshape(MLA_HEADS, V_HEAD, d_model), ((0, 0), (HEAD_PAD - V_HEAD, 0), (0, 0)))
        return (w_in_pad, mla_q_norm, wq_pad.reshape(MLA_HEADS * HEAD_PAD, Q_LORA), mla_kv_norm, wkv_t,
                wo_pad.reshape(MLA_HEADS * HEAD_PAD, d_model))

    zero = g_token[0, 0]
    rope = _rope_tables(zero)
    buckets = jnp.stack([_dil_buckets(dil) for _, dil in DIL_GROUPS]) + zero.astype(jnp.int32)
    onehot = (buckets[..., None] == jnp.arange(N_BUCKETS)).astype(F32)
    bias = jnp.einsum("gqkb,bgh->ghqk", onehot, rel_bias.reshape(N_BUCKETS, len(DIL_GROUPS), DIL_HEADS),
                      precision=lax.Precision.HIGHEST)

    norms = lambda i, s: (pre_full[i, s], post_full[i, s])
    mods = lambda i, s: (mod[i, s, 0], mod[i, s, 1], mod[i, s, 2])
    saved, weights = {}, {}
    h = x0
    forward("%d%d" % first, bias)
    for n, (i, s) in enumerate(order):
        got = mla_weights(h) if (s == 1 and i % 2 == 0) else tuple(weights_of("%d%d" % (i, s), h))
        mid = None if n + 1 == len(order) else (lambda after, nxt="%d%d" % order[n + 1]: forward(nxt, after))
        if s != 1:
            if len(got) == 3:
                h, saved[i, s], weights[i, s] = _ffn_fwd(h, norms(i, s), mods(i, s), got)
                if mid is not None:
                    mid(h)
            else:
                h, saved[i, s], weights[i, s] = _ffn_fwd(h, norms(i, s), mods(i, s), (*got, late_down), mid)
            continue
        weights[i, s] = got
        if i % 2 == 0:
            h, saved[i, s] = _mla_fwd(h, norms(i, s), mods(i, s), weights[i, s], rope, mid)
        else:
            h, saved[i, s] = _dil_fwd(h, norms(i, s), mods(i, s), weights[i, s], bias, mid)
    dh, loss_parts = _loss_grad(h, target, "loss")

    dnorm, dmod, sent = {}, {}, {}
    token = jnp.zeros((8, 128), F32)
    last = order[0]

    def send_last(j, dw):
        sent[last, j] = _split_start([dw], [[0]], False, "scatter_start_%d%d_%d" % (*last, j))
        return sent[last, j][3]

    for i, s in reversed(order):
        md = mods(i, s)
        md = (md[0], md[1], md[2] + token[:1, :1])
        if (i, s) == last:
            dh, dnorm[i, s], dmod[i, s], _ = _ffn_bwd(dh, saved[i, s], norms(i, s), md, weights[i, s], send_last)
            continue
        if s != 1:
            dh, dnorm[i, s], dmod[i, s], dws = _ffn_bwd(dh, saved[i, s], norms(i, s), md, weights[i, s])
        elif i % 2 == 0:
            dh, dnorm[i, s], dmod[i, s], dmla = _mla_bwd(dh, saved[i, s], norms(i, s), md, weights[i, s], rope)
            dw_in_pad, dq_norm, dwq_pad, dkv_norm, dwkv_t, dwo_pad = dmla
            dw_in = jnp.concatenate([dw_in_pad[:, :lat_real], dw_in_pad[:, lat_real + QK_NOPE:lat_real + qk]], axis=1)
            dwq_t = dwq_pad.reshape(MLA_HEADS, HEAD_PAD, Q_LORA)[:, :qk].reshape(MLA_HEADS * qk, Q_LORA)
            dwo = dwo_pad.reshape(MLA_HEADS, HEAD_PAD, d_model)[:, HEAD_PAD - V_HEAD:].reshape(MLA_HEADS * V_HEAD, d_model)
            dws = (dw_in, dwq_t, dwkv_t, dwo)
        else:
            dh, dnorm[i, s], dmod[i, s], dws, dbias = _dil_bwd(dh, saved[i, s], norms(i, s), md, weights[i, s], bias)
        sent[i, s] = _split_start(list(dws), [list(range(len(dws)))], False, "scatter_start_%d%d" % (i, s))
        token = sent[i, s][3]
    grad_x = dh[None]

    mine = {}
    transposed = {3 * n + j for n in range(len(ffn_ids)) for j in (0, 1)} | {n_ffn + 1, n_ffn + 2, n_ffn + 4}
    for key in reversed(order[1:]):
        sems, srcs, lands, _ = sent[key]
        parts = _split_wait(sems[0], srcs, lands, dh, NDEV - 1, False, "scatter_wait_%d%d" % key)
        for k, p in zip(members[key], parts):
            mine[k] = _sum_parts(p, "sum_parts", k in transposed)
    g_mla_in, g_q_up, g_kv_up, g_mla_o, g_dil_in, g_dil_o = (mine[k] for k in range(n_ffn, n_ffn + 6))
    g_mla_in, g_q_up, g_kv_up, g_mla_o = g_mla_in[None], g_q_up[None], g_kv_up[None], g_mla_o[None]
    g_dil_in, g_dil_o = g_dil_in[None], g_dil_o[None]
    early = {"mla_w_in": _adamw(mla_w_in, g_mla_in, m_mla_w_in, v_mla_w_in, "adamw"),
             "mla_w_q_up": _adamw(mla_w_q_up, g_q_up, m_mla_w_q_up, v_mla_w_q_up, "adamw"),
             "mla_w_kv_up": _adamw(mla_w_kv_up, g_kv_up, m_mla_w_kv_up, v_mla_w_kv_up, "adamw"),
             "mla_w_o": _adamw(mla_w_o, g_mla_o, m_mla_w_o, v_mla_w_o, "adamw"),
             "dil_w_in": _adamw(dil_w_in, g_dil_in, m_dil_w_in, v_dil_w_in, "adamw"),
             "dil_w_o": _adamw(dil_w_o, g_dil_o, m_dil_w_o, v_dil_w_o, "adamw")}
    dbias_sums = _bias_reduce(dbias, buckets, "bias_reduce")
    tied = lax.optimization_barrier((dbias_sums, *[a for step in early.values() for a in step]))
    dbias_sums, early = tied[0], {name: tuple(tied[1 + 3 * n:4 + 3 * n]) for n, name in enumerate(early)}
    for j in (2, 0, 1):
        sems, srcs, lands, _ = sent[last, j]
        parts = _split_wait(sems[0], srcs, lands, dbias_sums, NDEV - 1, False, "scatter_wait_%d%d_%d" % (*last, j))
        mine[members[last][j]] = _sum_parts(parts[0], "sum_parts", members[last][j] in transposed)
    g_gate = jnp.stack([mine[3 * n] for n in range(len(ffn_ids))]).reshape(ffn_w_gate.shape)
    g_up = jnp.stack([mine[3 * n + 1] for n in range(len(ffn_ids))]).reshape(ffn_w_up.shape)
    g_down = jnp.stack([mine[3 * n + 2] for n in range(len(ffn_ids))]).reshape(ffn_w_down.shape)

    dmod_mine = jnp.concatenate([jnp.concatenate(dmod[i, s], axis=0) for i in range(depth) for s in range(n_sub)], axis=0)
    dpre_mine = jnp.concatenate([dnorm[i, s][0] for i in range(depth) for s in range(n_sub)], axis=0)
    dpost_mine = jnp.concatenate([dnorm[i, s][1] for i in range(depth) for s in range(n_sub)], axis=0)
    dbias_tab = dbias_sums[:, 0, :N_BUCKETS].T
    pieces = [dmod_mine, dpre_mine, dpost_mine, dq_norm, dkv_norm, dbias_tab, jnp.sum(loss_parts).reshape(1, 1)]
    packed = [_lanes(p) for p in pieces]
    offs = [0]
    for p in packed:
        offs.append(offs[-1] + p.shape[0])
    everyone = _exchange([jnp.concatenate(packed, axis=0)], True, "gather_small_grads")[0].reshape(NDEV, offs[-1], 128)
    total = _sum_parts(everyone, "sum_small")
    take = lambda n, shape: total[offs[n]:offs[n + 1]].reshape(-1)[:math.prod(shape)].reshape(shape)
    g_b_mod = take(0, b_mod.shape)
    col0 = me * d_loc
    g_norm_pre = lax.dynamic_slice(take(1, (depth, n_sub, d_model)), (0, 0, col0), norm_pre.shape)
    g_norm_post = lax.dynamic_slice(take(2, (depth, n_sub, d_model)), (0, 0, col0), norm_post.shape)
    g_q_norm, g_kv_norm = take(3, mla_q_norm.shape), take(4, mla_kv_norm.shape)
    g_rel_bias = take(5, rel_bias.shape)
    loss = take(6, ())

    dmod_all = everyone[:, offs[0]:offs[1]].reshape(NDEV, depth, NDEV * mod_loc_cols)
    dmod_cols = lax.dynamic_slice(dmod_all, (0, 0, me * mod_loc_cols), (NDEV, depth, mod_loc_cols))
    silu_t = jnp.pad(silu_c.T, ((0, 0), (0, HEAD_PAD - NDEV)))
    g_w_mod = jnp.stack([_mm([(silu_t, jnp.pad(dmod_cols[:, i], ((0, HEAD_PAD - NDEV), (0, 0))))], "nn", F32, TOKEN_TILE,
                             mod_loc_cols, "mod_bwd") for i in range(depth)])

    ws = (norm_pre, norm_post, w_mod, b_mod, ffn_w_gate, ffn_w_up, ffn_w_down, mla_w_in, mla_q_norm, mla_w_q_up, mla_kv_norm,
          mla_w_kv_up, mla_w_o, dil_w_in, dil_w_o, rel_bias)
    gs = (g_norm_pre, g_norm_post, g_w_mod, g_b_mod, g_gate, g_up, g_down, g_mla_in, g_q_norm, g_q_up, g_kv_norm, g_kv_up,
          g_mla_o, g_dil_in, g_dil_o, g_rel_bias)
    ms = (m_norm_pre, m_norm_post, m_w_mod, m_b_mod, m_ffn_w_gate, m_ffn_w_up, m_ffn_w_down, m_mla_w_in, m_mla_q_norm,
          m_mla_w_q_up, m_mla_kv_norm, m_mla_w_kv_up, m_mla_w_o, m_dil_w_in, m_dil_w_o, m_rel_bias)
    vs = (v_norm_pre, v_norm_post, v_w_mod, v_b_mod, v_ffn_w_gate, v_ffn_w_up, v_ffn_w_down, v_mla_w_in, v_mla_q_norm,
          v_mla_w_q_up, v_mla_kv_norm, v_mla_w_kv_up, v_mla_w_o, v_dil_w_in, v_dil_w_o, v_rel_bias)
    names = ("norm_pre", "norm_post", "w_mod", "b_mod", "ffn_w_gate", "ffn_w_up", "ffn_w_down", "mla_w_in", "mla_q_norm",
             "mla_w_q_up", "mla_kv_norm", "mla_w_kv_up", "mla_w_o", "dil_w_in", "dil_w_o", "rel_bias")
    stepped = [early[n] if n in early else _adamw(w, g, m, v, "adamw") for n, w, g, m, v in zip(names, ws, gs, ms, vs)]
    deltas, new_m, new_v = zip(*stepped)
    return (loss, grad_x, *gs, *deltas, *new_m, *new_v)
```

```python
import math

import jax
import jax.numpy as jnp
from jax import lax
from jax.experimental import pallas as pl
from jax.experimental.pallas import tpu as pltpu

F32 = jnp.float32
BF16 = jnp.bfloat16
MESH = pl.DeviceIdType.MESH

NDEV = 8
OTHER_CHIPS = 3
D_MODEL = 1024
SEQ = 2048
D_FF = 2816
EPS = 1e-6
FFN_RES = 0.5
FFN_CHUNKS = 11

MLA_HEADS = 16
Q_LORA = 384
KV_LORA = 256
QK_NOPE = 64
QK_ROPE = 32
V_HEAD = 64
ROPE_THETA = 10000.0
HEAD_PAD = 128
LAT_PAD = Q_LORA + KV_LORA + HEAD_PAD
MLA_SCALE = (QK_NOPE + QK_ROPE) ** -0.5
MLA_QUERY_TILE = 256

DIL_GROUPS = ((128, 1), (512, 4), (2048, 16))
DIL_HEADS = 16
DIL_HEAD_DIM = 64
DIL_BLOCK = 128
DIL_PAIRS = DIL_HEADS // 2
DIL_SCALE = DIL_HEAD_DIM ** -0.5
DIL_GROUPED = 8
N_BUCKETS = 32
MAX_DISTANCE = 2048

ADAM_LR = 0.001
ADAM_B1 = 0.9
ADAM_B2 = 0.999
ADAM_EPS = 1e-08
ADAM_WD = 0.01
ADAM_STEP = 10

V7X_VMEM_BYTES = 64 * 2**20
VMEM_RESERVE = 10 * 2**20
TOKEN_TILE = 512


def _nbytes(shape, dtype):
    return math.prod(shape) * jnp.dtype(dtype).itemsize


def _params(semantics, blocks, extra=0):
    need = 2 * sum(_nbytes(s, d) for s, d in blocks) + extra + VMEM_RESERVE
    return pltpu.CompilerParams(dimension_semantics=semantics,
                                vmem_limit_bytes=int(min(need, V7X_VMEM_BYTES - VMEM_RESERVE)))


def _pcall(body, out_shape, **kw):
    call = pl.pallas_call(body, out_shape=jax.tree.map(lambda s: pltpu.HBM(s.shape, s.dtype), out_shape), **kw)
    return lambda *args: call(*[pltpu.with_memory_space_constraint(a, pltpu.HBM) for a in args])


def _dot_nn(a, b):
    return lax.dot_general(a, b, (((1,), (0,)), ((), ())), preferred_element_type=F32)


def _dot_nt(a, b):
    return lax.dot_general(a, b, (((1,), (1,)), ((), ())), preferred_element_type=F32)


def _dot_tn(a, b):
    return lax.dot_general(a, b, (((0,), (0,)), ((), ())), preferred_element_type=F32)


_DOTS = {"nn": _dot_nn, "nt": _dot_nt, "tn": _dot_tn}


def _rstd(v):
    return lax.rsqrt(jnp.mean(v * v, axis=-1, keepdims=True) + EPS)


def _rms_bwd(v, r, t):
    return r * t - v * (r * r * r) * jnp.mean(t * v, axis=-1, keepdims=True)


_TOKEN_SPEC = pl.BlockSpec((8, 128), lambda *_: (0, 0))


def _mm(pairs, mode, out_dtype, tm, tn, name, out_perm=1, after=None, b_rows=None):
    a0, b0 = pairs[0]
    m_dim = a0.shape[1] if mode == "tn" else a0.shape[0]
    n_dim = b0.shape[0] if mode == "nt" else b0.shape[1]
    tm, tn = min(tm, m_dim // out_perm), min(tn, n_dim)
    assert m_dim % tm == 0 and n_dim % tn == 0, (name, m_dim, n_dim, tm, tn)
    dot = _DOTS[mode]
    npairs = len(pairs)

    def body(*refs):
        acc = None
        for p in range(npairs):
            d = dot(refs[2 * p][...].astype(BF16), refs[2 * p + 1][...].astype(BF16))
            acc = d if acc is None else acc + d
        refs[-1][...] = acc.astype(out_dtype)

    in_specs, blocks, flat = [], [], []
    for n_pair, (a, b) in enumerate(pairs):
        if mode == "nn":
            k = a.shape[1]
            first_block = 0 if b_rows is None else b_rows[n_pair] // k
            sa, sb = ((tm, k), lambda i, j: (i, 0)), ((k, tn), lambda i, j, o=first_block: (o, j))
        elif mode == "nt":
            k = a.shape[1]
            sa, sb = ((tm, k), lambda i, j: (i, 0)), ((tn, k), lambda i, j: (j, 0))
        else:
            k = a.shape[0]
            sa, sb = ((k, tm), lambda i, j: (0, i)), ((k, tn), lambda i, j: (0, j))
        in_specs += [pl.BlockSpec(*sa), pl.BlockSpec(*sb)]
        blocks += [(sa[0], a.dtype), (sb[0], b.dtype)]
        flat += [a, b]
    if after is not None:
        in_specs.append(_TOKEN_SPEC)
        flat.append(after)
    if out_perm == 1:
        out_shape = (m_dim, n_dim)
        out_spec = pl.BlockSpec((tm, tn), lambda i, j: (i, j))
    else:
        rows = m_dim // out_perm
        assert tn == n_dim and rows % tm == 0, (name, rows, tm)
        nb = rows // tm
        out_shape = (rows, out_perm * n_dim)
        out_spec = pl.BlockSpec((tm, n_dim), lambda i, j: (i % nb, i // nb))
    blocks.append(((tm, tn), out_dtype))
    res = _pcall(
        body, out_shape=jax.ShapeDtypeStruct(out_shape, out_dtype), grid=(m_dim // tm, n_dim // tn),
        in_specs=in_specs, out_specs=out_spec, name=name,
        compiler_params=_params(("parallel", "parallel"), blocks, extra=2 * tm * tn * 4),
    )(*flat)
    return res.reshape(m_dim, n_dim)


def _prenorm_mm(x, pre_g, scale, shift, w, w_mode, out_dtype, tn, name, perm=1, w_rows=None):
    s_dim, d_dim = x.shape
    n_dim = w.shape[0] if w_mode == "nt" else w.shape[1]
    w_first = 0
    if w_rows is not None:
        w_first, n_dim = w_rows
    rows = s_dim // perm
    side = max(1, TOKEN_TILE // rows)
    tm = side * min(TOKEN_TILE, rows)
    nb = max(1, rows // tm)
    tn = min(tn, n_dim)
    assert n_dim % tn == 0 and w_first % tn == 0
    w_block0 = w_first // tn
    dot = _DOTS[w_mode]

    def body(x_ref, g_ref, sc_ref, sh_ref, w_ref, hn_ref, o_ref):
        @pl.when(pl.program_id(1) == 0)
        def _():
            xf = x_ref[...]
            if side > 1:
                xf = jnp.concatenate([xf[:, c * d_dim:(c + 1) * d_dim] for c in range(side)], axis=0)
            hn = (xf * _rstd(xf) * g_ref[...]) * (1.0 + sc_ref[...]) + sh_ref[...]
            hn_ref[...] = hn.astype(BF16)

        o_ref[...] = dot(hn_ref[...], w_ref[...]).astype(out_dtype)

    vec = pl.BlockSpec((1, d_dim), lambda i, j: (0, 0))
    w_block = (tn, d_dim) if w_mode == "nt" else (d_dim, tn)
    w_spec = pl.BlockSpec(w_block, (lambda i, j: (w_block0 + j, 0)) if w_mode == "nt" else (lambda i, j: (0, j)))
    hn, out = _pcall(
        body,
        out_shape=(jax.ShapeDtypeStruct((s_dim, d_dim), BF16), jax.ShapeDtypeStruct((s_dim, n_dim), out_dtype)),
        grid=(s_dim // tm, n_dim // tn),
        in_specs=[pl.BlockSpec((tm // side, side * d_dim), lambda i, j: (i % nb, i // nb)), vec, vec, vec, w_spec],
        out_specs=(pl.BlockSpec((tm, d_dim), lambda i, j: (i, 0)), pl.BlockSpec((tm, tn), lambda i, j: (i, j))),
        name=name,
        compiler_params=_params(("parallel", "arbitrary"),
                                [((tm, d_dim), F32), (w_block, BF16), ((tm, d_dim), BF16), ((tm, tn), out_dtype)],
                                extra=3 * tm * d_dim * 4 + tm * tn * 4),
    )(x.reshape(rows, perm * d_dim), pre_g, scale, shift, w)
    return hn, out


def _ffn_up(x, pre_g, scale, shift, wg_t, wu_t, name):
    s_dim, d_dim = x.shape
    f_dim = wg_t.shape[0]
    tm, tn = TOKEN_TILE, f_dim // 2

    def body(x_ref, g_ref, sc_ref, sh_ref, wg_ref, wu_ref, hn_ref, go_ref, uo_ref, a_ref):
        @pl.when(pl.program_id(1) == 0)
        def _():
            xf = x_ref[...]
            hn = (xf * _rstd(xf) * g_ref[...]) * (1.0 + sc_ref[...]) + sh_ref[...]
            hn_ref[...] = hn.astype(BF16)

        hn = hn_ref[...]
        g = _dot_nt(hn, wg_ref[...])
        u = _dot_nt(hn, wu_ref[...])
        go_ref[...] = g.astype(BF16)
        uo_ref[...] = u.astype(BF16)
        a_ref[...] = (g * jax.nn.sigmoid(g) * u).astype(BF16)

    vec = pl.BlockSpec((1, d_dim), lambda i, j: (0, 0))
    w_spec = pl.BlockSpec((tn, d_dim), lambda i, j: (j, 0))
    act = pl.BlockSpec((tm, tn), lambda i, j: (i, j))
    act_shape = jax.ShapeDtypeStruct((s_dim, f_dim), BF16)
    return _pcall(
        body,
        out_shape=(jax.ShapeDtypeStruct((s_dim, d_dim), BF16), act_shape, act_shape, act_shape),
        grid=(s_dim // tm, f_dim // tn),
        in_specs=[pl.BlockSpec((tm, d_dim), lambda i, j: (i, 0)), vec, vec, vec, w_spec, w_spec],
        out_specs=(pl.BlockSpec((tm, d_dim), lambda i, j: (i, 0)), act, act, act),
        name=name,
        compiler_params=_params(("parallel", "arbitrary"),
                                [((tm, d_dim), F32), ((tn, d_dim), BF16), ((tn, d_dim), BF16), ((tm, d_dim), BF16)]
                                + 3 * [((tm, tn), BF16)], extra=3 * tm * d_dim * 4 + 4 * tm * tn * 4),
    )(x, pre_g, scale, shift, wg_t, wu_t)


def _mm_post(a, w, x, post_g, gate, res_w, name):
    s_dim, k_dim = a.shape
    d_dim = w.shape[1]
    tm = TOKEN_TILE

    def body(a_ref, w_ref, x_ref, pg_ref, gt_ref, xo_ref, f_ref):
        f = _dot_nn(a_ref[...], w_ref[...])
        y = f * _rstd(f) * pg_ref[...]
        f_ref[...] = f
        xo_ref[...] = x_ref[...] + (res_w * gt_ref[...]) * y

    vec = pl.BlockSpec((1, d_dim), lambda i: (0, 0))
    row = pl.BlockSpec((tm, d_dim), lambda i: (i, 0))
    out = jax.ShapeDtypeStruct((s_dim, d_dim), F32)
    return _pcall(
        body, out_shape=(out, out), grid=(s_dim // tm,),
        in_specs=[pl.BlockSpec((tm, k_dim), lambda i: (i, 0)), pl.BlockSpec((k_dim, d_dim), lambda i: (0, 0)), row, vec, vec],
        out_specs=(row, row), name=name,
        compiler_params=_params(("parallel",), [((tm, k_dim), BF16), ((k_dim, d_dim), BF16)] + 3 * [((tm, d_dim), F32)],
                                extra=3 * tm * d_dim * 4),
    )(a, w, x, post_g, gate)


def _post_bwd(dx_out, f, post_g, gate, res_w, name):
    s_dim, d_dim = f.shape
    tm = TOKEN_TILE

    def body(dx_ref, f_ref, pg_ref, gt_ref, df_ref, dgate_ref, dpost_ref):
        @pl.when(pl.program_id(0) == 0)
        def _():
            dgate_ref[...] = jnp.zeros_like(dgate_ref)
            dpost_ref[...] = jnp.zeros_like(dpost_ref)

        dx, fv = dx_ref[...], f_ref[...]
        r = _rstd(fv)
        fr = fv * r
        dgate_ref[...] += res_w * jnp.sum(dx * (fr * pg_ref[...]), axis=0, keepdims=True)
        dy = (res_w * gt_ref[...]) * dx
        dpost_ref[...] += jnp.sum(dy * fr, axis=0, keepdims=True)
        df_ref[...] = _rms_bwd(fv, r, dy * pg_ref[...]).astype(BF16)

    vec = pl.BlockSpec((1, d_dim), lambda i: (0, 0))
    row = pl.BlockSpec((tm, d_dim), lambda i: (i, 0))
    vshape = jax.ShapeDtypeStruct((1, d_dim), F32)
    return _pcall(
        body, out_shape=(jax.ShapeDtypeStruct((s_dim, d_dim), BF16), vshape, vshape), grid=(s_dim // tm,),
        in_specs=[row, row, vec, vec], out_specs=(row, vec, vec), name=name,
        compiler_params=_params(("arbitrary",), 3 * [((tm, d_dim), F32)], extra=6 * tm * d_dim * 4),
    )(dx_out, f, post_g, gate)


def _prenorm_bwd(dx_out, dhns, x, pre_g, scale, name):
    s_dim, d_dim = x.shape
    tm = TOKEN_TILE
    n_in = len(dhns)

    def body(*refs):
        dx_ref, x_ref, pg_ref, sc_ref = refs[n_in + 0], refs[n_in + 1], refs[n_in + 2], refs[n_in + 3]
        dxo_ref, dsh_ref, dsc_ref, dpg_ref = refs[n_in + 4:]

        @pl.when(pl.program_id(0) == 0)
        def _():
            dsh_ref[...] = jnp.zeros_like(dsh_ref)
            dsc_ref[...] = jnp.zeros_like(dsc_ref)
            dpg_ref[...] = jnp.zeros_like(dpg_ref)

        dhn = refs[0][...]
        for k in range(1, n_in):
            dhn = dhn + refs[k][...]
        xv = x_ref[...]
        r = _rstd(xv)
        xr = xv * r
        dsh_ref[...] += jnp.sum(dhn, axis=0, keepdims=True)
        dsc_ref[...] += jnp.sum(dhn * (xr * pg_ref[...]), axis=0, keepdims=True)
        dn = dhn * (1.0 + sc_ref[...])
        dpg_ref[...] += jnp.sum(dn * xr, axis=0, keepdims=True)
        dxo_ref[...] = dx_ref[...] + _rms_bwd(xv, r, dn * pg_ref[...])

    vec = pl.BlockSpec((1, d_dim), lambda i: (0, 0))
    row = pl.BlockSpec((tm, d_dim), lambda i: (i, 0))
    vshape = jax.ShapeDtypeStruct((1, d_dim), F32)
    return _pcall(
        body, out_shape=(jax.ShapeDtypeStruct((s_dim, d_dim), F32), vshape, vshape, vshape), grid=(s_dim // tm,),
        in_specs=n_in * [row] + [row, row, vec, vec], out_specs=(row, vec, vec, vec), name=name,
        compiler_params=_params(("arbitrary",), (n_in + 3) * [((tm, d_dim), F32)], extra=6 * tm * d_dim * 4),
    )(*dhns, dx_out, x, pre_g, scale)


def _ffn_dgu(df, wd, g, u, name, after=None):
    s_dim, d_dim = df.shape
    f_dim = wd.shape[0]
    tm, tn = TOKEN_TILE, f_dim // 2

    def body(df_ref, wd_ref, g_ref, u_ref, *rest):
        dg_ref, du_ref = rest[-2:]
        da = _dot_nt(df_ref[...], wd_ref[...])
        gv, uv = g_ref[...].astype(F32), u_ref[...].astype(F32)
        sg = jax.nn.sigmoid(gv)
        du_ref[...] = (da * (gv * sg)).astype(BF16)
        dg_ref[...] = (da * uv * (sg * (1.0 + gv * (1.0 - sg)))).astype(BF16)

    act = pl.BlockSpec((tm, tn), lambda i, j: (i, j))
    act_shape = jax.ShapeDtypeStruct((s_dim, f_dim), BF16)
    token = [] if after is None else [after]
    return _pcall(
        body, out_shape=(act_shape, act_shape), grid=(s_dim // tm, f_dim // tn),
        in_specs=[pl.BlockSpec((tm, d_dim), lambda i, j: (i, 0)), pl.BlockSpec((tn, d_dim), lambda i, j: (j, 0)), act, act]
        + len(token) * [_TOKEN_SPEC],
        out_specs=(act, act), name=name,
        compiler_params=_params(("parallel", "parallel"), [((tm, d_dim), BF16), ((tn, d_dim), BF16)] + 4 * [((tm, tn), BF16)],
                                extra=6 * tm * tn * 4),
    )(df, wd, g, u, *token)


def _ffn_dw(dg, du, a, hn, df, name):
    s_dim, f_dim = dg.shape
    d_dim = hn.shape[1]
    tm = 256

    def body(dg_ref, du_ref, a_ref, hn_ref, df_ref, dwg_ref, dwu_ref, dwd_ref):
        dwg_ref[...] = _dot_tn(dg_ref[...], hn_ref[...]).astype(BF16)
        dwu_ref[...] = _dot_tn(du_ref[...], hn_ref[...]).astype(BF16)
        dwd_ref[...] = _dot_tn(a_ref[...], df_ref[...]).astype(BF16)

    col = pl.BlockSpec((s_dim, tm), lambda i: (0, i))
    full = pl.BlockSpec((s_dim, d_dim), lambda i: (0, 0), pipeline_mode=pl.Buffered(1))
    out = pl.BlockSpec((tm, d_dim), lambda i: (i, 0))
    shape = jax.ShapeDtypeStruct((f_dim, d_dim), BF16)
    need = 2 * s_dim * d_dim * 2 + 2 * 3 * (s_dim * tm * 2 + tm * d_dim * 2) + 3 * tm * d_dim * 4 + 3 * s_dim * tm * 2
    return _pcall(
        body, out_shape=(shape, shape, shape), grid=(f_dim // tm,), in_specs=[col, col, col, full, full],
        out_specs=(out, out, out), name=name,
        compiler_params=pltpu.CompilerParams(dimension_semantics=("parallel",),
                                             vmem_limit_bytes=int(min(need + VMEM_RESERVE, V7X_VMEM_BYTES - VMEM_RESERVE))),
    )(dg, du, a, hn, df)


def _mm_tn_shared(lhs, b, name):
    k_dim, m_dim = lhs[0].shape
    n_dim = b.shape[1]
    tm = 256
    n = len(lhs)

    def body(*refs):
        rhs = refs[n][...]
        for j in range(n):
            refs[n + 1 + j][...] = _dot_tn(refs[j][...], rhs).astype(BF16)

    col = pl.BlockSpec((k_dim, tm), lambda i: (0, i))
    out = pl.BlockSpec((tm, n_dim), lambda i: (i, 0))
    shape = jax.ShapeDtypeStruct((m_dim, n_dim), BF16)
    need = k_dim * n_dim * 2 + 2 * n * (k_dim * tm * 2 + tm * n_dim * 2) + n * tm * n_dim * 4 + n * k_dim * tm * 2
    return _pcall(
        body, out_shape=tuple(n * [shape]), grid=(m_dim // tm,),
        in_specs=n * [col] + [pl.BlockSpec((k_dim, n_dim), lambda i: (0, 0), pipeline_mode=pl.Buffered(1))],
        out_specs=tuple(n * [out]), name=name,
        compiler_params=pltpu.CompilerParams(dimension_semantics=("parallel",),
                                             vmem_limit_bytes=int(min(need + VMEM_RESERVE, V7X_VMEM_BYTES - VMEM_RESERVE))),
    )(*lhs, b)


class _chunked_load:
    def __init__(self, pairs, sems, chunks, rows):
        self.first = pl.program_id(0) == 0
        self.copies = {(k, c): pltpu.make_async_copy(hbm.at[c * rows:(c + 1) * rows, :], vmem.at[c * rows:(c + 1) * rows, :],
                                                     sems.at[k, c])
                       for c in range(chunks) for k, (hbm, vmem) in enumerate(pairs)}

        @pl.when(self.first)
        def _():
            for copy in self.copies.values():
                copy.start()

    def wait(self, k, c):
        @pl.when(self.first)
        def _():
            self.copies[k, c].wait()


def _ffn_fwd_fused(x, pre_g, scale, shift, post_g, gate, wg_t, wu_t, wd, name):
    s_dim, d_dim = x.shape
    f_dim = wd.shape[0]
    tm, chunks = 256, FFN_CHUNKS
    cw = f_dim // chunks

    def body(x_ref, prg_ref, sc_ref, sh_ref, pg_ref, gt_ref, wg_hbm, wu_hbm, wd_hbm, hn_ref, go_ref, uo_ref, a_ref, xo_ref, f_ref,
             wg_ref, wu_ref, wd_ref, sems):
        load = _chunked_load([(wg_hbm, wg_ref), (wu_hbm, wu_ref), (wd_hbm, wd_ref)], sems, chunks, cw)
        xf = x_ref[...]
        hn = ((xf * _rstd(xf) * prg_ref[...]) * (1.0 + sc_ref[...]) + sh_ref[...]).astype(BF16)
        hn_ref[...] = hn
        f = None
        load.wait(0, 0), load.wait(1, 0)
        ahead = (_dot_nt(hn, wg_ref[0:cw, :]), _dot_nt(hn, wu_ref[0:cw, :]))
        for c in range(chunks):
            g, u = ahead
            if c + 1 < chunks:
                nxt = slice((c + 1) * cw, (c + 2) * cw)
                load.wait(0, c + 1), load.wait(1, c + 1)
                ahead = (_dot_nt(hn, wg_ref[nxt, :]), _dot_nt(hn, wu_ref[nxt, :]))
            cols = slice(c * cw, (c + 1) * cw)
            go_ref[:, cols] = g.astype(BF16)
            uo_ref[:, cols] = u.astype(BF16)
            a = (g * jax.nn.sigmoid(g) * u).astype(BF16)
            a_ref[:, cols] = a
            load.wait(2, c)
            part = _dot_nn(a, wd_ref[cols, :])
            f = part if f is None else f + part
        f_ref[...] = f
        xo_ref[...] = xf + (FFN_RES * gt_ref[...]) * (f * _rstd(f) * pg_ref[...])

    vec = pl.BlockSpec((1, d_dim), lambda i: (0, 0))
    row = pl.BlockSpec((tm, d_dim), lambda i: (i, 0))
    act = pl.BlockSpec((tm, f_dim), lambda i: (i, 0))
    weight = pl.BlockSpec(memory_space=pl.ANY)
    act_shape = jax.ShapeDtypeStruct((s_dim, f_dim), BF16)
    res_shape = jax.ShapeDtypeStruct((s_dim, d_dim), F32)
    need = (3 * f_dim * d_dim * 2 + 2 * tm * d_dim * 4 + 2 * (tm * d_dim * 2 + 3 * tm * f_dim * 2 + 2 * tm * d_dim * 4)
            + 8 * tm * cw * 4 + 4 * tm * d_dim * 4)
    return _pcall(
        body, out_shape=(jax.ShapeDtypeStruct((s_dim, d_dim), BF16), act_shape, act_shape, act_shape, res_shape, res_shape),
        grid=(s_dim // tm,), in_specs=[row, vec, vec, vec, vec, vec, weight, weight, weight],
        out_specs=(row, act, act, act, row, row), name=name,
        scratch_shapes=3 * [pltpu.VMEM((f_dim, d_dim), BF16)] + [pltpu.SemaphoreType.DMA((3, chunks))],
        compiler_params=pltpu.CompilerParams(dimension_semantics=("arbitrary",),
                                             vmem_limit_bytes=int(min(need + VMEM_RESERVE, V7X_VMEM_BYTES - VMEM_RESERVE))),
    )(x, pre_g, scale, shift, post_g, gate, wg_t, wu_t, wd)


def _ffn_bwd_fused(dx_out, saved, pre_g, post_g, scale, gate, wg_t, wu_t, wd, name):
    x, _, g, u, _, f = saved
    s_dim, d_dim = x.shape
    f_dim = wd.shape[0]
    tm, chunks = 256, FFN_CHUNKS
    cw = f_dim // chunks

    def body(dx_ref, f_ref, g_ref, u_ref, x_ref, pg_ref, gt_ref, prg_ref, sc_ref, wd_hbm, wg_hbm, wu_hbm,
             df_ref, dg_ref, du_ref, dxo_ref, dgate_ref, dpost_ref, dsh_ref, dsc_ref, dpg_ref, wd_ref, wg_ref, wu_ref, sems):
        load = _chunked_load([(wd_hbm, wd_ref), (wg_hbm, wg_ref), (wu_hbm, wu_ref)], sems, chunks, cw)

        @pl.when(pl.program_id(0) == 0)
        def _():
            for acc in (dgate_ref, dpost_ref, dsh_ref, dsc_ref, dpg_ref):
                acc[...] = jnp.zeros_like(acc)

        dx, fv = dx_ref[...], f_ref[...]
        r = _rstd(fv)
        fr = fv * r
        dgate_ref[...] += FFN_RES * jnp.sum(dx * (fr * pg_ref[...]), axis=0, keepdims=True)
        dy = (FFN_RES * gt_ref[...]) * dx
        dpost_ref[...] += jnp.sum(dy * fr, axis=0, keepdims=True)
        df = _rms_bwd(fv, r, dy * pg_ref[...]).astype(BF16)
        df_ref[...] = df
        dhn = None
        load.wait(0, 0)
        ahead = _dot_nt(df, wd_ref[0:cw, :])
        for c in range(chunks):
            da = ahead
            if c + 1 < chunks:
                load.wait(0, c + 1)
                ahead = _dot_nt(df, wd_ref[(c + 1) * cw:(c + 2) * cw, :])
            cols = slice(c * cw, (c + 1) * cw)
            gv, uv = g_ref[:, cols].astype(F32), u_ref[:, cols].astype(F32)
            sg = jax.nn.sigmoid(gv)
            du = (da * (gv * sg)).astype(BF16)
            dg = (da * uv * (sg * (1.0 + gv * (1.0 - sg)))).astype(BF16)
            dg_ref[:, cols] = dg
            du_ref[:, cols] = du
            load.wait(1, c), load.wait(2, c)
            part = _dot_nn(dg, wg_ref[cols, :]) + _dot_nn(du, wu_ref[cols, :])
            dhn = part if dhn is None else dhn + part
        xv = x_ref[...]
        rx = _rstd(xv)
        xr = xv * rx
        dsh_ref[...] += jnp.sum(dhn, axis=0, keepdims=True)
        dsc_ref[...] += jnp.sum(dhn * (xr * prg_ref[...]), axis=0, keepdims=True)
        dn = dhn * (1.0 + sc_ref[...])
        dpg_ref[...] += jnp.sum(dn * xr, axis=0, keepdims=True)
        dxo_ref[...] = dx + _rms_bwd(xv, rx, dn * prg_ref[...])

    vec = pl.BlockSpec((1, d_dim), lambda i: (0, 0))
    row = pl.BlockSpec((tm, d_dim), lambda i: (i, 0))
    act = pl.BlockSpec((tm, f_dim), lambda i: (i, 0))
    weight = pl.BlockSpec(memory_space=pl.ANY)
    vshape = jax.ShapeDtypeStruct((1, d_dim), F32)
    act_shape = jax.ShapeDtypeStruct((s_dim, f_dim), BF16)
    need = (3 * f_dim * d_dim * 2 + 2 * (3 * tm * d_dim * 4 + 2 * tm * f_dim * 2) + 2 * (tm * d_dim * 2 + 2 * tm * f_dim * 2 + tm * d_dim * 4)
            + 6 * tm * cw * 4 + 6 * tm * d_dim * 4)
    return _pcall(
        body, out_shape=(jax.ShapeDtypeStruct((s_dim, d_dim), BF16), act_shape, act_shape, jax.ShapeDtypeStruct((s_dim, d_dim), F32),
                         vshape, vshape, vshape, vshape, vshape),
        grid=(s_dim // tm,), in_specs=[row, row, act, act, row, vec, vec, vec, vec, weight, weight, weight],
        out_specs=(row, act, act, row, vec, vec, vec, vec, vec), name=name,
        scratch_shapes=3 * [pltpu.VMEM((f_dim, d_dim), BF16)] + [pltpu.SemaphoreType.DMA((3, chunks))],
        compiler_params=pltpu.CompilerParams(dimension_semantics=("arbitrary",),
                                             vmem_limit_bytes=int(min(need + VMEM_RESERVE, V7X_VMEM_BYTES - VMEM_RESERVE))),
    )(dx_out, f, g, u, x, post_g, gate, pre_g, scale, wd, wg_t, wu_t)


def _rope_tables(zero=0.0):
    half = QK_ROPE // 2
    freqs = ROPE_THETA ** (-jnp.arange(half, dtype=F32) / half)
    ang = (jnp.arange(SEQ, dtype=F32)[:, None] + zero) * freqs[None, :]
    cos, sin = jnp.cos(ang), jnp.sin(ang)
    ones = jnp.ones((SEQ, QK_NOPE), F32)
    zeros = jnp.zeros((SEQ, QK_NOPE), F32)
    pad1 = jnp.ones((SEQ, HEAD_PAD - QK_NOPE - QK_ROPE), F32)
    pad0 = jnp.zeros((SEQ, HEAD_PAD - QK_NOPE - QK_ROPE), F32)
    zh = jnp.zeros((SEQ, half), F32)
    c = jnp.concatenate([ones, cos, cos, pad1], axis=1)
    s1 = jnp.concatenate([zeros, -sin, zh, pad0], axis=1)
    s2 = jnp.concatenate([zeros, zh, sin, pad0], axis=1)
    return c, s1, s2


def _rope(v, c, s1, s2):
    half = QK_ROPE // 2
    return v * c + pltpu.roll(v, HEAD_PAD - half, 1) * s1 + pltpu.roll(v, half, 1) * s2


def _rope_t(dv, c, s1, s2):
    half = QK_ROPE // 2
    return dv * c + pltpu.roll(dv * s1, half, 1) + pltpu.roll(dv * s2, HEAD_PAD - half, 1)


def _mla_qkv(lat, q_norm, kv_norm, wq_t, wkv_t, rope, name):
    s_dim = lat.shape[0]
    width = MLA_HEADS * HEAD_PAD
    tm = 256

    def body(lat_ref, qg_ref, kg_ref, wq_ref, wkv_ref, c_ref, s1_ref, s2_ref, q_ref, k_ref, v_ref, qn_ref, kvn_ref):
        cq = lat_ref[:, :Q_LORA]
        ckv = lat_ref[:, Q_LORA:Q_LORA + KV_LORA]
        kr = lat_ref[:, Q_LORA + KV_LORA:]
        c, s1, s2 = c_ref[...], s1_ref[...], s2_ref[...]
        qn = (cq * _rstd(cq) * qg_ref[...]).astype(BF16)
        kvn = (ckv * _rstd(ckv) * kg_ref[...]).astype(BF16)
        qn_ref[...] = qn
        kvn_ref[...] = kvn
        q = _dot_nt(qn, wq_ref[...])
        kv = _dot_nt(kvn, wkv_ref[...])
        krr = _rope(kr, c, s1, s2)
        low = lax.broadcasted_iota(jnp.int32, (tm, HEAD_PAD), 1) < QK_NOPE
        for h in range(MLA_HEADS):
            sl = slice(h * HEAD_PAD, (h + 1) * HEAD_PAD)
            q_ref[:, sl] = _rope(q[:, sl], c, s1, s2).astype(BF16)
            kvh = kv[:, sl]
            k_ref[:, sl] = (jnp.where(low, kvh, 0.0) + krr).astype(BF16)
            v_ref[:, sl] = jnp.where(low, 0.0, kvh).astype(BF16)

    row = lambda n: pl.BlockSpec((tm, n), lambda i: (i, 0))
    full = lambda a: pl.BlockSpec(a.shape, lambda i: (0, 0))
    wide = jax.ShapeDtypeStruct((s_dim, width), BF16)
    return _pcall(
        body,
        out_shape=(wide, wide, wide, jax.ShapeDtypeStruct((s_dim, Q_LORA), BF16), jax.ShapeDtypeStruct((s_dim, KV_LORA), BF16)),
        grid=(s_dim // tm,),
        in_specs=[row(LAT_PAD), full(q_norm), full(kv_norm), full(wq_t), full(wkv_t), row(HEAD_PAD), row(HEAD_PAD), row(HEAD_PAD)],
        out_specs=(row(width), row(width), row(width), row(Q_LORA), row(KV_LORA)), name=name,
        compiler_params=_params(("parallel",), [((tm, LAT_PAD), F32), (wq_t.shape, BF16), (wkv_t.shape, BF16)]
                                + 3 * [((tm, width), BF16)], extra=4 * tm * width * 4),
    )(lat, q_norm, kv_norm, wq_t, wkv_t, *rope)


def _mla_scores(q, k_ref, t, tq):
    lo = t * tq
    own = slice(lo, lo + tq)
    scores = [(_dot_nt(q, k_ref[own, :]), own)]
    if t > 0:
        scores.append((_dot_nt(q, k_ref[0:lo, :]), slice(0, lo)))
    return scores


def _mla_softmax(scores):
    s_own = scores[0][0] * MLA_SCALE
    rows = lax.broadcasted_iota(jnp.int32, s_own.shape, 0)
    cols = lax.broadcasted_iota(jnp.int32, s_own.shape, 1)
    s_own = jnp.where(cols <= rows, s_own, -jnp.inf)
    mx = jnp.max(s_own, axis=-1, keepdims=True)
    if len(scores) == 1:
        e_own = jnp.exp(s_own - mx)
        return [(e_own * (1.0 / jnp.sum(e_own, axis=-1, keepdims=True)), scores[0][1])]
    s_pre = scores[1][0] * MLA_SCALE
    mx = jnp.maximum(mx, jnp.max(s_pre, axis=-1, keepdims=True))
    e_own, e_pre = jnp.exp(s_own - mx), jnp.exp(s_pre - mx)
    inv = 1.0 / (jnp.sum(e_own, axis=-1, keepdims=True) + jnp.sum(e_pre, axis=-1, keepdims=True))
    return [(e_pre * inv, scores[1][1]), (e_own * inv, scores[0][1])]


def _mla_attn_fwd(q, k, v, name):
    s_dim = q.shape[0]
    tq = MLA_QUERY_TILE

    def body(q_ref, k_ref, v_ref, o_ref):
        n_tiles = s_dim // tq
        tile_of = lambda t: slice(t * tq, (t + 1) * tq)
        def weighted_values(t, probs):
            o = None
            for p, keys in probs:
                part = _dot_nn(p, v_ref[keys, :])
                o = part if o is None else o + part
            o_ref[tile_of(t), :] = o.astype(BF16)

        scores = _mla_scores(q_ref[tile_of(0), :], k_ref, 0, tq)
        probs = None
        for t in range(n_tiles):
            ahead = _mla_scores(q_ref[tile_of(t + 1), :], k_ref, t + 1, tq) if t + 1 < n_tiles else None
            if probs is not None:
                weighted_values(t - 1, probs)
            probs = [(p.astype(BF16), keys) for p, keys in _mla_softmax(scores)]
            scores = ahead
        weighted_values(n_tiles - 1, probs)

    head = pl.BlockSpec((s_dim, HEAD_PAD), lambda h: (0, h))
    return _pcall(
        body, out_shape=jax.ShapeDtypeStruct(q.shape, BF16), grid=(MLA_HEADS,),
        in_specs=[head, head, head], out_specs=head, name=name,
        compiler_params=_params(("parallel",), 4 * [((s_dim, HEAD_PAD), BF16)], extra=4 * tq * s_dim * 4),
    )(q, k, v)


def _mla_attn_bwd(q, k, v, d_o, name):
    s_dim = q.shape[0]
    tq = MLA_QUERY_TILE

    def body(q_ref, k_ref, v_ref, do_ref, dq_ref, dk_ref, dv_ref):
        dk_ref[...] = jnp.zeros_like(dk_ref)
        dv_ref[...] = jnp.zeros_like(dv_ref)
        n_tiles = s_dim // tq
        tile_of = lambda t: slice(t * tq, (t + 1) * tq)

        def products(t):
            scores = _mla_scores(q_ref[tile_of(t), :], k_ref, t, tq)
            dot = do_ref[tile_of(t), :].astype(BF16)
            return scores, [_dot_nt(dot, v_ref[keys, :]) for _, keys in scores]

        def gradients_of_scores(scores, dps):
            probs = _mla_softmax(scores)
            dp_of = {(keys.start, keys.stop): dp for (_, keys), dp in zip(scores, dps)}
            terms = [(p, keys, dp_of[keys.start, keys.stop]) for p, keys in probs]
            row = None
            for p, _, dp in terms:
                part = jnp.sum(p * dp, axis=-1, keepdims=True)
                row = part if row is None else row + part
            return [((p * (dp - row) * MLA_SCALE).astype(BF16), p.astype(BF16), keys) for p, keys, dp in terms]

        def accumulate(t, terms):
            qt = q_ref[tile_of(t), :]
            dot = do_ref[tile_of(t), :].astype(BF16)
            dq = None
            for dsb, pb, keys in terms:
                part = _dot_nn(dsb, k_ref[keys, :])
                dq = part if dq is None else dq + part
                dk_ref[keys, :] += _dot_tn(dsb, qt)
                dv_ref[keys, :] += _dot_tn(pb, dot)
            dq_ref[tile_of(t), :] = dq

        ready = products(0)
        terms = None
        for t in range(n_tiles):
            ahead = products(t + 1) if t + 1 < n_tiles else None
            if terms is not None:
                accumulate(t - 1, terms)
            terms = gradients_of_scores(*ready)
            ready = ahead
        accumulate(n_tiles - 1, terms)

    head = pl.BlockSpec((s_dim, HEAD_PAD), lambda h: (0, h))
    out = jax.ShapeDtypeStruct(q.shape, F32)
    return _pcall(
        body, out_shape=(out, out, out), grid=(MLA_HEADS,),
        in_specs=[head, head, head, head], out_specs=(head, head, head), name=name,
        compiler_params=_params(("parallel",), 3 * [((s_dim, HEAD_PAD), BF16)] + 4 * [((s_dim, HEAD_PAD), F32)],
                                extra=6 * tq * s_dim * 4),
    )(q, k, v, d_o)


def _mla_qkv_bwd(dq, dk, dv, lat, q_norm, kv_norm, wq_t, wkv_t, rope, name):
    s_dim = lat.shape[0]
    width = MLA_HEADS * HEAD_PAD
    tm = 256

    def body(dq_ref, dk_ref, dv_ref, lat_ref, qg_ref, kg_ref, wq_ref, wkv_ref, c_ref, s1_ref, s2_ref,
             dqp_ref, dkv_ref, dlat_ref, dqg_ref, dkg_ref):
        @pl.when(pl.program_id(0) == 0)
        def _():
            dqg_ref[...] = jnp.zeros_like(dqg_ref)
            dkg_ref[...] = jnp.zeros_like(dkg_ref)

        c, s1, s2 = c_ref[...], s1_ref[...], s2_ref[...]
        lane = lax.broadcasted_iota(jnp.int32, (tm, HEAD_PAD), 1)
        low = lane < QK_NOPE
        rot = (lane >= QK_NOPE) & (lane < QK_NOPE + QK_ROPE)
        dkrr = jnp.zeros((tm, HEAD_PAD), F32)
        for h in range(MLA_HEADS):
            sl = slice(h * HEAD_PAD, (h + 1) * HEAD_PAD)
            dqp_ref[:, sl] = _rope_t(dq_ref[:, sl], c, s1, s2).astype(BF16)
            dkh = dk_ref[:, sl]
            dkv_ref[:, sl] = jnp.where(low, dkh, dv_ref[:, sl]).astype(BF16)
            dkrr = dkrr + jnp.where(rot, dkh, 0.0)
        dqn = _dot_nn(dqp_ref[...], wq_ref[...])
        dkvn = _dot_nn(dkv_ref[...], wkv_ref[...])
        cq = lat_ref[:, :Q_LORA]
        ckv = lat_ref[:, Q_LORA:Q_LORA + KV_LORA]
        rq, rkv = _rstd(cq), _rstd(ckv)
        dqg_ref[...] += jnp.sum(dqn * cq * rq, axis=0, keepdims=True)
        dkg_ref[...] += jnp.sum(dkvn * ckv * rkv, axis=0, keepdims=True)
        dlat_ref[:, :Q_LORA] = _rms_bwd(cq, rq, dqn * qg_ref[...])
        dlat_ref[:, Q_LORA:Q_LORA + KV_LORA] = _rms_bwd(ckv, rkv, dkvn * kg_ref[...])
        dlat_ref[:, Q_LORA + KV_LORA:] = _rope_t(dkrr, c, s1, s2)

    row = lambda n: pl.BlockSpec((tm, n), lambda i: (i, 0))
    full = lambda a: pl.BlockSpec(a.shape, lambda i: (0, 0))
    wide = jax.ShapeDtypeStruct((s_dim, width), BF16)
    return _pcall(
        body,
        out_shape=(wide, wide, jax.ShapeDtypeStruct((s_dim, LAT_PAD), F32),
                   jax.ShapeDtypeStruct(q_norm.shape, F32), jax.ShapeDtypeStruct(kv_norm.shape, F32)),
        grid=(s_dim // tm,),
        in_specs=[row(width), row(width), row(width), row(LAT_PAD), full(q_norm), full(kv_norm), full(wq_t), full(wkv_t),
                  row(HEAD_PAD), row(HEAD_PAD), row(HEAD_PAD)],
        out_specs=(row(width), row(width), row(LAT_PAD), full(q_norm), full(kv_norm)), name=name,
        compiler_params=_params(("arbitrary",), 3 * [((tm, width), F32)] + [((tm, LAT_PAD), F32), (wq_t.shape, BF16),
                                                                           (wkv_t.shape, BF16)] + 2 * [((tm, width), BF16)],
                                extra=2 * tm * width * 4),
    )(dq, dk, dv, lat, q_norm, kv_norm, wq_t, wkv_t, *rope)


def _t5_bucket(dist):
    max_exact = N_BUCKETS // 2
    d = jnp.maximum(dist, 1).astype(F32)
    large = max_exact + (jnp.log(d / max_exact) / math.log(MAX_DISTANCE / max_exact)
                         * (N_BUCKETS - max_exact)).astype(jnp.int32)
    large = jnp.minimum(large, N_BUCKETS - 1)
    return jnp.where(dist < max_exact, dist, large)


def _dil_buckets(dilation):
    iq = jnp.arange(DIL_BLOCK)[:, None]
    ik = jnp.arange(2 * DIL_BLOCK)[None, :]
    return _t5_bucket(jnp.maximum(DIL_BLOCK + iq - ik, 0) * dilation)


def _dil_logits(qh, kb, bias_h, first, span):
    if first:
        s = _dot_nt(qh, kb) * DIL_SCALE + bias_h[:, DIL_BLOCK:]
        rel = lax.broadcasted_iota(jnp.int32, s.shape, 0) - lax.broadcasted_iota(jnp.int32, s.shape, 1)
    else:
        s = _dot_nt(qh, kb) * DIL_SCALE + bias_h
        rel = DIL_BLOCK + lax.broadcasted_iota(jnp.int32, s.shape, 0) - lax.broadcasted_iota(jnp.int32, s.shape, 1)
    return jnp.where((rel >= 0) & (rel <= span), s, -jnp.inf)


def _dil_blocks(s_dim, dilation):
    rows = s_dim // dilation
    for r in range(dilation):
        for n in range(rows // DIL_BLOCK):
            lo = r * rows + n * DIL_BLOCK
            keys = slice(lo, lo + DIL_BLOCK) if n == 0 else slice(lo - DIL_BLOCK, lo + DIL_BLOCK)
            start = r + n * DIL_BLOCK * dilation
            tokens = slice(start, start + DIL_BLOCK) if dilation == 1 else pl.ds(start, DIL_BLOCK, stride=dilation)
            yield n == 0, slice(lo, lo + DIL_BLOCK), keys, tokens


def _dil_views(s_dim):
    col = lambda which: pl.BlockSpec((s_dim, HEAD_PAD), lambda p: (0, which * DIL_PAIRS + p))
    nat = pl.BlockSpec((s_dim, HEAD_PAD), lambda p: (0, p))
    bias = pl.BlockSpec((2, DIL_BLOCK, 2 * DIL_BLOCK), lambda p: (p, 0, 0))
    return col, nat, bias


def _dil_attn_fwd(qkv, bias, dilation, span, name):
    s_dim = qkv.shape[0]
    d_dim = DIL_HEADS * DIL_HEAD_DIM
    col, nat, bias_spec = _dil_views(s_dim)

    def body(q_ref, k_ref, v_ref, b_ref, o_ref, l_ref):
        lane = lax.broadcasted_iota(jnp.int32, (DIL_BLOCK, HEAD_PAD), 1)
        klane = lax.broadcasted_iota(jnp.int32, (2 * DIL_BLOCK, HEAD_PAD), 1)
        blocks = list(_dil_blocks(s_dim, dilation))
        for g0 in range(0, len(blocks), DIL_GROUPED):
            group = blocks[g0:g0 + DIL_GROUPED]
            logits = [_dil_logits(jnp.where((lane < DIL_HEAD_DIM) == (h == 0), q_ref[blk, :], 0), k_ref[keys, :], b_ref[h],
                                  first, span) for first, blk, keys, _ in group for h in range(2)]
            soft = []
            for lg in logits:
                mx = jnp.max(lg, axis=-1, keepdims=True)
                e = jnp.exp(lg - mx)
                tot = jnp.sum(e, axis=-1, keepdims=True)
                soft.append(((e * (1.0 / tot)).astype(BF16), mx + jnp.log(tot)))
            for i, (_, _, keys, tokens) in enumerate(group):
                vb = v_ref[keys, :]
                o_acc = jnp.zeros((DIL_BLOCK, HEAD_PAD), F32)
                lse_acc = jnp.zeros((DIL_BLOCK, HEAD_PAD), F32)
                for h in range(2):
                    p, lse = soft[2 * i + h]
                    kmine = (klane[:vb.shape[0]] < DIL_HEAD_DIM) == (h == 0)
                    o_acc = o_acc + _dot_nn(p, jnp.where(kmine, vb, 0))
                    lse_acc = jnp.where((lane < DIL_HEAD_DIM) == (h == 0), lse, lse_acc)
                o_ref[tokens, :] = o_acc
                l_ref[tokens, :] = lse_acc

    out = jax.ShapeDtypeStruct((s_dim, d_dim), F32)
    return _pcall(
        body, out_shape=(out, out), grid=(DIL_PAIRS,),
        in_specs=[col(0), col(1), col(2), bias_spec], out_specs=(nat, nat), name=name,
        compiler_params=_params(("parallel",), 3 * [((s_dim, HEAD_PAD), BF16)] + 2 * [((s_dim, HEAD_PAD), F32)]
                                + [((2, DIL_BLOCK, 2 * DIL_BLOCK), F32)], extra=2**21),
    )(qkv, qkv, qkv, bias)


def _dil_mix(lses, outs, name):
    s_dim, d_dim = outs[0].shape
    tm = TOKEN_TILE
    ng = len(outs)

    def body(*refs):
        ls = [refs[g][...] for g in range(ng)]
        mx = ls[0]
        for g in range(1, ng):
            mx = jnp.maximum(mx, ls[g])
        es = [jnp.exp(l - mx) for l in ls]
        tot = es[0]
        for g in range(1, ng):
            tot = tot + es[g]
        o = None
        for g in range(ng):
            al = es[g] / tot
            refs[2 * ng + g][...] = al
            t = al * refs[ng + g][...]
            o = t if o is None else o + t
        refs[3 * ng][...] = o
        refs[3 * ng + 1][...] = o.astype(BF16)

    row = pl.BlockSpec((tm, d_dim), lambda i: (i, 0))
    f = jax.ShapeDtypeStruct((s_dim, d_dim), F32)
    res = _pcall(
        body, out_shape=tuple(ng * [f] + [f, jax.ShapeDtypeStruct((s_dim, d_dim), BF16)]), grid=(s_dim // tm,),
        in_specs=2 * ng * [row], out_specs=tuple((ng + 2) * [row]), name=name,
        compiler_params=_params(("parallel",), (3 * ng + 2) * [((tm, d_dim), F32)], extra=4 * tm * d_dim * 4),
    )(*lses, *outs)
    return res[:ng], res[ng], res[ng + 1]


def _dil_attn_bwd(qkv, bias, d_o, o_mix, alpha, lse, dilation, span, name):
    s_dim = qkv.shape[0]
    d_dim = DIL_HEADS * DIL_HEAD_DIM
    col, nat, bias_spec = _dil_views(s_dim)

    def body(q_ref, k_ref, v_ref, b_ref, do_ref, om_ref, al_ref, l_ref, dq_ref, dk_ref, dv_ref, db_ref, dk_acc, dv_acc):
        db_ref[...] = jnp.zeros_like(db_ref)
        dk_acc[...] = jnp.zeros_like(dk_acc)
        dv_acc[...] = jnp.zeros_like(dv_acc)
        lane = lax.broadcasted_iota(jnp.int32, (DIL_BLOCK, HEAD_PAD), 1)
        klane = lax.broadcasted_iota(jnp.int32, (2 * DIL_BLOCK, HEAD_PAD), 1)
        blocks = list(_dil_blocks(s_dim, dilation))
        heads = [(lane < DIL_HEAD_DIM) == (h == 0) for h in range(2)]
        for g0 in range(0, len(blocks), DIL_GROUPED):
            group = blocks[g0:g0 + DIL_GROUPED]
            staged = []
            for first, blk, kv_rows, tokens in group:
                qb, kb, vb = q_ref[blk, :], k_ref[kv_rows, :], v_ref[kv_rows, :]
                dog = al_ref[tokens, :] * do_ref[tokens, :]
                row_term = dog * om_ref[tokens, :]
                lse_b = l_ref[tokens, :]
                for h in range(2):
                    qh = jnp.where(heads[h], qb, 0)
                    dogh = jnp.where(heads[h], dog, 0.0).astype(BF16)
                    staged.append((_dil_logits(qh, kb, b_ref[h], first, span), _dot_nt(dogh, vb), qh, dogh,
                                   jnp.max(jnp.where(heads[h], lse_b, -jnp.inf), axis=-1, keepdims=True),
                                   jnp.sum(jnp.where(heads[h], row_term, 0.0), axis=-1, keepdims=True)))
            grads = []
            for i, (logits, dp, qh, dogh, lse_h, row) in enumerate(staged):
                p = jnp.exp(logits - lse_h)
                ds = p * (dp - row)
                if group[i // 2][0]:
                    db_ref[i % 2, :, DIL_BLOCK:] += ds
                else:
                    db_ref[i % 2] += ds
                grads.append(((ds * DIL_SCALE).astype(BF16), p.astype(BF16), qh, dogh))
            for i, (_, blk, kv_rows, _) in enumerate(group):
                kb = k_ref[kv_rows, :]
                dq_acc = jnp.zeros((DIL_BLOCK, HEAD_PAD), F32)
                dk_blk = jnp.zeros((kb.shape[0], HEAD_PAD), F32)
                dv_blk = jnp.zeros((kb.shape[0], HEAD_PAD), F32)
                for h in range(2):
                    dsb, pb, qh, dogh = grads[2 * i + h]
                    kmine = (klane[:kb.shape[0]] < DIL_HEAD_DIM) == (h == 0)
                    dq_acc = dq_acc + _dot_nn(dsb, jnp.where(kmine, kb, 0))
                    dk_blk = dk_blk + _dot_tn(dsb, qh)
                    dv_blk = dv_blk + _dot_tn(pb, dogh)
                dq_ref[blk, :] = dq_acc.astype(BF16)
                dk_acc[kv_rows, :] += dk_blk
                dv_acc[kv_rows, :] += dv_blk
        dk_ref[...] = dk_acc[...].astype(BF16)
        dv_ref[...] = dv_acc[...].astype(BF16)

    grad = jax.ShapeDtypeStruct((s_dim, d_dim), BF16)
    return _pcall(
        body, out_shape=(grad, grad, grad, jax.ShapeDtypeStruct(bias.shape, F32)), grid=(DIL_PAIRS,),
        in_specs=[col(0), col(1), col(2), bias_spec, nat, nat, nat, nat],
        out_specs=(nat, nat, nat, bias_spec), name=name,
        scratch_shapes=[pltpu.VMEM((s_dim, HEAD_PAD), F32), pltpu.VMEM((s_dim, HEAD_PAD), F32)],
        compiler_params=_params(("parallel",), 6 * [((s_dim, HEAD_PAD), BF16)] + 4 * [((s_dim, HEAD_PAD), F32)]
                                + 2 * [((2, DIL_BLOCK, 2 * DIL_BLOCK), F32)], extra=2 * s_dim * HEAD_PAD * 4 + 2**21),
    )(qkv, qkv, qkv, bias, d_o, o_mix, alpha, lse)


def _bias_reduce(dbias, buckets, name):
    n_heads = dbias.shape[0]

    def body(db_ref, bk_ref, o_ref):
        ds, bk = db_ref[0], bk_ref[0]
        lane = lax.broadcasted_iota(jnp.int32, (8, HEAD_PAD), 1)
        acc = jnp.zeros((8, HEAD_PAD), F32)
        for b in range(N_BUCKETS):
            acc = jnp.where(lane == b, jnp.sum(jnp.where(bk == b, ds, 0.0)), acc)
        o_ref[0] = acc

    blk = (1, DIL_BLOCK, 2 * DIL_BLOCK)
    return _pcall(
        body, out_shape=jax.ShapeDtypeStruct((n_heads, 8, HEAD_PAD), F32), grid=(n_heads,),
        in_specs=[pl.BlockSpec(blk, lambda h: (h, 0, 0)), pl.BlockSpec(blk, lambda h: (h // DIL_HEADS, 0, 0))],
        out_specs=pl.BlockSpec((1, 8, HEAD_PAD), lambda h: (h, 0, 0)), name=name,
        compiler_params=_params(("parallel",), [(blk, F32), (blk, jnp.int32)], extra=2**20),
    )(dbias, buckets)


def _loss_grad(y, target, name):
    s_dim, d_dim = y.shape
    tm = TOKEN_TILE

    def body(y_ref, t_ref, dy_ref, l_ref):
        @pl.when(pl.program_id(0) == 0)
        def _():
            l_ref[...] = jnp.zeros_like(l_ref)

        err = y_ref[...] - t_ref[...]
        dy_ref[...] = err / d_dim
        sq = (err * err).reshape(tm // 8, 8, d_dim)
        l_ref[...] += 0.5 * jnp.sum(sq, axis=0) / d_dim

    row = pl.BlockSpec((tm, d_dim), lambda i: (i, 0))
    acc = pl.BlockSpec((8, d_dim), lambda i: (0, 0))
    return _pcall(
        body, out_shape=(jax.ShapeDtypeStruct((s_dim, d_dim), F32), jax.ShapeDtypeStruct((8, d_dim), F32)),
        grid=(s_dim // tm,), in_specs=[row, row], out_specs=(row, acc), name=name,
        compiler_params=_params(("arbitrary",), 3 * [((tm, d_dim), F32)], extra=2 * tm * d_dim * 4),
    )(y, target)


def _mod_fwd(c_all, w_mod, b_loc, name):
    depth, d_dim, n = w_mod.shape
    nb = c_all.shape[0]

    def body(c_ref, w_ref, b_ref, o_ref, s_ref):
        cv = c_ref[...]
        sc = cv * jax.nn.sigmoid(cv)
        s_ref[...] = sc
        o_ref[0] = _dot_nn(sc.astype(BF16), w_ref[0].astype(BF16)) + b_ref[0]

    return _pcall(
        body, out_shape=(jax.ShapeDtypeStruct((depth, nb, n), F32), jax.ShapeDtypeStruct((nb, d_dim), F32)), grid=(depth,),
        in_specs=[pl.BlockSpec((nb, d_dim), lambda i: (0, 0)), pl.BlockSpec((1, d_dim, n), lambda i: (i, 0, 0)),
                  pl.BlockSpec((1, 1, n), lambda i: (i, 0, 0))],
        out_specs=(pl.BlockSpec((1, nb, n), lambda i: (i, 0, 0)), pl.BlockSpec((nb, d_dim), lambda i: (0, 0))), name=name,
        compiler_params=_params(("arbitrary",), [((1, d_dim, n), F32)], extra=d_dim * n * 2 + 2**20),
    )(c_all, w_mod, b_loc.reshape(depth, 1, n))


def _sum_parts(parts, name, transpose=False):
    _, rows, cols = parts.shape
    unit = 128 if transpose else 16
    budget = (7 if transpose else 3) * 2**20
    fits = [t for t in range(unit, rows // 2 + 1, unit) if rows % t == 0 and NDEV * t * cols * parts.dtype.itemsize <= budget]
    tr = max(fits) if fits else rows

    def body(p_ref, o_ref):
        acc = p_ref[0].astype(F32)
        for k in range(1, NDEV):
            acc = acc + p_ref[k].astype(F32)
        o_ref[...] = acc.T if transpose else acc

    out_shape, out_block = ((cols, rows), (cols, tr)) if transpose else ((rows, cols), (tr, cols))
    return _pcall(
        body, out_shape=jax.ShapeDtypeStruct(out_shape, F32), grid=(rows // tr,),
        in_specs=[pl.BlockSpec((NDEV, tr, cols), lambda i: (0, i, 0))],
        out_specs=pl.BlockSpec(out_block, (lambda i: (0, i)) if transpose else (lambda i: (i, 0))),
        name=name, compiler_params=_params(("parallel",), [((NDEV, tr, cols), parts.dtype), (out_block, F32)], extra=2**22),
    )(parts)


def _adamw(w, g, m, v, name):
    shape = w.shape
    cols = shape[-1]
    rows = math.prod(shape[:-1])
    tr = rows
    for cand in (2048, 1024, 512, 256, 128, 64, 32, 16, 8):
        if rows % cand == 0 and rows > cand and cand * cols * 4 <= 2**21:
            tr = cand
            break

    def body(w_ref, g_ref, m_ref, v_ref, d_ref, mo_ref, vo_ref):
        gv = g_ref[...]
        mn = ADAM_B1 * m_ref[...] + (1.0 - ADAM_B1) * gv
        vn = ADAM_B2 * v_ref[...] + (1.0 - ADAM_B2) * (gv * gv)
        m_hat = mn / (1.0 - ADAM_B1 ** ADAM_STEP)
        v_hat = vn / (1.0 - ADAM_B2 ** ADAM_STEP)
        d_ref[...] = -ADAM_LR * (m_hat / (jnp.sqrt(v_hat) + ADAM_EPS) + ADAM_WD * w_ref[...])
        mo_ref[...] = mn
        vo_ref[...] = vn

    blk = pl.BlockSpec((tr, cols), lambda i: (i, 0))
    out = jax.ShapeDtypeStruct((rows, cols), F32)
    res = _pcall(
        body, out_shape=(out, out, out), grid=(rows // tr,), in_specs=4 * [blk], out_specs=(blk, blk, blk), name=name,
        compiler_params=_params(("parallel",), 7 * [((tr, cols), F32)], extra=4 * tr * cols * 4),
    )(*(a.reshape(rows, cols) for a in (w, g, m, v)))
    return tuple(r.reshape(shape) for r in res)


def _peers():
    x, y, c = lax.axis_index("x"), lax.axis_index("y"), lax.axis_index("c")
    flip = lambda v, f: 1 - v if f else v
    peers = []
    for f in range(1, NDEV):
        px, py, pc = flip(x, f & 4), flip(y, f & 2), flip(c, f & 1)
        peers.append(((px, py, pc), 4 * px + 2 * py + pc))
    return (x, y, c), 4 * x + 2 * y + c, peers


def _places():
    x, y, c = lax.axis_index("x"), lax.axis_index("y"), lax.axis_index("c")
    place = lambda px, py, pc: ((px, py, pc), 4 * px + 2 * py + pc)
    return place(x, y, c), place(x, y, 1 - c), [place(1 - x, y, c), place(x, 1 - y, c), place(1 - x, 1 - y, c)]


def _exchange(arrs, gather, name):
    n = len(arrs)
    hbm = pl.BlockSpec(memory_space=pltpu.HBM)
    if gather:
        out_shape = [jax.ShapeDtypeStruct((NDEV * a.shape[0], a.shape[1]), a.dtype) for a in arrs]
    else:
        out_shape = [jax.ShapeDtypeStruct((NDEV, a.shape[0] // NDEV, a.shape[1]), a.dtype) for a in arrs]

    def body(*refs):
        ins, outs = refs[:n], refs[n:2 * n]
        send_sems, recv_sems, local_sems = refs[2 * n:]
        me_pos, me, peers = _peers()
        local = []
        for k in range(n):
            rows = arrs[k].shape[0] if gather else arrs[k].shape[0] // NDEV
            if gather:
                src_of = lambda idx: ins[k]
                dst_of = lambda idx: outs[k].at[pl.ds(me * rows, rows)]
                mine = (ins[k], outs[k].at[pl.ds(me * rows, rows)])
            else:
                src_of = lambda idx: ins[k].at[pl.ds(idx * rows, rows)]
                dst_of = lambda idx: outs[k].at[me]
                mine = (ins[k].at[pl.ds(me * rows, rows)], outs[k].at[me])
            cp = pltpu.make_async_copy(mine[0], mine[1], local_sems.at[k])
            cp.start()
            local.append(cp)
            for pos, idx in peers:
                pltpu.make_async_remote_copy(src_ref=src_of(idx), dst_ref=dst_of(idx), send_sem=send_sems.at[k],
                                             recv_sem=recv_sems.at[k], device_id=pos, device_id_type=MESH).start()
        for k in range(n):
            rows = arrs[k].shape[0] if gather else arrs[k].shape[0] // NDEV
            sent = ins[k].at[pl.ds(0, (NDEV - 1) * rows)] if not gather else outs[k].at[pl.ds(0, (NDEV - 1) * rows)]
            got = outs[k].at[pl.ds(0, (NDEV - 1) * rows)] if gather else outs[k].at[pl.ds(0, NDEV - 1)]
            pltpu.make_async_remote_copy(src_ref=sent, dst_ref=sent, send_sem=send_sems.at[k], recv_sem=recv_sems.at[k],
                                         device_id=me_pos, device_id_type=MESH).wait_send()
            pltpu.make_async_remote_copy(src_ref=got, dst_ref=got, send_sem=send_sems.at[k], recv_sem=recv_sems.at[k],
                                         device_id=me_pos, device_id_type=MESH).wait_recv()
            local[k].wait()

    return pl.pallas_call(
        body, out_shape=out_shape, in_specs=n * [hbm], out_specs=n * [hbm], name=name,
        scratch_shapes=[pltpu.SemaphoreType.DMA((n,)), pltpu.SemaphoreType.DMA((n,)), pltpu.SemaphoreType.DMA((n,))],
        compiler_params=pltpu.CompilerParams(has_side_effects=True),
    )(*arrs)


_HBM = pl.BlockSpec(memory_space=pltpu.HBM)
_SEM = pl.BlockSpec(memory_space=pltpu.SEMAPHORE)
_DATAFLOW = pltpu.SideEffectType.DATAFLOW_SIDE_EFFECTING


def _split_start(srcs, groups, gather, name, after=None):
    n = len(srcs)
    if gather:
        lands = [lax.empty((NDEV * a.shape[0], a.shape[1]), a.dtype) for a in srcs]
    else:
        lands = [lax.empty((NDEV, a.shape[0] // NDEV, a.shape[1]), a.dtype) for a in srcs]
    n_sem = 3 * len(groups)
    extra = [] if after is None else [after]
    n_in = 2 * n + len(extra)

    def body(*refs):
        src_refs, land_refs = refs[:n], refs[n:2 * n]
        sems = refs[n_in:n_in + n_sem]
        token = refs[-1]
        (_, my), sibling, chips = _places()
        _, _, peers = _peers()
        targets = [sibling] + chips if gather else peers
        for g, members in enumerate(groups):
            for j, k in enumerate(members):
                _own_copy(src_refs[k], land_refs[k], sems[3 * g + 2].at[j], my, gather).start()
        for g, members in enumerate(groups):
            for j, k in enumerate(members):
                rows = srcs[k].shape[0] if gather else srcs[k].shape[0] // NDEV
                for pos, idx in targets:
                    src = src_refs[k] if gather else src_refs[k].at[pl.ds(idx * rows, rows)]
                    dst = land_refs[k].at[pl.ds(my * rows, rows)] if gather else land_refs[k].at[my]
                    pltpu.make_async_remote_copy(src_ref=src, dst_ref=dst, send_sem=sems[3 * g].at[j],
                                                 recv_sem=sems[3 * g + 1].at[j], device_id=pos, device_id_type=MESH).start()
        token[...] = jnp.zeros_like(token)

    out_shape = []
    for members in groups:
        out_shape += 3 * [pltpu.SemaphoreType.DMA((len(members),))]
    out_shape += [pltpu.HBM(a.shape, a.dtype) for a in srcs] + [pltpu.HBM(a.shape, a.dtype) for a in lands]
    out_shape.append(jax.ShapeDtypeStruct((8, 128), F32))
    res = pl.pallas_call(
        body, name=name, out_shape=tuple(out_shape), in_specs=2 * n * [_HBM] + len(extra) * [pl.BlockSpec(memory_space=pl.ANY)],
        out_specs=tuple(n_sem * [_SEM] + 2 * n * [_HBM] + [pl.BlockSpec(memory_space=pltpu.VMEM)]),
        input_output_aliases={i: n_sem + i for i in range(2 * n)},
        compiler_params=pltpu.CompilerParams(has_side_effects=_DATAFLOW),
    )(*[pltpu.with_memory_space_constraint(a, pltpu.HBM) for a in list(srcs) + lands], *extra)
    sems = [tuple(res[3 * g:3 * g + 3]) for g in range(len(groups))]
    return sems, list(res[n_sem:n_sem + n]), list(res[n_sem + n:n_sem + 2 * n]), res[-1]


def _own_copy(src_ref, land_ref, sem, my, gather):
    if gather:
        rows = src_ref.shape[0]
        return pltpu.make_async_copy(src_ref, land_ref.at[pl.ds(my * rows, rows)], sem)
    rows = src_ref.shape[0] // NDEV
    return pltpu.make_async_copy(src_ref.at[pl.ds(my * rows, rows)], land_ref.at[my], sem)


def _wait_all(land_ref, blocks_per_dev, copies, send_sem, recv_sem, me_pos):
    part = land_ref.at[pl.ds(0, copies * blocks_per_dev)]
    pltpu.make_async_remote_copy(src_ref=part, dst_ref=part, send_sem=send_sem, recv_sem=recv_sem,
                                 device_id=me_pos, device_id_type=MESH).wait()


def _gather_forward(sems, srcs, lands, after, name):
    n = len(srcs)

    def body(*refs):
        land_refs = refs[n:2 * n]
        send_a, recv_a = refs[2 * n], refs[2 * n + 1]
        send_b, recv_b = refs[2 * n + 3], refs[2 * n + 4]
        token = refs[-1]
        (me_pos, _), sibling, chips = _places()
        for j in range(n):
            _wait_all(land_refs[j], lands[j].shape[0] // NDEV, 1 + OTHER_CHIPS, send_a.at[j], recv_a.at[j], me_pos)
        for j in range(n):
            rows = lands[j].shape[0] // NDEV
            for _, idx in chips:
                block = land_refs[j].at[pl.ds(idx * rows, rows)]
                pltpu.make_async_remote_copy(src_ref=block, dst_ref=block, send_sem=send_b.at[j], recv_sem=recv_b.at[j],
                                             device_id=sibling[0], device_id_type=MESH).start()
        token[...] = jnp.zeros_like(token)

    res = pl.pallas_call(
        body, name=name,
        out_shape=(pltpu.SemaphoreType.DMA((n,)), pltpu.SemaphoreType.DMA((n,)))
        + tuple(pltpu.HBM(a.shape, a.dtype) for a in list(srcs) + list(lands)) + (jax.ShapeDtypeStruct((8, 128), F32),),
        in_specs=2 * n * [_HBM] + [_SEM, _SEM, pl.BlockSpec(memory_space=pl.ANY)],
        out_specs=tuple([_SEM, _SEM] + 2 * n * [_HBM] + [pl.BlockSpec(memory_space=pltpu.VMEM)]),
        input_output_aliases={i: 2 + i for i in range(2 * n)},
        compiler_params=pltpu.CompilerParams(has_side_effects=_DATAFLOW),
    )(*srcs, *lands, sems[0], sems[1], after)
    return (res[0], res[1]), list(res[2:2 + n]), list(res[2 + n:2 + 2 * n]), res[-1]


def _split_wait(sems, srcs, lands, after, copies, gather, name):
    n = len(srcs)

    def body(*refs):
        src_refs, land_refs = refs[:n], refs[n:2 * n]
        send_sem, recv_sem, local_sem = refs[2 * n], refs[2 * n + 1], refs[2 * n + 2]
        (me_pos, my), _, _ = _places()
        for j in range(n):
            _wait_all(land_refs[j], lands[j].shape[0] // NDEV, copies, send_sem.at[j], recv_sem.at[j], me_pos)
            _own_copy(src_refs[j], land_refs[j], local_sem.at[j], my, gather).wait()

    res = pl.pallas_call(
        body, name=name, out_shape=tuple(pltpu.HBM(a.shape, a.dtype) for a in list(srcs) + list(lands)),
        in_specs=2 * n * [_HBM] + [_SEM, _SEM, _SEM, pl.BlockSpec(memory_space=pl.ANY)], out_specs=tuple(2 * n * [_HBM]),
        input_output_aliases={i: i for i in range(2 * n)},
        compiler_params=pltpu.CompilerParams(has_side_effects=_DATAFLOW),
    )(*srcs, *lands, sems[0], sems[1], sems[2], after)
    return list(res[n:])


def _chained(gate, mid, after):
    return gate if mid is None else gate + mid(after)[:1, :1]


def _ffn_fwd(x, norms, mod, w, mid=None):
    (pre_g, post_g), (shift, scale, gate), (wg_t, wu_t, wd) = norms, mod, w
    if not callable(wd):
        hn, g, u, a, x_out, f = _ffn_fwd_fused(x, pre_g, scale, shift, post_g, _chained(gate, mid, x), wg_t, wu_t, wd, "ffn_fwd")
        return x_out, (x, hn, g, u, a, f), (wg_t, wu_t, wd)
    hn, g, u, a = _ffn_up(x, pre_g, scale, shift, wg_t, wu_t, "ffn_up")
    wd = wd(a)
    x_out, f = _mm_post(a, wd, x, post_g, _chained(gate, mid, a), FFN_RES, "ffn_down")
    return x_out, (x, hn, g, u, a, f), (wg_t, wu_t, wd)


def _ffn_bwd(dx_out, saved, norms, mod, w, send=None):
    (pre_g, post_g), (_, scale, gate), (wg_t, wu_t, wd) = norms, mod, w
    x, hn, g, u, a, f = saved
    d_model = x.shape[1]
    if send is None:
        df, dg, du, dx, dgate, dpost, dshift, dscale, dpre = _ffn_bwd_fused(dx_out, saved, pre_g, post_g, scale, gate,
                                                                            wg_t, wu_t, wd, "ffn_bwd")
        return dx, (dpre, dpost), (dshift, dscale, dgate), tuple(_ffn_dw(dg, du, a, hn, df, "ffn_dw3"))
    sent = send
    df, dgate, dpost = _post_bwd(dx_out, f, post_g, gate, FFN_RES, "ffn_post_bwd")
    dwd = _mm([(a, df)], "tn", BF16, 256, d_model, "ffn_dw")
    dg, du = _ffn_dgu(df, wd, g, u, "ffn_dgu", after=sent(2, dwd))
    dwg_t = _mm([(dg, hn)], "tn", BF16, 256, d_model, "ffn_dw")
    dwu_t = _mm([(du, hn)], "tn", BF16, 256, d_model, "ffn_dw", after=sent(0, dwg_t))
    dhn = _mm([(dg, wg_t), (du, wu_t)], "nn", F32, TOKEN_TILE, d_model, "ffn_dhn", after=sent(1, dwu_t))
    dx, dshift, dscale, dpre = _prenorm_bwd(dx_out, [dhn], x, pre_g, scale, "prenorm_bwd")
    return dx, (dpre, dpost), (dshift, dscale, dgate), (dwg_t, dwu_t, dwd)


def _mla_fwd(x, norms, mod, w, rope, mid=None):
    (pre_g, post_g), (shift, scale, gate) = norms, mod
    w_in, q_norm, wq_t, kv_norm, wkv_t, wo = w
    hn, lat = _prenorm_mm(x, pre_g, scale, shift, w_in, "nn", F32, LAT_PAD, "mla_in")
    gate = _chained(gate, mid, lat)
    q, k, v, qn, kvn = _mla_qkv(lat, q_norm, kv_norm, wq_t, wkv_t, rope, "mla_qkv")
    o = _mla_attn_fwd(q, k, v, "mla_attn_fwd")
    x_out, f = _mm_post(o, wo, x, post_g, gate, 1.0, "mla_out")
    return x_out, (x, hn, lat, q, k, v, qn, kvn, o, f)


def _mla_bwd(dx_out, saved, norms, mod, w, rope):
    (pre_g, post_g), (_, scale, gate) = norms, mod
    w_in, q_norm, wq_t, kv_norm, wkv_t, wo = w
    x, hn, lat, q, k, v, qn, kvn, o, f = saved
    d_model = x.shape[1]
    df, dgate, dpost = _post_bwd(dx_out, f, post_g, gate, 1.0, "mix_post_bwd")
    d_o = _mm([(df, wo)], "nt", F32, TOKEN_TILE, wo.shape[0], "mla_do")
    dwo = _mm([(o, df)], "tn", BF16, TOKEN_TILE, d_model, "mla_dwo")
    dq, dk, dv = _mla_attn_bwd(q, k, v, d_o, "mla_attn_bwd")
    dqp, dkv, dlat, dq_norm, dkv_norm = _mla_qkv_bwd(dq, dk, dv, lat, q_norm, kv_norm, wq_t, wkv_t, rope, "mla_qkv_bwd")
    dwq_t = _mm([(dqp, qn)], "tn", BF16, TOKEN_TILE, Q_LORA, "mla_dwq")
    dwkv_t = _mm([(dkv, kvn)], "tn", BF16, TOKEN_TILE, KV_LORA, "mla_dwkv")
    dw_in = _mm([(hn, dlat)], "tn", BF16, TOKEN_TILE, LAT_PAD, "mla_dwin")
    dhn = _mm([(dlat, w_in)], "nt", F32, TOKEN_TILE, d_model, "mla_dhn")
    dx, dshift, dscale, dpre = _prenorm_bwd(dx_out, [dhn], x, pre_g, scale, "prenorm_bwd")
    return dx, (dpre, dpost), (dshift, dscale, dgate), (dw_in, dq_norm, dwq_t, dkv_norm, dwkv_t, dwo)


def _dil_fwd(x, norms, mod, w, bias, mid=None):
    (pre_g, post_g), (shift, scale, gate), (w_in_t, wo) = norms, mod, w
    width = 3 * DIL_HEADS * DIL_HEAD_DIM
    hns, qkvs, outs, lses = [], [], [], []
    for g, (window, dilation) in enumerate(DIL_GROUPS):
        hn, qkv = _prenorm_mm(x, pre_g, scale, shift, w_in_t, "nt", BF16, width, "dil_in", perm=dilation,
                              w_rows=(g * width, width))
        if g == 0:
            gate = _chained(gate, mid, qkv)
        o, lse = _dil_attn_fwd(qkv, bias[g], dilation, window // dilation, "dil_attn_fwd")
        hns.append(hn), qkvs.append(qkv), outs.append(o), lses.append(lse)
    alphas, o_mix, o_mix_b = _dil_mix(lses, outs, "dil_mix")
    x_out, f = _mm_post(o_mix_b, wo, x, post_g, gate, 1.0, "dil_out")
    return x_out, (x, hns, qkvs, lses, alphas, o_mix, o_mix_b, f)


def _dil_bwd(dx_out, saved, norms, mod, w, bias):
    (pre_g, post_g), (_, scale, gate), (w_in_t, wo) = norms, mod, w
    x, hns, qkvs, lses, alphas, o_mix, o_mix_b, f = saved
    d_model = x.shape[1]
    inner = DIL_HEADS * DIL_HEAD_DIM
    df, dgate, dpost = _post_bwd(dx_out, f, post_g, gate, 1.0, "mix_post_bwd")
    d_o = _mm([(df, wo)], "nt", F32, TOKEN_TILE, inner, "dil_do")
    dwo = _mm([(o_mix_b, df)], "tn", BF16, TOKEN_TILE, d_model, "dil_dwo")
    dhns, dws, dbs = [], [], []
    for g, (window, dilation) in enumerate(DIL_GROUPS):
        grads = _dil_attn_bwd(qkvs[g], bias[g], d_o, o_mix, alphas[g], lses[g], dilation, window // dilation, "dil_attn_bwd")
        dbs.append(grads[3])
        dhns.append(_mm([(grads[j], w_in_t) for j in range(3)], "nn", F32, TOKEN_TILE, d_model, "dil_dhn", out_perm=dilation,
                        b_rows=[(3 * g + j) * inner for j in range(3)]))
        dws += list(_mm_tn_shared(list(grads[:3]), hns[g], "dil_dwin"))
    dx, dshift, dscale, dpre = _prenorm_bwd(dx_out, dhns, x, pre_g, scale, "prenorm_bwd3")
    return dx, (dpre, dpost), (dshift, dscale, dgate), (jnp.concatenate(dws, axis=0), dwo), jnp.concatenate(dbs, axis=0)


def _pad_rows(a, rows):
    return jnp.pad(a, ((0, rows - a.shape[0]), (0, 0)))


def _lanes(a):
    flat = a.reshape(-1).astype(F32)
    rows = -(-flat.shape[0] // 1024) * 8
    return jnp.pad(flat, (0, rows * 128 - flat.shape[0])).reshape(rows, 128)


def kernel(x, c, norm_pre, norm_post, w_mod, b_mod, ffn_w_gate, ffn_w_up, ffn_w_down, mla_w_in, mla_q_norm, mla_w_q_up, mla_kv_norm, mla_w_kv_up, mla_w_o, dil_w_in, dil_w_o, rel_bias, loss_target, m_norm_pre, m_norm_post, m_w_mod, m_b_mod, m_ffn_w_gate, m_ffn_w_up, m_ffn_w_down, m_mla_w_in, m_mla_q_norm, m_mla_w_q_up, m_mla_kv_norm, m_mla_w_kv_up, m_mla_w_o, m_dil_w_in, m_dil_w_o, m_rel_bias, v_norm_pre, v_norm_post, v_w_mod, v_b_mod, v_ffn_w_gate, v_ffn_w_up, v_ffn_w_down, v_mla_w_in, v_mla_q_norm, v_mla_w_q_up, v_mla_kv_norm, v_mla_w_kv_up, v_mla_w_o, v_dil_w_in, v_dil_w_o, v_rel_bias):
    me = 4 * lax.axis_index("x") + 2 * lax.axis_index("y") + lax.axis_index("c")
    depth, n_sub, d_loc = norm_pre.shape
    d_model = x.shape[2]
    mod_loc_cols = w_mod.shape[2]
    x0, target = x[0], loss_target[0]

    bf_t = lambda a: a.astype(BF16).T
    ffn_ids = [(i, h) for i in range(depth) for h in range(2)]
    shards = []
    for i, h in ffn_ids:
        shards += [bf_t(ffn_w_gate[i, h]), bf_t(ffn_w_up[i, h]), ffn_w_down[i, h].astype(BF16)]
    shards += [mla_w_in[0].astype(BF16), bf_t(mla_w_q_up[0]), bf_t(mla_w_kv_up[0]), mla_w_o[0].astype(BF16),
               bf_t(dil_w_in[0]), dil_w_o[0].astype(BF16)]
    n_ffn = 3 * len(ffn_ids)
    members = {(0, 0): [0, 1, 2], (0, 1): [n_ffn, n_ffn + 1, n_ffn + 2, n_ffn + 3], (0, 2): [3, 4, 5],
               (1, 0): [6, 7, 8], (1, 1): [n_ffn + 4, n_ffn + 5], (1, 2): [9, 10, 11]}
    order = [(i, s) for i in range(depth) for s in range(n_sub)]

    small = jnp.concatenate([c.reshape(8, 128), _pad_rows(norm_pre.reshape(depth * n_sub, d_loc), 8),
                             _pad_rows(norm_post.reshape(depth * n_sub, d_loc), 8)], axis=0)
    small_all = _exchange([small], True, "gather_small")[0].reshape(NDEV, 24, 128)
    c_all = small_all[:, 0:8].reshape(NDEV, d_model)
    gains = lambda lo: jnp.transpose(small_all[:, lo:lo + depth * n_sub], (1, 0, 2)).reshape(depth, n_sub, 1, d_model)
    pre_full, post_full = gains(8), gains(16)

    b_loc = lax.dynamic_slice(b_mod, (0, me * mod_loc_cols), (depth, mod_loc_cols))
    mod_cols, silu_c = _mod_fwd(c_all, w_mod, b_loc, "mod_fwd")
    mod_all = _exchange([mod_cols.reshape(depth * NDEV, mod_loc_cols)], True, "gather_mod")[0]
    mod_all = mod_all.reshape(NDEV, depth, NDEV, mod_loc_cols)
    mod_mine = lax.dynamic_index_in_dim(mod_all, me, axis=2, keepdims=False)
    mod = jnp.transpose(mod_mine, (1, 0, 2)).reshape(depth, n_sub, 3, 1, d_model)

    first = order[0]
    stages = [("%d%d" % first, members[first][:2]), ("%d%dd" % first, members[first][2:])]
    stages += [("%d%d" % key, members[key]) for key in order[1:]]
    started = {}

    def start(these, name, after):
        used = [k for _, idx in these for k in idx]
        sems, srcs, lands, token = _split_start([shards[k] for k in used], [[used.index(k) for k in idx] for _, idx in these],
                                                True, name, after)
        for n, (stage, idx) in enumerate(these):
            started[stage] = (sems[n], [srcs[used.index(k)] for k in idx], [lands[used.index(k)] for k in idx])
        return token

    g_token = start(stages[:2], "gather_weights_start_first", mod_all)
    start(stages[2:], "gather_weights_start_rest", g_token)

    forwarded = {}

    def forward(stage, after):
        sems, srcs, lands = started[stage]
        forwarded[stage] = _gather_forward(sems, srcs, lands, after, "gather_forward_" + stage)
        return forwarded[stage][3]

    def weights_of(stage, after):
        (send_b, recv_b), srcs, lands, _ = forwarded[stage]
        return _split_wait((send_b, recv_b, started[stage][0][2]), srcs, lands, after, OTHER_CHIPS, True, "gather_wait_" + stage)

    def late_down(after):
        forward("%d%dd" % first, after)
        return weights_of("%d%dd" % first, after)[0]

    lat_real = Q_LORA + KV_LORA
    qk = QK_NOPE + QK_ROPE

    def mla_weights(after):
        w_in, wq_t, wkv_t, wo = weights_of("01", after)
        w_in_pad = jnp.concatenate([w_in[:, :lat_real], jnp.zeros((d_model, QK_NOPE), BF16), w_in[:, lat_real:],
                                    jnp.zeros((d_model, HEAD_PAD - QK_NOPE - QK_ROPE), BF16)], axis=1)
        wq_pad = jnp.pad(wq_t.reshape(MLA_HEADS, qk, Q_LORA), ((0, 0), (0, HEAD_PAD - qk), (0, 0)))
        wo_pad = jnp.pad(wo.reshape(MLA_HEADS, V_HEAD, d_model), ((0, 0), (HEAD_PAD - V_HEAD, 0), (0, 0)))
        return (w_in_pad, mla_q_norm, wq_pad.reshape(MLA_HEADS * HEAD_PAD, Q_LORA), mla_kv_norm, wkv_t,
                wo_pad.reshape(MLA_HEADS * HEAD_PAD, d_model))

    zero = g_token[0, 0]
    rope = _rope_tables(zero)
    buckets = jnp.stack([_dil_buckets(dil) for _, dil in DIL_GROUPS]) + zero.astype(jnp.int32)
    onehot = (buckets[..., None] == jnp.arange(N_BUCKETS)).astype(F32)
    bias = jnp.einsum("gqkb,bgh->ghqk", onehot, rel_bias.reshape(N_BUCKETS, len(DIL_GROUPS), DIL_HEADS),
                      precision=lax.Precision.HIGHEST)

    norms = lambda i, s: (pre_full[i, s], post_full[i, s])
    mods = lambda i, s: (mod[i, s, 0], mod[i, s, 1], mod[i, s, 2])
    saved, weights = {}, {}
    h = x0
    forward("%d%d" % first, bias)
    for n, (i, s) in enumerate(order):
        got = mla_weights(h) if (s == 1 and i % 2 == 0) else tuple(weights_of("%d%d" % (i, s), h))
        mid = None if n + 1 == len(order) else (lambda after, nxt="%d%d" % order[n + 1]: forward(nxt, after))
        if s != 1:
            if len(got) == 3:
                h, saved[i, s], weights[i, s] = _ffn_fwd(h, norms(i, s), mods(i, s), got)
                if mid is not None:
                    mid(h)
            else:
                h, saved[i, s], weights[i, s] = _ffn_fwd(h, norms(i, s), mods(i, s), (*got, late_down), mid)
            continue
        weights[i, s] = got
        if i % 2 == 0:
            h, saved[i, s] = _mla_fwd(h, norms(i, s), mods(i, s), weights[i, s], rope, mid)
        else:
            h, saved[i, s] = _dil_fwd(h, norms(i, s), mods(i, s), weights[i, s], bias, mid)
    dh, loss_parts = _loss_grad(h, target, "loss")

    dnorm, dmod, sent = {}, {}, {}
    token = jnp.zeros((8, 128), F32)
    last = order[0]

    def send_last(j, dw):
        sent[last, j] = _split_start([dw], [[0]], False, "scatter_start_%d%d_%d" % (*last, j))
        return sent[last, j][3]

    for i, s in reversed(order):
        md = mods(i, s)
        md = (md[0], md[1], md[2] + token[:1, :1])
        if (i, s) == last:
            dh, dnorm[i, s], dmod[i, s], _ = _ffn_bwd(dh, saved[i, s], norms(i, s), md, weights[i, s], send_last)
            continue
        if s != 1:
            dh, dnorm[i, s], dmod[i, s], dws = _ffn_bwd(dh, saved[i, s], norms(i, s), md, weights[i, s])
        elif i % 2 == 0:
            dh, dnorm[i, s], dmod[i, s], dmla = _mla_bwd(dh, saved[i, s], norms(i, s), md, weights[i, s], rope)
            dw_in_pad, dq_norm, dwq_pad, dkv_norm, dwkv_t, dwo_pad = dmla
            dw_in = jnp.concatenate([dw_in_pad[:, :lat_real], dw_in_pad[:, lat_real + QK_NOPE:lat_real + qk]], axis=1)
            dwq_t = dwq_pad.reshape(MLA_HEADS, HEAD_PAD, Q_LORA)[:, :qk].reshape(MLA_HEADS * qk, Q_LORA)
            dwo = dwo_pad.reshape(MLA_HEADS, HEAD_PAD, d_model)[:, HEAD_PAD - V_HEAD:].reshape(MLA_HEADS * V_HEAD, d_model)
            dws = (dw_in, dwq_t, dwkv_t, dwo)
        else:
            dh, dnorm[i, s], dmod[i, s], dws, dbias = _dil_bwd(dh, saved[i, s], norms(i, s), md, weights[i, s], bias)
        sent[i, s] = _split_start(list(dws), [list(range(len(dws)))], False, "scatter_start_%d%d" % (i, s))
        token = sent[i, s][3]
    grad_x = dh[None]

    mine = {}
    transposed = {3 * n + j for n in range(len(ffn_ids)) for j in (0, 1)} | {n_ffn + 1, n_ffn + 2, n_ffn + 4}
    for key in reversed(order[1:]):
        sems, srcs, lands, _ = sent[key]
        parts = _split_wait(sems[0], srcs, lands, dh, NDEV - 1, False, "scatter_wait_%d%d" % key)
        for k, p in zip(members[key], parts):
            mine[k] = _sum_parts(p, "sum_parts", k in transposed)
    g_mla_in, g_q_up, g_kv_up, g_mla_o, g_dil_in, g_dil_o = (mine[k] for k in range(n_ffn, n_ffn + 6))
    g_mla_in, g_q_up, g_kv_up, g_mla_o = g_mla_in[None], g_q_up[None], g_kv_up[None], g_mla_o[None]
    g_dil_in, g_dil_o = g_dil_in[None], g_dil_o[None]
    early = {"mla_w_in": _adamw(mla_w_in, g_mla_in, m_mla_w_in, v_mla_w_in, "adamw"),
             "mla_w_q_up": _adamw(mla_w_q_up, g_q_up, m_mla_w_q_up, v_mla_w_q_up, "adamw"),
             "mla_w_kv_up": _adamw(mla_w_kv_up, g_kv_up, m_mla_w_kv_up, v_mla_w_kv_up, "adamw"),
             "mla_w_o": _adamw(mla_w_o, g_mla_o, m_mla_w_o, v_mla_w_o, "adamw"),
             "dil_w_in": _adamw(dil_w_in, g_dil_in, m_dil_w_in, v_dil_w_in, "adamw"),
             "dil_w_o": _adamw(dil_w_o, g_dil_o, m_dil_w_o, v_dil_w_o, "adamw")}
    dbias_sums = _bias_reduce(dbias, buckets, "bias_reduce")
    tied = lax.optimization_barrier((dbias_sums, *[a for step in early.values() for a in step]))
    dbias_sums, early = tied[0], {name: tuple(tied[1 + 3 * n:4 + 3 * n]) for n, name in enumerate(early)}
    for j in (2, 0, 1):
        sems, srcs, lands, _ = sent[last, j]
        parts = _split_wait(sems[0], srcs, lands, dbias_sums, NDEV - 1, False, "scatter_wait_%d%d_%d" % (*last, j))
        mine[members[last][j]] = _sum_parts(parts[0], "sum_parts", members[last][j] in transposed)
    g_gate = jnp.stack([mine[3 * n] for n in range(len(ffn_ids))]).reshape(ffn_w_gate.shape)
    g_up = jnp.stack([mine[3 * n + 1] for n in range(len(ffn_ids))]).reshape(ffn_w_up.shape)
    g_down = jnp.stack([mine[3 * n + 2] for n in range(len(ffn_ids))]).reshape(ffn_w_down.shape)

    dmod_mine = jnp.concatenate([jnp.concatenate(dmod[i, s], axis=0) for i in range(depth) for s in range(n_sub)], axis=0)
    dpre_mine = jnp.concatenate([dnorm[i, s][0] for i in range(depth) for s in range(n_sub)], axis=0)
    dpost_mine = jnp.concatenate([dnorm[i, s][1] for i in range(depth) for s in range(n_sub)], axis=0)
    dbias_tab = dbias_sums[:, 0, :N_BUCKETS].T
    pieces = [dmod_mine, dpre_mine, dpost_mine, dq_norm, dkv_norm, dbias_tab, jnp.sum(loss_parts).reshape(1, 1)]
    packed = [_lanes(p) for p in pieces]
    offs = [0]
    for p in packed:
        offs.append(offs[-1] + p.shape[0])
    everyone = _exchange([jnp.concatenate(packed, axis=0)], True, "gather_small_grads")[0].reshape(NDEV, offs[-1], 128)
    total = _sum_parts(everyone, "sum_small")
    take = lambda n, shape: total[offs[n]:offs[n + 1]].reshape(-1)[:math.prod(shape)].reshape(shape)
    g_b_mod = take(0, b_mod.shape)
    col0 = me * d_loc
    g_norm_pre = lax.dynamic_slice(take(1, (depth, n_sub, d_model)), (0, 0, col0), norm_pre.shape)
    g_norm_post = lax.dynamic_slice(take(2, (depth, n_sub, d_model)), (0, 0, col0), norm_post.shape)
    g_q_norm, g_kv_norm = take(3, mla_q_norm.shape), take(4, mla_kv_norm.shape)
    g_rel_bias = take(5, rel_bias.shape)
    loss = take(6, ())

    dmod_all = everyone[:, offs[0]:offs[1]].reshape(NDEV, depth, NDEV * mod_loc_cols)
    dmod_cols = lax.dynamic_slice(dmod_all, (0, 0, me * mod_loc_cols), (NDEV, depth, mod_loc_cols))
    silu_t = jnp.pad(silu_c.T, ((0, 0), (0, HEAD_PAD - NDEV)))
    g_w_mod = jnp.stack([_mm([(silu_t, jnp.pad(dmod_cols[:, i], ((0, HEAD_PAD - NDEV), (0, 0))))], "nn", F32, TOKEN_TILE,
                             mod_loc_cols, "mod_bwd") for i in range(depth)])

    ws = (norm_pre, norm_post, w_mod, b_mod, ffn_w_gate, ffn_w_up, ffn_w_down, mla_w_in, mla_q_norm, mla_w_q_up, mla_kv_norm,
          mla_w_kv_up, mla_w_o, dil_w_in, dil_w_o, rel_bias)
    gs = (g_norm_pre, g_norm_post, g_w_mod, g_b_mod, g_gate, g_up, g_down, g_mla_in, g_q_norm, g_q_up, g_kv_norm, g_kv_up,
          g_mla_o, g_dil_in, g_dil_o, g_rel_bias)
    ms = (m_norm_pre, m_norm_post, m_w_mod, m_b_mod, m_ffn_w_gate, m_ffn_w_up, m_ffn_w_down, m_mla_w_in, m_mla_q_norm,
          m_mla_w_q_up, m_mla_kv_norm, m_mla_w_kv_up, m_mla_w_o, m_dil_w_in, m_dil_w_o, m_rel_bias)
    vs = (v_norm_pre, v_norm_post, v_w_mod, v_b_mod, v_ffn_w_gate, v_ffn_w_up, v_ffn_w_down, v_mla_w_in, v_mla_q_norm,
          v_mla_w_q_up, v_mla_kv_norm, v_mla_w_kv_up, v_mla_w_o, v_dil_w_in, v_dil_w_o, v_rel_bias)
    names = ("norm_pre", "norm_post", "w_mod", "b_mod", "ffn_w_gate", "ffn_w_up", "ffn_w_down", "mla_w_in", "mla_q_norm",
             "mla_w_q_up", "mla_kv_norm", "mla_w_kv_up", "mla_w_o", "dil_w_in", "dil_w_o", "rel_bias")
    stepped = [early[n] if n in early else _adamw(w, g, m, v, "adamw") for n, w, g, m, v in zip(names, ws, gs, ms, vs)]
    deltas, new_m, new_v = zip(*stepped)
    return (loss, grad_x, *gs, *deltas, *new_m, *new_v)
```

```python
import math

import jax
import jax.numpy as jnp
from jax import lax
from jax.experimental import pallas as pl
from jax.experimental.pallas import tpu as pltpu

F32 = jnp.float32
BF16 = jnp.bfloat16
MESH = pl.DeviceIdType.MESH

NDEV = 8
OTHER_CHIPS = 3
D_MODEL = 1024
SEQ = 2048
D_FF = 2816
EPS = 1e-6
FFN_RES = 0.5
FFN_CHUNKS = 11

MLA_HEADS = 16
Q_LORA = 384
KV_LORA = 256
QK_NOPE = 64
QK_ROPE = 32
V_HEAD = 64
ROPE_THETA = 10000.0
HEAD_PAD = 128
LAT_PAD = Q_LORA + KV_LORA + HEAD_PAD
MLA_SCALE = (QK_NOPE + QK_ROPE) ** -0.5
MLA_QUERY_TILE = 256

DIL_GROUPS = ((128, 1), (512, 4), (2048, 16))
DIL_HEADS = 16
DIL_HEAD_DIM = 64
DIL_BLOCK = 128
DIL_PAIRS = DIL_HEADS // 2
DIL_SCALE = DIL_HEAD_DIM ** -0.5
DIL_GROUPED = 8
N_BUCKETS = 32
MAX_DISTANCE = 2048

ADAM_LR = 0.001
ADAM_B1 = 0.9
ADAM_B2 = 0.999
ADAM_EPS = 1e-08
ADAM_WD = 0.01
ADAM_STEP = 10

V7X_VMEM_BYTES = 64 * 2**20
VMEM_RESERVE = 10 * 2**20
TOKEN_TILE = 512


def _nbytes(shape, dtype):
    return math.prod(shape) * jnp.dtype(dtype).itemsize


def _params(semantics, blocks, extra=0):
    need = 2 * sum(_nbytes(s, d) for s, d in blocks) + extra + VMEM_RESERVE
    return pltpu.CompilerParams(dimension_semantics=semantics,
                                vmem_limit_bytes=int(min(need, V7X_VMEM_BYTES - VMEM_RESERVE)))


def _pcall(body, out_shape, **kw):
    call = pl.pallas_call(body, out_shape=jax.tree.map(lambda s: pltpu.HBM(s.shape, s.dtype), out_shape), **kw)
    return lambda *args: call(*[pltpu.with_memory_space_constraint(a, pltpu.HBM) for a in args])


def _dot_nn(a, b):
    return lax.dot_general(a, b, (((1,), (0,)), ((), ())), preferred_element_type=F32)


def _dot_nt(a, b):
    return lax.dot_general(a, b, (((1,), (1,)), ((), ())), preferred_element_type=F32)


def _dot_tn(a, b):
    return lax.dot_general(a, b, (((0,), (0,)), ((), ())), preferred_element_type=F32)


_DOTS = {"nn": _dot_nn, "nt": _dot_nt, "tn": _dot_tn}


def _rstd(v):
    return lax.rsqrt(jnp.mean(v * v, axis=-1, keepdims=True) + EPS)


def _rms_bwd(v, r, t):
    return r * t - v * (r * r * r) * jnp.mean(t * v, axis=-1, keepdims=True)


_TOKEN_SPEC = pl.BlockSpec((8, 128), lambda *_: (0, 0))


def _mm(pairs, mode, out_dtype, tm, tn, name, out_perm=1, after=None, b_rows=None):
    a0, b0 = pairs[0]
    m_dim = a0.shape[1] if mode == "tn" else a0.shape[0]
    n_dim = b0.shape[0] if mode == "nt" else b0.shape[1]
    tm, tn = min(tm, m_dim // out_perm), min(tn, n_dim)
    assert m_dim % tm == 0 and n_dim % tn == 0, (name, m_dim, n_dim, tm, tn)
    dot = _DOTS[mode]
    npairs = len(pairs)

    def body(*refs):
        acc = None
        for p in range(npairs):
            d = dot(refs[2 * p][...].astype(BF16), refs[2 * p + 1][...].astype(BF16))
            acc = d if acc is None else acc + d
        refs[-1][...] = acc.astype(out_dtype)

    in_specs, blocks, flat = [], [], []
    for n_pair, (a, b) in enumerate(pairs):
        if mode == "nn":
            k = a.shape[1]
            first_block = 0 if b_rows is None else b_rows[n_pair] // k
            sa, sb = ((tm, k), lambda i, j: (i, 0)), ((k, tn), lambda i, j, o=first_block: (o, j))
        elif mode == "nt":
            k = a.shape[1]
            sa, sb = ((tm, k), lambda i, j: (i, 0)), ((tn, k), lambda i, j: (j, 0))
        else:
            k = a.shape[0]
            sa, sb = ((k, tm), lambda i, j: (0, i)), ((k, tn), lambda i, j: (0, j))
        in_specs += [pl.BlockSpec(*sa), pl.BlockSpec(*sb)]
        blocks += [(sa[0], a.dtype), (sb[0], b.dtype)]
        flat += [a, b]
    if after is not None:
        in_specs.append(_TOKEN_SPEC)
        flat.append(after)
    if out_perm == 1:
        out_shape = (m_dim, n_dim)
        out_spec = pl.BlockSpec((tm, tn), lambda i, j: (i, j))
    else:
        rows = m_dim // out_perm
        assert tn == n_dim and rows % tm == 0, (name, rows, tm)
        nb = rows // tm
        out_shape = (rows, out_perm * n_dim)
        out_spec = pl.BlockSpec((tm, n_dim), lambda i, j: (i % nb, i // nb))
    blocks.append(((tm, tn), out_dtype))
    res = _pcall(
        body, out_shape=jax.ShapeDtypeStruct(out_shape, out_dtype), grid=(m_dim // tm, n_dim // tn),
        in_specs=in_specs, out_specs=out_spec, name=name,
        compiler_params=_params(("parallel", "parallel"), blocks, extra=2 * tm * tn * 4),
    )(*flat)
    return res.reshape(m_dim, n_dim)


def _prenorm_mm(x, pre_g, scale, shift, w, w_mode, out_dtype, tn, name, perm=1, w_rows=None):
    s_dim, d_dim = x.shape
    n_dim = w.shape[0] if w_mode == "nt" else w.shape[1]
    w_first = 0
    if w_rows is not None:
        w_first, n_dim = w_rows
    rows = s_dim // perm
    side = max(1, TOKEN_TILE // rows)
    tm = side * min(TOKEN_TILE, rows)
    nb = max(1, rows // tm)
    tn = min(tn, n_dim)
    assert n_dim % tn == 0 and w_first % tn == 0
    w_block0 = w_first // tn
    dot = _DOTS[w_mode]

    def body(x_ref, g_ref, sc_ref, sh_ref, w_ref, hn_ref, o_ref):
        @pl.when(pl.program_id(1) == 0)
        def _():
            xf = x_ref[...]
            if side > 1:
                xf = jnp.concatenate([xf[:, c * d_dim:(c + 1) * d_dim] for c in range(side)], axis=0)
            hn = (xf * _rstd(xf) * g_ref[...]) * (1.0 + sc_ref[...]) + sh_ref[...]
            hn_ref[...] = hn.astype(BF16)

        o_ref[...] = dot(hn_ref[...], w_ref[...]).astype(out_dtype)

    vec = pl.BlockSpec((1, d_dim), lambda i, j: (0, 0))
    w_block = (tn, d_dim) if w_mode == "nt" else (d_dim, tn)
    w_spec = pl.BlockSpec(w_block, (lambda i, j: (w_block0 + j, 0)) if w_mode == "nt" else (lambda i, j: (0, j)))
    hn, out = _pcall(
        body,
        out_shape=(jax.ShapeDtypeStruct((s_dim, d_dim), BF16), jax.ShapeDtypeStruct((s_dim, n_dim), out_dtype)),
        grid=(s_dim // tm, n_dim // tn),
        in_specs=[pl.BlockSpec((tm // side, side * d_dim), lambda i, j: (i % nb, i // nb)), vec, vec, vec, w_spec],
        out_specs=(pl.BlockSpec((tm, d_dim), lambda i, j: (i, 0)), pl.BlockSpec((tm, tn), lambda i, j: (i, j))),
        name=name,
        compiler_params=_params(("parallel", "arbitrary"),
                                [((tm, d_dim), F32), (w_block, BF16), ((tm, d_dim), BF16), ((tm, tn), out_dtype)],
                                extra=3 * tm * d_dim * 4 + tm * tn * 4),
    )(x.reshape(rows, perm * d_dim), pre_g, scale, shift, w)
    return hn, out


def _ffn_up(x, pre_g, scale, shift, wg_t, wu_t, name):
    s_dim, d_dim = x.shape
    f_dim = wg_t.shape[0]
    tm, tn = TOKEN_TILE, f_dim // 2

    def body(x_ref, g_ref, sc_ref, sh_ref, wg_ref, wu_ref, hn_ref, go_ref, uo_ref, a_ref):
        @pl.when(pl.program_id(1) == 0)
        def _():
            xf = x_ref[...]
            hn = (xf * _rstd(xf) * g_ref[...]) * (1.0 + sc_ref[...]) + sh_ref[...]
            hn_ref[...] = hn.astype(BF16)

        hn = hn_ref[...]
        g = _dot_nt(hn, wg_ref[...])
        u = _dot_nt(hn, wu_ref[...])
        go_ref[...] = g.astype(BF16)
        uo_ref[...] = u.astype(BF16)
        a_ref[...] = (g * jax.nn.sigmoid(g) * u).astype(BF16)

    vec = pl.BlockSpec((1, d_dim), lambda i, j: (0, 0))
    w_spec = pl.BlockSpec((tn, d_dim), lambda i, j: (j, 0))
    act = pl.BlockSpec((tm, tn), lambda i, j: (i, j))
    act_shape = jax.ShapeDtypeStruct((s_dim, f_dim), BF16)
    return _pcall(
        body,
        out_shape=(jax.ShapeDtypeStruct((s_dim, d_dim), BF16), act_shape, act_shape, act_shape),
        grid=(s_dim // tm, f_dim // tn),
        in_specs=[pl.BlockSpec((tm, d_dim), lambda i, j: (i, 0)), vec, vec, vec, w_spec, w_spec],
        out_specs=(pl.BlockSpec((tm, d_dim), lambda i, j: (i, 0)), act, act, act),
        name=name,
        compiler_params=_params(("parallel", "arbitrary"),
                                [((tm, d_dim), F32), ((tn, d_dim), BF16), ((tn, d_dim), BF16), ((tm, d_dim), BF16)]
                                + 3 * [((tm, tn), BF16)], extra=3 * tm * d_dim * 4 + 4 * tm * tn * 4),
    )(x, pre_g, scale, shift, wg_t, wu_t)


def _mm_post(a, w, x, post_g, gate, res_w, name):
    s_dim, k_dim = a.shape
    d_dim = w.shape[1]
    tm = TOKEN_TILE

    def body(a_ref, w_ref, x_ref, pg_ref, gt_ref, xo_ref, f_ref):
        f = _dot_nn(a_ref[...], w_ref[...])
        y = f * _rstd(f) * pg_ref[...]
        f_ref[...] = f
        xo_ref[...] = x_ref[...] + (res_w * gt_ref[...]) * y

    vec = pl.BlockSpec((1, d_dim), lambda i: (0, 0))
    row = pl.BlockSpec((tm, d_dim), lambda i: (i, 0))
    out = jax.ShapeDtypeStruct((s_dim, d_dim), F32)
    return _pcall(
        body, out_shape=(out, out), grid=(s_dim // tm,),
        in_specs=[pl.BlockSpec((tm, k_dim), lambda i: (i, 0)), pl.BlockSpec((k_dim, d_dim), lambda i: (0, 0)), row, vec, vec],
        out_specs=(row, row), name=name,
        compiler_params=_params(("parallel",), [((tm, k_dim), BF16), ((k_dim, d_dim), BF16)] + 3 * [((tm, d_dim), F32)],
                                extra=3 * tm * d_dim * 4),
    )(a, w, x, post_g, gate)


def _post_bwd(dx_out, f, post_g, gate, res_w, name):
    s_dim, d_dim = f.shape
    tm = TOKEN_TILE

    def body(dx_ref, f_ref, pg_ref, gt_ref, df_ref, dgate_ref, dpost_ref):
        @pl.when(pl.program_id(0) == 0)
        def _():
            dgate_ref[...] = jnp.zeros_like(dgate_ref)
            dpost_ref[...] = jnp.zeros_like(dpost_ref)

        dx, fv = dx_ref[...], f_ref[...]
        r = _rstd(fv)
        fr = fv * r
        dgate_ref[...] += res_w * jnp.sum(dx * (fr * pg_ref[...]), axis=0, keepdims=True)
        dy = (res_w * gt_ref[...]) * dx
        dpost_ref[...] += jnp.sum(dy * fr, axis=0, keepdims=True)
        df_ref[...] = _rms_bwd(fv, r, dy * pg_ref[...]).astype(BF16)

    vec = pl.BlockSpec((1, d_dim), lambda i: (0, 0))
    row = pl.BlockSpec((tm, d_dim), lambda i: (i, 0))
    vshape = jax.ShapeDtypeStruct((1, d_dim), F32)
    return _pcall(
        body, out_shape=(jax.ShapeDtypeStruct((s_dim, d_dim), BF16), vshape, vshape), grid=(s_dim // tm,),
        in_specs=[row, row, vec, vec], out_specs=(row, vec, vec), name=name,
        compiler_params=_params(("arbitrary",), 3 * [((tm, d_dim), F32)], extra=6 * tm * d_dim * 4),
    )(dx_out, f, post_g, gate)


def _prenorm_bwd(dx_out, dhns, x, pre_g, scale, name):
    s_dim, d_dim = x.shape
    tm = TOKEN_TILE
    n_in = len(dhns)

    def body(*refs):
        dx_ref, x_ref, pg_ref, sc_ref = refs[n_in + 0], refs[n_in + 1], refs[n_in + 2], refs[n_in + 3]
        dxo_ref, dsh_ref, dsc_ref, dpg_ref = refs[n_in + 4:]

        @pl.when(pl.program_id(0) == 0)
        def _():
            dsh_ref[...] = jnp.zeros_like(dsh_ref)
            dsc_ref[...] = jnp.zeros_like(dsc_ref)
            dpg_ref[...] = jnp.zeros_like(dpg_ref)

        dhn = refs[0][...]
        for k in range(1, n_in):
            dhn = dhn + refs[k][...]
        xv = x_ref[...]
        r = _rstd(xv)
        xr = xv * r
        dsh_ref[...] += jnp.sum(dhn, axis=0, keepdims=True)
        dsc_ref[...] += jnp.sum(dhn * (xr * pg_ref[...]), axis=0, keepdims=True)
        dn = dhn * (1.0 + sc_ref[...])
        dpg_ref[...] += jnp.sum(dn * xr, axis=0, keepdims=True)
        dxo_ref[...] = dx_ref[...] + _rms_bwd(xv, r, dn * pg_ref[...])

    vec = pl.BlockSpec((1, d_dim), lambda i: (0, 0))
    row = pl.BlockSpec((tm, d_dim), lambda i: (i, 0))
    vshape = jax.ShapeDtypeStruct((1, d_dim), F32)
    return _pcall(
        body, out_shape=(jax.ShapeDtypeStruct((s_dim, d_dim), F32), vshape, vshape, vshape), grid=(s_dim // tm,),
        in_specs=n_in * [row] + [row, row, vec, vec], out_specs=(row, vec, vec, vec), name=name,
        compiler_params=_params(("arbitrary",), (n_in + 3) * [((tm, d_dim), F32)], extra=6 * tm * d_dim * 4),
    )(*dhns, dx_out, x, pre_g, scale)


def _ffn_dgu(df, wd, g, u, name, after=None):
    s_dim, d_dim = df.shape
    f_dim = wd.shape[0]
    tm, tn = TOKEN_TILE, f_dim // 2

    def body(df_ref, wd_ref, g_ref, u_ref, *rest):
        dg_ref, du_ref = rest[-2:]
        da = _dot_nt(df_ref[...], wd_ref[...])
        gv, uv = g_ref[...].astype(F32), u_ref[...].astype(F32)
        sg = jax.nn.sigmoid(gv)
        du_ref[...] = (da * (gv * sg)).astype(BF16)
        dg_ref[...] = (da * uv * (sg * (1.0 + gv * (1.0 - sg)))).astype(BF16)

    act = pl.BlockSpec((tm, tn), lambda i, j: (i, j))
    act_shape = jax.ShapeDtypeStruct((s_dim, f_dim), BF16)
    token = [] if after is None else [after]
    return _pcall(
        body, out_shape=(act_shape, act_shape), grid=(s_dim // tm, f_dim // tn),
        in_specs=[pl.BlockSpec((tm, d_dim), lambda i, j: (i, 0)), pl.BlockSpec((tn, d_dim), lambda i, j: (j, 0)), act, act]
        + len(token) * [_TOKEN_SPEC],
        out_specs=(act, act), name=name,
        compiler_params=_params(("parallel", "parallel"), [((tm, d_dim), BF16), ((tn, d_dim), BF16)] + 4 * [((tm, tn), BF16)],
                                extra=6 * tm * tn * 4),
    )(df, wd, g, u, *token)


def _ffn_dw(dg, du, a, hn, df, name):
    s_dim, f_dim = dg.shape
    d_dim = hn.shape[1]
    tm = 256

    def body(dg_ref, du_ref, a_ref, hn_ref, df_ref, dwg_ref, dwu_ref, dwd_ref):
        dwg_ref[...] = _dot_tn(dg_ref[...], hn_ref[...]).astype(BF16)
        dwu_ref[...] = _dot_tn(du_ref[...], hn_ref[...]).astype(BF16)
        dwd_ref[...] = _dot_tn(a_ref[...], df_ref[...]).astype(BF16)

    col = pl.BlockSpec((s_dim, tm), lambda i: (0, i))
    full = pl.BlockSpec((s_dim, d_dim), lambda i: (0, 0), pipeline_mode=pl.Buffered(1))
    out = pl.BlockSpec((tm, d_dim), lambda i: (i, 0))
    shape = jax.ShapeDtypeStruct((f_dim, d_dim), BF16)
    need = 2 * s_dim * d_dim * 2 + 2 * 3 * (s_dim * tm * 2 + tm * d_dim * 2) + 3 * tm * d_dim * 4 + 3 * s_dim * tm * 2
    return _pcall(
        body, out_shape=(shape, shape, shape), grid=(f_dim // tm,), in_specs=[col, col, col, full, full],
        out_specs=(out, out, out), name=name,
        compiler_params=pltpu.CompilerParams(dimension_semantics=("parallel",),
                                             vmem_limit_bytes=int(min(need + VMEM_RESERVE, V7X_VMEM_BYTES - VMEM_RESERVE))),
    )(dg, du, a, hn, df)


def _mm_tn_shared(lhs, b, name):
    k_dim, m_dim = lhs[0].shape
    n_dim = b.shape[1]
    tm = 256
    n = len(lhs)

    def body(*refs):
        rhs = refs[n][...]
        for j in range(n):
            refs[n + 1 + j][...] = _dot_tn(refs[j][...], rhs).astype(BF16)

    col = pl.BlockSpec((k_dim, tm), lambda i: (0, i))
    out = pl.BlockSpec((tm, n_dim), lambda i: (i, 0))
    shape = jax.ShapeDtypeStruct((m_dim, n_dim), BF16)
    need = k_dim * n_dim * 2 + 2 * n * (k_dim * tm * 2 + tm * n_dim * 2) + n * tm * n_dim * 4 + n * k_dim * tm * 2
    return _pcall(
        body, out_shape=tuple(n * [shape]), grid=(m_dim // tm,),
        in_specs=n * [col] + [pl.BlockSpec((k_dim, n_dim), lambda i: (0, 0), pipeline_mode=pl.Buffered(1))],
        out_specs=tuple(n * [out]), name=name,
        compiler_params=pltpu.CompilerParams(dimension_semantics=("parallel",),
                                             vmem_limit_bytes=int(min(need + VMEM_RESERVE, V7X_VMEM_BYTES - VMEM_RESERVE))),
    )(*lhs, b)


class _chunked_load:
    def __init__(self, pairs, sems, chunks, rows):
        self.first = pl.program_id(0) == 0
        self.copies = {(k, c): pltpu.make_async_copy(hbm.at[c * rows:(c + 1) * rows, :], vmem.at[c * rows:(c + 1) * rows, :],
                                                     sems.at[k, c])
                       for c in range(chunks) for k, (hbm, vmem) in enumerate(pairs)}

        @pl.when(self.first)
        def _():
            for copy in self.copies.values():
                copy.start()

    def both_ways(self, compute):
        @pl.when(self.first)
        def _():
            compute(lambda k, c: self.copies[k, c].wait())

        @pl.when(jnp.logical_not(self.first))
        def _():
            compute(lambda k, c: None)


def _ffn_fwd_fused(x, pre_g, scale, shift, post_g, gate, wg_t, wu_t, wd, name):
    s_dim, d_dim = x.shape
    f_dim = wd.shape[0]
    tm, chunks = 256, FFN_CHUNKS
    cw = f_dim // chunks

    def body(x_ref, prg_ref, sc_ref, sh_ref, pg_ref, gt_ref, wg_hbm, wu_hbm, wd_hbm, hn_ref, go_ref, uo_ref, a_ref, xo_ref, f_ref,
             wg_ref, wu_ref, wd_ref, sems):
        load = _chunked_load([(wg_hbm, wg_ref), (wu_hbm, wu_ref), (wd_hbm, wd_ref)], sems, chunks, cw)

        def compute(wait):
            xf = x_ref[...]
            hn = ((xf * _rstd(xf) * prg_ref[...]) * (1.0 + sc_ref[...]) + sh_ref[...]).astype(BF16)
            hn_ref[...] = hn
            f = None
            wait(0, 0), wait(1, 0)
            ahead = (_dot_nt(hn, wg_ref[0:cw, :]), _dot_nt(hn, wu_ref[0:cw, :]))
            for c in range(chunks):
                g, u = ahead
                if c + 1 < chunks:
                    nxt = slice((c + 1) * cw, (c + 2) * cw)
                    wait(0, c + 1), wait(1, c + 1)
                    ahead = (_dot_nt(hn, wg_ref[nxt, :]), _dot_nt(hn, wu_ref[nxt, :]))
                cols = slice(c * cw, (c + 1) * cw)
                go_ref[:, cols] = g.astype(BF16)
                uo_ref[:, cols] = u.astype(BF16)
                a = (g * jax.nn.sigmoid(g) * u).astype(BF16)
                a_ref[:, cols] = a
                wait(2, c)
                part = _dot_nn(a, wd_ref[cols, :])
                f = part if f is None else f + part
            f_ref[...] = f
            xo_ref[...] = xf + (FFN_RES * gt_ref[...]) * (f * _rstd(f) * pg_ref[...])

        load.both_ways(compute)

    vec = pl.BlockSpec((1, d_dim), lambda i: (0, 0))
    row = pl.BlockSpec((tm, d_dim), lambda i: (i, 0))
    act = pl.BlockSpec((tm, f_dim), lambda i: (i, 0))
    weight = pl.BlockSpec(memory_space=pl.ANY)
    act_shape = jax.ShapeDtypeStruct((s_dim, f_dim), BF16)
    res_shape = jax.ShapeDtypeStruct((s_dim, d_dim), F32)
    need = (3 * f_dim * d_dim * 2 + 2 * tm * d_dim * 4 + 2 * (tm * d_dim * 2 + 3 * tm * f_dim * 2 + 2 * tm * d_dim * 4)
            + 8 * tm * cw * 4 + 4 * tm * d_dim * 4)
    return _pcall(
        body, out_shape=(jax.ShapeDtypeStruct((s_dim, d_dim), BF16), act_shape, act_shape, act_shape, res_shape, res_shape),
        grid=(s_dim // tm,), in_specs=[row, vec, vec, vec, vec, vec, weight, weight, weight],
        out_specs=(row, act, act, act, row, row), name=name,
        scratch_shapes=3 * [pltpu.VMEM((f_dim, d_dim), BF16)] + [pltpu.SemaphoreType.DMA((3, chunks))],
        compiler_params=pltpu.CompilerParams(dimension_semantics=("arbitrary",),
                                             vmem_limit_bytes=int(min(need + VMEM_RESERVE, V7X_VMEM_BYTES - VMEM_RESERVE))),
    )(x, pre_g, scale, shift, post_g, gate, wg_t, wu_t, wd)


def _ffn_bwd_fused(dx_out, saved, pre_g, post_g, scale, gate, wg_t, wu_t, wd, name):
    x, _, g, u, _, f = saved
    s_dim, d_dim = x.shape
    f_dim = wd.shape[0]
    tm, chunks = 256, FFN_CHUNKS
    cw = f_dim // chunks

    def body(dx_ref, f_ref, g_ref, u_ref, x_ref, pg_ref, gt_ref, prg_ref, sc_ref, wd_hbm, wg_hbm, wu_hbm,
             df_ref, dg_ref, du_ref, dxo_ref, dgate_ref, dpost_ref, dsh_ref, dsc_ref, dpg_ref, wd_ref, wg_ref, wu_ref, sems):
        load = _chunked_load([(wd_hbm, wd_ref), (wg_hbm, wg_ref), (wu_hbm, wu_ref)], sems, chunks, cw)

        @pl.when(pl.program_id(0) == 0)
        def _():
            for acc in (dgate_ref, dpost_ref, dsh_ref, dsc_ref, dpg_ref):
                acc[...] = jnp.zeros_like(acc)

        def compute(wait):
            dx, fv = dx_ref[...], f_ref[...]
            r = _rstd(fv)
            fr = fv * r
            dgate_ref[...] += FFN_RES * jnp.sum(dx * (fr * pg_ref[...]), axis=0, keepdims=True)
            dy = (FFN_RES * gt_ref[...]) * dx
            dpost_ref[...] += jnp.sum(dy * fr, axis=0, keepdims=True)
            df = _rms_bwd(fv, r, dy * pg_ref[...]).astype(BF16)
            df_ref[...] = df
            dhn = None
            wait(0, 0)
            ahead = _dot_nt(df, wd_ref[0:cw, :])
            for c in range(chunks):
                da = ahead
                if c + 1 < chunks:
                    wait(0, c + 1)
                    ahead = _dot_nt(df, wd_ref[(c + 1) * cw:(c + 2) * cw, :])
                cols = slice(c * cw, (c + 1) * cw)
                gv, uv = g_ref[:, cols].astype(F32), u_ref[:, cols].astype(F32)
                sg = jax.nn.sigmoid(gv)
                du = (da * (gv * sg)).astype(BF16)
                dg = (da * uv * (sg * (1.0 + gv * (1.0 - sg)))).astype(BF16)
                dg_ref[:, cols] = dg
                du_ref[:, cols] = du
                wait(1, c), wait(2, c)
                part = _dot_nn(dg, wg_ref[cols, :]) + _dot_nn(du, wu_ref[cols, :])
                dhn = part if dhn is None else dhn + part
            xv = x_ref[...]
            rx = _rstd(xv)
            xr = xv * rx
            dsh_ref[...] += jnp.sum(dhn, axis=0, keepdims=True)
            dsc_ref[...] += jnp.sum(dhn * (xr * prg_ref[...]), axis=0, keepdims=True)
            dn = dhn * (1.0 + sc_ref[...])
            dpg_ref[...] += jnp.sum(dn * xr, axis=0, keepdims=True)
            dxo_ref[...] = dx + _rms_bwd(xv, rx, dn * prg_ref[...])

        load.both_ways(compute)

    vec = pl.BlockSpec((1, d_dim), lambda i: (0, 0))
    row = pl.BlockSpec((tm, d_dim), lambda i: (i, 0))
    act = pl.BlockSpec((tm, f_dim), lambda i: (i, 0))
    weight = pl.BlockSpec(memory_space=pl.ANY)
    vshape = jax.ShapeDtypeStruct((1, d_dim), F32)
    act_shape = jax.ShapeDtypeStruct((s_dim, f_dim), BF16)
    need = (3 * f_dim * d_dim * 2 + 2 * (3 * tm * d_dim * 4 + 2 * tm * f_dim * 2) + 2 * (tm * d_dim * 2 + 2 * tm * f_dim * 2 + tm * d_dim * 4)
            + 6 * tm * cw * 4 + 6 * tm * d_dim * 4)
    return _pcall(
        body, out_shape=(jax.ShapeDtypeStruct((s_dim, d_dim), BF16), act_shape, act_shape, jax.ShapeDtypeStruct((s_dim, d_dim), F32),
                         vshape, vshape, vshape, vshape, vshape),
        grid=(s_dim // tm,), in_specs=[row, row, act, act, row, vec, vec, vec, vec, weight, weight, weight],
        out_specs=(row, act, act, row, vec, vec, vec, vec, vec), name=name,
        scratch_shapes=3 * [pltpu.VMEM((f_dim, d_dim), BF16)] + [pltpu.SemaphoreType.DMA((3, chunks))],
        compiler_params=pltpu.CompilerParams(dimension_semantics=("arbitrary",),
                                             vmem_limit_bytes=int(min(need + VMEM_RESERVE, V7X_VMEM_BYTES - VMEM_RESERVE))),
    )(dx_out, f, g, u, x, post_g, gate, pre_g, scale, wd, wg_t, wu_t)


def _rope_tables(zero=0.0):
    half = QK_ROPE // 2
    freqs = ROPE_THETA ** (-jnp.arange(half, dtype=F32) / half)
    ang = (jnp.arange(SEQ, dtype=F32)[:, None] + zero) * freqs[None, :]
    cos, sin = jnp.cos(ang), jnp.sin(ang)
    ones = jnp.ones((SEQ, QK_NOPE), F32)
    zeros = jnp.zeros((SEQ, QK_NOPE), F32)
    pad1 = jnp.ones((SEQ, HEAD_PAD - QK_NOPE - QK_ROPE), F32)
    pad0 = jnp.zeros((SEQ, HEAD_PAD - QK_NOPE - QK_ROPE), F32)
    zh = jnp.zeros((SEQ, half), F32)
    c = jnp.concatenate([ones, cos, cos, pad1], axis=1)
    s1 = jnp.concatenate([zeros, -sin, zh, pad0], axis=1)
    s2 = jnp.concatenate([zeros, zh, sin, pad0], axis=1)
    return c, s1, s2


def _rope(v, c, s1, s2):
    half = QK_ROPE // 2
    return v * c + pltpu.roll(v, HEAD_PAD - half, 1) * s1 + pltpu.roll(v, half, 1) * s2


def _rope_t(dv, c, s1, s2):
    half = QK_ROPE // 2
    return dv * c + pltpu.roll(dv * s1, half, 1) + pltpu.roll(dv * s2, HEAD_PAD - half, 1)


def _mla_qkv(lat, q_norm, kv_norm, wq_t, wkv_t, rope, name):
    s_dim = lat.shape[0]
    width = MLA_HEADS * HEAD_PAD
    tm = 256

    def body(lat_ref, qg_ref, kg_ref, wq_ref, wkv_ref, c_ref, s1_ref, s2_ref, q_ref, k_ref, v_ref, qn_ref, kvn_ref):
        cq = lat_ref[:, :Q_LORA]
        ckv = lat_ref[:, Q_LORA:Q_LORA + KV_LORA]
        kr = lat_ref[:, Q_LORA + KV_LORA:]
        c, s1, s2 = c_ref[...], s1_ref[...], s2_ref[...]
        qn = (cq * _rstd(cq) * qg_ref[...]).astype(BF16)
        kvn = (ckv * _rstd(ckv) * kg_ref[...]).astype(BF16)
        qn_ref[...] = qn
        kvn_ref[...] = kvn
        q = _dot_nt(qn, wq_ref[...])
        kv = _dot_nt(kvn, wkv_ref[...])
        krr = _rope(kr, c, s1, s2)
        low = lax.broadcasted_iota(jnp.int32, (tm, HEAD_PAD), 1) < QK_NOPE
        for h in range(MLA_HEADS):
            sl = slice(h * HEAD_PAD, (h + 1) * HEAD_PAD)
            q_ref[:, sl] = _rope(q[:, sl], c, s1, s2).astype(BF16)
            kvh = kv[:, sl]
            k_ref[:, sl] = (jnp.where(low, kvh, 0.0) + krr).astype(BF16)
            v_ref[:, sl] = jnp.where(low, 0.0, kvh).astype(BF16)

    row = lambda n: pl.BlockSpec((tm, n), lambda i: (i, 0))
    full = lambda a: pl.BlockSpec(a.shape, lambda i: (0, 0))
    wide = jax.ShapeDtypeStruct((s_dim, width), BF16)
    return _pcall(
        body,
        out_shape=(wide, wide, wide, jax.ShapeDtypeStruct((s_dim, Q_LORA), BF16), jax.ShapeDtypeStruct((s_dim, KV_LORA), BF16)),
        grid=(s_dim // tm,),
        in_specs=[row(LAT_PAD), full(q_norm), full(kv_norm), full(wq_t), full(wkv_t), row(HEAD_PAD), row(HEAD_PAD), row(HEAD_PAD)],
        out_specs=(row(width), row(width), row(width), row(Q_LORA), row(KV_LORA)), name=name,
        compiler_params=_params(("parallel",), [((tm, LAT_PAD), F32), (wq_t.shape, BF16), (wkv_t.shape, BF16)]
                                + 3 * [((tm, width), BF16)], extra=4 * tm * width * 4),
    )(lat, q_norm, kv_norm, wq_t, wkv_t, *rope)


def _mla_scores(q, k_ref, t, tq):
    lo = t * tq
    own = slice(lo, lo + tq)
    scores = [(_dot_nt(q, k_ref[own, :]), own)]
    if t > 0:
        scores.append((_dot_nt(q, k_ref[0:lo, :]), slice(0, lo)))
    return scores


def _mla_softmax(scores):
    s_own = scores[0][0] * MLA_SCALE
    rows = lax.broadcasted_iota(jnp.int32, s_own.shape, 0)
    cols = lax.broadcasted_iota(jnp.int32, s_own.shape, 1)
    s_own = jnp.where(cols <= rows, s_own, -jnp.inf)
    mx = jnp.max(s_own, axis=-1, keepdims=True)
    if len(scores) == 1:
        e_own = jnp.exp(s_own - mx)
        return [(e_own * (1.0 / jnp.sum(e_own, axis=-1, keepdims=True)), scores[0][1])]
    s_pre = scores[1][0] * MLA_SCALE
    mx = jnp.maximum(mx, jnp.max(s_pre, axis=-1, keepdims=True))
    e_own, e_pre = jnp.exp(s_own - mx), jnp.exp(s_pre - mx)
    inv = 1.0 / (jnp.sum(e_own, axis=-1, keepdims=True) + jnp.sum(e_pre, axis=-1, keepdims=True))
    return [(e_pre * inv, scores[1][1]), (e_own * inv, scores[0][1])]


def _mla_attn_fwd(q, k, v, name):
    s_dim = q.shape[0]
    tq = MLA_QUERY_TILE

    def body(q_ref, k_ref, v_ref, o_ref):
        n_tiles = s_dim // tq
        tile_of = lambda t: slice(t * tq, (t + 1) * tq)
        def weighted_values(t, probs):
            o = None
            for p, keys in probs:
                part = _dot_nn(p, v_ref[keys, :])
                o = part if o is None else o + part
            o_ref[tile_of(t), :] = o.astype(BF16)

        scores = _mla_scores(q_ref[tile_of(0), :], k_ref, 0, tq)
        probs = None
        for t in range(n_tiles):
            ahead = _mla_scores(q_ref[tile_of(t + 1), :], k_ref, t + 1, tq) if t + 1 < n_tiles else None
            if probs is not None:
                weighted_values(t - 1, probs)
            probs = [(p.astype(BF16), keys) for p, keys in _mla_softmax(scores)]
            scores = ahead
        weighted_values(n_tiles - 1, probs)

    head = pl.BlockSpec((s_dim, HEAD_PAD), lambda h: (0, h))
    return _pcall(
        body, out_shape=jax.ShapeDtypeStruct(q.shape, BF16), grid=(MLA_HEADS,),
        in_specs=[head, head, head], out_specs=head, name=name,
        compiler_params=_params(("parallel",), 4 * [((s_dim, HEAD_PAD), BF16)], extra=4 * tq * s_dim * 4),
    )(q, k, v)


def _mla_attn_bwd(q, k, v, d_o, name):
    s_dim = q.shape[0]
    tq = MLA_QUERY_TILE

    def body(q_ref, k_ref, v_ref, do_ref, dq_ref, dk_ref, dv_ref):
        dk_ref[...] = jnp.zeros_like(dk_ref)
        dv_ref[...] = jnp.zeros_like(dv_ref)
        n_tiles = s_dim // tq
        tile_of = lambda t: slice(t * tq, (t + 1) * tq)

        def products(t):
            scores = _mla_scores(q_ref[tile_of(t), :], k_ref, t, tq)
            dot = do_ref[tile_of(t), :].astype(BF16)
            return scores, [_dot_nt(dot, v_ref[keys, :]) for _, keys in scores]

        def gradients_of_scores(scores, dps):
            probs = _mla_softmax(scores)
            dp_of = {(keys.start, keys.stop): dp for (_, keys), dp in zip(scores, dps)}
            terms = [(p, keys, dp_of[keys.start, keys.stop]) for p, keys in probs]
            row = None
            for p, _, dp in terms:
                part = jnp.sum(p * dp, axis=-1, keepdims=True)
                row = part if row is None else row + part
            return [((p * (dp - row) * MLA_SCALE).astype(BF16), p.astype(BF16), keys) for p, keys, dp in terms]

        def accumulate(t, terms):
            qt = q_ref[tile_of(t), :]
            dot = do_ref[tile_of(t), :].astype(BF16)
            dq = None
            for dsb, pb, keys in terms:
                part = _dot_nn(dsb, k_ref[keys, :])
                dq = part if dq is None else dq + part
                dk_ref[keys, :] += _dot_tn(dsb, qt)
                dv_ref[keys, :] += _dot_tn(pb, dot)
            dq_ref[tile_of(t), :] = dq

        ready = products(0)
        terms = None
        for t in range(n_tiles):
            ahead = products(t + 1) if t + 1 < n_tiles else None
            if terms is not None:
                accumulate(t - 1, terms)
            terms = gradients_of_scores(*ready)
            ready = ahead
        accumulate(n_tiles - 1, terms)

    head = pl.BlockSpec((s_dim, HEAD_PAD), lambda h: (0, h))
    out = jax.ShapeDtypeStruct(q.shape, F32)
    return _pcall(
        body, out_shape=(out, out, out), grid=(MLA_HEADS,),
        in_specs=[head, head, head, head], out_specs=(head, head, head), name=name,
        compiler_params=_params(("parallel",), 3 * [((s_dim, HEAD_PAD), BF16)] + 4 * [((s_dim, HEAD_PAD), F32)],
                                extra=6 * tq * s_dim * 4),
    )(q, k, v, d_o)


def _mla_qkv_bwd(dq, dk, dv, lat, q_norm, kv_norm, wq_t, wkv_t, rope, name):
    s_dim = lat.shape[0]
    width = MLA_HEADS * HEAD_PAD
    tm = 256

    def body(dq_ref, dk_ref, dv_ref, lat_ref, qg_ref, kg_ref, wq_ref, wkv_ref, c_ref, s1_ref, s2_ref,
             dqp_ref, dkv_ref, dlat_ref, dqg_ref, dkg_ref):
        @pl.when(pl.program_id(0) == 0)
        def _():
            dqg_ref[...] = jnp.zeros_like(dqg_ref)
            dkg_ref[...] = jnp.zeros_like(dkg_ref)

        c, s1, s2 = c_ref[...], s1_ref[...], s2_ref[...]
        lane = lax.broadcasted_iota(jnp.int32, (tm, HEAD_PAD), 1)
        low = lane < QK_NOPE
        rot = (lane >= QK_NOPE) & (lane < QK_NOPE + QK_ROPE)
        dkrr = jnp.zeros((tm, HEAD_PAD), F32)
        for h in range(MLA_HEADS):
            sl = slice(h * HEAD_PAD, (h + 1) * HEAD_PAD)
            dqp_ref[:, sl] = _rope_t(dq_ref[:, sl], c, s1, s2).astype(BF16)
            dkh = dk_ref[:, sl]
            dkv_ref[:, sl] = jnp.where(low, dkh, dv_ref[:, sl]).astype(BF16)
            dkrr = dkrr + jnp.where(rot, dkh, 0.0)
        dqn = _dot_nn(dqp_ref[...], wq_ref[...])
        dkvn = _dot_nn(dkv_ref[...], wkv_ref[...])
        cq = lat_ref[:, :Q_LORA]
        ckv = lat_ref[:, Q_LORA:Q_LORA + KV_LORA]
        rq, rkv = _rstd(cq), _rstd(ckv)
        dqg_ref[...] += jnp.sum(dqn * cq * rq, axis=0, keepdims=True)
        dkg_ref[...] += jnp.sum(dkvn * ckv * rkv, axis=0, keepdims=True)
        dlat_ref[:, :Q_LORA] = _rms_bwd(cq, rq, dqn * qg_ref[...])
        dlat_ref[:, Q_LORA:Q_LORA + KV_LORA] = _rms_bwd(ckv, rkv, dkvn * kg_ref[...])
        dlat_ref[:, Q_LORA + KV_LORA:] = _rope_t(dkrr, c, s1, s2)

    row = lambda n: pl.BlockSpec((tm, n), lambda i: (i, 0))
    full = lambda a: pl.BlockSpec(a.shape, lambda i: (0, 0))
    wide = jax.ShapeDtypeStruct((s_dim, width), BF16)
    return _pcall(
        body,
        out_shape=(wide, wide, jax.ShapeDtypeStruct((s_dim, LAT_PAD), F32),
                   jax.ShapeDtypeStruct(q_norm.shape, F32), jax.ShapeDtypeStruct(kv_norm.shape, F32)),
        grid=(s_dim // tm,),
        in_specs=[row(width), row(width), row(width), row(LAT_PAD), full(q_norm), full(kv_norm), full(wq_t), full(wkv_t),
                  row(HEAD_PAD), row(HEAD_PAD), row(HEAD_PAD)],
        out_specs=(row(width), row(width), row(LAT_PAD), full(q_norm), full(kv_norm)), name=name,
        compiler_params=_params(("arbitrary",), 3 * [((tm, width), F32)] + [((tm, LAT_PAD), F32), (wq_t.shape, BF16),
                                                                           (wkv_t.shape, BF16)] + 2 * [((tm, width), BF16)],
                                extra=2 * tm * width * 4),
    )(dq, dk, dv, lat, q_norm, kv_norm, wq_t, wkv_t, *rope)


def _t5_bucket(dist):
    max_exact = N_BUCKETS // 2
    d = jnp.maximum(dist, 1).astype(F32)
    large = max_exact + (jnp.log(d / max_exact) / math.log(MAX_DISTANCE / max_exact)
                         * (N_BUCKETS - max_exact)).astype(jnp.int32)
    large = jnp.minimum(large, N_BUCKETS - 1)
    return jnp.where(dist < max_exact, dist, large)


def _dil_buckets(dilation):
    iq = jnp.arange(DIL_BLOCK)[:, None]
    ik = jnp.arange(2 * DIL_BLOCK)[None, :]
    return _t5_bucket(jnp.maximum(DIL_BLOCK + iq - ik, 0) * dilation)


def _dil_logits(qh, kb, bias_h, first, span):
    if first:
        s = _dot_nt(qh, kb) * DIL_SCALE + bias_h[:, DIL_BLOCK:]
        rel = lax.broadcasted_iota(jnp.int32, s.shape, 0) - lax.broadcasted_iota(jnp.int32, s.shape, 1)
    else:
        s = _dot_nt(qh, kb) * DIL_SCALE + bias_h
        rel = DIL_BLOCK + lax.broadcasted_iota(jnp.int32, s.shape, 0) - lax.broadcasted_iota(jnp.int32, s.shape, 1)
    return jnp.where((rel >= 0) & (rel <= span), s, -jnp.inf)


def _dil_blocks(s_dim, dilation):
    rows = s_dim // dilation
    for r in range(dilation):
        for n in range(rows // DIL_BLOCK):
            lo = r * rows + n * DIL_BLOCK
            keys = slice(lo, lo + DIL_BLOCK) if n == 0 else slice(lo - DIL_BLOCK, lo + DIL_BLOCK)
            start = r + n * DIL_BLOCK * dilation
            tokens = slice(start, start + DIL_BLOCK) if dilation == 1 else pl.ds(start, DIL_BLOCK, stride=dilation)
            yield n == 0, slice(lo, lo + DIL_BLOCK), keys, tokens


def _dil_views(s_dim):
    col = lambda which: pl.BlockSpec((s_dim, HEAD_PAD), lambda p: (0, which * DIL_PAIRS + p))
    nat = pl.BlockSpec((s_dim, HEAD_PAD), lambda p: (0, p))
    bias = pl.BlockSpec((2, DIL_BLOCK, 2 * DIL_BLOCK), lambda p: (p, 0, 0))
    return col, nat, bias


def _dil_attn_fwd(qkv, bias, dilation, span, name):
    s_dim = qkv.shape[0]
    d_dim = DIL_HEADS * DIL_HEAD_DIM
    col, nat, bias_spec = _dil_views(s_dim)

    def body(q_ref, k_ref, v_ref, b_ref, o_ref, l_ref):
        lane = lax.broadcasted_iota(jnp.int32, (DIL_BLOCK, HEAD_PAD), 1)
        klane = lax.broadcasted_iota(jnp.int32, (2 * DIL_BLOCK, HEAD_PAD), 1)
        blocks = list(_dil_blocks(s_dim, dilation))
        for g0 in range(0, len(blocks), DIL_GROUPED):
            group = blocks[g0:g0 + DIL_GROUPED]
            logits = [_dil_logits(jnp.where((lane < DIL_HEAD_DIM) == (h == 0), q_ref[blk, :], 0), k_ref[keys, :], b_ref[h],
                                  first, span) for first, blk, keys, _ in group for h in range(2)]
            soft = []
            for lg in logits:
                mx = jnp.max(lg, axis=-1, keepdims=True)
                e = jnp.exp(lg - mx)
                tot = jnp.sum(e, axis=-1, keepdims=True)
                soft.append(((e * (1.0 / tot)).astype(BF16), mx + jnp.log(tot)))
            for i, (_, _, keys, tokens) in enumerate(group):
                vb = v_ref[keys, :]
                o_acc = jnp.zeros((DIL_BLOCK, HEAD_PAD), F32)
                lse_acc = jnp.zeros((DIL_BLOCK, HEAD_PAD), F32)
                for h in range(2):
                    p, lse = soft[2 * i + h]
                    kmine = (klane[:vb.shape[0]] < DIL_HEAD_DIM) == (h == 0)
                    o_acc = o_acc + _dot_nn(p, jnp.where(kmine, vb, 0))
                    lse_acc = jnp.where((lane < DIL_HEAD_DIM) == (h == 0), lse, lse_acc)
                o_ref[tokens, :] = o_acc
                l_ref[tokens, :] = lse_acc

    out = jax.ShapeDtypeStruct((s_dim, d_dim), F32)
    return _pcall(
        body, out_shape=(out, out), grid=(DIL_PAIRS,),
        in_specs=[col(0), col(1), col(2), bias_spec], out_specs=(nat, nat), name=name,
        compiler_params=_params(("parallel",), 3 * [((s_dim, HEAD_PAD), BF16)] + 2 * [((s_dim, HEAD_PAD), F32)]
                                + [((2, DIL_BLOCK, 2 * DIL_BLOCK), F32)], extra=2**21),
    )(qkv, qkv, qkv, bias)


def _dil_mix(lses, outs, name):
    s_dim, d_dim = outs[0].shape
    tm = TOKEN_TILE
    ng = len(outs)

    def body(*refs):
        ls = [refs[g][...] for g in range(ng)]
        mx = ls[0]
        for g in range(1, ng):
            mx = jnp.maximum(mx, ls[g])
        es = [jnp.exp(l - mx) for l in ls]
        tot = es[0]
        for g in range(1, ng):
            tot = tot + es[g]
        o = None
        for g in range(ng):
            al = es[g] / tot
            refs[2 * ng + g][...] = al
            t = al * refs[ng + g][...]
            o = t if o is None else o + t
        refs[3 * ng][...] = o
        refs[3 * ng + 1][...] = o.astype(BF16)

    row = pl.BlockSpec((tm, d_dim), lambda i: (i, 0))
    f = jax.ShapeDtypeStruct((s_dim, d_dim), F32)
    res = _pcall(
        body, out_shape=tuple(ng * [f] + [f, jax.ShapeDtypeStruct((s_dim, d_dim), BF16)]), grid=(s_dim // tm,),
        in_specs=2 * ng * [row], out_specs=tuple((ng + 2) * [row]), name=name,
        compiler_params=_params(("parallel",), (3 * ng + 2) * [((tm, d_dim), F32)], extra=4 * tm * d_dim * 4),
    )(*lses, *outs)
    return res[:ng], res[ng], res[ng + 1]


def _dil_attn_bwd(qkv, bias, d_o, o_mix, alpha, lse, dilation, span, name):
    s_dim = qkv.shape[0]
    d_dim = DIL_HEADS * DIL_HEAD_DIM
    col, nat, bias_spec = _dil_views(s_dim)

    def body(q_ref, k_ref, v_ref, b_ref, do_ref, om_ref, al_ref, l_ref, dq_ref, dk_ref, dv_ref, db_ref, dk_acc, dv_acc):
        db_ref[...] = jnp.zeros_like(db_ref)
        dk_acc[...] = jnp.zeros_like(dk_acc)
        dv_acc[...] = jnp.zeros_like(dv_acc)
        lane = lax.broadcasted_iota(jnp.int32, (DIL_BLOCK, HEAD_PAD), 1)
        klane = lax.broadcasted_iota(jnp.int32, (2 * DIL_BLOCK, HEAD_PAD), 1)
        blocks = list(_dil_blocks(s_dim, dilation))
        heads = [(lane < DIL_HEAD_DIM) == (h == 0) for h in range(2)]
        for g0 in range(0, len(blocks), DIL_GROUPED):
            group = blocks[g0:g0 + DIL_GROUPED]
            staged = []
            for first, blk, kv_rows, tokens in group:
                qb, kb, vb = q_ref[blk, :], k_ref[kv_rows, :], v_ref[kv_rows, :]
                dog = al_ref[tokens, :] * do_ref[tokens, :]
                row_term = dog * om_ref[tokens, :]
                lse_b = l_ref[tokens, :]
                for h in range(2):
                    qh = jnp.where(heads[h], qb, 0)
                    dogh = jnp.where(heads[h], dog, 0.0).astype(BF16)
                    staged.append((_dil_logits(qh, kb, b_ref[h], first, span), _dot_nt(dogh, vb), qh, dogh,
                                   jnp.max(jnp.where(heads[h], lse_b, -jnp.inf), axis=-1, keepdims=True),
                                   jnp.sum(jnp.where(heads[h], row_term, 0.0), axis=-1, keepdims=True)))
            grads = []
            for i, (logits, dp, qh, dogh, lse_h, row) in enumerate(staged):
                p = jnp.exp(logits - lse_h)
                ds = p * (dp - row)
                if group[i // 2][0]:
                    db_ref[i % 2, :, DIL_BLOCK:] += ds
                else:
                    db_ref[i % 2] += ds
                grads.append(((ds * DIL_SCALE).astype(BF16), p.astype(BF16), qh, dogh))
            for i, (_, blk, kv_rows, _) in enumerate(group):
                kb = k_ref[kv_rows, :]
                dq_acc = jnp.zeros((DIL_BLOCK, HEAD_PAD), F32)
                dk_blk = jnp.zeros((kb.shape[0], HEAD_PAD), F32)
                dv_blk = jnp.zeros((kb.shape[0], HEAD_PAD), F32)
                for h in range(2):
                    dsb, pb, qh, dogh = grads[2 * i + h]
                    kmine = (klane[:kb.shape[0]] < DIL_HEAD_DIM) == (h == 0)
                    dq_acc = dq_acc + _dot_nn(dsb, jnp.where(kmine, kb, 0))
                    dk_blk = dk_blk + _dot_tn(dsb, qh)
                    dv_blk = dv_blk + _dot_tn(pb, dogh)
                dq_ref[blk, :] = dq_acc.astype(BF16)
                dk_acc[kv_rows, :] += dk_blk
                dv_acc[kv_rows, :] += dv_blk
        dk_ref[...] = dk_acc[...].astype(BF16)
        dv_ref[...] = dv_acc[...].astype(BF16)

    grad = jax.ShapeDtypeStruct((s_dim, d_dim), BF16)
    return _pcall(
        body, out_shape=(grad, grad, grad, jax.ShapeDtypeStruct(bias.shape, F32)), grid=(DIL_PAIRS,),
        in_specs=[col(0), col(1), col(2), bias_spec, nat, nat, nat, nat],
        out_specs=(nat, nat, nat, bias_spec), name=name,
        scratch_shapes=[pltpu.VMEM((s_dim, HEAD_PAD), F32), pltpu.VMEM((s_dim, HEAD_PAD), F32)],
        compiler_params=_params(("parallel",), 6 * [((s_dim, HEAD_PAD), BF16)] + 4 * [((s_dim, HEAD_PAD), F32)]
                                + 2 * [((2, DIL_BLOCK, 2 * DIL_BLOCK), F32)], extra=2 * s_dim * HEAD_PAD * 4 + 2**21),
    )(qkv, qkv, qkv, bias, d_o, o_mix, alpha, lse)


def _bias_reduce(dbias, buckets, name):
    n_heads = dbias.shape[0]

    def body(db_ref, bk_ref, o_ref):
        ds, bk = db_ref[0], bk_ref[0]
        lane = lax.broadcasted_iota(jnp.int32, (8, HEAD_PAD), 1)
        acc = jnp.zeros((8, HEAD_PAD), F32)
        for b in range(N_BUCKETS):
            acc = jnp.where(lane == b, jnp.sum(jnp.where(bk == b, ds, 0.0)), acc)
        o_ref[0] = acc

    blk = (1, DIL_BLOCK, 2 * DIL_BLOCK)
    return _pcall(
        body, out_shape=jax.ShapeDtypeStruct((n_heads, 8, HEAD_PAD), F32), grid=(n_heads,),
        in_specs=[pl.BlockSpec(blk, lambda h: (h, 0, 0)), pl.BlockSpec(blk, lambda h: (h // DIL_HEADS, 0, 0))],
        out_specs=pl.BlockSpec((1, 8, HEAD_PAD), lambda h: (h, 0, 0)), name=name,
        compiler_params=_params(("parallel",), [(blk, F32), (blk, jnp.int32)], extra=2**20),
    )(dbias, buckets)


def _loss_grad(y, target, name):
    s_dim, d_dim = y.shape
    tm = TOKEN_TILE

    def body(y_ref, t_ref, dy_ref, l_ref):
        @pl.when(pl.program_id(0) == 0)
        def _():
            l_ref[...] = jnp.zeros_like(l_ref)

        err = y_ref[...] - t_ref[...]
        dy_ref[...] = err / d_dim
        sq = (err * err).reshape(tm // 8, 8, d_dim)
        l_ref[...] += 0.5 * jnp.sum(sq, axis=0) / d_dim

    row = pl.BlockSpec((tm, d_dim), lambda i: (i, 0))
    acc = pl.BlockSpec((8, d_dim), lambda i: (0, 0))
    return _pcall(
        body, out_shape=(jax.ShapeDtypeStruct((s_dim, d_dim), F32), jax.ShapeDtypeStruct((8, d_dim), F32)),
        grid=(s_dim // tm,), in_specs=[row, row], out_specs=(row, acc), name=name,
        compiler_params=_params(("arbitrary",), 3 * [((tm, d_dim), F32)], extra=2 * tm * d_dim * 4),
    )(y, target)


def _mod_fwd(c_all, w_mod, b_loc, name):
    depth, d_dim, n = w_mod.shape
    nb = c_all.shape[0]

    def body(c_ref, w_ref, b_ref, o_ref, s_ref):
        cv = c_ref[...]
        sc = cv * jax.nn.sigmoid(cv)
        s_ref[...] = sc
        o_ref[0] = _dot_nn(sc.astype(BF16), w_ref[0].astype(BF16)) + b_ref[0]

    return _pcall(
        body, out_shape=(jax.ShapeDtypeStruct((depth, nb, n), F32), jax.ShapeDtypeStruct((nb, d_dim), F32)), grid=(depth,),
        in_specs=[pl.BlockSpec((nb, d_dim), lambda i: (0, 0)), pl.BlockSpec((1, d_dim, n), lambda i: (i, 0, 0)),
                  pl.BlockSpec((1, 1, n), lambda i: (i, 0, 0))],
        out_specs=(pl.BlockSpec((1, nb, n), lambda i: (i, 0, 0)), pl.BlockSpec((nb, d_dim), lambda i: (0, 0))), name=name,
        compiler_params=_params(("arbitrary",), [((1, d_dim, n), F32)], extra=d_dim * n * 2 + 2**20),
    )(c_all, w_mod, b_loc.reshape(depth, 1, n))


def _sum_parts(parts, name, transpose=False):
    _, rows, cols = parts.shape
    unit = 128 if transpose else 16
    budget = (7 if transpose else 3) * 2**20
    fits = [t for t in range(unit, rows // 2 + 1, unit) if rows % t == 0 and NDEV * t * cols * parts.dtype.itemsize <= budget]
    tr = max(fits) if fits else rows

    def body(p_ref, o_ref):
        acc = p_ref[0].astype(F32)
        for k in range(1, NDEV):
            acc = acc + p_ref[k].astype(F32)
        o_ref[...] = acc.T if transpose else acc

    out_shape, out_block = ((cols, rows), (cols, tr)) if transpose else ((rows, cols), (tr, cols))
    return _pcall(
        body, out_shape=jax.ShapeDtypeStruct(out_shape, F32), grid=(rows // tr,),
        in_specs=[pl.BlockSpec((NDEV, tr, cols), lambda i: (0, i, 0))],
        out_specs=pl.BlockSpec(out_block, (lambda i: (0, i)) if transpose else (lambda i: (i, 0))),
        name=name, compiler_params=_params(("parallel",), [((NDEV, tr, cols), parts.dtype), (out_block, F32)], extra=2**22),
    )(parts)


def _adamw(w, g, m, v, name):
    shape = w.shape
    cols = shape[-1]
    rows = math.prod(shape[:-1])
    tr = rows
    for cand in (2048, 1024, 512, 256, 128, 64, 32, 16, 8):
        if rows % cand == 0 and rows > cand and cand * cols * 4 <= 2**21:
            tr = cand
            break

    def body(w_ref, g_ref, m_ref, v_ref, d_ref, mo_ref, vo_ref):
        gv = g_ref[...]
        mn = ADAM_B1 * m_ref[...] + (1.0 - ADAM_B1) * gv
        vn = ADAM_B2 * v_ref[...] + (1.0 - ADAM_B2) * (gv * gv)
        m_hat = mn / (1.0 - ADAM_B1 ** ADAM_STEP)
        v_hat = vn / (1.0 - ADAM_B2 ** ADAM_STEP)
        d_ref[...] = -ADAM_LR * (m_hat / (jnp.sqrt(v_hat) + ADAM_EPS) + ADAM_WD * w_ref[...])
        mo_ref[...] = mn
        vo_ref[...] = vn

    blk = pl.BlockSpec((tr, cols), lambda i: (i, 0))
    out = jax.ShapeDtypeStruct((rows, cols), F32)
    res = _pcall(
        body, out_shape=(out, out, out), grid=(rows // tr,), in_specs=4 * [blk], out_specs=(blk, blk, blk), name=name,
        compiler_params=_params(("parallel",), 7 * [((tr, cols), F32)], extra=4 * tr * cols * 4),
    )(*(a.reshape(rows, cols) for a in (w, g, m, v)))
    return tuple(r.reshape(shape) for r in res)


def _peers():
    x, y, c = lax.axis_index("x"), lax.axis_index("y"), lax.axis_index("c")
    flip = lambda v, f: 1 - v if f else v
    peers = []
    for f in range(1, NDEV):
        px, py, pc = flip(x, f & 4), flip(y, f & 2), flip(c, f & 1)
        peers.append(((px, py, pc), 4 * px + 2 * py + pc))
    return (x, y, c), 4 * x + 2 * y + c, peers


def _places():
    x, y, c = lax.axis_index("x"), lax.axis_index("y"), lax.axis_index("c")
    place = lambda px, py, pc: ((px, py, pc), 4 * px + 2 * py + pc)
    return place(x, y, c), place(x, y, 1 - c), [place(1 - x, y, c), place(x, 1 - y, c), place(1 - x, 1 - y, c)]


def _exchange(arrs, gather, name):
    n = len(arrs)
    hbm = pl.BlockSpec(memory_space=pltpu.HBM)
    if gather:
        out_shape = [jax.ShapeDtypeStruct((NDEV * a.shape[0], a.shape[1]), a.dtype) for a in arrs]
    else:
        out_shape = [jax.ShapeDtypeStruct((NDEV, a.shape[0] // NDEV, a.shape[1]), a.dtype) for a in arrs]

    def body(*refs):
        ins, outs = refs[:n], refs[n:2 * n]
        send_sems, recv_sems, local_sems = refs[2 * n:]
        me_pos, me, peers = _peers()
        local = []
        for k in range(n):
            rows = arrs[k].shape[0] if gather else arrs[k].shape[0] // NDEV
            if gather:
                src_of = lambda idx: ins[k]
                dst_of = lambda idx: outs[k].at[pl.ds(me * rows, rows)]
                mine = (ins[k], outs[k].at[pl.ds(me * rows, rows)])
            else:
                src_of = lambda idx: ins[k].at[pl.ds(idx * rows, rows)]
                dst_of = lambda idx: outs[k].at[me]
                mine = (ins[k].at[pl.ds(me * rows, rows)], outs[k].at[me])
            cp = pltpu.make_async_copy(mine[0], mine[1], local_sems.at[k])
            cp.start()
            local.append(cp)
            for pos, idx in peers:
                pltpu.make_async_remote_copy(src_ref=src_of(idx), dst_ref=dst_of(idx), send_sem=send_sems.at[k],
                                             recv_sem=recv_sems.at[k], device_id=pos, device_id_type=MESH).start()
        for k in range(n):
            rows = arrs[k].shape[0] if gather else arrs[k].shape[0] // NDEV
            sent = ins[k].at[pl.ds(0, (NDEV - 1) * rows)] if not gather else outs[k].at[pl.ds(0, (NDEV - 1) * rows)]
            got = outs[k].at[pl.ds(0, (NDEV - 1) * rows)] if gather else outs[k].at[pl.ds(0, NDEV - 1)]
            pltpu.make_async_remote_copy(src_ref=sent, dst_ref=sent, send_sem=send_sems.at[k], recv_sem=recv_sems.at[k],
                                         device_id=me_pos, device_id_type=MESH).wait_send()
            pltpu.make_async_remote_copy(src_ref=got, dst_ref=got, send_sem=send_sems.at[k], recv_sem=recv_sems.at[k],
                                         device_id=me_pos, device_id_type=MESH).wait_recv()
            local[k].wait()

    return pl.pallas_call(
        body, out_shape=out_shape, in_specs=n * [hbm], out_specs=n * [hbm], name=name,
        scratch_shapes=[pltpu.SemaphoreType.DMA((n,)), pltpu.SemaphoreType.DMA((n,)), pltpu.SemaphoreType.DMA((n,))],
        compiler_params=pltpu.CompilerParams(has_side_effects=True),
    )(*arrs)


_HBM = pl.BlockSpec(memory_space=pltpu.HBM)
_SEM = pl.BlockSpec(memory_space=pltpu.SEMAPHORE)
_DATAFLOW = pltpu.SideEffectType.DATAFLOW_SIDE_EFFECTING


def _split_start(srcs, groups, gather, name, after=None):
    n = len(srcs)
    if gather:
        lands = [lax.empty((NDEV * a.shape[0], a.shape[1]), a.dtype) for a in srcs]
    else:
        lands = [lax.empty((NDEV, a.shape[0] // NDEV, a.shape[1]), a.dtype) for a in srcs]
    n_sem = 3 * len(groups)
    extra = [] if after is None else [after]
    n_in = 2 * n + len(extra)

    def body(*refs):
        src_refs, land_refs = refs[:n], refs[n:2 * n]
        sems = refs[n_in:n_in + n_sem]
        token = refs[-1]
        (_, my), sibling, chips = _places()
        _, _, peers = _peers()
        targets = [sibling] + chips if gather else peers
        for g, members in enumerate(groups):
            for j, k in enumerate(members):
                _own_copy(src_refs[k], land_refs[k], sems[3 * g + 2].at[j], my, gather).start()
        for g, members in enumerate(groups):
            for j, k in enumerate(members):
                rows = srcs[k].shape[0] if gather else srcs[k].shape[0] // NDEV
                for pos, idx in targets:
                    src = src_refs[k] if gather else src_refs[k].at[pl.ds(idx * rows, rows)]
                    dst = land_refs[k].at[pl.ds(my * rows, rows)] if gather else land_refs[k].at[my]
                    pltpu.make_async_remote_copy(src_ref=src, dst_ref=dst, send_sem=sems[3 * g].at[j],
                                                 recv_sem=sems[3 * g + 1].at[j], device_id=pos, device_id_type=MESH).start()
        token[...] = jnp.zeros_like(token)

    out_shape = []
    for members in groups:
        out_shape += 3 * [pltpu.SemaphoreType.DMA((len(members),))]
    out_shape += [pltpu.HBM(a.shape, a.dtype) for a in srcs] + [pltpu.HBM(a.shape, a.dtype) for a in lands]
    out_shape.append(jax.ShapeDtypeStruct((8, 128), F32))
    res = pl.pallas_call(
        body, name=name, out_shape=tuple(out_shape), in_specs=2 * n * [_HBM] + len(extra) * [pl.BlockSpec(memory_space=pl.ANY)],
        out_specs=tuple(n_sem * [_SEM] + 2 * n * [_HBM] + [pl.BlockSpec(memory_space=pltpu.VMEM)]),
        input_output_aliases={i: n_sem + i for i in range(2 * n)},
        compiler_params=pltpu.CompilerParams(has_side_effects=_DATAFLOW),
    )(*[pltpu.with_memory_space_constraint(a, pltpu.HBM) for a in list(srcs) + lands], *extra)
    sems = [tuple(res[3 * g:3 * g + 3]) for g in range(len(groups))]
    return sems, list(res[n_sem:n_sem + n]), list(res[n_sem + n:n_sem + 2 * n]), res[-1]


def _own_copy(src_ref, land_ref, sem, my, gather):
    if gather:
        rows = src_ref.shape[0]
        return pltpu.make_async_copy(src_ref, land_ref.at[pl.ds(my * rows, rows)], sem)
    rows = src_ref.shape[0] // NDEV
    return pltpu.make_async_copy(src_ref.at[pl.ds(my * rows, rows)], land_ref.at[my], sem)


def _wait_all(land_ref, blocks_per_dev, copies, send_sem, recv_sem, me_pos):
    part = land_ref.at[pl.ds(0, copies * blocks_per_dev)]
    pltpu.make_async_remote_copy(src_ref=part, dst_ref=part, send_sem=send_sem, recv_sem=recv_sem,
                                 device_id=me_pos, device_id_type=MESH).wait()


def _gather_forward(sems, srcs, lands, after, name):
    n = len(srcs)

    def body(*refs):
        land_refs = refs[n:2 * n]
        send_a, recv_a = refs[2 * n], refs[2 * n + 1]
        send_b, recv_b = refs[2 * n + 3], refs[2 * n + 4]
        token = refs[-1]
        (me_pos, _), sibling, chips = _places()
        for j in range(n):
            _wait_all(land_refs[j], lands[j].shape[0] // NDEV, 1 + OTHER_CHIPS, send_a.at[j], recv_a.at[j], me_pos)
        for j in range(n):
            rows = lands[j].shape[0] // NDEV
            for _, idx in chips:
                block = land_refs[j].at[pl.ds(idx * rows, rows)]
                pltpu.make_async_remote_copy(src_ref=block, dst_ref=block, send_sem=send_b.at[j], recv_sem=recv_b.at[j],
                                             device_id=sibling[0], device_id_type=MESH).start()
        token[...] = jnp.zeros_like(token)

    res = pl.pallas_call(
        body, name=name,
        out_shape=(pltpu.SemaphoreType.DMA((n,)), pltpu.SemaphoreType.DMA((n,)))
        + tuple(pltpu.HBM(a.shape, a.dtype) for a in list(srcs) + list(lands)) + (jax.ShapeDtypeStruct((8, 128), F32),),
        in_specs=2 * n * [_HBM] + [_SEM, _SEM, pl.BlockSpec(memory_space=pl.ANY)],
        out_specs=tuple([_SEM, _SEM] + 2 * n * [_HBM] + [pl.BlockSpec(memory_space=pltpu.VMEM)]),
        input_output_aliases={i: 2 + i for i in range(2 * n)},
        compiler_params=pltpu.CompilerParams(has_side_effects=_DATAFLOW),
    )(*srcs, *lands, sems[0], sems[1], after)
    return (res[0], res[1]), list(res[2:2 + n]), list(res[2 + n:2 + 2 * n]), res[-1]


def _split_wait(sems, srcs, lands, after, copies, gather, name):
    n = len(srcs)

    def body(*refs):
        src_refs, land_refs = refs[:n], refs[n:2 * n]
        send_sem, recv_sem, local_sem = refs[2 * n], refs[2 * n + 1], refs[2 * n + 2]
        (me_pos, my), _, _ = _places()
        for j in range(n):
            _wait_all(land_refs[j], lands[j].shape[0] // NDEV, copies, send_sem.at[j], recv_sem.at[j], me_pos)
            _own_copy(src_refs[j], land_refs[j], local_sem.at[j], my, gather).wait()

    res = pl.pallas_call(
        body, name=name, out_shape=tuple(pltpu.HBM(a.shape, a.dtype) for a in list(srcs) + list(lands)),
        in_specs=2 * n * [_HBM] + [_SEM, _SEM, _SEM, pl.BlockSpec(memory_space=pl.ANY)], out_specs=tuple(2 * n * [_HBM]),
        input_output_aliases={i: i for i in range(2 * n)},
        compiler_params=pltpu.CompilerParams(has_side_effects=_DATAFLOW),
    )(*srcs, *lands, sems[0], sems[1], sems[2], after)
    return list(res[n:])


def _chained(gate, mid, after):
    return gate if mid is None else gate + mid(after)[:1, :1]


def _ffn_fwd(x, norms, mod, w, mid=None):
    (pre_g, post_g), (shift, scale, gate), (wg_t, wu_t, wd) = norms, mod, w
    if not callable(wd):
        hn, g, u, a, x_out, f = _ffn_fwd_fused(x, pre_g, scale, shift, post_g, _chained(gate, mid, x), wg_t, wu_t, wd, "ffn_fwd")
        return x_out, (x, hn, g, u, a, f), (wg_t, wu_t, wd)
    hn, g, u, a = _ffn_up(x, pre_g, scale, shift, wg_t, wu_t, "ffn_up")
    wd = wd(a)
    x_out, f = _mm_post(a, wd, x, post_g, _chained(gate, mid, a), FFN_RES, "ffn_down")
    return x_out, (x, hn, g, u, a, f), (wg_t, wu_t, wd)


def _ffn_bwd(dx_out, saved, norms, mod, w, send=None):
    (pre_g, post_g), (_, scale, gate), (wg_t, wu_t, wd) = norms, mod, w
    x, hn, g, u, a, f = saved
    d_model = x.shape[1]
    if send is None:
        df, dg, du, dx, dgate, dpost, dshift, dscale, dpre = _ffn_bwd_fused(dx_out, saved, pre_g, post_g, scale, gate,
                                                                            wg_t, wu_t, wd, "ffn_bwd")
        return dx, (dpre, dpost), (dshift, dscale, dgate), tuple(_ffn_dw(dg, du, a, hn, df, "ffn_dw3"))
    sent = send
    df, dgate, dpost = _post_bwd(dx_out, f, post_g, gate, FFN_RES, "ffn_post_bwd")
    dwd = _mm([(a, df)], "tn", BF16, 256, d_model, "ffn_dw")
    dg, du = _ffn_dgu(df, wd, g, u, "ffn_dgu", after=sent(2, dwd))
    dwg_t = _mm([(dg, hn)], "tn", BF16, 256, d_model, "ffn_dw")
    dwu_t = _mm([(du, hn)], "tn", BF16, 256, d_model, "ffn_dw", after=sent(0, dwg_t))
    dhn = _mm([(dg, wg_t), (du, wu_t)], "nn", F32, TOKEN_TILE, d_model, "ffn_dhn", after=sent(1, dwu_t))
    dx, dshift, dscale, dpre = _prenorm_bwd(dx_out, [dhn], x, pre_g, scale, "prenorm_bwd")
    return dx, (dpre, dpost), (dshift, dscale, dgate), (dwg_t, dwu_t, dwd)


def _mla_fwd(x, norms, mod, w, rope, mid=None):
    (pre_g, post_g), (shift, scale, gate) = norms, mod
    w_in, q_norm, wq_t, kv_norm, wkv_t, wo = w
    hn, lat = _prenorm_mm(x, pre_g, scale, shift, w_in, "nn", F32, LAT_PAD, "mla_in")
    gate = _chained(gate, mid, lat)
    q, k, v, qn, kvn = _mla_qkv(lat, q_norm, kv_norm, wq_t, wkv_t, rope, "mla_qkv")
    o = _mla_attn_fwd(q, k, v, "mla_attn_fwd")
    x_out, f = _mm_post(o, wo, x, post_g, gate, 1.0, "mla_out")
    return x_out, (x, hn, lat, q, k, v, qn, kvn, o, f)


def _mla_bwd(dx_out, saved, norms, mod, w, rope):
    (pre_g, post_g), (_, scale, gate) = norms, mod
    w_in, q_norm, wq_t, kv_norm, wkv_t, wo = w
    x, hn, lat, q, k, v, qn, kvn, o, f = saved
    d_model = x.shape[1]
    df, dgate, dpost = _post_bwd(dx_out, f, post_g, gate, 1.0, "mix_post_bwd")
    d_o = _mm([(df, wo)], "nt", F32, TOKEN_TILE, wo.shape[0], "mla_do")
    dwo = _mm([(o, df)], "tn", BF16, TOKEN_TILE, d_model, "mla_dwo")
    dq, dk, dv = _mla_attn_bwd(q, k, v, d_o, "mla_attn_bwd")
    dqp, dkv, dlat, dq_norm, dkv_norm = _mla_qkv_bwd(dq, dk, dv, lat, q_norm, kv_norm, wq_t, wkv_t, rope, "mla_qkv_bwd")
    dwq_t = _mm([(dqp, qn)], "tn", BF16, TOKEN_TILE, Q_LORA, "mla_dwq")
    dwkv_t = _mm([(dkv, kvn)], "tn", BF16, TOKEN_TILE, KV_LORA, "mla_dwkv")
    dw_in = _mm([(hn, dlat)], "tn", BF16, TOKEN_TILE, LAT_PAD, "mla_dwin")
    dhn = _mm([(dlat, w_in)], "nt", F32, TOKEN_TILE, d_model, "mla_dhn")
    dx, dshift, dscale, dpre = _prenorm_bwd(dx_out, [dhn], x, pre_g, scale, "prenorm_bwd")
    return dx, (dpre, dpost), (dshift, dscale, dgate), (dw_in, dq_norm, dwq_t, dkv_norm, dwkv_t, dwo)


def _dil_fwd(x, norms, mod, w, bias, mid=None):
    (pre_g, post_g), (shift, scale, gate), (w_in_t, wo) = norms, mod, w
    width = 3 * DIL_HEADS * DIL_HEAD_DIM
    hns, qkvs, outs, lses = [], [], [], []
    for g, (window, dilation) in enumerate(DIL_GROUPS):
        hn, qkv = _prenorm_mm(x, pre_g, scale, shift, w_in_t, "nt", BF16, width, "dil_in", perm=dilation,
                              w_rows=(g * width, width))
        if g == 0:
            gate = _chained(gate, mid, qkv)
        o, lse = _dil_attn_fwd(qkv, bias[g], dilation, window // dilation, "dil_attn_fwd")
        hns.append(hn), qkvs.append(qkv), outs.append(o), lses.append(lse)
    alphas, o_mix, o_mix_b = _dil_mix(lses, outs, "dil_mix")
    x_out, f = _mm_post(o_mix_b, wo, x, post_g, gate, 1.0, "dil_out")
    return x_out, (x, hns, qkvs, lses, alphas, o_mix, o_mix_b, f)


def _dil_bwd(dx_out, saved, norms, mod, w, bias):
    (pre_g, post_g), (_, scale, gate), (w_in_t, wo) = norms, mod, w
    x, hns, qkvs, lses, alphas, o_mix, o_mix_b, f = saved
    d_model = x.shape[1]
    inner = DIL_HEADS * DIL_HEAD_DIM
    df, dgate, dpost = _post_bwd(dx_out, f, post_g, gate, 1.0, "mix_post_bwd")
    d_o = _mm([(df, wo)], "nt", F32, TOKEN_TILE, inner, "dil_do")
    dwo = _mm([(o_mix_b, df)], "tn", BF16, TOKEN_TILE, d_model, "dil_dwo")
    dhns, dws, dbs = [], [], []
    for g, (window, dilation) in enumerate(DIL_GROUPS):
        grads = _dil_attn_bwd(qkvs[g], bias[g], d_o, o_mix, alphas[g], lses[g], dilation, window // dilation, "dil_attn_bwd")
        dbs.append(grads[3])
        dhns.append(_mm([(grads[j], w_in_t) for j in range(3)], "nn", F32, TOKEN_TILE, d_model, "dil_dhn", out_perm=dilation,
                        b_rows=[(3 * g + j) * inner for j in range(3)]))
        dws += list(_mm_tn_shared(list(grads[:3]), hns[g], "dil_dwin"))
    dx, dshift, dscale, dpre = _prenorm_bwd(dx_out, dhns, x, pre_g, scale, "prenorm_bwd3")
    return dx, (dpre, dpost), (dshift, dscale, dgate), (jnp.concatenate(dws, axis=0), dwo), jnp.concatenate(dbs, axis=0)


def _pad_rows(a, rows):
    return jnp.pad(a, ((0, rows - a.shape[0]), (0, 0)))


def _lanes(a):
    flat = a.reshape(-1).astype(F32)
    rows = -(-flat.shape[0] // 1024) * 8
    return jnp.pad(flat, (0, rows * 128 - flat.shape[0])).reshape(rows, 128)


def kernel(x, c, norm_pre, norm_post, w_mod, b_mod, ffn_w_gate, ffn_w_up, ffn_w_down, mla_w_in, mla_q_norm, mla_w_q_up, mla_kv_norm, mla_w_kv_up, mla_w_o, dil_w_in, dil_w_o, rel_bias, loss_target, m_norm_pre, m_norm_post, m_w_mod, m_b_mod, m_ffn_w_gate, m_ffn_w_up, m_ffn_w_down, m_mla_w_in, m_mla_q_norm, m_mla_w_q_up, m_mla_kv_norm, m_mla_w_kv_up, m_mla_w_o, m_dil_w_in, m_dil_w_o, m_rel_bias, v_norm_pre, v_norm_post, v_w_mod, v_b_mod, v_ffn_w_gate, v_ffn_w_up, v_ffn_w_down, v_mla_w_in, v_mla_q_norm, v_mla_w_q_up, v_mla_kv_norm, v_mla_w_kv_up, v_mla_w_o, v_dil_w_in, v_dil_w_o, v_rel_bias):
    me = 4 * lax.axis_index("x") + 2 * lax.axis_index("y") + lax.axis_index("c")
    depth, n_sub, d_loc = norm_pre.shape
    d_model = x.shape[2]
    mod_loc_cols = w_mod.shape[2]
    x0, target = x[0], loss_target[0]

    bf_t = lambda a: a.astype(BF16).T
    ffn_ids = [(i, h) for i in range(depth) for h in range(2)]
    shards = []
    for i, h in ffn_ids:
        shards += [bf_t(ffn_w_gate[i, h]), bf_t(ffn_w_up[i, h]), ffn_w_down[i, h].astype(BF16)]
    shards += [mla_w_in[0].astype(BF16), bf_t(mla_w_q_up[0]), bf_t(mla_w_kv_up[0]), mla_w_o[0].astype(BF16),
               bf_t(dil_w_in[0]), dil_w_o[0].astype(BF16)]
    n_ffn = 3 * len(ffn_ids)
    members = {(0, 0): [0, 1, 2], (0, 1): [n_ffn, n_ffn + 1, n_ffn + 2, n_ffn + 3], (0, 2): [3, 4, 5],
               (1, 0): [6, 7, 8], (1, 1): [n_ffn + 4, n_ffn + 5], (1, 2): [9, 10, 11]}
    order = [(i, s) for i in range(depth) for s in range(n_sub)]

    small = jnp.concatenate([c.reshape(8, 128), _pad_rows(norm_pre.reshape(depth * n_sub, d_loc), 8),
                             _pad_rows(norm_post.reshape(depth * n_sub, d_loc), 8)], axis=0)
    small_all = _exchange([small], True, "gather_small")[0].reshape(NDEV, 24, 128)
    c_all = small_all[:, 0:8].reshape(NDEV, d_model)
    gains = lambda lo: jnp.transpose(small_all[:, lo:lo + depth * n_sub], (1, 0, 2)).reshape(depth, n_sub, 1, d_model)
    pre_full, post_full = gains(8), gains(16)

    b_loc = lax.dynamic_slice(b_mod, (0, me * mod_loc_cols), (depth, mod_loc_cols))
    mod_cols, silu_c = _mod_fwd(c_all, w_mod, b_loc, "mod_fwd")
    mod_all = _exchange([mod_cols.reshape(depth * NDEV, mod_loc_cols)], True, "gather_mod")[0]
    mod_all = mod_all.reshape(NDEV, depth, NDEV, mod_loc_cols)
    mod_mine = lax.dynamic_index_in_dim(mod_all, me, axis=2, keepdims=False)
    mod = jnp.transpose(mod_mine, (1, 0, 2)).reshape(depth, n_sub, 3, 1, d_model)

    first = order[0]
    stages = [("%d%d" % first, members[first][:2]), ("%d%dd" % first, members[first][2:])]
    stages += [("%d%d" % key, members[key]) for key in order[1:]]
    started = {}

    def start(these, name, after):
        used = [k for _, idx in these for k in idx]
        sems, srcs, lands, token = _split_start([shards[k] for k in used], [[used.index(k) for k in idx] for _, idx in these],
                                                True, name, after)
        for n, (stage, idx) in enumerate(these):
            started[stage] = (sems[n], [srcs[used.index(k)] for k in idx], [lands[used.index(k)] for k in idx])
        return token

    g_token = start(stages[:2], "gather_weights_start_first", mod_all)
    start(stages[2:], "gather_weights_start_rest", g_token)

    forwarded = {}

    def forward(stage, after):
        sems, srcs, lands = started[stage]
        forwarded[stage] = _gather_forward(sems, srcs, lands, after, "gather_forward_" + stage)
        return forwarded[stage][3]

    def weights_of(stage, after):
        (send_b, recv_b), srcs, lands, _ = forwarded[stage]
        return _split_wait((send_b, recv_b, started[stage][0][2]), srcs, lands, after, OTHER_CHIPS, True, "gather_wait_" + stage)

    def late_down(after):
        forward("%d%dd" % first, after)
        return weights_of("%d%dd" % first, after)[0]

    lat_real = Q_LORA + KV_LORA
    qk = QK_NOPE + QK_ROPE

    def mla_weights(after):
        w_in, wq_t, wkv_t, wo = weights_of("01", after)
        w_in_pad = jnp.concatenate([w_in[:, :lat_real], jnp.zeros((d_model, QK_NOPE), BF16), w_in[:, lat_real:],
                                    jnp.zeros((d_model, HEAD_PAD - QK_NOPE - QK_ROPE), BF16)], axis=1)
        wq_pad = jnp.pad(wq_t.reshape(MLA_HEADS, qk, Q_LORA), ((0, 0), (0, HEAD_PAD - qk), (0, 0)))
        wo_pad = jnp.pad(wo.reshape(MLA_HEADS, V_HEAD, d_model), ((0, 0), (HEAD_PAD - V_HEAD, 0), (0, 0)))
        return (w_in_pad, mla_q_norm, wq_pad.reshape(MLA_HEADS * HEAD_PAD, Q_LORA), mla_kv_norm, wkv_t,
                wo_pad.reshape(MLA_HEADS * HEAD_PAD, d_model))

    zero = g_token[0, 0]
    rope = _rope_tables(zero)
    buckets = jnp.stack([_dil_buckets(dil) for _, dil in DIL_GROUPS]) + zero.astype(jnp.int32)
    onehot = (buckets[..., None] == jnp.arange(N_BUCKETS)).astype(F32)
    bias = jnp.einsum("gqkb,bgh->ghqk", onehot, rel_bias.reshape(N_BUCKETS, len(DIL_GROUPS), DIL_HEADS),
                      precision=lax.Precision.HIGHEST)

    norms = lambda i, s: (pre_full[i, s], post_full[i, s])
    mods = lambda i, s: (mod[i, s, 0], mod[i, s, 1], mod[i, s, 2])
    saved, weights = {}, {}
    h = x0
    forward("%d%d" % first, bias)
    for n, (i, s) in enumerate(order):
        got = mla_weights(h) if (s == 1 and i % 2 == 0) else tuple(weights_of("%d%d" % (i, s), h))
        mid = None if n + 1 == len(order) else (lambda after, nxt="%d%d" % order[n + 1]: forward(nxt, after))
        if s != 1:
            if len(got) == 3:
                h, saved[i, s], weights[i, s] = _ffn_fwd(h, norms(i, s), mods(i, s), got)
                if mid is not None:
                    mid(h)
            else:
                h, saved[i, s], weights[i, s] = _ffn_fwd(h, norms(i, s), mods(i, s), (*got, late_down), mid)
            continue
        weights[i, s] = got
        if i % 2 == 0:
            h, saved[i, s] = _mla_fwd(h, norms(i, s), mods(i, s), weights[i, s], rope, mid)
        else:
            h, saved[i, s] = _dil_fwd(h, norms(i, s), mods(i, s), weights[i, s], bias, mid)
    dh, loss_parts = _loss_grad(h, target, "loss")

    dnorm, dmod, sent = {}, {}, {}
    token = jnp.zeros((8, 128), F32)
    last = order[0]

    def send_last(j, dw):
        sent[last, j] = _split_start([dw], [[0]], False, "scatter_start_%d%d_%d" % (*last, j))
        return sent[last, j][3]

    for i, s in reversed(order):
        md = mods(i, s)
        md = (md[0], md[1], md[2] + token[:1, :1])
        if (i, s) == last:
            dh, dnorm[i, s], dmod[i, s], _ = _ffn_bwd(dh, saved[i, s], norms(i, s), md, weights[i, s], send_last)
            continue
        if s != 1:
            dh, dnorm[i, s], dmod[i, s], dws = _ffn_bwd(dh, saved[i, s], norms(i, s), md, weights[i, s])
        elif i % 2 == 0:
            dh, dnorm[i, s], dmod[i, s], dmla = _mla_bwd(dh, saved[i, s], norms(i, s), md, weights[i, s], rope)
            dw_in_pad, dq_norm, dwq_pad, dkv_norm, dwkv_t, dwo_pad = dmla
            dw_in = jnp.concatenate([dw_in_pad[:, :lat_real], dw_in_pad[:, lat_real + QK_NOPE:lat_real + qk]], axis=1)
            dwq_t = dwq_pad.reshape(MLA_HEADS, HEAD_PAD, Q_LORA)[:, :qk].reshape(MLA_HEADS * qk, Q_LORA)
            dwo = dwo_pad.reshape(MLA_HEADS, HEAD_PAD, d_model)[:, HEAD_PAD - V_HEAD:].reshape(MLA_HEADS * V_HEAD, d_model)
            dws = (dw_in, dwq_t, dwkv_t, dwo)
        else:
            dh, dnorm[i, s], dmod[i, s], dws, dbias = _dil_bwd(dh, saved[i, s], norms(i, s), md, weights[i, s], bias)
        sent[i, s] = _split_start(list(dws), [list(range(len(dws)))], False, "scatter_start_%d%d" % (i, s))
        token = sent[i, s][3]
    grad_x = dh[None]

    mine = {}
    transposed = {3 * n + j for n in range(len(ffn_ids)) for j in (0, 1)} | {n_ffn + 1, n_ffn + 2, n_ffn + 4}
    for key in reversed(order[1:]):
        sems, srcs, lands, _ = sent[key]
        parts = _split_wait(sems[0], srcs, lands, dh, NDEV - 1, False, "scatter_wait_%d%d" % key)
        for k, p in zip(members[key], parts):
            mine[k] = _sum_parts(p, "sum_parts", k in transposed)
    g_mla_in, g_q_up, g_kv_up, g_mla_o, g_dil_in, g_dil_o = (mine[k] for k in range(n_ffn, n_ffn + 6))
    g_mla_in, g_q_up, g_kv_up, g_mla_o = g_mla_in[None], g_q_up[None], g_kv_up[None], g_mla_o[None]
    g_dil_in, g_dil_o = g_dil_in[None], g_dil_o[None]
    early = {"mla_w_in": _adamw(mla_w_in, g_mla_in, m_mla_w_in, v_mla_w_in, "adamw"),
             "mla_w_q_up": _adamw(mla_w_q_up, g_q_up, m_mla_w_q_up, v_mla_w_q_up, "adamw"),
             "mla_w_kv_up": _adamw(mla_w_kv_up, g_kv_up, m_mla_w_kv_up, v_mla_w_kv_up, "adamw"),
             "mla_w_o": _adamw(mla_w_o, g_mla_o, m_mla_w_o, v_mla_w_o, "adamw"),
             "dil_w_in": _adamw(dil_w_in, g_dil_in, m_dil_w_in, v_dil_w_in, "adamw"),
             "dil_w_o": _adamw(dil_w_o, g_dil_o, m_dil_w_o, v_dil_w_o, "adamw")}
    dbias_sums = _bias_reduce(dbias, buckets, "bias_reduce")
    tied = lax.optimization_barrier((dbias_sums, *[a for step in early.values() for a in step]))
    dbias_sums, early = tied[0], {name: tuple(tied[1 + 3 * n:4 + 3 * n]) for n, name in enumerate(early)}
    for j in (2, 0, 1):
        sems, srcs, lands, _ = sent[last, j]
        parts = _split_wait(sems[0], srcs, lands, dbias_sums, NDEV - 1, False, "scatter_wait_%d%d_%d" % (*last, j))
        mine[members[last][j]] = _sum_parts(parts[0], "sum_parts", members[last][j] in transposed)
    g_gate = jnp.stack([mine[3 * n] for n in range(len(ffn_ids))]).reshape(ffn_w_gate.shape)
    g_up = jnp.stack([mine[3 * n + 1] for n in range(len(ffn_ids))]).reshape(ffn_w_up.shape)
    g_down = jnp.stack([mine[3 * n + 2] for n in range(len(ffn_ids))]).reshape(ffn_w_down.shape)

    dmod_mine = jnp.concatenate([jnp.concatenate(dmod[i, s], axis=0) for i in range(depth) for s in range(n_sub)], axis=0)
    dpre_mine = jnp.concatenate([dnorm[i, s][0] for i in range(depth) for s in range(n_sub)], axis=0)
    dpost_mine = jnp.concatenate([dnorm[i, s][1] for i in range(depth) for s in range(n_sub)], axis=0)
    dbias_tab = dbias_sums[:, 0, :N_BUCKETS].T
    pieces = [dmod_mine, dpre_mine, dpost_mine, dq_norm, dkv_norm, dbias_tab, jnp.sum(loss_parts).reshape(1, 1)]
    packed = [_lanes(p) for p in pieces]
    offs = [0]
    for p in packed:
        offs.append(offs[-1] + p.shape[0])
    everyone = _exchange([jnp.concatenate(packed, axis=0)], True, "gather_small_grads")[0].reshape(NDEV, offs[-1], 128)
    total = _sum_parts(everyone, "sum_small")
    take = lambda n, shape: total[offs[n]:offs[n + 1]].reshape(-1)[:math.prod(shape)].reshape(shape)
    g_b_mod = take(0, b_mod.shape)
    col0 = me * d_loc
    g_norm_pre = lax.dynamic_slice(take(1, (depth, n_sub, d_model)), (0, 0, col0), norm_pre.shape)
    g_norm_post = lax.dynamic_slice(take(2, (depth, n_sub, d_model)), (0, 0, col0), norm_post.shape)
    g_q_norm, g_kv_norm = take(3, mla_q_norm.shape), take(4, mla_kv_norm.shape)
    g_rel_bias = take(5, rel_bias.shape)
    loss = take(6, ())

    dmod_all = everyone[:, offs[0]:offs[1]].reshape(NDEV, depth, NDEV * mod_loc_cols)
    dmod_cols = lax.dynamic_slice(dmod_all, (0, 0, me * mod_loc_cols), (NDEV, depth, mod_loc_cols))
    silu_t = jnp.pad(silu_c.T, ((0, 0), (0, HEAD_PAD - NDEV)))
    g_w_mod = jnp.stack([_mm([(silu_t, jnp.pad(dmod_cols[:, i], ((0, HEAD_PAD - NDEV), (0, 0))))], "nn", F32, TOKEN_TILE,
                             mod_loc_cols, "mod_bwd") for i in range(depth)])

    ws = (norm_pre, norm_post, w_mod, b_mod, ffn_w_gate, ffn_w_up, ffn_w_down, mla_w_in, mla_q_norm, mla_w_q_up, mla_kv_norm,
          mla_w_kv_up, mla_w_o, dil_w_in, dil_w_o, rel_bias)
    gs = (g_norm_pre, g_norm_post, g_w_mod, g_b_mod, g_gate, g_up, g_down, g_mla_in, g_q_norm, g_q_up, g_kv_norm, g_kv_up,
          g_mla_o, g_dil_in, g_dil_o, g_rel_bias)
    ms = (m_norm_pre, m_norm_post, m_w_mod, m_b_mod, m_ffn_w_gate, m_ffn_w_up, m_ffn_w_down, m_mla_w_in, m_mla_q_norm,
          m_mla_w_q_up, m_mla_kv_norm, m_mla_w_kv_up, m_mla_w_o, m_dil_w_in, m_dil_w_o, m_rel_bias)
    vs = (v_norm_pre, v_norm_post, v_w_mod, v_b_mod, v_ffn_w_gate, v_ffn_w_up, v_ffn_w_down, v_mla_w_in, v_mla_q_norm,
          v_mla_w_q_up, v_mla_kv_norm, v_mla_w_kv_up, v_mla_w_o, v_dil_w_in, v_dil_w_o, v_rel_bias)
    names = ("norm_pre", "norm_post", "w_mod", "b_mod", "ffn_w_gate", "ffn_w_up", "ffn_w_down", "mla_w_in", "mla_q_norm",
             "mla_w_q_up", "mla_kv_norm", "mla_w_kv_up", "mla_w_o", "dil_w_in", "dil_w_o", "rel_bias")
    stepped = [early[n] if n in early else _adamw(w, g, m, v, "adamw") for n, w, g, m, v in zip(names, ws, gs, ms, vs)]
    deltas, new_m, new_v = zip(*stepped)
    return (loss, grad_x, *gs, *deltas, *new_m, *new_v)
```
